```python
import math
import jax, jax.numpy as jnp
from jax import lax
import numpy as np


D_MODEL = 1024
BATCH = 8
SEQ = 4096
DEPTH = 2

HEAD_DIM = 64
BLOCK = 128
EPS = 1e-6
RET_HEADS = 4
RET_QK_DIM = 64
RET_V_DIM = 128
RET_CHUNK = 128
RET_THETA = 10000.0
DIL_HEADS = 8
DIL_PATTERNS = ((128, 1), (512, 4), (2048, 16))
SWA_Q_HEADS = 16
SWA_KV_HEADS = 4
SWA_WINDOW = 128
ROPE_THETA = 500000.0
ROPE_DIMS = HEAD_DIM // 4
D_FF = 4 * D_MODEL
EVEN_SPLITS = (RET_HEADS * RET_QK_DIM, RET_HEADS * RET_QK_DIM, RET_HEADS * RET_V_DIM, RET_HEADS * RET_V_DIM,
               DIL_HEADS * HEAD_DIM, DIL_HEADS * HEAD_DIM, DIL_HEADS * HEAD_DIM)
EVEN_IN = sum(EVEN_SPLITS)
EVEN_MIX = RET_HEADS * RET_V_DIM + DIL_HEADS * HEAD_DIM
SWA_SPLITS = (SWA_Q_HEADS * HEAD_DIM, SWA_KV_HEADS * HEAD_DIM, SWA_KV_HEADS * HEAD_DIM)
SWA_IN = sum(SWA_SPLITS)
SWA_MIX = SWA_Q_HEADS * HEAD_DIM

kernel_name = 'hybrid_retention_dilated_swa_block'


def rmsnorm(x, g):
    xf = x.astype(jnp.float32)
    y = xf * lax.rsqrt(jnp.mean(xf * xf, axis=-1, keepdims=True) + EPS)
    return (y * g.astype(jnp.float32)).astype(x.dtype)


def to_heads(t, n_heads):
    b, s, _ = t.shape
    return t.reshape(b, s, n_heads, -1).transpose(0, 2, 1, 3)


def from_heads(t):
    b, h, s, d = t.shape
    return t.transpose(0, 2, 1, 3).reshape(b, s, h * d)


def rope(x, pos, n_rot, theta):
    half = n_rot // 2
    inv = jnp.power(jnp.float32(theta), -jnp.arange(half, dtype=jnp.float32) * (2.0 / n_rot))
    ang = pos.astype(jnp.float32)[:, None, :, None] * inv
    cos, sin = jnp.cos(ang), jnp.sin(ang)
    xf = x.astype(jnp.float32)
    x1, x2, rest = xf[..., :half], xf[..., half:n_rot], xf[..., n_rot:]
    return jnp.concatenate([x1 * cos - x2 * sin, x2 * cos + x1 * sin, rest], axis=-1)


def banded_attn(q, k, v, max_dist, sinks=None):
    b, hk, g, L, d = q.shape
    bs = math.gcd(L, BLOCK)
    nb = L // bs
    P = max_dist
    kp = jnp.pad(k.astype(jnp.float32), ((0, 0), (0, 0), (P, 0), (0, 0)))
    vp = jnp.pad(v.astype(jnp.float32), ((0, 0), (0, 0), (P, 0), (0, 0)))
    idx = (jnp.arange(nb) * bs)[:, None] + jnp.arange(bs + P)[None, :]
    kb = kp[:, :, idx]
    vb = vp[:, :, idx]
    qb = q.astype(jnp.float32).reshape(b, hk, g, nb, bs, d)
    s = jnp.einsum('bhgnqd,bhnkd->bhgnqk', qb, kb) * (d ** -0.5)
    qpos = jnp.arange(L).reshape(nb, bs)
    kpos = idx - P
    dist = qpos[:, :, None] - kpos[:, None, :]
    valid = (dist >= 0) & (dist <= max_dist) & (kpos[:, None, :] >= 0)
    s = jnp.where(valid, s, -jnp.inf)
    m = jnp.max(s, axis=-1, keepdims=True)
    if sinks is not None:
        sk = sinks.astype(jnp.float32).reshape(1, hk, g, 1, 1, 1)
        m = jnp.maximum(m, sk)
    p = jnp.exp(s - m)
    den = jnp.sum(p, axis=-1, keepdims=True)
    if sinks is not None:
        den = den + jnp.exp(sk - m)
    o = jnp.einsum('bhgnqk,bhnkd->bhgnqd', p, vb) / den
    lse = (m + jnp.log(den))[..., 0]
    return o.reshape(b, hk, g, L, d), lse.reshape(b, hk, g, L)


def dilated_attention(q, k, v):
    b, h, S, d = q.shape
    outs, lses = [], []
    for w, r in DIL_PATTERNS:
        L = S // r
        def split(t):
            return t.reshape(b, h, L, r, d).transpose(0, 1, 3, 2, 4).reshape(b, h * r, L, d)
        o, lse = banded_attn(split(q)[:, :, None], split(k), split(v), w // r)
        outs.append(o[:, :, 0].reshape(b, h, r, L, d).transpose(0, 1, 3, 2, 4).reshape(b, h, S, d))
        lses.append(lse[:, :, 0].reshape(b, h, r, L).transpose(0, 1, 3, 2).reshape(b, h, S))
    wts = jax.nn.softmax(jnp.stack(lses, axis=0), axis=0)
    return jnp.sum(wts[..., None] * jnp.stack(outs, axis=0), axis=0)


def retention(q, k, v):
    b, h, S, dk = q.shape
    dv = v.shape[-1]
    C = math.gcd(S, RET_CHUNK)
    nc = S // C
    log_gamma = jnp.log1p(-jnp.exp2(-5.0 - jnp.arange(h, dtype=jnp.float32)))
    qc = q.reshape(b, h, nc, C, dk)
    kc = k.reshape(b, h, nc, C, dk)
    vc = v.astype(jnp.float32).reshape(b, h, nc, C, dv)
    i = jnp.arange(C, dtype=jnp.float32)
    diff = i[:, None] - i[None, :]
    decay = jnp.where(diff >= 0, jnp.exp(log_gamma[:, None, None] * jnp.maximum(diff, 0.0)), 0.0)
    scores = jnp.einsum('bhncd,bhnkd->bhnck', qc, kc) * decay[None, :, None]
    o_inner = jnp.einsum('bhnck,bhnkv->bhncv', scores, vc)
    xi = jnp.exp(log_gamma[:, None] * (i + 1.0))
    zeta = jnp.exp(log_gamma[:, None] * (C - 1.0 - i))
    chunk_decay = jnp.exp(log_gamma * C)
    kz = kc * zeta[None, :, None, :, None]

    def step(R, inp):
        q_n, kz_n, v_n = inp
        o = jnp.einsum('bhcd,bhdv->bhcv', q_n, R)
        R = R * chunk_decay[None, :, None, None] + jnp.einsum('bhcd,bhcv->bhdv', kz_n, v_n)
        return R, o

    R0 = jnp.zeros((b, h, dk, dv), jnp.float32)
    _, o_cross = lax.scan(step, R0, (qc.transpose(2, 0, 1, 3, 4), kz.transpose(2, 0, 1, 3, 4),
                                     vc.transpose(2, 0, 1, 3, 4)))
    o_cross = o_cross.transpose(1, 2, 0, 3, 4) * xi[None, :, None, :, None]
    return (o_inner + o_cross).reshape(b, h, S, dv)


def even_mixer(h, positions, w_in, w_out, gn_gain, q_gain, k_gain):
    proj = h @ w_in
    cuts = [int(c) for c in np.cumsum(EVEN_SPLITS)[:-1]]
    rq, rk, rv, rg, dq, dk, dv = jnp.split(proj, cuts, axis=-1)
    rq = rope(to_heads(rq, RET_HEADS), positions, RET_QK_DIM, RET_THETA)
    rk = rope(to_heads(rk, RET_HEADS), positions, RET_QK_DIM, RET_THETA) * (RET_QK_DIM ** -0.5)
    ro = retention(rq, rk, to_heads(rv, RET_HEADS))
    mu = jnp.mean(ro, axis=-1, keepdims=True)
    var = jnp.mean(jnp.square(ro - mu), axis=-1, keepdims=True)
    ro = (ro - mu) * lax.rsqrt(var + EPS) * gn_gain.astype(jnp.float32)[None, :, None, :]
    ra = jax.nn.silu(rg.astype(jnp.float32)) * from_heads(ro)
    dq = rope(rmsnorm(to_heads(dq, DIL_HEADS), q_gain), positions, ROPE_DIMS, ROPE_THETA)
    dk = rope(rmsnorm(to_heads(dk, DIL_HEADS), k_gain), positions, ROPE_DIMS, ROPE_THETA)
    da = from_heads(dilated_attention(dq, dk, to_heads(dv, DIL_HEADS).astype(jnp.float32)))
    mixed = jnp.concatenate([ra, da], axis=-1).astype(h.dtype)
    return mixed @ w_out


def swa_mixer(h, positions, w_qkv, b_qkv, w_out, q_gain, k_gain, sinks):
    b, S, _ = h.shape
    g = SWA_Q_HEADS // SWA_KV_HEADS
    proj = h @ w_qkv + b_qkv
    cuts = [int(c) for c in np.cumsum(SWA_SPLITS)[:-1]]
    q, k, v = jnp.split(proj, cuts, axis=-1)
    q = rope(rmsnorm(to_heads(q, SWA_Q_HEADS), q_gain), positions, ROPE_DIMS, ROPE_THETA)
    k = rope(rmsnorm(to_heads(k, SWA_KV_HEADS), k_gain), positions, ROPE_DIMS, ROPE_THETA)
    v = to_heads(v, SWA_KV_HEADS)
    q = q.reshape(b, SWA_KV_HEADS, g, S, HEAD_DIM)
    o, _ = banded_attn(q, k, v, SWA_WINDOW - 1, sinks.reshape(SWA_KV_HEADS, g))
    o = o.reshape(b, SWA_Q_HEADS, S, HEAD_DIM)
    return from_heads(o).astype(h.dtype) @ w_out


def sq_relu_mlp(h, w_up, w_down):
    return jnp.square(jax.nn.relu(h @ w_up)) @ w_down


def _fwd_setup_inputs(seed: int = 0) -> dict:
    key = jax.random.key(seed)
    ks = jax.random.split(key, 20)
    ne, no = (DEPTH + 1) // 2, DEPTH // 2
    f32 = jnp.float32

    def w(k, shape, fan_in):
        return jax.random.normal(k, shape, f32) * (fan_in ** -0.5)

    def gain(k, shape):
        return 1.0 + 0.05 * jax.random.normal(k, shape, f32)

    return {
        'x': jax.random.normal(ks[0], (BATCH, SEQ, D_MODEL), f32),
        'positions': jnp.broadcast_to(jnp.arange(SEQ, dtype=jnp.int32), (BATCH, SEQ)),
        'norm_mix': gain(ks[1], (DEPTH, D_MODEL)),
        'norm_mlp': gain(ks[2], (DEPTH, D_MODEL)),
        'mlp_w_up': w(ks[3], (DEPTH, D_MODEL, D_FF), D_MODEL),
        'mlp_w_down': w(ks[4], (DEPTH, D_FF, D_MODEL), D_FF),
        'hyb_w_in': w(ks[5], (ne, D_MODEL, EVEN_IN), D_MODEL),
        'hyb_w_out': w(ks[6], (ne, EVEN_MIX, D_MODEL), EVEN_MIX),
        'ret_gn_gain': gain(ks[7], (ne, RET_HEADS, RET_V_DIM)),
        'dil_q_gain': gain(ks[8], (ne, HEAD_DIM)),
        'dil_k_gain': gain(ks[9], (ne, HEAD_DIM)),
        'swa_w_qkv': w(ks[10], (no, D_MODEL, SWA_IN), D_MODEL),
        'swa_b_qkv': 0.02 * jax.random.normal(ks[11], (no, SWA_IN), f32),
        'swa_w_out': w(ks[12], (no, SWA_MIX, D_MODEL), SWA_MIX),
        'swa_q_gain': gain(ks[13], (no, HEAD_DIM)),
        'swa_k_gain': gain(ks[14], (no, HEAD_DIM)),
        'swa_sinks': 0.5 * jax.random.normal(ks[15], (no, SWA_Q_HEADS), f32),
    }


def _fwd_reference(x, positions, norm_mix, norm_mlp, mlp_w_up, mlp_w_down, hyb_w_in, hyb_w_out,
              ret_gn_gain, dil_q_gain, dil_k_gain, swa_w_qkv, swa_b_qkv, swa_w_out,
              swa_q_gain, swa_k_gain, swa_sinks):
    for layer in range(DEPTH):
        i = layer // 2
        h = rmsnorm(x, norm_mix[layer])
        if layer % 2 == 0:
            x = x + even_mixer(h, positions, hyb_w_in[i], hyb_w_out[i], ret_gn_gain[i],
                               dil_q_gain[i], dil_k_gain[i])
        else:
            x = x + swa_mixer(h, positions, swa_w_qkv[i], swa_b_qkv[i], swa_w_out[i],
                              swa_q_gain[i], swa_k_gain[i], swa_sinks[i])
        h = rmsnorm(x, norm_mlp[layer])
        x = x + sq_relu_mlp(h, mlp_w_up[layer], mlp_w_down[layer])
    return x


import jax as _jax
import jax.numpy as _jnp

TWIN_FORMAT = 'train_step'
FWD_PARAMS = ['x', 'positions', 'norm_mix', 'norm_mlp', 'mlp_w_up', 'mlp_w_down', 'hyb_w_in', 'hyb_w_out', 'ret_gn_gain', 'dil_q_gain', 'dil_k_gain', 'swa_w_qkv', 'swa_b_qkv', 'swa_w_out', 'swa_q_gain', 'swa_k_gain', 'swa_sinks']
TWIN_WEIGHTS = ['norm_mix', 'norm_mlp', 'mlp_w_up', 'mlp_w_down', 'hyb_w_in', 'hyb_w_out', 'ret_gn_gain', 'dil_q_gain', 'dil_k_gain', 'swa_w_qkv', 'swa_b_qkv', 'swa_w_out', 'swa_q_gain', 'swa_k_gain', 'swa_sinks']
TWIN_DIFF_INPUT = 'x'
TWIN_INPUTS = ['x', 'positions', 'norm_mix', 'norm_mlp', 'mlp_w_up', 'mlp_w_down', 'hyb_w_in', 'hyb_w_out', 'ret_gn_gain', 'dil_q_gain', 'dil_k_gain', 'swa_w_qkv', 'swa_b_qkv', 'swa_w_out', 'swa_q_gain', 'swa_k_gain', 'swa_sinks', 'loss_target', 'm_norm_mix', 'm_norm_mlp', 'm_mlp_w_up', 'm_mlp_w_down', 'm_hyb_w_in', 'm_hyb_w_out', 'm_ret_gn_gain', 'm_dil_q_gain', 'm_dil_k_gain', 'm_swa_w_qkv', 'm_swa_b_qkv', 'm_swa_w_out', 'm_swa_q_gain', 'm_swa_k_gain', 'm_swa_sinks', 'v_norm_mix', 'v_norm_mlp', 'v_mlp_w_up', 'v_mlp_w_down', 'v_hyb_w_in', 'v_hyb_w_out', 'v_ret_gn_gain', 'v_dil_q_gain', 'v_dil_k_gain', 'v_swa_w_qkv', 'v_swa_b_qkv', 'v_swa_w_out', 'v_swa_q_gain', 'v_swa_k_gain', 'v_swa_sinks']
TWIN_OUTPUTS = ['loss', 'grad_x', 'grad_norm_mix', 'grad_norm_mlp', 'grad_mlp_w_up', 'grad_mlp_w_down', 'grad_hyb_w_in', 'grad_hyb_w_out', 'grad_ret_gn_gain', 'grad_dil_q_gain', 'grad_dil_k_gain', 'grad_swa_w_qkv', 'grad_swa_b_qkv', 'grad_swa_w_out', 'grad_swa_q_gain', 'grad_swa_k_gain', 'grad_swa_sinks', 'delta_norm_mix', 'delta_norm_mlp', 'delta_mlp_w_up', 'delta_mlp_w_down', 'delta_hyb_w_in', 'delta_hyb_w_out', 'delta_ret_gn_gain', 'delta_dil_q_gain', 'delta_dil_k_gain', 'delta_swa_w_qkv', 'delta_swa_b_qkv', 'delta_swa_w_out', 'delta_swa_q_gain', 'delta_swa_k_gain', 'delta_swa_sinks', 'new_m_norm_mix', 'new_m_norm_mlp', 'new_m_mlp_w_up', 'new_m_mlp_w_down', 'new_m_hyb_w_in', 'new_m_hyb_w_out', 'new_m_ret_gn_gain', 'new_m_dil_q_gain', 'new_m_dil_k_gain', 'new_m_swa_w_qkv', 'new_m_swa_b_qkv', 'new_m_swa_w_out', 'new_m_swa_q_gain', 'new_m_swa_k_gain', 'new_m_swa_sinks', 'new_v_norm_mix', 'new_v_norm_mlp', 'new_v_mlp_w_up', 'new_v_mlp_w_down', 'new_v_hyb_w_in', 'new_v_hyb_w_out', 'new_v_ret_gn_gain', 'new_v_dil_q_gain', 'new_v_dil_k_gain', 'new_v_swa_w_qkv', 'new_v_swa_b_qkv', 'new_v_swa_w_out', 'new_v_swa_q_gain', 'new_v_swa_k_gain', 'new_v_swa_sinks']
TWIN_LEAF_KINDS = {'loss': 'loss', 'grad_x': 'grad_x', 'grad_norm_mix': 'grad_w', 'grad_norm_mlp': 'grad_w', 'grad_mlp_w_up': 'grad_w', 'grad_mlp_w_down': 'grad_w', 'grad_hyb_w_in': 'grad_w', 'grad_hyb_w_out': 'grad_w', 'grad_ret_gn_gain': 'grad_w', 'grad_dil_q_gain': 'grad_w', 'grad_dil_k_gain': 'grad_w', 'grad_swa_w_qkv': 'grad_w', 'grad_swa_b_qkv': 'grad_w', 'grad_swa_w_out': 'grad_w', 'grad_swa_q_gain': 'grad_w', 'grad_swa_k_gain': 'grad_w', 'grad_swa_sinks': 'grad_w', 'delta_norm_mix': 'delta_w', 'delta_norm_mlp': 'delta_w', 'delta_mlp_w_up': 'delta_w', 'delta_mlp_w_down': 'delta_w', 'delta_hyb_w_in': 'delta_w', 'delta_hyb_w_out': 'delta_w', 'delta_ret_gn_gain': 'delta_w', 'delta_dil_q_gain': 'delta_w', 'delta_dil_k_gain': 'delta_w', 'delta_swa_w_qkv': 'delta_w', 'delta_swa_b_qkv': 'delta_w', 'delta_swa_w_out': 'delta_w', 'delta_swa_q_gain': 'delta_w', 'delta_swa_k_gain': 'delta_w', 'delta_swa_sinks': 'delta_w', 'new_m_norm_mix': 'new_m', 'new_m_norm_mlp': 'new_m', 'new_m_mlp_w_up': 'new_m', 'new_m_mlp_w_down': 'new_m', 'new_m_hyb_w_in': 'new_m', 'new_m_hyb_w_out': 'new_m', 'new_m_ret_gn_gain': 'new_m', 'new_m_dil_q_gain': 'new_m', 'new_m_dil_k_gain': 'new_m', 'new_m_swa_w_qkv': 'new_m', 'new_m_swa_b_qkv': 'new_m', 'new_m_swa_w_out': 'new_m', 'new_m_swa_q_gain': 'new_m', 'new_m_swa_k_gain': 'new_m', 'new_m_swa_sinks': 'new_m', 'new_v_norm_mix': 'new_v', 'new_v_norm_mlp': 'new_v', 'new_v_mlp_w_up': 'new_v', 'new_v_mlp_w_down': 'new_v', 'new_v_hyb_w_in': 'new_v', 'new_v_hyb_w_out': 'new_v', 'new_v_ret_gn_gain': 'new_v', 'new_v_dil_q_gain': 'new_v', 'new_v_dil_k_gain': 'new_v', 'new_v_swa_w_qkv': 'new_v', 'new_v_swa_b_qkv': 'new_v', 'new_v_swa_w_out': 'new_v', 'new_v_swa_q_gain': 'new_v', 'new_v_swa_k_gain': 'new_v', 'new_v_swa_sinks': 'new_v'}


def _forward(args):
    return _fwd_reference(*[args[k] for k in FWD_PARAMS])


def _output_shape():
    def fwd():
        inp = _fwd_setup_inputs(0)
        return _fwd_reference(*[inp[k] for k in FWD_PARAMS])
    out = _jax.eval_shape(fwd)
    return out.shape, out.dtype

N_MICROBATCH = 1
ADAM_LR = 0.001
ADAM_B1 = 0.9
ADAM_B2 = 0.999
ADAM_EPS = 1e-08
ADAM_WD = 0.01
ADAM_STEP = 10
PER_EXAMPLE_BATCH_AXIS = {'x': 0, 'positions': 0, 'loss_target': 0}
SHARED_INPUTS = []
_WEIGHT_DTYPES = {'norm_mix': _jnp.float32, 'norm_mlp': _jnp.float32, 'mlp_w_up': _jnp.float32, 'mlp_w_down': _jnp.float32, 'hyb_w_in': _jnp.float32, 'hyb_w_out': _jnp.float32, 'ret_gn_gain': _jnp.float32, 'dil_q_gain': _jnp.float32, 'dil_k_gain': _jnp.float32, 'swa_w_qkv': _jnp.float32, 'swa_b_qkv': _jnp.float32, 'swa_w_out': _jnp.float32, 'swa_q_gain': _jnp.float32, 'swa_k_gain': _jnp.float32, 'swa_sinks': _jnp.float32}
MOMENT_SCALE = {'norm_mix': 1.019152e+01, 'norm_mlp': 9.784853e+01, 'mlp_w_up': 4.973768e+00, 'mlp_w_down': 1.901851e+01, 'hyb_w_in': 6.426575e-01, 'hyb_w_out': 7.131948e-01, 'ret_gn_gain': 1.087853e+01, 'dil_q_gain': 2.417788e+00, 'dil_k_gain': 2.421551e+00, 'swa_w_qkv': 1.008212e+01, 'swa_b_qkv': 3.247661e+01, 'swa_w_out': 1.025292e+01, 'swa_q_gain': 6.312701e+00, 'swa_k_gain': 6.352143e+00, 'swa_sinks': 1.702701e+00}


def _to_microbatches(a, axis):
    t = _jnp.moveaxis(a, axis, 0)
    t = t.reshape((N_MICROBATCH, t.shape[0] // N_MICROBATCH) + t.shape[1:])
    return _jnp.moveaxis(t, 1, axis + 1)


def setup_inputs(seed: int = 0) -> dict:
    inp = _fwd_setup_inputs(seed)
    key = _jax.random.fold_in(_jax.random.key(seed), 7919)
    shape, _ = _output_shape()
    out = dict(inp)
    out["loss_target"] = _jax.random.normal(_jax.random.fold_in(key, 0), shape, _jnp.float32)
    for i, name in enumerate(TWIN_WEIGHTS):
        w = inp[name].astype(_jnp.float32)
        if MOMENT_SCALE is None:
            s = _jnp.sqrt(_jnp.mean(_jnp.square(w)) + 1e-30)
        else:
            s = MOMENT_SCALE[name]
        km, kv = _jax.random.split(_jax.random.fold_in(key, i + 1))
        out[name] = w
        out["m_" + name] = s * _jax.random.normal(km, w.shape, _jnp.float32)
        out["v_" + name] = (s * s) * _jax.random.uniform(kv, w.shape, _jnp.float32, 0.5, 1.5)
    if N_MICROBATCH > 1:
        for name, axis in PER_EXAMPLE_BATCH_AXIS.items():
            out[name] = _to_microbatches(out[name], axis)
    return {'x': out['x'], 'positions': out['positions'], 'norm_mix': out['norm_mix'], 'norm_mlp': out['norm_mlp'], 'mlp_w_up': out['mlp_w_up'], 'mlp_w_down': out['mlp_w_down'], 'hyb_w_in': out['hyb_w_in'], 'hyb_w_out': out['hyb_w_out'], 'ret_gn_gain': out['ret_gn_gain'], 'dil_q_gain': out['dil_q_gain'], 'dil_k_gain': out['dil_k_gain'], 'swa_w_qkv': out['swa_w_qkv'], 'swa_b_qkv': out['swa_b_qkv'], 'swa_w_out': out['swa_w_out'], 'swa_q_gain': out['swa_q_gain'], 'swa_k_gain': out['swa_k_gain'], 'swa_sinks': out['swa_sinks'], 'loss_target': out['loss_target'], 'm_norm_mix': out['m_norm_mix'], 'm_norm_mlp': out['m_norm_mlp'], 'm_mlp_w_up': out['m_mlp_w_up'], 'm_mlp_w_down': out['m_mlp_w_down'], 'm_hyb_w_in': out['m_hyb_w_in'], 'm_hyb_w_out': out['m_hyb_w_out'], 'm_ret_gn_gain': out['m_ret_gn_gain'], 'm_dil_q_gain': out['m_dil_q_gain'], 'm_dil_k_gain': out['m_dil_k_gain'], 'm_swa_w_qkv': out['m_swa_w_qkv'], 'm_swa_b_qkv': out['m_swa_b_qkv'], 'm_swa_w_out': out['m_swa_w_out'], 'm_swa_q_gain': out['m_swa_q_gain'], 'm_swa_k_gain': out['m_swa_k_gain'], 'm_swa_sinks': out['m_swa_sinks'], 'v_norm_mix': out['v_norm_mix'], 'v_norm_mlp': out['v_norm_mlp'], 'v_mlp_w_up': out['v_mlp_w_up'], 'v_mlp_w_down': out['v_mlp_w_down'], 'v_hyb_w_in': out['v_hyb_w_in'], 'v_hyb_w_out': out['v_hyb_w_out'], 'v_ret_gn_gain': out['v_ret_gn_gain'], 'v_dil_q_gain': out['v_dil_q_gain'], 'v_dil_k_gain': out['v_dil_k_gain'], 'v_swa_w_qkv': out['v_swa_w_qkv'], 'v_swa_b_qkv': out['v_swa_b_qkv'], 'v_swa_w_out': out['v_swa_w_out'], 'v_swa_q_gain': out['v_swa_q_gain'], 'v_swa_k_gain': out['v_swa_k_gain'], 'v_swa_sinks': out['v_swa_sinks']}


def _loss(weights, diff, rest, loss_target):
    with _jax.named_scope("forward"):
        args = {**rest, TWIN_DIFF_INPUT: diff, **{k: w.astype(_WEIGHT_DTYPES[k]) for k, w in weights.items()}}
        y = _forward(args)
    with _jax.named_scope("loss_head"):
        err = _jnp.square(y.astype(_jnp.float32) - loss_target)
        return 0.5 * _jnp.sum(_jnp.mean(err, axis=-1)) if err.ndim else 0.5 * err


def _adamw(w, g, m, v):
    m = ADAM_B1 * m + (1.0 - ADAM_B1) * g
    v = ADAM_B2 * v + (1.0 - ADAM_B2) * _jnp.square(g)
    m_hat = m / (1.0 - ADAM_B1 ** ADAM_STEP)
    v_hat = v / (1.0 - ADAM_B2 ** ADAM_STEP)
    delta = -ADAM_LR * (m_hat / (_jnp.sqrt(v_hat) + ADAM_EPS) + ADAM_WD * w)
    return delta, m, v


def reference(x, positions, norm_mix, norm_mlp, mlp_w_up, mlp_w_down, hyb_w_in, hyb_w_out, ret_gn_gain, dil_q_gain, dil_k_gain, swa_w_qkv, swa_b_qkv, swa_w_out, swa_q_gain, swa_k_gain, swa_sinks, loss_target, m_norm_mix, m_norm_mlp, m_mlp_w_up, m_mlp_w_down, m_hyb_w_in, m_hyb_w_out, m_ret_gn_gain, m_dil_q_gain, m_dil_k_gain, m_swa_w_qkv, m_swa_b_qkv, m_swa_w_out, m_swa_q_gain, m_swa_k_gain, m_swa_sinks, v_norm_mix, v_norm_mlp, v_mlp_w_up, v_mlp_w_down, v_hyb_w_in, v_hyb_w_out, v_ret_gn_gain, v_dil_q_gain, v_dil_k_gain, v_swa_w_qkv, v_swa_b_qkv, v_swa_w_out, v_swa_q_gain, v_swa_k_gain, v_swa_sinks):
    given = dict(x=x, positions=positions, norm_mix=norm_mix, norm_mlp=norm_mlp, mlp_w_up=mlp_w_up, mlp_w_down=mlp_w_down, hyb_w_in=hyb_w_in, hyb_w_out=hyb_w_out, ret_gn_gain=ret_gn_gain, dil_q_gain=dil_q_gain, dil_k_gain=dil_k_gain, swa_w_qkv=swa_w_qkv, swa_b_qkv=swa_b_qkv, swa_w_out=swa_w_out, swa_q_gain=swa_q_gain, swa_k_gain=swa_k_gain, swa_sinks=swa_sinks, loss_target=loss_target, m_norm_mix=m_norm_mix, m_norm_mlp=m_norm_mlp, m_mlp_w_up=m_mlp_w_up, m_mlp_w_down=m_mlp_w_down, m_hyb_w_in=m_hyb_w_in, m_hyb_w_out=m_hyb_w_out, m_ret_gn_gain=m_ret_gn_gain, m_dil_q_gain=m_dil_q_gain, m_dil_k_gain=m_dil_k_gain, m_swa_w_qkv=m_swa_w_qkv, m_swa_b_qkv=m_swa_b_qkv, m_swa_w_out=m_swa_w_out, m_swa_q_gain=m_swa_q_gain, m_swa_k_gain=m_swa_k_gain, m_swa_sinks=m_swa_sinks, v_norm_mix=v_norm_mix, v_norm_mlp=v_norm_mlp, v_mlp_w_up=v_mlp_w_up, v_mlp_w_down=v_mlp_w_down, v_hyb_w_in=v_hyb_w_in, v_hyb_w_out=v_hyb_w_out, v_ret_gn_gain=v_ret_gn_gain, v_dil_q_gain=v_dil_q_gain, v_dil_k_gain=v_dil_k_gain, v_swa_w_qkv=v_swa_w_qkv, v_swa_b_qkv=v_swa_b_qkv, v_swa_w_out=v_swa_w_out, v_swa_q_gain=v_swa_q_gain, v_swa_k_gain=v_swa_k_gain, v_swa_sinks=v_swa_sinks)
    weights = {n: given[n] for n in TWIN_WEIGHTS}
    shared = {n: given[n] for n in SHARED_INPUTS}
    per_example = {n: given[n] for n in ['x', 'positions']}
    grad_fn = _jax.value_and_grad(_loss, argnums=(0, 1))

    def one_microbatch(ex, loss_target):
        ex = dict(ex)
        diff = ex.pop(TWIN_DIFF_INPUT)
        return grad_fn(weights, diff, {**shared, **ex}, loss_target)

    if N_MICROBATCH == 1:
        loss, (grad_w, grad_x) = one_microbatch(per_example, given["loss_target"])
    else:
        def body(carry, xs):
            loss_sum, grad_sum = carry
            l_k, (gw_k, gx_k) = one_microbatch(xs[0], xs[1])
            with _jax.named_scope("update"):
                return (loss_sum + l_k, _jax.tree.map(_jnp.add, grad_sum, gw_k)), gx_k

        init = (_jnp.zeros((), _jnp.float32), _jax.tree.map(_jnp.zeros_like, weights))
        (loss, grad_w), grad_x = _jax.lax.scan(body, init, (per_example, given["loss_target"]))
    with _jax.named_scope("update"):
        delta_w, new_m, new_v = {}, {}, {}
        for n in TWIN_WEIGHTS:
            delta_w[n], new_m[n], new_v[n] = _adamw(weights[n], grad_w[n], given["m_" + n], given["v_" + n])
    return (loss, grad_x, *[grad_w[n] for n in TWIN_WEIGHTS], *[delta_w[n] for n in TWIN_WEIGHTS],
            *[new_m[n] for n in TWIN_WEIGHTS], *[new_v[n] for n in TWIN_WEIGHTS])
```

```python
import functools
import math

import numpy as np
import jax
import jax.numpy as jnp
from jax import lax
from jax.experimental import pallas as pl
from jax.experimental.pallas import tpu as pltpu

F32, BF16 = jnp.float32, jnp.bfloat16
HIGHEST = lax.Precision.HIGHEST
MESH = pl.DeviceIdType.MESH

LANES = 128
VMEM_LIMIT = 48 << 20
D_MODEL = 1024
D_FF = 4096
HEAD = 64
EPS = 1e-6
BLK = 128
RET_HEADS = 4
RET_THETA = 10000.0
ROPE_THETA = 500000.0
ROPE_DIMS = 16
DIL_PATTERNS = ((128, 1), (512, 4), (2048, 16))
SWA_DIST = 127
N_CHIPS = 4
ADAM_LR, ADAM_B1, ADAM_B2, ADAM_EPS, ADAM_WD, ADAM_STEP = 0.001, 0.9, 0.999, 1e-08, 0.01, 10

_LOG_GAMMA = [float(np.log1p(-np.exp2(np.float32(-5.0 - h)))) for h in range(RET_HEADS)]


def _pc(body, **kw):
    return pl.pallas_call(body, **kw)


def _params(sem):
    return pltpu.CompilerParams(dimension_semantics=sem, vmem_limit_bytes=VMEM_LIMIT)


def _matmul(a, b, *, dims, tm, tn, tk, outs, name, epilogue=None, extras=(), b_cs=False, o_cs=0):
    if dims == "nn":
        M, K = a.shape
        N = b.shape[0] * b.shape[2] if b_cs else b.shape[1]
        a_spec = pl.BlockSpec((tm, tk), lambda i, j, k: (i, k))
        if b_cs:
            npt = b.shape[2] // tn
            b_spec = pl.BlockSpec((None, tk, tn), lambda i, j, k: (j // npt, k, j % npt))
        else:
            b_spec = pl.BlockSpec((tk, tn), lambda i, j, k: (k, j))
        contract = (((1,), (0,)), ((), ()))
    elif dims == "nt":
        M, K = a.shape
        N = b.shape[1] if b_cs else b.shape[0]
        a_spec = pl.BlockSpec((tm, tk), lambda i, j, k: (i, k))
        if b_cs:
            kpt = b.shape[2] // tk
            b_spec = pl.BlockSpec((None, tn, tk), lambda i, j, k: (k // kpt, j, k % kpt))
        else:
            b_spec = pl.BlockSpec((tn, tk), lambda i, j, k: (j, k))
        contract = (((1,), (1,)), ((), ()))
    else:
        K, M = a.shape
        N = b.shape[1]
        a_spec = pl.BlockSpec((tk, tm), lambda i, j, k: (k, i))
        b_spec = pl.BlockSpec((tk, tn), lambda i, j, k: (k, j))
        contract = (((0,), (0,)), ((), ()))
    assert M % tm == 0 and N % tn == 0 and K % tk == 0, (name, M, N, K, tm, tn, tk)
    nk = K // tk
    ex_specs = []
    for arr, kind in extras:
        if kind == "mn":
            ex_specs.append(pl.BlockSpec((tm, tn), lambda i, j, k: (i, j)))
        else:
            ex_specs.append(pl.BlockSpec((1, tn), lambda i, j, k: (0, j)))
    if o_cs:
        n_sh = N // o_cs
        opt = n_sh // tn
        o_shape = (o_cs, M, n_sh)
        o_spec = pl.BlockSpec((None, tm, tn), lambda i, j, k: (j // opt, i, j % opt))
    else:
        o_shape = (M, N)
        o_spec = pl.BlockSpec((tm, tn), lambda i, j, k: (i, j))
    n_ex, n_out = len(extras), len(outs)
    if epilogue is None:
        epilogue = lambda acc: (acc,)

    def body(a_ref, b_ref, *rest):
        ex, o_refs, acc = rest[:n_ex], rest[n_ex:n_ex + n_out], rest[-1]
        k = pl.program_id(2)

        @pl.when(k == 0)
        def _():
            acc[...] = jnp.zeros_like(acc)

        acc[...] += lax.dot_general(a_ref[...].astype(BF16), b_ref[...].astype(BF16), contract,
                                    preferred_element_type=F32)

        @pl.when(k == nk - 1)
        def _():
            vals = epilogue(acc[...], *[e[...] for e in ex])
            for r, v in zip(o_refs, vals):
                r[...] = v.astype(r.dtype)

    res = _pc(
        body, name=name, grid=(M // tm, N // tn, nk),
        in_specs=[a_spec, b_spec] + ex_specs,
        out_specs=[o_spec] * n_out,
        out_shape=[jax.ShapeDtypeStruct(o_shape, dt) for dt in outs],
        scratch_shapes=[pltpu.VMEM((tm, tn), F32)],
        compiler_params=_params(("parallel", "parallel", "arbitrary")),
    )(a, b, *[e for e, _ in extras])
    return res[0] if n_out == 1 else res


def _roll(x, s):
    return pltpu.roll(x, s % LANES, 1)


def _rope(x, A, B, C, half):
    return x * A + _roll(x, LANES - half) * B + _roll(x, half) * C


def _rope_t(g, A, B, C, half):
    return g * A + _roll(g * B, half) + _roll(g * C, LANES - half)


def _gmean(x, G):
    return jnp.dot(x, G, precision=HIGHEST, preferred_element_type=F32)


def _head_mask(shape, half):
    lane = lax.broadcasted_iota(jnp.int32, shape, len(shape) - 1)
    return (lane >= HEAD) if half else (lane < HEAD)


def _group_matrix():
    i = np.arange(LANES)
    return jnp.asarray((i[:, None] // HEAD == i[None, :] // HEAD).astype(np.float32) / HEAD)


def _rope_inv():
    l = np.arange(LANES) % HEAD
    inv_r = np.power(np.float32(RET_THETA), -(l % 32).astype(np.float32) * np.float32(2.0 / HEAD))
    hp = ROPE_DIMS // 2
    inv_p = np.power(np.float32(ROPE_THETA), -(l % hp).astype(np.float32) * np.float32(2.0 / ROPE_DIMS))
    inv_p = np.where(l < ROPE_DIMS, inv_p, 0.0)
    return jnp.asarray(np.stack([inv_r, inv_p]).astype(np.float32))


def _tables(pos_col):
    S = pos_col.shape[0]
    tm = 512
    hp = ROPE_DIMS // 2

    def body(p_ref, inv_ref, o_ref):
        p = p_ref[...].astype(F32)
        lane = lax.broadcasted_iota(jnp.int32, (tm, LANES), 1) % HEAD
        ang = p * inv_ref[0:1, :]
        c, s = jnp.cos(ang), jnp.sin(ang)
        o_ref[:, 0:128] = c
        o_ref[:, 128:256] = jnp.where(lane < 32, -s, 0.0)
        o_ref[:, 256:384] = jnp.where(lane >= 32, s, 0.0)
        ang = p * inv_ref[1:2, :]
        c, s = jnp.cos(ang), jnp.sin(ang)
        o_ref[:, 384:512] = c
        o_ref[:, 512:640] = jnp.where(lane < hp, -s, 0.0)
        o_ref[:, 640:768] = jnp.where((lane >= hp) & (lane < ROPE_DIMS), s, 0.0)

    return _pc(
        body, name="rope_tables", grid=(S // tm,),
        in_specs=[pl.BlockSpec((tm, 1), lambda i: (i, 0)), pl.BlockSpec((2, LANES), lambda i: (0, 0))],
        out_specs=pl.BlockSpec((tm, 768), lambda i: (i, 0)),
        out_shape=jax.ShapeDtypeStruct((S, 768), F32),
        compiler_params=_params(("parallel",)),
    )(pos_col, _rope_inv())


def _tab(tab_ref, which):
    o = 384 * which
    return tab_ref[:, o:o + 128], tab_ref[:, o + 128:o + 256], tab_ref[:, o + 256:o + 384]


def _rms_fwd(x, g, name):
    S, Dm = x.shape
    tm = 512

    def body(x_ref, g_ref, h_ref):
        xv = x_ref[...]
        r = lax.rsqrt(jnp.mean(xv * xv, axis=-1, keepdims=True) + EPS)
        h_ref[...] = (xv * r * g_ref[...]).astype(BF16)

    return _pc(
        body, name=name, grid=(S // tm,),
        in_specs=[pl.BlockSpec((tm, Dm), lambda i: (i, 0)), pl.BlockSpec((1, Dm), lambda i: (0, 0))],
        out_specs=pl.BlockSpec((tm, Dm), lambda i: (i, 0)),
        out_shape=jax.ShapeDtypeStruct((S, Dm), BF16),
        compiler_params=_params(("parallel",)),
    )(x, g.reshape(1, Dm))


def _rms_bwd(x, g, dh, dres, name):
    S, Dm = x.shape
    tm = 512

    def body(x_ref, g_ref, dh_ref, dres_ref, dx_ref, dg_ref):
        xv, dhv = x_ref[...], dh_ref[...]
        r = lax.rsqrt(jnp.mean(xv * xv, axis=-1, keepdims=True) + EPS)
        t = dhv * g_ref[...]
        dx_ref[...] = dres_ref[...] + r * t - xv * (r * r * r) * jnp.mean(xv * t, axis=-1, keepdims=True)

        @pl.when(pl.program_id(0) == 0)
        def _():
            dg_ref[...] = jnp.zeros_like(dg_ref)

        dg_ref[...] += jnp.sum(dhv * xv * r, axis=0, keepdims=True)

    row = pl.BlockSpec((tm, Dm), lambda i: (i, 0))
    vec = pl.BlockSpec((1, Dm), lambda i: (0, 0))
    return _pc(
        body, name=name, grid=(S // tm,),
        in_specs=[row, vec, row, row], out_specs=[row, vec],
        out_shape=[jax.ShapeDtypeStruct((S, Dm), F32), jax.ShapeDtypeStruct((1, Dm), F32)],
        compiler_params=_params(("arbitrary",)),
    )(x, g.reshape(1, Dm), dh, dres)


def _hn_fwd(x, gain, G):
    r = lax.rsqrt(_gmean(x * x, G) + EPS)
    return x * r * gain


def _hn_bwd(x, gain, dy, G):
    r = lax.rsqrt(_gmean(x * x, G) + EPS)
    t = dy * gain
    dx = r * t - x * (r * r * r) * _gmean(x * t, G)
    return dx, jnp.sum(dy * x * r, axis=0, keepdims=True)


def _fold_halves(v):
    return v + _roll(v, HEAD)


def _even_pre_fwd(proj, tab, qg, kg):
    S = proj.shape[0]
    tm = 256

    def body(p_ref, tab_ref, qg_ref, kg_ref, g_ref, rq_ref, rk_ref, rv_ref, dq_ref, dk_ref, dv_ref):
        Ar, Br, Cr = _tab(tab_ref, 0)
        Ap, Bp, Cp = _tab(tab_ref, 1)
        G = g_ref[...]
        for c in range(2):
            sl = slice(c * 128, (c + 1) * 128)
            rq_ref[:, sl] = _rope(p_ref[:, c * 128:(c + 1) * 128], Ar, Br, Cr, 32).astype(BF16)
            rk_ref[:, sl] = (_rope(p_ref[:, 256 + c * 128:256 + (c + 1) * 128], Ar, Br, Cr, 32) * 0.125).astype(BF16)
        rv_ref[...] = p_ref[:, 512:1024].astype(BF16)
        for c in range(4):
            sl = slice(c * 128, (c + 1) * 128)
            q = _hn_fwd(p_ref[:, 1536 + c * 128:1536 + (c + 1) * 128], qg_ref[...], G)
            dq_ref[:, sl] = _rope(q, Ap, Bp, Cp, 8).astype(BF16)
            k = _hn_fwd(p_ref[:, 2048 + c * 128:2048 + (c + 1) * 128], kg_ref[...], G)
            dk_ref[:, sl] = _rope(k, Ap, Bp, Cp, 8).astype(BF16)
        dv_ref[...] = p_ref[:, 2560:3072].astype(BF16)

    row = lambda w: pl.BlockSpec((tm, w), lambda i: (i, 0))
    vec = pl.BlockSpec((1, LANES), lambda i: (0, 0))
    return _pc(
        body, name="even_pre_fwd", grid=(S // tm,),
        in_specs=[row(3072), row(768), vec, vec, pl.BlockSpec((LANES, LANES), lambda i: (0, 0))],
        out_specs=[row(256), row(256), row(512), row(512), row(512), row(512)],
        out_shape=[jax.ShapeDtypeStruct((S, w), BF16) for w in (256, 256, 512, 512, 512, 512)],
        compiler_params=_params(("parallel",)),
    )(proj, tab, qg, kg, _group_matrix())


def _even_pre_bwd(proj, tab, qg, kg, drq, drk, drv, drg, dqs, dks, dvs):
    S = proj.shape[0]
    tm = 256
    npat = len(dqs)

    def body(p_ref, tab_ref, qg_ref, kg_ref, g_ref, drq_ref, drk_ref, drv_ref, drg_ref, *rest):
        dq_refs, dk_refs, dv_refs = rest[:npat], rest[npat:2 * npat], rest[2 * npat:3 * npat]
        dp_ref, dqg_ref, dkg_ref = rest[3 * npat:]
        Ar, Br, Cr = _tab(tab_ref, 0)
        Ap, Bp, Cp = _tab(tab_ref, 1)
        G = g_ref[...]
        for c in range(2):
            sl = slice(c * 128, (c + 1) * 128)
            dp_ref[:, c * 128:(c + 1) * 128] = _rope_t(drq_ref[:, sl], Ar, Br, Cr, 32).astype(BF16)
            dp_ref[:, 256 + c * 128:256 + (c + 1) * 128] = _rope_t(drk_ref[:, sl] * 0.125, Ar, Br, Cr, 32).astype(BF16)
        dp_ref[:, 512:1024] = drv_ref[...].astype(BF16)
        dp_ref[:, 1024:1536] = drg_ref[...].astype(BF16)
        accq = jnp.zeros((1, LANES), F32)
        acck = jnp.zeros((1, LANES), F32)
        for c in range(4):
            sl = slice(c * 128, (c + 1) * 128)
            g = dq_refs[0][:, sl]
            for r in dq_refs[1:]:
                g = g + r[:, sl]
            dx, dg = _hn_bwd(p_ref[:, 1536 + c * 128:1536 + (c + 1) * 128], qg_ref[...], _rope_t(g, Ap, Bp, Cp, 8), G)
            dp_ref[:, 1536 + c * 128:1536 + (c + 1) * 128] = dx.astype(BF16)
            accq = accq + dg
            g = dk_refs[0][:, sl]
            for r in dk_refs[1:]:
                g = g + r[:, sl]
            dx, dg = _hn_bwd(p_ref[:, 2048 + c * 128:2048 + (c + 1) * 128], kg_ref[...], _rope_t(g, Ap, Bp, Cp, 8), G)
            dp_ref[:, 2048 + c * 128:2048 + (c + 1) * 128] = dx.astype(BF16)
            acck = acck + dg
        g = dv_refs[0][...]
        for r in dv_refs[1:]:
            g = g + r[...]
        dp_ref[:, 2560:3072] = g.astype(BF16)

        @pl.when(pl.program_id(0) == 0)
        def _():
            dqg_ref[...] = jnp.zeros_like(dqg_ref)
            dkg_ref[...] = jnp.zeros_like(dkg_ref)

        dqg_ref[...] += _fold_halves(accq)
        dkg_ref[...] += _fold_halves(acck)

    row = lambda w: pl.BlockSpec((tm, w), lambda i: (i, 0))
    vec = pl.BlockSpec((1, LANES), lambda i: (0, 0))
    return _pc(
        body, name="even_pre_bwd", grid=(S // tm,),
        in_specs=[row(3072), row(768), vec, vec, pl.BlockSpec((LANES, LANES), lambda i: (0, 0)),
                  row(256), row(256), row(512), row(512)] + [row(512)] * (3 * npat),
        out_specs=[row(3072), vec, vec],
        out_shape=[jax.ShapeDtypeStruct((S, 3072), BF16), jax.ShapeDtypeStruct((1, LANES), F32),
                   jax.ShapeDtypeStruct((1, LANES), F32)],
        compiler_params=_params(("arbitrary",)),
    )(proj, tab, qg, kg, _group_matrix(), drq, drk, drv, drg, *dqs, *dks, *dvs)


def _ret_consts(pair, half):
    lg = jnp.where(pair == 0, _LOG_GAMMA[half], _LOG_GAMMA[2 + half]).astype(F32)
    i = lax.broadcasted_iota(jnp.int32, (BLK, BLK), 0)
    j = lax.broadcasted_iota(jnp.int32, (BLK, BLK), 1)
    diff = (i - j).astype(F32)
    decay = jnp.where(diff >= 0, jnp.exp(lg * jnp.maximum(diff, 0.0)), 0.0)
    t = lax.broadcasted_iota(jnp.int32, (BLK, 1), 0).astype(F32)
    xi = jnp.exp(lg * (t + 1.0))
    zeta = jnp.exp(lg * (BLK - 1.0 - t))
    cd = jnp.exp(jnp.full((1, 1), BLK, F32) * lg)
    return decay, xi, zeta, cd


def _ret_fwd(rq, rk, rv):
    S = rq.shape[0]
    nc = S // BLK

    def body(q_ref, k_ref, v_ref, o_ref, st_ref, R):
        p, n = pl.program_id(0), pl.program_id(1)

        @pl.when(n == 0)
        def _():
            R[...] = jnp.zeros_like(R)

        q2, k2 = q_ref[...], k_ref[...]
        for half in range(2):
            decay, xi, zeta, cd = _ret_consts(p, half)
            m = _head_mask((BLK, LANES), half)
            qm = jnp.where(m, q2, jnp.zeros_like(q2))
            km = jnp.where(m, k2, jnp.zeros_like(k2))
            v = v_ref[:, half * 128:(half + 1) * 128]
            Rb = R[half].astype(BF16)
            st_ref[half] = Rb
            sc = lax.dot_general(qm, k2, (((1,), (1,)), ((), ())), preferred_element_type=F32) * decay
            o = jnp.dot(sc.astype(BF16), v, preferred_element_type=F32)
            o = o + jnp.dot(qm, Rb, preferred_element_type=F32) * xi
            o_ref[:, half * 128:(half + 1) * 128] = o
            kz = (km.astype(F32) * zeta).astype(BF16)
            R[half] = R[half] * cd + lax.dot_general(kz, v, (((0,), (0,)), ((), ())), preferred_element_type=F32)

    return _pc(
        body, name="ret_fwd", grid=(2, nc),
        in_specs=[pl.BlockSpec((BLK, 128), lambda p, n: (n, p)), pl.BlockSpec((BLK, 128), lambda p, n: (n, p)),
                  pl.BlockSpec((BLK, 256), lambda p, n: (n, p))],
        out_specs=[pl.BlockSpec((BLK, 256), lambda p, n: (n, p)),
                   pl.BlockSpec((None, None, 2, 128, 128), lambda p, n: (p, n, 0, 0, 0))],
        out_shape=[jax.ShapeDtypeStruct((S, 512), F32), jax.ShapeDtypeStruct((2, nc, 2, 128, 128), BF16)],
        scratch_shapes=[pltpu.VMEM((2, 128, 128), F32)],
        compiler_params=_params(("parallel", "arbitrary")),
    )(rq, rk, rv)


def _ret_bwd(rq, rk, rv, states, do):
    S = rq.shape[0]
    nc = S // BLK

    def body(q_ref, k_ref, v_ref, st_ref, do_ref, dq_ref, dk_ref, dv_ref, U):
        p, n = pl.program_id(0), pl.program_id(1)

        @pl.when(n == 0)
        def _():
            U[...] = jnp.zeros_like(U)

        q2, k2 = q_ref[...], k_ref[...]
        dq_acc = jnp.zeros((BLK, LANES), F32)
        dk_acc = jnp.zeros((BLK, LANES), F32)
        for half in range(2):
            decay, xi, zeta, cd = _ret_consts(p, half)
            m = _head_mask((BLK, LANES), half)
            qm = jnp.where(m, q2, jnp.zeros_like(q2))
            km = jnp.where(m, k2, jnp.zeros_like(k2))
            v = v_ref[:, half * 128:(half + 1) * 128]
            dob = do_ref[:, half * 128:(half + 1) * 128].astype(BF16)
            Rb = st_ref[half]
            Ub = U[half].astype(BF16)
            nt = (((1,), (1,)), ((), ()))
            tn = (((0,), (0,)), ((), ()))
            dsc = (lax.dot_general(dob, v, nt, preferred_element_type=F32) * decay).astype(BF16)
            xdo = (dob.astype(F32) * xi).astype(BF16)
            dq_acc += jnp.dot(dsc, km, preferred_element_type=F32) + lax.dot_general(xdo, Rb, nt, preferred_element_type=F32)
            dk_acc += lax.dot_general(dsc, qm, tn, preferred_element_type=F32) \
                + lax.dot_general(v, Ub, nt, preferred_element_type=F32) * zeta
            sc = (lax.dot_general(qm, k2, nt, preferred_element_type=F32) * decay).astype(BF16)
            kz = (km.astype(F32) * zeta).astype(BF16)
            dv_ref[:, half * 128:(half + 1) * 128] = lax.dot_general(sc, dob, tn, preferred_element_type=F32) \
                + jnp.dot(kz, Ub, preferred_element_type=F32)
            U[half] = U[half] * cd + lax.dot_general(qm, xdo, tn, preferred_element_type=F32)
        dq_ref[...] = dq_acc
        dk_ref[...] = dk_acc

    rev = lambda w: pl.BlockSpec((BLK, w), lambda p, n: (nc - 1 - n, p))
    return _pc(
        body, name="ret_bwd", grid=(2, nc),
        in_specs=[rev(128), rev(128), rev(256),
                  pl.BlockSpec((None, None, 2, 128, 128), lambda p, n: (p, nc - 1 - n, 0, 0, 0)), rev(256)],
        out_specs=[rev(128), rev(128), rev(256)],
        out_shape=[jax.ShapeDtypeStruct((S, 256), F32), jax.ShapeDtypeStruct((S, 256), F32),
                   jax.ShapeDtypeStruct((S, 512), F32)],
        scratch_shapes=[pltpu.VMEM((2, 128, 128), F32)],
        compiler_params=_params(("parallel", "arbitrary")),
    )(rq, rk, rv, states, do)


def _col_of(b, m):
    return jnp.max(jnp.where(m, b, -jnp.inf), axis=1, keepdims=True)


def _attn_fwd(q, k, v, *, nq, max_dist, name, sinks=None, want_bf16=False):
    L, Ck = k.shape
    nb, ncol = L // BLK, Ck // LANES
    scale = HEAD ** -0.5
    has_sink = sinks is not None

    def body(*refs):
        q_ref, kp_ref, kc_ref, vp_ref, vc_ref = refs[:5]
        sk_ref = refs[5] if has_sink else None
        outs = refs[5 + has_sink:]
        n = pl.program_id(1)
        kcat = jnp.concatenate([kp_ref[...], kc_ref[...]], axis=0)
        vcat = jnp.concatenate([vp_ref[...], vc_ref[...]], axis=0)
        r = lax.broadcasted_iota(jnp.int32, (BLK, 2 * BLK), 0)
        c = lax.broadcasted_iota(jnp.int32, (BLK, 2 * BLK), 1)
        dist = r + BLK - c
        valid = (dist >= 0) & (dist <= max_dist) & ((c >= BLK) | (n > 0))
        for i in range(nq):
            q2 = q_ref[:, i * 128:(i + 1) * 128]
            o2 = jnp.zeros((BLK, LANES), F32)
            l2 = jnp.zeros((BLK, LANES), F32)
            for half in range(2):
                m = _head_mask((BLK, LANES), half)
                qm = jnp.where(m, q2, jnp.zeros_like(q2))
                s = lax.dot_general(qm, kcat, (((1,), (1,)), ((), ())), preferred_element_type=F32) * scale
                s = jnp.where(valid, s, -jnp.inf)
                mx = jnp.max(s, axis=1, keepdims=True)
                if has_sink:
                    snk = _col_of(sk_ref[:, i * 128:(i + 1) * 128], _head_mask((1, LANES), half))
                    mx = jnp.maximum(mx, snk)
                pr = jnp.exp(s - mx)
                den = jnp.sum(pr, axis=1, keepdims=True)
                if has_sink:
                    den = den + jnp.exp(snk - mx)
                pv = jnp.dot(pr.astype(BF16), vcat, preferred_element_type=F32) / den
                o2 = jnp.where(m, pv, o2)
                l2 = jnp.where(m, mx + jnp.log(den), l2)
            outs[0][:, i * 128:(i + 1) * 128] = o2
            outs[1][:, i * 128:(i + 1) * 128] = l2
            if want_bf16:
                outs[2][:, i * 128:(i + 1) * 128] = o2.astype(BF16)

    qspec = pl.BlockSpec((BLK, nq * 128), lambda j, n: (n, j))
    cur = pl.BlockSpec((BLK, 128), lambda j, n: (n, j))
    prev = pl.BlockSpec((BLK, 128), lambda j, n: (jnp.maximum(n - 1, 0), j))
    in_specs = [qspec, prev, cur, prev, cur]
    args = [q, k, k, v, v]
    if has_sink:
        in_specs.append(pl.BlockSpec((1, nq * 128), lambda j, n: (0, j)))
        args.append(sinks)
    out_dts = [F32, F32] + ([BF16] if want_bf16 else [])
    return _pc(
        body, name=name, grid=(ncol, nb), in_specs=in_specs,
        out_specs=[qspec] * len(out_dts),
        out_shape=[jax.ShapeDtypeStruct(q.shape, dt) for dt in out_dts],
        compiler_params=_params(("parallel", "parallel")),
    )(*args)


def _attn_bwd(q, k, v, o, lse, do, *, nq, max_dist, name, sinks=None):
    L, Ck = k.shape
    nb, ncol = L // BLK, Ck // LANES
    scale = HEAD ** -0.5
    has_sink = sinks is not None
    nt = (((1,), (1,)), ((), ()))
    tn = (((0,), (0,)), ((), ()))

    def body(*refs):
        (qc_ref, qn_ref, kp_ref, kc_ref, vp_ref, vc_ref, oc_ref, on_ref, lc_ref, ln_ref, dc_ref, dn_ref) = refs[:12]
        sk_ref = refs[12] if has_sink else None
        outs = refs[12 + has_sink:]
        dq_ref, dk_ref, dv_ref = outs[:3]
        n = pl.program_id(1)
        kc, vc = kc_ref[...], vc_ref[...]
        kcat = jnp.concatenate([kp_ref[...], kc], axis=0)
        vcat = jnp.concatenate([vp_ref[...], vc], axis=0)
        r = lax.broadcasted_iota(jnp.int32, (BLK, 2 * BLK), 0)
        c = lax.broadcasted_iota(jnp.int32, (BLK, 2 * BLK), 1)
        dist = r + BLK - c
        valid_q = (dist >= 0) & (dist <= max_dist) & ((c >= BLK) | (n > 0))
        r2 = lax.broadcasted_iota(jnp.int32, (2 * BLK, BLK), 0)
        c2 = lax.broadcasted_iota(jnp.int32, (2 * BLK, BLK), 1)
        dist2 = r2 - c2
        valid_k = (dist2 >= 0) & (dist2 <= max_dist) & ((r2 < BLK) | (n < nb - 1))
        dk_acc = jnp.zeros((BLK, LANES), F32)
        dv_acc = jnp.zeros((BLK, LANES), F32)
        for i in range(nq):
            sl = slice(i * 128, (i + 1) * 128)
            qcur, docur = qc_ref[:, sl], dc_ref[:, sl]
            qcat = jnp.concatenate([qcur, qn_ref[:, sl]], axis=0)
            docat = jnp.concatenate([docur, dn_ref[:, sl]], axis=0)
            ocat = jnp.concatenate([oc_ref[:, sl], on_ref[:, sl]], axis=0)
            lcat = jnp.concatenate([lc_ref[:, sl], ln_ref[:, sl]], axis=0)
            dq2 = jnp.zeros((BLK, LANES), F32)
            ds2 = jnp.zeros((1, LANES), F32)
            for half in range(2):
                m1 = _head_mask((BLK, LANES), half)
                m2 = _head_mask((2 * BLK, LANES), half)
                dom = jnp.where(m2, docat, 0.0)
                delta = jnp.sum(dom * ocat, axis=1, keepdims=True)
                lcol = _col_of(lcat, m2)
                domb = dom.astype(BF16)
                qmcat = jnp.where(m2, qcat, jnp.zeros_like(qcat))
                qm = qmcat[:BLK]
                s = lax.dot_general(qm, kcat, nt, preferred_element_type=F32) * scale
                pr = jnp.where(valid_q, jnp.exp(s - lcol[:BLK]), 0.0)
                dp = lax.dot_general(domb[:BLK], vcat, nt, preferred_element_type=F32)
                ds = (pr * (dp - delta[:BLK])).astype(BF16)
                dq2 = jnp.where(m1, jnp.dot(ds, kcat, preferred_element_type=F32) * scale, dq2)
                if has_sink:
                    snk = _col_of(sk_ref[:, sl], _head_mask((1, LANES), half))
                    contrib = jnp.sum(-jnp.exp(snk - lcol[:BLK]) * delta[:BLK], axis=0, keepdims=True)
                    ds2 = jnp.where(_head_mask((1, LANES), half), contrib, ds2)
                s = lax.dot_general(qmcat, kc, nt, preferred_element_type=F32) * scale
                pr = jnp.where(valid_k, jnp.exp(s - lcol), 0.0)
                dv_acc += lax.dot_general(pr.astype(BF16), domb, tn, preferred_element_type=F32)
                dp = lax.dot_general(domb, vc, nt, preferred_element_type=F32)
                ds = (pr * (dp - delta)).astype(BF16)
                dk_acc += lax.dot_general(ds, qmcat, tn, preferred_element_type=F32) * scale
            dq_ref[:, sl] = dq2
            if has_sink:
                @pl.when(n == 0)
                def _():
                    outs[3][:, sl] = jnp.zeros((1, LANES), F32)

                outs[3][:, sl] += ds2
        dk_ref[...] = dk_acc
        dv_ref[...] = dv_acc

    qcur = pl.BlockSpec((BLK, nq * 128), lambda j, n: (n, j))
    qnext = pl.BlockSpec((BLK, nq * 128), lambda j, n: (jnp.minimum(n + 1, nb - 1), j))
    cur = pl.BlockSpec((BLK, 128), lambda j, n: (n, j))
    prev = pl.BlockSpec((BLK, 128), lambda j, n: (jnp.maximum(n - 1, 0), j))
    in_specs = [qcur, qnext, prev, cur, prev, cur, qcur, qnext, qcur, qnext, qcur, qnext]
    args = [q, q, k, k, v, v, o, o, lse, lse, do, do]
    out_specs = [qcur, cur, cur]
    out_shape = [jax.ShapeDtypeStruct(q.shape, F32), jax.ShapeDtypeStruct(k.shape, F32), jax.ShapeDtypeStruct(k.shape, F32)]
    if has_sink:
        vec = pl.BlockSpec((1, nq * 128), lambda j, n: (0, j))
        in_specs.append(vec)
        args.append(sinks)
        out_specs.append(vec)
        out_shape.append(jax.ShapeDtypeStruct((1, q.shape[1]), F32))
    return _pc(
        body, name=name, grid=(ncol, nb), in_specs=in_specs, out_specs=out_specs, out_shape=out_shape,
        compiler_params=_params(("parallel", "arbitrary")),
    )(*args)


def _even_post_fwd(ro, proj, gn, ops, lses):
    S = ro.shape[0]
    tm = 256
    npat = len(ops)

    def body(ro_ref, rg_ref, gn_ref, *rest):
        o_refs, l_refs = rest[:npat], rest[npat:2 * npat]
        mix_ref, da_ref, lse_ref = rest[2 * npat:]
        for c in range(4):
            sl = slice(c * 128, (c + 1) * 128)
            x = ro_ref[:, sl]
            mu = jnp.mean(x, axis=1, keepdims=True)
            xc = x - mu
            var = jnp.mean(xc * xc, axis=1, keepdims=True)
            y = xc * lax.rsqrt(var + EPS) * gn_ref[:, sl]
            z = rg_ref[:, sl]
            mix_ref[:, sl] = (z * jax.nn.sigmoid(z) * y).astype(BF16)
        ls = [r[...] for r in l_refs]
        mx = ls[0]
        for l in ls[1:]:
            mx = jnp.maximum(mx, l)
        ws = [jnp.exp(l - mx) for l in ls]
        tot = ws[0]
        for w in ws[1:]:
            tot = tot + w
        out = ws[0] * o_refs[0][...]
        for w, r in zip(ws[1:], o_refs[1:]):
            out = out + w * r[...]
        out = out / tot
        da_ref[...] = out
        lse_ref[...] = mx + jnp.log(tot)
        mix_ref[:, 512:1024] = out.astype(BF16)

    row = lambda w: pl.BlockSpec((tm, w), lambda i: (i, 0))
    return _pc(
        body, name="even_post_fwd", grid=(S // tm,),
        in_specs=[row(512), pl.BlockSpec((tm, 512), lambda i: (i, 2)), pl.BlockSpec((1, 512), lambda i: (0, 0))]
        + [row(512)] * (2 * npat),
        out_specs=[row(1024), row(512), row(512)],
        out_shape=[jax.ShapeDtypeStruct((S, 1024), BF16), jax.ShapeDtypeStruct((S, 512), F32),
                   jax.ShapeDtypeStruct((S, 512), F32)],
        compiler_params=_params(("parallel",)),
    )(ro, proj, gn, *ops, *lses)


def _even_post_bwd(ro, proj, gn, dmixed):
    S = ro.shape[0]
    tm = 256

    def body(ro_ref, rg_ref, gn_ref, dm_ref, dro_ref, drg_ref, dgn_ref):
        @pl.when(pl.program_id(0) == 0)
        def _():
            dgn_ref[...] = jnp.zeros_like(dgn_ref)

        for c in range(4):
            sl = slice(c * 128, (c + 1) * 128)
            x = ro_ref[:, sl]
            mu = jnp.mean(x, axis=1, keepdims=True)
            xc = x - mu
            rstd = lax.rsqrt(jnp.mean(xc * xc, axis=1, keepdims=True) + EPS)
            xh = xc * rstd
            gain = gn_ref[:, sl]
            y = xh * gain
            z = rg_ref[:, sl]
            sg = jax.nn.sigmoid(z)
            dra = dm_ref[:, sl]
            drg_ref[:, sl] = dra * y * sg * (1.0 + z * (1.0 - sg))
            dy = dra * z * sg
            dgn_ref[:, sl] += jnp.sum(dy * xh, axis=0, keepdims=True)
            dxh = dy * gain
            dro_ref[:, sl] = rstd * (dxh - jnp.mean(dxh, axis=1, keepdims=True)
                                     - xh * jnp.mean(dxh * xh, axis=1, keepdims=True))

    row = lambda w: pl.BlockSpec((tm, w), lambda i: (i, 0))
    vec = pl.BlockSpec((1, 512), lambda i: (0, 0))
    return _pc(
        body, name="even_post_bwd", grid=(S // tm,),
        in_specs=[row(512), pl.BlockSpec((tm, 512), lambda i: (i, 2)), vec, row(512)],
        out_specs=[row(512), row(512), vec],
        out_shape=[jax.ShapeDtypeStruct((S, 512), F32), jax.ShapeDtypeStruct((S, 512), F32),
                   jax.ShapeDtypeStruct((1, 512), F32)],
        compiler_params=_params(("arbitrary",)),
    )(ro, proj, gn, dmixed)


def _swa_pre_fwd(proj, tab, qg, kg):
    S = proj.shape[0]
    tm = 256

    def body(p_ref, tab_ref, qg_ref, kg_ref, g_ref, q_ref, k_ref, v_ref):
        Ap, Bp, Cp = _tab(tab_ref, 1)
        G = g_ref[...]
        lo = _head_mask((tm, LANES), 0)
        for c in range(8):
            sl = slice(c * 128, (c + 1) * 128)
            q_ref[:, sl] = _rope(_hn_fwd(p_ref[:, sl], qg_ref[...], G), Ap, Bp, Cp, 8).astype(BF16)
        for c in range(2):
            kn = _rope(_hn_fwd(p_ref[:, 1024 + c * 128:1024 + (c + 1) * 128], kg_ref[...], G), Ap, Bp, Cp, 8)
            vv = p_ref[:, 1280 + c * 128:1280 + (c + 1) * 128]
            for t, ref in ((kn, k_ref), (vv, v_ref)):
                sw = _roll(t, HEAD)
                ref[:, (2 * c) * 128:(2 * c + 1) * 128] = jnp.where(lo, t, sw).astype(BF16)
                ref[:, (2 * c + 1) * 128:(2 * c + 2) * 128] = jnp.where(lo, sw, t).astype(BF16)

    row = lambda w: pl.BlockSpec((tm, w), lambda i: (i, 0))
    vec = pl.BlockSpec((1, LANES), lambda i: (0, 0))
    return _pc(
        body, name="swa_pre_fwd", grid=(S // tm,),
        in_specs=[row(1536), row(768), vec, vec, pl.BlockSpec((LANES, LANES), lambda i: (0, 0))],
        out_specs=[row(1024), row(512), row(512)],
        out_shape=[jax.ShapeDtypeStruct((S, w), BF16) for w in (1024, 512, 512)],
        compiler_params=_params(("parallel",)),
    )(proj, tab, qg, kg, _group_matrix())


def _swa_pre_bwd(proj, tab, qg, kg, dq, dk, dv):
    S = proj.shape[0]
    tm = 256

    def body(p_ref, tab_ref, qg_ref, kg_ref, g_ref, dq_ref, dk_ref, dv_ref, dp_ref, db_ref, dqg_ref, dkg_ref):
        Ap, Bp, Cp = _tab(tab_ref, 1)
        G = g_ref[...]
        lo = _head_mask((tm, LANES), 0)

        @pl.when(pl.program_id(0) == 0)
        def _():
            db_ref[...] = jnp.zeros_like(db_ref)
            dqg_ref[...] = jnp.zeros_like(dqg_ref)
            dkg_ref[...] = jnp.zeros_like(dkg_ref)

        accq = jnp.zeros((1, LANES), F32)
        acck = jnp.zeros((1, LANES), F32)
        for c in range(8):
            sl = slice(c * 128, (c + 1) * 128)
            dx, dg = _hn_bwd(p_ref[:, sl], qg_ref[...], _rope_t(dq_ref[:, sl], Ap, Bp, Cp, 8), G)
            dp_ref[:, sl] = dx.astype(BF16)
            db_ref[:, sl] += jnp.sum(dx, axis=0, keepdims=True)
            accq = accq + dg
        for c in range(2):
            folded = []
            for ref in (dk_ref, dv_ref):
                a = ref[:, (2 * c) * 128:(2 * c + 1) * 128]
                b = ref[:, (2 * c + 1) * 128:(2 * c + 2) * 128]
                folded.append(jnp.where(lo, a + _roll(a, HEAD), b + _roll(b, HEAD)))
            ks = slice(1024 + c * 128, 1024 + (c + 1) * 128)
            dx, dg = _hn_bwd(p_ref[:, ks], kg_ref[...], _rope_t(folded[0], Ap, Bp, Cp, 8), G)
            dp_ref[:, ks] = dx.astype(BF16)
            db_ref[:, ks] += jnp.sum(dx, axis=0, keepdims=True)
            acck = acck + dg
            vs = slice(1280 + c * 128, 1280 + (c + 1) * 128)
            dp_ref[:, vs] = folded[1].astype(BF16)
            db_ref[:, vs] += jnp.sum(folded[1], axis=0, keepdims=True)
        dqg_ref[...] += _fold_halves(accq)
        dkg_ref[...] += _fold_halves(acck)

    row = lambda w: pl.BlockSpec((tm, w), lambda i: (i, 0))
    vec = pl.BlockSpec((1, LANES), lambda i: (0, 0))
    return _pc(
        body, name="swa_pre_bwd", grid=(S // tm,),
        in_specs=[row(1536), row(768), vec, vec, pl.BlockSpec((LANES, LANES), lambda i: (0, 0)),
                  row(1024), row(512), row(512)],
        out_specs=[row(1536), pl.BlockSpec((1, 1536), lambda i: (0, 0)), vec, vec],
        out_shape=[jax.ShapeDtypeStruct((S, 1536), BF16), jax.ShapeDtypeStruct((1, 1536), F32),
                   jax.ShapeDtypeStruct((1, LANES), F32), jax.ShapeDtypeStruct((1, LANES), F32)],
        compiler_params=_params(("arbitrary",)),
    )(proj, tab, qg, kg, _group_matrix(), dq, dk, dv)


def _loss_head(y, target):
    S, Dm = y.shape
    tm = 512

    def body(y_ref, t_ref, l_ref, dy_ref):
        @pl.when(pl.program_id(0) == 0)
        def _():
            l_ref[...] = jnp.zeros_like(l_ref)

        e = y_ref[...] - t_ref[...]
        dy_ref[...] = e * (1.0 / Dm)
        row = jnp.sum(e * e, axis=1, keepdims=True) * (0.5 / Dm)
        l_ref[...] += jnp.sum(row, axis=0, keepdims=True)

    row = pl.BlockSpec((tm, Dm), lambda i: (i, 0))
    return _pc(
        body, name="loss_head", grid=(S // tm,), in_specs=[row, row],
        out_specs=[pl.BlockSpec((1, LANES), lambda i: (0, 0)), row],
        out_shape=[jax.ShapeDtypeStruct((1, LANES), F32), jax.ShapeDtypeStruct((S, Dm), F32)],
        compiler_params=_params(("arbitrary",)),
    )(y, target)


def _relu2(acc):
    r = jnp.maximum(acc, 0.0)
    return acc, r * r


def _drelu2(acc, u):
    return (acc * 2.0 * jnp.maximum(u.astype(F32), 0.0),)


def _add(acc, res):
    return (acc + res,)


def _mlp_fwd(x, g, w_up, w_dn, tag):
    h = _rms_fwd(x, g, f"rms_mlp_fwd{tag}")
    u, a = _matmul(h, w_up, dims="nn", tm=512, tn=512, tk=1024, outs=[BF16, BF16], b_cs=True, epilogue=_relu2,
                   name=f"mlp_up{tag}")
    x_out = _matmul(a, w_dn, dims="nn", tm=512, tn=512, tk=1024, outs=[F32], epilogue=_add, extras=[(x, "mn")],
                    name=f"mlp_down{tag}")
    return x_out, (h, u, a)


def _mlp_bwd(x, g, w_up, w_dn, saved, dy, tag):
    h, u, a = saved
    du = _matmul(dy, w_dn, dims="nt", tm=512, tn=512, tk=1024, outs=[BF16], epilogue=_drelu2, extras=[(u, "mn")],
                 name=f"mlp_du{tag}")
    dw_dn = _matmul(a, dy, dims="tn", tm=512, tn=512, tk=1024, outs=[F32], name=f"mlp_dwdown{tag}")
    dw_up = _matmul(h, du, dims="tn", tm=512, tn=512, tk=1024, outs=[F32], o_cs=N_CHIPS, name=f"mlp_dwup{tag}")
    dh = _matmul(du, w_up, dims="nt", tm=512, tn=512, tk=1024, outs=[F32], b_cs=True, name=f"mlp_dh{tag}")
    dx, dg = _rms_bwd(x, g, dh, dy, f"rms_mlp_bwd{tag}")
    return dx, dg, dw_up, dw_dn


def _pattern_view(t, r):
    S, C = t.shape
    return t.reshape(S // r, r * C)


def _local_step(x, pos_col, target, W, P):
    S = x.shape[0]
    tab = _tables(pos_col)
    tile2 = lambda g: jnp.tile(g.reshape(1, HEAD), (1, 2))
    dqg, dkg = tile2(P["dil_q_gain"]), tile2(P["dil_k_gain"])
    sqg, skg = tile2(P["swa_q_gain"]), tile2(P["swa_k_gain"])
    gn = P["ret_gn_gain"].reshape(1, 512)
    sink_b = jnp.repeat(P["swa_sinks"].reshape(16), HEAD).reshape(1, 1024)

    h0 = _rms_fwd(x, P["norm_mix"][0], "rms_mix_fwd0")
    proj = _matmul(h0, W["hyb_w_in"], dims="nn", tm=512, tn=384, tk=1024, outs=[F32], b_cs=True, name="hyb_in")
    rq, rk, rv, dq, dk, dv = _even_pre_fwd(proj, tab, dqg, dkg)
    ro, states = _ret_fwd(rq, rk, rv)
    ops, lses = [], []
    for w, r in DIL_PATTERNS:
        o_p, l_p = _attn_fwd(_pattern_view(dq, r), _pattern_view(dk, r), _pattern_view(dv, r), nq=1, max_dist=w // r,
                             name=f"dil_fwd_r{r}")
        ops.append(o_p.reshape(S, 512))
        lses.append(l_p.reshape(S, 512))
    mixed, da, dlse = _even_post_fwd(ro, proj, gn, ops, lses)
    x1 = _matmul(mixed, W["hyb_w_out"], dims="nn", tm=512, tn=512, tk=1024, outs=[F32], epilogue=_add,
                 extras=[(x, "mn")], name="hyb_out")
    x2, mlp0 = _mlp_fwd(x1, P["norm_mlp"][0], W["mlp_w_up"][0], W["mlp_w_down"][0], "0")

    h2 = _rms_fwd(x2, P["norm_mix"][1], "rms_mix_fwd1")
    proj2 = _matmul(h2, W["swa_w_qkv"], dims="nn", tm=512, tn=384, tk=1024, outs=[F32], b_cs=True,
                    epilogue=_add, extras=[(P["swa_b_qkv_full"].reshape(1, 1536), "n")], name="swa_qkv")
    sq, sk, sv = _swa_pre_fwd(proj2, tab, sqg, skg)
    so, slse, so_b = _attn_fwd(sq, sk, sv, nq=2, max_dist=SWA_DIST, name="swa_fwd", sinks=sink_b, want_bf16=True)
    x3 = _matmul(so_b, W["swa_w_out"], dims="nn", tm=512, tn=512, tk=1024, outs=[F32], epilogue=_add,
                 extras=[(x2, "mn")], name="swa_out")
    y, mlp1 = _mlp_fwd(x3, P["norm_mlp"][1], W["mlp_w_up"][1], W["mlp_w_down"][1], "1")
    loss, dy = _loss_head(y, target)

    gw, gp = {}, {}
    dx3, dg_mlp1, gw["mlp_w_up1"], gw["mlp_w_down1"] = _mlp_bwd(x3, P["norm_mlp"][1], W["mlp_w_up"][1],
                                                                 W["mlp_w_down"][1], mlp1, dy, "1")
    gw["swa_w_out"] = _matmul(so_b, dx3, dims="tn", tm=512, tn=512, tk=1024, outs=[F32], name="swa_dwout")
    dso = _matmul(dx3, W["swa_w_out"], dims="nt", tm=512, tn=512, tk=1024, outs=[F32], name="swa_do")
    dsq, dsk, dsv, dsink = _attn_bwd(sq, sk, sv, so, slse, dso, nq=2, max_dist=SWA_DIST, name="swa_bwd", sinks=sink_b)
    dproj2, gp["swa_b_qkv"], gp["swa_q_gain"], gp["swa_k_gain"] = _swa_pre_bwd(proj2, tab, sqg, skg, dsq, dsk, dsv)
    gp["swa_sinks"] = dsink
    gw["swa_w_qkv"] = _matmul(h2, dproj2, dims="tn", tm=512, tn=384, tk=1024, outs=[F32], o_cs=N_CHIPS, name="swa_dwqkv")
    dh2 = _matmul(dproj2, W["swa_w_qkv"], dims="nt", tm=512, tn=512, tk=384, outs=[F32], b_cs=True, name="swa_dh")
    dx2, dg_mix1 = _rms_bwd(x2, P["norm_mix"][1], dh2, dx3, "rms_mix_bwd1")

    dx1, dg_mlp0, gw["mlp_w_up0"], gw["mlp_w_down0"] = _mlp_bwd(x1, P["norm_mlp"][0], W["mlp_w_up"][0],
                                                                 W["mlp_w_down"][0], mlp0, dx2, "0")
    gw["hyb_w_out"] = _matmul(mixed, dx1, dims="tn", tm=512, tn=512, tk=1024, outs=[F32], name="hyb_dwout")
    dmixed = _matmul(dx1, W["hyb_w_out"], dims="nt", tm=512, tn=512, tk=1024, outs=[F32], name="hyb_dmixed")
    dro, drg, gp["ret_gn_gain"] = _even_post_bwd(ro, proj, gn, dmixed)
    drq, drk, drv = _ret_bwd(rq, rk, rv, states, dro)
    dda = dmixed[:, 512:]
    dqs, dks, dvs = [], [], []
    for w, r in DIL_PATTERNS:
        a, b, c = _attn_bwd(_pattern_view(dq, r), _pattern_view(dk, r), _pattern_view(dv, r), _pattern_view(da, r),
                            _pattern_view(dlse, r), _pattern_view(dda, r), nq=1, max_dist=w // r, name=f"dil_bwd_r{r}")
        dqs.append(a.reshape(S, 512))
        dks.append(b.reshape(S, 512))
        dvs.append(c.reshape(S, 512))
    dproj, gp["dil_q_gain"], gp["dil_k_gain"] = _even_pre_bwd(proj, tab, dqg, dkg, drq, drk, drv, drg, dqs, dks, dvs)
    gw["hyb_w_in"] = _matmul(h0, dproj, dims="tn", tm=512, tn=384, tk=1024, outs=[F32], o_cs=N_CHIPS, name="hyb_dwin")
    dh0 = _matmul(dproj, W["hyb_w_in"], dims="nt", tm=512, tn=512, tk=768, outs=[F32], b_cs=True, name="hyb_dh")
    grad_x, dg_mix0 = _rms_bwd(x, P["norm_mix"][0], dh0, dx1, "rms_mix_bwd0")
    gp["norm_mix"] = jnp.concatenate([dg_mix0, dg_mix1], axis=0)
    gp["norm_mlp"] = jnp.concatenate([dg_mlp0, dg_mlp1], axis=0)
    return loss, grad_x, gw, gp


HBM = pl.BlockSpec(memory_space=pltpu.HBM)


def _place():
    x, y, c = lax.axis_index("x"), lax.axis_index("y"), lax.axis_index("c")
    chips = [(1 - x, y), (x, 1 - y), (1 - x, 1 - y)]
    return x, y, c, chips


def _allgather_shards(flat):
    R, Wd = flat.shape
    Rh = R // 2

    def body(x_ref, out_ref, send_sems, recv_sems, local_sem):
        x, y, c, chips = _place()
        j = 2 * x + y
        sibling = (x, y, 1 - c)
        mine = pltpu.make_async_copy(x_ref, out_ref.at[j], local_sem)
        mine.start()

        def block(chip, core):
            return out_ref.at[2 * chip[0] + chip[1], pl.ds(core * Rh, Rh), :]

        def copy(k, chip, core, to, src=None):
            return pltpu.make_async_remote_copy(
                src_ref=block(chip, core) if src is None else src, dst_ref=block(chip, core),
                send_sem=send_sems.at[k], recv_sem=recv_sems.at[k], device_id=to, device_id_type=MESH)

        first = [copy(k, (x, y), c, (*chip, c), src=x_ref.at[pl.ds(c * Rh, Rh), :]) for k, chip in enumerate(chips)]
        for cp in first:
            cp.start()
        passed = [copy(3 + k, chip, c, sibling) for k, chip in enumerate(chips)]
        for k, chip in enumerate(chips):
            copy(k, chip, c, (x, y, c)).wait_recv()
            passed[k].start()
        for k, chip in enumerate(chips):
            copy(3 + k, chip, 1 - c, (x, y, c)).wait_recv()
        for cp in first + passed:
            cp.wait_send()
        mine.wait()

    return _pc(
        body, name="allgather_weights", in_specs=[HBM], out_specs=HBM,
        out_shape=jax.ShapeDtypeStruct((N_CHIPS, R, Wd), flat.dtype),
        scratch_shapes=[pltpu.SemaphoreType.DMA((6,)), pltpu.SemaphoreType.DMA((6,)), pltpu.SemaphoreType.DMA],
    )(flat)


def _swap_halves(ts):
    nt = len(ts)

    def body(*refs):
        t_refs, l_refs, send_sems, recv_sems = refs[:nt], refs[nt:2 * nt], refs[-2], refs[-1]
        x, y, c, _ = _place()
        cps = []
        for k in range(nt):
            rh = t_refs[k].shape[1] // 2
            cps.append(pltpu.make_async_remote_copy(
                src_ref=t_refs[k].at[:, pl.ds((1 - c) * rh, rh), :], dst_ref=l_refs[k],
                send_sem=send_sems.at[k], recv_sem=recv_sems.at[k], device_id=(x, y, 1 - c), device_id_type=MESH))
        for cp in cps:
            cp.start()
        for cp in cps:
            cp.wait()

    return _pc(
        body, name="grad_swap_halves", in_specs=[HBM] * nt, out_specs=[HBM] * nt,
        out_shape=[jax.ShapeDtypeStruct((t.shape[0], t.shape[1] // 2, t.shape[2]), F32) for t in ts],
        scratch_shapes=[pltpu.SemaphoreType.DMA((nt,)), pltpu.SemaphoreType.DMA((nt,))],
    )(*ts)


def _pair_sum(t, l, place, name):
    _, r, cols = t.shape
    rh = r // 2
    tr = min(rh, 256)
    nr = rh // tr

    def body(pl_ref, t_ref, l_ref, o_ref):
        o_ref[...] = (t_ref[...] + l_ref[...]).astype(BF16)

    return _pc(
        body, name=name,
        grid_spec=pltpu.PrefetchScalarGridSpec(
            num_scalar_prefetch=1, grid=(N_CHIPS, nr),
            in_specs=[pl.BlockSpec((None, tr, cols), lambda s, i, p: (s, p[1] * nr + i, 0)),
                      pl.BlockSpec((None, tr, cols), lambda s, i, p: (s, i, 0))],
            out_specs=pl.BlockSpec((None, tr, cols), lambda s, i, p: (s, i, 0))),
        out_shape=jax.ShapeDtypeStruct((N_CHIPS, rh, cols), BF16),
        compiler_params=_params(("parallel", "parallel")),
    )(place, t, l)


def _exchange_chips(ps):
    nt = len(ps)

    def body(*refs):
        p_refs, r_refs, send_sems, recv_sems = refs[:nt], refs[nt:2 * nt], refs[-2], refs[-1]
        x, y, c, chips = _place()
        cps = []
        for t in range(nt):
            for k, chip in enumerate(chips):
                cps.append(pltpu.make_async_remote_copy(
                    src_ref=p_refs[t].at[2 * chip[0] + chip[1]], dst_ref=r_refs[t].at[k],
                    send_sem=send_sems.at[3 * t + k], recv_sem=recv_sems.at[3 * t + k],
                    device_id=(*chip, c), device_id_type=MESH))
        for cp in cps:
            cp.start()
        for cp in cps:
            cp.wait()

    return _pc(
        body, name="grad_exchange_chips", in_specs=[HBM] * nt, out_specs=[HBM] * nt,
        out_shape=[jax.ShapeDtypeStruct((3,) + p.shape[1:], BF16) for p in ps],
        scratch_shapes=[pltpu.SemaphoreType.DMA((3 * nt,)), pltpu.SemaphoreType.DMA((3 * nt,))],
    )(*ps)


def _final_sum(t, l, rcv, place, name):
    _, r, cols = t.shape
    rh = r // 2
    tr = min(rh, 256)
    nr = rh // tr

    def body(pl_ref, t_ref, l_ref, r_ref, o_ref):
        acc = t_ref[...] + l_ref[...]
        for k in range(3):
            acc = acc + r_ref[k].astype(F32)
        o_ref[...] = acc

    return _pc(
        body, name=name,
        grid_spec=pltpu.PrefetchScalarGridSpec(
            num_scalar_prefetch=1, grid=(nr,),
            in_specs=[pl.BlockSpec((None, tr, cols), lambda i, p: (p[0], p[1] * nr + i, 0)),
                      pl.BlockSpec((None, tr, cols), lambda i, p: (p[0], i, 0)),
                      pl.BlockSpec((3, tr, cols), lambda i, p: (0, i, 0))],
            out_specs=pl.BlockSpec((tr, cols), lambda i, p: (i, 0))),
        out_shape=jax.ShapeDtypeStruct((rh, cols), F32),
        compiler_params=_params(("parallel",)),
    )(place, t, l, rcv)


def _share_halves(hs):
    nt = len(hs)

    def body(*refs):
        h_refs, o_refs, send_sems, recv_sems, local_sems = refs[:nt], refs[nt:2 * nt], refs[-3], refs[-2], refs[-1]
        x, y, c, _ = _place()
        cps, loc = [], []
        for k in range(nt):
            rh = h_refs[k].shape[0]
            dst = o_refs[k].at[pl.ds(c * rh, rh), :]
            loc.append(pltpu.make_async_copy(h_refs[k], dst, local_sems.at[k]))
            cps.append(pltpu.make_async_remote_copy(
                src_ref=h_refs[k], dst_ref=dst, send_sem=send_sems.at[k], recv_sem=recv_sems.at[k],
                device_id=(x, y, 1 - c), device_id_type=MESH))
        for cp in loc + cps:
            cp.start()
        for cp in loc + cps:
            cp.wait()

    return _pc(
        body, name="grad_share_halves", in_specs=[HBM] * nt, out_specs=[HBM] * nt,
        out_shape=[jax.ShapeDtypeStruct((2 * h.shape[0], h.shape[1]), F32) for h in hs],
        scratch_shapes=[pltpu.SemaphoreType.DMA((nt,)), pltpu.SemaphoreType.DMA((nt,)), pltpu.SemaphoreType.DMA((nt,))],
    )(*hs)


def _allgather_small(v):
    rows = v.shape[0]

    def body(v_ref, out_ref, send_sems, recv_sems):
        x, y, c, _ = _place()
        me = 4 * x + 2 * y + c
        out_ref[me] = v_ref[...]
        cps = []
        for k in range(1, 8):
            fx, fy, fc = (k >> 2) & 1, (k >> 1) & 1, k & 1
            to = (1 - x if fx else x, 1 - y if fy else y, 1 - c if fc else c)
            cps.append(pltpu.make_async_remote_copy(
                src_ref=v_ref, dst_ref=out_ref.at[me], send_sem=send_sems.at[k - 1], recv_sem=recv_sems.at[k - 1],
                device_id=to, device_id_type=MESH))
        for cp in cps:
            cp.start()
        for cp in cps:
            cp.wait()

    return _pc(
        body, name="allgather_small",
        in_specs=[pl.BlockSpec(memory_space=pltpu.VMEM)], out_specs=pl.BlockSpec(memory_space=pltpu.VMEM),
        out_shape=jax.ShapeDtypeStruct((8, rows, LANES), F32),
        scratch_shapes=[pltpu.SemaphoreType.DMA((7,)), pltpu.SemaphoreType.DMA((7,))],
    )(v)


def _adamw_math(w, g, m, v):
    m = ADAM_B1 * m + (1.0 - ADAM_B1) * g
    v = ADAM_B2 * v + (1.0 - ADAM_B2) * (g * g)
    m_hat = m / (1.0 - ADAM_B1 ** ADAM_STEP)
    v_hat = v / (1.0 - ADAM_B2 ** ADAM_STEP)
    return -ADAM_LR * (m_hat / (jnp.sqrt(v_hat) + ADAM_EPS) + ADAM_WD * w), m, v


def _adamw(w, g, m, v, name):
    r, cols = w.shape
    tr = min(r, 256)

    def body(w_ref, g_ref, m_ref, v_ref, d_ref, mo_ref, vo_ref):
        d, mn, vn = _adamw_math(w_ref[...], g_ref[...], m_ref[...], v_ref[...])
        d_ref[...] = d
        mo_ref[...] = mn
        vo_ref[...] = vn

    row = pl.BlockSpec((tr, cols), lambda i: (i, 0))
    return _pc(
        body, name=name, grid=(r // tr,), in_specs=[row] * 4, out_specs=[row] * 3,
        out_shape=[jax.ShapeDtypeStruct((r, cols), F32)] * 3,
        compiler_params=_params(("parallel",)),
    )(w, g, m, v)


def _adamw_small(w, gathered, m, v):
    rows = w.shape[0]

    def body(w_ref, g_ref, m_ref, v_ref, go_ref, d_ref, mo_ref, vo_ref):
        g = g_ref[0]
        for k in range(1, 8):
            g = g + g_ref[k]
        d, mn, vn = _adamw_math(w_ref[...], g, m_ref[...], v_ref[...])
        go_ref[...] = g
        d_ref[...] = d
        mo_ref[...] = mn
        vo_ref[...] = vn

    return _pc(
        body, name="adamw_small",
        out_shape=[jax.ShapeDtypeStruct((rows, LANES), F32)] * 4,
    )(w, gathered, m, v)


_BIAS_ROWS = 32


def _pack_shard(mlp_w_up, mlp_w_down, hyb_w_in, hyb_w_out, swa_w_qkv, swa_w_out, swa_b_qkv):
    parts = [t.astype(BF16).reshape(-1, 1024) for t in (mlp_w_up, mlp_w_down, hyb_w_in, hyb_w_out, swa_w_qkv, swa_w_out)]
    bias = lax.bitcast_convert_type(swa_b_qkv.reshape(384), BF16).reshape(1, 768)
    bias = jnp.pad(bias, ((0, _BIAS_ROWS - 1), (0, 256)))
    return jnp.concatenate(parts + [bias], axis=0)


def _unpack_weights(g):
    W = {
        "mlp_w_up": [g[:, l * 1024:(l + 1) * 1024, :] for l in range(2)],
        "mlp_w_down": [g[:, 2048 + l * 1024:2048 + (l + 1) * 1024, :].reshape(D_FF, D_MODEL) for l in range(2)],
        "hyb_w_in": g[:, 4096:4864, :].reshape(N_CHIPS, 1024, 768),
        "hyb_w_out": g[:, 4864:5120, :].reshape(1024, 1024),
        "swa_w_qkv": g[:, 5120:5504, :].reshape(N_CHIPS, 1024, 384),
        "swa_w_out": g[:, 5504:5760, :].reshape(1024, 1024),
    }
    bias = lax.bitcast_convert_type(g[:, 5760, :768].reshape(N_CHIPS, 384, 2), F32).reshape(1536)
    return W, bias


_SMALL = (("norm_mix", 16), ("norm_mlp", 16), ("ret_gn_gain", 4), ("dil_q_gain", 1), ("dil_k_gain", 1),
          ("swa_b_qkv", 12), ("swa_q_gain", 1), ("swa_k_gain", 1), ("swa_sinks", 1))
_SMALL_ROWS = 56


def _pack_small(d):
    rows = [d[n].reshape(r, LANES) for n, r in _SMALL]
    used = sum(r for _, r in _SMALL)
    return jnp.concatenate(rows + [jnp.zeros((_SMALL_ROWS - used, LANES), F32)], axis=0)


def _unpack_small(p):
    out, o = {}, 0
    for n, r in _SMALL:
        out[n] = p[o:o + r]
        o += r
    return out


def kernel(x, positions, norm_mix, norm_mlp, mlp_w_up, mlp_w_down, hyb_w_in, hyb_w_out, ret_gn_gain, dil_q_gain, dil_k_gain, swa_w_qkv, swa_b_qkv, swa_w_out, swa_q_gain, swa_k_gain, swa_sinks, loss_target, m_norm_mix, m_norm_mlp, m_mlp_w_up, m_mlp_w_down, m_hyb_w_in, m_hyb_w_out, m_ret_gn_gain, m_dil_q_gain, m_dil_k_gain, m_swa_w_qkv, m_swa_b_qkv, m_swa_w_out, m_swa_q_gain, m_swa_k_gain, m_swa_sinks, v_norm_mix, v_norm_mlp, v_mlp_w_up, v_mlp_w_down, v_hyb_w_in, v_hyb_w_out, v_ret_gn_gain, v_dil_q_gain, v_dil_k_gain, v_swa_w_qkv, v_swa_b_qkv, v_swa_w_out, v_swa_q_gain, v_swa_k_gain, v_swa_sinks):
    ax, ay, ac = lax.axis_index("x"), lax.axis_index("y"), lax.axis_index("c")
    chip = 2 * ax + ay
    place = jnp.stack([chip, ac]).astype(jnp.int32)
    S = x.shape[1]

    flat = _pack_shard(mlp_w_up, mlp_w_down, hyb_w_in[0], hyb_w_out[0], swa_w_qkv[0], swa_w_out[0], swa_b_qkv[0])
    W, bias_full = _unpack_weights(_allgather_shards(flat))
    P = dict(norm_mix=norm_mix, norm_mlp=norm_mlp, ret_gn_gain=ret_gn_gain, dil_q_gain=dil_q_gain, dil_k_gain=dil_k_gain,
             swa_q_gain=swa_q_gain, swa_k_gain=swa_k_gain, swa_sinks=swa_sinks, swa_b_qkv_full=bias_full)

    loss_l, grad_x, gw, gp = _local_step(x[0], positions.reshape(S, 1), loss_target[0], W, P)
    loss = lax.psum(loss_l[0, 0], ("x", "y", "c"))

    slab = lambda t, r: t.reshape(N_CHIPS, r, t.size // (N_CHIPS * r))
    names = ["mlp_w_up0", "mlp_w_up1", "mlp_w_down0", "mlp_w_down1", "hyb_w_in", "hyb_w_out", "swa_w_qkv", "swa_w_out"]
    rows = [1024, 1024, 1024, 1024, 1024, 256, 1024, 256]
    ts = [slab(gw[n], r) for n, r in zip(names, rows)]
    ls = _swap_halves(ts)
    ps = [_pair_sum(t, l, place, f"pair_sum_{n}") for t, l, n in zip(ts, ls, names)]
    rs = _exchange_chips(ps)
    hs = [_final_sum(t, l, r, place, f"final_sum_{n}") for t, l, r, n in zip(ts, ls, rs, names)]
    gs = dict(zip(names, _share_halves(hs)))
    shards = dict(mlp_w_up0=(mlp_w_up[0], m_mlp_w_up[0], v_mlp_w_up[0]), mlp_w_up1=(mlp_w_up[1], m_mlp_w_up[1], v_mlp_w_up[1]),
                  mlp_w_down0=(mlp_w_down[0], m_mlp_w_down[0], v_mlp_w_down[0]),
                  mlp_w_down1=(mlp_w_down[1], m_mlp_w_down[1], v_mlp_w_down[1]),
                  hyb_w_in=(hyb_w_in[0], m_hyb_w_in[0], v_hyb_w_in[0]), hyb_w_out=(hyb_w_out[0], m_hyb_w_out[0], v_hyb_w_out[0]),
                  swa_w_qkv=(swa_w_qkv[0], m_swa_w_qkv[0], v_swa_w_qkv[0]), swa_w_out=(swa_w_out[0], m_swa_w_out[0], v_swa_w_out[0]))
    big = {}
    for n in names:
        w, m, v = shards[n]
        big[n] = (gs[n],) + tuple(_adamw(w, gs[n], m, v, f"adamw_{n}"))

    def big_out(n, k):
        if n in ("mlp_w_up", "mlp_w_down"):
            return jnp.stack([big[n + "0"][k], big[n + "1"][k]])
        return big[n][k][None]

    gsm = dict(gp)
    gsm["swa_sinks"] = jnp.pad(gp["swa_sinks"].reshape(16, HEAD)[:, 0], (0, LANES - 16))
    gathered = _allgather_small(_pack_small(gsm))

    def small_pack(norm_mix, norm_mlp, gn, dq, dk, b, sq, sk, sinks):
        dup = lambda t: jnp.tile(t.reshape(1, HEAD), (1, 2))
        bias = lax.dynamic_update_slice(jnp.zeros((12, LANES), F32), b.reshape(3, LANES), (3 * chip, 0))
        return _pack_small(dict(norm_mix=norm_mix, norm_mlp=norm_mlp, ret_gn_gain=gn, dil_q_gain=dup(dq), dil_k_gain=dup(dk),
                                swa_b_qkv=bias, swa_q_gain=dup(sq), swa_k_gain=dup(sk),
                                swa_sinks=jnp.pad(sinks.reshape(16), (0, LANES - 16))))

    pw = small_pack(norm_mix, norm_mlp, ret_gn_gain, dil_q_gain, dil_k_gain, swa_b_qkv, swa_q_gain, swa_k_gain, swa_sinks)
    pm = small_pack(m_norm_mix, m_norm_mlp, m_ret_gn_gain, m_dil_q_gain, m_dil_k_gain, m_swa_b_qkv, m_swa_q_gain, m_swa_k_gain, m_swa_sinks)
    pv = small_pack(v_norm_mix, v_norm_mlp, v_ret_gn_gain, v_dil_q_gain, v_dil_k_gain, v_swa_b_qkv, v_swa_q_gain, v_swa_k_gain, v_swa_sinks)
    small = [_unpack_small(t) for t in _adamw_small(pw, gathered, pm, pv)]

    def small_out(n, k):
        t = small[k][n]
        if n in ("norm_mix", "norm_mlp"):
            return t.reshape(2, D_MODEL)
        if n == "ret_gn_gain":
            return t.reshape(1, RET_HEADS, 128)
        if n == "swa_b_qkv":
            return lax.dynamic_slice(t, (3 * chip, 0), (3, LANES)).reshape(1, 384)
        if n == "swa_sinks":
            return t[0, :16].reshape(1, 16)
        return t[0, :HEAD].reshape(1, HEAD)

    order = ["norm_mix", "norm_mlp", "mlp_w_up", "mlp_w_down", "hyb_w_in", "hyb_w_out", "ret_gn_gain", "dil_q_gain",
             "dil_k_gain", "swa_w_qkv", "swa_b_qkv", "swa_w_out", "swa_q_gain", "swa_k_gain", "swa_sinks"]
    is_big = {"mlp_w_up", "mlp_w_down", "hyb_w_in", "hyb_w_out", "swa_w_qkv", "swa_w_out"}
    outs = [loss, grad_x[None]]
    for k in range(4):
        outs += [big_out(n, k) if n in is_big else small_out(n, k) for n in order]
    return tuple(outs)
```

```python
import functools
import math

import numpy as np
import jax
import jax.numpy as jnp
from jax import lax
from jax.experimental import pallas as pl
from jax.experimental.pallas import tpu as pltpu

F32, BF16 = jnp.float32, jnp.bfloat16
HIGHEST = lax.Precision.HIGHEST
MESH = pl.DeviceIdType.MESH

LANES = 128
VMEM_LIMIT = 48 << 20
D_MODEL = 1024
D_FF = 4096
HEAD = 64
EPS = 1e-6
BLK = 128
RET_HEADS = 4
RET_THETA = 10000.0
ROPE_THETA = 500000.0
ROPE_DIMS = 16
DIL_PATTERNS = ((128, 1), (512, 4), (2048, 16))
SWA_DIST = 127
N_CHIPS = 4
ADAM_LR, ADAM_B1, ADAM_B2, ADAM_EPS, ADAM_WD, ADAM_STEP = 0.001, 0.9, 0.999, 1e-08, 0.01, 10

_LOG_GAMMA = [float(np.log1p(-np.exp2(np.float32(-5.0 - h)))) for h in range(RET_HEADS)]


def _pc(body, **kw):
    return pl.pallas_call(body, **kw)


def _params(sem):
    return pltpu.CompilerParams(dimension_semantics=sem, vmem_limit_bytes=VMEM_LIMIT)


def _matmul(a, b, *, dims, tm, tn, tk, outs, name, epilogue=None, extras=(), b_cs=False, o_cs=0, a_pro=None):
    if dims == "nn":
        M, K = a.shape
        N = b.shape[0] * b.shape[2] if b_cs else b.shape[1]
        a_spec = pl.BlockSpec((tm, tk), lambda i, j, k: (i, k))
        if b_cs:
            npt = b.shape[2] // tn
            b_spec = pl.BlockSpec((None, tk, tn), lambda i, j, k: (j // npt, k, j % npt))
        else:
            b_spec = pl.BlockSpec((tk, tn), lambda i, j, k: (k, j))
        contract = (((1,), (0,)), ((), ()))
    elif dims == "nt":
        M, K = a.shape
        N = b.shape[1] if b_cs else b.shape[0]
        a_spec = pl.BlockSpec((tm, tk), lambda i, j, k: (i, k))
        if b_cs:
            kpt = b.shape[2] // tk
            b_spec = pl.BlockSpec((None, tn, tk), lambda i, j, k: (k // kpt, j, k % kpt))
        else:
            b_spec = pl.BlockSpec((tn, tk), lambda i, j, k: (j, k))
        contract = (((1,), (1,)), ((), ()))
    else:
        K, M = a.shape
        N = b.shape[1]
        a_spec = pl.BlockSpec((tk, tm), lambda i, j, k: (k, i))
        b_spec = pl.BlockSpec((tk, tn), lambda i, j, k: (k, j))
        contract = (((0,), (0,)), ((), ()))
    assert M % tm == 0 and N % tn == 0 and K % tk == 0, (name, M, N, K, tm, tn, tk)
    nk = K // tk
    ex_specs = []
    for arr, kind in extras:
        if kind == "mn":
            ex_specs.append(pl.BlockSpec((tm, tn), lambda i, j, k: (i, j)))
        else:
            ex_specs.append(pl.BlockSpec((1, tn), lambda i, j, k: (0, j)))
    if o_cs:
        n_sh = N // o_cs
        opt = n_sh // tn
        o_shape = (o_cs, M, n_sh)
        o_spec = pl.BlockSpec((None, tm, tn), lambda i, j, k: (j // opt, i, j % opt))
    else:
        o_shape = (M, N)
        o_spec = pl.BlockSpec((tm, tn), lambda i, j, k: (i, j))
    n_ex, n_out = len(extras), len(outs)
    if epilogue is None:
        epilogue = lambda acc: (acc,)

    def body(a_ref, b_ref, *rest):
        ex, o_refs, acc = rest[:n_ex], rest[n_ex:n_ex + n_out], rest[-1]
        k = pl.program_id(2)

        @pl.when(k == 0)
        def _():
            acc[...] = jnp.zeros_like(acc)

        av = a_ref[...] if a_pro is None else a_pro(a_ref[...])
        acc[...] += lax.dot_general(av.astype(BF16), b_ref[...].astype(BF16), contract, preferred_element_type=F32)

        @pl.when(k == nk - 1)
        def _():
            vals = epilogue(acc[...], *[e[...] for e in ex])
            for r, v in zip(o_refs, vals):
                r[...] = v.astype(r.dtype)

    res = _pc(
        body, name=name, grid=(M // tm, N // tn, nk),
        in_specs=[a_spec, b_spec] + ex_specs,
        out_specs=[o_spec] * n_out,
        out_shape=[jax.ShapeDtypeStruct(o_shape, dt) for dt in outs],
        scratch_shapes=[pltpu.VMEM((tm, tn), F32)],
        compiler_params=_params(("parallel", "parallel", "arbitrary")),
    )(a, b, *[e for e, _ in extras])
    return res[0] if n_out == 1 else res


def _roll(x, s):
    return pltpu.roll(x, s % LANES, 1)


def _rope(x, A, B, C, half):
    return x * A + _roll(x, LANES - half) * B + _roll(x, half) * C


def _rope_t(g, A, B, C, half):
    return g * A + _roll(g * B, half) + _roll(g * C, LANES - half)


def _gmean(x, G):
    return jnp.dot(x, G, precision=HIGHEST, preferred_element_type=F32)


def _head_mask(shape, half):
    lane = lax.broadcasted_iota(jnp.int32, shape, len(shape) - 1)
    return (lane >= HEAD) if half else (lane < HEAD)


def _group_matrix():
    i = np.arange(LANES)
    return jnp.asarray((i[:, None] // HEAD == i[None, :] // HEAD).astype(np.float32) / HEAD)


def _rope_inv():
    l = np.arange(LANES) % HEAD
    inv_r = np.power(np.float32(RET_THETA), -(l % 32).astype(np.float32) * np.float32(2.0 / HEAD))
    hp = ROPE_DIMS // 2
    inv_p = np.power(np.float32(ROPE_THETA), -(l % hp).astype(np.float32) * np.float32(2.0 / ROPE_DIMS))
    inv_p = np.where(l < ROPE_DIMS, inv_p, 0.0)
    return jnp.asarray(np.stack([inv_r, inv_p]).astype(np.float32))


def _tables(pos_col):
    S = pos_col.shape[0]
    tm = 512
    hp = ROPE_DIMS // 2

    def body(p_ref, inv_ref, o_ref):
        p = p_ref[...].astype(F32)
        lane = lax.broadcasted_iota(jnp.int32, (tm, LANES), 1) % HEAD
        ang = p * inv_ref[0:1, :]
        c, s = jnp.cos(ang), jnp.sin(ang)
        o_ref[:, 0:128] = c
        o_ref[:, 128:256] = jnp.where(lane < 32, -s, 0.0)
        o_ref[:, 256:384] = jnp.where(lane >= 32, s, 0.0)
        ang = p * inv_ref[1:2, :]
        c, s = jnp.cos(ang), jnp.sin(ang)
        o_ref[:, 384:512] = c
        o_ref[:, 512:640] = jnp.where(lane < hp, -s, 0.0)
        o_ref[:, 640:768] = jnp.where((lane >= hp) & (lane < ROPE_DIMS), s, 0.0)

    return _pc(
        body, name="rope_tables", grid=(S // tm,),
        in_specs=[pl.BlockSpec((tm, 1), lambda i: (i, 0)), pl.BlockSpec((2, LANES), lambda i: (0, 0))],
        out_specs=pl.BlockSpec((tm, 768), lambda i: (i, 0)),
        out_shape=jax.ShapeDtypeStruct((S, 768), F32),
        compiler_params=_params(("parallel",)),
    )(pos_col, _rope_inv())


def _tab(tab_ref, which):
    o = 384 * which
    return tab_ref[:, o:o + 128], tab_ref[:, o + 128:o + 256], tab_ref[:, o + 256:o + 384]


def _rms_fwd(x, g, name):
    S, Dm = x.shape
    tm = 512

    def body(x_ref, g_ref, h_ref):
        xv = x_ref[...]
        r = lax.rsqrt(jnp.mean(xv * xv, axis=-1, keepdims=True) + EPS)
        h_ref[...] = (xv * r * g_ref[...]).astype(BF16)

    return _pc(
        body, name=name, grid=(S // tm,),
        in_specs=[pl.BlockSpec((tm, Dm), lambda i: (i, 0)), pl.BlockSpec((1, Dm), lambda i: (0, 0))],
        out_specs=pl.BlockSpec((tm, Dm), lambda i: (i, 0)),
        out_shape=jax.ShapeDtypeStruct((S, Dm), BF16),
        compiler_params=_params(("parallel",)),
    )(x, g.reshape(1, Dm))


def _rms_bwd(x, g, dh, dres, name):
    S, Dm = x.shape
    tm = 512

    def body(x_ref, g_ref, dh_ref, dres_ref, dx_ref, dxb_ref, dg_ref):
        xv, dhv = x_ref[...], dh_ref[...]
        r = lax.rsqrt(jnp.mean(xv * xv, axis=-1, keepdims=True) + EPS)
        t = dhv * g_ref[...]
        dx = dres_ref[...] + r * t - xv * (r * r * r) * jnp.mean(xv * t, axis=-1, keepdims=True)
        dx_ref[...] = dx
        dxb_ref[...] = dx.astype(BF16)

        @pl.when(pl.program_id(0) == 0)
        def _():
            dg_ref[...] = jnp.zeros_like(dg_ref)

        dg_ref[...] += jnp.sum(dhv * xv * r, axis=0, keepdims=True)

    row = pl.BlockSpec((tm, Dm), lambda i: (i, 0))
    vec = pl.BlockSpec((1, Dm), lambda i: (0, 0))
    return _pc(
        body, name=name, grid=(S // tm,),
        in_specs=[row, vec, row, row], out_specs=[row, row, vec],
        out_shape=[jax.ShapeDtypeStruct((S, Dm), F32), jax.ShapeDtypeStruct((S, Dm), BF16),
                   jax.ShapeDtypeStruct((1, Dm), F32)],
        compiler_params=_params(("arbitrary",)),
    )(x, g.reshape(1, Dm), dh, dres)


def _hn_fwd(x, gain, G):
    r = lax.rsqrt(_gmean(x * x, G) + EPS)
    return x * r * gain


def _hn_bwd(x, gain, dy, G):
    r = lax.rsqrt(_gmean(x * x, G) + EPS)
    t = dy * gain
    dx = r * t - x * (r * r * r) * _gmean(x * t, G)
    return dx, jnp.sum(dy * x * r, axis=0, keepdims=True)


def _fold_halves(v):
    return v + _roll(v, HEAD)


def _even_pre_fwd(proj, tab, qg, kg):
    S = proj.shape[0]
    tm = 256

    def body(p_ref, tab_ref, qg_ref, kg_ref, g_ref, rq_ref, rk_ref, rv_ref, dq_ref, dk_ref, dv_ref):
        Ar, Br, Cr = _tab(tab_ref, 0)
        Ap, Bp, Cp = _tab(tab_ref, 1)
        G = g_ref[...]
        for c in range(2):
            sl = slice(c * 128, (c + 1) * 128)
            rq_ref[:, sl] = _rope(p_ref[:, c * 128:(c + 1) * 128], Ar, Br, Cr, 32).astype(BF16)
            rk_ref[:, sl] = (_rope(p_ref[:, 256 + c * 128:256 + (c + 1) * 128], Ar, Br, Cr, 32) * 0.125).astype(BF16)
        rv_ref[...] = p_ref[:, 512:1024].astype(BF16)
        for c in range(4):
            sl = slice(c * 128, (c + 1) * 128)
            q = _hn_fwd(p_ref[:, 1536 + c * 128:1536 + (c + 1) * 128], qg_ref[...], G)
            dq_ref[:, sl] = _rope(q, Ap, Bp, Cp, 8).astype(BF16)
            k = _hn_fwd(p_ref[:, 2048 + c * 128:2048 + (c + 1) * 128], kg_ref[...], G)
            dk_ref[:, sl] = _rope(k, Ap, Bp, Cp, 8).astype(BF16)
        dv_ref[...] = p_ref[:, 2560:3072].astype(BF16)

    row = lambda w: pl.BlockSpec((tm, w), lambda i: (i, 0))
    vec = pl.BlockSpec((1, LANES), lambda i: (0, 0))
    return _pc(
        body, name="even_pre_fwd", grid=(S // tm,),
        in_specs=[row(3072), row(768), vec, vec, pl.BlockSpec((LANES, LANES), lambda i: (0, 0))],
        out_specs=[row(256), row(256), row(512), row(512), row(512), row(512)],
        out_shape=[jax.ShapeDtypeStruct((S, w), BF16) for w in (256, 256, 512, 512, 512, 512)],
        compiler_params=_params(("parallel",)),
    )(proj, tab, qg, kg, _group_matrix())


def _even_pre_bwd(proj, tab, qg, kg, drq, drk, drv, drg, dqs, dks, dvs):
    S = proj.shape[0]
    tm = 256
    npat = len(dqs)

    def body(p_ref, tab_ref, qg_ref, kg_ref, g_ref, drq_ref, drk_ref, drv_ref, drg_ref, *rest):
        dq_refs, dk_refs, dv_refs = rest[:npat], rest[npat:2 * npat], rest[2 * npat:3 * npat]
        dp_ref, dqg_ref, dkg_ref = rest[3 * npat:]
        Ar, Br, Cr = _tab(tab_ref, 0)
        Ap, Bp, Cp = _tab(tab_ref, 1)
        G = g_ref[...]
        for c in range(2):
            sl = slice(c * 128, (c + 1) * 128)
            dp_ref[:, c * 128:(c + 1) * 128] = _rope_t(drq_ref[:, sl], Ar, Br, Cr, 32).astype(BF16)
            dp_ref[:, 256 + c * 128:256 + (c + 1) * 128] = _rope_t(drk_ref[:, sl] * 0.125, Ar, Br, Cr, 32).astype(BF16)
        dp_ref[:, 512:1024] = drv_ref[...].astype(BF16)
        dp_ref[:, 1024:1536] = drg_ref[...].astype(BF16)
        accq = jnp.zeros((1, LANES), F32)
        acck = jnp.zeros((1, LANES), F32)
        for c in range(4):
            sl = slice(c * 128, (c + 1) * 128)
            g = dq_refs[0][:, sl]
            for r in dq_refs[1:]:
                g = g + r[:, sl]
            dx, dg = _hn_bwd(p_ref[:, 1536 + c * 128:1536 + (c + 1) * 128], qg_ref[...], _rope_t(g, Ap, Bp, Cp, 8), G)
            dp_ref[:, 1536 + c * 128:1536 + (c + 1) * 128] = dx.astype(BF16)
            accq = accq + dg
            g = dk_refs[0][:, sl]
            for r in dk_refs[1:]:
                g = g + r[:, sl]
            dx, dg = _hn_bwd(p_ref[:, 2048 + c * 128:2048 + (c + 1) * 128], kg_ref[...], _rope_t(g, Ap, Bp, Cp, 8), G)
            dp_ref[:, 2048 + c * 128:2048 + (c + 1) * 128] = dx.astype(BF16)
            acck = acck + dg
        g = dv_refs[0][...]
        for r in dv_refs[1:]:
            g = g + r[...]
        dp_ref[:, 2560:3072] = g.astype(BF16)

        @pl.when(pl.program_id(0) == 0)
        def _():
            dqg_ref[...] = jnp.zeros_like(dqg_ref)
            dkg_ref[...] = jnp.zeros_like(dkg_ref)

        dqg_ref[...] += _fold_halves(accq)
        dkg_ref[...] += _fold_halves(acck)

    row = lambda w: pl.BlockSpec((tm, w), lambda i: (i, 0))
    vec = pl.BlockSpec((1, LANES), lambda i: (0, 0))
    return _pc(
        body, name="even_pre_bwd", grid=(S // tm,),
        in_specs=[row(3072), row(768), vec, vec, pl.BlockSpec((LANES, LANES), lambda i: (0, 0)),
                  row(256), row(256), row(512), row(512)] + [row(512)] * (3 * npat),
        out_specs=[row(3072), vec, vec],
        out_shape=[jax.ShapeDtypeStruct((S, 3072), BF16), jax.ShapeDtypeStruct((1, LANES), F32),
                   jax.ShapeDtypeStruct((1, LANES), F32)],
        compiler_params=_params(("arbitrary",)),
    )(proj, tab, qg, kg, _group_matrix(), drq, drk, drv, drg, *dqs, *dks, *dvs)


def _ret_consts(pair, half):
    lg = jnp.where(pair == 0, _LOG_GAMMA[half], _LOG_GAMMA[2 + half]).astype(F32)
    i = lax.broadcasted_iota(jnp.int32, (BLK, BLK), 0)
    j = lax.broadcasted_iota(jnp.int32, (BLK, BLK), 1)
    diff = (i - j).astype(F32)
    decay = jnp.where(diff >= 0, jnp.exp(lg * jnp.maximum(diff, 0.0)), 0.0)
    t = lax.broadcasted_iota(jnp.int32, (BLK, 1), 0).astype(F32)
    xi = jnp.exp(lg * (t + 1.0))
    zeta = jnp.exp(lg * (BLK - 1.0 - t))
    cd = jnp.exp(jnp.full((1, 1), BLK, F32) * lg)
    return decay, xi, zeta, cd


def _ret_fwd(rq, rk, rv):
    S = rq.shape[0]
    nc = S // BLK

    def body(q_ref, k_ref, v_ref, o_ref, st_ref, R):
        p, n = pl.program_id(0), pl.program_id(1)

        @pl.when(n == 0)
        def _():
            R[...] = jnp.zeros_like(R)

        q2, k2 = q_ref[...], k_ref[...]
        for half in range(2):
            decay, xi, zeta, cd = _ret_consts(p, half)
            m = _head_mask((BLK, LANES), half)
            qm = jnp.where(m, q2, jnp.zeros_like(q2))
            km = jnp.where(m, k2, jnp.zeros_like(k2))
            v = v_ref[:, half * 128:(half + 1) * 128]
            Rb = R[half].astype(BF16)
            st_ref[half] = Rb
            sc = lax.dot_general(qm, k2, (((1,), (1,)), ((), ())), preferred_element_type=F32) * decay
            o = jnp.dot(sc.astype(BF16), v, preferred_element_type=F32)
            o = o + jnp.dot(qm, Rb, preferred_element_type=F32) * xi
            o_ref[:, half * 128:(half + 1) * 128] = o
            kz = (km.astype(F32) * zeta).astype(BF16)
            R[half] = R[half] * cd + lax.dot_general(kz, v, (((0,), (0,)), ((), ())), preferred_element_type=F32)

    return _pc(
        body, name="ret_fwd", grid=(2, nc),
        in_specs=[pl.BlockSpec((BLK, 128), lambda p, n: (n, p)), pl.BlockSpec((BLK, 128), lambda p, n: (n, p)),
                  pl.BlockSpec((BLK, 256), lambda p, n: (n, p))],
        out_specs=[pl.BlockSpec((BLK, 256), lambda p, n: (n, p)),
                   pl.BlockSpec((None, None, 2, 128, 128), lambda p, n: (p, n, 0, 0, 0))],
        out_shape=[jax.ShapeDtypeStruct((S, 512), F32), jax.ShapeDtypeStruct((2, nc, 2, 128, 128), BF16)],
        scratch_shapes=[pltpu.VMEM((2, 128, 128), F32)],
        compiler_params=_params(("parallel", "arbitrary")),
    )(rq, rk, rv)


def _ret_bwd(rq, rk, rv, states, do):
    S = rq.shape[0]
    nc = S // BLK

    def body(q_ref, k_ref, v_ref, st_ref, do_ref, dq_ref, dk_ref, dv_ref, U):
        p, n = pl.program_id(0), pl.program_id(1)

        @pl.when(n == 0)
        def _():
            U[...] = jnp.zeros_like(U)

        q2, k2 = q_ref[...], k_ref[...]
        dq_acc = jnp.zeros((BLK, LANES), F32)
        dk_acc = jnp.zeros((BLK, LANES), F32)
        for half in range(2):
            decay, xi, zeta, cd = _ret_consts(p, half)
            m = _head_mask((BLK, LANES), half)
            qm = jnp.where(m, q2, jnp.zeros_like(q2))
            km = jnp.where(m, k2, jnp.zeros_like(k2))
            v = v_ref[:, half * 128:(half + 1) * 128]
            dob = do_ref[:, half * 128:(half + 1) * 128].astype(BF16)
            Rb = st_ref[half]
            Ub = U[half].astype(BF16)
            nt = (((1,), (1,)), ((), ()))
            tn = (((0,), (0,)), ((), ()))
            dsc = (lax.dot_general(dob, v, nt, preferred_element_type=F32) * decay).astype(BF16)
            xdo = (dob.astype(F32) * xi).astype(BF16)
            dq_acc += jnp.dot(dsc, km, preferred_element_type=F32) + lax.dot_general(xdo, Rb, nt, preferred_element_type=F32)
            dk_acc += lax.dot_general(dsc, qm, tn, preferred_element_type=F32) \
                + lax.dot_general(v, Ub, nt, preferred_element_type=F32) * zeta
            sc = (lax.dot_general(qm, k2, nt, preferred_element_type=F32) * decay).astype(BF16)
            kz = (km.astype(F32) * zeta).astype(BF16)
            dv_ref[:, half * 128:(half + 1) * 128] = lax.dot_general(sc, dob, tn, preferred_element_type=F32) \
                + jnp.dot(kz, Ub, preferred_element_type=F32)
            U[half] = U[half] * cd + lax.dot_general(qm, xdo, tn, preferred_element_type=F32)
        dq_ref[...] = dq_acc
        dk_ref[...] = dk_acc

    rev = lambda w: pl.BlockSpec((BLK, w), lambda p, n: (nc - 1 - n, p))
    return _pc(
        body, name="ret_bwd", grid=(2, nc),
        in_specs=[rev(128), rev(128), rev(256),
                  pl.BlockSpec((None, None, 2, 128, 128), lambda p, n: (p, nc - 1 - n, 0, 0, 0)), rev(256)],
        out_specs=[rev(128), rev(128), rev(256)],
        out_shape=[jax.ShapeDtypeStruct((S, 256), F32), jax.ShapeDtypeStruct((S, 256), F32),
                   jax.ShapeDtypeStruct((S, 512), F32)],
        scratch_shapes=[pltpu.VMEM((2, 128, 128), F32)],
        compiler_params=_params(("parallel", "arbitrary")),
    )(rq, rk, rv, states, do)


def _col_of(b, m):
    return jnp.max(jnp.where(m, b, -jnp.inf), axis=1, keepdims=True)


def _attn_fwd(q, k, v, *, nq, max_dist, name, sinks=None, want_bf16=False):
    L, Ck = k.shape
    nb, ncol = L // BLK, Ck // LANES
    scale = HEAD ** -0.5
    has_sink = sinks is not None

    def body(*refs):
        q_ref, kp_ref, kc_ref, vp_ref, vc_ref = refs[:5]
        sk_ref = refs[5] if has_sink else None
        outs = refs[5 + has_sink:]
        n = pl.program_id(1)
        kcat = jnp.concatenate([kp_ref[...], kc_ref[...]], axis=0)
        vcat = jnp.concatenate([vp_ref[...], vc_ref[...]], axis=0)
        r = lax.broadcasted_iota(jnp.int32, (BLK, 2 * BLK), 0)
        c = lax.broadcasted_iota(jnp.int32, (BLK, 2 * BLK), 1)
        dist = r + BLK - c
        valid = (dist >= 0) & (dist <= max_dist) & ((c >= BLK) | (n > 0))
        for i in range(nq):
            q2 = q_ref[:, i * 128:(i + 1) * 128]
            o2 = jnp.zeros((BLK, LANES), F32)
            l2 = jnp.zeros((BLK, LANES), F32)
            for half in range(2):
                m = _head_mask((BLK, LANES), half)
                qm = jnp.where(m, q2, jnp.zeros_like(q2))
                s = lax.dot_general(qm, kcat, (((1,), (1,)), ((), ())), preferred_element_type=F32) * scale
                s = jnp.where(valid, s, -jnp.inf)
                mx = jnp.max(s, axis=1, keepdims=True)
                if has_sink:
                    snk = _col_of(sk_ref[:, i * 128:(i + 1) * 128], _head_mask((1, LANES), half))
                    mx = jnp.maximum(mx, snk)
                pr = jnp.exp(s - mx)
                den = jnp.sum(pr, axis=1, keepdims=True)
                if has_sink:
                    den = den + jnp.exp(snk - mx)
                pv = jnp.dot(pr.astype(BF16), vcat, preferred_element_type=F32) / den
                o2 = jnp.where(m, pv, o2)
                l2 = jnp.where(m, mx + jnp.log(den), l2)
            outs[0][:, i * 128:(i + 1) * 128] = o2
            outs[1][:, i * 128:(i + 1) * 128] = l2
            if want_bf16:
                outs[2][:, i * 128:(i + 1) * 128] = o2.astype(BF16)

    qspec = pl.BlockSpec((BLK, nq * 128), lambda j, n: (n, j))
    cur = pl.BlockSpec((BLK, 128), lambda j, n: (n, j))
    prev = pl.BlockSpec((BLK, 128), lambda j, n: (jnp.maximum(n - 1, 0), j))
    in_specs = [qspec, prev, cur, prev, cur]
    args = [q, k, k, v, v]
    if has_sink:
        in_specs.append(pl.BlockSpec((1, nq * 128), lambda j, n: (0, j)))
        args.append(sinks)
    out_dts = [F32, F32] + ([BF16] if want_bf16 else [])
    return _pc(
        body, name=name, grid=(ncol, nb), in_specs=in_specs,
        out_specs=[qspec] * len(out_dts),
        out_shape=[jax.ShapeDtypeStruct(q.shape, dt) for dt in out_dts],
        compiler_params=_params(("parallel", "parallel")),
    )(*args)


def _attn_bwd(q, k, v, o, lse, do, *, nq, max_dist, name, sinks=None):
    L, Ck = k.shape
    nb, ncol = L // BLK, Ck // LANES
    scale = HEAD ** -0.5
    has_sink = sinks is not None
    nt = (((1,), (1,)), ((), ()))
    tn = (((0,), (0,)), ((), ()))

    def body(*refs):
        (qc_ref, qn_ref, kp_ref, kc_ref, vp_ref, vc_ref, oc_ref, on_ref, lc_ref, ln_ref, dc_ref, dn_ref) = refs[:12]
        sk_ref = refs[12] if has_sink else None
        outs = refs[12 + has_sink:]
        dq_ref, dk_ref, dv_ref = outs[:3]
        n = pl.program_id(1)
        kc, vc = kc_ref[...], vc_ref[...]
        kcat = jnp.concatenate([kp_ref[...], kc], axis=0)
        vcat = jnp.concatenate([vp_ref[...], vc], axis=0)
        r = lax.broadcasted_iota(jnp.int32, (BLK, 2 * BLK), 0)
        c = lax.broadcasted_iota(jnp.int32, (BLK, 2 * BLK), 1)
        dist = r + BLK - c
        valid_q = (dist >= 0) & (dist <= max_dist) & ((c >= BLK) | (n > 0))
        r2 = lax.broadcasted_iota(jnp.int32, (2 * BLK, BLK), 0)
        c2 = lax.broadcasted_iota(jnp.int32, (2 * BLK, BLK), 1)
        dist2 = r2 - c2
        valid_k = (dist2 >= 0) & (dist2 <= max_dist) & ((r2 < BLK) | (n < nb - 1))
        dk_acc = jnp.zeros((BLK, LANES), F32)
        dv_acc = jnp.zeros((BLK, LANES), F32)
        for i in range(nq):
            sl = slice(i * 128, (i + 1) * 128)
            qcur, docur = qc_ref[:, sl], dc_ref[:, sl]
            qcat = jnp.concatenate([qcur, qn_ref[:, sl]], axis=0)
            docat = jnp.concatenate([docur, dn_ref[:, sl]], axis=0)
            ocat = jnp.concatenate([oc_ref[:, sl], on_ref[:, sl]], axis=0)
            lcat = jnp.concatenate([lc_ref[:, sl], ln_ref[:, sl]], axis=0)
            dq2 = jnp.zeros((BLK, LANES), F32)
            ds2 = jnp.zeros((1, LANES), F32)
            for half in range(2):
                m1 = _head_mask((BLK, LANES), half)
                m2 = _head_mask((2 * BLK, LANES), half)
                dom = jnp.where(m2, docat, 0.0)
                delta = jnp.sum(dom * ocat, axis=1, keepdims=True)
                lcol = _col_of(lcat, m2)
                domb = dom.astype(BF16)
                qmcat = jnp.where(m2, qcat, jnp.zeros_like(qcat))
                qm = qmcat[:BLK]
                s = lax.dot_general(qm, kcat, nt, preferred_element_type=F32) * scale
                pr = jnp.where(valid_q, jnp.exp(s - lcol[:BLK]), 0.0)
                dp = lax.dot_general(domb[:BLK], vcat, nt, preferred_element_type=F32)
                ds = (pr * (dp - delta[:BLK])).astype(BF16)
                dq2 = jnp.where(m1, jnp.dot(ds, kcat, preferred_element_type=F32) * scale, dq2)
                if has_sink:
                    snk = _col_of(sk_ref[:, sl], _head_mask((1, LANES), half))
                    contrib = jnp.sum(-jnp.exp(snk - lcol[:BLK]) * delta[:BLK], axis=0, keepdims=True)
                    ds2 = jnp.where(_head_mask((1, LANES), half), contrib, ds2)
                s = lax.dot_general(qmcat, kc, nt, preferred_element_type=F32) * scale
                pr = jnp.where(valid_k, jnp.exp(s - lcol), 0.0)
                dv_acc += lax.dot_general(pr.astype(BF16), domb, tn, preferred_element_type=F32)
                dp = lax.dot_general(domb, vc, nt, preferred_element_type=F32)
                ds = (pr * (dp - delta)).astype(BF16)
                dk_acc += lax.dot_general(ds, qmcat, tn, preferred_element_type=F32) * scale
            dq_ref[:, sl] = dq2
            if has_sink:
                @pl.when(n == 0)
                def _():
                    outs[3][:, sl] = jnp.zeros((1, LANES), F32)

                outs[3][:, sl] += ds2
        dk_ref[...] = dk_acc
        dv_ref[...] = dv_acc

    qcur = pl.BlockSpec((BLK, nq * 128), lambda j, n: (n, j))
    qnext = pl.BlockSpec((BLK, nq * 128), lambda j, n: (jnp.minimum(n + 1, nb - 1), j))
    cur = pl.BlockSpec((BLK, 128), lambda j, n: (n, j))
    prev = pl.BlockSpec((BLK, 128), lambda j, n: (jnp.maximum(n - 1, 0), j))
    in_specs = [qcur, qnext, prev, cur, prev, cur, qcur, qnext, qcur, qnext, qcur, qnext]
    args = [q, q, k, k, v, v, o, o, lse, lse, do, do]
    out_specs = [qcur, cur, cur]
    out_shape = [jax.ShapeDtypeStruct(q.shape, F32), jax.ShapeDtypeStruct(k.shape, F32), jax.ShapeDtypeStruct(k.shape, F32)]
    if has_sink:
        vec = pl.BlockSpec((1, nq * 128), lambda j, n: (0, j))
        in_specs.append(vec)
        args.append(sinks)
        out_specs.append(vec)
        out_shape.append(jax.ShapeDtypeStruct((1, q.shape[1]), F32))
    return _pc(
        body, name=name, grid=(ncol, nb), in_specs=in_specs, out_specs=out_specs, out_shape=out_shape,
        compiler_params=_params(("parallel", "arbitrary")),
    )(*args)


def _even_post_fwd(ro, proj, gn, ops, lses):
    S = ro.shape[0]
    tm = 256
    npat = len(ops)

    def body(ro_ref, rg_ref, gn_ref, *rest):
        o_refs, l_refs = rest[:npat], rest[npat:2 * npat]
        mix_ref, da_ref, lse_ref = rest[2 * npat:]
        for c in range(4):
            sl = slice(c * 128, (c + 1) * 128)
            x = ro_ref[:, sl]
            mu = jnp.mean(x, axis=1, keepdims=True)
            xc = x - mu
            var = jnp.mean(xc * xc, axis=1, keepdims=True)
            y = xc * lax.rsqrt(var + EPS) * gn_ref[:, sl]
            z = rg_ref[:, sl]
            mix_ref[:, sl] = (z * jax.nn.sigmoid(z) * y).astype(BF16)
        ls = [r[...] for r in l_refs]
        mx = ls[0]
        for l in ls[1:]:
            mx = jnp.maximum(mx, l)
        ws = [jnp.exp(l - mx) for l in ls]
        tot = ws[0]
        for w in ws[1:]:
            tot = tot + w
        out = ws[0] * o_refs[0][...]
        for w, r in zip(ws[1:], o_refs[1:]):
            out = out + w * r[...]
        out = out / tot
        da_ref[...] = out
        lse_ref[...] = mx + jnp.log(tot)
        mix_ref[:, 512:1024] = out.astype(BF16)

    row = lambda w: pl.BlockSpec((tm, w), lambda i: (i, 0))
    return _pc(
        body, name="even_post_fwd", grid=(S // tm,),
        in_specs=[row(512), pl.BlockSpec((tm, 512), lambda i: (i, 2)), pl.BlockSpec((1, 512), lambda i: (0, 0))]
        + [row(512)] * (2 * npat),
        out_specs=[row(1024), row(512), row(512)],
        out_shape=[jax.ShapeDtypeStruct((S, 1024), BF16), jax.ShapeDtypeStruct((S, 512), F32),
                   jax.ShapeDtypeStruct((S, 512), F32)],
        compiler_params=_params(("parallel",)),
    )(ro, proj, gn, *ops, *lses)


def _even_post_bwd(ro, proj, gn, dmixed):
    S = ro.shape[0]
    tm = 256

    def body(ro_ref, rg_ref, gn_ref, dm_ref, dro_ref, drg_ref, dgn_ref):
        @pl.when(pl.program_id(0) == 0)
        def _():
            dgn_ref[...] = jnp.zeros_like(dgn_ref)

        for c in range(4):
            sl = slice(c * 128, (c + 1) * 128)
            x = ro_ref[:, sl]
            mu = jnp.mean(x, axis=1, keepdims=True)
            xc = x - mu
            rstd = lax.rsqrt(jnp.mean(xc * xc, axis=1, keepdims=True) + EPS)
            xh = xc * rstd
            gain = gn_ref[:, sl]
            y = xh * gain
            z = rg_ref[:, sl]
            sg = jax.nn.sigmoid(z)
            dra = dm_ref[:, sl]
            drg_ref[:, sl] = dra * y * sg * (1.0 + z * (1.0 - sg))
            dy = dra * z * sg
            dgn_ref[:, sl] += jnp.sum(dy * xh, axis=0, keepdims=True)
            dxh = dy * gain
            dro_ref[:, sl] = rstd * (dxh - jnp.mean(dxh, axis=1, keepdims=True)
                                     - xh * jnp.mean(dxh * xh, axis=1, keepdims=True))

    row = lambda w: pl.BlockSpec((tm, w), lambda i: (i, 0))
    vec = pl.BlockSpec((1, 512), lambda i: (0, 0))
    return _pc(
        body, name="even_post_bwd", grid=(S // tm,),
        in_specs=[row(512), pl.BlockSpec((tm, 512), lambda i: (i, 2)), vec, row(512)],
        out_specs=[row(512), row(512), vec],
        out_shape=[jax.ShapeDtypeStruct((S, 512), F32), jax.ShapeDtypeStruct((S, 512), F32),
                   jax.ShapeDtypeStruct((1, 512), F32)],
        compiler_params=_params(("arbitrary",)),
    )(ro, proj, gn, dmixed)


def _swa_pre_fwd(proj, tab, qg, kg):
    S = proj.shape[0]
    tm = 256

    def body(p_ref, tab_ref, qg_ref, kg_ref, g_ref, q_ref, k_ref, v_ref):
        Ap, Bp, Cp = _tab(tab_ref, 1)
        G = g_ref[...]
        lo = _head_mask((tm, LANES), 0)
        for c in range(8):
            sl = slice(c * 128, (c + 1) * 128)
            q_ref[:, sl] = _rope(_hn_fwd(p_ref[:, sl], qg_ref[...], G), Ap, Bp, Cp, 8).astype(BF16)
        for c in range(2):
            kn = _rope(_hn_fwd(p_ref[:, 1024 + c * 128:1024 + (c + 1) * 128], kg_ref[...], G), Ap, Bp, Cp, 8)
            vv = p_ref[:, 1280 + c * 128:1280 + (c + 1) * 128]
            for t, ref in ((kn, k_ref), (vv, v_ref)):
                sw = _roll(t, HEAD)
                ref[:, (2 * c) * 128:(2 * c + 1) * 128] = jnp.where(lo, t, sw).astype(BF16)
                ref[:, (2 * c + 1) * 128:(2 * c + 2) * 128] = jnp.where(lo, sw, t).astype(BF16)

    row = lambda w: pl.BlockSpec((tm, w), lambda i: (i, 0))
    vec = pl.BlockSpec((1, LANES), lambda i: (0, 0))
    return _pc(
        body, name="swa_pre_fwd", grid=(S // tm,),
        in_specs=[row(1536), row(768), vec, vec, pl.BlockSpec((LANES, LANES), lambda i: (0, 0))],
        out_specs=[row(1024), row(512), row(512)],
        out_shape=[jax.ShapeDtypeStruct((S, w), BF16) for w in (1024, 512, 512)],
        compiler_params=_params(("parallel",)),
    )(proj, tab, qg, kg, _group_matrix())


def _swa_pre_bwd(proj, tab, qg, kg, dq, dk, dv):
    S = proj.shape[0]
    tm = 256

    def body(p_ref, tab_ref, qg_ref, kg_ref, g_ref, dq_ref, dk_ref, dv_ref, dp_ref, db_ref, dqg_ref, dkg_ref):
        Ap, Bp, Cp = _tab(tab_ref, 1)
        G = g_ref[...]
        lo = _head_mask((tm, LANES), 0)

        @pl.when(pl.program_id(0) == 0)
        def _():
            db_ref[...] = jnp.zeros_like(db_ref)
            dqg_ref[...] = jnp.zeros_like(dqg_ref)
            dkg_ref[...] = jnp.zeros_like(dkg_ref)

        accq = jnp.zeros((1, LANES), F32)
        acck = jnp.zeros((1, LANES), F32)
        for c in range(8):
            sl = slice(c * 128, (c + 1) * 128)
            dx, dg = _hn_bwd(p_ref[:, sl], qg_ref[...], _rope_t(dq_ref[:, sl], Ap, Bp, Cp, 8), G)
            dp_ref[:, sl] = dx.astype(BF16)
            db_ref[:, sl] += jnp.sum(dx, axis=0, keepdims=True)
            accq = accq + dg
        for c in range(2):
            folded = []
            for ref in (dk_ref, dv_ref):
                a = ref[:, (2 * c) * 128:(2 * c + 1) * 128]
                b = ref[:, (2 * c + 1) * 128:(2 * c + 2) * 128]
                folded.append(jnp.where(lo, a + _roll(a, HEAD), b + _roll(b, HEAD)))
            ks = slice(1024 + c * 128, 1024 + (c + 1) * 128)
            dx, dg = _hn_bwd(p_ref[:, ks], kg_ref[...], _rope_t(folded[0], Ap, Bp, Cp, 8), G)
            dp_ref[:, ks] = dx.astype(BF16)
            db_ref[:, ks] += jnp.sum(dx, axis=0, keepdims=True)
            acck = acck + dg
            vs = slice(1280 + c * 128, 1280 + (c + 1) * 128)
            dp_ref[:, vs] = folded[1].astype(BF16)
            db_ref[:, vs] += jnp.sum(folded[1], axis=0, keepdims=True)
        dqg_ref[...] += _fold_halves(accq)
        dkg_ref[...] += _fold_halves(acck)

    row = lambda w: pl.BlockSpec((tm, w), lambda i: (i, 0))
    vec = pl.BlockSpec((1, LANES), lambda i: (0, 0))
    return _pc(
        body, name="swa_pre_bwd", grid=(S // tm,),
        in_specs=[row(1536), row(768), vec, vec, pl.BlockSpec((LANES, LANES), lambda i: (0, 0)),
                  row(1024), row(512), row(512)],
        out_specs=[row(1536), pl.BlockSpec((1, 1536), lambda i: (0, 0)), vec, vec],
        out_shape=[jax.ShapeDtypeStruct((S, 1536), BF16), jax.ShapeDtypeStruct((1, 1536), F32),
                   jax.ShapeDtypeStruct((1, LANES), F32), jax.ShapeDtypeStruct((1, LANES), F32)],
        compiler_params=_params(("arbitrary",)),
    )(proj, tab, qg, kg, _group_matrix(), dq, dk, dv)


def _loss_head(y, target):
    S, Dm = y.shape
    tm = 512

    def body(y_ref, t_ref, l_ref, dy_ref, dyb_ref):
        @pl.when(pl.program_id(0) == 0)
        def _():
            l_ref[...] = jnp.zeros_like(l_ref)

        e = y_ref[...] - t_ref[...]
        dy = e * (1.0 / Dm)
        dy_ref[...] = dy
        dyb_ref[...] = dy.astype(BF16)
        row = jnp.sum(e * e, axis=1, keepdims=True) * (0.5 / Dm)
        l_ref[...] += jnp.sum(row, axis=0, keepdims=True)

    row = pl.BlockSpec((tm, Dm), lambda i: (i, 0))
    return _pc(
        body, name="loss_head", grid=(S // tm,), in_specs=[row, row],
        out_specs=[pl.BlockSpec((1, LANES), lambda i: (0, 0)), row, row],
        out_shape=[jax.ShapeDtypeStruct((1, LANES), F32), jax.ShapeDtypeStruct((S, Dm), F32),
                   jax.ShapeDtypeStruct((S, Dm), BF16)],
        compiler_params=_params(("arbitrary",)),
    )(y, target)


def _relu2_of(u):
    r = jnp.maximum(u.astype(F32), 0.0)
    return r * r


def _drelu2(acc, u):
    return (acc * 2.0 * jnp.maximum(u.astype(F32), 0.0),)


def _add(acc, res):
    return (acc + res,)


_T = dict(tm=1024, tn=1024, tk=1024)


def _mlp_fwd(x, g, w_up, w_dn, tag):
    h = _rms_fwd(x, g, f"rms_mlp_fwd{tag}")
    u = _matmul(h, w_up, dims="nn", **_T, outs=[BF16], b_cs=True, name=f"mlp_up{tag}")
    x_out = _matmul(u, w_dn, dims="nn", **_T, outs=[F32], epilogue=_add, extras=[(x, "mn")], a_pro=_relu2_of,
                    name=f"mlp_down{tag}")
    return x_out, (h, u)


def _mlp_bwd(x, g, w_up, w_dn, saved, dy, dyb, tag):
    h, u = saved
    du = _matmul(dyb, w_dn, dims="nt", **_T, outs=[BF16], epilogue=_drelu2, extras=[(u, "mn")], name=f"mlp_du{tag}")
    dw_dn = _matmul(u, dyb, dims="tn", **_T, outs=[F32], a_pro=_relu2_of, name=f"mlp_dwdown{tag}")
    dw_up = _matmul(h, du, dims="tn", **_T, outs=[F32], o_cs=N_CHIPS, name=f"mlp_dwup{tag}")
    dh = _matmul(du, w_up, dims="nt", **_T, outs=[F32], b_cs=True, name=f"mlp_dh{tag}")
    dx, dxb, dg = _rms_bwd(x, g, dh, dy, f"rms_mlp_bwd{tag}")
    return dx, dxb, dg, dw_up, dw_dn


def _pattern_view(t, r):
    S, C = t.shape
    return t.reshape(S // r, r * C)


def _local_step(x, pos_col, target, W, P):
    S = x.shape[0]
    tab = _tables(pos_col)
    tile2 = lambda g: jnp.tile(g.reshape(1, HEAD), (1, 2))
    dqg, dkg = tile2(P["dil_q_gain"]), tile2(P["dil_k_gain"])
    sqg, skg = tile2(P["swa_q_gain"]), tile2(P["swa_k_gain"])
    gn = P["ret_gn_gain"].reshape(1, 512)
    sink_b = jnp.repeat(P["swa_sinks"].reshape(16), HEAD).reshape(1, 1024)

    h0 = _rms_fwd(x, P["norm_mix"][0], "rms_mix_fwd0")
    proj = _matmul(h0, W["hyb_w_in"], dims="nn", tm=1024, tn=768, tk=1024, outs=[F32], b_cs=True, name="hyb_in")
    rq, rk, rv, dq, dk, dv = _even_pre_fwd(proj, tab, dqg, dkg)
    ro, states = _ret_fwd(rq, rk, rv)
    ops, lses = [], []
    for w, r in DIL_PATTERNS:
        o_p, l_p = _attn_fwd(_pattern_view(dq, r), _pattern_view(dk, r), _pattern_view(dv, r), nq=1, max_dist=w // r,
                             name=f"dil_fwd_r{r}")
        ops.append(o_p.reshape(S, 512))
        lses.append(l_p.reshape(S, 512))
    mixed, da, dlse = _even_post_fwd(ro, proj, gn, ops, lses)
    x1 = _matmul(mixed, W["hyb_w_out"], dims="nn", **_T, outs=[F32], epilogue=_add, extras=[(x, "mn")], name="hyb_out")
    x2, mlp0 = _mlp_fwd(x1, P["norm_mlp"][0], W["mlp_w_up"][0], W["mlp_w_down"][0], "0")

    h2 = _rms_fwd(x2, P["norm_mix"][1], "rms_mix_fwd1")
    proj2 = _matmul(h2, W["swa_w_qkv"], dims="nn", tm=1024, tn=384, tk=1024, outs=[F32], b_cs=True,
                    epilogue=_add, extras=[(P["swa_b_qkv_full"].reshape(1, 1536), "n")], name="swa_qkv")
    sq, sk, sv = _swa_pre_fwd(proj2, tab, sqg, skg)
    so, slse, so_b = _attn_fwd(sq, sk, sv, nq=2, max_dist=SWA_DIST, name="swa_fwd", sinks=sink_b, want_bf16=True)
    x3 = _matmul(so_b, W["swa_w_out"], dims="nn", **_T, outs=[F32], epilogue=_add, extras=[(x2, "mn")], name="swa_out")
    y, mlp1 = _mlp_fwd(x3, P["norm_mlp"][1], W["mlp_w_up"][1], W["mlp_w_down"][1], "1")
    loss, dy, dyb = _loss_head(y, target)

    gw, gp = {}, {}
    dx3, dx3b, dg_mlp1, gw["mlp_w_up1"], gw["mlp_w_down1"] = _mlp_bwd(x3, P["norm_mlp"][1], W["mlp_w_up"][1],
                                                                       W["mlp_w_down"][1], mlp1, dy, dyb, "1")
    gw["swa_w_out"] = _matmul(so_b, dx3b, dims="tn", **_T, outs=[F32], name="swa_dwout")
    dso = _matmul(dx3b, W["swa_w_out"], dims="nt", **_T, outs=[F32], name="swa_do")
    dsq, dsk, dsv, dsink = _attn_bwd(sq, sk, sv, so, slse, dso, nq=2, max_dist=SWA_DIST, name="swa_bwd", sinks=sink_b)
    dproj2, gp["swa_b_qkv"], gp["swa_q_gain"], gp["swa_k_gain"] = _swa_pre_bwd(proj2, tab, sqg, skg, dsq, dsk, dsv)
    gp["swa_sinks"] = dsink
    gw["swa_w_qkv"] = _matmul(h2, dproj2, dims="tn", tm=1024, tn=384, tk=1024, outs=[F32], o_cs=N_CHIPS, name="swa_dwqkv")
    dh2 = _matmul(dproj2, W["swa_w_qkv"], dims="nt", tm=1024, tn=1024, tk=384, outs=[F32], b_cs=True, name="swa_dh")
    dx2, dx2b, dg_mix1 = _rms_bwd(x2, P["norm_mix"][1], dh2, dx3, "rms_mix_bwd1")

    dx1, dx1b, dg_mlp0, gw["mlp_w_up0"], gw["mlp_w_down0"] = _mlp_bwd(x1, P["norm_mlp"][0], W["mlp_w_up"][0],
                                                                       W["mlp_w_down"][0], mlp0, dx2, dx2b, "0")
    gw["hyb_w_out"] = _matmul(mixed, dx1b, dims="tn", **_T, outs=[F32], name="hyb_dwout")
    dmixed = _matmul(dx1b, W["hyb_w_out"], dims="nt", **_T, outs=[F32], name="hyb_dmixed")
    dro, drg, gp["ret_gn_gain"] = _even_post_bwd(ro, proj, gn, dmixed)
    drq, drk, drv = _ret_bwd(rq, rk, rv, states, dro)
    dda = dmixed[:, 512:]
    dqs, dks, dvs = [], [], []
    for w, r in DIL_PATTERNS:
        a, b, c = _attn_bwd(_pattern_view(dq, r), _pattern_view(dk, r), _pattern_view(dv, r), _pattern_view(da, r),
                            _pattern_view(dlse, r), _pattern_view(dda, r), nq=1, max_dist=w // r, name=f"dil_bwd_r{r}")
        dqs.append(a.reshape(S, 512))
        dks.append(b.reshape(S, 512))
        dvs.append(c.reshape(S, 512))
    dproj, gp["dil_q_gain"], gp["dil_k_gain"] = _even_pre_bwd(proj, tab, dqg, dkg, drq, drk, drv, drg, dqs, dks, dvs)
    gw["hyb_w_in"] = _matmul(h0, dproj, dims="tn", tm=1024, tn=768, tk=1024, outs=[F32], o_cs=N_CHIPS, name="hyb_dwin")
    dh0 = _matmul(dproj, W["hyb_w_in"], dims="nt", tm=1024, tn=1024, tk=768, outs=[F32], b_cs=True, name="hyb_dh")
    grad_x, _, dg_mix0 = _rms_bwd(x, P["norm_mix"][0], dh0, dx1, "rms_mix_bwd0")
    gp["norm_mix"] = jnp.concatenate([dg_mix0, dg_mix1], axis=0)
    gp["norm_mlp"] = jnp.concatenate([dg_mlp0, dg_mlp1], axis=0)
    return loss, grad_x, gw, gp


HBM = pl.BlockSpec(memory_space=pltpu.HBM)


def _place():
    x, y, c = lax.axis_index("x"), lax.axis_index("y"), lax.axis_index("c")
    chips = [(1 - x, y), (x, 1 - y), (1 - x, 1 - y)]
    return x, y, c, chips


def _allgather_shards(buf):
    _, R, Wd = buf.shape
    Rh = R // 2

    def body(b_ref, out_ref, send_sems, recv_sems):
        x, y, c, chips = _place()
        sibling = (x, y, 1 - c)

        def copy(k, chip, core, to):
            block = b_ref.at[2 * chip[0] + chip[1], pl.ds(core * Rh, Rh), :]
            return pltpu.make_async_remote_copy(
                src_ref=block, dst_ref=block, send_sem=send_sems.at[k], recv_sem=recv_sems.at[k],
                device_id=to, device_id_type=MESH)

        first = [copy(k, (x, y), c, (*chip, c)) for k, chip in enumerate(chips)]
        for cp in first:
            cp.start()
        passed = [copy(3 + k, chip, c, sibling) for k, chip in enumerate(chips)]
        for k, chip in enumerate(chips):
            copy(k, chip, c, (x, y, c)).wait_recv()
            passed[k].start()
        for k, chip in enumerate(chips):
            copy(3 + k, chip, 1 - c, (x, y, c)).wait_recv()
        for cp in first + passed:
            cp.wait_send()

    return _pc(
        body, name="allgather_weights", in_specs=[HBM], out_specs=HBM,
        out_shape=jax.ShapeDtypeStruct(buf.shape, buf.dtype), input_output_aliases={0: 0},
        scratch_shapes=[pltpu.SemaphoreType.DMA((6,)), pltpu.SemaphoreType.DMA((6,))],
    )(buf)


def _swap_halves(ts):
    nt = len(ts)

    def body(*refs):
        t_refs, l_refs, send_sems, recv_sems = refs[:nt], refs[nt:2 * nt], refs[-2], refs[-1]
        x, y, c, _ = _place()
        cps = []
        for k in range(nt):
            rh = t_refs[k].shape[1] // 2
            cps.append(pltpu.make_async_remote_copy(
                src_ref=t_refs[k].at[:, pl.ds((1 - c) * rh, rh), :], dst_ref=l_refs[k],
                send_sem=send_sems.at[k], recv_sem=recv_sems.at[k], device_id=(x, y, 1 - c), device_id_type=MESH))
        for cp in cps:
            cp.start()
        for cp in cps:
            cp.wait()

    return _pc(
        body, name="grad_swap_halves", in_specs=[HBM] * nt, out_specs=[HBM] * nt,
        out_shape=[jax.ShapeDtypeStruct((t.shape[0], t.shape[1] // 2, t.shape[2]), F32) for t in ts],
        scratch_shapes=[pltpu.SemaphoreType.DMA((nt,)), pltpu.SemaphoreType.DMA((nt,))],
    )(*ts)


def _pair_sum(t, l, place, name):
    _, r, cols = t.shape
    rh = r // 2
    tr = min(rh, 256)
    nr = rh // tr

    def body(pl_ref, t_ref, l_ref, o_ref):
        o_ref[...] = (t_ref[...] + l_ref[...]).astype(BF16)

    return _pc(
        body, name=name,
        grid_spec=pltpu.PrefetchScalarGridSpec(
            num_scalar_prefetch=1, grid=(N_CHIPS, nr),
            in_specs=[pl.BlockSpec((None, tr, cols), lambda s, i, p: (s, p[1] * nr + i, 0)),
                      pl.BlockSpec((None, tr, cols), lambda s, i, p: (s, i, 0))],
            out_specs=pl.BlockSpec((None, tr, cols), lambda s, i, p: (s, i, 0))),
        out_shape=jax.ShapeDtypeStruct((N_CHIPS, rh, cols), BF16),
        compiler_params=_params(("parallel", "parallel")),
    )(place, t, l)


def _exchange_chips(ps):
    nt = len(ps)

    def body(*refs):
        p_refs, r_refs, send_sems, recv_sems = refs[:nt], refs[nt:2 * nt], refs[-2], refs[-1]
        x, y, c, chips = _place()
        cps = []
        for t in range(nt):
            for k, chip in enumerate(chips):
                cps.append(pltpu.make_async_remote_copy(
                    src_ref=p_refs[t].at[2 * chip[0] + chip[1]], dst_ref=r_refs[t].at[k],
                    send_sem=send_sems.at[3 * t + k], recv_sem=recv_sems.at[3 * t + k],
                    device_id=(*chip, c), device_id_type=MESH))
        for cp in cps:
            cp.start()
        for cp in cps:
            cp.wait()

    return _pc(
        body, name="grad_exchange_chips", in_specs=[HBM] * nt, out_specs=[HBM] * nt,
        out_shape=[jax.ShapeDtypeStruct((3,) + p.shape[1:], BF16) for p in ps],
        scratch_shapes=[pltpu.SemaphoreType.DMA((3 * nt,)), pltpu.SemaphoreType.DMA((3 * nt,))],
    )(*ps)


def _final_sum(t, l, rcv, place, name):
    _, r, cols = t.shape
    rh = r // 2
    tr = min(rh, 256)
    nr = rh // tr

    def body(pl_ref, t_ref, l_ref, r_ref, o_ref):
        acc = t_ref[...] + l_ref[...]
        for k in range(3):
            acc = acc + r_ref[k].astype(F32)
        o_ref[...] = acc

    return _pc(
        body, name=name,
        grid_spec=pltpu.PrefetchScalarGridSpec(
            num_scalar_prefetch=1, grid=(nr,),
            in_specs=[pl.BlockSpec((None, tr, cols), lambda i, p: (p[0], p[1] * nr + i, 0)),
                      pl.BlockSpec((None, tr, cols), lambda i, p: (p[0], i, 0)),
                      pl.BlockSpec((3, tr, cols), lambda i, p: (0, i, 0))],
            out_specs=pl.BlockSpec((tr, cols), lambda i, p: (p[1] * nr + i, 0))),
        out_shape=jax.ShapeDtypeStruct((r, cols), F32),
        compiler_params=_params(("parallel",)),
    )(place, t, l, rcv)


def _share_halves(hs):
    nt = len(hs)

    def body(*refs):
        h_refs, send_sems, recv_sems = refs[:nt], refs[-2], refs[-1]
        x, y, c, _ = _place()
        cps = []
        for k in range(nt):
            rh = h_refs[k].shape[0] // 2
            half = h_refs[k].at[pl.ds(c * rh, rh), :]
            cps.append(pltpu.make_async_remote_copy(
                src_ref=half, dst_ref=half, send_sem=send_sems.at[k], recv_sem=recv_sems.at[k],
                device_id=(x, y, 1 - c), device_id_type=MESH))
        for cp in cps:
            cp.start()
        for cp in cps:
            cp.wait()

    return _pc(
        body, name="grad_share_halves", in_specs=[HBM] * nt, out_specs=[HBM] * nt,
        out_shape=[jax.ShapeDtypeStruct(h.shape, F32) for h in hs],
        input_output_aliases={k: k for k in range(nt)},
        scratch_shapes=[pltpu.SemaphoreType.DMA((nt,)), pltpu.SemaphoreType.DMA((nt,))],
    )(*hs)


def _allgather_small(v):
    rows = v.shape[0]

    def body(v_ref, out_ref, send_sems, recv_sems):
        x, y, c, _ = _place()
        me = 4 * x + 2 * y + c
        out_ref[me] = v_ref[...]
        cps = []
        for k in range(1, 8):
            fx, fy, fc = (k >> 2) & 1, (k >> 1) & 1, k & 1
            to = (1 - x if fx else x, 1 - y if fy else y, 1 - c if fc else c)
            cps.append(pltpu.make_async_remote_copy(
                src_ref=v_ref, dst_ref=out_ref.at[me], send_sem=send_sems.at[k - 1], recv_sem=recv_sems.at[k - 1],
                device_id=to, device_id_type=MESH))
        for cp in cps:
            cp.start()
        for cp in cps:
            cp.wait()

    return _pc(
        body, name="allgather_small",
        in_specs=[pl.BlockSpec(memory_space=pltpu.VMEM)], out_specs=pl.BlockSpec(memory_space=pltpu.VMEM),
        out_shape=jax.ShapeDtypeStruct((8, rows, LANES), F32),
        scratch_shapes=[pltpu.SemaphoreType.DMA((7,)), pltpu.SemaphoreType.DMA((7,))],
    )(v)


def _adamw_math(w, g, m, v):
    m = ADAM_B1 * m + (1.0 - ADAM_B1) * g
    v = ADAM_B2 * v + (1.0 - ADAM_B2) * (g * g)
    m_hat = m / (1.0 - ADAM_B1 ** ADAM_STEP)
    v_hat = v / (1.0 - ADAM_B2 ** ADAM_STEP)
    return -ADAM_LR * (m_hat / (jnp.sqrt(v_hat) + ADAM_EPS) + ADAM_WD * w), m, v


def _adamw(w, g, m, v, name):
    r, cols = w.shape
    tr = min(r, 256)

    def body(w_ref, g_ref, m_ref, v_ref, d_ref, mo_ref, vo_ref):
        d, mn, vn = _adamw_math(w_ref[...], g_ref[...], m_ref[...], v_ref[...])
        d_ref[...] = d
        mo_ref[...] = mn
        vo_ref[...] = vn

    row = pl.BlockSpec((tr, cols), lambda i: (i, 0))
    return _pc(
        body, name=name, grid=(r // tr,), in_specs=[row] * 4, out_specs=[row] * 3,
        out_shape=[jax.ShapeDtypeStruct((r, cols), F32)] * 3,
        compiler_params=_params(("parallel",)),
    )(w, g, m, v)


def _adamw_small(w, gathered, m, v):
    rows = w.shape[0]

    def body(w_ref, g_ref, m_ref, v_ref, go_ref, d_ref, mo_ref, vo_ref):
        g = g_ref[0]
        for k in range(1, 8):
            g = g + g_ref[k]
        d, mn, vn = _adamw_math(w_ref[...], g, m_ref[...], v_ref[...])
        go_ref[...] = g
        d_ref[...] = d
        mo_ref[...] = mn
        vo_ref[...] = vn

    return _pc(
        body, name="adamw_small",
        out_shape=[jax.ShapeDtypeStruct((rows, LANES), F32)] * 4,
    )(w, gathered, m, v)


_BIAS_ROWS = 32


def _pack_shard(mlp_w_up, mlp_w_down, hyb_w_in, hyb_w_out, swa_w_qkv, swa_w_out, swa_b_qkv):
    parts = [t.astype(BF16).reshape(-1, 1024) for t in (mlp_w_up, mlp_w_down, hyb_w_in, hyb_w_out, swa_w_qkv, swa_w_out)]
    bias = lax.bitcast_convert_type(swa_b_qkv.reshape(384), BF16).reshape(1, 768)
    bias = jnp.pad(bias, ((0, _BIAS_ROWS - 1), (0, 256)))
    return jnp.concatenate(parts + [bias], axis=0)


def _unpack_weights(g):
    W = {
        "mlp_w_up": [g[:, l * 1024:(l + 1) * 1024, :] for l in range(2)],
        "mlp_w_down": [g[:, 2048 + l * 1024:2048 + (l + 1) * 1024, :].reshape(D_FF, D_MODEL) for l in range(2)],
        "hyb_w_in": g[:, 4096:4864, :].reshape(N_CHIPS, 1024, 768),
        "hyb_w_out": g[:, 4864:5120, :].reshape(1024, 1024),
        "swa_w_qkv": g[:, 5120:5504, :].reshape(N_CHIPS, 1024, 384),
        "swa_w_out": g[:, 5504:5760, :].reshape(1024, 1024),
    }
    bias = lax.bitcast_convert_type(g[:, 5760, :768].reshape(N_CHIPS, 384, 2), F32).reshape(1536)
    return W, bias


_SMALL = (("norm_mix", 16), ("norm_mlp", 16), ("ret_gn_gain", 4), ("dil_q_gain", 1), ("dil_k_gain", 1),
          ("swa_b_qkv", 12), ("swa_q_gain", 1), ("swa_k_gain", 1), ("swa_sinks", 1))
_SMALL_ROWS = 56


def _pack_small(d):
    rows = [d[n].reshape(r, LANES) for n, r in _SMALL]
    used = sum(r for _, r in _SMALL)
    return jnp.concatenate(rows + [jnp.zeros((_SMALL_ROWS - used, LANES), F32)], axis=0)


def _unpack_small(p):
    out, o = {}, 0
    for n, r in _SMALL:
        out[n] = p[o:o + r]
        o += r
    return out


def kernel(x, positions, norm_mix, norm_mlp, mlp_w_up, mlp_w_down, hyb_w_in, hyb_w_out, ret_gn_gain, dil_q_gain, dil_k_gain, swa_w_qkv, swa_b_qkv, swa_w_out, swa_q_gain, swa_k_gain, swa_sinks, loss_target, m_norm_mix, m_norm_mlp, m_mlp_w_up, m_mlp_w_down, m_hyb_w_in, m_hyb_w_out, m_ret_gn_gain, m_dil_q_gain, m_dil_k_gain, m_swa_w_qkv, m_swa_b_qkv, m_swa_w_out, m_swa_q_gain, m_swa_k_gain, m_swa_sinks, v_norm_mix, v_norm_mlp, v_mlp_w_up, v_mlp_w_down, v_hyb_w_in, v_hyb_w_out, v_ret_gn_gain, v_dil_q_gain, v_dil_k_gain, v_swa_w_qkv, v_swa_b_qkv, v_swa_w_out, v_swa_q_gain, v_swa_k_gain, v_swa_sinks):
    ax, ay, ac = lax.axis_index("x"), lax.axis_index("y"), lax.axis_index("c")
    chip = 2 * ax + ay
    place = jnp.stack([chip, ac]).astype(jnp.int32)
    S = x.shape[1]

    flat = _pack_shard(mlp_w_up, mlp_w_down, hyb_w_in[0], hyb_w_out[0], swa_w_qkv[0], swa_w_out[0], swa_b_qkv[0])
    buf = lax.dynamic_update_slice(jnp.zeros((N_CHIPS,) + flat.shape, BF16), flat[None], (chip, 0, 0))
    W, bias_full = _unpack_weights(_allgather_shards(buf))
    P = dict(norm_mix=norm_mix, norm_mlp=norm_mlp, ret_gn_gain=ret_gn_gain, dil_q_gain=dil_q_gain, dil_k_gain=dil_k_gain,
             swa_q_gain=swa_q_gain, swa_k_gain=swa_k_gain, swa_sinks=swa_sinks, swa_b_qkv_full=bias_full)

    loss_l, grad_x, gw, gp = _local_step(x[0], positions.reshape(S, 1), loss_target[0], W, P)
    loss = lax.psum(loss_l[0, 0], ("x", "y", "c"))

    slab = lambda t, r: t.reshape(N_CHIPS, r, t.size // (N_CHIPS * r))
    names = ["mlp_w_up0", "mlp_w_up1", "mlp_w_down0", "mlp_w_down1", "hyb_w_in", "hyb_w_out", "swa_w_qkv", "swa_w_out"]
    rows = [1024, 1024, 1024, 1024, 1024, 256, 1024, 256]
    ts = [slab(gw[n], r) for n, r in zip(names, rows)]
    ls = _swap_halves(ts)
    ps = [_pair_sum(t, l, place, f"pair_sum_{n}") for t, l, n in zip(ts, ls, names)]
    rs = _exchange_chips(ps)
    hs = [_final_sum(t, l, r, place, f"final_sum_{n}") for t, l, r, n in zip(ts, ls, rs, names)]
    gs = dict(zip(names, _share_halves(hs)))
    shards = dict(mlp_w_up0=(mlp_w_up[0], m_mlp_w_up[0], v_mlp_w_up[0]), mlp_w_up1=(mlp_w_up[1], m_mlp_w_up[1], v_mlp_w_up[1]),
                  mlp_w_down0=(mlp_w_down[0], m_mlp_w_down[0], v_mlp_w_down[0]),
                  mlp_w_down1=(mlp_w_down[1], m_mlp_w_down[1], v_mlp_w_down[1]),
                  hyb_w_in=(hyb_w_in[0], m_hyb_w_in[0], v_hyb_w_in[0]), hyb_w_out=(hyb_w_out[0], m_hyb_w_out[0], v_hyb_w_out[0]),
                  swa_w_qkv=(swa_w_qkv[0], m_swa_w_qkv[0], v_swa_w_qkv[0]), swa_w_out=(swa_w_out[0], m_swa_w_out[0], v_swa_w_out[0]))
    big = {}
    for n in names:
        w, m, v = shards[n]
        big[n] = (gs[n],) + tuple(_adamw(w, gs[n], m, v, f"adamw_{n}"))

    def big_out(n, k):
        if n in ("mlp_w_up", "mlp_w_down"):
            return jnp.stack([big[n + "0"][k], big[n + "1"][k]])
        return big[n][k][None]

    gsm = dict(gp)
    gsm["swa_sinks"] = jnp.pad(gp["swa_sinks"].reshape(16, HEAD)[:, 0], (0, LANES - 16))
    gathered = _allgather_small(_pack_small(gsm))

    def small_pack(norm_mix, norm_mlp, gn, dq, dk, b, sq, sk, sinks):
        dup = lambda t: jnp.tile(t.reshape(1, HEAD), (1, 2))
        bias = lax.dynamic_update_slice(jnp.zeros((12, LANES), F32), b.reshape(3, LANES), (3 * chip, 0))
        return _pack_small(dict(norm_mix=norm_mix, norm_mlp=norm_mlp, ret_gn_gain=gn, dil_q_gain=dup(dq), dil_k_gain=dup(dk),
                                swa_b_qkv=bias, swa_q_gain=dup(sq), swa_k_gain=dup(sk),
                                swa_sinks=jnp.pad(sinks.reshape(16), (0, LANES - 16))))

    pw = small_pack(norm_mix, norm_mlp, ret_gn_gain, dil_q_gain, dil_k_gain, swa_b_qkv, swa_q_gain, swa_k_gain, swa_sinks)
    pm = small_pack(m_norm_mix, m_norm_mlp, m_ret_gn_gain, m_dil_q_gain, m_dil_k_gain, m_swa_b_qkv, m_swa_q_gain, m_swa_k_gain, m_swa_sinks)
    pv = small_pack(v_norm_mix, v_norm_mlp, v_ret_gn_gain, v_dil_q_gain, v_dil_k_gain, v_swa_b_qkv, v_swa_q_gain, v_swa_k_gain, v_swa_sinks)
    small = [_unpack_small(t) for t in _adamw_small(pw, gathered, pm, pv)]

    def small_out(n, k):
        t = small[k][n]
        if n in ("norm_mix", "norm_mlp"):
            return t.reshape(2, D_MODEL)
        if n == "ret_gn_gain":
            return t.reshape(1, RET_HEADS, 128)
        if n == "swa_b_qkv":
            return lax.dynamic_slice(t, (3 * chip, 0), (3, LANES)).reshape(1, 384)
        if n == "swa_sinks":
            return t[0, :16].reshape(1, 16)
        return t[0, :HEAD].reshape(1, HEAD)

    order = ["norm_mix", "norm_mlp", "mlp_w_up", "mlp_w_down", "hyb_w_in", "hyb_w_out", "ret_gn_gain", "dil_q_gain",
             "dil_k_gain", "swa_w_qkv", "swa_b_qkv", "swa_w_out", "swa_q_gain", "swa_k_gain", "swa_sinks"]
    is_big = {"mlp_w_up", "mlp_w_down", "hyb_w_in", "hyb_w_out", "swa_w_qkv", "swa_w_out"}
    outs = [loss, grad_x[None]]
    for k in range(4):
        outs += [big_out(n, k) if n in is_big else small_out(n, k) for n in order]
    return tuple(outs)
```

```python
import functools
import math

import numpy as np
import jax
import jax.numpy as jnp
from jax import lax
from jax.experimental import pallas as pl
from jax.experimental.pallas import tpu as pltpu

F32, BF16 = jnp.float32, jnp.bfloat16
HIGHEST = lax.Precision.HIGHEST
MESH = pl.DeviceIdType.MESH

LANES = 128
VMEM_LIMIT = 48 << 20
D_MODEL = 1024
D_FF = 4096
HEAD = 64
EPS = 1e-6
BLK = 128
RET_HEADS = 4
RET_THETA = 10000.0
ROPE_THETA = 500000.0
ROPE_DIMS = 16
DIL_PATTERNS = ((128, 1), (512, 4), (2048, 16))
SWA_DIST = 127
N_CHIPS = 4
ADAM_LR, ADAM_B1, ADAM_B2, ADAM_EPS, ADAM_WD, ADAM_STEP = 0.001, 0.9, 0.999, 1e-08, 0.01, 10

_LOG_GAMMA = [float(np.log1p(-np.exp2(np.float32(-5.0 - h)))) for h in range(RET_HEADS)]


def _pc(body, **kw):
    return pl.pallas_call(body, **kw)


def _params(sem):
    return pltpu.CompilerParams(dimension_semantics=sem, vmem_limit_bytes=VMEM_LIMIT)


def _matmul(a, b, *, dims, tm, tn, tk, outs, name, epilogue=None, extras=(), b_cs=False, o_cs=0, a_pro=None):
    if dims == "nn":
        M, K = a.shape
        N = b.shape[0] * b.shape[2] if b_cs else b.shape[1]
        a_spec = pl.BlockSpec((tm, tk), lambda i, j, k: (i, k))
        if b_cs:
            npt = b.shape[2] // tn
            b_spec = pl.BlockSpec((None, tk, tn), lambda i, j, k: (j // npt, k, j % npt))
        else:
            b_spec = pl.BlockSpec((tk, tn), lambda i, j, k: (k, j))
        contract = (((1,), (0,)), ((), ()))
    elif dims == "nt":
        M, K = a.shape
        N = b.shape[1] if b_cs else b.shape[0]
        a_spec = pl.BlockSpec((tm, tk), lambda i, j, k: (i, k))
        if b_cs:
            kpt = b.shape[2] // tk
            b_spec = pl.BlockSpec((None, tn, tk), lambda i, j, k: (k // kpt, j, k % kpt))
        else:
            b_spec = pl.BlockSpec((tn, tk), lambda i, j, k: (j, k))
        contract = (((1,), (1,)), ((), ()))
    else:
        K, M = a.shape
        N = b.shape[1]
        a_spec = pl.BlockSpec((tk, tm), lambda i, j, k: (k, i))
        b_spec = pl.BlockSpec((tk, tn), lambda i, j, k: (k, j))
        contract = (((0,), (0,)), ((), ()))
    assert M % tm == 0 and N % tn == 0 and K % tk == 0, (name, M, N, K, tm, tn, tk)
    nk = K // tk
    ex_specs = []
    for arr, kind in extras:
        if kind == "mn":
            ex_specs.append(pl.BlockSpec((tm, tn), lambda i, j, k: (i, j)))
        else:
            ex_specs.append(pl.BlockSpec((1, tn), lambda i, j, k: (0, j)))
    if o_cs:
        n_sh = N // o_cs
        opt = n_sh // tn
        o_shape = (o_cs, M, n_sh)
        o_spec = pl.BlockSpec((None, tm, tn), lambda i, j, k: (j // opt, i, j % opt))
    else:
        o_shape = (M, N)
        o_spec = pl.BlockSpec((tm, tn), lambda i, j, k: (i, j))
    n_ex, n_out = len(extras), len(outs)
    if epilogue is None:
        epilogue = lambda acc: (acc,)

    def body(a_ref, b_ref, *rest):
        ex, o_refs, acc = rest[:n_ex], rest[n_ex:n_ex + n_out], rest[-1]
        k = pl.program_id(2)

        @pl.when(k == 0)
        def _():
            acc[...] = jnp.zeros_like(acc)

        av = a_ref[...] if a_pro is None else a_pro(a_ref[...])
        acc[...] += lax.dot_general(av.astype(BF16), b_ref[...].astype(BF16), contract, preferred_element_type=F32)

        @pl.when(k == nk - 1)
        def _():
            vals = epilogue(acc[...], *[e[...] for e in ex])
            for r, v in zip(o_refs, vals):
                r[...] = v.astype(r.dtype)

    res = _pc(
        body, name=name, grid=(M // tm, N // tn, nk),
        in_specs=[a_spec, b_spec] + ex_specs,
        out_specs=[o_spec] * n_out,
        out_shape=[jax.ShapeDtypeStruct(o_shape, dt) for dt in outs],
        scratch_shapes=[pltpu.VMEM((tm, tn), F32)],
        compiler_params=_params(("parallel", "parallel", "arbitrary")),
    )(a, b, *[e for e, _ in extras])
    return res[0] if n_out == 1 else res


def _roll(x, s):
    return pltpu.roll(x, s % LANES, 1)


def _rope(x, A, B, C, half):
    return x * A + _roll(x, LANES - half) * B + _roll(x, half) * C


def _rope_t(g, A, B, C, half):
    return g * A + _roll(g * B, half) + _roll(g * C, LANES - half)


def _gmean(x, G):
    return jnp.dot(x, G, precision=HIGHEST, preferred_element_type=F32)


def _head_mask(shape, half):
    lane = lax.broadcasted_iota(jnp.int32, shape, len(shape) - 1)
    return (lane >= HEAD) if half else (lane < HEAD)


def _group_matrix():
    i = np.arange(LANES)
    return jnp.asarray((i[:, None] // HEAD == i[None, :] // HEAD).astype(np.float32) / HEAD)


def _rope_inv():
    l = np.arange(LANES) % HEAD
    inv_r = np.power(np.float32(RET_THETA), -(l % 32).astype(np.float32) * np.float32(2.0 / HEAD))
    hp = ROPE_DIMS // 2
    inv_p = np.power(np.float32(ROPE_THETA), -(l % hp).astype(np.float32) * np.float32(2.0 / ROPE_DIMS))
    inv_p = np.where(l < ROPE_DIMS, inv_p, 0.0)
    return jnp.asarray(np.stack([inv_r, inv_p]).astype(np.float32))


def _tables(pos_col):
    S = pos_col.shape[0]
    tm = 512
    hp = ROPE_DIMS // 2

    def body(p_ref, inv_ref, o_ref):
        p = p_ref[...].astype(F32)
        lane = lax.broadcasted_iota(jnp.int32, (tm, LANES), 1) % HEAD
        ang = p * inv_ref[0:1, :]
        c, s = jnp.cos(ang), jnp.sin(ang)
        o_ref[:, 0:128] = c
        o_ref[:, 128:256] = jnp.where(lane < 32, -s, 0.0)
        o_ref[:, 256:384] = jnp.where(lane >= 32, s, 0.0)
        ang = p * inv_ref[1:2, :]
        c, s = jnp.cos(ang), jnp.sin(ang)
        o_ref[:, 384:512] = c
        o_ref[:, 512:640] = jnp.where(lane < hp, -s, 0.0)
        o_ref[:, 640:768] = jnp.where((lane >= hp) & (lane < ROPE_DIMS), s, 0.0)

    return _pc(
        body, name="rope_tables", grid=(S // tm,),
        in_specs=[pl.BlockSpec((tm, 1), lambda i: (i, 0)), pl.BlockSpec((2, LANES), lambda i: (0, 0))],
        out_specs=pl.BlockSpec((tm, 768), lambda i: (i, 0)),
        out_shape=jax.ShapeDtypeStruct((S, 768), F32),
        compiler_params=_params(("parallel",)),
    )(pos_col, _rope_inv())


def _tab(tab_ref, which):
    o = 384 * which
    return tab_ref[:, o:o + 128], tab_ref[:, o + 128:o + 256], tab_ref[:, o + 256:o + 384]


def _rms_fwd(x, g, name):
    S, Dm = x.shape
    tm = 512

    def body(x_ref, g_ref, h_ref):
        xv = x_ref[...]
        r = lax.rsqrt(jnp.mean(xv * xv, axis=-1, keepdims=True) + EPS)
        h_ref[...] = (xv * r * g_ref[...]).astype(BF16)

    return _pc(
        body, name=name, grid=(S // tm,),
        in_specs=[pl.BlockSpec((tm, Dm), lambda i: (i, 0)), pl.BlockSpec((1, Dm), lambda i: (0, 0))],
        out_specs=pl.BlockSpec((tm, Dm), lambda i: (i, 0)),
        out_shape=jax.ShapeDtypeStruct((S, Dm), BF16),
        compiler_params=_params(("parallel",)),
    )(x, g.reshape(1, Dm))


def _rms_bwd(x, g, dh, dres, name):
    S, Dm = x.shape
    tm = 512

    def body(x_ref, g_ref, dh_ref, dres_ref, dx_ref, dxb_ref, dg_ref):
        xv, dhv = x_ref[...], dh_ref[...]
        r = lax.rsqrt(jnp.mean(xv * xv, axis=-1, keepdims=True) + EPS)
        t = dhv * g_ref[...]
        dx = dres_ref[...] + r * t - xv * (r * r * r) * jnp.mean(xv * t, axis=-1, keepdims=True)
        dx_ref[...] = dx
        dxb_ref[...] = dx.astype(BF16)

        @pl.when(pl.program_id(0) == 0)
        def _():
            dg_ref[...] = jnp.zeros_like(dg_ref)

        dg_ref[...] += jnp.sum(dhv * xv * r, axis=0, keepdims=True)

    row = pl.BlockSpec((tm, Dm), lambda i: (i, 0))
    vec = pl.BlockSpec((1, Dm), lambda i: (0, 0))
    return _pc(
        body, name=name, grid=(S // tm,),
        in_specs=[row, vec, row, row], out_specs=[row, row, vec],
        out_shape=[jax.ShapeDtypeStruct((S, Dm), F32), jax.ShapeDtypeStruct((S, Dm), BF16),
                   jax.ShapeDtypeStruct((1, Dm), F32)],
        compiler_params=_params(("arbitrary",)),
    )(x, g.reshape(1, Dm), dh, dres)


def _hn_fwd(x, gain, G):
    r = lax.rsqrt(_gmean(x * x, G) + EPS)
    return x * r * gain


def _hn_bwd(x, gain, dy, G):
    r = lax.rsqrt(_gmean(x * x, G) + EPS)
    t = dy * gain
    dx = r * t - x * (r * r * r) * _gmean(x * t, G)
    return dx, jnp.sum(dy * x * r, axis=0, keepdims=True)


def _fold_halves(v):
    return v + _roll(v, HEAD)


def _even_pre_fwd(proj, tab, qg, kg):
    S = proj.shape[0]
    tm = 256

    def body(p_ref, tab_ref, qg_ref, kg_ref, g_ref, rq_ref, rk_ref, rv_ref, dq_ref, dk_ref, dv_ref):
        Ar, Br, Cr = _tab(tab_ref, 0)
        Ap, Bp, Cp = _tab(tab_ref, 1)
        G = g_ref[...]
        for c in range(2):
            sl = slice(c * 128, (c + 1) * 128)
            rq_ref[:, sl] = _rope(p_ref[:, c * 128:(c + 1) * 128], Ar, Br, Cr, 32).astype(BF16)
            rk_ref[:, sl] = (_rope(p_ref[:, 256 + c * 128:256 + (c + 1) * 128], Ar, Br, Cr, 32) * 0.125).astype(BF16)
        rv_ref[...] = p_ref[:, 512:1024].astype(BF16)
        for c in range(4):
            sl = slice(c * 128, (c + 1) * 128)
            q = _hn_fwd(p_ref[:, 1536 + c * 128:1536 + (c + 1) * 128], qg_ref[...], G)
            dq_ref[:, sl] = _rope(q, Ap, Bp, Cp, 8).astype(BF16)
            k = _hn_fwd(p_ref[:, 2048 + c * 128:2048 + (c + 1) * 128], kg_ref[...], G)
            dk_ref[:, sl] = _rope(k, Ap, Bp, Cp, 8).astype(BF16)
        dv_ref[...] = p_ref[:, 2560:3072].astype(BF16)

    row = lambda w: pl.BlockSpec((tm, w), lambda i: (i, 0))
    vec = pl.BlockSpec((1, LANES), lambda i: (0, 0))
    return _pc(
        body, name="even_pre_fwd", grid=(S // tm,),
        in_specs=[row(3072), row(768), vec, vec, pl.BlockSpec((LANES, LANES), lambda i: (0, 0))],
        out_specs=[row(256), row(256), row(512), row(512), row(512), row(512)],
        out_shape=[jax.ShapeDtypeStruct((S, w), BF16) for w in (256, 256, 512, 512, 512, 512)],
        compiler_params=_params(("parallel",)),
    )(proj, tab, qg, kg, _group_matrix())


def _even_pre_bwd(proj, tab, qg, kg, drq, drk, drv, drg, dqs, dks, dvs):
    S = proj.shape[0]
    tm = 256
    npat = len(dqs)

    def body(p_ref, tab_ref, qg_ref, kg_ref, g_ref, drq_ref, drk_ref, drv_ref, drg_ref, *rest):
        dq_refs, dk_refs, dv_refs = rest[:npat], rest[npat:2 * npat], rest[2 * npat:3 * npat]
        dp_ref, dqg_ref, dkg_ref = rest[3 * npat:]
        Ar, Br, Cr = _tab(tab_ref, 0)
        Ap, Bp, Cp = _tab(tab_ref, 1)
        G = g_ref[...]
        for c in range(2):
            sl = slice(c * 128, (c + 1) * 128)
            dp_ref[:, c * 128:(c + 1) * 128] = _rope_t(drq_ref[:, sl], Ar, Br, Cr, 32).astype(BF16)
            dp_ref[:, 256 + c * 128:256 + (c + 1) * 128] = _rope_t(drk_ref[:, sl] * 0.125, Ar, Br, Cr, 32).astype(BF16)
        dp_ref[:, 512:1024] = drv_ref[...].astype(BF16)
        dp_ref[:, 1024:1536] = drg_ref[...].astype(BF16)
        accq = jnp.zeros((1, LANES), F32)
        acck = jnp.zeros((1, LANES), F32)
        for c in range(4):
            sl = slice(c * 128, (c + 1) * 128)
            g = dq_refs[0][:, sl]
            for r in dq_refs[1:]:
                g = g + r[:, sl]
            dx, dg = _hn_bwd(p_ref[:, 1536 + c * 128:1536 + (c + 1) * 128], qg_ref[...], _rope_t(g, Ap, Bp, Cp, 8), G)
            dp_ref[:, 1536 + c * 128:1536 + (c + 1) * 128] = dx.astype(BF16)
            accq = accq + dg
            g = dk_refs[0][:, sl]
            for r in dk_refs[1:]:
                g = g + r[:, sl]
            dx, dg = _hn_bwd(p_ref[:, 2048 + c * 128:2048 + (c + 1) * 128], kg_ref[...], _rope_t(g, Ap, Bp, Cp, 8), G)
            dp_ref[:, 2048 + c * 128:2048 + (c + 1) * 128] = dx.astype(BF16)
            acck = acck + dg
        g = dv_refs[0][...]
        for r in dv_refs[1:]:
            g = g + r[...]
        dp_ref[:, 2560:3072] = g.astype(BF16)

        @pl.when(pl.program_id(0) == 0)
        def _():
            dqg_ref[...] = jnp.zeros_like(dqg_ref)
            dkg_ref[...] = jnp.zeros_like(dkg_ref)

        dqg_ref[...] += _fold_halves(accq)
        dkg_ref[...] += _fold_halves(acck)

    row = lambda w: pl.BlockSpec((tm, w), lambda i: (i, 0))
    vec = pl.BlockSpec((1, LANES), lambda i: (0, 0))
    return _pc(
        body, name="even_pre_bwd", grid=(S // tm,),
        in_specs=[row(3072), row(768), vec, vec, pl.BlockSpec((LANES, LANES), lambda i: (0, 0)),
                  row(256), row(256), row(512), row(512)] + [row(512)] * (3 * npat),
        out_specs=[row(3072), vec, vec],
        out_shape=[jax.ShapeDtypeStruct((S, 3072), BF16), jax.ShapeDtypeStruct((1, LANES), F32),
                   jax.ShapeDtypeStruct((1, LANES), F32)],
        compiler_params=_params(("arbitrary",)),
    )(proj, tab, qg, kg, _group_matrix(), drq, drk, drv, drg, *dqs, *dks, *dvs)


def _ret_consts(pair, half):
    lg = jnp.where(pair == 0, _LOG_GAMMA[half], _LOG_GAMMA[2 + half]).astype(F32)
    i = lax.broadcasted_iota(jnp.int32, (BLK, BLK), 0)
    j = lax.broadcasted_iota(jnp.int32, (BLK, BLK), 1)
    diff = (i - j).astype(F32)
    decay = jnp.where(diff >= 0, jnp.exp(lg * jnp.maximum(diff, 0.0)), 0.0)
    t = lax.broadcasted_iota(jnp.int32, (BLK, 1), 0).astype(F32)
    xi = jnp.exp(lg * (t + 1.0))
    zeta = jnp.exp(lg * (BLK - 1.0 - t))
    cd = jnp.exp(jnp.full((1, 1), BLK, F32) * lg)
    return decay, xi, zeta, cd


def _ret_fwd(rq, rk, rv):
    S = rq.shape[0]
    nc = S // BLK

    def body(q_ref, k_ref, v_ref, o_ref, st_ref, R):
        p, n = pl.program_id(0), pl.program_id(1)

        @pl.when(n == 0)
        def _():
            R[...] = jnp.zeros_like(R)

        q2, k2 = q_ref[...], k_ref[...]
        for half in range(2):
            decay, xi, zeta, cd = _ret_consts(p, half)
            m = _head_mask((BLK, LANES), half)
            qm = jnp.where(m, q2, jnp.zeros_like(q2))
            km = jnp.where(m, k2, jnp.zeros_like(k2))
            v = v_ref[:, half * 128:(half + 1) * 128]
            Rb = R[half].astype(BF16)
            st_ref[half] = Rb
            sc = lax.dot_general(qm, k2, (((1,), (1,)), ((), ())), preferred_element_type=F32) * decay
            o = jnp.dot(sc.astype(BF16), v, preferred_element_type=F32)
            o = o + jnp.dot(qm, Rb, preferred_element_type=F32) * xi
            o_ref[:, half * 128:(half + 1) * 128] = o
            kz = (km.astype(F32) * zeta).astype(BF16)
            R[half] = R[half] * cd + lax.dot_general(kz, v, (((0,), (0,)), ((), ())), preferred_element_type=F32)

    return _pc(
        body, name="ret_fwd", grid=(2, nc),
        in_specs=[pl.BlockSpec((BLK, 128), lambda p, n: (n, p)), pl.BlockSpec((BLK, 128), lambda p, n: (n, p)),
                  pl.BlockSpec((BLK, 256), lambda p, n: (n, p))],
        out_specs=[pl.BlockSpec((BLK, 256), lambda p, n: (n, p)),
                   pl.BlockSpec((None, None, 2, 128, 128), lambda p, n: (p, n, 0, 0, 0))],
        out_shape=[jax.ShapeDtypeStruct((S, 512), F32), jax.ShapeDtypeStruct((2, nc, 2, 128, 128), BF16)],
        scratch_shapes=[pltpu.VMEM((2, 128, 128), F32)],
        compiler_params=_params(("parallel", "arbitrary")),
    )(rq, rk, rv)


def _ret_bwd(rq, rk, rv, states, do):
    S = rq.shape[0]
    nc = S // BLK

    def body(q_ref, k_ref, v_ref, st_ref, do_ref, dq_ref, dk_ref, dv_ref, U):
        p, n = pl.program_id(0), pl.program_id(1)

        @pl.when(n == 0)
        def _():
            U[...] = jnp.zeros_like(U)

        q2, k2 = q_ref[...], k_ref[...]
        dq_acc = jnp.zeros((BLK, LANES), F32)
        dk_acc = jnp.zeros((BLK, LANES), F32)
        for half in range(2):
            decay, xi, zeta, cd = _ret_consts(p, half)
            m = _head_mask((BLK, LANES), half)
            qm = jnp.where(m, q2, jnp.zeros_like(q2))
            km = jnp.where(m, k2, jnp.zeros_like(k2))
            v = v_ref[:, half * 128:(half + 1) * 128]
            dob = do_ref[:, half * 128:(half + 1) * 128].astype(BF16)
            Rb = st_ref[half]
            Ub = U[half].astype(BF16)
            nt = (((1,), (1,)), ((), ()))
            tn = (((0,), (0,)), ((), ()))
            dsc = (lax.dot_general(dob, v, nt, preferred_element_type=F32) * decay).astype(BF16)
            xdo = (dob.astype(F32) * xi).astype(BF16)
            dq_acc += jnp.dot(dsc, km, preferred_element_type=F32) + lax.dot_general(xdo, Rb, nt, preferred_element_type=F32)
            dk_acc += lax.dot_general(dsc, qm, tn, preferred_element_type=F32) \
                + lax.dot_general(v, Ub, nt, preferred_element_type=F32) * zeta
            sc = (lax.dot_general(qm, k2, nt, preferred_element_type=F32) * decay).astype(BF16)
            kz = (km.astype(F32) * zeta).astype(BF16)
            dv_ref[:, half * 128:(half + 1) * 128] = lax.dot_general(sc, dob, tn, preferred_element_type=F32) \
                + jnp.dot(kz, Ub, preferred_element_type=F32)
            U[half] = U[half] * cd + lax.dot_general(qm, xdo, tn, preferred_element_type=F32)
        dq_ref[...] = dq_acc
        dk_ref[...] = dk_acc

    rev = lambda w: pl.BlockSpec((BLK, w), lambda p, n: (nc - 1 - n, p))
    return _pc(
        body, name="ret_bwd", grid=(2, nc),
        in_specs=[rev(128), rev(128), rev(256),
                  pl.BlockSpec((None, None, 2, 128, 128), lambda p, n: (p, nc - 1 - n, 0, 0, 0)), rev(256)],
        out_specs=[rev(128), rev(128), rev(256)],
        out_shape=[jax.ShapeDtypeStruct((S, 256), F32), jax.ShapeDtypeStruct((S, 256), F32),
                   jax.ShapeDtypeStruct((S, 512), F32)],
        scratch_shapes=[pltpu.VMEM((2, 128, 128), F32)],
        compiler_params=_params(("parallel", "arbitrary")),
    )(rq, rk, rv, states, do)


def _col_of(b, m):
    return jnp.max(jnp.where(m, b, -jnp.inf), axis=1, keepdims=True)


def _attn_fwd(q, k, v, *, nq, max_dist, name, sinks=None, want_bf16=False):
    L, Ck = k.shape
    nb, ncol = L // BLK, Ck // LANES
    scale = HEAD ** -0.5
    has_sink = sinks is not None

    def body(*refs):
        q_ref, kp_ref, kc_ref, vp_ref, vc_ref = refs[:5]
        sk_ref = refs[5] if has_sink else None
        outs = refs[5 + has_sink:]
        n = pl.program_id(1)
        kcat = jnp.concatenate([kp_ref[...], kc_ref[...]], axis=0)
        vcat = jnp.concatenate([vp_ref[...], vc_ref[...]], axis=0)
        r = lax.broadcasted_iota(jnp.int32, (BLK, 2 * BLK), 0)
        c = lax.broadcasted_iota(jnp.int32, (BLK, 2 * BLK), 1)
        dist = r + BLK - c
        valid = (dist >= 0) & (dist <= max_dist) & ((c >= BLK) | (n > 0))
        for i in range(nq):
            q2 = q_ref[:, i * 128:(i + 1) * 128]
            o2 = jnp.zeros((BLK, LANES), F32)
            l2 = jnp.zeros((BLK, LANES), F32)
            for half in range(2):
                m = _head_mask((BLK, LANES), half)
                qm = jnp.where(m, q2, jnp.zeros_like(q2))
                s = lax.dot_general(qm, kcat, (((1,), (1,)), ((), ())), preferred_element_type=F32) * scale
                s = jnp.where(valid, s, -jnp.inf)
                mx = jnp.max(s, axis=1, keepdims=True)
                if has_sink:
                    snk = _col_of(sk_ref[:, i * 128:(i + 1) * 128], _head_mask((1, LANES), half))
                    mx = jnp.maximum(mx, snk)
                pr = jnp.exp(s - mx)
                den = jnp.sum(pr, axis=1, keepdims=True)
                if has_sink:
                    den = den + jnp.exp(snk - mx)
                pv = jnp.dot(pr.astype(BF16), vcat, preferred_element_type=F32) / den
                o2 = jnp.where(m, pv, o2)
                l2 = jnp.where(m, mx + jnp.log(den), l2)
            outs[0][:, i * 128:(i + 1) * 128] = o2
            outs[1][:, i * 128:(i + 1) * 128] = l2
            if want_bf16:
                outs[2][:, i * 128:(i + 1) * 128] = o2.astype(BF16)

    qspec = pl.BlockSpec((BLK, nq * 128), lambda j, n: (n, j))
    cur = pl.BlockSpec((BLK, 128), lambda j, n: (n, j))
    prev = pl.BlockSpec((BLK, 128), lambda j, n: (jnp.maximum(n - 1, 0), j))
    in_specs = [qspec, prev, cur, prev, cur]
    args = [q, k, k, v, v]
    if has_sink:
        in_specs.append(pl.BlockSpec((1, nq * 128), lambda j, n: (0, j)))
        args.append(sinks)
    out_dts = [F32, F32] + ([BF16] if want_bf16 else [])
    return _pc(
        body, name=name, grid=(ncol, nb), in_specs=in_specs,
        out_specs=[qspec] * len(out_dts),
        out_shape=[jax.ShapeDtypeStruct(q.shape, dt) for dt in out_dts],
        compiler_params=_params(("parallel", "parallel")),
    )(*args)


def _attn_bwd(q, k, v, o, lse, do, *, nq, max_dist, name, sinks=None):
    L, Ck = k.shape
    nb, ncol = L // BLK, Ck // LANES
    scale = HEAD ** -0.5
    has_sink = sinks is not None
    nt = (((1,), (1,)), ((), ()))
    tn = (((0,), (0,)), ((), ()))

    def body(*refs):
        (qc_ref, qn_ref, kp_ref, kc_ref, vp_ref, vc_ref, oc_ref, on_ref, lc_ref, ln_ref, dc_ref, dn_ref) = refs[:12]
        sk_ref = refs[12] if has_sink else None
        outs = refs[12 + has_sink:]
        dq_ref, dk_ref, dv_ref = outs[:3]
        n = pl.program_id(1)
        kc, vc = kc_ref[...], vc_ref[...]
        kcat = jnp.concatenate([kp_ref[...], kc], axis=0)
        vcat = jnp.concatenate([vp_ref[...], vc], axis=0)
        r = lax.broadcasted_iota(jnp.int32, (BLK, 2 * BLK), 0)
        c = lax.broadcasted_iota(jnp.int32, (BLK, 2 * BLK), 1)
        dist = r + BLK - c
        valid_q = (dist >= 0) & (dist <= max_dist) & ((c >= BLK) | (n > 0))
        r2 = lax.broadcasted_iota(jnp.int32, (2 * BLK, BLK), 0)
        c2 = lax.broadcasted_iota(jnp.int32, (2 * BLK, BLK), 1)
        dist2 = r2 - c2
        valid_k = (dist2 >= 0) & (dist2 <= max_dist) & ((r2 < BLK) | (n < nb - 1))
        dk_acc = jnp.zeros((BLK, LANES), F32)
        dv_acc = jnp.zeros((BLK, LANES), F32)
        for i in range(nq):
            sl = slice(i * 128, (i + 1) * 128)
            qcur, docur = qc_ref[:, sl], dc_ref[:, sl]
            qcat = jnp.concatenate([qcur, qn_ref[:, sl]], axis=0)
            docat = jnp.concatenate([docur, dn_ref[:, sl]], axis=0)
            ocat = jnp.concatenate([oc_ref[:, sl], on_ref[:, sl]], axis=0)
            lcat = jnp.concatenate([lc_ref[:, sl], ln_ref[:, sl]], axis=0)
            dq2 = jnp.zeros((BLK, LANES), F32)
            ds2 = jnp.zeros((1, LANES), F32)
            for half in range(2):
                m1 = _head_mask((BLK, LANES), half)
                m2 = _head_mask((2 * BLK, LANES), half)
                dom = jnp.where(m2, docat, 0.0)
                delta = jnp.sum(dom * ocat, axis=1, keepdims=True)
                lcol = _col_of(lcat, m2)
                domb = dom.astype(BF16)
                qmcat = jnp.where(m2, qcat, jnp.zeros_like(qcat))
                qm = qmcat[:BLK]
                s = lax.dot_general(qm, kcat, nt, preferred_element_type=F32) * scale
                pr = jnp.where(valid_q, jnp.exp(s - lcol[:BLK]), 0.0)
                dp = lax.dot_general(domb[:BLK], vcat, nt, preferred_element_type=F32)
                ds = (pr * (dp - delta[:BLK])).astype(BF16)
                dq2 = jnp.where(m1, jnp.dot(ds, kcat, preferred_element_type=F32) * scale, dq2)
                if has_sink:
                    snk = _col_of(sk_ref[:, sl], _head_mask((1, LANES), half))
                    contrib = jnp.sum(-jnp.exp(snk - lcol[:BLK]) * delta[:BLK], axis=0, keepdims=True)
                    ds2 = jnp.where(_head_mask((1, LANES), half), contrib, ds2)
                s = lax.dot_general(qmcat, kc, nt, preferred_element_type=F32) * scale
                pr = jnp.where(valid_k, jnp.exp(s - lcol), 0.0)
                dv_acc += lax.dot_general(pr.astype(BF16), domb, tn, preferred_element_type=F32)
                dp = lax.dot_general(domb, vc, nt, preferred_element_type=F32)
                ds = (pr * (dp - delta)).astype(BF16)
                dk_acc += lax.dot_general(ds, qmcat, tn, preferred_element_type=F32) * scale
            dq_ref[:, sl] = dq2
            if has_sink:
                @pl.when(n == 0)
                def _():
                    outs[3][:, sl] = jnp.zeros((1, LANES), F32)

                outs[3][:, sl] += ds2
        dk_ref[...] = dk_acc
        dv_ref[...] = dv_acc

    qcur = pl.BlockSpec((BLK, nq * 128), lambda j, n: (n, j))
    qnext = pl.BlockSpec((BLK, nq * 128), lambda j, n: (jnp.minimum(n + 1, nb - 1), j))
    cur = pl.BlockSpec((BLK, 128), lambda j, n: (n, j))
    prev = pl.BlockSpec((BLK, 128), lambda j, n: (jnp.maximum(n - 1, 0), j))
    in_specs = [qcur, qnext, prev, cur, prev, cur, qcur, qnext, qcur, qnext, qcur, qnext]
    args = [q, q, k, k, v, v, o, o, lse, lse, do, do]
    out_specs = [qcur, cur, cur]
    out_shape = [jax.ShapeDtypeStruct(q.shape, F32), jax.ShapeDtypeStruct(k.shape, F32), jax.ShapeDtypeStruct(k.shape, F32)]
    if has_sink:
        vec = pl.BlockSpec((1, nq * 128), lambda j, n: (0, j))
        in_specs.append(vec)
        args.append(sinks)
        out_specs.append(vec)
        out_shape.append(jax.ShapeDtypeStruct((1, q.shape[1]), F32))
    return _pc(
        body, name=name, grid=(ncol, nb), in_specs=in_specs, out_specs=out_specs, out_shape=out_shape,
        compiler_params=_params(("parallel", "arbitrary")),
    )(*args)


ATT_TILE = 2048


def _rows(ref, start, n, r):
    if r == 1:
        return ref[pl.ds(start, n), :]
    return ref[pl.ds(start, n, stride=r), :]


def _set_rows(ref, start, n, r, val):
    if r == 1:
        ref[pl.ds(start, n), :] = val
    else:
        ref[pl.ds(start, n, stride=r), :] = val


def _band_geometry(S, patterns):
    rmax = max(r for _, r in patterns)
    H = BLK * rmax
    T = min(S, ATT_TILE)
    assert T % H == 0 and S % T == 0
    return H, T, S // T, T // BLK


def _band_fwd(q, k, v, *, patterns, nq, name, sinks=None, want_bf16=False):
    S, Ck = k.shape
    H, T, nt, nbt = _band_geometry(S, patterns)
    ncol = Ck // LANES
    scale = HEAD ** -0.5
    has_sink = sinks is not None
    nt_dims = (((1,), (1,)), ((), ()))

    def body(*refs):
        q_ref, kp_ref, kc_ref, vp_ref, vc_ref = refs[:5]
        sk_ref = refs[5] if has_sink else None
        n_out = 3 if want_bf16 else 2
        outs = refs[5 + has_sink:5 + has_sink + n_out]
        qf, kf, vf, M, L, A = refs[5 + has_sink + n_out:]
        t = pl.program_id(1)
        kf[0:H, :] = kp_ref[...].astype(F32)
        kf[H:H + T, :] = kc_ref[...].astype(F32)
        vf[0:H, :] = vp_ref[...].astype(F32)
        vf[H:H + T, :] = vc_ref[...].astype(F32)
        r_i = lax.broadcasted_iota(jnp.int32, (BLK, 2 * BLK), 0)
        c_i = lax.broadcasted_iota(jnp.int32, (BLK, 2 * BLK), 1)
        dist_i = r_i + BLK - c_i
        masks = [_head_mask((BLK, LANES), h) for h in range(2)]

        for i in range(nq):
            qf[...] = q_ref[:, i * 128:(i + 1) * 128].astype(F32)
            for p, (dist, r) in enumerate(patterns):
                in_band = (dist_i >= 0) & (dist_i <= dist)

                def unit(j, b, p=p, r=r, in_band=in_band):
                    q0 = j + b * (BLK * r)
                    q2 = _rows(qf, q0, BLK, r).astype(BF16)
                    kcat = _rows(kf, H + q0 - BLK * r, 2 * BLK, r).astype(BF16)
                    vcat = _rows(vf, H + q0 - BLK * r, 2 * BLK, r).astype(BF16)
                    valid = in_band & ((c_i >= BLK) | ((b > 0) | (t > 0)))
                    m2 = jnp.zeros((BLK, LANES), F32)
                    l2 = jnp.zeros((BLK, LANES), F32)
                    a2 = jnp.zeros((BLK, LANES), F32)
                    for half in range(2):
                        m = masks[half]
                        qm = jnp.where(m, q2, jnp.zeros_like(q2))
                        s = lax.dot_general(qm, kcat, nt_dims, preferred_element_type=F32) * scale
                        s = jnp.where(valid, s, -jnp.inf)
                        mx = jnp.max(s, axis=1, keepdims=True)
                        pr = jnp.exp(s - mx)
                        den = jnp.sum(pr, axis=1, keepdims=True)
                        pv = jnp.dot(pr.astype(BF16), vcat, preferred_element_type=F32)
                        m2 = jnp.where(m, mx, m2)
                        l2 = jnp.where(m, den, l2)
                        a2 = jnp.where(m, pv, a2)
                    if p > 0:
                        mo = _rows(M, q0, BLK, r)
                        mn = jnp.maximum(mo, m2)
                        wa, wb = jnp.exp(mo - mn), jnp.exp(m2 - mn)
                        l2 = wa * _rows(L, q0, BLK, r) + wb * l2
                        a2 = wa * _rows(A, q0, BLK, r) + wb * a2
                        m2 = mn
                    _set_rows(M, q0, BLK, r, m2)
                    _set_rows(L, q0, BLK, r, l2)
                    _set_rows(A, q0, BLK, r, a2)

                if r == 1:
                    def step(b, carry, unit=unit):
                        unit(0, b)
                        return carry
                    lax.fori_loop(0, nbt, step, 0)
                else:
                    for u in range(nbt):
                        unit(u % r, u // r)
            sl = slice(i * 128, (i + 1) * 128)
            mm, ll, aa = M[...], L[...], A[...]
            if has_sink:
                snk = sk_ref[:, sl]
                mn = jnp.maximum(mm, snk)
                w = jnp.exp(mm - mn)
                ll = ll * w + jnp.exp(snk - mn)
                aa = aa * w
                mm = mn
            o = aa / ll
            outs[0][:, sl] = o
            outs[1][:, sl] = mm + jnp.log(ll)
            if want_bf16:
                outs[2][:, sl] = o.astype(BF16)

    th = T // H
    qspec = pl.BlockSpec((T, nq * 128), lambda j, t: (t, j))
    cur = pl.BlockSpec((T, 128), lambda j, t: (t, j))
    prev = pl.BlockSpec((H, 128), lambda j, t: (jnp.maximum(t * th - 1, 0), j))
    in_specs = [qspec, prev, cur, prev, cur]
    args = [q, k, k, v, v]
    if has_sink:
        in_specs.append(pl.BlockSpec((1, nq * 128), lambda j, t: (0, j)))
        args.append(sinks)
    out_dts = [F32, F32] + ([BF16] if want_bf16 else [])
    return _pc(
        body, name=name, grid=(ncol, nt), in_specs=in_specs,
        out_specs=[qspec] * len(out_dts),
        out_shape=[jax.ShapeDtypeStruct(q.shape, dt) for dt in out_dts],
        scratch_shapes=[pltpu.VMEM((T, LANES), F32), pltpu.VMEM((H + T, LANES), F32), pltpu.VMEM((H + T, LANES), F32),
                        pltpu.VMEM((T, LANES), F32), pltpu.VMEM((T, LANES), F32), pltpu.VMEM((T, LANES), F32)],
        compiler_params=_params(("parallel", "parallel")),
    )(*args)


def _band_bwd(q, k, v, lse, delta, do, *, patterns, nq, name, sinks=None):
    S, Ck = k.shape
    H, T, nt, nbt = _band_geometry(S, patterns)
    ncol = Ck // LANES
    scale = HEAD ** -0.5
    has_sink = sinks is not None
    nt_dims = (((1,), (1,)), ((), ()))
    tn_dims = (((0,), (0,)), ((), ()))

    def body(*refs):
        (qc_ref, qn_ref, kp_ref, kc_ref, vp_ref, vc_ref, lc_ref, ln_ref, ec_ref, en_ref, dc_ref, dn_ref) = refs[:12]
        sk_ref = refs[12] if has_sink else None
        n_out = 4 if has_sink else 3
        outs = refs[12 + has_sink:12 + has_sink + n_out]
        dq_ref, dk_ref, dv_ref = outs[:3]
        qf, kf, vf, lf, ef, df = refs[12 + has_sink + n_out:]
        t = pl.program_id(1)
        kf[0:H, :] = kp_ref[...].astype(F32)
        kf[H:H + T, :] = kc_ref[...].astype(F32)
        vf[0:H, :] = vp_ref[...].astype(F32)
        vf[H:H + T, :] = vc_ref[...].astype(F32)
        dk_ref[...] = jnp.zeros_like(dk_ref)
        dv_ref[...] = jnp.zeros_like(dv_ref)
        r_i = lax.broadcasted_iota(jnp.int32, (BLK, 2 * BLK), 0)
        c_i = lax.broadcasted_iota(jnp.int32, (BLK, 2 * BLK), 1)
        dist_q = r_i + BLK - c_i
        r_k = lax.broadcasted_iota(jnp.int32, (2 * BLK, BLK), 0)
        c_k = lax.broadcasted_iota(jnp.int32, (2 * BLK, BLK), 1)
        dist_k = r_k - c_k
        m1 = [_head_mask((BLK, LANES), h) for h in range(2)]
        m2 = [_head_mask((2 * BLK, LANES), h) for h in range(2)]

        for i in range(nq):
            sl = slice(i * 128, (i + 1) * 128)
            for buf, c_ref, n_ref in ((qf, qc_ref, qn_ref), (lf, lc_ref, ln_ref), (ef, ec_ref, en_ref), (df, dc_ref, dn_ref)):
                buf[0:T, :] = c_ref[:, sl].astype(F32)
                buf[T:T + H, :] = n_ref[:, sl].astype(F32)
            if has_sink:
                @pl.when(t == 0)
                def _():
                    outs[3][:, sl] = jnp.zeros((1, LANES), F32)

                outs[3][:, sl] += jnp.sum(-jnp.exp(sk_ref[:, sl] - lc_ref[:, sl]) * ec_ref[:, sl], axis=0, keepdims=True)
            for p, (dist, r) in enumerate(patterns):
                band_q = (dist_q >= 0) & (dist_q <= dist)
                band_k = (dist_k >= 0) & (dist_k <= dist)

                def unit(j, b, p=p, r=r, band_q=band_q, band_k=band_k):
                    q0 = j + b * (BLK * r)
                    qcat = _rows(qf, q0, 2 * BLK, r).astype(BF16)
                    docat = _rows(df, q0, 2 * BLK, r)
                    lcat = _rows(lf, q0, 2 * BLK, r)
                    ecat = _rows(ef, q0, 2 * BLK, r)
                    kcat = _rows(kf, H + q0 - BLK * r, 2 * BLK, r).astype(BF16)
                    vcat = _rows(vf, H + q0 - BLK * r, 2 * BLK, r).astype(BF16)
                    kc, vc = kcat[BLK:], vcat[BLK:]
                    valid_q = band_q & ((c_i >= BLK) | ((b > 0) | (t > 0)))
                    valid_k = band_k & ((r_k < BLK) | ((b < nbt // r - 1) | (t < nt - 1)))
                    dq2 = jnp.zeros((BLK, LANES), F32)
                    dk2 = jnp.zeros((BLK, LANES), F32)
                    dv2 = jnp.zeros((BLK, LANES), F32)
                    for half in range(2):
                        lcol = _col_of(lcat, m2[half])
                        ecol = _col_of(ecat, m2[half])
                        domb = jnp.where(m2[half], docat, 0.0).astype(BF16)
                        qmcat = jnp.where(m2[half], qcat, jnp.zeros_like(qcat))
                        s = lax.dot_general(qmcat[:BLK], kcat, nt_dims, preferred_element_type=F32) * scale
                        pr = jnp.where(valid_q, jnp.exp(s - lcol[:BLK]), 0.0)
                        dp = lax.dot_general(domb[:BLK], vcat, nt_dims, preferred_element_type=F32)
                        ds = (pr * (dp - ecol[:BLK])).astype(BF16)
                        dq2 = jnp.where(m1[half], jnp.dot(ds, kcat, preferred_element_type=F32) * scale, dq2)
                        s = lax.dot_general(qmcat, kc, nt_dims, preferred_element_type=F32) * scale
                        pr = jnp.where(valid_k, jnp.exp(s - lcol), 0.0)
                        dv2 += lax.dot_general(pr.astype(BF16), domb, tn_dims, preferred_element_type=F32)
                        dp = lax.dot_general(domb, vc, nt_dims, preferred_element_type=F32)
                        ds = (pr * (dp - ecol)).astype(BF16)
                        dk2 += lax.dot_general(ds, qmcat, tn_dims, preferred_element_type=F32) * scale
                    if p > 0:
                        dq2 = dq2 + _rows(dq_ref.at[:, sl], q0, BLK, r)
                    _set_rows(dq_ref.at[:, sl], q0, BLK, r, dq2)
                    _set_rows(dk_ref, q0, BLK, r, _rows(dk_ref, q0, BLK, r) + dk2)
                    _set_rows(dv_ref, q0, BLK, r, _rows(dv_ref, q0, BLK, r) + dv2)

                if r == 1:
                    def step(b, carry, unit=unit):
                        unit(0, b)
                        return carry
                    lax.fori_loop(0, nbt, step, 0)
                else:
                    for u in range(nbt):
                        unit(u % r, u // r)

    th = T // H
    last = S // H - 1
    qcur = pl.BlockSpec((T, nq * 128), lambda j, t: (t, j))
    qnext = pl.BlockSpec((H, nq * 128), lambda j, t: (jnp.minimum((t + 1) * th, last), j))
    cur = pl.BlockSpec((T, 128), lambda j, t: (t, j))
    prev = pl.BlockSpec((H, 128), lambda j, t: (jnp.maximum(t * th - 1, 0), j))
    in_specs = [qcur, qnext, prev, cur, prev, cur, qcur, qnext, qcur, qnext, qcur, qnext]
    args = [q, q, k, k, v, v, lse, lse, delta, delta, do, do]
    out_specs = [qcur, cur, cur]
    out_shape = [jax.ShapeDtypeStruct(q.shape, F32), jax.ShapeDtypeStruct(k.shape, F32), jax.ShapeDtypeStruct(k.shape, F32)]
    if has_sink:
        vec = pl.BlockSpec((1, nq * 128), lambda j, t: (0, j))
        in_specs.append(vec)
        args.append(sinks)
        out_specs.append(vec)
        out_shape.append(jax.ShapeDtypeStruct((1, q.shape[1]), F32))
    big = pltpu.VMEM((T + H, LANES), F32)
    return _pc(
        body, name=name, grid=(ncol, nt), in_specs=in_specs, out_specs=out_specs, out_shape=out_shape,
        scratch_shapes=[big] * 6,
        compiler_params=_params(("parallel", "arbitrary")),
    )(*args)


def _delta(do, o, name):
    S, C = do.shape
    tm = 512

    def body(do_ref, o_ref, g_ref, e_ref):
        for c in range(C // LANES):
            sl = slice(c * 128, (c + 1) * 128)
            e_ref[:, sl] = _gmean(do_ref[:, sl] * o_ref[:, sl], g_ref[...]) * float(HEAD)

    row = pl.BlockSpec((tm, C), lambda i: (i, 0))
    return _pc(
        body, name=name, grid=(S // tm,),
        in_specs=[row, row, pl.BlockSpec((LANES, LANES), lambda i: (0, 0))], out_specs=row,
        out_shape=jax.ShapeDtypeStruct((S, C), F32),
        compiler_params=_params(("parallel",)),
    )(do, o, _group_matrix())


def _even_post_fwd(ro, proj, gn, da):
    S = ro.shape[0]
    tm = 256

    def body(ro_ref, rg_ref, gn_ref, da_ref, mix_ref):
        for c in range(4):
            sl = slice(c * 128, (c + 1) * 128)
            x = ro_ref[:, sl]
            mu = jnp.mean(x, axis=1, keepdims=True)
            xc = x - mu
            var = jnp.mean(xc * xc, axis=1, keepdims=True)
            y = xc * lax.rsqrt(var + EPS) * gn_ref[:, sl]
            z = rg_ref[:, sl]
            mix_ref[:, sl] = (z * jax.nn.sigmoid(z) * y).astype(BF16)
        mix_ref[:, 512:1024] = da_ref[...].astype(BF16)

    row = lambda w: pl.BlockSpec((tm, w), lambda i: (i, 0))
    return _pc(
        body, name="even_post_fwd", grid=(S // tm,),
        in_specs=[row(512), pl.BlockSpec((tm, 512), lambda i: (i, 2)), pl.BlockSpec((1, 512), lambda i: (0, 0)), row(512)],
        out_specs=row(1024), out_shape=jax.ShapeDtypeStruct((S, 1024), BF16),
        compiler_params=_params(("parallel",)),
    )(ro, proj, gn, da)


def _even_post_bwd(ro, proj, gn, dmixed):
    S = ro.shape[0]
    tm = 256

    def body(ro_ref, rg_ref, gn_ref, dm_ref, dro_ref, drg_ref, dgn_ref):
        @pl.when(pl.program_id(0) == 0)
        def _():
            dgn_ref[...] = jnp.zeros_like(dgn_ref)

        for c in range(4):
            sl = slice(c * 128, (c + 1) * 128)
            x = ro_ref[:, sl]
            mu = jnp.mean(x, axis=1, keepdims=True)
            xc = x - mu
            rstd = lax.rsqrt(jnp.mean(xc * xc, axis=1, keepdims=True) + EPS)
            xh = xc * rstd
            gain = gn_ref[:, sl]
            y = xh * gain
            z = rg_ref[:, sl]
            sg = jax.nn.sigmoid(z)
            dra = dm_ref[:, sl]
            drg_ref[:, sl] = dra * y * sg * (1.0 + z * (1.0 - sg))
            dy = dra * z * sg
            dgn_ref[:, sl] += jnp.sum(dy * xh, axis=0, keepdims=True)
            dxh = dy * gain
            dro_ref[:, sl] = rstd * (dxh - jnp.mean(dxh, axis=1, keepdims=True)
                                     - xh * jnp.mean(dxh * xh, axis=1, keepdims=True))

    row = lambda w: pl.BlockSpec((tm, w), lambda i: (i, 0))
    vec = pl.BlockSpec((1, 512), lambda i: (0, 0))
    return _pc(
        body, name="even_post_bwd", grid=(S // tm,),
        in_specs=[row(512), pl.BlockSpec((tm, 512), lambda i: (i, 2)), vec, row(512)],
        out_specs=[row(512), row(512), vec],
        out_shape=[jax.ShapeDtypeStruct((S, 512), F32), jax.ShapeDtypeStruct((S, 512), F32),
                   jax.ShapeDtypeStruct((1, 512), F32)],
        compiler_params=_params(("arbitrary",)),
    )(ro, proj, gn, dmixed)


def _swa_pre_fwd(proj, tab, qg, kg):
    S = proj.shape[0]
    tm = 256

    def body(p_ref, tab_ref, qg_ref, kg_ref, g_ref, q_ref, k_ref, v_ref):
        Ap, Bp, Cp = _tab(tab_ref, 1)
        G = g_ref[...]
        lo = _head_mask((tm, LANES), 0)
        for c in range(8):
            sl = slice(c * 128, (c + 1) * 128)
            q_ref[:, sl] = _rope(_hn_fwd(p_ref[:, sl], qg_ref[...], G), Ap, Bp, Cp, 8).astype(BF16)
        for c in range(2):
            kn = _rope(_hn_fwd(p_ref[:, 1024 + c * 128:1024 + (c + 1) * 128], kg_ref[...], G), Ap, Bp, Cp, 8)
            vv = p_ref[:, 1280 + c * 128:1280 + (c + 1) * 128]
            for t, ref in ((kn, k_ref), (vv, v_ref)):
                sw = _roll(t, HEAD)
                ref[:, (2 * c) * 128:(2 * c + 1) * 128] = jnp.where(lo, t, sw).astype(BF16)
                ref[:, (2 * c + 1) * 128:(2 * c + 2) * 128] = jnp.where(lo, sw, t).astype(BF16)

    row = lambda w: pl.BlockSpec((tm, w), lambda i: (i, 0))
    vec = pl.BlockSpec((1, LANES), lambda i: (0, 0))
    return _pc(
        body, name="swa_pre_fwd", grid=(S // tm,),
        in_specs=[row(1536), row(768), vec, vec, pl.BlockSpec((LANES, LANES), lambda i: (0, 0))],
        out_specs=[row(1024), row(512), row(512)],
        out_shape=[jax.ShapeDtypeStruct((S, w), BF16) for w in (1024, 512, 512)],
        compiler_params=_params(("parallel",)),
    )(proj, tab, qg, kg, _group_matrix())


def _swa_pre_bwd(proj, tab, qg, kg, dq, dk, dv):
    S = proj.shape[0]
    tm = 256

    def body(p_ref, tab_ref, qg_ref, kg_ref, g_ref, dq_ref, dk_ref, dv_ref, dp_ref, db_ref, dqg_ref, dkg_ref):
        Ap, Bp, Cp = _tab(tab_ref, 1)
        G = g_ref[...]
        lo = _head_mask((tm, LANES), 0)

        @pl.when(pl.program_id(0) == 0)
        def _():
            db_ref[...] = jnp.zeros_like(db_ref)
            dqg_ref[...] = jnp.zeros_like(dqg_ref)
            dkg_ref[...] = jnp.zeros_like(dkg_ref)

        accq = jnp.zeros((1, LANES), F32)
        acck = jnp.zeros((1, LANES), F32)
        for c in range(8):
            sl = slice(c * 128, (c + 1) * 128)
            dx, dg = _hn_bwd(p_ref[:, sl], qg_ref[...], _rope_t(dq_ref[:, sl], Ap, Bp, Cp, 8), G)
            dp_ref[:, sl] = dx.astype(BF16)
            db_ref[:, sl] += jnp.sum(dx, axis=0, keepdims=True)
            accq = accq + dg
        for c in range(2):
            folded = []
            for ref in (dk_ref, dv_ref):
                a = ref[:, (2 * c) * 128:(2 * c + 1) * 128]
                b = ref[:, (2 * c + 1) * 128:(2 * c + 2) * 128]
                folded.append(jnp.where(lo, a + _roll(a, HEAD), b + _roll(b, HEAD)))
            ks = slice(1024 + c * 128, 1024 + (c + 1) * 128)
            dx, dg = _hn_bwd(p_ref[:, ks], kg_ref[...], _rope_t(folded[0], Ap, Bp, Cp, 8), G)
            dp_ref[:, ks] = dx.astype(BF16)
            db_ref[:, ks] += jnp.sum(dx, axis=0, keepdims=True)
            acck = acck + dg
            vs = slice(1280 + c * 128, 1280 + (c + 1) * 128)
            dp_ref[:, vs] = folded[1].astype(BF16)
            db_ref[:, vs] += jnp.sum(folded[1], axis=0, keepdims=True)
        dqg_ref[...] += _fold_halves(accq)
        dkg_ref[...] += _fold_halves(acck)

    row = lambda w: pl.BlockSpec((tm, w), lambda i: (i, 0))
    vec = pl.BlockSpec((1, LANES), lambda i: (0, 0))
    return _pc(
        body, name="swa_pre_bwd", grid=(S // tm,),
        in_specs=[row(1536), row(768), vec, vec, pl.BlockSpec((LANES, LANES), lambda i: (0, 0)),
                  row(1024), row(512), row(512)],
        out_specs=[row(1536), pl.BlockSpec((1, 1536), lambda i: (0, 0)), vec, vec],
        out_shape=[jax.ShapeDtypeStruct((S, 1536), BF16), jax.ShapeDtypeStruct((1, 1536), F32),
                   jax.ShapeDtypeStruct((1, LANES), F32), jax.ShapeDtypeStruct((1, LANES), F32)],
        compiler_params=_params(("arbitrary",)),
    )(proj, tab, qg, kg, _group_matrix(), dq, dk, dv)


def _loss_head(y, target):
    S, Dm = y.shape
    tm = 512

    def body(y_ref, t_ref, l_ref, dy_ref, dyb_ref):
        @pl.when(pl.program_id(0) == 0)
        def _():
            l_ref[...] = jnp.zeros_like(l_ref)

        e = y_ref[...] - t_ref[...]
        dy = e * (1.0 / Dm)
        dy_ref[...] = dy
        dyb_ref[...] = dy.astype(BF16)
        row = jnp.sum(e * e, axis=1, keepdims=True) * (0.5 / Dm)
        l_ref[...] += jnp.sum(row, axis=0, keepdims=True)

    row = pl.BlockSpec((tm, Dm), lambda i: (i, 0))
    return _pc(
        body, name="loss_head", grid=(S // tm,), in_specs=[row, row],
        out_specs=[pl.BlockSpec((1, LANES), lambda i: (0, 0)), row, row],
        out_shape=[jax.ShapeDtypeStruct((1, LANES), F32), jax.ShapeDtypeStruct((S, Dm), F32),
                   jax.ShapeDtypeStruct((S, Dm), BF16)],
        compiler_params=_params(("arbitrary",)),
    )(y, target)


def _relu2_of(u):
    r = jnp.maximum(u.astype(F32), 0.0)
    return r * r


def _drelu2(acc, u):
    return (acc * 2.0 * jnp.maximum(u.astype(F32), 0.0),)


def _add(acc, res):
    return (acc + res,)


_T = dict(tm=1024, tn=1024, tk=1024)


def _mlp_fwd(x, g, w_up, w_dn, tag):
    h = _rms_fwd(x, g, f"rms_mlp_fwd{tag}")
    u = _matmul(h, w_up, dims="nn", **_T, outs=[BF16], b_cs=True, name=f"mlp_up{tag}")
    x_out = _matmul(u, w_dn, dims="nn", **_T, outs=[F32], epilogue=_add, extras=[(x, "mn")], a_pro=_relu2_of,
                    name=f"mlp_down{tag}")
    return x_out, (h, u)


def _mlp_bwd(x, g, w_up, w_dn, saved, dy, dyb, tag):
    h, u = saved
    du = _matmul(dyb, w_dn, dims="nt", **_T, outs=[BF16], epilogue=_drelu2, extras=[(u, "mn")], name=f"mlp_du{tag}")
    dw_dn = _matmul(u, dyb, dims="tn", **_T, outs=[F32], a_pro=_relu2_of, name=f"mlp_dwdown{tag}")
    dw_up = _matmul(h, du, dims="tn", **_T, outs=[F32], o_cs=N_CHIPS, name=f"mlp_dwup{tag}")
    dh = _matmul(du, w_up, dims="nt", **_T, outs=[F32], b_cs=True, name=f"mlp_dh{tag}")
    dx, dxb, dg = _rms_bwd(x, g, dh, dy, f"rms_mlp_bwd{tag}")
    return dx, dxb, dg, dw_up, dw_dn


def _pattern_view(t, r):
    S, C = t.shape
    return t.reshape(S // r, r * C)


def _local_step(x, pos_col, target, W, P):
    S = x.shape[0]
    tab = _tables(pos_col)
    tile2 = lambda g: jnp.tile(g.reshape(1, HEAD), (1, 2))
    dqg, dkg = tile2(P["dil_q_gain"]), tile2(P["dil_k_gain"])
    sqg, skg = tile2(P["swa_q_gain"]), tile2(P["swa_k_gain"])
    gn = P["ret_gn_gain"].reshape(1, 512)
    sink_b = jnp.repeat(P["swa_sinks"].reshape(16), HEAD).reshape(1, 1024)

    h0 = _rms_fwd(x, P["norm_mix"][0], "rms_mix_fwd0")
    proj = _matmul(h0, W["hyb_w_in"], dims="nn", tm=1024, tn=768, tk=1024, outs=[F32], b_cs=True, name="hyb_in")
    rq, rk, rv, dq, dk, dv = _even_pre_fwd(proj, tab, dqg, dkg)
    ro, states = _ret_fwd(rq, rk, rv)
    dil = [(w // r, r) for w, r in DIL_PATTERNS]
    da, dlse = _band_fwd(dq, dk, dv, patterns=dil, nq=1, name="dil_fwd")
    mixed = _even_post_fwd(ro, proj, gn, da)
    x1 = _matmul(mixed, W["hyb_w_out"], dims="nn", **_T, outs=[F32], epilogue=_add, extras=[(x, "mn")], name="hyb_out")
    x2, mlp0 = _mlp_fwd(x1, P["norm_mlp"][0], W["mlp_w_up"][0], W["mlp_w_down"][0], "0")

    h2 = _rms_fwd(x2, P["norm_mix"][1], "rms_mix_fwd1")
    proj2 = _matmul(h2, W["swa_w_qkv"], dims="nn", tm=1024, tn=384, tk=1024, outs=[F32], b_cs=True,
                    epilogue=_add, extras=[(P["swa_b_qkv_full"].reshape(1, 1536), "n")], name="swa_qkv")
    sq, sk, sv = _swa_pre_fwd(proj2, tab, sqg, skg)
    swa = [(SWA_DIST, 1)]
    so, slse, so_b = _band_fwd(sq, sk, sv, patterns=swa, nq=2, name="swa_fwd", sinks=sink_b, want_bf16=True)
    x3 = _matmul(so_b, W["swa_w_out"], dims="nn", **_T, outs=[F32], epilogue=_add, extras=[(x2, "mn")], name="swa_out")
    y, mlp1 = _mlp_fwd(x3, P["norm_mlp"][1], W["mlp_w_up"][1], W["mlp_w_down"][1], "1")
    loss, dy, dyb = _loss_head(y, target)

    gw, gp = {}, {}
    dx3, dx3b, dg_mlp1, gw["mlp_w_up1"], gw["mlp_w_down1"] = _mlp_bwd(x3, P["norm_mlp"][1], W["mlp_w_up"][1],
                                                                       W["mlp_w_down"][1], mlp1, dy, dyb, "1")
    gw["swa_w_out"] = _matmul(so_b, dx3b, dims="tn", **_T, outs=[F32], name="swa_dwout")
    dso = _matmul(dx3b, W["swa_w_out"], dims="nt", **_T, outs=[F32], name="swa_do")
    dsq, dsk, dsv, dsink = _band_bwd(sq, sk, sv, slse, _delta(dso, so, "swa_delta"), dso, patterns=swa, nq=2,
                                     name="swa_bwd", sinks=sink_b)
    dproj2, gp["swa_b_qkv"], gp["swa_q_gain"], gp["swa_k_gain"] = _swa_pre_bwd(proj2, tab, sqg, skg, dsq, dsk, dsv)
    gp["swa_sinks"] = dsink
    gw["swa_w_qkv"] = _matmul(h2, dproj2, dims="tn", tm=1024, tn=384, tk=1024, outs=[F32], o_cs=N_CHIPS, name="swa_dwqkv")
    dh2 = _matmul(dproj2, W["swa_w_qkv"], dims="nt", tm=1024, tn=1024, tk=384, outs=[F32], b_cs=True, name="swa_dh")
    dx2, dx2b, dg_mix1 = _rms_bwd(x2, P["norm_mix"][1], dh2, dx3, "rms_mix_bwd1")

    dx1, dx1b, dg_mlp0, gw["mlp_w_up0"], gw["mlp_w_down0"] = _mlp_bwd(x1, P["norm_mlp"][0], W["mlp_w_up"][0],
                                                                       W["mlp_w_down"][0], mlp0, dx2, dx2b, "0")
    gw["hyb_w_out"] = _matmul(mixed, dx1b, dims="tn", **_T, outs=[F32], name="hyb_dwout")
    dmixed = _matmul(dx1b, W["hyb_w_out"], dims="nt", **_T, outs=[F32], name="hyb_dmixed")
    dro, drg, gp["ret_gn_gain"] = _even_post_bwd(ro, proj, gn, dmixed)
    drq, drk, drv = _ret_bwd(rq, rk, rv, states, dro)
    dda = dmixed[:, 512:]
    ddq, ddk, ddv = _band_bwd(dq, dk, dv, dlse, _delta(dda, da, "dil_delta"), dda, patterns=dil, nq=1, name="dil_bwd")
    dproj, gp["dil_q_gain"], gp["dil_k_gain"] = _even_pre_bwd(proj, tab, dqg, dkg, drq, drk, drv, drg, [ddq], [ddk], [ddv])
    gw["hyb_w_in"] = _matmul(h0, dproj, dims="tn", tm=1024, tn=768, tk=1024, outs=[F32], o_cs=N_CHIPS, name="hyb_dwin")
    dh0 = _matmul(dproj, W["hyb_w_in"], dims="nt", tm=1024, tn=1024, tk=768, outs=[F32], b_cs=True, name="hyb_dh")
    grad_x, _, dg_mix0 = _rms_bwd(x, P["norm_mix"][0], dh0, dx1, "rms_mix_bwd0")
    gp["norm_mix"] = jnp.concatenate([dg_mix0, dg_mix1], axis=0)
    gp["norm_mlp"] = jnp.concatenate([dg_mlp0, dg_mlp1], axis=0)
    return loss, grad_x, gw, gp


HBM = pl.BlockSpec(memory_space=pltpu.HBM)


def _place():
    x, y, c = lax.axis_index("x"), lax.axis_index("y"), lax.axis_index("c")
    chips = [(1 - x, y), (x, 1 - y), (1 - x, 1 - y)]
    return x, y, c, chips


def _allgather_shards(buf):
    _, R, Wd = buf.shape
    Rh = R // 2

    def body(b_ref, out_ref, send_sems, recv_sems):
        x, y, c, chips = _place()
        sibling = (x, y, 1 - c)

        def copy(k, chip, core, to):
            block = b_ref.at[2 * chip[0] + chip[1], pl.ds(core * Rh, Rh), :]
            return pltpu.make_async_remote_copy(
                src_ref=block, dst_ref=block, send_sem=send_sems.at[k], recv_sem=recv_sems.at[k],
                device_id=to, device_id_type=MESH)

        first = [copy(k, (x, y), c, (*chip, c)) for k, chip in enumerate(chips)]
        for cp in first:
            cp.start()
        passed = [copy(3 + k, chip, c, sibling) for k, chip in enumerate(chips)]
        for k, chip in enumerate(chips):
            copy(k, chip, c, (x, y, c)).wait_recv()
            passed[k].start()
        for k, chip in enumerate(chips):
            copy(3 + k, chip, 1 - c, (x, y, c)).wait_recv()
        for cp in first + passed:
            cp.wait_send()

    return _pc(
        body, name="allgather_weights", in_specs=[HBM], out_specs=HBM,
        out_shape=jax.ShapeDtypeStruct(buf.shape, buf.dtype), input_output_aliases={0: 0},
        scratch_shapes=[pltpu.SemaphoreType.DMA((6,)), pltpu.SemaphoreType.DMA((6,))],
    )(buf)


def _swap_halves(ts):
    nt = len(ts)

    def body(*refs):
        t_refs, l_refs, send_sems, recv_sems = refs[:nt], refs[nt:2 * nt], refs[-2], refs[-1]
        x, y, c, _ = _place()
        cps = []
        for k in range(nt):
            rh = t_refs[k].shape[1] // 2
            cps.append(pltpu.make_async_remote_copy(
                src_ref=t_refs[k].at[:, pl.ds((1 - c) * rh, rh), :], dst_ref=l_refs[k],
                send_sem=send_sems.at[k], recv_sem=recv_sems.at[k], device_id=(x, y, 1 - c), device_id_type=MESH))
        for cp in cps:
            cp.start()
        for cp in cps:
            cp.wait()

    return _pc(
        body, name="grad_swap_halves", in_specs=[HBM] * nt, out_specs=[HBM] * nt,
        out_shape=[jax.ShapeDtypeStruct((t.shape[0], t.shape[1] // 2, t.shape[2]), F32) for t in ts],
        scratch_shapes=[pltpu.SemaphoreType.DMA((nt,)), pltpu.SemaphoreType.DMA((nt,))],
    )(*ts)


def _pair_sum(t, l, place, name):
    _, r, cols = t.shape
    rh = r // 2
    tr = min(rh, 256)
    nr = rh // tr

    def body(pl_ref, t_ref, l_ref, o_ref):
        o_ref[...] = (t_ref[...] + l_ref[...]).astype(BF16)

    return _pc(
        body, name=name,
        grid_spec=pltpu.PrefetchScalarGridSpec(
            num_scalar_prefetch=1, grid=(N_CHIPS, nr),
            in_specs=[pl.BlockSpec((None, tr, cols), lambda s, i, p: (s, p[1] * nr + i, 0)),
                      pl.BlockSpec((None, tr, cols), lambda s, i, p: (s, i, 0))],
            out_specs=pl.BlockSpec((None, tr, cols), lambda s, i, p: (s, i, 0))),
        out_shape=jax.ShapeDtypeStruct((N_CHIPS, rh, cols), BF16),
        compiler_params=_params(("parallel", "parallel")),
    )(place, t, l)


def _exchange_chips(ps):
    nt = len(ps)

    def body(*refs):
        p_refs, r_refs, send_sems, recv_sems = refs[:nt], refs[nt:2 * nt], refs[-2], refs[-1]
        x, y, c, chips = _place()
        cps = []
        for t in range(nt):
            for k, chip in enumerate(chips):
                cps.append(pltpu.make_async_remote_copy(
                    src_ref=p_refs[t].at[2 * chip[0] + chip[1]], dst_ref=r_refs[t].at[k],
                    send_sem=send_sems.at[3 * t + k], recv_sem=recv_sems.at[3 * t + k],
                    device_id=(*chip, c), device_id_type=MESH))
        for cp in cps:
            cp.start()
        for cp in cps:
            cp.wait()

    return _pc(
        body, name="grad_exchange_chips", in_specs=[HBM] * nt, out_specs=[HBM] * nt,
        out_shape=[jax.ShapeDtypeStruct((3,) + p.shape[1:], BF16) for p in ps],
        scratch_shapes=[pltpu.SemaphoreType.DMA((3 * nt,)), pltpu.SemaphoreType.DMA((3 * nt,))],
    )(*ps)


def _final_sum(t, l, rcv, place, name):
    _, r, cols = t.shape
    rh = r // 2
    tr = min(rh, 256)
    nr = rh // tr

    def body(pl_ref, t_ref, l_ref, r_ref, o_ref):
        acc = t_ref[...] + l_ref[...]
        for k in range(3):
            acc = acc + r_ref[k].astype(F32)
        o_ref[...] = acc

    return _pc(
        body, name=name,
        grid_spec=pltpu.PrefetchScalarGridSpec(
            num_scalar_prefetch=1, grid=(nr,),
            in_specs=[pl.BlockSpec((None, tr, cols), lambda i, p: (p[0], p[1] * nr + i, 0)),
                      pl.BlockSpec((None, tr, cols), lambda i, p: (p[0], i, 0)),
                      pl.BlockSpec((3, tr, cols), lambda i, p: (0, i, 0))],
            out_specs=pl.BlockSpec((tr, cols), lambda i, p: (p[1] * nr + i, 0))),
        out_shape=jax.ShapeDtypeStruct((r, cols), F32),
        compiler_params=_params(("parallel",)),
    )(place, t, l, rcv)


def _share_halves(hs):
    nt = len(hs)

    def body(*refs):
        h_refs, send_sems, recv_sems = refs[:nt], refs[-2], refs[-1]
        x, y, c, _ = _place()
        cps = []
        for k in range(nt):
            rh = h_refs[k].shape[0] // 2
            half = h_refs[k].at[pl.ds(c * rh, rh), :]
            cps.append(pltpu.make_async_remote_copy(
                src_ref=half, dst_ref=half, send_sem=send_sems.at[k], recv_sem=recv_sems.at[k],
                device_id=(x, y, 1 - c), device_id_type=MESH))
        for cp in cps:
            cp.start()
        for cp in cps:
            cp.wait()

    return _pc(
        body, name="grad_share_halves", in_specs=[HBM] * nt, out_specs=[HBM] * nt,
        out_shape=[jax.ShapeDtypeStruct(h.shape, F32) for h in hs],
        input_output_aliases={k: k for k in range(nt)},
        scratch_shapes=[pltpu.SemaphoreType.DMA((nt,)), pltpu.SemaphoreType.DMA((nt,))],
    )(*hs)


def _allgather_small(v):
    rows = v.shape[0]

    def body(v_ref, out_ref, send_sems, recv_sems):
        x, y, c, _ = _place()
        me = 4 * x + 2 * y + c
        out_ref[me] = v_ref[...]
        cps = []
        for k in range(1, 8):
            fx, fy, fc = (k >> 2) & 1, (k >> 1) & 1, k & 1
            to = (1 - x if fx else x, 1 - y if fy else y, 1 - c if fc else c)
            cps.append(pltpu.make_async_remote_copy(
                src_ref=v_ref, dst_ref=out_ref.at[me], send_sem=send_sems.at[k - 1], recv_sem=recv_sems.at[k - 1],
                device_id=to, device_id_type=MESH))
        for cp in cps:
            cp.start()
        for cp in cps:
            cp.wait()

    return _pc(
        body, name="allgather_small",
        in_specs=[pl.BlockSpec(memory_space=pltpu.VMEM)], out_specs=pl.BlockSpec(memory_space=pltpu.VMEM),
        out_shape=jax.ShapeDtypeStruct((8, rows, LANES), F32),
        scratch_shapes=[pltpu.SemaphoreType.DMA((7,)), pltpu.SemaphoreType.DMA((7,))],
    )(v)


def _adamw_math(w, g, m, v):
    m = ADAM_B1 * m + (1.0 - ADAM_B1) * g
    v = ADAM_B2 * v + (1.0 - ADAM_B2) * (g * g)
    m_hat = m / (1.0 - ADAM_B1 ** ADAM_STEP)
    v_hat = v / (1.0 - ADAM_B2 ** ADAM_STEP)
    return -ADAM_LR * (m_hat / (jnp.sqrt(v_hat) + ADAM_EPS) + ADAM_WD * w), m, v


def _adamw(w, g, m, v, name):
    r, cols = w.shape
    tr = min(r, 256)

    def body(w_ref, g_ref, m_ref, v_ref, d_ref, mo_ref, vo_ref):
        d, mn, vn = _adamw_math(w_ref[...], g_ref[...], m_ref[...], v_ref[...])
        d_ref[...] = d
        mo_ref[...] = mn
        vo_ref[...] = vn

    row = pl.BlockSpec((tr, cols), lambda i: (i, 0))
    return _pc(
        body, name=name, grid=(r // tr,), in_specs=[row] * 4, out_specs=[row] * 3,
        out_shape=[jax.ShapeDtypeStruct((r, cols), F32)] * 3,
        compiler_params=_params(("parallel",)),
    )(w, g, m, v)


def _adamw_small(w, gathered, m, v):
    rows = w.shape[0]

    def body(w_ref, g_ref, m_ref, v_ref, go_ref, d_ref, mo_ref, vo_ref):
        g = g_ref[0]
        for k in range(1, 8):
            g = g + g_ref[k]
        d, mn, vn = _adamw_math(w_ref[...], g, m_ref[...], v_ref[...])
        go_ref[...] = g
        d_ref[...] = d
        mo_ref[...] = mn
        vo_ref[...] = vn

    return _pc(
        body, name="adamw_small",
        out_shape=[jax.ShapeDtypeStruct((rows, LANES), F32)] * 4,
    )(w, gathered, m, v)


_BIAS_ROWS = 32


def _pack_shard(mlp_w_up, mlp_w_down, hyb_w_in, hyb_w_out, swa_w_qkv, swa_w_out, swa_b_qkv):
    parts = [t.astype(BF16).reshape(-1, 1024) for t in (mlp_w_up, mlp_w_down, hyb_w_in, hyb_w_out, swa_w_qkv, swa_w_out)]
    bias = lax.bitcast_convert_type(swa_b_qkv.reshape(384), BF16).reshape(1, 768)
    bias = jnp.pad(bias, ((0, _BIAS_ROWS - 1), (0, 256)))
    return jnp.concatenate(parts + [bias], axis=0)


def _unpack_weights(g):
    W = {
        "mlp_w_up": [g[:, l * 1024:(l + 1) * 1024, :] for l in range(2)],
        "mlp_w_down": [g[:, 2048 + l * 1024:2048 + (l + 1) * 1024, :].reshape(D_FF, D_MODEL) for l in range(2)],
        "hyb_w_in": g[:, 4096:4864, :].reshape(N_CHIPS, 1024, 768),
        "hyb_w_out": g[:, 4864:5120, :].reshape(1024, 1024),
        "swa_w_qkv": g[:, 5120:5504, :].reshape(N_CHIPS, 1024, 384),
        "swa_w_out": g[:, 5504:5760, :].reshape(1024, 1024),
    }
    bias = lax.bitcast_convert_type(g[:, 5760, :768].reshape(N_CHIPS, 384, 2), F32).reshape(1536)
    return W, bias


_SMALL = (("norm_mix", 16), ("norm_mlp", 16), ("ret_gn_gain", 4), ("dil_q_gain", 1), ("dil_k_gain", 1),
          ("swa_b_qkv", 12), ("swa_q_gain", 1), ("swa_k_gain", 1), ("swa_sinks", 1))
_SMALL_ROWS = 56


def _pack_small(d):
    rows = [d[n].reshape(r, LANES) for n, r in _SMALL]
    used = sum(r for _, r in _SMALL)
    return jnp.concatenate(rows + [jnp.zeros((_SMALL_ROWS - used, LANES), F32)], axis=0)


def _unpack_small(p):
    out, o = {}, 0
    for n, r in _SMALL:
        out[n] = p[o:o + r]
        o += r
    return out


def kernel(x, positions, norm_mix, norm_mlp, mlp_w_up, mlp_w_down, hyb_w_in, hyb_w_out, ret_gn_gain, dil_q_gain, dil_k_gain, swa_w_qkv, swa_b_qkv, swa_w_out, swa_q_gain, swa_k_gain, swa_sinks, loss_target, m_norm_mix, m_norm_mlp, m_mlp_w_up, m_mlp_w_down, m_hyb_w_in, m_hyb_w_out, m_ret_gn_gain, m_dil_q_gain, m_dil_k_gain, m_swa_w_qkv, m_swa_b_qkv, m_swa_w_out, m_swa_q_gain, m_swa_k_gain, m_swa_sinks, v_norm_mix, v_norm_mlp, v_mlp_w_up, v_mlp_w_down, v_hyb_w_in, v_hyb_w_out, v_ret_gn_gain, v_dil_q_gain, v_dil_k_gain, v_swa_w_qkv, v_swa_b_qkv, v_swa_w_out, v_swa_q_gain, v_swa_k_gain, v_swa_sinks):
    ax, ay, ac = lax.axis_index("x"), lax.axis_index("y"), lax.axis_index("c")
    chip = 2 * ax + ay
    place = jnp.stack([chip, ac]).astype(jnp.int32)
    S = x.shape[1]

    flat = _pack_shard(mlp_w_up, mlp_w_down, hyb_w_in[0], hyb_w_out[0], swa_w_qkv[0], swa_w_out[0], swa_b_qkv[0])
    buf = lax.dynamic_update_slice(jnp.zeros((N_CHIPS,) + flat.shape, BF16), flat[None], (chip, 0, 0))
    W, bias_full = _unpack_weights(_allgather_shards(buf))
    P = dict(norm_mix=norm_mix, norm_mlp=norm_mlp, ret_gn_gain=ret_gn_gain, dil_q_gain=dil_q_gain, dil_k_gain=dil_k_gain,
             swa_q_gain=swa_q_gain, swa_k_gain=swa_k_gain, swa_sinks=swa_sinks, swa_b_qkv_full=bias_full)

    loss_l, grad_x, gw, gp = _local_step(x[0], positions.reshape(S, 1), loss_target[0], W, P)
    loss = lax.psum(loss_l[0, 0], ("x", "y", "c"))

    slab = lambda t, r: t.reshape(N_CHIPS, r, t.size // (N_CHIPS * r))
    names = ["mlp_w_up0", "mlp_w_up1", "mlp_w_down0", "mlp_w_down1", "hyb_w_in", "hyb_w_out", "swa_w_qkv", "swa_w_out"]
    rows = [1024, 1024, 1024, 1024, 1024, 256, 1024, 256]
    ts = [slab(gw[n], r) for n, r in zip(names, rows)]
    ls = _swap_halves(ts)
    ps = [_pair_sum(t, l, place, f"pair_sum_{n}") for t, l, n in zip(ts, ls, names)]
    rs = _exchange_chips(ps)
    hs = [_final_sum(t, l, r, place, f"final_sum_{n}") for t, l, r, n in zip(ts, ls, rs, names)]
    gs = dict(zip(names, _share_halves(hs)))
    shards = dict(mlp_w_up0=(mlp_w_up[0], m_mlp_w_up[0], v_mlp_w_up[0]), mlp_w_up1=(mlp_w_up[1], m_mlp_w_up[1], v_mlp_w_up[1]),
                  mlp_w_down0=(mlp_w_down[0], m_mlp_w_down[0], v_mlp_w_down[0]),
                  mlp_w_down1=(mlp_w_down[1], m_mlp_w_down[1], v_mlp_w_down[1]),
                  hyb_w_in=(hyb_w_in[0], m_hyb_w_in[0], v_hyb_w_in[0]), hyb_w_out=(hyb_w_out[0], m_hyb_w_out[0], v_hyb_w_out[0]),
                  swa_w_qkv=(swa_w_qkv[0], m_swa_w_qkv[0], v_swa_w_qkv[0]), swa_w_out=(swa_w_out[0], m_swa_w_out[0], v_swa_w_out[0]))
    big = {}
    for n in names:
        w, m, v = shards[n]
        big[n] = (gs[n],) + tuple(_adamw(w, gs[n], m, v, f"adamw_{n}"))

    def big_out(n, k):
        if n in ("mlp_w_up", "mlp_w_down"):
            return jnp.stack([big[n + "0"][k], big[n + "1"][k]])
        return big[n][k][None]

    gsm = dict(gp)
    gsm["swa_sinks"] = jnp.pad(gp["swa_sinks"].reshape(16, HEAD)[:, 0], (0, LANES - 16))
    gathered = _allgather_small(_pack_small(gsm))

    def small_pack(norm_mix, norm_mlp, gn, dq, dk, b, sq, sk, sinks):
        dup = lambda t: jnp.tile(t.reshape(1, HEAD), (1, 2))
        bias = lax.dynamic_update_slice(jnp.zeros((12, LANES), F32), b.reshape(3, LANES), (3 * chip, 0))
        return _pack_small(dict(norm_mix=norm_mix, norm_mlp=norm_mlp, ret_gn_gain=gn, dil_q_gain=dup(dq), dil_k_gain=dup(dk),
                                swa_b_qkv=bias, swa_q_gain=dup(sq), swa_k_gain=dup(sk),
                                swa_sinks=jnp.pad(sinks.reshape(16), (0, LANES - 16))))

    pw = small_pack(norm_mix, norm_mlp, ret_gn_gain, dil_q_gain, dil_k_gain, swa_b_qkv, swa_q_gain, swa_k_gain, swa_sinks)
    pm = small_pack(m_norm_mix, m_norm_mlp, m_ret_gn_gain, m_dil_q_gain, m_dil_k_gain, m_swa_b_qkv, m_swa_q_gain, m_swa_k_gain, m_swa_sinks)
    pv = small_pack(v_norm_mix, v_norm_mlp, v_ret_gn_gain, v_dil_q_gain, v_dil_k_gain, v_swa_b_qkv, v_swa_q_gain, v_swa_k_gain, v_swa_sinks)
    small = [_unpack_small(t) for t in _adamw_small(pw, gathered, pm, pv)]

    def small_out(n, k):
        t = small[k][n]
        if n in ("norm_mix", "norm_mlp"):
            return t.reshape(2, D_MODEL)
        if n == "ret_gn_gain":
            return t.reshape(1, RET_HEADS, 128)
        if n == "swa_b_qkv":
            return lax.dynamic_slice(t, (3 * chip, 0), (3, LANES)).reshape(1, 384)
        if n == "swa_sinks":
            return t[0, :16].reshape(1, 16)
        return t[0, :HEAD].reshape(1, HEAD)

    order = ["norm_mix", "norm_mlp", "mlp_w_up", "mlp_w_down", "hyb_w_in", "hyb_w_out", "ret_gn_gain", "dil_q_gain",
             "dil_k_gain", "swa_w_qkv", "swa_b_qkv", "swa_w_out", "swa_q_gain", "swa_k_gain", "swa_sinks"]
    is_big = {"mlp_w_up", "mlp_w_down", "hyb_w_in", "hyb_w_out", "swa_w_qkv", "swa_w_out"}
    outs = [loss, grad_x[None]]
    for k in range(4):
        outs += [big_out(n, k) if n in is_big else small_out(n, k) for n in order]
    return tuple(outs)
```

```python
import functools
import math

import numpy as np
import jax
import jax.numpy as jnp
from jax import lax
from jax.experimental import pallas as pl
from jax.experimental.pallas import tpu as pltpu

F32, BF16 = jnp.float32, jnp.bfloat16
HIGHEST = lax.Precision.HIGHEST
MESH = pl.DeviceIdType.MESH

LANES = 128
VMEM_LIMIT = 48 << 20
D_MODEL = 1024
D_FF = 4096
HEAD = 64
EPS = 1e-6
BLK = 128
RET_HEADS = 4
RET_THETA = 10000.0
ROPE_THETA = 500000.0
ROPE_DIMS = 16
DIL_PATTERNS = ((128, 1), (512, 4), (2048, 16))
SWA_DIST = 127
N_CHIPS = 4
ADAM_LR, ADAM_B1, ADAM_B2, ADAM_EPS, ADAM_WD, ADAM_STEP = 0.001, 0.9, 0.999, 1e-08, 0.01, 10

_LOG_GAMMA = [float(np.log1p(-np.exp2(np.float32(-5.0 - h)))) for h in range(RET_HEADS)]


def _pc(body, **kw):
    return pl.pallas_call(body, **kw)


def _params(sem):
    return pltpu.CompilerParams(dimension_semantics=sem, vmem_limit_bytes=VMEM_LIMIT)


def _matmul(a, b, *, dims, tm, tn, tk, outs, name, epilogue=None, extras=(), b_cs=False, o_cs=0, a_pro=None):
    if dims == "nn":
        M, K = a.shape
        N = b.shape[0] * b.shape[2] if b_cs else b.shape[1]
        a_spec = pl.BlockSpec((tm, tk), lambda i, j, k: (i, k))
        if b_cs:
            npt = b.shape[2] // tn
            b_spec = pl.BlockSpec((None, tk, tn), lambda i, j, k: (j // npt, k, j % npt))
        else:
            b_spec = pl.BlockSpec((tk, tn), lambda i, j, k: (k, j))
        contract = (((1,), (0,)), ((), ()))
    elif dims == "nt":
        M, K = a.shape
        N = b.shape[1] if b_cs else b.shape[0]
        a_spec = pl.BlockSpec((tm, tk), lambda i, j, k: (i, k))
        if b_cs:
            kpt = b.shape[2] // tk
            b_spec = pl.BlockSpec((None, tn, tk), lambda i, j, k: (k // kpt, j, k % kpt))
        else:
            b_spec = pl.BlockSpec((tn, tk), lambda i, j, k: (j, k))
        contract = (((1,), (1,)), ((), ()))
    else:
        K, M = a.shape
        N = b.shape[1]
        a_spec = pl.BlockSpec((tk, tm), lambda i, j, k: (k, i))
        b_spec = pl.BlockSpec((tk, tn), lambda i, j, k: (k, j))
        contract = (((0,), (0,)), ((), ()))
    assert M % tm == 0 and N % tn == 0 and K % tk == 0, (name, M, N, K, tm, tn, tk)
    nk = K // tk
    ex_specs = []
    for arr, kind in extras:
        if kind == "mn":
            ex_specs.append(pl.BlockSpec((tm, tn), lambda i, j, k: (i, j)))
        else:
            ex_specs.append(pl.BlockSpec((1, tn), lambda i, j, k: (0, j)))
    if o_cs:
        n_sh = N // o_cs
        opt = n_sh // tn
        o_shape = (o_cs, M, n_sh)
        o_spec = pl.BlockSpec((None, tm, tn), lambda i, j, k: (j // opt, i, j % opt))
    else:
        o_shape = (M, N)
        o_spec = pl.BlockSpec((tm, tn), lambda i, j, k: (i, j))
    n_ex, n_out = len(extras), len(outs)
    if epilogue is None:
        epilogue = lambda acc: (acc,)

    def body(a_ref, b_ref, *rest):
        ex, o_refs, acc = rest[:n_ex], rest[n_ex:n_ex + n_out], rest[-1]
        k = pl.program_id(2)

        @pl.when(k == 0)
        def _():
            acc[...] = jnp.zeros_like(acc)

        av = a_ref[...] if a_pro is None else a_pro(a_ref[...])
        acc[...] += lax.dot_general(av.astype(BF16), b_ref[...].astype(BF16), contract, preferred_element_type=F32)

        @pl.when(k == nk - 1)
        def _():
            vals = epilogue(acc[...], *[e[...] for e in ex])
            for r, v in zip(o_refs, vals):
                r[...] = v.astype(r.dtype)

    res = _pc(
        body, name=name, grid=(M // tm, N // tn, nk),
        in_specs=[a_spec, b_spec] + ex_specs,
        out_specs=[o_spec] * n_out,
        out_shape=[jax.ShapeDtypeStruct(o_shape, dt) for dt in outs],
        scratch_shapes=[pltpu.VMEM((tm, tn), F32)],
        compiler_params=_params(("parallel", "parallel", "arbitrary")),
    )(a, b, *[e for e, _ in extras])
    return res[0] if n_out == 1 else res


def _roll(x, s):
    return pltpu.roll(x, s % LANES, 1)


def _rope(x, A, B, C, half):
    return x * A + _roll(x, LANES - half) * B + _roll(x, half) * C


def _rope_t(g, A, B, C, half):
    return g * A + _roll(g * B, half) + _roll(g * C, LANES - half)


def _gmean(x, G):
    return jnp.dot(x, G, precision=HIGHEST, preferred_element_type=F32)


def _head_mask(shape, half):
    lane = lax.broadcasted_iota(jnp.int32, shape, len(shape) - 1)
    return (lane >= HEAD) if half else (lane < HEAD)


def _group_matrix():
    i = np.arange(LANES)
    return jnp.asarray((i[:, None] // HEAD == i[None, :] // HEAD).astype(np.float32) / HEAD)


def _rope_inv():
    l = np.arange(LANES) % HEAD
    inv_r = np.power(np.float32(RET_THETA), -(l % 32).astype(np.float32) * np.float32(2.0 / HEAD))
    hp = ROPE_DIMS // 2
    inv_p = np.power(np.float32(ROPE_THETA), -(l % hp).astype(np.float32) * np.float32(2.0 / ROPE_DIMS))
    inv_p = np.where(l < ROPE_DIMS, inv_p, 0.0)
    return jnp.asarray(np.stack([inv_r, inv_p]).astype(np.float32))


def _tables(pos_col):
    S = pos_col.shape[0]
    tm = 512
    hp = ROPE_DIMS // 2

    def body(p_ref, inv_ref, o_ref):
        p = p_ref[...].astype(F32)
        lane = lax.broadcasted_iota(jnp.int32, (tm, LANES), 1) % HEAD
        ang = p * inv_ref[0:1, :]
        c, s = jnp.cos(ang), jnp.sin(ang)
        o_ref[:, 0:128] = c
        o_ref[:, 128:256] = jnp.where(lane < 32, -s, 0.0)
        o_ref[:, 256:384] = jnp.where(lane >= 32, s, 0.0)
        ang = p * inv_ref[1:2, :]
        c, s = jnp.cos(ang), jnp.sin(ang)
        o_ref[:, 384:512] = c
        o_ref[:, 512:640] = jnp.where(lane < hp, -s, 0.0)
        o_ref[:, 640:768] = jnp.where((lane >= hp) & (lane < ROPE_DIMS), s, 0.0)

    return _pc(
        body, name="rope_tables", grid=(S // tm,),
        in_specs=[pl.BlockSpec((tm, 1), lambda i: (i, 0)), pl.BlockSpec((2, LANES), lambda i: (0, 0))],
        out_specs=pl.BlockSpec((tm, 768), lambda i: (i, 0)),
        out_shape=jax.ShapeDtypeStruct((S, 768), F32),
        compiler_params=_params(("parallel",)),
    )(pos_col, _rope_inv())


def _tab(tab_ref, which):
    o = 384 * which
    return tab_ref[:, o:o + 128], tab_ref[:, o + 128:o + 256], tab_ref[:, o + 256:o + 384]


def _rms_fwd(x, g, name):
    S, Dm = x.shape
    tm = 512

    def body(x_ref, g_ref, h_ref):
        xv = x_ref[...]
        r = lax.rsqrt(jnp.mean(xv * xv, axis=-1, keepdims=True) + EPS)
        h_ref[...] = (xv * r * g_ref[...]).astype(BF16)

    return _pc(
        body, name=name, grid=(S // tm,),
        in_specs=[pl.BlockSpec((tm, Dm), lambda i: (i, 0)), pl.BlockSpec((1, Dm), lambda i: (0, 0))],
        out_specs=pl.BlockSpec((tm, Dm), lambda i: (i, 0)),
        out_shape=jax.ShapeDtypeStruct((S, Dm), BF16),
        compiler_params=_params(("parallel",)),
    )(x, g.reshape(1, Dm))


def _rms_bwd(x, g, dh, dres, name):
    S, Dm = x.shape
    tm = 512

    def body(x_ref, g_ref, dh_ref, dres_ref, dx_ref, dxb_ref, dg_ref):
        xv, dhv = x_ref[...], dh_ref[...]
        r = lax.rsqrt(jnp.mean(xv * xv, axis=-1, keepdims=True) + EPS)
        t = dhv * g_ref[...]
        dx = dres_ref[...] + r * t - xv * (r * r * r) * jnp.mean(xv * t, axis=-1, keepdims=True)
        dx_ref[...] = dx
        dxb_ref[...] = dx.astype(BF16)

        @pl.when(pl.program_id(0) == 0)
        def _():
            dg_ref[...] = jnp.zeros_like(dg_ref)

        dg_ref[...] += jnp.sum(dhv * xv * r, axis=0, keepdims=True)

    row = pl.BlockSpec((tm, Dm), lambda i: (i, 0))
    vec = pl.BlockSpec((1, Dm), lambda i: (0, 0))
    return _pc(
        body, name=name, grid=(S // tm,),
        in_specs=[row, vec, row, row], out_specs=[row, row, vec],
        out_shape=[jax.ShapeDtypeStruct((S, Dm), F32), jax.ShapeDtypeStruct((S, Dm), BF16),
                   jax.ShapeDtypeStruct((1, Dm), F32)],
        compiler_params=_params(("arbitrary",)),
    )(x, g.reshape(1, Dm), dh, dres)


def _hn_fwd(x, gain, G):
    r = lax.rsqrt(_gmean(x * x, G) + EPS)
    return x * r * gain


def _hn_bwd(x, gain, dy, G):
    r = lax.rsqrt(_gmean(x * x, G) + EPS)
    t = dy * gain
    dx = r * t - x * (r * r * r) * _gmean(x * t, G)
    return dx, jnp.sum(dy * x * r, axis=0, keepdims=True)


def _fold_halves(v):
    return v + _roll(v, HEAD)


def _even_pre_fwd(proj, tab, qg, kg):
    S = proj.shape[0]
    tm = 256

    def body(p_ref, tab_ref, qg_ref, kg_ref, g_ref, rq_ref, rk_ref, rv_ref, dq_ref, dk_ref, dv_ref):
        Ar, Br, Cr = _tab(tab_ref, 0)
        Ap, Bp, Cp = _tab(tab_ref, 1)
        G = g_ref[...]
        for c in range(2):
            sl = slice(c * 128, (c + 1) * 128)
            rq_ref[:, sl] = _rope(p_ref[:, c * 128:(c + 1) * 128], Ar, Br, Cr, 32).astype(BF16)
            rk_ref[:, sl] = (_rope(p_ref[:, 256 + c * 128:256 + (c + 1) * 128], Ar, Br, Cr, 32) * 0.125).astype(BF16)
        rv_ref[...] = p_ref[:, 512:1024].astype(BF16)
        for c in range(4):
            sl = slice(c * 128, (c + 1) * 128)
            q = _hn_fwd(p_ref[:, 1536 + c * 128:1536 + (c + 1) * 128], qg_ref[...], G)
            dq_ref[:, sl] = _rope(q, Ap, Bp, Cp, 8).astype(BF16)
            k = _hn_fwd(p_ref[:, 2048 + c * 128:2048 + (c + 1) * 128], kg_ref[...], G)
            dk_ref[:, sl] = _rope(k, Ap, Bp, Cp, 8).astype(BF16)
        dv_ref[...] = p_ref[:, 2560:3072].astype(BF16)

    row = lambda w: pl.BlockSpec((tm, w), lambda i: (i, 0))
    vec = pl.BlockSpec((1, LANES), lambda i: (0, 0))
    return _pc(
        body, name="even_pre_fwd", grid=(S // tm,),
        in_specs=[row(3072), row(768), vec, vec, pl.BlockSpec((LANES, LANES), lambda i: (0, 0))],
        out_specs=[row(256), row(256), row(512), row(512), row(512), row(512)],
        out_shape=[jax.ShapeDtypeStruct((S, w), BF16) for w in (256, 256, 512, 512, 512, 512)],
        compiler_params=_params(("parallel",)),
    )(proj, tab, qg, kg, _group_matrix())


def _even_pre_bwd(proj, tab, qg, kg, drq, drk, drv, drg, dqs, dks, dvs):
    S = proj.shape[0]
    tm = 256
    npat = len(dqs)

    def body(p_ref, tab_ref, qg_ref, kg_ref, g_ref, drq_ref, drk_ref, drv_ref, drg_ref, *rest):
        dq_refs, dk_refs, dv_refs = rest[:npat], rest[npat:2 * npat], rest[2 * npat:3 * npat]
        dp_ref, dqg_ref, dkg_ref = rest[3 * npat:]
        Ar, Br, Cr = _tab(tab_ref, 0)
        Ap, Bp, Cp = _tab(tab_ref, 1)
        G = g_ref[...]
        for c in range(2):
            sl = slice(c * 128, (c + 1) * 128)
            dp_ref[:, c * 128:(c + 1) * 128] = _rope_t(drq_ref[:, sl], Ar, Br, Cr, 32).astype(BF16)
            dp_ref[:, 256 + c * 128:256 + (c + 1) * 128] = _rope_t(drk_ref[:, sl] * 0.125, Ar, Br, Cr, 32).astype(BF16)
        dp_ref[:, 512:1024] = drv_ref[...].astype(BF16)
        dp_ref[:, 1024:1536] = drg_ref[...].astype(BF16)
        accq = jnp.zeros((1, LANES), F32)
        acck = jnp.zeros((1, LANES), F32)
        for c in range(4):
            sl = slice(c * 128, (c + 1) * 128)
            g = dq_refs[0][:, sl]
            for r in dq_refs[1:]:
                g = g + r[:, sl]
            dx, dg = _hn_bwd(p_ref[:, 1536 + c * 128:1536 + (c + 1) * 128], qg_ref[...], _rope_t(g, Ap, Bp, Cp, 8), G)
            dp_ref[:, 1536 + c * 128:1536 + (c + 1) * 128] = dx.astype(BF16)
            accq = accq + dg
            g = dk_refs[0][:, sl]
            for r in dk_refs[1:]:
                g = g + r[:, sl]
            dx, dg = _hn_bwd(p_ref[:, 2048 + c * 128:2048 + (c + 1) * 128], kg_ref[...], _rope_t(g, Ap, Bp, Cp, 8), G)
            dp_ref[:, 2048 + c * 128:2048 + (c + 1) * 128] = dx.astype(BF16)
            acck = acck + dg
        g = dv_refs[0][...]
        for r in dv_refs[1:]:
            g = g + r[...]
        dp_ref[:, 2560:3072] = g.astype(BF16)

        @pl.when(pl.program_id(0) == 0)
        def _():
            dqg_ref[...] = jnp.zeros_like(dqg_ref)
            dkg_ref[...] = jnp.zeros_like(dkg_ref)

        dqg_ref[...] += _fold_halves(accq)
        dkg_ref[...] += _fold_halves(acck)

    row = lambda w: pl.BlockSpec((tm, w), lambda i: (i, 0))
    vec = pl.BlockSpec((1, LANES), lambda i: (0, 0))
    return _pc(
        body, name="even_pre_bwd", grid=(S // tm,),
        in_specs=[row(3072), row(768), vec, vec, pl.BlockSpec((LANES, LANES), lambda i: (0, 0)),
                  row(256), row(256), row(512), row(512)] + [row(512)] * (3 * npat),
        out_specs=[row(3072), vec, vec],
        out_shape=[jax.ShapeDtypeStruct((S, 3072), BF16), jax.ShapeDtypeStruct((1, LANES), F32),
                   jax.ShapeDtypeStruct((1, LANES), F32)],
        compiler_params=_params(("arbitrary",)),
    )(proj, tab, qg, kg, _group_matrix(), drq, drk, drv, drg, *dqs, *dks, *dvs)


def _ret_consts(pair, half):
    lg = jnp.where(pair == 0, _LOG_GAMMA[half], _LOG_GAMMA[2 + half]).astype(F32)
    i = lax.broadcasted_iota(jnp.int32, (BLK, BLK), 0)
    j = lax.broadcasted_iota(jnp.int32, (BLK, BLK), 1)
    diff = (i - j).astype(F32)
    decay = jnp.where(diff >= 0, jnp.exp(lg * jnp.maximum(diff, 0.0)), 0.0)
    t = lax.broadcasted_iota(jnp.int32, (BLK, 1), 0).astype(F32)
    xi = jnp.exp(lg * (t + 1.0))
    zeta = jnp.exp(lg * (BLK - 1.0 - t))
    cd = jnp.exp(jnp.full((1, 1), BLK, F32) * lg)
    return decay, xi, zeta, cd


def _ret_fwd(rq, rk, rv):
    S = rq.shape[0]
    nc = S // BLK

    def body(q_ref, k_ref, v_ref, o_ref, st_ref, R):
        p, n = pl.program_id(0), pl.program_id(1)

        @pl.when(n == 0)
        def _():
            R[...] = jnp.zeros_like(R)

        q2, k2 = q_ref[...], k_ref[...]
        for half in range(2):
            decay, xi, zeta, cd = _ret_consts(p, half)
            m = _head_mask((BLK, LANES), half)
            qm = jnp.where(m, q2, jnp.zeros_like(q2))
            km = jnp.where(m, k2, jnp.zeros_like(k2))
            v = v_ref[:, half * 128:(half + 1) * 128]
            Rb = R[half].astype(BF16)
            st_ref[half] = Rb
            sc = lax.dot_general(qm, k2, (((1,), (1,)), ((), ())), preferred_element_type=F32) * decay
            o = jnp.dot(sc.astype(BF16), v, preferred_element_type=F32)
            o = o + jnp.dot(qm, Rb, preferred_element_type=F32) * xi
            o_ref[:, half * 128:(half + 1) * 128] = o
            kz = (km.astype(F32) * zeta).astype(BF16)
            R[half] = R[half] * cd + lax.dot_general(kz, v, (((0,), (0,)), ((), ())), preferred_element_type=F32)

    return _pc(
        body, name="ret_fwd", grid=(2, nc),
        in_specs=[pl.BlockSpec((BLK, 128), lambda p, n: (n, p)), pl.BlockSpec((BLK, 128), lambda p, n: (n, p)),
                  pl.BlockSpec((BLK, 256), lambda p, n: (n, p))],
        out_specs=[pl.BlockSpec((BLK, 256), lambda p, n: (n, p)),
                   pl.BlockSpec((None, None, 2, 128, 128), lambda p, n: (p, n, 0, 0, 0))],
        out_shape=[jax.ShapeDtypeStruct((S, 512), F32), jax.ShapeDtypeStruct((2, nc, 2, 128, 128), BF16)],
        scratch_shapes=[pltpu.VMEM((2, 128, 128), F32)],
        compiler_params=_params(("parallel", "arbitrary")),
    )(rq, rk, rv)


def _ret_bwd(rq, rk, rv, states, do):
    S = rq.shape[0]
    nc = S // BLK

    def body(q_ref, k_ref, v_ref, st_ref, do_ref, dq_ref, dk_ref, dv_ref, U):
        p, n = pl.program_id(0), pl.program_id(1)

        @pl.when(n == 0)
        def _():
            U[...] = jnp.zeros_like(U)

        q2, k2 = q_ref[...], k_ref[...]
        dq_acc = jnp.zeros((BLK, LANES), F32)
        dk_acc = jnp.zeros((BLK, LANES), F32)
        for half in range(2):
            decay, xi, zeta, cd = _ret_consts(p, half)
            m = _head_mask((BLK, LANES), half)
            qm = jnp.where(m, q2, jnp.zeros_like(q2))
            km = jnp.where(m, k2, jnp.zeros_like(k2))
            v = v_ref[:, half * 128:(half + 1) * 128]
            dob = do_ref[:, half * 128:(half + 1) * 128].astype(BF16)
            Rb = st_ref[half]
            Ub = U[half].astype(BF16)
            nt = (((1,), (1,)), ((), ()))
            tn = (((0,), (0,)), ((), ()))
            dsc = (lax.dot_general(dob, v, nt, preferred_element_type=F32) * decay).astype(BF16)
            xdo = (dob.astype(F32) * xi).astype(BF16)
            dq_acc += jnp.dot(dsc, km, preferred_element_type=F32) + lax.dot_general(xdo, Rb, nt, preferred_element_type=F32)
            dk_acc += lax.dot_general(dsc, qm, tn, preferred_element_type=F32) \
                + lax.dot_general(v, Ub, nt, preferred_element_type=F32) * zeta
            sc = (lax.dot_general(qm, k2, nt, preferred_element_type=F32) * decay).astype(BF16)
            kz = (km.astype(F32) * zeta).astype(BF16)
            dv_ref[:, half * 128:(half + 1) * 128] = lax.dot_general(sc, dob, tn, preferred_element_type=F32) \
                + jnp.dot(kz, Ub, preferred_element_type=F32)
            U[half] = U[half] * cd + lax.dot_general(qm, xdo, tn, preferred_element_type=F32)
        dq_ref[...] = dq_acc
        dk_ref[...] = dk_acc

    rev = lambda w: pl.BlockSpec((BLK, w), lambda p, n: (nc - 1 - n, p))
    return _pc(
        body, name="ret_bwd", grid=(2, nc),
        in_specs=[rev(128), rev(128), rev(256),
                  pl.BlockSpec((None, None, 2, 128, 128), lambda p, n: (p, nc - 1 - n, 0, 0, 0)), rev(256)],
        out_specs=[rev(128), rev(128), rev(256)],
        out_shape=[jax.ShapeDtypeStruct((S, 256), F32), jax.ShapeDtypeStruct((S, 256), F32),
                   jax.ShapeDtypeStruct((S, 512), F32)],
        scratch_shapes=[pltpu.VMEM((2, 128, 128), F32)],
        compiler_params=_params(("parallel", "arbitrary")),
    )(rq, rk, rv, states, do)


def _col_of(b, m):
    return jnp.max(jnp.where(m, b, -jnp.inf), axis=1, keepdims=True)


def _attn_fwd(q, k, v, *, nq, max_dist, name, sinks=None, want_bf16=False):
    L, Ck = k.shape
    nb, ncol = L // BLK, Ck // LANES
    scale = HEAD ** -0.5
    has_sink = sinks is not None

    def body(*refs):
        q_ref, kp_ref, kc_ref, vp_ref, vc_ref = refs[:5]
        sk_ref = refs[5] if has_sink else None
        outs = refs[5 + has_sink:]
        n = pl.program_id(1)
        kcat = jnp.concatenate([kp_ref[...], kc_ref[...]], axis=0)
        vcat = jnp.concatenate([vp_ref[...], vc_ref[...]], axis=0)
        r = lax.broadcasted_iota(jnp.int32, (BLK, 2 * BLK), 0)
        c = lax.broadcasted_iota(jnp.int32, (BLK, 2 * BLK), 1)
        dist = r + BLK - c
        valid = (dist >= 0) & (dist <= max_dist) & ((c >= BLK) | (n > 0))
        for i in range(nq):
            q2 = q_ref[:, i * 128:(i + 1) * 128]
            o2 = jnp.zeros((BLK, LANES), F32)
            l2 = jnp.zeros((BLK, LANES), F32)
            for half in range(2):
                m = _head_mask((BLK, LANES), half)
                qm = jnp.where(m, q2, jnp.zeros_like(q2))
                s = lax.dot_general(qm, kcat, (((1,), (1,)), ((), ())), preferred_element_type=F32) * scale
                s = jnp.where(valid, s, -jnp.inf)
                mx = jnp.max(s, axis=1, keepdims=True)
                if has_sink:
                    snk = _col_of(sk_ref[:, i * 128:(i + 1) * 128], _head_mask((1, LANES), half))
                    mx = jnp.maximum(mx, snk)
                pr = jnp.exp(s - mx)
                den = jnp.sum(pr, axis=1, keepdims=True)
                if has_sink:
                    den = den + jnp.exp(snk - mx)
                pv = jnp.dot(pr.astype(BF16), vcat, preferred_element_type=F32) / den
                o2 = jnp.where(m, pv, o2)
                l2 = jnp.where(m, mx + jnp.log(den), l2)
            outs[0][:, i * 128:(i + 1) * 128] = o2
            outs[1][:, i * 128:(i + 1) * 128] = l2
            if want_bf16:
                outs[2][:, i * 128:(i + 1) * 128] = o2.astype(BF16)

    qspec = pl.BlockSpec((BLK, nq * 128), lambda j, n: (n, j))
    cur = pl.BlockSpec((BLK, 128), lambda j, n: (n, j))
    prev = pl.BlockSpec((BLK, 128), lambda j, n: (jnp.maximum(n - 1, 0), j))
    in_specs = [qspec, prev, cur, prev, cur]
    args = [q, k, k, v, v]
    if has_sink:
        in_specs.append(pl.BlockSpec((1, nq * 128), lambda j, n: (0, j)))
        args.append(sinks)
    out_dts = [F32, F32] + ([BF16] if want_bf16 else [])
    return _pc(
        body, name=name, grid=(ncol, nb), in_specs=in_specs,
        out_specs=[qspec] * len(out_dts),
        out_shape=[jax.ShapeDtypeStruct(q.shape, dt) for dt in out_dts],
        compiler_params=_params(("parallel", "parallel")),
    )(*args)


def _attn_bwd(q, k, v, o, lse, do, *, nq, max_dist, name, sinks=None):
    L, Ck = k.shape
    nb, ncol = L // BLK, Ck // LANES
    scale = HEAD ** -0.5
    has_sink = sinks is not None
    nt = (((1,), (1,)), ((), ()))
    tn = (((0,), (0,)), ((), ()))

    def body(*refs):
        (qc_ref, qn_ref, kp_ref, kc_ref, vp_ref, vc_ref, oc_ref, on_ref, lc_ref, ln_ref, dc_ref, dn_ref) = refs[:12]
        sk_ref = refs[12] if has_sink else None
        outs = refs[12 + has_sink:]
        dq_ref, dk_ref, dv_ref = outs[:3]
        n = pl.program_id(1)
        kc, vc = kc_ref[...], vc_ref[...]
        kcat = jnp.concatenate([kp_ref[...], kc], axis=0)
        vcat = jnp.concatenate([vp_ref[...], vc], axis=0)
        r = lax.broadcasted_iota(jnp.int32, (BLK, 2 * BLK), 0)
        c = lax.broadcasted_iota(jnp.int32, (BLK, 2 * BLK), 1)
        dist = r + BLK - c
        valid_q = (dist >= 0) & (dist <= max_dist) & ((c >= BLK) | (n > 0))
        r2 = lax.broadcasted_iota(jnp.int32, (2 * BLK, BLK), 0)
        c2 = lax.broadcasted_iota(jnp.int32, (2 * BLK, BLK), 1)
        dist2 = r2 - c2
        valid_k = (dist2 >= 0) & (dist2 <= max_dist) & ((r2 < BLK) | (n < nb - 1))
        dk_acc = jnp.zeros((BLK, LANES), F32)
        dv_acc = jnp.zeros((BLK, LANES), F32)
        for i in range(nq):
            sl = slice(i * 128, (i + 1) * 128)
            qcur, docur = qc_ref[:, sl], dc_ref[:, sl]
            qcat = jnp.concatenate([qcur, qn_ref[:, sl]], axis=0)
            docat = jnp.concatenate([docur, dn_ref[:, sl]], axis=0)
            ocat = jnp.concatenate([oc_ref[:, sl], on_ref[:, sl]], axis=0)
            lcat = jnp.concatenate([lc_ref[:, sl], ln_ref[:, sl]], axis=0)
            dq2 = jnp.zeros((BLK, LANES), F32)
            ds2 = jnp.zeros((1, LANES), F32)
            for half in range(2):
                m1 = _head_mask((BLK, LANES), half)
                m2 = _head_mask((2 * BLK, LANES), half)
                dom = jnp.where(m2, docat, 0.0)
                delta = jnp.sum(dom * ocat, axis=1, keepdims=True)
                lcol = _col_of(lcat, m2)
                domb = dom.astype(BF16)
                qmcat = jnp.where(m2, qcat, jnp.zeros_like(qcat))
                qm = qmcat[:BLK]
                s = lax.dot_general(qm, kcat, nt, preferred_element_type=F32) * scale
                pr = jnp.where(valid_q, jnp.exp(s - lcol[:BLK]), 0.0)
                dp = lax.dot_general(domb[:BLK], vcat, nt, preferred_element_type=F32)
                ds = (pr * (dp - delta[:BLK])).astype(BF16)
                dq2 = jnp.where(m1, jnp.dot(ds, kcat, preferred_element_type=F32) * scale, dq2)
                if has_sink:
                    snk = _col_of(sk_ref[:, sl], _head_mask((1, LANES), half))
                    contrib = jnp.sum(-jnp.exp(snk - lcol[:BLK]) * delta[:BLK], axis=0, keepdims=True)
                    ds2 = jnp.where(_head_mask((1, LANES), half), contrib, ds2)
                s = lax.dot_general(qmcat, kc, nt, preferred_element_type=F32) * scale
                pr = jnp.where(valid_k, jnp.exp(s - lcol), 0.0)
                dv_acc += lax.dot_general(pr.astype(BF16), domb, tn, preferred_element_type=F32)
                dp = lax.dot_general(domb, vc, nt, preferred_element_type=F32)
                ds = (pr * (dp - delta)).astype(BF16)
                dk_acc += lax.dot_general(ds, qmcat, tn, preferred_element_type=F32) * scale
            dq_ref[:, sl] = dq2
            if has_sink:
                @pl.when(n == 0)
                def _():
                    outs[3][:, sl] = jnp.zeros((1, LANES), F32)

                outs[3][:, sl] += ds2
        dk_ref[...] = dk_acc
        dv_ref[...] = dv_acc

    qcur = pl.BlockSpec((BLK, nq * 128), lambda j, n: (n, j))
    qnext = pl.BlockSpec((BLK, nq * 128), lambda j, n: (jnp.minimum(n + 1, nb - 1), j))
    cur = pl.BlockSpec((BLK, 128), lambda j, n: (n, j))
    prev = pl.BlockSpec((BLK, 128), lambda j, n: (jnp.maximum(n - 1, 0), j))
    in_specs = [qcur, qnext, prev, cur, prev, cur, qcur, qnext, qcur, qnext, qcur, qnext]
    args = [q, q, k, k, v, v, o, o, lse, lse, do, do]
    out_specs = [qcur, cur, cur]
    out_shape = [jax.ShapeDtypeStruct(q.shape, F32), jax.ShapeDtypeStruct(k.shape, F32), jax.ShapeDtypeStruct(k.shape, F32)]
    if has_sink:
        vec = pl.BlockSpec((1, nq * 128), lambda j, n: (0, j))
        in_specs.append(vec)
        args.append(sinks)
        out_specs.append(vec)
        out_shape.append(jax.ShapeDtypeStruct((1, q.shape[1]), F32))
    return _pc(
        body, name=name, grid=(ncol, nb), in_specs=in_specs, out_specs=out_specs, out_shape=out_shape,
        compiler_params=_params(("parallel", "arbitrary")),
    )(*args)


ATT_TILE = 2048


def _rows(ref, start, n, r):
    if r == 1:
        return ref[pl.ds(start, n), :]
    return ref[pl.ds(start, n, stride=r), :]


def _set_rows(ref, start, n, r, val):
    if r == 1:
        ref[pl.ds(start, n), :] = val
    else:
        ref[pl.ds(start, n, stride=r), :] = val


def _band_geometry(S, patterns):
    rmax = max(r for _, r in patterns)
    H = BLK * rmax
    T = min(S, ATT_TILE)
    assert T % H == 0 and S % T == 0
    return H, T, S // T, T // BLK


def _band_fwd(q, k, v, *, patterns, nq, name, sinks=None, want_bf16=False):
    S, Ck = k.shape
    H, T, nt, nbt = _band_geometry(S, patterns)
    ncol = Ck // LANES
    scale = HEAD ** -0.5
    has_sink = sinks is not None
    nt_dims = (((1,), (1,)), ((), ()))

    def body(*refs):
        q_ref, kp_ref, kc_ref, vp_ref, vc_ref = refs[:5]
        sk_ref = refs[5] if has_sink else None
        n_out = 3 if want_bf16 else 2
        outs = refs[5 + has_sink:5 + has_sink + n_out]
        qf, kf, vf, M, L, A = refs[5 + has_sink + n_out:]
        t = pl.program_id(1)
        kf[0:H, :] = kp_ref[...].astype(F32)
        kf[H:H + T, :] = kc_ref[...].astype(F32)
        vf[0:H, :] = vp_ref[...].astype(F32)
        vf[H:H + T, :] = vc_ref[...].astype(F32)
        r_i = lax.broadcasted_iota(jnp.int32, (BLK, 2 * BLK), 0)
        c_i = lax.broadcasted_iota(jnp.int32, (BLK, 2 * BLK), 1)
        dist_i = r_i + BLK - c_i
        masks = [_head_mask((BLK, LANES), h) for h in range(2)]

        for i in range(nq):
            qf[...] = q_ref[:, i * 128:(i + 1) * 128].astype(F32)
            for p, (dist, r) in enumerate(patterns):
                in_band = (dist_i >= 0) & (dist_i <= dist)

                def unit(j, b, p=p, r=r, in_band=in_band):
                    q0 = j + b * (BLK * r)
                    q2 = _rows(qf, q0, BLK, r).astype(BF16)
                    kcat = _rows(kf, H + q0 - BLK * r, 2 * BLK, r).astype(BF16)
                    vcat = _rows(vf, H + q0 - BLK * r, 2 * BLK, r).astype(BF16)
                    valid = in_band & ((c_i >= BLK) | ((b > 0) | (t > 0)))
                    m2 = jnp.zeros((BLK, LANES), F32)
                    l2 = jnp.zeros((BLK, LANES), F32)
                    a2 = jnp.zeros((BLK, LANES), F32)
                    for half in range(2):
                        m = masks[half]
                        qm = jnp.where(m, q2, jnp.zeros_like(q2))
                        s = lax.dot_general(qm, kcat, nt_dims, preferred_element_type=F32) * scale
                        s = jnp.where(valid, s, -jnp.inf)
                        mx = jnp.max(s, axis=1, keepdims=True)
                        pr = jnp.exp(s - mx)
                        den = jnp.sum(pr, axis=1, keepdims=True)
                        pv = jnp.dot(pr.astype(BF16), vcat, preferred_element_type=F32)
                        m2 = jnp.where(m, mx, m2)
                        l2 = jnp.where(m, den, l2)
                        a2 = jnp.where(m, pv, a2)
                    if p > 0:
                        mo = _rows(M, q0, BLK, r)
                        mn = jnp.maximum(mo, m2)
                        wa, wb = jnp.exp(mo - mn), jnp.exp(m2 - mn)
                        l2 = wa * _rows(L, q0, BLK, r) + wb * l2
                        a2 = wa * _rows(A, q0, BLK, r) + wb * a2
                        m2 = mn
                    _set_rows(M, q0, BLK, r, m2)
                    _set_rows(L, q0, BLK, r, l2)
                    _set_rows(A, q0, BLK, r, a2)

                for u in range(nbt):
                    unit(u % r, u // r)
            sl = slice(i * 128, (i + 1) * 128)
            mm, ll, aa = M[...], L[...], A[...]
            if has_sink:
                snk = sk_ref[:, sl]
                mn = jnp.maximum(mm, snk)
                w = jnp.exp(mm - mn)
                ll = ll * w + jnp.exp(snk - mn)
                aa = aa * w
                mm = mn
            o = aa / ll
            outs[0][:, sl] = o
            outs[1][:, sl] = mm + jnp.log(ll)
            if want_bf16:
                outs[2][:, sl] = o.astype(BF16)

    th = T // H
    qspec = pl.BlockSpec((T, nq * 128), lambda j, t: (t, j))
    cur = pl.BlockSpec((T, 128), lambda j, t: (t, j))
    prev = pl.BlockSpec((H, 128), lambda j, t: (jnp.maximum(t * th - 1, 0), j))
    in_specs = [qspec, prev, cur, prev, cur]
    args = [q, k, k, v, v]
    if has_sink:
        in_specs.append(pl.BlockSpec((1, nq * 128), lambda j, t: (0, j)))
        args.append(sinks)
    out_dts = [F32, F32] + ([BF16] if want_bf16 else [])
    return _pc(
        body, name=name, grid=(ncol, nt), in_specs=in_specs,
        out_specs=[qspec] * len(out_dts),
        out_shape=[jax.ShapeDtypeStruct(q.shape, dt) for dt in out_dts],
        scratch_shapes=[pltpu.VMEM((T, LANES), F32), pltpu.VMEM((H + T, LANES), F32), pltpu.VMEM((H + T, LANES), F32),
                        pltpu.VMEM((T, LANES), F32), pltpu.VMEM((T, LANES), F32), pltpu.VMEM((T, LANES), F32)],
        compiler_params=_params(("parallel", "parallel")),
    )(*args)


def _band_bwd(q, k, v, lse, delta, do, *, patterns, nq, name, sinks=None):
    S, Ck = k.shape
    H, T, nt, nbt = _band_geometry(S, patterns)
    ncol = Ck // LANES
    scale = HEAD ** -0.5
    has_sink = sinks is not None
    nt_dims = (((1,), (1,)), ((), ()))
    tn_dims = (((0,), (0,)), ((), ()))

    def body(*refs):
        (qc_ref, qn_ref, kp_ref, kc_ref, vp_ref, vc_ref, lc_ref, ln_ref, ec_ref, en_ref, dc_ref, dn_ref) = refs[:12]
        sk_ref = refs[12] if has_sink else None
        n_out = 4 if has_sink else 3
        outs = refs[12 + has_sink:12 + has_sink + n_out]
        dq_ref, dk_ref, dv_ref = outs[:3]
        qf, kf, vf, lf, ef, df = refs[12 + has_sink + n_out:]
        t = pl.program_id(1)
        kf[0:H, :] = kp_ref[...].astype(F32)
        kf[H:H + T, :] = kc_ref[...].astype(F32)
        vf[0:H, :] = vp_ref[...].astype(F32)
        vf[H:H + T, :] = vc_ref[...].astype(F32)
        dk_ref[...] = jnp.zeros_like(dk_ref)
        dv_ref[...] = jnp.zeros_like(dv_ref)
        r_i = lax.broadcasted_iota(jnp.int32, (BLK, 2 * BLK), 0)
        c_i = lax.broadcasted_iota(jnp.int32, (BLK, 2 * BLK), 1)
        dist_q = r_i + BLK - c_i
        r_k = lax.broadcasted_iota(jnp.int32, (2 * BLK, BLK), 0)
        c_k = lax.broadcasted_iota(jnp.int32, (2 * BLK, BLK), 1)
        dist_k = r_k - c_k
        m1 = [_head_mask((BLK, LANES), h) for h in range(2)]
        m2 = [_head_mask((2 * BLK, LANES), h) for h in range(2)]

        for i in range(nq):
            sl = slice(i * 128, (i + 1) * 128)
            for buf, c_ref, n_ref in ((qf, qc_ref, qn_ref), (lf, lc_ref, ln_ref), (ef, ec_ref, en_ref), (df, dc_ref, dn_ref)):
                buf[0:T, :] = c_ref[:, sl].astype(F32)
                buf[T:T + H, :] = n_ref[:, sl].astype(F32)
            if has_sink:
                @pl.when(t == 0)
                def _():
                    outs[3][:, sl] = jnp.zeros((1, LANES), F32)

                outs[3][:, sl] += jnp.sum(-jnp.exp(sk_ref[:, sl] - lc_ref[:, sl]) * ec_ref[:, sl], axis=0, keepdims=True)
            for p, (dist, r) in enumerate(patterns):
                band_q = (dist_q >= 0) & (dist_q <= dist)
                band_k = (dist_k >= 0) & (dist_k <= dist)

                def unit(j, b, p=p, r=r, band_q=band_q, band_k=band_k):
                    q0 = j + b * (BLK * r)
                    qcat = _rows(qf, q0, 2 * BLK, r).astype(BF16)
                    docat = _rows(df, q0, 2 * BLK, r)
                    lcat = _rows(lf, q0, 2 * BLK, r)
                    ecat = _rows(ef, q0, 2 * BLK, r)
                    kcat = _rows(kf, H + q0 - BLK * r, 2 * BLK, r).astype(BF16)
                    vcat = _rows(vf, H + q0 - BLK * r, 2 * BLK, r).astype(BF16)
                    kc, vc = kcat[BLK:], vcat[BLK:]
                    valid_q = band_q & ((c_i >= BLK) | ((b > 0) | (t > 0)))
                    valid_k = band_k & ((r_k < BLK) | ((b < nbt // r - 1) | (t < nt - 1)))
                    dq2 = jnp.zeros((BLK, LANES), F32)
                    dk2 = jnp.zeros((BLK, LANES), F32)
                    dv2 = jnp.zeros((BLK, LANES), F32)
                    for half in range(2):
                        lcol = _col_of(lcat, m2[half])
                        ecol = _col_of(ecat, m2[half])
                        domb = jnp.where(m2[half], docat, 0.0).astype(BF16)
                        qmcat = jnp.where(m2[half], qcat, jnp.zeros_like(qcat))
                        s = lax.dot_general(qmcat[:BLK], kcat, nt_dims, preferred_element_type=F32) * scale
                        pr = jnp.where(valid_q, jnp.exp(s - lcol[:BLK]), 0.0)
                        dp = lax.dot_general(domb[:BLK], vcat, nt_dims, preferred_element_type=F32)
                        ds = (pr * (dp - ecol[:BLK])).astype(BF16)
                        dq2 = jnp.where(m1[half], jnp.dot(ds, kcat, preferred_element_type=F32) * scale, dq2)
                        s = lax.dot_general(qmcat, kc, nt_dims, preferred_element_type=F32) * scale
                        pr = jnp.where(valid_k, jnp.exp(s - lcol), 0.0)
                        dv2 += lax.dot_general(pr.astype(BF16), domb, tn_dims, preferred_element_type=F32)
                        dp = lax.dot_general(domb, vc, nt_dims, preferred_element_type=F32)
                        ds = (pr * (dp - ecol)).astype(BF16)
                        dk2 += lax.dot_general(ds, qmcat, tn_dims, preferred_element_type=F32) * scale
                    if p > 0:
                        dq2 = dq2 + _rows(dq_ref.at[:, sl], q0, BLK, r)
                    _set_rows(dq_ref.at[:, sl], q0, BLK, r, dq2)
                    _set_rows(dk_ref, q0, BLK, r, _rows(dk_ref, q0, BLK, r) + dk2)
                    _set_rows(dv_ref, q0, BLK, r, _rows(dv_ref, q0, BLK, r) + dv2)

                for u in range(nbt):
                    unit(u % r, u // r)

    th = T // H
    last = S // H - 1
    qcur = pl.BlockSpec((T, nq * 128), lambda j, t: (t, j))
    qnext = pl.BlockSpec((H, nq * 128), lambda j, t: (jnp.minimum((t + 1) * th, last), j))
    cur = pl.BlockSpec((T, 128), lambda j, t: (t, j))
    prev = pl.BlockSpec((H, 128), lambda j, t: (jnp.maximum(t * th - 1, 0), j))
    in_specs = [qcur, qnext, prev, cur, prev, cur, qcur, qnext, qcur, qnext, qcur, qnext]
    args = [q, q, k, k, v, v, lse, lse, delta, delta, do, do]
    out_specs = [qcur, cur, cur]
    out_shape = [jax.ShapeDtypeStruct(q.shape, F32), jax.ShapeDtypeStruct(k.shape, F32), jax.ShapeDtypeStruct(k.shape, F32)]
    if has_sink:
        vec = pl.BlockSpec((1, nq * 128), lambda j, t: (0, j))
        in_specs.append(vec)
        args.append(sinks)
        out_specs.append(vec)
        out_shape.append(jax.ShapeDtypeStruct((1, q.shape[1]), F32))
    big = pltpu.VMEM((T + H, LANES), F32)
    return _pc(
        body, name=name, grid=(ncol, nt), in_specs=in_specs, out_specs=out_specs, out_shape=out_shape,
        scratch_shapes=[big] * 6,
        compiler_params=_params(("parallel", "arbitrary")),
    )(*args)


def _delta(do, o, name):
    S, C = do.shape
    tm = 512

    def body(do_ref, o_ref, g_ref, e_ref):
        for c in range(C // LANES):
            sl = slice(c * 128, (c + 1) * 128)
            e_ref[:, sl] = _gmean(do_ref[:, sl] * o_ref[:, sl], g_ref[...]) * float(HEAD)

    row = pl.BlockSpec((tm, C), lambda i: (i, 0))
    return _pc(
        body, name=name, grid=(S // tm,),
        in_specs=[row, row, pl.BlockSpec((LANES, LANES), lambda i: (0, 0))], out_specs=row,
        out_shape=jax.ShapeDtypeStruct((S, C), F32),
        compiler_params=_params(("parallel",)),
    )(do, o, _group_matrix())


def _even_post_fwd(ro, proj, gn, da):
    S = ro.shape[0]
    tm = 256

    def body(ro_ref, rg_ref, gn_ref, da_ref, mix_ref):
        for c in range(4):
            sl = slice(c * 128, (c + 1) * 128)
            x = ro_ref[:, sl]
            mu = jnp.mean(x, axis=1, keepdims=True)
            xc = x - mu
            var = jnp.mean(xc * xc, axis=1, keepdims=True)
            y = xc * lax.rsqrt(var + EPS) * gn_ref[:, sl]
            z = rg_ref[:, sl]
            mix_ref[:, sl] = (z * jax.nn.sigmoid(z) * y).astype(BF16)
        mix_ref[:, 512:1024] = da_ref[...].astype(BF16)

    row = lambda w: pl.BlockSpec((tm, w), lambda i: (i, 0))
    return _pc(
        body, name="even_post_fwd", grid=(S // tm,),
        in_specs=[row(512), pl.BlockSpec((tm, 512), lambda i: (i, 2)), pl.BlockSpec((1, 512), lambda i: (0, 0)), row(512)],
        out_specs=row(1024), out_shape=jax.ShapeDtypeStruct((S, 1024), BF16),
        compiler_params=_params(("parallel",)),
    )(ro, proj, gn, da)


def _even_post_bwd(ro, proj, gn, dmixed):
    S = ro.shape[0]
    tm = 256

    def body(ro_ref, rg_ref, gn_ref, dm_ref, dro_ref, drg_ref, dgn_ref):
        @pl.when(pl.program_id(0) == 0)
        def _():
            dgn_ref[...] = jnp.zeros_like(dgn_ref)

        for c in range(4):
            sl = slice(c * 128, (c + 1) * 128)
            x = ro_ref[:, sl]
            mu = jnp.mean(x, axis=1, keepdims=True)
            xc = x - mu
            rstd = lax.rsqrt(jnp.mean(xc * xc, axis=1, keepdims=True) + EPS)
            xh = xc * rstd
            gain = gn_ref[:, sl]
            y = xh * gain
            z = rg_ref[:, sl]
            sg = jax.nn.sigmoid(z)
            dra = dm_ref[:, sl]
            drg_ref[:, sl] = dra * y * sg * (1.0 + z * (1.0 - sg))
            dy = dra * z * sg
            dgn_ref[:, sl] += jnp.sum(dy * xh, axis=0, keepdims=True)
            dxh = dy * gain
            dro_ref[:, sl] = rstd * (dxh - jnp.mean(dxh, axis=1, keepdims=True)
                                     - xh * jnp.mean(dxh * xh, axis=1, keepdims=True))

    row = lambda w: pl.BlockSpec((tm, w), lambda i: (i, 0))
    vec = pl.BlockSpec((1, 512), lambda i: (0, 0))
    return _pc(
        body, name="even_post_bwd", grid=(S // tm,),
        in_specs=[row(512), pl.BlockSpec((tm, 512), lambda i: (i, 2)), vec, row(512)],
        out_specs=[row(512), row(512), vec],
        out_shape=[jax.ShapeDtypeStruct((S, 512), F32), jax.ShapeDtypeStruct((S, 512), F32),
                   jax.ShapeDtypeStruct((1, 512), F32)],
        compiler_params=_params(("arbitrary",)),
    )(ro, proj, gn, dmixed)


def _swa_pre_fwd(proj, tab, qg, kg):
    S = proj.shape[0]
    tm = 256

    def body(p_ref, tab_ref, qg_ref, kg_ref, g_ref, q_ref, k_ref, v_ref):
        Ap, Bp, Cp = _tab(tab_ref, 1)
        G = g_ref[...]
        lo = _head_mask((tm, LANES), 0)
        for c in range(8):
            sl = slice(c * 128, (c + 1) * 128)
            q_ref[:, sl] = _rope(_hn_fwd(p_ref[:, sl], qg_ref[...], G), Ap, Bp, Cp, 8).astype(BF16)
        for c in range(2):
            kn = _rope(_hn_fwd(p_ref[:, 1024 + c * 128:1024 + (c + 1) * 128], kg_ref[...], G), Ap, Bp, Cp, 8)
            vv = p_ref[:, 1280 + c * 128:1280 + (c + 1) * 128]
            for t, ref in ((kn, k_ref), (vv, v_ref)):
                sw = _roll(t, HEAD)
                ref[:, (2 * c) * 128:(2 * c + 1) * 128] = jnp.where(lo, t, sw).astype(BF16)
                ref[:, (2 * c + 1) * 128:(2 * c + 2) * 128] = jnp.where(lo, sw, t).astype(BF16)

    row = lambda w: pl.BlockSpec((tm, w), lambda i: (i, 0))
    vec = pl.BlockSpec((1, LANES), lambda i: (0, 0))
    return _pc(
        body, name="swa_pre_fwd", grid=(S // tm,),
        in_specs=[row(1536), row(768), vec, vec, pl.BlockSpec((LANES, LANES), lambda i: (0, 0))],
        out_specs=[row(1024), row(512), row(512)],
        out_shape=[jax.ShapeDtypeStruct((S, w), BF16) for w in (1024, 512, 512)],
        compiler_params=_params(("parallel",)),
    )(proj, tab, qg, kg, _group_matrix())


def _swa_pre_bwd(proj, tab, qg, kg, dq, dk, dv):
    S = proj.shape[0]
    tm = 256

    def body(p_ref, tab_ref, qg_ref, kg_ref, g_ref, dq_ref, dk_ref, dv_ref, dp_ref, db_ref, dqg_ref, dkg_ref):
        Ap, Bp, Cp = _tab(tab_ref, 1)
        G = g_ref[...]
        lo = _head_mask((tm, LANES), 0)

        @pl.when(pl.program_id(0) == 0)
        def _():
            db_ref[...] = jnp.zeros_like(db_ref)
            dqg_ref[...] = jnp.zeros_like(dqg_ref)
            dkg_ref[...] = jnp.zeros_like(dkg_ref)

        accq = jnp.zeros((1, LANES), F32)
        acck = jnp.zeros((1, LANES), F32)
        for c in range(8):
            sl = slice(c * 128, (c + 1) * 128)
            dx, dg = _hn_bwd(p_ref[:, sl], qg_ref[...], _rope_t(dq_ref[:, sl], Ap, Bp, Cp, 8), G)
            dp_ref[:, sl] = dx.astype(BF16)
            db_ref[:, sl] += jnp.sum(dx, axis=0, keepdims=True)
            accq = accq + dg
        for c in range(2):
            folded = []
            for ref in (dk_ref, dv_ref):
                a = ref[:, (2 * c) * 128:(2 * c + 1) * 128]
                b = ref[:, (2 * c + 1) * 128:(2 * c + 2) * 128]
                folded.append(jnp.where(lo, a + _roll(a, HEAD), b + _roll(b, HEAD)))
            ks = slice(1024 + c * 128, 1024 + (c + 1) * 128)
            dx, dg = _hn_bwd(p_ref[:, ks], kg_ref[...], _rope_t(folded[0], Ap, Bp, Cp, 8), G)
            dp_ref[:, ks] = dx.astype(BF16)
            db_ref[:, ks] += jnp.sum(dx, axis=0, keepdims=True)
            acck = acck + dg
            vs = slice(1280 + c * 128, 1280 + (c + 1) * 128)
            dp_ref[:, vs] = folded[1].astype(BF16)
            db_ref[:, vs] += jnp.sum(folded[1], axis=0, keepdims=True)
        dqg_ref[...] += _fold_halves(accq)
        dkg_ref[...] += _fold_halves(acck)

    row = lambda w: pl.BlockSpec((tm, w), lambda i: (i, 0))
    vec = pl.BlockSpec((1, LANES), lambda i: (0, 0))
    return _pc(
        body, name="swa_pre_bwd", grid=(S // tm,),
        in_specs=[row(1536), row(768), vec, vec, pl.BlockSpec((LANES, LANES), lambda i: (0, 0)),
                  row(1024), row(512), row(512)],
        out_specs=[row(1536), pl.BlockSpec((1, 1536), lambda i: (0, 0)), vec, vec],
        out_shape=[jax.ShapeDtypeStruct((S, 1536), BF16), jax.ShapeDtypeStruct((1, 1536), F32),
                   jax.ShapeDtypeStruct((1, LANES), F32), jax.ShapeDtypeStruct((1, LANES), F32)],
        compiler_params=_params(("arbitrary",)),
    )(proj, tab, qg, kg, _group_matrix(), dq, dk, dv)


def _loss_head(y, target):
    S, Dm = y.shape
    tm = 512

    def body(y_ref, t_ref, l_ref, dy_ref, dyb_ref):
        @pl.when(pl.program_id(0) == 0)
        def _():
            l_ref[...] = jnp.zeros_like(l_ref)

        e = y_ref[...] - t_ref[...]
        dy = e * (1.0 / Dm)
        dy_ref[...] = dy
        dyb_ref[...] = dy.astype(BF16)
        row = jnp.sum(e * e, axis=1, keepdims=True) * (0.5 / Dm)
        l_ref[...] += jnp.sum(row, axis=0, keepdims=True)

    row = pl.BlockSpec((tm, Dm), lambda i: (i, 0))
    return _pc(
        body, name="loss_head", grid=(S // tm,), in_specs=[row, row],
        out_specs=[pl.BlockSpec((1, LANES), lambda i: (0, 0)), row, row],
        out_shape=[jax.ShapeDtypeStruct((1, LANES), F32), jax.ShapeDtypeStruct((S, Dm), F32),
                   jax.ShapeDtypeStruct((S, Dm), BF16)],
        compiler_params=_params(("arbitrary",)),
    )(y, target)


def _relu2_of(u):
    r = jnp.maximum(u.astype(F32), 0.0)
    return r * r


def _drelu2(acc, u):
    return (acc * 2.0 * jnp.maximum(u.astype(F32), 0.0),)


def _add(acc, res):
    return (acc + res,)


_T = dict(tm=1024, tn=1024, tk=1024)


def _mlp_fwd(x, g, w_up, w_dn, tag):
    h = _rms_fwd(x, g, f"rms_mlp_fwd{tag}")
    u = _matmul(h, w_up, dims="nn", **_T, outs=[BF16], b_cs=True, name=f"mlp_up{tag}")
    x_out = _matmul(u, w_dn, dims="nn", **_T, outs=[F32], epilogue=_add, extras=[(x, "mn")], a_pro=_relu2_of,
                    name=f"mlp_down{tag}")
    return x_out, (h, u)


def _mlp_bwd(x, g, w_up, w_dn, saved, dy, dyb, tag):
    h, u = saved
    du = _matmul(dyb, w_dn, dims="nt", **_T, outs=[BF16], epilogue=_drelu2, extras=[(u, "mn")], name=f"mlp_du{tag}")
    dw_dn = _matmul(u, dyb, dims="tn", **_T, outs=[F32], a_pro=_relu2_of, name=f"mlp_dwdown{tag}")
    dw_up = _matmul(h, du, dims="tn", **_T, outs=[F32], o_cs=N_CHIPS, name=f"mlp_dwup{tag}")
    dh = _matmul(du, w_up, dims="nt", **_T, outs=[F32], b_cs=True, name=f"mlp_dh{tag}")
    dx, dxb, dg = _rms_bwd(x, g, dh, dy, f"rms_mlp_bwd{tag}")
    return dx, dxb, dg, dw_up, dw_dn


def _pattern_view(t, r):
    S, C = t.shape
    return t.reshape(S // r, r * C)


def _local_step(x, pos_col, target, W, P):
    S = x.shape[0]
    tab = _tables(pos_col)
    tile2 = lambda g: jnp.tile(g.reshape(1, HEAD), (1, 2))
    dqg, dkg = tile2(P["dil_q_gain"]), tile2(P["dil_k_gain"])
    sqg, skg = tile2(P["swa_q_gain"]), tile2(P["swa_k_gain"])
    gn = P["ret_gn_gain"].reshape(1, 512)
    sink_b = jnp.repeat(P["swa_sinks"].reshape(16), HEAD).reshape(1, 1024)

    h0 = _rms_fwd(x, P["norm_mix"][0], "rms_mix_fwd0")
    proj = _matmul(h0, W["hyb_w_in"], dims="nn", tm=1024, tn=768, tk=1024, outs=[F32], b_cs=True, name="hyb_in")
    rq, rk, rv, dq, dk, dv = _even_pre_fwd(proj, tab, dqg, dkg)
    ro, states = _ret_fwd(rq, rk, rv)
    dil = [(w // r, r) for w, r in DIL_PATTERNS]
    da, dlse = _band_fwd(dq, dk, dv, patterns=dil, nq=1, name="dil_fwd")
    mixed = _even_post_fwd(ro, proj, gn, da)
    x1 = _matmul(mixed, W["hyb_w_out"], dims="nn", **_T, outs=[F32], epilogue=_add, extras=[(x, "mn")], name="hyb_out")
    x2, mlp0 = _mlp_fwd(x1, P["norm_mlp"][0], W["mlp_w_up"][0], W["mlp_w_down"][0], "0")

    h2 = _rms_fwd(x2, P["norm_mix"][1], "rms_mix_fwd1")
    proj2 = _matmul(h2, W["swa_w_qkv"], dims="nn", tm=1024, tn=384, tk=1024, outs=[F32], b_cs=True,
                    epilogue=_add, extras=[(P["swa_b_qkv_full"].reshape(1, 1536), "n")], name="swa_qkv")
    sq, sk, sv = _swa_pre_fwd(proj2, tab, sqg, skg)
    swa = [(SWA_DIST, 1)]
    so, slse, so_b = _band_fwd(sq, sk, sv, patterns=swa, nq=2, name="swa_fwd", sinks=sink_b, want_bf16=True)
    x3 = _matmul(so_b, W["swa_w_out"], dims="nn", **_T, outs=[F32], epilogue=_add, extras=[(x2, "mn")], name="swa_out")
    y, mlp1 = _mlp_fwd(x3, P["norm_mlp"][1], W["mlp_w_up"][1], W["mlp_w_down"][1], "1")
    loss, dy, dyb = _loss_head(y, target)

    gw, gp = {}, {}
    dx3, dx3b, dg_mlp1, gw["mlp_w_up1"], gw["mlp_w_down1"] = _mlp_bwd(x3, P["norm_mlp"][1], W["mlp_w_up"][1],
                                                                       W["mlp_w_down"][1], mlp1, dy, dyb, "1")
    gw["swa_w_out"] = _matmul(so_b, dx3b, dims="tn", **_T, outs=[F32], name="swa_dwout")
    dso = _matmul(dx3b, W["swa_w_out"], dims="nt", **_T, outs=[F32], name="swa_do")
    dsq, dsk, dsv, dsink = _band_bwd(sq, sk, sv, slse, _delta(dso, so, "swa_delta"), dso, patterns=swa, nq=2,
                                     name="swa_bwd", sinks=sink_b)
    dproj2, gp["swa_b_qkv"], gp["swa_q_gain"], gp["swa_k_gain"] = _swa_pre_bwd(proj2, tab, sqg, skg, dsq, dsk, dsv)
    gp["swa_sinks"] = dsink
    gw["swa_w_qkv"] = _matmul(h2, dproj2, dims="tn", tm=1024, tn=384, tk=1024, outs=[F32], o_cs=N_CHIPS, name="swa_dwqkv")
    dh2 = _matmul(dproj2, W["swa_w_qkv"], dims="nt", tm=1024, tn=1024, tk=384, outs=[F32], b_cs=True, name="swa_dh")
    dx2, dx2b, dg_mix1 = _rms_bwd(x2, P["norm_mix"][1], dh2, dx3, "rms_mix_bwd1")

    dx1, dx1b, dg_mlp0, gw["mlp_w_up0"], gw["mlp_w_down0"] = _mlp_bwd(x1, P["norm_mlp"][0], W["mlp_w_up"][0],
                                                                       W["mlp_w_down"][0], mlp0, dx2, dx2b, "0")
    gw["hyb_w_out"] = _matmul(mixed, dx1b, dims="tn", **_T, outs=[F32], name="hyb_dwout")
    dmixed = _matmul(dx1b, W["hyb_w_out"], dims="nt", **_T, outs=[F32], name="hyb_dmixed")
    dro, drg, gp["ret_gn_gain"] = _even_post_bwd(ro, proj, gn, dmixed)
    drq, drk, drv = _ret_bwd(rq, rk, rv, states, dro)
    dda = dmixed[:, 512:]
    ddq, ddk, ddv = _band_bwd(dq, dk, dv, dlse, _delta(dda, da, "dil_delta"), dda, patterns=dil, nq=1, name="dil_bwd")
    dproj, gp["dil_q_gain"], gp["dil_k_gain"] = _even_pre_bwd(proj, tab, dqg, dkg, drq, drk, drv, drg, [ddq], [ddk], [ddv])
    gw["hyb_w_in"] = _matmul(h0, dproj, dims="tn", tm=1024, tn=768, tk=1024, outs=[F32], o_cs=N_CHIPS, name="hyb_dwin")
    dh0 = _matmul(dproj, W["hyb_w_in"], dims="nt", tm=1024, tn=1024, tk=768, outs=[F32], b_cs=True, name="hyb_dh")
    grad_x, _, dg_mix0 = _rms_bwd(x, P["norm_mix"][0], dh0, dx1, "rms_mix_bwd0")
    gp["norm_mix"] = jnp.concatenate([dg_mix0, dg_mix1], axis=0)
    gp["norm_mlp"] = jnp.concatenate([dg_mlp0, dg_mlp1], axis=0)
    return loss, grad_x, gw, gp


HBM = pl.BlockSpec(memory_space=pltpu.HBM)


def _place():
    x, y, c = lax.axis_index("x"), lax.axis_index("y"), lax.axis_index("c")
    chips = [(1 - x, y), (x, 1 - y), (1 - x, 1 - y)]
    return x, y, c, chips


def _allgather_shards(buf):
    _, R, Wd = buf.shape
    Rh = R // 2

    def body(b_ref, out_ref, send_sems, recv_sems):
        x, y, c, chips = _place()
        sibling = (x, y, 1 - c)

        def copy(k, chip, core, to):
            block = b_ref.at[2 * chip[0] + chip[1], pl.ds(core * Rh, Rh), :]
            return pltpu.make_async_remote_copy(
                src_ref=block, dst_ref=block, send_sem=send_sems.at[k], recv_sem=recv_sems.at[k],
                device_id=to, device_id_type=MESH)

        first = [copy(k, (x, y), c, (*chip, c)) for k, chip in enumerate(chips)]
        for cp in first:
            cp.start()
        passed = [copy(3 + k, chip, c, sibling) for k, chip in enumerate(chips)]
        for k, chip in enumerate(chips):
            copy(k, chip, c, (x, y, c)).wait_recv()
            passed[k].start()
        for k, chip in enumerate(chips):
            copy(3 + k, chip, 1 - c, (x, y, c)).wait_recv()
        for cp in first + passed:
            cp.wait_send()

    return _pc(
        body, name="allgather_weights", in_specs=[HBM], out_specs=HBM,
        out_shape=jax.ShapeDtypeStruct(buf.shape, buf.dtype), input_output_aliases={0: 0},
        scratch_shapes=[pltpu.SemaphoreType.DMA((6,)), pltpu.SemaphoreType.DMA((6,))],
    )(buf)


def _swap_halves(ts):
    nt = len(ts)

    def body(*refs):
        t_refs, l_refs, send_sems, recv_sems = refs[:nt], refs[nt:2 * nt], refs[-2], refs[-1]
        x, y, c, _ = _place()
        cps = []
        for k in range(nt):
            rh = t_refs[k].shape[1] // 2
            cps.append(pltpu.make_async_remote_copy(
                src_ref=t_refs[k].at[:, pl.ds((1 - c) * rh, rh), :], dst_ref=l_refs[k],
                send_sem=send_sems.at[k], recv_sem=recv_sems.at[k], device_id=(x, y, 1 - c), device_id_type=MESH))
        for cp in cps:
            cp.start()
        for cp in cps:
            cp.wait()

    return _pc(
        body, name="grad_swap_halves", in_specs=[HBM] * nt, out_specs=[HBM] * nt,
        out_shape=[jax.ShapeDtypeStruct((t.shape[0], t.shape[1] // 2, t.shape[2]), F32) for t in ts],
        scratch_shapes=[pltpu.SemaphoreType.DMA((nt,)), pltpu.SemaphoreType.DMA((nt,))],
    )(*ts)


def _pair_sum(t, l, place, name):
    _, r, cols = t.shape
    rh = r // 2
    tr = min(rh, 256)
    nr = rh // tr

    def body(pl_ref, t_ref, l_ref, o_ref):
        o_ref[...] = (t_ref[...] + l_ref[...]).astype(BF16)

    return _pc(
        body, name=name,
        grid_spec=pltpu.PrefetchScalarGridSpec(
            num_scalar_prefetch=1, grid=(N_CHIPS, nr),
            in_specs=[pl.BlockSpec((None, tr, cols), lambda s, i, p: (s, p[1] * nr + i, 0)),
                      pl.BlockSpec((None, tr, cols), lambda s, i, p: (s, i, 0))],
            out_specs=pl.BlockSpec((None, tr, cols), lambda s, i, p: (s, i, 0))),
        out_shape=jax.ShapeDtypeStruct((N_CHIPS, rh, cols), BF16),
        compiler_params=_params(("parallel", "parallel")),
    )(place, t, l)


def _exchange_chips(ps):
    nt = len(ps)

    def body(*refs):
        p_refs, r_refs, send_sems, recv_sems = refs[:nt], refs[nt:2 * nt], refs[-2], refs[-1]
        x, y, c, chips = _place()
        cps = []
        for t in range(nt):
            for k, chip in enumerate(chips):
                cps.append(pltpu.make_async_remote_copy(
                    src_ref=p_refs[t].at[2 * chip[0] + chip[1]], dst_ref=r_refs[t].at[k],
                    send_sem=send_sems.at[3 * t + k], recv_sem=recv_sems.at[3 * t + k],
                    device_id=(*chip, c), device_id_type=MESH))
        for cp in cps:
            cp.start()
        for cp in cps:
            cp.wait()

    return _pc(
        body, name="grad_exchange_chips", in_specs=[HBM] * nt, out_specs=[HBM] * nt,
        out_shape=[jax.ShapeDtypeStruct((3,) + p.shape[1:], BF16) for p in ps],
        scratch_shapes=[pltpu.SemaphoreType.DMA((3 * nt,)), pltpu.SemaphoreType.DMA((3 * nt,))],
    )(*ps)


def _final_sum(t, l, rcv, place, name):
    _, r, cols = t.shape
    rh = r // 2
    tr = min(rh, 256)
    nr = rh // tr

    def body(pl_ref, t_ref, l_ref, r_ref, o_ref):
        acc = t_ref[...] + l_ref[...]
        for k in range(3):
            acc = acc + r_ref[k].astype(F32)
        o_ref[...] = acc

    return _pc(
        body, name=name,
        grid_spec=pltpu.PrefetchScalarGridSpec(
            num_scalar_prefetch=1, grid=(nr,),
            in_specs=[pl.BlockSpec((None, tr, cols), lambda i, p: (p[0], p[1] * nr + i, 0)),
                      pl.BlockSpec((None, tr, cols), lambda i, p: (p[0], i, 0)),
                      pl.BlockSpec((3, tr, cols), lambda i, p: (0, i, 0))],
            out_specs=pl.BlockSpec((tr, cols), lambda i, p: (p[1] * nr + i, 0))),
        out_shape=jax.ShapeDtypeStruct((r, cols), F32),
        compiler_params=_params(("parallel",)),
    )(place, t, l, rcv)


def _share_halves(hs):
    nt = len(hs)

    def body(*refs):
        h_refs, send_sems, recv_sems = refs[:nt], refs[-2], refs[-1]
        x, y, c, _ = _place()
        cps = []
        for k in range(nt):
            rh = h_refs[k].shape[0] // 2
            half = h_refs[k].at[pl.ds(c * rh, rh), :]
            cps.append(pltpu.make_async_remote_copy(
                src_ref=half, dst_ref=half, send_sem=send_sems.at[k], recv_sem=recv_sems.at[k],
                device_id=(x, y, 1 - c), device_id_type=MESH))
        for cp in cps:
            cp.start()
        for cp in cps:
            cp.wait()

    return _pc(
        body, name="grad_share_halves", in_specs=[HBM] * nt, out_specs=[HBM] * nt,
        out_shape=[jax.ShapeDtypeStruct(h.shape, F32) for h in hs],
        input_output_aliases={k: k for k in range(nt)},
        scratch_shapes=[pltpu.SemaphoreType.DMA((nt,)), pltpu.SemaphoreType.DMA((nt,))],
    )(*hs)


def _allgather_small(v):
    rows = v.shape[0]

    def body(v_ref, out_ref, send_sems, recv_sems):
        x, y, c, _ = _place()
        me = 4 * x + 2 * y + c
        out_ref[me] = v_ref[...]
        cps = []
        for k in range(1, 8):
            fx, fy, fc = (k >> 2) & 1, (k >> 1) & 1, k & 1
            to = (1 - x if fx else x, 1 - y if fy else y, 1 - c if fc else c)
            cps.append(pltpu.make_async_remote_copy(
                src_ref=v_ref, dst_ref=out_ref.at[me], send_sem=send_sems.at[k - 1], recv_sem=recv_sems.at[k - 1],
                device_id=to, device_id_type=MESH))
        for cp in cps:
            cp.start()
        for cp in cps:
            cp.wait()

    return _pc(
        body, name="allgather_small",
        in_specs=[pl.BlockSpec(memory_space=pltpu.VMEM)], out_specs=pl.BlockSpec(memory_space=pltpu.VMEM),
        out_shape=jax.ShapeDtypeStruct((8, rows, LANES), F32),
        scratch_shapes=[pltpu.SemaphoreType.DMA((7,)), pltpu.SemaphoreType.DMA((7,))],
    )(v)


def _adamw_math(w, g, m, v):
    m = ADAM_B1 * m + (1.0 - ADAM_B1) * g
    v = ADAM_B2 * v + (1.0 - ADAM_B2) * (g * g)
    m_hat = m / (1.0 - ADAM_B1 ** ADAM_STEP)
    v_hat = v / (1.0 - ADAM_B2 ** ADAM_STEP)
    return -ADAM_LR * (m_hat / (jnp.sqrt(v_hat) + ADAM_EPS) + ADAM_WD * w), m, v


def _adamw(w, g, m, v, name):
    r, cols = w.shape
    tr = min(r, 256)

    def body(w_ref, g_ref, m_ref, v_ref, d_ref, mo_ref, vo_ref):
        d, mn, vn = _adamw_math(w_ref[...], g_ref[...], m_ref[...], v_ref[...])
        d_ref[...] = d
        mo_ref[...] = mn
        vo_ref[...] = vn

    row = pl.BlockSpec((tr, cols), lambda i: (i, 0))
    return _pc(
        body, name=name, grid=(r // tr,), in_specs=[row] * 4, out_specs=[row] * 3,
        out_shape=[jax.ShapeDtypeStruct((r, cols), F32)] * 3,
        compiler_params=_params(("parallel",)),
    )(w, g, m, v)


def _adamw_small(w, gathered, m, v):
    rows = w.shape[0]

    def body(w_ref, g_ref, m_ref, v_ref, go_ref, d_ref, mo_ref, vo_ref):
        g = g_ref[0]
        for k in range(1, 8):
            g = g + g_ref[k]
        d, mn, vn = _adamw_math(w_ref[...], g, m_ref[...], v_ref[...])
        go_ref[...] = g
        d_ref[...] = d
        mo_ref[...] = mn
        vo_ref[...] = vn

    return _pc(
        body, name="adamw_small",
        out_shape=[jax.ShapeDtypeStruct((rows, LANES), F32)] * 4,
    )(w, gathered, m, v)


_BIAS_ROWS = 32


def _pack_shard(mlp_w_up, mlp_w_down, hyb_w_in, hyb_w_out, swa_w_qkv, swa_w_out, swa_b_qkv):
    parts = [t.astype(BF16).reshape(-1, 1024) for t in (mlp_w_up, mlp_w_down, hyb_w_in, hyb_w_out, swa_w_qkv, swa_w_out)]
    bias = lax.bitcast_convert_type(swa_b_qkv.reshape(384), BF16).reshape(1, 768)
    bias = jnp.pad(bias, ((0, _BIAS_ROWS - 1), (0, 256)))
    return jnp.concatenate(parts + [bias], axis=0)


def _unpack_weights(g):
    W = {
        "mlp_w_up": [g[:, l * 1024:(l + 1) * 1024, :] for l in range(2)],
        "mlp_w_down": [g[:, 2048 + l * 1024:2048 + (l + 1) * 1024, :].reshape(D_FF, D_MODEL) for l in range(2)],
        "hyb_w_in": g[:, 4096:4864, :].reshape(N_CHIPS, 1024, 768),
        "hyb_w_out": g[:, 4864:5120, :].reshape(1024, 1024),
        "swa_w_qkv": g[:, 5120:5504, :].reshape(N_CHIPS, 1024, 384),
        "swa_w_out": g[:, 5504:5760, :].reshape(1024, 1024),
    }
    bias = lax.bitcast_convert_type(g[:, 5760, :768].reshape(N_CHIPS, 384, 2), F32).reshape(1536)
    return W, bias


_SMALL = (("norm_mix", 16), ("norm_mlp", 16), ("ret_gn_gain", 4), ("dil_q_gain", 1), ("dil_k_gain", 1),
          ("swa_b_qkv", 12), ("swa_q_gain", 1), ("swa_k_gain", 1), ("swa_sinks", 1))
_SMALL_ROWS = 56


def _pack_small(d):
    rows = [d[n].reshape(r, LANES) for n, r in _SMALL]
    used = sum(r for _, r in _SMALL)
    return jnp.concatenate(rows + [jnp.zeros((_SMALL_ROWS - used, LANES), F32)], axis=0)


def _unpack_small(p):
    out, o = {}, 0
    for n, r in _SMALL:
        out[n] = p[o:o + r]
        o += r
    return out


def kernel(x, positions, norm_mix, norm_mlp, mlp_w_up, mlp_w_down, hyb_w_in, hyb_w_out, ret_gn_gain, dil_q_gain, dil_k_gain, swa_w_qkv, swa_b_qkv, swa_w_out, swa_q_gain, swa_k_gain, swa_sinks, loss_target, m_norm_mix, m_norm_mlp, m_mlp_w_up, m_mlp_w_down, m_hyb_w_in, m_hyb_w_out, m_ret_gn_gain, m_dil_q_gain, m_dil_k_gain, m_swa_w_qkv, m_swa_b_qkv, m_swa_w_out, m_swa_q_gain, m_swa_k_gain, m_swa_sinks, v_norm_mix, v_norm_mlp, v_mlp_w_up, v_mlp_w_down, v_hyb_w_in, v_hyb_w_out, v_ret_gn_gain, v_dil_q_gain, v_dil_k_gain, v_swa_w_qkv, v_swa_b_qkv, v_swa_w_out, v_swa_q_gain, v_swa_k_gain, v_swa_sinks):
    ax, ay, ac = lax.axis_index("x"), lax.axis_index("y"), lax.axis_index("c")
    chip = 2 * ax + ay
    place = jnp.stack([chip, ac]).astype(jnp.int32)
    S = x.shape[1]

    flat = _pack_shard(mlp_w_up, mlp_w_down, hyb_w_in[0], hyb_w_out[0], swa_w_qkv[0], swa_w_out[0], swa_b_qkv[0])
    buf = lax.dynamic_update_slice(jnp.zeros((N_CHIPS,) + flat.shape, BF16), flat[None], (chip, 0, 0))
    W, bias_full = _unpack_weights(_allgather_shards(buf))
    P = dict(norm_mix=norm_mix, norm_mlp=norm_mlp, ret_gn_gain=ret_gn_gain, dil_q_gain=dil_q_gain, dil_k_gain=dil_k_gain,
             swa_q_gain=swa_q_gain, swa_k_gain=swa_k_gain, swa_sinks=swa_sinks, swa_b_qkv_full=bias_full)

    loss_l, grad_x, gw, gp = _local_step(x[0], positions.reshape(S, 1), loss_target[0], W, P)
    loss = lax.psum(loss_l[0, 0], ("x", "y", "c"))

    slab = lambda t, r: t.reshape(N_CHIPS, r, t.size // (N_CHIPS * r))
    names = ["mlp_w_up0", "mlp_w_up1", "mlp_w_down0", "mlp_w_down1", "hyb_w_in", "hyb_w_out", "swa_w_qkv", "swa_w_out"]
    rows = [1024, 1024, 1024, 1024, 1024, 256, 1024, 256]
    ts = [slab(gw[n], r) for n, r in zip(names, rows)]
    ls = _swap_halves(ts)
    ps = [_pair_sum(t, l, place, f"pair_sum_{n}") for t, l, n in zip(ts, ls, names)]
    rs = _exchange_chips(ps)
    hs = [_final_sum(t, l, r, place, f"final_sum_{n}") for t, l, r, n in zip(ts, ls, rs, names)]
    gs = dict(zip(names, _share_halves(hs)))
    shards = dict(mlp_w_up0=(mlp_w_up[0], m_mlp_w_up[0], v_mlp_w_up[0]), mlp_w_up1=(mlp_w_up[1], m_mlp_w_up[1], v_mlp_w_up[1]),
                  mlp_w_down0=(mlp_w_down[0], m_mlp_w_down[0], v_mlp_w_down[0]),
                  mlp_w_down1=(mlp_w_down[1], m_mlp_w_down[1], v_mlp_w_down[1]),
                  hyb_w_in=(hyb_w_in[0], m_hyb_w_in[0], v_hyb_w_in[0]), hyb_w_out=(hyb_w_out[0], m_hyb_w_out[0], v_hyb_w_out[0]),
                  swa_w_qkv=(swa_w_qkv[0], m_swa_w_qkv[0], v_swa_w_qkv[0]), swa_w_out=(swa_w_out[0], m_swa_w_out[0], v_swa_w_out[0]))
    big = {}
    for n in names:
        w, m, v = shards[n]
        big[n] = (gs[n],) + tuple(_adamw(w, gs[n], m, v, f"adamw_{n}"))

    def big_out(n, k):
        if n in ("mlp_w_up", "mlp_w_down"):
            return jnp.stack([big[n + "0"][k], big[n + "1"][k]])
        return big[n][k][None]

    gsm = dict(gp)
    gsm["swa_sinks"] = jnp.pad(gp["swa_sinks"].reshape(16, HEAD)[:, 0], (0, LANES - 16))
    gathered = _allgather_small(_pack_small(gsm))

    def small_pack(norm_mix, norm_mlp, gn, dq, dk, b, sq, sk, sinks):
        dup = lambda t: jnp.tile(t.reshape(1, HEAD), (1, 2))
        bias = lax.dynamic_update_slice(jnp.zeros((12, LANES), F32), b.reshape(3, LANES), (3 * chip, 0))
        return _pack_small(dict(norm_mix=norm_mix, norm_mlp=norm_mlp, ret_gn_gain=gn, dil_q_gain=dup(dq), dil_k_gain=dup(dk),
                                swa_b_qkv=bias, swa_q_gain=dup(sq), swa_k_gain=dup(sk),
                                swa_sinks=jnp.pad(sinks.reshape(16), (0, LANES - 16))))

    pw = small_pack(norm_mix, norm_mlp, ret_gn_gain, dil_q_gain, dil_k_gain, swa_b_qkv, swa_q_gain, swa_k_gain, swa_sinks)
    pm = small_pack(m_norm_mix, m_norm_mlp, m_ret_gn_gain, m_dil_q_gain, m_dil_k_gain, m_swa_b_qkv, m_swa_q_gain, m_swa_k_gain, m_swa_sinks)
    pv = small_pack(v_norm_mix, v_norm_mlp, v_ret_gn_gain, v_dil_q_gain, v_dil_k_gain, v_swa_b_qkv, v_swa_q_gain, v_swa_k_gain, v_swa_sinks)
    small = [_unpack_small(t) for t in _adamw_small(pw, gathered, pm, pv)]

    def small_out(n, k):
        t = small[k][n]
        if n in ("norm_mix", "norm_mlp"):
            return t.reshape(2, D_MODEL)
        if n == "ret_gn_gain":
            return t.reshape(1, RET_HEADS, 128)
        if n == "swa_b_qkv":
            return lax.dynamic_slice(t, (3 * chip, 0), (3, LANES)).reshape(1, 384)
        if n == "swa_sinks":
            return t[0, :16].reshape(1, 16)
        return t[0, :HEAD].reshape(1, HEAD)

    order = ["norm_mix", "norm_mlp", "mlp_w_up", "mlp_w_down", "hyb_w_in", "hyb_w_out", "ret_gn_gain", "dil_q_gain",
             "dil_k_gain", "swa_w_qkv", "swa_b_qkv", "swa_w_out", "swa_q_gain", "swa_k_gain", "swa_sinks"]
    is_big = {"mlp_w_up", "mlp_w_down", "hyb_w_in", "hyb_w_out", "swa_w_qkv", "swa_w_out"}
    outs = [loss, grad_x[None]]
    for k in range(4):
        outs += [big_out(n, k) if n in is_big else small_out(n, k) for n in order]
    return tuple(outs)
```

```python
import functools
import math

import numpy as np
import jax
import jax.numpy as jnp
from jax import lax
from jax.experimental import pallas as pl
from jax.experimental.pallas import tpu as pltpu

F32, BF16 = jnp.float32, jnp.bfloat16
HIGHEST = lax.Precision.HIGHEST
MESH = pl.DeviceIdType.MESH

LANES = 128
VMEM_LIMIT = 48 << 20
D_MODEL = 1024
D_FF = 4096
HEAD = 64
EPS = 1e-6
BLK = 128
RET_HEADS = 4
RET_THETA = 10000.0
ROPE_THETA = 500000.0
ROPE_DIMS = 16
DIL_PATTERNS = ((128, 1), (512, 4), (2048, 16))
SWA_DIST = 127
N_CHIPS = 4
ADAM_LR, ADAM_B1, ADAM_B2, ADAM_EPS, ADAM_WD, ADAM_STEP = 0.001, 0.9, 0.999, 1e-08, 0.01, 10

_LOG_GAMMA = [float(np.log1p(-np.exp2(np.float32(-5.0 - h)))) for h in range(RET_HEADS)]


def _pc(body, **kw):
    return pl.pallas_call(body, **kw)


def _params(sem):
    return pltpu.CompilerParams(dimension_semantics=sem, vmem_limit_bytes=VMEM_LIMIT)


def _matmul(a, b, *, dims, tm, tn, tk, outs, name, epilogue=None, extras=(), b_cs=False, o_cs=0, a_pro=None):
    if dims == "nn":
        M, K = a.shape
        N = b.shape[0] * b.shape[2] if b_cs else b.shape[1]
        a_spec = pl.BlockSpec((tm, tk), lambda i, j, k: (i, k))
        if b_cs:
            npt = b.shape[2] // tn
            b_spec = pl.BlockSpec((None, tk, tn), lambda i, j, k: (j // npt, k, j % npt))
        else:
            b_spec = pl.BlockSpec((tk, tn), lambda i, j, k: (k, j))
        contract = (((1,), (0,)), ((), ()))
    elif dims == "nt":
        M, K = a.shape
        N = b.shape[1] if b_cs else b.shape[0]
        a_spec = pl.BlockSpec((tm, tk), lambda i, j, k: (i, k))
        if b_cs:
            kpt = b.shape[2] // tk
            b_spec = pl.BlockSpec((None, tn, tk), lambda i, j, k: (k // kpt, j, k % kpt))
        else:
            b_spec = pl.BlockSpec((tn, tk), lambda i, j, k: (j, k))
        contract = (((1,), (1,)), ((), ()))
    else:
        K, M = a.shape
        N = b.shape[1]
        a_spec = pl.BlockSpec((tk, tm), lambda i, j, k: (k, i))
        b_spec = pl.BlockSpec((tk, tn), lambda i, j, k: (k, j))
        contract = (((0,), (0,)), ((), ()))
    assert M % tm == 0 and N % tn == 0 and K % tk == 0, (name, M, N, K, tm, tn, tk)
    nk = K // tk
    ex_specs = []
    for arr, kind in extras:
        if kind == "mn":
            ex_specs.append(pl.BlockSpec((tm, tn), lambda i, j, k: (i, j)))
        else:
            ex_specs.append(pl.BlockSpec((1, tn), lambda i, j, k: (0, j)))
    if o_cs:
        n_sh = N // o_cs
        opt = n_sh // tn
        o_shape = (o_cs, M, n_sh)
        o_spec = pl.BlockSpec((None, tm, tn), lambda i, j, k: (j // opt, i, j % opt))
    else:
        o_shape = (M, N)
        o_spec = pl.BlockSpec((tm, tn), lambda i, j, k: (i, j))
    n_ex, n_out = len(extras), len(outs)
    if epilogue is None:
        epilogue = lambda acc: (acc,)

    def body(a_ref, b_ref, *rest):
        ex, o_refs, acc = rest[:n_ex], rest[n_ex:n_ex + n_out], rest[-1]
        k = pl.program_id(2)

        @pl.when(k == 0)
        def _():
            acc[...] = jnp.zeros_like(acc)

        av = a_ref[...] if a_pro is None else a_pro(a_ref[...])
        acc[...] += lax.dot_general(av.astype(BF16), b_ref[...].astype(BF16), contract, preferred_element_type=F32)

        @pl.when(k == nk - 1)
        def _():
            vals = epilogue(acc[...], *[e[...] for e in ex])
            for r, v in zip(o_refs, vals):
                r[...] = v.astype(r.dtype)

    res = _pc(
        body, name=name, grid=(M // tm, N // tn, nk),
        in_specs=[a_spec, b_spec] + ex_specs,
        out_specs=[o_spec] * n_out,
        out_shape=[jax.ShapeDtypeStruct(o_shape, dt) for dt in outs],
        scratch_shapes=[pltpu.VMEM((tm, tn), F32)],
        compiler_params=_params(("parallel", "parallel", "arbitrary")),
    )(a, b, *[e for e, _ in extras])
    return res[0] if n_out == 1 else res


def _roll(x, s):
    return pltpu.roll(x, s % LANES, 1)


def _rope(x, A, B, C, half):
    return x * A + _roll(x, LANES - half) * B + _roll(x, half) * C


def _rope_t(g, A, B, C, half):
    return g * A + _roll(g * B, half) + _roll(g * C, LANES - half)


def _gmean(x, G):
    return jnp.dot(x, G, precision=HIGHEST, preferred_element_type=F32)


def _head_mask(shape, half):
    lane = lax.broadcasted_iota(jnp.int32, shape, len(shape) - 1)
    return (lane >= HEAD) if half else (lane < HEAD)


def _group_matrix():
    i = np.arange(LANES)
    return jnp.asarray((i[:, None] // HEAD == i[None, :] // HEAD).astype(np.float32) / HEAD)


def _rope_inv():
    l = np.arange(LANES) % HEAD
    inv_r = np.power(np.float32(RET_THETA), -(l % 32).astype(np.float32) * np.float32(2.0 / HEAD))
    hp = ROPE_DIMS // 2
    inv_p = np.power(np.float32(ROPE_THETA), -(l % hp).astype(np.float32) * np.float32(2.0 / ROPE_DIMS))
    inv_p = np.where(l < ROPE_DIMS, inv_p, 0.0)
    return jnp.asarray(np.stack([inv_r, inv_p]).astype(np.float32))


def _tables(pos_col):
    S = pos_col.shape[0]
    tm = 512
    hp = ROPE_DIMS // 2

    def body(p_ref, inv_ref, o_ref):
        p = p_ref[...].astype(F32)
        lane = lax.broadcasted_iota(jnp.int32, (tm, LANES), 1) % HEAD
        ang = p * inv_ref[0:1, :]
        c, s = jnp.cos(ang), jnp.sin(ang)
        o_ref[:, 0:128] = c
        o_ref[:, 128:256] = jnp.where(lane < 32, -s, 0.0)
        o_ref[:, 256:384] = jnp.where(lane >= 32, s, 0.0)
        ang = p * inv_ref[1:2, :]
        c, s = jnp.cos(ang), jnp.sin(ang)
        o_ref[:, 384:512] = c
        o_ref[:, 512:640] = jnp.where(lane < hp, -s, 0.0)
        o_ref[:, 640:768] = jnp.where((lane >= hp) & (lane < ROPE_DIMS), s, 0.0)

    return _pc(
        body, name="rope_tables", grid=(S // tm,),
        in_specs=[pl.BlockSpec((tm, 1), lambda i: (i, 0)), pl.BlockSpec((2, LANES), lambda i: (0, 0))],
        out_specs=pl.BlockSpec((tm, 768), lambda i: (i, 0)),
        out_shape=jax.ShapeDtypeStruct((S, 768), F32),
        compiler_params=_params(("parallel",)),
    )(pos_col, _rope_inv())


def _tab(tab_ref, which):
    o = 384 * which
    return tab_ref[:, o:o + 128], tab_ref[:, o + 128:o + 256], tab_ref[:, o + 256:o + 384]


def _rms_fwd(x, g, name):
    S, Dm = x.shape
    tm = 512

    def body(x_ref, g_ref, h_ref):
        xv = x_ref[...]
        r = lax.rsqrt(jnp.mean(xv * xv, axis=-1, keepdims=True) + EPS)
        h_ref[...] = (xv * r * g_ref[...]).astype(BF16)

    return _pc(
        body, name=name, grid=(S // tm,),
        in_specs=[pl.BlockSpec((tm, Dm), lambda i: (i, 0)), pl.BlockSpec((1, Dm), lambda i: (0, 0))],
        out_specs=pl.BlockSpec((tm, Dm), lambda i: (i, 0)),
        out_shape=jax.ShapeDtypeStruct((S, Dm), BF16),
        compiler_params=_params(("parallel",)),
    )(x, g.reshape(1, Dm))


def _rms_bwd(x, g, dh, dres, name):
    S, Dm = x.shape
    tm = 512

    def body(x_ref, g_ref, dh_ref, dres_ref, dx_ref, dxb_ref, dg_ref):
        xv, dhv = x_ref[...], dh_ref[...]
        r = lax.rsqrt(jnp.mean(xv * xv, axis=-1, keepdims=True) + EPS)
        t = dhv * g_ref[...]
        dx = dres_ref[...] + r * t - xv * (r * r * r) * jnp.mean(xv * t, axis=-1, keepdims=True)
        dx_ref[...] = dx
        dxb_ref[...] = dx.astype(BF16)

        @pl.when(pl.program_id(0) == 0)
        def _():
            dg_ref[...] = jnp.zeros_like(dg_ref)

        dg_ref[...] += jnp.sum(dhv * xv * r, axis=0, keepdims=True)

    row = pl.BlockSpec((tm, Dm), lambda i: (i, 0))
    vec = pl.BlockSpec((1, Dm), lambda i: (0, 0))
    return _pc(
        body, name=name, grid=(S // tm,),
        in_specs=[row, vec, row, row], out_specs=[row, row, vec],
        out_shape=[jax.ShapeDtypeStruct((S, Dm), F32), jax.ShapeDtypeStruct((S, Dm), BF16),
                   jax.ShapeDtypeStruct((1, Dm), F32)],
        compiler_params=_params(("arbitrary",)),
    )(x, g.reshape(1, Dm), dh, dres)


def _hn_fwd(x, gain, G):
    r = lax.rsqrt(_gmean(x * x, G) + EPS)
    return x * r * gain


def _hn_bwd(x, gain, dy, G):
    r = lax.rsqrt(_gmean(x * x, G) + EPS)
    t = dy * gain
    dx = r * t - x * (r * r * r) * _gmean(x * t, G)
    return dx, jnp.sum(dy * x * r, axis=0, keepdims=True)


def _fold_halves(v):
    return v + _roll(v, HEAD)


def _even_pre_fwd(proj, tab, qg, kg):
    S = proj.shape[0]
    tm = 256

    def body(p_ref, tab_ref, qg_ref, kg_ref, g_ref, rq_ref, rk_ref, rv_ref, dq_ref, dk_ref, dv_ref):
        Ar, Br, Cr = _tab(tab_ref, 0)
        Ap, Bp, Cp = _tab(tab_ref, 1)
        G = g_ref[...]
        for c in range(2):
            sl = slice(c * 128, (c + 1) * 128)
            rq_ref[:, sl] = _rope(p_ref[:, c * 128:(c + 1) * 128], Ar, Br, Cr, 32).astype(BF16)
            rk_ref[:, sl] = (_rope(p_ref[:, 256 + c * 128:256 + (c + 1) * 128], Ar, Br, Cr, 32) * 0.125).astype(BF16)
        rv_ref[...] = p_ref[:, 512:1024].astype(BF16)
        for c in range(4):
            sl = slice(c * 128, (c + 1) * 128)
            q = _hn_fwd(p_ref[:, 1536 + c * 128:1536 + (c + 1) * 128], qg_ref[...], G)
            dq_ref[:, sl] = _rope(q, Ap, Bp, Cp, 8).astype(BF16)
            k = _hn_fwd(p_ref[:, 2048 + c * 128:2048 + (c + 1) * 128], kg_ref[...], G)
            dk_ref[:, sl] = _rope(k, Ap, Bp, Cp, 8).astype(BF16)
        dv_ref[...] = p_ref[:, 2560:3072].astype(BF16)

    row = lambda w: pl.BlockSpec((tm, w), lambda i: (i, 0))
    vec = pl.BlockSpec((1, LANES), lambda i: (0, 0))
    return _pc(
        body, name="even_pre_fwd", grid=(S // tm,),
        in_specs=[row(3072), row(768), vec, vec, pl.BlockSpec((LANES, LANES), lambda i: (0, 0))],
        out_specs=[row(256), row(256), row(512), row(512), row(512), row(512)],
        out_shape=[jax.ShapeDtypeStruct((S, w), BF16) for w in (256, 256, 512, 512, 512, 512)],
        compiler_params=_params(("parallel",)),
    )(proj, tab, qg, kg, _group_matrix())


def _even_pre_bwd(proj, tab, qg, kg, drq, drk, drv, drg, dqs, dks, dvs):
    S = proj.shape[0]
    tm = 256
    npat = len(dqs)

    def body(p_ref, tab_ref, qg_ref, kg_ref, g_ref, drq_ref, drk_ref, drv_ref, drg_ref, *rest):
        dq_refs, dk_refs, dv_refs = rest[:npat], rest[npat:2 * npat], rest[2 * npat:3 * npat]
        dp_ref, dqg_ref, dkg_ref = rest[3 * npat:]
        Ar, Br, Cr = _tab(tab_ref, 0)
        Ap, Bp, Cp = _tab(tab_ref, 1)
        G = g_ref[...]
        for c in range(2):
            sl = slice(c * 128, (c + 1) * 128)
            dp_ref[:, c * 128:(c + 1) * 128] = _rope_t(drq_ref[:, sl], Ar, Br, Cr, 32).astype(BF16)
            dp_ref[:, 256 + c * 128:256 + (c + 1) * 128] = _rope_t(drk_ref[:, sl] * 0.125, Ar, Br, Cr, 32).astype(BF16)
        dp_ref[:, 512:1024] = drv_ref[...].astype(BF16)
        dp_ref[:, 1024:1536] = drg_ref[...].astype(BF16)
        accq = jnp.zeros((1, LANES), F32)
        acck = jnp.zeros((1, LANES), F32)
        for c in range(4):
            sl = slice(c * 128, (c + 1) * 128)
            g = dq_refs[0][:, sl]
            for r in dq_refs[1:]:
                g = g + r[:, sl]
            dx, dg = _hn_bwd(p_ref[:, 1536 + c * 128:1536 + (c + 1) * 128], qg_ref[...], _rope_t(g, Ap, Bp, Cp, 8), G)
            dp_ref[:, 1536 + c * 128:1536 + (c + 1) * 128] = dx.astype(BF16)
            accq = accq + dg
            g = dk_refs[0][:, sl]
            for r in dk_refs[1:]:
                g = g + r[:, sl]
            dx, dg = _hn_bwd(p_ref[:, 2048 + c * 128:2048 + (c + 1) * 128], kg_ref[...], _rope_t(g, Ap, Bp, Cp, 8), G)
            dp_ref[:, 2048 + c * 128:2048 + (c + 1) * 128] = dx.astype(BF16)
            acck = acck + dg
        g = dv_refs[0][...]
        for r in dv_refs[1:]:
            g = g + r[...]
        dp_ref[:, 2560:3072] = g.astype(BF16)

        @pl.when(pl.program_id(0) == 0)
        def _():
            dqg_ref[...] = jnp.zeros_like(dqg_ref)
            dkg_ref[...] = jnp.zeros_like(dkg_ref)

        dqg_ref[...] += _fold_halves(accq)
        dkg_ref[...] += _fold_halves(acck)

    row = lambda w: pl.BlockSpec((tm, w), lambda i: (i, 0))
    vec = pl.BlockSpec((1, LANES), lambda i: (0, 0))
    return _pc(
        body, name="even_pre_bwd", grid=(S // tm,),
        in_specs=[row(3072), row(768), vec, vec, pl.BlockSpec((LANES, LANES), lambda i: (0, 0)),
                  row(256), row(256), row(512), row(512)] + [row(512)] * (3 * npat),
        out_specs=[row(3072), vec, vec],
        out_shape=[jax.ShapeDtypeStruct((S, 3072), BF16), jax.ShapeDtypeStruct((1, LANES), F32),
                   jax.ShapeDtypeStruct((1, LANES), F32)],
        compiler_params=_params(("arbitrary",)),
    )(proj, tab, qg, kg, _group_matrix(), drq, drk, drv, drg, *dqs, *dks, *dvs)


def _ret_consts(pair, half):
    lg = jnp.where(pair == 0, _LOG_GAMMA[half], _LOG_GAMMA[2 + half]).astype(F32)
    i = lax.broadcasted_iota(jnp.int32, (BLK, BLK), 0)
    j = lax.broadcasted_iota(jnp.int32, (BLK, BLK), 1)
    diff = (i - j).astype(F32)
    decay = jnp.where(diff >= 0, jnp.exp(lg * jnp.maximum(diff, 0.0)), 0.0)
    t = lax.broadcasted_iota(jnp.int32, (BLK, 1), 0).astype(F32)
    xi = jnp.exp(lg * (t + 1.0))
    zeta = jnp.exp(lg * (BLK - 1.0 - t))
    cd = jnp.exp(jnp.full((1, 1), BLK, F32) * lg)
    return decay, xi, zeta, cd


def _ret_fwd(rq, rk, rv):
    S = rq.shape[0]
    nc = S // BLK

    def body(q_ref, k_ref, v_ref, o_ref, st_ref, R):
        p, n = pl.program_id(0), pl.program_id(1)

        @pl.when(n == 0)
        def _():
            R[...] = jnp.zeros_like(R)

        q2, k2 = q_ref[...], k_ref[...]
        for half in range(2):
            decay, xi, zeta, cd = _ret_consts(p, half)
            m = _head_mask((BLK, LANES), half)
            qm = jnp.where(m, q2, jnp.zeros_like(q2))
            km = jnp.where(m, k2, jnp.zeros_like(k2))
            v = v_ref[:, half * 128:(half + 1) * 128]
            Rb = R[half].astype(BF16)
            st_ref[half] = Rb
            sc = lax.dot_general(qm, k2, (((1,), (1,)), ((), ())), preferred_element_type=F32) * decay
            o = jnp.dot(sc.astype(BF16), v, preferred_element_type=F32)
            o = o + jnp.dot(qm, Rb, preferred_element_type=F32) * xi
            o_ref[:, half * 128:(half + 1) * 128] = o
            kz = (km.astype(F32) * zeta).astype(BF16)
            R[half] = R[half] * cd + lax.dot_general(kz, v, (((0,), (0,)), ((), ())), preferred_element_type=F32)

    return _pc(
        body, name="ret_fwd", grid=(2, nc),
        in_specs=[pl.BlockSpec((BLK, 128), lambda p, n: (n, p)), pl.BlockSpec((BLK, 128), lambda p, n: (n, p)),
                  pl.BlockSpec((BLK, 256), lambda p, n: (n, p))],
        out_specs=[pl.BlockSpec((BLK, 256), lambda p, n: (n, p)),
                   pl.BlockSpec((None, None, 2, 128, 128), lambda p, n: (p, n, 0, 0, 0))],
        out_shape=[jax.ShapeDtypeStruct((S, 512), F32), jax.ShapeDtypeStruct((2, nc, 2, 128, 128), BF16)],
        scratch_shapes=[pltpu.VMEM((2, 128, 128), F32)],
        compiler_params=_params(("parallel", "arbitrary")),
    )(rq, rk, rv)


def _ret_bwd(rq, rk, rv, states, do):
    S = rq.shape[0]
    nc = S // BLK

    def body(q_ref, k_ref, v_ref, st_ref, do_ref, dq_ref, dk_ref, dv_ref, U):
        p, n = pl.program_id(0), pl.program_id(1)

        @pl.when(n == 0)
        def _():
            U[...] = jnp.zeros_like(U)

        q2, k2 = q_ref[...], k_ref[...]
        dq_acc = jnp.zeros((BLK, LANES), F32)
        dk_acc = jnp.zeros((BLK, LANES), F32)
        for half in range(2):
            decay, xi, zeta, cd = _ret_consts(p, half)
            m = _head_mask((BLK, LANES), half)
            qm = jnp.where(m, q2, jnp.zeros_like(q2))
            km = jnp.where(m, k2, jnp.zeros_like(k2))
            v = v_ref[:, half * 128:(half + 1) * 128]
            dob = do_ref[:, half * 128:(half + 1) * 128].astype(BF16)
            Rb = st_ref[half]
            Ub = U[half].astype(BF16)
            nt = (((1,), (1,)), ((), ()))
            tn = (((0,), (0,)), ((), ()))
            dsc = (lax.dot_general(dob, v, nt, preferred_element_type=F32) * decay).astype(BF16)
            xdo = (dob.astype(F32) * xi).astype(BF16)
            dq_acc += jnp.dot(dsc, km, preferred_element_type=F32) + lax.dot_general(xdo, Rb, nt, preferred_element_type=F32)
            dk_acc += lax.dot_general(dsc, qm, tn, preferred_element_type=F32) \
                + lax.dot_general(v, Ub, nt, preferred_element_type=F32) * zeta
            sc = (lax.dot_general(qm, k2, nt, preferred_element_type=F32) * decay).astype(BF16)
            kz = (km.astype(F32) * zeta).astype(BF16)
            dv_ref[:, half * 128:(half + 1) * 128] = lax.dot_general(sc, dob, tn, preferred_element_type=F32) \
                + jnp.dot(kz, Ub, preferred_element_type=F32)
            U[half] = U[half] * cd + lax.dot_general(qm, xdo, tn, preferred_element_type=F32)
        dq_ref[...] = dq_acc
        dk_ref[...] = dk_acc

    rev = lambda w: pl.BlockSpec((BLK, w), lambda p, n: (nc - 1 - n, p))
    return _pc(
        body, name="ret_bwd", grid=(2, nc),
        in_specs=[rev(128), rev(128), rev(256),
                  pl.BlockSpec((None, None, 2, 128, 128), lambda p, n: (p, nc - 1 - n, 0, 0, 0)), rev(256)],
        out_specs=[rev(128), rev(128), rev(256)],
        out_shape=[jax.ShapeDtypeStruct((S, 256), F32), jax.ShapeDtypeStruct((S, 256), F32),
                   jax.ShapeDtypeStruct((S, 512), F32)],
        scratch_shapes=[pltpu.VMEM((2, 128, 128), F32)],
        compiler_params=_params(("parallel", "arbitrary")),
    )(rq, rk, rv, states, do)


def _col_of(b, m):
    return jnp.max(jnp.where(m, b, -jnp.inf), axis=1, keepdims=True)


def _attn_fwd(q, k, v, *, nq, max_dist, name, sinks=None, want_bf16=False):
    L, Ck = k.shape
    nb, ncol = L // BLK, Ck // LANES
    scale = HEAD ** -0.5
    has_sink = sinks is not None

    def body(*refs):
        q_ref, kp_ref, kc_ref, vp_ref, vc_ref = refs[:5]
        sk_ref = refs[5] if has_sink else None
        outs = refs[5 + has_sink:]
        n = pl.program_id(1)
        kcat = jnp.concatenate([kp_ref[...], kc_ref[...]], axis=0)
        vcat = jnp.concatenate([vp_ref[...], vc_ref[...]], axis=0)
        r = lax.broadcasted_iota(jnp.int32, (BLK, 2 * BLK), 0)
        c = lax.broadcasted_iota(jnp.int32, (BLK, 2 * BLK), 1)
        dist = r + BLK - c
        valid = (dist >= 0) & (dist <= max_dist) & ((c >= BLK) | (n > 0))
        for i in range(nq):
            q2 = q_ref[:, i * 128:(i + 1) * 128]
            o2 = jnp.zeros((BLK, LANES), F32)
            l2 = jnp.zeros((BLK, LANES), F32)
            for half in range(2):
                m = _head_mask((BLK, LANES), half)
                qm = jnp.where(m, q2, jnp.zeros_like(q2))
                s = lax.dot_general(qm, kcat, (((1,), (1,)), ((), ())), preferred_element_type=F32) * scale
                s = jnp.where(valid, s, -jnp.inf)
                mx = jnp.max(s, axis=1, keepdims=True)
                if has_sink:
                    snk = _col_of(sk_ref[:, i * 128:(i + 1) * 128], _head_mask((1, LANES), half))
                    mx = jnp.maximum(mx, snk)
                pr = jnp.exp(s - mx)
                den = jnp.sum(pr, axis=1, keepdims=True)
                if has_sink:
                    den = den + jnp.exp(snk - mx)
                pv = jnp.dot(pr.astype(BF16), vcat, preferred_element_type=F32) / den
                o2 = jnp.where(m, pv, o2)
                l2 = jnp.where(m, mx + jnp.log(den), l2)
            outs[0][:, i * 128:(i + 1) * 128] = o2
            outs[1][:, i * 128:(i + 1) * 128] = l2
            if want_bf16:
                outs[2][:, i * 128:(i + 1) * 128] = o2.astype(BF16)

    qspec = pl.BlockSpec((BLK, nq * 128), lambda j, n: (n, j))
    cur = pl.BlockSpec((BLK, 128), lambda j, n: (n, j))
    prev = pl.BlockSpec((BLK, 128), lambda j, n: (jnp.maximum(n - 1, 0), j))
    in_specs = [qspec, prev, cur, prev, cur]
    args = [q, k, k, v, v]
    if has_sink:
        in_specs.append(pl.BlockSpec((1, nq * 128), lambda j, n: (0, j)))
        args.append(sinks)
    out_dts = [F32, F32] + ([BF16] if want_bf16 else [])
    return _pc(
        body, name=name, grid=(ncol, nb), in_specs=in_specs,
        out_specs=[qspec] * len(out_dts),
        out_shape=[jax.ShapeDtypeStruct(q.shape, dt) for dt in out_dts],
        compiler_params=_params(("parallel", "parallel")),
    )(*args)


def _attn_bwd(q, k, v, o, lse, do, *, nq, max_dist, name, sinks=None):
    L, Ck = k.shape
    nb, ncol = L // BLK, Ck // LANES
    scale = HEAD ** -0.5
    has_sink = sinks is not None
    nt = (((1,), (1,)), ((), ()))
    tn = (((0,), (0,)), ((), ()))

    def body(*refs):
        (qc_ref, qn_ref, kp_ref, kc_ref, vp_ref, vc_ref, oc_ref, on_ref, lc_ref, ln_ref, dc_ref, dn_ref) = refs[:12]
        sk_ref = refs[12] if has_sink else None
        outs = refs[12 + has_sink:]
        dq_ref, dk_ref, dv_ref = outs[:3]
        n = pl.program_id(1)
        kc, vc = kc_ref[...], vc_ref[...]
        kcat = jnp.concatenate([kp_ref[...], kc], axis=0)
        vcat = jnp.concatenate([vp_ref[...], vc], axis=0)
        r = lax.broadcasted_iota(jnp.int32, (BLK, 2 * BLK), 0)
        c = lax.broadcasted_iota(jnp.int32, (BLK, 2 * BLK), 1)
        dist = r + BLK - c
        valid_q = (dist >= 0) & (dist <= max_dist) & ((c >= BLK) | (n > 0))
        r2 = lax.broadcasted_iota(jnp.int32, (2 * BLK, BLK), 0)
        c2 = lax.broadcasted_iota(jnp.int32, (2 * BLK, BLK), 1)
        dist2 = r2 - c2
        valid_k = (dist2 >= 0) & (dist2 <= max_dist) & ((r2 < BLK) | (n < nb - 1))
        dk_acc = jnp.zeros((BLK, LANES), F32)
        dv_acc = jnp.zeros((BLK, LANES), F32)
        for i in range(nq):
            sl = slice(i * 128, (i + 1) * 128)
            qcur, docur = qc_ref[:, sl], dc_ref[:, sl]
            qcat = jnp.concatenate([qcur, qn_ref[:, sl]], axis=0)
            docat = jnp.concatenate([docur, dn_ref[:, sl]], axis=0)
            ocat = jnp.concatenate([oc_ref[:, sl], on_ref[:, sl]], axis=0)
            lcat = jnp.concatenate([lc_ref[:, sl], ln_ref[:, sl]], axis=0)
            dq2 = jnp.zeros((BLK, LANES), F32)
            ds2 = jnp.zeros((1, LANES), F32)
            for half in range(2):
                m1 = _head_mask((BLK, LANES), half)
                m2 = _head_mask((2 * BLK, LANES), half)
                dom = jnp.where(m2, docat, 0.0)
                delta = jnp.sum(dom * ocat, axis=1, keepdims=True)
                lcol = _col_of(lcat, m2)
                domb = dom.astype(BF16)
                qmcat = jnp.where(m2, qcat, jnp.zeros_like(qcat))
                qm = qmcat[:BLK]
                s = lax.dot_general(qm, kcat, nt, preferred_element_type=F32) * scale
                pr = jnp.where(valid_q, jnp.exp(s - lcol[:BLK]), 0.0)
                dp = lax.dot_general(domb[:BLK], vcat, nt, preferred_element_type=F32)
                ds = (pr * (dp - delta[:BLK])).astype(BF16)
                dq2 = jnp.where(m1, jnp.dot(ds, kcat, preferred_element_type=F32) * scale, dq2)
                if has_sink:
                    snk = _col_of(sk_ref[:, sl], _head_mask((1, LANES), half))
                    contrib = jnp.sum(-jnp.exp(snk - lcol[:BLK]) * delta[:BLK], axis=0, keepdims=True)
                    ds2 = jnp.where(_head_mask((1, LANES), half), contrib, ds2)
                s = lax.dot_general(qmcat, kc, nt, preferred_element_type=F32) * scale
                pr = jnp.where(valid_k, jnp.exp(s - lcol), 0.0)
                dv_acc += lax.dot_general(pr.astype(BF16), domb, tn, preferred_element_type=F32)
                dp = lax.dot_general(domb, vc, nt, preferred_element_type=F32)
                ds = (pr * (dp - delta)).astype(BF16)
                dk_acc += lax.dot_general(ds, qmcat, tn, preferred_element_type=F32) * scale
            dq_ref[:, sl] = dq2
            if has_sink:
                @pl.when(n == 0)
                def _():
                    outs[3][:, sl] = jnp.zeros((1, LANES), F32)

                outs[3][:, sl] += ds2
        dk_ref[...] = dk_acc
        dv_ref[...] = dv_acc

    qcur = pl.BlockSpec((BLK, nq * 128), lambda j, n: (n, j))
    qnext = pl.BlockSpec((BLK, nq * 128), lambda j, n: (jnp.minimum(n + 1, nb - 1), j))
    cur = pl.BlockSpec((BLK, 128), lambda j, n: (n, j))
    prev = pl.BlockSpec((BLK, 128), lambda j, n: (jnp.maximum(n - 1, 0), j))
    in_specs = [qcur, qnext, prev, cur, prev, cur, qcur, qnext, qcur, qnext, qcur, qnext]
    args = [q, q, k, k, v, v, o, o, lse, lse, do, do]
    out_specs = [qcur, cur, cur]
    out_shape = [jax.ShapeDtypeStruct(q.shape, F32), jax.ShapeDtypeStruct(k.shape, F32), jax.ShapeDtypeStruct(k.shape, F32)]
    if has_sink:
        vec = pl.BlockSpec((1, nq * 128), lambda j, n: (0, j))
        in_specs.append(vec)
        args.append(sinks)
        out_specs.append(vec)
        out_shape.append(jax.ShapeDtypeStruct((1, q.shape[1]), F32))
    return _pc(
        body, name=name, grid=(ncol, nb), in_specs=in_specs, out_specs=out_specs, out_shape=out_shape,
        compiler_params=_params(("parallel", "arbitrary")),
    )(*args)


ATT_TILE = 2048


def _rows(ref, start, n, r):
    if r == 1:
        return ref[pl.ds(start, n), :]
    return ref[pl.ds(start, n, stride=r), :]


def _set_rows(ref, start, n, r, val):
    if r == 1:
        ref[pl.ds(start, n), :] = val
    else:
        ref[pl.ds(start, n, stride=r), :] = val


def _band_geometry(S, patterns):
    rmax = max(r for _, r in patterns)
    H = BLK * rmax
    T = min(S, ATT_TILE)
    assert T % H == 0 and S % T == 0
    return H, T, S // T, T // BLK


def _band_fwd(q, k, v, *, patterns, nq, name, sinks=None, want_bf16=False):
    S, Ck = k.shape
    H, T, nt, nbt = _band_geometry(S, patterns)
    ncol = Ck // LANES
    scale = HEAD ** -0.5
    has_sink = sinks is not None
    nt_dims = (((1,), (1,)), ((), ()))

    def body(*refs):
        q_ref, kp_ref, kc_ref, vp_ref, vc_ref = refs[:5]
        sk_ref = refs[5] if has_sink else None
        n_out = 3 if want_bf16 else 2
        outs = refs[5 + has_sink:5 + has_sink + n_out]
        qf, kf, vf, M, L, A = refs[5 + has_sink + n_out:]
        t = pl.program_id(1)
        kf[0:H, :] = kp_ref[...].astype(F32)
        kf[H:H + T, :] = kc_ref[...].astype(F32)
        vf[0:H, :] = vp_ref[...].astype(F32)
        vf[H:H + T, :] = vc_ref[...].astype(F32)
        r_i = lax.broadcasted_iota(jnp.int32, (BLK, 2 * BLK), 0)
        c_i = lax.broadcasted_iota(jnp.int32, (BLK, 2 * BLK), 1)
        dist_i = r_i + BLK - c_i
        masks = [_head_mask((BLK, LANES), h) for h in range(2)]

        for i in range(nq):
            qf[...] = q_ref[:, i * 128:(i + 1) * 128].astype(F32)
            for p, (dist, r) in enumerate(patterns):
                in_band = (dist_i >= 0) & (dist_i <= dist)

                def unit(j, b, p=p, r=r, in_band=in_band):
                    q0 = j + b * (BLK * r)
                    q2 = _rows(qf, q0, BLK, r).astype(BF16)
                    kcat = _rows(kf, H + q0 - BLK * r, 2 * BLK, r).astype(BF16)
                    vcat = _rows(vf, H + q0 - BLK * r, 2 * BLK, r).astype(BF16)
                    valid = in_band & ((c_i >= BLK) | ((b > 0) | (t > 0)))
                    m2 = jnp.zeros((BLK, LANES), F32)
                    l2 = jnp.zeros((BLK, LANES), F32)
                    a2 = jnp.zeros((BLK, LANES), F32)
                    for half in range(2):
                        m = masks[half]
                        qm = jnp.where(m, q2, jnp.zeros_like(q2))
                        s = lax.dot_general(qm, kcat, nt_dims, preferred_element_type=F32) * scale
                        s = jnp.where(valid, s, -jnp.inf)
                        mx = jnp.max(s, axis=1, keepdims=True)
                        pr = jnp.exp(s - mx)
                        den = jnp.sum(pr, axis=1, keepdims=True)
                        pv = jnp.dot(pr.astype(BF16), vcat, preferred_element_type=F32)
                        m2 = jnp.where(m, mx, m2)
                        l2 = jnp.where(m, den, l2)
                        a2 = jnp.where(m, pv, a2)
                    if p > 0:
                        mo = _rows(M, q0, BLK, r)
                        mn = jnp.maximum(mo, m2)
                        wa, wb = jnp.exp(mo - mn), jnp.exp(m2 - mn)
                        l2 = wa * _rows(L, q0, BLK, r) + wb * l2
                        a2 = wa * _rows(A, q0, BLK, r) + wb * a2
                        m2 = mn
                    _set_rows(M, q0, BLK, r, m2)
                    _set_rows(L, q0, BLK, r, l2)
                    _set_rows(A, q0, BLK, r, a2)

                for u in range(nbt):
                    unit(u % r, u // r)
            sl = slice(i * 128, (i + 1) * 128)
            mm, ll, aa = M[...], L[...], A[...]
            if has_sink:
                snk = sk_ref[:, sl]
                mn = jnp.maximum(mm, snk)
                w = jnp.exp(mm - mn)
                ll = ll * w + jnp.exp(snk - mn)
                aa = aa * w
                mm = mn
            o = aa / ll
            outs[0][:, sl] = o
            outs[1][:, sl] = mm + jnp.log(ll)
            if want_bf16:
                outs[2][:, sl] = o.astype(BF16)

    th = T // H
    qspec = pl.BlockSpec((T, nq * 128), lambda j, t: (t, j))
    cur = pl.BlockSpec((T, 128), lambda j, t: (t, j))
    prev = pl.BlockSpec((H, 128), lambda j, t: (jnp.maximum(t * th - 1, 0), j))
    in_specs = [qspec, prev, cur, prev, cur]
    args = [q, k, k, v, v]
    if has_sink:
        in_specs.append(pl.BlockSpec((1, nq * 128), lambda j, t: (0, j)))
        args.append(sinks)
    out_dts = [F32, F32] + ([BF16] if want_bf16 else [])
    return _pc(
        body, name=name, grid=(ncol, nt), in_specs=in_specs,
        out_specs=[qspec] * len(out_dts),
        out_shape=[jax.ShapeDtypeStruct(q.shape, dt) for dt in out_dts],
        scratch_shapes=[pltpu.VMEM((T, LANES), F32), pltpu.VMEM((H + T, LANES), F32), pltpu.VMEM((H + T, LANES), F32),
                        pltpu.VMEM((T, LANES), F32), pltpu.VMEM((T, LANES), F32), pltpu.VMEM((T, LANES), F32)],
        compiler_params=_params(("parallel", "parallel")),
    )(*args)


def _band_bwd(q, k, v, lse, delta, do, *, patterns, nq, name, sinks=None):
    S, Ck = k.shape
    H, T, nt, nbt = _band_geometry(S, patterns)
    ncol = Ck // LANES
    scale = HEAD ** -0.5
    has_sink = sinks is not None
    nt_dims = (((1,), (1,)), ((), ()))
    tn_dims = (((0,), (0,)), ((), ()))

    def body(*refs):
        (qc_ref, qn_ref, kp_ref, kc_ref, vp_ref, vc_ref, lc_ref, ln_ref, ec_ref, en_ref, dc_ref, dn_ref) = refs[:12]
        sk_ref = refs[12] if has_sink else None
        n_out = 4 if has_sink else 3
        outs = refs[12 + has_sink:12 + has_sink + n_out]
        dq_ref, dk_ref, dv_ref = outs[:3]
        qf, kf, vf, lf, ef, df = refs[12 + has_sink + n_out:]
        t = pl.program_id(1)
        kf[0:H, :] = kp_ref[...].astype(F32)
        kf[H:H + T, :] = kc_ref[...].astype(F32)
        vf[0:H, :] = vp_ref[...].astype(F32)
        vf[H:H + T, :] = vc_ref[...].astype(F32)
        dk_ref[...] = jnp.zeros_like(dk_ref)
        dv_ref[...] = jnp.zeros_like(dv_ref)
        r_i = lax.broadcasted_iota(jnp.int32, (BLK, 2 * BLK), 0)
        c_i = lax.broadcasted_iota(jnp.int32, (BLK, 2 * BLK), 1)
        dist_q = r_i + BLK - c_i
        dist_h = dist_q[:, :BLK]
        m1 = [_head_mask((BLK, LANES), h) for h in range(2)]

        def head_inputs(half, q2, do2, l2, e2):
            m = m1[half]
            lh = jnp.where(m, l2, _roll(l2, HEAD))
            eh = jnp.where(m, e2, _roll(e2, HEAD))
            return jnp.where(m, q2, jnp.zeros_like(q2)), jnp.where(m, do2, 0.0).astype(BF16), lh, eh

        for i in range(nq):
            sl = slice(i * 128, (i + 1) * 128)
            for buf, c_ref, n_ref in ((qf, qc_ref, qn_ref), (lf, lc_ref, ln_ref), (ef, ec_ref, en_ref), (df, dc_ref, dn_ref)):
                buf[0:T, :] = c_ref[:, sl].astype(F32)
                buf[T:T + H, :] = n_ref[:, sl].astype(F32)
            if has_sink:
                @pl.when(t == 0)
                def _():
                    outs[3][:, sl] = jnp.zeros((1, LANES), F32)

                outs[3][:, sl] += jnp.sum(-jnp.exp(sk_ref[:, sl] - lc_ref[:, sl]) * ec_ref[:, sl], axis=0, keepdims=True)
            for p, (dist, r) in enumerate(patterns):
                band_q = (dist_q >= 0) & (dist_q <= dist)
                band_h = (dist_h >= 0) & (dist_h <= dist)

                def add_rows(ref, start, val, r=r):
                    _set_rows(ref, start, BLK, r, _rows(ref, start, BLK, r) + val)

                def unit(j, b, p=p, r=r, band_q=band_q):
                    q0 = j + b * (BLK * r)
                    q2 = _rows(qf, q0, BLK, r).astype(BF16)
                    do2, l2, e2 = _rows(df, q0, BLK, r), _rows(lf, q0, BLK, r), _rows(ef, q0, BLK, r)
                    kcat = _rows(kf, H + q0 - BLK * r, 2 * BLK, r).astype(BF16)
                    vcat = _rows(vf, H + q0 - BLK * r, 2 * BLK, r).astype(BF16)
                    valid = band_q & ((c_i >= BLK) | ((b > 0) | (t > 0)))
                    dq2 = jnp.zeros((BLK, LANES), F32)
                    dkc = jnp.zeros((2 * BLK, LANES), F32)
                    dvc = jnp.zeros((2 * BLK, LANES), F32)
                    for half in range(2):
                        qm, dom, lh, eh = head_inputs(half, q2, do2, l2, e2)
                        s = lax.dot_general(qm, kcat, nt_dims, preferred_element_type=F32) * scale
                        pr = jnp.where(valid, jnp.exp(s - jnp.concatenate([lh, lh], axis=1)), 0.0)
                        dp = lax.dot_general(dom, vcat, nt_dims, preferred_element_type=F32)
                        ds = (pr * (dp - jnp.concatenate([eh, eh], axis=1))).astype(BF16)
                        dq2 = jnp.where(m1[half], jnp.dot(ds, kcat, preferred_element_type=F32) * scale, dq2)
                        dvc += lax.dot_general(pr.astype(BF16), dom, tn_dims, preferred_element_type=F32)
                        dkc += lax.dot_general(ds, qm, tn_dims, preferred_element_type=F32) * scale
                    if p > 0:
                        dq2 = dq2 + _rows(dq_ref.at[:, sl], q0, BLK, r)
                    _set_rows(dq_ref.at[:, sl], q0, BLK, r, dq2)
                    add_rows(dk_ref, q0, dkc[BLK:])
                    add_rows(dv_ref, q0, dvc[BLK:])
                    if b > 0:
                        add_rows(dk_ref, q0 - BLK * r, dkc[:BLK])
                        add_rows(dv_ref, q0 - BLK * r, dvc[:BLK])

                def halo_unit(j, r=r, band_h=band_h):
                    k0 = j + (nbt // r - 1) * (BLK * r)
                    q2 = _rows(qf, T + j, BLK, r).astype(BF16)
                    do2, l2, e2 = _rows(df, T + j, BLK, r), _rows(lf, T + j, BLK, r), _rows(ef, T + j, BLK, r)
                    kc = _rows(kf, H + k0, BLK, r).astype(BF16)
                    vc = _rows(vf, H + k0, BLK, r).astype(BF16)
                    dk2 = jnp.zeros((BLK, LANES), F32)
                    dv2 = jnp.zeros((BLK, LANES), F32)
                    for half in range(2):
                        qm, dom, lh, eh = head_inputs(half, q2, do2, l2, e2)
                        s = lax.dot_general(qm, kc, nt_dims, preferred_element_type=F32) * scale
                        pr = jnp.where(band_h, jnp.exp(s - lh), 0.0)
                        dp = lax.dot_general(dom, vc, nt_dims, preferred_element_type=F32)
                        ds = (pr * (dp - eh)).astype(BF16)
                        dv2 += lax.dot_general(pr.astype(BF16), dom, tn_dims, preferred_element_type=F32)
                        dk2 += lax.dot_general(ds, qm, tn_dims, preferred_element_type=F32) * scale
                    add_rows(dk_ref, k0, dk2)
                    add_rows(dv_ref, k0, dv2)

                for u in range(nbt):
                    unit(u % r, u // r)
                if nt > 1:
                    @pl.when(t < nt - 1)
                    def _(r=r, halo_unit=halo_unit):
                        for j in range(r):
                            halo_unit(j)

    th = T // H
    last = S // H - 1
    qcur = pl.BlockSpec((T, nq * 128), lambda j, t: (t, j))
    qnext = pl.BlockSpec((H, nq * 128), lambda j, t: (jnp.minimum((t + 1) * th, last), j))
    cur = pl.BlockSpec((T, 128), lambda j, t: (t, j))
    prev = pl.BlockSpec((H, 128), lambda j, t: (jnp.maximum(t * th - 1, 0), j))
    in_specs = [qcur, qnext, prev, cur, prev, cur, qcur, qnext, qcur, qnext, qcur, qnext]
    args = [q, q, k, k, v, v, lse, lse, delta, delta, do, do]
    out_specs = [qcur, cur, cur]
    out_shape = [jax.ShapeDtypeStruct(q.shape, F32), jax.ShapeDtypeStruct(k.shape, F32), jax.ShapeDtypeStruct(k.shape, F32)]
    if has_sink:
        vec = pl.BlockSpec((1, nq * 128), lambda j, t: (0, j))
        in_specs.append(vec)
        args.append(sinks)
        out_specs.append(vec)
        out_shape.append(jax.ShapeDtypeStruct((1, q.shape[1]), F32))
    big = pltpu.VMEM((T + H, LANES), F32)
    return _pc(
        body, name=name, grid=(ncol, nt), in_specs=in_specs, out_specs=out_specs, out_shape=out_shape,
        scratch_shapes=[big] * 6,
        compiler_params=_params(("parallel", "arbitrary")),
    )(*args)


def _delta(do, o, name):
    S, C = do.shape
    tm = 512

    def body(do_ref, o_ref, g_ref, e_ref):
        for c in range(C // LANES):
            sl = slice(c * 128, (c + 1) * 128)
            e_ref[:, sl] = _gmean(do_ref[:, sl] * o_ref[:, sl], g_ref[...]) * float(HEAD)

    row = pl.BlockSpec((tm, C), lambda i: (i, 0))
    return _pc(
        body, name=name, grid=(S // tm,),
        in_specs=[row, row, pl.BlockSpec((LANES, LANES), lambda i: (0, 0))], out_specs=row,
        out_shape=jax.ShapeDtypeStruct((S, C), F32),
        compiler_params=_params(("parallel",)),
    )(do, o, _group_matrix())


def _even_post_fwd(ro, proj, gn, da):
    S = ro.shape[0]
    tm = 256

    def body(ro_ref, rg_ref, gn_ref, da_ref, mix_ref):
        for c in range(4):
            sl = slice(c * 128, (c + 1) * 128)
            x = ro_ref[:, sl]
            mu = jnp.mean(x, axis=1, keepdims=True)
            xc = x - mu
            var = jnp.mean(xc * xc, axis=1, keepdims=True)
            y = xc * lax.rsqrt(var + EPS) * gn_ref[:, sl]
            z = rg_ref[:, sl]
            mix_ref[:, sl] = (z * jax.nn.sigmoid(z) * y).astype(BF16)
        mix_ref[:, 512:1024] = da_ref[...].astype(BF16)

    row = lambda w: pl.BlockSpec((tm, w), lambda i: (i, 0))
    return _pc(
        body, name="even_post_fwd", grid=(S // tm,),
        in_specs=[row(512), pl.BlockSpec((tm, 512), lambda i: (i, 2)), pl.BlockSpec((1, 512), lambda i: (0, 0)), row(512)],
        out_specs=row(1024), out_shape=jax.ShapeDtypeStruct((S, 1024), BF16),
        compiler_params=_params(("parallel",)),
    )(ro, proj, gn, da)


def _even_post_bwd(ro, proj, gn, dmixed):
    S = ro.shape[0]
    tm = 256

    def body(ro_ref, rg_ref, gn_ref, dm_ref, dro_ref, drg_ref, dgn_ref):
        @pl.when(pl.program_id(0) == 0)
        def _():
            dgn_ref[...] = jnp.zeros_like(dgn_ref)

        for c in range(4):
            sl = slice(c * 128, (c + 1) * 128)
            x = ro_ref[:, sl]
            mu = jnp.mean(x, axis=1, keepdims=True)
            xc = x - mu
            rstd = lax.rsqrt(jnp.mean(xc * xc, axis=1, keepdims=True) + EPS)
            xh = xc * rstd
            gain = gn_ref[:, sl]
            y = xh * gain
            z = rg_ref[:, sl]
            sg = jax.nn.sigmoid(z)
            dra = dm_ref[:, sl]
            drg_ref[:, sl] = dra * y * sg * (1.0 + z * (1.0 - sg))
            dy = dra * z * sg
            dgn_ref[:, sl] += jnp.sum(dy * xh, axis=0, keepdims=True)
            dxh = dy * gain
            dro_ref[:, sl] = rstd * (dxh - jnp.mean(dxh, axis=1, keepdims=True)
                                     - xh * jnp.mean(dxh * xh, axis=1, keepdims=True))

    row = lambda w: pl.BlockSpec((tm, w), lambda i: (i, 0))
    vec = pl.BlockSpec((1, 512), lambda i: (0, 0))
    return _pc(
        body, name="even_post_bwd", grid=(S // tm,),
        in_specs=[row(512), pl.BlockSpec((tm, 512), lambda i: (i, 2)), vec, row(512)],
        out_specs=[row(512), row(512), vec],
        out_shape=[jax.ShapeDtypeStruct((S, 512), F32), jax.ShapeDtypeStruct((S, 512), F32),
                   jax.ShapeDtypeStruct((1, 512), F32)],
        compiler_params=_params(("arbitrary",)),
    )(ro, proj, gn, dmixed)


def _swa_pre_fwd(proj, tab, qg, kg):
    S = proj.shape[0]
    tm = 256

    def body(p_ref, tab_ref, qg_ref, kg_ref, g_ref, q_ref, k_ref, v_ref):
        Ap, Bp, Cp = _tab(tab_ref, 1)
        G = g_ref[...]
        lo = _head_mask((tm, LANES), 0)
        for c in range(8):
            sl = slice(c * 128, (c + 1) * 128)
            q_ref[:, sl] = _rope(_hn_fwd(p_ref[:, sl], qg_ref[...], G), Ap, Bp, Cp, 8).astype(BF16)
        for c in range(2):
            kn = _rope(_hn_fwd(p_ref[:, 1024 + c * 128:1024 + (c + 1) * 128], kg_ref[...], G), Ap, Bp, Cp, 8)
            vv = p_ref[:, 1280 + c * 128:1280 + (c + 1) * 128]
            for t, ref in ((kn, k_ref), (vv, v_ref)):
                sw = _roll(t, HEAD)
                ref[:, (2 * c) * 128:(2 * c + 1) * 128] = jnp.where(lo, t, sw).astype(BF16)
                ref[:, (2 * c + 1) * 128:(2 * c + 2) * 128] = jnp.where(lo, sw, t).astype(BF16)

    row = lambda w: pl.BlockSpec((tm, w), lambda i: (i, 0))
    vec = pl.BlockSpec((1, LANES), lambda i: (0, 0))
    return _pc(
        body, name="swa_pre_fwd", grid=(S // tm,),
        in_specs=[row(1536), row(768), vec, vec, pl.BlockSpec((LANES, LANES), lambda i: (0, 0))],
        out_specs=[row(1024), row(512), row(512)],
        out_shape=[jax.ShapeDtypeStruct((S, w), BF16) for w in (1024, 512, 512)],
        compiler_params=_params(("parallel",)),
    )(proj, tab, qg, kg, _group_matrix())


def _swa_pre_bwd(proj, tab, qg, kg, dq, dk, dv):
    S = proj.shape[0]
    tm = 256

    def body(p_ref, tab_ref, qg_ref, kg_ref, g_ref, dq_ref, dk_ref, dv_ref, dp_ref, db_ref, dqg_ref, dkg_ref):
        Ap, Bp, Cp = _tab(tab_ref, 1)
        G = g_ref[...]
        lo = _head_mask((tm, LANES), 0)

        @pl.when(pl.program_id(0) == 0)
        def _():
            db_ref[...] = jnp.zeros_like(db_ref)
            dqg_ref[...] = jnp.zeros_like(dqg_ref)
            dkg_ref[...] = jnp.zeros_like(dkg_ref)

        accq = jnp.zeros((1, LANES), F32)
        acck = jnp.zeros((1, LANES), F32)
        for c in range(8):
            sl = slice(c * 128, (c + 1) * 128)
            dx, dg = _hn_bwd(p_ref[:, sl], qg_ref[...], _rope_t(dq_ref[:, sl], Ap, Bp, Cp, 8), G)
            dp_ref[:, sl] = dx.astype(BF16)
            db_ref[:, sl] += jnp.sum(dx, axis=0, keepdims=True)
            accq = accq + dg
        for c in range(2):
            folded = []
            for ref in (dk_ref, dv_ref):
                a = ref[:, (2 * c) * 128:(2 * c + 1) * 128]
                b = ref[:, (2 * c + 1) * 128:(2 * c + 2) * 128]
                folded.append(jnp.where(lo, a + _roll(a, HEAD), b + _roll(b, HEAD)))
            ks = slice(1024 + c * 128, 1024 + (c + 1) * 128)
            dx, dg = _hn_bwd(p_ref[:, ks], kg_ref[...], _rope_t(folded[0], Ap, Bp, Cp, 8), G)
            dp_ref[:, ks] = dx.astype(BF16)
            db_ref[:, ks] += jnp.sum(dx, axis=0, keepdims=True)
            acck = acck + dg
            vs = slice(1280 + c * 128, 1280 + (c + 1) * 128)
            dp_ref[:, vs] = folded[1].astype(BF16)
            db_ref[:, vs] += jnp.sum(folded[1], axis=0, keepdims=True)
        dqg_ref[...] += _fold_halves(accq)
        dkg_ref[...] += _fold_halves(acck)

    row = lambda w: pl.BlockSpec((tm, w), lambda i: (i, 0))
    vec = pl.BlockSpec((1, LANES), lambda i: (0, 0))
    return _pc(
        body, name="swa_pre_bwd", grid=(S // tm,),
        in_specs=[row(1536), row(768), vec, vec, pl.BlockSpec((LANES, LANES), lambda i: (0, 0)),
                  row(1024), row(512), row(512)],
        out_specs=[row(1536), pl.BlockSpec((1, 1536), lambda i: (0, 0)), vec, vec],
        out_shape=[jax.ShapeDtypeStruct((S, 1536), BF16), jax.ShapeDtypeStruct((1, 1536), F32),
                   jax.ShapeDtypeStruct((1, LANES), F32), jax.ShapeDtypeStruct((1, LANES), F32)],
        compiler_params=_params(("arbitrary",)),
    )(proj, tab, qg, kg, _group_matrix(), dq, dk, dv)


def _loss_head(y, target):
    S, Dm = y.shape
    tm = 512

    def body(y_ref, t_ref, l_ref, dy_ref, dyb_ref):
        @pl.when(pl.program_id(0) == 0)
        def _():
            l_ref[...] = jnp.zeros_like(l_ref)

        e = y_ref[...] - t_ref[...]
        dy = e * (1.0 / Dm)
        dy_ref[...] = dy
        dyb_ref[...] = dy.astype(BF16)
        row = jnp.sum(e * e, axis=1, keepdims=True) * (0.5 / Dm)
        l_ref[...] += jnp.sum(row, axis=0, keepdims=True)

    row = pl.BlockSpec((tm, Dm), lambda i: (i, 0))
    return _pc(
        body, name="loss_head", grid=(S // tm,), in_specs=[row, row],
        out_specs=[pl.BlockSpec((1, LANES), lambda i: (0, 0)), row, row],
        out_shape=[jax.ShapeDtypeStruct((1, LANES), F32), jax.ShapeDtypeStruct((S, Dm), F32),
                   jax.ShapeDtypeStruct((S, Dm), BF16)],
        compiler_params=_params(("arbitrary",)),
    )(y, target)


def _relu2_of(u):
    r = jnp.maximum(u.astype(F32), 0.0)
    return r * r


def _drelu2(acc, u):
    return (acc * 2.0 * jnp.maximum(u.astype(F32), 0.0),)


def _add(acc, res):
    return (acc + res,)


_T = dict(tm=1024, tn=1024, tk=1024)


def _mlp_fwd(x, g, w_up, w_dn, tag):
    h = _rms_fwd(x, g, f"rms_mlp_fwd{tag}")
    u = _matmul(h, w_up, dims="nn", **_T, outs=[BF16], b_cs=True, name=f"mlp_up{tag}")
    x_out = _matmul(u, w_dn, dims="nn", **_T, outs=[F32], epilogue=_add, extras=[(x, "mn")], a_pro=_relu2_of,
                    name=f"mlp_down{tag}")
    return x_out, (h, u)


def _mlp_bwd(x, g, w_up, w_dn, saved, dy, dyb, tag):
    h, u = saved
    du = _matmul(dyb, w_dn, dims="nt", **_T, outs=[BF16], epilogue=_drelu2, extras=[(u, "mn")], name=f"mlp_du{tag}")
    dw_dn = _matmul(u, dyb, dims="tn", **_T, outs=[F32], a_pro=_relu2_of, name=f"mlp_dwdown{tag}")
    dw_up = _matmul(h, du, dims="tn", **_T, outs=[F32], o_cs=N_CHIPS, name=f"mlp_dwup{tag}")
    dh = _matmul(du, w_up, dims="nt", **_T, outs=[F32], b_cs=True, name=f"mlp_dh{tag}")
    dx, dxb, dg = _rms_bwd(x, g, dh, dy, f"rms_mlp_bwd{tag}")
    return dx, dxb, dg, dw_up, dw_dn


def _pattern_view(t, r):
    S, C = t.shape
    return t.reshape(S // r, r * C)


def _local_step(x, pos_col, target, W, P):
    S = x.shape[0]
    tab = _tables(pos_col)
    tile2 = lambda g: jnp.tile(g.reshape(1, HEAD), (1, 2))
    dqg, dkg = tile2(P["dil_q_gain"]), tile2(P["dil_k_gain"])
    sqg, skg = tile2(P["swa_q_gain"]), tile2(P["swa_k_gain"])
    gn = P["ret_gn_gain"].reshape(1, 512)
    sink_b = jnp.repeat(P["swa_sinks"].reshape(16), HEAD).reshape(1, 1024)

    h0 = _rms_fwd(x, P["norm_mix"][0], "rms_mix_fwd0")
    proj = _matmul(h0, W["hyb_w_in"], dims="nn", tm=1024, tn=768, tk=1024, outs=[F32], b_cs=True, name="hyb_in")
    rq, rk, rv, dq, dk, dv = _even_pre_fwd(proj, tab, dqg, dkg)
    ro, states = _ret_fwd(rq, rk, rv)
    dil = [(w // r, r) for w, r in DIL_PATTERNS]
    da, dlse = _band_fwd(dq, dk, dv, patterns=dil, nq=1, name="dil_fwd")
    mixed = _even_post_fwd(ro, proj, gn, da)
    x1 = _matmul(mixed, W["hyb_w_out"], dims="nn", **_T, outs=[F32], epilogue=_add, extras=[(x, "mn")], name="hyb_out")
    x2, mlp0 = _mlp_fwd(x1, P["norm_mlp"][0], W["mlp_w_up"][0], W["mlp_w_down"][0], "0")

    h2 = _rms_fwd(x2, P["norm_mix"][1], "rms_mix_fwd1")
    proj2 = _matmul(h2, W["swa_w_qkv"], dims="nn", tm=1024, tn=384, tk=1024, outs=[F32], b_cs=True,
                    epilogue=_add, extras=[(P["swa_b_qkv_full"].reshape(1, 1536), "n")], name="swa_qkv")
    sq, sk, sv = _swa_pre_fwd(proj2, tab, sqg, skg)
    swa = [(SWA_DIST, 1)]
    so, slse, so_b = _band_fwd(sq, sk, sv, patterns=swa, nq=2, name="swa_fwd", sinks=sink_b, want_bf16=True)
    x3 = _matmul(so_b, W["swa_w_out"], dims="nn", **_T, outs=[F32], epilogue=_add, extras=[(x2, "mn")], name="swa_out")
    y, mlp1 = _mlp_fwd(x3, P["norm_mlp"][1], W["mlp_w_up"][1], W["mlp_w_down"][1], "1")
    loss, dy, dyb = _loss_head(y, target)

    gw, gp = {}, {}
    dx3, dx3b, dg_mlp1, gw["mlp_w_up1"], gw["mlp_w_down1"] = _mlp_bwd(x3, P["norm_mlp"][1], W["mlp_w_up"][1],
                                                                       W["mlp_w_down"][1], mlp1, dy, dyb, "1")
    gw["swa_w_out"] = _matmul(so_b, dx3b, dims="tn", **_T, outs=[F32], name="swa_dwout")
    dso = _matmul(dx3b, W["swa_w_out"], dims="nt", **_T, outs=[F32], name="swa_do")
    dsq, dsk, dsv, dsink = _band_bwd(sq, sk, sv, slse, _delta(dso, so, "swa_delta"), dso, patterns=swa, nq=2,
                                     name="swa_bwd", sinks=sink_b)
    dproj2, gp["swa_b_qkv"], gp["swa_q_gain"], gp["swa_k_gain"] = _swa_pre_bwd(proj2, tab, sqg, skg, dsq, dsk, dsv)
    gp["swa_sinks"] = dsink
    gw["swa_w_qkv"] = _matmul(h2, dproj2, dims="tn", tm=1024, tn=384, tk=1024, outs=[F32], o_cs=N_CHIPS, name="swa_dwqkv")
    dh2 = _matmul(dproj2, W["swa_w_qkv"], dims="nt", tm=1024, tn=1024, tk=384, outs=[F32], b_cs=True, name="swa_dh")
    dx2, dx2b, dg_mix1 = _rms_bwd(x2, P["norm_mix"][1], dh2, dx3, "rms_mix_bwd1")

    dx1, dx1b, dg_mlp0, gw["mlp_w_up0"], gw["mlp_w_down0"] = _mlp_bwd(x1, P["norm_mlp"][0], W["mlp_w_up"][0],
                                                                       W["mlp_w_down"][0], mlp0, dx2, dx2b, "0")
    gw["hyb_w_out"] = _matmul(mixed, dx1b, dims="tn", **_T, outs=[F32], name="hyb_dwout")
    dmixed = _matmul(dx1b, W["hyb_w_out"], dims="nt", **_T, outs=[F32], name="hyb_dmixed")
    dro, drg, gp["ret_gn_gain"] = _even_post_bwd(ro, proj, gn, dmixed)
    drq, drk, drv = _ret_bwd(rq, rk, rv, states, dro)
    dda = dmixed[:, 512:]
    ddq, ddk, ddv = _band_bwd(dq, dk, dv, dlse, _delta(dda, da, "dil_delta"), dda, patterns=dil, nq=1, name="dil_bwd")
    dproj, gp["dil_q_gain"], gp["dil_k_gain"] = _even_pre_bwd(proj, tab, dqg, dkg, drq, drk, drv, drg, [ddq], [ddk], [ddv])
    gw["hyb_w_in"] = _matmul(h0, dproj, dims="tn", tm=1024, tn=768, tk=1024, outs=[F32], o_cs=N_CHIPS, name="hyb_dwin")
    dh0 = _matmul(dproj, W["hyb_w_in"], dims="nt", tm=1024, tn=1024, tk=768, outs=[F32], b_cs=True, name="hyb_dh")
    grad_x, _, dg_mix0 = _rms_bwd(x, P["norm_mix"][0], dh0, dx1, "rms_mix_bwd0")
    gp["norm_mix"] = jnp.concatenate([dg_mix0, dg_mix1], axis=0)
    gp["norm_mlp"] = jnp.concatenate([dg_mlp0, dg_mlp1], axis=0)
    return loss, grad_x, gw, gp


HBM = pl.BlockSpec(memory_space=pltpu.HBM)


def _place():
    x, y, c = lax.axis_index("x"), lax.axis_index("y"), lax.axis_index("c")
    chips = [(1 - x, y), (x, 1 - y), (1 - x, 1 - y)]
    return x, y, c, chips


def _allgather_shards(buf):
    _, R, Wd = buf.shape
    Rh = R // 2

    def body(b_ref, out_ref, send_sems, recv_sems):
        x, y, c, chips = _place()
        sibling = (x, y, 1 - c)

        def copy(k, chip, core, to):
            block = b_ref.at[2 * chip[0] + chip[1], pl.ds(core * Rh, Rh), :]
            return pltpu.make_async_remote_copy(
                src_ref=block, dst_ref=block, send_sem=send_sems.at[k], recv_sem=recv_sems.at[k],
                device_id=to, device_id_type=MESH)

        first = [copy(k, (x, y), c, (*chip, c)) for k, chip in enumerate(chips)]
        for cp in first:
            cp.start()
        passed = [copy(3 + k, chip, c, sibling) for k, chip in enumerate(chips)]
        for k, chip in enumerate(chips):
            copy(k, chip, c, (x, y, c)).wait_recv()
            passed[k].start()
        for k, chip in enumerate(chips):
            copy(3 + k, chip, 1 - c, (x, y, c)).wait_recv()
        for cp in first + passed:
            cp.wait_send()

    return _pc(
        body, name="allgather_weights", in_specs=[HBM], out_specs=HBM,
        out_shape=jax.ShapeDtypeStruct(buf.shape, buf.dtype), input_output_aliases={0: 0},
        scratch_shapes=[pltpu.SemaphoreType.DMA((6,)), pltpu.SemaphoreType.DMA((6,))],
    )(buf)


def _swap_halves(ts):
    nt = len(ts)

    def body(*refs):
        t_refs, l_refs, send_sems, recv_sems = refs[:nt], refs[nt:2 * nt], refs[-2], refs[-1]
        x, y, c, _ = _place()
        cps = []
        for k in range(nt):
            rh = t_refs[k].shape[1] // 2
            cps.append(pltpu.make_async_remote_copy(
                src_ref=t_refs[k].at[:, pl.ds((1 - c) * rh, rh), :], dst_ref=l_refs[k],
                send_sem=send_sems.at[k], recv_sem=recv_sems.at[k], device_id=(x, y, 1 - c), device_id_type=MESH))
        for cp in cps:
            cp.start()
        for cp in cps:
            cp.wait()

    return _pc(
        body, name="grad_swap_halves", in_specs=[HBM] * nt, out_specs=[HBM] * nt,
        out_shape=[jax.ShapeDtypeStruct((t.shape[0], t.shape[1] // 2, t.shape[2]), F32) for t in ts],
        scratch_shapes=[pltpu.SemaphoreType.DMA((nt,)), pltpu.SemaphoreType.DMA((nt,))],
    )(*ts)


def _pair_sum(t, l, place, name):
    _, r, cols = t.shape
    rh = r // 2
    tr = min(rh, 256)
    nr = rh // tr

    def body(pl_ref, t_ref, l_ref, o_ref):
        o_ref[...] = (t_ref[...] + l_ref[...]).astype(BF16)

    return _pc(
        body, name=name,
        grid_spec=pltpu.PrefetchScalarGridSpec(
            num_scalar_prefetch=1, grid=(N_CHIPS, nr),
            in_specs=[pl.BlockSpec((None, tr, cols), lambda s, i, p: (s, p[1] * nr + i, 0)),
                      pl.BlockSpec((None, tr, cols), lambda s, i, p: (s, i, 0))],
            out_specs=pl.BlockSpec((None, tr, cols), lambda s, i, p: (s, i, 0))),
        out_shape=jax.ShapeDtypeStruct((N_CHIPS, rh, cols), BF16),
        compiler_params=_params(("parallel", "parallel")),
    )(place, t, l)


def _exchange_chips(ps):
    nt = len(ps)

    def body(*refs):
        p_refs, r_refs, send_sems, recv_sems = refs[:nt], refs[nt:2 * nt], refs[-2], refs[-1]
        x, y, c, chips = _place()
        cps = []
        for t in range(nt):
            for k, chip in enumerate(chips):
                cps.append(pltpu.make_async_remote_copy(
                    src_ref=p_refs[t].at[2 * chip[0] + chip[1]], dst_ref=r_refs[t].at[k],
                    send_sem=send_sems.at[3 * t + k], recv_sem=recv_sems.at[3 * t + k],
                    device_id=(*chip, c), device_id_type=MESH))
        for cp in cps:
            cp.start()
        for cp in cps:
            cp.wait()

    return _pc(
        body, name="grad_exchange_chips", in_specs=[HBM] * nt, out_specs=[HBM] * nt,
        out_shape=[jax.ShapeDtypeStruct((3,) + p.shape[1:], BF16) for p in ps],
        scratch_shapes=[pltpu.SemaphoreType.DMA((3 * nt,)), pltpu.SemaphoreType.DMA((3 * nt,))],
    )(*ps)


def _final_sum(t, l, rcv, place, name):
    _, r, cols = t.shape
    rh = r // 2
    tr = min(rh, 256)
    nr = rh // tr

    def body(pl_ref, t_ref, l_ref, r_ref, o_ref):
        acc = t_ref[...] + l_ref[...]
        for k in range(3):
            acc = acc + r_ref[k].astype(F32)
        o_ref[...] = acc

    return _pc(
        body, name=name,
        grid_spec=pltpu.PrefetchScalarGridSpec(
            num_scalar_prefetch=1, grid=(nr,),
            in_specs=[pl.BlockSpec((None, tr, cols), lambda i, p: (p[0], p[1] * nr + i, 0)),
                      pl.BlockSpec((None, tr, cols), lambda i, p: (p[0], i, 0)),
                      pl.BlockSpec((3, tr, cols), lambda i, p: (0, i, 0))],
            out_specs=pl.BlockSpec((tr, cols), lambda i, p: (p[1] * nr + i, 0))),
        out_shape=jax.ShapeDtypeStruct((r, cols), F32),
        compiler_params=_params(("parallel",)),
    )(place, t, l, rcv)


def _share_halves(hs):
    nt = len(hs)

    def body(*refs):
        h_refs, send_sems, recv_sems = refs[:nt], refs[-2], refs[-1]
        x, y, c, _ = _place()
        cps = []
        for k in range(nt):
            rh = h_refs[k].shape[0] // 2
            half = h_refs[k].at[pl.ds(c * rh, rh), :]
            cps.append(pltpu.make_async_remote_copy(
                src_ref=half, dst_ref=half, send_sem=send_sems.at[k], recv_sem=recv_sems.at[k],
                device_id=(x, y, 1 - c), device_id_type=MESH))
        for cp in cps:
            cp.start()
        for cp in cps:
            cp.wait()

    return _pc(
        body, name="grad_share_halves", in_specs=[HBM] * nt, out_specs=[HBM] * nt,
        out_shape=[jax.ShapeDtypeStruct(h.shape, F32) for h in hs],
        input_output_aliases={k: k for k in range(nt)},
        scratch_shapes=[pltpu.SemaphoreType.DMA((nt,)), pltpu.SemaphoreType.DMA((nt,))],
    )(*hs)


def _allgather_small(v):
    rows = v.shape[0]

    def body(v_ref, out_ref, send_sems, recv_sems):
        x, y, c, _ = _place()
        me = 4 * x + 2 * y + c
        out_ref[me] = v_ref[...]
        cps = []
        for k in range(1, 8):
            fx, fy, fc = (k >> 2) & 1, (k >> 1) & 1, k & 1
            to = (1 - x if fx else x, 1 - y if fy else y, 1 - c if fc else c)
            cps.append(pltpu.make_async_remote_copy(
                src_ref=v_ref, dst_ref=out_ref.at[me], send_sem=send_sems.at[k - 1], recv_sem=recv_sems.at[k - 1],
                device_id=to, device_id_type=MESH))
        for cp in cps:
            cp.start()
        for cp in cps:
            cp.wait()

    return _pc(
        body, name="allgather_small",
        in_specs=[pl.BlockSpec(memory_space=pltpu.VMEM)], out_specs=pl.BlockSpec(memory_space=pltpu.VMEM),
        out_shape=jax.ShapeDtypeStruct((8, rows, LANES), F32),
        scratch_shapes=[pltpu.SemaphoreType.DMA((7,)), pltpu.SemaphoreType.DMA((7,))],
    )(v)


def _adamw_math(w, g, m, v):
    m = ADAM_B1 * m + (1.0 - ADAM_B1) * g
    v = ADAM_B2 * v + (1.0 - ADAM_B2) * (g * g)
    m_hat = m / (1.0 - ADAM_B1 ** ADAM_STEP)
    v_hat = v / (1.0 - ADAM_B2 ** ADAM_STEP)
    return -ADAM_LR * (m_hat / (jnp.sqrt(v_hat) + ADAM_EPS) + ADAM_WD * w), m, v


def _adamw(w, g, m, v, name):
    r, cols = w.shape
    tr = min(r, 256)

    def body(w_ref, g_ref, m_ref, v_ref, d_ref, mo_ref, vo_ref):
        d, mn, vn = _adamw_math(w_ref[...], g_ref[...], m_ref[...], v_ref[...])
        d_ref[...] = d
        mo_ref[...] = mn
        vo_ref[...] = vn

    row = pl.BlockSpec((tr, cols), lambda i: (i, 0))
    return _pc(
        body, name=name, grid=(r // tr,), in_specs=[row] * 4, out_specs=[row] * 3,
        out_shape=[jax.ShapeDtypeStruct((r, cols), F32)] * 3,
        compiler_params=_params(("parallel",)),
    )(w, g, m, v)


def _adamw_small(w, gathered, m, v):
    rows = w.shape[0]

    def body(w_ref, g_ref, m_ref, v_ref, go_ref, d_ref, mo_ref, vo_ref):
        g = g_ref[0]
        for k in range(1, 8):
            g = g + g_ref[k]
        d, mn, vn = _adamw_math(w_ref[...], g, m_ref[...], v_ref[...])
        go_ref[...] = g
        d_ref[...] = d
        mo_ref[...] = mn
        vo_ref[...] = vn

    return _pc(
        body, name="adamw_small",
        out_shape=[jax.ShapeDtypeStruct((rows, LANES), F32)] * 4,
    )(w, gathered, m, v)


_BIAS_ROWS = 32


def _pack_shard(mlp_w_up, mlp_w_down, hyb_w_in, hyb_w_out, swa_w_qkv, swa_w_out, swa_b_qkv):
    parts = [t.astype(BF16).reshape(-1, 1024) for t in (mlp_w_up, mlp_w_down, hyb_w_in, hyb_w_out, swa_w_qkv, swa_w_out)]
    bias = lax.bitcast_convert_type(swa_b_qkv.reshape(384), BF16).reshape(1, 768)
    bias = jnp.pad(bias, ((0, _BIAS_ROWS - 1), (0, 256)))
    return jnp.concatenate(parts + [bias], axis=0)


def _unpack_weights(g):
    W = {
        "mlp_w_up": [g[:, l * 1024:(l + 1) * 1024, :] for l in range(2)],
        "mlp_w_down": [g[:, 2048 + l * 1024:2048 + (l + 1) * 1024, :].reshape(D_FF, D_MODEL) for l in range(2)],
        "hyb_w_in": g[:, 4096:4864, :].reshape(N_CHIPS, 1024, 768),
        "hyb_w_out": g[:, 4864:5120, :].reshape(1024, 1024),
        "swa_w_qkv": g[:, 5120:5504, :].reshape(N_CHIPS, 1024, 384),
        "swa_w_out": g[:, 5504:5760, :].reshape(1024, 1024),
    }
    bias = lax.bitcast_convert_type(g[:, 5760, :768].reshape(N_CHIPS, 384, 2), F32).reshape(1536)
    return W, bias


_SMALL = (("norm_mix", 16), ("norm_mlp", 16), ("ret_gn_gain", 4), ("dil_q_gain", 1), ("dil_k_gain", 1),
          ("swa_b_qkv", 12), ("swa_q_gain", 1), ("swa_k_gain", 1), ("swa_sinks", 1))
_SMALL_ROWS = 56


def _pack_small(d):
    rows = [d[n].reshape(r, LANES) for n, r in _SMALL]
    used = sum(r for _, r in _SMALL)
    return jnp.concatenate(rows + [jnp.zeros((_SMALL_ROWS - used, LANES), F32)], axis=0)


def _unpack_small(p):
    out, o = {}, 0
    for n, r in _SMALL:
        out[n] = p[o:o + r]
        o += r
    return out


def kernel(x, positions, norm_mix, norm_mlp, mlp_w_up, mlp_w_down, hyb_w_in, hyb_w_out, ret_gn_gain, dil_q_gain, dil_k_gain, swa_w_qkv, swa_b_qkv, swa_w_out, swa_q_gain, swa_k_gain, swa_sinks, loss_target, m_norm_mix, m_norm_mlp, m_mlp_w_up, m_mlp_w_down, m_hyb_w_in, m_hyb_w_out, m_ret_gn_gain, m_dil_q_gain, m_dil_k_gain, m_swa_w_qkv, m_swa_b_qkv, m_swa_w_out, m_swa_q_gain, m_swa_k_gain, m_swa_sinks, v_norm_mix, v_norm_mlp, v_mlp_w_up, v_mlp_w_down, v_hyb_w_in, v_hyb_w_out, v_ret_gn_gain, v_dil_q_gain, v_dil_k_gain, v_swa_w_qkv, v_swa_b_qkv, v_swa_w_out, v_swa_q_gain, v_swa_k_gain, v_swa_sinks):
    ax, ay, ac = lax.axis_index("x"), lax.axis_index("y"), lax.axis_index("c")
    chip = 2 * ax + ay
    place = jnp.stack([chip, ac]).astype(jnp.int32)
    S = x.shape[1]

    flat = _pack_shard(mlp_w_up, mlp_w_down, hyb_w_in[0], hyb_w_out[0], swa_w_qkv[0], swa_w_out[0], swa_b_qkv[0])
    buf = lax.dynamic_update_slice(jnp.zeros((N_CHIPS,) + flat.shape, BF16), flat[None], (chip, 0, 0))
    W, bias_full = _unpack_weights(_allgather_shards(buf))
    P = dict(norm_mix=norm_mix, norm_mlp=norm_mlp, ret_gn_gain=ret_gn_gain, dil_q_gain=dil_q_gain, dil_k_gain=dil_k_gain,
             swa_q_gain=swa_q_gain, swa_k_gain=swa_k_gain, swa_sinks=swa_sinks, swa_b_qkv_full=bias_full)

    loss_l, grad_x, gw, gp = _local_step(x[0], positions.reshape(S, 1), loss_target[0], W, P)
    loss = lax.psum(loss_l[0, 0], ("x", "y", "c"))

    slab = lambda t, r: t.reshape(N_CHIPS, r, t.size // (N_CHIPS * r))
    names = ["mlp_w_up0", "mlp_w_up1", "mlp_w_down0", "mlp_w_down1", "hyb_w_in", "hyb_w_out", "swa_w_qkv", "swa_w_out"]
    rows = [1024, 1024, 1024, 1024, 1024, 256, 1024, 256]
    ts = [slab(gw[n], r) for n, r in zip(names, rows)]
    ls = _swap_halves(ts)
    ps = [_pair_sum(t, l, place, f"pair_sum_{n}") for t, l, n in zip(ts, ls, names)]
    rs = _exchange_chips(ps)
    hs = [_final_sum(t, l, r, place, f"final_sum_{n}") for t, l, r, n in zip(ts, ls, rs, names)]
    gs = dict(zip(names, _share_halves(hs)))
    shards = dict(mlp_w_up0=(mlp_w_up[0], m_mlp_w_up[0], v_mlp_w_up[0]), mlp_w_up1=(mlp_w_up[1], m_mlp_w_up[1], v_mlp_w_up[1]),
                  mlp_w_down0=(mlp_w_down[0], m_mlp_w_down[0], v_mlp_w_down[0]),
                  mlp_w_down1=(mlp_w_down[1], m_mlp_w_down[1], v_mlp_w_down[1]),
                  hyb_w_in=(hyb_w_in[0], m_hyb_w_in[0], v_hyb_w_in[0]), hyb_w_out=(hyb_w_out[0], m_hyb_w_out[0], v_hyb_w_out[0]),
                  swa_w_qkv=(swa_w_qkv[0], m_swa_w_qkv[0], v_swa_w_qkv[0]), swa_w_out=(swa_w_out[0], m_swa_w_out[0], v_swa_w_out[0]))
    big = {}
    for n in names:
        w, m, v = shards[n]
        big[n] = (gs[n],) + tuple(_adamw(w, gs[n], m, v, f"adamw_{n}"))

    def big_out(n, k):
        if n in ("mlp_w_up", "mlp_w_down"):
            return jnp.stack([big[n + "0"][k], big[n + "1"][k]])
        return big[n][k][None]

    gsm = dict(gp)
    gsm["swa_sinks"] = jnp.pad(gp["swa_sinks"].reshape(16, HEAD)[:, 0], (0, LANES - 16))
    gathered = _allgather_small(_pack_small(gsm))

    def small_pack(norm_mix, norm_mlp, gn, dq, dk, b, sq, sk, sinks):
        dup = lambda t: jnp.tile(t.reshape(1, HEAD), (1, 2))
        bias = lax.dynamic_update_slice(jnp.zeros((12, LANES), F32), b.reshape(3, LANES), (3 * chip, 0))
        return _pack_small(dict(norm_mix=norm_mix, norm_mlp=norm_mlp, ret_gn_gain=gn, dil_q_gain=dup(dq), dil_k_gain=dup(dk),
                                swa_b_qkv=bias, swa_q_gain=dup(sq), swa_k_gain=dup(sk),
                                swa_sinks=jnp.pad(sinks.reshape(16), (0, LANES - 16))))

    pw = small_pack(norm_mix, norm_mlp, ret_gn_gain, dil_q_gain, dil_k_gain, swa_b_qkv, swa_q_gain, swa_k_gain, swa_sinks)
    pm = small_pack(m_norm_mix, m_norm_mlp, m_ret_gn_gain, m_dil_q_gain, m_dil_k_gain, m_swa_b_qkv, m_swa_q_gain, m_swa_k_gain, m_swa_sinks)
    pv = small_pack(v_norm_mix, v_norm_mlp, v_ret_gn_gain, v_dil_q_gain, v_dil_k_gain, v_swa_b_qkv, v_swa_q_gain, v_swa_k_gain, v_swa_sinks)
    small = [_unpack_small(t) for t in _adamw_small(pw, gathered, pm, pv)]

    def small_out(n, k):
        t = small[k][n]
        if n in ("norm_mix", "norm_mlp"):
            return t.reshape(2, D_MODEL)
        if n == "ret_gn_gain":
            return t.reshape(1, RET_HEADS, 128)
        if n == "swa_b_qkv":
            return lax.dynamic_slice(t, (3 * chip, 0), (3, LANES)).reshape(1, 384)
        if n == "swa_sinks":
            return t[0, :16].reshape(1, 16)
        return t[0, :HEAD].reshape(1, HEAD)

    order = ["norm_mix", "norm_mlp", "mlp_w_up", "mlp_w_down", "hyb_w_in", "hyb_w_out", "ret_gn_gain", "dil_q_gain",
             "dil_k_gain", "swa_w_qkv", "swa_b_qkv", "swa_w_out", "swa_q_gain", "swa_k_gain", "swa_sinks"]
    is_big = {"mlp_w_up", "mlp_w_down", "hyb_w_in", "hyb_w_out", "swa_w_qkv", "swa_w_out"}
    outs = [loss, grad_x[None]]
    for k in range(4):
        outs += [big_out(n, k) if n in is_big else small_out(n, k) for n in order]
    return tuple(outs)
```

```python
import functools
import math

import numpy as np
import jax
import jax.numpy as jnp
from jax import lax
from jax.experimental import pallas as pl
from jax.experimental.pallas import tpu as pltpu

F32, BF16 = jnp.float32, jnp.bfloat16
HIGHEST = lax.Precision.HIGHEST
MESH = pl.DeviceIdType.MESH

LANES = 128
VMEM_LIMIT = 48 << 20
D_MODEL = 1024
D_FF = 4096
HEAD = 64
EPS = 1e-6
BLK = 128
RET_HEADS = 4
RET_THETA = 10000.0
ROPE_THETA = 500000.0
ROPE_DIMS = 16
DIL_PATTERNS = ((128, 1), (512, 4), (2048, 16))
SWA_DIST = 127
N_CHIPS = 4
ADAM_LR, ADAM_B1, ADAM_B2, ADAM_EPS, ADAM_WD, ADAM_STEP = 0.001, 0.9, 0.999, 1e-08, 0.01, 10

_LOG_GAMMA = [float(np.log1p(-np.exp2(np.float32(-5.0 - h)))) for h in range(RET_HEADS)]


def _pc(body, **kw):
    return pl.pallas_call(body, **kw)


def _params(sem):
    return pltpu.CompilerParams(dimension_semantics=sem, vmem_limit_bytes=VMEM_LIMIT)


def _matmul(a, b, *, dims, tm, tn, tk, outs, name, epilogue=None, extras=(), b_cs=False, o_cs=0, a_pro=None):
    if dims == "nn":
        M, K = a.shape
        N = b.shape[0] * b.shape[2] if b_cs else b.shape[1]
        a_spec = pl.BlockSpec((tm, tk), lambda i, j, k: (i, k))
        if b_cs:
            npt = b.shape[2] // tn
            b_spec = pl.BlockSpec((None, tk, tn), lambda i, j, k: (j // npt, k, j % npt))
        else:
            b_spec = pl.BlockSpec((tk, tn), lambda i, j, k: (k, j))
        contract = (((1,), (0,)), ((), ()))
    elif dims == "nt":
        M, K = a.shape
        N = b.shape[1] if b_cs else b.shape[0]
        a_spec = pl.BlockSpec((tm, tk), lambda i, j, k: (i, k))
        if b_cs:
            kpt = b.shape[2] // tk
            b_spec = pl.BlockSpec((None, tn, tk), lambda i, j, k: (k // kpt, j, k % kpt))
        else:
            b_spec = pl.BlockSpec((tn, tk), lambda i, j, k: (j, k))
        contract = (((1,), (1,)), ((), ()))
    else:
        K, M = a.shape
        N = b.shape[1]
        a_spec = pl.BlockSpec((tk, tm), lambda i, j, k: (k, i))
        b_spec = pl.BlockSpec((tk, tn), lambda i, j, k: (k, j))
        contract = (((0,), (0,)), ((), ()))
    assert M % tm == 0 and N % tn == 0 and K % tk == 0, (name, M, N, K, tm, tn, tk)
    nk = K // tk
    ex_specs = []
    for arr, kind in extras:
        if kind == "mn":
            ex_specs.append(pl.BlockSpec((tm, tn), lambda i, j, k: (i, j)))
        else:
            ex_specs.append(pl.BlockSpec((1, tn), lambda i, j, k: (0, j)))
    if o_cs:
        n_sh = N // o_cs
        opt = n_sh // tn
        o_shape = (o_cs, M, n_sh)
        o_spec = pl.BlockSpec((None, tm, tn), lambda i, j, k: (j // opt, i, j % opt))
    else:
        o_shape = (M, N)
        o_spec = pl.BlockSpec((tm, tn), lambda i, j, k: (i, j))
    n_ex, n_out = len(extras), len(outs)
    if epilogue is None:
        epilogue = lambda acc: (acc,)

    def body(a_ref, b_ref, *rest):
        ex, o_refs, acc = rest[:n_ex], rest[n_ex:n_ex + n_out], rest[-1]
        k = pl.program_id(2)

        @pl.when(k == 0)
        def _():
            acc[...] = jnp.zeros_like(acc)

        av = a_ref[...] if a_pro is None else a_pro(a_ref[...])
        acc[...] += lax.dot_general(av.astype(BF16), b_ref[...].astype(BF16), contract, preferred_element_type=F32)

        @pl.when(k == nk - 1)
        def _():
            vals = epilogue(acc[...], *[e[...] for e in ex])
            for r, v in zip(o_refs, vals):
                r[...] = v.astype(r.dtype)

    res = _pc(
        body, name=name, grid=(M // tm, N // tn, nk),
        in_specs=[a_spec, b_spec] + ex_specs,
        out_specs=[o_spec] * n_out,
        out_shape=[jax.ShapeDtypeStruct(o_shape, dt) for dt in outs],
        scratch_shapes=[pltpu.VMEM((tm, tn), F32)],
        compiler_params=_params(("parallel", "parallel", "arbitrary")),
    )(a, b, *[e for e, _ in extras])
    return res[0] if n_out == 1 else res


def _roll(x, s):
    return pltpu.roll(x, s % LANES, 1)


def _rope(x, A, B, C, half):
    return x * A + _roll(x, LANES - half) * B + _roll(x, half) * C


def _rope_t(g, A, B, C, half):
    return g * A + _roll(g * B, half) + _roll(g * C, LANES - half)


def _gmean(x, G):
    return jnp.dot(x, G, precision=HIGHEST, preferred_element_type=F32)


def _head_mask(shape, half):
    lane = lax.broadcasted_iota(jnp.int32, shape, len(shape) - 1)
    return (lane >= HEAD) if half else (lane < HEAD)


def _group_matrix():
    i = np.arange(LANES)
    return jnp.asarray((i[:, None] // HEAD == i[None, :] // HEAD).astype(np.float32) / HEAD)


def _rope_inv():
    l = np.arange(LANES) % HEAD
    inv_r = np.power(np.float32(RET_THETA), -(l % 32).astype(np.float32) * np.float32(2.0 / HEAD))
    hp = ROPE_DIMS // 2
    inv_p = np.power(np.float32(ROPE_THETA), -(l % hp).astype(np.float32) * np.float32(2.0 / ROPE_DIMS))
    inv_p = np.where(l < ROPE_DIMS, inv_p, 0.0)
    return jnp.asarray(np.stack([inv_r, inv_p]).astype(np.float32))


def _tables(pos_col):
    S = pos_col.shape[0]
    tm = 512
    hp = ROPE_DIMS // 2

    def body(p_ref, inv_ref, o_ref):
        p = p_ref[...].astype(F32)
        lane = lax.broadcasted_iota(jnp.int32, (tm, LANES), 1) % HEAD
        ang = p * inv_ref[0:1, :]
        c, s = jnp.cos(ang), jnp.sin(ang)
        o_ref[:, 0:128] = c
        o_ref[:, 128:256] = jnp.where(lane < 32, -s, 0.0)
        o_ref[:, 256:384] = jnp.where(lane >= 32, s, 0.0)
        ang = p * inv_ref[1:2, :]
        c, s = jnp.cos(ang), jnp.sin(ang)
        o_ref[:, 384:512] = c
        o_ref[:, 512:640] = jnp.where(lane < hp, -s, 0.0)
        o_ref[:, 640:768] = jnp.where((lane >= hp) & (lane < ROPE_DIMS), s, 0.0)

    return _pc(
        body, name="rope_tables", grid=(S // tm,),
        in_specs=[pl.BlockSpec((tm, 1), lambda i: (i, 0)), pl.BlockSpec((2, LANES), lambda i: (0, 0))],
        out_specs=pl.BlockSpec((tm, 768), lambda i: (i, 0)),
        out_shape=jax.ShapeDtypeStruct((S, 768), F32),
        compiler_params=_params(("parallel",)),
    )(pos_col, _rope_inv())


def _tab(tab_ref, which):
    o = 384 * which
    return tab_ref[:, o:o + 128], tab_ref[:, o + 128:o + 256], tab_ref[:, o + 256:o + 384]


def _rms_fwd(x, g, name):
    S, Dm = x.shape
    tm = 512

    def body(x_ref, g_ref, h_ref):
        xv = x_ref[...]
        r = lax.rsqrt(jnp.mean(xv * xv, axis=-1, keepdims=True) + EPS)
        h_ref[...] = (xv * r * g_ref[...]).astype(BF16)

    return _pc(
        body, name=name, grid=(S // tm,),
        in_specs=[pl.BlockSpec((tm, Dm), lambda i: (i, 0)), pl.BlockSpec((1, Dm), lambda i: (0, 0))],
        out_specs=pl.BlockSpec((tm, Dm), lambda i: (i, 0)),
        out_shape=jax.ShapeDtypeStruct((S, Dm), BF16),
        compiler_params=_params(("parallel",)),
    )(x, g.reshape(1, Dm))


def _rms_bwd(x, g, dh, dres, name):
    S, Dm = x.shape
    tm = 512

    def body(x_ref, g_ref, dh_ref, dres_ref, dx_ref, dxb_ref, dg_ref):
        xv, dhv = x_ref[...], dh_ref[...]
        r = lax.rsqrt(jnp.mean(xv * xv, axis=-1, keepdims=True) + EPS)
        t = dhv * g_ref[...]
        dx = dres_ref[...] + r * t - xv * (r * r * r) * jnp.mean(xv * t, axis=-1, keepdims=True)
        dx_ref[...] = dx
        dxb_ref[...] = dx.astype(BF16)

        @pl.when(pl.program_id(0) == 0)
        def _():
            dg_ref[...] = jnp.zeros_like(dg_ref)

        dg_ref[...] += jnp.sum(dhv * xv * r, axis=0, keepdims=True)

    row = pl.BlockSpec((tm, Dm), lambda i: (i, 0))
    vec = pl.BlockSpec((1, Dm), lambda i: (0, 0))
    return _pc(
        body, name=name, grid=(S // tm,),
        in_specs=[row, vec, row, row], out_specs=[row, row, vec],
        out_shape=[jax.ShapeDtypeStruct((S, Dm), F32), jax.ShapeDtypeStruct((S, Dm), BF16),
                   jax.ShapeDtypeStruct((1, Dm), F32)],
        compiler_params=_params(("arbitrary",)),
    )(x, g.reshape(1, Dm), dh, dres)


def _hn_fwd(x, gain, G):
    r = lax.rsqrt(_gmean(x * x, G) + EPS)
    return x * r * gain


def _hn_bwd(x, gain, dy, G):
    r = lax.rsqrt(_gmean(x * x, G) + EPS)
    t = dy * gain
    dx = r * t - x * (r * r * r) * _gmean(x * t, G)
    return dx, jnp.sum(dy * x * r, axis=0, keepdims=True)


def _fold_halves(v):
    return v + _roll(v, HEAD)


def _even_pre_fwd(proj, tab, qg, kg):
    S = proj.shape[0]
    tm = 256

    def body(p_ref, tab_ref, qg_ref, kg_ref, g_ref, rq_ref, rk_ref, rv_ref, dq_ref, dk_ref, dv_ref):
        Ar, Br, Cr = _tab(tab_ref, 0)
        Ap, Bp, Cp = _tab(tab_ref, 1)
        G = g_ref[...]
        for c in range(2):
            sl = slice(c * 128, (c + 1) * 128)
            rq_ref[:, sl] = _rope(p_ref[:, c * 128:(c + 1) * 128], Ar, Br, Cr, 32).astype(BF16)
            rk_ref[:, sl] = (_rope(p_ref[:, 256 + c * 128:256 + (c + 1) * 128], Ar, Br, Cr, 32) * 0.125).astype(BF16)
        rv_ref[...] = p_ref[:, 512:1024].astype(BF16)
        for c in range(4):
            sl = slice(c * 128, (c + 1) * 128)
            q = _hn_fwd(p_ref[:, 1536 + c * 128:1536 + (c + 1) * 128], qg_ref[...], G)
            dq_ref[:, sl] = _rope(q, Ap, Bp, Cp, 8).astype(BF16)
            k = _hn_fwd(p_ref[:, 2048 + c * 128:2048 + (c + 1) * 128], kg_ref[...], G)
            dk_ref[:, sl] = _rope(k, Ap, Bp, Cp, 8).astype(BF16)
        dv_ref[...] = p_ref[:, 2560:3072].astype(BF16)

    row = lambda w: pl.BlockSpec((tm, w), lambda i: (i, 0))
    vec = pl.BlockSpec((1, LANES), lambda i: (0, 0))
    return _pc(
        body, name="even_pre_fwd", grid=(S // tm,),
        in_specs=[row(3072), row(768), vec, vec, pl.BlockSpec((LANES, LANES), lambda i: (0, 0))],
        out_specs=[row(256), row(256), row(512), row(512), row(512), row(512)],
        out_shape=[jax.ShapeDtypeStruct((S, w), BF16) for w in (256, 256, 512, 512, 512, 512)],
        compiler_params=_params(("parallel",)),
    )(proj, tab, qg, kg, _group_matrix())


def _even_pre_bwd(proj, tab, qg, kg, drq, drk, drv, drg, dqs, dks, dvs):
    S = proj.shape[0]
    tm = 256
    npat = len(dqs)

    def body(p_ref, tab_ref, qg_ref, kg_ref, g_ref, drq_ref, drk_ref, drv_ref, drg_ref, *rest):
        dq_refs, dk_refs, dv_refs = rest[:npat], rest[npat:2 * npat], rest[2 * npat:3 * npat]
        dp_ref, dqg_ref, dkg_ref = rest[3 * npat:]
        Ar, Br, Cr = _tab(tab_ref, 0)
        Ap, Bp, Cp = _tab(tab_ref, 1)
        G = g_ref[...]
        for c in range(2):
            sl = slice(c * 128, (c + 1) * 128)
            dp_ref[:, c * 128:(c + 1) * 128] = _rope_t(drq_ref[:, sl], Ar, Br, Cr, 32).astype(BF16)
            dp_ref[:, 256 + c * 128:256 + (c + 1) * 128] = _rope_t(drk_ref[:, sl] * 0.125, Ar, Br, Cr, 32).astype(BF16)
        dp_ref[:, 512:1024] = drv_ref[...].astype(BF16)
        dp_ref[:, 1024:1536] = drg_ref[...].astype(BF16)
        accq = jnp.zeros((1, LANES), F32)
        acck = jnp.zeros((1, LANES), F32)
        for c in range(4):
            sl = slice(c * 128, (c + 1) * 128)
            g = dq_refs[0][:, sl]
            for r in dq_refs[1:]:
                g = g + r[:, sl]
            dx, dg = _hn_bwd(p_ref[:, 1536 + c * 128:1536 + (c + 1) * 128], qg_ref[...], _rope_t(g, Ap, Bp, Cp, 8), G)
            dp_ref[:, 1536 + c * 128:1536 + (c + 1) * 128] = dx.astype(BF16)
            accq = accq + dg
            g = dk_refs[0][:, sl]
            for r in dk_refs[1:]:
                g = g + r[:, sl]
            dx, dg = _hn_bwd(p_ref[:, 2048 + c * 128:2048 + (c + 1) * 128], kg_ref[...], _rope_t(g, Ap, Bp, Cp, 8), G)
            dp_ref[:, 2048 + c * 128:2048 + (c + 1) * 128] = dx.astype(BF16)
            acck = acck + dg
        g = dv_refs[0][...]
        for r in dv_refs[1:]:
            g = g + r[...]
        dp_ref[:, 2560:3072] = g.astype(BF16)

        @pl.when(pl.program_id(0) == 0)
        def _():
            dqg_ref[...] = jnp.zeros_like(dqg_ref)
            dkg_ref[...] = jnp.zeros_like(dkg_ref)

        dqg_ref[...] += _fold_halves(accq)
        dkg_ref[...] += _fold_halves(acck)

    row = lambda w: pl.BlockSpec((tm, w), lambda i: (i, 0))
    vec = pl.BlockSpec((1, LANES), lambda i: (0, 0))
    return _pc(
        body, name="even_pre_bwd", grid=(S // tm,),
        in_specs=[row(3072), row(768), vec, vec, pl.BlockSpec((LANES, LANES), lambda i: (0, 0)),
                  row(256), row(256), row(512), row(512)] + [row(512)] * (3 * npat),
        out_specs=[row(3072), vec, vec],
        out_shape=[jax.ShapeDtypeStruct((S, 3072), BF16), jax.ShapeDtypeStruct((1, LANES), F32),
                   jax.ShapeDtypeStruct((1, LANES), F32)],
        compiler_params=_params(("arbitrary",)),
    )(proj, tab, qg, kg, _group_matrix(), drq, drk, drv, drg, *dqs, *dks, *dvs)


def _ret_consts(pair, half):
    lg = jnp.where(pair == 0, _LOG_GAMMA[half], _LOG_GAMMA[2 + half]).astype(F32)
    i = lax.broadcasted_iota(jnp.int32, (BLK, BLK), 0)
    j = lax.broadcasted_iota(jnp.int32, (BLK, BLK), 1)
    diff = (i - j).astype(F32)
    decay = jnp.where(diff >= 0, jnp.exp(lg * jnp.maximum(diff, 0.0)), 0.0)
    t = lax.broadcasted_iota(jnp.int32, (BLK, 1), 0).astype(F32)
    xi = jnp.exp(lg * (t + 1.0))
    zeta = jnp.exp(lg * (BLK - 1.0 - t))
    cd = jnp.exp(jnp.full((1, 1), BLK, F32) * lg)
    return decay, xi, zeta, cd


def _ret_fwd(rq, rk, rv):
    S = rq.shape[0]
    nc = S // BLK

    def body(q_ref, k_ref, v_ref, o_ref, st_ref, R):
        p, n = pl.program_id(0), pl.program_id(1)

        @pl.when(n == 0)
        def _():
            R[...] = jnp.zeros_like(R)

        q2, k2 = q_ref[...], k_ref[...]
        for half in range(2):
            decay, xi, zeta, cd = _ret_consts(p, half)
            m = _head_mask((BLK, LANES), half)
            qm = jnp.where(m, q2, jnp.zeros_like(q2))
            km = jnp.where(m, k2, jnp.zeros_like(k2))
            v = v_ref[:, half * 128:(half + 1) * 128]
            Rb = R[half].astype(BF16)
            st_ref[half] = Rb
            sc = lax.dot_general(qm, k2, (((1,), (1,)), ((), ())), preferred_element_type=F32) * decay
            o = jnp.dot(sc.astype(BF16), v, preferred_element_type=F32)
            o = o + jnp.dot(qm, Rb, preferred_element_type=F32) * xi
            o_ref[:, half * 128:(half + 1) * 128] = o
            kz = (km.astype(F32) * zeta).astype(BF16)
            R[half] = R[half] * cd + lax.dot_general(kz, v, (((0,), (0,)), ((), ())), preferred_element_type=F32)

    return _pc(
        body, name="ret_fwd", grid=(2, nc),
        in_specs=[pl.BlockSpec((BLK, 128), lambda p, n: (n, p)), pl.BlockSpec((BLK, 128), lambda p, n: (n, p)),
                  pl.BlockSpec((BLK, 256), lambda p, n: (n, p))],
        out_specs=[pl.BlockSpec((BLK, 256), lambda p, n: (n, p)),
                   pl.BlockSpec((None, None, 2, 128, 128), lambda p, n: (p, n, 0, 0, 0))],
        out_shape=[jax.ShapeDtypeStruct((S, 512), F32), jax.ShapeDtypeStruct((2, nc, 2, 128, 128), BF16)],
        scratch_shapes=[pltpu.VMEM((2, 128, 128), F32)],
        compiler_params=_params(("parallel", "arbitrary")),
    )(rq, rk, rv)


def _ret_bwd(rq, rk, rv, states, do):
    S = rq.shape[0]
    nc = S // BLK

    def body(q_ref, k_ref, v_ref, st_ref, do_ref, dq_ref, dk_ref, dv_ref, U):
        p, n = pl.program_id(0), pl.program_id(1)

        @pl.when(n == 0)
        def _():
            U[...] = jnp.zeros_like(U)

        q2, k2 = q_ref[...], k_ref[...]
        dq_acc = jnp.zeros((BLK, LANES), F32)
        dk_acc = jnp.zeros((BLK, LANES), F32)
        for half in range(2):
            decay, xi, zeta, cd = _ret_consts(p, half)
            m = _head_mask((BLK, LANES), half)
            qm = jnp.where(m, q2, jnp.zeros_like(q2))
            km = jnp.where(m, k2, jnp.zeros_like(k2))
            v = v_ref[:, half * 128:(half + 1) * 128]
            dob = do_ref[:, half * 128:(half + 1) * 128].astype(BF16)
            Rb = st_ref[half]
            Ub = U[half].astype(BF16)
            nt = (((1,), (1,)), ((), ()))
            tn = (((0,), (0,)), ((), ()))
            dsc = (lax.dot_general(dob, v, nt, preferred_element_type=F32) * decay).astype(BF16)
            xdo = (dob.astype(F32) * xi).astype(BF16)
            dq_acc += jnp.dot(dsc, km, preferred_element_type=F32) + lax.dot_general(xdo, Rb, nt, preferred_element_type=F32)
            dk_acc += lax.dot_general(dsc, qm, tn, preferred_element_type=F32) \
                + lax.dot_general(v, Ub, nt, preferred_element_type=F32) * zeta
            sc = (lax.dot_general(qm, k2, nt, preferred_element_type=F32) * decay).astype(BF16)
            kz = (km.astype(F32) * zeta).astype(BF16)
            dv_ref[:, half * 128:(half + 1) * 128] = lax.dot_general(sc, dob, tn, preferred_element_type=F32) \
                + jnp.dot(kz, Ub, preferred_element_type=F32)
            U[half] = U[half] * cd + lax.dot_general(qm, xdo, tn, preferred_element_type=F32)
        dq_ref[...] = dq_acc
        dk_ref[...] = dk_acc

    rev = lambda w: pl.BlockSpec((BLK, w), lambda p, n: (nc - 1 - n, p))
    return _pc(
        body, name="ret_bwd", grid=(2, nc),
        in_specs=[rev(128), rev(128), rev(256),
                  pl.BlockSpec((None, None, 2, 128, 128), lambda p, n: (p, nc - 1 - n, 0, 0, 0)), rev(256)],
        out_specs=[rev(128), rev(128), rev(256)],
        out_shape=[jax.ShapeDtypeStruct((S, 256), F32), jax.ShapeDtypeStruct((S, 256), F32),
                   jax.ShapeDtypeStruct((S, 512), F32)],
        scratch_shapes=[pltpu.VMEM((2, 128, 128), F32)],
        compiler_params=_params(("parallel", "arbitrary")),
    )(rq, rk, rv, states, do)


def _col_of(b, m):
    return jnp.max(jnp.where(m, b, -jnp.inf), axis=1, keepdims=True)


def _attn_fwd(q, k, v, *, nq, max_dist, name, sinks=None, want_bf16=False):
    L, Ck = k.shape
    nb, ncol = L // BLK, Ck // LANES
    scale = HEAD ** -0.5
    has_sink = sinks is not None

    def body(*refs):
        q_ref, kp_ref, kc_ref, vp_ref, vc_ref = refs[:5]
        sk_ref = refs[5] if has_sink else None
        outs = refs[5 + has_sink:]
        n = pl.program_id(1)
        kcat = jnp.concatenate([kp_ref[...], kc_ref[...]], axis=0)
        vcat = jnp.concatenate([vp_ref[...], vc_ref[...]], axis=0)
        r = lax.broadcasted_iota(jnp.int32, (BLK, 2 * BLK), 0)
        c = lax.broadcasted_iota(jnp.int32, (BLK, 2 * BLK), 1)
        dist = r + BLK - c
        valid = (dist >= 0) & (dist <= max_dist) & ((c >= BLK) | (n > 0))
        for i in range(nq):
            q2 = q_ref[:, i * 128:(i + 1) * 128]
            o2 = jnp.zeros((BLK, LANES), F32)
            l2 = jnp.zeros((BLK, LANES), F32)
            for half in range(2):
                m = _head_mask((BLK, LANES), half)
                qm = jnp.where(m, q2, jnp.zeros_like(q2))
                s = lax.dot_general(qm, kcat, (((1,), (1,)), ((), ())), preferred_element_type=F32) * scale
                s = jnp.where(valid, s, -jnp.inf)
                mx = jnp.max(s, axis=1, keepdims=True)
                if has_sink:
                    snk = _col_of(sk_ref[:, i * 128:(i + 1) * 128], _head_mask((1, LANES), half))
                    mx = jnp.maximum(mx, snk)
                pr = jnp.exp(s - mx)
                den = jnp.sum(pr, axis=1, keepdims=True)
                if has_sink:
                    den = den + jnp.exp(snk - mx)
                pv = jnp.dot(pr.astype(BF16), vcat, preferred_element_type=F32) / den
                o2 = jnp.where(m, pv, o2)
                l2 = jnp.where(m, mx + jnp.log(den), l2)
            outs[0][:, i * 128:(i + 1) * 128] = o2
            outs[1][:, i * 128:(i + 1) * 128] = l2
            if want_bf16:
                outs[2][:, i * 128:(i + 1) * 128] = o2.astype(BF16)

    qspec = pl.BlockSpec((BLK, nq * 128), lambda j, n: (n, j))
    cur = pl.BlockSpec((BLK, 128), lambda j, n: (n, j))
    prev = pl.BlockSpec((BLK, 128), lambda j, n: (jnp.maximum(n - 1, 0), j))
    in_specs = [qspec, prev, cur, prev, cur]
    args = [q, k, k, v, v]
    if has_sink:
        in_specs.append(pl.BlockSpec((1, nq * 128), lambda j, n: (0, j)))
        args.append(sinks)
    out_dts = [F32, F32] + ([BF16] if want_bf16 else [])
    return _pc(
        body, name=name, grid=(ncol, nb), in_specs=in_specs,
        out_specs=[qspec] * len(out_dts),
        out_shape=[jax.ShapeDtypeStruct(q.shape, dt) for dt in out_dts],
        compiler_params=_params(("parallel", "parallel")),
    )(*args)


def _attn_bwd(q, k, v, o, lse, do, *, nq, max_dist, name, sinks=None):
    L, Ck = k.shape
    nb, ncol = L // BLK, Ck // LANES
    scale = HEAD ** -0.5
    has_sink = sinks is not None
    nt = (((1,), (1,)), ((), ()))
    tn = (((0,), (0,)), ((), ()))

    def body(*refs):
        (qc_ref, qn_ref, kp_ref, kc_ref, vp_ref, vc_ref, oc_ref, on_ref, lc_ref, ln_ref, dc_ref, dn_ref) = refs[:12]
        sk_ref = refs[12] if has_sink else None
        outs = refs[12 + has_sink:]
        dq_ref, dk_ref, dv_ref = outs[:3]
        n = pl.program_id(1)
        kc, vc = kc_ref[...], vc_ref[...]
        kcat = jnp.concatenate([kp_ref[...], kc], axis=0)
        vcat = jnp.concatenate([vp_ref[...], vc], axis=0)
        r = lax.broadcasted_iota(jnp.int32, (BLK, 2 * BLK), 0)
        c = lax.broadcasted_iota(jnp.int32, (BLK, 2 * BLK), 1)
        dist = r + BLK - c
        valid_q = (dist >= 0) & (dist <= max_dist) & ((c >= BLK) | (n > 0))
        r2 = lax.broadcasted_iota(jnp.int32, (2 * BLK, BLK), 0)
        c2 = lax.broadcasted_iota(jnp.int32, (2 * BLK, BLK), 1)
        dist2 = r2 - c2
        valid_k = (dist2 >= 0) & (dist2 <= max_dist) & ((r2 < BLK) | (n < nb - 1))
        dk_acc = jnp.zeros((BLK, LANES), F32)
        dv_acc = jnp.zeros((BLK, LANES), F32)
        for i in range(nq):
            sl = slice(i * 128, (i + 1) * 128)
            qcur, docur = qc_ref[:, sl], dc_ref[:, sl]
            qcat = jnp.concatenate([qcur, qn_ref[:, sl]], axis=0)
            docat = jnp.concatenate([docur, dn_ref[:, sl]], axis=0)
            ocat = jnp.concatenate([oc_ref[:, sl], on_ref[:, sl]], axis=0)
            lcat = jnp.concatenate([lc_ref[:, sl], ln_ref[:, sl]], axis=0)
            dq2 = jnp.zeros((BLK, LANES), F32)
            ds2 = jnp.zeros((1, LANES), F32)
            for half in range(2):
                m1 = _head_mask((BLK, LANES), half)
                m2 = _head_mask((2 * BLK, LANES), half)
                dom = jnp.where(m2, docat, 0.0)
                delta = jnp.sum(dom * ocat, axis=1, keepdims=True)
                lcol = _col_of(lcat, m2)
                domb = dom.astype(BF16)
                qmcat = jnp.where(m2, qcat, jnp.zeros_like(qcat))
                qm = qmcat[:BLK]
                s = lax.dot_general(qm, kcat, nt, preferred_element_type=F32) * scale
                pr = jnp.where(valid_q, jnp.exp(s - lcol[:BLK]), 0.0)
                dp = lax.dot_general(domb[:BLK], vcat, nt, preferred_element_type=F32)
                ds = (pr * (dp - delta[:BLK])).astype(BF16)
                dq2 = jnp.where(m1, jnp.dot(ds, kcat, preferred_element_type=F32) * scale, dq2)
                if has_sink:
                    snk = _col_of(sk_ref[:, sl], _head_mask((1, LANES), half))
                    contrib = jnp.sum(-jnp.exp(snk - lcol[:BLK]) * delta[:BLK], axis=0, keepdims=True)
                    ds2 = jnp.where(_head_mask((1, LANES), half), contrib, ds2)
                s = lax.dot_general(qmcat, kc, nt, preferred_element_type=F32) * scale
                pr = jnp.where(valid_k, jnp.exp(s - lcol), 0.0)
                dv_acc += lax.dot_general(pr.astype(BF16), domb, tn, preferred_element_type=F32)
                dp = lax.dot_general(domb, vc, nt, preferred_element_type=F32)
                ds = (pr * (dp - delta)).astype(BF16)
                dk_acc += lax.dot_general(ds, qmcat, tn, preferred_element_type=F32) * scale
            dq_ref[:, sl] = dq2
            if has_sink:
                @pl.when(n == 0)
                def _():
                    outs[3][:, sl] = jnp.zeros((1, LANES), F32)

                outs[3][:, sl] += ds2
        dk_ref[...] = dk_acc
        dv_ref[...] = dv_acc

    qcur = pl.BlockSpec((BLK, nq * 128), lambda j, n: (n, j))
    qnext = pl.BlockSpec((BLK, nq * 128), lambda j, n: (jnp.minimum(n + 1, nb - 1), j))
    cur = pl.BlockSpec((BLK, 128), lambda j, n: (n, j))
    prev = pl.BlockSpec((BLK, 128), lambda j, n: (jnp.maximum(n - 1, 0), j))
    in_specs = [qcur, qnext, prev, cur, prev, cur, qcur, qnext, qcur, qnext, qcur, qnext]
    args = [q, q, k, k, v, v, o, o, lse, lse, do, do]
    out_specs = [qcur, cur, cur]
    out_shape = [jax.ShapeDtypeStruct(q.shape, F32), jax.ShapeDtypeStruct(k.shape, F32), jax.ShapeDtypeStruct(k.shape, F32)]
    if has_sink:
        vec = pl.BlockSpec((1, nq * 128), lambda j, n: (0, j))
        in_specs.append(vec)
        args.append(sinks)
        out_specs.append(vec)
        out_shape.append(jax.ShapeDtypeStruct((1, q.shape[1]), F32))
    return _pc(
        body, name=name, grid=(ncol, nb), in_specs=in_specs, out_specs=out_specs, out_shape=out_shape,
        compiler_params=_params(("parallel", "arbitrary")),
    )(*args)


ATT_TILE = 2048


def _rows(ref, start, n, r):
    if r == 1:
        return ref[pl.ds(start, n), :]
    return ref[pl.ds(start, n, stride=r), :]


def _set_rows(ref, start, n, r, val):
    if r == 1:
        ref[pl.ds(start, n), :] = val
    else:
        ref[pl.ds(start, n, stride=r), :] = val


def _band_geometry(S, patterns):
    rmax = max(r for _, r in patterns)
    H = BLK * rmax
    T = min(S, ATT_TILE)
    assert T % H == 0 and S % T == 0
    return H, T, S // T, T // BLK


def _band_fwd(q, k, v, *, patterns, nq, name, sinks=None, want_bf16=False):
    S, Ck = k.shape
    H, T, nt, nbt = _band_geometry(S, patterns)
    ncol = Ck // LANES
    scale = HEAD ** -0.5
    has_sink = sinks is not None
    nt_dims = (((1,), (1,)), ((), ()))

    def body(*refs):
        q_ref, kp_ref, kc_ref, vp_ref, vc_ref = refs[:5]
        sk_ref = refs[5] if has_sink else None
        n_out = 3 if want_bf16 else 2
        outs = refs[5 + has_sink:5 + has_sink + n_out]
        qf, kf, vf, M, L, A = refs[5 + has_sink + n_out:]
        t = pl.program_id(1)
        kf[0:H, :] = kp_ref[...].astype(F32)
        kf[H:H + T, :] = kc_ref[...].astype(F32)
        vf[0:H, :] = vp_ref[...].astype(F32)
        vf[H:H + T, :] = vc_ref[...].astype(F32)
        r_i = lax.broadcasted_iota(jnp.int32, (BLK, 2 * BLK), 0)
        c_i = lax.broadcasted_iota(jnp.int32, (BLK, 2 * BLK), 1)
        dist_i = r_i + BLK - c_i
        masks = [_head_mask((BLK, LANES), h) for h in range(2)]

        for i in range(nq):
            qf[...] = q_ref[:, i * 128:(i + 1) * 128].astype(F32)
            for p, (dist, r) in enumerate(patterns):
                in_band = (dist_i >= 0) & (dist_i <= dist)

                def unit(j, b, p=p, r=r, in_band=in_band):
                    q0 = j + b * (BLK * r)
                    q2 = _rows(qf, q0, BLK, r).astype(BF16)
                    kcat = _rows(kf, H + q0 - BLK * r, 2 * BLK, r).astype(BF16)
                    vcat = _rows(vf, H + q0 - BLK * r, 2 * BLK, r).astype(BF16)
                    valid = in_band & ((c_i >= BLK) | ((b > 0) | (t > 0)))
                    m2 = jnp.zeros((BLK, LANES), F32)
                    l2 = jnp.zeros((BLK, LANES), F32)
                    a2 = jnp.zeros((BLK, LANES), F32)
                    for half in range(2):
                        m = masks[half]
                        qm = jnp.where(m, q2, jnp.zeros_like(q2))
                        s = lax.dot_general(qm, kcat, nt_dims, preferred_element_type=F32) * scale
                        s = jnp.where(valid, s, -jnp.inf)
                        mx = jnp.max(s, axis=1, keepdims=True)
                        pr = jnp.exp(s - mx)
                        den = jnp.sum(pr, axis=1, keepdims=True)
                        pv = jnp.dot(pr.astype(BF16), vcat, preferred_element_type=F32)
                        m2 = jnp.where(m, mx, m2)
                        l2 = jnp.where(m, den, l2)
                        a2 = jnp.where(m, pv, a2)
                    if p > 0:
                        mo = _rows(M, q0, BLK, r)
                        mn = jnp.maximum(mo, m2)
                        wa, wb = jnp.exp(mo - mn), jnp.exp(m2 - mn)
                        l2 = wa * _rows(L, q0, BLK, r) + wb * l2
                        a2 = wa * _rows(A, q0, BLK, r) + wb * a2
                        m2 = mn
                    _set_rows(M, q0, BLK, r, m2)
                    _set_rows(L, q0, BLK, r, l2)
                    _set_rows(A, q0, BLK, r, a2)

                for u in range(nbt):
                    unit(u % r, u // r)
            sl = slice(i * 128, (i + 1) * 128)
            mm, ll, aa = M[...], L[...], A[...]
            if has_sink:
                snk = sk_ref[:, sl]
                mn = jnp.maximum(mm, snk)
                w = jnp.exp(mm - mn)
                ll = ll * w + jnp.exp(snk - mn)
                aa = aa * w
                mm = mn
            o = aa / ll
            outs[0][:, sl] = o
            outs[1][:, sl] = mm + jnp.log(ll)
            if want_bf16:
                outs[2][:, sl] = o.astype(BF16)

    th = T // H
    qspec = pl.BlockSpec((T, nq * 128), lambda j, t: (t, j))
    cur = pl.BlockSpec((T, 128), lambda j, t: (t, j))
    prev = pl.BlockSpec((H, 128), lambda j, t: (jnp.maximum(t * th - 1, 0), j))
    in_specs = [qspec, prev, cur, prev, cur]
    args = [q, k, k, v, v]
    if has_sink:
        in_specs.append(pl.BlockSpec((1, nq * 128), lambda j, t: (0, j)))
        args.append(sinks)
    out_dts = [F32, F32] + ([BF16] if want_bf16 else [])
    return _pc(
        body, name=name, grid=(ncol, nt), in_specs=in_specs,
        out_specs=[qspec] * len(out_dts),
        out_shape=[jax.ShapeDtypeStruct(q.shape, dt) for dt in out_dts],
        scratch_shapes=[pltpu.VMEM((T, LANES), F32), pltpu.VMEM((H + T, LANES), F32), pltpu.VMEM((H + T, LANES), F32),
                        pltpu.VMEM((T, LANES), F32), pltpu.VMEM((T, LANES), F32), pltpu.VMEM((T, LANES), F32)],
        compiler_params=_params(("parallel", "parallel")),
    )(*args)


def _band_bwd(q, k, v, lse, delta, do, *, patterns, nq, name, sinks=None):
    S, Ck = k.shape
    H, T, nt, nbt = _band_geometry(S, patterns)
    ncol = Ck // LANES
    scale = HEAD ** -0.5
    has_sink = sinks is not None
    nt_dims = (((1,), (1,)), ((), ()))
    tn_dims = (((0,), (0,)), ((), ()))

    def body(*refs):
        (qc_ref, qn_ref, kp_ref, kc_ref, vp_ref, vc_ref, lc_ref, ln_ref, ec_ref, en_ref, dc_ref, dn_ref) = refs[:12]
        sk_ref = refs[12] if has_sink else None
        n_out = 4 if has_sink else 3
        outs = refs[12 + has_sink:12 + has_sink + n_out]
        dq_ref, dk_ref, dv_ref = outs[:3]
        qf, kf, vf, lf, ef, df = refs[12 + has_sink + n_out:]
        t = pl.program_id(1)
        kf[0:H, :] = kp_ref[...].astype(F32)
        kf[H:H + T, :] = kc_ref[...].astype(F32)
        vf[0:H, :] = vp_ref[...].astype(F32)
        vf[H:H + T, :] = vc_ref[...].astype(F32)
        dk_ref[...] = jnp.zeros_like(dk_ref)
        dv_ref[...] = jnp.zeros_like(dv_ref)
        r_i = lax.broadcasted_iota(jnp.int32, (BLK, 2 * BLK), 0)
        c_i = lax.broadcasted_iota(jnp.int32, (BLK, 2 * BLK), 1)
        dist_q = r_i + BLK - c_i
        dist_h = dist_q[:, :BLK]
        m1 = [_head_mask((BLK, LANES), h) for h in range(2)]

        def head_inputs(half, q2, do2, l2, e2):
            m = m1[half]
            lh = jnp.where(m, l2, _roll(l2, HEAD))
            eh = jnp.where(m, e2, _roll(e2, HEAD))
            return jnp.where(m, q2, jnp.zeros_like(q2)), jnp.where(m, do2, 0.0).astype(BF16), lh, eh

        for i in range(nq):
            sl = slice(i * 128, (i + 1) * 128)
            for buf, c_ref, n_ref in ((qf, qc_ref, qn_ref), (lf, lc_ref, ln_ref), (ef, ec_ref, en_ref), (df, dc_ref, dn_ref)):
                buf[0:T, :] = c_ref[:, sl].astype(F32)
                buf[T:T + H, :] = n_ref[:, sl].astype(F32)
            if has_sink:
                @pl.when(t == 0)
                def _():
                    outs[3][:, sl] = jnp.zeros((1, LANES), F32)

                outs[3][:, sl] += jnp.sum(-jnp.exp(sk_ref[:, sl] - lc_ref[:, sl]) * ec_ref[:, sl], axis=0, keepdims=True)
            for p, (dist, r) in enumerate(patterns):
                band_q = (dist_q >= 0) & (dist_q <= dist)
                band_h = (dist_h >= 0) & (dist_h <= dist)

                def add_rows(ref, start, val, r=r):
                    _set_rows(ref, start, BLK, r, _rows(ref, start, BLK, r) + val)

                def unit(j, b, p=p, r=r, band_q=band_q):
                    q0 = j + b * (BLK * r)
                    q2 = _rows(qf, q0, BLK, r).astype(BF16)
                    do2, l2, e2 = _rows(df, q0, BLK, r), _rows(lf, q0, BLK, r), _rows(ef, q0, BLK, r)
                    kcat = _rows(kf, H + q0 - BLK * r, 2 * BLK, r).astype(BF16)
                    vcat = _rows(vf, H + q0 - BLK * r, 2 * BLK, r).astype(BF16)
                    valid = band_q & ((c_i >= BLK) | ((b > 0) | (t > 0)))
                    dq2 = jnp.zeros((BLK, LANES), F32)
                    dkc = jnp.zeros((2 * BLK, LANES), F32)
                    dvc = jnp.zeros((2 * BLK, LANES), F32)
                    for half in range(2):
                        qm, dom, lh, eh = head_inputs(half, q2, do2, l2, e2)
                        s = lax.dot_general(qm, kcat, nt_dims, preferred_element_type=F32) * scale
                        pr = jnp.where(valid, jnp.exp(s - jnp.concatenate([lh, lh], axis=1)), 0.0)
                        dp = lax.dot_general(dom, vcat, nt_dims, preferred_element_type=F32)
                        ds = (pr * (dp - jnp.concatenate([eh, eh], axis=1))).astype(BF16)
                        dq2 = jnp.where(m1[half], jnp.dot(ds, kcat, preferred_element_type=F32) * scale, dq2)
                        dvc += lax.dot_general(pr.astype(BF16), dom, tn_dims, preferred_element_type=F32)
                        dkc += lax.dot_general(ds, qm, tn_dims, preferred_element_type=F32) * scale
                    if p > 0:
                        dq2 = dq2 + _rows(dq_ref.at[:, sl], q0, BLK, r)
                    _set_rows(dq_ref.at[:, sl], q0, BLK, r, dq2)
                    add_rows(dk_ref, q0, dkc[BLK:])
                    add_rows(dv_ref, q0, dvc[BLK:])
                    if b > 0:
                        add_rows(dk_ref, q0 - BLK * r, dkc[:BLK])
                        add_rows(dv_ref, q0 - BLK * r, dvc[:BLK])

                def halo_unit(j, r=r, band_h=band_h):
                    k0 = j + (nbt // r - 1) * (BLK * r)
                    q2 = _rows(qf, T + j, BLK, r).astype(BF16)
                    do2, l2, e2 = _rows(df, T + j, BLK, r), _rows(lf, T + j, BLK, r), _rows(ef, T + j, BLK, r)
                    kc = _rows(kf, H + k0, BLK, r).astype(BF16)
                    vc = _rows(vf, H + k0, BLK, r).astype(BF16)
                    dk2 = jnp.zeros((BLK, LANES), F32)
                    dv2 = jnp.zeros((BLK, LANES), F32)
                    for half in range(2):
                        qm, dom, lh, eh = head_inputs(half, q2, do2, l2, e2)
                        s = lax.dot_general(qm, kc, nt_dims, preferred_element_type=F32) * scale
                        pr = jnp.where(band_h, jnp.exp(s - lh), 0.0)
                        dp = lax.dot_general(dom, vc, nt_dims, preferred_element_type=F32)
                        ds = (pr * (dp - eh)).astype(BF16)
                        dv2 += lax.dot_general(pr.astype(BF16), dom, tn_dims, preferred_element_type=F32)
                        dk2 += lax.dot_general(ds, qm, tn_dims, preferred_element_type=F32) * scale
                    add_rows(dk_ref, k0, dk2)
                    add_rows(dv_ref, k0, dv2)

                for u in range(nbt):
                    unit(u % r, u // r)
                if nt > 1:
                    @pl.when(t < nt - 1)
                    def _(r=r, halo_unit=halo_unit):
                        for j in range(r):
                            halo_unit(j)

    th = T // H
    last = S // H - 1
    qcur = pl.BlockSpec((T, nq * 128), lambda j, t: (t, j))
    qnext = pl.BlockSpec((H, nq * 128), lambda j, t: (jnp.minimum((t + 1) * th, last), j))
    cur = pl.BlockSpec((T, 128), lambda j, t: (t, j))
    prev = pl.BlockSpec((H, 128), lambda j, t: (jnp.maximum(t * th - 1, 0), j))
    in_specs = [qcur, qnext, prev, cur, prev, cur, qcur, qnext, qcur, qnext, qcur, qnext]
    args = [q, q, k, k, v, v, lse, lse, delta, delta, do, do]
    out_specs = [qcur, cur, cur]
    out_shape = [jax.ShapeDtypeStruct(q.shape, F32), jax.ShapeDtypeStruct(k.shape, F32), jax.ShapeDtypeStruct(k.shape, F32)]
    if has_sink:
        vec = pl.BlockSpec((1, nq * 128), lambda j, t: (0, j))
        in_specs.append(vec)
        args.append(sinks)
        out_specs.append(vec)
        out_shape.append(jax.ShapeDtypeStruct((1, q.shape[1]), F32))
    big = pltpu.VMEM((T + H, LANES), F32)
    return _pc(
        body, name=name, grid=(ncol, nt), in_specs=in_specs, out_specs=out_specs, out_shape=out_shape,
        scratch_shapes=[big] * 6,
        compiler_params=_params(("parallel", "arbitrary")),
    )(*args)


def _delta(do, o, name):
    S, C = do.shape
    tm = 512

    def body(do_ref, o_ref, g_ref, e_ref):
        for c in range(C // LANES):
            sl = slice(c * 128, (c + 1) * 128)
            e_ref[:, sl] = _gmean(do_ref[:, sl] * o_ref[:, sl], g_ref[...]) * float(HEAD)

    row = pl.BlockSpec((tm, C), lambda i: (i, 0))
    return _pc(
        body, name=name, grid=(S // tm,),
        in_specs=[row, row, pl.BlockSpec((LANES, LANES), lambda i: (0, 0))], out_specs=row,
        out_shape=jax.ShapeDtypeStruct((S, C), F32),
        compiler_params=_params(("parallel",)),
    )(do, o, _group_matrix())


def _even_post_fwd(ro, proj, gn, da):
    S = ro.shape[0]
    tm = 256

    def body(ro_ref, rg_ref, gn_ref, da_ref, mix_ref):
        for c in range(4):
            sl = slice(c * 128, (c + 1) * 128)
            x = ro_ref[:, sl]
            mu = jnp.mean(x, axis=1, keepdims=True)
            xc = x - mu
            var = jnp.mean(xc * xc, axis=1, keepdims=True)
            y = xc * lax.rsqrt(var + EPS) * gn_ref[:, sl]
            z = rg_ref[:, sl]
            mix_ref[:, sl] = (z * jax.nn.sigmoid(z) * y).astype(BF16)
        mix_ref[:, 512:1024] = da_ref[...].astype(BF16)

    row = lambda w: pl.BlockSpec((tm, w), lambda i: (i, 0))
    return _pc(
        body, name="even_post_fwd", grid=(S // tm,),
        in_specs=[row(512), pl.BlockSpec((tm, 512), lambda i: (i, 2)), pl.BlockSpec((1, 512), lambda i: (0, 0)), row(512)],
        out_specs=row(1024), out_shape=jax.ShapeDtypeStruct((S, 1024), BF16),
        compiler_params=_params(("parallel",)),
    )(ro, proj, gn, da)


def _even_post_bwd(ro, proj, gn, dmixed):
    S = ro.shape[0]
    tm = 256

    def body(ro_ref, rg_ref, gn_ref, dm_ref, dro_ref, drg_ref, dgn_ref):
        @pl.when(pl.program_id(0) == 0)
        def _():
            dgn_ref[...] = jnp.zeros_like(dgn_ref)

        for c in range(4):
            sl = slice(c * 128, (c + 1) * 128)
            x = ro_ref[:, sl]
            mu = jnp.mean(x, axis=1, keepdims=True)
            xc = x - mu
            rstd = lax.rsqrt(jnp.mean(xc * xc, axis=1, keepdims=True) + EPS)
            xh = xc * rstd
            gain = gn_ref[:, sl]
            y = xh * gain
            z = rg_ref[:, sl]
            sg = jax.nn.sigmoid(z)
            dra = dm_ref[:, sl]
            drg_ref[:, sl] = dra * y * sg * (1.0 + z * (1.0 - sg))
            dy = dra * z * sg
            dgn_ref[:, sl] += jnp.sum(dy * xh, axis=0, keepdims=True)
            dxh = dy * gain
            dro_ref[:, sl] = rstd * (dxh - jnp.mean(dxh, axis=1, keepdims=True)
                                     - xh * jnp.mean(dxh * xh, axis=1, keepdims=True))

    row = lambda w: pl.BlockSpec((tm, w), lambda i: (i, 0))
    vec = pl.BlockSpec((1, 512), lambda i: (0, 0))
    return _pc(
        body, name="even_post_bwd", grid=(S // tm,),
        in_specs=[row(512), pl.BlockSpec((tm, 512), lambda i: (i, 2)), vec, row(512)],
        out_specs=[row(512), row(512), vec],
        out_shape=[jax.ShapeDtypeStruct((S, 512), F32), jax.ShapeDtypeStruct((S, 512), F32),
                   jax.ShapeDtypeStruct((1, 512), F32)],
        compiler_params=_params(("arbitrary",)),
    )(ro, proj, gn, dmixed)


def _swa_pre_fwd(proj, tab, qg, kg):
    S = proj.shape[0]
    tm = 256

    def body(p_ref, tab_ref, qg_ref, kg_ref, g_ref, q_ref, k_ref, v_ref):
        Ap, Bp, Cp = _tab(tab_ref, 1)
        G = g_ref[...]
        lo = _head_mask((tm, LANES), 0)
        for c in range(8):
            sl = slice(c * 128, (c + 1) * 128)
            q_ref[:, sl] = _rope(_hn_fwd(p_ref[:, sl], qg_ref[...], G), Ap, Bp, Cp, 8).astype(BF16)
        for c in range(2):
            kn = _rope(_hn_fwd(p_ref[:, 1024 + c * 128:1024 + (c + 1) * 128], kg_ref[...], G), Ap, Bp, Cp, 8)
            vv = p_ref[:, 1280 + c * 128:1280 + (c + 1) * 128]
            for t, ref in ((kn, k_ref), (vv, v_ref)):
                sw = _roll(t, HEAD)
                ref[:, (2 * c) * 128:(2 * c + 1) * 128] = jnp.where(lo, t, sw).astype(BF16)
                ref[:, (2 * c + 1) * 128:(2 * c + 2) * 128] = jnp.where(lo, sw, t).astype(BF16)

    row = lambda w: pl.BlockSpec((tm, w), lambda i: (i, 0))
    vec = pl.BlockSpec((1, LANES), lambda i: (0, 0))
    return _pc(
        body, name="swa_pre_fwd", grid=(S // tm,),
        in_specs=[row(1536), row(768), vec, vec, pl.BlockSpec((LANES, LANES), lambda i: (0, 0))],
        out_specs=[row(1024), row(512), row(512)],
        out_shape=[jax.ShapeDtypeStruct((S, w), BF16) for w in (1024, 512, 512)],
        compiler_params=_params(("parallel",)),
    )(proj, tab, qg, kg, _group_matrix())


def _swa_pre_bwd(proj, tab, qg, kg, dq, dk, dv):
    S = proj.shape[0]
    tm = 256

    def body(p_ref, tab_ref, qg_ref, kg_ref, g_ref, dq_ref, dk_ref, dv_ref, dp_ref, db_ref, dqg_ref, dkg_ref):
        Ap, Bp, Cp = _tab(tab_ref, 1)
        G = g_ref[...]
        lo = _head_mask((tm, LANES), 0)

        @pl.when(pl.program_id(0) == 0)
        def _():
            db_ref[...] = jnp.zeros_like(db_ref)
            dqg_ref[...] = jnp.zeros_like(dqg_ref)
            dkg_ref[...] = jnp.zeros_like(dkg_ref)

        accq = jnp.zeros((1, LANES), F32)
        acck = jnp.zeros((1, LANES), F32)
        for c in range(8):
            sl = slice(c * 128, (c + 1) * 128)
            dx, dg = _hn_bwd(p_ref[:, sl], qg_ref[...], _rope_t(dq_ref[:, sl], Ap, Bp, Cp, 8), G)
            dp_ref[:, sl] = dx.astype(BF16)
            db_ref[:, sl] += jnp.sum(dx, axis=0, keepdims=True)
            accq = accq + dg
        for c in range(2):
            folded = []
            for ref in (dk_ref, dv_ref):
                a = ref[:, (2 * c) * 128:(2 * c + 1) * 128]
                b = ref[:, (2 * c + 1) * 128:(2 * c + 2) * 128]
                folded.append(jnp.where(lo, a + _roll(a, HEAD), b + _roll(b, HEAD)))
            ks = slice(1024 + c * 128, 1024 + (c + 1) * 128)
            dx, dg = _hn_bwd(p_ref[:, ks], kg_ref[...], _rope_t(folded[0], Ap, Bp, Cp, 8), G)
            dp_ref[:, ks] = dx.astype(BF16)
            db_ref[:, ks] += jnp.sum(dx, axis=0, keepdims=True)
            acck = acck + dg
            vs = slice(1280 + c * 128, 1280 + (c + 1) * 128)
            dp_ref[:, vs] = folded[1].astype(BF16)
            db_ref[:, vs] += jnp.sum(folded[1], axis=0, keepdims=True)
        dqg_ref[...] += _fold_halves(accq)
        dkg_ref[...] += _fold_halves(acck)

    row = lambda w: pl.BlockSpec((tm, w), lambda i: (i, 0))
    vec = pl.BlockSpec((1, LANES), lambda i: (0, 0))
    return _pc(
        body, name="swa_pre_bwd", grid=(S // tm,),
        in_specs=[row(1536), row(768), vec, vec, pl.BlockSpec((LANES, LANES), lambda i: (0, 0)),
                  row(1024), row(512), row(512)],
        out_specs=[row(1536), pl.BlockSpec((1, 1536), lambda i: (0, 0)), vec, vec],
        out_shape=[jax.ShapeDtypeStruct((S, 1536), BF16), jax.ShapeDtypeStruct((1, 1536), F32),
                   jax.ShapeDtypeStruct((1, LANES), F32), jax.ShapeDtypeStruct((1, LANES), F32)],
        compiler_params=_params(("arbitrary",)),
    )(proj, tab, qg, kg, _group_matrix(), dq, dk, dv)


def _loss_head(y, target):
    S, Dm = y.shape
    tm = 512

    def body(y_ref, t_ref, l_ref, dy_ref, dyb_ref):
        @pl.when(pl.program_id(0) == 0)
        def _():
            l_ref[...] = jnp.zeros_like(l_ref)

        e = y_ref[...] - t_ref[...]
        dy = e * (1.0 / Dm)
        dy_ref[...] = dy
        dyb_ref[...] = dy.astype(BF16)
        row = jnp.sum(e * e, axis=1, keepdims=True) * (0.5 / Dm)
        l_ref[...] += jnp.sum(row, axis=0, keepdims=True)

    row = pl.BlockSpec((tm, Dm), lambda i: (i, 0))
    return _pc(
        body, name="loss_head", grid=(S // tm,), in_specs=[row, row],
        out_specs=[pl.BlockSpec((1, LANES), lambda i: (0, 0)), row, row],
        out_shape=[jax.ShapeDtypeStruct((1, LANES), F32), jax.ShapeDtypeStruct((S, Dm), F32),
                   jax.ShapeDtypeStruct((S, Dm), BF16)],
        compiler_params=_params(("arbitrary",)),
    )(y, target)


def _relu2_of(u):
    r = jnp.maximum(u.astype(F32), 0.0)
    return r * r


def _drelu2(acc, u):
    return (acc * 2.0 * jnp.maximum(u.astype(F32), 0.0),)


def _add(acc, res):
    return (acc + res,)


_T = dict(tm=1024, tn=1024, tk=1024)


def _mlp_fwd(x, g, w_up, w_dn, tag):
    h = _rms_fwd(x, g, f"rms_mlp_fwd{tag}")
    u = _matmul(h, w_up, dims="nn", **_T, outs=[BF16], b_cs=True, name=f"mlp_up{tag}")
    x_out = _matmul(u, w_dn, dims="nn", **_T, outs=[F32], epilogue=_add, extras=[(x, "mn")], a_pro=_relu2_of,
                    name=f"mlp_down{tag}")
    return x_out, (h, u)


def _mlp_bwd(x, g, w_up, w_dn, saved, dy, dyb, tag):
    h, u = saved
    du = _matmul(dyb, w_dn, dims="nt", **_T, outs=[BF16], epilogue=_drelu2, extras=[(u, "mn")], name=f"mlp_du{tag}")
    dw_dn = _matmul(u, dyb, dims="tn", **_T, outs=[F32], a_pro=_relu2_of, name=f"mlp_dwdown{tag}")
    dw_up = _matmul(h, du, dims="tn", **_T, outs=[F32], o_cs=N_CHIPS, name=f"mlp_dwup{tag}")
    dh = _matmul(du, w_up, dims="nt", **_T, outs=[F32], b_cs=True, name=f"mlp_dh{tag}")
    dx, dxb, dg = _rms_bwd(x, g, dh, dy, f"rms_mlp_bwd{tag}")
    return dx, dxb, dg, dw_up, dw_dn


def _pattern_view(t, r):
    S, C = t.shape
    return t.reshape(S // r, r * C)


def _local_step(x, pos_col, target, W, rest_of, P):
    S = x.shape[0]
    tab = _tables(pos_col)
    tile2 = lambda g: jnp.tile(g.reshape(1, HEAD), (1, 2))
    dqg, dkg = tile2(P["dil_q_gain"]), tile2(P["dil_k_gain"])
    sqg, skg = tile2(P["swa_q_gain"]), tile2(P["swa_k_gain"])
    gn = P["ret_gn_gain"].reshape(1, 512)
    sink_b = jnp.repeat(P["swa_sinks"].reshape(16), HEAD).reshape(1, 1024)

    h0 = _rms_fwd(x, P["norm_mix"][0], "rms_mix_fwd0")
    proj = _matmul(h0, W["hyb_w_in"], dims="nn", tm=1024, tn=768, tk=1024, outs=[F32], b_cs=True, name="hyb_in")
    rq, rk, rv, dq, dk, dv = _even_pre_fwd(proj, tab, dqg, dkg)
    ro, states = _ret_fwd(rq, rk, rv)
    dil = [(w // r, r) for w, r in DIL_PATTERNS]
    da, dlse = _band_fwd(dq, dk, dv, patterns=dil, nq=1, name="dil_fwd")
    mixed = _even_post_fwd(ro, proj, gn, da)
    x1 = _matmul(mixed, W["hyb_w_out"], dims="nn", **_T, outs=[F32], epilogue=_add, extras=[(x, "mn")], name="hyb_out")
    rest, bias = rest_of(x1)
    W = {**W, **rest}
    x2, mlp0 = _mlp_fwd(x1, P["norm_mlp"][0], W["mlp_w_up"][0], W["mlp_w_down"][0], "0")

    h2 = _rms_fwd(x2, P["norm_mix"][1], "rms_mix_fwd1")
    proj2 = _matmul(h2, W["swa_w_qkv"], dims="nn", tm=1024, tn=384, tk=1024, outs=[F32], b_cs=True,
                    epilogue=_add, extras=[(bias.reshape(1, 1536), "n")], name="swa_qkv")
    sq, sk, sv = _swa_pre_fwd(proj2, tab, sqg, skg)
    swa = [(SWA_DIST, 1)]
    so, slse, so_b = _band_fwd(sq, sk, sv, patterns=swa, nq=2, name="swa_fwd", sinks=sink_b, want_bf16=True)
    x3 = _matmul(so_b, W["swa_w_out"], dims="nn", **_T, outs=[F32], epilogue=_add, extras=[(x2, "mn")], name="swa_out")
    y, mlp1 = _mlp_fwd(x3, P["norm_mlp"][1], W["mlp_w_up"][1], W["mlp_w_down"][1], "1")
    loss, dy, dyb = _loss_head(y, target)

    gw, gp = {}, {}
    dx3, dx3b, dg_mlp1, gw["mlp_w_up1"], gw["mlp_w_down1"] = _mlp_bwd(x3, P["norm_mlp"][1], W["mlp_w_up"][1],
                                                                       W["mlp_w_down"][1], mlp1, dy, dyb, "1")
    gw["swa_w_out"] = _matmul(so_b, dx3b, dims="tn", **_T, outs=[F32], name="swa_dwout")
    dso = _matmul(dx3b, W["swa_w_out"], dims="nt", **_T, outs=[F32], name="swa_do")
    dsq, dsk, dsv, dsink = _band_bwd(sq, sk, sv, slse, _delta(dso, so, "swa_delta"), dso, patterns=swa, nq=2,
                                     name="swa_bwd", sinks=sink_b)
    dproj2, gp["swa_b_qkv"], gp["swa_q_gain"], gp["swa_k_gain"] = _swa_pre_bwd(proj2, tab, sqg, skg, dsq, dsk, dsv)
    gp["swa_sinks"] = dsink
    gw["swa_w_qkv"] = _matmul(h2, dproj2, dims="tn", tm=1024, tn=384, tk=1024, outs=[F32], o_cs=N_CHIPS, name="swa_dwqkv")
    dh2 = _matmul(dproj2, W["swa_w_qkv"], dims="nt", tm=1024, tn=1024, tk=384, outs=[F32], b_cs=True, name="swa_dh")
    dx2, dx2b, dg_mix1 = _rms_bwd(x2, P["norm_mix"][1], dh2, dx3, "rms_mix_bwd1")

    dx1, dx1b, dg_mlp0, gw["mlp_w_up0"], gw["mlp_w_down0"] = _mlp_bwd(x1, P["norm_mlp"][0], W["mlp_w_up"][0],
                                                                       W["mlp_w_down"][0], mlp0, dx2, dx2b, "0")
    gw["hyb_w_out"] = _matmul(mixed, dx1b, dims="tn", **_T, outs=[F32], name="hyb_dwout")
    dmixed = _matmul(dx1b, W["hyb_w_out"], dims="nt", **_T, outs=[F32], name="hyb_dmixed")
    dro, drg, gp["ret_gn_gain"] = _even_post_bwd(ro, proj, gn, dmixed)
    drq, drk, drv = _ret_bwd(rq, rk, rv, states, dro)
    dda = dmixed[:, 512:]
    ddq, ddk, ddv = _band_bwd(dq, dk, dv, dlse, _delta(dda, da, "dil_delta"), dda, patterns=dil, nq=1, name="dil_bwd")
    dproj, gp["dil_q_gain"], gp["dil_k_gain"] = _even_pre_bwd(proj, tab, dqg, dkg, drq, drk, drv, drg, [ddq], [ddk], [ddv])
    gw["hyb_w_in"] = _matmul(h0, dproj, dims="tn", tm=1024, tn=768, tk=1024, outs=[F32], o_cs=N_CHIPS, name="hyb_dwin")
    dh0 = _matmul(dproj, W["hyb_w_in"], dims="nt", tm=1024, tn=1024, tk=768, outs=[F32], b_cs=True, name="hyb_dh")
    grad_x, _, dg_mix0 = _rms_bwd(x, P["norm_mix"][0], dh0, dx1, "rms_mix_bwd0")
    gp["norm_mix"] = jnp.concatenate([dg_mix0, dg_mix1], axis=0)
    gp["norm_mlp"] = jnp.concatenate([dg_mlp0, dg_mlp1], axis=0)
    return loss, grad_x, gw, gp


HBM = pl.BlockSpec(memory_space=pltpu.HBM)


def _place():
    x, y, c = lax.axis_index("x"), lax.axis_index("y"), lax.axis_index("c")
    chips = [(1 - x, y), (x, 1 - y), (1 - x, 1 - y)]
    return x, y, c, chips


def _allgather_shards(buf):
    _, R, Wd = buf.shape
    Rh = R // 2

    def body(b_ref, out_ref, send_sems, recv_sems):
        x, y, c, chips = _place()
        sibling = (x, y, 1 - c)

        def copy(k, chip, core, to):
            block = b_ref.at[2 * chip[0] + chip[1], pl.ds(core * Rh, Rh), :]
            return pltpu.make_async_remote_copy(
                src_ref=block, dst_ref=block, send_sem=send_sems.at[k], recv_sem=recv_sems.at[k],
                device_id=to, device_id_type=MESH)

        first = [copy(k, (x, y), c, (*chip, c)) for k, chip in enumerate(chips)]
        for cp in first:
            cp.start()
        passed = [copy(3 + k, chip, c, sibling) for k, chip in enumerate(chips)]
        for k, chip in enumerate(chips):
            copy(k, chip, c, (x, y, c)).wait_recv()
            passed[k].start()
        for k, chip in enumerate(chips):
            copy(3 + k, chip, 1 - c, (x, y, c)).wait_recv()
        for cp in first + passed:
            cp.wait_send()

    return _pc(
        body, name="allgather_first", in_specs=[HBM], out_specs=HBM,
        out_shape=jax.ShapeDtypeStruct(buf.shape, buf.dtype), input_output_aliases={0: 0},
        scratch_shapes=[pltpu.SemaphoreType.DMA((6,)), pltpu.SemaphoreType.DMA((6,))],
    )(buf)


SEM = pl.BlockSpec(memory_space=pltpu.SEMAPHORE)
EFFECT = pltpu.SideEffectType.DATAFLOW_SIDE_EFFECTING


def _half_block(ref, chip, core):
    rh = ref.shape[1] // 2
    return ref.at[2 * chip[0] + chip[1], pl.ds(core * rh, rh), :]


def _gather_start(buf, ride):
    def body(b_ref, ride_ref, s0, s1, s2, r0, r1, r2, b_out, ride_out):
        x, y, c, chips = _place()
        for chip, s, r in zip(chips, (s0, s1, s2), (r0, r1, r2)):
            mine = _half_block(b_ref, (x, y), c)
            pltpu.make_async_remote_copy(src_ref=mine, dst_ref=mine, send_sem=s, recv_sem=r,
                                         device_id=(*chip, c), device_id_type=MESH).start()

    sem = pltpu.SemaphoreType.DMA(())
    return _pc(
        body, name="allgather_rest_start",
        out_shape=(sem,) * 6 + (pltpu.HBM(buf.shape, buf.dtype), pltpu.HBM(ride.shape, ride.dtype)),
        in_specs=(HBM, HBM), out_specs=(SEM,) * 6 + (HBM, HBM), input_output_aliases={0: 6, 1: 7},
        compiler_params=pltpu.CompilerParams(has_side_effects=EFFECT),
    )(pltpu.with_memory_space_constraint(buf, pltpu.HBM), pltpu.with_memory_space_constraint(ride, pltpu.HBM))


def _gather_wait(buf, sems, after):
    def body(b_ref, s0, s1, s2, r0, r1, r2, after_ref, b_out):
        x, y, c, chips = _place()
        for chip, s, r in zip(chips, (s0, s1, s2), (r0, r1, r2)):
            cp = pltpu.make_async_remote_copy(src_ref=_half_block(b_ref, (x, y), c), dst_ref=_half_block(b_ref, chip, c),
                                              send_sem=s, recv_sem=r, device_id=(*chip, c), device_id_type=MESH)
            cp.wait_send()
            cp.wait_recv()

    return _pc(
        body, name="allgather_rest_wait", out_shape=pltpu.HBM(buf.shape, buf.dtype),
        in_specs=(HBM,) + (SEM,) * 6 + (pl.BlockSpec(memory_space=pl.ANY),), out_specs=HBM, input_output_aliases={0: 0},
        compiler_params=pltpu.CompilerParams(has_side_effects=EFFECT),
    )(buf, *sems, after)


def _gather_handover(buf):
    def body(b_ref, out_ref, send_sems, recv_sems):
        x, y, c, chips = _place()
        cps = []
        for k, chip in enumerate(chips):
            mine = _half_block(b_ref, chip, c)
            cps.append(pltpu.make_async_remote_copy(src_ref=mine, dst_ref=mine, send_sem=send_sems.at[k],
                                                    recv_sem=recv_sems.at[k], device_id=(x, y, 1 - c), device_id_type=MESH))
        for cp in cps:
            cp.start()
        for k, chip in enumerate(chips):
            theirs = _half_block(b_ref, chip, 1 - c)
            pltpu.make_async_remote_copy(src_ref=theirs, dst_ref=theirs, send_sem=send_sems.at[k], recv_sem=recv_sems.at[k],
                                         device_id=(x, y, 1 - c), device_id_type=MESH).wait_recv()
        for cp in cps:
            cp.wait_send()

    return _pc(
        body, name="allgather_rest_handover", in_specs=[HBM], out_specs=HBM,
        out_shape=jax.ShapeDtypeStruct(buf.shape, buf.dtype), input_output_aliases={0: 0},
        scratch_shapes=[pltpu.SemaphoreType.DMA((3,)), pltpu.SemaphoreType.DMA((3,))],
    )(buf)


def _swap_halves(ts):
    nt = len(ts)

    def body(*refs):
        t_refs, l_refs, send_sems, recv_sems = refs[:nt], refs[nt:2 * nt], refs[-2], refs[-1]
        x, y, c, _ = _place()
        cps = []
        for k in range(nt):
            rh = t_refs[k].shape[1] // 2
            cps.append(pltpu.make_async_remote_copy(
                src_ref=t_refs[k].at[:, pl.ds((1 - c) * rh, rh), :], dst_ref=l_refs[k],
                send_sem=send_sems.at[k], recv_sem=recv_sems.at[k], device_id=(x, y, 1 - c), device_id_type=MESH))
        for cp in cps:
            cp.start()
        for cp in cps:
            cp.wait()

    return _pc(
        body, name="grad_swap_halves", in_specs=[HBM] * nt, out_specs=[HBM] * nt,
        out_shape=[jax.ShapeDtypeStruct((t.shape[0], t.shape[1] // 2, t.shape[2]), F32) for t in ts],
        scratch_shapes=[pltpu.SemaphoreType.DMA((nt,)), pltpu.SemaphoreType.DMA((nt,))],
    )(*ts)


def _pair_sum(t, l, place, name):
    _, r, cols = t.shape
    rh = r // 2
    tr = min(rh, 256)
    nr = rh // tr

    def body(pl_ref, t_ref, l_ref, o_ref):
        o_ref[...] = (t_ref[...] + l_ref[...]).astype(BF16)

    return _pc(
        body, name=name,
        grid_spec=pltpu.PrefetchScalarGridSpec(
            num_scalar_prefetch=1, grid=(N_CHIPS, nr),
            in_specs=[pl.BlockSpec((None, tr, cols), lambda s, i, p: (s, p[1] * nr + i, 0)),
                      pl.BlockSpec((None, tr, cols), lambda s, i, p: (s, i, 0))],
            out_specs=pl.BlockSpec((None, tr, cols), lambda s, i, p: (s, i, 0))),
        out_shape=jax.ShapeDtypeStruct((N_CHIPS, rh, cols), BF16),
        compiler_params=_params(("parallel", "parallel")),
    )(place, t, l)


def _exchange_chips(ps):
    nt = len(ps)

    def body(*refs):
        p_refs, r_refs, send_sems, recv_sems = refs[:nt], refs[nt:2 * nt], refs[-2], refs[-1]
        x, y, c, chips = _place()
        cps = []
        for t in range(nt):
            for k, chip in enumerate(chips):
                cps.append(pltpu.make_async_remote_copy(
                    src_ref=p_refs[t].at[2 * chip[0] + chip[1]], dst_ref=r_refs[t].at[k],
                    send_sem=send_sems.at[3 * t + k], recv_sem=recv_sems.at[3 * t + k],
                    device_id=(*chip, c), device_id_type=MESH))
        for cp in cps:
            cp.start()
        for cp in cps:
            cp.wait()

    return _pc(
        body, name="grad_exchange_chips", in_specs=[HBM] * nt, out_specs=[HBM] * nt,
        out_shape=[jax.ShapeDtypeStruct((3,) + p.shape[1:], BF16) for p in ps],
        scratch_shapes=[pltpu.SemaphoreType.DMA((3 * nt,)), pltpu.SemaphoreType.DMA((3 * nt,))],
    )(*ps)


def _final_sum(t, l, rcv, place, name):
    _, r, cols = t.shape
    rh = r // 2
    tr = min(rh, 256)
    nr = rh // tr

    def body(pl_ref, t_ref, l_ref, r_ref, o_ref):
        acc = t_ref[...] + l_ref[...]
        for k in range(3):
            acc = acc + r_ref[k].astype(F32)
        o_ref[...] = acc

    return _pc(
        body, name=name,
        grid_spec=pltpu.PrefetchScalarGridSpec(
            num_scalar_prefetch=1, grid=(nr,),
            in_specs=[pl.BlockSpec((None, tr, cols), lambda i, p: (p[0], p[1] * nr + i, 0)),
                      pl.BlockSpec((None, tr, cols), lambda i, p: (p[0], i, 0)),
                      pl.BlockSpec((3, tr, cols), lambda i, p: (0, i, 0))],
            out_specs=pl.BlockSpec((tr, cols), lambda i, p: (p[1] * nr + i, 0))),
        out_shape=jax.ShapeDtypeStruct((r, cols), F32),
        compiler_params=_params(("parallel",)),
    )(place, t, l, rcv)


def _share_halves(hs):
    nt = len(hs)

    def body(*refs):
        h_refs, send_sems, recv_sems = refs[:nt], refs[-2], refs[-1]
        x, y, c, _ = _place()
        cps = []
        for k in range(nt):
            rh = h_refs[k].shape[0] // 2
            half = h_refs[k].at[pl.ds(c * rh, rh), :]
            cps.append(pltpu.make_async_remote_copy(
                src_ref=half, dst_ref=half, send_sem=send_sems.at[k], recv_sem=recv_sems.at[k],
                device_id=(x, y, 1 - c), device_id_type=MESH))
        for cp in cps:
            cp.start()
        for cp in cps:
            cp.wait()

    return _pc(
        body, name="grad_share_halves", in_specs=[HBM] * nt, out_specs=[HBM] * nt,
        out_shape=[jax.ShapeDtypeStruct(h.shape, F32) for h in hs],
        input_output_aliases={k: k for k in range(nt)},
        scratch_shapes=[pltpu.SemaphoreType.DMA((nt,)), pltpu.SemaphoreType.DMA((nt,))],
    )(*hs)


def _allgather_small(v):
    rows = v.shape[0]

    def body(v_ref, out_ref, send_sems, recv_sems):
        x, y, c, _ = _place()
        me = 4 * x + 2 * y + c
        out_ref[me] = v_ref[...]
        cps = []
        for k in range(1, 8):
            fx, fy, fc = (k >> 2) & 1, (k >> 1) & 1, k & 1
            to = (1 - x if fx else x, 1 - y if fy else y, 1 - c if fc else c)
            cps.append(pltpu.make_async_remote_copy(
                src_ref=v_ref, dst_ref=out_ref.at[me], send_sem=send_sems.at[k - 1], recv_sem=recv_sems.at[k - 1],
                device_id=to, device_id_type=MESH))
        for cp in cps:
            cp.start()
        for cp in cps:
            cp.wait()

    return _pc(
        body, name="allgather_small",
        in_specs=[pl.BlockSpec(memory_space=pltpu.VMEM)], out_specs=pl.BlockSpec(memory_space=pltpu.VMEM),
        out_shape=jax.ShapeDtypeStruct((8, rows, LANES), F32),
        scratch_shapes=[pltpu.SemaphoreType.DMA((7,)), pltpu.SemaphoreType.DMA((7,))],
    )(v)


def _adamw_math(w, g, m, v):
    m = ADAM_B1 * m + (1.0 - ADAM_B1) * g
    v = ADAM_B2 * v + (1.0 - ADAM_B2) * (g * g)
    m_hat = m / (1.0 - ADAM_B1 ** ADAM_STEP)
    v_hat = v / (1.0 - ADAM_B2 ** ADAM_STEP)
    return -ADAM_LR * (m_hat / (jnp.sqrt(v_hat) + ADAM_EPS) + ADAM_WD * w), m, v


def _adamw(w, g, m, v, name):
    r, cols = w.shape
    tr = min(r, 256)

    def body(w_ref, g_ref, m_ref, v_ref, d_ref, mo_ref, vo_ref):
        d, mn, vn = _adamw_math(w_ref[...], g_ref[...], m_ref[...], v_ref[...])
        d_ref[...] = d
        mo_ref[...] = mn
        vo_ref[...] = vn

    row = pl.BlockSpec((tr, cols), lambda i: (i, 0))
    return _pc(
        body, name=name, grid=(r // tr,), in_specs=[row] * 4, out_specs=[row] * 3,
        out_shape=[jax.ShapeDtypeStruct((r, cols), F32)] * 3,
        compiler_params=_params(("parallel",)),
    )(w, g, m, v)


def _adamw_small(w, gathered, m, v):
    rows = w.shape[0]

    def body(w_ref, g_ref, m_ref, v_ref, go_ref, d_ref, mo_ref, vo_ref):
        g = g_ref[0]
        for k in range(1, 8):
            g = g + g_ref[k]
        d, mn, vn = _adamw_math(w_ref[...], g, m_ref[...], v_ref[...])
        go_ref[...] = g
        d_ref[...] = d
        mo_ref[...] = mn
        vo_ref[...] = vn

    return _pc(
        body, name="adamw_small",
        out_shape=[jax.ShapeDtypeStruct((rows, LANES), F32)] * 4,
    )(w, gathered, m, v)


_BIAS_ROWS = 32


def _own_slot(flat, chip):
    return lax.dynamic_update_slice(jnp.zeros((N_CHIPS,) + flat.shape, flat.dtype), flat[None], (chip, 0, 0))


def _pack_first(hyb_w_in, hyb_w_out):
    return jnp.concatenate([t.astype(BF16).reshape(-1, 1024) for t in (hyb_w_in, hyb_w_out)], axis=0)


def _unpack_first(g):
    return {"hyb_w_in": g[:, 0:768, :].reshape(N_CHIPS, 1024, 768), "hyb_w_out": g[:, 768:1024, :].reshape(1024, 1024)}


def _pack_rest(mlp_w_up, mlp_w_down, swa_w_qkv, swa_w_out, swa_b_qkv):
    parts = [t.astype(BF16).reshape(-1, 1024) for t in (mlp_w_up, mlp_w_down, swa_w_qkv, swa_w_out)]
    bias = lax.bitcast_convert_type(swa_b_qkv.reshape(384), BF16).reshape(1, 768)
    bias = jnp.pad(bias, ((0, _BIAS_ROWS - 1), (0, 256)))
    return jnp.concatenate(parts + [bias], axis=0)


def _unpack_rest(g):
    W = {
        "mlp_w_up": [g[:, l * 1024:(l + 1) * 1024, :] for l in range(2)],
        "mlp_w_down": [g[:, 2048 + l * 1024:2048 + (l + 1) * 1024, :].reshape(D_FF, D_MODEL) for l in range(2)],
        "swa_w_qkv": g[:, 4096:4480, :].reshape(N_CHIPS, 1024, 384),
        "swa_w_out": g[:, 4480:4736, :].reshape(1024, 1024),
    }
    bias = lax.bitcast_convert_type(g[:, 4736, :768].reshape(N_CHIPS, 384, 2), F32).reshape(1536)
    return W, bias


_SMALL = (("norm_mix", 16), ("norm_mlp", 16), ("ret_gn_gain", 4), ("dil_q_gain", 1), ("dil_k_gain", 1),
          ("swa_b_qkv", 12), ("swa_q_gain", 1), ("swa_k_gain", 1), ("swa_sinks", 1))
_SUBLANES = 8


def _slot(r):
    return -(-r // _SUBLANES) * _SUBLANES


def _pack_small(d):
    return jnp.concatenate([jnp.pad(d[n].reshape(r, LANES), ((0, _slot(r) - r), (0, 0))) for n, r in _SMALL], axis=0)


def _unpack_small(p):
    out, o = {}, 0
    for n, r in _SMALL:
        out[n] = p[o:o + r]
        o += _slot(r)
    return out


def kernel(x, positions, norm_mix, norm_mlp, mlp_w_up, mlp_w_down, hyb_w_in, hyb_w_out, ret_gn_gain, dil_q_gain, dil_k_gain, swa_w_qkv, swa_b_qkv, swa_w_out, swa_q_gain, swa_k_gain, swa_sinks, loss_target, m_norm_mix, m_norm_mlp, m_mlp_w_up, m_mlp_w_down, m_hyb_w_in, m_hyb_w_out, m_ret_gn_gain, m_dil_q_gain, m_dil_k_gain, m_swa_w_qkv, m_swa_b_qkv, m_swa_w_out, m_swa_q_gain, m_swa_k_gain, m_swa_sinks, v_norm_mix, v_norm_mlp, v_mlp_w_up, v_mlp_w_down, v_hyb_w_in, v_hyb_w_out, v_ret_gn_gain, v_dil_q_gain, v_dil_k_gain, v_swa_w_qkv, v_swa_b_qkv, v_swa_w_out, v_swa_q_gain, v_swa_k_gain, v_swa_sinks):
    ax, ay, ac = lax.axis_index("x"), lax.axis_index("y"), lax.axis_index("c")
    chip = 2 * ax + ay
    place = jnp.stack([chip, ac]).astype(jnp.int32)
    S = x.shape[1]

    first = _allgather_shards(_own_slot(_pack_first(hyb_w_in[0], hyb_w_out[0]), chip))
    rest = _own_slot(_pack_rest(mlp_w_up, mlp_w_down, swa_w_qkv[0], swa_w_out[0], swa_b_qkv[0]), chip)
    *sems, rest, first = _gather_start(rest, first)

    def rest_of(after):
        return _unpack_rest(_gather_handover(_gather_wait(rest, sems, after)))

    P = dict(norm_mix=norm_mix, norm_mlp=norm_mlp, ret_gn_gain=ret_gn_gain, dil_q_gain=dil_q_gain, dil_k_gain=dil_k_gain,
             swa_q_gain=swa_q_gain, swa_k_gain=swa_k_gain, swa_sinks=swa_sinks)

    loss_l, grad_x, gw, gp = _local_step(x[0], positions.reshape(S, 1), loss_target[0], _unpack_first(first), rest_of, P)
    loss = lax.psum(loss_l[0, 0], ("x", "y", "c"))

    slab = lambda t, r: t.reshape(N_CHIPS, r, t.size // (N_CHIPS * r))
    names = ["mlp_w_up0", "mlp_w_up1", "mlp_w_down0", "mlp_w_down1", "hyb_w_in", "hyb_w_out", "swa_w_qkv", "swa_w_out"]
    rows = [1024, 1024, 1024, 1024, 1024, 256, 1024, 256]
    ts = [slab(gw[n], r) for n, r in zip(names, rows)]
    ls = _swap_halves(ts)
    ps = [_pair_sum(t, l, place, f"pair_sum_{n}") for t, l, n in zip(ts, ls, names)]
    rs = _exchange_chips(ps)
    hs = [_final_sum(t, l, r, place, f"final_sum_{n}") for t, l, r, n in zip(ts, ls, rs, names)]
    gs = dict(zip(names, _share_halves(hs)))
    shards = dict(mlp_w_up0=(mlp_w_up[0], m_mlp_w_up[0], v_mlp_w_up[0]), mlp_w_up1=(mlp_w_up[1], m_mlp_w_up[1], v_mlp_w_up[1]),
                  mlp_w_down0=(mlp_w_down[0], m_mlp_w_down[0], v_mlp_w_down[0]),
                  mlp_w_down1=(mlp_w_down[1], m_mlp_w_down[1], v_mlp_w_down[1]),
                  hyb_w_in=(hyb_w_in[0], m_hyb_w_in[0], v_hyb_w_in[0]), hyb_w_out=(hyb_w_out[0], m_hyb_w_out[0], v_hyb_w_out[0]),
                  swa_w_qkv=(swa_w_qkv[0], m_swa_w_qkv[0], v_swa_w_qkv[0]), swa_w_out=(swa_w_out[0], m_swa_w_out[0], v_swa_w_out[0]))
    big = {}
    for n in names:
        w, m, v = shards[n]
        big[n] = (gs[n],) + tuple(_adamw(w, gs[n], m, v, f"adamw_{n}"))

    def big_out(n, k):
        if n in ("mlp_w_up", "mlp_w_down"):
            return jnp.stack([big[n + "0"][k], big[n + "1"][k]])
        return big[n][k][None]

    gsm = dict(gp)
    gsm["swa_sinks"] = jnp.pad(gp["swa_sinks"].reshape(16, HEAD)[:, 0], (0, LANES - 16))
    gathered = _allgather_small(_pack_small(gsm))

    def small_pack(norm_mix, norm_mlp, gn, dq, dk, b, sq, sk, sinks):
        dup = lambda t: jnp.tile(t.reshape(1, HEAD), (1, 2))
        bias = lax.dynamic_update_slice(jnp.zeros((12, LANES), F32), b.reshape(3, LANES), (3 * chip, 0))
        return _pack_small(dict(norm_mix=norm_mix, norm_mlp=norm_mlp, ret_gn_gain=gn, dil_q_gain=dup(dq), dil_k_gain=dup(dk),
                                swa_b_qkv=bias, swa_q_gain=dup(sq), swa_k_gain=dup(sk),
                                swa_sinks=jnp.pad(sinks.reshape(16), (0, LANES - 16))))

    pw = small_pack(norm_mix, norm_mlp, ret_gn_gain, dil_q_gain, dil_k_gain, swa_b_qkv, swa_q_gain, swa_k_gain, swa_sinks)
    pm = small_pack(m_norm_mix, m_norm_mlp, m_ret_gn_gain, m_dil_q_gain, m_dil_k_gain, m_swa_b_qkv, m_swa_q_gain, m_swa_k_gain, m_swa_sinks)
    pv = small_pack(v_norm_mix, v_norm_mlp, v_ret_gn_gain, v_dil_q_gain, v_dil_k_gain, v_swa_b_qkv, v_swa_q_gain, v_swa_k_gain, v_swa_sinks)
    small = [_unpack_small(t) for t in _adamw_small(pw, gathered, pm, pv)]

    def small_out(n, k):
        t = small[k][n]
        if n in ("norm_mix", "norm_mlp"):
            return t.reshape(2, D_MODEL)
        if n == "ret_gn_gain":
            return t.reshape(1, RET_HEADS, 128)
        if n == "swa_b_qkv":
            return lax.dynamic_slice(t, (3 * chip, 0), (3, LANES)).reshape(1, 384)
        if n == "swa_sinks":
            return t[0, :16].reshape(1, 16)
        return t[0, :HEAD].reshape(1, HEAD)

    order = ["norm_mix", "norm_mlp", "mlp_w_up", "mlp_w_down", "hyb_w_in", "hyb_w_out", "ret_gn_gain", "dil_q_gain",
             "dil_k_gain", "swa_w_qkv", "swa_b_qkv", "swa_w_out", "swa_q_gain", "swa_k_gain", "swa_sinks"]
    is_big = {"mlp_w_up", "mlp_w_down", "hyb_w_in", "hyb_w_out", "swa_w_qkv", "swa_w_out"}
    outs = [loss, grad_x[None]]
    for k in range(4):
        outs += [big_out(n, k) if n in is_big else small_out(n, k) for n in order]
    return tuple(outs)
```

```python
import functools
import math

import numpy as np
import jax
import jax.numpy as jnp
from jax import lax
from jax.experimental import pallas as pl
from jax.experimental.pallas import tpu as pltpu

F32, BF16 = jnp.float32, jnp.bfloat16
HIGHEST = lax.Precision.HIGHEST
MESH = pl.DeviceIdType.MESH

LANES = 128
VMEM_LIMIT = 48 << 20
D_MODEL = 1024
D_FF = 4096
HEAD = 64
EPS = 1e-6
BLK = 128
RET_HEADS = 4
RET_THETA = 10000.0
ROPE_THETA = 500000.0
ROPE_DIMS = 16
DIL_PATTERNS = ((128, 1), (512, 4), (2048, 16))
SWA_DIST = 127
N_CHIPS = 4
ADAM_LR, ADAM_B1, ADAM_B2, ADAM_EPS, ADAM_WD, ADAM_STEP = 0.001, 0.9, 0.999, 1e-08, 0.01, 10

_LOG_GAMMA = [float(np.log1p(-np.exp2(np.float32(-5.0 - h)))) for h in range(RET_HEADS)]


def _pc(body, **kw):
    return pl.pallas_call(body, **kw)


def _params(sem):
    return pltpu.CompilerParams(dimension_semantics=sem, vmem_limit_bytes=VMEM_LIMIT)


def _matmul(a, b, *, dims, tm, tn, tk, outs, name, epilogue=None, extras=(), b_cs=False, o_cs=0, a_pro=None):
    if dims == "nn":
        M, K = a.shape
        N = b.shape[0] * b.shape[2] if b_cs else b.shape[1]
        a_spec = pl.BlockSpec((tm, tk), lambda i, j, k: (i, k))
        if b_cs:
            npt = b.shape[2] // tn
            b_spec = pl.BlockSpec((None, tk, tn), lambda i, j, k: (j // npt, k, j % npt))
        else:
            b_spec = pl.BlockSpec((tk, tn), lambda i, j, k: (k, j))
        contract = (((1,), (0,)), ((), ()))
    elif dims == "nt":
        M, K = a.shape
        N = b.shape[1] if b_cs else b.shape[0]
        a_spec = pl.BlockSpec((tm, tk), lambda i, j, k: (i, k))
        if b_cs:
            kpt = b.shape[2] // tk
            b_spec = pl.BlockSpec((None, tn, tk), lambda i, j, k: (k // kpt, j, k % kpt))
        else:
            b_spec = pl.BlockSpec((tn, tk), lambda i, j, k: (j, k))
        contract = (((1,), (1,)), ((), ()))
    else:
        K, M = a.shape
        N = b.shape[1]
        a_spec = pl.BlockSpec((tk, tm), lambda i, j, k: (k, i))
        b_spec = pl.BlockSpec((tk, tn), lambda i, j, k: (k, j))
        contract = (((0,), (0,)), ((), ()))
    assert M % tm == 0 and N % tn == 0 and K % tk == 0, (name, M, N, K, tm, tn, tk)
    nk = K // tk
    ex_specs = []
    for arr, kind in extras:
        if kind == "mn":
            ex_specs.append(pl.BlockSpec((tm, tn), lambda i, j, k: (i, j)))
        else:
            ex_specs.append(pl.BlockSpec((1, tn), lambda i, j, k: (0, j)))
    if o_cs:
        n_sh = N // o_cs
        opt = n_sh // tn
        o_shape = (o_cs, M, n_sh)
        o_spec = pl.BlockSpec((None, tm, tn), lambda i, j, k: (j // opt, i, j % opt))
    else:
        o_shape = (M, N)
        o_spec = pl.BlockSpec((tm, tn), lambda i, j, k: (i, j))
    n_ex, n_out = len(extras), len(outs)
    if epilogue is None:
        epilogue = lambda acc: (acc,)

    def body(a_ref, b_ref, *rest):
        ex, o_refs, acc = rest[:n_ex], rest[n_ex:n_ex + n_out], rest[-1]
        k = pl.program_id(2)

        @pl.when(k == 0)
        def _():
            acc[...] = jnp.zeros_like(acc)

        av = a_ref[...] if a_pro is None else a_pro(a_ref[...])
        acc[...] += lax.dot_general(av.astype(BF16), b_ref[...].astype(BF16), contract, preferred_element_type=F32)

        @pl.when(k == nk - 1)
        def _():
            vals = epilogue(acc[...], *[e[...] for e in ex])
            for r, v in zip(o_refs, vals):
                r[...] = v.astype(r.dtype)

    res = _pc(
        body, name=name, grid=(M // tm, N // tn, nk),
        in_specs=[a_spec, b_spec] + ex_specs,
        out_specs=[o_spec] * n_out,
        out_shape=[jax.ShapeDtypeStruct(o_shape, dt) for dt in outs],
        scratch_shapes=[pltpu.VMEM((tm, tn), F32)],
        compiler_params=_params(("parallel", "parallel", "arbitrary")),
    )(a, b, *[e for e, _ in extras])
    return res[0] if n_out == 1 else res


def _roll(x, s):
    return pltpu.roll(x, s % LANES, 1)


def _rope(x, A, B, C, half):
    return x * A + _roll(x, LANES - half) * B + _roll(x, half) * C


def _rope_t(g, A, B, C, half):
    return g * A + _roll(g * B, half) + _roll(g * C, LANES - half)


def _gmean(x, G):
    return jnp.dot(x, G, precision=HIGHEST, preferred_element_type=F32)


def _head_mask(shape, half):
    lane = lax.broadcasted_iota(jnp.int32, shape, len(shape) - 1)
    return (lane >= HEAD) if half else (lane < HEAD)


def _group_matrix():
    i = np.arange(LANES)
    return jnp.asarray((i[:, None] // HEAD == i[None, :] // HEAD).astype(np.float32) / HEAD)


def _rope_inv():
    l = np.arange(LANES) % HEAD
    inv_r = np.power(np.float32(RET_THETA), -(l % 32).astype(np.float32) * np.float32(2.0 / HEAD))
    hp = ROPE_DIMS // 2
    inv_p = np.power(np.float32(ROPE_THETA), -(l % hp).astype(np.float32) * np.float32(2.0 / ROPE_DIMS))
    inv_p = np.where(l < ROPE_DIMS, inv_p, 0.0)
    return jnp.asarray(np.stack([inv_r, inv_p]).astype(np.float32))


def _tables(pos_col):
    S = pos_col.shape[0]
    tm = 512
    hp = ROPE_DIMS // 2

    def body(p_ref, inv_ref, o_ref):
        p = p_ref[...].astype(F32)
        lane = lax.broadcasted_iota(jnp.int32, (tm, LANES), 1) % HEAD
        ang = p * inv_ref[0:1, :]
        c, s = jnp.cos(ang), jnp.sin(ang)
        o_ref[:, 0:128] = c
        o_ref[:, 128:256] = jnp.where(lane < 32, -s, 0.0)
        o_ref[:, 256:384] = jnp.where(lane >= 32, s, 0.0)
        ang = p * inv_ref[1:2, :]
        c, s = jnp.cos(ang), jnp.sin(ang)
        o_ref[:, 384:512] = c
        o_ref[:, 512:640] = jnp.where(lane < hp, -s, 0.0)
        o_ref[:, 640:768] = jnp.where((lane >= hp) & (lane < ROPE_DIMS), s, 0.0)

    return _pc(
        body, name="rope_tables", grid=(S // tm,),
        in_specs=[pl.BlockSpec((tm, 1), lambda i: (i, 0)), pl.BlockSpec((2, LANES), lambda i: (0, 0))],
        out_specs=pl.BlockSpec((tm, 768), lambda i: (i, 0)),
        out_shape=jax.ShapeDtypeStruct((S, 768), F32),
        compiler_params=_params(("parallel",)),
    )(pos_col, _rope_inv())


def _tab(tab_ref, which):
    o = 384 * which
    return tab_ref[:, o:o + 128], tab_ref[:, o + 128:o + 256], tab_ref[:, o + 256:o + 384]


def _rms_fwd(x, g, name):
    S, Dm = x.shape
    tm = 512

    def body(x_ref, g_ref, h_ref):
        xv = x_ref[...]
        r = lax.rsqrt(jnp.mean(xv * xv, axis=-1, keepdims=True) + EPS)
        h_ref[...] = (xv * r * g_ref[...]).astype(BF16)

    return _pc(
        body, name=name, grid=(S // tm,),
        in_specs=[pl.BlockSpec((tm, Dm), lambda i: (i, 0)), pl.BlockSpec((1, Dm), lambda i: (0, 0))],
        out_specs=pl.BlockSpec((tm, Dm), lambda i: (i, 0)),
        out_shape=jax.ShapeDtypeStruct((S, Dm), BF16),
        compiler_params=_params(("parallel",)),
    )(x, g.reshape(1, Dm))


def _rms_bwd(x, g, dh, dres, name):
    S, Dm = x.shape
    tm = 512

    def body(x_ref, g_ref, dh_ref, dres_ref, dx_ref, dxb_ref, dg_ref):
        xv, dhv = x_ref[...], dh_ref[...]
        r = lax.rsqrt(jnp.mean(xv * xv, axis=-1, keepdims=True) + EPS)
        t = dhv * g_ref[...]
        dx = dres_ref[...] + r * t - xv * (r * r * r) * jnp.mean(xv * t, axis=-1, keepdims=True)
        dx_ref[...] = dx
        dxb_ref[...] = dx.astype(BF16)

        @pl.when(pl.program_id(0) == 0)
        def _():
            dg_ref[...] = jnp.zeros_like(dg_ref)

        dg_ref[...] += jnp.sum(dhv * xv * r, axis=0, keepdims=True)

    row = pl.BlockSpec((tm, Dm), lambda i: (i, 0))
    vec = pl.BlockSpec((1, Dm), lambda i: (0, 0))
    return _pc(
        body, name=name, grid=(S // tm,),
        in_specs=[row, vec, row, row], out_specs=[row, row, vec],
        out_shape=[jax.ShapeDtypeStruct((S, Dm), F32), jax.ShapeDtypeStruct((S, Dm), BF16),
                   jax.ShapeDtypeStruct((1, Dm), F32)],
        compiler_params=_params(("arbitrary",)),
    )(x, g.reshape(1, Dm), dh, dres)


def _hn_fwd(x, gain, G):
    r = lax.rsqrt(_gmean(x * x, G) + EPS)
    return x * r * gain


def _hn_bwd(x, gain, dy, G):
    r = lax.rsqrt(_gmean(x * x, G) + EPS)
    t = dy * gain
    dx = r * t - x * (r * r * r) * _gmean(x * t, G)
    return dx, jnp.sum(dy * x * r, axis=0, keepdims=True)


def _fold_halves(v):
    return v + _roll(v, HEAD)


def _even_pre_fwd(proj, tab, qg, kg):
    S = proj.shape[0]
    tm = 256

    def body(p_ref, tab_ref, qg_ref, kg_ref, g_ref, rq_ref, rk_ref, rv_ref, dq_ref, dk_ref, dv_ref):
        Ar, Br, Cr = _tab(tab_ref, 0)
        Ap, Bp, Cp = _tab(tab_ref, 1)
        G = g_ref[...]
        for c in range(2):
            sl = slice(c * 128, (c + 1) * 128)
            rq_ref[:, sl] = _rope(p_ref[:, c * 128:(c + 1) * 128], Ar, Br, Cr, 32).astype(BF16)
            rk_ref[:, sl] = (_rope(p_ref[:, 256 + c * 128:256 + (c + 1) * 128], Ar, Br, Cr, 32) * 0.125).astype(BF16)
        rv_ref[...] = p_ref[:, 512:1024].astype(BF16)
        for c in range(4):
            sl = slice(c * 128, (c + 1) * 128)
            q = _hn_fwd(p_ref[:, 1536 + c * 128:1536 + (c + 1) * 128], qg_ref[...], G)
            dq_ref[:, sl] = _rope(q, Ap, Bp, Cp, 8).astype(BF16)
            k = _hn_fwd(p_ref[:, 2048 + c * 128:2048 + (c + 1) * 128], kg_ref[...], G)
            dk_ref[:, sl] = _rope(k, Ap, Bp, Cp, 8).astype(BF16)
        dv_ref[...] = p_ref[:, 2560:3072].astype(BF16)

    row = lambda w: pl.BlockSpec((tm, w), lambda i: (i, 0))
    vec = pl.BlockSpec((1, LANES), lambda i: (0, 0))
    return _pc(
        body, name="even_pre_fwd", grid=(S // tm,),
        in_specs=[row(3072), row(768), vec, vec, pl.BlockSpec((LANES, LANES), lambda i: (0, 0))],
        out_specs=[row(256), row(256), row(512), row(512), row(512), row(512)],
        out_shape=[jax.ShapeDtypeStruct((S, w), BF16) for w in (256, 256, 512, 512, 512, 512)],
        compiler_params=_params(("parallel",)),
    )(proj, tab, qg, kg, _group_matrix())


def _even_pre_bwd(proj, tab, qg, kg, drq, drk, drv, drg, dqs, dks, dvs):
    S = proj.shape[0]
    tm = 256
    npat = len(dqs)

    def body(p_ref, tab_ref, qg_ref, kg_ref, g_ref, drq_ref, drk_ref, drv_ref, drg_ref, *rest):
        dq_refs, dk_refs, dv_refs = rest[:npat], rest[npat:2 * npat], rest[2 * npat:3 * npat]
        dp_ref, dqg_ref, dkg_ref = rest[3 * npat:]
        Ar, Br, Cr = _tab(tab_ref, 0)
        Ap, Bp, Cp = _tab(tab_ref, 1)
        G = g_ref[...]
        for c in range(2):
            sl = slice(c * 128, (c + 1) * 128)
            dp_ref[:, c * 128:(c + 1) * 128] = _rope_t(drq_ref[:, sl], Ar, Br, Cr, 32).astype(BF16)
            dp_ref[:, 256 + c * 128:256 + (c + 1) * 128] = _rope_t(drk_ref[:, sl] * 0.125, Ar, Br, Cr, 32).astype(BF16)
        dp_ref[:, 512:1024] = drv_ref[...].astype(BF16)
        dp_ref[:, 1024:1536] = drg_ref[...].astype(BF16)
        accq = jnp.zeros((1, LANES), F32)
        acck = jnp.zeros((1, LANES), F32)
        for c in range(4):
            sl = slice(c * 128, (c + 1) * 128)
            g = dq_refs[0][:, sl]
            for r in dq_refs[1:]:
                g = g + r[:, sl]
            dx, dg = _hn_bwd(p_ref[:, 1536 + c * 128:1536 + (c + 1) * 128], qg_ref[...], _rope_t(g, Ap, Bp, Cp, 8), G)
            dp_ref[:, 1536 + c * 128:1536 + (c + 1) * 128] = dx.astype(BF16)
            accq = accq + dg
            g = dk_refs[0][:, sl]
            for r in dk_refs[1:]:
                g = g + r[:, sl]
            dx, dg = _hn_bwd(p_ref[:, 2048 + c * 128:2048 + (c + 1) * 128], kg_ref[...], _rope_t(g, Ap, Bp, Cp, 8), G)
            dp_ref[:, 2048 + c * 128:2048 + (c + 1) * 128] = dx.astype(BF16)
            acck = acck + dg
        g = dv_refs[0][...]
        for r in dv_refs[1:]:
            g = g + r[...]
        dp_ref[:, 2560:3072] = g.astype(BF16)

        @pl.when(pl.program_id(0) == 0)
        def _():
            dqg_ref[...] = jnp.zeros_like(dqg_ref)
            dkg_ref[...] = jnp.zeros_like(dkg_ref)

        dqg_ref[...] += _fold_halves(accq)
        dkg_ref[...] += _fold_halves(acck)

    row = lambda w: pl.BlockSpec((tm, w), lambda i: (i, 0))
    vec = pl.BlockSpec((1, LANES), lambda i: (0, 0))
    return _pc(
        body, name="even_pre_bwd", grid=(S // tm,),
        in_specs=[row(3072), row(768), vec, vec, pl.BlockSpec((LANES, LANES), lambda i: (0, 0)),
                  row(256), row(256), row(512), row(512)] + [row(512)] * (3 * npat),
        out_specs=[row(3072), vec, vec],
        out_shape=[jax.ShapeDtypeStruct((S, 3072), BF16), jax.ShapeDtypeStruct((1, LANES), F32),
                   jax.ShapeDtypeStruct((1, LANES), F32)],
        compiler_params=_params(("arbitrary",)),
    )(proj, tab, qg, kg, _group_matrix(), drq, drk, drv, drg, *dqs, *dks, *dvs)


def _ret_consts(pair, half):
    lg = jnp.where(pair == 0, _LOG_GAMMA[half], _LOG_GAMMA[2 + half]).astype(F32)
    i = lax.broadcasted_iota(jnp.int32, (BLK, BLK), 0)
    j = lax.broadcasted_iota(jnp.int32, (BLK, BLK), 1)
    diff = (i - j).astype(F32)
    decay = jnp.where(diff >= 0, jnp.exp(lg * jnp.maximum(diff, 0.0)), 0.0)
    t = lax.broadcasted_iota(jnp.int32, (BLK, 1), 0).astype(F32)
    xi = jnp.exp(lg * (t + 1.0))
    zeta = jnp.exp(lg * (BLK - 1.0 - t))
    cd = jnp.exp(jnp.full((1, 1), BLK, F32) * lg)
    return decay, xi, zeta, cd


def _ret_fwd(rq, rk, rv):
    S = rq.shape[0]
    nc = S // BLK

    def body(q_ref, k_ref, v_ref, o_ref, st_ref, R):
        p, n = pl.program_id(0), pl.program_id(1)

        @pl.when(n == 0)
        def _():
            R[...] = jnp.zeros_like(R)

        q2, k2 = q_ref[...], k_ref[...]
        for half in range(2):
            decay, xi, zeta, cd = _ret_consts(p, half)
            m = _head_mask((BLK, LANES), half)
            qm = jnp.where(m, q2, jnp.zeros_like(q2))
            km = jnp.where(m, k2, jnp.zeros_like(k2))
            v = v_ref[:, half * 128:(half + 1) * 128]
            Rb = R[half].astype(BF16)
            st_ref[half] = Rb
            sc = lax.dot_general(qm, k2, (((1,), (1,)), ((), ())), preferred_element_type=F32) * decay
            o = jnp.dot(sc.astype(BF16), v, preferred_element_type=F32)
            o = o + jnp.dot(qm, Rb, preferred_element_type=F32) * xi
            o_ref[:, half * 128:(half + 1) * 128] = o
            kz = (km.astype(F32) * zeta).astype(BF16)
            R[half] = R[half] * cd + lax.dot_general(kz, v, (((0,), (0,)), ((), ())), preferred_element_type=F32)

    return _pc(
        body, name="ret_fwd", grid=(2, nc),
        in_specs=[pl.BlockSpec((BLK, 128), lambda p, n: (n, p)), pl.BlockSpec((BLK, 128), lambda p, n: (n, p)),
                  pl.BlockSpec((BLK, 256), lambda p, n: (n, p))],
        out_specs=[pl.BlockSpec((BLK, 256), lambda p, n: (n, p)),
                   pl.BlockSpec((None, None, 2, 128, 128), lambda p, n: (p, n, 0, 0, 0))],
        out_shape=[jax.ShapeDtypeStruct((S, 512), F32), jax.ShapeDtypeStruct((2, nc, 2, 128, 128), BF16)],
        scratch_shapes=[pltpu.VMEM((2, 128, 128), F32)],
        compiler_params=_params(("parallel", "arbitrary")),
    )(rq, rk, rv)


def _ret_bwd(rq, rk, rv, states, do):
    S = rq.shape[0]
    nc = S // BLK

    def body(q_ref, k_ref, v_ref, st_ref, do_ref, dq_ref, dk_ref, dv_ref, U):
        p, n = pl.program_id(0), pl.program_id(1)

        @pl.when(n == 0)
        def _():
            U[...] = jnp.zeros_like(U)

        q2, k2 = q_ref[...], k_ref[...]
        dq_acc = jnp.zeros((BLK, LANES), F32)
        dk_acc = jnp.zeros((BLK, LANES), F32)
        for half in range(2):
            decay, xi, zeta, cd = _ret_consts(p, half)
            m = _head_mask((BLK, LANES), half)
            qm = jnp.where(m, q2, jnp.zeros_like(q2))
            km = jnp.where(m, k2, jnp.zeros_like(k2))
            v = v_ref[:, half * 128:(half + 1) * 128]
            dob = do_ref[:, half * 128:(half + 1) * 128].astype(BF16)
            Rb = st_ref[half]
            Ub = U[half].astype(BF16)
            nt = (((1,), (1,)), ((), ()))
            tn = (((0,), (0,)), ((), ()))
            dsc = (lax.dot_general(dob, v, nt, preferred_element_type=F32) * decay).astype(BF16)
            xdo = (dob.astype(F32) * xi).astype(BF16)
            dq_acc += jnp.dot(dsc, km, preferred_element_type=F32) + lax.dot_general(xdo, Rb, nt, preferred_element_type=F32)
            dk_acc += lax.dot_general(dsc, qm, tn, preferred_element_type=F32) \
                + lax.dot_general(v, Ub, nt, preferred_element_type=F32) * zeta
            sc = (lax.dot_general(qm, k2, nt, preferred_element_type=F32) * decay).astype(BF16)
            kz = (km.astype(F32) * zeta).astype(BF16)
            dv_ref[:, half * 128:(half + 1) * 128] = lax.dot_general(sc, dob, tn, preferred_element_type=F32) \
                + jnp.dot(kz, Ub, preferred_element_type=F32)
            U[half] = U[half] * cd + lax.dot_general(qm, xdo, tn, preferred_element_type=F32)
        dq_ref[...] = dq_acc
        dk_ref[...] = dk_acc

    rev = lambda w: pl.BlockSpec((BLK, w), lambda p, n: (nc - 1 - n, p))
    return _pc(
        body, name="ret_bwd", grid=(2, nc),
        in_specs=[rev(128), rev(128), rev(256),
                  pl.BlockSpec((None, None, 2, 128, 128), lambda p, n: (p, nc - 1 - n, 0, 0, 0)), rev(256)],
        out_specs=[rev(128), rev(128), rev(256)],
        out_shape=[jax.ShapeDtypeStruct((S, 256), F32), jax.ShapeDtypeStruct((S, 256), F32),
                   jax.ShapeDtypeStruct((S, 512), F32)],
        scratch_shapes=[pltpu.VMEM((2, 128, 128), F32)],
        compiler_params=_params(("parallel", "arbitrary")),
    )(rq, rk, rv, states, do)


def _col_of(b, m):
    return jnp.max(jnp.where(m, b, -jnp.inf), axis=1, keepdims=True)


def _attn_fwd(q, k, v, *, nq, max_dist, name, sinks=None, want_bf16=False):
    L, Ck = k.shape
    nb, ncol = L // BLK, Ck // LANES
    scale = HEAD ** -0.5
    has_sink = sinks is not None

    def body(*refs):
        q_ref, kp_ref, kc_ref, vp_ref, vc_ref = refs[:5]
        sk_ref = refs[5] if has_sink else None
        outs = refs[5 + has_sink:]
        n = pl.program_id(1)
        kcat = jnp.concatenate([kp_ref[...], kc_ref[...]], axis=0)
        vcat = jnp.concatenate([vp_ref[...], vc_ref[...]], axis=0)
        r = lax.broadcasted_iota(jnp.int32, (BLK, 2 * BLK), 0)
        c = lax.broadcasted_iota(jnp.int32, (BLK, 2 * BLK), 1)
        dist = r + BLK - c
        valid = (dist >= 0) & (dist <= max_dist) & ((c >= BLK) | (n > 0))
        for i in range(nq):
            q2 = q_ref[:, i * 128:(i + 1) * 128]
            o2 = jnp.zeros((BLK, LANES), F32)
            l2 = jnp.zeros((BLK, LANES), F32)
            for half in range(2):
                m = _head_mask((BLK, LANES), half)
                qm = jnp.where(m, q2, jnp.zeros_like(q2))
                s = lax.dot_general(qm, kcat, (((1,), (1,)), ((), ())), preferred_element_type=F32) * scale
                s = jnp.where(valid, s, -jnp.inf)
                mx = jnp.max(s, axis=1, keepdims=True)
                if has_sink:
                    snk = _col_of(sk_ref[:, i * 128:(i + 1) * 128], _head_mask((1, LANES), half))
                    mx = jnp.maximum(mx, snk)
                pr = jnp.exp(s - mx)
                den = jnp.sum(pr, axis=1, keepdims=True)
                if has_sink:
                    den = den + jnp.exp(snk - mx)
                pv = jnp.dot(pr.astype(BF16), vcat, preferred_element_type=F32) / den
                o2 = jnp.where(m, pv, o2)
                l2 = jnp.where(m, mx + jnp.log(den), l2)
            outs[0][:, i * 128:(i + 1) * 128] = o2
            outs[1][:, i * 128:(i + 1) * 128] = l2
            if want_bf16:
                outs[2][:, i * 128:(i + 1) * 128] = o2.astype(BF16)

    qspec = pl.BlockSpec((BLK, nq * 128), lambda j, n: (n, j))
    cur = pl.BlockSpec((BLK, 128), lambda j, n: (n, j))
    prev = pl.BlockSpec((BLK, 128), lambda j, n: (jnp.maximum(n - 1, 0), j))
    in_specs = [qspec, prev, cur, prev, cur]
    args = [q, k, k, v, v]
    if has_sink:
        in_specs.append(pl.BlockSpec((1, nq * 128), lambda j, n: (0, j)))
        args.append(sinks)
    out_dts = [F32, F32] + ([BF16] if want_bf16 else [])
    return _pc(
        body, name=name, grid=(ncol, nb), in_specs=in_specs,
        out_specs=[qspec] * len(out_dts),
        out_shape=[jax.ShapeDtypeStruct(q.shape, dt) for dt in out_dts],
        compiler_params=_params(("parallel", "parallel")),
    )(*args)


def _attn_bwd(q, k, v, o, lse, do, *, nq, max_dist, name, sinks=None):
    L, Ck = k.shape
    nb, ncol = L // BLK, Ck // LANES
    scale = HEAD ** -0.5
    has_sink = sinks is not None
    nt = (((1,), (1,)), ((), ()))
    tn = (((0,), (0,)), ((), ()))

    def body(*refs):
        (qc_ref, qn_ref, kp_ref, kc_ref, vp_ref, vc_ref, oc_ref, on_ref, lc_ref, ln_ref, dc_ref, dn_ref) = refs[:12]
        sk_ref = refs[12] if has_sink else None
        outs = refs[12 + has_sink:]
        dq_ref, dk_ref, dv_ref = outs[:3]
        n = pl.program_id(1)
        kc, vc = kc_ref[...], vc_ref[...]
        kcat = jnp.concatenate([kp_ref[...], kc], axis=0)
        vcat = jnp.concatenate([vp_ref[...], vc], axis=0)
        r = lax.broadcasted_iota(jnp.int32, (BLK, 2 * BLK), 0)
        c = lax.broadcasted_iota(jnp.int32, (BLK, 2 * BLK), 1)
        dist = r + BLK - c
        valid_q = (dist >= 0) & (dist <= max_dist) & ((c >= BLK) | (n > 0))
        r2 = lax.broadcasted_iota(jnp.int32, (2 * BLK, BLK), 0)
        c2 = lax.broadcasted_iota(jnp.int32, (2 * BLK, BLK), 1)
        dist2 = r2 - c2
        valid_k = (dist2 >= 0) & (dist2 <= max_dist) & ((r2 < BLK) | (n < nb - 1))
        dk_acc = jnp.zeros((BLK, LANES), F32)
        dv_acc = jnp.zeros((BLK, LANES), F32)
        for i in range(nq):
            sl = slice(i * 128, (i + 1) * 128)
            qcur, docur = qc_ref[:, sl], dc_ref[:, sl]
            qcat = jnp.concatenate([qcur, qn_ref[:, sl]], axis=0)
            docat = jnp.concatenate([docur, dn_ref[:, sl]], axis=0)
            ocat = jnp.concatenate([oc_ref[:, sl], on_ref[:, sl]], axis=0)
            lcat = jnp.concatenate([lc_ref[:, sl], ln_ref[:, sl]], axis=0)
            dq2 = jnp.zeros((BLK, LANES), F32)
            ds2 = jnp.zeros((1, LANES), F32)
            for half in range(2):
                m1 = _head_mask((BLK, LANES), half)
                m2 = _head_mask((2 * BLK, LANES), half)
                dom = jnp.where(m2, docat, 0.0)
                delta = jnp.sum(dom * ocat, axis=1, keepdims=True)
                lcol = _col_of(lcat, m2)
                domb = dom.astype(BF16)
                qmcat = jnp.where(m2, qcat, jnp.zeros_like(qcat))
                qm = qmcat[:BLK]
                s = lax.dot_general(qm, kcat, nt, preferred_element_type=F32) * scale
                pr = jnp.where(valid_q, jnp.exp(s - lcol[:BLK]), 0.0)
                dp = lax.dot_general(domb[:BLK], vcat, nt, preferred_element_type=F32)
                ds = (pr * (dp - delta[:BLK])).astype(BF16)
                dq2 = jnp.where(m1, jnp.dot(ds, kcat, preferred_element_type=F32) * scale, dq2)
                if has_sink:
                    snk = _col_of(sk_ref[:, sl], _head_mask((1, LANES), half))
                    contrib = jnp.sum(-jnp.exp(snk - lcol[:BLK]) * delta[:BLK], axis=0, keepdims=True)
                    ds2 = jnp.where(_head_mask((1, LANES), half), contrib, ds2)
                s = lax.dot_general(qmcat, kc, nt, preferred_element_type=F32) * scale
                pr = jnp.where(valid_k, jnp.exp(s - lcol), 0.0)
                dv_acc += lax.dot_general(pr.astype(BF16), domb, tn, preferred_element_type=F32)
                dp = lax.dot_general(domb, vc, nt, preferred_element_type=F32)
                ds = (pr * (dp - delta)).astype(BF16)
                dk_acc += lax.dot_general(ds, qmcat, tn, preferred_element_type=F32) * scale
            dq_ref[:, sl] = dq2
            if has_sink:
                @pl.when(n == 0)
                def _():
                    outs[3][:, sl] = jnp.zeros((1, LANES), F32)

                outs[3][:, sl] += ds2
        dk_ref[...] = dk_acc
        dv_ref[...] = dv_acc

    qcur = pl.BlockSpec((BLK, nq * 128), lambda j, n: (n, j))
    qnext = pl.BlockSpec((BLK, nq * 128), lambda j, n: (jnp.minimum(n + 1, nb - 1), j))
    cur = pl.BlockSpec((BLK, 128), lambda j, n: (n, j))
    prev = pl.BlockSpec((BLK, 128), lambda j, n: (jnp.maximum(n - 1, 0), j))
    in_specs = [qcur, qnext, prev, cur, prev, cur, qcur, qnext, qcur, qnext, qcur, qnext]
    args = [q, q, k, k, v, v, o, o, lse, lse, do, do]
    out_specs = [qcur, cur, cur]
    out_shape = [jax.ShapeDtypeStruct(q.shape, F32), jax.ShapeDtypeStruct(k.shape, F32), jax.ShapeDtypeStruct(k.shape, F32)]
    if has_sink:
        vec = pl.BlockSpec((1, nq * 128), lambda j, n: (0, j))
        in_specs.append(vec)
        args.append(sinks)
        out_specs.append(vec)
        out_shape.append(jax.ShapeDtypeStruct((1, q.shape[1]), F32))
    return _pc(
        body, name=name, grid=(ncol, nb), in_specs=in_specs, out_specs=out_specs, out_shape=out_shape,
        compiler_params=_params(("parallel", "arbitrary")),
    )(*args)


ATT_TILE = 2048


def _rows(ref, start, n, r):
    if r == 1:
        return ref[pl.ds(start, n), :]
    return ref[pl.ds(start, n, stride=r), :]


def _set_rows(ref, start, n, r, val):
    if r == 1:
        ref[pl.ds(start, n), :] = val
    else:
        ref[pl.ds(start, n, stride=r), :] = val


def _band_geometry(S, patterns):
    rmax = max(r for _, r in patterns)
    H = BLK * rmax
    T = min(S, ATT_TILE)
    assert T % H == 0 and S % T == 0
    return H, T, S // T, T // BLK


def _band_fwd(q, k, v, *, patterns, nq, name, sinks=None, want_bf16=False):
    S, Ck = k.shape
    H, T, nt, nbt = _band_geometry(S, patterns)
    ncol = Ck // LANES
    scale = HEAD ** -0.5
    has_sink = sinks is not None
    nt_dims = (((1,), (1,)), ((), ()))

    def body(*refs):
        q_ref, kp_ref, kc_ref, vp_ref, vc_ref = refs[:5]
        sk_ref = refs[5] if has_sink else None
        n_out = 3 if want_bf16 else 2
        outs = refs[5 + has_sink:5 + has_sink + n_out]
        qf, kf, vf, M, L, A = refs[5 + has_sink + n_out:]
        t = pl.program_id(1)
        kf[0:H, :] = kp_ref[...].astype(F32)
        kf[H:H + T, :] = kc_ref[...].astype(F32)
        vf[0:H, :] = vp_ref[...].astype(F32)
        vf[H:H + T, :] = vc_ref[...].astype(F32)
        r_i = lax.broadcasted_iota(jnp.int32, (BLK, 2 * BLK), 0)
        c_i = lax.broadcasted_iota(jnp.int32, (BLK, 2 * BLK), 1)
        dist_i = r_i + BLK - c_i
        masks = [_head_mask((BLK, LANES), h) for h in range(2)]

        for i in range(nq):
            qf[...] = q_ref[:, i * 128:(i + 1) * 128].astype(F32)
            for p, (dist, r) in enumerate(patterns):
                in_band = (dist_i >= 0) & (dist_i <= dist)

                def unit(j, b, p=p, r=r, in_band=in_band):
                    q0 = j + b * (BLK * r)
                    q2 = _rows(qf, q0, BLK, r).astype(BF16)
                    kcat = _rows(kf, H + q0 - BLK * r, 2 * BLK, r).astype(BF16)
                    vcat = _rows(vf, H + q0 - BLK * r, 2 * BLK, r).astype(BF16)
                    valid = in_band & ((c_i >= BLK) | ((b > 0) | (t > 0)))
                    m2 = jnp.zeros((BLK, LANES), F32)
                    l2 = jnp.zeros((BLK, LANES), F32)
                    a2 = jnp.zeros((BLK, LANES), F32)
                    for half in range(2):
                        m = masks[half]
                        qm = jnp.where(m, q2, jnp.zeros_like(q2))
                        s = lax.dot_general(qm, kcat, nt_dims, preferred_element_type=F32) * scale
                        s = jnp.where(valid, s, -jnp.inf)
                        mx = jnp.max(s, axis=1, keepdims=True)
                        pr = jnp.exp(s - mx)
                        den = jnp.sum(pr, axis=1, keepdims=True)
                        pv = jnp.dot(pr.astype(BF16), vcat, preferred_element_type=F32)
                        m2 = jnp.where(m, mx, m2)
                        l2 = jnp.where(m, den, l2)
                        a2 = jnp.where(m, pv, a2)
                    if p > 0:
                        mo = _rows(M, q0, BLK, r)
                        mn = jnp.maximum(mo, m2)
                        wa, wb = jnp.exp(mo - mn), jnp.exp(m2 - mn)
                        l2 = wa * _rows(L, q0, BLK, r) + wb * l2
                        a2 = wa * _rows(A, q0, BLK, r) + wb * a2
                        m2 = mn
                    _set_rows(M, q0, BLK, r, m2)
                    _set_rows(L, q0, BLK, r, l2)
                    _set_rows(A, q0, BLK, r, a2)

                for u in range(nbt):
                    unit(u % r, u // r)
            sl = slice(i * 128, (i + 1) * 128)
            mm, ll, aa = M[...], L[...], A[...]
            if has_sink:
                snk = sk_ref[:, sl]
                mn = jnp.maximum(mm, snk)
                w = jnp.exp(mm - mn)
                ll = ll * w + jnp.exp(snk - mn)
                aa = aa * w
                mm = mn
            o = aa / ll
            outs[0][:, sl] = o
            outs[1][:, sl] = mm + jnp.log(ll)
            if want_bf16:
                outs[2][:, sl] = o.astype(BF16)

    th = T // H
    qspec = pl.BlockSpec((T, nq * 128), lambda j, t: (t, j))
    cur = pl.BlockSpec((T, 128), lambda j, t: (t, j))
    prev = pl.BlockSpec((H, 128), lambda j, t: (jnp.maximum(t * th - 1, 0), j))
    in_specs = [qspec, prev, cur, prev, cur]
    args = [q, k, k, v, v]
    if has_sink:
        in_specs.append(pl.BlockSpec((1, nq * 128), lambda j, t: (0, j)))
        args.append(sinks)
    out_dts = [F32, F32] + ([BF16] if want_bf16 else [])
    return _pc(
        body, name=name, grid=(ncol, nt), in_specs=in_specs,
        out_specs=[qspec] * len(out_dts),
        out_shape=[jax.ShapeDtypeStruct(q.shape, dt) for dt in out_dts],
        scratch_shapes=[pltpu.VMEM((T, LANES), F32), pltpu.VMEM((H + T, LANES), F32), pltpu.VMEM((H + T, LANES), F32),
                        pltpu.VMEM((T, LANES), F32), pltpu.VMEM((T, LANES), F32), pltpu.VMEM((T, LANES), F32)],
        compiler_params=_params(("parallel", "parallel")),
    )(*args)


def _band_bwd(q, k, v, lse, delta, do, *, patterns, nq, name, sinks=None):
    S, Ck = k.shape
    H, T, nt, nbt = _band_geometry(S, patterns)
    ncol = Ck // LANES
    scale = HEAD ** -0.5
    has_sink = sinks is not None
    nt_dims = (((1,), (1,)), ((), ()))
    tn_dims = (((0,), (0,)), ((), ()))

    def body(*refs):
        (qc_ref, qn_ref, kp_ref, kc_ref, vp_ref, vc_ref, lc_ref, ln_ref, ec_ref, en_ref, dc_ref, dn_ref) = refs[:12]
        sk_ref = refs[12] if has_sink else None
        n_out = 4 if has_sink else 3
        outs = refs[12 + has_sink:12 + has_sink + n_out]
        dq_ref, dk_ref, dv_ref = outs[:3]
        qf, kf, vf, lf, ef, df = refs[12 + has_sink + n_out:]
        t = pl.program_id(1)
        kf[0:H, :] = kp_ref[...].astype(F32)
        kf[H:H + T, :] = kc_ref[...].astype(F32)
        vf[0:H, :] = vp_ref[...].astype(F32)
        vf[H:H + T, :] = vc_ref[...].astype(F32)
        dk_ref[...] = jnp.zeros_like(dk_ref)
        dv_ref[...] = jnp.zeros_like(dv_ref)
        r_i = lax.broadcasted_iota(jnp.int32, (BLK, 2 * BLK), 0)
        c_i = lax.broadcasted_iota(jnp.int32, (BLK, 2 * BLK), 1)
        dist_q = r_i + BLK - c_i
        dist_h = dist_q[:, :BLK]
        m1 = [_head_mask((BLK, LANES), h) for h in range(2)]

        def head_inputs(half, q2, do2, l2, e2):
            m = m1[half]
            lh = jnp.where(m, l2, _roll(l2, HEAD))
            eh = jnp.where(m, e2, _roll(e2, HEAD))
            return jnp.where(m, q2, jnp.zeros_like(q2)), jnp.where(m, do2, 0.0).astype(BF16), lh, eh

        for i in range(nq):
            sl = slice(i * 128, (i + 1) * 128)
            for buf, c_ref, n_ref in ((qf, qc_ref, qn_ref), (lf, lc_ref, ln_ref), (ef, ec_ref, en_ref), (df, dc_ref, dn_ref)):
                buf[0:T, :] = c_ref[:, sl].astype(F32)
                buf[T:T + H, :] = n_ref[:, sl].astype(F32)
            if has_sink:
                @pl.when(t == 0)
                def _():
                    outs[3][:, sl] = jnp.zeros((1, LANES), F32)

                outs[3][:, sl] += jnp.sum(-jnp.exp(sk_ref[:, sl] - lc_ref[:, sl]) * ec_ref[:, sl], axis=0, keepdims=True)
            for p, (dist, r) in enumerate(patterns):
                band_q = (dist_q >= 0) & (dist_q <= dist)
                band_h = (dist_h >= 0) & (dist_h <= dist)

                def add_rows(ref, start, val, r=r):
                    _set_rows(ref, start, BLK, r, _rows(ref, start, BLK, r) + val)

                def unit(j, b, p=p, r=r, band_q=band_q):
                    q0 = j + b * (BLK * r)
                    q2 = _rows(qf, q0, BLK, r).astype(BF16)
                    do2, l2, e2 = _rows(df, q0, BLK, r), _rows(lf, q0, BLK, r), _rows(ef, q0, BLK, r)
                    kcat = _rows(kf, H + q0 - BLK * r, 2 * BLK, r).astype(BF16)
                    vcat = _rows(vf, H + q0 - BLK * r, 2 * BLK, r).astype(BF16)
                    valid = band_q & ((c_i >= BLK) | ((b > 0) | (t > 0)))
                    dq2 = jnp.zeros((BLK, LANES), F32)
                    dkc = jnp.zeros((2 * BLK, LANES), F32)
                    dvc = jnp.zeros((2 * BLK, LANES), F32)
                    for half in range(2):
                        qm, dom, lh, eh = head_inputs(half, q2, do2, l2, e2)
                        s = lax.dot_general(qm, kcat, nt_dims, preferred_element_type=F32) * scale
                        pr = jnp.where(valid, jnp.exp(s - jnp.concatenate([lh, lh], axis=1)), 0.0)
                        dp = lax.dot_general(dom, vcat, nt_dims, preferred_element_type=F32)
                        ds = (pr * (dp - jnp.concatenate([eh, eh], axis=1))).astype(BF16)
                        dq2 = jnp.where(m1[half], jnp.dot(ds, kcat, preferred_element_type=F32) * scale, dq2)
                        dvc += lax.dot_general(pr.astype(BF16), dom, tn_dims, preferred_element_type=F32)
                        dkc += lax.dot_general(ds, qm, tn_dims, preferred_element_type=F32) * scale
                    if p > 0:
                        dq2 = dq2 + _rows(dq_ref.at[:, sl], q0, BLK, r)
                    _set_rows(dq_ref.at[:, sl], q0, BLK, r, dq2)
                    add_rows(dk_ref, q0, dkc[BLK:])
                    add_rows(dv_ref, q0, dvc[BLK:])
                    if b > 0:
                        add_rows(dk_ref, q0 - BLK * r, dkc[:BLK])
                        add_rows(dv_ref, q0 - BLK * r, dvc[:BLK])

                def halo_unit(j, r=r, band_h=band_h):
                    k0 = j + (nbt // r - 1) * (BLK * r)
                    q2 = _rows(qf, T + j, BLK, r).astype(BF16)
                    do2, l2, e2 = _rows(df, T + j, BLK, r), _rows(lf, T + j, BLK, r), _rows(ef, T + j, BLK, r)
                    kc = _rows(kf, H + k0, BLK, r).astype(BF16)
                    vc = _rows(vf, H + k0, BLK, r).astype(BF16)
                    dk2 = jnp.zeros((BLK, LANES), F32)
                    dv2 = jnp.zeros((BLK, LANES), F32)
                    for half in range(2):
                        qm, dom, lh, eh = head_inputs(half, q2, do2, l2, e2)
                        s = lax.dot_general(qm, kc, nt_dims, preferred_element_type=F32) * scale
                        pr = jnp.where(band_h, jnp.exp(s - lh), 0.0)
                        dp = lax.dot_general(dom, vc, nt_dims, preferred_element_type=F32)
                        ds = (pr * (dp - eh)).astype(BF16)
                        dv2 += lax.dot_general(pr.astype(BF16), dom, tn_dims, preferred_element_type=F32)
                        dk2 += lax.dot_general(ds, qm, tn_dims, preferred_element_type=F32) * scale
                    add_rows(dk_ref, k0, dk2)
                    add_rows(dv_ref, k0, dv2)

                for u in range(nbt):
                    unit(u % r, u // r)
                if nt > 1:
                    @pl.when(t < nt - 1)
                    def _(r=r, halo_unit=halo_unit):
                        for j in range(r):
                            halo_unit(j)

    th = T // H
    last = S // H - 1
    qcur = pl.BlockSpec((T, nq * 128), lambda j, t: (t, j))
    qnext = pl.BlockSpec((H, nq * 128), lambda j, t: (jnp.minimum((t + 1) * th, last), j))
    cur = pl.BlockSpec((T, 128), lambda j, t: (t, j))
    prev = pl.BlockSpec((H, 128), lambda j, t: (jnp.maximum(t * th - 1, 0), j))
    in_specs = [qcur, qnext, prev, cur, prev, cur, qcur, qnext, qcur, qnext, qcur, qnext]
    args = [q, q, k, k, v, v, lse, lse, delta, delta, do, do]
    out_specs = [qcur, cur, cur]
    out_shape = [jax.ShapeDtypeStruct(q.shape, F32), jax.ShapeDtypeStruct(k.shape, F32), jax.ShapeDtypeStruct(k.shape, F32)]
    if has_sink:
        vec = pl.BlockSpec((1, nq * 128), lambda j, t: (0, j))
        in_specs.append(vec)
        args.append(sinks)
        out_specs.append(vec)
        out_shape.append(jax.ShapeDtypeStruct((1, q.shape[1]), F32))
    big = pltpu.VMEM((T + H, LANES), F32)
    return _pc(
        body, name=name, grid=(ncol, nt), in_specs=in_specs, out_specs=out_specs, out_shape=out_shape,
        scratch_shapes=[big] * 6,
        compiler_params=_params(("parallel", "arbitrary")),
    )(*args)


def _delta(do, o, name):
    S, C = do.shape
    tm = 512

    def body(do_ref, o_ref, g_ref, e_ref):
        for c in range(C // LANES):
            sl = slice(c * 128, (c + 1) * 128)
            e_ref[:, sl] = _gmean(do_ref[:, sl] * o_ref[:, sl], g_ref[...]) * float(HEAD)

    row = pl.BlockSpec((tm, C), lambda i: (i, 0))
    return _pc(
        body, name=name, grid=(S // tm,),
        in_specs=[row, row, pl.BlockSpec((LANES, LANES), lambda i: (0, 0))], out_specs=row,
        out_shape=jax.ShapeDtypeStruct((S, C), F32),
        compiler_params=_params(("parallel",)),
    )(do, o, _group_matrix())


def _even_post_fwd(ro, proj, gn, da):
    S = ro.shape[0]
    tm = 256

    def body(ro_ref, rg_ref, gn_ref, da_ref, mix_ref):
        for c in range(4):
            sl = slice(c * 128, (c + 1) * 128)
            x = ro_ref[:, sl]
            mu = jnp.mean(x, axis=1, keepdims=True)
            xc = x - mu
            var = jnp.mean(xc * xc, axis=1, keepdims=True)
            y = xc * lax.rsqrt(var + EPS) * gn_ref[:, sl]
            z = rg_ref[:, sl]
            mix_ref[:, sl] = (z * jax.nn.sigmoid(z) * y).astype(BF16)
        mix_ref[:, 512:1024] = da_ref[...].astype(BF16)

    row = lambda w: pl.BlockSpec((tm, w), lambda i: (i, 0))
    return _pc(
        body, name="even_post_fwd", grid=(S // tm,),
        in_specs=[row(512), pl.BlockSpec((tm, 512), lambda i: (i, 2)), pl.BlockSpec((1, 512), lambda i: (0, 0)), row(512)],
        out_specs=row(1024), out_shape=jax.ShapeDtypeStruct((S, 1024), BF16),
        compiler_params=_params(("parallel",)),
    )(ro, proj, gn, da)


def _even_post_bwd(ro, proj, gn, dmixed):
    S = ro.shape[0]
    tm = 256

    def body(ro_ref, rg_ref, gn_ref, dm_ref, dro_ref, drg_ref, dgn_ref):
        @pl.when(pl.program_id(0) == 0)
        def _():
            dgn_ref[...] = jnp.zeros_like(dgn_ref)

        for c in range(4):
            sl = slice(c * 128, (c + 1) * 128)
            x = ro_ref[:, sl]
            mu = jnp.mean(x, axis=1, keepdims=True)
            xc = x - mu
            rstd = lax.rsqrt(jnp.mean(xc * xc, axis=1, keepdims=True) + EPS)
            xh = xc * rstd
            gain = gn_ref[:, sl]
            y = xh * gain
            z = rg_ref[:, sl]
            sg = jax.nn.sigmoid(z)
            dra = dm_ref[:, sl]
            drg_ref[:, sl] = dra * y * sg * (1.0 + z * (1.0 - sg))
            dy = dra * z * sg
            dgn_ref[:, sl] += jnp.sum(dy * xh, axis=0, keepdims=True)
            dxh = dy * gain
            dro_ref[:, sl] = rstd * (dxh - jnp.mean(dxh, axis=1, keepdims=True)
                                     - xh * jnp.mean(dxh * xh, axis=1, keepdims=True))

    row = lambda w: pl.BlockSpec((tm, w), lambda i: (i, 0))
    vec = pl.BlockSpec((1, 512), lambda i: (0, 0))
    return _pc(
        body, name="even_post_bwd", grid=(S // tm,),
        in_specs=[row(512), pl.BlockSpec((tm, 512), lambda i: (i, 2)), vec, row(512)],
        out_specs=[row(512), row(512), vec],
        out_shape=[jax.ShapeDtypeStruct((S, 512), F32), jax.ShapeDtypeStruct((S, 512), F32),
                   jax.ShapeDtypeStruct((1, 512), F32)],
        compiler_params=_params(("arbitrary",)),
    )(ro, proj, gn, dmixed)


def _swa_pre_fwd(proj, tab, qg, kg):
    S = proj.shape[0]
    tm = 256

    def body(p_ref, tab_ref, qg_ref, kg_ref, g_ref, q_ref, k_ref, v_ref):
        Ap, Bp, Cp = _tab(tab_ref, 1)
        G = g_ref[...]
        lo = _head_mask((tm, LANES), 0)
        for c in range(8):
            sl = slice(c * 128, (c + 1) * 128)
            q_ref[:, sl] = _rope(_hn_fwd(p_ref[:, sl], qg_ref[...], G), Ap, Bp, Cp, 8).astype(BF16)
        for c in range(2):
            kn = _rope(_hn_fwd(p_ref[:, 1024 + c * 128:1024 + (c + 1) * 128], kg_ref[...], G), Ap, Bp, Cp, 8)
            vv = p_ref[:, 1280 + c * 128:1280 + (c + 1) * 128]
            for t, ref in ((kn, k_ref), (vv, v_ref)):
                sw = _roll(t, HEAD)
                ref[:, (2 * c) * 128:(2 * c + 1) * 128] = jnp.where(lo, t, sw).astype(BF16)
                ref[:, (2 * c + 1) * 128:(2 * c + 2) * 128] = jnp.where(lo, sw, t).astype(BF16)

    row = lambda w: pl.BlockSpec((tm, w), lambda i: (i, 0))
    vec = pl.BlockSpec((1, LANES), lambda i: (0, 0))
    return _pc(
        body, name="swa_pre_fwd", grid=(S // tm,),
        in_specs=[row(1536), row(768), vec, vec, pl.BlockSpec((LANES, LANES), lambda i: (0, 0))],
        out_specs=[row(1024), row(512), row(512)],
        out_shape=[jax.ShapeDtypeStruct((S, w), BF16) for w in (1024, 512, 512)],
        compiler_params=_params(("parallel",)),
    )(proj, tab, qg, kg, _group_matrix())


def _swa_pre_bwd(proj, tab, qg, kg, dq, dk, dv):
    S = proj.shape[0]
    tm = 256

    def body(p_ref, tab_ref, qg_ref, kg_ref, g_ref, dq_ref, dk_ref, dv_ref, dp_ref, db_ref, dqg_ref, dkg_ref):
        Ap, Bp, Cp = _tab(tab_ref, 1)
        G = g_ref[...]
        lo = _head_mask((tm, LANES), 0)

        @pl.when(pl.program_id(0) == 0)
        def _():
            db_ref[...] = jnp.zeros_like(db_ref)
            dqg_ref[...] = jnp.zeros_like(dqg_ref)
            dkg_ref[...] = jnp.zeros_like(dkg_ref)

        accq = jnp.zeros((1, LANES), F32)
        acck = jnp.zeros((1, LANES), F32)
        for c in range(8):
            sl = slice(c * 128, (c + 1) * 128)
            dx, dg = _hn_bwd(p_ref[:, sl], qg_ref[...], _rope_t(dq_ref[:, sl], Ap, Bp, Cp, 8), G)
            dp_ref[:, sl] = dx.astype(BF16)
            db_ref[:, sl] += jnp.sum(dx, axis=0, keepdims=True)
            accq = accq + dg
        for c in range(2):
            folded = []
            for ref in (dk_ref, dv_ref):
                a = ref[:, (2 * c) * 128:(2 * c + 1) * 128]
                b = ref[:, (2 * c + 1) * 128:(2 * c + 2) * 128]
                folded.append(jnp.where(lo, a + _roll(a, HEAD), b + _roll(b, HEAD)))
            ks = slice(1024 + c * 128, 1024 + (c + 1) * 128)
            dx, dg = _hn_bwd(p_ref[:, ks], kg_ref[...], _rope_t(folded[0], Ap, Bp, Cp, 8), G)
            dp_ref[:, ks] = dx.astype(BF16)
            db_ref[:, ks] += jnp.sum(dx, axis=0, keepdims=True)
            acck = acck + dg
            vs = slice(1280 + c * 128, 1280 + (c + 1) * 128)
            dp_ref[:, vs] = folded[1].astype(BF16)
            db_ref[:, vs] += jnp.sum(folded[1], axis=0, keepdims=True)
        dqg_ref[...] += _fold_halves(accq)
        dkg_ref[...] += _fold_halves(acck)

    row = lambda w: pl.BlockSpec((tm, w), lambda i: (i, 0))
    vec = pl.BlockSpec((1, LANES), lambda i: (0, 0))
    return _pc(
        body, name="swa_pre_bwd", grid=(S // tm,),
        in_specs=[row(1536), row(768), vec, vec, pl.BlockSpec((LANES, LANES), lambda i: (0, 0)),
                  row(1024), row(512), row(512)],
        out_specs=[row(1536), pl.BlockSpec((1, 1536), lambda i: (0, 0)), vec, vec],
        out_shape=[jax.ShapeDtypeStruct((S, 1536), BF16), jax.ShapeDtypeStruct((1, 1536), F32),
                   jax.ShapeDtypeStruct((1, LANES), F32), jax.ShapeDtypeStruct((1, LANES), F32)],
        compiler_params=_params(("arbitrary",)),
    )(proj, tab, qg, kg, _group_matrix(), dq, dk, dv)


def _loss_head(y, target):
    S, Dm = y.shape
    tm = 512

    def body(y_ref, t_ref, l_ref, dy_ref, dyb_ref):
        @pl.when(pl.program_id(0) == 0)
        def _():
            l_ref[...] = jnp.zeros_like(l_ref)

        e = y_ref[...] - t_ref[...]
        dy = e * (1.0 / Dm)
        dy_ref[...] = dy
        dyb_ref[...] = dy.astype(BF16)
        row = jnp.sum(e * e, axis=1, keepdims=True) * (0.5 / Dm)
        l_ref[...] += jnp.sum(row, axis=0, keepdims=True)

    row = pl.BlockSpec((tm, Dm), lambda i: (i, 0))
    return _pc(
        body, name="loss_head", grid=(S // tm,), in_specs=[row, row],
        out_specs=[pl.BlockSpec((1, LANES), lambda i: (0, 0)), row, row],
        out_shape=[jax.ShapeDtypeStruct((1, LANES), F32), jax.ShapeDtypeStruct((S, Dm), F32),
                   jax.ShapeDtypeStruct((S, Dm), BF16)],
        compiler_params=_params(("arbitrary",)),
    )(y, target)


def _relu2_of(u):
    r = jnp.maximum(u.astype(F32), 0.0)
    return r * r


def _drelu2(acc, u):
    return (acc * 2.0 * jnp.maximum(u.astype(F32), 0.0),)


def _add(acc, res):
    return (acc + res,)


_T = dict(tm=1024, tn=1024, tk=1024)


def _mlp_fwd(x, g, w_up, w_dn, tag):
    h = _rms_fwd(x, g, f"rms_mlp_fwd{tag}")
    u = _matmul(h, w_up, dims="nn", **_T, outs=[BF16], b_cs=True, name=f"mlp_up{tag}")
    x_out = _matmul(u, w_dn, dims="nn", **_T, outs=[F32], epilogue=_add, extras=[(x, "mn")], a_pro=_relu2_of,
                    name=f"mlp_down{tag}")
    return x_out, (h, u)


def _mlp_bwd(x, g, w_up, w_dn, saved, dy, dyb, tag):
    h, u = saved
    du = _matmul(dyb, w_dn, dims="nt", **_T, outs=[BF16], epilogue=_drelu2, extras=[(u, "mn")], name=f"mlp_du{tag}")
    dw_dn = _matmul(u, dyb, dims="tn", **_T, outs=[F32], a_pro=_relu2_of, name=f"mlp_dwdown{tag}")
    dw_up = _matmul(h, du, dims="tn", **_T, outs=[F32], o_cs=N_CHIPS, name=f"mlp_dwup{tag}")
    dh = _matmul(du, w_up, dims="nt", **_T, outs=[F32], b_cs=True, name=f"mlp_dh{tag}")
    dx, dxb, dg = _rms_bwd(x, g, dh, dy, f"rms_mlp_bwd{tag}")
    return dx, dxb, dg, dw_up, dw_dn


def _pattern_view(t, r):
    S, C = t.shape
    return t.reshape(S // r, r * C)


def _local_step(x, pos_col, target, W, rest_of, P, red):
    S = x.shape[0]
    tab = _tables(pos_col)
    tile2 = lambda g: jnp.tile(g.reshape(1, HEAD), (1, 2))
    dqg, dkg = tile2(P["dil_q_gain"]), tile2(P["dil_k_gain"])
    sqg, skg = tile2(P["swa_q_gain"]), tile2(P["swa_k_gain"])
    gn = P["ret_gn_gain"].reshape(1, 512)
    sink_b = jnp.repeat(P["swa_sinks"].reshape(16), HEAD).reshape(1, 1024)

    h0 = _rms_fwd(x, P["norm_mix"][0], "rms_mix_fwd0")
    proj = _matmul(h0, W["hyb_w_in"], dims="nn", tm=1024, tn=768, tk=1024, outs=[F32], b_cs=True, name="hyb_in")
    rq, rk, rv, dq, dk, dv = _even_pre_fwd(proj, tab, dqg, dkg)
    ro, states = _ret_fwd(rq, rk, rv)
    dil = [(w // r, r) for w, r in DIL_PATTERNS]
    da, dlse = _band_fwd(dq, dk, dv, patterns=dil, nq=1, name="dil_fwd")
    mixed = _even_post_fwd(ro, proj, gn, da)
    x1 = _matmul(mixed, W["hyb_w_out"], dims="nn", **_T, outs=[F32], epilogue=_add, extras=[(x, "mn")], name="hyb_out")
    rest, bias = rest_of(x1)
    W = {**W, **rest}
    x2, mlp0 = _mlp_fwd(x1, P["norm_mlp"][0], W["mlp_w_up"][0], W["mlp_w_down"][0], "0")

    h2 = _rms_fwd(x2, P["norm_mix"][1], "rms_mix_fwd1")
    proj2 = _matmul(h2, W["swa_w_qkv"], dims="nn", tm=1024, tn=384, tk=1024, outs=[F32], b_cs=True,
                    epilogue=_add, extras=[(bias.reshape(1, 1536), "n")], name="swa_qkv")
    sq, sk, sv = _swa_pre_fwd(proj2, tab, sqg, skg)
    swa = [(SWA_DIST, 1)]
    so, slse, so_b = _band_fwd(sq, sk, sv, patterns=swa, nq=2, name="swa_fwd", sinks=sink_b, want_bf16=True)
    x3 = _matmul(so_b, W["swa_w_out"], dims="nn", **_T, outs=[F32], epilogue=_add, extras=[(x2, "mn")], name="swa_out")
    y, mlp1 = _mlp_fwd(x3, P["norm_mlp"][1], W["mlp_w_up"][1], W["mlp_w_down"][1], "1")
    loss, dy, dyb = _loss_head(y, target)

    gw, gp = {}, {}
    dx3, dx3b, dg_mlp1, gw["mlp_w_up1"], gw["mlp_w_down1"] = _mlp_bwd(x3, P["norm_mlp"][1], W["mlp_w_up"][1],
                                                                       W["mlp_w_down"][1], mlp1, dy, dyb, "1")
    dx3b = red.begin("mlp1", {n: (gw[n], 1024) for n in ("mlp_w_up1", "mlp_w_down1")}, dx3b)
    gw["swa_w_out"] = _matmul(so_b, dx3b, dims="tn", **_T, outs=[F32], name="swa_dwout")
    dso = _matmul(dx3b, W["swa_w_out"], dims="nt", **_T, outs=[F32], name="swa_do")
    dsq, dsk, dsv, dsink = _band_bwd(sq, sk, sv, slse, _delta(dso, so, "swa_delta"), dso, patterns=swa, nq=2,
                                     name="swa_bwd", sinks=sink_b)
    dproj2, gp["swa_b_qkv"], gp["swa_q_gain"], gp["swa_k_gain"] = _swa_pre_bwd(proj2, tab, sqg, skg, dsq, dsk, dsv)
    gp["swa_sinks"] = dsink
    gw["swa_w_qkv"] = _matmul(h2, dproj2, dims="tn", tm=1024, tn=384, tk=1024, outs=[F32], o_cs=N_CHIPS, name="swa_dwqkv")
    dh2 = _matmul(dproj2, W["swa_w_qkv"], dims="nt", tm=1024, tn=1024, tk=384, outs=[F32], b_cs=True, name="swa_dh")
    dh2 = red.begin("swa", {"swa_w_qkv": (gw["swa_w_qkv"], 1024), "swa_w_out": (gw["swa_w_out"], 256)}, dh2)
    dh2 = red.advance("mlp1", dh2, dh2)
    dx2, dx2b, dg_mix1 = _rms_bwd(x2, P["norm_mix"][1], dh2, dx3, "rms_mix_bwd1")

    dx1, dx1b, dg_mlp0, gw["mlp_w_up0"], gw["mlp_w_down0"] = _mlp_bwd(x1, P["norm_mlp"][0], W["mlp_w_up"][0],
                                                                       W["mlp_w_down"][0], mlp0, dx2, dx2b, "0")
    dx1b = red.begin("mlp0", {n: (gw[n], 1024) for n in ("mlp_w_up0", "mlp_w_down0")}, dx1b)
    dx1b = red.advance("swa", dx1b, dx1b)
    red.finish("mlp1", dx1b)
    gw["hyb_w_out"] = _matmul(mixed, dx1b, dims="tn", **_T, outs=[F32], name="hyb_dwout")
    dmixed = _matmul(dx1b, W["hyb_w_out"], dims="nt", **_T, outs=[F32], name="hyb_dmixed")
    dro, drg, gp["ret_gn_gain"] = _even_post_bwd(ro, proj, gn, dmixed)
    drq, drk, drv = _ret_bwd(rq, rk, rv, states, dro)
    dda = dmixed[:, 512:]
    ddq, ddk, ddv = _band_bwd(dq, dk, dv, dlse, _delta(dda, da, "dil_delta"), dda, patterns=dil, nq=1, name="dil_bwd")
    ddq = red.advance("mlp0", ddq, ddq)
    red.finish("swa", ddq)
    dproj, gp["dil_q_gain"], gp["dil_k_gain"] = _even_pre_bwd(proj, tab, dqg, dkg, drq, drk, drv, drg, [ddq], [ddk], [ddv])
    gw["hyb_w_in"] = _matmul(h0, dproj, dims="tn", tm=1024, tn=768, tk=1024, outs=[F32], o_cs=N_CHIPS, name="hyb_dwin")
    dh0 = _matmul(dproj, W["hyb_w_in"], dims="nt", tm=1024, tn=1024, tk=768, outs=[F32], b_cs=True, name="hyb_dh")
    grad_x, _, dg_mix0 = _rms_bwd(x, P["norm_mix"][0], dh0, dx1, "rms_mix_bwd0")
    red.finish("mlp0", grad_x)
    red.last({"hyb_w_in": (gw["hyb_w_in"], 1024), "hyb_w_out": (gw["hyb_w_out"], 256)})
    gp["norm_mix"] = jnp.concatenate([dg_mix0, dg_mix1], axis=0)
    gp["norm_mlp"] = jnp.concatenate([dg_mlp0, dg_mlp1], axis=0)
    return loss, grad_x, gp


HBM = pl.BlockSpec(memory_space=pltpu.HBM)


def _place():
    x, y, c = lax.axis_index("x"), lax.axis_index("y"), lax.axis_index("c")
    chips = [(1 - x, y), (x, 1 - y), (1 - x, 1 - y)]
    return x, y, c, chips


def _allgather_shards(buf):
    _, R, Wd = buf.shape
    Rh = R // 2

    def body(b_ref, out_ref, send_sems, recv_sems):
        x, y, c, chips = _place()
        sibling = (x, y, 1 - c)

        def copy(k, chip, core, to):
            block = b_ref.at[2 * chip[0] + chip[1], pl.ds(core * Rh, Rh), :]
            return pltpu.make_async_remote_copy(
                src_ref=block, dst_ref=block, send_sem=send_sems.at[k], recv_sem=recv_sems.at[k],
                device_id=to, device_id_type=MESH)

        first = [copy(k, (x, y), c, (*chip, c)) for k, chip in enumerate(chips)]
        for cp in first:
            cp.start()
        passed = [copy(3 + k, chip, c, sibling) for k, chip in enumerate(chips)]
        for k, chip in enumerate(chips):
            copy(k, chip, c, (x, y, c)).wait_recv()
            passed[k].start()
        for k, chip in enumerate(chips):
            copy(3 + k, chip, 1 - c, (x, y, c)).wait_recv()
        for cp in first + passed:
            cp.wait_send()

    return _pc(
        body, name="allgather_first", in_specs=[HBM], out_specs=HBM,
        out_shape=jax.ShapeDtypeStruct(buf.shape, buf.dtype), input_output_aliases={0: 0},
        scratch_shapes=[pltpu.SemaphoreType.DMA((6,)), pltpu.SemaphoreType.DMA((6,))],
    )(buf)


SEM = pl.BlockSpec(memory_space=pltpu.SEMAPHORE)
EFFECT = pltpu.SideEffectType.DATAFLOW_SIDE_EFFECTING


def _half_block(ref, chip, core):
    rh = ref.shape[1] // 2
    return ref.at[2 * chip[0] + chip[1], pl.ds(core * rh, rh), :]


def _gather_start(buf, ride):
    def body(b_ref, ride_ref, s0, s1, s2, r0, r1, r2, b_out, ride_out):
        x, y, c, chips = _place()
        for chip, s, r in zip(chips, (s0, s1, s2), (r0, r1, r2)):
            mine = _half_block(b_ref, (x, y), c)
            pltpu.make_async_remote_copy(src_ref=mine, dst_ref=mine, send_sem=s, recv_sem=r,
                                         device_id=(*chip, c), device_id_type=MESH).start()

    sem = pltpu.SemaphoreType.DMA(())
    return _pc(
        body, name="allgather_rest_start",
        out_shape=(sem,) * 6 + (pltpu.HBM(buf.shape, buf.dtype), pltpu.HBM(ride.shape, ride.dtype)),
        in_specs=(HBM, HBM), out_specs=(SEM,) * 6 + (HBM, HBM), input_output_aliases={0: 6, 1: 7},
        compiler_params=pltpu.CompilerParams(has_side_effects=EFFECT),
    )(pltpu.with_memory_space_constraint(buf, pltpu.HBM), pltpu.with_memory_space_constraint(ride, pltpu.HBM))


def _gather_wait(buf, sems, after):
    def body(b_ref, s0, s1, s2, r0, r1, r2, after_ref, b_out):
        x, y, c, chips = _place()
        for chip, s, r in zip(chips, (s0, s1, s2), (r0, r1, r2)):
            cp = pltpu.make_async_remote_copy(src_ref=_half_block(b_ref, (x, y), c), dst_ref=_half_block(b_ref, chip, c),
                                              send_sem=s, recv_sem=r, device_id=(*chip, c), device_id_type=MESH)
            cp.wait_send()
            cp.wait_recv()

    return _pc(
        body, name="allgather_rest_wait", out_shape=pltpu.HBM(buf.shape, buf.dtype),
        in_specs=(HBM,) + (SEM,) * 6 + (pl.BlockSpec(memory_space=pl.ANY),), out_specs=HBM, input_output_aliases={0: 0},
        compiler_params=pltpu.CompilerParams(has_side_effects=EFFECT),
    )(buf, *sems, after)


def _gather_handover(buf):
    def body(b_ref, out_ref, send_sems, recv_sems):
        x, y, c, chips = _place()
        cps = []
        for k, chip in enumerate(chips):
            mine = _half_block(b_ref, chip, c)
            cps.append(pltpu.make_async_remote_copy(src_ref=mine, dst_ref=mine, send_sem=send_sems.at[k],
                                                    recv_sem=recv_sems.at[k], device_id=(x, y, 1 - c), device_id_type=MESH))
        for cp in cps:
            cp.start()
        for k, chip in enumerate(chips):
            theirs = _half_block(b_ref, chip, 1 - c)
            pltpu.make_async_remote_copy(src_ref=theirs, dst_ref=theirs, send_sem=send_sems.at[k], recv_sem=recv_sems.at[k],
                                         device_id=(x, y, 1 - c), device_id_type=MESH).wait_recv()
        for cp in cps:
            cp.wait_send()

    return _pc(
        body, name="allgather_rest_handover", in_specs=[HBM], out_specs=HBM,
        out_shape=jax.ShapeDtypeStruct(buf.shape, buf.dtype), input_output_aliases={0: 0},
        scratch_shapes=[pltpu.SemaphoreType.DMA((3,)), pltpu.SemaphoreType.DMA((3,))],
    )(buf)


def _swap_halves(ts):
    nt = len(ts)

    def body(*refs):
        t_refs, l_refs, send_sems, recv_sems = refs[:nt], refs[nt:2 * nt], refs[-2], refs[-1]
        x, y, c, _ = _place()
        cps = []
        for k in range(nt):
            rh = t_refs[k].shape[1] // 2
            cps.append(pltpu.make_async_remote_copy(
                src_ref=t_refs[k].at[:, pl.ds((1 - c) * rh, rh), :], dst_ref=l_refs[k],
                send_sem=send_sems.at[k], recv_sem=recv_sems.at[k], device_id=(x, y, 1 - c), device_id_type=MESH))
        for cp in cps:
            cp.start()
        for cp in cps:
            cp.wait()

    return _pc(
        body, name="grad_swap_halves", in_specs=[HBM] * nt, out_specs=[HBM] * nt,
        out_shape=[jax.ShapeDtypeStruct((t.shape[0], t.shape[1] // 2, t.shape[2]), F32) for t in ts],
        scratch_shapes=[pltpu.SemaphoreType.DMA((nt,)), pltpu.SemaphoreType.DMA((nt,))],
    )(*ts)


def _pair_sum(t, l, place, name):
    _, r, cols = t.shape
    rh = r // 2
    tr = min(rh, 256)
    nr = rh // tr

    def body(pl_ref, t_ref, l_ref, o_ref):
        o_ref[...] = (t_ref[...] + l_ref[...]).astype(BF16)

    return _pc(
        body, name=name,
        grid_spec=pltpu.PrefetchScalarGridSpec(
            num_scalar_prefetch=1, grid=(N_CHIPS, nr),
            in_specs=[pl.BlockSpec((None, tr, cols), lambda s, i, p: (s, p[1] * nr + i, 0)),
                      pl.BlockSpec((None, tr, cols), lambda s, i, p: (s, i, 0))],
            out_specs=pl.BlockSpec((None, tr, cols), lambda s, i, p: (s, i, 0))),
        out_shape=jax.ShapeDtypeStruct((N_CHIPS, rh, cols), BF16),
        compiler_params=_params(("parallel", "parallel")),
    )(place, t, l)


def _exchange_chips(ps):
    nt = len(ps)

    def body(*refs):
        p_refs, r_refs, send_sems, recv_sems = refs[:nt], refs[nt:2 * nt], refs[-2], refs[-1]
        x, y, c, chips = _place()
        cps = []
        for t in range(nt):
            for k, chip in enumerate(chips):
                cps.append(pltpu.make_async_remote_copy(
                    src_ref=p_refs[t].at[2 * chip[0] + chip[1]], dst_ref=r_refs[t].at[k],
                    send_sem=send_sems.at[3 * t + k], recv_sem=recv_sems.at[3 * t + k],
                    device_id=(*chip, c), device_id_type=MESH))
        for cp in cps:
            cp.start()
        for cp in cps:
            cp.wait()

    return _pc(
        body, name="grad_exchange_chips", in_specs=[HBM] * nt, out_specs=[HBM] * nt,
        out_shape=[jax.ShapeDtypeStruct((3,) + p.shape[1:], BF16) for p in ps],
        scratch_shapes=[pltpu.SemaphoreType.DMA((3 * nt,)), pltpu.SemaphoreType.DMA((3 * nt,))],
    )(*ps)


def _final_sum(t, l, rcv, place, name):
    _, r, cols = t.shape
    rh = r // 2
    tr = min(rh, 256)
    nr = rh // tr

    def body(pl_ref, t_ref, l_ref, r_ref, o_ref):
        acc = t_ref[...] + l_ref[...]
        for k in range(3):
            acc = acc + r_ref[k].astype(F32)
        o_ref[...] = acc

    return _pc(
        body, name=name,
        grid_spec=pltpu.PrefetchScalarGridSpec(
            num_scalar_prefetch=1, grid=(nr,),
            in_specs=[pl.BlockSpec((None, tr, cols), lambda i, p: (p[0], p[1] * nr + i, 0)),
                      pl.BlockSpec((None, tr, cols), lambda i, p: (p[0], i, 0)),
                      pl.BlockSpec((3, tr, cols), lambda i, p: (0, i, 0))],
            out_specs=pl.BlockSpec((tr, cols), lambda i, p: (p[1] * nr + i, 0))),
        out_shape=jax.ShapeDtypeStruct((r, cols), F32),
        compiler_params=_params(("parallel",)),
    )(place, t, l, rcv)


def _share_halves(hs):
    nt = len(hs)

    def body(*refs):
        h_refs, send_sems, recv_sems = refs[:nt], refs[-2], refs[-1]
        x, y, c, _ = _place()
        cps = []
        for k in range(nt):
            rh = h_refs[k].shape[0] // 2
            half = h_refs[k].at[pl.ds(c * rh, rh), :]
            cps.append(pltpu.make_async_remote_copy(
                src_ref=half, dst_ref=half, send_sem=send_sems.at[k], recv_sem=recv_sems.at[k],
                device_id=(x, y, 1 - c), device_id_type=MESH))
        for cp in cps:
            cp.start()
        for cp in cps:
            cp.wait()

    return _pc(
        body, name="grad_share_halves", in_specs=[HBM] * nt, out_specs=[HBM] * nt,
        out_shape=[jax.ShapeDtypeStruct(h.shape, F32) for h in hs],
        input_output_aliases={k: k for k in range(nt)},
        scratch_shapes=[pltpu.SemaphoreType.DMA((nt,)), pltpu.SemaphoreType.DMA((nt,))],
    )(*hs)


def _split_start(name, bufs, ride, n, copies_of):
    nb = len(bufs)

    def body(*refs):
        sems = refs[nb + 1:nb + 1 + 2 * n]
        for cp in copies_of(refs[:nb], sems[:n], sems[n:]):
            cp.start()

    outs = _pc(
        body, name=name,
        out_shape=(pltpu.SemaphoreType.DMA(()),) * (2 * n) + tuple(pltpu.HBM(b.shape, b.dtype) for b in bufs)
        + (pltpu.HBM(ride.shape, ride.dtype),),
        in_specs=(HBM,) * (nb + 1), out_specs=(SEM,) * (2 * n) + (HBM,) * (nb + 1),
        input_output_aliases={k: 2 * n + k for k in range(nb + 1)},
        compiler_params=pltpu.CompilerParams(has_side_effects=EFFECT),
    )(*[pltpu.with_memory_space_constraint(b, pltpu.HBM) for b in bufs], pltpu.with_memory_space_constraint(ride, pltpu.HBM))
    return list(outs[:2 * n]), list(outs[2 * n:2 * n + nb]), outs[-1]


def _split_wait(name, bufs, sems, after, n, copies_of):
    nb = len(bufs)

    def body(*refs):
        s = refs[nb:nb + 2 * n]
        for cp in copies_of(refs[:nb], s[:n], s[n:]):
            cp.wait_send()
            cp.wait_recv()

    outs = _pc(
        body, name=name, out_shape=tuple(pltpu.HBM(b.shape, b.dtype) for b in bufs),
        in_specs=(HBM,) * nb + (SEM,) * (2 * n) + (pl.BlockSpec(memory_space=pl.ANY),), out_specs=(HBM,) * nb,
        input_output_aliases={k: k for k in range(nb)},
        compiler_params=pltpu.CompilerParams(has_side_effects=EFFECT),
    )(*bufs, *sems, after)
    return list(outs)


def _swap_copies(nt):
    def copies_of(refs, send, recv):
        x, y, c, _ = _place()
        cps = []
        for k in range(nt):
            rh = refs[k].shape[1] // 2
            cps.append(pltpu.make_async_remote_copy(
                src_ref=refs[k].at[:, pl.ds((1 - c) * rh, rh), :], dst_ref=refs[nt + k],
                send_sem=send[k], recv_sem=recv[k], device_id=(x, y, 1 - c), device_id_type=MESH))
        return cps
    return copies_of


def _exchange_copies(nt):
    def copies_of(refs, send, recv):
        x, y, c, chips = _place()
        cps = []
        for t in range(nt):
            for k, chip in enumerate(chips):
                cps.append(pltpu.make_async_remote_copy(
                    src_ref=refs[t].at[2 * chip[0] + chip[1]], dst_ref=refs[nt + t].at[k],
                    send_sem=send[3 * t + k], recv_sem=recv[3 * t + k], device_id=(*chip, c), device_id_type=MESH))
        return cps
    return copies_of


class _StagedReduce:
    def __init__(self, place):
        self.place = place
        self.groups = {}
        self.halves = {}

    @staticmethod
    def slab(t, r):
        return t.reshape(N_CHIPS, r, t.size // (N_CHIPS * r))

    def begin(self, g, grads, ride):
        names = list(grads)
        ts = [self.slab(t, r) for t, r in grads.values()]
        lands = [lax.empty((N_CHIPS, t.shape[1] // 2, t.shape[2]), F32) for t in ts]
        sems, bufs, ride = _split_start(f"grad_swap_start_{g}", ts + lands, ride, len(ts), _swap_copies(len(ts)))
        self.groups[g] = dict(names=names, bufs=bufs, sems=sems)
        return ride

    def advance(self, g, after, ride):
        st = self.groups[g]
        nt = len(st["names"])
        bufs = _split_wait(f"grad_swap_wait_{g}", st["bufs"], st["sems"], after, nt, _swap_copies(nt))
        st["ts"], st["ls"] = bufs[:nt], bufs[nt:]
        ps = [_pair_sum(t, l, self.place, f"pair_sum_{n}") for t, l, n in zip(st["ts"], st["ls"], st["names"])]
        lands = [lax.empty((3,) + p.shape[1:], BF16) for p in ps]
        st["sems"], st["bufs"], ride = _split_start(f"grad_exchange_start_{g}", ps + lands, ride, 3 * nt, _exchange_copies(nt))
        return ride

    def finish(self, g, after):
        st = self.groups[g]
        nt = len(st["names"])
        bufs = _split_wait(f"grad_exchange_wait_{g}", st["bufs"], st["sems"], after, 3 * nt, _exchange_copies(nt))
        for t, l, r, n in zip(st["ts"], st["ls"], bufs[nt:], st["names"]):
            self.halves[n] = _final_sum(t, l, r, self.place, f"final_sum_{n}")

    def last(self, grads):
        names = list(grads)
        ts = [self.slab(t, r) for t, r in grads.values()]
        ls = _swap_halves(ts)
        ps = [_pair_sum(t, l, self.place, f"pair_sum_{n}") for t, l, n in zip(ts, ls, names)]
        rs = _exchange_chips(ps)
        for t, l, r, n in zip(ts, ls, rs, names):
            self.halves[n] = _final_sum(t, l, r, self.place, f"final_sum_{n}")


def _allgather_small(v):
    rows = v.shape[0]

    def body(v_ref, out_ref, send_sems, recv_sems):
        x, y, c, _ = _place()
        me = 4 * x + 2 * y + c
        out_ref[me] = v_ref[...]
        cps = []
        for k in range(1, 8):
            fx, fy, fc = (k >> 2) & 1, (k >> 1) & 1, k & 1
            to = (1 - x if fx else x, 1 - y if fy else y, 1 - c if fc else c)
            cps.append(pltpu.make_async_remote_copy(
                src_ref=v_ref, dst_ref=out_ref.at[me], send_sem=send_sems.at[k - 1], recv_sem=recv_sems.at[k - 1],
                device_id=to, device_id_type=MESH))
        for cp in cps:
            cp.start()
        for cp in cps:
            cp.wait()

    return _pc(
        body, name="allgather_small",
        in_specs=[pl.BlockSpec(memory_space=pltpu.VMEM)], out_specs=pl.BlockSpec(memory_space=pltpu.VMEM),
        out_shape=jax.ShapeDtypeStruct((8, rows, LANES), F32),
        scratch_shapes=[pltpu.SemaphoreType.DMA((7,)), pltpu.SemaphoreType.DMA((7,))],
    )(v)


def _adamw_math(w, g, m, v):
    m = ADAM_B1 * m + (1.0 - ADAM_B1) * g
    v = ADAM_B2 * v + (1.0 - ADAM_B2) * (g * g)
    m_hat = m / (1.0 - ADAM_B1 ** ADAM_STEP)
    v_hat = v / (1.0 - ADAM_B2 ** ADAM_STEP)
    return -ADAM_LR * (m_hat / (jnp.sqrt(v_hat) + ADAM_EPS) + ADAM_WD * w), m, v


def _adamw(w, g, m, v, name):
    r, cols = w.shape
    tr = min(r, 256)

    def body(w_ref, g_ref, m_ref, v_ref, d_ref, mo_ref, vo_ref):
        d, mn, vn = _adamw_math(w_ref[...], g_ref[...], m_ref[...], v_ref[...])
        d_ref[...] = d
        mo_ref[...] = mn
        vo_ref[...] = vn

    row = pl.BlockSpec((tr, cols), lambda i: (i, 0))
    return _pc(
        body, name=name, grid=(r // tr,), in_specs=[row] * 4, out_specs=[row] * 3,
        out_shape=[jax.ShapeDtypeStruct((r, cols), F32)] * 3,
        compiler_params=_params(("parallel",)),
    )(w, g, m, v)


def _adamw_small(w, gathered, m, v):
    rows = w.shape[0]

    def body(w_ref, g_ref, m_ref, v_ref, go_ref, d_ref, mo_ref, vo_ref):
        g = g_ref[0]
        for k in range(1, 8):
            g = g + g_ref[k]
        d, mn, vn = _adamw_math(w_ref[...], g, m_ref[...], v_ref[...])
        go_ref[...] = g
        d_ref[...] = d
        mo_ref[...] = mn
        vo_ref[...] = vn

    return _pc(
        body, name="adamw_small",
        out_shape=[jax.ShapeDtypeStruct((rows, LANES), F32)] * 4,
    )(w, gathered, m, v)


_BIAS_ROWS = 32


def _own_slot(flat, chip):
    return lax.dynamic_update_slice(jnp.zeros((N_CHIPS,) + flat.shape, flat.dtype), flat[None], (chip, 0, 0))


def _pack_first(hyb_w_in, hyb_w_out):
    return jnp.concatenate([t.astype(BF16).reshape(-1, 1024) for t in (hyb_w_in, hyb_w_out)], axis=0)


def _unpack_first(g):
    return {"hyb_w_in": g[:, 0:768, :].reshape(N_CHIPS, 1024, 768), "hyb_w_out": g[:, 768:1024, :].reshape(1024, 1024)}


def _pack_rest(mlp_w_up, mlp_w_down, swa_w_qkv, swa_w_out, swa_b_qkv):
    parts = [t.astype(BF16).reshape(-1, 1024) for t in (mlp_w_up, mlp_w_down, swa_w_qkv, swa_w_out)]
    bias = lax.bitcast_convert_type(swa_b_qkv.reshape(384), BF16).reshape(1, 768)
    bias = jnp.pad(bias, ((0, _BIAS_ROWS - 1), (0, 256)))
    return jnp.concatenate(parts + [bias], axis=0)


def _unpack_rest(g):
    W = {
        "mlp_w_up": [g[:, l * 1024:(l + 1) * 1024, :] for l in range(2)],
        "mlp_w_down": [g[:, 2048 + l * 1024:2048 + (l + 1) * 1024, :].reshape(D_FF, D_MODEL) for l in range(2)],
        "swa_w_qkv": g[:, 4096:4480, :].reshape(N_CHIPS, 1024, 384),
        "swa_w_out": g[:, 4480:4736, :].reshape(1024, 1024),
    }
    bias = lax.bitcast_convert_type(g[:, 4736, :768].reshape(N_CHIPS, 384, 2), F32).reshape(1536)
    return W, bias


_SMALL = (("norm_mix", 16), ("norm_mlp", 16), ("ret_gn_gain", 4), ("dil_q_gain", 1), ("dil_k_gain", 1),
          ("swa_b_qkv", 12), ("swa_q_gain", 1), ("swa_k_gain", 1), ("swa_sinks", 1))
_SUBLANES = 8


def _slot(r):
    return -(-r // _SUBLANES) * _SUBLANES


def _pack_small(d):
    return jnp.concatenate([jnp.pad(d[n].reshape(r, LANES), ((0, _slot(r) - r), (0, 0))) for n, r in _SMALL], axis=0)


def _unpack_small(p):
    out, o = {}, 0
    for n, r in _SMALL:
        out[n] = p[o:o + r]
        o += _slot(r)
    return out


def kernel(x, positions, norm_mix, norm_mlp, mlp_w_up, mlp_w_down, hyb_w_in, hyb_w_out, ret_gn_gain, dil_q_gain, dil_k_gain, swa_w_qkv, swa_b_qkv, swa_w_out, swa_q_gain, swa_k_gain, swa_sinks, loss_target, m_norm_mix, m_norm_mlp, m_mlp_w_up, m_mlp_w_down, m_hyb_w_in, m_hyb_w_out, m_ret_gn_gain, m_dil_q_gain, m_dil_k_gain, m_swa_w_qkv, m_swa_b_qkv, m_swa_w_out, m_swa_q_gain, m_swa_k_gain, m_swa_sinks, v_norm_mix, v_norm_mlp, v_mlp_w_up, v_mlp_w_down, v_hyb_w_in, v_hyb_w_out, v_ret_gn_gain, v_dil_q_gain, v_dil_k_gain, v_swa_w_qkv, v_swa_b_qkv, v_swa_w_out, v_swa_q_gain, v_swa_k_gain, v_swa_sinks):
    ax, ay, ac = lax.axis_index("x"), lax.axis_index("y"), lax.axis_index("c")
    chip = 2 * ax + ay
    place = jnp.stack([chip, ac]).astype(jnp.int32)
    S = x.shape[1]

    first = _allgather_shards(_own_slot(_pack_first(hyb_w_in[0], hyb_w_out[0]), chip))
    rest = _own_slot(_pack_rest(mlp_w_up, mlp_w_down, swa_w_qkv[0], swa_w_out[0], swa_b_qkv[0]), chip)
    *sems, rest, first = _gather_start(rest, first)

    def rest_of(after):
        return _unpack_rest(_gather_handover(_gather_wait(rest, sems, after)))

    P = dict(norm_mix=norm_mix, norm_mlp=norm_mlp, ret_gn_gain=ret_gn_gain, dil_q_gain=dil_q_gain, dil_k_gain=dil_k_gain,
             swa_q_gain=swa_q_gain, swa_k_gain=swa_k_gain, swa_sinks=swa_sinks)

    red = _StagedReduce(place)
    loss_l, grad_x, gp = _local_step(x[0], positions.reshape(S, 1), loss_target[0], _unpack_first(first), rest_of, P, red)
    loss = lax.psum(loss_l[0, 0], ("x", "y", "c"))

    names = ["mlp_w_up0", "mlp_w_up1", "mlp_w_down0", "mlp_w_down1", "hyb_w_in", "hyb_w_out", "swa_w_qkv", "swa_w_out"]
    gs = dict(zip(names, _share_halves([red.halves[n] for n in names])))
    shards = dict(mlp_w_up0=(mlp_w_up[0], m_mlp_w_up[0], v_mlp_w_up[0]), mlp_w_up1=(mlp_w_up[1], m_mlp_w_up[1], v_mlp_w_up[1]),
                  mlp_w_down0=(mlp_w_down[0], m_mlp_w_down[0], v_mlp_w_down[0]),
                  mlp_w_down1=(mlp_w_down[1], m_mlp_w_down[1], v_mlp_w_down[1]),
                  hyb_w_in=(hyb_w_in[0], m_hyb_w_in[0], v_hyb_w_in[0]), hyb_w_out=(hyb_w_out[0], m_hyb_w_out[0], v_hyb_w_out[0]),
                  swa_w_qkv=(swa_w_qkv[0], m_swa_w_qkv[0], v_swa_w_qkv[0]), swa_w_out=(swa_w_out[0], m_swa_w_out[0], v_swa_w_out[0]))
    big = {}
    for n in names:
        w, m, v = shards[n]
        big[n] = (gs[n],) + tuple(_adamw(w, gs[n], m, v, f"adamw_{n}"))

    def big_out(n, k):
        if n in ("mlp_w_up", "mlp_w_down"):
            return jnp.stack([big[n + "0"][k], big[n + "1"][k]])
        return big[n][k][None]

    gsm = dict(gp)
    gsm["swa_sinks"] = jnp.pad(gp["swa_sinks"].reshape(16, HEAD)[:, 0], (0, LANES - 16))
    gathered = _allgather_small(_pack_small(gsm))

    def small_pack(norm_mix, norm_mlp, gn, dq, dk, b, sq, sk, sinks):
        dup = lambda t: jnp.tile(t.reshape(1, HEAD), (1, 2))
        bias = lax.dynamic_update_slice(jnp.zeros((12, LANES), F32), b.reshape(3, LANES), (3 * chip, 0))
        return _pack_small(dict(norm_mix=norm_mix, norm_mlp=norm_mlp, ret_gn_gain=gn, dil_q_gain=dup(dq), dil_k_gain=dup(dk),
                                swa_b_qkv=bias, swa_q_gain=dup(sq), swa_k_gain=dup(sk),
                                swa_sinks=jnp.pad(sinks.reshape(16), (0, LANES - 16))))

    pw = small_pack(norm_mix, norm_mlp, ret_gn_gain, dil_q_gain, dil_k_gain, swa_b_qkv, swa_q_gain, swa_k_gain, swa_sinks)
    pm = small_pack(m_norm_mix, m_norm_mlp, m_ret_gn_gain, m_dil_q_gain, m_dil_k_gain, m_swa_b_qkv, m_swa_q_gain, m_swa_k_gain, m_swa_sinks)
    pv = small_pack(v_norm_mix, v_norm_mlp, v_ret_gn_gain, v_dil_q_gain, v_dil_k_gain, v_swa_b_qkv, v_swa_q_gain, v_swa_k_gain, v_swa_sinks)
    small = [_unpack_small(t) for t in _adamw_small(pw, gathered, pm, pv)]

    def small_out(n, k):
        t = small[k][n]
        if n in ("norm_mix", "norm_mlp"):
            return t.reshape(2, D_MODEL)
        if n == "ret_gn_gain":
            return t.reshape(1, RET_HEADS, 128)
        if n == "swa_b_qkv":
            return lax.dynamic_slice(t, (3 * chip, 0), (3, LANES)).reshape(1, 384)
        if n == "swa_sinks":
            return t[0, :16].reshape(1, 16)
        return t[0, :HEAD].reshape(1, HEAD)

    order = ["norm_mix", "norm_mlp", "mlp_w_up", "mlp_w_down", "hyb_w_in", "hyb_w_out", "ret_gn_gain", "dil_q_gain",
             "dil_k_gain", "swa_w_qkv", "swa_b_qkv", "swa_w_out", "swa_q_gain", "swa_k_gain", "swa_sinks"]
    is_big = {"mlp_w_up", "mlp_w_down", "hyb_w_in", "hyb_w_out", "swa_w_qkv", "swa_w_out"}
    outs = [loss, grad_x[None]]
    for k in range(4):
        outs += [big_out(n, k) if n in is_big else small_out(n, k) for n in order]
    return tuple(outs)
```

```python
import functools
import math

import numpy as np
import jax
import jax.numpy as jnp
from jax import lax
from jax.experimental import pallas as pl
from jax.experimental.pallas import tpu as pltpu

F32, BF16 = jnp.float32, jnp.bfloat16
HIGHEST = lax.Precision.HIGHEST
MESH = pl.DeviceIdType.MESH

LANES = 128
VMEM_LIMIT = 48 << 20
D_MODEL = 1024
D_FF = 4096
HEAD = 64
EPS = 1e-6
BLK = 128
RET_HEADS = 4
RET_THETA = 10000.0
ROPE_THETA = 500000.0
ROPE_DIMS = 16
DIL_PATTERNS = ((128, 1), (512, 4), (2048, 16))
SWA_DIST = 127
N_CHIPS = 4
ADAM_LR, ADAM_B1, ADAM_B2, ADAM_EPS, ADAM_WD, ADAM_STEP = 0.001, 0.9, 0.999, 1e-08, 0.01, 10

_LOG_GAMMA = [float(np.log1p(-np.exp2(np.float32(-5.0 - h)))) for h in range(RET_HEADS)]


def _pc(body, **kw):
    return pl.pallas_call(body, **kw)


def _params(sem):
    return pltpu.CompilerParams(dimension_semantics=sem, vmem_limit_bytes=VMEM_LIMIT)


def _matmul(a, b, *, dims, tm, tn, tk, outs, name, epilogue=None, extras=(), b_cs=False, o_cs=0, a_pro=None):
    if dims == "nn":
        M, K = a.shape
        N = b.shape[0] * b.shape[2] if b_cs else b.shape[1]
        a_spec = pl.BlockSpec((tm, tk), lambda i, j, k: (i, k))
        if b_cs:
            npt = b.shape[2] // tn
            b_spec = pl.BlockSpec((None, tk, tn), lambda i, j, k: (j // npt, k, j % npt))
        else:
            b_spec = pl.BlockSpec((tk, tn), lambda i, j, k: (k, j))
        contract = (((1,), (0,)), ((), ()))
    elif dims == "nt":
        M, K = a.shape
        N = b.shape[1] if b_cs else b.shape[0]
        a_spec = pl.BlockSpec((tm, tk), lambda i, j, k: (i, k))
        if b_cs:
            kpt = b.shape[2] // tk
            b_spec = pl.BlockSpec((None, tn, tk), lambda i, j, k: (k // kpt, j, k % kpt))
        else:
            b_spec = pl.BlockSpec((tn, tk), lambda i, j, k: (j, k))
        contract = (((1,), (1,)), ((), ()))
    else:
        K, M = a.shape
        N = b.shape[1]
        a_spec = pl.BlockSpec((tk, tm), lambda i, j, k: (k, i))
        b_spec = pl.BlockSpec((tk, tn), lambda i, j, k: (k, j))
        contract = (((0,), (0,)), ((), ()))
    assert M % tm == 0 and N % tn == 0 and K % tk == 0, (name, M, N, K, tm, tn, tk)
    nk = K // tk
    ex_specs = []
    for arr, kind in extras:
        if kind == "mn":
            ex_specs.append(pl.BlockSpec((tm, tn), lambda i, j, k: (i, j)))
        elif kind == "n":
            ex_specs.append(pl.BlockSpec((1, tn), lambda i, j, k: (0, j)))
        elif kind == "full":
            ex_specs.append(pl.BlockSpec(arr.shape, lambda i, j, k, nd=arr.ndim: (0,) * nd))
        else:
            ex_specs.append(pl.BlockSpec((tm, kind), lambda i, j, k: (i, 0)))
    if o_cs:
        n_sh = N // o_cs
        opt = n_sh // tn
        o_shape = (o_cs, M, n_sh)
        o_spec = pl.BlockSpec((None, tm, tn), lambda i, j, k: (j // opt, i, j % opt))
    else:
        o_shape = (M, N)
        o_spec = pl.BlockSpec((tm, tn), lambda i, j, k: (i, j))
    o_specs, o_shapes, summed = [], [], []
    for o in outs:
        if isinstance(o, tuple) and o[0] == "colsum":
            assert N == tn
            o_specs.append(pl.BlockSpec((1, tn), lambda i, j, k: (0, j)))
            o_shapes.append(jax.ShapeDtypeStruct((1, N), F32))
            summed.append(True)
        elif isinstance(o, tuple):
            o_specs.append(pl.BlockSpec((tm, o[1]), lambda i, j, k: (i, 0)))
            o_shapes.append(jax.ShapeDtypeStruct((M, o[1]), o[0]))
            summed.append(False)
        else:
            o_specs.append(o_spec)
            o_shapes.append(jax.ShapeDtypeStruct(o_shape, o))
            summed.append(False)
    n_ex, n_out = len(extras), len(outs)
    if epilogue is None:
        epilogue = lambda acc: (acc,)

    def body(a_ref, b_ref, *rest):
        ex, o_refs, acc = rest[:n_ex], rest[n_ex:n_ex + n_out], rest[-1]
        i, k = pl.program_id(0), pl.program_id(2)

        @pl.when(k == 0)
        def _():
            acc[...] = jnp.zeros_like(acc)

        av = a_ref[...] if a_pro is None else a_pro(a_ref[...])
        acc[...] += lax.dot_general(av.astype(BF16), b_ref[...].astype(BF16), contract, preferred_element_type=F32)

        @pl.when(k == nk - 1)
        def _():
            vals = epilogue(acc[...], *[e[...] for e in ex])
            for r, v, sm in zip(o_refs, vals, summed):
                if sm:
                    @pl.when(i == 0)
                    def _(r=r):
                        r[...] = jnp.zeros_like(r)

                    r[...] += v
                else:
                    r[...] = v.astype(r.dtype)

    res = _pc(
        body, name=name, grid=(M // tm, N // tn, nk),
        in_specs=[a_spec, b_spec] + ex_specs, out_specs=o_specs, out_shape=o_shapes,
        scratch_shapes=[pltpu.VMEM((tm, tn), F32)],
        compiler_params=_params(("arbitrary" if any(summed) else "parallel", "parallel", "arbitrary")),
    )(a, b, *[e for e, _ in extras])
    return res[0] if n_out == 1 else res


def _roll(x, s):
    return pltpu.roll(x, s % LANES, 1)


def _rope(x, A, B, C, half):
    return x * A + _roll(x, LANES - half) * B + _roll(x, half) * C


def _rope_t(g, A, B, C, half):
    return g * A + _roll(g * B, half) + _roll(g * C, LANES - half)


def _gmean(x, G):
    return jnp.dot(x, G, precision=HIGHEST, preferred_element_type=F32)


def _head_mask(shape, half):
    lane = lax.broadcasted_iota(jnp.int32, shape, len(shape) - 1)
    return (lane >= HEAD) if half else (lane < HEAD)


def _group_matrix():
    i = np.arange(LANES)
    return jnp.asarray((i[:, None] // HEAD == i[None, :] // HEAD).astype(np.float32) / HEAD)


def _rope_inv():
    l = np.arange(LANES) % HEAD
    inv_r = np.power(np.float32(RET_THETA), -(l % 32).astype(np.float32) * np.float32(2.0 / HEAD))
    hp = ROPE_DIMS // 2
    inv_p = np.power(np.float32(ROPE_THETA), -(l % hp).astype(np.float32) * np.float32(2.0 / ROPE_DIMS))
    inv_p = np.where(l < ROPE_DIMS, inv_p, 0.0)
    return jnp.asarray(np.stack([inv_r, inv_p]).astype(np.float32))


def _tables(pos_col):
    S = pos_col.shape[0]
    tm = 512
    hp = ROPE_DIMS // 2

    def body(p_ref, inv_ref, o_ref):
        p = p_ref[...].astype(F32)
        lane = lax.broadcasted_iota(jnp.int32, (tm, LANES), 1) % HEAD
        ang = p * inv_ref[0:1, :]
        c, s = jnp.cos(ang), jnp.sin(ang)
        o_ref[:, 0:128] = c
        o_ref[:, 128:256] = jnp.where(lane < 32, -s, 0.0)
        o_ref[:, 256:384] = jnp.where(lane >= 32, s, 0.0)
        ang = p * inv_ref[1:2, :]
        c, s = jnp.cos(ang), jnp.sin(ang)
        o_ref[:, 384:512] = c
        o_ref[:, 512:640] = jnp.where(lane < hp, -s, 0.0)
        o_ref[:, 640:768] = jnp.where((lane >= hp) & (lane < ROPE_DIMS), s, 0.0)

    return _pc(
        body, name="rope_tables", grid=(S // tm,),
        in_specs=[pl.BlockSpec((tm, 1), lambda i: (i, 0)), pl.BlockSpec((2, LANES), lambda i: (0, 0))],
        out_specs=pl.BlockSpec((tm, 768), lambda i: (i, 0)),
        out_shape=jax.ShapeDtypeStruct((S, 768), F32),
        compiler_params=_params(("parallel",)),
    )(pos_col, _rope_inv())


def _tab(tab_ref, which):
    o = 384 * which
    return tab_ref[:, o:o + 128], tab_ref[:, o + 128:o + 256], tab_ref[:, o + 256:o + 384]


def _rms_fwd(x, g, name):
    S, Dm = x.shape
    tm = 512

    def body(x_ref, g_ref, h_ref):
        xv = x_ref[...]
        r = lax.rsqrt(jnp.mean(xv * xv, axis=-1, keepdims=True) + EPS)
        h_ref[...] = (xv * r * g_ref[...]).astype(BF16)

    return _pc(
        body, name=name, grid=(S // tm,),
        in_specs=[pl.BlockSpec((tm, Dm), lambda i: (i, 0)), pl.BlockSpec((1, Dm), lambda i: (0, 0))],
        out_specs=pl.BlockSpec((tm, Dm), lambda i: (i, 0)),
        out_shape=jax.ShapeDtypeStruct((S, Dm), BF16),
        compiler_params=_params(("parallel",)),
    )(x, g.reshape(1, Dm))


def _rms_bwd(x, g, dh, dres, name):
    S, Dm = x.shape
    tm = 512

    def body(x_ref, g_ref, dh_ref, dres_ref, dx_ref, dxb_ref, dg_ref):
        xv, dhv = x_ref[...], dh_ref[...]
        r = lax.rsqrt(jnp.mean(xv * xv, axis=-1, keepdims=True) + EPS)
        t = dhv * g_ref[...]
        dx = dres_ref[...] + r * t - xv * (r * r * r) * jnp.mean(xv * t, axis=-1, keepdims=True)
        dx_ref[...] = dx
        dxb_ref[...] = dx.astype(BF16)

        @pl.when(pl.program_id(0) == 0)
        def _():
            dg_ref[...] = jnp.zeros_like(dg_ref)

        dg_ref[...] += jnp.sum(dhv * xv * r, axis=0, keepdims=True)

    row = pl.BlockSpec((tm, Dm), lambda i: (i, 0))
    vec = pl.BlockSpec((1, Dm), lambda i: (0, 0))
    return _pc(
        body, name=name, grid=(S // tm,),
        in_specs=[row, vec, row, row], out_specs=[row, row, vec],
        out_shape=[jax.ShapeDtypeStruct((S, Dm), F32), jax.ShapeDtypeStruct((S, Dm), BF16),
                   jax.ShapeDtypeStruct((1, Dm), F32)],
        compiler_params=_params(("arbitrary",)),
    )(x, g.reshape(1, Dm), dh, dres)


def _hn_fwd(x, gain, G):
    r = lax.rsqrt(_gmean(x * x, G) + EPS)
    return x * r * gain


def _hn_bwd(x, gain, dy, G):
    r = lax.rsqrt(_gmean(x * x, G) + EPS)
    t = dy * gain
    dx = r * t - x * (r * r * r) * _gmean(x * t, G)
    return dx, jnp.sum(dy * x * r, axis=0, keepdims=True)


def _fold_halves(v):
    return v + _roll(v, HEAD)


def _even_pre_fwd(proj, tab, qg, kg):
    S = proj.shape[0]
    tm = 256

    def body(p_ref, tab_ref, qg_ref, kg_ref, g_ref, rq_ref, rk_ref, rv_ref, dq_ref, dk_ref, dv_ref):
        Ar, Br, Cr = _tab(tab_ref, 0)
        Ap, Bp, Cp = _tab(tab_ref, 1)
        G = g_ref[...]
        for c in range(2):
            sl = slice(c * 128, (c + 1) * 128)
            rq_ref[:, sl] = _rope(p_ref[:, c * 128:(c + 1) * 128], Ar, Br, Cr, 32).astype(BF16)
            rk_ref[:, sl] = (_rope(p_ref[:, 256 + c * 128:256 + (c + 1) * 128], Ar, Br, Cr, 32) * 0.125).astype(BF16)
        rv_ref[...] = p_ref[:, 512:1024].astype(BF16)
        for c in range(4):
            sl = slice(c * 128, (c + 1) * 128)
            q = _hn_fwd(p_ref[:, 1536 + c * 128:1536 + (c + 1) * 128], qg_ref[...], G)
            dq_ref[:, sl] = _rope(q, Ap, Bp, Cp, 8).astype(BF16)
            k = _hn_fwd(p_ref[:, 2048 + c * 128:2048 + (c + 1) * 128], kg_ref[...], G)
            dk_ref[:, sl] = _rope(k, Ap, Bp, Cp, 8).astype(BF16)
        dv_ref[...] = p_ref[:, 2560:3072].astype(BF16)

    row = lambda w: pl.BlockSpec((tm, w), lambda i: (i, 0))
    vec = pl.BlockSpec((1, LANES), lambda i: (0, 0))
    return _pc(
        body, name="even_pre_fwd", grid=(S // tm,),
        in_specs=[row(3072), row(768), vec, vec, pl.BlockSpec((LANES, LANES), lambda i: (0, 0))],
        out_specs=[row(256), row(256), row(512), row(512), row(512), row(512)],
        out_shape=[jax.ShapeDtypeStruct((S, w), BF16) for w in (256, 256, 512, 512, 512, 512)],
        compiler_params=_params(("parallel",)),
    )(proj, tab, qg, kg, _group_matrix())


def _even_pre_bwd(proj, tab, qg, kg, drq, drk, drv, drg, dqs, dks, dvs):
    S = proj.shape[0]
    tm = 256
    npat = len(dqs)

    def body(p_ref, tab_ref, qg_ref, kg_ref, g_ref, drq_ref, drk_ref, drv_ref, drg_ref, *rest):
        dq_refs, dk_refs, dv_refs = rest[:npat], rest[npat:2 * npat], rest[2 * npat:3 * npat]
        dp_ref, dqg_ref, dkg_ref = rest[3 * npat:]
        Ar, Br, Cr = _tab(tab_ref, 0)
        Ap, Bp, Cp = _tab(tab_ref, 1)
        G = g_ref[...]
        for c in range(2):
            sl = slice(c * 128, (c + 1) * 128)
            dp_ref[:, c * 128:(c + 1) * 128] = _rope_t(drq_ref[:, sl], Ar, Br, Cr, 32).astype(BF16)
            dp_ref[:, 256 + c * 128:256 + (c + 1) * 128] = _rope_t(drk_ref[:, sl] * 0.125, Ar, Br, Cr, 32).astype(BF16)
        dp_ref[:, 512:1024] = drv_ref[...].astype(BF16)
        dp_ref[:, 1024:1536] = drg_ref[...].astype(BF16)
        accq = jnp.zeros((1, LANES), F32)
        acck = jnp.zeros((1, LANES), F32)
        for c in range(4):
            sl = slice(c * 128, (c + 1) * 128)
            g = dq_refs[0][:, sl]
            for r in dq_refs[1:]:
                g = g + r[:, sl]
            dx, dg = _hn_bwd(p_ref[:, 1536 + c * 128:1536 + (c + 1) * 128], qg_ref[...], _rope_t(g, Ap, Bp, Cp, 8), G)
            dp_ref[:, 1536 + c * 128:1536 + (c + 1) * 128] = dx.astype(BF16)
            accq = accq + dg
            g = dk_refs[0][:, sl]
            for r in dk_refs[1:]:
                g = g + r[:, sl]
            dx, dg = _hn_bwd(p_ref[:, 2048 + c * 128:2048 + (c + 1) * 128], kg_ref[...], _rope_t(g, Ap, Bp, Cp, 8), G)
            dp_ref[:, 2048 + c * 128:2048 + (c + 1) * 128] = dx.astype(BF16)
            acck = acck + dg
        g = dv_refs[0][...]
        for r in dv_refs[1:]:
            g = g + r[...]
        dp_ref[:, 2560:3072] = g.astype(BF16)

        @pl.when(pl.program_id(0) == 0)
        def _():
            dqg_ref[...] = jnp.zeros_like(dqg_ref)
            dkg_ref[...] = jnp.zeros_like(dkg_ref)

        dqg_ref[...] += _fold_halves(accq)
        dkg_ref[...] += _fold_halves(acck)

    row = lambda w: pl.BlockSpec((tm, w), lambda i: (i, 0))
    vec = pl.BlockSpec((1, LANES), lambda i: (0, 0))
    return _pc(
        body, name="even_pre_bwd", grid=(S // tm,),
        in_specs=[row(3072), row(768), vec, vec, pl.BlockSpec((LANES, LANES), lambda i: (0, 0)),
                  row(256), row(256), row(512), row(512)] + [row(512)] * (3 * npat),
        out_specs=[row(3072), vec, vec],
        out_shape=[jax.ShapeDtypeStruct((S, 3072), BF16), jax.ShapeDtypeStruct((1, LANES), F32),
                   jax.ShapeDtypeStruct((1, LANES), F32)],
        compiler_params=_params(("arbitrary",)),
    )(proj, tab, qg, kg, _group_matrix(), drq, drk, drv, drg, *dqs, *dks, *dvs)


def _ret_consts(pair, half):
    lg = jnp.where(pair == 0, _LOG_GAMMA[half], _LOG_GAMMA[2 + half]).astype(F32)
    i = lax.broadcasted_iota(jnp.int32, (BLK, BLK), 0)
    j = lax.broadcasted_iota(jnp.int32, (BLK, BLK), 1)
    diff = (i - j).astype(F32)
    decay = jnp.where(diff >= 0, jnp.exp(lg * jnp.maximum(diff, 0.0)), 0.0)
    t = lax.broadcasted_iota(jnp.int32, (BLK, 1), 0).astype(F32)
    xi = jnp.exp(lg * (t + 1.0))
    zeta = jnp.exp(lg * (BLK - 1.0 - t))
    cd = jnp.exp(jnp.full((1, 1), BLK, F32) * lg)
    return decay, xi, zeta, cd


def _ret_fwd(rq, rk, rv):
    S = rq.shape[0]
    nc = S // BLK

    def body(q_ref, k_ref, v_ref, o_ref, st_ref, R):
        p, n = pl.program_id(0), pl.program_id(1)

        @pl.when(n == 0)
        def _():
            R[...] = jnp.zeros_like(R)

        q2, k2 = q_ref[...], k_ref[...]
        for half in range(2):
            decay, xi, zeta, cd = _ret_consts(p, half)
            m = _head_mask((BLK, LANES), half)
            qm = jnp.where(m, q2, jnp.zeros_like(q2))
            km = jnp.where(m, k2, jnp.zeros_like(k2))
            v = v_ref[:, half * 128:(half + 1) * 128]
            Rb = R[half].astype(BF16)
            st_ref[half] = Rb
            sc = lax.dot_general(qm, k2, (((1,), (1,)), ((), ())), preferred_element_type=F32) * decay
            o = jnp.dot(sc.astype(BF16), v, preferred_element_type=F32)
            o = o + jnp.dot(qm, Rb, preferred_element_type=F32) * xi
            o_ref[:, half * 128:(half + 1) * 128] = o
            kz = (km.astype(F32) * zeta).astype(BF16)
            R[half] = R[half] * cd + lax.dot_general(kz, v, (((0,), (0,)), ((), ())), preferred_element_type=F32)

    return _pc(
        body, name="ret_fwd", grid=(2, nc),
        in_specs=[pl.BlockSpec((BLK, 128), lambda p, n: (n, p)), pl.BlockSpec((BLK, 128), lambda p, n: (n, p)),
                  pl.BlockSpec((BLK, 256), lambda p, n: (n, p))],
        out_specs=[pl.BlockSpec((BLK, 256), lambda p, n: (n, p)),
                   pl.BlockSpec((None, None, 2, 128, 128), lambda p, n: (p, n, 0, 0, 0))],
        out_shape=[jax.ShapeDtypeStruct((S, 512), F32), jax.ShapeDtypeStruct((2, nc, 2, 128, 128), BF16)],
        scratch_shapes=[pltpu.VMEM((2, 128, 128), F32)],
        compiler_params=_params(("parallel", "arbitrary")),
    )(rq, rk, rv)


def _ret_bwd(rq, rk, rv, states, do):
    S = rq.shape[0]
    nc = S // BLK

    def body(q_ref, k_ref, v_ref, st_ref, do_ref, dq_ref, dk_ref, dv_ref, U):
        p, n = pl.program_id(0), pl.program_id(1)

        @pl.when(n == 0)
        def _():
            U[...] = jnp.zeros_like(U)

        q2, k2 = q_ref[...], k_ref[...]
        dq_acc = jnp.zeros((BLK, LANES), F32)
        dk_acc = jnp.zeros((BLK, LANES), F32)
        for half in range(2):
            decay, xi, zeta, cd = _ret_consts(p, half)
            m = _head_mask((BLK, LANES), half)
            qm = jnp.where(m, q2, jnp.zeros_like(q2))
            km = jnp.where(m, k2, jnp.zeros_like(k2))
            v = v_ref[:, half * 128:(half + 1) * 128]
            dob = do_ref[:, half * 128:(half + 1) * 128].astype(BF16)
            Rb = st_ref[half]
            Ub = U[half].astype(BF16)
            nt = (((1,), (1,)), ((), ()))
            tn = (((0,), (0,)), ((), ()))
            dsc = (lax.dot_general(dob, v, nt, preferred_element_type=F32) * decay).astype(BF16)
            xdo = (dob.astype(F32) * xi).astype(BF16)
            dq_acc += jnp.dot(dsc, km, preferred_element_type=F32) + lax.dot_general(xdo, Rb, nt, preferred_element_type=F32)
            dk_acc += lax.dot_general(dsc, qm, tn, preferred_element_type=F32) \
                + lax.dot_general(v, Ub, nt, preferred_element_type=F32) * zeta
            sc = (lax.dot_general(qm, k2, nt, preferred_element_type=F32) * decay).astype(BF16)
            kz = (km.astype(F32) * zeta).astype(BF16)
            dv_ref[:, half * 128:(half + 1) * 128] = lax.dot_general(sc, dob, tn, preferred_element_type=F32) \
                + jnp.dot(kz, Ub, preferred_element_type=F32)
            U[half] = U[half] * cd + lax.dot_general(qm, xdo, tn, preferred_element_type=F32)
        dq_ref[...] = dq_acc
        dk_ref[...] = dk_acc

    rev = lambda w: pl.BlockSpec((BLK, w), lambda p, n: (nc - 1 - n, p))
    return _pc(
        body, name="ret_bwd", grid=(2, nc),
        in_specs=[rev(128), rev(128), rev(256),
                  pl.BlockSpec((None, None, 2, 128, 128), lambda p, n: (p, nc - 1 - n, 0, 0, 0)), rev(256)],
        out_specs=[rev(128), rev(128), rev(256)],
        out_shape=[jax.ShapeDtypeStruct((S, 256), F32), jax.ShapeDtypeStruct((S, 256), F32),
                   jax.ShapeDtypeStruct((S, 512), F32)],
        scratch_shapes=[pltpu.VMEM((2, 128, 128), F32)],
        compiler_params=_params(("parallel", "arbitrary")),
    )(rq, rk, rv, states, do)


def _col_of(b, m):
    return jnp.max(jnp.where(m, b, -jnp.inf), axis=1, keepdims=True)


def _attn_fwd(q, k, v, *, nq, max_dist, name, sinks=None, want_bf16=False):
    L, Ck = k.shape
    nb, ncol = L // BLK, Ck // LANES
    scale = HEAD ** -0.5
    has_sink = sinks is not None

    def body(*refs):
        q_ref, kp_ref, kc_ref, vp_ref, vc_ref = refs[:5]
        sk_ref = refs[5] if has_sink else None
        outs = refs[5 + has_sink:]
        n = pl.program_id(1)
        kcat = jnp.concatenate([kp_ref[...], kc_ref[...]], axis=0)
        vcat = jnp.concatenate([vp_ref[...], vc_ref[...]], axis=0)
        r = lax.broadcasted_iota(jnp.int32, (BLK, 2 * BLK), 0)
        c = lax.broadcasted_iota(jnp.int32, (BLK, 2 * BLK), 1)
        dist = r + BLK - c
        valid = (dist >= 0) & (dist <= max_dist) & ((c >= BLK) | (n > 0))
        for i in range(nq):
            q2 = q_ref[:, i * 128:(i + 1) * 128]
            o2 = jnp.zeros((BLK, LANES), F32)
            l2 = jnp.zeros((BLK, LANES), F32)
            for half in range(2):
                m = _head_mask((BLK, LANES), half)
                qm = jnp.where(m, q2, jnp.zeros_like(q2))
                s = lax.dot_general(qm, kcat, (((1,), (1,)), ((), ())), preferred_element_type=F32) * scale
                s = jnp.where(valid, s, -jnp.inf)
                mx = jnp.max(s, axis=1, keepdims=True)
                if has_sink:
                    snk = _col_of(sk_ref[:, i * 128:(i + 1) * 128], _head_mask((1, LANES), half))
                    mx = jnp.maximum(mx, snk)
                pr = jnp.exp(s - mx)
                den = jnp.sum(pr, axis=1, keepdims=True)
                if has_sink:
                    den = den + jnp.exp(snk - mx)
                pv = jnp.dot(pr.astype(BF16), vcat, preferred_element_type=F32) / den
                o2 = jnp.where(m, pv, o2)
                l2 = jnp.where(m, mx + jnp.log(den), l2)
            outs[0][:, i * 128:(i + 1) * 128] = o2
            outs[1][:, i * 128:(i + 1) * 128] = l2
            if want_bf16:
                outs[2][:, i * 128:(i + 1) * 128] = o2.astype(BF16)

    qspec = pl.BlockSpec((BLK, nq * 128), lambda j, n: (n, j))
    cur = pl.BlockSpec((BLK, 128), lambda j, n: (n, j))
    prev = pl.BlockSpec((BLK, 128), lambda j, n: (jnp.maximum(n - 1, 0), j))
    in_specs = [qspec, prev, cur, prev, cur]
    args = [q, k, k, v, v]
    if has_sink:
        in_specs.append(pl.BlockSpec((1, nq * 128), lambda j, n: (0, j)))
        args.append(sinks)
    out_dts = [F32, F32] + ([BF16] if want_bf16 else [])
    return _pc(
        body, name=name, grid=(ncol, nb), in_specs=in_specs,
        out_specs=[qspec] * len(out_dts),
        out_shape=[jax.ShapeDtypeStruct(q.shape, dt) for dt in out_dts],
        compiler_params=_params(("parallel", "parallel")),
    )(*args)


def _attn_bwd(q, k, v, o, lse, do, *, nq, max_dist, name, sinks=None):
    L, Ck = k.shape
    nb, ncol = L // BLK, Ck // LANES
    scale = HEAD ** -0.5
    has_sink = sinks is not None
    nt = (((1,), (1,)), ((), ()))
    tn = (((0,), (0,)), ((), ()))

    def body(*refs):
        (qc_ref, qn_ref, kp_ref, kc_ref, vp_ref, vc_ref, oc_ref, on_ref, lc_ref, ln_ref, dc_ref, dn_ref) = refs[:12]
        sk_ref = refs[12] if has_sink else None
        outs = refs[12 + has_sink:]
        dq_ref, dk_ref, dv_ref = outs[:3]
        n = pl.program_id(1)
        kc, vc = kc_ref[...], vc_ref[...]
        kcat = jnp.concatenate([kp_ref[...], kc], axis=0)
        vcat = jnp.concatenate([vp_ref[...], vc], axis=0)
        r = lax.broadcasted_iota(jnp.int32, (BLK, 2 * BLK), 0)
        c = lax.broadcasted_iota(jnp.int32, (BLK, 2 * BLK), 1)
        dist = r + BLK - c
        valid_q = (dist >= 0) & (dist <= max_dist) & ((c >= BLK) | (n > 0))
        r2 = lax.broadcasted_iota(jnp.int32, (2 * BLK, BLK), 0)
        c2 = lax.broadcasted_iota(jnp.int32, (2 * BLK, BLK), 1)
        dist2 = r2 - c2
        valid_k = (dist2 >= 0) & (dist2 <= max_dist) & ((r2 < BLK) | (n < nb - 1))
        dk_acc = jnp.zeros((BLK, LANES), F32)
        dv_acc = jnp.zeros((BLK, LANES), F32)
        for i in range(nq):
            sl = slice(i * 128, (i + 1) * 128)
            qcur, docur = qc_ref[:, sl], dc_ref[:, sl]
            qcat = jnp.concatenate([qcur, qn_ref[:, sl]], axis=0)
            docat = jnp.concatenate([docur, dn_ref[:, sl]], axis=0)
            ocat = jnp.concatenate([oc_ref[:, sl], on_ref[:, sl]], axis=0)
            lcat = jnp.concatenate([lc_ref[:, sl], ln_ref[:, sl]], axis=0)
            dq2 = jnp.zeros((BLK, LANES), F32)
            ds2 = jnp.zeros((1, LANES), F32)
            for half in range(2):
                m1 = _head_mask((BLK, LANES), half)
                m2 = _head_mask((2 * BLK, LANES), half)
                dom = jnp.where(m2, docat, 0.0)
                delta = jnp.sum(dom * ocat, axis=1, keepdims=True)
                lcol = _col_of(lcat, m2)
                domb = dom.astype(BF16)
                qmcat = jnp.where(m2, qcat, jnp.zeros_like(qcat))
                qm = qmcat[:BLK]
                s = lax.dot_general(qm, kcat, nt, preferred_element_type=F32) * scale
                pr = jnp.where(valid_q, jnp.exp(s - lcol[:BLK]), 0.0)
                dp = lax.dot_general(domb[:BLK], vcat, nt, preferred_element_type=F32)
                ds = (pr * (dp - delta[:BLK])).astype(BF16)
                dq2 = jnp.where(m1, jnp.dot(ds, kcat, preferred_element_type=F32) * scale, dq2)
                if has_sink:
                    snk = _col_of(sk_ref[:, sl], _head_mask((1, LANES), half))
                    contrib = jnp.sum(-jnp.exp(snk - lcol[:BLK]) * delta[:BLK], axis=0, keepdims=True)
                    ds2 = jnp.where(_head_mask((1, LANES), half), contrib, ds2)
                s = lax.dot_general(qmcat, kc, nt, preferred_element_type=F32) * scale
                pr = jnp.where(valid_k, jnp.exp(s - lcol), 0.0)
                dv_acc += lax.dot_general(pr.astype(BF16), domb, tn, preferred_element_type=F32)
                dp = lax.dot_general(domb, vc, nt, preferred_element_type=F32)
                ds = (pr * (dp - delta)).astype(BF16)
                dk_acc += lax.dot_general(ds, qmcat, tn, preferred_element_type=F32) * scale
            dq_ref[:, sl] = dq2
            if has_sink:
                @pl.when(n == 0)
                def _():
                    outs[3][:, sl] = jnp.zeros((1, LANES), F32)

                outs[3][:, sl] += ds2
        dk_ref[...] = dk_acc
        dv_ref[...] = dv_acc

    qcur = pl.BlockSpec((BLK, nq * 128), lambda j, n: (n, j))
    qnext = pl.BlockSpec((BLK, nq * 128), lambda j, n: (jnp.minimum(n + 1, nb - 1), j))
    cur = pl.BlockSpec((BLK, 128), lambda j, n: (n, j))
    prev = pl.BlockSpec((BLK, 128), lambda j, n: (jnp.maximum(n - 1, 0), j))
    in_specs = [qcur, qnext, prev, cur, prev, cur, qcur, qnext, qcur, qnext, qcur, qnext]
    args = [q, q, k, k, v, v, o, o, lse, lse, do, do]
    out_specs = [qcur, cur, cur]
    out_shape = [jax.ShapeDtypeStruct(q.shape, F32), jax.ShapeDtypeStruct(k.shape, F32), jax.ShapeDtypeStruct(k.shape, F32)]
    if has_sink:
        vec = pl.BlockSpec((1, nq * 128), lambda j, n: (0, j))
        in_specs.append(vec)
        args.append(sinks)
        out_specs.append(vec)
        out_shape.append(jax.ShapeDtypeStruct((1, q.shape[1]), F32))
    return _pc(
        body, name=name, grid=(ncol, nb), in_specs=in_specs, out_specs=out_specs, out_shape=out_shape,
        compiler_params=_params(("parallel", "arbitrary")),
    )(*args)


ATT_TILE = 2048


def _rows(ref, start, n, r):
    if r == 1:
        return ref[pl.ds(start, n), :]
    return ref[pl.ds(start, n, stride=r), :]


def _set_rows(ref, start, n, r, val):
    if r == 1:
        ref[pl.ds(start, n), :] = val
    else:
        ref[pl.ds(start, n, stride=r), :] = val


def _band_geometry(S, patterns):
    rmax = max(r for _, r in patterns)
    H = BLK * rmax
    T = min(S, ATT_TILE)
    assert T % H == 0 and S % T == 0
    return H, T, S // T, T // BLK


def _band_fwd(q, k, v, *, patterns, nq, name, sinks=None, want_bf16=False):
    S, Ck = k.shape
    H, T, nt, nbt = _band_geometry(S, patterns)
    ncol = Ck // LANES
    scale = HEAD ** -0.5
    has_sink = sinks is not None
    nt_dims = (((1,), (1,)), ((), ()))

    def body(*refs):
        q_ref, kp_ref, kc_ref, vp_ref, vc_ref = refs[:5]
        sk_ref = refs[5] if has_sink else None
        n_out = 3 if want_bf16 else 2
        outs = refs[5 + has_sink:5 + has_sink + n_out]
        qf, kf, vf, M, L, A = refs[5 + has_sink + n_out:]
        t = pl.program_id(1)
        kf[0:H, :] = kp_ref[...].astype(F32)
        kf[H:H + T, :] = kc_ref[...].astype(F32)
        vf[0:H, :] = vp_ref[...].astype(F32)
        vf[H:H + T, :] = vc_ref[...].astype(F32)
        r_i = lax.broadcasted_iota(jnp.int32, (BLK, 2 * BLK), 0)
        c_i = lax.broadcasted_iota(jnp.int32, (BLK, 2 * BLK), 1)
        dist_i = r_i + BLK - c_i
        masks = [_head_mask((BLK, LANES), h) for h in range(2)]

        for i in range(nq):
            qf[...] = q_ref[:, i * 128:(i + 1) * 128].astype(F32) * scale
            for p, (dist, r) in enumerate(patterns):
                in_band = (dist_i >= 0) & (dist_i <= dist)
                in_band_first = in_band & ((c_i >= BLK) | (t > 0))

                def unit(j, b, p=p, r=r, in_band=in_band, in_band_first=in_band_first):
                    q0 = j + b * (BLK * r)
                    q2 = _rows(qf, q0, BLK, r).astype(BF16)
                    kcat = _rows(kf, H + q0 - BLK * r, 2 * BLK, r).astype(BF16)
                    vcat = _rows(vf, H + q0 - BLK * r, 2 * BLK, r).astype(BF16)
                    valid = in_band if b > 0 else in_band_first
                    m2 = jnp.zeros((BLK, LANES), F32)
                    l2 = jnp.zeros((BLK, LANES), F32)
                    a2 = jnp.zeros((BLK, LANES), F32)
                    for half in range(2):
                        m = masks[half]
                        qm = jnp.where(m, q2, jnp.zeros_like(q2))
                        s = lax.dot_general(qm, kcat, nt_dims, preferred_element_type=F32)
                        s = jnp.where(valid, s, -jnp.inf)
                        mx = jnp.max(s, axis=1, keepdims=True)
                        pr = jnp.exp(s - mx)
                        den = jnp.sum(pr, axis=1, keepdims=True)
                        pv = jnp.dot(pr.astype(BF16), vcat, preferred_element_type=F32)
                        m2 = jnp.where(m, mx, m2)
                        l2 = jnp.where(m, den, l2)
                        a2 = jnp.where(m, pv, a2)
                    if p > 0:
                        mo = _rows(M, q0, BLK, r)
                        mn = jnp.maximum(mo, m2)
                        wa, wb = jnp.exp(mo - mn), jnp.exp(m2 - mn)
                        l2 = wa * _rows(L, q0, BLK, r) + wb * l2
                        a2 = wa * _rows(A, q0, BLK, r) + wb * a2
                        m2 = mn
                    _set_rows(M, q0, BLK, r, m2)
                    _set_rows(L, q0, BLK, r, l2)
                    _set_rows(A, q0, BLK, r, a2)

                for u in range(nbt):
                    unit(u % r, u // r)
            sl = slice(i * 128, (i + 1) * 128)
            mm, ll, aa = M[...], L[...], A[...]
            if has_sink:
                snk = sk_ref[:, sl]
                mn = jnp.maximum(mm, snk)
                w = jnp.exp(mm - mn)
                ll = ll * w + jnp.exp(snk - mn)
                aa = aa * w
                mm = mn
            o = aa / ll
            outs[0][:, sl] = o
            outs[1][:, sl] = mm + jnp.log(ll)
            if want_bf16:
                outs[2][:, sl] = o.astype(BF16)

    th = T // H
    qspec = pl.BlockSpec((T, nq * 128), lambda j, t: (t, j))
    cur = pl.BlockSpec((T, 128), lambda j, t: (t, j))
    prev = pl.BlockSpec((H, 128), lambda j, t: (jnp.maximum(t * th - 1, 0), j))
    in_specs = [qspec, prev, cur, prev, cur]
    args = [q, k, k, v, v]
    if has_sink:
        in_specs.append(pl.BlockSpec((1, nq * 128), lambda j, t: (0, j)))
        args.append(sinks)
    out_dts = [F32, F32] + ([BF16] if want_bf16 else [])
    return _pc(
        body, name=name, grid=(ncol, nt), in_specs=in_specs,
        out_specs=[qspec] * len(out_dts),
        out_shape=[jax.ShapeDtypeStruct(q.shape, dt) for dt in out_dts],
        scratch_shapes=[pltpu.VMEM((T, LANES), F32), pltpu.VMEM((H + T, LANES), F32), pltpu.VMEM((H + T, LANES), F32),
                        pltpu.VMEM((T, LANES), F32), pltpu.VMEM((T, LANES), F32), pltpu.VMEM((T, LANES), F32)],
        compiler_params=_params(("parallel", "parallel")),
    )(*args)


def _band_bwd(q, k, v, lse, delta, do, *, patterns, nq, name, sinks=None, do_col0=0):
    S, Ck = k.shape
    H, T, nt, nbt = _band_geometry(S, patterns)
    ncol = Ck // LANES
    scale = HEAD ** -0.5
    has_sink = sinks is not None
    nt_dims = (((1,), (1,)), ((), ()))
    tn_dims = (((0,), (0,)), ((), ()))

    def body(*refs):
        (qc_ref, qn_ref, kp_ref, kc_ref, vp_ref, vc_ref, lc_ref, ln_ref, ec_ref, en_ref, dc_ref, dn_ref) = refs[:12]
        sk_ref = refs[12] if has_sink else None
        n_out = 4 if has_sink else 3
        outs = refs[12 + has_sink:12 + has_sink + n_out]
        dq_ref, dk_ref, dv_ref = outs[:3]
        qf, kf, vf, lf, ef, df = refs[12 + has_sink + n_out:]
        t = pl.program_id(1)
        kf[0:H, :] = kp_ref[...].astype(F32)
        kf[H:H + T, :] = kc_ref[...].astype(F32)
        vf[0:H, :] = vp_ref[...].astype(F32)
        vf[H:H + T, :] = vc_ref[...].astype(F32)
        dk_ref[...] = jnp.zeros_like(dk_ref)
        dv_ref[...] = jnp.zeros_like(dv_ref)
        r_i = lax.broadcasted_iota(jnp.int32, (BLK, 2 * BLK), 0)
        c_i = lax.broadcasted_iota(jnp.int32, (BLK, 2 * BLK), 1)
        dist_q = r_i + BLK - c_i
        dist_h = dist_q[:, :BLK]
        m1 = [_head_mask((BLK, LANES), h) for h in range(2)]

        def head_inputs(half, q2, do2, l2, e2):
            m = m1[half]
            lh = jnp.where(m, l2, _roll(l2, HEAD))
            eh = jnp.where(m, e2, _roll(e2, HEAD))
            return jnp.where(m, q2, jnp.zeros_like(q2)), jnp.where(m, do2, 0.0).astype(BF16), lh, eh

        for i in range(nq):
            sl = slice(i * 128, (i + 1) * 128)
            qf[0:T, :] = qc_ref[:, sl].astype(F32) * scale
            qf[T:T + H, :] = qn_ref[:, sl].astype(F32) * scale
            for buf, c_ref, n_ref in ((lf, lc_ref, ln_ref), (ef, ec_ref, en_ref), (df, dc_ref, dn_ref)):
                buf[0:T, :] = c_ref[:, sl]
                buf[T:T + H, :] = n_ref[:, sl]
            if has_sink:
                @pl.when(t == 0)
                def _():
                    outs[3][:, sl] = jnp.zeros((1, LANES), F32)

                outs[3][:, sl] += jnp.sum(-jnp.exp(sk_ref[:, sl] - lc_ref[:, sl]) * ec_ref[:, sl], axis=0, keepdims=True)
            for p, (dist, r) in enumerate(patterns):
                band_q = (dist_q >= 0) & (dist_q <= dist)
                band_first = band_q & ((c_i >= BLK) | (t > 0))
                band_h = (dist_h >= 0) & (dist_h <= dist)

                def add_rows(ref, start, val, r=r):
                    _set_rows(ref, start, BLK, r, _rows(ref, start, BLK, r) + val)

                def unit(j, b, p=p, r=r, band_q=band_q, band_first=band_first):
                    q0 = j + b * (BLK * r)
                    q2 = _rows(qf, q0, BLK, r).astype(BF16)
                    do2, l2, e2 = _rows(df, q0, BLK, r), _rows(lf, q0, BLK, r), _rows(ef, q0, BLK, r)
                    kcat = _rows(kf, H + q0 - BLK * r, 2 * BLK, r).astype(BF16)
                    vcat = _rows(vf, H + q0 - BLK * r, 2 * BLK, r).astype(BF16)
                    valid = band_q if b > 0 else band_first
                    dq2 = jnp.zeros((BLK, LANES), F32)
                    dkc = jnp.zeros((2 * BLK, LANES), F32)
                    dvc = jnp.zeros((2 * BLK, LANES), F32)
                    for half in range(2):
                        qm, dom, lh, eh = head_inputs(half, q2, do2, l2, e2)
                        s = lax.dot_general(qm, kcat, nt_dims, preferred_element_type=F32)
                        pr = jnp.where(valid, jnp.exp(s - jnp.concatenate([lh, lh], axis=1)), 0.0)
                        dp = lax.dot_general(dom, vcat, nt_dims, preferred_element_type=F32)
                        ds = (pr * (dp - jnp.concatenate([eh, eh], axis=1))).astype(BF16)
                        dq2 = jnp.where(m1[half], jnp.dot(ds, kcat, preferred_element_type=F32) * scale, dq2)
                        dvc += lax.dot_general(pr.astype(BF16), dom, tn_dims, preferred_element_type=F32)
                        dkc += lax.dot_general(ds, qm, tn_dims, preferred_element_type=F32)
                    if p > 0:
                        dq2 = dq2 + _rows(dq_ref.at[:, sl], q0, BLK, r)
                    _set_rows(dq_ref.at[:, sl], q0, BLK, r, dq2)
                    add_rows(dk_ref, q0, dkc[BLK:])
                    add_rows(dv_ref, q0, dvc[BLK:])
                    if b > 0:
                        add_rows(dk_ref, q0 - BLK * r, dkc[:BLK])
                        add_rows(dv_ref, q0 - BLK * r, dvc[:BLK])

                def halo_unit(j, r=r, band_h=band_h):
                    k0 = j + (nbt // r - 1) * (BLK * r)
                    q2 = _rows(qf, T + j, BLK, r).astype(BF16)
                    do2, l2, e2 = _rows(df, T + j, BLK, r), _rows(lf, T + j, BLK, r), _rows(ef, T + j, BLK, r)
                    kc = _rows(kf, H + k0, BLK, r).astype(BF16)
                    vc = _rows(vf, H + k0, BLK, r).astype(BF16)
                    dk2 = jnp.zeros((BLK, LANES), F32)
                    dv2 = jnp.zeros((BLK, LANES), F32)
                    for half in range(2):
                        qm, dom, lh, eh = head_inputs(half, q2, do2, l2, e2)
                        s = lax.dot_general(qm, kc, nt_dims, preferred_element_type=F32)
                        pr = jnp.where(band_h, jnp.exp(s - lh), 0.0)
                        dp = lax.dot_general(dom, vc, nt_dims, preferred_element_type=F32)
                        ds = (pr * (dp - eh)).astype(BF16)
                        dv2 += lax.dot_general(pr.astype(BF16), dom, tn_dims, preferred_element_type=F32)
                        dk2 += lax.dot_general(ds, qm, tn_dims, preferred_element_type=F32)
                    add_rows(dk_ref, k0, dk2)
                    add_rows(dv_ref, k0, dv2)

                for u in range(nbt):
                    unit(u % r, u // r)
                if nt > 1:
                    @pl.when(t < nt - 1)
                    def _(r=r, halo_unit=halo_unit):
                        for j in range(r):
                            halo_unit(j)

    th = T // H
    last = S // H - 1
    qcur = pl.BlockSpec((T, nq * 128), lambda j, t: (t, j))
    qnext = pl.BlockSpec((H, nq * 128), lambda j, t: (jnp.minimum((t + 1) * th, last), j))
    cur = pl.BlockSpec((T, 128), lambda j, t: (t, j))
    prev = pl.BlockSpec((H, 128), lambda j, t: (jnp.maximum(t * th - 1, 0), j))
    dcur = pl.BlockSpec((T, nq * 128), lambda j, t: (t, j + do_col0))
    dnext = pl.BlockSpec((H, nq * 128), lambda j, t: (jnp.minimum((t + 1) * th, last), j + do_col0))
    in_specs = [qcur, qnext, prev, cur, prev, cur, qcur, qnext, qcur, qnext, dcur, dnext]
    args = [q, q, k, k, v, v, lse, lse, delta, delta, do, do]
    out_specs = [qcur, cur, cur]
    out_shape = [jax.ShapeDtypeStruct(q.shape, F32), jax.ShapeDtypeStruct(k.shape, F32), jax.ShapeDtypeStruct(k.shape, F32)]
    if has_sink:
        vec = pl.BlockSpec((1, nq * 128), lambda j, t: (0, j))
        in_specs.append(vec)
        args.append(sinks)
        out_specs.append(vec)
        out_shape.append(jax.ShapeDtypeStruct((1, q.shape[1]), F32))
    big = pltpu.VMEM((T + H, LANES), F32)
    return _pc(
        body, name=name, grid=(ncol, nt), in_specs=in_specs, out_specs=out_specs, out_shape=out_shape,
        scratch_shapes=[big] * 6,
        compiler_params=_params(("parallel", "arbitrary")),
    )(*args)


def _delta(do, o, name):
    S, C = do.shape
    tm = 512

    def body(do_ref, o_ref, g_ref, e_ref):
        for c in range(C // LANES):
            sl = slice(c * 128, (c + 1) * 128)
            e_ref[:, sl] = _gmean(do_ref[:, sl] * o_ref[:, sl], g_ref[...]) * float(HEAD)

    row = pl.BlockSpec((tm, C), lambda i: (i, 0))
    return _pc(
        body, name=name, grid=(S // tm,),
        in_specs=[row, row, pl.BlockSpec((LANES, LANES), lambda i: (0, 0))], out_specs=row,
        out_shape=jax.ShapeDtypeStruct((S, C), F32),
        compiler_params=_params(("parallel",)),
    )(do, o, _group_matrix())


def _even_post_fwd(ro, proj, gn, da):
    S = ro.shape[0]
    tm = 256

    def body(ro_ref, rg_ref, gn_ref, da_ref, mix_ref):
        for c in range(4):
            sl = slice(c * 128, (c + 1) * 128)
            x = ro_ref[:, sl]
            mu = jnp.mean(x, axis=1, keepdims=True)
            xc = x - mu
            var = jnp.mean(xc * xc, axis=1, keepdims=True)
            y = xc * lax.rsqrt(var + EPS) * gn_ref[:, sl]
            z = rg_ref[:, sl]
            mix_ref[:, sl] = (z * jax.nn.sigmoid(z) * y).astype(BF16)
        mix_ref[:, 512:1024] = da_ref[...].astype(BF16)

    row = lambda w: pl.BlockSpec((tm, w), lambda i: (i, 0))
    return _pc(
        body, name="even_post_fwd", grid=(S // tm,),
        in_specs=[row(512), pl.BlockSpec((tm, 512), lambda i: (i, 2)), pl.BlockSpec((1, 512), lambda i: (0, 0)), row(512)],
        out_specs=row(1024), out_shape=jax.ShapeDtypeStruct((S, 1024), BF16),
        compiler_params=_params(("parallel",)),
    )(ro, proj, gn, da)


def _even_post_bwd(ro, proj, gn, dmixed):
    S = ro.shape[0]
    tm = 256

    def body(ro_ref, rg_ref, gn_ref, dm_ref, dro_ref, drg_ref, dgn_ref):
        @pl.when(pl.program_id(0) == 0)
        def _():
            dgn_ref[...] = jnp.zeros_like(dgn_ref)

        for c in range(4):
            sl = slice(c * 128, (c + 1) * 128)
            x = ro_ref[:, sl]
            mu = jnp.mean(x, axis=1, keepdims=True)
            xc = x - mu
            rstd = lax.rsqrt(jnp.mean(xc * xc, axis=1, keepdims=True) + EPS)
            xh = xc * rstd
            gain = gn_ref[:, sl]
            y = xh * gain
            z = rg_ref[:, sl]
            sg = jax.nn.sigmoid(z)
            dra = dm_ref[:, sl]
            drg_ref[:, sl] = dra * y * sg * (1.0 + z * (1.0 - sg))
            dy = dra * z * sg
            dgn_ref[:, sl] += jnp.sum(dy * xh, axis=0, keepdims=True)
            dxh = dy * gain
            dro_ref[:, sl] = rstd * (dxh - jnp.mean(dxh, axis=1, keepdims=True)
                                     - xh * jnp.mean(dxh * xh, axis=1, keepdims=True))

    row = lambda w: pl.BlockSpec((tm, w), lambda i: (i, 0))
    vec = pl.BlockSpec((1, 512), lambda i: (0, 0))
    return _pc(
        body, name="even_post_bwd", grid=(S // tm,),
        in_specs=[row(512), pl.BlockSpec((tm, 512), lambda i: (i, 2)), vec, row(512)],
        out_specs=[row(512), row(512), vec],
        out_shape=[jax.ShapeDtypeStruct((S, 512), F32), jax.ShapeDtypeStruct((S, 512), F32),
                   jax.ShapeDtypeStruct((1, 512), F32)],
        compiler_params=_params(("arbitrary",)),
    )(ro, proj, gn, dmixed)


def _swa_pre_fwd(proj, tab, qg, kg):
    S = proj.shape[0]
    tm = 256

    def body(p_ref, tab_ref, qg_ref, kg_ref, g_ref, q_ref, k_ref, v_ref):
        Ap, Bp, Cp = _tab(tab_ref, 1)
        G = g_ref[...]
        lo = _head_mask((tm, LANES), 0)
        for c in range(8):
            sl = slice(c * 128, (c + 1) * 128)
            q_ref[:, sl] = _rope(_hn_fwd(p_ref[:, sl], qg_ref[...], G), Ap, Bp, Cp, 8).astype(BF16)
        for c in range(2):
            kn = _rope(_hn_fwd(p_ref[:, 1024 + c * 128:1024 + (c + 1) * 128], kg_ref[...], G), Ap, Bp, Cp, 8)
            vv = p_ref[:, 1280 + c * 128:1280 + (c + 1) * 128]
            for t, ref in ((kn, k_ref), (vv, v_ref)):
                sw = _roll(t, HEAD)
                ref[:, (2 * c) * 128:(2 * c + 1) * 128] = jnp.where(lo, t, sw).astype(BF16)
                ref[:, (2 * c + 1) * 128:(2 * c + 2) * 128] = jnp.where(lo, sw, t).astype(BF16)

    row = lambda w: pl.BlockSpec((tm, w), lambda i: (i, 0))
    vec = pl.BlockSpec((1, LANES), lambda i: (0, 0))
    return _pc(
        body, name="swa_pre_fwd", grid=(S // tm,),
        in_specs=[row(1536), row(768), vec, vec, pl.BlockSpec((LANES, LANES), lambda i: (0, 0))],
        out_specs=[row(1024), row(512), row(512)],
        out_shape=[jax.ShapeDtypeStruct((S, w), BF16) for w in (1024, 512, 512)],
        compiler_params=_params(("parallel",)),
    )(proj, tab, qg, kg, _group_matrix())


def _swa_pre_bwd(proj, tab, qg, kg, dq, dk, dv):
    S = proj.shape[0]
    tm = 256

    def body(p_ref, tab_ref, qg_ref, kg_ref, g_ref, dq_ref, dk_ref, dv_ref, dp_ref, db_ref, dqg_ref, dkg_ref):
        Ap, Bp, Cp = _tab(tab_ref, 1)
        G = g_ref[...]
        lo = _head_mask((tm, LANES), 0)

        @pl.when(pl.program_id(0) == 0)
        def _():
            db_ref[...] = jnp.zeros_like(db_ref)
            dqg_ref[...] = jnp.zeros_like(dqg_ref)
            dkg_ref[...] = jnp.zeros_like(dkg_ref)

        accq = jnp.zeros((1, LANES), F32)
        acck = jnp.zeros((1, LANES), F32)
        for c in range(8):
            sl = slice(c * 128, (c + 1) * 128)
            dx, dg = _hn_bwd(p_ref[:, sl], qg_ref[...], _rope_t(dq_ref[:, sl], Ap, Bp, Cp, 8), G)
            dp_ref[:, sl] = dx.astype(BF16)
            db_ref[:, sl] += jnp.sum(dx, axis=0, keepdims=True)
            accq = accq + dg
        for c in range(2):
            folded = []
            for ref in (dk_ref, dv_ref):
                a = ref[:, (2 * c) * 128:(2 * c + 1) * 128]
                b = ref[:, (2 * c + 1) * 128:(2 * c + 2) * 128]
                folded.append(jnp.where(lo, a + _roll(a, HEAD), b + _roll(b, HEAD)))
            ks = slice(1024 + c * 128, 1024 + (c + 1) * 128)
            dx, dg = _hn_bwd(p_ref[:, ks], kg_ref[...], _rope_t(folded[0], Ap, Bp, Cp, 8), G)
            dp_ref[:, ks] = dx.astype(BF16)
            db_ref[:, ks] += jnp.sum(dx, axis=0, keepdims=True)
            acck = acck + dg
            vs = slice(1280 + c * 128, 1280 + (c + 1) * 128)
            dp_ref[:, vs] = folded[1].astype(BF16)
            db_ref[:, vs] += jnp.sum(folded[1], axis=0, keepdims=True)
        dqg_ref[...] += _fold_halves(accq)
        dkg_ref[...] += _fold_halves(acck)

    row = lambda w: pl.BlockSpec((tm, w), lambda i: (i, 0))
    vec = pl.BlockSpec((1, LANES), lambda i: (0, 0))
    return _pc(
        body, name="swa_pre_bwd", grid=(S // tm,),
        in_specs=[row(1536), row(768), vec, vec, pl.BlockSpec((LANES, LANES), lambda i: (0, 0)),
                  row(1024), row(512), row(512)],
        out_specs=[row(1536), pl.BlockSpec((1, 1536), lambda i: (0, 0)), vec, vec],
        out_shape=[jax.ShapeDtypeStruct((S, 1536), BF16), jax.ShapeDtypeStruct((1, 1536), F32),
                   jax.ShapeDtypeStruct((1, LANES), F32), jax.ShapeDtypeStruct((1, LANES), F32)],
        compiler_params=_params(("arbitrary",)),
    )(proj, tab, qg, kg, _group_matrix(), dq, dk, dv)


def _loss_head(y, target):
    S, Dm = y.shape
    tm = 512

    def body(y_ref, t_ref, l_ref, dy_ref, dyb_ref):
        @pl.when(pl.program_id(0) == 0)
        def _():
            l_ref[...] = jnp.zeros_like(l_ref)

        e = y_ref[...] - t_ref[...]
        dy = e * (1.0 / Dm)
        dy_ref[...] = dy
        dyb_ref[...] = dy.astype(BF16)
        row = jnp.sum(e * e, axis=1, keepdims=True) * (0.5 / Dm)
        l_ref[...] += jnp.sum(row, axis=0, keepdims=True)

    row = pl.BlockSpec((tm, Dm), lambda i: (i, 0))
    return _pc(
        body, name="loss_head", grid=(S // tm,), in_specs=[row, row],
        out_specs=[pl.BlockSpec((1, LANES), lambda i: (0, 0)), row, row],
        out_shape=[jax.ShapeDtypeStruct((1, LANES), F32), jax.ShapeDtypeStruct((S, Dm), F32),
                   jax.ShapeDtypeStruct((S, Dm), BF16)],
        compiler_params=_params(("arbitrary",)),
    )(y, target)


def _relu2_of(u):
    r = jnp.maximum(u.astype(F32), 0.0)
    return r * r


def _drelu2(acc, u):
    return (acc * 2.0 * jnp.maximum(u.astype(F32), 0.0),)


def _add(acc, res):
    return (acc + res,)


_T = dict(tm=1024, tn=1024, tk=1024)


def _rms_bwd_in(x, g, dres):
    def epilogue(dh, xv, gv, dr):
        r = lax.rsqrt(jnp.mean(xv * xv, axis=-1, keepdims=True) + EPS)
        t = dh * gv
        dx = dr + r * t - xv * (r * r * r) * jnp.mean(xv * t, axis=-1, keepdims=True)
        return dx, dx, jnp.sum(dh * xv * r, axis=0, keepdims=True)

    return dict(outs=[F32, BF16, ("colsum",)], epilogue=epilogue,
                extras=[(x, "mn"), (g.reshape(1, D_MODEL), "n"), (dres, "mn")])


def _delta_in(o, col0):
    width = D_MODEL - col0

    def epilogue(do, ov, G):
        parts = [_gmean(do[:, col0 + c * 128:col0 + (c + 1) * 128] * ov[:, c * 128:(c + 1) * 128], G) * float(HEAD)
                 for c in range(width // LANES)]
        return do, jnp.concatenate(parts, axis=1)

    return dict(outs=[F32, (F32, width)], epilogue=epilogue, extras=[(o, width), (_group_matrix(), "full")])


def _mlp_fwd(x, g, w_up, w_dn, tag):
    h = _rms_fwd(x, g, f"rms_mlp_fwd{tag}")
    u = _matmul(h, w_up, dims="nn", **_T, outs=[BF16], b_cs=True, name=f"mlp_up{tag}")
    x_out = _matmul(u, w_dn, dims="nn", **_T, outs=[F32], epilogue=_add, extras=[(x, "mn")], a_pro=_relu2_of,
                    name=f"mlp_down{tag}")
    return x_out, (h, u)


def _mlp_bwd(x, g, w_up, w_dn, saved, dy, dyb, tag):
    h, u = saved
    du = _matmul(dyb, w_dn, dims="nt", **_T, outs=[BF16], epilogue=_drelu2, extras=[(u, "mn")], name=f"mlp_du{tag}")
    dw_dn = _matmul(u, dyb, dims="tn", **_T, outs=[F32], a_pro=_relu2_of, name=f"mlp_dwdown{tag}")
    dw_up = _matmul(h, du, dims="tn", **_T, outs=[F32], o_cs=N_CHIPS, name=f"mlp_dwup{tag}")
    dx, dxb, dg = _matmul(du, w_up, dims="nt", tm=512, tn=1024, tk=1024, b_cs=True, name=f"mlp_dh{tag}", **_rms_bwd_in(x, g, dy))
    return dx, dxb, dg, dw_up, dw_dn


def _pattern_view(t, r):
    S, C = t.shape
    return t.reshape(S // r, r * C)


def _local_step(x, pos_col, target, W, rest_of, P, red):
    S = x.shape[0]
    tab = _tables(pos_col)
    tile2 = lambda g: jnp.tile(g.reshape(1, HEAD), (1, 2))
    dqg, dkg = tile2(P["dil_q_gain"]), tile2(P["dil_k_gain"])
    sqg, skg = tile2(P["swa_q_gain"]), tile2(P["swa_k_gain"])
    gn = P["ret_gn_gain"].reshape(1, 512)
    sink_b = jnp.repeat(P["swa_sinks"].reshape(16), HEAD).reshape(1, 1024)

    h0 = _rms_fwd(x, P["norm_mix"][0], "rms_mix_fwd0")
    proj = _matmul(h0, W["hyb_w_in"], dims="nn", tm=1024, tn=768, tk=1024, outs=[F32], b_cs=True, name="hyb_in")
    rq, rk, rv, dq, dk, dv = _even_pre_fwd(proj, tab, dqg, dkg)
    ro, states = _ret_fwd(rq, rk, rv)
    dil = [(w // r, r) for w, r in DIL_PATTERNS]
    da, dlse = _band_fwd(dq, dk, dv, patterns=dil, nq=1, name="dil_fwd")
    mixed = _even_post_fwd(ro, proj, gn, da)
    x1 = _matmul(mixed, W["hyb_w_out"], dims="nn", **_T, outs=[F32], epilogue=_add, extras=[(x, "mn")], name="hyb_out")
    rest, bias = rest_of(x1)
    W = {**W, **rest}
    x2, mlp0 = _mlp_fwd(x1, P["norm_mlp"][0], W["mlp_w_up"][0], W["mlp_w_down"][0], "0")

    h2 = _rms_fwd(x2, P["norm_mix"][1], "rms_mix_fwd1")
    proj2 = _matmul(h2, W["swa_w_qkv"], dims="nn", tm=1024, tn=384, tk=1024, outs=[F32], b_cs=True,
                    epilogue=_add, extras=[(bias.reshape(1, 1536), "n")], name="swa_qkv")
    sq, sk, sv = _swa_pre_fwd(proj2, tab, sqg, skg)
    swa = [(SWA_DIST, 1)]
    so, slse, so_b = _band_fwd(sq, sk, sv, patterns=swa, nq=2, name="swa_fwd", sinks=sink_b, want_bf16=True)
    x3 = _matmul(so_b, W["swa_w_out"], dims="nn", **_T, outs=[F32], epilogue=_add, extras=[(x2, "mn")], name="swa_out")
    y, mlp1 = _mlp_fwd(x3, P["norm_mlp"][1], W["mlp_w_up"][1], W["mlp_w_down"][1], "1")
    loss, dy, dyb = _loss_head(y, target)

    gw, gp = {}, {}
    dx3, dx3b, dg_mlp1, gw["mlp_w_up1"], gw["mlp_w_down1"] = _mlp_bwd(x3, P["norm_mlp"][1], W["mlp_w_up"][1],
                                                                       W["mlp_w_down"][1], mlp1, dy, dyb, "1")
    dx3b = red.begin("mlp1", {n: (gw[n], 1024) for n in ("mlp_w_up1", "mlp_w_down1")}, dx3b)
    gw["swa_w_out"] = _matmul(so_b, dx3b, dims="tn", **_T, outs=[F32], name="swa_dwout")
    dso, sdelta = _matmul(dx3b, W["swa_w_out"], dims="nt", tm=512, tn=1024, tk=1024, name="swa_do", **_delta_in(so, 0))
    dsq, dsk, dsv, dsink = _band_bwd(sq, sk, sv, slse, sdelta, dso, patterns=swa, nq=2, name="swa_bwd", sinks=sink_b)
    dproj2, gp["swa_b_qkv"], gp["swa_q_gain"], gp["swa_k_gain"] = _swa_pre_bwd(proj2, tab, sqg, skg, dsq, dsk, dsv)
    gp["swa_sinks"] = dsink
    gw["swa_w_qkv"] = _matmul(h2, dproj2, dims="tn", tm=1024, tn=384, tk=1024, outs=[F32], o_cs=N_CHIPS, name="swa_dwqkv")
    dx2, dx2b, dg_mix1 = _matmul(dproj2, W["swa_w_qkv"], dims="nt", tm=512, tn=1024, tk=384, b_cs=True, name="swa_dh",
                                 **_rms_bwd_in(x2, P["norm_mix"][1], dx3))
    dx2b = red.begin("swa", {"swa_w_qkv": (gw["swa_w_qkv"], 1024), "swa_w_out": (gw["swa_w_out"], 256)}, dx2b)
    dx2b = red.advance("mlp1", dx2b, dx2b)

    dx1, dx1b, dg_mlp0, gw["mlp_w_up0"], gw["mlp_w_down0"] = _mlp_bwd(x1, P["norm_mlp"][0], W["mlp_w_up"][0],
                                                                       W["mlp_w_down"][0], mlp0, dx2, dx2b, "0")
    dx1b = red.begin("mlp0", {n: (gw[n], 1024) for n in ("mlp_w_up0", "mlp_w_down0")}, dx1b)
    dx1b = red.advance("swa", dx1b, dx1b)
    red.finish("mlp1", dx1b)
    gw["hyb_w_out"] = _matmul(mixed, dx1b, dims="tn", **_T, outs=[F32], name="hyb_dwout")
    dmixed, ddelta = _matmul(dx1b, W["hyb_w_out"], dims="nt", tm=512, tn=1024, tk=1024, name="hyb_dmixed", **_delta_in(da, 512))
    dro, drg, gp["ret_gn_gain"] = _even_post_bwd(ro, proj, gn, dmixed)
    drq, drk, drv = _ret_bwd(rq, rk, rv, states, dro)
    ddq, ddk, ddv = _band_bwd(dq, dk, dv, dlse, ddelta, dmixed, patterns=dil, nq=1, name="dil_bwd", do_col0=4)
    ddq = red.advance("mlp0", ddq, ddq)
    red.finish("swa", ddq)
    dproj, gp["dil_q_gain"], gp["dil_k_gain"] = _even_pre_bwd(proj, tab, dqg, dkg, drq, drk, drv, drg, [ddq], [ddk], [ddv])
    gw["hyb_w_in"] = _matmul(h0, dproj, dims="tn", tm=1024, tn=768, tk=1024, outs=[F32], o_cs=N_CHIPS, name="hyb_dwin")
    grad_x, _, dg_mix0 = _matmul(dproj, W["hyb_w_in"], dims="nt", tm=512, tn=1024, tk=768, b_cs=True, name="hyb_dh",
                                 **_rms_bwd_in(x, P["norm_mix"][0], dx1))
    red.finish("mlp0", grad_x)
    red.last({"hyb_w_in": (gw["hyb_w_in"], 1024), "hyb_w_out": (gw["hyb_w_out"], 256)})
    gp["norm_mix"] = jnp.concatenate([dg_mix0, dg_mix1], axis=0)
    gp["norm_mlp"] = jnp.concatenate([dg_mlp0, dg_mlp1], axis=0)
    return loss, grad_x, gp


HBM = pl.BlockSpec(memory_space=pltpu.HBM)


def _place():
    x, y, c = lax.axis_index("x"), lax.axis_index("y"), lax.axis_index("c")
    chips = [(1 - x, y), (x, 1 - y), (1 - x, 1 - y)]
    return x, y, c, chips


def _allgather_shards(buf):
    _, R, Wd = buf.shape
    Rh = R // 2

    def body(b_ref, out_ref, send_sems, recv_sems):
        x, y, c, chips = _place()
        sibling = (x, y, 1 - c)

        def copy(k, chip, core, to):
            block = b_ref.at[2 * chip[0] + chip[1], pl.ds(core * Rh, Rh), :]
            return pltpu.make_async_remote_copy(
                src_ref=block, dst_ref=block, send_sem=send_sems.at[k], recv_sem=recv_sems.at[k],
                device_id=to, device_id_type=MESH)

        first = [copy(k, (x, y), c, (*chip, c)) for k, chip in enumerate(chips)]
        for cp in first:
            cp.start()
        passed = [copy(3 + k, chip, c, sibling) for k, chip in enumerate(chips)]
        for k, chip in enumerate(chips):
            copy(k, chip, c, (x, y, c)).wait_recv()
            passed[k].start()
        for k, chip in enumerate(chips):
            copy(3 + k, chip, 1 - c, (x, y, c)).wait_recv()
        for cp in first + passed:
            cp.wait_send()

    return _pc(
        body, name="allgather_first", in_specs=[HBM], out_specs=HBM,
        out_shape=jax.ShapeDtypeStruct(buf.shape, buf.dtype), input_output_aliases={0: 0},
        scratch_shapes=[pltpu.SemaphoreType.DMA((6,)), pltpu.SemaphoreType.DMA((6,))],
    )(buf)


SEM = pl.BlockSpec(memory_space=pltpu.SEMAPHORE)
EFFECT = pltpu.SideEffectType.DATAFLOW_SIDE_EFFECTING


def _half_block(ref, chip, core):
    rh = ref.shape[1] // 2
    return ref.at[2 * chip[0] + chip[1], pl.ds(core * rh, rh), :]


def _gather_start(buf, ride):
    def body(b_ref, ride_ref, s0, s1, s2, r0, r1, r2, b_out, ride_out):
        x, y, c, chips = _place()
        for chip, s, r in zip(chips, (s0, s1, s2), (r0, r1, r2)):
            mine = _half_block(b_ref, (x, y), c)
            pltpu.make_async_remote_copy(src_ref=mine, dst_ref=mine, send_sem=s, recv_sem=r,
                                         device_id=(*chip, c), device_id_type=MESH).start()

    sem = pltpu.SemaphoreType.DMA(())
    return _pc(
        body, name="allgather_rest_start",
        out_shape=(sem,) * 6 + (pltpu.HBM(buf.shape, buf.dtype), pltpu.HBM(ride.shape, ride.dtype)),
        in_specs=(HBM, HBM), out_specs=(SEM,) * 6 + (HBM, HBM), input_output_aliases={0: 6, 1: 7},
        compiler_params=pltpu.CompilerParams(has_side_effects=EFFECT),
    )(pltpu.with_memory_space_constraint(buf, pltpu.HBM), pltpu.with_memory_space_constraint(ride, pltpu.HBM))


def _gather_wait(buf, sems, after):
    def body(b_ref, s0, s1, s2, r0, r1, r2, after_ref, b_out):
        x, y, c, chips = _place()
        for chip, s, r in zip(chips, (s0, s1, s2), (r0, r1, r2)):
            cp = pltpu.make_async_remote_copy(src_ref=_half_block(b_ref, (x, y), c), dst_ref=_half_block(b_ref, chip, c),
                                              send_sem=s, recv_sem=r, device_id=(*chip, c), device_id_type=MESH)
            cp.wait_send()
            cp.wait_recv()

    return _pc(
        body, name="allgather_rest_wait", out_shape=pltpu.HBM(buf.shape, buf.dtype),
        in_specs=(HBM,) + (SEM,) * 6 + (pl.BlockSpec(memory_space=pl.ANY),), out_specs=HBM, input_output_aliases={0: 0},
        compiler_params=pltpu.CompilerParams(has_side_effects=EFFECT),
    )(buf, *sems, after)


def _gather_handover(buf):
    def body(b_ref, out_ref, send_sems, recv_sems):
        x, y, c, chips = _place()
        cps = []
        for k, chip in enumerate(chips):
            mine = _half_block(b_ref, chip, c)
            cps.append(pltpu.make_async_remote_copy(src_ref=mine, dst_ref=mine, send_sem=send_sems.at[k],
                                                    recv_sem=recv_sems.at[k], device_id=(x, y, 1 - c), device_id_type=MESH))
        for cp in cps:
            cp.start()
        for k, chip in enumerate(chips):
            theirs = _half_block(b_ref, chip, 1 - c)
            pltpu.make_async_remote_copy(src_ref=theirs, dst_ref=theirs, send_sem=send_sems.at[k], recv_sem=recv_sems.at[k],
                                         device_id=(x, y, 1 - c), device_id_type=MESH).wait_recv()
        for cp in cps:
            cp.wait_send()

    return _pc(
        body, name="allgather_rest_handover", in_specs=[HBM], out_specs=HBM,
        out_shape=jax.ShapeDtypeStruct(buf.shape, buf.dtype), input_output_aliases={0: 0},
        scratch_shapes=[pltpu.SemaphoreType.DMA((3,)), pltpu.SemaphoreType.DMA((3,))],
    )(buf)


def _swap_halves(ts):
    nt = len(ts)

    def body(*refs):
        t_refs, l_refs, send_sems, recv_sems = refs[:nt], refs[nt:2 * nt], refs[-2], refs[-1]
        x, y, c, _ = _place()
        cps = []
        for k in range(nt):
            rh = t_refs[k].shape[1] // 2
            cps.append(pltpu.make_async_remote_copy(
                src_ref=t_refs[k].at[:, pl.ds((1 - c) * rh, rh), :], dst_ref=l_refs[k],
                send_sem=send_sems.at[k], recv_sem=recv_sems.at[k], device_id=(x, y, 1 - c), device_id_type=MESH))
        for cp in cps:
            cp.start()
        for cp in cps:
            cp.wait()

    return _pc(
        body, name="grad_swap_halves", in_specs=[HBM] * nt, out_specs=[HBM] * nt,
        out_shape=[jax.ShapeDtypeStruct((t.shape[0], t.shape[1] // 2, t.shape[2]), F32) for t in ts],
        scratch_shapes=[pltpu.SemaphoreType.DMA((nt,)), pltpu.SemaphoreType.DMA((nt,))],
    )(*ts)


def _pair_sum(t, l, place, name):
    _, r, cols = t.shape
    rh = r // 2
    tr = min(rh, 256)
    nr = rh // tr

    def body(pl_ref, t_ref, l_ref, o_ref):
        o_ref[...] = (t_ref[...] + l_ref[...]).astype(BF16)

    return _pc(
        body, name=name,
        grid_spec=pltpu.PrefetchScalarGridSpec(
            num_scalar_prefetch=1, grid=(N_CHIPS, nr),
            in_specs=[pl.BlockSpec((None, tr, cols), lambda s, i, p: (s, p[1] * nr + i, 0)),
                      pl.BlockSpec((None, tr, cols), lambda s, i, p: (s, i, 0))],
            out_specs=pl.BlockSpec((None, tr, cols), lambda s, i, p: (s, i, 0))),
        out_shape=jax.ShapeDtypeStruct((N_CHIPS, rh, cols), BF16),
        compiler_params=_params(("parallel", "parallel")),
    )(place, t, l)


def _exchange_chips(ps):
    nt = len(ps)

    def body(*refs):
        p_refs, r_refs, send_sems, recv_sems = refs[:nt], refs[nt:2 * nt], refs[-2], refs[-1]
        x, y, c, chips = _place()
        cps = []
        for t in range(nt):
            for k, chip in enumerate(chips):
                cps.append(pltpu.make_async_remote_copy(
                    src_ref=p_refs[t].at[2 * chip[0] + chip[1]], dst_ref=r_refs[t].at[k],
                    send_sem=send_sems.at[3 * t + k], recv_sem=recv_sems.at[3 * t + k],
                    device_id=(*chip, c), device_id_type=MESH))
        for cp in cps:
            cp.start()
        for cp in cps:
            cp.wait()

    return _pc(
        body, name="grad_exchange_chips", in_specs=[HBM] * nt, out_specs=[HBM] * nt,
        out_shape=[jax.ShapeDtypeStruct((3,) + p.shape[1:], BF16) for p in ps],
        scratch_shapes=[pltpu.SemaphoreType.DMA((3 * nt,)), pltpu.SemaphoreType.DMA((3 * nt,))],
    )(*ps)


def _final_sum(t, l, rcv, place, name):
    _, r, cols = t.shape
    rh = r // 2
    tr = min(rh, 256)
    nr = rh // tr

    def body(pl_ref, t_ref, l_ref, r_ref, o_ref):
        acc = t_ref[...] + l_ref[...]
        for k in range(3):
            acc = acc + r_ref[k].astype(F32)
        o_ref[...] = acc

    return _pc(
        body, name=name,
        grid_spec=pltpu.PrefetchScalarGridSpec(
            num_scalar_prefetch=1, grid=(nr,),
            in_specs=[pl.BlockSpec((None, tr, cols), lambda i, p: (p[0], p[1] * nr + i, 0)),
                      pl.BlockSpec((None, tr, cols), lambda i, p: (p[0], i, 0)),
                      pl.BlockSpec((3, tr, cols), lambda i, p: (0, i, 0))],
            out_specs=pl.BlockSpec((tr, cols), lambda i, p: (p[1] * nr + i, 0))),
        out_shape=jax.ShapeDtypeStruct((r, cols), F32),
        compiler_params=_params(("parallel",)),
    )(place, t, l, rcv)


def _share_halves(hs):
    nt = len(hs)

    def body(*refs):
        h_refs, send_sems, recv_sems = refs[:nt], refs[-2], refs[-1]
        x, y, c, _ = _place()
        cps = []
        for k in range(nt):
            rh = h_refs[k].shape[0] // 2
            half = h_refs[k].at[pl.ds(c * rh, rh), :]
            cps.append(pltpu.make_async_remote_copy(
                src_ref=half, dst_ref=half, send_sem=send_sems.at[k], recv_sem=recv_sems.at[k],
                device_id=(x, y, 1 - c), device_id_type=MESH))
        for cp in cps:
            cp.start()
        for cp in cps:
            cp.wait()

    return _pc(
        body, name="grad_share_halves", in_specs=[HBM] * nt, out_specs=[HBM] * nt,
        out_shape=[jax.ShapeDtypeStruct(h.shape, F32) for h in hs],
        input_output_aliases={k: k for k in range(nt)},
        scratch_shapes=[pltpu.SemaphoreType.DMA((nt,)), pltpu.SemaphoreType.DMA((nt,))],
    )(*hs)


def _split_start(name, bufs, ride, n, copies_of):
    nb = len(bufs)

    def body(*refs):
        sems = refs[nb + 1:nb + 1 + 2 * n]
        for cp in copies_of(refs[:nb], sems[:n], sems[n:]):
            cp.start()

    outs = _pc(
        body, name=name,
        out_shape=(pltpu.SemaphoreType.DMA(()),) * (2 * n) + tuple(pltpu.HBM(b.shape, b.dtype) for b in bufs)
        + (pltpu.HBM(ride.shape, ride.dtype),),
        in_specs=(HBM,) * (nb + 1), out_specs=(SEM,) * (2 * n) + (HBM,) * (nb + 1),
        input_output_aliases={k: 2 * n + k for k in range(nb + 1)},
        compiler_params=pltpu.CompilerParams(has_side_effects=EFFECT),
    )(*[pltpu.with_memory_space_constraint(b, pltpu.HBM) for b in bufs], pltpu.with_memory_space_constraint(ride, pltpu.HBM))
    return list(outs[:2 * n]), list(outs[2 * n:2 * n + nb]), outs[-1]


def _split_wait(name, bufs, sems, after, n, copies_of):
    nb = len(bufs)

    def body(*refs):
        s = refs[nb:nb + 2 * n]
        for cp in copies_of(refs[:nb], s[:n], s[n:]):
            cp.wait_send()
            cp.wait_recv()

    outs = _pc(
        body, name=name, out_shape=tuple(pltpu.HBM(b.shape, b.dtype) for b in bufs),
        in_specs=(HBM,) * nb + (SEM,) * (2 * n) + (pl.BlockSpec(memory_space=pl.ANY),), out_specs=(HBM,) * nb,
        input_output_aliases={k: k for k in range(nb)},
        compiler_params=pltpu.CompilerParams(has_side_effects=EFFECT),
    )(*bufs, *sems, after)
    return list(outs)


def _swap_copies(nt):
    def copies_of(refs, send, recv):
        x, y, c, _ = _place()
        cps = []
        for k in range(nt):
            rh = refs[k].shape[1] // 2
            cps.append(pltpu.make_async_remote_copy(
                src_ref=refs[k].at[:, pl.ds((1 - c) * rh, rh), :], dst_ref=refs[nt + k],
                send_sem=send[k], recv_sem=recv[k], device_id=(x, y, 1 - c), device_id_type=MESH))
        return cps
    return copies_of


def _exchange_copies(nt):
    def copies_of(refs, send, recv):
        x, y, c, chips = _place()
        cps = []
        for t in range(nt):
            for k, chip in enumerate(chips):
                cps.append(pltpu.make_async_remote_copy(
                    src_ref=refs[t].at[2 * chip[0] + chip[1]], dst_ref=refs[nt + t].at[k],
                    send_sem=send[3 * t + k], recv_sem=recv[3 * t + k], device_id=(*chip, c), device_id_type=MESH))
        return cps
    return copies_of


class _StagedReduce:
    def __init__(self, place):
        self.place = place
        self.groups = {}
        self.halves = {}

    @staticmethod
    def slab(t, r):
        return t.reshape(N_CHIPS, r, t.size // (N_CHIPS * r))

    def begin(self, g, grads, ride):
        names = list(grads)
        ts = [self.slab(t, r) for t, r in grads.values()]
        lands = [lax.empty((N_CHIPS, t.shape[1] // 2, t.shape[2]), F32) for t in ts]
        sems, bufs, ride = _split_start(f"grad_swap_start_{g}", ts + lands, ride, len(ts), _swap_copies(len(ts)))
        self.groups[g] = dict(names=names, bufs=bufs, sems=sems)
        return ride

    def advance(self, g, after, ride):
        st = self.groups[g]
        nt = len(st["names"])
        bufs = _split_wait(f"grad_swap_wait_{g}", st["bufs"], st["sems"], after, nt, _swap_copies(nt))
        st["ts"], st["ls"] = bufs[:nt], bufs[nt:]
        ps = [_pair_sum(t, l, self.place, f"pair_sum_{n}") for t, l, n in zip(st["ts"], st["ls"], st["names"])]
        lands = [lax.empty((3,) + p.shape[1:], BF16) for p in ps]
        st["sems"], st["bufs"], ride = _split_start(f"grad_exchange_start_{g}", ps + lands, ride, 3 * nt, _exchange_copies(nt))
        return ride

    def finish(self, g, after):
        st = self.groups[g]
        nt = len(st["names"])
        bufs = _split_wait(f"grad_exchange_wait_{g}", st["bufs"], st["sems"], after, 3 * nt, _exchange_copies(nt))
        for t, l, r, n in zip(st["ts"], st["ls"], bufs[nt:], st["names"]):
            self.halves[n] = _final_sum(t, l, r, self.place, f"final_sum_{n}")

    def last(self, grads):
        names = list(grads)
        ts = [self.slab(t, r) for t, r in grads.values()]
        ls = _swap_halves(ts)
        ps = [_pair_sum(t, l, self.place, f"pair_sum_{n}") for t, l, n in zip(ts, ls, names)]
        rs = _exchange_chips(ps)
        for t, l, r, n in zip(ts, ls, rs, names):
            self.halves[n] = _final_sum(t, l, r, self.place, f"final_sum_{n}")


def _allgather_small(v):
    rows = v.shape[0]

    def body(v_ref, out_ref, send_sems, recv_sems):
        x, y, c, _ = _place()
        me = 4 * x + 2 * y + c
        out_ref[me] = v_ref[...]
        cps = []
        for k in range(1, 8):
            fx, fy, fc = (k >> 2) & 1, (k >> 1) & 1, k & 1
            to = (1 - x if fx else x, 1 - y if fy else y, 1 - c if fc else c)
            cps.append(pltpu.make_async_remote_copy(
                src_ref=v_ref, dst_ref=out_ref.at[me], send_sem=send_sems.at[k - 1], recv_sem=recv_sems.at[k - 1],
                device_id=to, device_id_type=MESH))
        for cp in cps:
            cp.start()
        for cp in cps:
            cp.wait()

    return _pc(
        body, name="allgather_small",
        in_specs=[pl.BlockSpec(memory_space=pltpu.VMEM)], out_specs=pl.BlockSpec(memory_space=pltpu.VMEM),
        out_shape=jax.ShapeDtypeStruct((8, rows, LANES), F32),
        scratch_shapes=[pltpu.SemaphoreType.DMA((7,)), pltpu.SemaphoreType.DMA((7,))],
    )(v)


def _adamw_math(w, g, m, v):
    m = ADAM_B1 * m + (1.0 - ADAM_B1) * g
    v = ADAM_B2 * v + (1.0 - ADAM_B2) * (g * g)
    m_hat = m / (1.0 - ADAM_B1 ** ADAM_STEP)
    v_hat = v / (1.0 - ADAM_B2 ** ADAM_STEP)
    return -ADAM_LR * (m_hat / (jnp.sqrt(v_hat) + ADAM_EPS) + ADAM_WD * w), m, v


def _adamw(w, g, m, v, name):
    r, cols = w.shape
    tr = min(r, 256)

    def body(w_ref, g_ref, m_ref, v_ref, d_ref, mo_ref, vo_ref):
        d, mn, vn = _adamw_math(w_ref[...], g_ref[...], m_ref[...], v_ref[...])
        d_ref[...] = d
        mo_ref[...] = mn
        vo_ref[...] = vn

    row = pl.BlockSpec((tr, cols), lambda i: (i, 0))
    return _pc(
        body, name=name, grid=(r // tr,), in_specs=[row] * 4, out_specs=[row] * 3,
        out_shape=[jax.ShapeDtypeStruct((r, cols), F32)] * 3,
        compiler_params=_params(("parallel",)),
    )(w, g, m, v)


def _adamw_small(w, gathered, m, v):
    rows = w.shape[0]

    def body(w_ref, g_ref, m_ref, v_ref, go_ref, d_ref, mo_ref, vo_ref):
        g = g_ref[0]
        for k in range(1, 8):
            g = g + g_ref[k]
        d, mn, vn = _adamw_math(w_ref[...], g, m_ref[...], v_ref[...])
        go_ref[...] = g
        d_ref[...] = d
        mo_ref[...] = mn
        vo_ref[...] = vn

    return _pc(
        body, name="adamw_small",
        out_shape=[jax.ShapeDtypeStruct((rows, LANES), F32)] * 4,
    )(w, gathered, m, v)


_BIAS_ROWS = 32


def _own_slot(flat, chip):
    return lax.dynamic_update_slice(lax.empty((N_CHIPS,) + flat.shape, flat.dtype), flat[None], (chip, 0, 0))


def _pack_first(hyb_w_in, hyb_w_out):
    return jnp.concatenate([t.astype(BF16).reshape(-1, 1024) for t in (hyb_w_in, hyb_w_out)], axis=0)


def _unpack_first(g):
    return {"hyb_w_in": g[:, 0:768, :].reshape(N_CHIPS, 1024, 768), "hyb_w_out": g[:, 768:1024, :].reshape(1024, 1024)}


def _pack_rest(mlp_w_up, mlp_w_down, swa_w_qkv, swa_w_out, swa_b_qkv):
    parts = [t.astype(BF16).reshape(-1, 1024) for t in (mlp_w_up, mlp_w_down, swa_w_qkv, swa_w_out)]
    bias = lax.bitcast_convert_type(swa_b_qkv.reshape(384), BF16).reshape(1, 768)
    bias = jnp.pad(bias, ((0, _BIAS_ROWS - 1), (0, 256)))
    return jnp.concatenate(parts + [bias], axis=0)


def _unpack_rest(g):
    W = {
        "mlp_w_up": [g[:, l * 1024:(l + 1) * 1024, :] for l in range(2)],
        "mlp_w_down": [g[:, 2048 + l * 1024:2048 + (l + 1) * 1024, :].reshape(D_FF, D_MODEL) for l in range(2)],
        "swa_w_qkv": g[:, 4096:4480, :].reshape(N_CHIPS, 1024, 384),
        "swa_w_out": g[:, 4480:4736, :].reshape(1024, 1024),
    }
    bias = lax.bitcast_convert_type(g[:, 4736, :768].reshape(N_CHIPS, 384, 2), F32).reshape(1536)
    return W, bias


_SMALL = (("norm_mix", 16), ("norm_mlp", 16), ("ret_gn_gain", 4), ("dil_q_gain", 1), ("dil_k_gain", 1),
          ("swa_b_qkv", 12), ("swa_q_gain", 1), ("swa_k_gain", 1), ("swa_sinks", 1))
_SUBLANES = 8


def _slot(r):
    return -(-r // _SUBLANES) * _SUBLANES


def _pack_small(d):
    return jnp.concatenate([jnp.pad(d[n].reshape(r, LANES), ((0, _slot(r) - r), (0, 0))) for n, r in _SMALL], axis=0)


def _unpack_small(p):
    out, o = {}, 0
    for n, r in _SMALL:
        out[n] = p[o:o + r]
        o += _slot(r)
    return out


def kernel(x, positions, norm_mix, norm_mlp, mlp_w_up, mlp_w_down, hyb_w_in, hyb_w_out, ret_gn_gain, dil_q_gain, dil_k_gain, swa_w_qkv, swa_b_qkv, swa_w_out, swa_q_gain, swa_k_gain, swa_sinks, loss_target, m_norm_mix, m_norm_mlp, m_mlp_w_up, m_mlp_w_down, m_hyb_w_in, m_hyb_w_out, m_ret_gn_gain, m_dil_q_gain, m_dil_k_gain, m_swa_w_qkv, m_swa_b_qkv, m_swa_w_out, m_swa_q_gain, m_swa_k_gain, m_swa_sinks, v_norm_mix, v_norm_mlp, v_mlp_w_up, v_mlp_w_down, v_hyb_w_in, v_hyb_w_out, v_ret_gn_gain, v_dil_q_gain, v_dil_k_gain, v_swa_w_qkv, v_swa_b_qkv, v_swa_w_out, v_swa_q_gain, v_swa_k_gain, v_swa_sinks):
    ax, ay, ac = lax.axis_index("x"), lax.axis_index("y"), lax.axis_index("c")
    chip = 2 * ax + ay
    place = jnp.stack([chip, ac]).astype(jnp.int32)
    S = x.shape[1]

    first = _allgather_shards(_own_slot(_pack_first(hyb_w_in[0], hyb_w_out[0]), chip))
    rest = _own_slot(_pack_rest(mlp_w_up, mlp_w_down, swa_w_qkv[0], swa_w_out[0], swa_b_qkv[0]), chip)
    *sems, rest, first = _gather_start(rest, first)

    def rest_of(after):
        return _unpack_rest(_gather_handover(_gather_wait(rest, sems, after)))

    P = dict(norm_mix=norm_mix, norm_mlp=norm_mlp, ret_gn_gain=ret_gn_gain, dil_q_gain=dil_q_gain, dil_k_gain=dil_k_gain,
             swa_q_gain=swa_q_gain, swa_k_gain=swa_k_gain, swa_sinks=swa_sinks)

    red = _StagedReduce(place)
    loss_l, grad_x, gp = _local_step(x[0], positions.reshape(S, 1), loss_target[0], _unpack_first(first), rest_of, P, red)
    loss = lax.psum(loss_l[0, 0], ("x", "y", "c"))

    names = ["mlp_w_up0", "mlp_w_up1", "mlp_w_down0", "mlp_w_down1", "hyb_w_in", "hyb_w_out", "swa_w_qkv", "swa_w_out"]
    gs = dict(zip(names, _share_halves([red.halves[n] for n in names])))
    shards = dict(mlp_w_up0=(mlp_w_up[0], m_mlp_w_up[0], v_mlp_w_up[0]), mlp_w_up1=(mlp_w_up[1], m_mlp_w_up[1], v_mlp_w_up[1]),
                  mlp_w_down0=(mlp_w_down[0], m_mlp_w_down[0], v_mlp_w_down[0]),
                  mlp_w_down1=(mlp_w_down[1], m_mlp_w_down[1], v_mlp_w_down[1]),
                  hyb_w_in=(hyb_w_in[0], m_hyb_w_in[0], v_hyb_w_in[0]), hyb_w_out=(hyb_w_out[0], m_hyb_w_out[0], v_hyb_w_out[0]),
                  swa_w_qkv=(swa_w_qkv[0], m_swa_w_qkv[0], v_swa_w_qkv[0]), swa_w_out=(swa_w_out[0], m_swa_w_out[0], v_swa_w_out[0]))
    big = {}
    for n in names:
        w, m, v = shards[n]
        big[n] = (gs[n],) + tuple(_adamw(w, gs[n], m, v, f"adamw_{n}"))

    def big_out(n, k):
        if n in ("mlp_w_up", "mlp_w_down"):
            return jnp.stack([big[n + "0"][k], big[n + "1"][k]])
        return big[n][k][None]

    gsm = dict(gp)
    gsm["swa_sinks"] = jnp.pad(gp["swa_sinks"].reshape(16, HEAD)[:, 0], (0, LANES - 16))
    gathered = _allgather_small(_pack_small(gsm))

    def small_pack(norm_mix, norm_mlp, gn, dq, dk, b, sq, sk, sinks):
        dup = lambda t: jnp.tile(t.reshape(1, HEAD), (1, 2))
        bias = lax.dynamic_update_slice(jnp.zeros((12, LANES), F32), b.reshape(3, LANES), (3 * chip, 0))
        return _pack_small(dict(norm_mix=norm_mix, norm_mlp=norm_mlp, ret_gn_gain=gn, dil_q_gain=dup(dq), dil_k_gain=dup(dk),
                                swa_b_qkv=bias, swa_q_gain=dup(sq), swa_k_gain=dup(sk),
                                swa_sinks=jnp.pad(sinks.reshape(16), (0, LANES - 16))))

    pw = small_pack(norm_mix, norm_mlp, ret_gn_gain, dil_q_gain, dil_k_gain, swa_b_qkv, swa_q_gain, swa_k_gain, swa_sinks)
    pm = small_pack(m_norm_mix, m_norm_mlp, m_ret_gn_gain, m_dil_q_gain, m_dil_k_gain, m_swa_b_qkv, m_swa_q_gain, m_swa_k_gain, m_swa_sinks)
    pv = small_pack(v_norm_mix, v_norm_mlp, v_ret_gn_gain, v_dil_q_gain, v_dil_k_gain, v_swa_b_qkv, v_swa_q_gain, v_swa_k_gain, v_swa_sinks)
    small = [_unpack_small(t) for t in _adamw_small(pw, gathered, pm, pv)]

    def small_out(n, k):
        t = small[k][n]
        if n in ("norm_mix", "norm_mlp"):
            return t.reshape(2, D_MODEL)
        if n == "ret_gn_gain":
            return t.reshape(1, RET_HEADS, 128)
        if n == "swa_b_qkv":
            return lax.dynamic_slice(t, (3 * chip, 0), (3, LANES)).reshape(1, 384)
        if n == "swa_sinks":
            return t[0, :16].reshape(1, 16)
        return t[0, :HEAD].reshape(1, HEAD)

    order = ["norm_mix", "norm_mlp", "mlp_w_up", "mlp_w_down", "hyb_w_in", "hyb_w_out", "ret_gn_gain", "dil_q_gain",
             "dil_k_gain", "swa_w_qkv", "swa_b_qkv", "swa_w_out", "swa_q_gain", "swa_k_gain", "swa_sinks"]
    is_big = {"mlp_w_up", "mlp_w_down", "hyb_w_in", "hyb_w_out", "swa_w_qkv", "swa_w_out"}
    outs = [loss, grad_x[None]]
    for k in range(4):
        outs += [big_out(n, k) if n in is_big else small_out(n, k) for n in order]
    return tuple(outs)
```

```python
import functools
import math

import numpy as np
import jax
import jax.numpy as jnp
from jax import lax
from jax.experimental import pallas as pl
from jax.experimental.pallas import tpu as pltpu

F32, BF16 = jnp.float32, jnp.bfloat16
HIGHEST = lax.Precision.HIGHEST
MESH = pl.DeviceIdType.MESH

LANES = 128
VMEM_LIMIT = 48 << 20
D_MODEL = 1024
D_FF = 4096
HEAD = 64
EPS = 1e-6
BLK = 128
RET_HEADS = 4
RET_THETA = 10000.0
ROPE_THETA = 500000.0
ROPE_DIMS = 16
DIL_PATTERNS = ((128, 1), (512, 4), (2048, 16))
SWA_DIST = 127
N_CHIPS = 4
ADAM_LR, ADAM_B1, ADAM_B2, ADAM_EPS, ADAM_WD, ADAM_STEP = 0.001, 0.9, 0.999, 1e-08, 0.01, 10

_LOG_GAMMA = [float(np.log1p(-np.exp2(np.float32(-5.0 - h)))) for h in range(RET_HEADS)]


def _pc(body, **kw):
    return pl.pallas_call(body, **kw)


def _params(sem):
    return pltpu.CompilerParams(dimension_semantics=sem, vmem_limit_bytes=VMEM_LIMIT)


def _matmul(a, b, *, dims, tm, tn, tk, outs, name, epilogue=None, extras=(), b_cs=False, o_cs=0, a_pro=None):
    if dims == "nn":
        M, K = a.shape
        N = b.shape[0] * b.shape[2] if b_cs else b.shape[1]
        a_spec = pl.BlockSpec((tm, tk), lambda i, j, k: (i, k))
        if b_cs:
            npt = b.shape[2] // tn
            b_spec = pl.BlockSpec((None, tk, tn), lambda i, j, k: (j // npt, k, j % npt))
        else:
            b_spec = pl.BlockSpec((tk, tn), lambda i, j, k: (k, j))
        contract = (((1,), (0,)), ((), ()))
    elif dims == "nt":
        M, K = a.shape
        N = b.shape[1] if b_cs else b.shape[0]
        a_spec = pl.BlockSpec((tm, tk), lambda i, j, k: (i, k))
        if b_cs:
            kpt = b.shape[2] // tk
            b_spec = pl.BlockSpec((None, tn, tk), lambda i, j, k: (k // kpt, j, k % kpt))
        else:
            b_spec = pl.BlockSpec((tn, tk), lambda i, j, k: (j, k))
        contract = (((1,), (1,)), ((), ()))
    else:
        K, M = a.shape
        N = b.shape[1]
        a_spec = pl.BlockSpec((tk, tm), lambda i, j, k: (k, i))
        b_spec = pl.BlockSpec((tk, tn), lambda i, j, k: (k, j))
        contract = (((0,), (0,)), ((), ()))
    assert M % tm == 0 and N % tn == 0 and K % tk == 0, (name, M, N, K, tm, tn, tk)
    nk = K // tk
    ex_specs = []
    for arr, kind in extras:
        if kind == "mn":
            ex_specs.append(pl.BlockSpec((tm, tn), lambda i, j, k: (i, j)))
        elif kind == "n":
            ex_specs.append(pl.BlockSpec((1, tn), lambda i, j, k: (0, j)))
        elif kind == "full":
            ex_specs.append(pl.BlockSpec(arr.shape, lambda i, j, k, nd=arr.ndim: (0,) * nd))
        else:
            ex_specs.append(pl.BlockSpec((tm, kind), lambda i, j, k: (i, 0)))
    if o_cs:
        n_sh = N // o_cs
        opt = n_sh // tn
        o_shape = (o_cs, M, n_sh)
        o_spec = pl.BlockSpec((None, tm, tn), lambda i, j, k: (j // opt, i, j % opt))
    else:
        o_shape = (M, N)
        o_spec = pl.BlockSpec((tm, tn), lambda i, j, k: (i, j))
    o_specs, o_shapes, summed = [], [], []
    for o in outs:
        if isinstance(o, tuple) and o[0] == "colsum":
            assert N == tn
            o_specs.append(pl.BlockSpec((1, tn), lambda i, j, k: (0, j)))
            o_shapes.append(jax.ShapeDtypeStruct((1, N), F32))
            summed.append(True)
        elif isinstance(o, tuple):
            o_specs.append(pl.BlockSpec((tm, o[1]), lambda i, j, k: (i, 0)))
            o_shapes.append(jax.ShapeDtypeStruct((M, o[1]), o[0]))
            summed.append(False)
        else:
            o_specs.append(o_spec)
            o_shapes.append(jax.ShapeDtypeStruct(o_shape, o))
            summed.append(False)
    n_ex, n_out = len(extras), len(outs)
    if epilogue is None:
        epilogue = lambda acc: (acc,)

    def body(a_ref, b_ref, *rest):
        ex, o_refs, acc = rest[:n_ex], rest[n_ex:n_ex + n_out], rest[-1]
        i, k = pl.program_id(0), pl.program_id(2)

        @pl.when(k == 0)
        def _():
            acc[...] = jnp.zeros_like(acc)

        av = a_ref[...] if a_pro is None else a_pro(a_ref[...])
        acc[...] += lax.dot_general(av.astype(BF16), b_ref[...].astype(BF16), contract, preferred_element_type=F32)

        @pl.when(k == nk - 1)
        def _():
            vals = epilogue(acc[...], *[e[...] for e in ex])
            for r, v, sm in zip(o_refs, vals, summed):
                if sm:
                    @pl.when(i == 0)
                    def _(r=r):
                        r[...] = jnp.zeros_like(r)

                    r[...] += v
                else:
                    r[...] = v.astype(r.dtype)

    res = _pc(
        body, name=name, grid=(M // tm, N // tn, nk),
        in_specs=[a_spec, b_spec] + ex_specs, out_specs=o_specs, out_shape=o_shapes,
        scratch_shapes=[pltpu.VMEM((tm, tn), F32)],
        compiler_params=_params(("arbitrary" if any(summed) else "parallel", "parallel", "arbitrary")),
    )(a, b, *[e for e, _ in extras])
    return res[0] if n_out == 1 else res


def _roll(x, s):
    return pltpu.roll(x, s % LANES, 1)


def _rope(x, A, B, C, half):
    return x * A + _roll(x, LANES - half) * B + _roll(x, half) * C


def _rope_t(g, A, B, C, half):
    return g * A + _roll(g * B, half) + _roll(g * C, LANES - half)


def _gmean(x, G):
    return jnp.dot(x, G, precision=HIGHEST, preferred_element_type=F32)


def _head_mask(shape, half):
    lane = lax.broadcasted_iota(jnp.int32, shape, len(shape) - 1)
    return (lane >= HEAD) if half else (lane < HEAD)


def _group_matrix():
    i = np.arange(LANES)
    return jnp.asarray((i[:, None] // HEAD == i[None, :] // HEAD).astype(np.float32) / HEAD)


def _rope_inv():
    l = np.arange(LANES) % HEAD
    inv_r = np.power(np.float32(RET_THETA), -(l % 32).astype(np.float32) * np.float32(2.0 / HEAD))
    hp = ROPE_DIMS // 2
    inv_p = np.power(np.float32(ROPE_THETA), -(l % hp).astype(np.float32) * np.float32(2.0 / ROPE_DIMS))
    inv_p = np.where(l < ROPE_DIMS, inv_p, 0.0)
    return jnp.asarray(np.stack([inv_r, inv_p]).astype(np.float32))


def _tables(pos_col):
    S = pos_col.shape[0]
    tm = 512
    hp = ROPE_DIMS // 2

    def body(p_ref, inv_ref, o_ref):
        p = p_ref[...].astype(F32)
        lane = lax.broadcasted_iota(jnp.int32, (tm, LANES), 1) % HEAD
        ang = p * inv_ref[0:1, :]
        c, s = jnp.cos(ang), jnp.sin(ang)
        o_ref[:, 0:128] = c
        o_ref[:, 128:256] = jnp.where(lane < 32, -s, 0.0)
        o_ref[:, 256:384] = jnp.where(lane >= 32, s, 0.0)
        ang = p * inv_ref[1:2, :]
        c, s = jnp.cos(ang), jnp.sin(ang)
        o_ref[:, 384:512] = c
        o_ref[:, 512:640] = jnp.where(lane < hp, -s, 0.0)
        o_ref[:, 640:768] = jnp.where((lane >= hp) & (lane < ROPE_DIMS), s, 0.0)

    return _pc(
        body, name="rope_tables", grid=(S // tm,),
        in_specs=[pl.BlockSpec((tm, 1), lambda i: (i, 0)), pl.BlockSpec((2, LANES), lambda i: (0, 0))],
        out_specs=pl.BlockSpec((tm, 768), lambda i: (i, 0)),
        out_shape=jax.ShapeDtypeStruct((S, 768), F32),
        compiler_params=_params(("parallel",)),
    )(pos_col, _rope_inv())


def _tab(tab_ref, which):
    o = 384 * which
    return tab_ref[:, o:o + 128], tab_ref[:, o + 128:o + 256], tab_ref[:, o + 256:o + 384]


def _rms_fwd(x, g, name):
    S, Dm = x.shape
    tm = 512

    def body(x_ref, g_ref, h_ref):
        xv = x_ref[...]
        r = lax.rsqrt(jnp.mean(xv * xv, axis=-1, keepdims=True) + EPS)
        h_ref[...] = (xv * r * g_ref[...]).astype(BF16)

    return _pc(
        body, name=name, grid=(S // tm,),
        in_specs=[pl.BlockSpec((tm, Dm), lambda i: (i, 0)), pl.BlockSpec((1, Dm), lambda i: (0, 0))],
        out_specs=pl.BlockSpec((tm, Dm), lambda i: (i, 0)),
        out_shape=jax.ShapeDtypeStruct((S, Dm), BF16),
        compiler_params=_params(("parallel",)),
    )(x, g.reshape(1, Dm))


def _rms_bwd(x, g, dh, dres, name):
    S, Dm = x.shape
    tm = 512

    def body(x_ref, g_ref, dh_ref, dres_ref, dx_ref, dxb_ref, dg_ref):
        xv, dhv = x_ref[...], dh_ref[...]
        r = lax.rsqrt(jnp.mean(xv * xv, axis=-1, keepdims=True) + EPS)
        t = dhv * g_ref[...]
        dx = dres_ref[...] + r * t - xv * (r * r * r) * jnp.mean(xv * t, axis=-1, keepdims=True)
        dx_ref[...] = dx
        dxb_ref[...] = dx.astype(BF16)

        @pl.when(pl.program_id(0) == 0)
        def _():
            dg_ref[...] = jnp.zeros_like(dg_ref)

        dg_ref[...] += jnp.sum(dhv * xv * r, axis=0, keepdims=True)

    row = pl.BlockSpec((tm, Dm), lambda i: (i, 0))
    vec = pl.BlockSpec((1, Dm), lambda i: (0, 0))
    return _pc(
        body, name=name, grid=(S // tm,),
        in_specs=[row, vec, row, row], out_specs=[row, row, vec],
        out_shape=[jax.ShapeDtypeStruct((S, Dm), F32), jax.ShapeDtypeStruct((S, Dm), BF16),
                   jax.ShapeDtypeStruct((1, Dm), F32)],
        compiler_params=_params(("arbitrary",)),
    )(x, g.reshape(1, Dm), dh, dres)


def _hn_fwd(x, gain, G):
    r = lax.rsqrt(_gmean(x * x, G) + EPS)
    return x * r * gain


def _hn_bwd(x, gain, dy, G):
    r = lax.rsqrt(_gmean(x * x, G) + EPS)
    t = dy * gain
    dx = r * t - x * (r * r * r) * _gmean(x * t, G)
    return dx, jnp.sum(dy * x * r, axis=0, keepdims=True)


def _fold_halves(v):
    return v + _roll(v, HEAD)


def _even_pre_fwd(proj, tab, qg, kg):
    S = proj.shape[0]
    tm = 256

    def body(p_ref, tab_ref, qg_ref, kg_ref, g_ref, rq_ref, rk_ref, rv_ref, dq_ref, dk_ref, dv_ref):
        Ar, Br, Cr = _tab(tab_ref, 0)
        Ap, Bp, Cp = _tab(tab_ref, 1)
        G = g_ref[...]
        for c in range(2):
            sl = slice(c * 128, (c + 1) * 128)
            rq_ref[:, sl] = _rope(p_ref[:, c * 128:(c + 1) * 128], Ar, Br, Cr, 32).astype(BF16)
            rk_ref[:, sl] = (_rope(p_ref[:, 256 + c * 128:256 + (c + 1) * 128], Ar, Br, Cr, 32) * 0.125).astype(BF16)
        rv_ref[...] = p_ref[:, 512:1024].astype(BF16)
        for c in range(4):
            sl = slice(c * 128, (c + 1) * 128)
            q = _hn_fwd(p_ref[:, 1536 + c * 128:1536 + (c + 1) * 128], qg_ref[...], G)
            dq_ref[:, sl] = _rope(q, Ap, Bp, Cp, 8).astype(BF16)
            k = _hn_fwd(p_ref[:, 2048 + c * 128:2048 + (c + 1) * 128], kg_ref[...], G)
            dk_ref[:, sl] = _rope(k, Ap, Bp, Cp, 8).astype(BF16)
        dv_ref[...] = p_ref[:, 2560:3072].astype(BF16)

    row = lambda w: pl.BlockSpec((tm, w), lambda i: (i, 0))
    vec = pl.BlockSpec((1, LANES), lambda i: (0, 0))
    return _pc(
        body, name="even_pre_fwd", grid=(S // tm,),
        in_specs=[row(3072), row(768), vec, vec, pl.BlockSpec((LANES, LANES), lambda i: (0, 0))],
        out_specs=[row(256), row(256), row(512), row(512), row(512), row(512)],
        out_shape=[jax.ShapeDtypeStruct((S, w), BF16) for w in (256, 256, 512, 512, 512, 512)],
        compiler_params=_params(("parallel",)),
    )(proj, tab, qg, kg, _group_matrix())


def _even_pre_bwd(proj, tab, qg, kg, drq, drk, drv, drg, dqs, dks, dvs):
    S = proj.shape[0]
    tm = 256
    npat = len(dqs)

    def body(p_ref, tab_ref, qg_ref, kg_ref, g_ref, drq_ref, drk_ref, drv_ref, drg_ref, *rest):
        dq_refs, dk_refs, dv_refs = rest[:npat], rest[npat:2 * npat], rest[2 * npat:3 * npat]
        dp_ref, dqg_ref, dkg_ref = rest[3 * npat:]
        Ar, Br, Cr = _tab(tab_ref, 0)
        Ap, Bp, Cp = _tab(tab_ref, 1)
        G = g_ref[...]
        for c in range(2):
            sl = slice(c * 128, (c + 1) * 128)
            dp_ref[:, c * 128:(c + 1) * 128] = _rope_t(drq_ref[:, sl], Ar, Br, Cr, 32).astype(BF16)
            dp_ref[:, 256 + c * 128:256 + (c + 1) * 128] = _rope_t(drk_ref[:, sl] * 0.125, Ar, Br, Cr, 32).astype(BF16)
        dp_ref[:, 512:1024] = drv_ref[...].astype(BF16)
        dp_ref[:, 1024:1536] = drg_ref[...].astype(BF16)
        accq = jnp.zeros((1, LANES), F32)
        acck = jnp.zeros((1, LANES), F32)
        for c in range(4):
            sl = slice(c * 128, (c + 1) * 128)
            g = dq_refs[0][:, sl]
            for r in dq_refs[1:]:
                g = g + r[:, sl]
            dx, dg = _hn_bwd(p_ref[:, 1536 + c * 128:1536 + (c + 1) * 128], qg_ref[...], _rope_t(g, Ap, Bp, Cp, 8), G)
            dp_ref[:, 1536 + c * 128:1536 + (c + 1) * 128] = dx.astype(BF16)
            accq = accq + dg
            g = dk_refs[0][:, sl]
            for r in dk_refs[1:]:
                g = g + r[:, sl]
            dx, dg = _hn_bwd(p_ref[:, 2048 + c * 128:2048 + (c + 1) * 128], kg_ref[...], _rope_t(g, Ap, Bp, Cp, 8), G)
            dp_ref[:, 2048 + c * 128:2048 + (c + 1) * 128] = dx.astype(BF16)
            acck = acck + dg
        g = dv_refs[0][...]
        for r in dv_refs[1:]:
            g = g + r[...]
        dp_ref[:, 2560:3072] = g.astype(BF16)

        @pl.when(pl.program_id(0) == 0)
        def _():
            dqg_ref[...] = jnp.zeros_like(dqg_ref)
            dkg_ref[...] = jnp.zeros_like(dkg_ref)

        dqg_ref[...] += _fold_halves(accq)
        dkg_ref[...] += _fold_halves(acck)

    row = lambda w: pl.BlockSpec((tm, w), lambda i: (i, 0))
    vec = pl.BlockSpec((1, LANES), lambda i: (0, 0))
    return _pc(
        body, name="even_pre_bwd", grid=(S // tm,),
        in_specs=[row(3072), row(768), vec, vec, pl.BlockSpec((LANES, LANES), lambda i: (0, 0)),
                  row(256), row(256), row(512), row(512)] + [row(512)] * (3 * npat),
        out_specs=[row(3072), vec, vec],
        out_shape=[jax.ShapeDtypeStruct((S, 3072), BF16), jax.ShapeDtypeStruct((1, LANES), F32),
                   jax.ShapeDtypeStruct((1, LANES), F32)],
        compiler_params=_params(("arbitrary",)),
    )(proj, tab, qg, kg, _group_matrix(), drq, drk, drv, drg, *dqs, *dks, *dvs)


def _ret_consts(pair, half):
    lg = jnp.where(pair == 0, _LOG_GAMMA[half], _LOG_GAMMA[2 + half]).astype(F32)
    i = lax.broadcasted_iota(jnp.int32, (BLK, BLK), 0)
    j = lax.broadcasted_iota(jnp.int32, (BLK, BLK), 1)
    diff = (i - j).astype(F32)
    decay = jnp.where(diff >= 0, jnp.exp(lg * jnp.maximum(diff, 0.0)), 0.0)
    t = lax.broadcasted_iota(jnp.int32, (BLK, 1), 0).astype(F32)
    xi = jnp.exp(lg * (t + 1.0))
    zeta = jnp.exp(lg * (BLK - 1.0 - t))
    cd = jnp.exp(jnp.full((1, 1), BLK, F32) * lg)
    return decay, xi, zeta, cd


def _ret_fwd(rq, rk, rv):
    S = rq.shape[0]
    nc = S // BLK

    def body(q_ref, k_ref, v_ref, o_ref, st_ref, R):
        p, n = pl.program_id(0), pl.program_id(1)

        @pl.when(n == 0)
        def _():
            R[...] = jnp.zeros_like(R)

        q2, k2 = q_ref[...], k_ref[...]
        for half in range(2):
            decay, xi, zeta, cd = _ret_consts(p, half)
            m = _head_mask((BLK, LANES), half)
            qm = jnp.where(m, q2, jnp.zeros_like(q2))
            km = jnp.where(m, k2, jnp.zeros_like(k2))
            v = v_ref[:, half * 128:(half + 1) * 128]
            Rb = R[half].astype(BF16)
            st_ref[half] = Rb
            sc = lax.dot_general(qm, k2, (((1,), (1,)), ((), ())), preferred_element_type=F32) * decay
            o = jnp.dot(sc.astype(BF16), v, preferred_element_type=F32)
            o = o + jnp.dot(qm, Rb, preferred_element_type=F32) * xi
            o_ref[:, half * 128:(half + 1) * 128] = o
            kz = (km.astype(F32) * zeta).astype(BF16)
            R[half] = R[half] * cd + lax.dot_general(kz, v, (((0,), (0,)), ((), ())), preferred_element_type=F32)

    return _pc(
        body, name="ret_fwd", grid=(2, nc),
        in_specs=[pl.BlockSpec((BLK, 128), lambda p, n: (n, p)), pl.BlockSpec((BLK, 128), lambda p, n: (n, p)),
                  pl.BlockSpec((BLK, 256), lambda p, n: (n, p))],
        out_specs=[pl.BlockSpec((BLK, 256), lambda p, n: (n, p)),
                   pl.BlockSpec((None, None, 2, 128, 128), lambda p, n: (p, n, 0, 0, 0))],
        out_shape=[jax.ShapeDtypeStruct((S, 512), F32), jax.ShapeDtypeStruct((2, nc, 2, 128, 128), BF16)],
        scratch_shapes=[pltpu.VMEM((2, 128, 128), F32)],
        compiler_params=_params(("parallel", "arbitrary")),
    )(rq, rk, rv)


def _ret_bwd(rq, rk, rv, states, do):
    S = rq.shape[0]
    nc = S // BLK

    def body(q_ref, k_ref, v_ref, st_ref, do_ref, dq_ref, dk_ref, dv_ref, U):
        p, n = pl.program_id(0), pl.program_id(1)

        @pl.when(n == 0)
        def _():
            U[...] = jnp.zeros_like(U)

        q2, k2 = q_ref[...], k_ref[...]
        dq_acc = jnp.zeros((BLK, LANES), F32)
        dk_acc = jnp.zeros((BLK, LANES), F32)
        for half in range(2):
            decay, xi, zeta, cd = _ret_consts(p, half)
            m = _head_mask((BLK, LANES), half)
            qm = jnp.where(m, q2, jnp.zeros_like(q2))
            km = jnp.where(m, k2, jnp.zeros_like(k2))
            v = v_ref[:, half * 128:(half + 1) * 128]
            dob = do_ref[:, half * 128:(half + 1) * 128].astype(BF16)
            Rb = st_ref[half]
            Ub = U[half].astype(BF16)
            nt = (((1,), (1,)), ((), ()))
            tn = (((0,), (0,)), ((), ()))
            dsc = (lax.dot_general(dob, v, nt, preferred_element_type=F32) * decay).astype(BF16)
            xdo = (dob.astype(F32) * xi).astype(BF16)
            dq_acc += jnp.dot(dsc, km, preferred_element_type=F32) + lax.dot_general(xdo, Rb, nt, preferred_element_type=F32)
            dk_acc += lax.dot_general(dsc, qm, tn, preferred_element_type=F32) \
                + lax.dot_general(v, Ub, nt, preferred_element_type=F32) * zeta
            sc = (lax.dot_general(qm, k2, nt, preferred_element_type=F32) * decay).astype(BF16)
            kz = (km.astype(F32) * zeta).astype(BF16)
            dv_ref[:, half * 128:(half + 1) * 128] = lax.dot_general(sc, dob, tn, preferred_element_type=F32) \
                + jnp.dot(kz, Ub, preferred_element_type=F32)
            U[half] = U[half] * cd + lax.dot_general(qm, xdo, tn, preferred_element_type=F32)
        dq_ref[...] = dq_acc
        dk_ref[...] = dk_acc

    rev = lambda w: pl.BlockSpec((BLK, w), lambda p, n: (nc - 1 - n, p))
    return _pc(
        body, name="ret_bwd", grid=(2, nc),
        in_specs=[rev(128), rev(128), rev(256),
                  pl.BlockSpec((None, None, 2, 128, 128), lambda p, n: (p, nc - 1 - n, 0, 0, 0)), rev(256)],
        out_specs=[rev(128), rev(128), rev(256)],
        out_shape=[jax.ShapeDtypeStruct((S, 256), F32), jax.ShapeDtypeStruct((S, 256), F32),
                   jax.ShapeDtypeStruct((S, 512), F32)],
        scratch_shapes=[pltpu.VMEM((2, 128, 128), F32)],
        compiler_params=_params(("parallel", "arbitrary")),
    )(rq, rk, rv, states, do)


def _col_of(b, m):
    return jnp.max(jnp.where(m, b, -jnp.inf), axis=1, keepdims=True)


def _attn_fwd(q, k, v, *, nq, max_dist, name, sinks=None, want_bf16=False):
    L, Ck = k.shape
    nb, ncol = L // BLK, Ck // LANES
    scale = HEAD ** -0.5
    has_sink = sinks is not None

    def body(*refs):
        q_ref, kp_ref, kc_ref, vp_ref, vc_ref = refs[:5]
        sk_ref = refs[5] if has_sink else None
        outs = refs[5 + has_sink:]
        n = pl.program_id(1)
        kcat = jnp.concatenate([kp_ref[...], kc_ref[...]], axis=0)
        vcat = jnp.concatenate([vp_ref[...], vc_ref[...]], axis=0)
        r = lax.broadcasted_iota(jnp.int32, (BLK, 2 * BLK), 0)
        c = lax.broadcasted_iota(jnp.int32, (BLK, 2 * BLK), 1)
        dist = r + BLK - c
        valid = (dist >= 0) & (dist <= max_dist) & ((c >= BLK) | (n > 0))
        for i in range(nq):
            q2 = q_ref[:, i * 128:(i + 1) * 128]
            o2 = jnp.zeros((BLK, LANES), F32)
            l2 = jnp.zeros((BLK, LANES), F32)
            for half in range(2):
                m = _head_mask((BLK, LANES), half)
                qm = jnp.where(m, q2, jnp.zeros_like(q2))
                s = lax.dot_general(qm, kcat, (((1,), (1,)), ((), ())), preferred_element_type=F32) * scale
                s = jnp.where(valid, s, -jnp.inf)
                mx = jnp.max(s, axis=1, keepdims=True)
                if has_sink:
                    snk = _col_of(sk_ref[:, i * 128:(i + 1) * 128], _head_mask((1, LANES), half))
                    mx = jnp.maximum(mx, snk)
                pr = jnp.exp(s - mx)
                den = jnp.sum(pr, axis=1, keepdims=True)
                if has_sink:
                    den = den + jnp.exp(snk - mx)
                pv = jnp.dot(pr.astype(BF16), vcat, preferred_element_type=F32) / den
                o2 = jnp.where(m, pv, o2)
                l2 = jnp.where(m, mx + jnp.log(den), l2)
            outs[0][:, i * 128:(i + 1) * 128] = o2
            outs[1][:, i * 128:(i + 1) * 128] = l2
            if want_bf16:
                outs[2][:, i * 128:(i + 1) * 128] = o2.astype(BF16)

    qspec = pl.BlockSpec((BLK, nq * 128), lambda j, n: (n, j))
    cur = pl.BlockSpec((BLK, 128), lambda j, n: (n, j))
    prev = pl.BlockSpec((BLK, 128), lambda j, n: (jnp.maximum(n - 1, 0), j))
    in_specs = [qspec, prev, cur, prev, cur]
    args = [q, k, k, v, v]
    if has_sink:
        in_specs.append(pl.BlockSpec((1, nq * 128), lambda j, n: (0, j)))
        args.append(sinks)
    out_dts = [F32, F32] + ([BF16] if want_bf16 else [])
    return _pc(
        body, name=name, grid=(ncol, nb), in_specs=in_specs,
        out_specs=[qspec] * len(out_dts),
        out_shape=[jax.ShapeDtypeStruct(q.shape, dt) for dt in out_dts],
        compiler_params=_params(("parallel", "parallel")),
    )(*args)


def _attn_bwd(q, k, v, o, lse, do, *, nq, max_dist, name, sinks=None):
    L, Ck = k.shape
    nb, ncol = L // BLK, Ck // LANES
    scale = HEAD ** -0.5
    has_sink = sinks is not None
    nt = (((1,), (1,)), ((), ()))
    tn = (((0,), (0,)), ((), ()))

    def body(*refs):
        (qc_ref, qn_ref, kp_ref, kc_ref, vp_ref, vc_ref, oc_ref, on_ref, lc_ref, ln_ref, dc_ref, dn_ref) = refs[:12]
        sk_ref = refs[12] if has_sink else None
        outs = refs[12 + has_sink:]
        dq_ref, dk_ref, dv_ref = outs[:3]
        n = pl.program_id(1)
        kc, vc = kc_ref[...], vc_ref[...]
        kcat = jnp.concatenate([kp_ref[...], kc], axis=0)
        vcat = jnp.concatenate([vp_ref[...], vc], axis=0)
        r = lax.broadcasted_iota(jnp.int32, (BLK, 2 * BLK), 0)
        c = lax.broadcasted_iota(jnp.int32, (BLK, 2 * BLK), 1)
        dist = r + BLK - c
        valid_q = (dist >= 0) & (dist <= max_dist) & ((c >= BLK) | (n > 0))
        r2 = lax.broadcasted_iota(jnp.int32, (2 * BLK, BLK), 0)
        c2 = lax.broadcasted_iota(jnp.int32, (2 * BLK, BLK), 1)
        dist2 = r2 - c2
        valid_k = (dist2 >= 0) & (dist2 <= max_dist) & ((r2 < BLK) | (n < nb - 1))
        dk_acc = jnp.zeros((BLK, LANES), F32)
        dv_acc = jnp.zeros((BLK, LANES), F32)
        for i in range(nq):
            sl = slice(i * 128, (i + 1) * 128)
            qcur, docur = qc_ref[:, sl], dc_ref[:, sl]
            qcat = jnp.concatenate([qcur, qn_ref[:, sl]], axis=0)
            docat = jnp.concatenate([docur, dn_ref[:, sl]], axis=0)
            ocat = jnp.concatenate([oc_ref[:, sl], on_ref[:, sl]], axis=0)
            lcat = jnp.concatenate([lc_ref[:, sl], ln_ref[:, sl]], axis=0)
            dq2 = jnp.zeros((BLK, LANES), F32)
            ds2 = jnp.zeros((1, LANES), F32)
            for half in range(2):
                m1 = _head_mask((BLK, LANES), half)
                m2 = _head_mask((2 * BLK, LANES), half)
                dom = jnp.where(m2, docat, 0.0)
                delta = jnp.sum(dom * ocat, axis=1, keepdims=True)
                lcol = _col_of(lcat, m2)
                domb = dom.astype(BF16)
                qmcat = jnp.where(m2, qcat, jnp.zeros_like(qcat))
                qm = qmcat[:BLK]
                s = lax.dot_general(qm, kcat, nt, preferred_element_type=F32) * scale
                pr = jnp.where(valid_q, jnp.exp(s - lcol[:BLK]), 0.0)
                dp = lax.dot_general(domb[:BLK], vcat, nt, preferred_element_type=F32)
                ds = (pr * (dp - delta[:BLK])).astype(BF16)
                dq2 = jnp.where(m1, jnp.dot(ds, kcat, preferred_element_type=F32) * scale, dq2)
                if has_sink:
                    snk = _col_of(sk_ref[:, sl], _head_mask((1, LANES), half))
                    contrib = jnp.sum(-jnp.exp(snk - lcol[:BLK]) * delta[:BLK], axis=0, keepdims=True)
                    ds2 = jnp.where(_head_mask((1, LANES), half), contrib, ds2)
                s = lax.dot_general(qmcat, kc, nt, preferred_element_type=F32) * scale
                pr = jnp.where(valid_k, jnp.exp(s - lcol), 0.0)
                dv_acc += lax.dot_general(pr.astype(BF16), domb, tn, preferred_element_type=F32)
                dp = lax.dot_general(domb, vc, nt, preferred_element_type=F32)
                ds = (pr * (dp - delta)).astype(BF16)
                dk_acc += lax.dot_general(ds, qmcat, tn, preferred_element_type=F32) * scale
            dq_ref[:, sl] = dq2
            if has_sink:
                @pl.when(n == 0)
                def _():
                    outs[3][:, sl] = jnp.zeros((1, LANES), F32)

                outs[3][:, sl] += ds2
        dk_ref[...] = dk_acc
        dv_ref[...] = dv_acc

    qcur = pl.BlockSpec((BLK, nq * 128), lambda j, n: (n, j))
    qnext = pl.BlockSpec((BLK, nq * 128), lambda j, n: (jnp.minimum(n + 1, nb - 1), j))
    cur = pl.BlockSpec((BLK, 128), lambda j, n: (n, j))
    prev = pl.BlockSpec((BLK, 128), lambda j, n: (jnp.maximum(n - 1, 0), j))
    in_specs = [qcur, qnext, prev, cur, prev, cur, qcur, qnext, qcur, qnext, qcur, qnext]
    args = [q, q, k, k, v, v, o, o, lse, lse, do, do]
    out_specs = [qcur, cur, cur]
    out_shape = [jax.ShapeDtypeStruct(q.shape, F32), jax.ShapeDtypeStruct(k.shape, F32), jax.ShapeDtypeStruct(k.shape, F32)]
    if has_sink:
        vec = pl.BlockSpec((1, nq * 128), lambda j, n: (0, j))
        in_specs.append(vec)
        args.append(sinks)
        out_specs.append(vec)
        out_shape.append(jax.ShapeDtypeStruct((1, q.shape[1]), F32))
    return _pc(
        body, name=name, grid=(ncol, nb), in_specs=in_specs, out_specs=out_specs, out_shape=out_shape,
        compiler_params=_params(("parallel", "arbitrary")),
    )(*args)


ATT_TILE = 2048


def _rows(ref, start, n, r):
    if r == 1:
        return ref[pl.ds(start, n), :]
    return ref[pl.ds(start, n, stride=r), :]


def _set_rows(ref, start, n, r, val):
    if r == 1:
        ref[pl.ds(start, n), :] = val
    else:
        ref[pl.ds(start, n, stride=r), :] = val


def _band_geometry(S, patterns):
    rmax = max(r for _, r in patterns)
    H = BLK * rmax
    T = min(S, ATT_TILE)
    assert T % H == 0 and S % T == 0
    return H, T, S // T, T // BLK


def _band_fwd(q, k, v, *, patterns, nq, name, sinks=None, want_bf16=False):
    S, Ck = k.shape
    H, T, nt, nbt = _band_geometry(S, patterns)
    ncol = Ck // LANES
    scale = HEAD ** -0.5
    has_sink = sinks is not None
    nt_dims = (((1,), (1,)), ((), ()))

    def body(*refs):
        q_ref, kp_ref, kc_ref, vp_ref, vc_ref = refs[:5]
        sk_ref = refs[5] if has_sink else None
        n_out = 3 if want_bf16 else 2
        outs = refs[5 + has_sink:5 + has_sink + n_out]
        qf, kf, vf, M, L, A = refs[5 + has_sink + n_out:]
        t = pl.program_id(1)
        kf[0:H, :] = kp_ref[...].astype(F32)
        kf[H:H + T, :] = kc_ref[...].astype(F32)
        vf[0:H, :] = vp_ref[...].astype(F32)
        vf[H:H + T, :] = vc_ref[...].astype(F32)
        r_i = lax.broadcasted_iota(jnp.int32, (BLK, 2 * BLK), 0)
        c_i = lax.broadcasted_iota(jnp.int32, (BLK, 2 * BLK), 1)
        dist_i = r_i + BLK - c_i
        masks = [_head_mask((BLK, LANES), h) for h in range(2)]

        for i in range(nq):
            qf[...] = q_ref[:, i * 128:(i + 1) * 128].astype(F32) * scale
            for p, (dist, r) in enumerate(patterns):
                in_band = (dist_i >= 0) & (dist_i <= dist)
                in_band_first = in_band & ((c_i >= BLK) | (t > 0))

                def unit(j, b, p=p, r=r, in_band=in_band, in_band_first=in_band_first):
                    q0 = j + b * (BLK * r)
                    q2 = _rows(qf, q0, BLK, r).astype(BF16)
                    kcat = _rows(kf, H + q0 - BLK * r, 2 * BLK, r).astype(BF16)
                    vcat = _rows(vf, H + q0 - BLK * r, 2 * BLK, r).astype(BF16)
                    valid = in_band if b > 0 else in_band_first
                    m2 = jnp.zeros((BLK, LANES), F32)
                    l2 = jnp.zeros((BLK, LANES), F32)
                    a2 = jnp.zeros((BLK, LANES), F32)
                    for half in range(2):
                        m = masks[half]
                        qm = jnp.where(m, q2, jnp.zeros_like(q2))
                        s = lax.dot_general(qm, kcat, nt_dims, preferred_element_type=F32)
                        s = jnp.where(valid, s, -jnp.inf)
                        mx = jnp.max(s, axis=1, keepdims=True)
                        pr = jnp.exp(s - mx)
                        den = jnp.sum(pr, axis=1, keepdims=True)
                        pv = jnp.dot(pr.astype(BF16), vcat, preferred_element_type=F32)
                        m2 = jnp.where(m, mx, m2)
                        l2 = jnp.where(m, den, l2)
                        a2 = jnp.where(m, pv, a2)
                    if p > 0:
                        mo = _rows(M, q0, BLK, r)
                        mn = jnp.maximum(mo, m2)
                        wa, wb = jnp.exp(mo - mn), jnp.exp(m2 - mn)
                        l2 = wa * _rows(L, q0, BLK, r) + wb * l2
                        a2 = wa * _rows(A, q0, BLK, r) + wb * a2
                        m2 = mn
                    _set_rows(M, q0, BLK, r, m2)
                    _set_rows(L, q0, BLK, r, l2)
                    _set_rows(A, q0, BLK, r, a2)

                for u in range(nbt):
                    unit(u % r, u // r)
            sl = slice(i * 128, (i + 1) * 128)
            mm, ll, aa = M[...], L[...], A[...]
            if has_sink:
                snk = sk_ref[:, sl]
                mn = jnp.maximum(mm, snk)
                w = jnp.exp(mm - mn)
                ll = ll * w + jnp.exp(snk - mn)
                aa = aa * w
                mm = mn
            o = aa / ll
            outs[0][:, sl] = o
            outs[1][:, sl] = mm + jnp.log(ll)
            if want_bf16:
                outs[2][:, sl] = o.astype(BF16)

    th = T // H
    qspec = pl.BlockSpec((T, nq * 128), lambda j, t: (t, j))
    cur = pl.BlockSpec((T, 128), lambda j, t: (t, j))
    prev = pl.BlockSpec((H, 128), lambda j, t: (jnp.maximum(t * th - 1, 0), j))
    in_specs = [qspec, prev, cur, prev, cur]
    args = [q, k, k, v, v]
    if has_sink:
        in_specs.append(pl.BlockSpec((1, nq * 128), lambda j, t: (0, j)))
        args.append(sinks)
    out_dts = [F32, F32] + ([BF16] if want_bf16 else [])
    return _pc(
        body, name=name, grid=(ncol, nt), in_specs=in_specs,
        out_specs=[qspec] * len(out_dts),
        out_shape=[jax.ShapeDtypeStruct(q.shape, dt) for dt in out_dts],
        scratch_shapes=[pltpu.VMEM((T, LANES), F32), pltpu.VMEM((H + T, LANES), F32), pltpu.VMEM((H + T, LANES), F32),
                        pltpu.VMEM((T, LANES), F32), pltpu.VMEM((T, LANES), F32), pltpu.VMEM((T, LANES), F32)],
        compiler_params=_params(("parallel", "parallel")),
    )(*args)


def _band_bwd(q, k, v, lse, delta, do, *, patterns, nq, name, sinks=None, do_col0=0):
    S, Ck = k.shape
    H, T, nt, nbt = _band_geometry(S, patterns)
    ncol = Ck // LANES
    scale = HEAD ** -0.5
    has_sink = sinks is not None
    nt_dims = (((1,), (1,)), ((), ()))
    tn_dims = (((0,), (0,)), ((), ()))

    def body(*refs):
        (qc_ref, qn_ref, kp_ref, kc_ref, vp_ref, vc_ref, lc_ref, ln_ref, ec_ref, en_ref, dc_ref, dn_ref) = refs[:12]
        sk_ref = refs[12] if has_sink else None
        n_out = 4 if has_sink else 3
        outs = refs[12 + has_sink:12 + has_sink + n_out]
        dq_ref, dk_ref, dv_ref = outs[:3]
        qf, kf, vf, lf, ef, df = refs[12 + has_sink + n_out:]
        t = pl.program_id(1)
        kf[0:H, :] = kp_ref[...].astype(F32)
        kf[H:H + T, :] = kc_ref[...].astype(F32)
        vf[0:H, :] = vp_ref[...].astype(F32)
        vf[H:H + T, :] = vc_ref[...].astype(F32)
        dk_ref[...] = jnp.zeros_like(dk_ref)
        dv_ref[...] = jnp.zeros_like(dv_ref)
        r_i = lax.broadcasted_iota(jnp.int32, (BLK, 2 * BLK), 0)
        c_i = lax.broadcasted_iota(jnp.int32, (BLK, 2 * BLK), 1)
        dist_q = r_i + BLK - c_i
        dist_h = dist_q[:, :BLK]
        m1 = [_head_mask((BLK, LANES), h) for h in range(2)]

        def head_inputs(half, q2, do2, l2, e2):
            m = m1[half]
            lh = jnp.where(m, l2, _roll(l2, HEAD))
            eh = jnp.where(m, e2, _roll(e2, HEAD))
            return jnp.where(m, q2, jnp.zeros_like(q2)), jnp.where(m, do2, 0.0).astype(BF16), lh, eh

        for i in range(nq):
            sl = slice(i * 128, (i + 1) * 128)
            qf[0:T, :] = qc_ref[:, sl].astype(F32) * scale
            qf[T:T + H, :] = qn_ref[:, sl].astype(F32) * scale
            for buf, c_ref, n_ref in ((lf, lc_ref, ln_ref), (ef, ec_ref, en_ref), (df, dc_ref, dn_ref)):
                buf[0:T, :] = c_ref[:, sl]
                buf[T:T + H, :] = n_ref[:, sl]
            if has_sink:
                @pl.when(t == 0)
                def _():
                    outs[3][:, sl] = jnp.zeros((1, LANES), F32)

                outs[3][:, sl] += jnp.sum(-jnp.exp(sk_ref[:, sl] - lc_ref[:, sl]) * ec_ref[:, sl], axis=0, keepdims=True)
            for p, (dist, r) in enumerate(patterns):
                band_q = (dist_q >= 0) & (dist_q <= dist)
                band_first = band_q & ((c_i >= BLK) | (t > 0))
                band_h = (dist_h >= 0) & (dist_h <= dist)

                def add_rows(ref, start, val, r=r):
                    _set_rows(ref, start, BLK, r, _rows(ref, start, BLK, r) + val)

                def unit(j, b, p=p, r=r, band_q=band_q, band_first=band_first):
                    q0 = j + b * (BLK * r)
                    q2 = _rows(qf, q0, BLK, r).astype(BF16)
                    do2, l2, e2 = _rows(df, q0, BLK, r), _rows(lf, q0, BLK, r), _rows(ef, q0, BLK, r)
                    kcat = _rows(kf, H + q0 - BLK * r, 2 * BLK, r).astype(BF16)
                    vcat = _rows(vf, H + q0 - BLK * r, 2 * BLK, r).astype(BF16)
                    valid = band_q if b > 0 else band_first
                    dq2 = jnp.zeros((BLK, LANES), F32)
                    dkc = jnp.zeros((2 * BLK, LANES), F32)
                    dvc = jnp.zeros((2 * BLK, LANES), F32)
                    for half in range(2):
                        qm, dom, lh, eh = head_inputs(half, q2, do2, l2, e2)
                        s = lax.dot_general(qm, kcat, nt_dims, preferred_element_type=F32)
                        pr = jnp.where(valid, jnp.exp(s - jnp.concatenate([lh, lh], axis=1)), 0.0)
                        dp = lax.dot_general(dom, vcat, nt_dims, preferred_element_type=F32)
                        ds = (pr * (dp - jnp.concatenate([eh, eh], axis=1))).astype(BF16)
                        dq2 = jnp.where(m1[half], jnp.dot(ds, kcat, preferred_element_type=F32) * scale, dq2)
                        dvc += lax.dot_general(pr.astype(BF16), dom, tn_dims, preferred_element_type=F32)
                        dkc += lax.dot_general(ds, qm, tn_dims, preferred_element_type=F32)
                    if p > 0:
                        dq2 = dq2 + _rows(dq_ref.at[:, sl], q0, BLK, r)
                    _set_rows(dq_ref.at[:, sl], q0, BLK, r, dq2)
                    add_rows(dk_ref, q0, dkc[BLK:])
                    add_rows(dv_ref, q0, dvc[BLK:])
                    if b > 0:
                        add_rows(dk_ref, q0 - BLK * r, dkc[:BLK])
                        add_rows(dv_ref, q0 - BLK * r, dvc[:BLK])

                def halo_unit(j, r=r, band_h=band_h):
                    k0 = j + (nbt // r - 1) * (BLK * r)
                    q2 = _rows(qf, T + j, BLK, r).astype(BF16)
                    do2, l2, e2 = _rows(df, T + j, BLK, r), _rows(lf, T + j, BLK, r), _rows(ef, T + j, BLK, r)
                    kc = _rows(kf, H + k0, BLK, r).astype(BF16)
                    vc = _rows(vf, H + k0, BLK, r).astype(BF16)
                    dk2 = jnp.zeros((BLK, LANES), F32)
                    dv2 = jnp.zeros((BLK, LANES), F32)
                    for half in range(2):
                        qm, dom, lh, eh = head_inputs(half, q2, do2, l2, e2)
                        s = lax.dot_general(qm, kc, nt_dims, preferred_element_type=F32)
                        pr = jnp.where(band_h, jnp.exp(s - lh), 0.0)
                        dp = lax.dot_general(dom, vc, nt_dims, preferred_element_type=F32)
                        ds = (pr * (dp - eh)).astype(BF16)
                        dv2 += lax.dot_general(pr.astype(BF16), dom, tn_dims, preferred_element_type=F32)
                        dk2 += lax.dot_general(ds, qm, tn_dims, preferred_element_type=F32)
                    add_rows(dk_ref, k0, dk2)
                    add_rows(dv_ref, k0, dv2)

                for u in range(nbt):
                    unit(u % r, u // r)
                if nt > 1:
                    @pl.when(t < nt - 1)
                    def _(r=r, halo_unit=halo_unit):
                        for j in range(r):
                            halo_unit(j)

    th = T // H
    last = S // H - 1
    qcur = pl.BlockSpec((T, nq * 128), lambda j, t: (t, j))
    qnext = pl.BlockSpec((H, nq * 128), lambda j, t: (jnp.minimum((t + 1) * th, last), j))
    cur = pl.BlockSpec((T, 128), lambda j, t: (t, j))
    prev = pl.BlockSpec((H, 128), lambda j, t: (jnp.maximum(t * th - 1, 0), j))
    dcur = pl.BlockSpec((T, nq * 128), lambda j, t: (t, j + do_col0))
    dnext = pl.BlockSpec((H, nq * 128), lambda j, t: (jnp.minimum((t + 1) * th, last), j + do_col0))
    in_specs = [qcur, qnext, prev, cur, prev, cur, qcur, qnext, qcur, qnext, dcur, dnext]
    args = [q, q, k, k, v, v, lse, lse, delta, delta, do, do]
    out_specs = [qcur, cur, cur]
    out_shape = [jax.ShapeDtypeStruct(q.shape, F32), jax.ShapeDtypeStruct(k.shape, F32), jax.ShapeDtypeStruct(k.shape, F32)]
    if has_sink:
        vec = pl.BlockSpec((1, nq * 128), lambda j, t: (0, j))
        in_specs.append(vec)
        args.append(sinks)
        out_specs.append(vec)
        out_shape.append(jax.ShapeDtypeStruct((1, q.shape[1]), F32))
    big = pltpu.VMEM((T + H, LANES), F32)
    return _pc(
        body, name=name, grid=(ncol, nt), in_specs=in_specs, out_specs=out_specs, out_shape=out_shape,
        scratch_shapes=[big] * 6,
        compiler_params=_params(("parallel", "arbitrary")),
    )(*args)


def _delta(do, o, name):
    S, C = do.shape
    tm = 512

    def body(do_ref, o_ref, g_ref, e_ref):
        for c in range(C // LANES):
            sl = slice(c * 128, (c + 1) * 128)
            e_ref[:, sl] = _gmean(do_ref[:, sl] * o_ref[:, sl], g_ref[...]) * float(HEAD)

    row = pl.BlockSpec((tm, C), lambda i: (i, 0))
    return _pc(
        body, name=name, grid=(S // tm,),
        in_specs=[row, row, pl.BlockSpec((LANES, LANES), lambda i: (0, 0))], out_specs=row,
        out_shape=jax.ShapeDtypeStruct((S, C), F32),
        compiler_params=_params(("parallel",)),
    )(do, o, _group_matrix())


def _even_post_fwd(ro, proj, gn, da):
    S = ro.shape[0]
    tm = 256

    def body(ro_ref, rg_ref, gn_ref, da_ref, mix_ref):
        for c in range(4):
            sl = slice(c * 128, (c + 1) * 128)
            x = ro_ref[:, sl]
            mu = jnp.mean(x, axis=1, keepdims=True)
            xc = x - mu
            var = jnp.mean(xc * xc, axis=1, keepdims=True)
            y = xc * lax.rsqrt(var + EPS) * gn_ref[:, sl]
            z = rg_ref[:, sl]
            mix_ref[:, sl] = (z * jax.nn.sigmoid(z) * y).astype(BF16)
        mix_ref[:, 512:1024] = da_ref[...].astype(BF16)

    row = lambda w: pl.BlockSpec((tm, w), lambda i: (i, 0))
    return _pc(
        body, name="even_post_fwd", grid=(S // tm,),
        in_specs=[row(512), pl.BlockSpec((tm, 512), lambda i: (i, 2)), pl.BlockSpec((1, 512), lambda i: (0, 0)), row(512)],
        out_specs=row(1024), out_shape=jax.ShapeDtypeStruct((S, 1024), BF16),
        compiler_params=_params(("parallel",)),
    )(ro, proj, gn, da)


def _even_post_bwd(ro, proj, gn, dmixed):
    S = ro.shape[0]
    tm = 256

    def body(ro_ref, rg_ref, gn_ref, dm_ref, dro_ref, drg_ref, dgn_ref):
        @pl.when(pl.program_id(0) == 0)
        def _():
            dgn_ref[...] = jnp.zeros_like(dgn_ref)

        for c in range(4):
            sl = slice(c * 128, (c + 1) * 128)
            x = ro_ref[:, sl]
            mu = jnp.mean(x, axis=1, keepdims=True)
            xc = x - mu
            rstd = lax.rsqrt(jnp.mean(xc * xc, axis=1, keepdims=True) + EPS)
            xh = xc * rstd
            gain = gn_ref[:, sl]
            y = xh * gain
            z = rg_ref[:, sl]
            sg = jax.nn.sigmoid(z)
            dra = dm_ref[:, sl]
            drg_ref[:, sl] = dra * y * sg * (1.0 + z * (1.0 - sg))
            dy = dra * z * sg
            dgn_ref[:, sl] += jnp.sum(dy * xh, axis=0, keepdims=True)
            dxh = dy * gain
            dro_ref[:, sl] = rstd * (dxh - jnp.mean(dxh, axis=1, keepdims=True)
                                     - xh * jnp.mean(dxh * xh, axis=1, keepdims=True))

    row = lambda w: pl.BlockSpec((tm, w), lambda i: (i, 0))
    vec = pl.BlockSpec((1, 512), lambda i: (0, 0))
    return _pc(
        body, name="even_post_bwd", grid=(S // tm,),
        in_specs=[row(512), pl.BlockSpec((tm, 512), lambda i: (i, 2)), vec, row(512)],
        out_specs=[row(512), row(512), vec],
        out_shape=[jax.ShapeDtypeStruct((S, 512), F32), jax.ShapeDtypeStruct((S, 512), F32),
                   jax.ShapeDtypeStruct((1, 512), F32)],
        compiler_params=_params(("arbitrary",)),
    )(ro, proj, gn, dmixed)


def _swa_pre_fwd(proj, tab, qg, kg):
    S = proj.shape[0]
    tm = 256

    def body(p_ref, tab_ref, qg_ref, kg_ref, g_ref, q_ref, k_ref, v_ref):
        Ap, Bp, Cp = _tab(tab_ref, 1)
        G = g_ref[...]
        lo = _head_mask((tm, LANES), 0)
        for c in range(8):
            sl = slice(c * 128, (c + 1) * 128)
            q_ref[:, sl] = _rope(_hn_fwd(p_ref[:, sl], qg_ref[...], G), Ap, Bp, Cp, 8).astype(BF16)
        for c in range(2):
            kn = _rope(_hn_fwd(p_ref[:, 1024 + c * 128:1024 + (c + 1) * 128], kg_ref[...], G), Ap, Bp, Cp, 8)
            vv = p_ref[:, 1280 + c * 128:1280 + (c + 1) * 128]
            for t, ref in ((kn, k_ref), (vv, v_ref)):
                sw = _roll(t, HEAD)
                ref[:, (2 * c) * 128:(2 * c + 1) * 128] = jnp.where(lo, t, sw).astype(BF16)
                ref[:, (2 * c + 1) * 128:(2 * c + 2) * 128] = jnp.where(lo, sw, t).astype(BF16)

    row = lambda w: pl.BlockSpec((tm, w), lambda i: (i, 0))
    vec = pl.BlockSpec((1, LANES), lambda i: (0, 0))
    return _pc(
        body, name="swa_pre_fwd", grid=(S // tm,),
        in_specs=[row(1536), row(768), vec, vec, pl.BlockSpec((LANES, LANES), lambda i: (0, 0))],
        out_specs=[row(1024), row(512), row(512)],
        out_shape=[jax.ShapeDtypeStruct((S, w), BF16) for w in (1024, 512, 512)],
        compiler_params=_params(("parallel",)),
    )(proj, tab, qg, kg, _group_matrix())


def _swa_pre_bwd(proj, tab, qg, kg, dq, dk, dv):
    S = proj.shape[0]
    tm = 256

    def body(p_ref, tab_ref, qg_ref, kg_ref, g_ref, dq_ref, dk_ref, dv_ref, dp_ref, db_ref, dqg_ref, dkg_ref):
        Ap, Bp, Cp = _tab(tab_ref, 1)
        G = g_ref[...]
        lo = _head_mask((tm, LANES), 0)

        @pl.when(pl.program_id(0) == 0)
        def _():
            db_ref[...] = jnp.zeros_like(db_ref)
            dqg_ref[...] = jnp.zeros_like(dqg_ref)
            dkg_ref[...] = jnp.zeros_like(dkg_ref)

        accq = jnp.zeros((1, LANES), F32)
        acck = jnp.zeros((1, LANES), F32)
        for c in range(8):
            sl = slice(c * 128, (c + 1) * 128)
            dx, dg = _hn_bwd(p_ref[:, sl], qg_ref[...], _rope_t(dq_ref[:, sl], Ap, Bp, Cp, 8), G)
            dp_ref[:, sl] = dx.astype(BF16)
            db_ref[:, sl] += jnp.sum(dx, axis=0, keepdims=True)
            accq = accq + dg
        for c in range(2):
            folded = []
            for ref in (dk_ref, dv_ref):
                a = ref[:, (2 * c) * 128:(2 * c + 1) * 128]
                b = ref[:, (2 * c + 1) * 128:(2 * c + 2) * 128]
                folded.append(jnp.where(lo, a + _roll(a, HEAD), b + _roll(b, HEAD)))
            ks = slice(1024 + c * 128, 1024 + (c + 1) * 128)
            dx, dg = _hn_bwd(p_ref[:, ks], kg_ref[...], _rope_t(folded[0], Ap, Bp, Cp, 8), G)
            dp_ref[:, ks] = dx.astype(BF16)
            db_ref[:, ks] += jnp.sum(dx, axis=0, keepdims=True)
            acck = acck + dg
            vs = slice(1280 + c * 128, 1280 + (c + 1) * 128)
            dp_ref[:, vs] = folded[1].astype(BF16)
            db_ref[:, vs] += jnp.sum(folded[1], axis=0, keepdims=True)
        dqg_ref[...] += _fold_halves(accq)
        dkg_ref[...] += _fold_halves(acck)

    row = lambda w: pl.BlockSpec((tm, w), lambda i: (i, 0))
    vec = pl.BlockSpec((1, LANES), lambda i: (0, 0))
    return _pc(
        body, name="swa_pre_bwd", grid=(S // tm,),
        in_specs=[row(1536), row(768), vec, vec, pl.BlockSpec((LANES, LANES), lambda i: (0, 0)),
                  row(1024), row(512), row(512)],
        out_specs=[row(1536), pl.BlockSpec((1, 1536), lambda i: (0, 0)), vec, vec],
        out_shape=[jax.ShapeDtypeStruct((S, 1536), BF16), jax.ShapeDtypeStruct((1, 1536), F32),
                   jax.ShapeDtypeStruct((1, LANES), F32), jax.ShapeDtypeStruct((1, LANES), F32)],
        compiler_params=_params(("arbitrary",)),
    )(proj, tab, qg, kg, _group_matrix(), dq, dk, dv)


def _loss_head(y, target):
    S, Dm = y.shape
    tm = 512

    def body(y_ref, t_ref, l_ref, dy_ref, dyb_ref):
        @pl.when(pl.program_id(0) == 0)
        def _():
            l_ref[...] = jnp.zeros_like(l_ref)

        e = y_ref[...] - t_ref[...]
        dy = e * (1.0 / Dm)
        dy_ref[...] = dy
        dyb_ref[...] = dy.astype(BF16)
        row = jnp.sum(e * e, axis=1, keepdims=True) * (0.5 / Dm)
        l_ref[...] += jnp.sum(row, axis=0, keepdims=True)

    row = pl.BlockSpec((tm, Dm), lambda i: (i, 0))
    return _pc(
        body, name="loss_head", grid=(S // tm,), in_specs=[row, row],
        out_specs=[pl.BlockSpec((1, LANES), lambda i: (0, 0)), row, row],
        out_shape=[jax.ShapeDtypeStruct((1, LANES), F32), jax.ShapeDtypeStruct((S, Dm), F32),
                   jax.ShapeDtypeStruct((S, Dm), BF16)],
        compiler_params=_params(("arbitrary",)),
    )(y, target)


def _relu2_of(u):
    r = jnp.maximum(u.astype(F32), 0.0)
    return r * r


def _drelu2(acc, u):
    return (acc * 2.0 * jnp.maximum(u.astype(F32), 0.0),)


def _add(acc, res):
    return (acc + res,)


_T = dict(tm=1024, tn=1024, tk=1024)


def _rms_bwd_in(x, g, dres):
    def epilogue(dh, xv, gv, dr):
        r = lax.rsqrt(jnp.mean(xv * xv, axis=-1, keepdims=True) + EPS)
        t = dh * gv
        dx = dr + r * t - xv * (r * r * r) * jnp.mean(xv * t, axis=-1, keepdims=True)
        return dx, dx, jnp.sum(dh * xv * r, axis=0, keepdims=True)

    return dict(outs=[F32, BF16, ("colsum",)], epilogue=epilogue,
                extras=[(x, "mn"), (g.reshape(1, D_MODEL), "n"), (dres, "mn")])


def _delta_in(o, col0):
    width = D_MODEL - col0

    def epilogue(do, ov, G):
        parts = [_gmean(do[:, col0 + c * 128:col0 + (c + 1) * 128] * ov[:, c * 128:(c + 1) * 128], G) * float(HEAD)
                 for c in range(width // LANES)]
        return do, jnp.concatenate(parts, axis=1)

    return dict(outs=[F32, (F32, width)], epilogue=epilogue, extras=[(o, width), (_group_matrix(), "full")])


def _mlp_fwd(x, g, w_up, w_dn, tag):
    h = _rms_fwd(x, g, f"rms_mlp_fwd{tag}")
    u = _matmul(h, w_up, dims="nn", **_T, outs=[BF16], b_cs=True, name=f"mlp_up{tag}")
    x_out = _matmul(u, w_dn, dims="nn", **_T, outs=[F32], epilogue=_add, extras=[(x, "mn")], a_pro=_relu2_of,
                    name=f"mlp_down{tag}")
    return x_out, (h, u)


def _mlp_bwd(x, g, w_up, w_dn, saved, dy, dyb, tag):
    h, u = saved
    du = _matmul(dyb, w_dn, dims="nt", **_T, outs=[BF16], epilogue=_drelu2, extras=[(u, "mn")], name=f"mlp_du{tag}")
    dw_dn = _matmul(u, dyb, dims="tn", **_T, outs=[F32], a_pro=_relu2_of, name=f"mlp_dwdown{tag}")
    dw_up = _matmul(h, du, dims="tn", **_T, outs=[F32], o_cs=N_CHIPS, name=f"mlp_dwup{tag}")
    dx, dxb, dg = _matmul(du, w_up, dims="nt", tm=512, tn=1024, tk=1024, b_cs=True, name=f"mlp_dh{tag}", **_rms_bwd_in(x, g, dy))
    return dx, dxb, dg, dw_up, dw_dn


def _pattern_view(t, r):
    S, C = t.shape
    return t.reshape(S // r, r * C)


def _local_step(x, pos_col, target, W, rest_of, P, red):
    S = x.shape[0]
    tab = _tables(pos_col)
    tile2 = lambda g: jnp.tile(g.reshape(1, HEAD), (1, 2))
    dqg, dkg = tile2(P["dil_q_gain"]), tile2(P["dil_k_gain"])
    sqg, skg = tile2(P["swa_q_gain"]), tile2(P["swa_k_gain"])
    gn = P["ret_gn_gain"].reshape(1, 512)
    sink_b = jnp.repeat(P["swa_sinks"].reshape(16), HEAD).reshape(1, 1024)

    h0 = _rms_fwd(x, P["norm_mix"][0], "rms_mix_fwd0")
    proj = _matmul(h0, W["hyb_w_in"], dims="nn", tm=1024, tn=768, tk=1024, outs=[F32], b_cs=True, name="hyb_in")
    rq, rk, rv, dq, dk, dv = _even_pre_fwd(proj, tab, dqg, dkg)
    ro, states = _ret_fwd(rq, rk, rv)
    dil = [(w // r, r) for w, r in DIL_PATTERNS]
    da, dlse = _band_fwd(dq, dk, dv, patterns=dil, nq=1, name="dil_fwd")
    mixed = _even_post_fwd(ro, proj, gn, da)
    x1 = _matmul(mixed, W["hyb_w_out"], dims="nn", **_T, outs=[F32], epilogue=_add, extras=[(x, "mn")], name="hyb_out")
    rest, bias = rest_of(x1)
    W = {**W, **rest}
    x2, mlp0 = _mlp_fwd(x1, P["norm_mlp"][0], W["mlp_w_up"][0], W["mlp_w_down"][0], "0")

    h2 = _rms_fwd(x2, P["norm_mix"][1], "rms_mix_fwd1")
    proj2 = _matmul(h2, W["swa_w_qkv"], dims="nn", tm=1024, tn=384, tk=1024, outs=[F32], b_cs=True,
                    epilogue=_add, extras=[(bias.reshape(1, 1536), "n")], name="swa_qkv")
    sq, sk, sv = _swa_pre_fwd(proj2, tab, sqg, skg)
    swa = [(SWA_DIST, 1)]
    so, slse, so_b = _band_fwd(sq, sk, sv, patterns=swa, nq=2, name="swa_fwd", sinks=sink_b, want_bf16=True)
    x3 = _matmul(so_b, W["swa_w_out"], dims="nn", **_T, outs=[F32], epilogue=_add, extras=[(x2, "mn")], name="swa_out")
    y, mlp1 = _mlp_fwd(x3, P["norm_mlp"][1], W["mlp_w_up"][1], W["mlp_w_down"][1], "1")
    loss, dy, dyb = _loss_head(y, target)

    gw, gp = {}, {}
    dx3, dx3b, dg_mlp1, gw["mlp_w_up1"], gw["mlp_w_down1"] = _mlp_bwd(x3, P["norm_mlp"][1], W["mlp_w_up"][1],
                                                                       W["mlp_w_down"][1], mlp1, dy, dyb, "1")
    dx3b = red.begin("mlp1", {n: (gw[n], 1024) for n in ("mlp_w_up1", "mlp_w_down1")}, dx3b)
    gw["swa_w_out"] = _matmul(so_b, dx3b, dims="tn", **_T, outs=[F32], name="swa_dwout")
    dso, sdelta = _matmul(dx3b, W["swa_w_out"], dims="nt", tm=512, tn=1024, tk=1024, name="swa_do", **_delta_in(so, 0))
    dsq, dsk, dsv, dsink = _band_bwd(sq, sk, sv, slse, sdelta, dso, patterns=swa, nq=2, name="swa_bwd", sinks=sink_b)
    dproj2, gp["swa_b_qkv"], gp["swa_q_gain"], gp["swa_k_gain"] = _swa_pre_bwd(proj2, tab, sqg, skg, dsq, dsk, dsv)
    gp["swa_sinks"] = dsink
    gw["swa_w_qkv"] = _matmul(h2, dproj2, dims="tn", tm=1024, tn=384, tk=1024, outs=[F32], o_cs=N_CHIPS, name="swa_dwqkv")
    dx2, dx2b, dg_mix1 = _matmul(dproj2, W["swa_w_qkv"], dims="nt", tm=512, tn=1024, tk=384, b_cs=True, name="swa_dh",
                                 **_rms_bwd_in(x2, P["norm_mix"][1], dx3))
    dx2b = red.begin("swa", {"swa_w_qkv": (gw["swa_w_qkv"], 1024), "swa_w_out": (gw["swa_w_out"], 256)}, dx2b)
    dx2b = red.advance("mlp1", dx2b, dx2b)

    dx1, dx1b, dg_mlp0, gw["mlp_w_up0"], gw["mlp_w_down0"] = _mlp_bwd(x1, P["norm_mlp"][0], W["mlp_w_up"][0],
                                                                       W["mlp_w_down"][0], mlp0, dx2, dx2b, "0")
    gw["hyb_w_out"] = _matmul(mixed, dx1b, dims="tn", **_T, outs=[F32], name="hyb_dwout")
    dx1b = red.begin("mlp0", {"mlp_w_up0": (gw["mlp_w_up0"], 1024), "mlp_w_down0": (gw["mlp_w_down0"], 1024),
                              "hyb_w_out": (gw["hyb_w_out"], 256)}, dx1b)
    dx1b = red.advance("swa", dx1b, dx1b)
    red.finish("mlp1", dx1b)
    dmixed, ddelta = _matmul(dx1b, W["hyb_w_out"], dims="nt", tm=512, tn=1024, tk=1024, name="hyb_dmixed", **_delta_in(da, 512))
    dro, drg, gp["ret_gn_gain"] = _even_post_bwd(ro, proj, gn, dmixed)
    drq, drk, drv = _ret_bwd(rq, rk, rv, states, dro)
    ddq, ddk, ddv = _band_bwd(dq, dk, dv, dlse, ddelta, dmixed, patterns=dil, nq=1, name="dil_bwd", do_col0=4)
    ddq = red.advance("mlp0", ddq, ddq)
    red.finish("swa", ddq)
    dproj, gp["dil_q_gain"], gp["dil_k_gain"] = _even_pre_bwd(proj, tab, dqg, dkg, drq, drk, drv, drg, [ddq], [ddk], [ddv])
    gw["hyb_w_in"] = _matmul(h0, dproj, dims="tn", tm=1024, tn=768, tk=1024, outs=[F32], o_cs=N_CHIPS, name="hyb_dwin")
    dproj = red.begin("win", {"hyb_w_in": (gw["hyb_w_in"], 1024)}, dproj)
    grad_x, _, dg_mix0 = _matmul(dproj, W["hyb_w_in"], dims="nt", tm=512, tn=1024, tk=768, b_cs=True, name="hyb_dh",
                                 **_rms_bwd_in(x, P["norm_mix"][0], dx1))
    red.finish("mlp0", grad_x)
    gp["norm_mix"] = jnp.concatenate([dg_mix0, dg_mix1], axis=0)
    gp["norm_mlp"] = jnp.concatenate([dg_mlp0, dg_mlp1], axis=0)
    return loss, grad_x, gp


HBM = pl.BlockSpec(memory_space=pltpu.HBM)


def _place():
    x, y, c = lax.axis_index("x"), lax.axis_index("y"), lax.axis_index("c")
    chips = [(1 - x, y), (x, 1 - y), (1 - x, 1 - y)]
    return x, y, c, chips


def _allgather_shards(buf):
    _, R, Wd = buf.shape
    Rh = R // 2

    def body(b_ref, out_ref, send_sems, recv_sems):
        x, y, c, chips = _place()
        sibling = (x, y, 1 - c)

        def copy(k, chip, core, to):
            block = b_ref.at[2 * chip[0] + chip[1], pl.ds(core * Rh, Rh), :]
            return pltpu.make_async_remote_copy(
                src_ref=block, dst_ref=block, send_sem=send_sems.at[k], recv_sem=recv_sems.at[k],
                device_id=to, device_id_type=MESH)

        first = [copy(k, (x, y), c, (*chip, c)) for k, chip in enumerate(chips)]
        for cp in first:
            cp.start()
        passed = [copy(3 + k, chip, c, sibling) for k, chip in enumerate(chips)]
        for k, chip in enumerate(chips):
            copy(k, chip, c, (x, y, c)).wait_recv()
            passed[k].start()
        for k, chip in enumerate(chips):
            copy(3 + k, chip, 1 - c, (x, y, c)).wait_recv()
        for cp in first + passed:
            cp.wait_send()

    return _pc(
        body, name="allgather_first", in_specs=[HBM], out_specs=HBM,
        out_shape=jax.ShapeDtypeStruct(buf.shape, buf.dtype), input_output_aliases={0: 0},
        scratch_shapes=[pltpu.SemaphoreType.DMA((6,)), pltpu.SemaphoreType.DMA((6,))],
    )(buf)


SEM = pl.BlockSpec(memory_space=pltpu.SEMAPHORE)
EFFECT = pltpu.SideEffectType.DATAFLOW_SIDE_EFFECTING


def _half_block(ref, chip, core):
    rh = ref.shape[1] // 2
    return ref.at[2 * chip[0] + chip[1], pl.ds(core * rh, rh), :]


def _gather_start(buf, ride):
    def body(b_ref, ride_ref, s0, s1, s2, r0, r1, r2, b_out, ride_out):
        x, y, c, chips = _place()
        for chip, s, r in zip(chips, (s0, s1, s2), (r0, r1, r2)):
            mine = _half_block(b_ref, (x, y), c)
            pltpu.make_async_remote_copy(src_ref=mine, dst_ref=mine, send_sem=s, recv_sem=r,
                                         device_id=(*chip, c), device_id_type=MESH).start()

    sem = pltpu.SemaphoreType.DMA(())
    return _pc(
        body, name="allgather_rest_start",
        out_shape=(sem,) * 6 + (pltpu.HBM(buf.shape, buf.dtype), pltpu.HBM(ride.shape, ride.dtype)),
        in_specs=(HBM, HBM), out_specs=(SEM,) * 6 + (HBM, HBM), input_output_aliases={0: 6, 1: 7},
        compiler_params=pltpu.CompilerParams(has_side_effects=EFFECT),
    )(pltpu.with_memory_space_constraint(buf, pltpu.HBM), pltpu.with_memory_space_constraint(ride, pltpu.HBM))


def _gather_wait(buf, sems, after):
    def body(b_ref, s0, s1, s2, r0, r1, r2, after_ref, b_out):
        x, y, c, chips = _place()
        for chip, s, r in zip(chips, (s0, s1, s2), (r0, r1, r2)):
            cp = pltpu.make_async_remote_copy(src_ref=_half_block(b_ref, (x, y), c), dst_ref=_half_block(b_ref, chip, c),
                                              send_sem=s, recv_sem=r, device_id=(*chip, c), device_id_type=MESH)
            cp.wait_send()
            cp.wait_recv()

    return _pc(
        body, name="allgather_rest_wait", out_shape=pltpu.HBM(buf.shape, buf.dtype),
        in_specs=(HBM,) + (SEM,) * 6 + (pl.BlockSpec(memory_space=pl.ANY),), out_specs=HBM, input_output_aliases={0: 0},
        compiler_params=pltpu.CompilerParams(has_side_effects=EFFECT),
    )(buf, *sems, after)


def _gather_handover(buf):
    def body(b_ref, out_ref, send_sems, recv_sems):
        x, y, c, chips = _place()
        cps = []
        for k, chip in enumerate(chips):
            mine = _half_block(b_ref, chip, c)
            cps.append(pltpu.make_async_remote_copy(src_ref=mine, dst_ref=mine, send_sem=send_sems.at[k],
                                                    recv_sem=recv_sems.at[k], device_id=(x, y, 1 - c), device_id_type=MESH))
        for cp in cps:
            cp.start()
        for k, chip in enumerate(chips):
            theirs = _half_block(b_ref, chip, 1 - c)
            pltpu.make_async_remote_copy(src_ref=theirs, dst_ref=theirs, send_sem=send_sems.at[k], recv_sem=recv_sems.at[k],
                                         device_id=(x, y, 1 - c), device_id_type=MESH).wait_recv()
        for cp in cps:
            cp.wait_send()

    return _pc(
        body, name="allgather_rest_handover", in_specs=[HBM], out_specs=HBM,
        out_shape=jax.ShapeDtypeStruct(buf.shape, buf.dtype), input_output_aliases={0: 0},
        scratch_shapes=[pltpu.SemaphoreType.DMA((3,)), pltpu.SemaphoreType.DMA((3,))],
    )(buf)


def _swap_halves(ts):
    nt = len(ts)

    def body(*refs):
        t_refs, l_refs, send_sems, recv_sems = refs[:nt], refs[nt:2 * nt], refs[-2], refs[-1]
        x, y, c, _ = _place()
        cps = []
        for k in range(nt):
            rh = t_refs[k].shape[1] // 2
            cps.append(pltpu.make_async_remote_copy(
                src_ref=t_refs[k].at[:, pl.ds((1 - c) * rh, rh), :], dst_ref=l_refs[k],
                send_sem=send_sems.at[k], recv_sem=recv_sems.at[k], device_id=(x, y, 1 - c), device_id_type=MESH))
        for cp in cps:
            cp.start()
        for cp in cps:
            cp.wait()

    return _pc(
        body, name="grad_swap_halves", in_specs=[HBM] * nt, out_specs=[HBM] * nt,
        out_shape=[jax.ShapeDtypeStruct((t.shape[0], t.shape[1] // 2, t.shape[2]), F32) for t in ts],
        scratch_shapes=[pltpu.SemaphoreType.DMA((nt,)), pltpu.SemaphoreType.DMA((nt,))],
    )(*ts)


def _pair_sum(t, l, place, name):
    _, r, cols = t.shape
    rh = r // 2
    tr = min(rh, 256)
    nr = rh // tr

    def body(pl_ref, t_ref, l_ref, o_ref):
        o_ref[...] = (t_ref[...] + l_ref[...]).astype(BF16)

    other = lambda s, p: s + jnp.where(s >= p[0], 1, 0)
    return _pc(
        body, name=name,
        grid_spec=pltpu.PrefetchScalarGridSpec(
            num_scalar_prefetch=1, grid=(N_CHIPS - 1, nr),
            in_specs=[pl.BlockSpec((None, tr, cols), lambda s, i, p: (other(s, p), p[1] * nr + i, 0)),
                      pl.BlockSpec((None, tr, cols), lambda s, i, p: (other(s, p), i, 0))],
            out_specs=pl.BlockSpec((None, tr, cols), lambda s, i, p: (other(s, p), i, 0))),
        out_shape=jax.ShapeDtypeStruct((N_CHIPS, rh, cols), BF16),
        compiler_params=_params(("parallel", "parallel")),
    )(place, t, l)


def _exchange_chips(ps):
    nt = len(ps)

    def body(*refs):
        p_refs, r_refs, send_sems, recv_sems = refs[:nt], refs[nt:2 * nt], refs[-2], refs[-1]
        x, y, c, chips = _place()
        cps = []
        for t in range(nt):
            for k, chip in enumerate(chips):
                cps.append(pltpu.make_async_remote_copy(
                    src_ref=p_refs[t].at[2 * chip[0] + chip[1]], dst_ref=r_refs[t].at[k],
                    send_sem=send_sems.at[3 * t + k], recv_sem=recv_sems.at[3 * t + k],
                    device_id=(*chip, c), device_id_type=MESH))
        for cp in cps:
            cp.start()
        for cp in cps:
            cp.wait()

    return _pc(
        body, name="grad_exchange_chips", in_specs=[HBM] * nt, out_specs=[HBM] * nt,
        out_shape=[jax.ShapeDtypeStruct((3,) + p.shape[1:], BF16) for p in ps],
        scratch_shapes=[pltpu.SemaphoreType.DMA((3 * nt,)), pltpu.SemaphoreType.DMA((3 * nt,))],
    )(*ps)


def _final_sum(t, l, rcv, place, name):
    _, r, cols = t.shape
    rh = r // 2
    tr = min(rh, 256)
    nr = rh // tr

    def body(pl_ref, t_ref, l_ref, r_ref, o_ref):
        acc = t_ref[...] + l_ref[...]
        for k in range(3):
            acc = acc + r_ref[k].astype(F32)
        o_ref[...] = acc

    return _pc(
        body, name=name,
        grid_spec=pltpu.PrefetchScalarGridSpec(
            num_scalar_prefetch=1, grid=(nr,),
            in_specs=[pl.BlockSpec((None, tr, cols), lambda i, p: (p[0], p[1] * nr + i, 0)),
                      pl.BlockSpec((None, tr, cols), lambda i, p: (p[0], i, 0)),
                      pl.BlockSpec((3, tr, cols), lambda i, p: (0, i, 0))],
            out_specs=pl.BlockSpec((tr, cols), lambda i, p: (p[1] * nr + i, 0))),
        out_shape=jax.ShapeDtypeStruct((r, cols), F32),
        compiler_params=_params(("parallel",)),
    )(place, t, l, rcv)


def _share_halves(hs, name):
    nt = len(hs)

    def body(*refs):
        h_refs, send_sems, recv_sems = refs[:nt], refs[-2], refs[-1]
        x, y, c, _ = _place()
        cps = []
        for k in range(nt):
            rh = h_refs[k].shape[0] // 2
            half = h_refs[k].at[pl.ds(c * rh, rh), :]
            cps.append(pltpu.make_async_remote_copy(
                src_ref=half, dst_ref=half, send_sem=send_sems.at[k], recv_sem=recv_sems.at[k],
                device_id=(x, y, 1 - c), device_id_type=MESH))
        for cp in cps:
            cp.start()
        for cp in cps:
            cp.wait()

    return _pc(
        body, name=name, in_specs=[HBM] * nt, out_specs=[HBM] * nt,
        out_shape=[jax.ShapeDtypeStruct(h.shape, F32) for h in hs],
        input_output_aliases={k: k for k in range(nt)},
        scratch_shapes=[pltpu.SemaphoreType.DMA((nt,)), pltpu.SemaphoreType.DMA((nt,))],
    )(*hs)


def _split_start(name, bufs, ride, n, copies_of):
    nb = len(bufs)

    def body(*refs):
        sems = refs[nb + 1:nb + 1 + 2 * n]
        for cp in copies_of(refs[:nb], sems[:n], sems[n:]):
            cp.start()

    outs = _pc(
        body, name=name,
        out_shape=(pltpu.SemaphoreType.DMA(()),) * (2 * n) + tuple(pltpu.HBM(b.shape, b.dtype) for b in bufs)
        + (pltpu.HBM(ride.shape, ride.dtype),),
        in_specs=(HBM,) * (nb + 1), out_specs=(SEM,) * (2 * n) + (HBM,) * (nb + 1),
        input_output_aliases={k: 2 * n + k for k in range(nb + 1)},
        compiler_params=pltpu.CompilerParams(has_side_effects=EFFECT),
    )(*[pltpu.with_memory_space_constraint(b, pltpu.HBM) for b in bufs], pltpu.with_memory_space_constraint(ride, pltpu.HBM))
    return list(outs[:2 * n]), list(outs[2 * n:2 * n + nb]), outs[-1]


def _split_wait(name, bufs, sems, after, n, copies_of):
    nb = len(bufs)

    def body(*refs):
        s = refs[nb:nb + 2 * n]
        for cp in copies_of(refs[:nb], s[:n], s[n:]):
            cp.wait_send()
            cp.wait_recv()

    outs = _pc(
        body, name=name, out_shape=tuple(pltpu.HBM(b.shape, b.dtype) for b in bufs),
        in_specs=(HBM,) * nb + (SEM,) * (2 * n) + (pl.BlockSpec(memory_space=pl.ANY),), out_specs=(HBM,) * nb,
        input_output_aliases={k: k for k in range(nb)},
        compiler_params=pltpu.CompilerParams(has_side_effects=EFFECT),
    )(*bufs, *sems, after)
    return list(outs)


def _swap_copies(nt):
    def copies_of(refs, send, recv):
        x, y, c, _ = _place()
        cps = []
        for k in range(nt):
            rh = refs[k].shape[1] // 2
            cps.append(pltpu.make_async_remote_copy(
                src_ref=refs[k].at[:, pl.ds((1 - c) * rh, rh), :], dst_ref=refs[nt + k],
                send_sem=send[k], recv_sem=recv[k], device_id=(x, y, 1 - c), device_id_type=MESH))
        return cps
    return copies_of


def _exchange_copies(nt):
    def copies_of(refs, send, recv):
        x, y, c, chips = _place()
        cps = []
        for t in range(nt):
            for k, chip in enumerate(chips):
                cps.append(pltpu.make_async_remote_copy(
                    src_ref=refs[t].at[2 * chip[0] + chip[1]], dst_ref=refs[nt + t].at[k],
                    send_sem=send[3 * t + k], recv_sem=recv[3 * t + k], device_id=(*chip, c), device_id_type=MESH))
        return cps
    return copies_of


class _StagedReduce:
    def __init__(self, place):
        self.place = place
        self.groups = {}
        self.halves = {}

    @staticmethod
    def slab(t, r):
        return t.reshape(N_CHIPS, r, t.size // (N_CHIPS * r))

    def begin(self, g, grads, ride):
        names = list(grads)
        ts = [self.slab(t, r) for t, r in grads.values()]
        lands = [lax.empty((N_CHIPS, t.shape[1] // 2, t.shape[2]), F32) for t in ts]
        sems, bufs, ride = _split_start(f"grad_swap_start_{g}", ts + lands, ride, len(ts), _swap_copies(len(ts)))
        self.groups[g] = dict(names=names, bufs=bufs, sems=sems)
        return ride

    def advance(self, g, after, ride):
        st = self.groups[g]
        nt = len(st["names"])
        bufs = _split_wait(f"grad_swap_wait_{g}", st["bufs"], st["sems"], after, nt, _swap_copies(nt))
        st["ts"], st["ls"] = bufs[:nt], bufs[nt:]
        ps = [_pair_sum(t, l, self.place, f"pair_sum_{n}") for t, l, n in zip(st["ts"], st["ls"], st["names"])]
        lands = [lax.empty((3,) + p.shape[1:], BF16) for p in ps]
        st["sems"], st["bufs"], ride = _split_start(f"grad_exchange_start_{g}", ps + lands, ride, 3 * nt, _exchange_copies(nt))
        return ride

    def finish(self, g, after):
        st = self.groups[g]
        nt = len(st["names"])
        bufs = _split_wait(f"grad_exchange_wait_{g}", st["bufs"], st["sems"], after, 3 * nt, _exchange_copies(nt))
        for t, l, r, n in zip(st["ts"], st["ls"], bufs[nt:], st["names"]):
            self.halves[n] = _final_sum(t, l, r, self.place, f"final_sum_{n}")


def _allgather_small(v):
    rows = v.shape[0]

    def body(v_ref, out_ref, send_sems, recv_sems):
        x, y, c, _ = _place()
        me = 4 * x + 2 * y + c
        out_ref[me] = v_ref[...]
        cps = []
        for k in range(1, 8):
            fx, fy, fc = (k >> 2) & 1, (k >> 1) & 1, k & 1
            to = (1 - x if fx else x, 1 - y if fy else y, 1 - c if fc else c)
            cps.append(pltpu.make_async_remote_copy(
                src_ref=v_ref, dst_ref=out_ref.at[me], send_sem=send_sems.at[k - 1], recv_sem=recv_sems.at[k - 1],
                device_id=to, device_id_type=MESH))
        for cp in cps:
            cp.start()
        for cp in cps:
            cp.wait()

    return _pc(
        body, name="allgather_small",
        in_specs=[pl.BlockSpec(memory_space=pltpu.VMEM)], out_specs=pl.BlockSpec(memory_space=pltpu.VMEM),
        out_shape=jax.ShapeDtypeStruct((8, rows, LANES), F32),
        scratch_shapes=[pltpu.SemaphoreType.DMA((7,)), pltpu.SemaphoreType.DMA((7,))],
    )(v)


def _adamw_math(w, g, m, v):
    m = ADAM_B1 * m + (1.0 - ADAM_B1) * g
    v = ADAM_B2 * v + (1.0 - ADAM_B2) * (g * g)
    m_hat = m / (1.0 - ADAM_B1 ** ADAM_STEP)
    v_hat = v / (1.0 - ADAM_B2 ** ADAM_STEP)
    return -ADAM_LR * (m_hat / (jnp.sqrt(v_hat) + ADAM_EPS) + ADAM_WD * w), m, v


def _adamw(w, g, m, v, name):
    r, cols = w.shape
    tr = min(r, 256)

    def body(w_ref, g_ref, m_ref, v_ref, d_ref, mo_ref, vo_ref):
        d, mn, vn = _adamw_math(w_ref[...], g_ref[...], m_ref[...], v_ref[...])
        d_ref[...] = d
        mo_ref[...] = mn
        vo_ref[...] = vn

    row = pl.BlockSpec((tr, cols), lambda i: (i, 0))
    return _pc(
        body, name=name, grid=(r // tr,), in_specs=[row] * 4, out_specs=[row] * 3,
        out_shape=[jax.ShapeDtypeStruct((r, cols), F32)] * 3,
        compiler_params=_params(("parallel",)),
    )(w, g, m, v)


def _adamw_small(w, gathered, m, v):
    rows = w.shape[0]

    def body(w_ref, g_ref, m_ref, v_ref, go_ref, d_ref, mo_ref, vo_ref):
        g = g_ref[0]
        for k in range(1, 8):
            g = g + g_ref[k]
        d, mn, vn = _adamw_math(w_ref[...], g, m_ref[...], v_ref[...])
        go_ref[...] = g
        d_ref[...] = d
        mo_ref[...] = mn
        vo_ref[...] = vn

    return _pc(
        body, name="adamw_small",
        out_shape=[jax.ShapeDtypeStruct((rows, LANES), F32)] * 4,
    )(w, gathered, m, v)


_BIAS_ROWS = 32


def _own_slot(flat, chip):
    return lax.dynamic_update_slice(lax.empty((N_CHIPS,) + flat.shape, flat.dtype), flat[None], (chip, 0, 0))


def _pack_first(hyb_w_in, hyb_w_out):
    return jnp.concatenate([t.astype(BF16).reshape(-1, 1024) for t in (hyb_w_in, hyb_w_out)], axis=0)


def _unpack_first(g):
    return {"hyb_w_in": g[:, 0:768, :].reshape(N_CHIPS, 1024, 768), "hyb_w_out": g[:, 768:1024, :].reshape(1024, 1024)}


def _pack_rest(mlp_w_up, mlp_w_down, swa_w_qkv, swa_w_out, swa_b_qkv):
    parts = [t.astype(BF16).reshape(-1, 1024) for t in (mlp_w_up, mlp_w_down, swa_w_qkv, swa_w_out)]
    bias = lax.bitcast_convert_type(swa_b_qkv.reshape(384), BF16).reshape(1, 768)
    bias = jnp.pad(bias, ((0, _BIAS_ROWS - 1), (0, 256)))
    return jnp.concatenate(parts + [bias], axis=0)


def _unpack_rest(g):
    W = {
        "mlp_w_up": [g[:, l * 1024:(l + 1) * 1024, :] for l in range(2)],
        "mlp_w_down": [g[:, 2048 + l * 1024:2048 + (l + 1) * 1024, :].reshape(D_FF, D_MODEL) for l in range(2)],
        "swa_w_qkv": g[:, 4096:4480, :].reshape(N_CHIPS, 1024, 384),
        "swa_w_out": g[:, 4480:4736, :].reshape(1024, 1024),
    }
    bias = lax.bitcast_convert_type(g[:, 4736, :768].reshape(N_CHIPS, 384, 2), F32).reshape(1536)
    return W, bias


_SMALL = (("norm_mix", 16), ("norm_mlp", 16), ("ret_gn_gain", 4), ("dil_q_gain", 1), ("dil_k_gain", 1),
          ("swa_b_qkv", 12), ("swa_q_gain", 1), ("swa_k_gain", 1), ("swa_sinks", 1))
_SUBLANES = 8


def _slot(r):
    return -(-r // _SUBLANES) * _SUBLANES


def _pack_small(d):
    return jnp.concatenate([jnp.pad(d[n].reshape(r, LANES), ((0, _slot(r) - r), (0, 0))) for n, r in _SMALL], axis=0)


def _unpack_small(p):
    out, o = {}, 0
    for n, r in _SMALL:
        out[n] = p[o:o + r]
        o += _slot(r)
    return out


def kernel(x, positions, norm_mix, norm_mlp, mlp_w_up, mlp_w_down, hyb_w_in, hyb_w_out, ret_gn_gain, dil_q_gain, dil_k_gain, swa_w_qkv, swa_b_qkv, swa_w_out, swa_q_gain, swa_k_gain, swa_sinks, loss_target, m_norm_mix, m_norm_mlp, m_mlp_w_up, m_mlp_w_down, m_hyb_w_in, m_hyb_w_out, m_ret_gn_gain, m_dil_q_gain, m_dil_k_gain, m_swa_w_qkv, m_swa_b_qkv, m_swa_w_out, m_swa_q_gain, m_swa_k_gain, m_swa_sinks, v_norm_mix, v_norm_mlp, v_mlp_w_up, v_mlp_w_down, v_hyb_w_in, v_hyb_w_out, v_ret_gn_gain, v_dil_q_gain, v_dil_k_gain, v_swa_w_qkv, v_swa_b_qkv, v_swa_w_out, v_swa_q_gain, v_swa_k_gain, v_swa_sinks):
    ax, ay, ac = lax.axis_index("x"), lax.axis_index("y"), lax.axis_index("c")
    chip = 2 * ax + ay
    place = jnp.stack([chip, ac]).astype(jnp.int32)
    S = x.shape[1]

    first = _allgather_shards(_own_slot(_pack_first(hyb_w_in[0], hyb_w_out[0]), chip))
    rest = _own_slot(_pack_rest(mlp_w_up, mlp_w_down, swa_w_qkv[0], swa_w_out[0], swa_b_qkv[0]), chip)
    *sems, rest, first = _gather_start(rest, first)

    def rest_of(after):
        return _unpack_rest(_gather_handover(_gather_wait(rest, sems, after)))

    P = dict(norm_mix=norm_mix, norm_mlp=norm_mlp, ret_gn_gain=ret_gn_gain, dil_q_gain=dil_q_gain, dil_k_gain=dil_k_gain,
             swa_q_gain=swa_q_gain, swa_k_gain=swa_k_gain, swa_sinks=swa_sinks)

    red = _StagedReduce(place)
    loss_l, grad_x, gp = _local_step(x[0], positions.reshape(S, 1), loss_target[0], _unpack_first(first), rest_of, P, red)
    loss = lax.psum(loss_l[0, 0], ("x", "y", "c"))

    names = ["mlp_w_up0", "mlp_w_up1", "mlp_w_down0", "mlp_w_down1", "hyb_w_out", "swa_w_qkv", "swa_w_out"]
    red.halves[names[0]] = red.advance("win", grad_x, red.halves[names[0]])
    gs = dict(zip(names, _share_halves([red.halves[n] for n in names], "grad_share_halves")))
    shards = dict(mlp_w_up0=(mlp_w_up[0], m_mlp_w_up[0], v_mlp_w_up[0]), mlp_w_up1=(mlp_w_up[1], m_mlp_w_up[1], v_mlp_w_up[1]),
                  mlp_w_down0=(mlp_w_down[0], m_mlp_w_down[0], v_mlp_w_down[0]),
                  mlp_w_down1=(mlp_w_down[1], m_mlp_w_down[1], v_mlp_w_down[1]),
                  hyb_w_in=(hyb_w_in[0], m_hyb_w_in[0], v_hyb_w_in[0]), hyb_w_out=(hyb_w_out[0], m_hyb_w_out[0], v_hyb_w_out[0]),
                  swa_w_qkv=(swa_w_qkv[0], m_swa_w_qkv[0], v_swa_w_qkv[0]), swa_w_out=(swa_w_out[0], m_swa_w_out[0], v_swa_w_out[0]))
    big = {}
    for n in names:
        w, m, v = shards[n]
        big[n] = (gs[n],) + tuple(_adamw(w, gs[n], m, v, f"adamw_{n}"))
    red.finish("win", big[names[-1]][1])
    w, m, v = shards["hyb_w_in"]
    g_win = _share_halves([red.halves["hyb_w_in"]], "grad_share_last")[0]
    big["hyb_w_in"] = (g_win,) + tuple(_adamw(w, g_win, m, v, "adamw_hyb_w_in"))

    def big_out(n, k):
        if n in ("mlp_w_up", "mlp_w_down"):
            return jnp.stack([big[n + "0"][k], big[n + "1"][k]])
        return big[n][k][None]

    gsm = dict(gp)
    gsm["swa_sinks"] = jnp.pad(gp["swa_sinks"].reshape(16, HEAD)[:, 0], (0, LANES - 16))
    gathered = _allgather_small(_pack_small(gsm))

    def small_pack(norm_mix, norm_mlp, gn, dq, dk, b, sq, sk, sinks):
        dup = lambda t: jnp.tile(t.reshape(1, HEAD), (1, 2))
        bias = lax.dynamic_update_slice(jnp.zeros((12, LANES), F32), b.reshape(3, LANES), (3 * chip, 0))
        return _pack_small(dict(norm_mix=norm_mix, norm_mlp=norm_mlp, ret_gn_gain=gn, dil_q_gain=dup(dq), dil_k_gain=dup(dk),
                                swa_b_qkv=bias, swa_q_gain=dup(sq), swa_k_gain=dup(sk),
                                swa_sinks=jnp.pad(sinks.reshape(16), (0, LANES - 16))))

    pw = small_pack(norm_mix, norm_mlp, ret_gn_gain, dil_q_gain, dil_k_gain, swa_b_qkv, swa_q_gain, swa_k_gain, swa_sinks)
    pm = small_pack(m_norm_mix, m_norm_mlp, m_ret_gn_gain, m_dil_q_gain, m_dil_k_gain, m_swa_b_qkv, m_swa_q_gain, m_swa_k_gain, m_swa_sinks)
    pv = small_pack(v_norm_mix, v_norm_mlp, v_ret_gn_gain, v_dil_q_gain, v_dil_k_gain, v_swa_b_qkv, v_swa_q_gain, v_swa_k_gain, v_swa_sinks)
    small = [_unpack_small(t) for t in _adamw_small(pw, gathered, pm, pv)]

    def small_out(n, k):
        t = small[k][n]
        if n in ("norm_mix", "norm_mlp"):
            return t.reshape(2, D_MODEL)
        if n == "ret_gn_gain":
            return t.reshape(1, RET_HEADS, 128)
        if n == "swa_b_qkv":
            return lax.dynamic_slice(t, (3 * chip, 0), (3, LANES)).reshape(1, 384)
        if n == "swa_sinks":
            return t[0, :16].reshape(1, 16)
        return t[0, :HEAD].reshape(1, HEAD)

    order = ["norm_mix", "norm_mlp", "mlp_w_up", "mlp_w_down", "hyb_w_in", "hyb_w_out", "ret_gn_gain", "dil_q_gain",
             "dil_k_gain", "swa_w_qkv", "swa_b_qkv", "swa_w_out", "swa_q_gain", "swa_k_gain", "swa_sinks"]
    is_big = {"mlp_w_up", "mlp_w_down", "hyb_w_in", "hyb_w_out", "swa_w_qkv", "swa_w_out"}
    outs = [loss, grad_x[None]]
    for k in range(4):
        outs += [big_out(n, k) if n in is_big else small_out(n, k) for n in order]
    return tuple(outs)
```

```python
import functools
import math

import numpy as np
import jax
import jax.numpy as jnp
from jax import lax
from jax.experimental import pallas as pl
from jax.experimental.pallas import tpu as pltpu

F32, BF16 = jnp.float32, jnp.bfloat16
HIGHEST = lax.Precision.HIGHEST
MESH = pl.DeviceIdType.MESH

LANES = 128
VMEM_LIMIT = 48 << 20
D_MODEL = 1024
D_FF = 4096
HEAD = 64
EPS = 1e-6
BLK = 128
RET_HEADS = 4
RET_THETA = 10000.0
ROPE_THETA = 500000.0
ROPE_DIMS = 16
DIL_PATTERNS = ((128, 1), (512, 4), (2048, 16))
SWA_DIST = 127
N_CHIPS = 4
ADAM_LR, ADAM_B1, ADAM_B2, ADAM_EPS, ADAM_WD, ADAM_STEP = 0.001, 0.9, 0.999, 1e-08, 0.01, 10

_LOG_GAMMA = [float(np.log1p(-np.exp2(np.float32(-5.0 - h)))) for h in range(RET_HEADS)]


def _pc(body, **kw):
    return pl.pallas_call(body, **kw)


def _params(sem):
    return pltpu.CompilerParams(dimension_semantics=sem, vmem_limit_bytes=VMEM_LIMIT)


def _matmul(a, b, *, dims, tm, tn, tk, outs, name, epilogue=None, extras=(), b_cs=False, b_rs=0, b_row0=0, b_rows=0,
            o_cs=0, a_pro=None):
    if dims == "nn":
        M, K = a.shape
        N = b.shape[0] * b.shape[2] if b_cs else b.shape[1]
        a_spec = pl.BlockSpec((tm, tk), lambda i, j, k: (i, k))
        if b_cs:
            npt = b.shape[2] // tn
            b_spec = pl.BlockSpec((None, tk, tn), lambda i, j, k: (j // npt, k + b_row0, j % npt))
        elif b_rs:
            K, N, kps = b.shape[0] * b_rs, b.shape[2], b_rs // tk
            b_spec = pl.BlockSpec((None, tk, tn), lambda i, j, k: (k // kps, b_row0 + k % kps, j))
        else:
            b_spec = pl.BlockSpec((tk, tn), lambda i, j, k: (k, j))
        contract = (((1,), (0,)), ((), ()))
    elif dims == "nt":
        M, K = a.shape
        N = (b_rows or b.shape[1]) if b_cs else b.shape[0]
        a_spec = pl.BlockSpec((tm, tk), lambda i, j, k: (i, k))
        if b_cs:
            kpt = b.shape[2] // tk
            b_spec = pl.BlockSpec((None, tn, tk), lambda i, j, k: (k // kpt, j + b_row0, k % kpt))
        elif b_rs:
            N, jps = b.shape[0] * b_rs, b_rs // tn
            b_spec = pl.BlockSpec((None, tn, tk), lambda i, j, k: (j // jps, b_row0 + j % jps, k))
        else:
            b_spec = pl.BlockSpec((tn, tk), lambda i, j, k: (j, k))
        contract = (((1,), (1,)), ((), ()))
    else:
        K, M = a.shape
        N = b.shape[1]
        a_spec = pl.BlockSpec((tk, tm), lambda i, j, k: (k, i))
        b_spec = pl.BlockSpec((tk, tn), lambda i, j, k: (k, j))
        contract = (((0,), (0,)), ((), ()))
    assert M % tm == 0 and N % tn == 0 and K % tk == 0, (name, M, N, K, tm, tn, tk)
    nk = K // tk
    ex_specs = []
    for arr, kind in extras:
        if kind == "mn":
            ex_specs.append(pl.BlockSpec((tm, tn), lambda i, j, k: (i, j)))
        elif kind == "n":
            ex_specs.append(pl.BlockSpec((1, tn), lambda i, j, k: (0, j)))
        elif kind == "full":
            ex_specs.append(pl.BlockSpec(arr.shape, lambda i, j, k, nd=arr.ndim: (0,) * nd))
        else:
            ex_specs.append(pl.BlockSpec((tm, kind), lambda i, j, k: (i, 0)))
    if o_cs:
        n_sh = N // o_cs
        opt = n_sh // tn
        o_shape = (o_cs, M, n_sh)
        o_spec = pl.BlockSpec((None, tm, tn), lambda i, j, k: (j // opt, i, j % opt))
    else:
        o_shape = (M, N)
        o_spec = pl.BlockSpec((tm, tn), lambda i, j, k: (i, j))
    o_specs, o_shapes, summed = [], [], []
    for o in outs:
        if isinstance(o, tuple) and o[0] == "colsum":
            assert N == tn
            o_specs.append(pl.BlockSpec((1, tn), lambda i, j, k: (0, j)))
            o_shapes.append(jax.ShapeDtypeStruct((1, N), F32))
            summed.append(True)
        elif isinstance(o, tuple):
            o_specs.append(pl.BlockSpec((tm, o[1]), lambda i, j, k: (i, 0)))
            o_shapes.append(jax.ShapeDtypeStruct((M, o[1]), o[0]))
            summed.append(False)
        else:
            o_specs.append(o_spec)
            o_shapes.append(jax.ShapeDtypeStruct(o_shape, o))
            summed.append(False)
    n_ex, n_out = len(extras), len(outs)
    if epilogue is None:
        epilogue = lambda acc: (acc,)

    def body(a_ref, b_ref, *rest):
        ex, o_refs, acc = rest[:n_ex], rest[n_ex:n_ex + n_out], rest[-1]
        i, k = pl.program_id(0), pl.program_id(2)

        @pl.when(k == 0)
        def _():
            acc[...] = jnp.zeros_like(acc)

        av = a_ref[...] if a_pro is None else a_pro(a_ref[...])
        acc[...] += lax.dot_general(av.astype(BF16), b_ref[...].astype(BF16), contract, preferred_element_type=F32)

        @pl.when(k == nk - 1)
        def _():
            vals = epilogue(acc[...], *[e[...] for e in ex])
            for r, v, sm in zip(o_refs, vals, summed):
                if sm:
                    @pl.when(i == 0)
                    def _(r=r):
                        r[...] = jnp.zeros_like(r)

                    r[...] += v
                else:
                    r[...] = v.astype(r.dtype)

    res = _pc(
        body, name=name, grid=(M // tm, N // tn, nk),
        in_specs=[a_spec, b_spec] + ex_specs, out_specs=o_specs, out_shape=o_shapes,
        scratch_shapes=[pltpu.VMEM((tm, tn), F32)],
        compiler_params=_params(("arbitrary" if any(summed) else "parallel", "parallel", "arbitrary")),
    )(a, b, *[e for e, _ in extras])
    return res[0] if n_out == 1 else res


def _roll(x, s):
    return pltpu.roll(x, s % LANES, 1)


def _rope(x, A, B, C, half):
    return x * A + _roll(x, LANES - half) * B + _roll(x, half) * C


def _rope_t(g, A, B, C, half):
    return g * A + _roll(g * B, half) + _roll(g * C, LANES - half)


def _gmean(x, G):
    return jnp.dot(x, G, precision=HIGHEST, preferred_element_type=F32)


def _head_mask(shape, half):
    lane = lax.broadcasted_iota(jnp.int32, shape, len(shape) - 1)
    return (lane >= HEAD) if half else (lane < HEAD)


def _group_matrix():
    i = np.arange(LANES)
    return jnp.asarray((i[:, None] // HEAD == i[None, :] // HEAD).astype(np.float32) / HEAD)


def _rope_inv():
    l = np.arange(LANES) % HEAD
    inv_r = np.power(np.float32(RET_THETA), -(l % 32).astype(np.float32) * np.float32(2.0 / HEAD))
    hp = ROPE_DIMS // 2
    inv_p = np.power(np.float32(ROPE_THETA), -(l % hp).astype(np.float32) * np.float32(2.0 / ROPE_DIMS))
    inv_p = np.where(l < ROPE_DIMS, inv_p, 0.0)
    return jnp.asarray(np.stack([inv_r, inv_p]).astype(np.float32))


def _tables(pos_col):
    S = pos_col.shape[0]
    tm = 512
    hp = ROPE_DIMS // 2

    def body(p_ref, inv_ref, o_ref):
        p = p_ref[...].astype(F32)
        lane = lax.broadcasted_iota(jnp.int32, (tm, LANES), 1) % HEAD
        ang = p * inv_ref[0:1, :]
        c, s = jnp.cos(ang), jnp.sin(ang)
        o_ref[:, 0:128] = c
        o_ref[:, 128:256] = jnp.where(lane < 32, -s, 0.0)
        o_ref[:, 256:384] = jnp.where(lane >= 32, s, 0.0)
        ang = p * inv_ref[1:2, :]
        c, s = jnp.cos(ang), jnp.sin(ang)
        o_ref[:, 384:512] = c
        o_ref[:, 512:640] = jnp.where(lane < hp, -s, 0.0)
        o_ref[:, 640:768] = jnp.where((lane >= hp) & (lane < ROPE_DIMS), s, 0.0)

    return _pc(
        body, name="rope_tables", grid=(S // tm,),
        in_specs=[pl.BlockSpec((tm, 1), lambda i: (i, 0)), pl.BlockSpec((2, LANES), lambda i: (0, 0))],
        out_specs=pl.BlockSpec((tm, 768), lambda i: (i, 0)),
        out_shape=jax.ShapeDtypeStruct((S, 768), F32),
        compiler_params=_params(("parallel",)),
    )(pos_col, _rope_inv())


def _tab(tab_ref, which):
    o = 384 * which
    return tab_ref[:, o:o + 128], tab_ref[:, o + 128:o + 256], tab_ref[:, o + 256:o + 384]


def _rms_fwd(x, g, name):
    S, Dm = x.shape
    tm = 512

    def body(x_ref, g_ref, h_ref):
        xv = x_ref[...]
        r = lax.rsqrt(jnp.mean(xv * xv, axis=-1, keepdims=True) + EPS)
        h_ref[...] = (xv * r * g_ref[...]).astype(BF16)

    return _pc(
        body, name=name, grid=(S // tm,),
        in_specs=[pl.BlockSpec((tm, Dm), lambda i: (i, 0)), pl.BlockSpec((1, Dm), lambda i: (0, 0))],
        out_specs=pl.BlockSpec((tm, Dm), lambda i: (i, 0)),
        out_shape=jax.ShapeDtypeStruct((S, Dm), BF16),
        compiler_params=_params(("parallel",)),
    )(x, g.reshape(1, Dm))


def _rms_bwd(x, g, dh, dres, name):
    S, Dm = x.shape
    tm = 512

    def body(x_ref, g_ref, dh_ref, dres_ref, dx_ref, dxb_ref, dg_ref):
        xv, dhv = x_ref[...], dh_ref[...]
        r = lax.rsqrt(jnp.mean(xv * xv, axis=-1, keepdims=True) + EPS)
        t = dhv * g_ref[...]
        dx = dres_ref[...] + r * t - xv * (r * r * r) * jnp.mean(xv * t, axis=-1, keepdims=True)
        dx_ref[...] = dx
        dxb_ref[...] = dx.astype(BF16)

        @pl.when(pl.program_id(0) == 0)
        def _():
            dg_ref[...] = jnp.zeros_like(dg_ref)

        dg_ref[...] += jnp.sum(dhv * xv * r, axis=0, keepdims=True)

    row = pl.BlockSpec((tm, Dm), lambda i: (i, 0))
    vec = pl.BlockSpec((1, Dm), lambda i: (0, 0))
    return _pc(
        body, name=name, grid=(S // tm,),
        in_specs=[row, vec, row, row], out_specs=[row, row, vec],
        out_shape=[jax.ShapeDtypeStruct((S, Dm), F32), jax.ShapeDtypeStruct((S, Dm), BF16),
                   jax.ShapeDtypeStruct((1, Dm), F32)],
        compiler_params=_params(("arbitrary",)),
    )(x, g.reshape(1, Dm), dh, dres)


def _hn_fwd(x, gain, G):
    r = lax.rsqrt(_gmean(x * x, G) + EPS)
    return x * r * gain


def _hn_bwd(x, gain, dy, G):
    r = lax.rsqrt(_gmean(x * x, G) + EPS)
    t = dy * gain
    dx = r * t - x * (r * r * r) * _gmean(x * t, G)
    return dx, jnp.sum(dy * x * r, axis=0, keepdims=True)


def _fold_halves(v):
    return v + _roll(v, HEAD)


def _even_pre_fwd(proj, tab, qg, kg):
    S = proj.shape[0]
    tm = 256

    def body(p_ref, tab_ref, qg_ref, kg_ref, g_ref, rq_ref, rk_ref, rv_ref, dq_ref, dk_ref, dv_ref):
        Ar, Br, Cr = _tab(tab_ref, 0)
        Ap, Bp, Cp = _tab(tab_ref, 1)
        G = g_ref[...]
        for c in range(2):
            sl = slice(c * 128, (c + 1) * 128)
            rq_ref[:, sl] = _rope(p_ref[:, c * 128:(c + 1) * 128], Ar, Br, Cr, 32).astype(BF16)
            rk_ref[:, sl] = (_rope(p_ref[:, 256 + c * 128:256 + (c + 1) * 128], Ar, Br, Cr, 32) * 0.125).astype(BF16)
        rv_ref[...] = p_ref[:, 512:1024].astype(BF16)
        for c in range(4):
            sl = slice(c * 128, (c + 1) * 128)
            q = _hn_fwd(p_ref[:, 1536 + c * 128:1536 + (c + 1) * 128], qg_ref[...], G)
            dq_ref[:, sl] = _rope(q, Ap, Bp, Cp, 8).astype(BF16)
            k = _hn_fwd(p_ref[:, 2048 + c * 128:2048 + (c + 1) * 128], kg_ref[...], G)
            dk_ref[:, sl] = _rope(k, Ap, Bp, Cp, 8).astype(BF16)
        dv_ref[...] = p_ref[:, 2560:3072].astype(BF16)

    row = lambda w: pl.BlockSpec((tm, w), lambda i: (i, 0))
    vec = pl.BlockSpec((1, LANES), lambda i: (0, 0))
    return _pc(
        body, name="even_pre_fwd", grid=(S // tm,),
        in_specs=[row(3072), row(768), vec, vec, pl.BlockSpec((LANES, LANES), lambda i: (0, 0))],
        out_specs=[row(256), row(256), row(512), row(512), row(512), row(512)],
        out_shape=[jax.ShapeDtypeStruct((S, w), BF16) for w in (256, 256, 512, 512, 512, 512)],
        compiler_params=_params(("parallel",)),
    )(proj, tab, qg, kg, _group_matrix())


def _even_pre_bwd(proj, tab, qg, kg, drq, drk, drv, drg, dqs, dks, dvs):
    S = proj.shape[0]
    tm = 256
    npat = len(dqs)

    def body(p_ref, tab_ref, qg_ref, kg_ref, g_ref, drq_ref, drk_ref, drv_ref, drg_ref, *rest):
        dq_refs, dk_refs, dv_refs = rest[:npat], rest[npat:2 * npat], rest[2 * npat:3 * npat]
        dp_ref, dqg_ref, dkg_ref = rest[3 * npat:]
        Ar, Br, Cr = _tab(tab_ref, 0)
        Ap, Bp, Cp = _tab(tab_ref, 1)
        G = g_ref[...]
        for c in range(2):
            sl = slice(c * 128, (c + 1) * 128)
            dp_ref[:, c * 128:(c + 1) * 128] = _rope_t(drq_ref[:, sl], Ar, Br, Cr, 32).astype(BF16)
            dp_ref[:, 256 + c * 128:256 + (c + 1) * 128] = _rope_t(drk_ref[:, sl] * 0.125, Ar, Br, Cr, 32).astype(BF16)
        dp_ref[:, 512:1024] = drv_ref[...].astype(BF16)
        dp_ref[:, 1024:1536] = drg_ref[...].astype(BF16)
        accq = jnp.zeros((1, LANES), F32)
        acck = jnp.zeros((1, LANES), F32)
        for c in range(4):
            sl = slice(c * 128, (c + 1) * 128)
            g = dq_refs[0][:, sl]
            for r in dq_refs[1:]:
                g = g + r[:, sl]
            dx, dg = _hn_bwd(p_ref[:, 1536 + c * 128:1536 + (c + 1) * 128], qg_ref[...], _rope_t(g, Ap, Bp, Cp, 8), G)
            dp_ref[:, 1536 + c * 128:1536 + (c + 1) * 128] = dx.astype(BF16)
            accq = accq + dg
            g = dk_refs[0][:, sl]
            for r in dk_refs[1:]:
                g = g + r[:, sl]
            dx, dg = _hn_bwd(p_ref[:, 2048 + c * 128:2048 + (c + 1) * 128], kg_ref[...], _rope_t(g, Ap, Bp, Cp, 8), G)
            dp_ref[:, 2048 + c * 128:2048 + (c + 1) * 128] = dx.astype(BF16)
            acck = acck + dg
        g = dv_refs[0][...]
        for r in dv_refs[1:]:
            g = g + r[...]
        dp_ref[:, 2560:3072] = g.astype(BF16)

        @pl.when(pl.program_id(0) == 0)
        def _():
            dqg_ref[...] = jnp.zeros_like(dqg_ref)
            dkg_ref[...] = jnp.zeros_like(dkg_ref)

        dqg_ref[...] += _fold_halves(accq)
        dkg_ref[...] += _fold_halves(acck)

    row = lambda w: pl.BlockSpec((tm, w), lambda i: (i, 0))
    vec = pl.BlockSpec((1, LANES), lambda i: (0, 0))
    return _pc(
        body, name="even_pre_bwd", grid=(S // tm,),
        in_specs=[row(3072), row(768), vec, vec, pl.BlockSpec((LANES, LANES), lambda i: (0, 0)),
                  row(256), row(256), row(512), row(512)] + [row(512)] * (3 * npat),
        out_specs=[row(3072), vec, vec],
        out_shape=[jax.ShapeDtypeStruct((S, 3072), BF16), jax.ShapeDtypeStruct((1, LANES), F32),
                   jax.ShapeDtypeStruct((1, LANES), F32)],
        compiler_params=_params(("arbitrary",)),
    )(proj, tab, qg, kg, _group_matrix(), drq, drk, drv, drg, *dqs, *dks, *dvs)


def _ret_consts(pair, half):
    lg = jnp.where(pair == 0, _LOG_GAMMA[half], _LOG_GAMMA[2 + half]).astype(F32)
    i = lax.broadcasted_iota(jnp.int32, (BLK, BLK), 0)
    j = lax.broadcasted_iota(jnp.int32, (BLK, BLK), 1)
    diff = (i - j).astype(F32)
    decay = jnp.where(diff >= 0, jnp.exp(lg * jnp.maximum(diff, 0.0)), 0.0)
    t = lax.broadcasted_iota(jnp.int32, (BLK, 1), 0).astype(F32)
    xi = jnp.exp(lg * (t + 1.0))
    zeta = jnp.exp(lg * (BLK - 1.0 - t))
    cd = jnp.exp(jnp.full((1, 1), BLK, F32) * lg)
    return decay, xi, zeta, cd


def _ret_fwd(rq, rk, rv):
    S = rq.shape[0]
    nc = S // BLK

    def body(q_ref, k_ref, v_ref, o_ref, st_ref, R):
        p, n = pl.program_id(0), pl.program_id(1)

        @pl.when(n == 0)
        def _():
            R[...] = jnp.zeros_like(R)

        q2, k2 = q_ref[...], k_ref[...]
        for half in range(2):
            decay, xi, zeta, cd = _ret_consts(p, half)
            m = _head_mask((BLK, LANES), half)
            qm = jnp.where(m, q2, jnp.zeros_like(q2))
            km = jnp.where(m, k2, jnp.zeros_like(k2))
            v = v_ref[:, half * 128:(half + 1) * 128]
            Rb = R[half].astype(BF16)
            st_ref[half] = Rb
            sc = lax.dot_general(qm, k2, (((1,), (1,)), ((), ())), preferred_element_type=F32) * decay
            o = jnp.dot(sc.astype(BF16), v, preferred_element_type=F32)
            o = o + jnp.dot(qm, Rb, preferred_element_type=F32) * xi
            o_ref[:, half * 128:(half + 1) * 128] = o
            kz = (km.astype(F32) * zeta).astype(BF16)
            R[half] = R[half] * cd + lax.dot_general(kz, v, (((0,), (0,)), ((), ())), preferred_element_type=F32)

    return _pc(
        body, name="ret_fwd", grid=(2, nc),
        in_specs=[pl.BlockSpec((BLK, 128), lambda p, n: (n, p)), pl.BlockSpec((BLK, 128), lambda p, n: (n, p)),
                  pl.BlockSpec((BLK, 256), lambda p, n: (n, p))],
        out_specs=[pl.BlockSpec((BLK, 256), lambda p, n: (n, p)),
                   pl.BlockSpec((None, None, 2, 128, 128), lambda p, n: (p, n, 0, 0, 0))],
        out_shape=[jax.ShapeDtypeStruct((S, 512), F32), jax.ShapeDtypeStruct((2, nc, 2, 128, 128), BF16)],
        scratch_shapes=[pltpu.VMEM((2, 128, 128), F32)],
        compiler_params=_params(("parallel", "arbitrary")),
    )(rq, rk, rv)


def _ret_bwd(rq, rk, rv, states, do):
    S = rq.shape[0]
    nc = S // BLK

    def body(q_ref, k_ref, v_ref, st_ref, do_ref, dq_ref, dk_ref, dv_ref, U):
        p, n = pl.program_id(0), pl.program_id(1)

        @pl.when(n == 0)
        def _():
            U[...] = jnp.zeros_like(U)

        q2, k2 = q_ref[...], k_ref[...]
        dq_acc = jnp.zeros((BLK, LANES), F32)
        dk_acc = jnp.zeros((BLK, LANES), F32)
        for half in range(2):
            decay, xi, zeta, cd = _ret_consts(p, half)
            m = _head_mask((BLK, LANES), half)
            qm = jnp.where(m, q2, jnp.zeros_like(q2))
            km = jnp.where(m, k2, jnp.zeros_like(k2))
            v = v_ref[:, half * 128:(half + 1) * 128]
            dob = do_ref[:, half * 128:(half + 1) * 128].astype(BF16)
            Rb = st_ref[half]
            Ub = U[half].astype(BF16)
            nt = (((1,), (1,)), ((), ()))
            tn = (((0,), (0,)), ((), ()))
            dsc = (lax.dot_general(dob, v, nt, preferred_element_type=F32) * decay).astype(BF16)
            xdo = (dob.astype(F32) * xi).astype(BF16)
            dq_acc += jnp.dot(dsc, km, preferred_element_type=F32) + lax.dot_general(xdo, Rb, nt, preferred_element_type=F32)
            dk_acc += lax.dot_general(dsc, qm, tn, preferred_element_type=F32) \
                + lax.dot_general(v, Ub, nt, preferred_element_type=F32) * zeta
            sc = (lax.dot_general(qm, k2, nt, preferred_element_type=F32) * decay).astype(BF16)
            kz = (km.astype(F32) * zeta).astype(BF16)
            dv_ref[:, half * 128:(half + 1) * 128] = lax.dot_general(sc, dob, tn, preferred_element_type=F32) \
                + jnp.dot(kz, Ub, preferred_element_type=F32)
            U[half] = U[half] * cd + lax.dot_general(qm, xdo, tn, preferred_element_type=F32)
        dq_ref[...] = dq_acc
        dk_ref[...] = dk_acc

    rev = lambda w: pl.BlockSpec((BLK, w), lambda p, n: (nc - 1 - n, p))
    return _pc(
        body, name="ret_bwd", grid=(2, nc),
        in_specs=[rev(128), rev(128), rev(256),
                  pl.BlockSpec((None, None, 2, 128, 128), lambda p, n: (p, nc - 1 - n, 0, 0, 0)), rev(256)],
        out_specs=[rev(128), rev(128), rev(256)],
        out_shape=[jax.ShapeDtypeStruct((S, 256), F32), jax.ShapeDtypeStruct((S, 256), F32),
                   jax.ShapeDtypeStruct((S, 512), F32)],
        scratch_shapes=[pltpu.VMEM((2, 128, 128), F32)],
        compiler_params=_params(("parallel", "arbitrary")),
    )(rq, rk, rv, states, do)


def _col_of(b, m):
    return jnp.max(jnp.where(m, b, -jnp.inf), axis=1, keepdims=True)


def _attn_fwd(q, k, v, *, nq, max_dist, name, sinks=None, want_bf16=False):
    L, Ck = k.shape
    nb, ncol = L // BLK, Ck // LANES
    scale = HEAD ** -0.5
    has_sink = sinks is not None

    def body(*refs):
        q_ref, kp_ref, kc_ref, vp_ref, vc_ref = refs[:5]
        sk_ref = refs[5] if has_sink else None
        outs = refs[5 + has_sink:]
        n = pl.program_id(1)
        kcat = jnp.concatenate([kp_ref[...], kc_ref[...]], axis=0)
        vcat = jnp.concatenate([vp_ref[...], vc_ref[...]], axis=0)
        r = lax.broadcasted_iota(jnp.int32, (BLK, 2 * BLK), 0)
        c = lax.broadcasted_iota(jnp.int32, (BLK, 2 * BLK), 1)
        dist = r + BLK - c
        valid = (dist >= 0) & (dist <= max_dist) & ((c >= BLK) | (n > 0))
        for i in range(nq):
            q2 = q_ref[:, i * 128:(i + 1) * 128]
            o2 = jnp.zeros((BLK, LANES), F32)
            l2 = jnp.zeros((BLK, LANES), F32)
            for half in range(2):
                m = _head_mask((BLK, LANES), half)
                qm = jnp.where(m, q2, jnp.zeros_like(q2))
                s = lax.dot_general(qm, kcat, (((1,), (1,)), ((), ())), preferred_element_type=F32) * scale
                s = jnp.where(valid, s, -jnp.inf)
                mx = jnp.max(s, axis=1, keepdims=True)
                if has_sink:
                    snk = _col_of(sk_ref[:, i * 128:(i + 1) * 128], _head_mask((1, LANES), half))
                    mx = jnp.maximum(mx, snk)
                pr = jnp.exp(s - mx)
                den = jnp.sum(pr, axis=1, keepdims=True)
                if has_sink:
                    den = den + jnp.exp(snk - mx)
                pv = jnp.dot(pr.astype(BF16), vcat, preferred_element_type=F32) / den
                o2 = jnp.where(m, pv, o2)
                l2 = jnp.where(m, mx + jnp.log(den), l2)
            outs[0][:, i * 128:(i + 1) * 128] = o2
            outs[1][:, i * 128:(i + 1) * 128] = l2
            if want_bf16:
                outs[2][:, i * 128:(i + 1) * 128] = o2.astype(BF16)

    qspec = pl.BlockSpec((BLK, nq * 128), lambda j, n: (n, j))
    cur = pl.BlockSpec((BLK, 128), lambda j, n: (n, j))
    prev = pl.BlockSpec((BLK, 128), lambda j, n: (jnp.maximum(n - 1, 0), j))
    in_specs = [qspec, prev, cur, prev, cur]
    args = [q, k, k, v, v]
    if has_sink:
        in_specs.append(pl.BlockSpec((1, nq * 128), lambda j, n: (0, j)))
        args.append(sinks)
    out_dts = [F32, F32] + ([BF16] if want_bf16 else [])
    return _pc(
        body, name=name, grid=(ncol, nb), in_specs=in_specs,
        out_specs=[qspec] * len(out_dts),
        out_shape=[jax.ShapeDtypeStruct(q.shape, dt) for dt in out_dts],
        compiler_params=_params(("parallel", "parallel")),
    )(*args)


def _attn_bwd(q, k, v, o, lse, do, *, nq, max_dist, name, sinks=None):
    L, Ck = k.shape
    nb, ncol = L // BLK, Ck // LANES
    scale = HEAD ** -0.5
    has_sink = sinks is not None
    nt = (((1,), (1,)), ((), ()))
    tn = (((0,), (0,)), ((), ()))

    def body(*refs):
        (qc_ref, qn_ref, kp_ref, kc_ref, vp_ref, vc_ref, oc_ref, on_ref, lc_ref, ln_ref, dc_ref, dn_ref) = refs[:12]
        sk_ref = refs[12] if has_sink else None
        outs = refs[12 + has_sink:]
        dq_ref, dk_ref, dv_ref = outs[:3]
        n = pl.program_id(1)
        kc, vc = kc_ref[...], vc_ref[...]
        kcat = jnp.concatenate([kp_ref[...], kc], axis=0)
        vcat = jnp.concatenate([vp_ref[...], vc], axis=0)
        r = lax.broadcasted_iota(jnp.int32, (BLK, 2 * BLK), 0)
        c = lax.broadcasted_iota(jnp.int32, (BLK, 2 * BLK), 1)
        dist = r + BLK - c
        valid_q = (dist >= 0) & (dist <= max_dist) & ((c >= BLK) | (n > 0))
        r2 = lax.broadcasted_iota(jnp.int32, (2 * BLK, BLK), 0)
        c2 = lax.broadcasted_iota(jnp.int32, (2 * BLK, BLK), 1)
        dist2 = r2 - c2
        valid_k = (dist2 >= 0) & (dist2 <= max_dist) & ((r2 < BLK) | (n < nb - 1))
        dk_acc = jnp.zeros((BLK, LANES), F32)
        dv_acc = jnp.zeros((BLK, LANES), F32)
        for i in range(nq):
            sl = slice(i * 128, (i + 1) * 128)
            qcur, docur = qc_ref[:, sl], dc_ref[:, sl]
            qcat = jnp.concatenate([qcur, qn_ref[:, sl]], axis=0)
            docat = jnp.concatenate([docur, dn_ref[:, sl]], axis=0)
            ocat = jnp.concatenate([oc_ref[:, sl], on_ref[:, sl]], axis=0)
            lcat = jnp.concatenate([lc_ref[:, sl], ln_ref[:, sl]], axis=0)
            dq2 = jnp.zeros((BLK, LANES), F32)
            ds2 = jnp.zeros((1, LANES), F32)
            for half in range(2):
                m1 = _head_mask((BLK, LANES), half)
                m2 = _head_mask((2 * BLK, LANES), half)
                dom = jnp.where(m2, docat, 0.0)
                delta = jnp.sum(dom * ocat, axis=1, keepdims=True)
                lcol = _col_of(lcat, m2)
                domb = dom.astype(BF16)
                qmcat = jnp.where(m2, qcat, jnp.zeros_like(qcat))
                qm = qmcat[:BLK]
                s = lax.dot_general(qm, kcat, nt, preferred_element_type=F32) * scale
                pr = jnp.where(valid_q, jnp.exp(s - lcol[:BLK]), 0.0)
                dp = lax.dot_general(domb[:BLK], vcat, nt, preferred_element_type=F32)
                ds = (pr * (dp - delta[:BLK])).astype(BF16)
                dq2 = jnp.where(m1, jnp.dot(ds, kcat, preferred_element_type=F32) * scale, dq2)
                if has_sink:
                    snk = _col_of(sk_ref[:, sl], _head_mask((1, LANES), half))
                    contrib = jnp.sum(-jnp.exp(snk - lcol[:BLK]) * delta[:BLK], axis=0, keepdims=True)
                    ds2 = jnp.where(_head_mask((1, LANES), half), contrib, ds2)
                s = lax.dot_general(qmcat, kc, nt, preferred_element_type=F32) * scale
                pr = jnp.where(valid_k, jnp.exp(s - lcol), 0.0)
                dv_acc += lax.dot_general(pr.astype(BF16), domb, tn, preferred_element_type=F32)
                dp = lax.dot_general(domb, vc, nt, preferred_element_type=F32)
                ds = (pr * (dp - delta)).astype(BF16)
                dk_acc += lax.dot_general(ds, qmcat, tn, preferred_element_type=F32) * scale
            dq_ref[:, sl] = dq2
            if has_sink:
                @pl.when(n == 0)
                def _():
                    outs[3][:, sl] = jnp.zeros((1, LANES), F32)

                outs[3][:, sl] += ds2
        dk_ref[...] = dk_acc
        dv_ref[...] = dv_acc

    qcur = pl.BlockSpec((BLK, nq * 128), lambda j, n: (n, j))
    qnext = pl.BlockSpec((BLK, nq * 128), lambda j, n: (jnp.minimum(n + 1, nb - 1), j))
    cur = pl.BlockSpec((BLK, 128), lambda j, n: (n, j))
    prev = pl.BlockSpec((BLK, 128), lambda j, n: (jnp.maximum(n - 1, 0), j))
    in_specs = [qcur, qnext, prev, cur, prev, cur, qcur, qnext, qcur, qnext, qcur, qnext]
    args = [q, q, k, k, v, v, o, o, lse, lse, do, do]
    out_specs = [qcur, cur, cur]
    out_shape = [jax.ShapeDtypeStruct(q.shape, F32), jax.ShapeDtypeStruct(k.shape, F32), jax.ShapeDtypeStruct(k.shape, F32)]
    if has_sink:
        vec = pl.BlockSpec((1, nq * 128), lambda j, n: (0, j))
        in_specs.append(vec)
        args.append(sinks)
        out_specs.append(vec)
        out_shape.append(jax.ShapeDtypeStruct((1, q.shape[1]), F32))
    return _pc(
        body, name=name, grid=(ncol, nb), in_specs=in_specs, out_specs=out_specs, out_shape=out_shape,
        compiler_params=_params(("parallel", "arbitrary")),
    )(*args)


ATT_TILE = 2048


def _rows(ref, start, n, r):
    if r == 1:
        return ref[pl.ds(start, n), :]
    return ref[pl.ds(start, n, stride=r), :]


def _set_rows(ref, start, n, r, val):
    if r == 1:
        ref[pl.ds(start, n), :] = val
    else:
        ref[pl.ds(start, n, stride=r), :] = val


def _band_geometry(S, patterns):
    rmax = max(r for _, r in patterns)
    H = BLK * rmax
    T = min(S, ATT_TILE)
    assert T % H == 0 and S % T == 0
    return H, T, S // T, T // BLK


def _band_fwd(q, k, v, *, patterns, nq, name, sinks=None, want_bf16=False):
    S, Ck = k.shape
    H, T, nt, nbt = _band_geometry(S, patterns)
    ncol = Ck // LANES
    scale = HEAD ** -0.5
    has_sink = sinks is not None
    nt_dims = (((1,), (1,)), ((), ()))

    def body(*refs):
        q_ref, kp_ref, kc_ref, vp_ref, vc_ref = refs[:5]
        sk_ref = refs[5] if has_sink else None
        n_out = 3 if want_bf16 else 2
        outs = refs[5 + has_sink:5 + has_sink + n_out]
        qf, kf, vf, M, L, A = refs[5 + has_sink + n_out:]
        t = pl.program_id(1)
        kf[0:H, :] = kp_ref[...].astype(F32)
        kf[H:H + T, :] = kc_ref[...].astype(F32)
        vf[0:H, :] = vp_ref[...].astype(F32)
        vf[H:H + T, :] = vc_ref[...].astype(F32)
        r_i = lax.broadcasted_iota(jnp.int32, (BLK, 2 * BLK), 0)
        c_i = lax.broadcasted_iota(jnp.int32, (BLK, 2 * BLK), 1)
        dist_i = r_i + BLK - c_i
        masks = [_head_mask((BLK, LANES), h) for h in range(2)]

        for i in range(nq):
            qf[...] = q_ref[:, i * 128:(i + 1) * 128].astype(F32) * scale
            for p, (dist, r) in enumerate(patterns):
                in_band = (dist_i >= 0) & (dist_i <= dist)
                in_band_first = in_band & ((c_i >= BLK) | (t > 0))

                def unit(j, b, p=p, r=r, in_band=in_band, in_band_first=in_band_first):
                    q0 = j + b * (BLK * r)
                    q2 = _rows(qf, q0, BLK, r).astype(BF16)
                    kcat = _rows(kf, H + q0 - BLK * r, 2 * BLK, r).astype(BF16)
                    vcat = _rows(vf, H + q0 - BLK * r, 2 * BLK, r).astype(BF16)
                    valid = in_band if b > 0 else in_band_first
                    m2 = jnp.zeros((BLK, LANES), F32)
                    l2 = jnp.zeros((BLK, LANES), F32)
                    a2 = jnp.zeros((BLK, LANES), F32)
                    for half in range(2):
                        m = masks[half]
                        qm = jnp.where(m, q2, jnp.zeros_like(q2))
                        s = lax.dot_general(qm, kcat, nt_dims, preferred_element_type=F32)
                        s = jnp.where(valid, s, -jnp.inf)
                        mx = jnp.max(s, axis=1, keepdims=True)
                        pr = jnp.exp(s - mx)
                        den = jnp.sum(pr, axis=1, keepdims=True)
                        pv = jnp.dot(pr.astype(BF16), vcat, preferred_element_type=F32)
                        m2 = jnp.where(m, mx, m2)
                        l2 = jnp.where(m, den, l2)
                        a2 = jnp.where(m, pv, a2)
                    if p > 0:
                        mo = _rows(M, q0, BLK, r)
                        mn = jnp.maximum(mo, m2)
                        wa, wb = jnp.exp(mo - mn), jnp.exp(m2 - mn)
                        l2 = wa * _rows(L, q0, BLK, r) + wb * l2
                        a2 = wa * _rows(A, q0, BLK, r) + wb * a2
                        m2 = mn
                    _set_rows(M, q0, BLK, r, m2)
                    _set_rows(L, q0, BLK, r, l2)
                    _set_rows(A, q0, BLK, r, a2)

                for u in range(nbt):
                    unit(u % r, u // r)
            sl = slice(i * 128, (i + 1) * 128)
            mm, ll, aa = M[...], L[...], A[...]
            if has_sink:
                snk = sk_ref[:, sl]
                mn = jnp.maximum(mm, snk)
                w = jnp.exp(mm - mn)
                ll = ll * w + jnp.exp(snk - mn)
                aa = aa * w
                mm = mn
            o = aa / ll
            outs[0][:, sl] = o
            outs[1][:, sl] = mm + jnp.log(ll)
            if want_bf16:
                outs[2][:, sl] = o.astype(BF16)

    th = T // H
    qspec = pl.BlockSpec((T, nq * 128), lambda j, t: (t, j))
    cur = pl.BlockSpec((T, 128), lambda j, t: (t, j))
    prev = pl.BlockSpec((H, 128), lambda j, t: (jnp.maximum(t * th - 1, 0), j))
    in_specs = [qspec, prev, cur, prev, cur]
    args = [q, k, k, v, v]
    if has_sink:
        in_specs.append(pl.BlockSpec((1, nq * 128), lambda j, t: (0, j)))
        args.append(sinks)
    out_dts = [F32, F32] + ([BF16] if want_bf16 else [])
    return _pc(
        body, name=name, grid=(ncol, nt), in_specs=in_specs,
        out_specs=[qspec] * len(out_dts),
        out_shape=[jax.ShapeDtypeStruct(q.shape, dt) for dt in out_dts],
        scratch_shapes=[pltpu.VMEM((T, LANES), F32), pltpu.VMEM((H + T, LANES), F32), pltpu.VMEM((H + T, LANES), F32),
                        pltpu.VMEM((T, LANES), F32), pltpu.VMEM((T, LANES), F32), pltpu.VMEM((T, LANES), F32)],
        compiler_params=_params(("parallel", "parallel")),
    )(*args)


def _band_bwd(q, k, v, lse, delta, do, *, patterns, nq, name, sinks=None, do_col0=0):
    S, Ck = k.shape
    H, T, nt, nbt = _band_geometry(S, patterns)
    ncol = Ck // LANES
    scale = HEAD ** -0.5
    has_sink = sinks is not None
    nt_dims = (((1,), (1,)), ((), ()))
    tn_dims = (((0,), (0,)), ((), ()))

    def body(*refs):
        (qc_ref, qn_ref, kp_ref, kc_ref, vp_ref, vc_ref, lc_ref, ln_ref, ec_ref, en_ref, dc_ref, dn_ref) = refs[:12]
        sk_ref = refs[12] if has_sink else None
        n_out = 4 if has_sink else 3
        outs = refs[12 + has_sink:12 + has_sink + n_out]
        dq_ref, dk_ref, dv_ref = outs[:3]
        qf, kf, vf, lf, ef, df = refs[12 + has_sink + n_out:]
        t = pl.program_id(1)
        kf[0:H, :] = kp_ref[...].astype(F32)
        kf[H:H + T, :] = kc_ref[...].astype(F32)
        vf[0:H, :] = vp_ref[...].astype(F32)
        vf[H:H + T, :] = vc_ref[...].astype(F32)
        dk_ref[...] = jnp.zeros_like(dk_ref)
        dv_ref[...] = jnp.zeros_like(dv_ref)
        r_i = lax.broadcasted_iota(jnp.int32, (BLK, 2 * BLK), 0)
        c_i = lax.broadcasted_iota(jnp.int32, (BLK, 2 * BLK), 1)
        dist_q = r_i + BLK - c_i
        dist_h = dist_q[:, :BLK]
        m1 = [_head_mask((BLK, LANES), h) for h in range(2)]

        def head_inputs(half, q2, do2, l2, e2):
            m = m1[half]
            lh = jnp.where(m, l2, _roll(l2, HEAD))
            eh = jnp.where(m, e2, _roll(e2, HEAD))
            return jnp.where(m, q2, jnp.zeros_like(q2)), jnp.where(m, do2, 0.0).astype(BF16), lh, eh

        for i in range(nq):
            sl = slice(i * 128, (i + 1) * 128)
            qf[0:T, :] = qc_ref[:, sl].astype(F32) * scale
            qf[T:T + H, :] = qn_ref[:, sl].astype(F32) * scale
            for buf, c_ref, n_ref in ((lf, lc_ref, ln_ref), (ef, ec_ref, en_ref), (df, dc_ref, dn_ref)):
                buf[0:T, :] = c_ref[:, sl]
                buf[T:T + H, :] = n_ref[:, sl]
            if has_sink:
                @pl.when(t == 0)
                def _():
                    outs[3][:, sl] = jnp.zeros((1, LANES), F32)

                outs[3][:, sl] += jnp.sum(-jnp.exp(sk_ref[:, sl] - lc_ref[:, sl]) * ec_ref[:, sl], axis=0, keepdims=True)
            for p, (dist, r) in enumerate(patterns):
                band_q = (dist_q >= 0) & (dist_q <= dist)
                band_first = band_q & ((c_i >= BLK) | (t > 0))
                band_h = (dist_h >= 0) & (dist_h <= dist)

                def add_rows(ref, start, val, r=r):
                    _set_rows(ref, start, BLK, r, _rows(ref, start, BLK, r) + val)

                def unit(j, b, p=p, r=r, band_q=band_q, band_first=band_first):
                    q0 = j + b * (BLK * r)
                    q2 = _rows(qf, q0, BLK, r).astype(BF16)
                    do2, l2, e2 = _rows(df, q0, BLK, r), _rows(lf, q0, BLK, r), _rows(ef, q0, BLK, r)
                    kcat = _rows(kf, H + q0 - BLK * r, 2 * BLK, r).astype(BF16)
                    vcat = _rows(vf, H + q0 - BLK * r, 2 * BLK, r).astype(BF16)
                    valid = band_q if b > 0 else band_first
                    dq2 = jnp.zeros((BLK, LANES), F32)
                    dkc = jnp.zeros((2 * BLK, LANES), F32)
                    dvc = jnp.zeros((2 * BLK, LANES), F32)
                    for half in range(2):
                        qm, dom, lh, eh = head_inputs(half, q2, do2, l2, e2)
                        s = lax.dot_general(qm, kcat, nt_dims, preferred_element_type=F32)
                        pr = jnp.where(valid, jnp.exp(s - jnp.concatenate([lh, lh], axis=1)), 0.0)
                        dp = lax.dot_general(dom, vcat, nt_dims, preferred_element_type=F32)
                        ds = (pr * (dp - jnp.concatenate([eh, eh], axis=1))).astype(BF16)
                        dq2 = jnp.where(m1[half], jnp.dot(ds, kcat, preferred_element_type=F32) * scale, dq2)
                        dvc += lax.dot_general(pr.astype(BF16), dom, tn_dims, preferred_element_type=F32)
                        dkc += lax.dot_general(ds, qm, tn_dims, preferred_element_type=F32)
                    if p > 0:
                        dq2 = dq2 + _rows(dq_ref.at[:, sl], q0, BLK, r)
                    _set_rows(dq_ref.at[:, sl], q0, BLK, r, dq2)
                    add_rows(dk_ref, q0, dkc[BLK:])
                    add_rows(dv_ref, q0, dvc[BLK:])
                    if b > 0:
                        add_rows(dk_ref, q0 - BLK * r, dkc[:BLK])
                        add_rows(dv_ref, q0 - BLK * r, dvc[:BLK])

                def halo_unit(j, r=r, band_h=band_h):
                    k0 = j + (nbt // r - 1) * (BLK * r)
                    q2 = _rows(qf, T + j, BLK, r).astype(BF16)
                    do2, l2, e2 = _rows(df, T + j, BLK, r), _rows(lf, T + j, BLK, r), _rows(ef, T + j, BLK, r)
                    kc = _rows(kf, H + k0, BLK, r).astype(BF16)
                    vc = _rows(vf, H + k0, BLK, r).astype(BF16)
                    dk2 = jnp.zeros((BLK, LANES), F32)
                    dv2 = jnp.zeros((BLK, LANES), F32)
                    for half in range(2):
                        qm, dom, lh, eh = head_inputs(half, q2, do2, l2, e2)
                        s = lax.dot_general(qm, kc, nt_dims, preferred_element_type=F32)
                        pr = jnp.where(band_h, jnp.exp(s - lh), 0.0)
                        dp = lax.dot_general(dom, vc, nt_dims, preferred_element_type=F32)
                        ds = (pr * (dp - eh)).astype(BF16)
                        dv2 += lax.dot_general(pr.astype(BF16), dom, tn_dims, preferred_element_type=F32)
                        dk2 += lax.dot_general(ds, qm, tn_dims, preferred_element_type=F32)
                    add_rows(dk_ref, k0, dk2)
                    add_rows(dv_ref, k0, dv2)

                for u in range(nbt):
                    unit(u % r, u // r)
                if nt > 1:
                    @pl.when(t < nt - 1)
                    def _(r=r, halo_unit=halo_unit):
                        for j in range(r):
                            halo_unit(j)

    th = T // H
    last = S // H - 1
    qcur = pl.BlockSpec((T, nq * 128), lambda j, t: (t, j))
    qnext = pl.BlockSpec((H, nq * 128), lambda j, t: (jnp.minimum((t + 1) * th, last), j))
    cur = pl.BlockSpec((T, 128), lambda j, t: (t, j))
    prev = pl.BlockSpec((H, 128), lambda j, t: (jnp.maximum(t * th - 1, 0), j))
    dcur = pl.BlockSpec((T, nq * 128), lambda j, t: (t, j + do_col0))
    dnext = pl.BlockSpec((H, nq * 128), lambda j, t: (jnp.minimum((t + 1) * th, last), j + do_col0))
    in_specs = [qcur, qnext, prev, cur, prev, cur, qcur, qnext, qcur, qnext, dcur, dnext]
    args = [q, q, k, k, v, v, lse, lse, delta, delta, do, do]
    out_specs = [qcur, cur, cur]
    out_shape = [jax.ShapeDtypeStruct(q.shape, F32), jax.ShapeDtypeStruct(k.shape, F32), jax.ShapeDtypeStruct(k.shape, F32)]
    if has_sink:
        vec = pl.BlockSpec((1, nq * 128), lambda j, t: (0, j))
        in_specs.append(vec)
        args.append(sinks)
        out_specs.append(vec)
        out_shape.append(jax.ShapeDtypeStruct((1, q.shape[1]), F32))
    big = pltpu.VMEM((T + H, LANES), F32)
    return _pc(
        body, name=name, grid=(ncol, nt), in_specs=in_specs, out_specs=out_specs, out_shape=out_shape,
        scratch_shapes=[big] * 6,
        compiler_params=_params(("parallel", "arbitrary")),
    )(*args)


def _delta(do, o, name):
    S, C = do.shape
    tm = 512

    def body(do_ref, o_ref, g_ref, e_ref):
        for c in range(C // LANES):
            sl = slice(c * 128, (c + 1) * 128)
            e_ref[:, sl] = _gmean(do_ref[:, sl] * o_ref[:, sl], g_ref[...]) * float(HEAD)

    row = pl.BlockSpec((tm, C), lambda i: (i, 0))
    return _pc(
        body, name=name, grid=(S // tm,),
        in_specs=[row, row, pl.BlockSpec((LANES, LANES), lambda i: (0, 0))], out_specs=row,
        out_shape=jax.ShapeDtypeStruct((S, C), F32),
        compiler_params=_params(("parallel",)),
    )(do, o, _group_matrix())


def _even_post_fwd(ro, proj, gn, da):
    S = ro.shape[0]
    tm = 256

    def body(ro_ref, rg_ref, gn_ref, da_ref, mix_ref):
        for c in range(4):
            sl = slice(c * 128, (c + 1) * 128)
            x = ro_ref[:, sl]
            mu = jnp.mean(x, axis=1, keepdims=True)
            xc = x - mu
            var = jnp.mean(xc * xc, axis=1, keepdims=True)
            y = xc * lax.rsqrt(var + EPS) * gn_ref[:, sl]
            z = rg_ref[:, sl]
            mix_ref[:, sl] = (z * jax.nn.sigmoid(z) * y).astype(BF16)
        mix_ref[:, 512:1024] = da_ref[...].astype(BF16)

    row = lambda w: pl.BlockSpec((tm, w), lambda i: (i, 0))
    return _pc(
        body, name="even_post_fwd", grid=(S // tm,),
        in_specs=[row(512), pl.BlockSpec((tm, 512), lambda i: (i, 2)), pl.BlockSpec((1, 512), lambda i: (0, 0)), row(512)],
        out_specs=row(1024), out_shape=jax.ShapeDtypeStruct((S, 1024), BF16),
        compiler_params=_params(("parallel",)),
    )(ro, proj, gn, da)


def _even_post_bwd(ro, proj, gn, dmixed):
    S = ro.shape[0]
    tm = 256

    def body(ro_ref, rg_ref, gn_ref, dm_ref, dro_ref, drg_ref, dgn_ref):
        @pl.when(pl.program_id(0) == 0)
        def _():
            dgn_ref[...] = jnp.zeros_like(dgn_ref)

        for c in range(4):
            sl = slice(c * 128, (c + 1) * 128)
            x = ro_ref[:, sl]
            mu = jnp.mean(x, axis=1, keepdims=True)
            xc = x - mu
            rstd = lax.rsqrt(jnp.mean(xc * xc, axis=1, keepdims=True) + EPS)
            xh = xc * rstd
            gain = gn_ref[:, sl]
            y = xh * gain
            z = rg_ref[:, sl]
            sg = jax.nn.sigmoid(z)
            dra = dm_ref[:, sl]
            drg_ref[:, sl] = dra * y * sg * (1.0 + z * (1.0 - sg))
            dy = dra * z * sg
            dgn_ref[:, sl] += jnp.sum(dy * xh, axis=0, keepdims=True)
            dxh = dy * gain
            dro_ref[:, sl] = rstd * (dxh - jnp.mean(dxh, axis=1, keepdims=True)
                                     - xh * jnp.mean(dxh * xh, axis=1, keepdims=True))

    row = lambda w: pl.BlockSpec((tm, w), lambda i: (i, 0))
    vec = pl.BlockSpec((1, 512), lambda i: (0, 0))
    return _pc(
        body, name="even_post_bwd", grid=(S // tm,),
        in_specs=[row(512), pl.BlockSpec((tm, 512), lambda i: (i, 2)), vec, row(512)],
        out_specs=[row(512), row(512), vec],
        out_shape=[jax.ShapeDtypeStruct((S, 512), F32), jax.ShapeDtypeStruct((S, 512), F32),
                   jax.ShapeDtypeStruct((1, 512), F32)],
        compiler_params=_params(("arbitrary",)),
    )(ro, proj, gn, dmixed)


def _swa_pre_fwd(proj, tab, qg, kg):
    S = proj.shape[0]
    tm = 256

    def body(p_ref, tab_ref, qg_ref, kg_ref, g_ref, q_ref, k_ref, v_ref):
        Ap, Bp, Cp = _tab(tab_ref, 1)
        G = g_ref[...]
        lo = _head_mask((tm, LANES), 0)
        for c in range(8):
            sl = slice(c * 128, (c + 1) * 128)
            q_ref[:, sl] = _rope(_hn_fwd(p_ref[:, sl], qg_ref[...], G), Ap, Bp, Cp, 8).astype(BF16)
        for c in range(2):
            kn = _rope(_hn_fwd(p_ref[:, 1024 + c * 128:1024 + (c + 1) * 128], kg_ref[...], G), Ap, Bp, Cp, 8)
            vv = p_ref[:, 1280 + c * 128:1280 + (c + 1) * 128]
            for t, ref in ((kn, k_ref), (vv, v_ref)):
                sw = _roll(t, HEAD)
                ref[:, (2 * c) * 128:(2 * c + 1) * 128] = jnp.where(lo, t, sw).astype(BF16)
                ref[:, (2 * c + 1) * 128:(2 * c + 2) * 128] = jnp.where(lo, sw, t).astype(BF16)

    row = lambda w: pl.BlockSpec((tm, w), lambda i: (i, 0))
    vec = pl.BlockSpec((1, LANES), lambda i: (0, 0))
    return _pc(
        body, name="swa_pre_fwd", grid=(S // tm,),
        in_specs=[row(1536), row(768), vec, vec, pl.BlockSpec((LANES, LANES), lambda i: (0, 0))],
        out_specs=[row(1024), row(512), row(512)],
        out_shape=[jax.ShapeDtypeStruct((S, w), BF16) for w in (1024, 512, 512)],
        compiler_params=_params(("parallel",)),
    )(proj, tab, qg, kg, _group_matrix())


def _swa_pre_bwd(proj, tab, qg, kg, dq, dk, dv):
    S = proj.shape[0]
    tm = 256

    def body(p_ref, tab_ref, qg_ref, kg_ref, g_ref, dq_ref, dk_ref, dv_ref, dp_ref, db_ref, dqg_ref, dkg_ref):
        Ap, Bp, Cp = _tab(tab_ref, 1)
        G = g_ref[...]
        lo = _head_mask((tm, LANES), 0)

        @pl.when(pl.program_id(0) == 0)
        def _():
            db_ref[...] = jnp.zeros_like(db_ref)
            dqg_ref[...] = jnp.zeros_like(dqg_ref)
            dkg_ref[...] = jnp.zeros_like(dkg_ref)

        accq = jnp.zeros((1, LANES), F32)
        acck = jnp.zeros((1, LANES), F32)
        for c in range(8):
            sl = slice(c * 128, (c + 1) * 128)
            dx, dg = _hn_bwd(p_ref[:, sl], qg_ref[...], _rope_t(dq_ref[:, sl], Ap, Bp, Cp, 8), G)
            dp_ref[:, sl] = dx.astype(BF16)
            db_ref[:, sl] += jnp.sum(dx, axis=0, keepdims=True)
            accq = accq + dg
        for c in range(2):
            folded = []
            for ref in (dk_ref, dv_ref):
                a = ref[:, (2 * c) * 128:(2 * c + 1) * 128]
                b = ref[:, (2 * c + 1) * 128:(2 * c + 2) * 128]
                folded.append(jnp.where(lo, a + _roll(a, HEAD), b + _roll(b, HEAD)))
            ks = slice(1024 + c * 128, 1024 + (c + 1) * 128)
            dx, dg = _hn_bwd(p_ref[:, ks], kg_ref[...], _rope_t(folded[0], Ap, Bp, Cp, 8), G)
            dp_ref[:, ks] = dx.astype(BF16)
            db_ref[:, ks] += jnp.sum(dx, axis=0, keepdims=True)
            acck = acck + dg
            vs = slice(1280 + c * 128, 1280 + (c + 1) * 128)
            dp_ref[:, vs] = folded[1].astype(BF16)
            db_ref[:, vs] += jnp.sum(folded[1], axis=0, keepdims=True)
        dqg_ref[...] += _fold_halves(accq)
        dkg_ref[...] += _fold_halves(acck)

    row = lambda w: pl.BlockSpec((tm, w), lambda i: (i, 0))
    vec = pl.BlockSpec((1, LANES), lambda i: (0, 0))
    return _pc(
        body, name="swa_pre_bwd", grid=(S // tm,),
        in_specs=[row(1536), row(768), vec, vec, pl.BlockSpec((LANES, LANES), lambda i: (0, 0)),
                  row(1024), row(512), row(512)],
        out_specs=[row(1536), pl.BlockSpec((1, 1536), lambda i: (0, 0)), vec, vec],
        out_shape=[jax.ShapeDtypeStruct((S, 1536), BF16), jax.ShapeDtypeStruct((1, 1536), F32),
                   jax.ShapeDtypeStruct((1, LANES), F32), jax.ShapeDtypeStruct((1, LANES), F32)],
        compiler_params=_params(("arbitrary",)),
    )(proj, tab, qg, kg, _group_matrix(), dq, dk, dv)


def _loss_head(y, target):
    S, Dm = y.shape
    tm = 512

    def body(y_ref, t_ref, l_ref, dy_ref, dyb_ref):
        @pl.when(pl.program_id(0) == 0)
        def _():
            l_ref[...] = jnp.zeros_like(l_ref)

        e = y_ref[...] - t_ref[...]
        dy = e * (1.0 / Dm)
        dy_ref[...] = dy
        dyb_ref[...] = dy.astype(BF16)
        row = jnp.sum(e * e, axis=1, keepdims=True) * (0.5 / Dm)
        l_ref[...] += jnp.sum(row, axis=0, keepdims=True)

    row = pl.BlockSpec((tm, Dm), lambda i: (i, 0))
    return _pc(
        body, name="loss_head", grid=(S // tm,), in_specs=[row, row],
        out_specs=[pl.BlockSpec((1, LANES), lambda i: (0, 0)), row, row],
        out_shape=[jax.ShapeDtypeStruct((1, LANES), F32), jax.ShapeDtypeStruct((S, Dm), F32),
                   jax.ShapeDtypeStruct((S, Dm), BF16)],
        compiler_params=_params(("arbitrary",)),
    )(y, target)


def _relu2_of(u):
    r = jnp.maximum(u.astype(F32), 0.0)
    return r * r


def _drelu2(acc, u):
    return (acc * 2.0 * jnp.maximum(u.astype(F32), 0.0),)


def _add(acc, res):
    return (acc + res,)


_T = dict(tm=1024, tn=1024, tk=1024)


def _rms_bwd_in(x, g, dres):
    def epilogue(dh, xv, gv, dr):
        r = lax.rsqrt(jnp.mean(xv * xv, axis=-1, keepdims=True) + EPS)
        t = dh * gv
        dx = dr + r * t - xv * (r * r * r) * jnp.mean(xv * t, axis=-1, keepdims=True)
        return dx, dx, jnp.sum(dh * xv * r, axis=0, keepdims=True)

    return dict(outs=[F32, BF16, ("colsum",)], epilogue=epilogue,
                extras=[(x, "mn"), (g.reshape(1, D_MODEL), "n"), (dres, "mn")])


def _delta_in(o, col0):
    width = D_MODEL - col0

    def epilogue(do, ov, G):
        parts = [_gmean(do[:, col0 + c * 128:col0 + (c + 1) * 128] * ov[:, c * 128:(c + 1) * 128], G) * float(HEAD)
                 for c in range(width // LANES)]
        return do, jnp.concatenate(parts, axis=1)

    return dict(outs=[F32, (F32, width)], epilogue=epilogue, extras=[(o, width), (_group_matrix(), "full")])


def _mlp_fwd(x, g, wts, layer, tag):
    h = _rms_fwd(x, g, f"rms_mlp_fwd{tag}")
    u = _matmul(h, wts, dims="nn", **_T, outs=[BF16], b_cs=True, b_row0=layer, name=f"mlp_up{tag}")
    x_out = _matmul(u, wts, dims="nn", **_T, outs=[F32], epilogue=_add, extras=[(x, "mn")], a_pro=_relu2_of,
                    b_rs=1024, b_row0=2 + layer, name=f"mlp_down{tag}")
    return x_out, (h, u)


def _mlp_bwd(x, g, wts, layer, saved, dy, dyb, tag):
    h, u = saved
    du = _matmul(dyb, wts, dims="nt", **_T, outs=[BF16], epilogue=_drelu2, extras=[(u, "mn")], b_rs=1024,
                 b_row0=2 + layer, name=f"mlp_du{tag}")
    dw_dn = _matmul(u, dyb, dims="tn", **_T, outs=[F32], a_pro=_relu2_of, name=f"mlp_dwdown{tag}")
    dw_up = _matmul(h, du, dims="tn", **_T, outs=[F32], o_cs=N_CHIPS, name=f"mlp_dwup{tag}")
    dx, dxb, dg = _matmul(du, wts, dims="nt", tm=512, tn=1024, tk=1024, b_cs=True, b_row0=layer, b_rows=1024, name=f"mlp_dh{tag}",
                          **_rms_bwd_in(x, g, dy))
    return dx, dxb, dg, dw_up, dw_dn


def _pattern_view(t, r):
    S, C = t.shape
    return t.reshape(S // r, r * C)


def _local_step(x, pos_col, target, W, rest_of, P, red):
    S = x.shape[0]
    tab = _tables(pos_col)
    tile2 = lambda g: jnp.tile(g.reshape(1, HEAD), (1, 2))
    dqg, dkg = tile2(P["dil_q_gain"]), tile2(P["dil_k_gain"])
    sqg, skg = tile2(P["swa_q_gain"]), tile2(P["swa_k_gain"])
    gn = P["ret_gn_gain"].reshape(1, 512)
    sink_b = jnp.repeat(P["swa_sinks"].reshape(16), HEAD).reshape(1, 1024)

    h0 = _rms_fwd(x, P["norm_mix"][0], "rms_mix_fwd0")
    proj = _matmul(h0, W["hyb_w_in"], dims="nn", tm=1024, tn=768, tk=1024, outs=[F32], b_cs=True, name="hyb_in")
    rq, rk, rv, dq, dk, dv = _even_pre_fwd(proj, tab, dqg, dkg)
    ro, states = _ret_fwd(rq, rk, rv)
    dil = [(w // r, r) for w, r in DIL_PATTERNS]
    da, dlse = _band_fwd(dq, dk, dv, patterns=dil, nq=1, name="dil_fwd")
    mixed = _even_post_fwd(ro, proj, gn, da)
    x1 = _matmul(mixed, W["hyb_w_out"], dims="nn", **_T, outs=[F32], epilogue=_add, extras=[(x, "mn")], name="hyb_out")
    rest, bias = rest_of(x1)
    W = {**W, **rest}
    x2, mlp0 = _mlp_fwd(x1, P["norm_mlp"][0], W["packed"], 0, "0")

    h2 = _rms_fwd(x2, P["norm_mix"][1], "rms_mix_fwd1")
    proj2 = _matmul(h2, W["swa_w_qkv"], dims="nn", tm=1024, tn=384, tk=1024, outs=[F32], b_cs=True,
                    epilogue=_add, extras=[(bias.reshape(1, 1536), "n")], name="swa_qkv")
    sq, sk, sv = _swa_pre_fwd(proj2, tab, sqg, skg)
    swa = [(SWA_DIST, 1)]
    so, slse, so_b = _band_fwd(sq, sk, sv, patterns=swa, nq=2, name="swa_fwd", sinks=sink_b, want_bf16=True)
    x3 = _matmul(so_b, W["swa_w_out"], dims="nn", **_T, outs=[F32], epilogue=_add, extras=[(x2, "mn")], name="swa_out")
    y, mlp1 = _mlp_fwd(x3, P["norm_mlp"][1], W["packed"], 1, "1")
    loss, dy, dyb = _loss_head(y, target)

    gw, gp = {}, {}
    dx3, dx3b, dg_mlp1, gw["mlp_w_up1"], gw["mlp_w_down1"] = _mlp_bwd(x3, P["norm_mlp"][1], W["packed"], 1, mlp1, dy, dyb, "1")
    dx3b = red.begin("mlp1", {n: (gw[n], 1024) for n in ("mlp_w_up1", "mlp_w_down1")}, dx3b)
    gw["swa_w_out"] = _matmul(so_b, dx3b, dims="tn", **_T, outs=[F32], name="swa_dwout")
    dso, sdelta = _matmul(dx3b, W["swa_w_out"], dims="nt", tm=512, tn=1024, tk=1024, name="swa_do", **_delta_in(so, 0))
    dsq, dsk, dsv, dsink = _band_bwd(sq, sk, sv, slse, sdelta, dso, patterns=swa, nq=2, name="swa_bwd", sinks=sink_b)
    dproj2, gp["swa_b_qkv"], gp["swa_q_gain"], gp["swa_k_gain"] = _swa_pre_bwd(proj2, tab, sqg, skg, dsq, dsk, dsv)
    gp["swa_sinks"] = dsink
    gw["swa_w_qkv"] = _matmul(h2, dproj2, dims="tn", tm=1024, tn=384, tk=1024, outs=[F32], o_cs=N_CHIPS, name="swa_dwqkv")
    dx2, dx2b, dg_mix1 = _matmul(dproj2, W["swa_w_qkv"], dims="nt", tm=512, tn=1024, tk=384, b_cs=True, name="swa_dh",
                                 **_rms_bwd_in(x2, P["norm_mix"][1], dx3))
    dx2b = red.begin("swa", {"swa_w_qkv": (gw["swa_w_qkv"], 1024), "swa_w_out": (gw["swa_w_out"], 256)}, dx2b)
    dx2b = red.advance("mlp1", dx2b, dx2b)

    dx1, dx1b, dg_mlp0, gw["mlp_w_up0"], gw["mlp_w_down0"] = _mlp_bwd(x1, P["norm_mlp"][0], W["packed"], 0, mlp0, dx2, dx2b, "0")
    gw["hyb_w_out"] = _matmul(mixed, dx1b, dims="tn", **_T, outs=[F32], name="hyb_dwout")
    dx1b = red.begin("mlp0", {"mlp_w_up0": (gw["mlp_w_up0"], 1024), "mlp_w_down0": (gw["mlp_w_down0"], 1024),
                              "hyb_w_out": (gw["hyb_w_out"], 256)}, dx1b)
    dx1b = red.advance("swa", dx1b, dx1b)
    red.finish("mlp1", dx1b)
    dmixed, ddelta = _matmul(dx1b, W["hyb_w_out"], dims="nt", tm=512, tn=1024, tk=1024, name="hyb_dmixed", **_delta_in(da, 512))
    dro, drg, gp["ret_gn_gain"] = _even_post_bwd(ro, proj, gn, dmixed)
    drq, drk, drv = _ret_bwd(rq, rk, rv, states, dro)
    ddq, ddk, ddv = _band_bwd(dq, dk, dv, dlse, ddelta, dmixed, patterns=dil, nq=1, name="dil_bwd", do_col0=4)
    ddq = red.advance("mlp0", ddq, ddq)
    red.finish("swa", ddq)
    dproj, gp["dil_q_gain"], gp["dil_k_gain"] = _even_pre_bwd(proj, tab, dqg, dkg, drq, drk, drv, drg, [ddq], [ddk], [ddv])
    gw["hyb_w_in"] = _matmul(h0, dproj, dims="tn", tm=1024, tn=768, tk=1024, outs=[F32], o_cs=N_CHIPS, name="hyb_dwin")
    dproj = red.begin("win", {"hyb_w_in": (gw["hyb_w_in"], 1024)}, dproj)
    grad_x, _, dg_mix0 = _matmul(dproj, W["hyb_w_in"], dims="nt", tm=512, tn=1024, tk=768, b_cs=True, name="hyb_dh",
                                 **_rms_bwd_in(x, P["norm_mix"][0], dx1))
    red.finish("mlp0", grad_x)
    gp["norm_mix"] = jnp.concatenate([dg_mix0, dg_mix1], axis=0)
    gp["norm_mlp"] = jnp.concatenate([dg_mlp0, dg_mlp1], axis=0)
    return loss, grad_x, gp


HBM = pl.BlockSpec(memory_space=pltpu.HBM)


def _place():
    x, y, c = lax.axis_index("x"), lax.axis_index("y"), lax.axis_index("c")
    chips = [(1 - x, y), (x, 1 - y), (1 - x, 1 - y)]
    return x, y, c, chips


def _allgather_shards(buf):
    _, R, Wd = buf.shape
    Rh = R // 2

    def body(b_ref, out_ref, send_sems, recv_sems):
        x, y, c, chips = _place()
        sibling = (x, y, 1 - c)

        def copy(k, chip, core, to):
            block = b_ref.at[2 * chip[0] + chip[1], pl.ds(core * Rh, Rh), :]
            return pltpu.make_async_remote_copy(
                src_ref=block, dst_ref=block, send_sem=send_sems.at[k], recv_sem=recv_sems.at[k],
                device_id=to, device_id_type=MESH)

        first = [copy(k, (x, y), c, (*chip, c)) for k, chip in enumerate(chips)]
        for cp in first:
            cp.start()
        passed = [copy(3 + k, chip, c, sibling) for k, chip in enumerate(chips)]
        for k, chip in enumerate(chips):
            copy(k, chip, c, (x, y, c)).wait_recv()
            passed[k].start()
        for k, chip in enumerate(chips):
            copy(3 + k, chip, 1 - c, (x, y, c)).wait_recv()
        for cp in first + passed:
            cp.wait_send()

    return _pc(
        body, name="allgather_first", in_specs=[HBM], out_specs=HBM,
        out_shape=jax.ShapeDtypeStruct(buf.shape, buf.dtype), input_output_aliases={0: 0},
        scratch_shapes=[pltpu.SemaphoreType.DMA((6,)), pltpu.SemaphoreType.DMA((6,))],
    )(buf)


SEM = pl.BlockSpec(memory_space=pltpu.SEMAPHORE)
EFFECT = pltpu.SideEffectType.DATAFLOW_SIDE_EFFECTING


def _half_block(ref, chip, core):
    rh = ref.shape[1] // 2
    return ref.at[2 * chip[0] + chip[1], pl.ds(core * rh, rh), :]


def _gather_start(buf, ride):
    def body(b_ref, ride_ref, s0, s1, s2, r0, r1, r2, b_out, ride_out):
        x, y, c, chips = _place()
        for chip, s, r in zip(chips, (s0, s1, s2), (r0, r1, r2)):
            mine = _half_block(b_ref, (x, y), c)
            pltpu.make_async_remote_copy(src_ref=mine, dst_ref=mine, send_sem=s, recv_sem=r,
                                         device_id=(*chip, c), device_id_type=MESH).start()

    sem = pltpu.SemaphoreType.DMA(())
    return _pc(
        body, name="allgather_rest_start",
        out_shape=(sem,) * 6 + (pltpu.HBM(buf.shape, buf.dtype), pltpu.HBM(ride.shape, ride.dtype)),
        in_specs=(HBM, HBM), out_specs=(SEM,) * 6 + (HBM, HBM), input_output_aliases={0: 6, 1: 7},
        compiler_params=pltpu.CompilerParams(has_side_effects=EFFECT),
    )(pltpu.with_memory_space_constraint(buf, pltpu.HBM), pltpu.with_memory_space_constraint(ride, pltpu.HBM))


def _gather_wait(buf, sems, after):
    def body(b_ref, s0, s1, s2, r0, r1, r2, after_ref, b_out):
        x, y, c, chips = _place()
        for chip, s, r in zip(chips, (s0, s1, s2), (r0, r1, r2)):
            cp = pltpu.make_async_remote_copy(src_ref=_half_block(b_ref, (x, y), c), dst_ref=_half_block(b_ref, chip, c),
                                              send_sem=s, recv_sem=r, device_id=(*chip, c), device_id_type=MESH)
            cp.wait_send()
            cp.wait_recv()

    return _pc(
        body, name="allgather_rest_wait", out_shape=pltpu.HBM(buf.shape, buf.dtype),
        in_specs=(HBM,) + (SEM,) * 6 + (pl.BlockSpec(memory_space=pl.ANY),), out_specs=HBM, input_output_aliases={0: 0},
        compiler_params=pltpu.CompilerParams(has_side_effects=EFFECT),
    )(buf, *sems, after)


def _gather_handover(buf):
    def body(b_ref, out_ref, send_sems, recv_sems):
        x, y, c, chips = _place()
        cps = []
        for k, chip in enumerate(chips):
            mine = _half_block(b_ref, chip, c)
            cps.append(pltpu.make_async_remote_copy(src_ref=mine, dst_ref=mine, send_sem=send_sems.at[k],
                                                    recv_sem=recv_sems.at[k], device_id=(x, y, 1 - c), device_id_type=MESH))
        for cp in cps:
            cp.start()
        for k, chip in enumerate(chips):
            theirs = _half_block(b_ref, chip, 1 - c)
            pltpu.make_async_remote_copy(src_ref=theirs, dst_ref=theirs, send_sem=send_sems.at[k], recv_sem=recv_sems.at[k],
                                         device_id=(x, y, 1 - c), device_id_type=MESH).wait_recv()
        for cp in cps:
            cp.wait_send()

    return _pc(
        body, name="allgather_rest_handover", in_specs=[HBM], out_specs=HBM,
        out_shape=jax.ShapeDtypeStruct(buf.shape, buf.dtype), input_output_aliases={0: 0},
        scratch_shapes=[pltpu.SemaphoreType.DMA((3,)), pltpu.SemaphoreType.DMA((3,))],
    )(buf)


def _swap_halves(ts):
    nt = len(ts)

    def body(*refs):
        t_refs, l_refs, send_sems, recv_sems = refs[:nt], refs[nt:2 * nt], refs[-2], refs[-1]
        x, y, c, _ = _place()
        cps = []
        for k in range(nt):
            rh = t_refs[k].shape[1] // 2
            cps.append(pltpu.make_async_remote_copy(
                src_ref=t_refs[k].at[:, pl.ds((1 - c) * rh, rh), :], dst_ref=l_refs[k],
                send_sem=send_sems.at[k], recv_sem=recv_sems.at[k], device_id=(x, y, 1 - c), device_id_type=MESH))
        for cp in cps:
            cp.start()
        for cp in cps:
            cp.wait()

    return _pc(
        body, name="grad_swap_halves", in_specs=[HBM] * nt, out_specs=[HBM] * nt,
        out_shape=[jax.ShapeDtypeStruct((t.shape[0], t.shape[1] // 2, t.shape[2]), F32) for t in ts],
        scratch_shapes=[pltpu.SemaphoreType.DMA((nt,)), pltpu.SemaphoreType.DMA((nt,))],
    )(*ts)


def _pair_sum(t, l, place, name):
    _, r, cols = t.shape
    rh = r // 2
    tr = min(rh, 256)
    nr = rh // tr

    def body(pl_ref, t_ref, l_ref, o_ref):
        o_ref[...] = (t_ref[...] + l_ref[...]).astype(BF16)

    other = lambda s, p: s + jnp.where(s >= p[0], 1, 0)
    return _pc(
        body, name=name,
        grid_spec=pltpu.PrefetchScalarGridSpec(
            num_scalar_prefetch=1, grid=(N_CHIPS - 1, nr),
            in_specs=[pl.BlockSpec((None, tr, cols), lambda s, i, p: (other(s, p), p[1] * nr + i, 0)),
                      pl.BlockSpec((None, tr, cols), lambda s, i, p: (other(s, p), i, 0))],
            out_specs=pl.BlockSpec((None, tr, cols), lambda s, i, p: (other(s, p), i, 0))),
        out_shape=jax.ShapeDtypeStruct((N_CHIPS, rh, cols), BF16),
        compiler_params=_params(("parallel", "parallel")),
    )(place, t, l)


def _exchange_chips(ps):
    nt = len(ps)

    def body(*refs):
        p_refs, r_refs, send_sems, recv_sems = refs[:nt], refs[nt:2 * nt], refs[-2], refs[-1]
        x, y, c, chips = _place()
        cps = []
        for t in range(nt):
            for k, chip in enumerate(chips):
                cps.append(pltpu.make_async_remote_copy(
                    src_ref=p_refs[t].at[2 * chip[0] + chip[1]], dst_ref=r_refs[t].at[k],
                    send_sem=send_sems.at[3 * t + k], recv_sem=recv_sems.at[3 * t + k],
                    device_id=(*chip, c), device_id_type=MESH))
        for cp in cps:
            cp.start()
        for cp in cps:
            cp.wait()

    return _pc(
        body, name="grad_exchange_chips", in_specs=[HBM] * nt, out_specs=[HBM] * nt,
        out_shape=[jax.ShapeDtypeStruct((3,) + p.shape[1:], BF16) for p in ps],
        scratch_shapes=[pltpu.SemaphoreType.DMA((3 * nt,)), pltpu.SemaphoreType.DMA((3 * nt,))],
    )(*ps)


def _final_sum(t, l, rcv, place, name, layer=0, layers=1, into=None):
    _, r, cols = t.shape
    rh = r // 2
    tr = min(rh, 256)
    nr = rh // tr

    def body(pl_ref, t_ref, l_ref, r_ref, *rest):
        acc = t_ref[...] + l_ref[...]
        for k in range(3):
            acc = acc + r_ref[k].astype(F32)
        rest[-1][...] = acc

    in_specs = [pl.BlockSpec((None, tr, cols), lambda i, p: (p[0], p[1] * nr + i, 0)),
                pl.BlockSpec((None, tr, cols), lambda i, p: (p[0], i, 0)),
                pl.BlockSpec((3, tr, cols), lambda i, p: (0, i, 0))]
    args = [place, t, l, rcv]
    aliases = {}
    if into is not None:
        in_specs.append(pl.BlockSpec(memory_space=pl.ANY))
        args.append(into)
        aliases = {4: 0}
    return _pc(
        body, name=name,
        grid_spec=pltpu.PrefetchScalarGridSpec(
            num_scalar_prefetch=1, grid=(nr,), in_specs=in_specs,
            out_specs=pl.BlockSpec((tr, cols), lambda i, p: (2 * nr * layer + p[1] * nr + i, 0))),
        out_shape=jax.ShapeDtypeStruct((layers * r, cols), F32), input_output_aliases=aliases,
        compiler_params=_params(("parallel",)),
    )(*args)


def _share_halves(hs, name):
    nt = len(hs)
    n = sum(layers for _, layers in hs)

    def body(*refs):
        h_refs, send_sems, recv_sems = refs[:nt], refs[-2], refs[-1]
        x, y, c, _ = _place()
        cps = []
        for k, (_, layers) in enumerate(hs):
            rh = h_refs[k].shape[0] // (2 * layers)
            for layer in range(layers):
                half = h_refs[k].at[pl.ds((2 * layer + c) * rh, rh), :]
                cps.append(pltpu.make_async_remote_copy(
                    src_ref=half, dst_ref=half, send_sem=send_sems.at[len(cps)], recv_sem=recv_sems.at[len(cps)],
                    device_id=(x, y, 1 - c), device_id_type=MESH))
        for cp in cps:
            cp.start()
        for cp in cps:
            cp.wait()

    return _pc(
        body, name=name, in_specs=[HBM] * nt, out_specs=[HBM] * nt,
        out_shape=[jax.ShapeDtypeStruct(h.shape, F32) for h, _ in hs],
        input_output_aliases={k: k for k in range(nt)},
        scratch_shapes=[pltpu.SemaphoreType.DMA((n,)), pltpu.SemaphoreType.DMA((n,))],
    )(*[h for h, _ in hs])


def _split_start(name, bufs, ride, n, copies_of):
    nb = len(bufs)

    def body(*refs):
        sems = refs[nb + 1:nb + 1 + 2 * n]
        for cp in copies_of(refs[:nb], sems[:n], sems[n:]):
            cp.start()

    outs = _pc(
        body, name=name,
        out_shape=(pltpu.SemaphoreType.DMA(()),) * (2 * n) + tuple(pltpu.HBM(b.shape, b.dtype) for b in bufs)
        + (pltpu.HBM(ride.shape, ride.dtype),),
        in_specs=(HBM,) * (nb + 1), out_specs=(SEM,) * (2 * n) + (HBM,) * (nb + 1),
        input_output_aliases={k: 2 * n + k for k in range(nb + 1)},
        compiler_params=pltpu.CompilerParams(has_side_effects=EFFECT),
    )(*[pltpu.with_memory_space_constraint(b, pltpu.HBM) for b in bufs], pltpu.with_memory_space_constraint(ride, pltpu.HBM))
    return list(outs[:2 * n]), list(outs[2 * n:2 * n + nb]), outs[-1]


def _split_wait(name, bufs, sems, after, n, copies_of):
    nb = len(bufs)

    def body(*refs):
        s = refs[nb:nb + 2 * n]
        for cp in copies_of(refs[:nb], s[:n], s[n:]):
            cp.wait_send()
            cp.wait_recv()

    outs = _pc(
        body, name=name, out_shape=tuple(pltpu.HBM(b.shape, b.dtype) for b in bufs),
        in_specs=(HBM,) * nb + (SEM,) * (2 * n) + (pl.BlockSpec(memory_space=pl.ANY),), out_specs=(HBM,) * nb,
        input_output_aliases={k: k for k in range(nb)},
        compiler_params=pltpu.CompilerParams(has_side_effects=EFFECT),
    )(*bufs, *sems, after)
    return list(outs)


def _swap_copies(nt):
    def copies_of(refs, send, recv):
        x, y, c, _ = _place()
        cps = []
        for k in range(nt):
            rh = refs[k].shape[1] // 2
            cps.append(pltpu.make_async_remote_copy(
                src_ref=refs[k].at[:, pl.ds((1 - c) * rh, rh), :], dst_ref=refs[nt + k],
                send_sem=send[k], recv_sem=recv[k], device_id=(x, y, 1 - c), device_id_type=MESH))
        return cps
    return copies_of


def _exchange_copies(nt):
    def copies_of(refs, send, recv):
        x, y, c, chips = _place()
        cps = []
        for t in range(nt):
            for k, chip in enumerate(chips):
                cps.append(pltpu.make_async_remote_copy(
                    src_ref=refs[t].at[2 * chip[0] + chip[1]], dst_ref=refs[nt + t].at[k],
                    send_sem=send[3 * t + k], recv_sem=recv[3 * t + k], device_id=(*chip, c), device_id_type=MESH))
        return cps
    return copies_of


class _StagedReduce:
    def __init__(self, place):
        self.place = place
        self.groups = {}
        self.halves = {}

    @staticmethod
    def slab(t, r):
        return t.reshape(N_CHIPS, r, t.size // (N_CHIPS * r))

    def begin(self, g, grads, ride):
        names = list(grads)
        ts = [self.slab(t, r) for t, r in grads.values()]
        lands = [lax.empty((N_CHIPS, t.shape[1] // 2, t.shape[2]), F32) for t in ts]
        sems, bufs, ride = _split_start(f"grad_swap_start_{g}", ts + lands, ride, len(ts), _swap_copies(len(ts)))
        self.groups[g] = dict(names=names, bufs=bufs, sems=sems)
        return ride

    def advance(self, g, after, ride):
        st = self.groups[g]
        nt = len(st["names"])
        bufs = _split_wait(f"grad_swap_wait_{g}", st["bufs"], st["sems"], after, nt, _swap_copies(nt))
        st["ts"], st["ls"] = bufs[:nt], bufs[nt:]
        ps = [_pair_sum(t, l, self.place, f"pair_sum_{n}") for t, l, n in zip(st["ts"], st["ls"], st["names"])]
        lands = [lax.empty((3,) + p.shape[1:], BF16) for p in ps]
        st["sems"], st["bufs"], ride = _split_start(f"grad_exchange_start_{g}", ps + lands, ride, 3 * nt, _exchange_copies(nt))
        return ride

    def finish(self, g, after):
        st = self.groups[g]
        nt = len(st["names"])
        bufs = _split_wait(f"grad_exchange_wait_{g}", st["bufs"], st["sems"], after, 3 * nt, _exchange_copies(nt))
        for t, l, r, n in zip(st["ts"], st["ls"], bufs[nt:], st["names"]):
            if n[-1] in "01":
                self.halves[n[:-1]] = _final_sum(t, l, r, self.place, f"final_sum_{n}", layer=int(n[-1]), layers=2,
                                                 into=self.halves.get(n[:-1]))
            else:
                self.halves[n] = _final_sum(t, l, r, self.place, f"final_sum_{n}")


def _allgather_small(v):
    rows = v.shape[0]

    def body(v_ref, out_ref, send_sems, recv_sems):
        x, y, c, _ = _place()
        me = 4 * x + 2 * y + c
        out_ref[me] = v_ref[...]
        cps = []
        for k in range(1, 8):
            fx, fy, fc = (k >> 2) & 1, (k >> 1) & 1, k & 1
            to = (1 - x if fx else x, 1 - y if fy else y, 1 - c if fc else c)
            cps.append(pltpu.make_async_remote_copy(
                src_ref=v_ref, dst_ref=out_ref.at[me], send_sem=send_sems.at[k - 1], recv_sem=recv_sems.at[k - 1],
                device_id=to, device_id_type=MESH))
        for cp in cps:
            cp.start()
        for cp in cps:
            cp.wait()

    return _pc(
        body, name="allgather_small",
        in_specs=[pl.BlockSpec(memory_space=pltpu.VMEM)], out_specs=pl.BlockSpec(memory_space=pltpu.VMEM),
        out_shape=jax.ShapeDtypeStruct((8, rows, LANES), F32),
        scratch_shapes=[pltpu.SemaphoreType.DMA((7,)), pltpu.SemaphoreType.DMA((7,))],
    )(v)


def _adamw_math(w, g, m, v):
    m = ADAM_B1 * m + (1.0 - ADAM_B1) * g
    v = ADAM_B2 * v + (1.0 - ADAM_B2) * (g * g)
    m_hat = m / (1.0 - ADAM_B1 ** ADAM_STEP)
    v_hat = v / (1.0 - ADAM_B2 ** ADAM_STEP)
    return -ADAM_LR * (m_hat / (jnp.sqrt(v_hat) + ADAM_EPS) + ADAM_WD * w), m, v


def _adamw(w, g, m, v, name):
    r, cols = w.shape
    tr = min(r, 256)

    def body(w_ref, g_ref, m_ref, v_ref, d_ref, mo_ref, vo_ref):
        d, mn, vn = _adamw_math(w_ref[...], g_ref[...], m_ref[...], v_ref[...])
        d_ref[...] = d
        mo_ref[...] = mn
        vo_ref[...] = vn

    row = pl.BlockSpec((tr, cols), lambda i: (i, 0))
    return _pc(
        body, name=name, grid=(r // tr,), in_specs=[row] * 4, out_specs=[row] * 3,
        out_shape=[jax.ShapeDtypeStruct((r, cols), F32)] * 3,
        compiler_params=_params(("parallel",)),
    )(w, g, m, v)


def _adamw_small(w, gathered, m, v):
    rows = w.shape[0]

    def body(w_ref, g_ref, m_ref, v_ref, go_ref, d_ref, mo_ref, vo_ref):
        g = g_ref[0]
        for k in range(1, 8):
            g = g + g_ref[k]
        d, mn, vn = _adamw_math(w_ref[...], g, m_ref[...], v_ref[...])
        go_ref[...] = g
        d_ref[...] = d
        mo_ref[...] = mn
        vo_ref[...] = vn

    return _pc(
        body, name="adamw_small",
        out_shape=[jax.ShapeDtypeStruct((rows, LANES), F32)] * 4,
    )(w, gathered, m, v)


_BIAS_ROWS = 32


def _own_slot(flat, chip):
    return lax.dynamic_update_slice(lax.empty((N_CHIPS,) + flat.shape, flat.dtype), flat[None], (chip, 0, 0))


def _pack_first(hyb_w_in, hyb_w_out):
    return jnp.concatenate([t.astype(BF16).reshape(-1, 1024) for t in (hyb_w_in, hyb_w_out)], axis=0)


def _unpack_first(g):
    return {"hyb_w_in": g[:, 0:768, :].reshape(N_CHIPS, 1024, 768), "hyb_w_out": g[:, 768:1024, :].reshape(1024, 1024)}


def _pack_rest(mlp_w_up, mlp_w_down, swa_w_qkv, swa_w_out, swa_b_qkv):
    parts = [t.astype(BF16).reshape(-1, 1024) for t in (mlp_w_up, mlp_w_down, swa_w_qkv, swa_w_out)]
    bias = lax.bitcast_convert_type(swa_b_qkv.reshape(384), BF16).reshape(1, 768)
    bias = jnp.pad(bias, ((0, _BIAS_ROWS - 1), (0, 256)))
    return jnp.concatenate(parts + [bias], axis=0)


def _unpack_rest(g):
    W = {
        "packed": g,
        "swa_w_qkv": g[:, 4096:4480, :].reshape(N_CHIPS, 1024, 384),
        "swa_w_out": g[:, 4480:4736, :].reshape(1024, 1024),
    }
    bias = lax.bitcast_convert_type(g[:, 4736, :768].reshape(N_CHIPS, 384, 2), F32).reshape(1536)
    return W, bias


_SMALL = (("norm_mix", 16), ("norm_mlp", 16), ("ret_gn_gain", 4), ("dil_q_gain", 1), ("dil_k_gain", 1),
          ("swa_b_qkv", 12), ("swa_q_gain", 1), ("swa_k_gain", 1), ("swa_sinks", 1))
_SUBLANES = 8


def _slot(r):
    return -(-r // _SUBLANES) * _SUBLANES


def _pack_small(d):
    return jnp.concatenate([jnp.pad(d[n].reshape(r, LANES), ((0, _slot(r) - r), (0, 0))) for n, r in _SMALL], axis=0)


def _unpack_small(p):
    out, o = {}, 0
    for n, r in _SMALL:
        out[n] = p[o:o + r]
        o += _slot(r)
    return out


def kernel(x, positions, norm_mix, norm_mlp, mlp_w_up, mlp_w_down, hyb_w_in, hyb_w_out, ret_gn_gain, dil_q_gain, dil_k_gain, swa_w_qkv, swa_b_qkv, swa_w_out, swa_q_gain, swa_k_gain, swa_sinks, loss_target, m_norm_mix, m_norm_mlp, m_mlp_w_up, m_mlp_w_down, m_hyb_w_in, m_hyb_w_out, m_ret_gn_gain, m_dil_q_gain, m_dil_k_gain, m_swa_w_qkv, m_swa_b_qkv, m_swa_w_out, m_swa_q_gain, m_swa_k_gain, m_swa_sinks, v_norm_mix, v_norm_mlp, v_mlp_w_up, v_mlp_w_down, v_hyb_w_in, v_hyb_w_out, v_ret_gn_gain, v_dil_q_gain, v_dil_k_gain, v_swa_w_qkv, v_swa_b_qkv, v_swa_w_out, v_swa_q_gain, v_swa_k_gain, v_swa_sinks):
    ax, ay, ac = lax.axis_index("x"), lax.axis_index("y"), lax.axis_index("c")
    chip = 2 * ax + ay
    place = jnp.stack([chip, ac]).astype(jnp.int32)
    S = x.shape[1]

    first = _allgather_shards(_own_slot(_pack_first(hyb_w_in[0], hyb_w_out[0]), chip))
    rest = _own_slot(_pack_rest(mlp_w_up, mlp_w_down, swa_w_qkv[0], swa_w_out[0], swa_b_qkv[0]), chip)
    *sems, rest, first = _gather_start(rest, first)

    def rest_of(after):
        return _unpack_rest(_gather_handover(_gather_wait(rest, sems, after)))

    P = dict(norm_mix=norm_mix, norm_mlp=norm_mlp, ret_gn_gain=ret_gn_gain, dil_q_gain=dil_q_gain, dil_k_gain=dil_k_gain,
             swa_q_gain=swa_q_gain, swa_k_gain=swa_k_gain, swa_sinks=swa_sinks)

    red = _StagedReduce(place)
    loss_l, grad_x, gp = _local_step(x[0], positions.reshape(S, 1), loss_target[0], _unpack_first(first), rest_of, P, red)
    loss = lax.psum(loss_l[0, 0], ("x", "y", "c"))

    params = dict(mlp_w_up=(mlp_w_up, m_mlp_w_up, v_mlp_w_up), mlp_w_down=(mlp_w_down, m_mlp_w_down, v_mlp_w_down),
                  hyb_w_in=(hyb_w_in, m_hyb_w_in, v_hyb_w_in), hyb_w_out=(hyb_w_out, m_hyb_w_out, v_hyb_w_out),
                  swa_w_qkv=(swa_w_qkv, m_swa_w_qkv, v_swa_w_qkv), swa_w_out=(swa_w_out, m_swa_w_out, v_swa_w_out))
    big = {}

    def update(names, share_name):
        hs = [(red.halves[n], params[n][0].shape[0]) for n in names]
        for n, g in zip(names, _share_halves(hs, share_name)):
            rows = g.shape[0]
            w, m, v = (t.reshape(rows, -1) for t in params[n])
            big[n] = [t.reshape(params[n][0].shape) for t in (g,) + tuple(_adamw(w, g, m, v, f"adamw_{n}"))]

    names = ["mlp_w_up", "mlp_w_down", "hyb_w_out", "swa_w_qkv", "swa_w_out"]
    red.halves[names[0]] = red.advance("win", grad_x, red.halves[names[0]])
    update(names, "grad_share_halves")
    red.finish("win", big[names[-1]][1])
    update(["hyb_w_in"], "grad_share_last")

    gsm = dict(gp)
    gsm["swa_sinks"] = jnp.pad(gp["swa_sinks"].reshape(16, HEAD)[:, 0], (0, LANES - 16))
    gathered = _allgather_small(_pack_small(gsm))

    def small_pack(norm_mix, norm_mlp, gn, dq, dk, b, sq, sk, sinks):
        dup = lambda t: jnp.tile(t.reshape(1, HEAD), (1, 2))
        bias = lax.dynamic_update_slice(jnp.zeros((12, LANES), F32), b.reshape(3, LANES), (3 * chip, 0))
        return _pack_small(dict(norm_mix=norm_mix, norm_mlp=norm_mlp, ret_gn_gain=gn, dil_q_gain=dup(dq), dil_k_gain=dup(dk),
                                swa_b_qkv=bias, swa_q_gain=dup(sq), swa_k_gain=dup(sk),
                                swa_sinks=jnp.pad(sinks.reshape(16), (0, LANES - 16))))

    pw = small_pack(norm_mix, norm_mlp, ret_gn_gain, dil_q_gain, dil_k_gain, swa_b_qkv, swa_q_gain, swa_k_gain, swa_sinks)
    pm = small_pack(m_norm_mix, m_norm_mlp, m_ret_gn_gain, m_dil_q_gain, m_dil_k_gain, m_swa_b_qkv, m_swa_q_gain, m_swa_k_gain, m_swa_sinks)
    pv = small_pack(v_norm_mix, v_norm_mlp, v_ret_gn_gain, v_dil_q_gain, v_dil_k_gain, v_swa_b_qkv, v_swa_q_gain, v_swa_k_gain, v_swa_sinks)
    small = [_unpack_small(t) for t in _adamw_small(pw, gathered, pm, pv)]

    def small_out(n, k):
        t = small[k][n]
        if n in ("norm_mix", "norm_mlp"):
            return t.reshape(2, D_MODEL)
        if n == "ret_gn_gain":
            return t.reshape(1, RET_HEADS, 128)
        if n == "swa_b_qkv":
            return lax.dynamic_slice(t, (3 * chip, 0), (3, LANES)).reshape(1, 384)
        if n == "swa_sinks":
            return t[0, :16].reshape(1, 16)
        return t[0, :HEAD].reshape(1, HEAD)

    order = ["norm_mix", "norm_mlp", "mlp_w_up", "mlp_w_down", "hyb_w_in", "hyb_w_out", "ret_gn_gain", "dil_q_gain",
             "dil_k_gain", "swa_w_qkv", "swa_b_qkv", "swa_w_out", "swa_q_gain", "swa_k_gain", "swa_sinks"]
    is_big = {"mlp_w_up", "mlp_w_down", "hyb_w_in", "hyb_w_out", "swa_w_qkv", "swa_w_out"}
    outs = [loss, grad_x[None]]
    for k in range(4):
        outs += [big[n][k] if n in is_big else small_out(n, k) for n in order]
    return tuple(outs)
```

```python
import functools
import math

import numpy as np
import jax
import jax.numpy as jnp
from jax import lax
from jax.experimental import pallas as pl
from jax.experimental.pallas import tpu as pltpu

F32, BF16 = jnp.float32, jnp.bfloat16
HIGHEST = lax.Precision.HIGHEST
MESH = pl.DeviceIdType.MESH

LANES = 128
VMEM_LIMIT = 48 << 20
D_MODEL = 1024
D_FF = 4096
HEAD = 64
EPS = 1e-6
BLK = 128
RET_HEADS = 4
RET_THETA = 10000.0
ROPE_THETA = 500000.0
ROPE_DIMS = 16
DIL_PATTERNS = ((128, 1), (512, 4), (2048, 16))
SWA_DIST = 127
N_CHIPS = 4
ADAM_LR, ADAM_B1, ADAM_B2, ADAM_EPS, ADAM_WD, ADAM_STEP = 0.001, 0.9, 0.999, 1e-08, 0.01, 10

_LOG_GAMMA = [float(np.log1p(-np.exp2(np.float32(-5.0 - h)))) for h in range(RET_HEADS)]


def _pc(body, **kw):
    return pl.pallas_call(body, **kw)


def _params(sem):
    return pltpu.CompilerParams(dimension_semantics=sem, vmem_limit_bytes=VMEM_LIMIT)


def _matmul(a, b, *, dims, tm, tn, tk, outs, name, epilogue=None, extras=(), b_cs=False, b_rs=0, b_row0=0, b_rows=0,
            o_cs=0, a_pro=None):
    if dims == "nn":
        M, K = a.shape
        N = b.shape[0] * b.shape[2] if b_cs else b.shape[1]
        a_spec = pl.BlockSpec((tm, tk), lambda i, j, k: (i, k))
        if b_cs:
            npt = b.shape[2] // tn
            b_spec = pl.BlockSpec((None, tk, tn), lambda i, j, k: (j // npt, k + b_row0, j % npt))
        elif b_rs:
            K, N, kps = b.shape[0] * b_rs, b.shape[2], b_rs // tk
            b_spec = pl.BlockSpec((None, tk, tn), lambda i, j, k: (k // kps, b_row0 + k % kps, j))
        else:
            b_spec = pl.BlockSpec((tk, tn), lambda i, j, k: (k, j))
        contract = (((1,), (0,)), ((), ()))
    elif dims == "nt":
        M, K = a.shape
        N = (b_rows or b.shape[1]) if b_cs else b.shape[0]
        a_spec = pl.BlockSpec((tm, tk), lambda i, j, k: (i, k))
        if b_cs:
            kpt = b.shape[2] // tk
            b_spec = pl.BlockSpec((None, tn, tk), lambda i, j, k: (k // kpt, j + b_row0, k % kpt))
        elif b_rs:
            N, jps = b.shape[0] * b_rs, b_rs // tn
            b_spec = pl.BlockSpec((None, tn, tk), lambda i, j, k: (j // jps, b_row0 + j % jps, k))
        else:
            b_spec = pl.BlockSpec((tn, tk), lambda i, j, k: (j, k))
        contract = (((1,), (1,)), ((), ()))
    else:
        K, M = a.shape
        N = b.shape[1]
        a_spec = pl.BlockSpec((tk, tm), lambda i, j, k: (k, i))
        b_spec = pl.BlockSpec((tk, tn), lambda i, j, k: (k, j))
        contract = (((0,), (0,)), ((), ()))
    assert M % tm == 0 and N % tn == 0 and K % tk == 0, (name, M, N, K, tm, tn, tk)
    nk = K // tk
    ex_specs = []
    for arr, kind in extras:
        if kind == "mn":
            ex_specs.append(pl.BlockSpec((tm, tn), lambda i, j, k: (i, j)))
        elif kind == "n":
            ex_specs.append(pl.BlockSpec((1, tn), lambda i, j, k: (0, j)))
        elif kind == "full":
            ex_specs.append(pl.BlockSpec(arr.shape, lambda i, j, k, nd=arr.ndim: (0,) * nd))
        else:
            ex_specs.append(pl.BlockSpec((tm, kind), lambda i, j, k: (i, 0)))
    if o_cs:
        n_sh = N // o_cs
        opt = n_sh // tn
        o_shape = (o_cs, M, n_sh)
        o_spec = pl.BlockSpec((None, tm, tn), lambda i, j, k: (j // opt, i, j % opt))
    else:
        o_shape = (M, N)
        o_spec = pl.BlockSpec((tm, tn), lambda i, j, k: (i, j))
    o_specs, o_shapes, summed = [], [], []
    for o in outs:
        if isinstance(o, tuple) and o[0] == "colsum":
            assert N == tn
            o_specs.append(pl.BlockSpec((1, tn), lambda i, j, k: (0, j)))
            o_shapes.append(jax.ShapeDtypeStruct((1, N), F32))
            summed.append(True)
        elif isinstance(o, tuple):
            o_specs.append(pl.BlockSpec((tm, o[1]), lambda i, j, k: (i, 0)))
            o_shapes.append(jax.ShapeDtypeStruct((M, o[1]), o[0]))
            summed.append(False)
        else:
            o_specs.append(o_spec)
            o_shapes.append(jax.ShapeDtypeStruct(o_shape, o))
            summed.append(False)
    n_ex, n_out = len(extras), len(outs)
    if epilogue is None:
        epilogue = lambda acc: (acc,)

    def body(a_ref, b_ref, *rest):
        ex, o_refs, acc = rest[:n_ex], rest[n_ex:n_ex + n_out], rest[-1]
        i, k = pl.program_id(0), pl.program_id(2)

        @pl.when(k == 0)
        def _():
            acc[...] = jnp.zeros_like(acc)

        av = a_ref[...] if a_pro is None else a_pro(a_ref[...])
        acc[...] += lax.dot_general(av.astype(BF16), b_ref[...].astype(BF16), contract, preferred_element_type=F32)

        @pl.when(k == nk - 1)
        def _():
            vals = epilogue(acc[...], *[e[...] for e in ex])
            for r, v, sm in zip(o_refs, vals, summed):
                if sm:
                    @pl.when(i == 0)
                    def _(r=r):
                        r[...] = jnp.zeros_like(r)

                    r[...] += v
                else:
                    r[...] = v.astype(r.dtype)

    res = _pc(
        body, name=name, grid=(M // tm, N // tn, nk),
        in_specs=[a_spec, b_spec] + ex_specs, out_specs=o_specs, out_shape=o_shapes,
        scratch_shapes=[pltpu.VMEM((tm, tn), F32)],
        compiler_params=_params(("arbitrary" if any(summed) else "parallel", "parallel", "arbitrary")),
    )(a, b, *[e for e, _ in extras])
    return res[0] if n_out == 1 else res


def _roll(x, s):
    return pltpu.roll(x, s % LANES, 1)


def _rope(x, A, B, C, half):
    return x * A + _roll(x, LANES - half) * B + _roll(x, half) * C


def _rope_t(g, A, B, C, half):
    return g * A + _roll(g * B, half) + _roll(g * C, LANES - half)


def _gmean(x, G):
    hi = x.astype(BF16)
    lo = (x - hi.astype(F32)).astype(BF16)
    Gb = G.astype(BF16)
    return jnp.dot(hi, Gb, preferred_element_type=F32) + jnp.dot(lo, Gb, preferred_element_type=F32)


def _head_mask(shape, half):
    lane = lax.broadcasted_iota(jnp.int32, shape, len(shape) - 1)
    return (lane >= HEAD) if half else (lane < HEAD)


def _group_matrix():
    i = np.arange(LANES)
    return jnp.asarray((i[:, None] // HEAD == i[None, :] // HEAD).astype(np.float32) / HEAD)


def _rope_inv():
    l = np.arange(LANES) % HEAD
    inv_r = np.power(np.float32(RET_THETA), -(l % 32).astype(np.float32) * np.float32(2.0 / HEAD))
    hp = ROPE_DIMS // 2
    inv_p = np.power(np.float32(ROPE_THETA), -(l % hp).astype(np.float32) * np.float32(2.0 / ROPE_DIMS))
    inv_p = np.where(l < ROPE_DIMS, inv_p, 0.0)
    return jnp.asarray(np.stack([inv_r, inv_p]).astype(np.float32))


def _tables(pos_col):
    S = pos_col.shape[0]
    tm = 512
    hp = ROPE_DIMS // 2

    def body(p_ref, inv_ref, o_ref):
        p = p_ref[...].astype(F32)
        lane = lax.broadcasted_iota(jnp.int32, (tm, LANES), 1) % HEAD
        ang = p * inv_ref[0:1, :]
        c, s = jnp.cos(ang), jnp.sin(ang)
        o_ref[:, 0:128] = c
        o_ref[:, 128:256] = jnp.where(lane < 32, -s, 0.0)
        o_ref[:, 256:384] = jnp.where(lane >= 32, s, 0.0)
        ang = p * inv_ref[1:2, :]
        c, s = jnp.cos(ang), jnp.sin(ang)
        o_ref[:, 384:512] = c
        o_ref[:, 512:640] = jnp.where(lane < hp, -s, 0.0)
        o_ref[:, 640:768] = jnp.where((lane >= hp) & (lane < ROPE_DIMS), s, 0.0)

    return _pc(
        body, name="rope_tables", grid=(S // tm,),
        in_specs=[pl.BlockSpec((tm, 1), lambda i: (i, 0)), pl.BlockSpec((2, LANES), lambda i: (0, 0))],
        out_specs=pl.BlockSpec((tm, 768), lambda i: (i, 0)),
        out_shape=jax.ShapeDtypeStruct((S, 768), F32),
        compiler_params=_params(("parallel",)),
    )(pos_col, _rope_inv())


def _tab(tab_ref, which):
    o = 384 * which
    return tab_ref[:, o:o + 128], tab_ref[:, o + 128:o + 256], tab_ref[:, o + 256:o + 384]


def _rms_fwd(x, g, name):
    S, Dm = x.shape
    tm = 512

    def body(x_ref, g_ref, h_ref):
        xv = x_ref[...]
        r = lax.rsqrt(jnp.mean(xv * xv, axis=-1, keepdims=True) + EPS)
        h_ref[...] = (xv * r * g_ref[...]).astype(BF16)

    return _pc(
        body, name=name, grid=(S // tm,),
        in_specs=[pl.BlockSpec((tm, Dm), lambda i: (i, 0)), pl.BlockSpec((1, Dm), lambda i: (0, 0))],
        out_specs=pl.BlockSpec((tm, Dm), lambda i: (i, 0)),
        out_shape=jax.ShapeDtypeStruct((S, Dm), BF16),
        compiler_params=_params(("parallel",)),
    )(x, g.reshape(1, Dm))


def _rms_bwd(x, g, dh, dres, name):
    S, Dm = x.shape
    tm = 512

    def body(x_ref, g_ref, dh_ref, dres_ref, dx_ref, dxb_ref, dg_ref):
        xv, dhv = x_ref[...], dh_ref[...]
        r = lax.rsqrt(jnp.mean(xv * xv, axis=-1, keepdims=True) + EPS)
        t = dhv * g_ref[...]
        dx = dres_ref[...] + r * t - xv * (r * r * r) * jnp.mean(xv * t, axis=-1, keepdims=True)
        dx_ref[...] = dx
        dxb_ref[...] = dx.astype(BF16)

        @pl.when(pl.program_id(0) == 0)
        def _():
            dg_ref[...] = jnp.zeros_like(dg_ref)

        dg_ref[...] += jnp.sum(dhv * xv * r, axis=0, keepdims=True)

    row = pl.BlockSpec((tm, Dm), lambda i: (i, 0))
    vec = pl.BlockSpec((1, Dm), lambda i: (0, 0))
    return _pc(
        body, name=name, grid=(S // tm,),
        in_specs=[row, vec, row, row], out_specs=[row, row, vec],
        out_shape=[jax.ShapeDtypeStruct((S, Dm), F32), jax.ShapeDtypeStruct((S, Dm), BF16),
                   jax.ShapeDtypeStruct((1, Dm), F32)],
        compiler_params=_params(("arbitrary",)),
    )(x, g.reshape(1, Dm), dh, dres)


def _hn_fwd(x, gain, G):
    r = lax.rsqrt(_gmean(x * x, G) + EPS)
    return x * r * gain


def _hn_bwd(x, gain, dy, G):
    r = lax.rsqrt(_gmean(x * x, G) + EPS)
    t = dy * gain
    dx = r * t - x * (r * r * r) * _gmean(x * t, G)
    return dx, jnp.sum(dy * x * r, axis=0, keepdims=True)


def _fold_halves(v):
    return v + _roll(v, HEAD)


def _even_pre_fwd(proj, tab, qg, kg):
    S = proj.shape[0]
    tm = 256

    def body(p_ref, tab_ref, qg_ref, kg_ref, g_ref, rq_ref, rk_ref, rv_ref, dq_ref, dk_ref, dv_ref):
        Ar, Br, Cr = _tab(tab_ref, 0)
        Ap, Bp, Cp = _tab(tab_ref, 1)
        G = g_ref[...]
        for c in range(2):
            sl = slice(c * 128, (c + 1) * 128)
            rq_ref[:, sl] = _rope(p_ref[:, c * 128:(c + 1) * 128], Ar, Br, Cr, 32).astype(BF16)
            rk_ref[:, sl] = (_rope(p_ref[:, 256 + c * 128:256 + (c + 1) * 128], Ar, Br, Cr, 32) * 0.125).astype(BF16)
        rv_ref[...] = p_ref[:, 512:1024].astype(BF16)
        for c in range(4):
            sl = slice(c * 128, (c + 1) * 128)
            q = _hn_fwd(p_ref[:, 1536 + c * 128:1536 + (c + 1) * 128], qg_ref[...], G)
            dq_ref[:, sl] = _rope(q, Ap, Bp, Cp, 8).astype(BF16)
            k = _hn_fwd(p_ref[:, 2048 + c * 128:2048 + (c + 1) * 128], kg_ref[...], G)
            dk_ref[:, sl] = _rope(k, Ap, Bp, Cp, 8).astype(BF16)
        dv_ref[...] = p_ref[:, 2560:3072].astype(BF16)

    row = lambda w: pl.BlockSpec((tm, w), lambda i: (i, 0))
    vec = pl.BlockSpec((1, LANES), lambda i: (0, 0))
    return _pc(
        body, name="even_pre_fwd", grid=(S // tm,),
        in_specs=[row(3072), row(768), vec, vec, pl.BlockSpec((LANES, LANES), lambda i: (0, 0))],
        out_specs=[row(256), row(256), row(512), row(512), row(512), row(512)],
        out_shape=[jax.ShapeDtypeStruct((S, w), BF16) for w in (256, 256, 512, 512, 512, 512)],
        compiler_params=_params(("parallel",)),
    )(proj, tab, qg, kg, _group_matrix())


def _even_pre_bwd(proj, tab, qg, kg, drq, drk, drv, drg, dqs, dks, dvs):
    S = proj.shape[0]
    tm = 256
    npat = len(dqs)

    def body(p_ref, tab_ref, qg_ref, kg_ref, g_ref, drq_ref, drk_ref, drv_ref, drg_ref, *rest):
        dq_refs, dk_refs, dv_refs = rest[:npat], rest[npat:2 * npat], rest[2 * npat:3 * npat]
        dp_ref, dqg_ref, dkg_ref = rest[3 * npat:]
        Ar, Br, Cr = _tab(tab_ref, 0)
        Ap, Bp, Cp = _tab(tab_ref, 1)
        G = g_ref[...]
        for c in range(2):
            sl = slice(c * 128, (c + 1) * 128)
            dp_ref[:, c * 128:(c + 1) * 128] = _rope_t(drq_ref[:, sl], Ar, Br, Cr, 32).astype(BF16)
            dp_ref[:, 256 + c * 128:256 + (c + 1) * 128] = _rope_t(drk_ref[:, sl] * 0.125, Ar, Br, Cr, 32).astype(BF16)
        dp_ref[:, 512:1024] = drv_ref[...].astype(BF16)
        dp_ref[:, 1024:1536] = drg_ref[...].astype(BF16)
        accq = jnp.zeros((1, LANES), F32)
        acck = jnp.zeros((1, LANES), F32)
        for c in range(4):
            sl = slice(c * 128, (c + 1) * 128)
            g = dq_refs[0][:, sl]
            for r in dq_refs[1:]:
                g = g + r[:, sl]
            dx, dg = _hn_bwd(p_ref[:, 1536 + c * 128:1536 + (c + 1) * 128], qg_ref[...], _rope_t(g, Ap, Bp, Cp, 8), G)
            dp_ref[:, 1536 + c * 128:1536 + (c + 1) * 128] = dx.astype(BF16)
            accq = accq + dg
            g = dk_refs[0][:, sl]
            for r in dk_refs[1:]:
                g = g + r[:, sl]
            dx, dg = _hn_bwd(p_ref[:, 2048 + c * 128:2048 + (c + 1) * 128], kg_ref[...], _rope_t(g, Ap, Bp, Cp, 8), G)
            dp_ref[:, 2048 + c * 128:2048 + (c + 1) * 128] = dx.astype(BF16)
            acck = acck + dg
        g = dv_refs[0][...]
        for r in dv_refs[1:]:
            g = g + r[...]
        dp_ref[:, 2560:3072] = g.astype(BF16)

        @pl.when(pl.program_id(0) == 0)
        def _():
            dqg_ref[...] = jnp.zeros_like(dqg_ref)
            dkg_ref[...] = jnp.zeros_like(dkg_ref)

        dqg_ref[...] += _fold_halves(accq)
        dkg_ref[...] += _fold_halves(acck)

    row = lambda w: pl.BlockSpec((tm, w), lambda i: (i, 0))
    vec = pl.BlockSpec((1, LANES), lambda i: (0, 0))
    return _pc(
        body, name="even_pre_bwd", grid=(S // tm,),
        in_specs=[row(3072), row(768), vec, vec, pl.BlockSpec((LANES, LANES), lambda i: (0, 0)),
                  row(256), row(256), row(512), row(512)] + [row(512)] * (3 * npat),
        out_specs=[row(3072), vec, vec],
        out_shape=[jax.ShapeDtypeStruct((S, 3072), BF16), jax.ShapeDtypeStruct((1, LANES), F32),
                   jax.ShapeDtypeStruct((1, LANES), F32)],
        compiler_params=_params(("arbitrary",)),
    )(proj, tab, qg, kg, _group_matrix(), drq, drk, drv, drg, *dqs, *dks, *dvs)


def _ret_consts(pair, half):
    lg = jnp.where(pair == 0, _LOG_GAMMA[half], _LOG_GAMMA[2 + half]).astype(F32)
    i = lax.broadcasted_iota(jnp.int32, (BLK, BLK), 0)
    j = lax.broadcasted_iota(jnp.int32, (BLK, BLK), 1)
    diff = (i - j).astype(F32)
    decay = jnp.where(diff >= 0, jnp.exp(lg * jnp.maximum(diff, 0.0)), 0.0)
    t = lax.broadcasted_iota(jnp.int32, (BLK, 1), 0).astype(F32)
    xi = jnp.exp(lg * (t + 1.0))
    zeta = jnp.exp(lg * (BLK - 1.0 - t))
    cd = jnp.exp(jnp.full((1, 1), BLK, F32) * lg)
    return decay, xi, zeta, cd


RET_STEP = 4


def _ret_fwd(rq, rk, rv):
    S = rq.shape[0]
    nc = S // BLK
    rows = RET_STEP * BLK

    def body(q_ref, k_ref, v_ref, o_ref, st_ref, R):
        p, n = pl.program_id(0), pl.program_id(1)

        @pl.when(n == 0)
        def _():
            R[...] = jnp.zeros_like(R)

        consts = [_ret_consts(p, half) for half in range(2)]
        masks = [_head_mask((BLK, LANES), half) for half in range(2)]
        for ci in range(RET_STEP):
            rs = slice(ci * BLK, (ci + 1) * BLK)
            q2, k2 = q_ref[rs, :], k_ref[rs, :]
            for half in range(2):
                decay, xi, zeta, cd = consts[half]
                m = masks[half]
                qm = jnp.where(m, q2, jnp.zeros_like(q2))
                km = jnp.where(m, k2, jnp.zeros_like(k2))
                v = v_ref[rs, half * 128:(half + 1) * 128]
                Rb = R[half].astype(BF16)
                st_ref[ci, half] = Rb
                sc = lax.dot_general(qm, k2, (((1,), (1,)), ((), ())), preferred_element_type=F32) * decay
                o = jnp.dot(sc.astype(BF16), v, preferred_element_type=F32)
                o = o + jnp.dot(qm, Rb, preferred_element_type=F32) * xi
                o_ref[rs, half * 128:(half + 1) * 128] = o
                kz = (km.astype(F32) * zeta).astype(BF16)
                R[half] = R[half] * cd + lax.dot_general(kz, v, (((0,), (0,)), ((), ())), preferred_element_type=F32)

    return _pc(
        body, name="ret_fwd", grid=(2, nc // RET_STEP),
        in_specs=[pl.BlockSpec((rows, 128), lambda p, n: (n, p)), pl.BlockSpec((rows, 128), lambda p, n: (n, p)),
                  pl.BlockSpec((rows, 256), lambda p, n: (n, p))],
        out_specs=[pl.BlockSpec((rows, 256), lambda p, n: (n, p)),
                   pl.BlockSpec((None, RET_STEP, 2, 128, 128), lambda p, n: (p, n, 0, 0, 0))],
        out_shape=[jax.ShapeDtypeStruct((S, 512), F32), jax.ShapeDtypeStruct((2, nc, 2, 128, 128), BF16)],
        scratch_shapes=[pltpu.VMEM((2, 128, 128), F32)],
        compiler_params=_params(("parallel", "arbitrary")),
    )(rq, rk, rv)


def _ret_bwd(rq, rk, rv, states, do):
    S = rq.shape[0]
    nc = S // BLK
    ns = nc // RET_STEP
    rows = RET_STEP * BLK
    nt = (((1,), (1,)), ((), ()))
    tn = (((0,), (0,)), ((), ()))

    def body(q_ref, k_ref, v_ref, st_ref, do_ref, dq_ref, dk_ref, dv_ref, U):
        p, n = pl.program_id(0), pl.program_id(1)

        @pl.when(n == 0)
        def _():
            U[...] = jnp.zeros_like(U)

        consts = [_ret_consts(p, half) for half in range(2)]
        masks = [_head_mask((BLK, LANES), half) for half in range(2)]
        for ci in reversed(range(RET_STEP)):
            rs = slice(ci * BLK, (ci + 1) * BLK)
            q2, k2 = q_ref[rs, :], k_ref[rs, :]
            dq_acc = jnp.zeros((BLK, LANES), F32)
            dk_acc = jnp.zeros((BLK, LANES), F32)
            for half in range(2):
                decay, xi, zeta, cd = consts[half]
                m = masks[half]
                qm = jnp.where(m, q2, jnp.zeros_like(q2))
                km = jnp.where(m, k2, jnp.zeros_like(k2))
                v = v_ref[rs, half * 128:(half + 1) * 128]
                dob = do_ref[rs, half * 128:(half + 1) * 128].astype(BF16)
                Rb = st_ref[ci, half]
                Ub = U[half].astype(BF16)
                dsc = (lax.dot_general(dob, v, nt, preferred_element_type=F32) * decay).astype(BF16)
                xdo = (dob.astype(F32) * xi).astype(BF16)
                dq_acc += jnp.dot(dsc, km, preferred_element_type=F32) + lax.dot_general(xdo, Rb, nt, preferred_element_type=F32)
                dk_acc += lax.dot_general(dsc, qm, tn, preferred_element_type=F32) \
                    + lax.dot_general(v, Ub, nt, preferred_element_type=F32) * zeta
                sc = (lax.dot_general(qm, k2, nt, preferred_element_type=F32) * decay).astype(BF16)
                kz = (km.astype(F32) * zeta).astype(BF16)
                dv_ref[rs, half * 128:(half + 1) * 128] = lax.dot_general(sc, dob, tn, preferred_element_type=F32) \
                    + jnp.dot(kz, Ub, preferred_element_type=F32)
                U[half] = U[half] * cd + lax.dot_general(qm, xdo, tn, preferred_element_type=F32)
            dq_ref[rs, :] = dq_acc
            dk_ref[rs, :] = dk_acc

    rev = lambda w: pl.BlockSpec((rows, w), lambda p, n: (ns - 1 - n, p))
    return _pc(
        body, name="ret_bwd", grid=(2, ns),
        in_specs=[rev(128), rev(128), rev(256),
                  pl.BlockSpec((None, RET_STEP, 2, 128, 128), lambda p, n: (p, ns - 1 - n, 0, 0, 0)), rev(256)],
        out_specs=[rev(128), rev(128), rev(256)],
        out_shape=[jax.ShapeDtypeStruct((S, 256), F32), jax.ShapeDtypeStruct((S, 256), F32),
                   jax.ShapeDtypeStruct((S, 512), F32)],
        scratch_shapes=[pltpu.VMEM((2, 128, 128), F32)],
        compiler_params=_params(("parallel", "arbitrary")),
    )(rq, rk, rv, states, do)


def _col_of(b, m):
    return jnp.max(jnp.where(m, b, -jnp.inf), axis=1, keepdims=True)


def _attn_fwd(q, k, v, *, nq, max_dist, name, sinks=None, want_bf16=False):
    L, Ck = k.shape
    nb, ncol = L // BLK, Ck // LANES
    scale = HEAD ** -0.5
    has_sink = sinks is not None

    def body(*refs):
        q_ref, kp_ref, kc_ref, vp_ref, vc_ref = refs[:5]
        sk_ref = refs[5] if has_sink else None
        outs = refs[5 + has_sink:]
        n = pl.program_id(1)
        kcat = jnp.concatenate([kp_ref[...], kc_ref[...]], axis=0)
        vcat = jnp.concatenate([vp_ref[...], vc_ref[...]], axis=0)
        r = lax.broadcasted_iota(jnp.int32, (BLK, 2 * BLK), 0)
        c = lax.broadcasted_iota(jnp.int32, (BLK, 2 * BLK), 1)
        dist = r + BLK - c
        valid = (dist >= 0) & (dist <= max_dist) & ((c >= BLK) | (n > 0))
        for i in range(nq):
            q2 = q_ref[:, i * 128:(i + 1) * 128]
            o2 = jnp.zeros((BLK, LANES), F32)
            l2 = jnp.zeros((BLK, LANES), F32)
            for half in range(2):
                m = _head_mask((BLK, LANES), half)
                qm = jnp.where(m, q2, jnp.zeros_like(q2))
                s = lax.dot_general(qm, kcat, (((1,), (1,)), ((), ())), preferred_element_type=F32) * scale
                s = jnp.where(valid, s, -jnp.inf)
                mx = jnp.max(s, axis=1, keepdims=True)
                if has_sink:
                    snk = _col_of(sk_ref[:, i * 128:(i + 1) * 128], _head_mask((1, LANES), half))
                    mx = jnp.maximum(mx, snk)
                pr = jnp.exp(s - mx)
                den = jnp.sum(pr, axis=1, keepdims=True)
                if has_sink:
                    den = den + jnp.exp(snk - mx)
                pv = jnp.dot(pr.astype(BF16), vcat, preferred_element_type=F32) / den
                o2 = jnp.where(m, pv, o2)
                l2 = jnp.where(m, mx + jnp.log(den), l2)
            outs[0][:, i * 128:(i + 1) * 128] = o2
            outs[1][:, i * 128:(i + 1) * 128] = l2
            if want_bf16:
                outs[2][:, i * 128:(i + 1) * 128] = o2.astype(BF16)

    qspec = pl.BlockSpec((BLK, nq * 128), lambda j, n: (n, j))
    cur = pl.BlockSpec((BLK, 128), lambda j, n: (n, j))
    prev = pl.BlockSpec((BLK, 128), lambda j, n: (jnp.maximum(n - 1, 0), j))
    in_specs = [qspec, prev, cur, prev, cur]
    args = [q, k, k, v, v]
    if has_sink:
        in_specs.append(pl.BlockSpec((1, nq * 128), lambda j, n: (0, j)))
        args.append(sinks)
    out_dts = [F32, F32] + ([BF16] if want_bf16 else [])
    return _pc(
        body, name=name, grid=(ncol, nb), in_specs=in_specs,
        out_specs=[qspec] * len(out_dts),
        out_shape=[jax.ShapeDtypeStruct(q.shape, dt) for dt in out_dts],
        compiler_params=_params(("parallel", "parallel")),
    )(*args)


def _attn_bwd(q, k, v, o, lse, do, *, nq, max_dist, name, sinks=None):
    L, Ck = k.shape
    nb, ncol = L // BLK, Ck // LANES
    scale = HEAD ** -0.5
    has_sink = sinks is not None
    nt = (((1,), (1,)), ((), ()))
    tn = (((0,), (0,)), ((), ()))

    def body(*refs):
        (qc_ref, qn_ref, kp_ref, kc_ref, vp_ref, vc_ref, oc_ref, on_ref, lc_ref, ln_ref, dc_ref, dn_ref) = refs[:12]
        sk_ref = refs[12] if has_sink else None
        outs = refs[12 + has_sink:]
        dq_ref, dk_ref, dv_ref = outs[:3]
        n = pl.program_id(1)
        kc, vc = kc_ref[...], vc_ref[...]
        kcat = jnp.concatenate([kp_ref[...], kc], axis=0)
        vcat = jnp.concatenate([vp_ref[...], vc], axis=0)
        r = lax.broadcasted_iota(jnp.int32, (BLK, 2 * BLK), 0)
        c = lax.broadcasted_iota(jnp.int32, (BLK, 2 * BLK), 1)
        dist = r + BLK - c
        valid_q = (dist >= 0) & (dist <= max_dist) & ((c >= BLK) | (n > 0))
        r2 = lax.broadcasted_iota(jnp.int32, (2 * BLK, BLK), 0)
        c2 = lax.broadcasted_iota(jnp.int32, (2 * BLK, BLK), 1)
        dist2 = r2 - c2
        valid_k = (dist2 >= 0) & (dist2 <= max_dist) & ((r2 < BLK) | (n < nb - 1))
        dk_acc = jnp.zeros((BLK, LANES), F32)
        dv_acc = jnp.zeros((BLK, LANES), F32)
        for i in range(nq):
            sl = slice(i * 128, (i + 1) * 128)
            qcur, docur = qc_ref[:, sl], dc_ref[:, sl]
            qcat = jnp.concatenate([qcur, qn_ref[:, sl]], axis=0)
            docat = jnp.concatenate([docur, dn_ref[:, sl]], axis=0)
            ocat = jnp.concatenate([oc_ref[:, sl], on_ref[:, sl]], axis=0)
            lcat = jnp.concatenate([lc_ref[:, sl], ln_ref[:, sl]], axis=0)
            dq2 = jnp.zeros((BLK, LANES), F32)
            ds2 = jnp.zeros((1, LANES), F32)
            for half in range(2):
                m1 = _head_mask((BLK, LANES), half)
                m2 = _head_mask((2 * BLK, LANES), half)
                dom = jnp.where(m2, docat, 0.0)
                delta = jnp.sum(dom * ocat, axis=1, keepdims=True)
                lcol = _col_of(lcat, m2)
                domb = dom.astype(BF16)
                qmcat = jnp.where(m2, qcat, jnp.zeros_like(qcat))
                qm = qmcat[:BLK]
                s = lax.dot_general(qm, kcat, nt, preferred_element_type=F32) * scale
                pr = jnp.where(valid_q, jnp.exp(s - lcol[:BLK]), 0.0)
                dp = lax.dot_general(domb[:BLK], vcat, nt, preferred_element_type=F32)
                ds = (pr * (dp - delta[:BLK])).astype(BF16)
                dq2 = jnp.where(m1, jnp.dot(ds, kcat, preferred_element_type=F32) * scale, dq2)
                if has_sink:
                    snk = _col_of(sk_ref[:, sl], _head_mask((1, LANES), half))
                    contrib = jnp.sum(-jnp.exp(snk - lcol[:BLK]) * delta[:BLK], axis=0, keepdims=True)
                    ds2 = jnp.where(_head_mask((1, LANES), half), contrib, ds2)
                s = lax.dot_general(qmcat, kc, nt, preferred_element_type=F32) * scale
                pr = jnp.where(valid_k, jnp.exp(s - lcol), 0.0)
                dv_acc += lax.dot_general(pr.astype(BF16), domb, tn, preferred_element_type=F32)
                dp = lax.dot_general(domb, vc, nt, preferred_element_type=F32)
                ds = (pr * (dp - delta)).astype(BF16)
                dk_acc += lax.dot_general(ds, qmcat, tn, preferred_element_type=F32) * scale
            dq_ref[:, sl] = dq2
            if has_sink:
                @pl.when(n == 0)
                def _():
                    outs[3][:, sl] = jnp.zeros((1, LANES), F32)

                outs[3][:, sl] += ds2
        dk_ref[...] = dk_acc
        dv_ref[...] = dv_acc

    qcur = pl.BlockSpec((BLK, nq * 128), lambda j, n: (n, j))
    qnext = pl.BlockSpec((BLK, nq * 128), lambda j, n: (jnp.minimum(n + 1, nb - 1), j))
    cur = pl.BlockSpec((BLK, 128), lambda j, n: (n, j))
    prev = pl.BlockSpec((BLK, 128), lambda j, n: (jnp.maximum(n - 1, 0), j))
    in_specs = [qcur, qnext, prev, cur, prev, cur, qcur, qnext, qcur, qnext, qcur, qnext]
    args = [q, q, k, k, v, v, o, o, lse, lse, do, do]
    out_specs = [qcur, cur, cur]
    out_shape = [jax.ShapeDtypeStruct(q.shape, F32), jax.ShapeDtypeStruct(k.shape, F32), jax.ShapeDtypeStruct(k.shape, F32)]
    if has_sink:
        vec = pl.BlockSpec((1, nq * 128), lambda j, n: (0, j))
        in_specs.append(vec)
        args.append(sinks)
        out_specs.append(vec)
        out_shape.append(jax.ShapeDtypeStruct((1, q.shape[1]), F32))
    return _pc(
        body, name=name, grid=(ncol, nb), in_specs=in_specs, out_specs=out_specs, out_shape=out_shape,
        compiler_params=_params(("parallel", "arbitrary")),
    )(*args)


ATT_TILE = 2048


def _rows(ref, start, n, r):
    if r == 1:
        return ref[pl.ds(start, n), :]
    return ref[pl.ds(start, n, stride=r), :]


def _set_rows(ref, start, n, r, val):
    if r == 1:
        ref[pl.ds(start, n), :] = val
    else:
        ref[pl.ds(start, n, stride=r), :] = val


def _band_geometry(S, patterns):
    rmax = max(r for _, r in patterns)
    H = BLK * rmax
    T = min(S, ATT_TILE)
    assert T % H == 0 and S % T == 0
    return H, T, S // T, T // BLK


def _band_fwd(q, k, v, *, patterns, nq, name, sinks=None, want_bf16=False):
    S, Ck = k.shape
    H, T, nt, nbt = _band_geometry(S, patterns)
    ncol = Ck // LANES
    scale = HEAD ** -0.5
    has_sink = sinks is not None
    nt_dims = (((1,), (1,)), ((), ()))

    def body(*refs):
        q_ref, kp_ref, kc_ref, vp_ref, vc_ref = refs[:5]
        sk_ref = refs[5] if has_sink else None
        n_out = 3 if want_bf16 else 2
        outs = refs[5 + has_sink:5 + has_sink + n_out]
        qf, kf, vf, M, L, A = refs[5 + has_sink + n_out:]
        t = pl.program_id(1)
        kf[0:H, :] = kp_ref[...].astype(F32)
        kf[H:H + T, :] = kc_ref[...].astype(F32)
        vf[0:H, :] = vp_ref[...].astype(F32)
        vf[H:H + T, :] = vc_ref[...].astype(F32)
        r_i = lax.broadcasted_iota(jnp.int32, (BLK, 2 * BLK), 0)
        c_i = lax.broadcasted_iota(jnp.int32, (BLK, 2 * BLK), 1)
        dist_i = r_i + BLK - c_i
        masks = [_head_mask((BLK, LANES), h) for h in range(2)]

        for i in range(nq):
            qf[...] = q_ref[:, i * 128:(i + 1) * 128].astype(F32) * scale
            for p, (dist, r) in enumerate(patterns):
                in_band = (dist_i >= 0) & (dist_i <= dist)
                in_band_first = in_band & ((c_i >= BLK) | (t > 0))

                def unit(j, b, p=p, r=r, in_band=in_band, in_band_first=in_band_first):
                    q0 = j + b * (BLK * r)
                    q2 = _rows(qf, q0, BLK, r).astype(BF16)
                    kcat = _rows(kf, H + q0 - BLK * r, 2 * BLK, r).astype(BF16)
                    vcat = _rows(vf, H + q0 - BLK * r, 2 * BLK, r).astype(BF16)
                    valid = in_band if b > 0 else in_band_first
                    m2 = jnp.zeros((BLK, LANES), F32)
                    l2 = jnp.zeros((BLK, LANES), F32)
                    a2 = jnp.zeros((BLK, LANES), F32)
                    for half in range(2):
                        m = masks[half]
                        qm = jnp.where(m, q2, jnp.zeros_like(q2))
                        s = lax.dot_general(qm, kcat, nt_dims, preferred_element_type=F32)
                        s = jnp.where(valid, s, -jnp.inf)
                        mx = jnp.max(s, axis=1, keepdims=True)
                        pr = jnp.exp(s - mx)
                        den = jnp.sum(pr, axis=1, keepdims=True)
                        pv = jnp.dot(pr.astype(BF16), vcat, preferred_element_type=F32)
                        m2 = jnp.where(m, mx, m2)
                        l2 = jnp.where(m, den, l2)
                        a2 = jnp.where(m, pv, a2)
                    if p > 0:
                        mo = _rows(M, q0, BLK, r)
                        mn = jnp.maximum(mo, m2)
                        wa, wb = jnp.exp(mo - mn), jnp.exp(m2 - mn)
                        l2 = wa * _rows(L, q0, BLK, r) + wb * l2
                        a2 = wa * _rows(A, q0, BLK, r) + wb * a2
                        m2 = mn
                    _set_rows(M, q0, BLK, r, m2)
                    _set_rows(L, q0, BLK, r, l2)
                    _set_rows(A, q0, BLK, r, a2)

                for u in range(nbt):
                    unit(u % r, u // r)
            sl = slice(i * 128, (i + 1) * 128)
            mm, ll, aa = M[...], L[...], A[...]
            if has_sink:
                snk = sk_ref[:, sl]
                mn = jnp.maximum(mm, snk)
                w = jnp.exp(mm - mn)
                ll = ll * w + jnp.exp(snk - mn)
                aa = aa * w
                mm = mn
            o = aa / ll
            outs[0][:, sl] = o
            outs[1][:, sl] = mm + jnp.log(ll)
            if want_bf16:
                outs[2][:, sl] = o.astype(BF16)

    th = T // H
    qspec = pl.BlockSpec((T, nq * 128), lambda j, t: (t, j))
    cur = pl.BlockSpec((T, 128), lambda j, t: (t, j))
    prev = pl.BlockSpec((H, 128), lambda j, t: (jnp.maximum(t * th - 1, 0), j))
    in_specs = [qspec, prev, cur, prev, cur]
    args = [q, k, k, v, v]
    if has_sink:
        in_specs.append(pl.BlockSpec((1, nq * 128), lambda j, t: (0, j)))
        args.append(sinks)
    out_dts = [F32, F32] + ([BF16] if want_bf16 else [])
    return _pc(
        body, name=name, grid=(ncol, nt), in_specs=in_specs,
        out_specs=[qspec] * len(out_dts),
        out_shape=[jax.ShapeDtypeStruct(q.shape, dt) for dt in out_dts],
        scratch_shapes=[pltpu.VMEM((T, LANES), F32), pltpu.VMEM((H + T, LANES), F32), pltpu.VMEM((H + T, LANES), F32),
                        pltpu.VMEM((T, LANES), F32), pltpu.VMEM((T, LANES), F32), pltpu.VMEM((T, LANES), F32)],
        compiler_params=_params(("parallel", "parallel")),
    )(*args)


def _band_bwd(q, k, v, lse, delta, do, *, patterns, nq, name, sinks=None, do_col0=0):
    S, Ck = k.shape
    H, T, nt, nbt = _band_geometry(S, patterns)
    ncol = Ck // LANES
    scale = HEAD ** -0.5
    has_sink = sinks is not None
    nt_dims = (((1,), (1,)), ((), ()))
    tn_dims = (((0,), (0,)), ((), ()))

    def body(*refs):
        (qc_ref, qn_ref, kp_ref, kc_ref, vp_ref, vc_ref, lc_ref, ln_ref, ec_ref, en_ref, dc_ref, dn_ref) = refs[:12]
        sk_ref = refs[12] if has_sink else None
        n_out = 4 if has_sink else 3
        outs = refs[12 + has_sink:12 + has_sink + n_out]
        dq_ref, dk_ref, dv_ref = outs[:3]
        qf, kf, vf, lf, ef, df = refs[12 + has_sink + n_out:]
        t = pl.program_id(1)
        kf[0:H, :] = kp_ref[...].astype(F32)
        kf[H:H + T, :] = kc_ref[...].astype(F32)
        vf[0:H, :] = vp_ref[...].astype(F32)
        vf[H:H + T, :] = vc_ref[...].astype(F32)
        dk_ref[...] = jnp.zeros_like(dk_ref)
        dv_ref[...] = jnp.zeros_like(dv_ref)
        r_i = lax.broadcasted_iota(jnp.int32, (BLK, 2 * BLK), 0)
        c_i = lax.broadcasted_iota(jnp.int32, (BLK, 2 * BLK), 1)
        dist_q = r_i + BLK - c_i
        dist_h = dist_q[:, :BLK]
        m1 = [_head_mask((BLK, LANES), h) for h in range(2)]

        def head_inputs(half, q2, do2, l2, e2):
            m = m1[half]
            lh = jnp.where(m, l2, _roll(l2, HEAD))
            eh = jnp.where(m, e2, _roll(e2, HEAD))
            return jnp.where(m, q2, jnp.zeros_like(q2)), jnp.where(m, do2, 0.0).astype(BF16), lh, eh

        for i in range(nq):
            sl = slice(i * 128, (i + 1) * 128)
            qf[0:T, :] = qc_ref[:, sl].astype(F32) * scale
            qf[T:T + H, :] = qn_ref[:, sl].astype(F32) * scale
            for buf, c_ref, n_ref in ((lf, lc_ref, ln_ref), (ef, ec_ref, en_ref), (df, dc_ref, dn_ref)):
                buf[0:T, :] = c_ref[:, sl]
                buf[T:T + H, :] = n_ref[:, sl]
            if has_sink:
                @pl.when(t == 0)
                def _():
                    outs[3][:, sl] = jnp.zeros((1, LANES), F32)

                outs[3][:, sl] += jnp.sum(-jnp.exp(sk_ref[:, sl] - lc_ref[:, sl]) * ec_ref[:, sl], axis=0, keepdims=True)
            for p, (dist, r) in enumerate(patterns):
                band_q = (dist_q >= 0) & (dist_q <= dist)
                band_first = band_q & ((c_i >= BLK) | (t > 0))
                band_h = (dist_h >= 0) & (dist_h <= dist)

                def add_rows(ref, start, val, r=r):
                    _set_rows(ref, start, BLK, r, _rows(ref, start, BLK, r) + val)

                def unit(j, b, p=p, r=r, band_q=band_q, band_first=band_first):
                    q0 = j + b * (BLK * r)
                    q2 = _rows(qf, q0, BLK, r).astype(BF16)
                    do2, l2, e2 = _rows(df, q0, BLK, r), _rows(lf, q0, BLK, r), _rows(ef, q0, BLK, r)
                    kcat = _rows(kf, H + q0 - BLK * r, 2 * BLK, r).astype(BF16)
                    vcat = _rows(vf, H + q0 - BLK * r, 2 * BLK, r).astype(BF16)
                    valid = band_q if b > 0 else band_first
                    dq2 = jnp.zeros((BLK, LANES), F32)
                    dkc = jnp.zeros((2 * BLK, LANES), F32)
                    dvc = jnp.zeros((2 * BLK, LANES), F32)
                    for half in range(2):
                        qm, dom, lh, eh = head_inputs(half, q2, do2, l2, e2)
                        s = lax.dot_general(qm, kcat, nt_dims, preferred_element_type=F32)
                        pr = jnp.where(valid, jnp.exp(s - jnp.concatenate([lh, lh], axis=1)), 0.0)
                        dp = lax.dot_general(dom, vcat, nt_dims, preferred_element_type=F32)
                        ds = (pr * (dp - jnp.concatenate([eh, eh], axis=1))).astype(BF16)
                        dq2 = jnp.where(m1[half], jnp.dot(ds, kcat, preferred_element_type=F32) * scale, dq2)
                        dvc += lax.dot_general(pr.astype(BF16), dom, tn_dims, preferred_element_type=F32)
                        dkc += lax.dot_general(ds, qm, tn_dims, preferred_element_type=F32)
                    if p > 0:
                        dq2 = dq2 + _rows(dq_ref.at[:, sl], q0, BLK, r)
                    _set_rows(dq_ref.at[:, sl], q0, BLK, r, dq2)
                    add_rows(dk_ref, q0, dkc[BLK:])
                    add_rows(dv_ref, q0, dvc[BLK:])
                    if b > 0:
                        add_rows(dk_ref, q0 - BLK * r, dkc[:BLK])
                        add_rows(dv_ref, q0 - BLK * r, dvc[:BLK])

                def halo_unit(j, r=r, band_h=band_h):
                    k0 = j + (nbt // r - 1) * (BLK * r)
                    q2 = _rows(qf, T + j, BLK, r).astype(BF16)
                    do2, l2, e2 = _rows(df, T + j, BLK, r), _rows(lf, T + j, BLK, r), _rows(ef, T + j, BLK, r)
                    kc = _rows(kf, H + k0, BLK, r).astype(BF16)
                    vc = _rows(vf, H + k0, BLK, r).astype(BF16)
                    dk2 = jnp.zeros((BLK, LANES), F32)
                    dv2 = jnp.zeros((BLK, LANES), F32)
                    for half in range(2):
                        qm, dom, lh, eh = head_inputs(half, q2, do2, l2, e2)
                        s = lax.dot_general(qm, kc, nt_dims, preferred_element_type=F32)
                        pr = jnp.where(band_h, jnp.exp(s - lh), 0.0)
                        dp = lax.dot_general(dom, vc, nt_dims, preferred_element_type=F32)
                        ds = (pr * (dp - eh)).astype(BF16)
                        dv2 += lax.dot_general(pr.astype(BF16), dom, tn_dims, preferred_element_type=F32)
                        dk2 += lax.dot_general(ds, qm, tn_dims, preferred_element_type=F32)
                    add_rows(dk_ref, k0, dk2)
                    add_rows(dv_ref, k0, dv2)

                for u in range(nbt):
                    unit(u % r, u // r)
                if nt > 1:
                    @pl.when(t < nt - 1)
                    def _(r=r, halo_unit=halo_unit):
                        for j in range(r):
                            halo_unit(j)

    th = T // H
    last = S // H - 1
    qcur = pl.BlockSpec((T, nq * 128), lambda j, t: (t, j))
    qnext = pl.BlockSpec((H, nq * 128), lambda j, t: (jnp.minimum((t + 1) * th, last), j))
    cur = pl.BlockSpec((T, 128), lambda j, t: (t, j))
    prev = pl.BlockSpec((H, 128), lambda j, t: (jnp.maximum(t * th - 1, 0), j))
    dcur = pl.BlockSpec((T, nq * 128), lambda j, t: (t, j + do_col0))
    dnext = pl.BlockSpec((H, nq * 128), lambda j, t: (jnp.minimum((t + 1) * th, last), j + do_col0))
    in_specs = [qcur, qnext, prev, cur, prev, cur, qcur, qnext, qcur, qnext, dcur, dnext]
    args = [q, q, k, k, v, v, lse, lse, delta, delta, do, do]
    out_specs = [qcur, cur, cur]
    out_shape = [jax.ShapeDtypeStruct(q.shape, F32), jax.ShapeDtypeStruct(k.shape, F32), jax.ShapeDtypeStruct(k.shape, F32)]
    if has_sink:
        vec = pl.BlockSpec((1, nq * 128), lambda j, t: (0, j))
        in_specs.append(vec)
        args.append(sinks)
        out_specs.append(vec)
        out_shape.append(jax.ShapeDtypeStruct((1, q.shape[1]), F32))
    big = pltpu.VMEM((T + H, LANES), F32)
    return _pc(
        body, name=name, grid=(ncol, nt), in_specs=in_specs, out_specs=out_specs, out_shape=out_shape,
        scratch_shapes=[big] * 6,
        compiler_params=_params(("parallel", "arbitrary")),
    )(*args)


def _delta(do, o, name):
    S, C = do.shape
    tm = 512

    def body(do_ref, o_ref, g_ref, e_ref):
        for c in range(C // LANES):
            sl = slice(c * 128, (c + 1) * 128)
            e_ref[:, sl] = _gmean(do_ref[:, sl] * o_ref[:, sl], g_ref[...]) * float(HEAD)

    row = pl.BlockSpec((tm, C), lambda i: (i, 0))
    return _pc(
        body, name=name, grid=(S // tm,),
        in_specs=[row, row, pl.BlockSpec((LANES, LANES), lambda i: (0, 0))], out_specs=row,
        out_shape=jax.ShapeDtypeStruct((S, C), F32),
        compiler_params=_params(("parallel",)),
    )(do, o, _group_matrix())


def _even_post_fwd(ro, proj, gn, da):
    S = ro.shape[0]
    tm = 256

    def body(ro_ref, rg_ref, gn_ref, da_ref, mix_ref):
        for c in range(4):
            sl = slice(c * 128, (c + 1) * 128)
            x = ro_ref[:, sl]
            mu = jnp.mean(x, axis=1, keepdims=True)
            xc = x - mu
            var = jnp.mean(xc * xc, axis=1, keepdims=True)
            y = xc * lax.rsqrt(var + EPS) * gn_ref[:, sl]
            z = rg_ref[:, sl]
            mix_ref[:, sl] = (z * jax.nn.sigmoid(z) * y).astype(BF16)
        mix_ref[:, 512:1024] = da_ref[...].astype(BF16)

    row = lambda w: pl.BlockSpec((tm, w), lambda i: (i, 0))
    return _pc(
        body, name="even_post_fwd", grid=(S // tm,),
        in_specs=[row(512), pl.BlockSpec((tm, 512), lambda i: (i, 2)), pl.BlockSpec((1, 512), lambda i: (0, 0)), row(512)],
        out_specs=row(1024), out_shape=jax.ShapeDtypeStruct((S, 1024), BF16),
        compiler_params=_params(("parallel",)),
    )(ro, proj, gn, da)


def _even_post_bwd(ro, proj, gn, dmixed):
    S = ro.shape[0]
    tm = 256

    def body(ro_ref, rg_ref, gn_ref, dm_ref, dro_ref, drg_ref, dgn_ref):
        @pl.when(pl.program_id(0) == 0)
        def _():
            dgn_ref[...] = jnp.zeros_like(dgn_ref)

        for c in range(4):
            sl = slice(c * 128, (c + 1) * 128)
            x = ro_ref[:, sl]
            mu = jnp.mean(x, axis=1, keepdims=True)
            xc = x - mu
            rstd = lax.rsqrt(jnp.mean(xc * xc, axis=1, keepdims=True) + EPS)
            xh = xc * rstd
            gain = gn_ref[:, sl]
            y = xh * gain
            z = rg_ref[:, sl]
            sg = jax.nn.sigmoid(z)
            dra = dm_ref[:, sl]
            drg_ref[:, sl] = dra * y * sg * (1.0 + z * (1.0 - sg))
            dy = dra * z * sg
            dgn_ref[:, sl] += jnp.sum(dy * xh, axis=0, keepdims=True)
            dxh = dy * gain
            dro_ref[:, sl] = rstd * (dxh - jnp.mean(dxh, axis=1, keepdims=True)
                                     - xh * jnp.mean(dxh * xh, axis=1, keepdims=True))

    row = lambda w: pl.BlockSpec((tm, w), lambda i: (i, 0))
    vec = pl.BlockSpec((1, 512), lambda i: (0, 0))
    return _pc(
        body, name="even_post_bwd", grid=(S // tm,),
        in_specs=[row(512), pl.BlockSpec((tm, 512), lambda i: (i, 2)), vec, row(512)],
        out_specs=[row(512), row(512), vec],
        out_shape=[jax.ShapeDtypeStruct((S, 512), F32), jax.ShapeDtypeStruct((S, 512), F32),
                   jax.ShapeDtypeStruct((1, 512), F32)],
        compiler_params=_params(("arbitrary",)),
    )(ro, proj, gn, dmixed)


def _swa_pre_fwd(proj, tab, qg, kg):
    S = proj.shape[0]
    tm = 256

    def body(p_ref, tab_ref, qg_ref, kg_ref, g_ref, q_ref, k_ref, v_ref):
        Ap, Bp, Cp = _tab(tab_ref, 1)
        G = g_ref[...]
        lo = _head_mask((tm, LANES), 0)
        for c in range(8):
            sl = slice(c * 128, (c + 1) * 128)
            q_ref[:, sl] = _rope(_hn_fwd(p_ref[:, sl], qg_ref[...], G), Ap, Bp, Cp, 8).astype(BF16)
        for c in range(2):
            kn = _rope(_hn_fwd(p_ref[:, 1024 + c * 128:1024 + (c + 1) * 128], kg_ref[...], G), Ap, Bp, Cp, 8)
            vv = p_ref[:, 1280 + c * 128:1280 + (c + 1) * 128]
            for t, ref in ((kn, k_ref), (vv, v_ref)):
                sw = _roll(t, HEAD)
                ref[:, (2 * c) * 128:(2 * c + 1) * 128] = jnp.where(lo, t, sw).astype(BF16)
                ref[:, (2 * c + 1) * 128:(2 * c + 2) * 128] = jnp.where(lo, sw, t).astype(BF16)

    row = lambda w: pl.BlockSpec((tm, w), lambda i: (i, 0))
    vec = pl.BlockSpec((1, LANES), lambda i: (0, 0))
    return _pc(
        body, name="swa_pre_fwd", grid=(S // tm,),
        in_specs=[row(1536), row(768), vec, vec, pl.BlockSpec((LANES, LANES), lambda i: (0, 0))],
        out_specs=[row(1024), row(512), row(512)],
        out_shape=[jax.ShapeDtypeStruct((S, w), BF16) for w in (1024, 512, 512)],
        compiler_params=_params(("parallel",)),
    )(proj, tab, qg, kg, _group_matrix())


def _swa_pre_bwd(proj, tab, qg, kg, dq, dk, dv):
    S = proj.shape[0]
    tm = 256

    def body(p_ref, tab_ref, qg_ref, kg_ref, g_ref, dq_ref, dk_ref, dv_ref, dp_ref, db_ref, dqg_ref, dkg_ref):
        Ap, Bp, Cp = _tab(tab_ref, 1)
        G = g_ref[...]
        lo = _head_mask((tm, LANES), 0)

        @pl.when(pl.program_id(0) == 0)
        def _():
            db_ref[...] = jnp.zeros_like(db_ref)
            dqg_ref[...] = jnp.zeros_like(dqg_ref)
            dkg_ref[...] = jnp.zeros_like(dkg_ref)

        accq = jnp.zeros((1, LANES), F32)
        acck = jnp.zeros((1, LANES), F32)
        for c in range(8):
            sl = slice(c * 128, (c + 1) * 128)
            dx, dg = _hn_bwd(p_ref[:, sl], qg_ref[...], _rope_t(dq_ref[:, sl], Ap, Bp, Cp, 8), G)
            dp_ref[:, sl] = dx.astype(BF16)
            db_ref[:, sl] += jnp.sum(dx, axis=0, keepdims=True)
            accq = accq + dg
        for c in range(2):
            folded = []
            for ref in (dk_ref, dv_ref):
                a = ref[:, (2 * c) * 128:(2 * c + 1) * 128]
                b = ref[:, (2 * c + 1) * 128:(2 * c + 2) * 128]
                folded.append(jnp.where(lo, a + _roll(a, HEAD), b + _roll(b, HEAD)))
            ks = slice(1024 + c * 128, 1024 + (c + 1) * 128)
            dx, dg = _hn_bwd(p_ref[:, ks], kg_ref[...], _rope_t(folded[0], Ap, Bp, Cp, 8), G)
            dp_ref[:, ks] = dx.astype(BF16)
            db_ref[:, ks] += jnp.sum(dx, axis=0, keepdims=True)
            acck = acck + dg
            vs = slice(1280 + c * 128, 1280 + (c + 1) * 128)
            dp_ref[:, vs] = folded[1].astype(BF16)
            db_ref[:, vs] += jnp.sum(folded[1], axis=0, keepdims=True)
        dqg_ref[...] += _fold_halves(accq)
        dkg_ref[...] += _fold_halves(acck)

    row = lambda w: pl.BlockSpec((tm, w), lambda i: (i, 0))
    vec = pl.BlockSpec((1, LANES), lambda i: (0, 0))
    return _pc(
        body, name="swa_pre_bwd", grid=(S // tm,),
        in_specs=[row(1536), row(768), vec, vec, pl.BlockSpec((LANES, LANES), lambda i: (0, 0)),
                  row(1024), row(512), row(512)],
        out_specs=[row(1536), pl.BlockSpec((1, 1536), lambda i: (0, 0)), vec, vec],
        out_shape=[jax.ShapeDtypeStruct((S, 1536), BF16), jax.ShapeDtypeStruct((1, 1536), F32),
                   jax.ShapeDtypeStruct((1, LANES), F32), jax.ShapeDtypeStruct((1, LANES), F32)],
        compiler_params=_params(("arbitrary",)),
    )(proj, tab, qg, kg, _group_matrix(), dq, dk, dv)


def _loss_head(y, target):
    S, Dm = y.shape
    tm = 512

    def body(y_ref, t_ref, l_ref, dy_ref, dyb_ref):
        @pl.when(pl.program_id(0) == 0)
        def _():
            l_ref[...] = jnp.zeros_like(l_ref)

        e = y_ref[...] - t_ref[...]
        dy = e * (1.0 / Dm)
        dy_ref[...] = dy
        dyb_ref[...] = dy.astype(BF16)
        row = jnp.sum(e * e, axis=1, keepdims=True) * (0.5 / Dm)
        l_ref[...] += jnp.sum(row, axis=0, keepdims=True)

    row = pl.BlockSpec((tm, Dm), lambda i: (i, 0))
    return _pc(
        body, name="loss_head", grid=(S // tm,), in_specs=[row, row],
        out_specs=[pl.BlockSpec((1, LANES), lambda i: (0, 0)), row, row],
        out_shape=[jax.ShapeDtypeStruct((1, LANES), F32), jax.ShapeDtypeStruct((S, Dm), F32),
                   jax.ShapeDtypeStruct((S, Dm), BF16)],
        compiler_params=_params(("arbitrary",)),
    )(y, target)


def _relu2_of(u):
    r = jnp.maximum(u.astype(F32), 0.0)
    return r * r


def _drelu2(acc, u):
    return (acc * 2.0 * jnp.maximum(u.astype(F32), 0.0),)


def _add(acc, res):
    return (acc + res,)


_T = dict(tm=1024, tn=1024, tk=1024)


def _rms_bwd_in(x, g, dres):
    def epilogue(dh, xv, gv, dr):
        r = lax.rsqrt(jnp.mean(xv * xv, axis=-1, keepdims=True) + EPS)
        t = dh * gv
        dx = dr + r * t - xv * (r * r * r) * jnp.mean(xv * t, axis=-1, keepdims=True)
        return dx, dx, jnp.sum(dh * xv * r, axis=0, keepdims=True)

    return dict(outs=[F32, BF16, ("colsum",)], epilogue=epilogue,
                extras=[(x, "mn"), (g.reshape(1, D_MODEL), "n"), (dres, "mn")])


def _delta_in(o, col0):
    width = D_MODEL - col0

    def epilogue(do, ov, G):
        parts = [_gmean(do[:, col0 + c * 128:col0 + (c + 1) * 128] * ov[:, c * 128:(c + 1) * 128], G) * float(HEAD)
                 for c in range(width // LANES)]
        return do, jnp.concatenate(parts, axis=1)

    return dict(outs=[F32, (F32, width)], epilogue=epilogue, extras=[(o, width), (_group_matrix(), "full")])


def _mlp_fwd(x, g, wts, layer, tag):
    h = _rms_fwd(x, g, f"rms_mlp_fwd{tag}")
    u = _matmul(h, wts, dims="nn", **_T, outs=[BF16], b_cs=True, b_row0=layer, name=f"mlp_up{tag}")
    x_out = _matmul(u, wts, dims="nn", **_T, outs=[F32], epilogue=_add, extras=[(x, "mn")], a_pro=_relu2_of,
                    b_rs=1024, b_row0=2 + layer, name=f"mlp_down{tag}")
    return x_out, (h, u)


def _mlp_bwd(x, g, wts, layer, saved, dy, dyb, tag):
    h, u = saved
    du = _matmul(dyb, wts, dims="nt", **_T, outs=[BF16], epilogue=_drelu2, extras=[(u, "mn")], b_rs=1024,
                 b_row0=2 + layer, name=f"mlp_du{tag}")
    dw_dn = _matmul(u, dyb, dims="tn", **_T, outs=[F32], a_pro=_relu2_of, name=f"mlp_dwdown{tag}")
    dw_up = _matmul(h, du, dims="tn", **_T, outs=[F32], o_cs=N_CHIPS, name=f"mlp_dwup{tag}")
    dx, dxb, dg = _matmul(du, wts, dims="nt", tm=512, tn=1024, tk=1024, b_cs=True, b_row0=layer, b_rows=1024, name=f"mlp_dh{tag}",
                          **_rms_bwd_in(x, g, dy))
    return dx, dxb, dg, dw_up, dw_dn


def _pattern_view(t, r):
    S, C = t.shape
    return t.reshape(S // r, r * C)


def _local_step(x, pos_col, target, W, rest_of, P, red):
    S = x.shape[0]
    tab = _tables(pos_col)
    tile2 = lambda g: jnp.tile(g.reshape(1, HEAD), (1, 2))
    dqg, dkg = tile2(P["dil_q_gain"]), tile2(P["dil_k_gain"])
    sqg, skg = tile2(P["swa_q_gain"]), tile2(P["swa_k_gain"])
    gn = P["ret_gn_gain"].reshape(1, 512)
    sink_b = jnp.repeat(P["swa_sinks"].reshape(16), HEAD).reshape(1, 1024)

    h0 = _rms_fwd(x, P["norm_mix"][0], "rms_mix_fwd0")
    proj = _matmul(h0, W["hyb_w_in"], dims="nn", tm=1024, tn=768, tk=1024, outs=[F32], b_cs=True, name="hyb_in")
    rq, rk, rv, dq, dk, dv = _even_pre_fwd(proj, tab, dqg, dkg)
    ro, states = _ret_fwd(rq, rk, rv)
    dil = [(w // r, r) for w, r in DIL_PATTERNS]
    da, dlse = _band_fwd(dq, dk, dv, patterns=dil, nq=1, name="dil_fwd")
    mixed = _even_post_fwd(ro, proj, gn, da)
    x1 = _matmul(mixed, W["hyb_w_out"], dims="nn", **_T, outs=[F32], epilogue=_add, extras=[(x, "mn")], name="hyb_out")
    rest, bias = rest_of(x1)
    W = {**W, **rest}
    x2, mlp0 = _mlp_fwd(x1, P["norm_mlp"][0], W["packed"], 0, "0")

    h2 = _rms_fwd(x2, P["norm_mix"][1], "rms_mix_fwd1")
    proj2 = _matmul(h2, W["swa_w_qkv"], dims="nn", tm=1024, tn=384, tk=1024, outs=[F32], b_cs=True,
                    epilogue=_add, extras=[(bias.reshape(1, 1536), "n")], name="swa_qkv")
    sq, sk, sv = _swa_pre_fwd(proj2, tab, sqg, skg)
    swa = [(SWA_DIST, 1)]
    so, slse, so_b = _band_fwd(sq, sk, sv, patterns=swa, nq=2, name="swa_fwd", sinks=sink_b, want_bf16=True)
    x3 = _matmul(so_b, W["swa_w_out"], dims="nn", **_T, outs=[F32], epilogue=_add, extras=[(x2, "mn")], name="swa_out")
    y, mlp1 = _mlp_fwd(x3, P["norm_mlp"][1], W["packed"], 1, "1")
    loss, dy, dyb = _loss_head(y, target)

    gw, gp = {}, {}
    dx3, dx3b, dg_mlp1, gw["mlp_w_up1"], gw["mlp_w_down1"] = _mlp_bwd(x3, P["norm_mlp"][1], W["packed"], 1, mlp1, dy, dyb, "1")
    dx3b = red.begin("mlp1", {n: (gw[n], 1024) for n in ("mlp_w_up1", "mlp_w_down1")}, dx3b)
    gw["swa_w_out"] = _matmul(so_b, dx3b, dims="tn", **_T, outs=[F32], name="swa_dwout")
    dso, sdelta = _matmul(dx3b, W["swa_w_out"], dims="nt", tm=512, tn=1024, tk=1024, name="swa_do", **_delta_in(so, 0))
    dsq, dsk, dsv, dsink = _band_bwd(sq, sk, sv, slse, sdelta, dso, patterns=swa, nq=2, name="swa_bwd", sinks=sink_b)
    dproj2, gp["swa_b_qkv"], gp["swa_q_gain"], gp["swa_k_gain"] = _swa_pre_bwd(proj2, tab, sqg, skg, dsq, dsk, dsv)
    gp["swa_sinks"] = dsink
    gw["swa_w_qkv"] = _matmul(h2, dproj2, dims="tn", tm=1024, tn=384, tk=1024, outs=[F32], o_cs=N_CHIPS, name="swa_dwqkv")
    dx2, dx2b, dg_mix1 = _matmul(dproj2, W["swa_w_qkv"], dims="nt", tm=512, tn=1024, tk=384, b_cs=True, name="swa_dh",
                                 **_rms_bwd_in(x2, P["norm_mix"][1], dx3))
    dx2b = red.begin("swa", {"swa_w_qkv": (gw["swa_w_qkv"], 1024), "swa_w_out": (gw["swa_w_out"], 256)}, dx2b)
    dx2b = red.advance("mlp1", dx2b, dx2b)

    dx1, dx1b, dg_mlp0, gw["mlp_w_up0"], gw["mlp_w_down0"] = _mlp_bwd(x1, P["norm_mlp"][0], W["packed"], 0, mlp0, dx2, dx2b, "0")
    gw["hyb_w_out"] = _matmul(mixed, dx1b, dims="tn", **_T, outs=[F32], name="hyb_dwout")
    dx1b = red.begin("mlp0", {"mlp_w_up0": (gw["mlp_w_up0"], 1024), "mlp_w_down0": (gw["mlp_w_down0"], 1024),
                              "hyb_w_out": (gw["hyb_w_out"], 256)}, dx1b)
    dx1b = red.advance("swa", dx1b, dx1b)
    red.finish("mlp1", dx1b)
    dmixed, ddelta = _matmul(dx1b, W["hyb_w_out"], dims="nt", tm=512, tn=1024, tk=1024, name="hyb_dmixed", **_delta_in(da, 512))
    dro, drg, gp["ret_gn_gain"] = _even_post_bwd(ro, proj, gn, dmixed)
    drq, drk, drv = _ret_bwd(rq, rk, rv, states, dro)
    ddq, ddk, ddv = _band_bwd(dq, dk, dv, dlse, ddelta, dmixed, patterns=dil, nq=1, name="dil_bwd", do_col0=4)
    ddq = red.advance("mlp0", ddq, ddq)
    red.finish("swa", ddq)
    dproj, gp["dil_q_gain"], gp["dil_k_gain"] = _even_pre_bwd(proj, tab, dqg, dkg, drq, drk, drv, drg, [ddq], [ddk], [ddv])
    gw["hyb_w_in"] = _matmul(h0, dproj, dims="tn", tm=1024, tn=768, tk=1024, outs=[F32], o_cs=N_CHIPS, name="hyb_dwin")
    dproj = red.begin("win", {"hyb_w_in": (gw["hyb_w_in"], 1024)}, dproj)
    grad_x, _, dg_mix0 = _matmul(dproj, W["hyb_w_in"], dims="nt", tm=512, tn=1024, tk=768, b_cs=True, name="hyb_dh",
                                 **_rms_bwd_in(x, P["norm_mix"][0], dx1))
    red.finish("mlp0", grad_x)
    gp["norm_mix"] = jnp.concatenate([dg_mix0, dg_mix1], axis=0)
    gp["norm_mlp"] = jnp.concatenate([dg_mlp0, dg_mlp1], axis=0)
    return loss, grad_x, gp


HBM = pl.BlockSpec(memory_space=pltpu.HBM)


def _place():
    x, y, c = lax.axis_index("x"), lax.axis_index("y"), lax.axis_index("c")
    chips = [(1 - x, y), (x, 1 - y), (1 - x, 1 - y)]
    return x, y, c, chips


def _allgather_shards(buf):
    _, R, Wd = buf.shape
    Rh = R // 2

    def body(b_ref, out_ref, send_sems, recv_sems):
        x, y, c, chips = _place()
        sibling = (x, y, 1 - c)

        def copy(k, chip, core, to):
            block = b_ref.at[2 * chip[0] + chip[1], pl.ds(core * Rh, Rh), :]
            return pltpu.make_async_remote_copy(
                src_ref=block, dst_ref=block, send_sem=send_sems.at[k], recv_sem=recv_sems.at[k],
                device_id=to, device_id_type=MESH)

        first = [copy(k, (x, y), c, (*chip, c)) for k, chip in enumerate(chips)]
        for cp in first:
            cp.start()
        passed = [copy(3 + k, chip, c, sibling) for k, chip in enumerate(chips)]
        for k, chip in enumerate(chips):
            copy(k, chip, c, (x, y, c)).wait_recv()
            passed[k].start()
        for k, chip in enumerate(chips):
            copy(3 + k, chip, 1 - c, (x, y, c)).wait_recv()
        for cp in first + passed:
            cp.wait_send()

    return _pc(
        body, name="allgather_first", in_specs=[HBM], out_specs=HBM,
        out_shape=jax.ShapeDtypeStruct(buf.shape, buf.dtype), input_output_aliases={0: 0},
        scratch_shapes=[pltpu.SemaphoreType.DMA((6,)), pltpu.SemaphoreType.DMA((6,))],
    )(buf)


SEM = pl.BlockSpec(memory_space=pltpu.SEMAPHORE)
EFFECT = pltpu.SideEffectType.DATAFLOW_SIDE_EFFECTING


def _half_block(ref, chip, core):
    rh = ref.shape[1] // 2
    return ref.at[2 * chip[0] + chip[1], pl.ds(core * rh, rh), :]


def _gather_start(buf, ride):
    def body(b_ref, ride_ref, s0, s1, s2, r0, r1, r2, b_out, ride_out):
        x, y, c, chips = _place()
        for chip, s, r in zip(chips, (s0, s1, s2), (r0, r1, r2)):
            mine = _half_block(b_ref, (x, y), c)
            pltpu.make_async_remote_copy(src_ref=mine, dst_ref=mine, send_sem=s, recv_sem=r,
                                         device_id=(*chip, c), device_id_type=MESH).start()

    sem = pltpu.SemaphoreType.DMA(())
    return _pc(
        body, name="allgather_rest_start",
        out_shape=(sem,) * 6 + (pltpu.HBM(buf.shape, buf.dtype), pltpu.HBM(ride.shape, ride.dtype)),
        in_specs=(HBM, HBM), out_specs=(SEM,) * 6 + (HBM, HBM), input_output_aliases={0: 6, 1: 7},
        compiler_params=pltpu.CompilerParams(has_side_effects=EFFECT),
    )(pltpu.with_memory_space_constraint(buf, pltpu.HBM), pltpu.with_memory_space_constraint(ride, pltpu.HBM))


def _gather_wait(buf, sems, after):
    def body(b_ref, s0, s1, s2, r0, r1, r2, after_ref, b_out):
        x, y, c, chips = _place()
        for chip, s, r in zip(chips, (s0, s1, s2), (r0, r1, r2)):
            cp = pltpu.make_async_remote_copy(src_ref=_half_block(b_ref, (x, y), c), dst_ref=_half_block(b_ref, chip, c),
                                              send_sem=s, recv_sem=r, device_id=(*chip, c), device_id_type=MESH)
            cp.wait_send()
            cp.wait_recv()

    return _pc(
        body, name="allgather_rest_wait", out_shape=pltpu.HBM(buf.shape, buf.dtype),
        in_specs=(HBM,) + (SEM,) * 6 + (pl.BlockSpec(memory_space=pl.ANY),), out_specs=HBM, input_output_aliases={0: 0},
        compiler_params=pltpu.CompilerParams(has_side_effects=EFFECT),
    )(buf, *sems, after)


def _gather_handover(buf):
    def body(b_ref, out_ref, send_sems, recv_sems):
        x, y, c, chips = _place()
        cps = []
        for k, chip in enumerate(chips):
            mine = _half_block(b_ref, chip, c)
            cps.append(pltpu.make_async_remote_copy(src_ref=mine, dst_ref=mine, send_sem=send_sems.at[k],
                                                    recv_sem=recv_sems.at[k], device_id=(x, y, 1 - c), device_id_type=MESH))
        for cp in cps:
            cp.start()
        for k, chip in enumerate(chips):
            theirs = _half_block(b_ref, chip, 1 - c)
            pltpu.make_async_remote_copy(src_ref=theirs, dst_ref=theirs, send_sem=send_sems.at[k], recv_sem=recv_sems.at[k],
                                         device_id=(x, y, 1 - c), device_id_type=MESH).wait_recv()
        for cp in cps:
            cp.wait_send()

    return _pc(
        body, name="allgather_rest_handover", in_specs=[HBM], out_specs=HBM,
        out_shape=jax.ShapeDtypeStruct(buf.shape, buf.dtype), input_output_aliases={0: 0},
        scratch_shapes=[pltpu.SemaphoreType.DMA((3,)), pltpu.SemaphoreType.DMA((3,))],
    )(buf)


def _swap_halves(ts):
    nt = len(ts)

    def body(*refs):
        t_refs, l_refs, send_sems, recv_sems = refs[:nt], refs[nt:2 * nt], refs[-2], refs[-1]
        x, y, c, _ = _place()
        cps = []
        for k in range(nt):
            rh = t_refs[k].shape[1] // 2
            cps.append(pltpu.make_async_remote_copy(
                src_ref=t_refs[k].at[:, pl.ds((1 - c) * rh, rh), :], dst_ref=l_refs[k],
                send_sem=send_sems.at[k], recv_sem=recv_sems.at[k], device_id=(x, y, 1 - c), device_id_type=MESH))
        for cp in cps:
            cp.start()
        for cp in cps:
            cp.wait()

    return _pc(
        body, name="grad_swap_halves", in_specs=[HBM] * nt, out_specs=[HBM] * nt,
        out_shape=[jax.ShapeDtypeStruct((t.shape[0], t.shape[1] // 2, t.shape[2]), F32) for t in ts],
        scratch_shapes=[pltpu.SemaphoreType.DMA((nt,)), pltpu.SemaphoreType.DMA((nt,))],
    )(*ts)


def _pair_sum(t, l, place, name):
    _, r, cols = t.shape
    rh = r // 2
    tr = min(rh, 256)
    nr = rh // tr

    def body(pl_ref, t_ref, l_ref, o_ref):
        o_ref[...] = (t_ref[...] + l_ref[...]).astype(BF16)

    other = lambda s, p: s + jnp.where(s >= p[0], 1, 0)
    return _pc(
        body, name=name,
        grid_spec=pltpu.PrefetchScalarGridSpec(
            num_scalar_prefetch=1, grid=(N_CHIPS - 1, nr),
            in_specs=[pl.BlockSpec((None, tr, cols), lambda s, i, p: (other(s, p), p[1] * nr + i, 0)),
                      pl.BlockSpec((None, tr, cols), lambda s, i, p: (other(s, p), i, 0))],
            out_specs=pl.BlockSpec((None, tr, cols), lambda s, i, p: (other(s, p), i, 0))),
        out_shape=jax.ShapeDtypeStruct((N_CHIPS, rh, cols), BF16),
        compiler_params=_params(("parallel", "parallel")),
    )(place, t, l)


def _exchange_chips(ps):
    nt = len(ps)

    def body(*refs):
        p_refs, r_refs, send_sems, recv_sems = refs[:nt], refs[nt:2 * nt], refs[-2], refs[-1]
        x, y, c, chips = _place()
        cps = []
        for t in range(nt):
            for k, chip in enumerate(chips):
                cps.append(pltpu.make_async_remote_copy(
                    src_ref=p_refs[t].at[2 * chip[0] + chip[1]], dst_ref=r_refs[t].at[k],
                    send_sem=send_sems.at[3 * t + k], recv_sem=recv_sems.at[3 * t + k],
                    device_id=(*chip, c), device_id_type=MESH))
        for cp in cps:
            cp.start()
        for cp in cps:
            cp.wait()

    return _pc(
        body, name="grad_exchange_chips", in_specs=[HBM] * nt, out_specs=[HBM] * nt,
        out_shape=[jax.ShapeDtypeStruct((3,) + p.shape[1:], BF16) for p in ps],
        scratch_shapes=[pltpu.SemaphoreType.DMA((3 * nt,)), pltpu.SemaphoreType.DMA((3 * nt,))],
    )(*ps)


def _final_sum(t, l, rcv, place, name, layer=0, layers=1, into=None):
    _, r, cols = t.shape
    rh = r // 2
    tr = min(rh, 256)
    nr = rh // tr

    def body(pl_ref, t_ref, l_ref, r_ref, *rest):
        acc = t_ref[...] + l_ref[...]
        for k in range(3):
            acc = acc + r_ref[k].astype(F32)
        rest[-1][...] = acc

    in_specs = [pl.BlockSpec((None, tr, cols), lambda i, p: (p[0], p[1] * nr + i, 0)),
                pl.BlockSpec((None, tr, cols), lambda i, p: (p[0], i, 0)),
                pl.BlockSpec((3, tr, cols), lambda i, p: (0, i, 0))]
    args = [place, t, l, rcv]
    aliases = {}
    if into is not None:
        in_specs.append(pl.BlockSpec(memory_space=pl.ANY))
        args.append(into)
        aliases = {4: 0}
    return _pc(
        body, name=name,
        grid_spec=pltpu.PrefetchScalarGridSpec(
            num_scalar_prefetch=1, grid=(nr,), in_specs=in_specs,
            out_specs=pl.BlockSpec((tr, cols), lambda i, p: (2 * nr * layer + p[1] * nr + i, 0))),
        out_shape=jax.ShapeDtypeStruct((layers * r, cols), F32), input_output_aliases=aliases,
        compiler_params=_params(("parallel",)),
    )(*args)


def _share_halves(hs, name):
    nt = len(hs)
    n = sum(layers for _, layers in hs)

    def body(*refs):
        h_refs, send_sems, recv_sems = refs[:nt], refs[-2], refs[-1]
        x, y, c, _ = _place()
        cps = []
        for k, (_, layers) in enumerate(hs):
            rh = h_refs[k].shape[0] // (2 * layers)
            for layer in range(layers):
                half = h_refs[k].at[pl.ds((2 * layer + c) * rh, rh), :]
                cps.append(pltpu.make_async_remote_copy(
                    src_ref=half, dst_ref=half, send_sem=send_sems.at[len(cps)], recv_sem=recv_sems.at[len(cps)],
                    device_id=(x, y, 1 - c), device_id_type=MESH))
        for cp in cps:
            cp.start()
        for cp in cps:
            cp.wait()

    return _pc(
        body, name=name, in_specs=[HBM] * nt, out_specs=[HBM] * nt,
        out_shape=[jax.ShapeDtypeStruct(h.shape, F32) for h, _ in hs],
        input_output_aliases={k: k for k in range(nt)},
        scratch_shapes=[pltpu.SemaphoreType.DMA((n,)), pltpu.SemaphoreType.DMA((n,))],
    )(*[h for h, _ in hs])


def _split_start(name, bufs, ride, n, copies_of):
    nb = len(bufs)

    def body(*refs):
        sems = refs[nb + 1:nb + 1 + 2 * n]
        for cp in copies_of(refs[:nb], sems[:n], sems[n:]):
            cp.start()

    outs = _pc(
        body, name=name,
        out_shape=(pltpu.SemaphoreType.DMA(()),) * (2 * n) + tuple(pltpu.HBM(b.shape, b.dtype) for b in bufs)
        + (pltpu.HBM(ride.shape, ride.dtype),),
        in_specs=(HBM,) * (nb + 1), out_specs=(SEM,) * (2 * n) + (HBM,) * (nb + 1),
        input_output_aliases={k: 2 * n + k for k in range(nb + 1)},
        compiler_params=pltpu.CompilerParams(has_side_effects=EFFECT),
    )(*[pltpu.with_memory_space_constraint(b, pltpu.HBM) for b in bufs], pltpu.with_memory_space_constraint(ride, pltpu.HBM))
    return list(outs[:2 * n]), list(outs[2 * n:2 * n + nb]), outs[-1]


def _split_wait(name, bufs, sems, after, n, copies_of):
    nb = len(bufs)

    def body(*refs):
        s = refs[nb:nb + 2 * n]
        for cp in copies_of(refs[:nb], s[:n], s[n:]):
            cp.wait_send()
            cp.wait_recv()

    outs = _pc(
        body, name=name, out_shape=tuple(pltpu.HBM(b.shape, b.dtype) for b in bufs),
        in_specs=(HBM,) * nb + (SEM,) * (2 * n) + (pl.BlockSpec(memory_space=pl.ANY),), out_specs=(HBM,) * nb,
        input_output_aliases={k: k for k in range(nb)},
        compiler_params=pltpu.CompilerParams(has_side_effects=EFFECT),
    )(*bufs, *sems, after)
    return list(outs)


def _swap_copies(nt):
    def copies_of(refs, send, recv):
        x, y, c, _ = _place()
        cps = []
        for k in range(nt):
            rh = refs[k].shape[1] // 2
            cps.append(pltpu.make_async_remote_copy(
                src_ref=refs[k].at[:, pl.ds((1 - c) * rh, rh), :], dst_ref=refs[nt + k],
                send_sem=send[k], recv_sem=recv[k], device_id=(x, y, 1 - c), device_id_type=MESH))
        return cps
    return copies_of


def _exchange_copies(nt):
    def copies_of(refs, send, recv):
        x, y, c, chips = _place()
        cps = []
        for t in range(nt):
            for k, chip in enumerate(chips):
                cps.append(pltpu.make_async_remote_copy(
                    src_ref=refs[t].at[2 * chip[0] + chip[1]], dst_ref=refs[nt + t].at[k],
                    send_sem=send[3 * t + k], recv_sem=recv[3 * t + k], device_id=(*chip, c), device_id_type=MESH))
        return cps
    return copies_of


class _StagedReduce:
    def __init__(self, place):
        self.place = place
        self.groups = {}
        self.halves = {}

    @staticmethod
    def slab(t, r):
        return t.reshape(N_CHIPS, r, t.size // (N_CHIPS * r))

    def begin(self, g, grads, ride):
        names = list(grads)
        ts = [self.slab(t, r) for t, r in grads.values()]
        lands = [lax.empty((N_CHIPS, t.shape[1] // 2, t.shape[2]), F32) for t in ts]
        sems, bufs, ride = _split_start(f"grad_swap_start_{g}", ts + lands, ride, len(ts), _swap_copies(len(ts)))
        self.groups[g] = dict(names=names, bufs=bufs, sems=sems)
        return ride

    def advance(self, g, after, ride):
        st = self.groups[g]
        nt = len(st["names"])
        bufs = _split_wait(f"grad_swap_wait_{g}", st["bufs"], st["sems"], after, nt, _swap_copies(nt))
        st["ts"], st["ls"] = bufs[:nt], bufs[nt:]
        ps = [_pair_sum(t, l, self.place, f"pair_sum_{n}") for t, l, n in zip(st["ts"], st["ls"], st["names"])]
        lands = [lax.empty((3,) + p.shape[1:], BF16) for p in ps]
        st["sems"], st["bufs"], ride = _split_start(f"grad_exchange_start_{g}", ps + lands, ride, 3 * nt, _exchange_copies(nt))
        return ride

    def finish(self, g, after):
        st = self.groups[g]
        nt = len(st["names"])
        bufs = _split_wait(f"grad_exchange_wait_{g}", st["bufs"], st["sems"], after, 3 * nt, _exchange_copies(nt))
        for t, l, r, n in zip(st["ts"], st["ls"], bufs[nt:], st["names"]):
            if n[-1] in "01":
                self.halves[n[:-1]] = _final_sum(t, l, r, self.place, f"final_sum_{n}", layer=int(n[-1]), layers=2,
                                                 into=self.halves.get(n[:-1]))
            else:
                self.halves[n] = _final_sum(t, l, r, self.place, f"final_sum_{n}")


def _allgather_small(v):
    rows = v.shape[0]

    def body(v_ref, out_ref, send_sems, recv_sems):
        x, y, c, _ = _place()
        me = 4 * x + 2 * y + c
        out_ref[me] = v_ref[...]
        cps = []
        for k in range(1, 8):
            fx, fy, fc = (k >> 2) & 1, (k >> 1) & 1, k & 1
            to = (1 - x if fx else x, 1 - y if fy else y, 1 - c if fc else c)
            cps.append(pltpu.make_async_remote_copy(
                src_ref=v_ref, dst_ref=out_ref.at[me], send_sem=send_sems.at[k - 1], recv_sem=recv_sems.at[k - 1],
                device_id=to, device_id_type=MESH))
        for cp in cps:
            cp.start()
        for cp in cps:
            cp.wait()

    return _pc(
        body, name="allgather_small",
        in_specs=[pl.BlockSpec(memory_space=pltpu.VMEM)], out_specs=pl.BlockSpec(memory_space=pltpu.VMEM),
        out_shape=jax.ShapeDtypeStruct((8, rows, LANES), F32),
        scratch_shapes=[pltpu.SemaphoreType.DMA((7,)), pltpu.SemaphoreType.DMA((7,))],
    )(v)


def _adamw_math(w, g, m, v):
    m = ADAM_B1 * m + (1.0 - ADAM_B1) * g
    v = ADAM_B2 * v + (1.0 - ADAM_B2) * (g * g)
    m_hat = m / (1.0 - ADAM_B1 ** ADAM_STEP)
    v_hat = v / (1.0 - ADAM_B2 ** ADAM_STEP)
    return -ADAM_LR * (m_hat / (jnp.sqrt(v_hat) + ADAM_EPS) + ADAM_WD * w), m, v


def _adamw(w, g, m, v, name):
    r, cols = w.shape
    tr = min(r, 256)

    def body(w_ref, g_ref, m_ref, v_ref, d_ref, mo_ref, vo_ref):
        d, mn, vn = _adamw_math(w_ref[...], g_ref[...], m_ref[...], v_ref[...])
        d_ref[...] = d
        mo_ref[...] = mn
        vo_ref[...] = vn

    row = pl.BlockSpec((tr, cols), lambda i: (i, 0))
    return _pc(
        body, name=name, grid=(r // tr,), in_specs=[row] * 4, out_specs=[row] * 3,
        out_shape=[jax.ShapeDtypeStruct((r, cols), F32)] * 3,
        compiler_params=_params(("parallel",)),
    )(w, g, m, v)


def _adamw_small(w, gathered, m, v):
    rows = w.shape[0]

    def body(w_ref, g_ref, m_ref, v_ref, go_ref, d_ref, mo_ref, vo_ref):
        g = g_ref[0]
        for k in range(1, 8):
            g = g + g_ref[k]
        d, mn, vn = _adamw_math(w_ref[...], g, m_ref[...], v_ref[...])
        go_ref[...] = g
        d_ref[...] = d
        mo_ref[...] = mn
        vo_ref[...] = vn

    return _pc(
        body, name="adamw_small",
        out_shape=[jax.ShapeDtypeStruct((rows, LANES), F32)] * 4,
    )(w, gathered, m, v)


_BIAS_ROWS = 32


def _own_slot(flat, chip):
    return lax.dynamic_update_slice(lax.empty((N_CHIPS,) + flat.shape, flat.dtype), flat[None], (chip, 0, 0))


def _pack_first(hyb_w_in, hyb_w_out):
    return jnp.concatenate([t.astype(BF16).reshape(-1, 1024) for t in (hyb_w_in, hyb_w_out)], axis=0)


def _unpack_first(g):
    return {"hyb_w_in": g[:, 0:768, :].reshape(N_CHIPS, 1024, 768), "hyb_w_out": g[:, 768:1024, :].reshape(1024, 1024)}


def _pack_rest(mlp_w_up, mlp_w_down, swa_w_qkv, swa_w_out, swa_b_qkv):
    parts = [t.astype(BF16).reshape(-1, 1024) for t in (mlp_w_up, mlp_w_down, swa_w_qkv, swa_w_out)]
    bias = lax.bitcast_convert_type(swa_b_qkv.reshape(384), BF16).reshape(1, 768)
    bias = jnp.pad(bias, ((0, _BIAS_ROWS - 1), (0, 256)))
    return jnp.concatenate(parts + [bias], axis=0)


def _unpack_rest(g):
    W = {
        "packed": g,
        "swa_w_qkv": g[:, 4096:4480, :].reshape(N_CHIPS, 1024, 384),
        "swa_w_out": g[:, 4480:4736, :].reshape(1024, 1024),
    }
    bias = lax.bitcast_convert_type(g[:, 4736, :768].reshape(N_CHIPS, 384, 2), F32).reshape(1536)
    return W, bias


_SMALL = (("norm_mix", 16), ("norm_mlp", 16), ("ret_gn_gain", 4), ("dil_q_gain", 1), ("dil_k_gain", 1),
          ("swa_b_qkv", 12), ("swa_q_gain", 1), ("swa_k_gain", 1), ("swa_sinks", 1))
_SUBLANES = 8


def _slot(r):
    return -(-r // _SUBLANES) * _SUBLANES


def _pack_small(d):
    return jnp.concatenate([jnp.pad(d[n].reshape(r, LANES), ((0, _slot(r) - r), (0, 0))) for n, r in _SMALL], axis=0)


def _unpack_small(p):
    out, o = {}, 0
    for n, r in _SMALL:
        out[n] = p[o:o + r]
        o += _slot(r)
    return out


def kernel(x, positions, norm_mix, norm_mlp, mlp_w_up, mlp_w_down, hyb_w_in, hyb_w_out, ret_gn_gain, dil_q_gain, dil_k_gain, swa_w_qkv, swa_b_qkv, swa_w_out, swa_q_gain, swa_k_gain, swa_sinks, loss_target, m_norm_mix, m_norm_mlp, m_mlp_w_up, m_mlp_w_down, m_hyb_w_in, m_hyb_w_out, m_ret_gn_gain, m_dil_q_gain, m_dil_k_gain, m_swa_w_qkv, m_swa_b_qkv, m_swa_w_out, m_swa_q_gain, m_swa_k_gain, m_swa_sinks, v_norm_mix, v_norm_mlp, v_mlp_w_up, v_mlp_w_down, v_hyb_w_in, v_hyb_w_out, v_ret_gn_gain, v_dil_q_gain, v_dil_k_gain, v_swa_w_qkv, v_swa_b_qkv, v_swa_w_out, v_swa_q_gain, v_swa_k_gain, v_swa_sinks):
    ax, ay, ac = lax.axis_index("x"), lax.axis_index("y"), lax.axis_index("c")
    chip = 2 * ax + ay
    place = jnp.stack([chip, ac]).astype(jnp.int32)
    S = x.shape[1]

    first = _allgather_shards(_own_slot(_pack_first(hyb_w_in[0], hyb_w_out[0]), chip))
    rest = _own_slot(_pack_rest(mlp_w_up, mlp_w_down, swa_w_qkv[0], swa_w_out[0], swa_b_qkv[0]), chip)
    *sems, rest, first = _gather_start(rest, first)

    def rest_of(after):
        return _unpack_rest(_gather_handover(_gather_wait(rest, sems, after)))

    P = dict(norm_mix=norm_mix, norm_mlp=norm_mlp, ret_gn_gain=ret_gn_gain, dil_q_gain=dil_q_gain, dil_k_gain=dil_k_gain,
             swa_q_gain=swa_q_gain, swa_k_gain=swa_k_gain, swa_sinks=swa_sinks)

    red = _StagedReduce(place)
    loss_l, grad_x, gp = _local_step(x[0], positions.reshape(S, 1), loss_target[0], _unpack_first(first), rest_of, P, red)
    loss = lax.psum(loss_l[0, 0], ("x", "y", "c"))

    params = dict(mlp_w_up=(mlp_w_up, m_mlp_w_up, v_mlp_w_up), mlp_w_down=(mlp_w_down, m_mlp_w_down, v_mlp_w_down),
                  hyb_w_in=(hyb_w_in, m_hyb_w_in, v_hyb_w_in), hyb_w_out=(hyb_w_out, m_hyb_w_out, v_hyb_w_out),
                  swa_w_qkv=(swa_w_qkv, m_swa_w_qkv, v_swa_w_qkv), swa_w_out=(swa_w_out, m_swa_w_out, v_swa_w_out))
    big = {}

    def update(names, share_name):
        hs = [(red.halves[n], params[n][0].shape[0]) for n in names]
        for n, g in zip(names, _share_halves(hs, share_name)):
            rows = g.shape[0]
            w, m, v = (t.reshape(rows, -1) for t in params[n])
            big[n] = [t.reshape(params[n][0].shape) for t in (g,) + tuple(_adamw(w, g, m, v, f"adamw_{n}"))]

    names = ["mlp_w_up", "mlp_w_down", "hyb_w_out", "swa_w_qkv", "swa_w_out"]
    red.halves[names[0]] = red.advance("win", grad_x, red.halves[names[0]])
    update(names, "grad_share_halves")
    red.finish("win", big[names[-1]][1])
    update(["hyb_w_in"], "grad_share_last")

    gsm = dict(gp)
    gsm["swa_sinks"] = jnp.pad(gp["swa_sinks"].reshape(16, HEAD)[:, 0], (0, LANES - 16))
    gathered = _allgather_small(_pack_small(gsm))

    def small_pack(norm_mix, norm_mlp, gn, dq, dk, b, sq, sk, sinks):
        dup = lambda t: jnp.tile(t.reshape(1, HEAD), (1, 2))
        bias = lax.dynamic_update_slice(jnp.zeros((12, LANES), F32), b.reshape(3, LANES), (3 * chip, 0))
        return _pack_small(dict(norm_mix=norm_mix, norm_mlp=norm_mlp, ret_gn_gain=gn, dil_q_gain=dup(dq), dil_k_gain=dup(dk),
                                swa_b_qkv=bias, swa_q_gain=dup(sq), swa_k_gain=dup(sk),
                                swa_sinks=jnp.pad(sinks.reshape(16), (0, LANES - 16))))

    pw = small_pack(norm_mix, norm_mlp, ret_gn_gain, dil_q_gain, dil_k_gain, swa_b_qkv, swa_q_gain, swa_k_gain, swa_sinks)
    pm = small_pack(m_norm_mix, m_norm_mlp, m_ret_gn_gain, m_dil_q_gain, m_dil_k_gain, m_swa_b_qkv, m_swa_q_gain, m_swa_k_gain, m_swa_sinks)
    pv = small_pack(v_norm_mix, v_norm_mlp, v_ret_gn_gain, v_dil_q_gain, v_dil_k_gain, v_swa_b_qkv, v_swa_q_gain, v_swa_k_gain, v_swa_sinks)
    small = [_unpack_small(t) for t in _adamw_small(pw, gathered, pm, pv)]

    def small_out(n, k):
        t = small[k][n]
        if n in ("norm_mix", "norm_mlp"):
            return t.reshape(2, D_MODEL)
        if n == "ret_gn_gain":
            return t.reshape(1, RET_HEADS, 128)
        if n == "swa_b_qkv":
            return lax.dynamic_slice(t, (3 * chip, 0), (3, LANES)).reshape(1, 384)
        if n == "swa_sinks":
            return t[0, :16].reshape(1, 16)
        return t[0, :HEAD].reshape(1, HEAD)

    order = ["norm_mix", "norm_mlp", "mlp_w_up", "mlp_w_down", "hyb_w_in", "hyb_w_out", "ret_gn_gain", "dil_q_gain",
             "dil_k_gain", "swa_w_qkv", "swa_b_qkv", "swa_w_out", "swa_q_gain", "swa_k_gain", "swa_sinks"]
    is_big = {"mlp_w_up", "mlp_w_down", "hyb_w_in", "hyb_w_out", "swa_w_qkv", "swa_w_out"}
    outs = [loss, grad_x[None]]
    for k in range(4):
        outs += [big[n][k] if n in is_big else small_out(n, k) for n in order]
    return tuple(outs)
```

```python
import functools
import math

import numpy as np
import jax
import jax.numpy as jnp
from jax import lax
from jax.experimental import pallas as pl
from jax.experimental.pallas import tpu as pltpu

F32, BF16 = jnp.float32, jnp.bfloat16
HIGHEST = lax.Precision.HIGHEST
MESH = pl.DeviceIdType.MESH

LANES = 128
VMEM_LIMIT = 48 << 20
D_MODEL = 1024
D_FF = 4096
HEAD = 64
EPS = 1e-6
BLK = 128
RET_HEADS = 4
RET_THETA = 10000.0
ROPE_THETA = 500000.0
ROPE_DIMS = 16
DIL_PATTERNS = ((128, 1), (512, 4), (2048, 16))
SWA_DIST = 127
N_CHIPS = 4
ADAM_LR, ADAM_B1, ADAM_B2, ADAM_EPS, ADAM_WD, ADAM_STEP = 0.001, 0.9, 0.999, 1e-08, 0.01, 10

_LOG_GAMMA = [float(np.log1p(-np.exp2(np.float32(-5.0 - h)))) for h in range(RET_HEADS)]


def _pc(body, **kw):
    return pl.pallas_call(body, **kw)


def _params(sem):
    return pltpu.CompilerParams(dimension_semantics=sem, vmem_limit_bytes=VMEM_LIMIT)


def _matmul(a, b, *, dims, tm, tn, tk, outs, name, epilogue=None, extras=(), b_cs=False, b_rs=0, b_row0=0, b_rows=0,
            o_cs=0, a_pro=None):
    if dims == "nn":
        M, K = a.shape
        N = b.shape[0] * b.shape[2] if b_cs else b.shape[1]
        a_spec = pl.BlockSpec((tm, tk), lambda i, j, k: (i, k))
        if b_cs:
            npt = b.shape[2] // tn
            b_spec = pl.BlockSpec((None, tk, tn), lambda i, j, k: (j // npt, k + b_row0, j % npt))
        elif b_rs:
            K, N, kps = b.shape[0] * b_rs, b.shape[2], b_rs // tk
            b_spec = pl.BlockSpec((None, tk, tn), lambda i, j, k: (k // kps, b_row0 + k % kps, j))
        else:
            b_spec = pl.BlockSpec((tk, tn), lambda i, j, k: (k, j))
        contract = (((1,), (0,)), ((), ()))
    elif dims == "nt":
        M, K = a.shape
        N = (b_rows or b.shape[1]) if b_cs else b.shape[0]
        a_spec = pl.BlockSpec((tm, tk), lambda i, j, k: (i, k))
        if b_cs:
            kpt = b.shape[2] // tk
            b_spec = pl.BlockSpec((None, tn, tk), lambda i, j, k: (k // kpt, j + b_row0, k % kpt))
        elif b_rs:
            N, jps = b.shape[0] * b_rs, b_rs // tn
            b_spec = pl.BlockSpec((None, tn, tk), lambda i, j, k: (j // jps, b_row0 + j % jps, k))
        else:
            b_spec = pl.BlockSpec((tn, tk), lambda i, j, k: (j, k))
        contract = (((1,), (1,)), ((), ()))
    else:
        K, M = a.shape
        N = b.shape[1]
        a_spec = pl.BlockSpec((tk, tm), lambda i, j, k: (k, i))
        b_spec = pl.BlockSpec((tk, tn), lambda i, j, k: (k, j))
        contract = (((0,), (0,)), ((), ()))
    assert M % tm == 0 and N % tn == 0 and K % tk == 0, (name, M, N, K, tm, tn, tk)
    nk = K // tk
    ex_specs = []
    for arr, kind in extras:
        if kind == "mn":
            ex_specs.append(pl.BlockSpec((tm, tn), lambda i, j, k: (i, j)))
        elif kind == "n":
            ex_specs.append(pl.BlockSpec((1, tn), lambda i, j, k: (0, j)))
        elif kind == "full":
            ex_specs.append(pl.BlockSpec(arr.shape, lambda i, j, k, nd=arr.ndim: (0,) * nd))
        else:
            ex_specs.append(pl.BlockSpec((tm, kind), lambda i, j, k: (i, 0)))
    if o_cs:
        n_sh = N // o_cs
        opt = n_sh // tn
        o_shape = (o_cs, M, n_sh)
        o_spec = pl.BlockSpec((None, tm, tn), lambda i, j, k: (j // opt, i, j % opt))
    else:
        o_shape = (M, N)
        o_spec = pl.BlockSpec((tm, tn), lambda i, j, k: (i, j))
    o_specs, o_shapes, summed = [], [], []
    for o in outs:
        if isinstance(o, tuple) and o[0] == "colsum":
            assert N == tn
            o_specs.append(pl.BlockSpec((1, tn), lambda i, j, k: (0, j)))
            o_shapes.append(jax.ShapeDtypeStruct((1, N), F32))
            summed.append(True)
        elif isinstance(o, tuple):
            o_specs.append(pl.BlockSpec((tm, o[1]), lambda i, j, k: (i, 0)))
            o_shapes.append(jax.ShapeDtypeStruct((M, o[1]), o[0]))
            summed.append(False)
        else:
            o_specs.append(o_spec)
            o_shapes.append(jax.ShapeDtypeStruct(o_shape, o))
            summed.append(False)
    n_ex, n_out = len(extras), len(outs)
    if epilogue is None:
        epilogue = lambda acc: (acc,)

    def body(a_ref, b_ref, *rest):
        ex, o_refs, acc = rest[:n_ex], rest[n_ex:n_ex + n_out], rest[-1]
        i, k = pl.program_id(0), pl.program_id(2)

        @pl.when(k == 0)
        def _():
            acc[...] = jnp.zeros_like(acc)

        av = a_ref[...] if a_pro is None else a_pro(a_ref[...])
        acc[...] += lax.dot_general(av.astype(BF16), b_ref[...].astype(BF16), contract, preferred_element_type=F32)

        @pl.when(k == nk - 1)
        def _():
            vals = epilogue(acc[...], *[e[...] for e in ex])
            for r, v, sm in zip(o_refs, vals, summed):
                if sm:
                    @pl.when(i == 0)
                    def _(r=r):
                        r[...] = jnp.zeros_like(r)

                    r[...] += v
                else:
                    r[...] = v.astype(r.dtype)

    res = _pc(
        body, name=name, grid=(M // tm, N // tn, nk),
        in_specs=[a_spec, b_spec] + ex_specs, out_specs=o_specs, out_shape=o_shapes,
        scratch_shapes=[pltpu.VMEM((tm, tn), F32)],
        compiler_params=_params(("arbitrary" if any(summed) else "parallel", "parallel", "arbitrary")),
    )(a, b, *[e for e, _ in extras])
    return res[0] if n_out == 1 else res


def _roll(x, s):
    return pltpu.roll(x, s % LANES, 1)


def _rope(x, A, B, C, half):
    return x * A + _roll(x, LANES - half) * B + _roll(x, half) * C


def _rope_t(g, A, B, C, half):
    return g * A + _roll(g * B, half) + _roll(g * C, LANES - half)


def _gmean(x, G):
    hi = x.astype(BF16)
    lo = (x - hi.astype(F32)).astype(BF16)
    Gb = G.astype(BF16)
    return jnp.dot(hi, Gb, preferred_element_type=F32) + jnp.dot(lo, Gb, preferred_element_type=F32)


def _head_mask(shape, half):
    lane = lax.broadcasted_iota(jnp.int32, shape, len(shape) - 1)
    return (lane >= HEAD) if half else (lane < HEAD)


def _group_matrix():
    i = np.arange(LANES)
    return jnp.asarray((i[:, None] // HEAD == i[None, :] // HEAD).astype(np.float32) / HEAD)


def _rope_inv():
    l = np.arange(LANES) % HEAD
    inv_r = np.power(np.float32(RET_THETA), -(l % 32).astype(np.float32) * np.float32(2.0 / HEAD))
    hp = ROPE_DIMS // 2
    inv_p = np.power(np.float32(ROPE_THETA), -(l % hp).astype(np.float32) * np.float32(2.0 / ROPE_DIMS))
    inv_p = np.where(l < ROPE_DIMS, inv_p, 0.0)
    return jnp.asarray(np.stack([inv_r, inv_p]).astype(np.float32))


def _tables(pos_col):
    S = pos_col.shape[0]
    tm = 512
    hp = ROPE_DIMS // 2

    def body(p_ref, inv_ref, o_ref):
        p = p_ref[...].astype(F32)
        lane = lax.broadcasted_iota(jnp.int32, (tm, LANES), 1) % HEAD
        ang = p * inv_ref[0:1, :]
        c, s = jnp.cos(ang), jnp.sin(ang)
        o_ref[:, 0:128] = c
        o_ref[:, 128:256] = jnp.where(lane < 32, -s, 0.0)
        o_ref[:, 256:384] = jnp.where(lane >= 32, s, 0.0)
        ang = p * inv_ref[1:2, :]
        c, s = jnp.cos(ang), jnp.sin(ang)
        o_ref[:, 384:512] = c
        o_ref[:, 512:640] = jnp.where(lane < hp, -s, 0.0)
        o_ref[:, 640:768] = jnp.where((lane >= hp) & (lane < ROPE_DIMS), s, 0.0)

    return _pc(
        body, name="rope_tables", grid=(S // tm,),
        in_specs=[pl.BlockSpec((tm, 1), lambda i: (i, 0)), pl.BlockSpec((2, LANES), lambda i: (0, 0))],
        out_specs=pl.BlockSpec((tm, 768), lambda i: (i, 0)),
        out_shape=jax.ShapeDtypeStruct((S, 768), F32),
        compiler_params=_params(("parallel",)),
    )(pos_col, _rope_inv())


def _tab(tab_ref, which):
    o = 384 * which
    return tab_ref[:, o:o + 128], tab_ref[:, o + 128:o + 256], tab_ref[:, o + 256:o + 384]


def _rms_fwd(x, g, name):
    S, Dm = x.shape
    tm = 512

    def body(x_ref, g_ref, h_ref):
        xv = x_ref[...]
        r = lax.rsqrt(jnp.mean(xv * xv, axis=-1, keepdims=True) + EPS)
        h_ref[...] = (xv * r * g_ref[...]).astype(BF16)

    return _pc(
        body, name=name, grid=(S // tm,),
        in_specs=[pl.BlockSpec((tm, Dm), lambda i: (i, 0)), pl.BlockSpec((1, Dm), lambda i: (0, 0))],
        out_specs=pl.BlockSpec((tm, Dm), lambda i: (i, 0)),
        out_shape=jax.ShapeDtypeStruct((S, Dm), BF16),
        compiler_params=_params(("parallel",)),
    )(x, g.reshape(1, Dm))


def _rms_bwd(x, g, dh, dres, name):
    S, Dm = x.shape
    tm = 512

    def body(x_ref, g_ref, dh_ref, dres_ref, dx_ref, dxb_ref, dg_ref):
        xv, dhv = x_ref[...], dh_ref[...]
        r = lax.rsqrt(jnp.mean(xv * xv, axis=-1, keepdims=True) + EPS)
        t = dhv * g_ref[...]
        dx = dres_ref[...] + r * t - xv * (r * r * r) * jnp.mean(xv * t, axis=-1, keepdims=True)
        dx_ref[...] = dx
        dxb_ref[...] = dx.astype(BF16)

        @pl.when(pl.program_id(0) == 0)
        def _():
            dg_ref[...] = jnp.zeros_like(dg_ref)

        dg_ref[...] += jnp.sum(dhv * xv * r, axis=0, keepdims=True)

    row = pl.BlockSpec((tm, Dm), lambda i: (i, 0))
    vec = pl.BlockSpec((1, Dm), lambda i: (0, 0))
    return _pc(
        body, name=name, grid=(S // tm,),
        in_specs=[row, vec, row, row], out_specs=[row, row, vec],
        out_shape=[jax.ShapeDtypeStruct((S, Dm), F32), jax.ShapeDtypeStruct((S, Dm), BF16),
                   jax.ShapeDtypeStruct((1, Dm), F32)],
        compiler_params=_params(("arbitrary",)),
    )(x, g.reshape(1, Dm), dh, dres)


def _hn_fwd(x, gain, G):
    r = lax.rsqrt(_gmean(x * x, G) + EPS)
    return x * r * gain


def _hn_bwd(x, gain, dy, G):
    r = lax.rsqrt(_gmean(x * x, G) + EPS)
    t = dy * gain
    dx = r * t - x * (r * r * r) * _gmean(x * t, G)
    return dx, jnp.sum(dy * x * r, axis=0, keepdims=True)


def _fold_halves(v):
    return v + _roll(v, HEAD)


def _even_pre_fwd(proj, tab, qg, kg):
    S = proj.shape[0]
    tm = 256

    def body(p_ref, tab_ref, qg_ref, kg_ref, g_ref, rq_ref, rk_ref, rv_ref, dq_ref, dk_ref, dv_ref):
        Ar, Br, Cr = _tab(tab_ref, 0)
        Ap, Bp, Cp = _tab(tab_ref, 1)
        G = g_ref[...]
        for c in range(2):
            sl = slice(c * 128, (c + 1) * 128)
            rq_ref[:, sl] = _rope(p_ref[:, c * 128:(c + 1) * 128], Ar, Br, Cr, 32).astype(BF16)
            rk_ref[:, sl] = (_rope(p_ref[:, 256 + c * 128:256 + (c + 1) * 128], Ar, Br, Cr, 32) * 0.125).astype(BF16)
        rv_ref[...] = p_ref[:, 512:1024].astype(BF16)
        for c in range(4):
            sl = slice(c * 128, (c + 1) * 128)
            q = _hn_fwd(p_ref[:, 1536 + c * 128:1536 + (c + 1) * 128], qg_ref[...], G)
            dq_ref[:, sl] = _rope(q, Ap, Bp, Cp, 8).astype(BF16)
            k = _hn_fwd(p_ref[:, 2048 + c * 128:2048 + (c + 1) * 128], kg_ref[...], G)
            dk_ref[:, sl] = _rope(k, Ap, Bp, Cp, 8).astype(BF16)
        dv_ref[...] = p_ref[:, 2560:3072].astype(BF16)

    row = lambda w: pl.BlockSpec((tm, w), lambda i: (i, 0))
    vec = pl.BlockSpec((1, LANES), lambda i: (0, 0))
    return _pc(
        body, name="even_pre_fwd", grid=(S // tm,),
        in_specs=[row(3072), row(768), vec, vec, pl.BlockSpec((LANES, LANES), lambda i: (0, 0))],
        out_specs=[row(256), row(256), row(512), row(512), row(512), row(512)],
        out_shape=[jax.ShapeDtypeStruct((S, w), BF16) for w in (256, 256, 512, 512, 512, 512)],
        compiler_params=_params(("parallel",)),
    )(proj, tab, qg, kg, _group_matrix())


def _even_pre_bwd(proj, tab, qg, kg, drq, drk, drv, drg, dqs, dks, dvs):
    S = proj.shape[0]
    tm = 256
    npat = len(dqs)

    def body(p_ref, tab_ref, qg_ref, kg_ref, g_ref, drq_ref, drk_ref, drv_ref, drg_ref, *rest):
        dq_refs, dk_refs, dv_refs = rest[:npat], rest[npat:2 * npat], rest[2 * npat:3 * npat]
        dp_ref, dqg_ref, dkg_ref = rest[3 * npat:]
        Ar, Br, Cr = _tab(tab_ref, 0)
        Ap, Bp, Cp = _tab(tab_ref, 1)
        G = g_ref[...]
        for c in range(2):
            sl = slice(c * 128, (c + 1) * 128)
            dp_ref[:, c * 128:(c + 1) * 128] = _rope_t(drq_ref[:, sl], Ar, Br, Cr, 32).astype(BF16)
            dp_ref[:, 256 + c * 128:256 + (c + 1) * 128] = _rope_t(drk_ref[:, sl] * 0.125, Ar, Br, Cr, 32).astype(BF16)
        dp_ref[:, 512:1024] = drv_ref[...].astype(BF16)
        dp_ref[:, 1024:1536] = drg_ref[...].astype(BF16)
        accq = jnp.zeros((1, LANES), F32)
        acck = jnp.zeros((1, LANES), F32)
        for c in range(4):
            sl = slice(c * 128, (c + 1) * 128)
            g = dq_refs[0][:, sl]
            for r in dq_refs[1:]:
                g = g + r[:, sl]
            dx, dg = _hn_bwd(p_ref[:, 1536 + c * 128:1536 + (c + 1) * 128], qg_ref[...], _rope_t(g, Ap, Bp, Cp, 8), G)
            dp_ref[:, 1536 + c * 128:1536 + (c + 1) * 128] = dx.astype(BF16)
            accq = accq + dg
            g = dk_refs[0][:, sl]
            for r in dk_refs[1:]:
                g = g + r[:, sl]
            dx, dg = _hn_bwd(p_ref[:, 2048 + c * 128:2048 + (c + 1) * 128], kg_ref[...], _rope_t(g, Ap, Bp, Cp, 8), G)
            dp_ref[:, 2048 + c * 128:2048 + (c + 1) * 128] = dx.astype(BF16)
            acck = acck + dg
        g = dv_refs[0][...]
        for r in dv_refs[1:]:
            g = g + r[...]
        dp_ref[:, 2560:3072] = g.astype(BF16)

        @pl.when(pl.program_id(0) == 0)
        def _():
            dqg_ref[...] = jnp.zeros_like(dqg_ref)
            dkg_ref[...] = jnp.zeros_like(dkg_ref)

        dqg_ref[...] += _fold_halves(accq)
        dkg_ref[...] += _fold_halves(acck)

    row = lambda w: pl.BlockSpec((tm, w), lambda i: (i, 0))
    vec = pl.BlockSpec((1, LANES), lambda i: (0, 0))
    return _pc(
        body, name="even_pre_bwd", grid=(S // tm,),
        in_specs=[row(3072), row(768), vec, vec, pl.BlockSpec((LANES, LANES), lambda i: (0, 0)),
                  row(256), row(256), row(512), row(512)] + [row(512)] * (3 * npat),
        out_specs=[row(3072), vec, vec],
        out_shape=[jax.ShapeDtypeStruct((S, 3072), BF16), jax.ShapeDtypeStruct((1, LANES), F32),
                   jax.ShapeDtypeStruct((1, LANES), F32)],
        compiler_params=_params(("arbitrary",)),
    )(proj, tab, qg, kg, _group_matrix(), drq, drk, drv, drg, *dqs, *dks, *dvs)


def _ret_consts(pair, half):
    lg = jnp.where(pair == 0, _LOG_GAMMA[half], _LOG_GAMMA[2 + half]).astype(F32)
    i = lax.broadcasted_iota(jnp.int32, (BLK, BLK), 0)
    j = lax.broadcasted_iota(jnp.int32, (BLK, BLK), 1)
    diff = (i - j).astype(F32)
    decay = jnp.where(diff >= 0, jnp.exp(lg * jnp.maximum(diff, 0.0)), 0.0)
    t = lax.broadcasted_iota(jnp.int32, (BLK, 1), 0).astype(F32)
    xi = jnp.exp(lg * (t + 1.0))
    zeta = jnp.exp(lg * (BLK - 1.0 - t))
    cd = jnp.exp(jnp.full((1, 1), BLK, F32) * lg)
    return decay, xi, zeta, cd


RET_STEP = 8


def _ret_fwd(rq, rk, rv):
    S = rq.shape[0]
    nc = S // BLK
    rows = RET_STEP * BLK

    def body(q_ref, k_ref, v_ref, o_ref, st_ref, R):
        p, n = pl.program_id(0), pl.program_id(1)

        @pl.when(n == 0)
        def _():
            R[...] = jnp.zeros_like(R)

        consts = [_ret_consts(p, half) for half in range(2)]
        masks = [_head_mask((BLK, LANES), half) for half in range(2)]
        for ci in range(RET_STEP):
            rs = slice(ci * BLK, (ci + 1) * BLK)
            q2, k2 = q_ref[rs, :], k_ref[rs, :]
            for half in range(2):
                decay, xi, zeta, cd = consts[half]
                m = masks[half]
                qm = jnp.where(m, q2, jnp.zeros_like(q2))
                km = jnp.where(m, k2, jnp.zeros_like(k2))
                v = v_ref[rs, half * 128:(half + 1) * 128]
                Rb = R[half].astype(BF16)
                st_ref[ci, half] = Rb
                sc = lax.dot_general(qm, k2, (((1,), (1,)), ((), ())), preferred_element_type=F32) * decay
                o = jnp.dot(sc.astype(BF16), v, preferred_element_type=F32)
                o = o + jnp.dot(qm, Rb, preferred_element_type=F32) * xi
                o_ref[rs, half * 128:(half + 1) * 128] = o
                kz = (km.astype(F32) * zeta).astype(BF16)
                R[half] = R[half] * cd + lax.dot_general(kz, v, (((0,), (0,)), ((), ())), preferred_element_type=F32)

    return _pc(
        body, name="ret_fwd", grid=(2, nc // RET_STEP),
        in_specs=[pl.BlockSpec((rows, 128), lambda p, n: (n, p)), pl.BlockSpec((rows, 128), lambda p, n: (n, p)),
                  pl.BlockSpec((rows, 256), lambda p, n: (n, p))],
        out_specs=[pl.BlockSpec((rows, 256), lambda p, n: (n, p)),
                   pl.BlockSpec((None, RET_STEP, 2, 128, 128), lambda p, n: (p, n, 0, 0, 0))],
        out_shape=[jax.ShapeDtypeStruct((S, 512), F32), jax.ShapeDtypeStruct((2, nc, 2, 128, 128), BF16)],
        scratch_shapes=[pltpu.VMEM((2, 128, 128), F32)],
        compiler_params=_params(("parallel", "arbitrary")),
    )(rq, rk, rv)


def _ret_bwd(rq, rk, rv, states, do):
    S = rq.shape[0]
    nc = S // BLK
    ns = nc // RET_STEP
    rows = RET_STEP * BLK
    nt = (((1,), (1,)), ((), ()))
    tn = (((0,), (0,)), ((), ()))

    def body(q_ref, k_ref, v_ref, st_ref, do_ref, dq_ref, dk_ref, dv_ref, U):
        p, n = pl.program_id(0), pl.program_id(1)

        @pl.when(n == 0)
        def _():
            U[...] = jnp.zeros_like(U)

        consts = [_ret_consts(p, half) for half in range(2)]
        masks = [_head_mask((BLK, LANES), half) for half in range(2)]
        for ci in reversed(range(RET_STEP)):
            rs = slice(ci * BLK, (ci + 1) * BLK)
            q2, k2 = q_ref[rs, :], k_ref[rs, :]
            dq_acc = jnp.zeros((BLK, LANES), F32)
            dk_acc = jnp.zeros((BLK, LANES), F32)
            for half in range(2):
                decay, xi, zeta, cd = consts[half]
                m = masks[half]
                qm = jnp.where(m, q2, jnp.zeros_like(q2))
                km = jnp.where(m, k2, jnp.zeros_like(k2))
                v = v_ref[rs, half * 128:(half + 1) * 128]
                dob = do_ref[rs, half * 128:(half + 1) * 128].astype(BF16)
                Rb = st_ref[ci, half]
                Ub = U[half].astype(BF16)
                dsc = (lax.dot_general(dob, v, nt, preferred_element_type=F32) * decay).astype(BF16)
                xdo = (dob.astype(F32) * xi).astype(BF16)
                dq_acc += jnp.dot(dsc, km, preferred_element_type=F32) + lax.dot_general(xdo, Rb, nt, preferred_element_type=F32)
                dk_acc += lax.dot_general(dsc, qm, tn, preferred_element_type=F32) \
                    + lax.dot_general(v, Ub, nt, preferred_element_type=F32) * zeta
                sc = (lax.dot_general(qm, k2, nt, preferred_element_type=F32) * decay).astype(BF16)
                kz = (km.astype(F32) * zeta).astype(BF16)
                dv_ref[rs, half * 128:(half + 1) * 128] = lax.dot_general(sc, dob, tn, preferred_element_type=F32) \
                    + jnp.dot(kz, Ub, preferred_element_type=F32)
                U[half] = U[half] * cd + lax.dot_general(qm, xdo, tn, preferred_element_type=F32)
            dq_ref[rs, :] = dq_acc
            dk_ref[rs, :] = dk_acc

    rev = lambda w: pl.BlockSpec((rows, w), lambda p, n: (ns - 1 - n, p))
    return _pc(
        body, name="ret_bwd", grid=(2, ns),
        in_specs=[rev(128), rev(128), rev(256),
                  pl.BlockSpec((None, RET_STEP, 2, 128, 128), lambda p, n: (p, ns - 1 - n, 0, 0, 0)), rev(256)],
        out_specs=[rev(128), rev(128), rev(256)],
        out_shape=[jax.ShapeDtypeStruct((S, 256), F32), jax.ShapeDtypeStruct((S, 256), F32),
                   jax.ShapeDtypeStruct((S, 512), F32)],
        scratch_shapes=[pltpu.VMEM((2, 128, 128), F32)],
        compiler_params=_params(("parallel", "arbitrary")),
    )(rq, rk, rv, states, do)


def _col_of(b, m):
    return jnp.max(jnp.where(m, b, -jnp.inf), axis=1, keepdims=True)


def _attn_fwd(q, k, v, *, nq, max_dist, name, sinks=None, want_bf16=False):
    L, Ck = k.shape
    nb, ncol = L // BLK, Ck // LANES
    scale = HEAD ** -0.5
    has_sink = sinks is not None

    def body(*refs):
        q_ref, kp_ref, kc_ref, vp_ref, vc_ref = refs[:5]
        sk_ref = refs[5] if has_sink else None
        outs = refs[5 + has_sink:]
        n = pl.program_id(1)
        kcat = jnp.concatenate([kp_ref[...], kc_ref[...]], axis=0)
        vcat = jnp.concatenate([vp_ref[...], vc_ref[...]], axis=0)
        r = lax.broadcasted_iota(jnp.int32, (BLK, 2 * BLK), 0)
        c = lax.broadcasted_iota(jnp.int32, (BLK, 2 * BLK), 1)
        dist = r + BLK - c
        valid = (dist >= 0) & (dist <= max_dist) & ((c >= BLK) | (n > 0))
        for i in range(nq):
            q2 = q_ref[:, i * 128:(i + 1) * 128]
            o2 = jnp.zeros((BLK, LANES), F32)
            l2 = jnp.zeros((BLK, LANES), F32)
            for half in range(2):
                m = _head_mask((BLK, LANES), half)
                qm = jnp.where(m, q2, jnp.zeros_like(q2))
                s = lax.dot_general(qm, kcat, (((1,), (1,)), ((), ())), preferred_element_type=F32) * scale
                s = jnp.where(valid, s, -jnp.inf)
                mx = jnp.max(s, axis=1, keepdims=True)
                if has_sink:
                    snk = _col_of(sk_ref[:, i * 128:(i + 1) * 128], _head_mask((1, LANES), half))
                    mx = jnp.maximum(mx, snk)
                pr = jnp.exp(s - mx)
                den = jnp.sum(pr, axis=1, keepdims=True)
                if has_sink:
                    den = den + jnp.exp(snk - mx)
                pv = jnp.dot(pr.astype(BF16), vcat, preferred_element_type=F32) / den
                o2 = jnp.where(m, pv, o2)
                l2 = jnp.where(m, mx + jnp.log(den), l2)
            outs[0][:, i * 128:(i + 1) * 128] = o2
            outs[1][:, i * 128:(i + 1) * 128] = l2
            if want_bf16:
                outs[2][:, i * 128:(i + 1) * 128] = o2.astype(BF16)

    qspec = pl.BlockSpec((BLK, nq * 128), lambda j, n: (n, j))
    cur = pl.BlockSpec((BLK, 128), lambda j, n: (n, j))
    prev = pl.BlockSpec((BLK, 128), lambda j, n: (jnp.maximum(n - 1, 0), j))
    in_specs = [qspec, prev, cur, prev, cur]
    args = [q, k, k, v, v]
    if has_sink:
        in_specs.append(pl.BlockSpec((1, nq * 128), lambda j, n: (0, j)))
        args.append(sinks)
    out_dts = [F32, F32] + ([BF16] if want_bf16 else [])
    return _pc(
        body, name=name, grid=(ncol, nb), in_specs=in_specs,
        out_specs=[qspec] * len(out_dts),
        out_shape=[jax.ShapeDtypeStruct(q.shape, dt) for dt in out_dts],
        compiler_params=_params(("parallel", "parallel")),
    )(*args)


def _attn_bwd(q, k, v, o, lse, do, *, nq, max_dist, name, sinks=None):
    L, Ck = k.shape
    nb, ncol = L // BLK, Ck // LANES
    scale = HEAD ** -0.5
    has_sink = sinks is not None
    nt = (((1,), (1,)), ((), ()))
    tn = (((0,), (0,)), ((), ()))

    def body(*refs):
        (qc_ref, qn_ref, kp_ref, kc_ref, vp_ref, vc_ref, oc_ref, on_ref, lc_ref, ln_ref, dc_ref, dn_ref) = refs[:12]
        sk_ref = refs[12] if has_sink else None
        outs = refs[12 + has_sink:]
        dq_ref, dk_ref, dv_ref = outs[:3]
        n = pl.program_id(1)
        kc, vc = kc_ref[...], vc_ref[...]
        kcat = jnp.concatenate([kp_ref[...], kc], axis=0)
        vcat = jnp.concatenate([vp_ref[...], vc], axis=0)
        r = lax.broadcasted_iota(jnp.int32, (BLK, 2 * BLK), 0)
        c = lax.broadcasted_iota(jnp.int32, (BLK, 2 * BLK), 1)
        dist = r + BLK - c
        valid_q = (dist >= 0) & (dist <= max_dist) & ((c >= BLK) | (n > 0))
        r2 = lax.broadcasted_iota(jnp.int32, (2 * BLK, BLK), 0)
        c2 = lax.broadcasted_iota(jnp.int32, (2 * BLK, BLK), 1)
        dist2 = r2 - c2
        valid_k = (dist2 >= 0) & (dist2 <= max_dist) & ((r2 < BLK) | (n < nb - 1))
        dk_acc = jnp.zeros((BLK, LANES), F32)
        dv_acc = jnp.zeros((BLK, LANES), F32)
        for i in range(nq):
            sl = slice(i * 128, (i + 1) * 128)
            qcur, docur = qc_ref[:, sl], dc_ref[:, sl]
            qcat = jnp.concatenate([qcur, qn_ref[:, sl]], axis=0)
            docat = jnp.concatenate([docur, dn_ref[:, sl]], axis=0)
            ocat = jnp.concatenate([oc_ref[:, sl], on_ref[:, sl]], axis=0)
            lcat = jnp.concatenate([lc_ref[:, sl], ln_ref[:, sl]], axis=0)
            dq2 = jnp.zeros((BLK, LANES), F32)
            ds2 = jnp.zeros((1, LANES), F32)
            for half in range(2):
                m1 = _head_mask((BLK, LANES), half)
                m2 = _head_mask((2 * BLK, LANES), half)
                dom = jnp.where(m2, docat, 0.0)
                delta = jnp.sum(dom * ocat, axis=1, keepdims=True)
                lcol = _col_of(lcat, m2)
                domb = dom.astype(BF16)
                qmcat = jnp.where(m2, qcat, jnp.zeros_like(qcat))
                qm = qmcat[:BLK]
                s = lax.dot_general(qm, kcat, nt, preferred_element_type=F32) * scale
                pr = jnp.where(valid_q, jnp.exp(s - lcol[:BLK]), 0.0)
                dp = lax.dot_general(domb[:BLK], vcat, nt, preferred_element_type=F32)
                ds = (pr * (dp - delta[:BLK])).astype(BF16)
                dq2 = jnp.where(m1, jnp.dot(ds, kcat, preferred_element_type=F32) * scale, dq2)
                if has_sink:
                    snk = _col_of(sk_ref[:, sl], _head_mask((1, LANES), half))
                    contrib = jnp.sum(-jnp.exp(snk - lcol[:BLK]) * delta[:BLK], axis=0, keepdims=True)
                    ds2 = jnp.where(_head_mask((1, LANES), half), contrib, ds2)
                s = lax.dot_general(qmcat, kc, nt, preferred_element_type=F32) * scale
                pr = jnp.where(valid_k, jnp.exp(s - lcol), 0.0)
                dv_acc += lax.dot_general(pr.astype(BF16), domb, tn, preferred_element_type=F32)
                dp = lax.dot_general(domb, vc, nt, preferred_element_type=F32)
                ds = (pr * (dp - delta)).astype(BF16)
                dk_acc += lax.dot_general(ds, qmcat, tn, preferred_element_type=F32) * scale
            dq_ref[:, sl] = dq2
            if has_sink:
                @pl.when(n == 0)
                def _():
                    outs[3][:, sl] = jnp.zeros((1, LANES), F32)

                outs[3][:, sl] += ds2
        dk_ref[...] = dk_acc
        dv_ref[...] = dv_acc

    qcur = pl.BlockSpec((BLK, nq * 128), lambda j, n: (n, j))
    qnext = pl.BlockSpec((BLK, nq * 128), lambda j, n: (jnp.minimum(n + 1, nb - 1), j))
    cur = pl.BlockSpec((BLK, 128), lambda j, n: (n, j))
    prev = pl.BlockSpec((BLK, 128), lambda j, n: (jnp.maximum(n - 1, 0), j))
    in_specs = [qcur, qnext, prev, cur, prev, cur, qcur, qnext, qcur, qnext, qcur, qnext]
    args = [q, q, k, k, v, v, o, o, lse, lse, do, do]
    out_specs = [qcur, cur, cur]
    out_shape = [jax.ShapeDtypeStruct(q.shape, F32), jax.ShapeDtypeStruct(k.shape, F32), jax.ShapeDtypeStruct(k.shape, F32)]
    if has_sink:
        vec = pl.BlockSpec((1, nq * 128), lambda j, n: (0, j))
        in_specs.append(vec)
        args.append(sinks)
        out_specs.append(vec)
        out_shape.append(jax.ShapeDtypeStruct((1, q.shape[1]), F32))
    return _pc(
        body, name=name, grid=(ncol, nb), in_specs=in_specs, out_specs=out_specs, out_shape=out_shape,
        compiler_params=_params(("parallel", "arbitrary")),
    )(*args)


ATT_TILE = 2048


def _rows(ref, start, n, r):
    if r == 1:
        return ref[pl.ds(start, n), :]
    return ref[pl.ds(start, n, stride=r), :]


def _set_rows(ref, start, n, r, val):
    if r == 1:
        ref[pl.ds(start, n), :] = val
    else:
        ref[pl.ds(start, n, stride=r), :] = val


def _band_geometry(S, patterns):
    rmax = max(r for _, r in patterns)
    H = BLK * rmax
    T = min(S, ATT_TILE)
    assert T % H == 0 and S % T == 0
    return H, T, S // T, T // BLK


def _band_fwd(q, k, v, *, patterns, nq, name, sinks=None, want_bf16=False):
    S, Ck = k.shape
    H, T, nt, nbt = _band_geometry(S, patterns)
    ncol = Ck // LANES
    scale = HEAD ** -0.5
    has_sink = sinks is not None
    nt_dims = (((1,), (1,)), ((), ()))

    def body(*refs):
        q_ref, kp_ref, kc_ref, vp_ref, vc_ref = refs[:5]
        sk_ref = refs[5] if has_sink else None
        n_out = 3 if want_bf16 else 2
        outs = refs[5 + has_sink:5 + has_sink + n_out]
        qf, kf, vf, M, L, A = refs[5 + has_sink + n_out:]
        t = pl.program_id(1)
        kf[0:H, :] = kp_ref[...].astype(F32)
        kf[H:H + T, :] = kc_ref[...].astype(F32)
        vf[0:H, :] = vp_ref[...].astype(F32)
        vf[H:H + T, :] = vc_ref[...].astype(F32)
        r_i = lax.broadcasted_iota(jnp.int32, (BLK, 2 * BLK), 0)
        c_i = lax.broadcasted_iota(jnp.int32, (BLK, 2 * BLK), 1)
        dist_i = r_i + BLK - c_i
        masks = [_head_mask((BLK, LANES), h) for h in range(2)]

        for i in range(nq):
            qf[...] = q_ref[:, i * 128:(i + 1) * 128].astype(F32) * scale
            for p, (dist, r) in enumerate(patterns):
                in_band = (dist_i >= 0) & (dist_i <= dist)
                in_band_first = in_band & ((c_i >= BLK) | (t > 0))

                def unit(j, b, p=p, r=r, in_band=in_band, in_band_first=in_band_first):
                    q0 = j + b * (BLK * r)
                    q2 = _rows(qf, q0, BLK, r).astype(BF16)
                    kcat = _rows(kf, H + q0 - BLK * r, 2 * BLK, r).astype(BF16)
                    vcat = _rows(vf, H + q0 - BLK * r, 2 * BLK, r).astype(BF16)
                    valid = in_band if b > 0 else in_band_first
                    m2 = jnp.zeros((BLK, LANES), F32)
                    l2 = jnp.zeros((BLK, LANES), F32)
                    a2 = jnp.zeros((BLK, LANES), F32)
                    for half in range(2):
                        m = masks[half]
                        qm = jnp.where(m, q2, jnp.zeros_like(q2))
                        s = lax.dot_general(qm, kcat, nt_dims, preferred_element_type=F32)
                        s = jnp.where(valid, s, -jnp.inf)
                        mx = jnp.max(s, axis=1, keepdims=True)
                        pr = jnp.exp(s - mx)
                        den = jnp.sum(pr, axis=1, keepdims=True)
                        pv = jnp.dot(pr.astype(BF16), vcat, preferred_element_type=F32)
                        m2 = jnp.where(m, mx, m2)
                        l2 = jnp.where(m, den, l2)
                        a2 = jnp.where(m, pv, a2)
                    if p > 0:
                        mo = _rows(M, q0, BLK, r)
                        mn = jnp.maximum(mo, m2)
                        wa, wb = jnp.exp(mo - mn), jnp.exp(m2 - mn)
                        l2 = wa * _rows(L, q0, BLK, r) + wb * l2
                        a2 = wa * _rows(A, q0, BLK, r) + wb * a2
                        m2 = mn
                    _set_rows(M, q0, BLK, r, m2)
                    _set_rows(L, q0, BLK, r, l2)
                    _set_rows(A, q0, BLK, r, a2)

                for u in range(nbt):
                    unit(u % r, u // r)
            sl = slice(i * 128, (i + 1) * 128)
            mm, ll, aa = M[...], L[...], A[...]
            if has_sink:
                snk = sk_ref[:, sl]
                mn = jnp.maximum(mm, snk)
                w = jnp.exp(mm - mn)
                ll = ll * w + jnp.exp(snk - mn)
                aa = aa * w
                mm = mn
            o = aa / ll
            outs[0][:, sl] = o
            outs[1][:, sl] = mm + jnp.log(ll)
            if want_bf16:
                outs[2][:, sl] = o.astype(BF16)

    th = T // H
    qspec = pl.BlockSpec((T, nq * 128), lambda j, t: (t, j))
    cur = pl.BlockSpec((T, 128), lambda j, t: (t, j))
    prev = pl.BlockSpec((H, 128), lambda j, t: (jnp.maximum(t * th - 1, 0), j))
    in_specs = [qspec, prev, cur, prev, cur]
    args = [q, k, k, v, v]
    if has_sink:
        in_specs.append(pl.BlockSpec((1, nq * 128), lambda j, t: (0, j)))
        args.append(sinks)
    out_dts = [F32, F32] + ([BF16] if want_bf16 else [])
    return _pc(
        body, name=name, grid=(ncol, nt), in_specs=in_specs,
        out_specs=[qspec] * len(out_dts),
        out_shape=[jax.ShapeDtypeStruct(q.shape, dt) for dt in out_dts],
        scratch_shapes=[pltpu.VMEM((T, LANES), F32), pltpu.VMEM((H + T, LANES), F32), pltpu.VMEM((H + T, LANES), F32),
                        pltpu.VMEM((T, LANES), F32), pltpu.VMEM((T, LANES), F32), pltpu.VMEM((T, LANES), F32)],
        compiler_params=_params(("parallel", "parallel")),
    )(*args)


def _band_bwd(q, k, v, lse, delta, do, *, patterns, nq, name, sinks=None, do_col0=0):
    S, Ck = k.shape
    H, T, nt, nbt = _band_geometry(S, patterns)
    ncol = Ck // LANES
    scale = HEAD ** -0.5
    has_sink = sinks is not None
    nt_dims = (((1,), (1,)), ((), ()))
    tn_dims = (((0,), (0,)), ((), ()))

    def body(*refs):
        (qc_ref, qn_ref, kp_ref, kc_ref, vp_ref, vc_ref, lc_ref, ln_ref, ec_ref, en_ref, dc_ref, dn_ref) = refs[:12]
        sk_ref = refs[12] if has_sink else None
        n_out = 4 if has_sink else 3
        outs = refs[12 + has_sink:12 + has_sink + n_out]
        dq_ref, dk_ref, dv_ref = outs[:3]
        qf, kf, vf, lf, ef, df = refs[12 + has_sink + n_out:]
        t = pl.program_id(1)
        kf[0:H, :] = kp_ref[...].astype(F32)
        kf[H:H + T, :] = kc_ref[...].astype(F32)
        vf[0:H, :] = vp_ref[...].astype(F32)
        vf[H:H + T, :] = vc_ref[...].astype(F32)
        dk_ref[...] = jnp.zeros_like(dk_ref)
        dv_ref[...] = jnp.zeros_like(dv_ref)
        r_i = lax.broadcasted_iota(jnp.int32, (BLK, 2 * BLK), 0)
        c_i = lax.broadcasted_iota(jnp.int32, (BLK, 2 * BLK), 1)
        dist_q = r_i + BLK - c_i
        dist_h = dist_q[:, :BLK]
        m1 = [_head_mask((BLK, LANES), h) for h in range(2)]

        def head_inputs(half, q2, do2, l2, e2):
            m = m1[half]
            lh = jnp.where(m, l2, _roll(l2, HEAD))
            eh = jnp.where(m, e2, _roll(e2, HEAD))
            return jnp.where(m, q2, jnp.zeros_like(q2)), jnp.where(m, do2, 0.0).astype(BF16), lh, eh

        for i in range(nq):
            sl = slice(i * 128, (i + 1) * 128)
            qf[0:T, :] = qc_ref[:, sl].astype(F32) * scale
            qf[T:T + H, :] = qn_ref[:, sl].astype(F32) * scale
            for buf, c_ref, n_ref in ((lf, lc_ref, ln_ref), (ef, ec_ref, en_ref), (df, dc_ref, dn_ref)):
                buf[0:T, :] = c_ref[:, sl]
                buf[T:T + H, :] = n_ref[:, sl]
            if has_sink:
                @pl.when(t == 0)
                def _():
                    outs[3][:, sl] = jnp.zeros((1, LANES), F32)

                outs[3][:, sl] += jnp.sum(-jnp.exp(sk_ref[:, sl] - lc_ref[:, sl]) * ec_ref[:, sl], axis=0, keepdims=True)
            for p, (dist, r) in enumerate(patterns):
                band_q = (dist_q >= 0) & (dist_q <= dist)
                band_first = band_q & ((c_i >= BLK) | (t > 0))
                band_h = (dist_h >= 0) & (dist_h <= dist)

                def add_rows(ref, start, val, r=r):
                    _set_rows(ref, start, BLK, r, _rows(ref, start, BLK, r) + val)

                def unit(j, b, p=p, r=r, band_q=band_q, band_first=band_first):
                    q0 = j + b * (BLK * r)
                    q2 = _rows(qf, q0, BLK, r).astype(BF16)
                    do2, l2, e2 = _rows(df, q0, BLK, r), _rows(lf, q0, BLK, r), _rows(ef, q0, BLK, r)
                    kcat = _rows(kf, H + q0 - BLK * r, 2 * BLK, r).astype(BF16)
                    vcat = _rows(vf, H + q0 - BLK * r, 2 * BLK, r).astype(BF16)
                    valid = band_q if b > 0 else band_first
                    dq2 = jnp.zeros((BLK, LANES), F32)
                    dkc = jnp.zeros((2 * BLK, LANES), F32)
                    dvc = jnp.zeros((2 * BLK, LANES), F32)
                    for half in range(2):
                        qm, dom, lh, eh = head_inputs(half, q2, do2, l2, e2)
                        s = lax.dot_general(qm, kcat, nt_dims, preferred_element_type=F32)
                        pr = jnp.where(valid, jnp.exp(s - jnp.concatenate([lh, lh], axis=1)), 0.0)
                        dp = lax.dot_general(dom, vcat, nt_dims, preferred_element_type=F32)
                        ds = (pr * (dp - jnp.concatenate([eh, eh], axis=1))).astype(BF16)
                        dq2 = jnp.where(m1[half], jnp.dot(ds, kcat, preferred_element_type=F32) * scale, dq2)
                        dvc += lax.dot_general(pr.astype(BF16), dom, tn_dims, preferred_element_type=F32)
                        dkc += lax.dot_general(ds, qm, tn_dims, preferred_element_type=F32)
                    if p > 0:
                        dq2 = dq2 + _rows(dq_ref.at[:, sl], q0, BLK, r)
                    _set_rows(dq_ref.at[:, sl], q0, BLK, r, dq2)
                    add_rows(dk_ref, q0, dkc[BLK:])
                    add_rows(dv_ref, q0, dvc[BLK:])
                    if b > 0:
                        add_rows(dk_ref, q0 - BLK * r, dkc[:BLK])
                        add_rows(dv_ref, q0 - BLK * r, dvc[:BLK])

                def halo_unit(j, r=r, band_h=band_h):
                    k0 = j + (nbt // r - 1) * (BLK * r)
                    q2 = _rows(qf, T + j, BLK, r).astype(BF16)
                    do2, l2, e2 = _rows(df, T + j, BLK, r), _rows(lf, T + j, BLK, r), _rows(ef, T + j, BLK, r)
                    kc = _rows(kf, H + k0, BLK, r).astype(BF16)
                    vc = _rows(vf, H + k0, BLK, r).astype(BF16)
                    dk2 = jnp.zeros((BLK, LANES), F32)
                    dv2 = jnp.zeros((BLK, LANES), F32)
                    for half in range(2):
                        qm, dom, lh, eh = head_inputs(half, q2, do2, l2, e2)
                        s = lax.dot_general(qm, kc, nt_dims, preferred_element_type=F32)
                        pr = jnp.where(band_h, jnp.exp(s - lh), 0.0)
                        dp = lax.dot_general(dom, vc, nt_dims, preferred_element_type=F32)
                        ds = (pr * (dp - eh)).astype(BF16)
                        dv2 += lax.dot_general(pr.astype(BF16), dom, tn_dims, preferred_element_type=F32)
                        dk2 += lax.dot_general(ds, qm, tn_dims, preferred_element_type=F32)
                    add_rows(dk_ref, k0, dk2)
                    add_rows(dv_ref, k0, dv2)

                for u in range(nbt):
                    unit(u % r, u // r)
                if nt > 1:
                    @pl.when(t < nt - 1)
                    def _(r=r, halo_unit=halo_unit):
                        for j in range(r):
                            halo_unit(j)

    th = T // H
    last = S // H - 1
    qcur = pl.BlockSpec((T, nq * 128), lambda j, t: (t, j))
    qnext = pl.BlockSpec((H, nq * 128), lambda j, t: (jnp.minimum((t + 1) * th, last), j))
    cur = pl.BlockSpec((T, 128), lambda j, t: (t, j))
    prev = pl.BlockSpec((H, 128), lambda j, t: (jnp.maximum(t * th - 1, 0), j))
    dcur = pl.BlockSpec((T, nq * 128), lambda j, t: (t, j + do_col0))
    dnext = pl.BlockSpec((H, nq * 128), lambda j, t: (jnp.minimum((t + 1) * th, last), j + do_col0))
    in_specs = [qcur, qnext, prev, cur, prev, cur, qcur, qnext, qcur, qnext, dcur, dnext]
    args = [q, q, k, k, v, v, lse, lse, delta, delta, do, do]
    out_specs = [qcur, cur, cur]
    out_shape = [jax.ShapeDtypeStruct(q.shape, F32), jax.ShapeDtypeStruct(k.shape, F32), jax.ShapeDtypeStruct(k.shape, F32)]
    if has_sink:
        vec = pl.BlockSpec((1, nq * 128), lambda j, t: (0, j))
        in_specs.append(vec)
        args.append(sinks)
        out_specs.append(vec)
        out_shape.append(jax.ShapeDtypeStruct((1, q.shape[1]), F32))
    big = pltpu.VMEM((T + H, LANES), F32)
    return _pc(
        body, name=name, grid=(ncol, nt), in_specs=in_specs, out_specs=out_specs, out_shape=out_shape,
        scratch_shapes=[big] * 6,
        compiler_params=_params(("parallel", "arbitrary")),
    )(*args)


def _delta(do, o, name):
    S, C = do.shape
    tm = 512

    def body(do_ref, o_ref, g_ref, e_ref):
        for c in range(C // LANES):
            sl = slice(c * 128, (c + 1) * 128)
            e_ref[:, sl] = _gmean(do_ref[:, sl] * o_ref[:, sl], g_ref[...]) * float(HEAD)

    row = pl.BlockSpec((tm, C), lambda i: (i, 0))
    return _pc(
        body, name=name, grid=(S // tm,),
        in_specs=[row, row, pl.BlockSpec((LANES, LANES), lambda i: (0, 0))], out_specs=row,
        out_shape=jax.ShapeDtypeStruct((S, C), F32),
        compiler_params=_params(("parallel",)),
    )(do, o, _group_matrix())


def _even_post_fwd(ro, proj, gn, da):
    S = ro.shape[0]
    tm = 256

    def body(ro_ref, rg_ref, gn_ref, da_ref, mix_ref):
        for c in range(4):
            sl = slice(c * 128, (c + 1) * 128)
            x = ro_ref[:, sl]
            mu = jnp.mean(x, axis=1, keepdims=True)
            xc = x - mu
            var = jnp.mean(xc * xc, axis=1, keepdims=True)
            y = xc * lax.rsqrt(var + EPS) * gn_ref[:, sl]
            z = rg_ref[:, sl]
            mix_ref[:, sl] = (z * jax.nn.sigmoid(z) * y).astype(BF16)
        mix_ref[:, 512:1024] = da_ref[...].astype(BF16)

    row = lambda w: pl.BlockSpec((tm, w), lambda i: (i, 0))
    return _pc(
        body, name="even_post_fwd", grid=(S // tm,),
        in_specs=[row(512), pl.BlockSpec((tm, 512), lambda i: (i, 2)), pl.BlockSpec((1, 512), lambda i: (0, 0)), row(512)],
        out_specs=row(1024), out_shape=jax.ShapeDtypeStruct((S, 1024), BF16),
        compiler_params=_params(("parallel",)),
    )(ro, proj, gn, da)


def _even_post_bwd(ro, proj, gn, dmixed):
    S = ro.shape[0]
    tm = 256

    def body(ro_ref, rg_ref, gn_ref, dm_ref, dro_ref, drg_ref, dgn_ref):
        @pl.when(pl.program_id(0) == 0)
        def _():
            dgn_ref[...] = jnp.zeros_like(dgn_ref)

        for c in range(4):
            sl = slice(c * 128, (c + 1) * 128)
            x = ro_ref[:, sl]
            mu = jnp.mean(x, axis=1, keepdims=True)
            xc = x - mu
            rstd = lax.rsqrt(jnp.mean(xc * xc, axis=1, keepdims=True) + EPS)
            xh = xc * rstd
            gain = gn_ref[:, sl]
            y = xh * gain
            z = rg_ref[:, sl]
            sg = jax.nn.sigmoid(z)
            dra = dm_ref[:, sl]
            drg_ref[:, sl] = dra * y * sg * (1.0 + z * (1.0 - sg))
            dy = dra * z * sg
            dgn_ref[:, sl] += jnp.sum(dy * xh, axis=0, keepdims=True)
            dxh = dy * gain
            dro_ref[:, sl] = rstd * (dxh - jnp.mean(dxh, axis=1, keepdims=True)
                                     - xh * jnp.mean(dxh * xh, axis=1, keepdims=True))

    row = lambda w: pl.BlockSpec((tm, w), lambda i: (i, 0))
    vec = pl.BlockSpec((1, 512), lambda i: (0, 0))
    return _pc(
        body, name="even_post_bwd", grid=(S // tm,),
        in_specs=[row(512), pl.BlockSpec((tm, 512), lambda i: (i, 2)), vec, row(512)],
        out_specs=[row(512), row(512), vec],
        out_shape=[jax.ShapeDtypeStruct((S, 512), F32), jax.ShapeDtypeStruct((S, 512), F32),
                   jax.ShapeDtypeStruct((1, 512), F32)],
        compiler_params=_params(("arbitrary",)),
    )(ro, proj, gn, dmixed)


def _swa_pre_fwd(proj, tab, qg, kg):
    S = proj.shape[0]
    tm = 256

    def body(p_ref, tab_ref, qg_ref, kg_ref, g_ref, q_ref, k_ref, v_ref):
        Ap, Bp, Cp = _tab(tab_ref, 1)
        G = g_ref[...]
        lo = _head_mask((tm, LANES), 0)
        for c in range(8):
            sl = slice(c * 128, (c + 1) * 128)
            q_ref[:, sl] = _rope(_hn_fwd(p_ref[:, sl], qg_ref[...], G), Ap, Bp, Cp, 8).astype(BF16)
        for c in range(2):
            kn = _rope(_hn_fwd(p_ref[:, 1024 + c * 128:1024 + (c + 1) * 128], kg_ref[...], G), Ap, Bp, Cp, 8)
            vv = p_ref[:, 1280 + c * 128:1280 + (c + 1) * 128]
            for t, ref in ((kn, k_ref), (vv, v_ref)):
                sw = _roll(t, HEAD)
                ref[:, (2 * c) * 128:(2 * c + 1) * 128] = jnp.where(lo, t, sw).astype(BF16)
                ref[:, (2 * c + 1) * 128:(2 * c + 2) * 128] = jnp.where(lo, sw, t).astype(BF16)

    row = lambda w: pl.BlockSpec((tm, w), lambda i: (i, 0))
    vec = pl.BlockSpec((1, LANES), lambda i: (0, 0))
    return _pc(
        body, name="swa_pre_fwd", grid=(S // tm,),
        in_specs=[row(1536), row(768), vec, vec, pl.BlockSpec((LANES, LANES), lambda i: (0, 0))],
        out_specs=[row(1024), row(512), row(512)],
        out_shape=[jax.ShapeDtypeStruct((S, w), BF16) for w in (1024, 512, 512)],
        compiler_params=_params(("parallel",)),
    )(proj, tab, qg, kg, _group_matrix())


def _swa_pre_bwd(proj, tab, qg, kg, dq, dk, dv):
    S = proj.shape[0]
    tm = 256

    def body(p_ref, tab_ref, qg_ref, kg_ref, g_ref, dq_ref, dk_ref, dv_ref, dp_ref, db_ref, dqg_ref, dkg_ref):
        Ap, Bp, Cp = _tab(tab_ref, 1)
        G = g_ref[...]
        lo = _head_mask((tm, LANES), 0)

        @pl.when(pl.program_id(0) == 0)
        def _():
            db_ref[...] = jnp.zeros_like(db_ref)
            dqg_ref[...] = jnp.zeros_like(dqg_ref)
            dkg_ref[...] = jnp.zeros_like(dkg_ref)

        accq = jnp.zeros((1, LANES), F32)
        acck = jnp.zeros((1, LANES), F32)
        for c in range(8):
            sl = slice(c * 128, (c + 1) * 128)
            dx, dg = _hn_bwd(p_ref[:, sl], qg_ref[...], _rope_t(dq_ref[:, sl], Ap, Bp, Cp, 8), G)
            dp_ref[:, sl] = dx.astype(BF16)
            db_ref[:, sl] += jnp.sum(dx, axis=0, keepdims=True)
            accq = accq + dg
        for c in range(2):
            folded = []
            for ref in (dk_ref, dv_ref):
                a = ref[:, (2 * c) * 128:(2 * c + 1) * 128]
                b = ref[:, (2 * c + 1) * 128:(2 * c + 2) * 128]
                folded.append(jnp.where(lo, a + _roll(a, HEAD), b + _roll(b, HEAD)))
            ks = slice(1024 + c * 128, 1024 + (c + 1) * 128)
            dx, dg = _hn_bwd(p_ref[:, ks], kg_ref[...], _rope_t(folded[0], Ap, Bp, Cp, 8), G)
            dp_ref[:, ks] = dx.astype(BF16)
            db_ref[:, ks] += jnp.sum(dx, axis=0, keepdims=True)
            acck = acck + dg
            vs = slice(1280 + c * 128, 1280 + (c + 1) * 128)
            dp_ref[:, vs] = folded[1].astype(BF16)
            db_ref[:, vs] += jnp.sum(folded[1], axis=0, keepdims=True)
        dqg_ref[...] += _fold_halves(accq)
        dkg_ref[...] += _fold_halves(acck)

    row = lambda w: pl.BlockSpec((tm, w), lambda i: (i, 0))
    vec = pl.BlockSpec((1, LANES), lambda i: (0, 0))
    return _pc(
        body, name="swa_pre_bwd", grid=(S // tm,),
        in_specs=[row(1536), row(768), vec, vec, pl.BlockSpec((LANES, LANES), lambda i: (0, 0)),
                  row(1024), row(512), row(512)],
        out_specs=[row(1536), pl.BlockSpec((1, 1536), lambda i: (0, 0)), vec, vec],
        out_shape=[jax.ShapeDtypeStruct((S, 1536), BF16), jax.ShapeDtypeStruct((1, 1536), F32),
                   jax.ShapeDtypeStruct((1, LANES), F32), jax.ShapeDtypeStruct((1, LANES), F32)],
        compiler_params=_params(("arbitrary",)),
    )(proj, tab, qg, kg, _group_matrix(), dq, dk, dv)


def _loss_head(y, target):
    S, Dm = y.shape
    tm = 512

    def body(y_ref, t_ref, l_ref, dy_ref, dyb_ref):
        @pl.when(pl.program_id(0) == 0)
        def _():
            l_ref[...] = jnp.zeros_like(l_ref)

        e = y_ref[...] - t_ref[...]
        dy = e * (1.0 / Dm)
        dy_ref[...] = dy
        dyb_ref[...] = dy.astype(BF16)
        row = jnp.sum(e * e, axis=1, keepdims=True) * (0.5 / Dm)
        l_ref[...] += jnp.sum(row, axis=0, keepdims=True)

    row = pl.BlockSpec((tm, Dm), lambda i: (i, 0))
    return _pc(
        body, name="loss_head", grid=(S // tm,), in_specs=[row, row],
        out_specs=[pl.BlockSpec((1, LANES), lambda i: (0, 0)), row, row],
        out_shape=[jax.ShapeDtypeStruct((1, LANES), F32), jax.ShapeDtypeStruct((S, Dm), F32),
                   jax.ShapeDtypeStruct((S, Dm), BF16)],
        compiler_params=_params(("arbitrary",)),
    )(y, target)


def _relu2_of(u):
    r = jnp.maximum(u.astype(F32), 0.0)
    return r * r


def _drelu2(acc, u):
    return (acc * 2.0 * jnp.maximum(u.astype(F32), 0.0),)


def _add(acc, res):
    return (acc + res,)


_T = dict(tm=1024, tn=1024, tk=1024)


def _rms_bwd_in(x, g, dres):
    def epilogue(dh, xv, gv, dr):
        r = lax.rsqrt(jnp.mean(xv * xv, axis=-1, keepdims=True) + EPS)
        t = dh * gv
        dx = dr + r * t - xv * (r * r * r) * jnp.mean(xv * t, axis=-1, keepdims=True)
        return dx, dx, jnp.sum(dh * xv * r, axis=0, keepdims=True)

    return dict(outs=[F32, BF16, ("colsum",)], epilogue=epilogue,
                extras=[(x, "mn"), (g.reshape(1, D_MODEL), "n"), (dres, "mn")])


def _delta_in(o, col0):
    width = D_MODEL - col0

    def epilogue(do, ov, G):
        parts = [_gmean(do[:, col0 + c * 128:col0 + (c + 1) * 128] * ov[:, c * 128:(c + 1) * 128], G) * float(HEAD)
                 for c in range(width // LANES)]
        return do, jnp.concatenate(parts, axis=1)

    return dict(outs=[F32, (F32, width)], epilogue=epilogue, extras=[(o, width), (_group_matrix(), "full")])


def _mlp_fwd(x, g, wts, layer, tag):
    h = _rms_fwd(x, g, f"rms_mlp_fwd{tag}")
    u = _matmul(h, wts, dims="nn", **_T, outs=[BF16], b_cs=True, b_row0=layer, name=f"mlp_up{tag}")
    x_out = _matmul(u, wts, dims="nn", **_T, outs=[F32], epilogue=_add, extras=[(x, "mn")], a_pro=_relu2_of,
                    b_rs=1024, b_row0=2 + layer, name=f"mlp_down{tag}")
    return x_out, (h, u)


def _mlp_bwd(x, g, wts, layer, saved, dy, dyb, tag):
    h, u = saved
    du = _matmul(dyb, wts, dims="nt", **_T, outs=[BF16], epilogue=_drelu2, extras=[(u, "mn")], b_rs=1024,
                 b_row0=2 + layer, name=f"mlp_du{tag}")
    dw_dn = _matmul(u, dyb, dims="tn", **_T, outs=[F32], a_pro=_relu2_of, name=f"mlp_dwdown{tag}")
    dw_up = _matmul(h, du, dims="tn", **_T, outs=[F32], o_cs=N_CHIPS, name=f"mlp_dwup{tag}")
    dx, dxb, dg = _matmul(du, wts, dims="nt", tm=512, tn=1024, tk=1024, b_cs=True, b_row0=layer, b_rows=1024, name=f"mlp_dh{tag}",
                          **_rms_bwd_in(x, g, dy))
    return dx, dxb, dg, dw_up, dw_dn


def _pattern_view(t, r):
    S, C = t.shape
    return t.reshape(S // r, r * C)


def _local_step(x, pos_col, target, first_of, rest_of, P, red):
    S = x.shape[0]
    tab = _tables(pos_col)
    tile2 = lambda g: jnp.tile(g.reshape(1, HEAD), (1, 2))
    dqg, dkg = tile2(P["dil_q_gain"]), tile2(P["dil_k_gain"])
    sqg, skg = tile2(P["swa_q_gain"]), tile2(P["swa_k_gain"])
    gn = P["ret_gn_gain"].reshape(1, 512)
    sink_b = jnp.repeat(P["swa_sinks"].reshape(16), HEAD).reshape(1, 1024)

    h0 = _rms_fwd(x, P["norm_mix"][0], "rms_mix_fwd0")
    W = first_of(h0)
    proj = _matmul(h0, W["hyb_w_in"], dims="nn", tm=1024, tn=768, tk=1024, outs=[F32], b_cs=True, name="hyb_in")
    rq, rk, rv, dq, dk, dv = _even_pre_fwd(proj, tab, dqg, dkg)
    ro, states = _ret_fwd(rq, rk, rv)
    dil = [(w // r, r) for w, r in DIL_PATTERNS]
    da, dlse = _band_fwd(dq, dk, dv, patterns=dil, nq=1, name="dil_fwd")
    mixed = _even_post_fwd(ro, proj, gn, da)
    x1 = _matmul(mixed, W["hyb_w_out"], dims="nn", **_T, outs=[F32], epilogue=_add, extras=[(x, "mn")], name="hyb_out")
    rest, bias = rest_of(x1)
    W = {**W, **rest}
    x2, mlp0 = _mlp_fwd(x1, P["norm_mlp"][0], W["packed"], 0, "0")

    h2 = _rms_fwd(x2, P["norm_mix"][1], "rms_mix_fwd1")
    proj2 = _matmul(h2, W["swa_w_qkv"], dims="nn", tm=1024, tn=384, tk=1024, outs=[F32], b_cs=True,
                    epilogue=_add, extras=[(bias.reshape(1, 1536), "n")], name="swa_qkv")
    sq, sk, sv = _swa_pre_fwd(proj2, tab, sqg, skg)
    swa = [(SWA_DIST, 1)]
    so, slse, so_b = _band_fwd(sq, sk, sv, patterns=swa, nq=2, name="swa_fwd", sinks=sink_b, want_bf16=True)
    x3 = _matmul(so_b, W["swa_w_out"], dims="nn", **_T, outs=[F32], epilogue=_add, extras=[(x2, "mn")], name="swa_out")
    y, mlp1 = _mlp_fwd(x3, P["norm_mlp"][1], W["packed"], 1, "1")
    loss, dy, dyb = _loss_head(y, target)

    gw, gp = {}, {}
    dx3, dx3b, dg_mlp1, gw["mlp_w_up1"], gw["mlp_w_down1"] = _mlp_bwd(x3, P["norm_mlp"][1], W["packed"], 1, mlp1, dy, dyb, "1")
    dx3b = red.begin("mlp1", {n: (gw[n], 1024) for n in ("mlp_w_up1", "mlp_w_down1")}, dx3b)
    gw["swa_w_out"] = _matmul(so_b, dx3b, dims="tn", **_T, outs=[F32], name="swa_dwout")
    dso, sdelta = _matmul(dx3b, W["swa_w_out"], dims="nt", tm=512, tn=1024, tk=1024, name="swa_do", **_delta_in(so, 0))
    dsq, dsk, dsv, dsink = _band_bwd(sq, sk, sv, slse, sdelta, dso, patterns=swa, nq=2, name="swa_bwd", sinks=sink_b)
    dproj2, gp["swa_b_qkv"], gp["swa_q_gain"], gp["swa_k_gain"] = _swa_pre_bwd(proj2, tab, sqg, skg, dsq, dsk, dsv)
    gp["swa_sinks"] = dsink
    gw["swa_w_qkv"] = _matmul(h2, dproj2, dims="tn", tm=1024, tn=384, tk=1024, outs=[F32], o_cs=N_CHIPS, name="swa_dwqkv")
    dx2, dx2b, dg_mix1 = _matmul(dproj2, W["swa_w_qkv"], dims="nt", tm=512, tn=1024, tk=384, b_cs=True, name="swa_dh",
                                 **_rms_bwd_in(x2, P["norm_mix"][1], dx3))
    dx2b = red.begin("swa", {"swa_w_qkv": (gw["swa_w_qkv"], 1024), "swa_w_out": (gw["swa_w_out"], 256)}, dx2b)
    dx2b = red.advance("mlp1", dx2b, dx2b)

    dx1, dx1b, dg_mlp0, gw["mlp_w_up0"], gw["mlp_w_down0"] = _mlp_bwd(x1, P["norm_mlp"][0], W["packed"], 0, mlp0, dx2, dx2b, "0")
    gw["hyb_w_out"] = _matmul(mixed, dx1b, dims="tn", **_T, outs=[F32], name="hyb_dwout")
    dx1b = red.begin("mlp0", {"mlp_w_up0": (gw["mlp_w_up0"], 1024), "mlp_w_down0": (gw["mlp_w_down0"], 1024),
                              "hyb_w_out": (gw["hyb_w_out"], 256)}, dx1b)
    dx1b = red.advance("swa", dx1b, dx1b)
    red.finish("mlp1", dx1b)
    dmixed, ddelta = _matmul(dx1b, W["hyb_w_out"], dims="nt", tm=512, tn=1024, tk=1024, name="hyb_dmixed", **_delta_in(da, 512))
    dro, drg, gp["ret_gn_gain"] = _even_post_bwd(ro, proj, gn, dmixed)
    drq, drk, drv = _ret_bwd(rq, rk, rv, states, dro)
    ddq, ddk, ddv = _band_bwd(dq, dk, dv, dlse, ddelta, dmixed, patterns=dil, nq=1, name="dil_bwd", do_col0=4)
    ddq = red.advance("mlp0", ddq, ddq)
    red.finish("swa", ddq)
    dproj, gp["dil_q_gain"], gp["dil_k_gain"] = _even_pre_bwd(proj, tab, dqg, dkg, drq, drk, drv, drg, [ddq], [ddk], [ddv])
    gw["hyb_w_in"] = _matmul(h0, dproj, dims="tn", tm=1024, tn=768, tk=1024, outs=[F32], o_cs=N_CHIPS, name="hyb_dwin")
    dproj = red.begin("win", {"hyb_w_in": (gw["hyb_w_in"], 1024)}, dproj)
    grad_x, _, dg_mix0 = _matmul(dproj, W["hyb_w_in"], dims="nt", tm=512, tn=1024, tk=768, b_cs=True, name="hyb_dh",
                                 **_rms_bwd_in(x, P["norm_mix"][0], dx1))
    red.finish("mlp0", grad_x)
    gp["norm_mix"] = jnp.concatenate([dg_mix0, dg_mix1], axis=0)
    gp["norm_mlp"] = jnp.concatenate([dg_mlp0, dg_mlp1], axis=0)
    return loss, grad_x, gp


HBM = pl.BlockSpec(memory_space=pltpu.HBM)


def _place():
    x, y, c = lax.axis_index("x"), lax.axis_index("y"), lax.axis_index("c")
    chips = [(1 - x, y), (x, 1 - y), (1 - x, 1 - y)]
    return x, y, c, chips


def _allgather_shards(buf):
    _, R, Wd = buf.shape
    Rh = R // 2

    def body(b_ref, out_ref, send_sems, recv_sems):
        x, y, c, chips = _place()
        sibling = (x, y, 1 - c)

        def copy(k, chip, core, to):
            block = b_ref.at[2 * chip[0] + chip[1], pl.ds(core * Rh, Rh), :]
            return pltpu.make_async_remote_copy(
                src_ref=block, dst_ref=block, send_sem=send_sems.at[k], recv_sem=recv_sems.at[k],
                device_id=to, device_id_type=MESH)

        first = [copy(k, (x, y), c, (*chip, c)) for k, chip in enumerate(chips)]
        for cp in first:
            cp.start()
        passed = [copy(3 + k, chip, c, sibling) for k, chip in enumerate(chips)]
        for k, chip in enumerate(chips):
            copy(k, chip, c, (x, y, c)).wait_recv()
            passed[k].start()
        for k, chip in enumerate(chips):
            copy(3 + k, chip, 1 - c, (x, y, c)).wait_recv()
        for cp in first + passed:
            cp.wait_send()

    return _pc(
        body, name="allgather_first", in_specs=[HBM], out_specs=HBM,
        out_shape=jax.ShapeDtypeStruct(buf.shape, buf.dtype), input_output_aliases={0: 0},
        scratch_shapes=[pltpu.SemaphoreType.DMA((6,)), pltpu.SemaphoreType.DMA((6,))],
    )(buf)


SEM = pl.BlockSpec(memory_space=pltpu.SEMAPHORE)
EFFECT = pltpu.SideEffectType.DATAFLOW_SIDE_EFFECTING


def _half_block(ref, chip, core):
    rh = ref.shape[1] // 2
    return ref.at[2 * chip[0] + chip[1], pl.ds(core * rh, rh), :]


def _gather_start(buf, ride, name):
    def body(b_ref, ride_ref, s0, s1, s2, r0, r1, r2, b_out, ride_out):
        x, y, c, chips = _place()
        for chip, s, r in zip(chips, (s0, s1, s2), (r0, r1, r2)):
            mine = _half_block(b_ref, (x, y), c)
            pltpu.make_async_remote_copy(src_ref=mine, dst_ref=mine, send_sem=s, recv_sem=r,
                                         device_id=(*chip, c), device_id_type=MESH).start()

    sem = pltpu.SemaphoreType.DMA(())
    return _pc(
        body, name=name,
        out_shape=(sem,) * 6 + (pltpu.HBM(buf.shape, buf.dtype), pltpu.HBM(ride.shape, ride.dtype)),
        in_specs=(HBM, HBM), out_specs=(SEM,) * 6 + (HBM, HBM), input_output_aliases={0: 6, 1: 7},
        compiler_params=pltpu.CompilerParams(has_side_effects=EFFECT),
    )(pltpu.with_memory_space_constraint(buf, pltpu.HBM), pltpu.with_memory_space_constraint(ride, pltpu.HBM))


def _gather_wait(buf, sems, after, name):
    def body(b_ref, s0, s1, s2, r0, r1, r2, after_ref, b_out):
        x, y, c, chips = _place()
        for chip, s, r in zip(chips, (s0, s1, s2), (r0, r1, r2)):
            cp = pltpu.make_async_remote_copy(src_ref=_half_block(b_ref, (x, y), c), dst_ref=_half_block(b_ref, chip, c),
                                              send_sem=s, recv_sem=r, device_id=(*chip, c), device_id_type=MESH)
            cp.wait_send()
            cp.wait_recv()

    return _pc(
        body, name=name, out_shape=pltpu.HBM(buf.shape, buf.dtype),
        in_specs=(HBM,) + (SEM,) * 6 + (pl.BlockSpec(memory_space=pl.ANY),), out_specs=HBM, input_output_aliases={0: 0},
        compiler_params=pltpu.CompilerParams(has_side_effects=EFFECT),
    )(buf, *sems, after)


def _gather_handover(buf, name):
    def body(b_ref, out_ref, send_sems, recv_sems):
        x, y, c, chips = _place()
        cps = []
        for k, chip in enumerate(chips):
            mine = _half_block(b_ref, chip, c)
            cps.append(pltpu.make_async_remote_copy(src_ref=mine, dst_ref=mine, send_sem=send_sems.at[k],
                                                    recv_sem=recv_sems.at[k], device_id=(x, y, 1 - c), device_id_type=MESH))
        for cp in cps:
            cp.start()
        for k, chip in enumerate(chips):
            theirs = _half_block(b_ref, chip, 1 - c)
            pltpu.make_async_remote_copy(src_ref=theirs, dst_ref=theirs, send_sem=send_sems.at[k], recv_sem=recv_sems.at[k],
                                         device_id=(x, y, 1 - c), device_id_type=MESH).wait_recv()
        for cp in cps:
            cp.wait_send()

    return _pc(
        body, name=name, in_specs=[HBM], out_specs=HBM,
        out_shape=jax.ShapeDtypeStruct(buf.shape, buf.dtype), input_output_aliases={0: 0},
        scratch_shapes=[pltpu.SemaphoreType.DMA((3,)), pltpu.SemaphoreType.DMA((3,))],
    )(buf)


def _swap_halves(ts):
    nt = len(ts)

    def body(*refs):
        t_refs, l_refs, send_sems, recv_sems = refs[:nt], refs[nt:2 * nt], refs[-2], refs[-1]
        x, y, c, _ = _place()
        cps = []
        for k in range(nt):
            rh = t_refs[k].shape[1] // 2
            cps.append(pltpu.make_async_remote_copy(
                src_ref=t_refs[k].at[:, pl.ds((1 - c) * rh, rh), :], dst_ref=l_refs[k],
                send_sem=send_sems.at[k], recv_sem=recv_sems.at[k], device_id=(x, y, 1 - c), device_id_type=MESH))
        for cp in cps:
            cp.start()
        for cp in cps:
            cp.wait()

    return _pc(
        body, name="grad_swap_halves", in_specs=[HBM] * nt, out_specs=[HBM] * nt,
        out_shape=[jax.ShapeDtypeStruct((t.shape[0], t.shape[1] // 2, t.shape[2]), F32) for t in ts],
        scratch_shapes=[pltpu.SemaphoreType.DMA((nt,)), pltpu.SemaphoreType.DMA((nt,))],
    )(*ts)


def _pair_sum(t, l, place, name):
    _, r, cols = t.shape
    rh = r // 2
    tr = min(rh, 256)
    nr = rh // tr

    def body(pl_ref, t_ref, l_ref, o_ref):
        o_ref[...] = (t_ref[...] + l_ref[...]).astype(BF16)

    other = lambda s, p: s + jnp.where(s >= p[0], 1, 0)
    return _pc(
        body, name=name,
        grid_spec=pltpu.PrefetchScalarGridSpec(
            num_scalar_prefetch=1, grid=(N_CHIPS - 1, nr),
            in_specs=[pl.BlockSpec((None, tr, cols), lambda s, i, p: (other(s, p), p[1] * nr + i, 0)),
                      pl.BlockSpec((None, tr, cols), lambda s, i, p: (other(s, p), i, 0))],
            out_specs=pl.BlockSpec((None, tr, cols), lambda s, i, p: (other(s, p), i, 0))),
        out_shape=jax.ShapeDtypeStruct((N_CHIPS, rh, cols), BF16),
        compiler_params=_params(("parallel", "parallel")),
    )(place, t, l)


def _exchange_chips(ps):
    nt = len(ps)

    def body(*refs):
        p_refs, r_refs, send_sems, recv_sems = refs[:nt], refs[nt:2 * nt], refs[-2], refs[-1]
        x, y, c, chips = _place()
        cps = []
        for t in range(nt):
            for k, chip in enumerate(chips):
                cps.append(pltpu.make_async_remote_copy(
                    src_ref=p_refs[t].at[2 * chip[0] + chip[1]], dst_ref=r_refs[t].at[k],
                    send_sem=send_sems.at[3 * t + k], recv_sem=recv_sems.at[3 * t + k],
                    device_id=(*chip, c), device_id_type=MESH))
        for cp in cps:
            cp.start()
        for cp in cps:
            cp.wait()

    return _pc(
        body, name="grad_exchange_chips", in_specs=[HBM] * nt, out_specs=[HBM] * nt,
        out_shape=[jax.ShapeDtypeStruct((3,) + p.shape[1:], BF16) for p in ps],
        scratch_shapes=[pltpu.SemaphoreType.DMA((3 * nt,)), pltpu.SemaphoreType.DMA((3 * nt,))],
    )(*ps)


def _final_sum(t, l, rcv, place, name, layer=0, layers=1, into=None):
    _, r, cols = t.shape
    rh = r // 2
    tr = min(rh, 256)
    nr = rh // tr

    def body(pl_ref, t_ref, l_ref, r_ref, *rest):
        acc = t_ref[...] + l_ref[...]
        for k in range(3):
            acc = acc + r_ref[k].astype(F32)
        rest[-1][...] = acc

    in_specs = [pl.BlockSpec((None, tr, cols), lambda i, p: (p[0], p[1] * nr + i, 0)),
                pl.BlockSpec((None, tr, cols), lambda i, p: (p[0], i, 0)),
                pl.BlockSpec((3, tr, cols), lambda i, p: (0, i, 0))]
    args = [place, t, l, rcv]
    aliases = {}
    if into is not None:
        in_specs.append(pl.BlockSpec(memory_space=pl.ANY))
        args.append(into)
        aliases = {4: 0}
    return _pc(
        body, name=name,
        grid_spec=pltpu.PrefetchScalarGridSpec(
            num_scalar_prefetch=1, grid=(nr,), in_specs=in_specs,
            out_specs=pl.BlockSpec((tr, cols), lambda i, p: (2 * nr * layer + p[1] * nr + i, 0))),
        out_shape=jax.ShapeDtypeStruct((layers * r, cols), F32), input_output_aliases=aliases,
        compiler_params=_params(("parallel",)),
    )(*args)


def _share_halves(hs, name):
    nt = len(hs)
    n = sum(layers for _, layers in hs)

    def body(*refs):
        h_refs, send_sems, recv_sems = refs[:nt], refs[-2], refs[-1]
        x, y, c, _ = _place()
        cps = []
        for k, (_, layers) in enumerate(hs):
            rh = h_refs[k].shape[0] // (2 * layers)
            for layer in range(layers):
                half = h_refs[k].at[pl.ds((2 * layer + c) * rh, rh), :]
                cps.append(pltpu.make_async_remote_copy(
                    src_ref=half, dst_ref=half, send_sem=send_sems.at[len(cps)], recv_sem=recv_sems.at[len(cps)],
                    device_id=(x, y, 1 - c), device_id_type=MESH))
        for cp in cps:
            cp.start()
        for cp in cps:
            cp.wait()

    return _pc(
        body, name=name, in_specs=[HBM] * nt, out_specs=[HBM] * nt,
        out_shape=[jax.ShapeDtypeStruct(h.shape, F32) for h, _ in hs],
        input_output_aliases={k: k for k in range(nt)},
        scratch_shapes=[pltpu.SemaphoreType.DMA((n,)), pltpu.SemaphoreType.DMA((n,))],
    )(*[h for h, _ in hs])


def _split_start(name, bufs, ride, n, copies_of):
    nb = len(bufs)

    def body(*refs):
        sems = refs[nb + 1:nb + 1 + 2 * n]
        for cp in copies_of(refs[:nb], sems[:n], sems[n:]):
            cp.start()

    outs = _pc(
        body, name=name,
        out_shape=(pltpu.SemaphoreType.DMA(()),) * (2 * n) + tuple(pltpu.HBM(b.shape, b.dtype) for b in bufs)
        + (pltpu.HBM(ride.shape, ride.dtype),),
        in_specs=(HBM,) * (nb + 1), out_specs=(SEM,) * (2 * n) + (HBM,) * (nb + 1),
        input_output_aliases={k: 2 * n + k for k in range(nb + 1)},
        compiler_params=pltpu.CompilerParams(has_side_effects=EFFECT),
    )(*[pltpu.with_memory_space_constraint(b, pltpu.HBM) for b in bufs], pltpu.with_memory_space_constraint(ride, pltpu.HBM))
    return list(outs[:2 * n]), list(outs[2 * n:2 * n + nb]), outs[-1]


def _split_wait(name, bufs, sems, after, n, copies_of):
    nb = len(bufs)

    def body(*refs):
        s = refs[nb:nb + 2 * n]
        for cp in copies_of(refs[:nb], s[:n], s[n:]):
            cp.wait_send()
            cp.wait_recv()

    outs = _pc(
        body, name=name, out_shape=tuple(pltpu.HBM(b.shape, b.dtype) for b in bufs),
        in_specs=(HBM,) * nb + (SEM,) * (2 * n) + (pl.BlockSpec(memory_space=pl.ANY),), out_specs=(HBM,) * nb,
        input_output_aliases={k: k for k in range(nb)},
        compiler_params=pltpu.CompilerParams(has_side_effects=EFFECT),
    )(*bufs, *sems, after)
    return list(outs)


def _swap_copies(nt):
    def copies_of(refs, send, recv):
        x, y, c, _ = _place()
        cps = []
        for k in range(nt):
            rh = refs[k].shape[1] // 2
            cps.append(pltpu.make_async_remote_copy(
                src_ref=refs[k].at[:, pl.ds((1 - c) * rh, rh), :], dst_ref=refs[nt + k],
                send_sem=send[k], recv_sem=recv[k], device_id=(x, y, 1 - c), device_id_type=MESH))
        return cps
    return copies_of


def _exchange_copies(nt):
    def copies_of(refs, send, recv):
        x, y, c, chips = _place()
        cps = []
        for t in range(nt):
            for k, chip in enumerate(chips):
                cps.append(pltpu.make_async_remote_copy(
                    src_ref=refs[t].at[2 * chip[0] + chip[1]], dst_ref=refs[nt + t].at[k],
                    send_sem=send[3 * t + k], recv_sem=recv[3 * t + k], device_id=(*chip, c), device_id_type=MESH))
        return cps
    return copies_of


class _StagedReduce:
    def __init__(self, place):
        self.place = place
        self.groups = {}
        self.halves = {}

    @staticmethod
    def slab(t, r):
        return t.reshape(N_CHIPS, r, t.size // (N_CHIPS * r))

    def begin(self, g, grads, ride):
        names = list(grads)
        ts = [self.slab(t, r) for t, r in grads.values()]
        lands = [lax.empty((N_CHIPS, t.shape[1] // 2, t.shape[2]), F32) for t in ts]
        sems, bufs, ride = _split_start(f"grad_swap_start_{g}", ts + lands, ride, len(ts), _swap_copies(len(ts)))
        self.groups[g] = dict(names=names, bufs=bufs, sems=sems)
        return ride

    def advance(self, g, after, ride):
        st = self.groups[g]
        nt = len(st["names"])
        bufs = _split_wait(f"grad_swap_wait_{g}", st["bufs"], st["sems"], after, nt, _swap_copies(nt))
        st["ts"], st["ls"] = bufs[:nt], bufs[nt:]
        ps = [_pair_sum(t, l, self.place, f"pair_sum_{n}") for t, l, n in zip(st["ts"], st["ls"], st["names"])]
        lands = [lax.empty((3,) + p.shape[1:], BF16) for p in ps]
        st["sems"], st["bufs"], ride = _split_start(f"grad_exchange_start_{g}", ps + lands, ride, 3 * nt, _exchange_copies(nt))
        return ride

    def finish(self, g, after):
        st = self.groups[g]
        nt = len(st["names"])
        bufs = _split_wait(f"grad_exchange_wait_{g}", st["bufs"], st["sems"], after, 3 * nt, _exchange_copies(nt))
        for t, l, r, n in zip(st["ts"], st["ls"], bufs[nt:], st["names"]):
            if n[-1] in "01":
                self.halves[n[:-1]] = _final_sum(t, l, r, self.place, f"final_sum_{n}", layer=int(n[-1]), layers=2,
                                                 into=self.halves.get(n[:-1]))
            else:
                self.halves[n] = _final_sum(t, l, r, self.place, f"final_sum_{n}")


def _allgather_small(v):
    rows = v.shape[0]

    def body(v_ref, out_ref, send_sems, recv_sems):
        x, y, c, _ = _place()
        me = 4 * x + 2 * y + c
        out_ref[me] = v_ref[...]
        cps = []
        for k in range(1, 8):
            fx, fy, fc = (k >> 2) & 1, (k >> 1) & 1, k & 1
            to = (1 - x if fx else x, 1 - y if fy else y, 1 - c if fc else c)
            cps.append(pltpu.make_async_remote_copy(
                src_ref=v_ref, dst_ref=out_ref.at[me], send_sem=send_sems.at[k - 1], recv_sem=recv_sems.at[k - 1],
                device_id=to, device_id_type=MESH))
        for cp in cps:
            cp.start()
        for cp in cps:
            cp.wait()

    return _pc(
        body, name="allgather_small",
        in_specs=[pl.BlockSpec(memory_space=pltpu.VMEM)], out_specs=pl.BlockSpec(memory_space=pltpu.VMEM),
        out_shape=jax.ShapeDtypeStruct((8, rows, LANES), F32),
        scratch_shapes=[pltpu.SemaphoreType.DMA((7,)), pltpu.SemaphoreType.DMA((7,))],
    )(v)


def _adamw_math(w, g, m, v):
    m = ADAM_B1 * m + (1.0 - ADAM_B1) * g
    v = ADAM_B2 * v + (1.0 - ADAM_B2) * (g * g)
    m_hat = m / (1.0 - ADAM_B1 ** ADAM_STEP)
    v_hat = v / (1.0 - ADAM_B2 ** ADAM_STEP)
    return -ADAM_LR * (m_hat / (jnp.sqrt(v_hat) + ADAM_EPS) + ADAM_WD * w), m, v


def _adamw(w, g, m, v, name):
    r, cols = w.shape
    tr = min(r, 256)

    def body(w_ref, g_ref, m_ref, v_ref, d_ref, mo_ref, vo_ref):
        d, mn, vn = _adamw_math(w_ref[...], g_ref[...], m_ref[...], v_ref[...])
        d_ref[...] = d
        mo_ref[...] = mn
        vo_ref[...] = vn

    row = pl.BlockSpec((tr, cols), lambda i: (i, 0))
    return _pc(
        body, name=name, grid=(r // tr,), in_specs=[row] * 4, out_specs=[row] * 3,
        out_shape=[jax.ShapeDtypeStruct((r, cols), F32)] * 3,
        compiler_params=_params(("parallel",)),
    )(w, g, m, v)


def _adamw_small(w, gathered, m, v):
    rows = w.shape[0]

    def body(w_ref, g_ref, m_ref, v_ref, go_ref, d_ref, mo_ref, vo_ref):
        g = g_ref[0]
        for k in range(1, 8):
            g = g + g_ref[k]
        d, mn, vn = _adamw_math(w_ref[...], g, m_ref[...], v_ref[...])
        go_ref[...] = g
        d_ref[...] = d
        mo_ref[...] = mn
        vo_ref[...] = vn

    return _pc(
        body, name="adamw_small",
        out_shape=[jax.ShapeDtypeStruct((rows, LANES), F32)] * 4,
    )(w, gathered, m, v)


_BIAS_ROWS = 32


def _own_slot(flat, chip):
    return lax.dynamic_update_slice(lax.empty((N_CHIPS,) + flat.shape, flat.dtype), flat[None], (chip, 0, 0))


def _pack_first(hyb_w_in, hyb_w_out):
    return jnp.concatenate([t.astype(BF16).reshape(-1, 1024) for t in (hyb_w_in, hyb_w_out)], axis=0)


def _unpack_first(g):
    return {"hyb_w_in": g[:, 0:768, :].reshape(N_CHIPS, 1024, 768), "hyb_w_out": g[:, 768:1024, :].reshape(1024, 1024)}


def _pack_rest(mlp_w_up, mlp_w_down, swa_w_qkv, swa_w_out, swa_b_qkv):
    parts = [t.astype(BF16).reshape(-1, 1024) for t in (mlp_w_up, mlp_w_down, swa_w_qkv, swa_w_out)]
    bias = lax.bitcast_convert_type(swa_b_qkv.reshape(384), BF16).reshape(1, 768)
    bias = jnp.pad(bias, ((0, _BIAS_ROWS - 1), (0, 256)))
    return jnp.concatenate(parts + [bias], axis=0)


def _unpack_rest(g):
    W = {
        "packed": g,
        "swa_w_qkv": g[:, 4096:4480, :].reshape(N_CHIPS, 1024, 384),
        "swa_w_out": g[:, 4480:4736, :].reshape(1024, 1024),
    }
    bias = lax.bitcast_convert_type(g[:, 4736, :768].reshape(N_CHIPS, 384, 2), F32).reshape(1536)
    return W, bias


_SMALL = (("norm_mix", 16), ("norm_mlp", 16), ("ret_gn_gain", 4), ("dil_q_gain", 1), ("dil_k_gain", 1),
          ("swa_b_qkv", 12), ("swa_q_gain", 1), ("swa_k_gain", 1), ("swa_sinks", 1), ("loss", 1))
_SUBLANES = 8


def _slot(r):
    return -(-r // _SUBLANES) * _SUBLANES


def _pack_small(d):
    return jnp.concatenate([jnp.pad(d[n].reshape(r, LANES), ((0, _slot(r) - r), (0, 0))) for n, r in _SMALL], axis=0)


def _unpack_small(p):
    out, o = {}, 0
    for n, r in _SMALL:
        out[n] = p[o:o + r]
        o += _slot(r)
    return out


def kernel(x, positions, norm_mix, norm_mlp, mlp_w_up, mlp_w_down, hyb_w_in, hyb_w_out, ret_gn_gain, dil_q_gain, dil_k_gain, swa_w_qkv, swa_b_qkv, swa_w_out, swa_q_gain, swa_k_gain, swa_sinks, loss_target, m_norm_mix, m_norm_mlp, m_mlp_w_up, m_mlp_w_down, m_hyb_w_in, m_hyb_w_out, m_ret_gn_gain, m_dil_q_gain, m_dil_k_gain, m_swa_w_qkv, m_swa_b_qkv, m_swa_w_out, m_swa_q_gain, m_swa_k_gain, m_swa_sinks, v_norm_mix, v_norm_mlp, v_mlp_w_up, v_mlp_w_down, v_hyb_w_in, v_hyb_w_out, v_ret_gn_gain, v_dil_q_gain, v_dil_k_gain, v_swa_w_qkv, v_swa_b_qkv, v_swa_w_out, v_swa_q_gain, v_swa_k_gain, v_swa_sinks):
    ax, ay, ac = lax.axis_index("x"), lax.axis_index("y"), lax.axis_index("c")
    chip = 2 * ax + ay
    place = jnp.stack([chip, ac]).astype(jnp.int32)
    S = x.shape[1]

    first = _own_slot(_pack_first(hyb_w_in[0], hyb_w_out[0]), chip)
    rest = _own_slot(_pack_rest(mlp_w_up, mlp_w_down, swa_w_qkv[0], swa_w_out[0], swa_b_qkv[0]), chip)
    *sems, first, pos_col = _gather_start(first, positions.reshape(S, 1), "allgather_first_start")
    flight = {}

    def first_of(after):
        g = _gather_handover(_gather_wait(first, sems, after, "allgather_first_wait"), "allgather_first_handover")
        *flight["sems"], flight["buf"], g = _gather_start(rest, g, "allgather_rest_start")
        return _unpack_first(g)

    def rest_of(after):
        return _unpack_rest(_gather_handover(_gather_wait(flight["buf"], flight["sems"], after, "allgather_rest_wait"),
                                             "allgather_rest_handover"))

    P = dict(norm_mix=norm_mix, norm_mlp=norm_mlp, ret_gn_gain=ret_gn_gain, dil_q_gain=dil_q_gain, dil_k_gain=dil_k_gain,
             swa_q_gain=swa_q_gain, swa_k_gain=swa_k_gain, swa_sinks=swa_sinks)

    red = _StagedReduce(place)
    loss_l, grad_x, gp = _local_step(x[0], pos_col, loss_target[0], first_of, rest_of, P, red)

    params = dict(mlp_w_up=(mlp_w_up, m_mlp_w_up, v_mlp_w_up), mlp_w_down=(mlp_w_down, m_mlp_w_down, v_mlp_w_down),
                  hyb_w_in=(hyb_w_in, m_hyb_w_in, v_hyb_w_in), hyb_w_out=(hyb_w_out, m_hyb_w_out, v_hyb_w_out),
                  swa_w_qkv=(swa_w_qkv, m_swa_w_qkv, v_swa_w_qkv), swa_w_out=(swa_w_out, m_swa_w_out, v_swa_w_out))
    big = {}

    def update(names, share_name):
        hs = [(red.halves[n], params[n][0].shape[0]) for n in names]
        for n, g in zip(names, _share_halves(hs, share_name)):
            rows = g.shape[0]
            w, m, v = (t.reshape(rows, -1) for t in params[n])
            big[n] = [t.reshape(params[n][0].shape) for t in (g,) + tuple(_adamw(w, g, m, v, f"adamw_{n}"))]

    names = ["mlp_w_up", "mlp_w_down", "hyb_w_out", "swa_w_qkv", "swa_w_out"]
    red.halves[names[0]] = red.advance("win", grad_x, red.halves[names[0]])
    update(names, "grad_share_halves")
    red.finish("win", big[names[-1]][1])
    update(["hyb_w_in"], "grad_share_last")

    gsm = dict(gp, loss=loss_l)
    gsm["swa_sinks"] = jnp.pad(gp["swa_sinks"].reshape(16, HEAD)[:, 0], (0, LANES - 16))
    gathered = _allgather_small(_pack_small(gsm))

    def small_pack(norm_mix, norm_mlp, gn, dq, dk, b, sq, sk, sinks):
        dup = lambda t: jnp.tile(t.reshape(1, HEAD), (1, 2))
        bias = lax.dynamic_update_slice(jnp.zeros((12, LANES), F32), b.reshape(3, LANES), (3 * chip, 0))
        return _pack_small(dict(norm_mix=norm_mix, norm_mlp=norm_mlp, ret_gn_gain=gn, dil_q_gain=dup(dq), dil_k_gain=dup(dk),
                                swa_b_qkv=bias, swa_q_gain=dup(sq), swa_k_gain=dup(sk),
                                swa_sinks=jnp.pad(sinks.reshape(16), (0, LANES - 16)), loss=jnp.zeros((1, LANES), F32)))

    pw = small_pack(norm_mix, norm_mlp, ret_gn_gain, dil_q_gain, dil_k_gain, swa_b_qkv, swa_q_gain, swa_k_gain, swa_sinks)
    pm = small_pack(m_norm_mix, m_norm_mlp, m_ret_gn_gain, m_dil_q_gain, m_dil_k_gain, m_swa_b_qkv, m_swa_q_gain, m_swa_k_gain, m_swa_sinks)
    pv = small_pack(v_norm_mix, v_norm_mlp, v_ret_gn_gain, v_dil_q_gain, v_dil_k_gain, v_swa_b_qkv, v_swa_q_gain, v_swa_k_gain, v_swa_sinks)
    small = [_unpack_small(t) for t in _adamw_small(pw, gathered, pm, pv)]

    def small_out(n, k):
        t = small[k][n]
        if n in ("norm_mix", "norm_mlp"):
            return t.reshape(2, D_MODEL)
        if n == "ret_gn_gain":
            return t.reshape(1, RET_HEADS, 128)
        if n == "swa_b_qkv":
            return lax.dynamic_slice(t, (3 * chip, 0), (3, LANES)).reshape(1, 384)
        if n == "swa_sinks":
            return t[0, :16].reshape(1, 16)
        return t[0, :HEAD].reshape(1, HEAD)

    order = ["norm_mix", "norm_mlp", "mlp_w_up", "mlp_w_down", "hyb_w_in", "hyb_w_out", "ret_gn_gain", "dil_q_gain",
             "dil_k_gain", "swa_w_qkv", "swa_b_qkv", "swa_w_out", "swa_q_gain", "swa_k_gain", "swa_sinks"]
    is_big = {"mlp_w_up", "mlp_w_down", "hyb_w_in", "hyb_w_out", "swa_w_qkv", "swa_w_out"}
    outs = [small[0]["loss"][0, 0], grad_x[None]]
    for k in range(4):
        outs += [big[n][k] if n in is_big else small_out(n, k) for n in order]
    return tuple(outs)
```

```python
import functools
import math

import numpy as np
import jax
import jax.numpy as jnp
from jax import lax
from jax.experimental import pallas as pl
from jax.experimental.pallas import tpu as pltpu

F32, BF16 = jnp.float32, jnp.bfloat16
HIGHEST = lax.Precision.HIGHEST
MESH = pl.DeviceIdType.MESH

LANES = 128
VMEM_LIMIT = 48 << 20
D_MODEL = 1024
D_FF = 4096
HEAD = 64
EPS = 1e-6
BLK = 128
RET_HEADS = 4
RET_THETA = 10000.0
ROPE_THETA = 500000.0
ROPE_DIMS = 16
DIL_PATTERNS = ((128, 1), (512, 4), (2048, 16))
SWA_DIST = 127
N_CHIPS = 4
ADAM_LR, ADAM_B1, ADAM_B2, ADAM_EPS, ADAM_WD, ADAM_STEP = 0.001, 0.9, 0.999, 1e-08, 0.01, 10

_LOG_GAMMA = [float(np.log1p(-np.exp2(np.float32(-5.0 - h)))) for h in range(RET_HEADS)]


def _pc(body, **kw):
    return pl.pallas_call(body, **kw)


def _params(sem):
    return pltpu.CompilerParams(dimension_semantics=sem, vmem_limit_bytes=VMEM_LIMIT)


def _matmul(a, b, *, dims, tm, tn, tk, outs, name, epilogue=None, extras=(), b_cs=False, b_rs=0, b_row0=0, b_rows=0,
            o_cs=0, a_pro=None):
    if dims == "nn":
        M, K = a.shape
        N = b.shape[0] * b.shape[2] if b_cs else b.shape[1]
        a_spec = pl.BlockSpec((tm, tk), lambda i, j, k: (i, k))
        if b_cs:
            npt = b.shape[2] // tn
            b_spec = pl.BlockSpec((None, tk, tn), lambda i, j, k: (j // npt, k + b_row0, j % npt))
        elif b_rs:
            K, N, kps = b.shape[0] * b_rs, b.shape[2], b_rs // tk
            b_spec = pl.BlockSpec((None, tk, tn), lambda i, j, k: (k // kps, b_row0 + k % kps, j))
        else:
            b_spec = pl.BlockSpec((tk, tn), lambda i, j, k: (k, j))
        contract = (((1,), (0,)), ((), ()))
    elif dims == "nt":
        M, K = a.shape
        N = (b_rows or b.shape[1]) if b_cs else b.shape[0]
        a_spec = pl.BlockSpec((tm, tk), lambda i, j, k: (i, k))
        if b_cs:
            kpt = b.shape[2] // tk
            b_spec = pl.BlockSpec((None, tn, tk), lambda i, j, k: (k // kpt, j + b_row0, k % kpt))
        elif b_rs:
            N, jps = b.shape[0] * b_rs, b_rs // tn
            b_spec = pl.BlockSpec((None, tn, tk), lambda i, j, k: (j // jps, b_row0 + j % jps, k))
        else:
            b_spec = pl.BlockSpec((tn, tk), lambda i, j, k: (j, k))
        contract = (((1,), (1,)), ((), ()))
    else:
        K, M = a.shape
        N = b.shape[1]
        a_spec = pl.BlockSpec((tk, tm), lambda i, j, k: (k, i))
        b_spec = pl.BlockSpec((tk, tn), lambda i, j, k: (k, j))
        contract = (((0,), (0,)), ((), ()))
    assert M % tm == 0 and N % tn == 0 and K % tk == 0, (name, M, N, K, tm, tn, tk)
    nk = K // tk
    ex_specs = []
    for arr, kind in extras:
        if kind == "mn":
            ex_specs.append(pl.BlockSpec((tm, tn), lambda i, j, k: (i, j)))
        elif kind == "n":
            ex_specs.append(pl.BlockSpec((1, tn), lambda i, j, k: (0, j)))
        elif kind == "full":
            ex_specs.append(pl.BlockSpec(arr.shape, lambda i, j, k, nd=arr.ndim: (0,) * nd))
        else:
            ex_specs.append(pl.BlockSpec((tm, kind), lambda i, j, k: (i, 0)))
    if o_cs:
        n_sh = N // o_cs
        opt = n_sh // tn
        o_shape = (o_cs, M, n_sh)
        o_spec = pl.BlockSpec((None, tm, tn), lambda i, j, k: (j // opt, i, j % opt))
    else:
        o_shape = (M, N)
        o_spec = pl.BlockSpec((tm, tn), lambda i, j, k: (i, j))
    o_specs, o_shapes, summed = [], [], []
    for o in outs:
        if isinstance(o, tuple) and o[0] == "colsum":
            assert N == tn
            o_specs.append(pl.BlockSpec((1, tn), lambda i, j, k: (0, j)))
            o_shapes.append(jax.ShapeDtypeStruct((1, N), F32))
            summed.append(True)
        elif isinstance(o, tuple):
            o_specs.append(pl.BlockSpec((tm, o[1]), lambda i, j, k: (i, 0)))
            o_shapes.append(jax.ShapeDtypeStruct((M, o[1]), o[0]))
            summed.append(False)
        else:
            o_specs.append(o_spec)
            o_shapes.append(jax.ShapeDtypeStruct(o_shape, o))
            summed.append(False)
    n_ex, n_out = len(extras), len(outs)
    if epilogue is None:
        epilogue = lambda acc: (acc,)

    def body(a_ref, b_ref, *rest):
        ex, o_refs, acc = rest[:n_ex], rest[n_ex:n_ex + n_out], rest[-1]
        i, k = pl.program_id(0), pl.program_id(2)

        @pl.when(k == 0)
        def _():
            acc[...] = jnp.zeros_like(acc)

        av = a_ref[...] if a_pro is None else a_pro(a_ref[...])
        acc[...] += lax.dot_general(av.astype(BF16), b_ref[...].astype(BF16), contract, preferred_element_type=F32)

        @pl.when(k == nk - 1)
        def _():
            vals = epilogue(acc[...], *[e[...] for e in ex])
            for r, v, sm in zip(o_refs, vals, summed):
                if sm:
                    @pl.when(i == 0)
                    def _(r=r):
                        r[...] = jnp.zeros_like(r)

                    r[...] += v
                else:
                    r[...] = v.astype(r.dtype)

    res = _pc(
        body, name=name, grid=(M // tm, N // tn, nk),
        in_specs=[a_spec, b_spec] + ex_specs, out_specs=o_specs, out_shape=o_shapes,
        scratch_shapes=[pltpu.VMEM((tm, tn), F32)],
        compiler_params=_params(("arbitrary" if any(summed) else "parallel", "parallel", "arbitrary")),
    )(a, b, *[e for e, _ in extras])
    return res[0] if n_out == 1 else res


def _roll(x, s):
    return pltpu.roll(x, s % LANES, 1)


def _rope(x, A, B, C, half):
    return x * A + _roll(x, LANES - half) * B + _roll(x, half) * C


def _rope_t(g, A, B, C, half):
    return g * A + _roll(g * B, half) + _roll(g * C, LANES - half)


def _gmean(x, G):
    hi = x.astype(BF16)
    lo = (x - hi.astype(F32)).astype(BF16)
    Gb = G.astype(BF16)
    return jnp.dot(hi, Gb, preferred_element_type=F32) + jnp.dot(lo, Gb, preferred_element_type=F32)


def _head_mask(shape, half):
    lane = lax.broadcasted_iota(jnp.int32, shape, len(shape) - 1)
    return (lane >= HEAD) if half else (lane < HEAD)


def _group_matrix():
    i = np.arange(LANES)
    return jnp.asarray((i[:, None] // HEAD == i[None, :] // HEAD).astype(np.float32) / HEAD)


def _rope_inv():
    l = np.arange(LANES) % HEAD
    inv_r = np.power(np.float32(RET_THETA), -(l % 32).astype(np.float32) * np.float32(2.0 / HEAD))
    hp = ROPE_DIMS // 2
    inv_p = np.power(np.float32(ROPE_THETA), -(l % hp).astype(np.float32) * np.float32(2.0 / ROPE_DIMS))
    inv_p = np.where(l < ROPE_DIMS, inv_p, 0.0)
    return jnp.asarray(np.stack([inv_r, inv_p]).astype(np.float32))


def _tables(pos_col):
    S = pos_col.shape[0]
    tm = 512
    hp = ROPE_DIMS // 2

    def body(p_ref, inv_ref, o_ref):
        p = p_ref[...].astype(F32)
        lane = lax.broadcasted_iota(jnp.int32, (tm, LANES), 1) % HEAD
        ang = p * inv_ref[0:1, :]
        c, s = jnp.cos(ang), jnp.sin(ang)
        o_ref[:, 0:128] = c
        o_ref[:, 128:256] = jnp.where(lane < 32, -s, 0.0)
        o_ref[:, 256:384] = jnp.where(lane >= 32, s, 0.0)
        ang = p * inv_ref[1:2, :]
        c, s = jnp.cos(ang), jnp.sin(ang)
        o_ref[:, 384:512] = c
        o_ref[:, 512:640] = jnp.where(lane < hp, -s, 0.0)
        o_ref[:, 640:768] = jnp.where((lane >= hp) & (lane < ROPE_DIMS), s, 0.0)

    return _pc(
        body, name="rope_tables", grid=(S // tm,),
        in_specs=[pl.BlockSpec((tm, 1), lambda i: (i, 0)), pl.BlockSpec((2, LANES), lambda i: (0, 0))],
        out_specs=pl.BlockSpec((tm, 768), lambda i: (i, 0)),
        out_shape=jax.ShapeDtypeStruct((S, 768), F32),
        compiler_params=_params(("parallel",)),
    )(pos_col, _rope_inv())


def _tab(tab_ref, which):
    o = 384 * which
    return tab_ref[:, o:o + 128], tab_ref[:, o + 128:o + 256], tab_ref[:, o + 256:o + 384]


def _rms_fwd(x, g, name):
    S, Dm = x.shape
    tm = 512

    def body(x_ref, g_ref, h_ref):
        xv = x_ref[...]
        r = lax.rsqrt(jnp.mean(xv * xv, axis=-1, keepdims=True) + EPS)
        h_ref[...] = (xv * r * g_ref[...]).astype(BF16)

    return _pc(
        body, name=name, grid=(S // tm,),
        in_specs=[pl.BlockSpec((tm, Dm), lambda i: (i, 0)), pl.BlockSpec((1, Dm), lambda i: (0, 0))],
        out_specs=pl.BlockSpec((tm, Dm), lambda i: (i, 0)),
        out_shape=jax.ShapeDtypeStruct((S, Dm), BF16),
        compiler_params=_params(("parallel",)),
    )(x, g.reshape(1, Dm))


def _rms_bwd(x, g, dh, dres, name):
    S, Dm = x.shape
    tm = 512

    def body(x_ref, g_ref, dh_ref, dres_ref, dx_ref, dxb_ref, dg_ref):
        xv, dhv = x_ref[...], dh_ref[...]
        r = lax.rsqrt(jnp.mean(xv * xv, axis=-1, keepdims=True) + EPS)
        t = dhv * g_ref[...]
        dx = dres_ref[...] + r * t - xv * (r * r * r) * jnp.mean(xv * t, axis=-1, keepdims=True)
        dx_ref[...] = dx
        dxb_ref[...] = dx.astype(BF16)

        @pl.when(pl.program_id(0) == 0)
        def _():
            dg_ref[...] = jnp.zeros_like(dg_ref)

        dg_ref[...] += jnp.sum(dhv * xv * r, axis=0, keepdims=True)

    row = pl.BlockSpec((tm, Dm), lambda i: (i, 0))
    vec = pl.BlockSpec((1, Dm), lambda i: (0, 0))
    return _pc(
        body, name=name, grid=(S // tm,),
        in_specs=[row, vec, row, row], out_specs=[row, row, vec],
        out_shape=[jax.ShapeDtypeStruct((S, Dm), F32), jax.ShapeDtypeStruct((S, Dm), BF16),
                   jax.ShapeDtypeStruct((1, Dm), F32)],
        compiler_params=_params(("arbitrary",)),
    )(x, g.reshape(1, Dm), dh, dres)


def _hn_fwd(x, gain, G):
    r = lax.rsqrt(_gmean(x * x, G) + EPS)
    return x * r * gain


def _hn_bwd(x, gain, dy, G):
    r = lax.rsqrt(_gmean(x * x, G) + EPS)
    t = dy * gain
    dx = r * t - x * (r * r * r) * _gmean(x * t, G)
    return dx, jnp.sum(dy * x * r, axis=0, keepdims=True)


def _fold_halves(v):
    return v + _roll(v, HEAD)


def _even_pre_fwd(proj, tab, qg, kg):
    S = proj.shape[0]
    tm = 256

    def body(p_ref, tab_ref, qg_ref, kg_ref, g_ref, rq_ref, rk_ref, rv_ref, dq_ref, dk_ref, dv_ref):
        Ar, Br, Cr = _tab(tab_ref, 0)
        Ap, Bp, Cp = _tab(tab_ref, 1)
        G = g_ref[...]
        for c in range(2):
            sl = slice(c * 128, (c + 1) * 128)
            rq_ref[:, sl] = _rope(p_ref[:, c * 128:(c + 1) * 128], Ar, Br, Cr, 32).astype(BF16)
            rk_ref[:, sl] = (_rope(p_ref[:, 256 + c * 128:256 + (c + 1) * 128], Ar, Br, Cr, 32) * 0.125).astype(BF16)
        rv_ref[...] = p_ref[:, 512:1024].astype(BF16)
        for c in range(4):
            sl = slice(c * 128, (c + 1) * 128)
            q = _hn_fwd(p_ref[:, 1536 + c * 128:1536 + (c + 1) * 128], qg_ref[...], G)
            dq_ref[:, sl] = _rope(q, Ap, Bp, Cp, 8).astype(BF16)
            k = _hn_fwd(p_ref[:, 2048 + c * 128:2048 + (c + 1) * 128], kg_ref[...], G)
            dk_ref[:, sl] = _rope(k, Ap, Bp, Cp, 8).astype(BF16)
        dv_ref[...] = p_ref[:, 2560:3072].astype(BF16)

    row = lambda w: pl.BlockSpec((tm, w), lambda i: (i, 0))
    vec = pl.BlockSpec((1, LANES), lambda i: (0, 0))
    return _pc(
        body, name="even_pre_fwd", grid=(S // tm,),
        in_specs=[row(3072), row(768), vec, vec, pl.BlockSpec((LANES, LANES), lambda i: (0, 0))],
        out_specs=[row(256), row(256), row(512), row(512), row(512), row(512)],
        out_shape=[jax.ShapeDtypeStruct((S, w), BF16) for w in (256, 256, 512, 512, 512, 512)],
        compiler_params=_params(("parallel",)),
    )(proj, tab, qg, kg, _group_matrix())


def _even_pre_bwd(proj, tab, qg, kg, drq, drk, drv, drg, dqs, dks, dvs):
    S = proj.shape[0]
    tm = 256
    npat = len(dqs)

    def body(p_ref, tab_ref, qg_ref, kg_ref, g_ref, drq_ref, drk_ref, drv_ref, drg_ref, *rest):
        dq_refs, dk_refs, dv_refs = rest[:npat], rest[npat:2 * npat], rest[2 * npat:3 * npat]
        dp_ref, dqg_ref, dkg_ref = rest[3 * npat:]
        Ar, Br, Cr = _tab(tab_ref, 0)
        Ap, Bp, Cp = _tab(tab_ref, 1)
        G = g_ref[...]
        for c in range(2):
            sl = slice(c * 128, (c + 1) * 128)
            dp_ref[:, c * 128:(c + 1) * 128] = _rope_t(drq_ref[:, sl], Ar, Br, Cr, 32).astype(BF16)
            dp_ref[:, 256 + c * 128:256 + (c + 1) * 128] = _rope_t(drk_ref[:, sl] * 0.125, Ar, Br, Cr, 32).astype(BF16)
        dp_ref[:, 512:1024] = drv_ref[...].astype(BF16)
        dp_ref[:, 1024:1536] = drg_ref[...].astype(BF16)
        accq = jnp.zeros((1, LANES), F32)
        acck = jnp.zeros((1, LANES), F32)
        for c in range(4):
            sl = slice(c * 128, (c + 1) * 128)
            g = dq_refs[0][:, sl]
            for r in dq_refs[1:]:
                g = g + r[:, sl]
            dx, dg = _hn_bwd(p_ref[:, 1536 + c * 128:1536 + (c + 1) * 128], qg_ref[...], _rope_t(g, Ap, Bp, Cp, 8), G)
            dp_ref[:, 1536 + c * 128:1536 + (c + 1) * 128] = dx.astype(BF16)
            accq = accq + dg
            g = dk_refs[0][:, sl]
            for r in dk_refs[1:]:
                g = g + r[:, sl]
            dx, dg = _hn_bwd(p_ref[:, 2048 + c * 128:2048 + (c + 1) * 128], kg_ref[...], _rope_t(g, Ap, Bp, Cp, 8), G)
            dp_ref[:, 2048 + c * 128:2048 + (c + 1) * 128] = dx.astype(BF16)
            acck = acck + dg
        g = dv_refs[0][...]
        for r in dv_refs[1:]:
            g = g + r[...]
        dp_ref[:, 2560:3072] = g.astype(BF16)

        @pl.when(pl.program_id(0) == 0)
        def _():
            dqg_ref[...] = jnp.zeros_like(dqg_ref)
            dkg_ref[...] = jnp.zeros_like(dkg_ref)

        dqg_ref[...] += _fold_halves(accq)
        dkg_ref[...] += _fold_halves(acck)

    row = lambda w: pl.BlockSpec((tm, w), lambda i: (i, 0))
    vec = pl.BlockSpec((1, LANES), lambda i: (0, 0))
    return _pc(
        body, name="even_pre_bwd", grid=(S // tm,),
        in_specs=[row(3072), row(768), vec, vec, pl.BlockSpec((LANES, LANES), lambda i: (0, 0)),
                  row(256), row(256), row(512), row(512)] + [row(512)] * (3 * npat),
        out_specs=[row(3072), vec, vec],
        out_shape=[jax.ShapeDtypeStruct((S, 3072), BF16), jax.ShapeDtypeStruct((1, LANES), F32),
                   jax.ShapeDtypeStruct((1, LANES), F32)],
        compiler_params=_params(("arbitrary",)),
    )(proj, tab, qg, kg, _group_matrix(), drq, drk, drv, drg, *dqs, *dks, *dvs)


def _ret_consts(pair, half):
    lg = jnp.where(pair == 0, _LOG_GAMMA[half], _LOG_GAMMA[2 + half]).astype(F32)
    i = lax.broadcasted_iota(jnp.int32, (BLK, BLK), 0)
    j = lax.broadcasted_iota(jnp.int32, (BLK, BLK), 1)
    diff = (i - j).astype(F32)
    decay = jnp.where(diff >= 0, jnp.exp(lg * jnp.maximum(diff, 0.0)), 0.0)
    t = lax.broadcasted_iota(jnp.int32, (BLK, 1), 0).astype(F32)
    xi = jnp.exp(lg * (t + 1.0))
    zeta = jnp.exp(lg * (BLK - 1.0 - t))
    cd = jnp.exp(jnp.full((1, 1), BLK, F32) * lg)
    return decay, xi, zeta, cd


RET_STEP = 8


def _ret_fwd(rq, rk, rv):
    S = rq.shape[0]
    nc = S // BLK
    rows = RET_STEP * BLK

    def body(q_ref, k_ref, v_ref, o_ref, st_ref, R):
        p, n = pl.program_id(0), pl.program_id(1)

        @pl.when(n == 0)
        def _():
            R[...] = jnp.zeros_like(R)

        consts = [_ret_consts(p, half) for half in range(2)]
        masks = [_head_mask((BLK, LANES), half) for half in range(2)]
        for ci in range(RET_STEP):
            rs = slice(ci * BLK, (ci + 1) * BLK)
            q2, k2 = q_ref[rs, :], k_ref[rs, :]
            for half in range(2):
                decay, xi, zeta, cd = consts[half]
                m = masks[half]
                qm = jnp.where(m, q2, jnp.zeros_like(q2))
                km = jnp.where(m, k2, jnp.zeros_like(k2))
                v = v_ref[rs, half * 128:(half + 1) * 128]
                Rb = R[half].astype(BF16)
                st_ref[ci, half] = Rb
                sc = lax.dot_general(qm, k2, (((1,), (1,)), ((), ())), preferred_element_type=F32) * decay
                o = jnp.dot(sc.astype(BF16), v, preferred_element_type=F32)
                o = o + jnp.dot(qm, Rb, preferred_element_type=F32) * xi
                o_ref[rs, half * 128:(half + 1) * 128] = o
                kz = (km.astype(F32) * zeta).astype(BF16)
                R[half] = R[half] * cd + lax.dot_general(kz, v, (((0,), (0,)), ((), ())), preferred_element_type=F32)

    return _pc(
        body, name="ret_fwd", grid=(2, nc // RET_STEP),
        in_specs=[pl.BlockSpec((rows, 128), lambda p, n: (n, p)), pl.BlockSpec((rows, 128), lambda p, n: (n, p)),
                  pl.BlockSpec((rows, 256), lambda p, n: (n, p))],
        out_specs=[pl.BlockSpec((rows, 256), lambda p, n: (n, p)),
                   pl.BlockSpec((None, RET_STEP, 2, 128, 128), lambda p, n: (p, n, 0, 0, 0))],
        out_shape=[jax.ShapeDtypeStruct((S, 512), F32), jax.ShapeDtypeStruct((2, nc, 2, 128, 128), BF16)],
        scratch_shapes=[pltpu.VMEM((2, 128, 128), F32)],
        compiler_params=_params(("parallel", "arbitrary")),
    )(rq, rk, rv)


def _ret_bwd(rq, rk, rv, states, do):
    S = rq.shape[0]
    nc = S // BLK
    ns = nc // RET_STEP
    rows = RET_STEP * BLK
    nt = (((1,), (1,)), ((), ()))
    tn = (((0,), (0,)), ((), ()))

    def body(q_ref, k_ref, v_ref, st_ref, do_ref, dq_ref, dk_ref, dv_ref, U):
        p, n = pl.program_id(0), pl.program_id(1)

        @pl.when(n == 0)
        def _():
            U[...] = jnp.zeros_like(U)

        consts = [_ret_consts(p, half) for half in range(2)]
        masks = [_head_mask((BLK, LANES), half) for half in range(2)]
        for ci in reversed(range(RET_STEP)):
            rs = slice(ci * BLK, (ci + 1) * BLK)
            q2, k2 = q_ref[rs, :], k_ref[rs, :]
            dq_acc = jnp.zeros((BLK, LANES), F32)
            dk_acc = jnp.zeros((BLK, LANES), F32)
            for half in range(2):
                decay, xi, zeta, cd = consts[half]
                m = masks[half]
                qm = jnp.where(m, q2, jnp.zeros_like(q2))
                km = jnp.where(m, k2, jnp.zeros_like(k2))
                v = v_ref[rs, half * 128:(half + 1) * 128]
                dob = do_ref[rs, half * 128:(half + 1) * 128].astype(BF16)
                Rb = st_ref[ci, half]
                Ub = U[half].astype(BF16)
                dsc = (lax.dot_general(dob, v, nt, preferred_element_type=F32) * decay).astype(BF16)
                xdo = (dob.astype(F32) * xi).astype(BF16)
                dq_acc += jnp.dot(dsc, km, preferred_element_type=F32) + lax.dot_general(xdo, Rb, nt, preferred_element_type=F32)
                dk_acc += lax.dot_general(dsc, qm, tn, preferred_element_type=F32) \
                    + lax.dot_general(v, Ub, nt, preferred_element_type=F32) * zeta
                sc = (lax.dot_general(qm, k2, nt, preferred_element_type=F32) * decay).astype(BF16)
                kz = (km.astype(F32) * zeta).astype(BF16)
                dv_ref[rs, half * 128:(half + 1) * 128] = lax.dot_general(sc, dob, tn, preferred_element_type=F32) \
                    + jnp.dot(kz, Ub, preferred_element_type=F32)
                U[half] = U[half] * cd + lax.dot_general(qm, xdo, tn, preferred_element_type=F32)
            dq_ref[rs, :] = dq_acc
            dk_ref[rs, :] = dk_acc

    rev = lambda w: pl.BlockSpec((rows, w), lambda p, n: (ns - 1 - n, p))
    return _pc(
        body, name="ret_bwd", grid=(2, ns),
        in_specs=[rev(128), rev(128), rev(256),
                  pl.BlockSpec((None, RET_STEP, 2, 128, 128), lambda p, n: (p, ns - 1 - n, 0, 0, 0)), rev(256)],
        out_specs=[rev(128), rev(128), rev(256)],
        out_shape=[jax.ShapeDtypeStruct((S, 256), F32), jax.ShapeDtypeStruct((S, 256), F32),
                   jax.ShapeDtypeStruct((S, 512), F32)],
        scratch_shapes=[pltpu.VMEM((2, 128, 128), F32)],
        compiler_params=_params(("parallel", "arbitrary")),
    )(rq, rk, rv, states, do)


def _col_of(b, m):
    return jnp.max(jnp.where(m, b, -jnp.inf), axis=1, keepdims=True)


def _attn_fwd(q, k, v, *, nq, max_dist, name, sinks=None, want_bf16=False):
    L, Ck = k.shape
    nb, ncol = L // BLK, Ck // LANES
    scale = HEAD ** -0.5
    has_sink = sinks is not None

    def body(*refs):
        q_ref, kp_ref, kc_ref, vp_ref, vc_ref = refs[:5]
        sk_ref = refs[5] if has_sink else None
        outs = refs[5 + has_sink:]
        n = pl.program_id(1)
        kcat = jnp.concatenate([kp_ref[...], kc_ref[...]], axis=0)
        vcat = jnp.concatenate([vp_ref[...], vc_ref[...]], axis=0)
        r = lax.broadcasted_iota(jnp.int32, (BLK, 2 * BLK), 0)
        c = lax.broadcasted_iota(jnp.int32, (BLK, 2 * BLK), 1)
        dist = r + BLK - c
        valid = (dist >= 0) & (dist <= max_dist) & ((c >= BLK) | (n > 0))
        for i in range(nq):
            q2 = q_ref[:, i * 128:(i + 1) * 128]
            o2 = jnp.zeros((BLK, LANES), F32)
            l2 = jnp.zeros((BLK, LANES), F32)
            for half in range(2):
                m = _head_mask((BLK, LANES), half)
                qm = jnp.where(m, q2, jnp.zeros_like(q2))
                s = lax.dot_general(qm, kcat, (((1,), (1,)), ((), ())), preferred_element_type=F32) * scale
                s = jnp.where(valid, s, -jnp.inf)
                mx = jnp.max(s, axis=1, keepdims=True)
                if has_sink:
                    snk = _col_of(sk_ref[:, i * 128:(i + 1) * 128], _head_mask((1, LANES), half))
                    mx = jnp.maximum(mx, snk)
                pr = jnp.exp(s - mx)
                den = jnp.sum(pr, axis=1, keepdims=True)
                if has_sink:
                    den = den + jnp.exp(snk - mx)
                pv = jnp.dot(pr.astype(BF16), vcat, preferred_element_type=F32) / den
                o2 = jnp.where(m, pv, o2)
                l2 = jnp.where(m, mx + jnp.log(den), l2)
            outs[0][:, i * 128:(i + 1) * 128] = o2
            outs[1][:, i * 128:(i + 1) * 128] = l2
            if want_bf16:
                outs[2][:, i * 128:(i + 1) * 128] = o2.astype(BF16)

    qspec = pl.BlockSpec((BLK, nq * 128), lambda j, n: (n, j))
    cur = pl.BlockSpec((BLK, 128), lambda j, n: (n, j))
    prev = pl.BlockSpec((BLK, 128), lambda j, n: (jnp.maximum(n - 1, 0), j))
    in_specs = [qspec, prev, cur, prev, cur]
    args = [q, k, k, v, v]
    if has_sink:
        in_specs.append(pl.BlockSpec((1, nq * 128), lambda j, n: (0, j)))
        args.append(sinks)
    out_dts = [F32, F32] + ([BF16] if want_bf16 else [])
    return _pc(
        body, name=name, grid=(ncol, nb), in_specs=in_specs,
        out_specs=[qspec] * len(out_dts),
        out_shape=[jax.ShapeDtypeStruct(q.shape, dt) for dt in out_dts],
        compiler_params=_params(("parallel", "parallel")),
    )(*args)


def _attn_bwd(q, k, v, o, lse, do, *, nq, max_dist, name, sinks=None):
    L, Ck = k.shape
    nb, ncol = L // BLK, Ck // LANES
    scale = HEAD ** -0.5
    has_sink = sinks is not None
    nt = (((1,), (1,)), ((), ()))
    tn = (((0,), (0,)), ((), ()))

    def body(*refs):
        (qc_ref, qn_ref, kp_ref, kc_ref, vp_ref, vc_ref, oc_ref, on_ref, lc_ref, ln_ref, dc_ref, dn_ref) = refs[:12]
        sk_ref = refs[12] if has_sink else None
        outs = refs[12 + has_sink:]
        dq_ref, dk_ref, dv_ref = outs[:3]
        n = pl.program_id(1)
        kc, vc = kc_ref[...], vc_ref[...]
        kcat = jnp.concatenate([kp_ref[...], kc], axis=0)
        vcat = jnp.concatenate([vp_ref[...], vc], axis=0)
        r = lax.broadcasted_iota(jnp.int32, (BLK, 2 * BLK), 0)
        c = lax.broadcasted_iota(jnp.int32, (BLK, 2 * BLK), 1)
        dist = r + BLK - c
        valid_q = (dist >= 0) & (dist <= max_dist) & ((c >= BLK) | (n > 0))
        r2 = lax.broadcasted_iota(jnp.int32, (2 * BLK, BLK), 0)
        c2 = lax.broadcasted_iota(jnp.int32, (2 * BLK, BLK), 1)
        dist2 = r2 - c2
        valid_k = (dist2 >= 0) & (dist2 <= max_dist) & ((r2 < BLK) | (n < nb - 1))
        dk_acc = jnp.zeros((BLK, LANES), F32)
        dv_acc = jnp.zeros((BLK, LANES), F32)
        for i in range(nq):
            sl = slice(i * 128, (i + 1) * 128)
            qcur, docur = qc_ref[:, sl], dc_ref[:, sl]
            qcat = jnp.concatenate([qcur, qn_ref[:, sl]], axis=0)
            docat = jnp.concatenate([docur, dn_ref[:, sl]], axis=0)
            ocat = jnp.concatenate([oc_ref[:, sl], on_ref[:, sl]], axis=0)
            lcat = jnp.concatenate([lc_ref[:, sl], ln_ref[:, sl]], axis=0)
            dq2 = jnp.zeros((BLK, LANES), F32)
            ds2 = jnp.zeros((1, LANES), F32)
            for half in range(2):
                m1 = _head_mask((BLK, LANES), half)
                m2 = _head_mask((2 * BLK, LANES), half)
                dom = jnp.where(m2, docat, 0.0)
                delta = jnp.sum(dom * ocat, axis=1, keepdims=True)
                lcol = _col_of(lcat, m2)
                domb = dom.astype(BF16)
                qmcat = jnp.where(m2, qcat, jnp.zeros_like(qcat))
                qm = qmcat[:BLK]
                s = lax.dot_general(qm, kcat, nt, preferred_element_type=F32) * scale
                pr = jnp.where(valid_q, jnp.exp(s - lcol[:BLK]), 0.0)
                dp = lax.dot_general(domb[:BLK], vcat, nt, preferred_element_type=F32)
                ds = (pr * (dp - delta[:BLK])).astype(BF16)
                dq2 = jnp.where(m1, jnp.dot(ds, kcat, preferred_element_type=F32) * scale, dq2)
                if has_sink:
                    snk = _col_of(sk_ref[:, sl], _head_mask((1, LANES), half))
                    contrib = jnp.sum(-jnp.exp(snk - lcol[:BLK]) * delta[:BLK], axis=0, keepdims=True)
                    ds2 = jnp.where(_head_mask((1, LANES), half), contrib, ds2)
                s = lax.dot_general(qmcat, kc, nt, preferred_element_type=F32) * scale
                pr = jnp.where(valid_k, jnp.exp(s - lcol), 0.0)
                dv_acc += lax.dot_general(pr.astype(BF16), domb, tn, preferred_element_type=F32)
                dp = lax.dot_general(domb, vc, nt, preferred_element_type=F32)
                ds = (pr * (dp - delta)).astype(BF16)
                dk_acc += lax.dot_general(ds, qmcat, tn, preferred_element_type=F32) * scale
            dq_ref[:, sl] = dq2
            if has_sink:
                @pl.when(n == 0)
                def _():
                    outs[3][:, sl] = jnp.zeros((1, LANES), F32)

                outs[3][:, sl] += ds2
        dk_ref[...] = dk_acc
        dv_ref[...] = dv_acc

    qcur = pl.BlockSpec((BLK, nq * 128), lambda j, n: (n, j))
    qnext = pl.BlockSpec((BLK, nq * 128), lambda j, n: (jnp.minimum(n + 1, nb - 1), j))
    cur = pl.BlockSpec((BLK, 128), lambda j, n: (n, j))
    prev = pl.BlockSpec((BLK, 128), lambda j, n: (jnp.maximum(n - 1, 0), j))
    in_specs = [qcur, qnext, prev, cur, prev, cur, qcur, qnext, qcur, qnext, qcur, qnext]
    args = [q, q, k, k, v, v, o, o, lse, lse, do, do]
    out_specs = [qcur, cur, cur]
    out_shape = [jax.ShapeDtypeStruct(q.shape, F32), jax.ShapeDtypeStruct(k.shape, F32), jax.ShapeDtypeStruct(k.shape, F32)]
    if has_sink:
        vec = pl.BlockSpec((1, nq * 128), lambda j, n: (0, j))
        in_specs.append(vec)
        args.append(sinks)
        out_specs.append(vec)
        out_shape.append(jax.ShapeDtypeStruct((1, q.shape[1]), F32))
    return _pc(
        body, name=name, grid=(ncol, nb), in_specs=in_specs, out_specs=out_specs, out_shape=out_shape,
        compiler_params=_params(("parallel", "arbitrary")),
    )(*args)


ATT_TILE = 2048


def _rows(ref, start, n, r):
    if r == 1:
        return ref[pl.ds(start, n), :]
    return ref[pl.ds(start, n, stride=r), :]


def _set_rows(ref, start, n, r, val):
    if r == 1:
        ref[pl.ds(start, n), :] = val
    else:
        ref[pl.ds(start, n, stride=r), :] = val


def _band_geometry(S, patterns):
    rmax = max(r for _, r in patterns)
    H = BLK * rmax
    T = min(S, ATT_TILE)
    assert T % H == 0 and S % T == 0
    return H, T, S // T, T // BLK


def _band_fwd(q, k, v, *, patterns, nq, name, sinks=None, want_bf16=False):
    S, Ck = k.shape
    H, T, nt, nbt = _band_geometry(S, patterns)
    ncol = Ck // LANES
    scale = HEAD ** -0.5
    has_sink = sinks is not None
    nt_dims = (((1,), (1,)), ((), ()))

    def body(*refs):
        q_ref, kp_ref, kc_ref, vp_ref, vc_ref = refs[:5]
        sk_ref = refs[5] if has_sink else None
        n_out = 3 if want_bf16 else 2
        outs = refs[5 + has_sink:5 + has_sink + n_out]
        qf, kf, vf, M, L, A = refs[5 + has_sink + n_out:]
        t = pl.program_id(1)
        kf[0:H, :] = kp_ref[...].astype(F32)
        kf[H:H + T, :] = kc_ref[...].astype(F32)
        vf[0:H, :] = vp_ref[...].astype(F32)
        vf[H:H + T, :] = vc_ref[...].astype(F32)
        r_i = lax.broadcasted_iota(jnp.int32, (BLK, 2 * BLK), 0)
        c_i = lax.broadcasted_iota(jnp.int32, (BLK, 2 * BLK), 1)
        dist_i = r_i + BLK - c_i
        masks = [_head_mask((BLK, LANES), h) for h in range(2)]

        for i in range(nq):
            qf[...] = q_ref[:, i * 128:(i + 1) * 128].astype(F32) * scale
            for p, (dist, r) in enumerate(patterns):
                in_band = (dist_i >= 0) & (dist_i <= dist)
                in_band_first = in_band & ((c_i >= BLK) | (t > 0))

                def unit(j, b, p=p, r=r, in_band=in_band, in_band_first=in_band_first):
                    q0 = j + b * (BLK * r)
                    q2 = _rows(qf, q0, BLK, r).astype(BF16)
                    kcat = _rows(kf, H + q0 - BLK * r, 2 * BLK, r).astype(BF16)
                    vcat = _rows(vf, H + q0 - BLK * r, 2 * BLK, r).astype(BF16)
                    valid = in_band if b > 0 else in_band_first
                    m2 = jnp.zeros((BLK, LANES), F32)
                    l2 = jnp.zeros((BLK, LANES), F32)
                    a2 = jnp.zeros((BLK, LANES), F32)
                    for half in range(2):
                        m = masks[half]
                        qm = jnp.where(m, q2, jnp.zeros_like(q2))
                        s = lax.dot_general(qm, kcat, nt_dims, preferred_element_type=F32)
                        s = jnp.where(valid, s, -jnp.inf)
                        mx = jnp.max(s, axis=1, keepdims=True)
                        pr = jnp.exp(s - mx)
                        den = jnp.sum(pr, axis=1, keepdims=True)
                        pv = jnp.dot(pr.astype(BF16), vcat, preferred_element_type=F32)
                        m2 = jnp.where(m, mx, m2)
                        l2 = jnp.where(m, den, l2)
                        a2 = jnp.where(m, pv, a2)
                    if p > 0:
                        mo = _rows(M, q0, BLK, r)
                        mn = jnp.maximum(mo, m2)
                        wa, wb = jnp.exp(mo - mn), jnp.exp(m2 - mn)
                        l2 = wa * _rows(L, q0, BLK, r) + wb * l2
                        a2 = wa * _rows(A, q0, BLK, r) + wb * a2
                        m2 = mn
                    _set_rows(M, q0, BLK, r, m2)
                    _set_rows(L, q0, BLK, r, l2)
                    _set_rows(A, q0, BLK, r, a2)

                for u in range(nbt):
                    unit(u % r, u // r)
            sl = slice(i * 128, (i + 1) * 128)
            mm, ll, aa = M[...], L[...], A[...]
            if has_sink:
                snk = sk_ref[:, sl]
                mn = jnp.maximum(mm, snk)
                w = jnp.exp(mm - mn)
                ll = ll * w + jnp.exp(snk - mn)
                aa = aa * w
                mm = mn
            o = aa / ll
            outs[0][:, sl] = o
            outs[1][:, sl] = mm + jnp.log(ll)
            if want_bf16:
                outs[2][:, sl] = o.astype(BF16)

    th = T // H
    qspec = pl.BlockSpec((T, nq * 128), lambda j, t: (t, j))
    cur = pl.BlockSpec((T, 128), lambda j, t: (t, j))
    prev = pl.BlockSpec((H, 128), lambda j, t: (jnp.maximum(t * th - 1, 0), j))
    in_specs = [qspec, prev, cur, prev, cur]
    args = [q, k, k, v, v]
    if has_sink:
        in_specs.append(pl.BlockSpec((1, nq * 128), lambda j, t: (0, j)))
        args.append(sinks)
    out_dts = [F32, F32] + ([BF16] if want_bf16 else [])
    return _pc(
        body, name=name, grid=(ncol, nt), in_specs=in_specs,
        out_specs=[qspec] * len(out_dts),
        out_shape=[jax.ShapeDtypeStruct(q.shape, dt) for dt in out_dts],
        scratch_shapes=[pltpu.VMEM((T, LANES), F32), pltpu.VMEM((H + T, LANES), F32), pltpu.VMEM((H + T, LANES), F32),
                        pltpu.VMEM((T, LANES), F32), pltpu.VMEM((T, LANES), F32), pltpu.VMEM((T, LANES), F32)],
        compiler_params=_params(("parallel", "parallel")),
    )(*args)


def _band_bwd(q, k, v, lse, delta, do, *, patterns, nq, name, sinks=None, do_col0=0):
    S, Ck = k.shape
    H, T, nt, nbt = _band_geometry(S, patterns)
    ncol = Ck // LANES
    scale = HEAD ** -0.5
    has_sink = sinks is not None
    nt_dims = (((1,), (1,)), ((), ()))
    tn_dims = (((0,), (0,)), ((), ()))

    def body(*refs):
        (qc_ref, qn_ref, kp_ref, kc_ref, vp_ref, vc_ref, lc_ref, ln_ref, ec_ref, en_ref, dc_ref, dn_ref) = refs[:12]
        sk_ref = refs[12] if has_sink else None
        n_out = 4 if has_sink else 3
        outs = refs[12 + has_sink:12 + has_sink + n_out]
        dq_ref, dk_ref, dv_ref = outs[:3]
        qf, kf, vf, lf, ef, df = refs[12 + has_sink + n_out:]
        t = pl.program_id(1)
        kf[0:H, :] = kp_ref[...].astype(F32)
        kf[H:H + T, :] = kc_ref[...].astype(F32)
        vf[0:H, :] = vp_ref[...].astype(F32)
        vf[H:H + T, :] = vc_ref[...].astype(F32)
        dk_ref[...] = jnp.zeros_like(dk_ref)
        dv_ref[...] = jnp.zeros_like(dv_ref)
        r_i = lax.broadcasted_iota(jnp.int32, (BLK, 2 * BLK), 0)
        c_i = lax.broadcasted_iota(jnp.int32, (BLK, 2 * BLK), 1)
        dist_q = r_i + BLK - c_i
        dist_h = dist_q[:, :BLK]
        m1 = [_head_mask((BLK, LANES), h) for h in range(2)]

        def head_inputs(half, q2, do2, l2, e2):
            m = m1[half]
            lh = jnp.where(m, l2, _roll(l2, HEAD))
            eh = jnp.where(m, e2, _roll(e2, HEAD))
            return jnp.where(m, q2, jnp.zeros_like(q2)), jnp.where(m, do2, 0.0).astype(BF16), lh, eh

        for i in range(nq):
            sl = slice(i * 128, (i + 1) * 128)
            qf[0:T, :] = qc_ref[:, sl].astype(F32) * scale
            qf[T:T + H, :] = qn_ref[:, sl].astype(F32) * scale
            for buf, c_ref, n_ref in ((lf, lc_ref, ln_ref), (ef, ec_ref, en_ref), (df, dc_ref, dn_ref)):
                buf[0:T, :] = c_ref[:, sl]
                buf[T:T + H, :] = n_ref[:, sl]
            if has_sink:
                @pl.when(t == 0)
                def _():
                    outs[3][:, sl] = jnp.zeros((1, LANES), F32)

                outs[3][:, sl] += jnp.sum(-jnp.exp(sk_ref[:, sl] - lc_ref[:, sl]) * ec_ref[:, sl], axis=0, keepdims=True)
            for p, (dist, r) in enumerate(patterns):
                band_q = (dist_q >= 0) & (dist_q <= dist)
                band_first = band_q & ((c_i >= BLK) | (t > 0))
                band_h = (dist_h >= 0) & (dist_h <= dist)

                def add_rows(ref, start, val, r=r):
                    _set_rows(ref, start, BLK, r, _rows(ref, start, BLK, r) + val)

                def unit(j, b, p=p, r=r, band_q=band_q, band_first=band_first):
                    q0 = j + b * (BLK * r)
                    q2 = _rows(qf, q0, BLK, r).astype(BF16)
                    do2, l2, e2 = _rows(df, q0, BLK, r), _rows(lf, q0, BLK, r), _rows(ef, q0, BLK, r)
                    kcat = _rows(kf, H + q0 - BLK * r, 2 * BLK, r).astype(BF16)
                    vcat = _rows(vf, H + q0 - BLK * r, 2 * BLK, r).astype(BF16)
                    valid = band_q if b > 0 else band_first
                    dq2 = jnp.zeros((BLK, LANES), F32)
                    dkc = jnp.zeros((2 * BLK, LANES), F32)
                    dvc = jnp.zeros((2 * BLK, LANES), F32)
                    for half in range(2):
                        qm, dom, lh, eh = head_inputs(half, q2, do2, l2, e2)
                        s = lax.dot_general(qm, kcat, nt_dims, preferred_element_type=F32)
                        pr = jnp.where(valid, jnp.exp(s - jnp.concatenate([lh, lh], axis=1)), 0.0)
                        dp = lax.dot_general(dom, vcat, nt_dims, preferred_element_type=F32)
                        ds = (pr * (dp - jnp.concatenate([eh, eh], axis=1))).astype(BF16)
                        dq2 = jnp.where(m1[half], jnp.dot(ds, kcat, preferred_element_type=F32) * scale, dq2)
                        dvc += lax.dot_general(pr.astype(BF16), dom, tn_dims, preferred_element_type=F32)
                        dkc += lax.dot_general(ds, qm, tn_dims, preferred_element_type=F32)
                    if p > 0:
                        dq2 = dq2 + _rows(dq_ref.at[:, sl], q0, BLK, r)
                    _set_rows(dq_ref.at[:, sl], q0, BLK, r, dq2)
                    add_rows(dk_ref, q0, dkc[BLK:])
                    add_rows(dv_ref, q0, dvc[BLK:])
                    if b > 0:
                        add_rows(dk_ref, q0 - BLK * r, dkc[:BLK])
                        add_rows(dv_ref, q0 - BLK * r, dvc[:BLK])

                def halo_unit(j, r=r, band_h=band_h):
                    k0 = j + (nbt // r - 1) * (BLK * r)
                    q2 = _rows(qf, T + j, BLK, r).astype(BF16)
                    do2, l2, e2 = _rows(df, T + j, BLK, r), _rows(lf, T + j, BLK, r), _rows(ef, T + j, BLK, r)
                    kc = _rows(kf, H + k0, BLK, r).astype(BF16)
                    vc = _rows(vf, H + k0, BLK, r).astype(BF16)
                    dk2 = jnp.zeros((BLK, LANES), F32)
                    dv2 = jnp.zeros((BLK, LANES), F32)
                    for half in range(2):
                        qm, dom, lh, eh = head_inputs(half, q2, do2, l2, e2)
                        s = lax.dot_general(qm, kc, nt_dims, preferred_element_type=F32)
                        pr = jnp.where(band_h, jnp.exp(s - lh), 0.0)
                        dp = lax.dot_general(dom, vc, nt_dims, preferred_element_type=F32)
                        ds = (pr * (dp - eh)).astype(BF16)
                        dv2 += lax.dot_general(pr.astype(BF16), dom, tn_dims, preferred_element_type=F32)
                        dk2 += lax.dot_general(ds, qm, tn_dims, preferred_element_type=F32)
                    add_rows(dk_ref, k0, dk2)
                    add_rows(dv_ref, k0, dv2)

                for u in range(nbt):
                    unit(u % r, u // r)
                if nt > 1:
                    @pl.when(t < nt - 1)
                    def _(r=r, halo_unit=halo_unit):
                        for j in range(r):
                            halo_unit(j)

    th = T // H
    last = S // H - 1
    qcur = pl.BlockSpec((T, nq * 128), lambda j, t: (t, j))
    qnext = pl.BlockSpec((H, nq * 128), lambda j, t: (jnp.minimum((t + 1) * th, last), j))
    cur = pl.BlockSpec((T, 128), lambda j, t: (t, j))
    prev = pl.BlockSpec((H, 128), lambda j, t: (jnp.maximum(t * th - 1, 0), j))
    dcur = pl.BlockSpec((T, nq * 128), lambda j, t: (t, j + do_col0))
    dnext = pl.BlockSpec((H, nq * 128), lambda j, t: (jnp.minimum((t + 1) * th, last), j + do_col0))
    in_specs = [qcur, qnext, prev, cur, prev, cur, qcur, qnext, qcur, qnext, dcur, dnext]
    args = [q, q, k, k, v, v, lse, lse, delta, delta, do, do]
    out_specs = [qcur, cur, cur]
    out_shape = [jax.ShapeDtypeStruct(q.shape, F32), jax.ShapeDtypeStruct(k.shape, F32), jax.ShapeDtypeStruct(k.shape, F32)]
    if has_sink:
        vec = pl.BlockSpec((1, nq * 128), lambda j, t: (0, j))
        in_specs.append(vec)
        args.append(sinks)
        out_specs.append(vec)
        out_shape.append(jax.ShapeDtypeStruct((1, q.shape[1]), F32))
    big = pltpu.VMEM((T + H, LANES), F32)
    return _pc(
        body, name=name, grid=(ncol, nt), in_specs=in_specs, out_specs=out_specs, out_shape=out_shape,
        scratch_shapes=[big] * 6,
        compiler_params=_params(("parallel", "arbitrary")),
    )(*args)


def _delta(do, o, name):
    S, C = do.shape
    tm = 512

    def body(do_ref, o_ref, g_ref, e_ref):
        for c in range(C // LANES):
            sl = slice(c * 128, (c + 1) * 128)
            e_ref[:, sl] = _gmean(do_ref[:, sl] * o_ref[:, sl], g_ref[...]) * float(HEAD)

    row = pl.BlockSpec((tm, C), lambda i: (i, 0))
    return _pc(
        body, name=name, grid=(S // tm,),
        in_specs=[row, row, pl.BlockSpec((LANES, LANES), lambda i: (0, 0))], out_specs=row,
        out_shape=jax.ShapeDtypeStruct((S, C), F32),
        compiler_params=_params(("parallel",)),
    )(do, o, _group_matrix())


def _even_post_fwd(ro, proj, gn, da):
    S = ro.shape[0]
    tm = 256

    def body(ro_ref, rg_ref, gn_ref, da_ref, mix_ref):
        for c in range(4):
            sl = slice(c * 128, (c + 1) * 128)
            x = ro_ref[:, sl]
            mu = jnp.mean(x, axis=1, keepdims=True)
            xc = x - mu
            var = jnp.mean(xc * xc, axis=1, keepdims=True)
            y = xc * lax.rsqrt(var + EPS) * gn_ref[:, sl]
            z = rg_ref[:, sl]
            mix_ref[:, sl] = (z * jax.nn.sigmoid(z) * y).astype(BF16)
        mix_ref[:, 512:1024] = da_ref[...].astype(BF16)

    row = lambda w: pl.BlockSpec((tm, w), lambda i: (i, 0))
    return _pc(
        body, name="even_post_fwd", grid=(S // tm,),
        in_specs=[row(512), pl.BlockSpec((tm, 512), lambda i: (i, 2)), pl.BlockSpec((1, 512), lambda i: (0, 0)), row(512)],
        out_specs=row(1024), out_shape=jax.ShapeDtypeStruct((S, 1024), BF16),
        compiler_params=_params(("parallel",)),
    )(ro, proj, gn, da)


def _even_post_bwd(ro, proj, gn, dmixed):
    S = ro.shape[0]
    tm = 256

    def body(ro_ref, rg_ref, gn_ref, dm_ref, dro_ref, drg_ref, dgn_ref):
        @pl.when(pl.program_id(0) == 0)
        def _():
            dgn_ref[...] = jnp.zeros_like(dgn_ref)

        for c in range(4):
            sl = slice(c * 128, (c + 1) * 128)
            x = ro_ref[:, sl]
            mu = jnp.mean(x, axis=1, keepdims=True)
            xc = x - mu
            rstd = lax.rsqrt(jnp.mean(xc * xc, axis=1, keepdims=True) + EPS)
            xh = xc * rstd
            gain = gn_ref[:, sl]
            y = xh * gain
            z = rg_ref[:, sl]
            sg = jax.nn.sigmoid(z)
            dra = dm_ref[:, sl]
            drg_ref[:, sl] = dra * y * sg * (1.0 + z * (1.0 - sg))
            dy = dra * z * sg
            dgn_ref[:, sl] += jnp.sum(dy * xh, axis=0, keepdims=True)
            dxh = dy * gain
            dro_ref[:, sl] = rstd * (dxh - jnp.mean(dxh, axis=1, keepdims=True)
                                     - xh * jnp.mean(dxh * xh, axis=1, keepdims=True))

    row = lambda w: pl.BlockSpec((tm, w), lambda i: (i, 0))
    vec = pl.BlockSpec((1, 512), lambda i: (0, 0))
    return _pc(
        body, name="even_post_bwd", grid=(S // tm,),
        in_specs=[row(512), pl.BlockSpec((tm, 512), lambda i: (i, 2)), vec, row(512)],
        out_specs=[row(512), row(512), vec],
        out_shape=[jax.ShapeDtypeStruct((S, 512), F32), jax.ShapeDtypeStruct((S, 512), F32),
                   jax.ShapeDtypeStruct((1, 512), F32)],
        compiler_params=_params(("arbitrary",)),
    )(ro, proj, gn, dmixed)


def _swa_pre_fwd(proj, tab, qg, kg):
    S = proj.shape[0]
    tm = 256

    def body(p_ref, tab_ref, qg_ref, kg_ref, g_ref, q_ref, k_ref, v_ref):
        Ap, Bp, Cp = _tab(tab_ref, 1)
        G = g_ref[...]
        lo = _head_mask((tm, LANES), 0)
        for c in range(8):
            sl = slice(c * 128, (c + 1) * 128)
            q_ref[:, sl] = _rope(_hn_fwd(p_ref[:, sl], qg_ref[...], G), Ap, Bp, Cp, 8).astype(BF16)
        for c in range(2):
            kn = _rope(_hn_fwd(p_ref[:, 1024 + c * 128:1024 + (c + 1) * 128], kg_ref[...], G), Ap, Bp, Cp, 8)
            vv = p_ref[:, 1280 + c * 128:1280 + (c + 1) * 128]
            for t, ref in ((kn, k_ref), (vv, v_ref)):
                sw = _roll(t, HEAD)
                ref[:, (2 * c) * 128:(2 * c + 1) * 128] = jnp.where(lo, t, sw).astype(BF16)
                ref[:, (2 * c + 1) * 128:(2 * c + 2) * 128] = jnp.where(lo, sw, t).astype(BF16)

    row = lambda w: pl.BlockSpec((tm, w), lambda i: (i, 0))
    vec = pl.BlockSpec((1, LANES), lambda i: (0, 0))
    return _pc(
        body, name="swa_pre_fwd", grid=(S // tm,),
        in_specs=[row(1536), row(768), vec, vec, pl.BlockSpec((LANES, LANES), lambda i: (0, 0))],
        out_specs=[row(1024), row(512), row(512)],
        out_shape=[jax.ShapeDtypeStruct((S, w), BF16) for w in (1024, 512, 512)],
        compiler_params=_params(("parallel",)),
    )(proj, tab, qg, kg, _group_matrix())


def _swa_pre_bwd(proj, tab, qg, kg, dq, dk, dv):
    S = proj.shape[0]
    tm = 256

    def body(p_ref, tab_ref, qg_ref, kg_ref, g_ref, dq_ref, dk_ref, dv_ref, dp_ref, db_ref, dqg_ref, dkg_ref):
        Ap, Bp, Cp = _tab(tab_ref, 1)
        G = g_ref[...]
        lo = _head_mask((tm, LANES), 0)

        @pl.when(pl.program_id(0) == 0)
        def _():
            db_ref[...] = jnp.zeros_like(db_ref)
            dqg_ref[...] = jnp.zeros_like(dqg_ref)
            dkg_ref[...] = jnp.zeros_like(dkg_ref)

        accq = jnp.zeros((1, LANES), F32)
        acck = jnp.zeros((1, LANES), F32)
        for c in range(8):
            sl = slice(c * 128, (c + 1) * 128)
            dx, dg = _hn_bwd(p_ref[:, sl], qg_ref[...], _rope_t(dq_ref[:, sl], Ap, Bp, Cp, 8), G)
            dp_ref[:, sl] = dx.astype(BF16)
            db_ref[:, sl] += jnp.sum(dx, axis=0, keepdims=True)
            accq = accq + dg
        for c in range(2):
            folded = []
            for ref in (dk_ref, dv_ref):
                a = ref[:, (2 * c) * 128:(2 * c + 1) * 128]
                b = ref[:, (2 * c + 1) * 128:(2 * c + 2) * 128]
                folded.append(jnp.where(lo, a + _roll(a, HEAD), b + _roll(b, HEAD)))
            ks = slice(1024 + c * 128, 1024 + (c + 1) * 128)
            dx, dg = _hn_bwd(p_ref[:, ks], kg_ref[...], _rope_t(folded[0], Ap, Bp, Cp, 8), G)
            dp_ref[:, ks] = dx.astype(BF16)
            db_ref[:, ks] += jnp.sum(dx, axis=0, keepdims=True)
            acck = acck + dg
            vs = slice(1280 + c * 128, 1280 + (c + 1) * 128)
            dp_ref[:, vs] = folded[1].astype(BF16)
            db_ref[:, vs] += jnp.sum(folded[1], axis=0, keepdims=True)
        dqg_ref[...] += _fold_halves(accq)
        dkg_ref[...] += _fold_halves(acck)

    row = lambda w: pl.BlockSpec((tm, w), lambda i: (i, 0))
    vec = pl.BlockSpec((1, LANES), lambda i: (0, 0))
    return _pc(
        body, name="swa_pre_bwd", grid=(S // tm,),
        in_specs=[row(1536), row(768), vec, vec, pl.BlockSpec((LANES, LANES), lambda i: (0, 0)),
                  row(1024), row(512), row(512)],
        out_specs=[row(1536), pl.BlockSpec((1, 1536), lambda i: (0, 0)), vec, vec],
        out_shape=[jax.ShapeDtypeStruct((S, 1536), BF16), jax.ShapeDtypeStruct((1, 1536), F32),
                   jax.ShapeDtypeStruct((1, LANES), F32), jax.ShapeDtypeStruct((1, LANES), F32)],
        compiler_params=_params(("arbitrary",)),
    )(proj, tab, qg, kg, _group_matrix(), dq, dk, dv)


def _loss_head(y, target):
    S, Dm = y.shape
    tm = 512

    def body(y_ref, t_ref, l_ref, dy_ref, dyb_ref):
        @pl.when(pl.program_id(0) == 0)
        def _():
            l_ref[...] = jnp.zeros_like(l_ref)

        e = y_ref[...] - t_ref[...]
        dy = e * (1.0 / Dm)
        dy_ref[...] = dy
        dyb_ref[...] = dy.astype(BF16)
        row = jnp.sum(e * e, axis=1, keepdims=True) * (0.5 / Dm)
        l_ref[...] += jnp.sum(row, axis=0, keepdims=True)

    row = pl.BlockSpec((tm, Dm), lambda i: (i, 0))
    return _pc(
        body, name="loss_head", grid=(S // tm,), in_specs=[row, row],
        out_specs=[pl.BlockSpec((1, LANES), lambda i: (0, 0)), row, row],
        out_shape=[jax.ShapeDtypeStruct((1, LANES), F32), jax.ShapeDtypeStruct((S, Dm), F32),
                   jax.ShapeDtypeStruct((S, Dm), BF16)],
        compiler_params=_params(("arbitrary",)),
    )(y, target)


def _relu2_of(u):
    r = jnp.maximum(u.astype(F32), 0.0)
    return r * r


def _drelu2(acc, u):
    return (acc * 2.0 * jnp.maximum(u.astype(F32), 0.0),)


def _add(acc, res):
    return (acc + res,)


def _add_norm_in(res, g):
    def epilogue(acc, r, gv):
        xn = acc + r
        return xn, xn * lax.rsqrt(jnp.mean(xn * xn, axis=-1, keepdims=True) + EPS) * gv

    return dict(outs=[F32, BF16], epilogue=epilogue, extras=[(res, "mn"), (g.reshape(1, D_MODEL), "n")])


_T = dict(tm=1024, tn=1024, tk=1024)


def _rms_bwd_in(x, g, dres):
    def epilogue(dh, xv, gv, dr):
        r = lax.rsqrt(jnp.mean(xv * xv, axis=-1, keepdims=True) + EPS)
        t = dh * gv
        dx = dr + r * t - xv * (r * r * r) * jnp.mean(xv * t, axis=-1, keepdims=True)
        return dx, dx, jnp.sum(dh * xv * r, axis=0, keepdims=True)

    return dict(outs=[F32, BF16, ("colsum",)], epilogue=epilogue,
                extras=[(x, "mn"), (g.reshape(1, D_MODEL), "n"), (dres, "mn")])


def _delta_in(o, col0):
    width = D_MODEL - col0

    def epilogue(do, ov, G):
        parts = [_gmean(do[:, col0 + c * 128:col0 + (c + 1) * 128] * ov[:, c * 128:(c + 1) * 128], G) * float(HEAD)
                 for c in range(width // LANES)]
        return do, jnp.concatenate(parts, axis=1)

    return dict(outs=[F32, (F32, width)], epilogue=epilogue, extras=[(o, width), (_group_matrix(), "full")])


def _mlp_fwd(x, h, wts, layer, tag, next_gain=None):
    u = _matmul(h, wts, dims="nn", **_T, outs=[BF16], b_cs=True, b_row0=layer, name=f"mlp_up{tag}")
    tail = _add_norm_in(x, next_gain) if next_gain is not None else dict(outs=[F32], epilogue=_add, extras=[(x, "mn")])
    out = _matmul(u, wts, dims="nn", **_T, a_pro=_relu2_of, b_rs=1024, b_row0=2 + layer, name=f"mlp_down{tag}", **tail)
    return out, (h, u)


def _mlp_bwd(x, g, wts, layer, saved, dy, dyb, tag):
    h, u = saved
    du = _matmul(dyb, wts, dims="nt", **_T, outs=[BF16], epilogue=_drelu2, extras=[(u, "mn")], b_rs=1024,
                 b_row0=2 + layer, name=f"mlp_du{tag}")
    dw_dn = _matmul(u, dyb, dims="tn", **_T, outs=[F32], a_pro=_relu2_of, name=f"mlp_dwdown{tag}")
    dw_up = _matmul(h, du, dims="tn", **_T, outs=[F32], o_cs=N_CHIPS, name=f"mlp_dwup{tag}")
    dx, dxb, dg = _matmul(du, wts, dims="nt", tm=512, tn=1024, tk=1024, b_cs=True, b_row0=layer, b_rows=1024, name=f"mlp_dh{tag}",
                          **_rms_bwd_in(x, g, dy))
    return dx, dxb, dg, dw_up, dw_dn


def _pattern_view(t, r):
    S, C = t.shape
    return t.reshape(S // r, r * C)


def _local_step(x, pos_col, target, first_of, rest_begin, rest_of, P, red):
    S = x.shape[0]
    tab = _tables(pos_col)
    tile2 = lambda g: jnp.tile(g.reshape(1, HEAD), (1, 2))
    dqg, dkg = tile2(P["dil_q_gain"]), tile2(P["dil_k_gain"])
    sqg, skg = tile2(P["swa_q_gain"]), tile2(P["swa_k_gain"])
    gn = P["ret_gn_gain"].reshape(1, 512)
    sink_b = jnp.repeat(P["swa_sinks"].reshape(16), HEAD).reshape(1, 1024)

    h0 = _rms_fwd(x, P["norm_mix"][0], "rms_mix_fwd0")
    W = first_of(h0)
    proj = _matmul(h0, W["hyb_w_in"], dims="nn", tm=1024, tn=768, tk=1024, outs=[F32], b_cs=True, name="hyb_in")
    rq, rk, rv, dq, dk, dv = _even_pre_fwd(proj, tab, dqg, dkg)
    ro, states = _ret_fwd(rq, rk, rv)
    dil = [(w // r, r) for w, r in DIL_PATTERNS]
    da, dlse = _band_fwd(dq, dk, dv, patterns=dil, nq=1, name="dil_fwd")
    da = rest_begin(da)
    mixed = _even_post_fwd(ro, proj, gn, da)
    x1, h1 = _matmul(mixed, W["hyb_w_out"], dims="nn", **_T, name="hyb_out", **_add_norm_in(x, P["norm_mlp"][0]))
    rest, bias = rest_of(x1)
    W = {**W, **rest}
    (x2, h2), mlp0 = _mlp_fwd(x1, h1, W["packed"], 0, "0", next_gain=P["norm_mix"][1])

    proj2 = _matmul(h2, W["swa_w_qkv"], dims="nn", tm=1024, tn=384, tk=1024, outs=[F32], b_cs=True,
                    epilogue=_add, extras=[(bias.reshape(1, 1536), "n")], name="swa_qkv")
    sq, sk, sv = _swa_pre_fwd(proj2, tab, sqg, skg)
    swa = [(SWA_DIST, 1)]
    so, slse, so_b = _band_fwd(sq, sk, sv, patterns=swa, nq=2, name="swa_fwd", sinks=sink_b, want_bf16=True)
    x3, h3 = _matmul(so_b, W["swa_w_out"], dims="nn", **_T, name="swa_out", **_add_norm_in(x2, P["norm_mlp"][1]))
    y, mlp1 = _mlp_fwd(x3, h3, W["packed"], 1, "1")
    loss, dy, dyb = _loss_head(y, target)

    gw, gp = {}, {}
    dx3, dx3b, dg_mlp1, gw["mlp_w_up1"], gw["mlp_w_down1"] = _mlp_bwd(x3, P["norm_mlp"][1], W["packed"], 1, mlp1, dy, dyb, "1")
    dx3b = red.begin("mlp1", {n: (gw[n], 1024) for n in ("mlp_w_up1", "mlp_w_down1")}, dx3b)
    gw["swa_w_out"] = _matmul(so_b, dx3b, dims="tn", **_T, outs=[F32], name="swa_dwout")
    dso, sdelta = _matmul(dx3b, W["swa_w_out"], dims="nt", tm=512, tn=1024, tk=1024, name="swa_do", **_delta_in(so, 0))
    dsq, dsk, dsv, dsink = _band_bwd(sq, sk, sv, slse, sdelta, dso, patterns=swa, nq=2, name="swa_bwd", sinks=sink_b)
    dproj2, gp["swa_b_qkv"], gp["swa_q_gain"], gp["swa_k_gain"] = _swa_pre_bwd(proj2, tab, sqg, skg, dsq, dsk, dsv)
    gp["swa_sinks"] = dsink
    gw["swa_w_qkv"] = _matmul(h2, dproj2, dims="tn", tm=1024, tn=384, tk=1024, outs=[F32], o_cs=N_CHIPS, name="swa_dwqkv")
    dx2, dx2b, dg_mix1 = _matmul(dproj2, W["swa_w_qkv"], dims="nt", tm=512, tn=1024, tk=384, b_cs=True, name="swa_dh",
                                 **_rms_bwd_in(x2, P["norm_mix"][1], dx3))
    dx2b = red.begin("swa", {"swa_w_qkv": (gw["swa_w_qkv"], 1024), "swa_w_out": (gw["swa_w_out"], 256)}, dx2b)
    dx2b = red.advance("mlp1", dx2b, dx2b)

    dx1, dx1b, dg_mlp0, gw["mlp_w_up0"], gw["mlp_w_down0"] = _mlp_bwd(x1, P["norm_mlp"][0], W["packed"], 0, mlp0, dx2, dx2b, "0")
    gw["hyb_w_out"] = _matmul(mixed, dx1b, dims="tn", **_T, outs=[F32], name="hyb_dwout")
    dx1b = red.begin("mlp0", {"mlp_w_up0": (gw["mlp_w_up0"], 1024), "mlp_w_down0": (gw["mlp_w_down0"], 1024),
                              "hyb_w_out": (gw["hyb_w_out"], 256)}, dx1b)
    dx1b = red.advance("swa", dx1b, dx1b)
    red.finish("mlp1", dx1b)
    dmixed, ddelta = _matmul(dx1b, W["hyb_w_out"], dims="nt", tm=512, tn=1024, tk=1024, name="hyb_dmixed", **_delta_in(da, 512))
    dro, drg, gp["ret_gn_gain"] = _even_post_bwd(ro, proj, gn, dmixed)
    drq, drk, drv = _ret_bwd(rq, rk, rv, states, dro)
    ddq, ddk, ddv = _band_bwd(dq, dk, dv, dlse, ddelta, dmixed, patterns=dil, nq=1, name="dil_bwd", do_col0=4)
    ddq = red.advance("mlp0", ddq, ddq)
    red.finish("swa", ddq)
    dproj, gp["dil_q_gain"], gp["dil_k_gain"] = _even_pre_bwd(proj, tab, dqg, dkg, drq, drk, drv, drg, [ddq], [ddk], [ddv])
    gw["hyb_w_in"] = _matmul(h0, dproj, dims="tn", tm=1024, tn=768, tk=1024, outs=[F32], o_cs=N_CHIPS, name="hyb_dwin")
    dproj = red.begin("win", {"hyb_w_in": (gw["hyb_w_in"], 1024)}, dproj)
    grad_x, _, dg_mix0 = _matmul(dproj, W["hyb_w_in"], dims="nt", tm=512, tn=1024, tk=768, b_cs=True, name="hyb_dh",
                                 **_rms_bwd_in(x, P["norm_mix"][0], dx1))
    red.finish("mlp0", grad_x)
    gp["norm_mix"] = jnp.concatenate([dg_mix0, dg_mix1], axis=0)
    gp["norm_mlp"] = jnp.concatenate([dg_mlp0, dg_mlp1], axis=0)
    return loss, grad_x, gp


HBM = pl.BlockSpec(memory_space=pltpu.HBM)


def _place():
    x, y, c = lax.axis_index("x"), lax.axis_index("y"), lax.axis_index("c")
    chips = [(1 - x, y), (x, 1 - y), (1 - x, 1 - y)]
    return x, y, c, chips


def _allgather_shards(buf):
    _, R, Wd = buf.shape
    Rh = R // 2

    def body(b_ref, out_ref, send_sems, recv_sems):
        x, y, c, chips = _place()
        sibling = (x, y, 1 - c)

        def copy(k, chip, core, to):
            block = b_ref.at[2 * chip[0] + chip[1], pl.ds(core * Rh, Rh), :]
            return pltpu.make_async_remote_copy(
                src_ref=block, dst_ref=block, send_sem=send_sems.at[k], recv_sem=recv_sems.at[k],
                device_id=to, device_id_type=MESH)

        first = [copy(k, (x, y), c, (*chip, c)) for k, chip in enumerate(chips)]
        for cp in first:
            cp.start()
        passed = [copy(3 + k, chip, c, sibling) for k, chip in enumerate(chips)]
        for k, chip in enumerate(chips):
            copy(k, chip, c, (x, y, c)).wait_recv()
            passed[k].start()
        for k, chip in enumerate(chips):
            copy(3 + k, chip, 1 - c, (x, y, c)).wait_recv()
        for cp in first + passed:
            cp.wait_send()

    return _pc(
        body, name="allgather_first", in_specs=[HBM], out_specs=HBM,
        out_shape=jax.ShapeDtypeStruct(buf.shape, buf.dtype), input_output_aliases={0: 0},
        scratch_shapes=[pltpu.SemaphoreType.DMA((6,)), pltpu.SemaphoreType.DMA((6,))],
    )(buf)


SEM = pl.BlockSpec(memory_space=pltpu.SEMAPHORE)
EFFECT = pltpu.SideEffectType.DATAFLOW_SIDE_EFFECTING


def _half_block(ref, chip, core):
    rh = ref.shape[1] // 2
    return ref.at[2 * chip[0] + chip[1], pl.ds(core * rh, rh), :]


def _gather_start(buf, ride, name):
    def body(b_ref, ride_ref, s0, s1, s2, r0, r1, r2, b_out, ride_out):
        x, y, c, chips = _place()
        for chip, s, r in zip(chips, (s0, s1, s2), (r0, r1, r2)):
            mine = _half_block(b_ref, (x, y), c)
            pltpu.make_async_remote_copy(src_ref=mine, dst_ref=mine, send_sem=s, recv_sem=r,
                                         device_id=(*chip, c), device_id_type=MESH).start()

    sem = pltpu.SemaphoreType.DMA(())
    return _pc(
        body, name=name,
        out_shape=(sem,) * 6 + (pltpu.HBM(buf.shape, buf.dtype), pltpu.HBM(ride.shape, ride.dtype)),
        in_specs=(HBM, HBM), out_specs=(SEM,) * 6 + (HBM, HBM), input_output_aliases={0: 6, 1: 7},
        compiler_params=pltpu.CompilerParams(has_side_effects=EFFECT),
    )(pltpu.with_memory_space_constraint(buf, pltpu.HBM), pltpu.with_memory_space_constraint(ride, pltpu.HBM))


def _gather_wait(buf, sems, after, name):
    def body(b_ref, s0, s1, s2, r0, r1, r2, after_ref, b_out):
        x, y, c, chips = _place()
        for chip, s, r in zip(chips, (s0, s1, s2), (r0, r1, r2)):
            cp = pltpu.make_async_remote_copy(src_ref=_half_block(b_ref, (x, y), c), dst_ref=_half_block(b_ref, chip, c),
                                              send_sem=s, recv_sem=r, device_id=(*chip, c), device_id_type=MESH)
            cp.wait_send()
            cp.wait_recv()

    return _pc(
        body, name=name, out_shape=pltpu.HBM(buf.shape, buf.dtype),
        in_specs=(HBM,) + (SEM,) * 6 + (pl.BlockSpec(memory_space=pl.ANY),), out_specs=HBM, input_output_aliases={0: 0},
        compiler_params=pltpu.CompilerParams(has_side_effects=EFFECT),
    )(buf, *sems, after)


def _gather_handover(buf, name):
    def body(b_ref, out_ref, send_sems, recv_sems):
        x, y, c, chips = _place()
        cps = []
        for k, chip in enumerate(chips):
            mine = _half_block(b_ref, chip, c)
            cps.append(pltpu.make_async_remote_copy(src_ref=mine, dst_ref=mine, send_sem=send_sems.at[k],
                                                    recv_sem=recv_sems.at[k], device_id=(x, y, 1 - c), device_id_type=MESH))
        for cp in cps:
            cp.start()
        for k, chip in enumerate(chips):
            theirs = _half_block(b_ref, chip, 1 - c)
            pltpu.make_async_remote_copy(src_ref=theirs, dst_ref=theirs, send_sem=send_sems.at[k], recv_sem=recv_sems.at[k],
                                         device_id=(x, y, 1 - c), device_id_type=MESH).wait_recv()
        for cp in cps:
            cp.wait_send()

    return _pc(
        body, name=name, in_specs=[HBM], out_specs=HBM,
        out_shape=jax.ShapeDtypeStruct(buf.shape, buf.dtype), input_output_aliases={0: 0},
        scratch_shapes=[pltpu.SemaphoreType.DMA((3,)), pltpu.SemaphoreType.DMA((3,))],
    )(buf)


def _swap_halves(ts):
    nt = len(ts)

    def body(*refs):
        t_refs, l_refs, send_sems, recv_sems = refs[:nt], refs[nt:2 * nt], refs[-2], refs[-1]
        x, y, c, _ = _place()
        cps = []
        for k in range(nt):
            rh = t_refs[k].shape[1] // 2
            cps.append(pltpu.make_async_remote_copy(
                src_ref=t_refs[k].at[:, pl.ds((1 - c) * rh, rh), :], dst_ref=l_refs[k],
                send_sem=send_sems.at[k], recv_sem=recv_sems.at[k], device_id=(x, y, 1 - c), device_id_type=MESH))
        for cp in cps:
            cp.start()
        for cp in cps:
            cp.wait()

    return _pc(
        body, name="grad_swap_halves", in_specs=[HBM] * nt, out_specs=[HBM] * nt,
        out_shape=[jax.ShapeDtypeStruct((t.shape[0], t.shape[1] // 2, t.shape[2]), F32) for t in ts],
        scratch_shapes=[pltpu.SemaphoreType.DMA((nt,)), pltpu.SemaphoreType.DMA((nt,))],
    )(*ts)


def _pair_sum(t, l, place, name):
    _, r, cols = t.shape
    rh = r // 2
    tr = min(rh, 256)
    nr = rh // tr

    def body(pl_ref, t_ref, l_ref, o_ref):
        o_ref[...] = (t_ref[...] + l_ref[...]).astype(BF16)

    other = lambda s, p: s + jnp.where(s >= p[0], 1, 0)
    return _pc(
        body, name=name,
        grid_spec=pltpu.PrefetchScalarGridSpec(
            num_scalar_prefetch=1, grid=(N_CHIPS - 1, nr),
            in_specs=[pl.BlockSpec((None, tr, cols), lambda s, i, p: (other(s, p), p[1] * nr + i, 0)),
                      pl.BlockSpec((None, tr, cols), lambda s, i, p: (other(s, p), i, 0))],
            out_specs=pl.BlockSpec((None, tr, cols), lambda s, i, p: (other(s, p), i, 0))),
        out_shape=jax.ShapeDtypeStruct((N_CHIPS, rh, cols), BF16),
        compiler_params=_params(("parallel", "parallel")),
    )(place, t, l)


def _exchange_chips(ps):
    nt = len(ps)

    def body(*refs):
        p_refs, r_refs, send_sems, recv_sems = refs[:nt], refs[nt:2 * nt], refs[-2], refs[-1]
        x, y, c, chips = _place()
        cps = []
        for t in range(nt):
            for k, chip in enumerate(chips):
                cps.append(pltpu.make_async_remote_copy(
                    src_ref=p_refs[t].at[2 * chip[0] + chip[1]], dst_ref=r_refs[t].at[k],
                    send_sem=send_sems.at[3 * t + k], recv_sem=recv_sems.at[3 * t + k],
                    device_id=(*chip, c), device_id_type=MESH))
        for cp in cps:
            cp.start()
        for cp in cps:
            cp.wait()

    return _pc(
        body, name="grad_exchange_chips", in_specs=[HBM] * nt, out_specs=[HBM] * nt,
        out_shape=[jax.ShapeDtypeStruct((3,) + p.shape[1:], BF16) for p in ps],
        scratch_shapes=[pltpu.SemaphoreType.DMA((3 * nt,)), pltpu.SemaphoreType.DMA((3 * nt,))],
    )(*ps)


def _final_sum(t, l, rcv, place, name, layer=0, layers=1, into=None):
    _, r, cols = t.shape
    rh = r // 2
    tr = min(rh, 256)
    nr = rh // tr

    def body(pl_ref, t_ref, l_ref, r_ref, *rest):
        acc = t_ref[...] + l_ref[...]
        for k in range(3):
            acc = acc + r_ref[k].astype(F32)
        rest[-1][...] = acc

    in_specs = [pl.BlockSpec((None, tr, cols), lambda i, p: (p[0], p[1] * nr + i, 0)),
                pl.BlockSpec((None, tr, cols), lambda i, p: (p[0], i, 0)),
                pl.BlockSpec((3, tr, cols), lambda i, p: (0, i, 0))]
    args = [place, t, l, rcv]
    aliases = {}
    if into is not None:
        in_specs.append(pl.BlockSpec(memory_space=pl.ANY))
        args.append(into)
        aliases = {4: 0}
    return _pc(
        body, name=name,
        grid_spec=pltpu.PrefetchScalarGridSpec(
            num_scalar_prefetch=1, grid=(nr,), in_specs=in_specs,
            out_specs=pl.BlockSpec((tr, cols), lambda i, p: (2 * nr * layer + p[1] * nr + i, 0))),
        out_shape=jax.ShapeDtypeStruct((layers * r, cols), F32), input_output_aliases=aliases,
        compiler_params=_params(("parallel",)),
    )(*args)


def _share_halves(hs, name):
    nt = len(hs)
    n = sum(layers for _, layers in hs)

    def body(*refs):
        h_refs, send_sems, recv_sems = refs[:nt], refs[-2], refs[-1]
        x, y, c, _ = _place()
        cps = []
        for k, (_, layers) in enumerate(hs):
            rh = h_refs[k].shape[0] // (2 * layers)
            for layer in range(layers):
                half = h_refs[k].at[pl.ds((2 * layer + c) * rh, rh), :]
                cps.append(pltpu.make_async_remote_copy(
                    src_ref=half, dst_ref=half, send_sem=send_sems.at[len(cps)], recv_sem=recv_sems.at[len(cps)],
                    device_id=(x, y, 1 - c), device_id_type=MESH))
        for cp in cps:
            cp.start()
        for cp in cps:
            cp.wait()

    return _pc(
        body, name=name, in_specs=[HBM] * nt, out_specs=[HBM] * nt,
        out_shape=[jax.ShapeDtypeStruct(h.shape, F32) for h, _ in hs],
        input_output_aliases={k: k for k in range(nt)},
        scratch_shapes=[pltpu.SemaphoreType.DMA((n,)), pltpu.SemaphoreType.DMA((n,))],
    )(*[h for h, _ in hs])


def _split_start(name, bufs, ride, n, copies_of):
    nb = len(bufs)

    def body(*refs):
        sems = refs[nb + 1:nb + 1 + 2 * n]
        for cp in copies_of(refs[:nb], sems[:n], sems[n:]):
            (cp[0] if isinstance(cp, tuple) else cp).start()

    outs = _pc(
        body, name=name,
        out_shape=(pltpu.SemaphoreType.DMA(()),) * (2 * n) + tuple(pltpu.HBM(b.shape, b.dtype) for b in bufs)
        + (pltpu.HBM(ride.shape, ride.dtype),),
        in_specs=(HBM,) * (nb + 1), out_specs=(SEM,) * (2 * n) + (HBM,) * (nb + 1),
        input_output_aliases={k: 2 * n + k for k in range(nb + 1)},
        compiler_params=pltpu.CompilerParams(has_side_effects=EFFECT),
    )(*[pltpu.with_memory_space_constraint(b, pltpu.HBM) for b in bufs], pltpu.with_memory_space_constraint(ride, pltpu.HBM))
    return list(outs[:2 * n]), list(outs[2 * n:2 * n + nb]), outs[-1]


def _split_wait(name, bufs, sems, after, n, copies_of):
    nb = len(bufs)

    def body(*refs):
        s = refs[nb:nb + 2 * n]
        for cp in copies_of(refs[:nb], s[:n], s[n:]):
            sent, landed = cp if isinstance(cp, tuple) else (cp, cp)
            sent.wait_send()
            landed.wait_recv()

    outs = _pc(
        body, name=name, out_shape=tuple(pltpu.HBM(b.shape, b.dtype) for b in bufs),
        in_specs=(HBM,) * nb + (SEM,) * (2 * n) + (pl.BlockSpec(memory_space=pl.ANY),), out_specs=(HBM,) * nb,
        input_output_aliases={k: k for k in range(nb)},
        compiler_params=pltpu.CompilerParams(has_side_effects=EFFECT),
    )(*bufs, *sems, after)
    return list(outs)


def _handover_copies(refs, send, recv):
    x, y, c, chips = _place()
    cps = []
    for k, chip in enumerate(chips):
        mine, theirs = _half_block(refs[0], chip, c), _half_block(refs[0], chip, 1 - c)
        desc = lambda blk: pltpu.make_async_remote_copy(src_ref=blk, dst_ref=blk, send_sem=send[k], recv_sem=recv[k],
                                                        device_id=(x, y, 1 - c), device_id_type=MESH)
        cps.append((desc(mine), desc(theirs)))
    return cps


def _swap_copies(nt):
    def copies_of(refs, send, recv):
        x, y, c, _ = _place()
        cps = []
        for k in range(nt):
            rh = refs[k].shape[1] // 2
            cps.append(pltpu.make_async_remote_copy(
                src_ref=refs[k].at[:, pl.ds((1 - c) * rh, rh), :], dst_ref=refs[nt + k],
                send_sem=send[k], recv_sem=recv[k], device_id=(x, y, 1 - c), device_id_type=MESH))
        return cps
    return copies_of


def _exchange_copies(nt):
    def copies_of(refs, send, recv):
        x, y, c, chips = _place()
        cps = []
        for t in range(nt):
            for k, chip in enumerate(chips):
                cps.append(pltpu.make_async_remote_copy(
                    src_ref=refs[t].at[2 * chip[0] + chip[1]], dst_ref=refs[nt + t].at[k],
                    send_sem=send[3 * t + k], recv_sem=recv[3 * t + k], device_id=(*chip, c), device_id_type=MESH))
        return cps
    return copies_of


class _StagedReduce:
    def __init__(self, place):
        self.place = place
        self.groups = {}
        self.halves = {}

    @staticmethod
    def slab(t, r):
        return t.reshape(N_CHIPS, r, t.size // (N_CHIPS * r))

    def begin(self, g, grads, ride):
        names = list(grads)
        ts = [self.slab(t, r) for t, r in grads.values()]
        lands = [lax.empty((N_CHIPS, t.shape[1] // 2, t.shape[2]), F32) for t in ts]
        sems, bufs, ride = _split_start(f"grad_swap_start_{g}", ts + lands, ride, len(ts), _swap_copies(len(ts)))
        self.groups[g] = dict(names=names, bufs=bufs, sems=sems)
        return ride

    def advance(self, g, after, ride):
        st = self.groups[g]
        nt = len(st["names"])
        bufs = _split_wait(f"grad_swap_wait_{g}", st["bufs"], st["sems"], after, nt, _swap_copies(nt))
        st["ts"], st["ls"] = bufs[:nt], bufs[nt:]
        ps = [_pair_sum(t, l, self.place, f"pair_sum_{n}") for t, l, n in zip(st["ts"], st["ls"], st["names"])]
        lands = [lax.empty((3,) + p.shape[1:], BF16) for p in ps]
        st["sems"], st["bufs"], ride = _split_start(f"grad_exchange_start_{g}", ps + lands, ride, 3 * nt, _exchange_copies(nt))
        return ride

    def finish(self, g, after):
        st = self.groups[g]
        nt = len(st["names"])
        bufs = _split_wait(f"grad_exchange_wait_{g}", st["bufs"], st["sems"], after, 3 * nt, _exchange_copies(nt))
        for t, l, r, n in zip(st["ts"], st["ls"], bufs[nt:], st["names"]):
            if n[-1] in "01":
                self.halves[n[:-1]] = _final_sum(t, l, r, self.place, f"final_sum_{n}", layer=int(n[-1]), layers=2,
                                                 into=self.halves.get(n[:-1]))
            else:
                self.halves[n] = _final_sum(t, l, r, self.place, f"final_sum_{n}")


def _allgather_small(v):
    rows = v.shape[0]

    def body(v_ref, out_ref, send_sems, recv_sems):
        x, y, c, _ = _place()
        me = 4 * x + 2 * y + c
        out_ref[me] = v_ref[...]
        cps = []
        for k in range(1, 8):
            fx, fy, fc = (k >> 2) & 1, (k >> 1) & 1, k & 1
            to = (1 - x if fx else x, 1 - y if fy else y, 1 - c if fc else c)
            cps.append(pltpu.make_async_remote_copy(
                src_ref=v_ref, dst_ref=out_ref.at[me], send_sem=send_sems.at[k - 1], recv_sem=recv_sems.at[k - 1],
                device_id=to, device_id_type=MESH))
        for cp in cps:
            cp.start()
        for cp in cps:
            cp.wait()

    return _pc(
        body, name="allgather_small",
        in_specs=[pl.BlockSpec(memory_space=pltpu.VMEM)], out_specs=pl.BlockSpec(memory_space=pltpu.VMEM),
        out_shape=jax.ShapeDtypeStruct((8, rows, LANES), F32),
        scratch_shapes=[pltpu.SemaphoreType.DMA((7,)), pltpu.SemaphoreType.DMA((7,))],
    )(v)


def _adamw_math(w, g, m, v):
    m = ADAM_B1 * m + (1.0 - ADAM_B1) * g
    v = ADAM_B2 * v + (1.0 - ADAM_B2) * (g * g)
    m_hat = m / (1.0 - ADAM_B1 ** ADAM_STEP)
    v_hat = v / (1.0 - ADAM_B2 ** ADAM_STEP)
    return -ADAM_LR * (m_hat / (jnp.sqrt(v_hat) + ADAM_EPS) + ADAM_WD * w), m, v


def _adamw(w, g, m, v, name):
    r, cols = w.shape
    tr = min(r, 256)

    def body(w_ref, g_ref, m_ref, v_ref, d_ref, mo_ref, vo_ref):
        d, mn, vn = _adamw_math(w_ref[...], g_ref[...], m_ref[...], v_ref[...])
        d_ref[...] = d
        mo_ref[...] = mn
        vo_ref[...] = vn

    row = pl.BlockSpec((tr, cols), lambda i: (i, 0))
    return _pc(
        body, name=name, grid=(r // tr,), in_specs=[row] * 4, out_specs=[row] * 3,
        out_shape=[jax.ShapeDtypeStruct((r, cols), F32)] * 3,
        compiler_params=_params(("parallel",)),
    )(w, g, m, v)


def _adamw_small(w, gathered, m, v):
    rows = w.shape[0]

    def body(w_ref, g_ref, m_ref, v_ref, go_ref, d_ref, mo_ref, vo_ref):
        g = g_ref[0]
        for k in range(1, 8):
            g = g + g_ref[k]
        d, mn, vn = _adamw_math(w_ref[...], g, m_ref[...], v_ref[...])
        go_ref[...] = g
        d_ref[...] = d
        mo_ref[...] = mn
        vo_ref[...] = vn

    return _pc(
        body, name="adamw_small",
        out_shape=[jax.ShapeDtypeStruct((rows, LANES), F32)] * 4,
    )(w, gathered, m, v)


_BIAS_ROWS = 32


def _own_slot(flat, chip):
    return lax.dynamic_update_slice(lax.empty((N_CHIPS,) + flat.shape, flat.dtype), flat[None], (chip, 0, 0))


def _pack_first(hyb_w_in, hyb_w_out):
    return jnp.concatenate([t.astype(BF16).reshape(-1, 1024) for t in (hyb_w_in, hyb_w_out)], axis=0)


def _unpack_first(g):
    return {"hyb_w_in": g[:, 0:768, :].reshape(N_CHIPS, 1024, 768), "hyb_w_out": g[:, 768:1024, :].reshape(1024, 1024)}


def _pack_rest(mlp_w_up, mlp_w_down, swa_w_qkv, swa_w_out, swa_b_qkv):
    parts = [t.astype(BF16).reshape(-1, 1024) for t in (mlp_w_up, mlp_w_down, swa_w_qkv, swa_w_out)]
    bias = lax.bitcast_convert_type(swa_b_qkv.reshape(384), BF16).reshape(1, 768)
    bias = jnp.pad(bias, ((0, _BIAS_ROWS - 1), (0, 256)))
    return jnp.concatenate(parts + [bias], axis=0)


def _unpack_rest(g):
    W = {
        "packed": g,
        "swa_w_qkv": g[:, 4096:4480, :].reshape(N_CHIPS, 1024, 384),
        "swa_w_out": g[:, 4480:4736, :].reshape(1024, 1024),
    }
    bias = lax.bitcast_convert_type(g[:, 4736, :768].reshape(N_CHIPS, 384, 2), F32).reshape(1536)
    return W, bias


_SMALL = (("norm_mix", 16), ("norm_mlp", 16), ("ret_gn_gain", 4), ("dil_q_gain", 1), ("dil_k_gain", 1),
          ("swa_b_qkv", 12), ("swa_q_gain", 1), ("swa_k_gain", 1), ("swa_sinks", 1), ("loss", 1))
_SUBLANES = 8


def _slot(r):
    return -(-r // _SUBLANES) * _SUBLANES


def _pack_small(d):
    return jnp.concatenate([jnp.pad(d[n].reshape(r, LANES), ((0, _slot(r) - r), (0, 0))) for n, r in _SMALL], axis=0)


def _unpack_small(p):
    out, o = {}, 0
    for n, r in _SMALL:
        out[n] = p[o:o + r]
        o += _slot(r)
    return out


def kernel(x, positions, norm_mix, norm_mlp, mlp_w_up, mlp_w_down, hyb_w_in, hyb_w_out, ret_gn_gain, dil_q_gain, dil_k_gain, swa_w_qkv, swa_b_qkv, swa_w_out, swa_q_gain, swa_k_gain, swa_sinks, loss_target, m_norm_mix, m_norm_mlp, m_mlp_w_up, m_mlp_w_down, m_hyb_w_in, m_hyb_w_out, m_ret_gn_gain, m_dil_q_gain, m_dil_k_gain, m_swa_w_qkv, m_swa_b_qkv, m_swa_w_out, m_swa_q_gain, m_swa_k_gain, m_swa_sinks, v_norm_mix, v_norm_mlp, v_mlp_w_up, v_mlp_w_down, v_hyb_w_in, v_hyb_w_out, v_ret_gn_gain, v_dil_q_gain, v_dil_k_gain, v_swa_w_qkv, v_swa_b_qkv, v_swa_w_out, v_swa_q_gain, v_swa_k_gain, v_swa_sinks):
    ax, ay, ac = lax.axis_index("x"), lax.axis_index("y"), lax.axis_index("c")
    chip = 2 * ax + ay
    place = jnp.stack([chip, ac]).astype(jnp.int32)
    S = x.shape[1]

    first = _own_slot(_pack_first(hyb_w_in[0], hyb_w_out[0]), chip)
    rest = _own_slot(_pack_rest(mlp_w_up, mlp_w_down, swa_w_qkv[0], swa_w_out[0], swa_b_qkv[0]), chip)
    *sems, first, pos_col = _gather_start(first, positions.reshape(S, 1), "allgather_first_start")
    flight = {}

    def first_of(after):
        g = _gather_handover(_gather_wait(first, sems, after, "allgather_first_wait"), "allgather_first_handover")
        *flight["sems"], flight["buf"], g = _gather_start(rest, g, "allgather_rest_start")
        return _unpack_first(g)

    def rest_begin(ride):
        buf = _gather_wait(flight["buf"], flight["sems"], ride, "allgather_rest_wait")
        flight["sems"], flight["bufs"], ride = _split_start("allgather_rest_handover_start", [buf], ride, 3, _handover_copies)
        return ride

    def rest_of(after):
        return _unpack_rest(_split_wait("allgather_rest_handover_wait", flight["bufs"], flight["sems"], after, 3,
                                        _handover_copies)[0])

    P = dict(norm_mix=norm_mix, norm_mlp=norm_mlp, ret_gn_gain=ret_gn_gain, dil_q_gain=dil_q_gain, dil_k_gain=dil_k_gain,
             swa_q_gain=swa_q_gain, swa_k_gain=swa_k_gain, swa_sinks=swa_sinks)

    red = _StagedReduce(place)
    loss_l, grad_x, gp = _local_step(x[0], pos_col, loss_target[0], first_of, rest_begin, rest_of, P, red)

    params = dict(mlp_w_up=(mlp_w_up, m_mlp_w_up, v_mlp_w_up), mlp_w_down=(mlp_w_down, m_mlp_w_down, v_mlp_w_down),
                  hyb_w_in=(hyb_w_in, m_hyb_w_in, v_hyb_w_in), hyb_w_out=(hyb_w_out, m_hyb_w_out, v_hyb_w_out),
                  swa_w_qkv=(swa_w_qkv, m_swa_w_qkv, v_swa_w_qkv), swa_w_out=(swa_w_out, m_swa_w_out, v_swa_w_out))
    big = {}

    def update(names, share_name):
        hs = [(red.halves[n], params[n][0].shape[0]) for n in names]
        for n, g in zip(names, _share_halves(hs, share_name)):
            rows = g.shape[0]
            w, m, v = (t.reshape(rows, -1) for t in params[n])
            big[n] = [t.reshape(params[n][0].shape) for t in (g,) + tuple(_adamw(w, g, m, v, f"adamw_{n}"))]

    names = ["mlp_w_up", "mlp_w_down", "hyb_w_out", "swa_w_qkv", "swa_w_out"]
    red.halves[names[0]] = red.advance("win", grad_x, red.halves[names[0]])
    update(names, "grad_share_halves")
    red.finish("win", big[names[-1]][1])
    update(["hyb_w_in"], "grad_share_last")

    gsm = dict(gp, loss=loss_l)
    gsm["swa_sinks"] = jnp.pad(gp["swa_sinks"].reshape(16, HEAD)[:, 0], (0, LANES - 16))
    gathered = _allgather_small(_pack_small(gsm))

    def small_pack(norm_mix, norm_mlp, gn, dq, dk, b, sq, sk, sinks):
        dup = lambda t: jnp.tile(t.reshape(1, HEAD), (1, 2))
        bias = lax.dynamic_update_slice(jnp.zeros((12, LANES), F32), b.reshape(3, LANES), (3 * chip, 0))
        return _pack_small(dict(norm_mix=norm_mix, norm_mlp=norm_mlp, ret_gn_gain=gn, dil_q_gain=dup(dq), dil_k_gain=dup(dk),
                                swa_b_qkv=bias, swa_q_gain=dup(sq), swa_k_gain=dup(sk),
                                swa_sinks=jnp.pad(sinks.reshape(16), (0, LANES - 16)), loss=jnp.zeros((1, LANES), F32)))

    pw = small_pack(norm_mix, norm_mlp, ret_gn_gain, dil_q_gain, dil_k_gain, swa_b_qkv, swa_q_gain, swa_k_gain, swa_sinks)
    pm = small_pack(m_norm_mix, m_norm_mlp, m_ret_gn_gain, m_dil_q_gain, m_dil_k_gain, m_swa_b_qkv, m_swa_q_gain, m_swa_k_gain, m_swa_sinks)
    pv = small_pack(v_norm_mix, v_norm_mlp, v_ret_gn_gain, v_dil_q_gain, v_dil_k_gain, v_swa_b_qkv, v_swa_q_gain, v_swa_k_gain, v_swa_sinks)
    small = [_unpack_small(t) for t in _adamw_small(pw, gathered, pm, pv)]

    def small_out(n, k):
        t = small[k][n]
        if n in ("norm_mix", "norm_mlp"):
            return t.reshape(2, D_MODEL)
        if n == "ret_gn_gain":
            return t.reshape(1, RET_HEADS, 128)
        if n == "swa_b_qkv":
            return lax.dynamic_slice(t, (3 * chip, 0), (3, LANES)).reshape(1, 384)
        if n == "swa_sinks":
            return t[0, :16].reshape(1, 16)
        return t[0, :HEAD].reshape(1, HEAD)

    order = ["norm_mix", "norm_mlp", "mlp_w_up", "mlp_w_down", "hyb_w_in", "hyb_w_out", "ret_gn_gain", "dil_q_gain",
             "dil_k_gain", "swa_w_qkv", "swa_b_qkv", "swa_w_out", "swa_q_gain", "swa_k_gain", "swa_sinks"]
    is_big = {"mlp_w_up", "mlp_w_down", "hyb_w_in", "hyb_w_out", "swa_w_qkv", "swa_w_out"}
    outs = [small[0]["loss"][0, 0], grad_x[None]]
    for k in range(4):
        outs += [big[n][k] if n in is_big else small_out(n, k) for n in order]
    return tuple(outs)
```

```python
import functools
import math

import numpy as np
import jax
import jax.numpy as jnp
from jax import lax
from jax.experimental import pallas as pl
from jax.experimental.pallas import tpu as pltpu

F32, BF16 = jnp.float32, jnp.bfloat16
HIGHEST = lax.Precision.HIGHEST
MESH = pl.DeviceIdType.MESH

LANES = 128
VMEM_LIMIT = 48 << 20
D_MODEL = 1024
D_FF = 4096
HEAD = 64
EPS = 1e-6
BLK = 128
RET_HEADS = 4
RET_THETA = 10000.0
ROPE_THETA = 500000.0
ROPE_DIMS = 16
DIL_PATTERNS = ((128, 1), (512, 4), (2048, 16))
SWA_DIST = 127
N_CHIPS = 4
ADAM_LR, ADAM_B1, ADAM_B2, ADAM_EPS, ADAM_WD, ADAM_STEP = 0.001, 0.9, 0.999, 1e-08, 0.01, 10

_LOG_GAMMA = [float(np.log1p(-np.exp2(np.float32(-5.0 - h)))) for h in range(RET_HEADS)]


def _pc(body, **kw):
    return pl.pallas_call(body, **kw)


def _params(sem):
    return pltpu.CompilerParams(dimension_semantics=sem, vmem_limit_bytes=VMEM_LIMIT)


def _matmul(a, b, *, dims, tm, tn, tk, outs, name, epilogue=None, extras=(), b_cs=False, b_rs=0, b_row0=0, b_rows=0,
            o_cs=0, a_pro=None):
    if dims == "nn":
        M, K = a.shape
        N = b.shape[0] * b.shape[2] if b_cs else b.shape[1]
        a_spec = pl.BlockSpec((tm, tk), lambda i, j, k: (i, k))
        if b_cs:
            npt = b.shape[2] // tn
            b_spec = pl.BlockSpec((None, tk, tn), lambda i, j, k: (j // npt, k + b_row0, j % npt))
        elif b_rs:
            K, N, kps = b.shape[0] * b_rs, b.shape[2], b_rs // tk
            b_spec = pl.BlockSpec((None, tk, tn), lambda i, j, k: (k // kps, b_row0 + k % kps, j))
        else:
            b_spec = pl.BlockSpec((tk, tn), lambda i, j, k: (k, j))
        contract = (((1,), (0,)), ((), ()))
    elif dims == "nt":
        M, K = a.shape
        N = (b_rows or b.shape[1]) if b_cs else b.shape[0]
        a_spec = pl.BlockSpec((tm, tk), lambda i, j, k: (i, k))
        if b_cs:
            kpt = b.shape[2] // tk
            b_spec = pl.BlockSpec((None, tn, tk), lambda i, j, k: (k // kpt, j + b_row0, k % kpt))
        elif b_rs:
            N, jps = b.shape[0] * b_rs, b_rs // tn
            b_spec = pl.BlockSpec((None, tn, tk), lambda i, j, k: (j // jps, b_row0 + j % jps, k))
        else:
            b_spec = pl.BlockSpec((tn, tk), lambda i, j, k: (j, k))
        contract = (((1,), (1,)), ((), ()))
    else:
        K, M = a.shape
        N = b.shape[1]
        a_spec = pl.BlockSpec((tk, tm), lambda i, j, k: (k, i))
        b_spec = pl.BlockSpec((tk, tn), lambda i, j, k: (k, j))
        contract = (((0,), (0,)), ((), ()))
    assert M % tm == 0 and N % tn == 0 and K % tk == 0, (name, M, N, K, tm, tn, tk)
    nk = K // tk
    ex_specs = []
    for arr, kind in extras:
        if kind == "mn":
            ex_specs.append(pl.BlockSpec((tm, tn), lambda i, j, k: (i, j)))
        elif kind == "n":
            ex_specs.append(pl.BlockSpec((1, tn), lambda i, j, k: (0, j)))
        elif kind == "full":
            ex_specs.append(pl.BlockSpec(arr.shape, lambda i, j, k, nd=arr.ndim: (0,) * nd))
        else:
            ex_specs.append(pl.BlockSpec((tm, kind), lambda i, j, k: (i, 0)))
    if o_cs:
        n_sh = N // o_cs
        opt = n_sh // tn
        o_shape = (o_cs, M, n_sh)
        o_spec = pl.BlockSpec((None, tm, tn), lambda i, j, k: (j // opt, i, j % opt))
    else:
        o_shape = (M, N)
        o_spec = pl.BlockSpec((tm, tn), lambda i, j, k: (i, j))
    o_specs, o_shapes, summed = [], [], []
    for o in outs:
        if isinstance(o, tuple) and o[0] == "colsum":
            assert N == tn
            o_specs.append(pl.BlockSpec((1, tn), lambda i, j, k: (0, j)))
            o_shapes.append(jax.ShapeDtypeStruct((1, N), F32))
            summed.append(True)
        elif isinstance(o, tuple):
            o_specs.append(pl.BlockSpec((tm, o[1]), lambda i, j, k: (i, 0)))
            o_shapes.append(jax.ShapeDtypeStruct((M, o[1]), o[0]))
            summed.append(False)
        else:
            o_specs.append(o_spec)
            o_shapes.append(jax.ShapeDtypeStruct(o_shape, o))
            summed.append(False)
    n_ex, n_out = len(extras), len(outs)
    if epilogue is None:
        epilogue = lambda acc: (acc,)

    def body(a_ref, b_ref, *rest):
        ex, o_refs, acc = rest[:n_ex], rest[n_ex:n_ex + n_out], rest[-1]
        i, k = pl.program_id(0), pl.program_id(2)

        @pl.when(k == 0)
        def _():
            acc[...] = jnp.zeros_like(acc)

        av = a_ref[...] if a_pro is None else a_pro(a_ref[...])
        acc[...] += lax.dot_general(av.astype(BF16), b_ref[...].astype(BF16), contract, preferred_element_type=F32)

        @pl.when(k == nk - 1)
        def _():
            vals = epilogue(acc[...], *[e[...] for e in ex])
            for r, v, sm in zip(o_refs, vals, summed):
                if sm:
                    @pl.when(i == 0)
                    def _(r=r):
                        r[...] = jnp.zeros_like(r)

                    r[...] += v
                else:
                    r[...] = v.astype(r.dtype)

    res = _pc(
        body, name=name, grid=(M // tm, N // tn, nk),
        in_specs=[a_spec, b_spec] + ex_specs, out_specs=o_specs, out_shape=o_shapes,
        scratch_shapes=[pltpu.VMEM((tm, tn), F32)],
        compiler_params=_params(("arbitrary" if any(summed) else "parallel", "parallel", "arbitrary")),
    )(a, b, *[e for e, _ in extras])
    return res[0] if n_out == 1 else res


def _roll(x, s):
    return pltpu.roll(x, s % LANES, 1)


def _rope(x, A, B, C, half):
    return x * A + _roll(x, LANES - half) * B + _roll(x, half) * C


def _rope_t(g, A, B, C, half):
    return g * A + _roll(g * B, half) + _roll(g * C, LANES - half)


def _gmean(x, G):
    hi = x.astype(BF16)
    lo = (x - hi.astype(F32)).astype(BF16)
    Gb = G.astype(BF16)
    return jnp.dot(hi, Gb, preferred_element_type=F32) + jnp.dot(lo, Gb, preferred_element_type=F32)


def _head_mask(shape, half):
    lane = lax.broadcasted_iota(jnp.int32, shape, len(shape) - 1)
    return (lane >= HEAD) if half else (lane < HEAD)


def _group_matrix():
    i = np.arange(LANES)
    return jnp.asarray((i[:, None] // HEAD == i[None, :] // HEAD).astype(np.float32) / HEAD)


def _rope_inv():
    l = np.arange(LANES) % HEAD
    inv_r = np.power(np.float32(RET_THETA), -(l % 32).astype(np.float32) * np.float32(2.0 / HEAD))
    hp = ROPE_DIMS // 2
    inv_p = np.power(np.float32(ROPE_THETA), -(l % hp).astype(np.float32) * np.float32(2.0 / ROPE_DIMS))
    inv_p = np.where(l < ROPE_DIMS, inv_p, 0.0)
    return jnp.asarray(np.stack([inv_r, inv_p]).astype(np.float32))


def _tables(pos_col):
    S = pos_col.shape[0]
    tm = 512
    hp = ROPE_DIMS // 2

    def body(p_ref, inv_ref, o_ref):
        p = p_ref[...].astype(F32)
        lane = lax.broadcasted_iota(jnp.int32, (tm, LANES), 1) % HEAD
        ang = p * inv_ref[0:1, :]
        c, s = jnp.cos(ang), jnp.sin(ang)
        o_ref[:, 0:128] = c
        o_ref[:, 128:256] = jnp.where(lane < 32, -s, 0.0)
        o_ref[:, 256:384] = jnp.where(lane >= 32, s, 0.0)
        ang = p * inv_ref[1:2, :]
        c, s = jnp.cos(ang), jnp.sin(ang)
        o_ref[:, 384:512] = c
        o_ref[:, 512:640] = jnp.where(lane < hp, -s, 0.0)
        o_ref[:, 640:768] = jnp.where((lane >= hp) & (lane < ROPE_DIMS), s, 0.0)

    return _pc(
        body, name="rope_tables", grid=(S // tm,),
        in_specs=[pl.BlockSpec((tm, 1), lambda i: (i, 0)), pl.BlockSpec((2, LANES), lambda i: (0, 0))],
        out_specs=pl.BlockSpec((tm, 768), lambda i: (i, 0)),
        out_shape=jax.ShapeDtypeStruct((S, 768), F32),
        compiler_params=_params(("parallel",)),
    )(pos_col, _rope_inv())


def _tab(tab_ref, which):
    o = 384 * which
    return tab_ref[:, o:o + 128], tab_ref[:, o + 128:o + 256], tab_ref[:, o + 256:o + 384]


def _rms_fwd(x, g, name):
    S, Dm = x.shape
    tm = 512

    def body(x_ref, g_ref, h_ref):
        xv = x_ref[...]
        r = lax.rsqrt(jnp.mean(xv * xv, axis=-1, keepdims=True) + EPS)
        h_ref[...] = (xv * r * g_ref[...]).astype(BF16)

    return _pc(
        body, name=name, grid=(S // tm,),
        in_specs=[pl.BlockSpec((tm, Dm), lambda i: (i, 0)), pl.BlockSpec((1, Dm), lambda i: (0, 0))],
        out_specs=pl.BlockSpec((tm, Dm), lambda i: (i, 0)),
        out_shape=jax.ShapeDtypeStruct((S, Dm), BF16),
        compiler_params=_params(("parallel",)),
    )(x, g.reshape(1, Dm))


def _rms_bwd(x, g, dh, dres, name):
    S, Dm = x.shape
    tm = 512

    def body(x_ref, g_ref, dh_ref, dres_ref, dx_ref, dxb_ref, dg_ref):
        xv, dhv = x_ref[...], dh_ref[...]
        r = lax.rsqrt(jnp.mean(xv * xv, axis=-1, keepdims=True) + EPS)
        t = dhv * g_ref[...]
        dx = dres_ref[...] + r * t - xv * (r * r * r) * jnp.mean(xv * t, axis=-1, keepdims=True)
        dx_ref[...] = dx
        dxb_ref[...] = dx.astype(BF16)

        @pl.when(pl.program_id(0) == 0)
        def _():
            dg_ref[...] = jnp.zeros_like(dg_ref)

        dg_ref[...] += jnp.sum(dhv * xv * r, axis=0, keepdims=True)

    row = pl.BlockSpec((tm, Dm), lambda i: (i, 0))
    vec = pl.BlockSpec((1, Dm), lambda i: (0, 0))
    return _pc(
        body, name=name, grid=(S // tm,),
        in_specs=[row, vec, row, row], out_specs=[row, row, vec],
        out_shape=[jax.ShapeDtypeStruct((S, Dm), F32), jax.ShapeDtypeStruct((S, Dm), BF16),
                   jax.ShapeDtypeStruct((1, Dm), F32)],
        compiler_params=_params(("arbitrary",)),
    )(x, g.reshape(1, Dm), dh, dres)


def _hn_fwd(x, gain, G):
    r = lax.rsqrt(_gmean(x * x, G) + EPS)
    return x * r * gain


def _hn_bwd(x, gain, dy, G):
    r = lax.rsqrt(_gmean(x * x, G) + EPS)
    t = dy * gain
    dx = r * t - x * (r * r * r) * _gmean(x * t, G)
    return dx, jnp.sum(dy * x * r, axis=0, keepdims=True)


def _fold_halves(v):
    return v + _roll(v, HEAD)


def _even_pre_fwd(proj, tab, qg, kg):
    S = proj.shape[0]
    tm = 256

    def body(p_ref, tab_ref, qg_ref, kg_ref, g_ref, rq_ref, rk_ref, rv_ref, dq_ref, dk_ref, dv_ref):
        Ar, Br, Cr = _tab(tab_ref, 0)
        Ap, Bp, Cp = _tab(tab_ref, 1)
        G = g_ref[...]
        for c in range(2):
            sl = slice(c * 128, (c + 1) * 128)
            rq_ref[:, sl] = _rope(p_ref[:, c * 128:(c + 1) * 128], Ar, Br, Cr, 32).astype(BF16)
            rk_ref[:, sl] = (_rope(p_ref[:, 256 + c * 128:256 + (c + 1) * 128], Ar, Br, Cr, 32) * 0.125).astype(BF16)
        rv_ref[...] = p_ref[:, 512:1024].astype(BF16)
        for c in range(4):
            sl = slice(c * 128, (c + 1) * 128)
            q = _hn_fwd(p_ref[:, 1536 + c * 128:1536 + (c + 1) * 128], qg_ref[...], G)
            dq_ref[:, sl] = _rope(q, Ap, Bp, Cp, 8).astype(BF16)
            k = _hn_fwd(p_ref[:, 2048 + c * 128:2048 + (c + 1) * 128], kg_ref[...], G)
            dk_ref[:, sl] = _rope(k, Ap, Bp, Cp, 8).astype(BF16)
        dv_ref[...] = p_ref[:, 2560:3072].astype(BF16)

    row = lambda w: pl.BlockSpec((tm, w), lambda i: (i, 0))
    vec = pl.BlockSpec((1, LANES), lambda i: (0, 0))
    return _pc(
        body, name="even_pre_fwd", grid=(S // tm,),
        in_specs=[row(3072), row(768), vec, vec, pl.BlockSpec((LANES, LANES), lambda i: (0, 0))],
        out_specs=[row(256), row(256), row(512), row(512), row(512), row(512)],
        out_shape=[jax.ShapeDtypeStruct((S, w), BF16) for w in (256, 256, 512, 512, 512, 512)],
        compiler_params=_params(("parallel",)),
    )(proj, tab, qg, kg, _group_matrix())


def _even_pre_bwd(proj, tab, qg, kg, drq, drk, drv, drg, dqs, dks, dvs):
    S = proj.shape[0]
    tm = 256
    npat = len(dqs)

    def body(p_ref, tab_ref, qg_ref, kg_ref, g_ref, drq_ref, drk_ref, drv_ref, drg_ref, *rest):
        dq_refs, dk_refs, dv_refs = rest[:npat], rest[npat:2 * npat], rest[2 * npat:3 * npat]
        dp_ref, dqg_ref, dkg_ref = rest[3 * npat:]
        Ar, Br, Cr = _tab(tab_ref, 0)
        Ap, Bp, Cp = _tab(tab_ref, 1)
        G = g_ref[...]
        for c in range(2):
            sl = slice(c * 128, (c + 1) * 128)
            dp_ref[:, c * 128:(c + 1) * 128] = _rope_t(drq_ref[:, sl], Ar, Br, Cr, 32).astype(BF16)
            dp_ref[:, 256 + c * 128:256 + (c + 1) * 128] = _rope_t(drk_ref[:, sl] * 0.125, Ar, Br, Cr, 32).astype(BF16)
        dp_ref[:, 512:1024] = drv_ref[...].astype(BF16)
        dp_ref[:, 1024:1536] = drg_ref[...].astype(BF16)
        accq = jnp.zeros((1, LANES), F32)
        acck = jnp.zeros((1, LANES), F32)
        for c in range(4):
            sl = slice(c * 128, (c + 1) * 128)
            g = dq_refs[0][:, sl]
            for r in dq_refs[1:]:
                g = g + r[:, sl]
            dx, dg = _hn_bwd(p_ref[:, 1536 + c * 128:1536 + (c + 1) * 128], qg_ref[...], _rope_t(g, Ap, Bp, Cp, 8), G)
            dp_ref[:, 1536 + c * 128:1536 + (c + 1) * 128] = dx.astype(BF16)
            accq = accq + dg
            g = dk_refs[0][:, sl]
            for r in dk_refs[1:]:
                g = g + r[:, sl]
            dx, dg = _hn_bwd(p_ref[:, 2048 + c * 128:2048 + (c + 1) * 128], kg_ref[...], _rope_t(g, Ap, Bp, Cp, 8), G)
            dp_ref[:, 2048 + c * 128:2048 + (c + 1) * 128] = dx.astype(BF16)
            acck = acck + dg
        g = dv_refs[0][...]
        for r in dv_refs[1:]:
            g = g + r[...]
        dp_ref[:, 2560:3072] = g.astype(BF16)

        @pl.when(pl.program_id(0) == 0)
        def _():
            dqg_ref[...] = jnp.zeros_like(dqg_ref)
            dkg_ref[...] = jnp.zeros_like(dkg_ref)

        dqg_ref[...] += _fold_halves(accq)
        dkg_ref[...] += _fold_halves(acck)

    row = lambda w: pl.BlockSpec((tm, w), lambda i: (i, 0))
    vec = pl.BlockSpec((1, LANES), lambda i: (0, 0))
    return _pc(
        body, name="even_pre_bwd", grid=(S // tm,),
        in_specs=[row(3072), row(768), vec, vec, pl.BlockSpec((LANES, LANES), lambda i: (0, 0)),
                  row(256), row(256), row(512), row(512)] + [row(512)] * (3 * npat),
        out_specs=[row(3072), vec, vec],
        out_shape=[jax.ShapeDtypeStruct((S, 3072), BF16), jax.ShapeDtypeStruct((1, LANES), F32),
                   jax.ShapeDtypeStruct((1, LANES), F32)],
        compiler_params=_params(("arbitrary",)),
    )(proj, tab, qg, kg, _group_matrix(), drq, drk, drv, drg, *dqs, *dks, *dvs)


def _ret_consts(pair, half):
    lg = jnp.where(pair == 0, _LOG_GAMMA[half], _LOG_GAMMA[2 + half]).astype(F32)
    i = lax.broadcasted_iota(jnp.int32, (BLK, BLK), 0)
    j = lax.broadcasted_iota(jnp.int32, (BLK, BLK), 1)
    diff = (i - j).astype(F32)
    decay = jnp.where(diff >= 0, jnp.exp(lg * jnp.maximum(diff, 0.0)), 0.0)
    t = lax.broadcasted_iota(jnp.int32, (BLK, 1), 0).astype(F32)
    xi = jnp.exp(lg * (t + 1.0))
    zeta = jnp.exp(lg * (BLK - 1.0 - t))
    cd = jnp.exp(jnp.full((1, 1), BLK, F32) * lg)
    return decay, xi, zeta, cd


RET_STEP = 8


def _ret_fwd(rq, rk, rv):
    S = rq.shape[0]
    nc = S // BLK
    rows = RET_STEP * BLK

    def body(q_ref, k_ref, v_ref, o_ref, st_ref, R):
        p, n = pl.program_id(0), pl.program_id(1)

        @pl.when(n == 0)
        def _():
            R[...] = jnp.zeros_like(R)

        consts = [_ret_consts(p, half) for half in range(2)]
        masks = [_head_mask((BLK, LANES), half) for half in range(2)]
        for ci in range(RET_STEP):
            rs = slice(ci * BLK, (ci + 1) * BLK)
            q2, k2 = q_ref[rs, :], k_ref[rs, :]
            for half in range(2):
                decay, xi, zeta, cd = consts[half]
                m = masks[half]
                qm = jnp.where(m, q2, jnp.zeros_like(q2))
                km = jnp.where(m, k2, jnp.zeros_like(k2))
                v = v_ref[rs, half * 128:(half + 1) * 128]
                Rb = R[half].astype(BF16)
                st_ref[ci, half] = Rb
                sc = lax.dot_general(qm, k2, (((1,), (1,)), ((), ())), preferred_element_type=F32) * decay
                o = jnp.dot(sc.astype(BF16), v, preferred_element_type=F32)
                o = o + jnp.dot(qm, Rb, preferred_element_type=F32) * xi
                o_ref[rs, half * 128:(half + 1) * 128] = o
                kz = (km.astype(F32) * zeta).astype(BF16)
                R[half] = R[half] * cd + lax.dot_general(kz, v, (((0,), (0,)), ((), ())), preferred_element_type=F32)

    return _pc(
        body, name="ret_fwd", grid=(2, nc // RET_STEP),
        in_specs=[pl.BlockSpec((rows, 128), lambda p, n: (n, p)), pl.BlockSpec((rows, 128), lambda p, n: (n, p)),
                  pl.BlockSpec((rows, 256), lambda p, n: (n, p))],
        out_specs=[pl.BlockSpec((rows, 256), lambda p, n: (n, p)),
                   pl.BlockSpec((None, RET_STEP, 2, 128, 128), lambda p, n: (p, n, 0, 0, 0))],
        out_shape=[jax.ShapeDtypeStruct((S, 512), F32), jax.ShapeDtypeStruct((2, nc, 2, 128, 128), BF16)],
        scratch_shapes=[pltpu.VMEM((2, 128, 128), F32)],
        compiler_params=_params(("parallel", "arbitrary")),
    )(rq, rk, rv)


def _ret_bwd(rq, rk, rv, states, do):
    S = rq.shape[0]
    nc = S // BLK
    ns = nc // RET_STEP
    rows = RET_STEP * BLK
    nt = (((1,), (1,)), ((), ()))
    tn = (((0,), (0,)), ((), ()))

    def body(q_ref, k_ref, v_ref, st_ref, do_ref, dq_ref, dk_ref, dv_ref, U):
        p, n = pl.program_id(0), pl.program_id(1)

        @pl.when(n == 0)
        def _():
            U[...] = jnp.zeros_like(U)

        consts = [_ret_consts(p, half) for half in range(2)]
        masks = [_head_mask((BLK, LANES), half) for half in range(2)]
        for ci in reversed(range(RET_STEP)):
            rs = slice(ci * BLK, (ci + 1) * BLK)
            q2, k2 = q_ref[rs, :], k_ref[rs, :]
            dq_acc = jnp.zeros((BLK, LANES), F32)
            dk_acc = jnp.zeros((BLK, LANES), F32)
            for half in range(2):
                decay, xi, zeta, cd = consts[half]
                m = masks[half]
                qm = jnp.where(m, q2, jnp.zeros_like(q2))
                km = jnp.where(m, k2, jnp.zeros_like(k2))
                v = v_ref[rs, half * 128:(half + 1) * 128]
                dob = do_ref[rs, half * 128:(half + 1) * 128].astype(BF16)
                Rb = st_ref[ci, half]
                Ub = U[half].astype(BF16)
                dsc = (lax.dot_general(dob, v, nt, preferred_element_type=F32) * decay).astype(BF16)
                xdo = (dob.astype(F32) * xi).astype(BF16)
                dq_acc += jnp.dot(dsc, km, preferred_element_type=F32) + lax.dot_general(xdo, Rb, nt, preferred_element_type=F32)
                dk_acc += lax.dot_general(dsc, qm, tn, preferred_element_type=F32) \
                    + lax.dot_general(v, Ub, nt, preferred_element_type=F32) * zeta
                sc = (lax.dot_general(qm, k2, nt, preferred_element_type=F32) * decay).astype(BF16)
                kz = (km.astype(F32) * zeta).astype(BF16)
                dv_ref[rs, half * 128:(half + 1) * 128] = lax.dot_general(sc, dob, tn, preferred_element_type=F32) \
                    + jnp.dot(kz, Ub, preferred_element_type=F32)
                U[half] = U[half] * cd + lax.dot_general(qm, xdo, tn, preferred_element_type=F32)
            dq_ref[rs, :] = dq_acc
            dk_ref[rs, :] = dk_acc

    rev = lambda w: pl.BlockSpec((rows, w), lambda p, n: (ns - 1 - n, p))
    return _pc(
        body, name="ret_bwd", grid=(2, ns),
        in_specs=[rev(128), rev(128), rev(256),
                  pl.BlockSpec((None, RET_STEP, 2, 128, 128), lambda p, n: (p, ns - 1 - n, 0, 0, 0)), rev(256)],
        out_specs=[rev(128), rev(128), rev(256)],
        out_shape=[jax.ShapeDtypeStruct((S, 256), F32), jax.ShapeDtypeStruct((S, 256), F32),
                   jax.ShapeDtypeStruct((S, 512), F32)],
        scratch_shapes=[pltpu.VMEM((2, 128, 128), F32)],
        compiler_params=_params(("parallel", "arbitrary")),
    )(rq, rk, rv, states, do)


def _col_of(b, m):
    return jnp.max(jnp.where(m, b, -jnp.inf), axis=1, keepdims=True)


def _attn_fwd(q, k, v, *, nq, max_dist, name, sinks=None, want_bf16=False):
    L, Ck = k.shape
    nb, ncol = L // BLK, Ck // LANES
    scale = HEAD ** -0.5
    has_sink = sinks is not None

    def body(*refs):
        q_ref, kp_ref, kc_ref, vp_ref, vc_ref = refs[:5]
        sk_ref = refs[5] if has_sink else None
        outs = refs[5 + has_sink:]
        n = pl.program_id(1)
        kcat = jnp.concatenate([kp_ref[...], kc_ref[...]], axis=0)
        vcat = jnp.concatenate([vp_ref[...], vc_ref[...]], axis=0)
        r = lax.broadcasted_iota(jnp.int32, (BLK, 2 * BLK), 0)
        c = lax.broadcasted_iota(jnp.int32, (BLK, 2 * BLK), 1)
        dist = r + BLK - c
        valid = (dist >= 0) & (dist <= max_dist) & ((c >= BLK) | (n > 0))
        for i in range(nq):
            q2 = q_ref[:, i * 128:(i + 1) * 128]
            o2 = jnp.zeros((BLK, LANES), F32)
            l2 = jnp.zeros((BLK, LANES), F32)
            for half in range(2):
                m = _head_mask((BLK, LANES), half)
                qm = jnp.where(m, q2, jnp.zeros_like(q2))
                s = lax.dot_general(qm, kcat, (((1,), (1,)), ((), ())), preferred_element_type=F32) * scale
                s = jnp.where(valid, s, -jnp.inf)
                mx = jnp.max(s, axis=1, keepdims=True)
                if has_sink:
                    snk = _col_of(sk_ref[:, i * 128:(i + 1) * 128], _head_mask((1, LANES), half))
                    mx = jnp.maximum(mx, snk)
                pr = jnp.exp(s - mx)
                den = jnp.sum(pr, axis=1, keepdims=True)
                if has_sink:
                    den = den + jnp.exp(snk - mx)
                pv = jnp.dot(pr.astype(BF16), vcat, preferred_element_type=F32) / den
                o2 = jnp.where(m, pv, o2)
                l2 = jnp.where(m, mx + jnp.log(den), l2)
            outs[0][:, i * 128:(i + 1) * 128] = o2
            outs[1][:, i * 128:(i + 1) * 128] = l2
            if want_bf16:
                outs[2][:, i * 128:(i + 1) * 128] = o2.astype(BF16)

    qspec = pl.BlockSpec((BLK, nq * 128), lambda j, n: (n, j))
    cur = pl.BlockSpec((BLK, 128), lambda j, n: (n, j))
    prev = pl.BlockSpec((BLK, 128), lambda j, n: (jnp.maximum(n - 1, 0), j))
    in_specs = [qspec, prev, cur, prev, cur]
    args = [q, k, k, v, v]
    if has_sink:
        in_specs.append(pl.BlockSpec((1, nq * 128), lambda j, n: (0, j)))
        args.append(sinks)
    out_dts = [F32, F32] + ([BF16] if want_bf16 else [])
    return _pc(
        body, name=name, grid=(ncol, nb), in_specs=in_specs,
        out_specs=[qspec] * len(out_dts),
        out_shape=[jax.ShapeDtypeStruct(q.shape, dt) for dt in out_dts],
        compiler_params=_params(("parallel", "parallel")),
    )(*args)


def _attn_bwd(q, k, v, o, lse, do, *, nq, max_dist, name, sinks=None):
    L, Ck = k.shape
    nb, ncol = L // BLK, Ck // LANES
    scale = HEAD ** -0.5
    has_sink = sinks is not None
    nt = (((1,), (1,)), ((), ()))
    tn = (((0,), (0,)), ((), ()))

    def body(*refs):
        (qc_ref, qn_ref, kp_ref, kc_ref, vp_ref, vc_ref, oc_ref, on_ref, lc_ref, ln_ref, dc_ref, dn_ref) = refs[:12]
        sk_ref = refs[12] if has_sink else None
        outs = refs[12 + has_sink:]
        dq_ref, dk_ref, dv_ref = outs[:3]
        n = pl.program_id(1)
        kc, vc = kc_ref[...], vc_ref[...]
        kcat = jnp.concatenate([kp_ref[...], kc], axis=0)
        vcat = jnp.concatenate([vp_ref[...], vc], axis=0)
        r = lax.broadcasted_iota(jnp.int32, (BLK, 2 * BLK), 0)
        c = lax.broadcasted_iota(jnp.int32, (BLK, 2 * BLK), 1)
        dist = r + BLK - c
        valid_q = (dist >= 0) & (dist <= max_dist) & ((c >= BLK) | (n > 0))
        r2 = lax.broadcasted_iota(jnp.int32, (2 * BLK, BLK), 0)
        c2 = lax.broadcasted_iota(jnp.int32, (2 * BLK, BLK), 1)
        dist2 = r2 - c2
        valid_k = (dist2 >= 0) & (dist2 <= max_dist) & ((r2 < BLK) | (n < nb - 1))
        dk_acc = jnp.zeros((BLK, LANES), F32)
        dv_acc = jnp.zeros((BLK, LANES), F32)
        for i in range(nq):
            sl = slice(i * 128, (i + 1) * 128)
            qcur, docur = qc_ref[:, sl], dc_ref[:, sl]
            qcat = jnp.concatenate([qcur, qn_ref[:, sl]], axis=0)
            docat = jnp.concatenate([docur, dn_ref[:, sl]], axis=0)
            ocat = jnp.concatenate([oc_ref[:, sl], on_ref[:, sl]], axis=0)
            lcat = jnp.concatenate([lc_ref[:, sl], ln_ref[:, sl]], axis=0)
            dq2 = jnp.zeros((BLK, LANES), F32)
            ds2 = jnp.zeros((1, LANES), F32)
            for half in range(2):
                m1 = _head_mask((BLK, LANES), half)
                m2 = _head_mask((2 * BLK, LANES), half)
                dom = jnp.where(m2, docat, 0.0)
                delta = jnp.sum(dom * ocat, axis=1, keepdims=True)
                lcol = _col_of(lcat, m2)
                domb = dom.astype(BF16)
                qmcat = jnp.where(m2, qcat, jnp.zeros_like(qcat))
                qm = qmcat[:BLK]
                s = lax.dot_general(qm, kcat, nt, preferred_element_type=F32) * scale
                pr = jnp.where(valid_q, jnp.exp(s - lcol[:BLK]), 0.0)
                dp = lax.dot_general(domb[:BLK], vcat, nt, preferred_element_type=F32)
                ds = (pr * (dp - delta[:BLK])).astype(BF16)
                dq2 = jnp.where(m1, jnp.dot(ds, kcat, preferred_element_type=F32) * scale, dq2)
                if has_sink:
                    snk = _col_of(sk_ref[:, sl], _head_mask((1, LANES), half))
                    contrib = jnp.sum(-jnp.exp(snk - lcol[:BLK]) * delta[:BLK], axis=0, keepdims=True)
                    ds2 = jnp.where(_head_mask((1, LANES), half), contrib, ds2)
                s = lax.dot_general(qmcat, kc, nt, preferred_element_type=F32) * scale
                pr = jnp.where(valid_k, jnp.exp(s - lcol), 0.0)
                dv_acc += lax.dot_general(pr.astype(BF16), domb, tn, preferred_element_type=F32)
                dp = lax.dot_general(domb, vc, nt, preferred_element_type=F32)
                ds = (pr * (dp - delta)).astype(BF16)
                dk_acc += lax.dot_general(ds, qmcat, tn, preferred_element_type=F32) * scale
            dq_ref[:, sl] = dq2
            if has_sink:
                @pl.when(n == 0)
                def _():
                    outs[3][:, sl] = jnp.zeros((1, LANES), F32)

                outs[3][:, sl] += ds2
        dk_ref[...] = dk_acc
        dv_ref[...] = dv_acc

    qcur = pl.BlockSpec((BLK, nq * 128), lambda j, n: (n, j))
    qnext = pl.BlockSpec((BLK, nq * 128), lambda j, n: (jnp.minimum(n + 1, nb - 1), j))
    cur = pl.BlockSpec((BLK, 128), lambda j, n: (n, j))
    prev = pl.BlockSpec((BLK, 128), lambda j, n: (jnp.maximum(n - 1, 0), j))
    in_specs = [qcur, qnext, prev, cur, prev, cur, qcur, qnext, qcur, qnext, qcur, qnext]
    args = [q, q, k, k, v, v, o, o, lse, lse, do, do]
    out_specs = [qcur, cur, cur]
    out_shape = [jax.ShapeDtypeStruct(q.shape, F32), jax.ShapeDtypeStruct(k.shape, F32), jax.ShapeDtypeStruct(k.shape, F32)]
    if has_sink:
        vec = pl.BlockSpec((1, nq * 128), lambda j, n: (0, j))
        in_specs.append(vec)
        args.append(sinks)
        out_specs.append(vec)
        out_shape.append(jax.ShapeDtypeStruct((1, q.shape[1]), F32))
    return _pc(
        body, name=name, grid=(ncol, nb), in_specs=in_specs, out_specs=out_specs, out_shape=out_shape,
        compiler_params=_params(("parallel", "arbitrary")),
    )(*args)


ATT_TILE = 2048


def _rows(ref, start, n, r):
    if r == 1:
        return ref[pl.ds(start, n), :]
    return ref[pl.ds(start, n, stride=r), :]


def _twice(x):
    return jnp.concatenate([x, x], axis=0)


def _stack_heads(x, masks):
    zero = jnp.zeros_like(x)
    return jnp.concatenate([jnp.where(masks[0], x, zero), jnp.where(masks[1], x, zero)], axis=0)


def _set_rows(ref, start, n, r, val):
    if r == 1:
        ref[pl.ds(start, n), :] = val
    else:
        ref[pl.ds(start, n, stride=r), :] = val


def _band_geometry(S, patterns):
    rmax = max(r for _, r in patterns)
    H = BLK * rmax
    T = min(S, ATT_TILE)
    assert T % H == 0 and S % T == 0
    return H, T, S // T, T // BLK


def _band_fwd(q, k, v, *, patterns, nq, name, sinks=None, want_bf16=False):
    S, Ck = k.shape
    H, T, nt, nbt = _band_geometry(S, patterns)
    ncol = Ck // LANES
    scale = HEAD ** -0.5
    has_sink = sinks is not None
    nt_dims = (((1,), (1,)), ((), ()))

    def body(*refs):
        q_ref, kp_ref, kc_ref, vp_ref, vc_ref = refs[:5]
        sk_ref = refs[5] if has_sink else None
        n_out = 3 if want_bf16 else 2
        outs = refs[5 + has_sink:5 + has_sink + n_out]
        qf, kf, vf, M, L, A = refs[5 + has_sink + n_out:]
        t = pl.program_id(1)
        kf[0:H, :] = kp_ref[...].astype(F32)
        kf[H:H + T, :] = kc_ref[...].astype(F32)
        vf[0:H, :] = vp_ref[...].astype(F32)
        vf[H:H + T, :] = vc_ref[...].astype(F32)
        r_i = lax.broadcasted_iota(jnp.int32, (BLK, 2 * BLK), 0)
        c_i = lax.broadcasted_iota(jnp.int32, (BLK, 2 * BLK), 1)
        dist_i = r_i + BLK - c_i
        masks = [_head_mask((BLK, LANES), h) for h in range(2)]

        for i in range(nq):
            qf[...] = q_ref[:, i * 128:(i + 1) * 128].astype(F32) * scale
            for p, (dist, r) in enumerate(patterns):
                in_band = (dist_i >= 0) & (dist_i <= dist)
                in_band_first = in_band & ((c_i >= BLK) | (t > 0))
                in_band, in_band_first = _twice(in_band), _twice(in_band_first)

                def unit(j, b, p=p, r=r, in_band=in_band, in_band_first=in_band_first):
                    q0 = j + b * (BLK * r)
                    q2 = _rows(qf, q0, BLK, r).astype(BF16)
                    kcat = _rows(kf, H + q0 - BLK * r, 2 * BLK, r).astype(BF16)
                    vcat = _rows(vf, H + q0 - BLK * r, 2 * BLK, r).astype(BF16)
                    valid = in_band if b > 0 else in_band_first
                    s = lax.dot_general(_stack_heads(q2, masks), kcat, nt_dims, preferred_element_type=F32)
                    s = jnp.where(valid, s, -jnp.inf)
                    mx = jnp.max(s, axis=1, keepdims=True)
                    pr = jnp.exp(s - mx)
                    den = jnp.sum(pr, axis=1, keepdims=True)
                    pv = jnp.dot(pr.astype(BF16), vcat, preferred_element_type=F32)
                    m2 = jnp.where(masks[0], mx[:BLK], mx[BLK:])
                    l2 = jnp.where(masks[0], den[:BLK], den[BLK:])
                    a2 = jnp.where(masks[0], pv[:BLK], pv[BLK:])
                    if p > 0:
                        mo = _rows(M, q0, BLK, r)
                        mn = jnp.maximum(mo, m2)
                        wa, wb = jnp.exp(mo - mn), jnp.exp(m2 - mn)
                        l2 = wa * _rows(L, q0, BLK, r) + wb * l2
                        a2 = wa * _rows(A, q0, BLK, r) + wb * a2
                        m2 = mn
                    _set_rows(M, q0, BLK, r, m2)
                    _set_rows(L, q0, BLK, r, l2)
                    _set_rows(A, q0, BLK, r, a2)

                for u in range(nbt):
                    unit(u % r, u // r)
            sl = slice(i * 128, (i + 1) * 128)
            mm, ll, aa = M[...], L[...], A[...]
            if has_sink:
                snk = sk_ref[:, sl]
                mn = jnp.maximum(mm, snk)
                w = jnp.exp(mm - mn)
                ll = ll * w + jnp.exp(snk - mn)
                aa = aa * w
                mm = mn
            o = aa / ll
            outs[0][:, sl] = o
            outs[1][:, sl] = mm + jnp.log(ll)
            if want_bf16:
                outs[2][:, sl] = o.astype(BF16)

    th = T // H
    qspec = pl.BlockSpec((T, nq * 128), lambda j, t: (t, j))
    cur = pl.BlockSpec((T, 128), lambda j, t: (t, j))
    prev = pl.BlockSpec((H, 128), lambda j, t: (jnp.maximum(t * th - 1, 0), j))
    in_specs = [qspec, prev, cur, prev, cur]
    args = [q, k, k, v, v]
    if has_sink:
        in_specs.append(pl.BlockSpec((1, nq * 128), lambda j, t: (0, j)))
        args.append(sinks)
    out_dts = [F32, F32] + ([BF16] if want_bf16 else [])
    return _pc(
        body, name=name, grid=(ncol, nt), in_specs=in_specs,
        out_specs=[qspec] * len(out_dts),
        out_shape=[jax.ShapeDtypeStruct(q.shape, dt) for dt in out_dts],
        scratch_shapes=[pltpu.VMEM((T, LANES), F32), pltpu.VMEM((H + T, LANES), F32), pltpu.VMEM((H + T, LANES), F32),
                        pltpu.VMEM((T, LANES), F32), pltpu.VMEM((T, LANES), F32), pltpu.VMEM((T, LANES), F32)],
        compiler_params=_params(("parallel", "parallel")),
    )(*args)


def _band_bwd(q, k, v, lse, delta, do, *, patterns, nq, name, sinks=None, do_col0=0):
    S, Ck = k.shape
    H, T, nt, nbt = _band_geometry(S, patterns)
    ncol = Ck // LANES
    scale = HEAD ** -0.5
    has_sink = sinks is not None
    nt_dims = (((1,), (1,)), ((), ()))
    tn_dims = (((0,), (0,)), ((), ()))

    def body(*refs):
        (qc_ref, qn_ref, kp_ref, kc_ref, vp_ref, vc_ref, lc_ref, ln_ref, ec_ref, en_ref, dc_ref, dn_ref) = refs[:12]
        sk_ref = refs[12] if has_sink else None
        n_out = 4 if has_sink else 3
        outs = refs[12 + has_sink:12 + has_sink + n_out]
        dq_ref, dk_ref, dv_ref = outs[:3]
        qf, kf, vf, lf, ef, df = refs[12 + has_sink + n_out:]
        t = pl.program_id(1)
        kf[0:H, :] = kp_ref[...].astype(F32)
        kf[H:H + T, :] = kc_ref[...].astype(F32)
        vf[0:H, :] = vp_ref[...].astype(F32)
        vf[H:H + T, :] = vc_ref[...].astype(F32)
        dk_ref[...] = jnp.zeros_like(dk_ref)
        dv_ref[...] = jnp.zeros_like(dv_ref)
        r_i = lax.broadcasted_iota(jnp.int32, (BLK, 2 * BLK), 0)
        c_i = lax.broadcasted_iota(jnp.int32, (BLK, 2 * BLK), 1)
        dist_q = r_i + BLK - c_i
        dist_h = dist_q[:, :BLK]
        m1 = [_head_mask((BLK, LANES), h) for h in range(2)]

        def stacked_inputs(q2, do2, l2, e2):
            spread = lambda v: jnp.concatenate([jnp.where(m1[0], v, _roll(v, HEAD)), jnp.where(m1[1], v, _roll(v, HEAD))], axis=0)
            return _stack_heads(q2, m1), _stack_heads(do2.astype(BF16), m1), spread(l2), spread(e2)

        for i in range(nq):
            sl = slice(i * 128, (i + 1) * 128)
            qf[0:T, :] = qc_ref[:, sl].astype(F32) * scale
            qf[T:T + H, :] = qn_ref[:, sl].astype(F32) * scale
            for buf, c_ref, n_ref in ((lf, lc_ref, ln_ref), (ef, ec_ref, en_ref), (df, dc_ref, dn_ref)):
                buf[0:T, :] = c_ref[:, sl]
                buf[T:T + H, :] = n_ref[:, sl]
            if has_sink:
                @pl.when(t == 0)
                def _():
                    outs[3][:, sl] = jnp.zeros((1, LANES), F32)

                outs[3][:, sl] += jnp.sum(-jnp.exp(sk_ref[:, sl] - lc_ref[:, sl]) * ec_ref[:, sl], axis=0, keepdims=True)
            for p, (dist, r) in enumerate(patterns):
                band_q = (dist_q >= 0) & (dist_q <= dist)
                band_first = band_q & ((c_i >= BLK) | (t > 0))
                band_h = (dist_h >= 0) & (dist_h <= dist)
                band_q, band_first, band_h = _twice(band_q), _twice(band_first), _twice(band_h)

                def add_rows(ref, start, val, r=r):
                    _set_rows(ref, start, BLK, r, _rows(ref, start, BLK, r) + val)

                def unit(j, b, p=p, r=r, band_q=band_q, band_first=band_first):
                    q0 = j + b * (BLK * r)
                    q2 = _rows(qf, q0, BLK, r).astype(BF16)
                    do2, l2, e2 = _rows(df, q0, BLK, r), _rows(lf, q0, BLK, r), _rows(ef, q0, BLK, r)
                    kcat = _rows(kf, H + q0 - BLK * r, 2 * BLK, r).astype(BF16)
                    vcat = _rows(vf, H + q0 - BLK * r, 2 * BLK, r).astype(BF16)
                    valid = band_q if b > 0 else band_first
                    qs, dos, ls, es = stacked_inputs(q2, do2, l2, e2)
                    s = lax.dot_general(qs, kcat, nt_dims, preferred_element_type=F32)
                    pr = jnp.where(valid, jnp.exp(s - jnp.concatenate([ls, ls], axis=1)), 0.0)
                    dp = lax.dot_general(dos, vcat, nt_dims, preferred_element_type=F32)
                    ds = (pr * (dp - jnp.concatenate([es, es], axis=1))).astype(BF16)
                    dqs = jnp.dot(ds, kcat, preferred_element_type=F32) * scale
                    dq2 = jnp.where(m1[0], dqs[:BLK], dqs[BLK:])
                    dvc = lax.dot_general(pr.astype(BF16), dos, tn_dims, preferred_element_type=F32)
                    dkc = lax.dot_general(ds, qs, tn_dims, preferred_element_type=F32)
                    if p > 0:
                        dq2 = dq2 + _rows(dq_ref.at[:, sl], q0, BLK, r)
                    _set_rows(dq_ref.at[:, sl], q0, BLK, r, dq2)
                    add_rows(dk_ref, q0, dkc[BLK:])
                    add_rows(dv_ref, q0, dvc[BLK:])
                    if b > 0:
                        add_rows(dk_ref, q0 - BLK * r, dkc[:BLK])
                        add_rows(dv_ref, q0 - BLK * r, dvc[:BLK])

                def halo_unit(j, r=r, band_h=band_h):
                    k0 = j + (nbt // r - 1) * (BLK * r)
                    q2 = _rows(qf, T + j, BLK, r).astype(BF16)
                    do2, l2, e2 = _rows(df, T + j, BLK, r), _rows(lf, T + j, BLK, r), _rows(ef, T + j, BLK, r)
                    kc = _rows(kf, H + k0, BLK, r).astype(BF16)
                    vc = _rows(vf, H + k0, BLK, r).astype(BF16)
                    qs, dos, ls, es = stacked_inputs(q2, do2, l2, e2)
                    s = lax.dot_general(qs, kc, nt_dims, preferred_element_type=F32)
                    pr = jnp.where(band_h, jnp.exp(s - ls), 0.0)
                    dp = lax.dot_general(dos, vc, nt_dims, preferred_element_type=F32)
                    ds = (pr * (dp - es)).astype(BF16)
                    add_rows(dk_ref, k0, lax.dot_general(ds, qs, tn_dims, preferred_element_type=F32))
                    add_rows(dv_ref, k0, lax.dot_general(pr.astype(BF16), dos, tn_dims, preferred_element_type=F32))

                for u in range(nbt):
                    unit(u % r, u // r)
                if nt > 1:
                    @pl.when(t < nt - 1)
                    def _(r=r, halo_unit=halo_unit):
                        for j in range(r):
                            halo_unit(j)

    th = T // H
    last = S // H - 1
    qcur = pl.BlockSpec((T, nq * 128), lambda j, t: (t, j))
    qnext = pl.BlockSpec((H, nq * 128), lambda j, t: (jnp.minimum((t + 1) * th, last), j))
    cur = pl.BlockSpec((T, 128), lambda j, t: (t, j))
    prev = pl.BlockSpec((H, 128), lambda j, t: (jnp.maximum(t * th - 1, 0), j))
    dcur = pl.BlockSpec((T, nq * 128), lambda j, t: (t, j + do_col0))
    dnext = pl.BlockSpec((H, nq * 128), lambda j, t: (jnp.minimum((t + 1) * th, last), j + do_col0))
    in_specs = [qcur, qnext, prev, cur, prev, cur, qcur, qnext, qcur, qnext, dcur, dnext]
    args = [q, q, k, k, v, v, lse, lse, delta, delta, do, do]
    out_specs = [qcur, cur, cur]
    out_shape = [jax.ShapeDtypeStruct(q.shape, F32), jax.ShapeDtypeStruct(k.shape, F32), jax.ShapeDtypeStruct(k.shape, F32)]
    if has_sink:
        vec = pl.BlockSpec((1, nq * 128), lambda j, t: (0, j))
        in_specs.append(vec)
        args.append(sinks)
        out_specs.append(vec)
        out_shape.append(jax.ShapeDtypeStruct((1, q.shape[1]), F32))
    big = pltpu.VMEM((T + H, LANES), F32)
    return _pc(
        body, name=name, grid=(ncol, nt), in_specs=in_specs, out_specs=out_specs, out_shape=out_shape,
        scratch_shapes=[big] * 6,
        compiler_params=_params(("parallel", "arbitrary")),
    )(*args)


def _delta(do, o, name):
    S, C = do.shape
    tm = 512

    def body(do_ref, o_ref, g_ref, e_ref):
        for c in range(C // LANES):
            sl = slice(c * 128, (c + 1) * 128)
            e_ref[:, sl] = _gmean(do_ref[:, sl] * o_ref[:, sl], g_ref[...]) * float(HEAD)

    row = pl.BlockSpec((tm, C), lambda i: (i, 0))
    return _pc(
        body, name=name, grid=(S // tm,),
        in_specs=[row, row, pl.BlockSpec((LANES, LANES), lambda i: (0, 0))], out_specs=row,
        out_shape=jax.ShapeDtypeStruct((S, C), F32),
        compiler_params=_params(("parallel",)),
    )(do, o, _group_matrix())


def _even_post_fwd(ro, proj, gn, da):
    S = ro.shape[0]
    tm = 256

    def body(ro_ref, rg_ref, gn_ref, da_ref, mix_ref):
        for c in range(4):
            sl = slice(c * 128, (c + 1) * 128)
            x = ro_ref[:, sl]
            mu = jnp.mean(x, axis=1, keepdims=True)
            xc = x - mu
            var = jnp.mean(xc * xc, axis=1, keepdims=True)
            y = xc * lax.rsqrt(var + EPS) * gn_ref[:, sl]
            z = rg_ref[:, sl]
            mix_ref[:, sl] = (z * jax.nn.sigmoid(z) * y).astype(BF16)
        mix_ref[:, 512:1024] = da_ref[...].astype(BF16)

    row = lambda w: pl.BlockSpec((tm, w), lambda i: (i, 0))
    return _pc(
        body, name="even_post_fwd", grid=(S // tm,),
        in_specs=[row(512), pl.BlockSpec((tm, 512), lambda i: (i, 2)), pl.BlockSpec((1, 512), lambda i: (0, 0)), row(512)],
        out_specs=row(1024), out_shape=jax.ShapeDtypeStruct((S, 1024), BF16),
        compiler_params=_params(("parallel",)),
    )(ro, proj, gn, da)


def _even_post_bwd(ro, proj, gn, dmixed):
    S = ro.shape[0]
    tm = 256

    def body(ro_ref, rg_ref, gn_ref, dm_ref, dro_ref, drg_ref, dgn_ref):
        @pl.when(pl.program_id(0) == 0)
        def _():
            dgn_ref[...] = jnp.zeros_like(dgn_ref)

        for c in range(4):
            sl = slice(c * 128, (c + 1) * 128)
            x = ro_ref[:, sl]
            mu = jnp.mean(x, axis=1, keepdims=True)
            xc = x - mu
            rstd = lax.rsqrt(jnp.mean(xc * xc, axis=1, keepdims=True) + EPS)
            xh = xc * rstd
            gain = gn_ref[:, sl]
            y = xh * gain
            z = rg_ref[:, sl]
            sg = jax.nn.sigmoid(z)
            dra = dm_ref[:, sl]
            drg_ref[:, sl] = dra * y * sg * (1.0 + z * (1.0 - sg))
            dy = dra * z * sg
            dgn_ref[:, sl] += jnp.sum(dy * xh, axis=0, keepdims=True)
            dxh = dy * gain
            dro_ref[:, sl] = rstd * (dxh - jnp.mean(dxh, axis=1, keepdims=True)
                                     - xh * jnp.mean(dxh * xh, axis=1, keepdims=True))

    row = lambda w: pl.BlockSpec((tm, w), lambda i: (i, 0))
    vec = pl.BlockSpec((1, 512), lambda i: (0, 0))
    return _pc(
        body, name="even_post_bwd", grid=(S // tm,),
        in_specs=[row(512), pl.BlockSpec((tm, 512), lambda i: (i, 2)), vec, row(512)],
        out_specs=[row(512), row(512), vec],
        out_shape=[jax.ShapeDtypeStruct((S, 512), F32), jax.ShapeDtypeStruct((S, 512), F32),
                   jax.ShapeDtypeStruct((1, 512), F32)],
        compiler_params=_params(("arbitrary",)),
    )(ro, proj, gn, dmixed)


def _swa_pre_fwd(proj, tab, qg, kg):
    S = proj.shape[0]
    tm = 256

    def body(p_ref, tab_ref, qg_ref, kg_ref, g_ref, q_ref, k_ref, v_ref):
        Ap, Bp, Cp = _tab(tab_ref, 1)
        G = g_ref[...]
        lo = _head_mask((tm, LANES), 0)
        for c in range(8):
            sl = slice(c * 128, (c + 1) * 128)
            q_ref[:, sl] = _rope(_hn_fwd(p_ref[:, sl], qg_ref[...], G), Ap, Bp, Cp, 8).astype(BF16)
        for c in range(2):
            kn = _rope(_hn_fwd(p_ref[:, 1024 + c * 128:1024 + (c + 1) * 128], kg_ref[...], G), Ap, Bp, Cp, 8)
            vv = p_ref[:, 1280 + c * 128:1280 + (c + 1) * 128]
            for t, ref in ((kn, k_ref), (vv, v_ref)):
                sw = _roll(t, HEAD)
                ref[:, (2 * c) * 128:(2 * c + 1) * 128] = jnp.where(lo, t, sw).astype(BF16)
                ref[:, (2 * c + 1) * 128:(2 * c + 2) * 128] = jnp.where(lo, sw, t).astype(BF16)

    row = lambda w: pl.BlockSpec((tm, w), lambda i: (i, 0))
    vec = pl.BlockSpec((1, LANES), lambda i: (0, 0))
    return _pc(
        body, name="swa_pre_fwd", grid=(S // tm,),
        in_specs=[row(1536), row(768), vec, vec, pl.BlockSpec((LANES, LANES), lambda i: (0, 0))],
        out_specs=[row(1024), row(512), row(512)],
        out_shape=[jax.ShapeDtypeStruct((S, w), BF16) for w in (1024, 512, 512)],
        compiler_params=_params(("parallel",)),
    )(proj, tab, qg, kg, _group_matrix())


def _swa_pre_bwd(proj, tab, qg, kg, dq, dk, dv):
    S = proj.shape[0]
    tm = 256

    def body(p_ref, tab_ref, qg_ref, kg_ref, g_ref, dq_ref, dk_ref, dv_ref, dp_ref, db_ref, dqg_ref, dkg_ref):
        Ap, Bp, Cp = _tab(tab_ref, 1)
        G = g_ref[...]
        lo = _head_mask((tm, LANES), 0)

        @pl.when(pl.program_id(0) == 0)
        def _():
            db_ref[...] = jnp.zeros_like(db_ref)
            dqg_ref[...] = jnp.zeros_like(dqg_ref)
            dkg_ref[...] = jnp.zeros_like(dkg_ref)

        accq = jnp.zeros((1, LANES), F32)
        acck = jnp.zeros((1, LANES), F32)
        for c in range(8):
            sl = slice(c * 128, (c + 1) * 128)
            dx, dg = _hn_bwd(p_ref[:, sl], qg_ref[...], _rope_t(dq_ref[:, sl], Ap, Bp, Cp, 8), G)
            dp_ref[:, sl] = dx.astype(BF16)
            db_ref[:, sl] += jnp.sum(dx, axis=0, keepdims=True)
            accq = accq + dg
        for c in range(2):
            folded = []
            for ref in (dk_ref, dv_ref):
                a = ref[:, (2 * c) * 128:(2 * c + 1) * 128]
                b = ref[:, (2 * c + 1) * 128:(2 * c + 2) * 128]
                folded.append(jnp.where(lo, a + _roll(a, HEAD), b + _roll(b, HEAD)))
            ks = slice(1024 + c * 128, 1024 + (c + 1) * 128)
            dx, dg = _hn_bwd(p_ref[:, ks], kg_ref[...], _rope_t(folded[0], Ap, Bp, Cp, 8), G)
            dp_ref[:, ks] = dx.astype(BF16)
            db_ref[:, ks] += jnp.sum(dx, axis=0, keepdims=True)
            acck = acck + dg
            vs = slice(1280 + c * 128, 1280 + (c + 1) * 128)
            dp_ref[:, vs] = folded[1].astype(BF16)
            db_ref[:, vs] += jnp.sum(folded[1], axis=0, keepdims=True)
        dqg_ref[...] += _fold_halves(accq)
        dkg_ref[...] += _fold_halves(acck)

    row = lambda w: pl.BlockSpec((tm, w), lambda i: (i, 0))
    vec = pl.BlockSpec((1, LANES), lambda i: (0, 0))
    return _pc(
        body, name="swa_pre_bwd", grid=(S // tm,),
        in_specs=[row(1536), row(768), vec, vec, pl.BlockSpec((LANES, LANES), lambda i: (0, 0)),
                  row(1024), row(512), row(512)],
        out_specs=[row(1536), pl.BlockSpec((1, 1536), lambda i: (0, 0)), vec, vec],
        out_shape=[jax.ShapeDtypeStruct((S, 1536), BF16), jax.ShapeDtypeStruct((1, 1536), F32),
                   jax.ShapeDtypeStruct((1, LANES), F32), jax.ShapeDtypeStruct((1, LANES), F32)],
        compiler_params=_params(("arbitrary",)),
    )(proj, tab, qg, kg, _group_matrix(), dq, dk, dv)


def _loss_head(y, target):
    S, Dm = y.shape
    tm = 512

    def body(y_ref, t_ref, l_ref, dy_ref, dyb_ref):
        @pl.when(pl.program_id(0) == 0)
        def _():
            l_ref[...] = jnp.zeros_like(l_ref)

        e = y_ref[...] - t_ref[...]
        dy = e * (1.0 / Dm)
        dy_ref[...] = dy
        dyb_ref[...] = dy.astype(BF16)
        row = jnp.sum(e * e, axis=1, keepdims=True) * (0.5 / Dm)
        l_ref[...] += jnp.sum(row, axis=0, keepdims=True)

    row = pl.BlockSpec((tm, Dm), lambda i: (i, 0))
    return _pc(
        body, name="loss_head", grid=(S // tm,), in_specs=[row, row],
        out_specs=[pl.BlockSpec((1, LANES), lambda i: (0, 0)), row, row],
        out_shape=[jax.ShapeDtypeStruct((1, LANES), F32), jax.ShapeDtypeStruct((S, Dm), F32),
                   jax.ShapeDtypeStruct((S, Dm), BF16)],
        compiler_params=_params(("arbitrary",)),
    )(y, target)


def _relu2_of(u):
    r = jnp.maximum(u.astype(F32), 0.0)
    return r * r


def _drelu2(acc, u):
    return (acc * 2.0 * jnp.maximum(u.astype(F32), 0.0),)


def _add(acc, res):
    return (acc + res,)


def _add_norm_in(res, g):
    def epilogue(acc, r, gv):
        xn = acc + r
        return xn, xn * lax.rsqrt(jnp.mean(xn * xn, axis=-1, keepdims=True) + EPS) * gv

    return dict(outs=[F32, BF16], epilogue=epilogue, extras=[(res, "mn"), (g.reshape(1, D_MODEL), "n")])


_T = dict(tm=1024, tn=1024, tk=1024)


def _rms_bwd_in(x, g, dres):
    def epilogue(dh, xv, gv, dr):
        r = lax.rsqrt(jnp.mean(xv * xv, axis=-1, keepdims=True) + EPS)
        t = dh * gv
        dx = dr + r * t - xv * (r * r * r) * jnp.mean(xv * t, axis=-1, keepdims=True)
        return dx, dx, jnp.sum(dh * xv * r, axis=0, keepdims=True)

    return dict(outs=[F32, BF16, ("colsum",)], epilogue=epilogue,
                extras=[(x, "mn"), (g.reshape(1, D_MODEL), "n"), (dres, "mn")])


def _delta_in(o, col0):
    width = D_MODEL - col0

    def epilogue(do, ov, G):
        parts = [_gmean(do[:, col0 + c * 128:col0 + (c + 1) * 128] * ov[:, c * 128:(c + 1) * 128], G) * float(HEAD)
                 for c in range(width // LANES)]
        return do, jnp.concatenate(parts, axis=1)

    return dict(outs=[F32, (F32, width)], epilogue=epilogue, extras=[(o, width), (_group_matrix(), "full")])


def _mlp_fwd(x, h, wts, layer, tag, next_gain=None):
    u = _matmul(h, wts, dims="nn", **_T, outs=[BF16], b_cs=True, b_row0=layer, name=f"mlp_up{tag}")
    tail = _add_norm_in(x, next_gain) if next_gain is not None else dict(outs=[F32], epilogue=_add, extras=[(x, "mn")])
    out = _matmul(u, wts, dims="nn", **_T, a_pro=_relu2_of, b_rs=1024, b_row0=2 + layer, name=f"mlp_down{tag}", **tail)
    return out, (h, u)


def _mlp_bwd(x, g, wts, layer, saved, dy, dyb, tag):
    h, u = saved
    du = _matmul(dyb, wts, dims="nt", **_T, outs=[BF16], epilogue=_drelu2, extras=[(u, "mn")], b_rs=1024,
                 b_row0=2 + layer, name=f"mlp_du{tag}")
    dw_dn = _matmul(u, dyb, dims="tn", **_T, outs=[F32], a_pro=_relu2_of, name=f"mlp_dwdown{tag}")
    dw_up = _matmul(h, du, dims="tn", **_T, outs=[F32], o_cs=N_CHIPS, name=f"mlp_dwup{tag}")
    dx, dxb, dg = _matmul(du, wts, dims="nt", tm=512, tn=1024, tk=1024, b_cs=True, b_row0=layer, b_rows=1024, name=f"mlp_dh{tag}",
                          **_rms_bwd_in(x, g, dy))
    return dx, dxb, dg, dw_up, dw_dn


def _pattern_view(t, r):
    S, C = t.shape
    return t.reshape(S // r, r * C)


def _local_step(x, pos_col, target, first_of, rest_begin, rest_of, P, red):
    S = x.shape[0]
    tab = _tables(pos_col)
    tile2 = lambda g: jnp.tile(g.reshape(1, HEAD), (1, 2))
    dqg, dkg = tile2(P["dil_q_gain"]), tile2(P["dil_k_gain"])
    sqg, skg = tile2(P["swa_q_gain"]), tile2(P["swa_k_gain"])
    gn = P["ret_gn_gain"].reshape(1, 512)
    sink_b = jnp.repeat(P["swa_sinks"].reshape(16), HEAD).reshape(1, 1024)

    h0 = _rms_fwd(x, P["norm_mix"][0], "rms_mix_fwd0")
    W = first_of(h0)
    proj = _matmul(h0, W["hyb_w_in"], dims="nn", tm=1024, tn=768, tk=1024, outs=[F32], b_cs=True, name="hyb_in")
    rq, rk, rv, dq, dk, dv = _even_pre_fwd(proj, tab, dqg, dkg)
    ro, states = _ret_fwd(rq, rk, rv)
    dil = [(w // r, r) for w, r in DIL_PATTERNS]
    da, dlse = _band_fwd(dq, dk, dv, patterns=dil, nq=1, name="dil_fwd")
    da = rest_begin(da)
    mixed = _even_post_fwd(ro, proj, gn, da)
    x1, h1 = _matmul(mixed, W["hyb_w_out"], dims="nn", **_T, name="hyb_out", **_add_norm_in(x, P["norm_mlp"][0]))
    rest, bias = rest_of(x1)
    W = {**W, **rest}
    (x2, h2), mlp0 = _mlp_fwd(x1, h1, W["packed"], 0, "0", next_gain=P["norm_mix"][1])

    proj2 = _matmul(h2, W["swa_w_qkv"], dims="nn", tm=1024, tn=384, tk=1024, outs=[F32], b_cs=True,
                    epilogue=_add, extras=[(bias.reshape(1, 1536), "n")], name="swa_qkv")
    sq, sk, sv = _swa_pre_fwd(proj2, tab, sqg, skg)
    swa = [(SWA_DIST, 1)]
    so, slse, so_b = _band_fwd(sq, sk, sv, patterns=swa, nq=2, name="swa_fwd", sinks=sink_b, want_bf16=True)
    x3, h3 = _matmul(so_b, W["swa_w_out"], dims="nn", **_T, name="swa_out", **_add_norm_in(x2, P["norm_mlp"][1]))
    y, mlp1 = _mlp_fwd(x3, h3, W["packed"], 1, "1")
    loss, dy, dyb = _loss_head(y, target)

    gw, gp = {}, {}
    dx3, dx3b, dg_mlp1, gw["mlp_w_up1"], gw["mlp_w_down1"] = _mlp_bwd(x3, P["norm_mlp"][1], W["packed"], 1, mlp1, dy, dyb, "1")
    dx3b = red.begin("mlp1", {n: (gw[n], 1024) for n in ("mlp_w_up1", "mlp_w_down1")}, dx3b)
    gw["swa_w_out"] = _matmul(so_b, dx3b, dims="tn", **_T, outs=[F32], name="swa_dwout")
    dso, sdelta = _matmul(dx3b, W["swa_w_out"], dims="nt", tm=512, tn=1024, tk=1024, name="swa_do", **_delta_in(so, 0))
    dsq, dsk, dsv, dsink = _band_bwd(sq, sk, sv, slse, sdelta, dso, patterns=swa, nq=2, name="swa_bwd", sinks=sink_b)
    dproj2, gp["swa_b_qkv"], gp["swa_q_gain"], gp["swa_k_gain"] = _swa_pre_bwd(proj2, tab, sqg, skg, dsq, dsk, dsv)
    gp["swa_sinks"] = dsink
    gw["swa_w_qkv"] = _matmul(h2, dproj2, dims="tn", tm=1024, tn=384, tk=1024, outs=[F32], o_cs=N_CHIPS, name="swa_dwqkv")
    dx2, dx2b, dg_mix1 = _matmul(dproj2, W["swa_w_qkv"], dims="nt", tm=512, tn=1024, tk=384, b_cs=True, name="swa_dh",
                                 **_rms_bwd_in(x2, P["norm_mix"][1], dx3))
    dx2b = red.begin("swa", {"swa_w_qkv": (gw["swa_w_qkv"], 1024), "swa_w_out": (gw["swa_w_out"], 256)}, dx2b)
    dx2b = red.advance("mlp1", dx2b, dx2b)

    dx1, dx1b, dg_mlp0, gw["mlp_w_up0"], gw["mlp_w_down0"] = _mlp_bwd(x1, P["norm_mlp"][0], W["packed"], 0, mlp0, dx2, dx2b, "0")
    gw["hyb_w_out"] = _matmul(mixed, dx1b, dims="tn", **_T, outs=[F32], name="hyb_dwout")
    dx1b = red.begin("mlp0", {"mlp_w_up0": (gw["mlp_w_up0"], 1024), "mlp_w_down0": (gw["mlp_w_down0"], 1024),
                              "hyb_w_out": (gw["hyb_w_out"], 256)}, dx1b)
    dx1b = red.advance("swa", dx1b, dx1b)
    red.finish("mlp1", dx1b)
    dmixed, ddelta = _matmul(dx1b, W["hyb_w_out"], dims="nt", tm=512, tn=1024, tk=1024, name="hyb_dmixed", **_delta_in(da, 512))
    dro, drg, gp["ret_gn_gain"] = _even_post_bwd(ro, proj, gn, dmixed)
    drq, drk, drv = _ret_bwd(rq, rk, rv, states, dro)
    ddq, ddk, ddv = _band_bwd(dq, dk, dv, dlse, ddelta, dmixed, patterns=dil, nq=1, name="dil_bwd", do_col0=4)
    ddq = red.advance("mlp0", ddq, ddq)
    red.finish("swa", ddq)
    dproj, gp["dil_q_gain"], gp["dil_k_gain"] = _even_pre_bwd(proj, tab, dqg, dkg, drq, drk, drv, drg, [ddq], [ddk], [ddv])
    gw["hyb_w_in"] = _matmul(h0, dproj, dims="tn", tm=1024, tn=768, tk=1024, outs=[F32], o_cs=N_CHIPS, name="hyb_dwin")
    dproj = red.begin("win", {"hyb_w_in": (gw["hyb_w_in"], 1024)}, dproj)
    grad_x, _, dg_mix0 = _matmul(dproj, W["hyb_w_in"], dims="nt", tm=512, tn=1024, tk=768, b_cs=True, name="hyb_dh",
                                 **_rms_bwd_in(x, P["norm_mix"][0], dx1))
    red.finish("mlp0", grad_x)
    gp["norm_mix"] = jnp.concatenate([dg_mix0, dg_mix1], axis=0)
    gp["norm_mlp"] = jnp.concatenate([dg_mlp0, dg_mlp1], axis=0)
    return loss, grad_x, gp


HBM = pl.BlockSpec(memory_space=pltpu.HBM)


def _place():
    x, y, c = lax.axis_index("x"), lax.axis_index("y"), lax.axis_index("c")
    chips = [(1 - x, y), (x, 1 - y), (1 - x, 1 - y)]
    return x, y, c, chips


def _allgather_shards(buf):
    _, R, Wd = buf.shape
    Rh = R // 2

    def body(b_ref, out_ref, send_sems, recv_sems):
        x, y, c, chips = _place()
        sibling = (x, y, 1 - c)

        def copy(k, chip, core, to):
            block = b_ref.at[2 * chip[0] + chip[1], pl.ds(core * Rh, Rh), :]
            return pltpu.make_async_remote_copy(
                src_ref=block, dst_ref=block, send_sem=send_sems.at[k], recv_sem=recv_sems.at[k],
                device_id=to, device_id_type=MESH)

        first = [copy(k, (x, y), c, (*chip, c)) for k, chip in enumerate(chips)]
        for cp in first:
            cp.start()
        passed = [copy(3 + k, chip, c, sibling) for k, chip in enumerate(chips)]
        for k, chip in enumerate(chips):
            copy(k, chip, c, (x, y, c)).wait_recv()
            passed[k].start()
        for k, chip in enumerate(chips):
            copy(3 + k, chip, 1 - c, (x, y, c)).wait_recv()
        for cp in first + passed:
            cp.wait_send()

    return _pc(
        body, name="allgather_first", in_specs=[HBM], out_specs=HBM,
        out_shape=jax.ShapeDtypeStruct(buf.shape, buf.dtype), input_output_aliases={0: 0},
        scratch_shapes=[pltpu.SemaphoreType.DMA((6,)), pltpu.SemaphoreType.DMA((6,))],
    )(buf)


SEM = pl.BlockSpec(memory_space=pltpu.SEMAPHORE)
EFFECT = pltpu.SideEffectType.DATAFLOW_SIDE_EFFECTING


def _half_block(ref, chip, core):
    rh = ref.shape[1] // 2
    return ref.at[2 * chip[0] + chip[1], pl.ds(core * rh, rh), :]


def _gather_start(buf, ride, name):
    def body(b_ref, ride_ref, s0, s1, s2, r0, r1, r2, b_out, ride_out):
        x, y, c, chips = _place()
        for chip, s, r in zip(chips, (s0, s1, s2), (r0, r1, r2)):
            mine = _half_block(b_ref, (x, y), c)
            pltpu.make_async_remote_copy(src_ref=mine, dst_ref=mine, send_sem=s, recv_sem=r,
                                         device_id=(*chip, c), device_id_type=MESH).start()

    sem = pltpu.SemaphoreType.DMA(())
    return _pc(
        body, name=name,
        out_shape=(sem,) * 6 + (pltpu.HBM(buf.shape, buf.dtype), pltpu.HBM(ride.shape, ride.dtype)),
        in_specs=(HBM, HBM), out_specs=(SEM,) * 6 + (HBM, HBM), input_output_aliases={0: 6, 1: 7},
        compiler_params=pltpu.CompilerParams(has_side_effects=EFFECT),
    )(pltpu.with_memory_space_constraint(buf, pltpu.HBM), pltpu.with_memory_space_constraint(ride, pltpu.HBM))


def _gather_wait(buf, sems, after, name):
    def body(b_ref, s0, s1, s2, r0, r1, r2, after_ref, b_out):
        x, y, c, chips = _place()
        for chip, s, r in zip(chips, (s0, s1, s2), (r0, r1, r2)):
            cp = pltpu.make_async_remote_copy(src_ref=_half_block(b_ref, (x, y), c), dst_ref=_half_block(b_ref, chip, c),
                                              send_sem=s, recv_sem=r, device_id=(*chip, c), device_id_type=MESH)
            cp.wait_send()
            cp.wait_recv()

    return _pc(
        body, name=name, out_shape=pltpu.HBM(buf.shape, buf.dtype),
        in_specs=(HBM,) + (SEM,) * 6 + (pl.BlockSpec(memory_space=pl.ANY),), out_specs=HBM, input_output_aliases={0: 0},
        compiler_params=pltpu.CompilerParams(has_side_effects=EFFECT),
    )(buf, *sems, after)


def _gather_handover(buf, name):
    def body(b_ref, out_ref, send_sems, recv_sems):
        x, y, c, chips = _place()
        cps = []
        for k, chip in enumerate(chips):
            mine = _half_block(b_ref, chip, c)
            cps.append(pltpu.make_async_remote_copy(src_ref=mine, dst_ref=mine, send_sem=send_sems.at[k],
                                                    recv_sem=recv_sems.at[k], device_id=(x, y, 1 - c), device_id_type=MESH))
        for cp in cps:
            cp.start()
        for k, chip in enumerate(chips):
            theirs = _half_block(b_ref, chip, 1 - c)
            pltpu.make_async_remote_copy(src_ref=theirs, dst_ref=theirs, send_sem=send_sems.at[k], recv_sem=recv_sems.at[k],
                                         device_id=(x, y, 1 - c), device_id_type=MESH).wait_recv()
        for cp in cps:
            cp.wait_send()

    return _pc(
        body, name=name, in_specs=[HBM], out_specs=HBM,
        out_shape=jax.ShapeDtypeStruct(buf.shape, buf.dtype), input_output_aliases={0: 0},
        scratch_shapes=[pltpu.SemaphoreType.DMA((3,)), pltpu.SemaphoreType.DMA((3,))],
    )(buf)


def _swap_halves(ts):
    nt = len(ts)

    def body(*refs):
        t_refs, l_refs, send_sems, recv_sems = refs[:nt], refs[nt:2 * nt], refs[-2], refs[-1]
        x, y, c, _ = _place()
        cps = []
        for k in range(nt):
            rh = t_refs[k].shape[1] // 2
            cps.append(pltpu.make_async_remote_copy(
                src_ref=t_refs[k].at[:, pl.ds((1 - c) * rh, rh), :], dst_ref=l_refs[k],
                send_sem=send_sems.at[k], recv_sem=recv_sems.at[k], device_id=(x, y, 1 - c), device_id_type=MESH))
        for cp in cps:
            cp.start()
        for cp in cps:
            cp.wait()

    return _pc(
        body, name="grad_swap_halves", in_specs=[HBM] * nt, out_specs=[HBM] * nt,
        out_shape=[jax.ShapeDtypeStruct((t.shape[0], t.shape[1] // 2, t.shape[2]), F32) for t in ts],
        scratch_shapes=[pltpu.SemaphoreType.DMA((nt,)), pltpu.SemaphoreType.DMA((nt,))],
    )(*ts)


def _pair_sum(t, l, place, name):
    _, r, cols = t.shape
    rh = r // 2
    tr = min(rh, 256)
    nr = rh // tr

    def body(pl_ref, t_ref, l_ref, o_ref):
        o_ref[...] = (t_ref[...] + l_ref[...]).astype(BF16)

    other = lambda s, p: s + jnp.where(s >= p[0], 1, 0)
    return _pc(
        body, name=name,
        grid_spec=pltpu.PrefetchScalarGridSpec(
            num_scalar_prefetch=1, grid=(N_CHIPS - 1, nr),
            in_specs=[pl.BlockSpec((None, tr, cols), lambda s, i, p: (other(s, p), p[1] * nr + i, 0)),
                      pl.BlockSpec((None, tr, cols), lambda s, i, p: (other(s, p), i, 0))],
            out_specs=pl.BlockSpec((None, tr, cols), lambda s, i, p: (other(s, p), i, 0))),
        out_shape=jax.ShapeDtypeStruct((N_CHIPS, rh, cols), BF16),
        compiler_params=_params(("parallel", "parallel")),
    )(place, t, l)


def _exchange_chips(ps):
    nt = len(ps)

    def body(*refs):
        p_refs, r_refs, send_sems, recv_sems = refs[:nt], refs[nt:2 * nt], refs[-2], refs[-1]
        x, y, c, chips = _place()
        cps = []
        for t in range(nt):
            for k, chip in enumerate(chips):
                cps.append(pltpu.make_async_remote_copy(
                    src_ref=p_refs[t].at[2 * chip[0] + chip[1]], dst_ref=r_refs[t].at[k],
                    send_sem=send_sems.at[3 * t + k], recv_sem=recv_sems.at[3 * t + k],
                    device_id=(*chip, c), device_id_type=MESH))
        for cp in cps:
            cp.start()
        for cp in cps:
            cp.wait()

    return _pc(
        body, name="grad_exchange_chips", in_specs=[HBM] * nt, out_specs=[HBM] * nt,
        out_shape=[jax.ShapeDtypeStruct((3,) + p.shape[1:], BF16) for p in ps],
        scratch_shapes=[pltpu.SemaphoreType.DMA((3 * nt,)), pltpu.SemaphoreType.DMA((3 * nt,))],
    )(*ps)


def _final_sum(t, l, rcv, place, name, layer=0, layers=1, into=None):
    _, r, cols = t.shape
    rh = r // 2
    tr = min(rh, 256)
    nr = rh // tr

    def body(pl_ref, t_ref, l_ref, r_ref, *rest):
        acc = t_ref[...] + l_ref[...]
        for k in range(3):
            acc = acc + r_ref[k].astype(F32)
        rest[-1][...] = acc

    in_specs = [pl.BlockSpec((None, tr, cols), lambda i, p: (p[0], p[1] * nr + i, 0)),
                pl.BlockSpec((None, tr, cols), lambda i, p: (p[0], i, 0)),
                pl.BlockSpec((3, tr, cols), lambda i, p: (0, i, 0))]
    args = [place, t, l, rcv]
    aliases = {}
    if into is not None:
        in_specs.append(pl.BlockSpec(memory_space=pl.ANY))
        args.append(into)
        aliases = {4: 0}
    return _pc(
        body, name=name,
        grid_spec=pltpu.PrefetchScalarGridSpec(
            num_scalar_prefetch=1, grid=(nr,), in_specs=in_specs,
            out_specs=pl.BlockSpec((tr, cols), lambda i, p: (2 * nr * layer + p[1] * nr + i, 0))),
        out_shape=jax.ShapeDtypeStruct((layers * r, cols), F32), input_output_aliases=aliases,
        compiler_params=_params(("parallel",)),
    )(*args)


def _share_halves(hs, name):
    nt = len(hs)
    n = sum(layers for _, layers in hs)

    def body(*refs):
        h_refs, send_sems, recv_sems = refs[:nt], refs[-2], refs[-1]
        x, y, c, _ = _place()
        cps = []
        for k, (_, layers) in enumerate(hs):
            rh = h_refs[k].shape[0] // (2 * layers)
            for layer in range(layers):
                half = h_refs[k].at[pl.ds((2 * layer + c) * rh, rh), :]
                cps.append(pltpu.make_async_remote_copy(
                    src_ref=half, dst_ref=half, send_sem=send_sems.at[len(cps)], recv_sem=recv_sems.at[len(cps)],
                    device_id=(x, y, 1 - c), device_id_type=MESH))
        for cp in cps:
            cp.start()
        for cp in cps:
            cp.wait()

    return _pc(
        body, name=name, in_specs=[HBM] * nt, out_specs=[HBM] * nt,
        out_shape=[jax.ShapeDtypeStruct(h.shape, F32) for h, _ in hs],
        input_output_aliases={k: k for k in range(nt)},
        scratch_shapes=[pltpu.SemaphoreType.DMA((n,)), pltpu.SemaphoreType.DMA((n,))],
    )(*[h for h, _ in hs])


def _split_start(name, bufs, ride, n, copies_of):
    nb = len(bufs)

    def body(*refs):
        sems = refs[nb + 1:nb + 1 + 2 * n]
        for cp in copies_of(refs[:nb], sems[:n], sems[n:]):
            (cp[0] if isinstance(cp, tuple) else cp).start()

    outs = _pc(
        body, name=name,
        out_shape=(pltpu.SemaphoreType.DMA(()),) * (2 * n) + tuple(pltpu.HBM(b.shape, b.dtype) for b in bufs)
        + (pltpu.HBM(ride.shape, ride.dtype),),
        in_specs=(HBM,) * (nb + 1), out_specs=(SEM,) * (2 * n) + (HBM,) * (nb + 1),
        input_output_aliases={k: 2 * n + k for k in range(nb + 1)},
        compiler_params=pltpu.CompilerParams(has_side_effects=EFFECT),
    )(*[pltpu.with_memory_space_constraint(b, pltpu.HBM) for b in bufs], pltpu.with_memory_space_constraint(ride, pltpu.HBM))
    return list(outs[:2 * n]), list(outs[2 * n:2 * n + nb]), outs[-1]


def _split_wait(name, bufs, sems, after, n, copies_of):
    nb = len(bufs)

    def body(*refs):
        s = refs[nb:nb + 2 * n]
        for cp in copies_of(refs[:nb], s[:n], s[n:]):
            sent, landed = cp if isinstance(cp, tuple) else (cp, cp)
            sent.wait_send()
            landed.wait_recv()

    outs = _pc(
        body, name=name, out_shape=tuple(pltpu.HBM(b.shape, b.dtype) for b in bufs),
        in_specs=(HBM,) * nb + (SEM,) * (2 * n) + (pl.BlockSpec(memory_space=pl.ANY),), out_specs=(HBM,) * nb,
        input_output_aliases={k: k for k in range(nb)},
        compiler_params=pltpu.CompilerParams(has_side_effects=EFFECT),
    )(*bufs, *sems, after)
    return list(outs)


def _handover_copies(refs, send, recv):
    x, y, c, chips = _place()
    cps = []
    for k, chip in enumerate(chips):
        mine, theirs = _half_block(refs[0], chip, c), _half_block(refs[0], chip, 1 - c)
        desc = lambda blk: pltpu.make_async_remote_copy(src_ref=blk, dst_ref=blk, send_sem=send[k], recv_sem=recv[k],
                                                        device_id=(x, y, 1 - c), device_id_type=MESH)
        cps.append((desc(mine), desc(theirs)))
    return cps


def _swap_copies(nt):
    def copies_of(refs, send, recv):
        x, y, c, _ = _place()
        cps = []
        for k in range(nt):
            rh = refs[k].shape[1] // 2
            cps.append(pltpu.make_async_remote_copy(
                src_ref=refs[k].at[:, pl.ds((1 - c) * rh, rh), :], dst_ref=refs[nt + k],
                send_sem=send[k], recv_sem=recv[k], device_id=(x, y, 1 - c), device_id_type=MESH))
        return cps
    return copies_of


def _exchange_copies(nt):
    def copies_of(refs, send, recv):
        x, y, c, chips = _place()
        cps = []
        for t in range(nt):
            for k, chip in enumerate(chips):
                cps.append(pltpu.make_async_remote_copy(
                    src_ref=refs[t].at[2 * chip[0] + chip[1]], dst_ref=refs[nt + t].at[k],
                    send_sem=send[3 * t + k], recv_sem=recv[3 * t + k], device_id=(*chip, c), device_id_type=MESH))
        return cps
    return copies_of


class _StagedReduce:
    def __init__(self, place):
        self.place = place
        self.groups = {}
        self.halves = {}

    @staticmethod
    def slab(t, r):
        return t.reshape(N_CHIPS, r, t.size // (N_CHIPS * r))

    def begin(self, g, grads, ride):
        names = list(grads)
        ts = [self.slab(t, r) for t, r in grads.values()]
        lands = [lax.empty((N_CHIPS, t.shape[1] // 2, t.shape[2]), F32) for t in ts]
        sems, bufs, ride = _split_start(f"grad_swap_start_{g}", ts + lands, ride, len(ts), _swap_copies(len(ts)))
        self.groups[g] = dict(names=names, bufs=bufs, sems=sems)
        return ride

    def advance(self, g, after, ride):
        st = self.groups[g]
        nt = len(st["names"])
        bufs = _split_wait(f"grad_swap_wait_{g}", st["bufs"], st["sems"], after, nt, _swap_copies(nt))
        st["ts"], st["ls"] = bufs[:nt], bufs[nt:]
        ps = [_pair_sum(t, l, self.place, f"pair_sum_{n}") for t, l, n in zip(st["ts"], st["ls"], st["names"])]
        lands = [lax.empty((3,) + p.shape[1:], BF16) for p in ps]
        st["sems"], st["bufs"], ride = _split_start(f"grad_exchange_start_{g}", ps + lands, ride, 3 * nt, _exchange_copies(nt))
        return ride

    def finish(self, g, after):
        st = self.groups[g]
        nt = len(st["names"])
        bufs = _split_wait(f"grad_exchange_wait_{g}", st["bufs"], st["sems"], after, 3 * nt, _exchange_copies(nt))
        for t, l, r, n in zip(st["ts"], st["ls"], bufs[nt:], st["names"]):
            if n[-1] in "01":
                self.halves[n[:-1]] = _final_sum(t, l, r, self.place, f"final_sum_{n}", layer=int(n[-1]), layers=2,
                                                 into=self.halves.get(n[:-1]))
            else:
                self.halves[n] = _final_sum(t, l, r, self.place, f"final_sum_{n}")


def _allgather_small(v):
    rows = v.shape[0]

    def body(v_ref, out_ref, send_sems, recv_sems):
        x, y, c, _ = _place()
        me = 4 * x + 2 * y + c
        out_ref[me] = v_ref[...]
        cps = []
        for k in range(1, 8):
            fx, fy, fc = (k >> 2) & 1, (k >> 1) & 1, k & 1
            to = (1 - x if fx else x, 1 - y if fy else y, 1 - c if fc else c)
            cps.append(pltpu.make_async_remote_copy(
                src_ref=v_ref, dst_ref=out_ref.at[me], send_sem=send_sems.at[k - 1], recv_sem=recv_sems.at[k - 1],
                device_id=to, device_id_type=MESH))
        for cp in cps:
            cp.start()
        for cp in cps:
            cp.wait()

    return _pc(
        body, name="allgather_small",
        in_specs=[pl.BlockSpec(memory_space=pltpu.VMEM)], out_specs=pl.BlockSpec(memory_space=pltpu.VMEM),
        out_shape=jax.ShapeDtypeStruct((8, rows, LANES), F32),
        scratch_shapes=[pltpu.SemaphoreType.DMA((7,)), pltpu.SemaphoreType.DMA((7,))],
    )(v)


def _adamw_math(w, g, m, v):
    m = ADAM_B1 * m + (1.0 - ADAM_B1) * g
    v = ADAM_B2 * v + (1.0 - ADAM_B2) * (g * g)
    m_hat = m / (1.0 - ADAM_B1 ** ADAM_STEP)
    v_hat = v / (1.0 - ADAM_B2 ** ADAM_STEP)
    return -ADAM_LR * (m_hat / (jnp.sqrt(v_hat) + ADAM_EPS) + ADAM_WD * w), m, v


def _adamw(w, g, m, v, name):
    r, cols = w.shape
    tr = min(r, 256)

    def body(w_ref, g_ref, m_ref, v_ref, d_ref, mo_ref, vo_ref):
        d, mn, vn = _adamw_math(w_ref[...], g_ref[...], m_ref[...], v_ref[...])
        d_ref[...] = d
        mo_ref[...] = mn
        vo_ref[...] = vn

    row = pl.BlockSpec((tr, cols), lambda i: (i, 0))
    return _pc(
        body, name=name, grid=(r // tr,), in_specs=[row] * 4, out_specs=[row] * 3,
        out_shape=[jax.ShapeDtypeStruct((r, cols), F32)] * 3,
        compiler_params=_params(("parallel",)),
    )(w, g, m, v)


def _adamw_small(w, gathered, m, v):
    rows = w.shape[0]

    def body(w_ref, g_ref, m_ref, v_ref, go_ref, d_ref, mo_ref, vo_ref):
        g = g_ref[0]
        for k in range(1, 8):
            g = g + g_ref[k]
        d, mn, vn = _adamw_math(w_ref[...], g, m_ref[...], v_ref[...])
        go_ref[...] = g
        d_ref[...] = d
        mo_ref[...] = mn
        vo_ref[...] = vn

    return _pc(
        body, name="adamw_small",
        out_shape=[jax.ShapeDtypeStruct((rows, LANES), F32)] * 4,
    )(w, gathered, m, v)


_BIAS_ROWS = 32


def _own_slot(flat, chip):
    return lax.dynamic_update_slice(lax.empty((N_CHIPS,) + flat.shape, flat.dtype), flat[None], (chip, 0, 0))


def _pack_first(hyb_w_in, hyb_w_out):
    return jnp.concatenate([t.astype(BF16).reshape(-1, 1024) for t in (hyb_w_in, hyb_w_out)], axis=0)


def _unpack_first(g):
    return {"hyb_w_in": g[:, 0:768, :].reshape(N_CHIPS, 1024, 768), "hyb_w_out": g[:, 768:1024, :].reshape(1024, 1024)}


def _pack_rest(mlp_w_up, mlp_w_down, swa_w_qkv, swa_w_out, swa_b_qkv):
    parts = [t.astype(BF16).reshape(-1, 1024) for t in (mlp_w_up, mlp_w_down, swa_w_qkv, swa_w_out)]
    bias = lax.bitcast_convert_type(swa_b_qkv.reshape(384), BF16).reshape(1, 768)
    bias = jnp.pad(bias, ((0, _BIAS_ROWS - 1), (0, 256)))
    return jnp.concatenate(parts + [bias], axis=0)


def _unpack_rest(g):
    W = {
        "packed": g,
        "swa_w_qkv": g[:, 4096:4480, :].reshape(N_CHIPS, 1024, 384),
        "swa_w_out": g[:, 4480:4736, :].reshape(1024, 1024),
    }
    bias = lax.bitcast_convert_type(g[:, 4736, :768].reshape(N_CHIPS, 384, 2), F32).reshape(1536)
    return W, bias


_SMALL = (("norm_mix", 16), ("norm_mlp", 16), ("ret_gn_gain", 4), ("dil_q_gain", 1), ("dil_k_gain", 1),
          ("swa_b_qkv", 12), ("swa_q_gain", 1), ("swa_k_gain", 1), ("swa_sinks", 1), ("loss", 1))
_SUBLANES = 8


def _slot(r):
    return -(-r // _SUBLANES) * _SUBLANES


def _pack_small(d):
    return jnp.concatenate([jnp.pad(d[n].reshape(r, LANES), ((0, _slot(r) - r), (0, 0))) for n, r in _SMALL], axis=0)


def _unpack_small(p):
    out, o = {}, 0
    for n, r in _SMALL:
        out[n] = p[o:o + r]
        o += _slot(r)
    return out


def kernel(x, positions, norm_mix, norm_mlp, mlp_w_up, mlp_w_down, hyb_w_in, hyb_w_out, ret_gn_gain, dil_q_gain, dil_k_gain, swa_w_qkv, swa_b_qkv, swa_w_out, swa_q_gain, swa_k_gain, swa_sinks, loss_target, m_norm_mix, m_norm_mlp, m_mlp_w_up, m_mlp_w_down, m_hyb_w_in, m_hyb_w_out, m_ret_gn_gain, m_dil_q_gain, m_dil_k_gain, m_swa_w_qkv, m_swa_b_qkv, m_swa_w_out, m_swa_q_gain, m_swa_k_gain, m_swa_sinks, v_norm_mix, v_norm_mlp, v_mlp_w_up, v_mlp_w_down, v_hyb_w_in, v_hyb_w_out, v_ret_gn_gain, v_dil_q_gain, v_dil_k_gain, v_swa_w_qkv, v_swa_b_qkv, v_swa_w_out, v_swa_q_gain, v_swa_k_gain, v_swa_sinks):
    ax, ay, ac = lax.axis_index("x"), lax.axis_index("y"), lax.axis_index("c")
    chip = 2 * ax + ay
    place = jnp.stack([chip, ac]).astype(jnp.int32)
    S = x.shape[1]

    first = _own_slot(_pack_first(hyb_w_in[0], hyb_w_out[0]), chip)
    rest = _own_slot(_pack_rest(mlp_w_up, mlp_w_down, swa_w_qkv[0], swa_w_out[0], swa_b_qkv[0]), chip)
    *sems, first, pos_col = _gather_start(first, positions.reshape(S, 1), "allgather_first_start")
    flight = {}

    def first_of(after):
        g = _gather_handover(_gather_wait(first, sems, after, "allgather_first_wait"), "allgather_first_handover")
        *flight["sems"], flight["buf"], g = _gather_start(rest, g, "allgather_rest_start")
        return _unpack_first(g)

    def rest_begin(ride):
        buf = _gather_wait(flight["buf"], flight["sems"], ride, "allgather_rest_wait")
        flight["sems"], flight["bufs"], ride = _split_start("allgather_rest_handover_start", [buf], ride, 3, _handover_copies)
        return ride

    def rest_of(after):
        return _unpack_rest(_split_wait("allgather_rest_handover_wait", flight["bufs"], flight["sems"], after, 3,
                                        _handover_copies)[0])

    P = dict(norm_mix=norm_mix, norm_mlp=norm_mlp, ret_gn_gain=ret_gn_gain, dil_q_gain=dil_q_gain, dil_k_gain=dil_k_gain,
             swa_q_gain=swa_q_gain, swa_k_gain=swa_k_gain, swa_sinks=swa_sinks)

    red = _StagedReduce(place)
    loss_l, grad_x, gp = _local_step(x[0], pos_col, loss_target[0], first_of, rest_begin, rest_of, P, red)

    params = dict(mlp_w_up=(mlp_w_up, m_mlp_w_up, v_mlp_w_up), mlp_w_down=(mlp_w_down, m_mlp_w_down, v_mlp_w_down),
                  hyb_w_in=(hyb_w_in, m_hyb_w_in, v_hyb_w_in), hyb_w_out=(hyb_w_out, m_hyb_w_out, v_hyb_w_out),
                  swa_w_qkv=(swa_w_qkv, m_swa_w_qkv, v_swa_w_qkv), swa_w_out=(swa_w_out, m_swa_w_out, v_swa_w_out))
    big = {}

    def update(names, share_name):
        hs = [(red.halves[n], params[n][0].shape[0]) for n in names]
        for n, g in zip(names, _share_halves(hs, share_name)):
            rows = g.shape[0]
            w, m, v = (t.reshape(rows, -1) for t in params[n])
            big[n] = [t.reshape(params[n][0].shape) for t in (g,) + tuple(_adamw(w, g, m, v, f"adamw_{n}"))]

    names = ["mlp_w_up", "mlp_w_down", "hyb_w_out", "swa_w_qkv", "swa_w_out"]
    red.halves[names[0]] = red.advance("win", grad_x, red.halves[names[0]])
    update(names, "grad_share_halves")
    red.finish("win", big[names[-1]][1])
    update(["hyb_w_in"], "grad_share_last")

    gsm = dict(gp, loss=loss_l)
    gsm["swa_sinks"] = jnp.pad(gp["swa_sinks"].reshape(16, HEAD)[:, 0], (0, LANES - 16))
    gathered = _allgather_small(_pack_small(gsm))

    def small_pack(norm_mix, norm_mlp, gn, dq, dk, b, sq, sk, sinks):
        dup = lambda t: jnp.tile(t.reshape(1, HEAD), (1, 2))
        bias = lax.dynamic_update_slice(jnp.zeros((12, LANES), F32), b.reshape(3, LANES), (3 * chip, 0))
        return _pack_small(dict(norm_mix=norm_mix, norm_mlp=norm_mlp, ret_gn_gain=gn, dil_q_gain=dup(dq), dil_k_gain=dup(dk),
                                swa_b_qkv=bias, swa_q_gain=dup(sq), swa_k_gain=dup(sk),
                                swa_sinks=jnp.pad(sinks.reshape(16), (0, LANES - 16)), loss=jnp.zeros((1, LANES), F32)))

    pw = small_pack(norm_mix, norm_mlp, ret_gn_gain, dil_q_gain, dil_k_gain, swa_b_qkv, swa_q_gain, swa_k_gain, swa_sinks)
    pm = small_pack(m_norm_mix, m_norm_mlp, m_ret_gn_gain, m_dil_q_gain, m_dil_k_gain, m_swa_b_qkv, m_swa_q_gain, m_swa_k_gain, m_swa_sinks)
    pv = small_pack(v_norm_mix, v_norm_mlp, v_ret_gn_gain, v_dil_q_gain, v_dil_k_gain, v_swa_b_qkv, v_swa_q_gain, v_swa_k_gain, v_swa_sinks)
    small = [_unpack_small(t) for t in _adamw_small(pw, gathered, pm, pv)]

    def small_out(n, k):
        t = small[k][n]
        if n in ("norm_mix", "norm_mlp"):
            return t.reshape(2, D_MODEL)
        if n == "ret_gn_gain":
            return t.reshape(1, RET_HEADS, 128)
        if n == "swa_b_qkv":
            return lax.dynamic_slice(t, (3 * chip, 0), (3, LANES)).reshape(1, 384)
        if n == "swa_sinks":
            return t[0, :16].reshape(1, 16)
        return t[0, :HEAD].reshape(1, HEAD)

    order = ["norm_mix", "norm_mlp", "mlp_w_up", "mlp_w_down", "hyb_w_in", "hyb_w_out", "ret_gn_gain", "dil_q_gain",
             "dil_k_gain", "swa_w_qkv", "swa_b_qkv", "swa_w_out", "swa_q_gain", "swa_k_gain", "swa_sinks"]
    is_big = {"mlp_w_up", "mlp_w_down", "hyb_w_in", "hyb_w_out", "swa_w_qkv", "swa_w_out"}
    outs = [small[0]["loss"][0, 0], grad_x[None]]
    for k in range(4):
        outs += [big[n][k] if n in is_big else small_out(n, k) for n in order]
    return tuple(outs)
```

```python
import functools
import math

import numpy as np
import jax
import jax.numpy as jnp
from jax import lax
from jax.experimental import pallas as pl
from jax.experimental.pallas import tpu as pltpu

F32, BF16 = jnp.float32, jnp.bfloat16
HIGHEST = lax.Precision.HIGHEST
MESH = pl.DeviceIdType.MESH

LANES = 128
VMEM_LIMIT = 48 << 20
D_MODEL = 1024
D_FF = 4096
HEAD = 64
EPS = 1e-6
BLK = 128
RET_HEADS = 4
RET_THETA = 10000.0
ROPE_THETA = 500000.0
ROPE_DIMS = 16
DIL_PATTERNS = ((128, 1), (512, 4), (2048, 16))
SWA_DIST = 127
N_CHIPS = 4
ADAM_LR, ADAM_B1, ADAM_B2, ADAM_EPS, ADAM_WD, ADAM_STEP = 0.001, 0.9, 0.999, 1e-08, 0.01, 10

_LOG_GAMMA = [float(np.log1p(-np.exp2(np.float32(-5.0 - h)))) for h in range(RET_HEADS)]


def _pc(body, **kw):
    return pl.pallas_call(body, **kw)


def _params(sem):
    return pltpu.CompilerParams(dimension_semantics=sem, vmem_limit_bytes=VMEM_LIMIT)


def _matmul(a, b, *, dims, tm, tn, tk, outs, name, epilogue=None, extras=(), b_cs=False, b_rs=0, b_row0=0, b_rows=0,
            o_cs=0, a_pro=None):
    if dims == "nn":
        M, K = a.shape
        N = b.shape[0] * b.shape[2] if b_cs else b.shape[1]
        a_spec = pl.BlockSpec((tm, tk), lambda i, j, k: (i, k))
        if b_cs:
            npt = b.shape[2] // tn
            b_spec = pl.BlockSpec((None, tk, tn), lambda i, j, k: (j // npt, k + b_row0, j % npt))
        elif b_rs:
            K, N, kps = b.shape[0] * b_rs, b.shape[2], b_rs // tk
            b_spec = pl.BlockSpec((None, tk, tn), lambda i, j, k: (k // kps, b_row0 + k % kps, j))
        else:
            b_spec = pl.BlockSpec((tk, tn), lambda i, j, k: (k, j))
        contract = (((1,), (0,)), ((), ()))
    elif dims == "nt":
        M, K = a.shape
        N = (b_rows or b.shape[1]) if b_cs else b.shape[0]
        a_spec = pl.BlockSpec((tm, tk), lambda i, j, k: (i, k))
        if b_cs:
            kpt = b.shape[2] // tk
            b_spec = pl.BlockSpec((None, tn, tk), lambda i, j, k: (k // kpt, j + b_row0, k % kpt))
        elif b_rs:
            N, jps = b.shape[0] * b_rs, b_rs // tn
            b_spec = pl.BlockSpec((None, tn, tk), lambda i, j, k: (j // jps, b_row0 + j % jps, k))
        else:
            b_spec = pl.BlockSpec((tn, tk), lambda i, j, k: (j, k))
        contract = (((1,), (1,)), ((), ()))
    else:
        K, M = a.shape
        N = b.shape[1]
        a_spec = pl.BlockSpec((tk, tm), lambda i, j, k: (k, i))
        b_spec = pl.BlockSpec((tk, tn), lambda i, j, k: (k, j))
        contract = (((0,), (0,)), ((), ()))
    assert M % tm == 0 and N % tn == 0 and K % tk == 0, (name, M, N, K, tm, tn, tk)
    nk = K // tk
    ex_specs = []
    for arr, kind in extras:
        if kind == "mn":
            ex_specs.append(pl.BlockSpec((tm, tn), lambda i, j, k: (i, j)))
        elif kind == "n":
            ex_specs.append(pl.BlockSpec((1, tn), lambda i, j, k: (0, j)))
        elif kind == "full":
            ex_specs.append(pl.BlockSpec(arr.shape, lambda i, j, k, nd=arr.ndim: (0,) * nd))
        else:
            ex_specs.append(pl.BlockSpec((tm, kind), lambda i, j, k: (i, 0)))
    if o_cs:
        n_sh = N // o_cs
        opt = n_sh // tn
        o_shape = (o_cs, M, n_sh)
        o_spec = pl.BlockSpec((None, tm, tn), lambda i, j, k: (j // opt, i, j % opt))
    else:
        o_shape = (M, N)
        o_spec = pl.BlockSpec((tm, tn), lambda i, j, k: (i, j))
    o_specs, o_shapes, summed = [], [], []
    for o in outs:
        if isinstance(o, tuple) and o[0] == "colsum":
            assert N == tn
            o_specs.append(pl.BlockSpec((1, tn), lambda i, j, k: (0, j)))
            o_shapes.append(jax.ShapeDtypeStruct((1, N), F32))
            summed.append(True)
        elif isinstance(o, tuple):
            o_specs.append(pl.BlockSpec((tm, o[1]), lambda i, j, k: (i, 0)))
            o_shapes.append(jax.ShapeDtypeStruct((M, o[1]), o[0]))
            summed.append(False)
        else:
            o_specs.append(o_spec)
            o_shapes.append(jax.ShapeDtypeStruct(o_shape, o))
            summed.append(False)
    n_ex, n_out = len(extras), len(outs)
    if epilogue is None:
        epilogue = lambda acc: (acc,)

    def body(a_ref, b_ref, *rest):
        ex, o_refs, acc = rest[:n_ex], rest[n_ex:n_ex + n_out], rest[-1]
        i, k = pl.program_id(0), pl.program_id(2)

        @pl.when(k == 0)
        def _():
            acc[...] = jnp.zeros_like(acc)

        av = a_ref[...] if a_pro is None else a_pro(a_ref[...])
        acc[...] += lax.dot_general(av.astype(BF16), b_ref[...].astype(BF16), contract, preferred_element_type=F32)

        @pl.when(k == nk - 1)
        def _():
            vals = epilogue(acc[...], *[e[...] for e in ex])
            for r, v, sm in zip(o_refs, vals, summed):
                if sm:
                    @pl.when(i == 0)
                    def _(r=r):
                        r[...] = jnp.zeros_like(r)

                    r[...] += v
                else:
                    r[...] = v.astype(r.dtype)

    res = _pc(
        body, name=name, grid=(M // tm, N // tn, nk),
        in_specs=[a_spec, b_spec] + ex_specs, out_specs=o_specs, out_shape=o_shapes,
        scratch_shapes=[pltpu.VMEM((tm, tn), F32)],
        compiler_params=_params(("arbitrary" if any(summed) else "parallel", "parallel", "arbitrary")),
    )(a, b, *[e for e, _ in extras])
    return res[0] if n_out == 1 else res


def _roll(x, s):
    return pltpu.roll(x, s % LANES, 1)


def _rope(x, A, B, C, half):
    return x * A + _roll(x, LANES - half) * B + _roll(x, half) * C


def _rope_t(g, A, B, C, half):
    return g * A + _roll(g * B, half) + _roll(g * C, LANES - half)


def _gmean(x, G):
    hi = x.astype(BF16)
    lo = (x - hi.astype(F32)).astype(BF16)
    Gb = G.astype(BF16)
    return jnp.dot(hi, Gb, preferred_element_type=F32) + jnp.dot(lo, Gb, preferred_element_type=F32)


def _head_mask(shape, half):
    lane = lax.broadcasted_iota(jnp.int32, shape, len(shape) - 1)
    return (lane >= HEAD) if half else (lane < HEAD)


def _group_matrix():
    i = np.arange(LANES)
    return jnp.asarray((i[:, None] // HEAD == i[None, :] // HEAD).astype(np.float32) / HEAD)


def _rope_inv():
    l = np.arange(LANES) % HEAD
    inv_r = np.power(np.float32(RET_THETA), -(l % 32).astype(np.float32) * np.float32(2.0 / HEAD))
    hp = ROPE_DIMS // 2
    inv_p = np.power(np.float32(ROPE_THETA), -(l % hp).astype(np.float32) * np.float32(2.0 / ROPE_DIMS))
    inv_p = np.where(l < ROPE_DIMS, inv_p, 0.0)
    return jnp.asarray(np.stack([inv_r, inv_p]).astype(np.float32))


def _tables(pos_col):
    S = pos_col.shape[0]
    tm = 512
    hp = ROPE_DIMS // 2

    def body(p_ref, inv_ref, o_ref):
        p = p_ref[...].astype(F32)
        lane = lax.broadcasted_iota(jnp.int32, (tm, LANES), 1) % HEAD
        ang = p * inv_ref[0:1, :]
        c, s = jnp.cos(ang), jnp.sin(ang)
        o_ref[:, 0:128] = c
        o_ref[:, 128:256] = jnp.where(lane < 32, -s, 0.0)
        o_ref[:, 256:384] = jnp.where(lane >= 32, s, 0.0)
        ang = p * inv_ref[1:2, :]
        c, s = jnp.cos(ang), jnp.sin(ang)
        o_ref[:, 384:512] = c
        o_ref[:, 512:640] = jnp.where(lane < hp, -s, 0.0)
        o_ref[:, 640:768] = jnp.where((lane >= hp) & (lane < ROPE_DIMS), s, 0.0)

    return _pc(
        body, name="rope_tables", grid=(S // tm,),
        in_specs=[pl.BlockSpec((tm, 1), lambda i: (i, 0)), pl.BlockSpec((2, LANES), lambda i: (0, 0))],
        out_specs=pl.BlockSpec((tm, 768), lambda i: (i, 0)),
        out_shape=jax.ShapeDtypeStruct((S, 768), F32),
        compiler_params=_params(("parallel",)),
    )(pos_col, _rope_inv())


def _tab(tab_ref, which):
    o = 384 * which
    return tab_ref[:, o:o + 128], tab_ref[:, o + 128:o + 256], tab_ref[:, o + 256:o + 384]


def _rms_fwd(x, g, name):
    S, Dm = x.shape
    tm = 512

    def body(x_ref, g_ref, h_ref):
        xv = x_ref[...]
        r = lax.rsqrt(jnp.mean(xv * xv, axis=-1, keepdims=True) + EPS)
        h_ref[...] = (xv * r * g_ref[...]).astype(BF16)

    return _pc(
        body, name=name, grid=(S // tm,),
        in_specs=[pl.BlockSpec((tm, Dm), lambda i: (i, 0)), pl.BlockSpec((1, Dm), lambda i: (0, 0))],
        out_specs=pl.BlockSpec((tm, Dm), lambda i: (i, 0)),
        out_shape=jax.ShapeDtypeStruct((S, Dm), BF16),
        compiler_params=_params(("parallel",)),
    )(x, g.reshape(1, Dm))


def _rms_bwd(x, g, dh, dres, name):
    S, Dm = x.shape
    tm = 512

    def body(x_ref, g_ref, dh_ref, dres_ref, dx_ref, dxb_ref, dg_ref):
        xv, dhv = x_ref[...], dh_ref[...]
        r = lax.rsqrt(jnp.mean(xv * xv, axis=-1, keepdims=True) + EPS)
        t = dhv * g_ref[...]
        dx = dres_ref[...] + r * t - xv * (r * r * r) * jnp.mean(xv * t, axis=-1, keepdims=True)
        dx_ref[...] = dx
        dxb_ref[...] = dx.astype(BF16)

        @pl.when(pl.program_id(0) == 0)
        def _():
            dg_ref[...] = jnp.zeros_like(dg_ref)

        dg_ref[...] += jnp.sum(dhv * xv * r, axis=0, keepdims=True)

    row = pl.BlockSpec((tm, Dm), lambda i: (i, 0))
    vec = pl.BlockSpec((1, Dm), lambda i: (0, 0))
    return _pc(
        body, name=name, grid=(S // tm,),
        in_specs=[row, vec, row, row], out_specs=[row, row, vec],
        out_shape=[jax.ShapeDtypeStruct((S, Dm), F32), jax.ShapeDtypeStruct((S, Dm), BF16),
                   jax.ShapeDtypeStruct((1, Dm), F32)],
        compiler_params=_params(("arbitrary",)),
    )(x, g.reshape(1, Dm), dh, dres)


def _hn_fwd(x, gain, G):
    r = lax.rsqrt(_gmean(x * x, G) + EPS)
    return x * r * gain


def _hn_bwd(x, gain, dy, G):
    r = lax.rsqrt(_gmean(x * x, G) + EPS)
    t = dy * gain
    dx = r * t - x * (r * r * r) * _gmean(x * t, G)
    return dx, jnp.sum(dy * x * r, axis=0, keepdims=True)


def _fold_halves(v):
    return v + _roll(v, HEAD)


def _even_pre_fwd(proj, tab, qg, kg):
    S = proj.shape[0]
    tm = 256

    def body(p_ref, tab_ref, qg_ref, kg_ref, g_ref, rq_ref, rk_ref, rv_ref, dq_ref, dk_ref, dv_ref):
        Ar, Br, Cr = _tab(tab_ref, 0)
        Ap, Bp, Cp = _tab(tab_ref, 1)
        G = g_ref[...]
        for c in range(2):
            sl = slice(c * 128, (c + 1) * 128)
            rq_ref[:, sl] = _rope(p_ref[:, c * 128:(c + 1) * 128], Ar, Br, Cr, 32).astype(BF16)
            rk_ref[:, sl] = (_rope(p_ref[:, 256 + c * 128:256 + (c + 1) * 128], Ar, Br, Cr, 32) * 0.125).astype(BF16)
        rv_ref[...] = p_ref[:, 512:1024].astype(BF16)
        for c in range(4):
            sl = slice(c * 128, (c + 1) * 128)
            q = _hn_fwd(p_ref[:, 1536 + c * 128:1536 + (c + 1) * 128], qg_ref[...], G)
            dq_ref[:, sl] = _rope(q, Ap, Bp, Cp, 8).astype(BF16)
            k = _hn_fwd(p_ref[:, 2048 + c * 128:2048 + (c + 1) * 128], kg_ref[...], G)
            dk_ref[:, sl] = _rope(k, Ap, Bp, Cp, 8).astype(BF16)
        dv_ref[...] = p_ref[:, 2560:3072].astype(BF16)

    row = lambda w: pl.BlockSpec((tm, w), lambda i: (i, 0))
    vec = pl.BlockSpec((1, LANES), lambda i: (0, 0))
    return _pc(
        body, name="even_pre_fwd", grid=(S // tm,),
        in_specs=[row(3072), row(768), vec, vec, pl.BlockSpec((LANES, LANES), lambda i: (0, 0))],
        out_specs=[row(256), row(256), row(512), row(512), row(512), row(512)],
        out_shape=[jax.ShapeDtypeStruct((S, w), BF16) for w in (256, 256, 512, 512, 512, 512)],
        compiler_params=_params(("parallel",)),
    )(proj, tab, qg, kg, _group_matrix())


def _even_pre_bwd(proj, tab, qg, kg, drq, drk, drv, drg, dqs, dks, dvs):
    S = proj.shape[0]
    tm = 256
    npat = len(dqs)

    def body(p_ref, tab_ref, qg_ref, kg_ref, g_ref, drq_ref, drk_ref, drv_ref, drg_ref, *rest):
        dq_refs, dk_refs, dv_refs = rest[:npat], rest[npat:2 * npat], rest[2 * npat:3 * npat]
        dp_ref, dqg_ref, dkg_ref = rest[3 * npat:]
        Ar, Br, Cr = _tab(tab_ref, 0)
        Ap, Bp, Cp = _tab(tab_ref, 1)
        G = g_ref[...]
        for c in range(2):
            sl = slice(c * 128, (c + 1) * 128)
            dp_ref[:, c * 128:(c + 1) * 128] = _rope_t(drq_ref[:, sl], Ar, Br, Cr, 32).astype(BF16)
            dp_ref[:, 256 + c * 128:256 + (c + 1) * 128] = _rope_t(drk_ref[:, sl] * 0.125, Ar, Br, Cr, 32).astype(BF16)
        dp_ref[:, 512:1024] = drv_ref[...].astype(BF16)
        dp_ref[:, 1024:1536] = drg_ref[...].astype(BF16)
        accq = jnp.zeros((1, LANES), F32)
        acck = jnp.zeros((1, LANES), F32)
        for c in range(4):
            sl = slice(c * 128, (c + 1) * 128)
            g = dq_refs[0][:, sl]
            for r in dq_refs[1:]:
                g = g + r[:, sl]
            dx, dg = _hn_bwd(p_ref[:, 1536 + c * 128:1536 + (c + 1) * 128], qg_ref[...], _rope_t(g, Ap, Bp, Cp, 8), G)
            dp_ref[:, 1536 + c * 128:1536 + (c + 1) * 128] = dx.astype(BF16)
            accq = accq + dg
            g = dk_refs[0][:, sl]
            for r in dk_refs[1:]:
                g = g + r[:, sl]
            dx, dg = _hn_bwd(p_ref[:, 2048 + c * 128:2048 + (c + 1) * 128], kg_ref[...], _rope_t(g, Ap, Bp, Cp, 8), G)
            dp_ref[:, 2048 + c * 128:2048 + (c + 1) * 128] = dx.astype(BF16)
            acck = acck + dg
        g = dv_refs[0][...]
        for r in dv_refs[1:]:
            g = g + r[...]
        dp_ref[:, 2560:3072] = g.astype(BF16)

        @pl.when(pl.program_id(0) == 0)
        def _():
            dqg_ref[...] = jnp.zeros_like(dqg_ref)
            dkg_ref[...] = jnp.zeros_like(dkg_ref)

        dqg_ref[...] += _fold_halves(accq)
        dkg_ref[...] += _fold_halves(acck)

    row = lambda w: pl.BlockSpec((tm, w), lambda i: (i, 0))
    vec = pl.BlockSpec((1, LANES), lambda i: (0, 0))
    return _pc(
        body, name="even_pre_bwd", grid=(S // tm,),
        in_specs=[row(3072), row(768), vec, vec, pl.BlockSpec((LANES, LANES), lambda i: (0, 0)),
                  row(256), row(256), row(512), row(512)] + [row(512)] * (3 * npat),
        out_specs=[row(3072), vec, vec],
        out_shape=[jax.ShapeDtypeStruct((S, 3072), BF16), jax.ShapeDtypeStruct((1, LANES), F32),
                   jax.ShapeDtypeStruct((1, LANES), F32)],
        compiler_params=_params(("arbitrary",)),
    )(proj, tab, qg, kg, _group_matrix(), drq, drk, drv, drg, *dqs, *dks, *dvs)


def _ret_consts(pair, half):
    lg = jnp.where(pair == 0, _LOG_GAMMA[half], _LOG_GAMMA[2 + half]).astype(F32)
    i = lax.broadcasted_iota(jnp.int32, (BLK, BLK), 0)
    j = lax.broadcasted_iota(jnp.int32, (BLK, BLK), 1)
    diff = (i - j).astype(F32)
    decay = jnp.where(diff >= 0, jnp.exp(lg * jnp.maximum(diff, 0.0)), 0.0)
    t = lax.broadcasted_iota(jnp.int32, (BLK, 1), 0).astype(F32)
    xi = jnp.exp(lg * (t + 1.0))
    zeta = jnp.exp(lg * (BLK - 1.0 - t))
    cd = jnp.exp(jnp.full((1, 1), BLK, F32) * lg)
    return decay, xi, zeta, cd


RET_STEP = 8


def _ret_fwd(rq, rk, rv):
    S = rq.shape[0]
    nc = S // BLK
    rows = RET_STEP * BLK

    def body(q_ref, k_ref, v_ref, o_ref, st_ref, R):
        p, n = pl.program_id(0), pl.program_id(1)

        @pl.when(n == 0)
        def _():
            R[...] = jnp.zeros_like(R)

        consts = [_ret_consts(p, half) for half in range(2)]
        masks = [_head_mask((BLK, LANES), half) for half in range(2)]
        for ci in range(RET_STEP):
            rs = slice(ci * BLK, (ci + 1) * BLK)
            q2, k2 = q_ref[rs, :], k_ref[rs, :]
            for half in range(2):
                decay, xi, zeta, cd = consts[half]
                m = masks[half]
                qm = jnp.where(m, q2, jnp.zeros_like(q2))
                km = jnp.where(m, k2, jnp.zeros_like(k2))
                v = v_ref[rs, half * 128:(half + 1) * 128]
                Rb = R[half].astype(BF16)
                st_ref[ci, half] = Rb
                sc = lax.dot_general(qm, k2, (((1,), (1,)), ((), ())), preferred_element_type=F32) * decay
                o = jnp.dot(sc.astype(BF16), v, preferred_element_type=F32)
                o = o + jnp.dot(qm, Rb, preferred_element_type=F32) * xi
                o_ref[rs, half * 128:(half + 1) * 128] = o
                kz = (km.astype(F32) * zeta).astype(BF16)
                R[half] = R[half] * cd + lax.dot_general(kz, v, (((0,), (0,)), ((), ())), preferred_element_type=F32)

    return _pc(
        body, name="ret_fwd", grid=(2, nc // RET_STEP),
        in_specs=[pl.BlockSpec((rows, 128), lambda p, n: (n, p)), pl.BlockSpec((rows, 128), lambda p, n: (n, p)),
                  pl.BlockSpec((rows, 256), lambda p, n: (n, p))],
        out_specs=[pl.BlockSpec((rows, 256), lambda p, n: (n, p)),
                   pl.BlockSpec((None, RET_STEP, 2, 128, 128), lambda p, n: (p, n, 0, 0, 0))],
        out_shape=[jax.ShapeDtypeStruct((S, 512), F32), jax.ShapeDtypeStruct((2, nc, 2, 128, 128), BF16)],
        scratch_shapes=[pltpu.VMEM((2, 128, 128), F32)],
        compiler_params=_params(("parallel", "arbitrary")),
    )(rq, rk, rv)


def _ret_bwd(rq, rk, rv, states, do):
    S = rq.shape[0]
    nc = S // BLK
    ns = nc // RET_STEP
    rows = RET_STEP * BLK
    nt = (((1,), (1,)), ((), ()))
    tn = (((0,), (0,)), ((), ()))

    def body(q_ref, k_ref, v_ref, st_ref, do_ref, dq_ref, dk_ref, dv_ref, U):
        p, n = pl.program_id(0), pl.program_id(1)

        @pl.when(n == 0)
        def _():
            U[...] = jnp.zeros_like(U)

        consts = [_ret_consts(p, half) for half in range(2)]
        masks = [_head_mask((BLK, LANES), half) for half in range(2)]
        for ci in reversed(range(RET_STEP)):
            rs = slice(ci * BLK, (ci + 1) * BLK)
            q2, k2 = q_ref[rs, :], k_ref[rs, :]
            dq_acc = jnp.zeros((BLK, LANES), F32)
            dk_acc = jnp.zeros((BLK, LANES), F32)
            for half in range(2):
                decay, xi, zeta, cd = consts[half]
                m = masks[half]
                qm = jnp.where(m, q2, jnp.zeros_like(q2))
                km = jnp.where(m, k2, jnp.zeros_like(k2))
                v = v_ref[rs, half * 128:(half + 1) * 128]
                dob = do_ref[rs, half * 128:(half + 1) * 128].astype(BF16)
                Rb = st_ref[ci, half]
                Ub = U[half].astype(BF16)
                dsc = (lax.dot_general(dob, v, nt, preferred_element_type=F32) * decay).astype(BF16)
                xdo = (dob.astype(F32) * xi).astype(BF16)
                dq_acc += jnp.dot(dsc, km, preferred_element_type=F32) + lax.dot_general(xdo, Rb, nt, preferred_element_type=F32)
                dk_acc += lax.dot_general(dsc, qm, tn, preferred_element_type=F32) \
                    + lax.dot_general(v, Ub, nt, preferred_element_type=F32) * zeta
                sc = (lax.dot_general(qm, k2, nt, preferred_element_type=F32) * decay).astype(BF16)
                kz = (km.astype(F32) * zeta).astype(BF16)
                dv_ref[rs, half * 128:(half + 1) * 128] = lax.dot_general(sc, dob, tn, preferred_element_type=F32) \
                    + jnp.dot(kz, Ub, preferred_element_type=F32)
                U[half] = U[half] * cd + lax.dot_general(qm, xdo, tn, preferred_element_type=F32)
            dq_ref[rs, :] = dq_acc
            dk_ref[rs, :] = dk_acc

    rev = lambda w: pl.BlockSpec((rows, w), lambda p, n: (ns - 1 - n, p))
    return _pc(
        body, name="ret_bwd", grid=(2, ns),
        in_specs=[rev(128), rev(128), rev(256),
                  pl.BlockSpec((None, RET_STEP, 2, 128, 128), lambda p, n: (p, ns - 1 - n, 0, 0, 0)), rev(256)],
        out_specs=[rev(128), rev(128), rev(256)],
        out_shape=[jax.ShapeDtypeStruct((S, 256), F32), jax.ShapeDtypeStruct((S, 256), F32),
                   jax.ShapeDtypeStruct((S, 512), F32)],
        scratch_shapes=[pltpu.VMEM((2, 128, 128), F32)],
        compiler_params=_params(("parallel", "arbitrary")),
    )(rq, rk, rv, states, do)


def _col_of(b, m):
    return jnp.max(jnp.where(m, b, -jnp.inf), axis=1, keepdims=True)


def _attn_fwd(q, k, v, *, nq, max_dist, name, sinks=None, want_bf16=False):
    L, Ck = k.shape
    nb, ncol = L // BLK, Ck // LANES
    scale = HEAD ** -0.5
    has_sink = sinks is not None

    def body(*refs):
        q_ref, kp_ref, kc_ref, vp_ref, vc_ref = refs[:5]
        sk_ref = refs[5] if has_sink else None
        outs = refs[5 + has_sink:]
        n = pl.program_id(1)
        kcat = jnp.concatenate([kp_ref[...], kc_ref[...]], axis=0)
        vcat = jnp.concatenate([vp_ref[...], vc_ref[...]], axis=0)
        r = lax.broadcasted_iota(jnp.int32, (BLK, 2 * BLK), 0)
        c = lax.broadcasted_iota(jnp.int32, (BLK, 2 * BLK), 1)
        dist = r + BLK - c
        valid = (dist >= 0) & (dist <= max_dist) & ((c >= BLK) | (n > 0))
        for i in range(nq):
            q2 = q_ref[:, i * 128:(i + 1) * 128]
            o2 = jnp.zeros((BLK, LANES), F32)
            l2 = jnp.zeros((BLK, LANES), F32)
            for half in range(2):
                m = _head_mask((BLK, LANES), half)
                qm = jnp.where(m, q2, jnp.zeros_like(q2))
                s = lax.dot_general(qm, kcat, (((1,), (1,)), ((), ())), preferred_element_type=F32) * scale
                s = jnp.where(valid, s, -jnp.inf)
                mx = jnp.max(s, axis=1, keepdims=True)
                if has_sink:
                    snk = _col_of(sk_ref[:, i * 128:(i + 1) * 128], _head_mask((1, LANES), half))
                    mx = jnp.maximum(mx, snk)
                pr = jnp.exp(s - mx)
                den = jnp.sum(pr, axis=1, keepdims=True)
                if has_sink:
                    den = den + jnp.exp(snk - mx)
                pv = jnp.dot(pr.astype(BF16), vcat, preferred_element_type=F32) / den
                o2 = jnp.where(m, pv, o2)
                l2 = jnp.where(m, mx + jnp.log(den), l2)
            outs[0][:, i * 128:(i + 1) * 128] = o2
            outs[1][:, i * 128:(i + 1) * 128] = l2
            if want_bf16:
                outs[2][:, i * 128:(i + 1) * 128] = o2.astype(BF16)

    qspec = pl.BlockSpec((BLK, nq * 128), lambda j, n: (n, j))
    cur = pl.BlockSpec((BLK, 128), lambda j, n: (n, j))
    prev = pl.BlockSpec((BLK, 128), lambda j, n: (jnp.maximum(n - 1, 0), j))
    in_specs = [qspec, prev, cur, prev, cur]
    args = [q, k, k, v, v]
    if has_sink:
        in_specs.append(pl.BlockSpec((1, nq * 128), lambda j, n: (0, j)))
        args.append(sinks)
    out_dts = [F32, F32] + ([BF16] if want_bf16 else [])
    return _pc(
        body, name=name, grid=(ncol, nb), in_specs=in_specs,
        out_specs=[qspec] * len(out_dts),
        out_shape=[jax.ShapeDtypeStruct(q.shape, dt) for dt in out_dts],
        compiler_params=_params(("parallel", "parallel")),
    )(*args)


def _attn_bwd(q, k, v, o, lse, do, *, nq, max_dist, name, sinks=None):
    L, Ck = k.shape
    nb, ncol = L // BLK, Ck // LANES
    scale = HEAD ** -0.5
    has_sink = sinks is not None
    nt = (((1,), (1,)), ((), ()))
    tn = (((0,), (0,)), ((), ()))

    def body(*refs):
        (qc_ref, qn_ref, kp_ref, kc_ref, vp_ref, vc_ref, oc_ref, on_ref, lc_ref, ln_ref, dc_ref, dn_ref) = refs[:12]
        sk_ref = refs[12] if has_sink else None
        outs = refs[12 + has_sink:]
        dq_ref, dk_ref, dv_ref = outs[:3]
        n = pl.program_id(1)
        kc, vc = kc_ref[...], vc_ref[...]
        kcat = jnp.concatenate([kp_ref[...], kc], axis=0)
        vcat = jnp.concatenate([vp_ref[...], vc], axis=0)
        r = lax.broadcasted_iota(jnp.int32, (BLK, 2 * BLK), 0)
        c = lax.broadcasted_iota(jnp.int32, (BLK, 2 * BLK), 1)
        dist = r + BLK - c
        valid_q = (dist >= 0) & (dist <= max_dist) & ((c >= BLK) | (n > 0))
        r2 = lax.broadcasted_iota(jnp.int32, (2 * BLK, BLK), 0)
        c2 = lax.broadcasted_iota(jnp.int32, (2 * BLK, BLK), 1)
        dist2 = r2 - c2
        valid_k = (dist2 >= 0) & (dist2 <= max_dist) & ((r2 < BLK) | (n < nb - 1))
        dk_acc = jnp.zeros((BLK, LANES), F32)
        dv_acc = jnp.zeros((BLK, LANES), F32)
        for i in range(nq):
            sl = slice(i * 128, (i + 1) * 128)
            qcur, docur = qc_ref[:, sl], dc_ref[:, sl]
            qcat = jnp.concatenate([qcur, qn_ref[:, sl]], axis=0)
            docat = jnp.concatenate([docur, dn_ref[:, sl]], axis=0)
            ocat = jnp.concatenate([oc_ref[:, sl], on_ref[:, sl]], axis=0)
            lcat = jnp.concatenate([lc_ref[:, sl], ln_ref[:, sl]], axis=0)
            dq2 = jnp.zeros((BLK, LANES), F32)
            ds2 = jnp.zeros((1, LANES), F32)
            for half in range(2):
                m1 = _head_mask((BLK, LANES), half)
                m2 = _head_mask((2 * BLK, LANES), half)
                dom = jnp.where(m2, docat, 0.0)
                delta = jnp.sum(dom * ocat, axis=1, keepdims=True)
                lcol = _col_of(lcat, m2)
                domb = dom.astype(BF16)
                qmcat = jnp.where(m2, qcat, jnp.zeros_like(qcat))
                qm = qmcat[:BLK]
                s = lax.dot_general(qm, kcat, nt, preferred_element_type=F32) * scale
                pr = jnp.where(valid_q, jnp.exp(s - lcol[:BLK]), 0.0)
                dp = lax.dot_general(domb[:BLK], vcat, nt, preferred_element_type=F32)
                ds = (pr * (dp - delta[:BLK])).astype(BF16)
                dq2 = jnp.where(m1, jnp.dot(ds, kcat, preferred_element_type=F32) * scale, dq2)
                if has_sink:
                    snk = _col_of(sk_ref[:, sl], _head_mask((1, LANES), half))
                    contrib = jnp.sum(-jnp.exp(snk - lcol[:BLK]) * delta[:BLK], axis=0, keepdims=True)
                    ds2 = jnp.where(_head_mask((1, LANES), half), contrib, ds2)
                s = lax.dot_general(qmcat, kc, nt, preferred_element_type=F32) * scale
                pr = jnp.where(valid_k, jnp.exp(s - lcol), 0.0)
                dv_acc += lax.dot_general(pr.astype(BF16), domb, tn, preferred_element_type=F32)
                dp = lax.dot_general(domb, vc, nt, preferred_element_type=F32)
                ds = (pr * (dp - delta)).astype(BF16)
                dk_acc += lax.dot_general(ds, qmcat, tn, preferred_element_type=F32) * scale
            dq_ref[:, sl] = dq2
            if has_sink:
                @pl.when(n == 0)
                def _():
                    outs[3][:, sl] = jnp.zeros((1, LANES), F32)

                outs[3][:, sl] += ds2
        dk_ref[...] = dk_acc
        dv_ref[...] = dv_acc

    qcur = pl.BlockSpec((BLK, nq * 128), lambda j, n: (n, j))
    qnext = pl.BlockSpec((BLK, nq * 128), lambda j, n: (jnp.minimum(n + 1, nb - 1), j))
    cur = pl.BlockSpec((BLK, 128), lambda j, n: (n, j))
    prev = pl.BlockSpec((BLK, 128), lambda j, n: (jnp.maximum(n - 1, 0), j))
    in_specs = [qcur, qnext, prev, cur, prev, cur, qcur, qnext, qcur, qnext, qcur, qnext]
    args = [q, q, k, k, v, v, o, o, lse, lse, do, do]
    out_specs = [qcur, cur, cur]
    out_shape = [jax.ShapeDtypeStruct(q.shape, F32), jax.ShapeDtypeStruct(k.shape, F32), jax.ShapeDtypeStruct(k.shape, F32)]
    if has_sink:
        vec = pl.BlockSpec((1, nq * 128), lambda j, n: (0, j))
        in_specs.append(vec)
        args.append(sinks)
        out_specs.append(vec)
        out_shape.append(jax.ShapeDtypeStruct((1, q.shape[1]), F32))
    return _pc(
        body, name=name, grid=(ncol, nb), in_specs=in_specs, out_specs=out_specs, out_shape=out_shape,
        compiler_params=_params(("parallel", "arbitrary")),
    )(*args)


ATT_TILE = 2048


def _rows(ref, start, n, r):
    if r == 1:
        return ref[pl.ds(start, n), :]
    return ref[pl.ds(start, n, stride=r), :]


def _twice(x):
    return jnp.concatenate([x, x], axis=0)


def _stack_heads(x, masks):
    zero = jnp.zeros_like(x)
    return jnp.concatenate([jnp.where(masks[0], x, zero), jnp.where(masks[1], x, zero)], axis=0)


def _set_rows(ref, start, n, r, val):
    if r == 1:
        ref[pl.ds(start, n), :] = val
    else:
        ref[pl.ds(start, n, stride=r), :] = val


def _band_geometry(S, patterns):
    rmax = max(r for _, r in patterns)
    H = BLK * rmax
    T = min(S, ATT_TILE)
    assert T % H == 0 and S % T == 0
    return H, T, S // T, T // BLK


def _band_fwd(q, k, v, *, patterns, nq, name, sinks=None, want_bf16=False):
    S, Ck = k.shape
    H, T, nt, nbt = _band_geometry(S, patterns)
    ncol = Ck // LANES
    scale = HEAD ** -0.5
    has_sink = sinks is not None
    nt_dims = (((1,), (1,)), ((), ()))

    def body(*refs):
        q_ref, kp_ref, kc_ref, vp_ref, vc_ref = refs[:5]
        sk_ref = refs[5] if has_sink else None
        n_out = 3 if want_bf16 else 2
        outs = refs[5 + has_sink:5 + has_sink + n_out]
        qf, kf, vf, M, L, A = refs[5 + has_sink + n_out:]
        t = pl.program_id(1)
        kf[0:H, :] = kp_ref[...].astype(F32)
        kf[H:H + T, :] = kc_ref[...].astype(F32)
        vf[0:H, :] = vp_ref[...].astype(F32)
        vf[H:H + T, :] = vc_ref[...].astype(F32)
        r_i = lax.broadcasted_iota(jnp.int32, (BLK, 2 * BLK), 0)
        c_i = lax.broadcasted_iota(jnp.int32, (BLK, 2 * BLK), 1)
        dist_i = r_i + BLK - c_i
        masks = [_head_mask((BLK, LANES), h) for h in range(2)]

        for i in range(nq):
            qf[...] = q_ref[:, i * 128:(i + 1) * 128].astype(F32) * scale
            for p, (dist, r) in enumerate(patterns):
                in_band = (dist_i >= 0) & (dist_i <= dist)
                in_band_first = in_band & ((c_i >= BLK) | (t > 0))
                in_band, in_band_first = _twice(in_band), _twice(in_band_first)

                def unit(j, b, p=p, r=r, in_band=in_band, in_band_first=in_band_first):
                    q0 = j + b * (BLK * r)
                    q2 = _rows(qf, q0, BLK, r).astype(BF16)
                    kcat = _rows(kf, H + q0 - BLK * r, 2 * BLK, r).astype(BF16)
                    vcat = _rows(vf, H + q0 - BLK * r, 2 * BLK, r).astype(BF16)
                    valid = in_band if b > 0 else in_band_first
                    s = lax.dot_general(_stack_heads(q2, masks), kcat, nt_dims, preferred_element_type=F32)
                    s = jnp.where(valid, s, -jnp.inf)
                    mx = jnp.max(s, axis=1, keepdims=True)
                    pr = jnp.exp(s - mx)
                    den = jnp.sum(pr, axis=1, keepdims=True)
                    pv = jnp.dot(pr.astype(BF16), vcat, preferred_element_type=F32)
                    m2 = jnp.where(masks[0], mx[:BLK], mx[BLK:])
                    l2 = jnp.where(masks[0], den[:BLK], den[BLK:])
                    a2 = jnp.where(masks[0], pv[:BLK], pv[BLK:])
                    if p > 0:
                        mo = _rows(M, q0, BLK, r)
                        mn = jnp.maximum(mo, m2)
                        wa, wb = jnp.exp(mo - mn), jnp.exp(m2 - mn)
                        l2 = wa * _rows(L, q0, BLK, r) + wb * l2
                        a2 = wa * _rows(A, q0, BLK, r) + wb * a2
                        m2 = mn
                    _set_rows(M, q0, BLK, r, m2)
                    _set_rows(L, q0, BLK, r, l2)
                    _set_rows(A, q0, BLK, r, a2)

                for u in range(nbt):
                    unit(u % r, u // r)
            sl = slice(i * 128, (i + 1) * 128)
            mm, ll, aa = M[...], L[...], A[...]
            if has_sink:
                snk = sk_ref[:, sl]
                mn = jnp.maximum(mm, snk)
                w = jnp.exp(mm - mn)
                ll = ll * w + jnp.exp(snk - mn)
                aa = aa * w
                mm = mn
            o = aa / ll
            outs[0][:, sl] = o
            outs[1][:, sl] = mm + jnp.log(ll)
            if want_bf16:
                outs[2][:, sl] = o.astype(BF16)

    th = T // H
    qspec = pl.BlockSpec((T, nq * 128), lambda j, t: (t, j))
    cur = pl.BlockSpec((T, 128), lambda j, t: (t, j))
    prev = pl.BlockSpec((H, 128), lambda j, t: (jnp.maximum(t * th - 1, 0), j))
    in_specs = [qspec, prev, cur, prev, cur]
    args = [q, k, k, v, v]
    if has_sink:
        in_specs.append(pl.BlockSpec((1, nq * 128), lambda j, t: (0, j)))
        args.append(sinks)
    out_dts = [F32, F32] + ([BF16] if want_bf16 else [])
    return _pc(
        body, name=name, grid=(ncol, nt), in_specs=in_specs,
        out_specs=[qspec] * len(out_dts),
        out_shape=[jax.ShapeDtypeStruct(q.shape, dt) for dt in out_dts],
        scratch_shapes=[pltpu.VMEM((T, LANES), F32), pltpu.VMEM((H + T, LANES), F32), pltpu.VMEM((H + T, LANES), F32),
                        pltpu.VMEM((T, LANES), F32), pltpu.VMEM((T, LANES), F32), pltpu.VMEM((T, LANES), F32)],
        compiler_params=_params(("parallel", "parallel")),
    )(*args)


def _band_bwd(q, k, v, lse, delta, do, *, patterns, nq, name, sinks=None, do_col0=0):
    S, Ck = k.shape
    H, T, nt, nbt = _band_geometry(S, patterns)
    ncol = Ck // LANES
    scale = HEAD ** -0.5
    has_sink = sinks is not None
    nt_dims = (((1,), (1,)), ((), ()))
    tn_dims = (((0,), (0,)), ((), ()))

    def body(*refs):
        (qc_ref, qn_ref, kp_ref, kc_ref, vp_ref, vc_ref, lc_ref, ln_ref, ec_ref, en_ref, dc_ref, dn_ref) = refs[:12]
        sk_ref = refs[12] if has_sink else None
        n_out = 4 if has_sink else 3
        outs = refs[12 + has_sink:12 + has_sink + n_out]
        dq_ref, dk_ref, dv_ref = outs[:3]
        qf, kf, vf, lf, ef, df = refs[12 + has_sink + n_out:]
        t = pl.program_id(1)
        kf[0:H, :] = kp_ref[...].astype(F32)
        kf[H:H + T, :] = kc_ref[...].astype(F32)
        vf[0:H, :] = vp_ref[...].astype(F32)
        vf[H:H + T, :] = vc_ref[...].astype(F32)
        dk_ref[...] = jnp.zeros_like(dk_ref)
        dv_ref[...] = jnp.zeros_like(dv_ref)
        r_i = lax.broadcasted_iota(jnp.int32, (BLK, 2 * BLK), 0)
        c_i = lax.broadcasted_iota(jnp.int32, (BLK, 2 * BLK), 1)
        dist_q = r_i + BLK - c_i
        dist_h = dist_q[:, :BLK]
        m1 = [_head_mask((BLK, LANES), h) for h in range(2)]

        def stacked_inputs(q2, do2, l2, e2):
            spread = lambda v: jnp.concatenate([jnp.where(m1[0], v, _roll(v, HEAD)), jnp.where(m1[1], v, _roll(v, HEAD))], axis=0)
            return _stack_heads(q2, m1), _stack_heads(do2.astype(BF16), m1), spread(l2), spread(e2)

        for i in range(nq):
            sl = slice(i * 128, (i + 1) * 128)
            qf[0:T, :] = qc_ref[:, sl].astype(F32) * scale
            qf[T:T + H, :] = qn_ref[:, sl].astype(F32) * scale
            for buf, c_ref, n_ref in ((lf, lc_ref, ln_ref), (ef, ec_ref, en_ref), (df, dc_ref, dn_ref)):
                buf[0:T, :] = c_ref[:, sl]
                buf[T:T + H, :] = n_ref[:, sl]
            if has_sink:
                @pl.when(t == 0)
                def _():
                    outs[3][:, sl] = jnp.zeros((1, LANES), F32)

                outs[3][:, sl] += jnp.sum(-jnp.exp(sk_ref[:, sl] - lc_ref[:, sl]) * ec_ref[:, sl], axis=0, keepdims=True)
            for p, (dist, r) in enumerate(patterns):
                band_q = (dist_q >= 0) & (dist_q <= dist)
                band_first = band_q & ((c_i >= BLK) | (t > 0))
                band_h = (dist_h >= 0) & (dist_h <= dist)
                band_q, band_first, band_h = _twice(band_q), _twice(band_first), _twice(band_h)

                def add_rows(ref, start, val, r=r):
                    _set_rows(ref, start, BLK, r, _rows(ref, start, BLK, r) + val)

                def unit(j, b, p=p, r=r, band_q=band_q, band_first=band_first):
                    q0 = j + b * (BLK * r)
                    q2 = _rows(qf, q0, BLK, r).astype(BF16)
                    do2, l2, e2 = _rows(df, q0, BLK, r), _rows(lf, q0, BLK, r), _rows(ef, q0, BLK, r)
                    kcat = _rows(kf, H + q0 - BLK * r, 2 * BLK, r).astype(BF16)
                    vcat = _rows(vf, H + q0 - BLK * r, 2 * BLK, r).astype(BF16)
                    valid = band_q if b > 0 else band_first
                    qs, dos, ls, es = stacked_inputs(q2, do2, l2, e2)
                    s = lax.dot_general(qs, kcat, nt_dims, preferred_element_type=F32)
                    pr = jnp.where(valid, jnp.exp(s - jnp.concatenate([ls, ls], axis=1)), 0.0)
                    dp = lax.dot_general(dos, vcat, nt_dims, preferred_element_type=F32)
                    ds = (pr * (dp - jnp.concatenate([es, es], axis=1))).astype(BF16)
                    dqs = jnp.dot(ds, kcat, preferred_element_type=F32) * scale
                    dq2 = jnp.where(m1[0], dqs[:BLK], dqs[BLK:])
                    dvc = lax.dot_general(pr.astype(BF16), dos, tn_dims, preferred_element_type=F32)
                    dkc = lax.dot_general(ds, qs, tn_dims, preferred_element_type=F32)
                    if p > 0:
                        dq2 = dq2 + _rows(dq_ref.at[:, sl], q0, BLK, r)
                    _set_rows(dq_ref.at[:, sl], q0, BLK, r, dq2)
                    add_rows(dk_ref, q0, dkc[BLK:])
                    add_rows(dv_ref, q0, dvc[BLK:])
                    if b > 0:
                        add_rows(dk_ref, q0 - BLK * r, dkc[:BLK])
                        add_rows(dv_ref, q0 - BLK * r, dvc[:BLK])

                def halo_unit(j, r=r, band_h=band_h):
                    k0 = j + (nbt // r - 1) * (BLK * r)
                    q2 = _rows(qf, T + j, BLK, r).astype(BF16)
                    do2, l2, e2 = _rows(df, T + j, BLK, r), _rows(lf, T + j, BLK, r), _rows(ef, T + j, BLK, r)
                    kc = _rows(kf, H + k0, BLK, r).astype(BF16)
                    vc = _rows(vf, H + k0, BLK, r).astype(BF16)
                    qs, dos, ls, es = stacked_inputs(q2, do2, l2, e2)
                    s = lax.dot_general(qs, kc, nt_dims, preferred_element_type=F32)
                    pr = jnp.where(band_h, jnp.exp(s - ls), 0.0)
                    dp = lax.dot_general(dos, vc, nt_dims, preferred_element_type=F32)
                    ds = (pr * (dp - es)).astype(BF16)
                    add_rows(dk_ref, k0, lax.dot_general(ds, qs, tn_dims, preferred_element_type=F32))
                    add_rows(dv_ref, k0, lax.dot_general(pr.astype(BF16), dos, tn_dims, preferred_element_type=F32))

                for u in range(nbt):
                    unit(u % r, u // r)
                if nt > 1:
                    @pl.when(t < nt - 1)
                    def _(r=r, halo_unit=halo_unit):
                        for j in range(r):
                            halo_unit(j)

    th = T // H
    last = S // H - 1
    qcur = pl.BlockSpec((T, nq * 128), lambda j, t: (t, j))
    qnext = pl.BlockSpec((H, nq * 128), lambda j, t: (jnp.minimum((t + 1) * th, last), j))
    cur = pl.BlockSpec((T, 128), lambda j, t: (t, j))
    prev = pl.BlockSpec((H, 128), lambda j, t: (jnp.maximum(t * th - 1, 0), j))
    dcur = pl.BlockSpec((T, nq * 128), lambda j, t: (t, j + do_col0))
    dnext = pl.BlockSpec((H, nq * 128), lambda j, t: (jnp.minimum((t + 1) * th, last), j + do_col0))
    in_specs = [qcur, qnext, prev, cur, prev, cur, qcur, qnext, qcur, qnext, dcur, dnext]
    args = [q, q, k, k, v, v, lse, lse, delta, delta, do, do]
    out_specs = [qcur, cur, cur]
    out_shape = [jax.ShapeDtypeStruct(q.shape, F32), jax.ShapeDtypeStruct(k.shape, F32), jax.ShapeDtypeStruct(k.shape, F32)]
    if has_sink:
        vec = pl.BlockSpec((1, nq * 128), lambda j, t: (0, j))
        in_specs.append(vec)
        args.append(sinks)
        out_specs.append(vec)
        out_shape.append(jax.ShapeDtypeStruct((1, q.shape[1]), F32))
    big = pltpu.VMEM((T + H, LANES), F32)
    return _pc(
        body, name=name, grid=(ncol, nt), in_specs=in_specs, out_specs=out_specs, out_shape=out_shape,
        scratch_shapes=[big] * 6,
        compiler_params=_params(("parallel", "arbitrary")),
    )(*args)


def _delta(do, o, name):
    S, C = do.shape
    tm = 512

    def body(do_ref, o_ref, g_ref, e_ref):
        for c in range(C // LANES):
            sl = slice(c * 128, (c + 1) * 128)
            e_ref[:, sl] = _gmean(do_ref[:, sl] * o_ref[:, sl], g_ref[...]) * float(HEAD)

    row = pl.BlockSpec((tm, C), lambda i: (i, 0))
    return _pc(
        body, name=name, grid=(S // tm,),
        in_specs=[row, row, pl.BlockSpec((LANES, LANES), lambda i: (0, 0))], out_specs=row,
        out_shape=jax.ShapeDtypeStruct((S, C), F32),
        compiler_params=_params(("parallel",)),
    )(do, o, _group_matrix())


def _even_post_fwd(ro, proj, gn, da):
    S = ro.shape[0]
    tm = 256

    def body(ro_ref, rg_ref, gn_ref, da_ref, mix_ref):
        for c in range(4):
            sl = slice(c * 128, (c + 1) * 128)
            x = ro_ref[:, sl]
            mu = jnp.mean(x, axis=1, keepdims=True)
            xc = x - mu
            var = jnp.mean(xc * xc, axis=1, keepdims=True)
            y = xc * lax.rsqrt(var + EPS) * gn_ref[:, sl]
            z = rg_ref[:, sl]
            mix_ref[:, sl] = (z * jax.nn.sigmoid(z) * y).astype(BF16)
        mix_ref[:, 512:1024] = da_ref[...].astype(BF16)

    row = lambda w: pl.BlockSpec((tm, w), lambda i: (i, 0))
    return _pc(
        body, name="even_post_fwd", grid=(S // tm,),
        in_specs=[row(512), pl.BlockSpec((tm, 512), lambda i: (i, 2)), pl.BlockSpec((1, 512), lambda i: (0, 0)), row(512)],
        out_specs=row(1024), out_shape=jax.ShapeDtypeStruct((S, 1024), BF16),
        compiler_params=_params(("parallel",)),
    )(ro, proj, gn, da)


def _even_post_bwd(ro, proj, gn, dmixed):
    S = ro.shape[0]
    tm = 256

    def body(ro_ref, rg_ref, gn_ref, dm_ref, dro_ref, drg_ref, dgn_ref):
        @pl.when(pl.program_id(0) == 0)
        def _():
            dgn_ref[...] = jnp.zeros_like(dgn_ref)

        for c in range(4):
            sl = slice(c * 128, (c + 1) * 128)
            x = ro_ref[:, sl]
            mu = jnp.mean(x, axis=1, keepdims=True)
            xc = x - mu
            rstd = lax.rsqrt(jnp.mean(xc * xc, axis=1, keepdims=True) + EPS)
            xh = xc * rstd
            gain = gn_ref[:, sl]
            y = xh * gain
            z = rg_ref[:, sl]
            sg = jax.nn.sigmoid(z)
            dra = dm_ref[:, sl]
            drg_ref[:, sl] = dra * y * sg * (1.0 + z * (1.0 - sg))
            dy = dra * z * sg
            dgn_ref[:, sl] += jnp.sum(dy * xh, axis=0, keepdims=True)
            dxh = dy * gain
            dro_ref[:, sl] = rstd * (dxh - jnp.mean(dxh, axis=1, keepdims=True)
                                     - xh * jnp.mean(dxh * xh, axis=1, keepdims=True))

    row = lambda w: pl.BlockSpec((tm, w), lambda i: (i, 0))
    vec = pl.BlockSpec((1, 512), lambda i: (0, 0))
    return _pc(
        body, name="even_post_bwd", grid=(S // tm,),
        in_specs=[row(512), pl.BlockSpec((tm, 512), lambda i: (i, 2)), vec, row(512)],
        out_specs=[row(512), row(512), vec],
        out_shape=[jax.ShapeDtypeStruct((S, 512), F32), jax.ShapeDtypeStruct((S, 512), F32),
                   jax.ShapeDtypeStruct((1, 512), F32)],
        compiler_params=_params(("arbitrary",)),
    )(ro, proj, gn, dmixed)


def _swa_pre_fwd(proj, tab, qg, kg):
    S = proj.shape[0]
    tm = 256

    def body(p_ref, tab_ref, qg_ref, kg_ref, g_ref, q_ref, k_ref, v_ref):
        Ap, Bp, Cp = _tab(tab_ref, 1)
        G = g_ref[...]
        lo = _head_mask((tm, LANES), 0)
        for c in range(8):
            sl = slice(c * 128, (c + 1) * 128)
            q_ref[:, sl] = _rope(_hn_fwd(p_ref[:, sl], qg_ref[...], G), Ap, Bp, Cp, 8).astype(BF16)
        for c in range(2):
            kn = _rope(_hn_fwd(p_ref[:, 1024 + c * 128:1024 + (c + 1) * 128], kg_ref[...], G), Ap, Bp, Cp, 8)
            vv = p_ref[:, 1280 + c * 128:1280 + (c + 1) * 128]
            for t, ref in ((kn, k_ref), (vv, v_ref)):
                sw = _roll(t, HEAD)
                ref[:, (2 * c) * 128:(2 * c + 1) * 128] = jnp.where(lo, t, sw).astype(BF16)
                ref[:, (2 * c + 1) * 128:(2 * c + 2) * 128] = jnp.where(lo, sw, t).astype(BF16)

    row = lambda w: pl.BlockSpec((tm, w), lambda i: (i, 0))
    vec = pl.BlockSpec((1, LANES), lambda i: (0, 0))
    return _pc(
        body, name="swa_pre_fwd", grid=(S // tm,),
        in_specs=[row(1536), row(768), vec, vec, pl.BlockSpec((LANES, LANES), lambda i: (0, 0))],
        out_specs=[row(1024), row(512), row(512)],
        out_shape=[jax.ShapeDtypeStruct((S, w), BF16) for w in (1024, 512, 512)],
        compiler_params=_params(("parallel",)),
    )(proj, tab, qg, kg, _group_matrix())


def _swa_pre_bwd(proj, tab, qg, kg, dq, dk, dv):
    S = proj.shape[0]
    tm = 256

    def body(p_ref, tab_ref, qg_ref, kg_ref, g_ref, dq_ref, dk_ref, dv_ref, dp_ref, db_ref, dqg_ref, dkg_ref):
        Ap, Bp, Cp = _tab(tab_ref, 1)
        G = g_ref[...]
        lo = _head_mask((tm, LANES), 0)

        @pl.when(pl.program_id(0) == 0)
        def _():
            db_ref[...] = jnp.zeros_like(db_ref)
            dqg_ref[...] = jnp.zeros_like(dqg_ref)
            dkg_ref[...] = jnp.zeros_like(dkg_ref)

        accq = jnp.zeros((1, LANES), F32)
        acck = jnp.zeros((1, LANES), F32)
        for c in range(8):
            sl = slice(c * 128, (c + 1) * 128)
            dx, dg = _hn_bwd(p_ref[:, sl], qg_ref[...], _rope_t(dq_ref[:, sl], Ap, Bp, Cp, 8), G)
            dp_ref[:, sl] = dx.astype(BF16)
            db_ref[:, sl] += jnp.sum(dx, axis=0, keepdims=True)
            accq = accq + dg
        for c in range(2):
            folded = []
            for ref in (dk_ref, dv_ref):
                a = ref[:, (2 * c) * 128:(2 * c + 1) * 128]
                b = ref[:, (2 * c + 1) * 128:(2 * c + 2) * 128]
                folded.append(jnp.where(lo, a + _roll(a, HEAD), b + _roll(b, HEAD)))
            ks = slice(1024 + c * 128, 1024 + (c + 1) * 128)
            dx, dg = _hn_bwd(p_ref[:, ks], kg_ref[...], _rope_t(folded[0], Ap, Bp, Cp, 8), G)
            dp_ref[:, ks] = dx.astype(BF16)
            db_ref[:, ks] += jnp.sum(dx, axis=0, keepdims=True)
            acck = acck + dg
            vs = slice(1280 + c * 128, 1280 + (c + 1) * 128)
            dp_ref[:, vs] = folded[1].astype(BF16)
            db_ref[:, vs] += jnp.sum(folded[1], axis=0, keepdims=True)
        dqg_ref[...] += _fold_halves(accq)
        dkg_ref[...] += _fold_halves(acck)

    row = lambda w: pl.BlockSpec((tm, w), lambda i: (i, 0))
    vec = pl.BlockSpec((1, LANES), lambda i: (0, 0))
    return _pc(
        body, name="swa_pre_bwd", grid=(S // tm,),
        in_specs=[row(1536), row(768), vec, vec, pl.BlockSpec((LANES, LANES), lambda i: (0, 0)),
                  row(1024), row(512), row(512)],
        out_specs=[row(1536), pl.BlockSpec((1, 1536), lambda i: (0, 0)), vec, vec],
        out_shape=[jax.ShapeDtypeStruct((S, 1536), BF16), jax.ShapeDtypeStruct((1, 1536), F32),
                   jax.ShapeDtypeStruct((1, LANES), F32), jax.ShapeDtypeStruct((1, LANES), F32)],
        compiler_params=_params(("arbitrary",)),
    )(proj, tab, qg, kg, _group_matrix(), dq, dk, dv)


def _loss_head(y, target):
    S, Dm = y.shape
    tm = 512

    def body(y_ref, t_ref, l_ref, dy_ref, dyb_ref):
        @pl.when(pl.program_id(0) == 0)
        def _():
            l_ref[...] = jnp.zeros_like(l_ref)

        e = y_ref[...] - t_ref[...]
        dy = e * (1.0 / Dm)
        dy_ref[...] = dy
        dyb_ref[...] = dy.astype(BF16)
        row = jnp.sum(e * e, axis=1, keepdims=True) * (0.5 / Dm)
        l_ref[...] += jnp.sum(row, axis=0, keepdims=True)

    row = pl.BlockSpec((tm, Dm), lambda i: (i, 0))
    return _pc(
        body, name="loss_head", grid=(S // tm,), in_specs=[row, row],
        out_specs=[pl.BlockSpec((1, LANES), lambda i: (0, 0)), row, row],
        out_shape=[jax.ShapeDtypeStruct((1, LANES), F32), jax.ShapeDtypeStruct((S, Dm), F32),
                   jax.ShapeDtypeStruct((S, Dm), BF16)],
        compiler_params=_params(("arbitrary",)),
    )(y, target)


def _relu2_of(u):
    r = jnp.maximum(u.astype(F32), 0.0)
    return r * r


def _drelu2(acc, u):
    return (acc * 2.0 * jnp.maximum(u.astype(F32), 0.0),)


def _add(acc, res):
    return (acc + res,)


def _add_norm_in(res, g):
    def epilogue(acc, r, gv):
        xn = acc + r
        return xn, xn * lax.rsqrt(jnp.mean(xn * xn, axis=-1, keepdims=True) + EPS) * gv

    return dict(outs=[F32, BF16], epilogue=epilogue, extras=[(res, "mn"), (g.reshape(1, D_MODEL), "n")])


_T = dict(tm=1024, tn=1024, tk=1024)


def _rms_bwd_in(x, g, dres):
    def epilogue(dh, xv, gv, dr):
        r = lax.rsqrt(jnp.mean(xv * xv, axis=-1, keepdims=True) + EPS)
        t = dh * gv
        dx = dr + r * t - xv * (r * r * r) * jnp.mean(xv * t, axis=-1, keepdims=True)
        return dx, dx, jnp.sum(dh * xv * r, axis=0, keepdims=True)

    return dict(outs=[F32, BF16, ("colsum",)], epilogue=epilogue,
                extras=[(x, "mn"), (g.reshape(1, D_MODEL), "n"), (dres, "mn")])


def _delta_in(o, col0):
    width = D_MODEL - col0

    def epilogue(do, ov, G):
        parts = [_gmean(do[:, col0 + c * 128:col0 + (c + 1) * 128] * ov[:, c * 128:(c + 1) * 128], G) * float(HEAD)
                 for c in range(width // LANES)]
        return do, jnp.concatenate(parts, axis=1)

    return dict(outs=[F32, (F32, width)], epilogue=epilogue, extras=[(o, width), (_group_matrix(), "full")])


def _mlp_fwd(x, h, wts, layer, tag, next_gain=None):
    u = _matmul(h, wts, dims="nn", **_T, outs=[BF16], b_cs=True, b_row0=layer, name=f"mlp_up{tag}")
    tail = _add_norm_in(x, next_gain) if next_gain is not None else dict(outs=[F32], epilogue=_add, extras=[(x, "mn")])
    out = _matmul(u, wts, dims="nn", **_T, a_pro=_relu2_of, b_rs=1024, b_row0=2 + layer, name=f"mlp_down{tag}", **tail)
    return out, (h, u)


def _mlp_bwd(x, g, wts, layer, saved, dy, dyb, tag):
    h, u = saved
    du = _matmul(dyb, wts, dims="nt", **_T, outs=[BF16], epilogue=_drelu2, extras=[(u, "mn")], b_rs=1024,
                 b_row0=2 + layer, name=f"mlp_du{tag}")
    dw_dn = _matmul(u, dyb, dims="tn", **_T, outs=[F32], a_pro=_relu2_of, name=f"mlp_dwdown{tag}")
    dw_up = _matmul(h, du, dims="tn", **_T, outs=[F32], o_cs=N_CHIPS, name=f"mlp_dwup{tag}")
    dx, dxb, dg = _matmul(du, wts, dims="nt", tm=512, tn=1024, tk=1024, b_cs=True, b_row0=layer, b_rows=1024, name=f"mlp_dh{tag}",
                          **_rms_bwd_in(x, g, dy))
    return dx, dxb, dg, dw_up, dw_dn


def _pattern_view(t, r):
    S, C = t.shape
    return t.reshape(S // r, r * C)


def _local_step(x, pos_col, target, first_of, rest_begin, rest_of, P, red):
    S = x.shape[0]
    tab = _tables(pos_col)
    tile2 = lambda g: jnp.tile(g.reshape(1, HEAD), (1, 2))
    dqg, dkg = tile2(P["dil_q_gain"]), tile2(P["dil_k_gain"])
    sqg, skg = tile2(P["swa_q_gain"]), tile2(P["swa_k_gain"])
    gn = P["ret_gn_gain"].reshape(1, 512)
    sink_b = jnp.repeat(P["swa_sinks"].reshape(16), HEAD).reshape(1, 1024)

    h0 = _rms_fwd(x, P["norm_mix"][0], "rms_mix_fwd0")
    W = first_of(h0)
    proj = _matmul(h0, W["hyb_w_in"], dims="nn", tm=1024, tn=768, tk=1024, outs=[F32], b_cs=True, name="hyb_in")
    rq, rk, rv, dq, dk, dv = _even_pre_fwd(proj, tab, dqg, dkg)
    ro, states = _ret_fwd(rq, rk, rv)
    dil = [(w // r, r) for w, r in DIL_PATTERNS]
    da, dlse = _band_fwd(dq, dk, dv, patterns=dil, nq=1, name="dil_fwd")
    mixed = rest_begin(_even_post_fwd(ro, proj, gn, da))
    x1, h1 = _matmul(mixed, W["hyb_w_out"], dims="nn", **_T, name="hyb_out", **_add_norm_in(x, P["norm_mlp"][0]))
    rest, bias = rest_of(x1)
    W = {**W, **rest}
    (x2, h2), mlp0 = _mlp_fwd(x1, h1, W["packed"], 0, "0", next_gain=P["norm_mix"][1])

    proj2 = _matmul(h2, W["swa_w_qkv"], dims="nn", tm=1024, tn=384, tk=1024, outs=[F32], b_cs=True,
                    epilogue=_add, extras=[(bias.reshape(1, 1536), "n")], name="swa_qkv")
    sq, sk, sv = _swa_pre_fwd(proj2, tab, sqg, skg)
    swa = [(SWA_DIST, 1)]
    so, slse, so_b = _band_fwd(sq, sk, sv, patterns=swa, nq=2, name="swa_fwd", sinks=sink_b, want_bf16=True)
    x3, h3 = _matmul(so_b, W["swa_w_out"], dims="nn", **_T, name="swa_out", **_add_norm_in(x2, P["norm_mlp"][1]))
    y, mlp1 = _mlp_fwd(x3, h3, W["packed"], 1, "1")
    loss, dy, dyb = _loss_head(y, target)

    gw, gp = {}, {}
    dx3, dx3b, dg_mlp1, gw["mlp_w_up1"], gw["mlp_w_down1"] = _mlp_bwd(x3, P["norm_mlp"][1], W["packed"], 1, mlp1, dy, dyb, "1")
    dx3b = red.begin("mlp1", {n: (gw[n], 1024) for n in ("mlp_w_up1", "mlp_w_down1")}, dx3b)
    gw["swa_w_out"] = _matmul(so_b, dx3b, dims="tn", **_T, outs=[F32], name="swa_dwout")
    dso, sdelta = _matmul(dx3b, W["swa_w_out"], dims="nt", tm=512, tn=1024, tk=1024, name="swa_do", **_delta_in(so, 0))
    dsq, dsk, dsv, dsink = _band_bwd(sq, sk, sv, slse, sdelta, dso, patterns=swa, nq=2, name="swa_bwd", sinks=sink_b)
    dproj2, gp["swa_b_qkv"], gp["swa_q_gain"], gp["swa_k_gain"] = _swa_pre_bwd(proj2, tab, sqg, skg, dsq, dsk, dsv)
    gp["swa_sinks"] = dsink
    gw["swa_w_qkv"] = _matmul(h2, dproj2, dims="tn", tm=1024, tn=384, tk=1024, outs=[F32], o_cs=N_CHIPS, name="swa_dwqkv")
    dx2, dx2b, dg_mix1 = _matmul(dproj2, W["swa_w_qkv"], dims="nt", tm=512, tn=1024, tk=384, b_cs=True, name="swa_dh",
                                 **_rms_bwd_in(x2, P["norm_mix"][1], dx3))
    dx2b = red.begin("swa", {"swa_w_qkv": (gw["swa_w_qkv"], 1024), "swa_w_out": (gw["swa_w_out"], 256)}, dx2b)
    dx2b = red.advance("mlp1", dx2b, dx2b)

    dx1, dx1b, dg_mlp0, gw["mlp_w_up0"], gw["mlp_w_down0"] = _mlp_bwd(x1, P["norm_mlp"][0], W["packed"], 0, mlp0, dx2, dx2b, "0")
    gw["hyb_w_out"] = _matmul(mixed, dx1b, dims="tn", **_T, outs=[F32], name="hyb_dwout")
    dx1b = red.begin("mlp0", {"mlp_w_up0": (gw["mlp_w_up0"], 1024), "mlp_w_down0": (gw["mlp_w_down0"], 1024),
                              "hyb_w_out": (gw["hyb_w_out"], 256)}, dx1b)
    dx1b = red.advance("swa", dx1b, dx1b)
    red.finish("mlp1", dx1b)
    dmixed, ddelta = _matmul(dx1b, W["hyb_w_out"], dims="nt", tm=512, tn=1024, tk=1024, name="hyb_dmixed", **_delta_in(da, 512))
    dro, drg, gp["ret_gn_gain"] = _even_post_bwd(ro, proj, gn, dmixed)
    drq, drk, drv = _ret_bwd(rq, rk, rv, states, dro)
    ddq, ddk, ddv = _band_bwd(dq, dk, dv, dlse, ddelta, dmixed, patterns=dil, nq=1, name="dil_bwd", do_col0=4)
    ddq = red.advance("mlp0", ddq, ddq)
    red.finish("swa", ddq)
    dproj, gp["dil_q_gain"], gp["dil_k_gain"] = _even_pre_bwd(proj, tab, dqg, dkg, drq, drk, drv, drg, [ddq], [ddk], [ddv])
    gw["hyb_w_in"] = _matmul(h0, dproj, dims="tn", tm=1024, tn=768, tk=1024, outs=[F32], o_cs=N_CHIPS, name="hyb_dwin")
    dproj = red.begin("win", {"hyb_w_in": (gw["hyb_w_in"], 1024)}, dproj)
    grad_x, _, dg_mix0 = _matmul(dproj, W["hyb_w_in"], dims="nt", tm=512, tn=1024, tk=768, b_cs=True, name="hyb_dh",
                                 **_rms_bwd_in(x, P["norm_mix"][0], dx1))
    red.finish("mlp0", grad_x)
    gp["norm_mix"] = jnp.concatenate([dg_mix0, dg_mix1], axis=0)
    gp["norm_mlp"] = jnp.concatenate([dg_mlp0, dg_mlp1], axis=0)
    return loss, grad_x, gp


HBM = pl.BlockSpec(memory_space=pltpu.HBM)


def _place():
    x, y, c = lax.axis_index("x"), lax.axis_index("y"), lax.axis_index("c")
    chips = [(1 - x, y), (x, 1 - y), (1 - x, 1 - y)]
    return x, y, c, chips


def _allgather_shards(buf):
    _, R, Wd = buf.shape
    Rh = R // 2

    def body(b_ref, out_ref, send_sems, recv_sems):
        x, y, c, chips = _place()
        sibling = (x, y, 1 - c)

        def copy(k, chip, core, to):
            block = b_ref.at[2 * chip[0] + chip[1], pl.ds(core * Rh, Rh), :]
            return pltpu.make_async_remote_copy(
                src_ref=block, dst_ref=block, send_sem=send_sems.at[k], recv_sem=recv_sems.at[k],
                device_id=to, device_id_type=MESH)

        first = [copy(k, (x, y), c, (*chip, c)) for k, chip in enumerate(chips)]
        for cp in first:
            cp.start()
        passed = [copy(3 + k, chip, c, sibling) for k, chip in enumerate(chips)]
        for k, chip in enumerate(chips):
            copy(k, chip, c, (x, y, c)).wait_recv()
            passed[k].start()
        for k, chip in enumerate(chips):
            copy(3 + k, chip, 1 - c, (x, y, c)).wait_recv()
        for cp in first + passed:
            cp.wait_send()

    return _pc(
        body, name="allgather_first", in_specs=[HBM], out_specs=HBM,
        out_shape=jax.ShapeDtypeStruct(buf.shape, buf.dtype), input_output_aliases={0: 0},
        scratch_shapes=[pltpu.SemaphoreType.DMA((6,)), pltpu.SemaphoreType.DMA((6,))],
    )(buf)


SEM = pl.BlockSpec(memory_space=pltpu.SEMAPHORE)
EFFECT = pltpu.SideEffectType.DATAFLOW_SIDE_EFFECTING


def _half_block(ref, chip, core):
    rh = ref.shape[1] // 2
    return ref.at[2 * chip[0] + chip[1], pl.ds(core * rh, rh), :]


def _gather_start(buf, ride, name):
    def body(b_ref, ride_ref, s0, s1, s2, r0, r1, r2, b_out, ride_out):
        x, y, c, chips = _place()
        for chip, s, r in zip(chips, (s0, s1, s2), (r0, r1, r2)):
            mine = _half_block(b_ref, (x, y), c)
            pltpu.make_async_remote_copy(src_ref=mine, dst_ref=mine, send_sem=s, recv_sem=r,
                                         device_id=(*chip, c), device_id_type=MESH).start()

    sem = pltpu.SemaphoreType.DMA(())
    return _pc(
        body, name=name,
        out_shape=(sem,) * 6 + (pltpu.HBM(buf.shape, buf.dtype), pltpu.HBM(ride.shape, ride.dtype)),
        in_specs=(HBM, HBM), out_specs=(SEM,) * 6 + (HBM, HBM), input_output_aliases={0: 6, 1: 7},
        compiler_params=pltpu.CompilerParams(has_side_effects=EFFECT),
    )(pltpu.with_memory_space_constraint(buf, pltpu.HBM), pltpu.with_memory_space_constraint(ride, pltpu.HBM))


def _gather_wait(buf, sems, after, name):
    def body(b_ref, s0, s1, s2, r0, r1, r2, after_ref, b_out):
        x, y, c, chips = _place()
        for chip, s, r in zip(chips, (s0, s1, s2), (r0, r1, r2)):
            cp = pltpu.make_async_remote_copy(src_ref=_half_block(b_ref, (x, y), c), dst_ref=_half_block(b_ref, chip, c),
                                              send_sem=s, recv_sem=r, device_id=(*chip, c), device_id_type=MESH)
            cp.wait_send()
            cp.wait_recv()

    return _pc(
        body, name=name, out_shape=pltpu.HBM(buf.shape, buf.dtype),
        in_specs=(HBM,) + (SEM,) * 6 + (pl.BlockSpec(memory_space=pl.ANY),), out_specs=HBM, input_output_aliases={0: 0},
        compiler_params=pltpu.CompilerParams(has_side_effects=EFFECT),
    )(buf, *sems, after)


def _gather_handover(buf, name):
    def body(b_ref, out_ref, send_sems, recv_sems):
        x, y, c, chips = _place()
        cps = []
        for k, chip in enumerate(chips):
            mine = _half_block(b_ref, chip, c)
            cps.append(pltpu.make_async_remote_copy(src_ref=mine, dst_ref=mine, send_sem=send_sems.at[k],
                                                    recv_sem=recv_sems.at[k], device_id=(x, y, 1 - c), device_id_type=MESH))
        for cp in cps:
            cp.start()
        for k, chip in enumerate(chips):
            theirs = _half_block(b_ref, chip, 1 - c)
            pltpu.make_async_remote_copy(src_ref=theirs, dst_ref=theirs, send_sem=send_sems.at[k], recv_sem=recv_sems.at[k],
                                         device_id=(x, y, 1 - c), device_id_type=MESH).wait_recv()
        for cp in cps:
            cp.wait_send()

    return _pc(
        body, name=name, in_specs=[HBM], out_specs=HBM,
        out_shape=jax.ShapeDtypeStruct(buf.shape, buf.dtype), input_output_aliases={0: 0},
        scratch_shapes=[pltpu.SemaphoreType.DMA((3,)), pltpu.SemaphoreType.DMA((3,))],
    )(buf)


def _swap_halves(ts):
    nt = len(ts)

    def body(*refs):
        t_refs, l_refs, send_sems, recv_sems = refs[:nt], refs[nt:2 * nt], refs[-2], refs[-1]
        x, y, c, _ = _place()
        cps = []
        for k in range(nt):
            rh = t_refs[k].shape[1] // 2
            cps.append(pltpu.make_async_remote_copy(
                src_ref=t_refs[k].at[:, pl.ds((1 - c) * rh, rh), :], dst_ref=l_refs[k],
                send_sem=send_sems.at[k], recv_sem=recv_sems.at[k], device_id=(x, y, 1 - c), device_id_type=MESH))
        for cp in cps:
            cp.start()
        for cp in cps:
            cp.wait()

    return _pc(
        body, name="grad_swap_halves", in_specs=[HBM] * nt, out_specs=[HBM] * nt,
        out_shape=[jax.ShapeDtypeStruct((t.shape[0], t.shape[1] // 2, t.shape[2]), F32) for t in ts],
        scratch_shapes=[pltpu.SemaphoreType.DMA((nt,)), pltpu.SemaphoreType.DMA((nt,))],
    )(*ts)


def _pair_sum(t, l, place, name):
    _, r, cols = t.shape
    rh = r // 2
    tr = min(rh, 256)
    nr = rh // tr

    def body(pl_ref, t_ref, l_ref, o_ref):
        o_ref[...] = (t_ref[...] + l_ref[...]).astype(BF16)

    other = lambda s, p: s + jnp.where(s >= p[0], 1, 0)
    return _pc(
        body, name=name,
        grid_spec=pltpu.PrefetchScalarGridSpec(
            num_scalar_prefetch=1, grid=(N_CHIPS - 1, nr),
            in_specs=[pl.BlockSpec((None, tr, cols), lambda s, i, p: (other(s, p), p[1] * nr + i, 0)),
                      pl.BlockSpec((None, tr, cols), lambda s, i, p: (other(s, p), i, 0))],
            out_specs=pl.BlockSpec((None, tr, cols), lambda s, i, p: (other(s, p), i, 0))),
        out_shape=jax.ShapeDtypeStruct((N_CHIPS, rh, cols), BF16),
        compiler_params=_params(("parallel", "parallel")),
    )(place, t, l)


def _exchange_chips(ps):
    nt = len(ps)

    def body(*refs):
        p_refs, r_refs, send_sems, recv_sems = refs[:nt], refs[nt:2 * nt], refs[-2], refs[-1]
        x, y, c, chips = _place()
        cps = []
        for t in range(nt):
            for k, chip in enumerate(chips):
                cps.append(pltpu.make_async_remote_copy(
                    src_ref=p_refs[t].at[2 * chip[0] + chip[1]], dst_ref=r_refs[t].at[k],
                    send_sem=send_sems.at[3 * t + k], recv_sem=recv_sems.at[3 * t + k],
                    device_id=(*chip, c), device_id_type=MESH))
        for cp in cps:
            cp.start()
        for cp in cps:
            cp.wait()

    return _pc(
        body, name="grad_exchange_chips", in_specs=[HBM] * nt, out_specs=[HBM] * nt,
        out_shape=[jax.ShapeDtypeStruct((3,) + p.shape[1:], BF16) for p in ps],
        scratch_shapes=[pltpu.SemaphoreType.DMA((3 * nt,)), pltpu.SemaphoreType.DMA((3 * nt,))],
    )(*ps)


def _final_sum(t, l, rcv, place, name, layer=0, layers=1, into=None):
    _, r, cols = t.shape
    rh = r // 2
    tr = min(rh, 256)
    nr = rh // tr

    def body(pl_ref, t_ref, l_ref, r_ref, *rest):
        acc = t_ref[...] + l_ref[...]
        for k in range(3):
            acc = acc + r_ref[k].astype(F32)
        rest[-1][...] = acc

    in_specs = [pl.BlockSpec((None, tr, cols), lambda i, p: (p[0], p[1] * nr + i, 0)),
                pl.BlockSpec((None, tr, cols), lambda i, p: (p[0], i, 0)),
                pl.BlockSpec((3, tr, cols), lambda i, p: (0, i, 0))]
    args = [place, t, l, rcv]
    aliases = {}
    if into is not None:
        in_specs.append(pl.BlockSpec(memory_space=pl.ANY))
        args.append(into)
        aliases = {4: 0}
    return _pc(
        body, name=name,
        grid_spec=pltpu.PrefetchScalarGridSpec(
            num_scalar_prefetch=1, grid=(nr,), in_specs=in_specs,
            out_specs=pl.BlockSpec((tr, cols), lambda i, p: (2 * nr * layer + p[1] * nr + i, 0))),
        out_shape=jax.ShapeDtypeStruct((layers * r, cols), F32), input_output_aliases=aliases,
        compiler_params=_params(("parallel",)),
    )(*args)


def _share_halves(hs, name):
    nt = len(hs)
    n = sum(layers for _, layers in hs)

    def body(*refs):
        h_refs, send_sems, recv_sems = refs[:nt], refs[-2], refs[-1]
        x, y, c, _ = _place()
        cps = []
        for k, (_, layers) in enumerate(hs):
            rh = h_refs[k].shape[0] // (2 * layers)
            for layer in range(layers):
                half = h_refs[k].at[pl.ds((2 * layer + c) * rh, rh), :]
                cps.append(pltpu.make_async_remote_copy(
                    src_ref=half, dst_ref=half, send_sem=send_sems.at[len(cps)], recv_sem=recv_sems.at[len(cps)],
                    device_id=(x, y, 1 - c), device_id_type=MESH))
        for cp in cps:
            cp.start()
        for cp in cps:
            cp.wait()

    return _pc(
        body, name=name, in_specs=[HBM] * nt, out_specs=[HBM] * nt,
        out_shape=[jax.ShapeDtypeStruct(h.shape, F32) for h, _ in hs],
        input_output_aliases={k: k for k in range(nt)},
        scratch_shapes=[pltpu.SemaphoreType.DMA((n,)), pltpu.SemaphoreType.DMA((n,))],
    )(*[h for h, _ in hs])


def _split_start(name, bufs, ride, n, copies_of):
    nb = len(bufs)

    def body(*refs):
        sems = refs[nb + 1:nb + 1 + 2 * n]
        for cp in copies_of(refs[:nb], sems[:n], sems[n:]):
            (cp[0] if isinstance(cp, tuple) else cp).start()

    outs = _pc(
        body, name=name,
        out_shape=(pltpu.SemaphoreType.DMA(()),) * (2 * n) + tuple(pltpu.HBM(b.shape, b.dtype) for b in bufs)
        + (pltpu.HBM(ride.shape, ride.dtype),),
        in_specs=(HBM,) * (nb + 1), out_specs=(SEM,) * (2 * n) + (HBM,) * (nb + 1),
        input_output_aliases={k: 2 * n + k for k in range(nb + 1)},
        compiler_params=pltpu.CompilerParams(has_side_effects=EFFECT),
    )(*[pltpu.with_memory_space_constraint(b, pltpu.HBM) for b in bufs], pltpu.with_memory_space_constraint(ride, pltpu.HBM))
    return list(outs[:2 * n]), list(outs[2 * n:2 * n + nb]), outs[-1]


def _split_wait(name, bufs, sems, after, n, copies_of):
    nb = len(bufs)

    def body(*refs):
        s = refs[nb:nb + 2 * n]
        for cp in copies_of(refs[:nb], s[:n], s[n:]):
            sent, landed = cp if isinstance(cp, tuple) else (cp, cp)
            sent.wait_send()
            landed.wait_recv()

    outs = _pc(
        body, name=name, out_shape=tuple(pltpu.HBM(b.shape, b.dtype) for b in bufs),
        in_specs=(HBM,) * nb + (SEM,) * (2 * n) + (pl.BlockSpec(memory_space=pl.ANY),), out_specs=(HBM,) * nb,
        input_output_aliases={k: k for k in range(nb)},
        compiler_params=pltpu.CompilerParams(has_side_effects=EFFECT),
    )(*bufs, *sems, after)
    return list(outs)


def _handover_copies(refs, send, recv):
    x, y, c, chips = _place()
    cps = []
    for k, chip in enumerate(chips):
        mine, theirs = _half_block(refs[0], chip, c), _half_block(refs[0], chip, 1 - c)
        desc = lambda blk: pltpu.make_async_remote_copy(src_ref=blk, dst_ref=blk, send_sem=send[k], recv_sem=recv[k],
                                                        device_id=(x, y, 1 - c), device_id_type=MESH)
        cps.append((desc(mine), desc(theirs)))
    return cps


def _swap_copies(nt):
    def copies_of(refs, send, recv):
        x, y, c, _ = _place()
        cps = []
        for k in range(nt):
            rh = refs[k].shape[1] // 2
            cps.append(pltpu.make_async_remote_copy(
                src_ref=refs[k].at[:, pl.ds((1 - c) * rh, rh), :], dst_ref=refs[nt + k],
                send_sem=send[k], recv_sem=recv[k], device_id=(x, y, 1 - c), device_id_type=MESH))
        return cps
    return copies_of


def _exchange_copies(nt):
    def copies_of(refs, send, recv):
        x, y, c, chips = _place()
        cps = []
        for t in range(nt):
            for k, chip in enumerate(chips):
                cps.append(pltpu.make_async_remote_copy(
                    src_ref=refs[t].at[2 * chip[0] + chip[1]], dst_ref=refs[nt + t].at[k],
                    send_sem=send[3 * t + k], recv_sem=recv[3 * t + k], device_id=(*chip, c), device_id_type=MESH))
        return cps
    return copies_of


class _StagedReduce:
    def __init__(self, place):
        self.place = place
        self.groups = {}
        self.halves = {}

    @staticmethod
    def slab(t, r):
        return t.reshape(N_CHIPS, r, t.size // (N_CHIPS * r))

    def begin(self, g, grads, ride):
        names = list(grads)
        ts = [self.slab(t, r) for t, r in grads.values()]
        lands = [lax.empty((N_CHIPS, t.shape[1] // 2, t.shape[2]), F32) for t in ts]
        sems, bufs, ride = _split_start(f"grad_swap_start_{g}", ts + lands, ride, len(ts), _swap_copies(len(ts)))
        self.groups[g] = dict(names=names, bufs=bufs, sems=sems)
        return ride

    def advance(self, g, after, ride):
        st = self.groups[g]
        nt = len(st["names"])
        bufs = _split_wait(f"grad_swap_wait_{g}", st["bufs"], st["sems"], after, nt, _swap_copies(nt))
        st["ts"], st["ls"] = bufs[:nt], bufs[nt:]
        ps = [_pair_sum(t, l, self.place, f"pair_sum_{n}") for t, l, n in zip(st["ts"], st["ls"], st["names"])]
        lands = [lax.empty((3,) + p.shape[1:], BF16) for p in ps]
        st["sems"], st["bufs"], ride = _split_start(f"grad_exchange_start_{g}", ps + lands, ride, 3 * nt, _exchange_copies(nt))
        return ride

    def finish(self, g, after):
        st = self.groups[g]
        nt = len(st["names"])
        bufs = _split_wait(f"grad_exchange_wait_{g}", st["bufs"], st["sems"], after, 3 * nt, _exchange_copies(nt))
        for t, l, r, n in zip(st["ts"], st["ls"], bufs[nt:], st["names"]):
            if n[-1] in "01":
                self.halves[n[:-1]] = _final_sum(t, l, r, self.place, f"final_sum_{n}", layer=int(n[-1]), layers=2,
                                                 into=self.halves.get(n[:-1]))
            else:
                self.halves[n] = _final_sum(t, l, r, self.place, f"final_sum_{n}")


def _allgather_small(v):
    rows = v.shape[0]

    def body(v_ref, out_ref, send_sems, recv_sems):
        x, y, c, _ = _place()
        me = 4 * x + 2 * y + c
        out_ref[me] = v_ref[...]
        cps = []
        for k in range(1, 8):
            fx, fy, fc = (k >> 2) & 1, (k >> 1) & 1, k & 1
            to = (1 - x if fx else x, 1 - y if fy else y, 1 - c if fc else c)
            cps.append(pltpu.make_async_remote_copy(
                src_ref=v_ref, dst_ref=out_ref.at[me], send_sem=send_sems.at[k - 1], recv_sem=recv_sems.at[k - 1],
                device_id=to, device_id_type=MESH))
        for cp in cps:
            cp.start()
        for cp in cps:
            cp.wait()

    return _pc(
        body, name="allgather_small",
        in_specs=[pl.BlockSpec(memory_space=pltpu.VMEM)], out_specs=pl.BlockSpec(memory_space=pltpu.VMEM),
        out_shape=jax.ShapeDtypeStruct((8, rows, LANES), F32),
        scratch_shapes=[pltpu.SemaphoreType.DMA((7,)), pltpu.SemaphoreType.DMA((7,))],
    )(v)


def _adamw_math(w, g, m, v):
    m = ADAM_B1 * m + (1.0 - ADAM_B1) * g
    v = ADAM_B2 * v + (1.0 - ADAM_B2) * (g * g)
    m_hat = m / (1.0 - ADAM_B1 ** ADAM_STEP)
    v_hat = v / (1.0 - ADAM_B2 ** ADAM_STEP)
    return -ADAM_LR * (m_hat / (jnp.sqrt(v_hat) + ADAM_EPS) + ADAM_WD * w), m, v


def _adamw(w, g, m, v, name):
    r, cols = w.shape
    tr = min(r, 256)

    def body(w_ref, g_ref, m_ref, v_ref, go_ref, d_ref, mo_ref, vo_ref):
        gv = g_ref[...]
        d, mn, vn = _adamw_math(w_ref[...], gv, m_ref[...], v_ref[...])
        go_ref[...] = gv
        d_ref[...] = d
        mo_ref[...] = mn
        vo_ref[...] = vn

    row = pl.BlockSpec((tr, cols), lambda i: (i, 0))
    return _pc(
        body, name=name, grid=(r // tr,), in_specs=[row] * 4, out_specs=[row] * 4,
        out_shape=[jax.ShapeDtypeStruct((r, cols), F32)] * 4,
        compiler_params=_params(("parallel",)),
    )(w, g, m, v)


def _adamw_small(w, gathered, m, v):
    rows = w.shape[0]

    def body(w_ref, g_ref, m_ref, v_ref, go_ref, d_ref, mo_ref, vo_ref):
        g = g_ref[0]
        for k in range(1, 8):
            g = g + g_ref[k]
        d, mn, vn = _adamw_math(w_ref[...], g, m_ref[...], v_ref[...])
        go_ref[...] = g
        d_ref[...] = d
        mo_ref[...] = mn
        vo_ref[...] = vn

    return _pc(
        body, name="adamw_small",
        out_shape=[jax.ShapeDtypeStruct((rows, LANES), F32)] * 4,
    )(w, gathered, m, v)


_BIAS_ROWS = 32


def _own_slot(flat, chip):
    return lax.dynamic_update_slice(lax.empty((N_CHIPS,) + flat.shape, flat.dtype), flat[None], (chip, 0, 0))


def _pack_first(hyb_w_in, hyb_w_out):
    return jnp.concatenate([t.astype(BF16).reshape(-1, 1024) for t in (hyb_w_in, hyb_w_out)], axis=0)


def _unpack_first(g):
    return {"hyb_w_in": g[:, 0:768, :].reshape(N_CHIPS, 1024, 768), "hyb_w_out": g[:, 768:1024, :].reshape(1024, 1024)}


def _pack_rest(mlp_w_up, mlp_w_down, swa_w_qkv, swa_w_out, swa_b_qkv):
    parts = [t.astype(BF16).reshape(-1, 1024) for t in (mlp_w_up, mlp_w_down, swa_w_qkv, swa_w_out)]
    bias = lax.bitcast_convert_type(swa_b_qkv.reshape(384), BF16).reshape(1, 768)
    bias = jnp.pad(bias, ((0, _BIAS_ROWS - 1), (0, 256)))
    return jnp.concatenate(parts + [bias], axis=0)


def _unpack_rest(g):
    W = {
        "packed": g,
        "swa_w_qkv": g[:, 4096:4480, :].reshape(N_CHIPS, 1024, 384),
        "swa_w_out": g[:, 4480:4736, :].reshape(1024, 1024),
    }
    bias = lax.bitcast_convert_type(g[:, 4736, :768].reshape(N_CHIPS, 384, 2), F32).reshape(1536)
    return W, bias


_SMALL = (("norm_mix", 16), ("norm_mlp", 16), ("ret_gn_gain", 4), ("dil_q_gain", 1), ("dil_k_gain", 1),
          ("swa_b_qkv", 12), ("swa_q_gain", 1), ("swa_k_gain", 1), ("swa_sinks", 1), ("loss", 1))
_SUBLANES = 8


def _slot(r):
    return -(-r // _SUBLANES) * _SUBLANES


def _pack_small(d):
    return jnp.concatenate([jnp.pad(d[n].reshape(r, LANES), ((0, _slot(r) - r), (0, 0))) for n, r in _SMALL], axis=0)


def _unpack_small(p):
    out, o = {}, 0
    for n, r in _SMALL:
        out[n] = p[o:o + r]
        o += _slot(r)
    return out


def kernel(x, positions, norm_mix, norm_mlp, mlp_w_up, mlp_w_down, hyb_w_in, hyb_w_out, ret_gn_gain, dil_q_gain, dil_k_gain, swa_w_qkv, swa_b_qkv, swa_w_out, swa_q_gain, swa_k_gain, swa_sinks, loss_target, m_norm_mix, m_norm_mlp, m_mlp_w_up, m_mlp_w_down, m_hyb_w_in, m_hyb_w_out, m_ret_gn_gain, m_dil_q_gain, m_dil_k_gain, m_swa_w_qkv, m_swa_b_qkv, m_swa_w_out, m_swa_q_gain, m_swa_k_gain, m_swa_sinks, v_norm_mix, v_norm_mlp, v_mlp_w_up, v_mlp_w_down, v_hyb_w_in, v_hyb_w_out, v_ret_gn_gain, v_dil_q_gain, v_dil_k_gain, v_swa_w_qkv, v_swa_b_qkv, v_swa_w_out, v_swa_q_gain, v_swa_k_gain, v_swa_sinks):
    ax, ay, ac = lax.axis_index("x"), lax.axis_index("y"), lax.axis_index("c")
    chip = 2 * ax + ay
    place = jnp.stack([chip, ac]).astype(jnp.int32)
    S = x.shape[1]

    first = _own_slot(_pack_first(hyb_w_in[0], hyb_w_out[0]), chip)
    rest = _own_slot(_pack_rest(mlp_w_up, mlp_w_down, swa_w_qkv[0], swa_w_out[0], swa_b_qkv[0]), chip)
    *sems, first, pos_col = _gather_start(first, positions.reshape(S, 1), "allgather_first_start")
    flight = {}

    def first_of(after):
        g = _gather_handover(_gather_wait(first, sems, after, "allgather_first_wait"), "allgather_first_handover")
        *flight["sems"], flight["buf"], g = _gather_start(rest, g, "allgather_rest_start")
        return _unpack_first(g)

    def rest_begin(ride):
        buf = _gather_wait(flight["buf"], flight["sems"], ride, "allgather_rest_wait")
        flight["sems"], flight["bufs"], ride = _split_start("allgather_rest_handover_start", [buf], ride, 3, _handover_copies)
        return ride

    def rest_of(after):
        return _unpack_rest(_split_wait("allgather_rest_handover_wait", flight["bufs"], flight["sems"], after, 3,
                                        _handover_copies)[0])

    P = dict(norm_mix=norm_mix, norm_mlp=norm_mlp, ret_gn_gain=ret_gn_gain, dil_q_gain=dil_q_gain, dil_k_gain=dil_k_gain,
             swa_q_gain=swa_q_gain, swa_k_gain=swa_k_gain, swa_sinks=swa_sinks)

    red = _StagedReduce(place)
    loss_l, grad_x, gp = _local_step(x[0], pos_col, loss_target[0], first_of, rest_begin, rest_of, P, red)

    params = dict(mlp_w_up=(mlp_w_up, m_mlp_w_up, v_mlp_w_up), mlp_w_down=(mlp_w_down, m_mlp_w_down, v_mlp_w_down),
                  hyb_w_in=(hyb_w_in, m_hyb_w_in, v_hyb_w_in), hyb_w_out=(hyb_w_out, m_hyb_w_out, v_hyb_w_out),
                  swa_w_qkv=(swa_w_qkv, m_swa_w_qkv, v_swa_w_qkv), swa_w_out=(swa_w_out, m_swa_w_out, v_swa_w_out))
    big = {}

    def update(names, share_name):
        hs = [(red.halves[n], params[n][0].shape[0]) for n in names]
        for n, g in zip(names, _share_halves(hs, share_name)):
            rows = g.shape[0]
            w, m, v = (t.reshape(rows, -1) for t in params[n])
            big[n] = [t.reshape(params[n][0].shape) for t in _adamw(w, g, m, v, f"adamw_{n}")]

    names = ["mlp_w_up", "mlp_w_down", "hyb_w_out", "swa_w_qkv", "swa_w_out"]
    red.halves[names[0]] = red.advance("win", grad_x, red.halves[names[0]])
    update(names, "grad_share_halves")
    red.finish("win", big[names[-1]][1])
    update(["hyb_w_in"], "grad_share_last")

    gsm = dict(gp, loss=loss_l)
    gsm["swa_sinks"] = jnp.pad(gp["swa_sinks"].reshape(16, HEAD)[:, 0], (0, LANES - 16))
    gathered = _allgather_small(_pack_small(gsm))

    def small_pack(norm_mix, norm_mlp, gn, dq, dk, b, sq, sk, sinks):
        dup = lambda t: jnp.tile(t.reshape(1, HEAD), (1, 2))
        bias = lax.dynamic_update_slice(jnp.zeros((12, LANES), F32), b.reshape(3, LANES), (3 * chip, 0))
        return _pack_small(dict(norm_mix=norm_mix, norm_mlp=norm_mlp, ret_gn_gain=gn, dil_q_gain=dup(dq), dil_k_gain=dup(dk),
                                swa_b_qkv=bias, swa_q_gain=dup(sq), swa_k_gain=dup(sk),
                                swa_sinks=jnp.pad(sinks.reshape(16), (0, LANES - 16)), loss=jnp.zeros((1, LANES), F32)))

    pw = small_pack(norm_mix, norm_mlp, ret_gn_gain, dil_q_gain, dil_k_gain, swa_b_qkv, swa_q_gain, swa_k_gain, swa_sinks)
    pm = small_pack(m_norm_mix, m_norm_mlp, m_ret_gn_gain, m_dil_q_gain, m_dil_k_gain, m_swa_b_qkv, m_swa_q_gain, m_swa_k_gain, m_swa_sinks)
    pv = small_pack(v_norm_mix, v_norm_mlp, v_ret_gn_gain, v_dil_q_gain, v_dil_k_gain, v_swa_b_qkv, v_swa_q_gain, v_swa_k_gain, v_swa_sinks)
    small = [_unpack_small(t) for t in _adamw_small(pw, gathered, pm, pv)]

    def small_out(n, k):
        t = small[k][n]
        if n in ("norm_mix", "norm_mlp"):
            return t.reshape(2, D_MODEL)
        if n == "ret_gn_gain":
            return t.reshape(1, RET_HEADS, 128)
        if n == "swa_b_qkv":
            return lax.dynamic_slice(t, (3 * chip, 0), (3, LANES)).reshape(1, 384)
        if n == "swa_sinks":
            return t[0, :16].reshape(1, 16)
        return t[0, :HEAD].reshape(1, HEAD)

    order = ["norm_mix", "norm_mlp", "mlp_w_up", "mlp_w_down", "hyb_w_in", "hyb_w_out", "ret_gn_gain", "dil_q_gain",
             "dil_k_gain", "swa_w_qkv", "swa_b_qkv", "swa_w_out", "swa_q_gain", "swa_k_gain", "swa_sinks"]
    is_big = {"mlp_w_up", "mlp_w_down", "hyb_w_in", "hyb_w_out", "swa_w_qkv", "swa_w_out"}
    outs = [small[0]["loss"][0, 0], grad_x[None]]
    for k in range(4):
        outs += [big[n][k] if n in is_big else small_out(n, k) for n in order]
    return tuple(outs)
```

```python
import functools
import math

import numpy as np
import jax
import jax.numpy as jnp
from jax import lax
from jax.experimental import pallas as pl
from jax.experimental.pallas import tpu as pltpu

F32, BF16 = jnp.float32, jnp.bfloat16
HIGHEST = lax.Precision.HIGHEST
MESH = pl.DeviceIdType.MESH

LANES = 128
VMEM_LIMIT = 48 << 20
D_MODEL = 1024
D_FF = 4096
HEAD = 64
EPS = 1e-6
BLK = 128
RET_HEADS = 4
RET_THETA = 10000.0
ROPE_THETA = 500000.0
ROPE_DIMS = 16
DIL_PATTERNS = ((128, 1), (512, 4), (2048, 16))
SWA_DIST = 127
N_CHIPS = 4
ADAM_LR, ADAM_B1, ADAM_B2, ADAM_EPS, ADAM_WD, ADAM_STEP = 0.001, 0.9, 0.999, 1e-08, 0.01, 10

_LOG_GAMMA = [float(np.log1p(-np.exp2(np.float32(-5.0 - h)))) for h in range(RET_HEADS)]


def _pc(body, **kw):
    return pl.pallas_call(body, **kw)


def _params(sem):
    return pltpu.CompilerParams(dimension_semantics=sem, vmem_limit_bytes=VMEM_LIMIT)


def _matmul(a, b, *, dims, tm, tn, tk, outs, name, epilogue=None, extras=(), b_cs=False, b_rs=0, b_row0=0, b_rows=0,
            o_cs=0, a_pro=None):
    if dims == "nn":
        M, K = a.shape
        N = b.shape[0] * b.shape[2] if b_cs else b.shape[1]
        a_spec = pl.BlockSpec((tm, tk), lambda i, j, k: (i, k))
        if b_cs:
            npt = b.shape[2] // tn
            b_spec = pl.BlockSpec((None, tk, tn), lambda i, j, k: (j // npt, k + b_row0, j % npt))
        elif b_rs:
            K, N, kps = b.shape[0] * b_rs, b.shape[2], b_rs // tk
            b_spec = pl.BlockSpec((None, tk, tn), lambda i, j, k: (k // kps, b_row0 + k % kps, j))
        else:
            b_spec = pl.BlockSpec((tk, tn), lambda i, j, k: (k, j))
        contract = (((1,), (0,)), ((), ()))
    elif dims == "nt":
        M, K = a.shape
        N = (b_rows or b.shape[1]) if b_cs else b.shape[0]
        a_spec = pl.BlockSpec((tm, tk), lambda i, j, k: (i, k))
        if b_cs:
            kpt = b.shape[2] // tk
            b_spec = pl.BlockSpec((None, tn, tk), lambda i, j, k: (k // kpt, j + b_row0, k % kpt))
        elif b_rs:
            N, jps = b.shape[0] * b_rs, b_rs // tn
            b_spec = pl.BlockSpec((None, tn, tk), lambda i, j, k: (j // jps, b_row0 + j % jps, k))
        else:
            b_spec = pl.BlockSpec((tn, tk), lambda i, j, k: (j, k))
        contract = (((1,), (1,)), ((), ()))
    else:
        K, M = a.shape
        N = b.shape[1]
        a_spec = pl.BlockSpec((tk, tm), lambda i, j, k: (k, i))
        b_spec = pl.BlockSpec((tk, tn), lambda i, j, k: (k, j))
        contract = (((0,), (0,)), ((), ()))
    assert M % tm == 0 and N % tn == 0 and K % tk == 0, (name, M, N, K, tm, tn, tk)
    nk = K // tk
    ex_specs = []
    for arr, kind in extras:
        if kind == "mn":
            ex_specs.append(pl.BlockSpec((tm, tn), lambda i, j, k: (i, j)))
        elif kind == "n":
            ex_specs.append(pl.BlockSpec((1, tn), lambda i, j, k: (0, j)))
        elif kind == "full":
            ex_specs.append(pl.BlockSpec(arr.shape, lambda i, j, k, nd=arr.ndim: (0,) * nd))
        else:
            ex_specs.append(pl.BlockSpec((tm, kind), lambda i, j, k: (i, 0)))
    if o_cs:
        n_sh = N // o_cs
        opt = n_sh // tn
        o_shape = (o_cs, M, n_sh)
        o_spec = pl.BlockSpec((None, tm, tn), lambda i, j, k: (j // opt, i, j % opt))
    else:
        o_shape = (M, N)
        o_spec = pl.BlockSpec((tm, tn), lambda i, j, k: (i, j))
    o_specs, o_shapes, summed = [], [], []
    for o in outs:
        if isinstance(o, tuple) and o[0] == "colsum":
            assert N == tn
            o_specs.append(pl.BlockSpec((1, tn), lambda i, j, k: (0, j)))
            o_shapes.append(jax.ShapeDtypeStruct((1, N), F32))
            summed.append(True)
        elif isinstance(o, tuple):
            o_specs.append(pl.BlockSpec((tm, o[1]), lambda i, j, k: (i, 0)))
            o_shapes.append(jax.ShapeDtypeStruct((M, o[1]), o[0]))
            summed.append(False)
        else:
            o_specs.append(o_spec)
            o_shapes.append(jax.ShapeDtypeStruct(o_shape, o))
            summed.append(False)
    n_ex, n_out = len(extras), len(outs)
    if epilogue is None:
        epilogue = lambda acc: (acc,)

    def body(a_ref, b_ref, *rest):
        ex, o_refs, acc = rest[:n_ex], rest[n_ex:n_ex + n_out], rest[-1]
        i, k = pl.program_id(0), pl.program_id(2)

        @pl.when(k == 0)
        def _():
            acc[...] = jnp.zeros_like(acc)

        av = a_ref[...] if a_pro is None else a_pro(a_ref[...])
        acc[...] += lax.dot_general(av.astype(BF16), b_ref[...].astype(BF16), contract, preferred_element_type=F32)

        @pl.when(k == nk - 1)
        def _():
            vals = epilogue(acc[...], *[e[...] for e in ex])
            for r, v, sm in zip(o_refs, vals, summed):
                if sm:
                    @pl.when(i == 0)
                    def _(r=r):
                        r[...] = jnp.zeros_like(r)

                    r[...] += v
                else:
                    r[...] = v.astype(r.dtype)

    res = _pc(
        body, name=name, grid=(M // tm, N // tn, nk),
        in_specs=[a_spec, b_spec] + ex_specs, out_specs=o_specs, out_shape=o_shapes,
        scratch_shapes=[pltpu.VMEM((tm, tn), F32)],
        compiler_params=_params(("arbitrary" if any(summed) else "parallel", "parallel", "arbitrary")),
    )(a, b, *[e for e, _ in extras])
    return res[0] if n_out == 1 else res


def _roll(x, s):
    return pltpu.roll(x, s % LANES, 1)


def _rope(x, A, B, C, half):
    return x * A + _roll(x, LANES - half) * B + _roll(x, half) * C


def _rope_t(g, A, B, C, half):
    return g * A + _roll(g * B, half) + _roll(g * C, LANES - half)


def _gmean(x, G):
    hi = x.astype(BF16)
    lo = (x - hi.astype(F32)).astype(BF16)
    Gb = G.astype(BF16)
    return jnp.dot(hi, Gb, preferred_element_type=F32) + jnp.dot(lo, Gb, preferred_element_type=F32)


def _head_mask(shape, half):
    lane = lax.broadcasted_iota(jnp.int32, shape, len(shape) - 1)
    return (lane >= HEAD) if half else (lane < HEAD)


def _group_matrix():
    i = np.arange(LANES)
    return jnp.asarray((i[:, None] // HEAD == i[None, :] // HEAD).astype(np.float32) / HEAD)


def _rope_inv():
    l = np.arange(LANES) % HEAD
    inv_r = np.power(np.float32(RET_THETA), -(l % 32).astype(np.float32) * np.float32(2.0 / HEAD))
    hp = ROPE_DIMS // 2
    inv_p = np.power(np.float32(ROPE_THETA), -(l % hp).astype(np.float32) * np.float32(2.0 / ROPE_DIMS))
    inv_p = np.where(l < ROPE_DIMS, inv_p, 0.0)
    return jnp.asarray(np.stack([inv_r, inv_p]).astype(np.float32))


def _tables(pos_col):
    S = pos_col.shape[0]
    tm = 512
    hp = ROPE_DIMS // 2

    def body(p_ref, inv_ref, o_ref):
        p = p_ref[...].astype(F32)
        lane = lax.broadcasted_iota(jnp.int32, (tm, LANES), 1) % HEAD
        ang = p * inv_ref[0:1, :]
        c, s = jnp.cos(ang), jnp.sin(ang)
        o_ref[:, 0:128] = c
        o_ref[:, 128:256] = jnp.where(lane < 32, -s, 0.0)
        o_ref[:, 256:384] = jnp.where(lane >= 32, s, 0.0)
        ang = p * inv_ref[1:2, :]
        c, s = jnp.cos(ang), jnp.sin(ang)
        o_ref[:, 384:512] = c
        o_ref[:, 512:640] = jnp.where(lane < hp, -s, 0.0)
        o_ref[:, 640:768] = jnp.where((lane >= hp) & (lane < ROPE_DIMS), s, 0.0)

    return _pc(
        body, name="rope_tables", grid=(S // tm,),
        in_specs=[pl.BlockSpec((tm, 1), lambda i: (i, 0)), pl.BlockSpec((2, LANES), lambda i: (0, 0))],
        out_specs=pl.BlockSpec((tm, 768), lambda i: (i, 0)),
        out_shape=jax.ShapeDtypeStruct((S, 768), F32),
        compiler_params=_params(("parallel",)),
    )(pos_col, _rope_inv())


def _tab(tab_ref, which):
    o = 384 * which
    return tab_ref[:, o:o + 128], tab_ref[:, o + 128:o + 256], tab_ref[:, o + 256:o + 384]


def _rms_fwd(x, g, name):
    S, Dm = x.shape
    tm = 512

    def body(x_ref, g_ref, h_ref):
        xv = x_ref[...]
        r = lax.rsqrt(jnp.mean(xv * xv, axis=-1, keepdims=True) + EPS)
        h_ref[...] = (xv * r * g_ref[...]).astype(BF16)

    return _pc(
        body, name=name, grid=(S // tm,),
        in_specs=[pl.BlockSpec((tm, Dm), lambda i: (i, 0)), pl.BlockSpec((1, Dm), lambda i: (0, 0))],
        out_specs=pl.BlockSpec((tm, Dm), lambda i: (i, 0)),
        out_shape=jax.ShapeDtypeStruct((S, Dm), BF16),
        compiler_params=_params(("parallel",)),
    )(x, g.reshape(1, Dm))


def _rms_bwd(x, g, dh, dres, name):
    S, Dm = x.shape
    tm = 512

    def body(x_ref, g_ref, dh_ref, dres_ref, dx_ref, dxb_ref, dg_ref):
        xv, dhv = x_ref[...], dh_ref[...]
        r = lax.rsqrt(jnp.mean(xv * xv, axis=-1, keepdims=True) + EPS)
        t = dhv * g_ref[...]
        dx = dres_ref[...] + r * t - xv * (r * r * r) * jnp.mean(xv * t, axis=-1, keepdims=True)
        dx_ref[...] = dx
        dxb_ref[...] = dx.astype(BF16)

        @pl.when(pl.program_id(0) == 0)
        def _():
            dg_ref[...] = jnp.zeros_like(dg_ref)

        dg_ref[...] += jnp.sum(dhv * xv * r, axis=0, keepdims=True)

    row = pl.BlockSpec((tm, Dm), lambda i: (i, 0))
    vec = pl.BlockSpec((1, Dm), lambda i: (0, 0))
    return _pc(
        body, name=name, grid=(S // tm,),
        in_specs=[row, vec, row, row], out_specs=[row, row, vec],
        out_shape=[jax.ShapeDtypeStruct((S, Dm), F32), jax.ShapeDtypeStruct((S, Dm), BF16),
                   jax.ShapeDtypeStruct((1, Dm), F32)],
        compiler_params=_params(("arbitrary",)),
    )(x, g.reshape(1, Dm), dh, dres)


def _hn_fwd(x, gain, G):
    r = lax.rsqrt(_gmean(x * x, G) + EPS)
    return x * r * gain


def _hn_bwd(x, gain, dy, G):
    r = lax.rsqrt(_gmean(x * x, G) + EPS)
    t = dy * gain
    dx = r * t - x * (r * r * r) * _gmean(x * t, G)
    return dx, jnp.sum(dy * x * r, axis=0, keepdims=True)


def _fold_halves(v):
    return v + _roll(v, HEAD)


def _even_pre_fwd(proj, tab, qg, kg):
    S = proj.shape[0]
    tm = 256

    def body(p_ref, tab_ref, qg_ref, kg_ref, g_ref, rq_ref, rk_ref, rv_ref, dq_ref, dk_ref, dv_ref):
        Ar, Br, Cr = _tab(tab_ref, 0)
        Ap, Bp, Cp = _tab(tab_ref, 1)
        G = g_ref[...]
        for c in range(2):
            sl = slice(c * 128, (c + 1) * 128)
            rq_ref[:, sl] = _rope(p_ref[:, c * 128:(c + 1) * 128], Ar, Br, Cr, 32).astype(BF16)
            rk_ref[:, sl] = (_rope(p_ref[:, 256 + c * 128:256 + (c + 1) * 128], Ar, Br, Cr, 32) * 0.125).astype(BF16)
        rv_ref[...] = p_ref[:, 512:1024].astype(BF16)
        for c in range(4):
            sl = slice(c * 128, (c + 1) * 128)
            q = _hn_fwd(p_ref[:, 1536 + c * 128:1536 + (c + 1) * 128], qg_ref[...], G)
            dq_ref[:, sl] = _rope(q, Ap, Bp, Cp, 8).astype(BF16)
            k = _hn_fwd(p_ref[:, 2048 + c * 128:2048 + (c + 1) * 128], kg_ref[...], G)
            dk_ref[:, sl] = _rope(k, Ap, Bp, Cp, 8).astype(BF16)
        dv_ref[...] = p_ref[:, 2560:3072].astype(BF16)

    row = lambda w: pl.BlockSpec((tm, w), lambda i: (i, 0))
    vec = pl.BlockSpec((1, LANES), lambda i: (0, 0))
    return _pc(
        body, name="even_pre_fwd", grid=(S // tm,),
        in_specs=[row(3072), row(768), vec, vec, pl.BlockSpec((LANES, LANES), lambda i: (0, 0))],
        out_specs=[row(256), row(256), row(512), row(512), row(512), row(512)],
        out_shape=[jax.ShapeDtypeStruct((S, w), BF16) for w in (256, 256, 512, 512, 512, 512)],
        compiler_params=_params(("parallel",)),
    )(proj, tab, qg, kg, _group_matrix())


def _even_pre_bwd(proj, tab, qg, kg, drq, drk, drv, drg, dqs, dks, dvs):
    S = proj.shape[0]
    tm = 256
    npat = len(dqs)

    def body(p_ref, tab_ref, qg_ref, kg_ref, g_ref, drq_ref, drk_ref, drv_ref, drg_ref, *rest):
        dq_refs, dk_refs, dv_refs = rest[:npat], rest[npat:2 * npat], rest[2 * npat:3 * npat]
        dp_ref, dqg_ref, dkg_ref = rest[3 * npat:]
        Ar, Br, Cr = _tab(tab_ref, 0)
        Ap, Bp, Cp = _tab(tab_ref, 1)
        G = g_ref[...]
        for c in range(2):
            sl = slice(c * 128, (c + 1) * 128)
            dp_ref[:, c * 128:(c + 1) * 128] = _rope_t(drq_ref[:, sl], Ar, Br, Cr, 32).astype(BF16)
            dp_ref[:, 256 + c * 128:256 + (c + 1) * 128] = _rope_t(drk_ref[:, sl] * 0.125, Ar, Br, Cr, 32).astype(BF16)
        dp_ref[:, 512:1024] = drv_ref[...].astype(BF16)
        dp_ref[:, 1024:1536] = drg_ref[...].astype(BF16)
        accq = jnp.zeros((1, LANES), F32)
        acck = jnp.zeros((1, LANES), F32)
        for c in range(4):
            sl = slice(c * 128, (c + 1) * 128)
            g = dq_refs[0][:, sl]
            for r in dq_refs[1:]:
                g = g + r[:, sl]
            dx, dg = _hn_bwd(p_ref[:, 1536 + c * 128:1536 + (c + 1) * 128], qg_ref[...], _rope_t(g, Ap, Bp, Cp, 8), G)
            dp_ref[:, 1536 + c * 128:1536 + (c + 1) * 128] = dx.astype(BF16)
            accq = accq + dg
            g = dk_refs[0][:, sl]
            for r in dk_refs[1:]:
                g = g + r[:, sl]
            dx, dg = _hn_bwd(p_ref[:, 2048 + c * 128:2048 + (c + 1) * 128], kg_ref[...], _rope_t(g, Ap, Bp, Cp, 8), G)
            dp_ref[:, 2048 + c * 128:2048 + (c + 1) * 128] = dx.astype(BF16)
            acck = acck + dg
        g = dv_refs[0][...]
        for r in dv_refs[1:]:
            g = g + r[...]
        dp_ref[:, 2560:3072] = g.astype(BF16)

        @pl.when(pl.program_id(0) == 0)
        def _():
            dqg_ref[...] = jnp.zeros_like(dqg_ref)
            dkg_ref[...] = jnp.zeros_like(dkg_ref)

        dqg_ref[...] += _fold_halves(accq)
        dkg_ref[...] += _fold_halves(acck)

    row = lambda w: pl.BlockSpec((tm, w), lambda i: (i, 0))
    vec = pl.BlockSpec((1, LANES), lambda i: (0, 0))
    return _pc(
        body, name="even_pre_bwd", grid=(S // tm,),
        in_specs=[row(3072), row(768), vec, vec, pl.BlockSpec((LANES, LANES), lambda i: (0, 0)),
                  row(256), row(256), row(512), row(512)] + [row(512)] * (3 * npat),
        out_specs=[row(3072), vec, vec],
        out_shape=[jax.ShapeDtypeStruct((S, 3072), BF16), jax.ShapeDtypeStruct((1, LANES), F32),
                   jax.ShapeDtypeStruct((1, LANES), F32)],
        compiler_params=_params(("arbitrary",)),
    )(proj, tab, qg, kg, _group_matrix(), drq, drk, drv, drg, *dqs, *dks, *dvs)


def _ret_consts(pair, half):
    lg = jnp.where(pair == 0, _LOG_GAMMA[half], _LOG_GAMMA[2 + half]).astype(F32)
    i = lax.broadcasted_iota(jnp.int32, (BLK, BLK), 0)
    j = lax.broadcasted_iota(jnp.int32, (BLK, BLK), 1)
    diff = (i - j).astype(F32)
    decay = jnp.where(diff >= 0, jnp.exp(lg * jnp.maximum(diff, 0.0)), 0.0)
    t = lax.broadcasted_iota(jnp.int32, (BLK, 1), 0).astype(F32)
    xi = jnp.exp(lg * (t + 1.0))
    zeta = jnp.exp(lg * (BLK - 1.0 - t))
    cd = jnp.exp(jnp.full((1, 1), BLK, F32) * lg)
    return decay, xi, zeta, cd


RET_STEP = 8


def _ret_fwd(rq, rk, rv):
    S = rq.shape[0]
    nc = S // BLK
    rows = RET_STEP * BLK

    def body(q_ref, k_ref, v_ref, o_ref, st_ref, R):
        p, n = pl.program_id(0), pl.program_id(1)

        @pl.when(n == 0)
        def _():
            R[...] = jnp.zeros_like(R)

        consts = [_ret_consts(p, half) for half in range(2)]
        masks = [_head_mask((BLK, LANES), half) for half in range(2)]
        for ci in range(RET_STEP):
            rs = slice(ci * BLK, (ci + 1) * BLK)
            q2, k2 = q_ref[rs, :], k_ref[rs, :]
            for half in range(2):
                decay, xi, zeta, cd = consts[half]
                m = masks[half]
                qm = jnp.where(m, q2, jnp.zeros_like(q2))
                km = jnp.where(m, k2, jnp.zeros_like(k2))
                v = v_ref[rs, half * 128:(half + 1) * 128]
                Rb = R[half].astype(BF16)
                st_ref[ci, half] = Rb
                sc = lax.dot_general(qm, k2, (((1,), (1,)), ((), ())), preferred_element_type=F32) * decay
                o = jnp.dot(sc.astype(BF16), v, preferred_element_type=F32)
                o = o + jnp.dot(qm, Rb, preferred_element_type=F32) * xi
                o_ref[rs, half * 128:(half + 1) * 128] = o
                kz = (km.astype(F32) * zeta).astype(BF16)
                R[half] = R[half] * cd + lax.dot_general(kz, v, (((0,), (0,)), ((), ())), preferred_element_type=F32)

    return _pc(
        body, name="ret_fwd", grid=(2, nc // RET_STEP),
        in_specs=[pl.BlockSpec((rows, 128), lambda p, n: (n, p)), pl.BlockSpec((rows, 128), lambda p, n: (n, p)),
                  pl.BlockSpec((rows, 256), lambda p, n: (n, p))],
        out_specs=[pl.BlockSpec((rows, 256), lambda p, n: (n, p)),
                   pl.BlockSpec((None, RET_STEP, 2, 128, 128), lambda p, n: (p, n, 0, 0, 0))],
        out_shape=[jax.ShapeDtypeStruct((S, 512), F32), jax.ShapeDtypeStruct((2, nc, 2, 128, 128), BF16)],
        scratch_shapes=[pltpu.VMEM((2, 128, 128), F32)],
        compiler_params=_params(("parallel", "arbitrary")),
    )(rq, rk, rv)


def _ret_bwd(rq, rk, rv, states, do):
    S = rq.shape[0]
    nc = S // BLK
    ns = nc // RET_STEP
    rows = RET_STEP * BLK
    nt = (((1,), (1,)), ((), ()))
    tn = (((0,), (0,)), ((), ()))

    def body(q_ref, k_ref, v_ref, st_ref, do_ref, dq_ref, dk_ref, dv_ref, U):
        p, n = pl.program_id(0), pl.program_id(1)

        @pl.when(n == 0)
        def _():
            U[...] = jnp.zeros_like(U)

        consts = [_ret_consts(p, half) for half in range(2)]
        masks = [_head_mask((BLK, LANES), half) for half in range(2)]
        for ci in reversed(range(RET_STEP)):
            rs = slice(ci * BLK, (ci + 1) * BLK)
            q2, k2 = q_ref[rs, :], k_ref[rs, :]
            dq_acc = jnp.zeros((BLK, LANES), F32)
            dk_acc = jnp.zeros((BLK, LANES), F32)
            for half in range(2):
                decay, xi, zeta, cd = consts[half]
                m = masks[half]
                qm = jnp.where(m, q2, jnp.zeros_like(q2))
                km = jnp.where(m, k2, jnp.zeros_like(k2))
                v = v_ref[rs, half * 128:(half + 1) * 128]
                dob = do_ref[rs, half * 128:(half + 1) * 128].astype(BF16)
                Rb = st_ref[ci, half]
                Ub = U[half].astype(BF16)
                dsc = (lax.dot_general(dob, v, nt, preferred_element_type=F32) * decay).astype(BF16)
                xdo = (dob.astype(F32) * xi).astype(BF16)
                dq_acc += jnp.dot(dsc, km, preferred_element_type=F32) + lax.dot_general(xdo, Rb, nt, preferred_element_type=F32)
                dk_acc += lax.dot_general(dsc, qm, tn, preferred_element_type=F32) \
                    + lax.dot_general(v, Ub, nt, preferred_element_type=F32) * zeta
                sc = (lax.dot_general(qm, k2, nt, preferred_element_type=F32) * decay).astype(BF16)
                kz = (km.astype(F32) * zeta).astype(BF16)
                dv_ref[rs, half * 128:(half + 1) * 128] = lax.dot_general(sc, dob, tn, preferred_element_type=F32) \
                    + jnp.dot(kz, Ub, preferred_element_type=F32)
                U[half] = U[half] * cd + lax.dot_general(qm, xdo, tn, preferred_element_type=F32)
            dq_ref[rs, :] = dq_acc
            dk_ref[rs, :] = dk_acc

    rev = lambda w: pl.BlockSpec((rows, w), lambda p, n: (ns - 1 - n, p))
    return _pc(
        body, name="ret_bwd", grid=(2, ns),
        in_specs=[rev(128), rev(128), rev(256),
                  pl.BlockSpec((None, RET_STEP, 2, 128, 128), lambda p, n: (p, ns - 1 - n, 0, 0, 0)), rev(256)],
        out_specs=[rev(128), rev(128), rev(256)],
        out_shape=[jax.ShapeDtypeStruct((S, 256), F32), jax.ShapeDtypeStruct((S, 256), F32),
                   jax.ShapeDtypeStruct((S, 512), F32)],
        scratch_shapes=[pltpu.VMEM((2, 128, 128), F32)],
        compiler_params=_params(("parallel", "arbitrary")),
    )(rq, rk, rv, states, do)


def _col_of(b, m):
    return jnp.max(jnp.where(m, b, -jnp.inf), axis=1, keepdims=True)


def _attn_fwd(q, k, v, *, nq, max_dist, name, sinks=None, want_bf16=False):
    L, Ck = k.shape
    nb, ncol = L // BLK, Ck // LANES
    scale = HEAD ** -0.5
    has_sink = sinks is not None

    def body(*refs):
        q_ref, kp_ref, kc_ref, vp_ref, vc_ref = refs[:5]
        sk_ref = refs[5] if has_sink else None
        outs = refs[5 + has_sink:]
        n = pl.program_id(1)
        kcat = jnp.concatenate([kp_ref[...], kc_ref[...]], axis=0)
        vcat = jnp.concatenate([vp_ref[...], vc_ref[...]], axis=0)
        r = lax.broadcasted_iota(jnp.int32, (BLK, 2 * BLK), 0)
        c = lax.broadcasted_iota(jnp.int32, (BLK, 2 * BLK), 1)
        dist = r + BLK - c
        valid = (dist >= 0) & (dist <= max_dist) & ((c >= BLK) | (n > 0))
        for i in range(nq):
            q2 = q_ref[:, i * 128:(i + 1) * 128]
            o2 = jnp.zeros((BLK, LANES), F32)
            l2 = jnp.zeros((BLK, LANES), F32)
            for half in range(2):
                m = _head_mask((BLK, LANES), half)
                qm = jnp.where(m, q2, jnp.zeros_like(q2))
                s = lax.dot_general(qm, kcat, (((1,), (1,)), ((), ())), preferred_element_type=F32) * scale
                s = jnp.where(valid, s, -jnp.inf)
                mx = jnp.max(s, axis=1, keepdims=True)
                if has_sink:
                    snk = _col_of(sk_ref[:, i * 128:(i + 1) * 128], _head_mask((1, LANES), half))
                    mx = jnp.maximum(mx, snk)
                pr = jnp.exp(s - mx)
                den = jnp.sum(pr, axis=1, keepdims=True)
                if has_sink:
                    den = den + jnp.exp(snk - mx)
                pv = jnp.dot(pr.astype(BF16), vcat, preferred_element_type=F32) / den
                o2 = jnp.where(m, pv, o2)
                l2 = jnp.where(m, mx + jnp.log(den), l2)
            outs[0][:, i * 128:(i + 1) * 128] = o2
            outs[1][:, i * 128:(i + 1) * 128] = l2
            if want_bf16:
                outs[2][:, i * 128:(i + 1) * 128] = o2.astype(BF16)

    qspec = pl.BlockSpec((BLK, nq * 128), lambda j, n: (n, j))
    cur = pl.BlockSpec((BLK, 128), lambda j, n: (n, j))
    prev = pl.BlockSpec((BLK, 128), lambda j, n: (jnp.maximum(n - 1, 0), j))
    in_specs = [qspec, prev, cur, prev, cur]
    args = [q, k, k, v, v]
    if has_sink:
        in_specs.append(pl.BlockSpec((1, nq * 128), lambda j, n: (0, j)))
        args.append(sinks)
    out_dts = [F32, F32] + ([BF16] if want_bf16 else [])
    return _pc(
        body, name=name, grid=(ncol, nb), in_specs=in_specs,
        out_specs=[qspec] * len(out_dts),
        out_shape=[jax.ShapeDtypeStruct(q.shape, dt) for dt in out_dts],
        compiler_params=_params(("parallel", "parallel")),
    )(*args)


def _attn_bwd(q, k, v, o, lse, do, *, nq, max_dist, name, sinks=None):
    L, Ck = k.shape
    nb, ncol = L // BLK, Ck // LANES
    scale = HEAD ** -0.5
    has_sink = sinks is not None
    nt = (((1,), (1,)), ((), ()))
    tn = (((0,), (0,)), ((), ()))

    def body(*refs):
        (qc_ref, qn_ref, kp_ref, kc_ref, vp_ref, vc_ref, oc_ref, on_ref, lc_ref, ln_ref, dc_ref, dn_ref) = refs[:12]
        sk_ref = refs[12] if has_sink else None
        outs = refs[12 + has_sink:]
        dq_ref, dk_ref, dv_ref = outs[:3]
        n = pl.program_id(1)
        kc, vc = kc_ref[...], vc_ref[...]
        kcat = jnp.concatenate([kp_ref[...], kc], axis=0)
        vcat = jnp.concatenate([vp_ref[...], vc], axis=0)
        r = lax.broadcasted_iota(jnp.int32, (BLK, 2 * BLK), 0)
        c = lax.broadcasted_iota(jnp.int32, (BLK, 2 * BLK), 1)
        dist = r + BLK - c
        valid_q = (dist >= 0) & (dist <= max_dist) & ((c >= BLK) | (n > 0))
        r2 = lax.broadcasted_iota(jnp.int32, (2 * BLK, BLK), 0)
        c2 = lax.broadcasted_iota(jnp.int32, (2 * BLK, BLK), 1)
        dist2 = r2 - c2
        valid_k = (dist2 >= 0) & (dist2 <= max_dist) & ((r2 < BLK) | (n < nb - 1))
        dk_acc = jnp.zeros((BLK, LANES), F32)
        dv_acc = jnp.zeros((BLK, LANES), F32)
        for i in range(nq):
            sl = slice(i * 128, (i + 1) * 128)
            qcur, docur = qc_ref[:, sl], dc_ref[:, sl]
            qcat = jnp.concatenate([qcur, qn_ref[:, sl]], axis=0)
            docat = jnp.concatenate([docur, dn_ref[:, sl]], axis=0)
            ocat = jnp.concatenate([oc_ref[:, sl], on_ref[:, sl]], axis=0)
            lcat = jnp.concatenate([lc_ref[:, sl], ln_ref[:, sl]], axis=0)
            dq2 = jnp.zeros((BLK, LANES), F32)
            ds2 = jnp.zeros((1, LANES), F32)
            for half in range(2):
                m1 = _head_mask((BLK, LANES), half)
                m2 = _head_mask((2 * BLK, LANES), half)
                dom = jnp.where(m2, docat, 0.0)
                delta = jnp.sum(dom * ocat, axis=1, keepdims=True)
                lcol = _col_of(lcat, m2)
                domb = dom.astype(BF16)
                qmcat = jnp.where(m2, qcat, jnp.zeros_like(qcat))
                qm = qmcat[:BLK]
                s = lax.dot_general(qm, kcat, nt, preferred_element_type=F32) * scale
                pr = jnp.where(valid_q, jnp.exp(s - lcol[:BLK]), 0.0)
                dp = lax.dot_general(domb[:BLK], vcat, nt, preferred_element_type=F32)
                ds = (pr * (dp - delta[:BLK])).astype(BF16)
                dq2 = jnp.where(m1, jnp.dot(ds, kcat, preferred_element_type=F32) * scale, dq2)
                if has_sink:
                    snk = _col_of(sk_ref[:, sl], _head_mask((1, LANES), half))
                    contrib = jnp.sum(-jnp.exp(snk - lcol[:BLK]) * delta[:BLK], axis=0, keepdims=True)
                    ds2 = jnp.where(_head_mask((1, LANES), half), contrib, ds2)
                s = lax.dot_general(qmcat, kc, nt, preferred_element_type=F32) * scale
                pr = jnp.where(valid_k, jnp.exp(s - lcol), 0.0)
                dv_acc += lax.dot_general(pr.astype(BF16), domb, tn, preferred_element_type=F32)
                dp = lax.dot_general(domb, vc, nt, preferred_element_type=F32)
                ds = (pr * (dp - delta)).astype(BF16)
                dk_acc += lax.dot_general(ds, qmcat, tn, preferred_element_type=F32) * scale
            dq_ref[:, sl] = dq2
            if has_sink:
                @pl.when(n == 0)
                def _():
                    outs[3][:, sl] = jnp.zeros((1, LANES), F32)

                outs[3][:, sl] += ds2
        dk_ref[...] = dk_acc
        dv_ref[...] = dv_acc

    qcur = pl.BlockSpec((BLK, nq * 128), lambda j, n: (n, j))
    qnext = pl.BlockSpec((BLK, nq * 128), lambda j, n: (jnp.minimum(n + 1, nb - 1), j))
    cur = pl.BlockSpec((BLK, 128), lambda j, n: (n, j))
    prev = pl.BlockSpec((BLK, 128), lambda j, n: (jnp.maximum(n - 1, 0), j))
    in_specs = [qcur, qnext, prev, cur, prev, cur, qcur, qnext, qcur, qnext, qcur, qnext]
    args = [q, q, k, k, v, v, o, o, lse, lse, do, do]
    out_specs = [qcur, cur, cur]
    out_shape = [jax.ShapeDtypeStruct(q.shape, F32), jax.ShapeDtypeStruct(k.shape, F32), jax.ShapeDtypeStruct(k.shape, F32)]
    if has_sink:
        vec = pl.BlockSpec((1, nq * 128), lambda j, n: (0, j))
        in_specs.append(vec)
        args.append(sinks)
        out_specs.append(vec)
        out_shape.append(jax.ShapeDtypeStruct((1, q.shape[1]), F32))
    return _pc(
        body, name=name, grid=(ncol, nb), in_specs=in_specs, out_specs=out_specs, out_shape=out_shape,
        compiler_params=_params(("parallel", "arbitrary")),
    )(*args)


ATT_TILE = 2048


def _rows(ref, start, n, r):
    if r == 1:
        return ref[pl.ds(start, n), :]
    return ref[pl.ds(start, n, stride=r), :]


def _twice(x):
    return jnp.concatenate([x, x], axis=0)


def _stack_heads(x, masks):
    zero = jnp.zeros_like(x)
    return jnp.concatenate([jnp.where(masks[0], x, zero), jnp.where(masks[1], x, zero)], axis=0)


def _set_rows(ref, start, n, r, val):
    if r == 1:
        ref[pl.ds(start, n), :] = val
    else:
        ref[pl.ds(start, n, stride=r), :] = val


def _band_geometry(S, patterns):
    rmax = max(r for _, r in patterns)
    H = BLK * rmax
    T = min(S, ATT_TILE)
    assert T % H == 0 and S % T == 0
    return H, T, S // T, T // BLK


def _band_fwd(q, k, v, *, patterns, nq, name, sinks=None, want_bf16=False):
    S, Ck = k.shape
    H, T, nt, nbt = _band_geometry(S, patterns)
    ncol = Ck // LANES
    scale = HEAD ** -0.5
    has_sink = sinks is not None
    nt_dims = (((1,), (1,)), ((), ()))

    def body(*refs):
        q_ref, kp_ref, kc_ref, vp_ref, vc_ref = refs[:5]
        sk_ref = refs[5] if has_sink else None
        n_out = 3 if want_bf16 else 2
        outs = refs[5 + has_sink:5 + has_sink + n_out]
        qf, kf, vf, M, L, A = refs[5 + has_sink + n_out:]
        t = pl.program_id(1)
        kf[0:H, :] = kp_ref[...].astype(F32)
        kf[H:H + T, :] = kc_ref[...].astype(F32)
        vf[0:H, :] = vp_ref[...].astype(F32)
        vf[H:H + T, :] = vc_ref[...].astype(F32)
        r_i = lax.broadcasted_iota(jnp.int32, (BLK, 2 * BLK), 0)
        c_i = lax.broadcasted_iota(jnp.int32, (BLK, 2 * BLK), 1)
        dist_i = r_i + BLK - c_i
        masks = [_head_mask((BLK, LANES), h) for h in range(2)]

        for i in range(nq):
            qf[...] = q_ref[:, i * 128:(i + 1) * 128].astype(F32) * scale
            for p, (dist, r) in enumerate(patterns):
                in_band = (dist_i >= 0) & (dist_i <= dist)
                in_band_first = in_band & ((c_i >= BLK) | (t > 0))
                in_band, in_band_first = _twice(in_band), _twice(in_band_first)

                def unit(j, b, p=p, r=r, in_band=in_band, in_band_first=in_band_first):
                    q0 = j + b * (BLK * r)
                    q2 = _rows(qf, q0, BLK, r).astype(BF16)
                    kcat = _rows(kf, H + q0 - BLK * r, 2 * BLK, r).astype(BF16)
                    vcat = _rows(vf, H + q0 - BLK * r, 2 * BLK, r).astype(BF16)
                    valid = in_band if b > 0 else in_band_first
                    s = lax.dot_general(_stack_heads(q2, masks), kcat, nt_dims, preferred_element_type=F32)
                    s = jnp.where(valid, s, -jnp.inf)
                    mx = jnp.max(s, axis=1, keepdims=True)
                    pr = jnp.exp(s - mx)
                    den = jnp.sum(pr, axis=1, keepdims=True)
                    pv = jnp.dot(pr.astype(BF16), vcat, preferred_element_type=F32)
                    m2 = jnp.where(masks[0], mx[:BLK], mx[BLK:])
                    l2 = jnp.where(masks[0], den[:BLK], den[BLK:])
                    a2 = jnp.where(masks[0], pv[:BLK], pv[BLK:])
                    if p > 0:
                        mo = _rows(M, q0, BLK, r)
                        mn = jnp.maximum(mo, m2)
                        wa, wb = jnp.exp(mo - mn), jnp.exp(m2 - mn)
                        l2 = wa * _rows(L, q0, BLK, r) + wb * l2
                        a2 = wa * _rows(A, q0, BLK, r) + wb * a2
                        m2 = mn
                    _set_rows(M, q0, BLK, r, m2)
                    _set_rows(L, q0, BLK, r, l2)
                    _set_rows(A, q0, BLK, r, a2)

                for u in range(nbt):
                    unit(u % r, u // r)
            sl = slice(i * 128, (i + 1) * 128)
            mm, ll, aa = M[...], L[...], A[...]
            if has_sink:
                snk = sk_ref[:, sl]
                mn = jnp.maximum(mm, snk)
                w = jnp.exp(mm - mn)
                ll = ll * w + jnp.exp(snk - mn)
                aa = aa * w
                mm = mn
            o = aa / ll
            outs[0][:, sl] = o
            outs[1][:, sl] = mm + jnp.log(ll)
            if want_bf16:
                outs[2][:, sl] = o.astype(BF16)

    th = T // H
    qspec = pl.BlockSpec((T, nq * 128), lambda j, t: (t, j))
    cur = pl.BlockSpec((T, 128), lambda j, t: (t, j))
    prev = pl.BlockSpec((H, 128), lambda j, t: (jnp.maximum(t * th - 1, 0), j))
    in_specs = [qspec, prev, cur, prev, cur]
    args = [q, k, k, v, v]
    if has_sink:
        in_specs.append(pl.BlockSpec((1, nq * 128), lambda j, t: (0, j)))
        args.append(sinks)
    out_dts = [F32, F32] + ([BF16] if want_bf16 else [])
    return _pc(
        body, name=name, grid=(ncol, nt), in_specs=in_specs,
        out_specs=[qspec] * len(out_dts),
        out_shape=[jax.ShapeDtypeStruct(q.shape, dt) for dt in out_dts],
        scratch_shapes=[pltpu.VMEM((T, LANES), F32), pltpu.VMEM((H + T, LANES), F32), pltpu.VMEM((H + T, LANES), F32),
                        pltpu.VMEM((T, LANES), F32), pltpu.VMEM((T, LANES), F32), pltpu.VMEM((T, LANES), F32)],
        compiler_params=_params(("parallel", "parallel")),
    )(*args)


def _band_bwd(q, k, v, lse, delta, do, *, patterns, nq, name, sinks=None, do_col0=0):
    S, Ck = k.shape
    H, T, nt, nbt = _band_geometry(S, patterns)
    ncol = Ck // LANES
    scale = HEAD ** -0.5
    has_sink = sinks is not None
    nt_dims = (((1,), (1,)), ((), ()))
    tn_dims = (((0,), (0,)), ((), ()))

    def body(*refs):
        (qc_ref, qn_ref, kp_ref, kc_ref, vp_ref, vc_ref, lc_ref, ln_ref, ec_ref, en_ref, dc_ref, dn_ref) = refs[:12]
        sk_ref = refs[12] if has_sink else None
        n_out = 4 if has_sink else 3
        outs = refs[12 + has_sink:12 + has_sink + n_out]
        dq_ref, dk_ref, dv_ref = outs[:3]
        qf, kf, vf, lf, ef, df = refs[12 + has_sink + n_out:]
        t = pl.program_id(1)
        kf[0:H, :] = kp_ref[...].astype(F32)
        kf[H:H + T, :] = kc_ref[...].astype(F32)
        vf[0:H, :] = vp_ref[...].astype(F32)
        vf[H:H + T, :] = vc_ref[...].astype(F32)
        dk_ref[...] = jnp.zeros_like(dk_ref)
        dv_ref[...] = jnp.zeros_like(dv_ref)
        r_i = lax.broadcasted_iota(jnp.int32, (BLK, 2 * BLK), 0)
        c_i = lax.broadcasted_iota(jnp.int32, (BLK, 2 * BLK), 1)
        dist_q = r_i + BLK - c_i
        dist_h = dist_q[:, :BLK]
        m1 = [_head_mask((BLK, LANES), h) for h in range(2)]

        def stacked_inputs(q2, do2, l2, e2):
            spread = lambda v: jnp.concatenate([jnp.where(m1[0], v, _roll(v, HEAD)), jnp.where(m1[1], v, _roll(v, HEAD))], axis=0)
            return _stack_heads(q2, m1), _stack_heads(do2.astype(BF16), m1), spread(l2), spread(e2)

        for i in range(nq):
            sl = slice(i * 128, (i + 1) * 128)
            qf[0:T, :] = qc_ref[:, sl].astype(F32) * scale
            qf[T:T + H, :] = qn_ref[:, sl].astype(F32) * scale
            for buf, c_ref, n_ref in ((lf, lc_ref, ln_ref), (ef, ec_ref, en_ref), (df, dc_ref, dn_ref)):
                buf[0:T, :] = c_ref[:, sl]
                buf[T:T + H, :] = n_ref[:, sl]
            if has_sink:
                @pl.when(t == 0)
                def _():
                    outs[3][:, sl] = jnp.zeros((1, LANES), F32)

                outs[3][:, sl] += jnp.sum(-jnp.exp(sk_ref[:, sl] - lc_ref[:, sl]) * ec_ref[:, sl], axis=0, keepdims=True)
            for p, (dist, r) in enumerate(patterns):
                band_q = (dist_q >= 0) & (dist_q <= dist)
                band_first = band_q & ((c_i >= BLK) | (t > 0))
                band_h = (dist_h >= 0) & (dist_h <= dist)
                band_q, band_first, band_h = _twice(band_q), _twice(band_first), _twice(band_h)

                def add_rows(ref, start, val, r=r):
                    _set_rows(ref, start, BLK, r, _rows(ref, start, BLK, r) + val)

                def unit(j, b, p=p, r=r, band_q=band_q, band_first=band_first):
                    q0 = j + b * (BLK * r)
                    q2 = _rows(qf, q0, BLK, r).astype(BF16)
                    do2, l2, e2 = _rows(df, q0, BLK, r), _rows(lf, q0, BLK, r), _rows(ef, q0, BLK, r)
                    kcat = _rows(kf, H + q0 - BLK * r, 2 * BLK, r).astype(BF16)
                    vcat = _rows(vf, H + q0 - BLK * r, 2 * BLK, r).astype(BF16)
                    valid = band_q if b > 0 else band_first
                    qs, dos, ls, es = stacked_inputs(q2, do2, l2, e2)
                    s = lax.dot_general(qs, kcat, nt_dims, preferred_element_type=F32)
                    pr = jnp.where(valid, jnp.exp(s - jnp.concatenate([ls, ls], axis=1)), 0.0)
                    dp = lax.dot_general(dos, vcat, nt_dims, preferred_element_type=F32)
                    ds = (pr * (dp - jnp.concatenate([es, es], axis=1))).astype(BF16)
                    dqs = jnp.dot(ds, kcat, preferred_element_type=F32) * scale
                    dq2 = jnp.where(m1[0], dqs[:BLK], dqs[BLK:])
                    dvc = lax.dot_general(pr.astype(BF16), dos, tn_dims, preferred_element_type=F32)
                    dkc = lax.dot_general(ds, qs, tn_dims, preferred_element_type=F32)
                    if p > 0:
                        dq2 = dq2 + _rows(dq_ref.at[:, sl], q0, BLK, r)
                    _set_rows(dq_ref.at[:, sl], q0, BLK, r, dq2)
                    add_rows(dk_ref, q0, dkc[BLK:])
                    add_rows(dv_ref, q0, dvc[BLK:])
                    if b > 0:
                        add_rows(dk_ref, q0 - BLK * r, dkc[:BLK])
                        add_rows(dv_ref, q0 - BLK * r, dvc[:BLK])

                def halo_unit(j, r=r, band_h=band_h):
                    k0 = j + (nbt // r - 1) * (BLK * r)
                    q2 = _rows(qf, T + j, BLK, r).astype(BF16)
                    do2, l2, e2 = _rows(df, T + j, BLK, r), _rows(lf, T + j, BLK, r), _rows(ef, T + j, BLK, r)
                    kc = _rows(kf, H + k0, BLK, r).astype(BF16)
                    vc = _rows(vf, H + k0, BLK, r).astype(BF16)
                    qs, dos, ls, es = stacked_inputs(q2, do2, l2, e2)
                    s = lax.dot_general(qs, kc, nt_dims, preferred_element_type=F32)
                    pr = jnp.where(band_h, jnp.exp(s - ls), 0.0)
                    dp = lax.dot_general(dos, vc, nt_dims, preferred_element_type=F32)
                    ds = (pr * (dp - es)).astype(BF16)
                    add_rows(dk_ref, k0, lax.dot_general(ds, qs, tn_dims, preferred_element_type=F32))
                    add_rows(dv_ref, k0, lax.dot_general(pr.astype(BF16), dos, tn_dims, preferred_element_type=F32))

                for u in range(nbt):
                    unit(u % r, u // r)
                if nt > 1:
                    @pl.when(t < nt - 1)
                    def _(r=r, halo_unit=halo_unit):
                        for j in range(r):
                            halo_unit(j)

    th = T // H
    last = S // H - 1
    qcur = pl.BlockSpec((T, nq * 128), lambda j, t: (t, j))
    qnext = pl.BlockSpec((H, nq * 128), lambda j, t: (jnp.minimum((t + 1) * th, last), j))
    cur = pl.BlockSpec((T, 128), lambda j, t: (t, j))
    prev = pl.BlockSpec((H, 128), lambda j, t: (jnp.maximum(t * th - 1, 0), j))
    dcur = pl.BlockSpec((T, nq * 128), lambda j, t: (t, j + do_col0))
    dnext = pl.BlockSpec((H, nq * 128), lambda j, t: (jnp.minimum((t + 1) * th, last), j + do_col0))
    in_specs = [qcur, qnext, prev, cur, prev, cur, qcur, qnext, qcur, qnext, dcur, dnext]
    args = [q, q, k, k, v, v, lse, lse, delta, delta, do, do]
    out_specs = [qcur, cur, cur]
    out_shape = [jax.ShapeDtypeStruct(q.shape, F32), jax.ShapeDtypeStruct(k.shape, F32), jax.ShapeDtypeStruct(k.shape, F32)]
    if has_sink:
        vec = pl.BlockSpec((1, nq * 128), lambda j, t: (0, j))
        in_specs.append(vec)
        args.append(sinks)
        out_specs.append(vec)
        out_shape.append(jax.ShapeDtypeStruct((1, q.shape[1]), F32))
    big = pltpu.VMEM((T + H, LANES), F32)
    return _pc(
        body, name=name, grid=(ncol, nt), in_specs=in_specs, out_specs=out_specs, out_shape=out_shape,
        scratch_shapes=[big] * 6,
        compiler_params=_params(("parallel", "arbitrary")),
    )(*args)


def _delta(do, o, name):
    S, C = do.shape
    tm = 512

    def body(do_ref, o_ref, g_ref, e_ref):
        for c in range(C // LANES):
            sl = slice(c * 128, (c + 1) * 128)
            e_ref[:, sl] = _gmean(do_ref[:, sl] * o_ref[:, sl], g_ref[...]) * float(HEAD)

    row = pl.BlockSpec((tm, C), lambda i: (i, 0))
    return _pc(
        body, name=name, grid=(S // tm,),
        in_specs=[row, row, pl.BlockSpec((LANES, LANES), lambda i: (0, 0))], out_specs=row,
        out_shape=jax.ShapeDtypeStruct((S, C), F32),
        compiler_params=_params(("parallel",)),
    )(do, o, _group_matrix())


def _even_post_fwd(ro, proj, gn, da):
    S = ro.shape[0]
    tm = 256

    def body(ro_ref, rg_ref, gn_ref, da_ref, mix_ref):
        for c in range(4):
            sl = slice(c * 128, (c + 1) * 128)
            x = ro_ref[:, sl]
            mu = jnp.mean(x, axis=1, keepdims=True)
            xc = x - mu
            var = jnp.mean(xc * xc, axis=1, keepdims=True)
            y = xc * lax.rsqrt(var + EPS) * gn_ref[:, sl]
            z = rg_ref[:, sl]
            mix_ref[:, sl] = (z * jax.nn.sigmoid(z) * y).astype(BF16)
        mix_ref[:, 512:1024] = da_ref[...].astype(BF16)

    row = lambda w: pl.BlockSpec((tm, w), lambda i: (i, 0))
    return _pc(
        body, name="even_post_fwd", grid=(S // tm,),
        in_specs=[row(512), pl.BlockSpec((tm, 512), lambda i: (i, 2)), pl.BlockSpec((1, 512), lambda i: (0, 0)), row(512)],
        out_specs=row(1024), out_shape=jax.ShapeDtypeStruct((S, 1024), BF16),
        compiler_params=_params(("parallel",)),
    )(ro, proj, gn, da)


def _even_post_bwd(ro, proj, gn, dmixed):
    S = ro.shape[0]
    tm = 256

    def body(ro_ref, rg_ref, gn_ref, dm_ref, dro_ref, drg_ref, dgn_ref):
        @pl.when(pl.program_id(0) == 0)
        def _():
            dgn_ref[...] = jnp.zeros_like(dgn_ref)

        for c in range(4):
            sl = slice(c * 128, (c + 1) * 128)
            x = ro_ref[:, sl]
            mu = jnp.mean(x, axis=1, keepdims=True)
            xc = x - mu
            rstd = lax.rsqrt(jnp.mean(xc * xc, axis=1, keepdims=True) + EPS)
            xh = xc * rstd
            gain = gn_ref[:, sl]
            y = xh * gain
            z = rg_ref[:, sl]
            sg = jax.nn.sigmoid(z)
            dra = dm_ref[:, sl]
            drg_ref[:, sl] = dra * y * sg * (1.0 + z * (1.0 - sg))
            dy = dra * z * sg
            dgn_ref[:, sl] += jnp.sum(dy * xh, axis=0, keepdims=True)
            dxh = dy * gain
            dro_ref[:, sl] = rstd * (dxh - jnp.mean(dxh, axis=1, keepdims=True)
                                     - xh * jnp.mean(dxh * xh, axis=1, keepdims=True))

    row = lambda w: pl.BlockSpec((tm, w), lambda i: (i, 0))
    vec = pl.BlockSpec((1, 512), lambda i: (0, 0))
    return _pc(
        body, name="even_post_bwd", grid=(S // tm,),
        in_specs=[row(512), pl.BlockSpec((tm, 512), lambda i: (i, 2)), vec, row(512)],
        out_specs=[row(512), row(512), vec],
        out_shape=[jax.ShapeDtypeStruct((S, 512), F32), jax.ShapeDtypeStruct((S, 512), F32),
                   jax.ShapeDtypeStruct((1, 512), F32)],
        compiler_params=_params(("arbitrary",)),
    )(ro, proj, gn, dmixed)


def _swa_pre_fwd(proj, tab, qg, kg):
    S = proj.shape[0]
    tm = 256

    def body(p_ref, tab_ref, qg_ref, kg_ref, g_ref, q_ref, k_ref, v_ref):
        Ap, Bp, Cp = _tab(tab_ref, 1)
        G = g_ref[...]
        lo = _head_mask((tm, LANES), 0)
        for c in range(8):
            sl = slice(c * 128, (c + 1) * 128)
            q_ref[:, sl] = _rope(_hn_fwd(p_ref[:, sl], qg_ref[...], G), Ap, Bp, Cp, 8).astype(BF16)
        for c in range(2):
            kn = _rope(_hn_fwd(p_ref[:, 1024 + c * 128:1024 + (c + 1) * 128], kg_ref[...], G), Ap, Bp, Cp, 8)
            vv = p_ref[:, 1280 + c * 128:1280 + (c + 1) * 128]
            for t, ref in ((kn, k_ref), (vv, v_ref)):
                sw = _roll(t, HEAD)
                ref[:, (2 * c) * 128:(2 * c + 1) * 128] = jnp.where(lo, t, sw).astype(BF16)
                ref[:, (2 * c + 1) * 128:(2 * c + 2) * 128] = jnp.where(lo, sw, t).astype(BF16)

    row = lambda w: pl.BlockSpec((tm, w), lambda i: (i, 0))
    vec = pl.BlockSpec((1, LANES), lambda i: (0, 0))
    return _pc(
        body, name="swa_pre_fwd", grid=(S // tm,),
        in_specs=[row(1536), row(768), vec, vec, pl.BlockSpec((LANES, LANES), lambda i: (0, 0))],
        out_specs=[row(1024), row(512), row(512)],
        out_shape=[jax.ShapeDtypeStruct((S, w), BF16) for w in (1024, 512, 512)],
        compiler_params=_params(("parallel",)),
    )(proj, tab, qg, kg, _group_matrix())


def _swa_pre_bwd(proj, tab, qg, kg, dq, dk, dv):
    S = proj.shape[0]
    tm = 256

    def body(p_ref, tab_ref, qg_ref, kg_ref, g_ref, dq_ref, dk_ref, dv_ref, dp_ref, db_ref, dqg_ref, dkg_ref):
        Ap, Bp, Cp = _tab(tab_ref, 1)
        G = g_ref[...]
        lo = _head_mask((tm, LANES), 0)

        @pl.when(pl.program_id(0) == 0)
        def _():
            db_ref[...] = jnp.zeros_like(db_ref)
            dqg_ref[...] = jnp.zeros_like(dqg_ref)
            dkg_ref[...] = jnp.zeros_like(dkg_ref)

        accq = jnp.zeros((1, LANES), F32)
        acck = jnp.zeros((1, LANES), F32)
        for c in range(8):
            sl = slice(c * 128, (c + 1) * 128)
            dx, dg = _hn_bwd(p_ref[:, sl], qg_ref[...], _rope_t(dq_ref[:, sl], Ap, Bp, Cp, 8), G)
            dp_ref[:, sl] = dx.astype(BF16)
            db_ref[:, sl] += jnp.sum(dx, axis=0, keepdims=True)
            accq = accq + dg
        for c in range(2):
            folded = []
            for ref in (dk_ref, dv_ref):
                a = ref[:, (2 * c) * 128:(2 * c + 1) * 128]
                b = ref[:, (2 * c + 1) * 128:(2 * c + 2) * 128]
                folded.append(jnp.where(lo, a + _roll(a, HEAD), b + _roll(b, HEAD)))
            ks = slice(1024 + c * 128, 1024 + (c + 1) * 128)
            dx, dg = _hn_bwd(p_ref[:, ks], kg_ref[...], _rope_t(folded[0], Ap, Bp, Cp, 8), G)
            dp_ref[:, ks] = dx.astype(BF16)
            db_ref[:, ks] += jnp.sum(dx, axis=0, keepdims=True)
            acck = acck + dg
            vs = slice(1280 + c * 128, 1280 + (c + 1) * 128)
            dp_ref[:, vs] = folded[1].astype(BF16)
            db_ref[:, vs] += jnp.sum(folded[1], axis=0, keepdims=True)
        dqg_ref[...] += _fold_halves(accq)
        dkg_ref[...] += _fold_halves(acck)

    row = lambda w: pl.BlockSpec((tm, w), lambda i: (i, 0))
    vec = pl.BlockSpec((1, LANES), lambda i: (0, 0))
    return _pc(
        body, name="swa_pre_bwd", grid=(S // tm,),
        in_specs=[row(1536), row(768), vec, vec, pl.BlockSpec((LANES, LANES), lambda i: (0, 0)),
                  row(1024), row(512), row(512)],
        out_specs=[row(1536), pl.BlockSpec((1, 1536), lambda i: (0, 0)), vec, vec],
        out_shape=[jax.ShapeDtypeStruct((S, 1536), BF16), jax.ShapeDtypeStruct((1, 1536), F32),
                   jax.ShapeDtypeStruct((1, LANES), F32), jax.ShapeDtypeStruct((1, LANES), F32)],
        compiler_params=_params(("arbitrary",)),
    )(proj, tab, qg, kg, _group_matrix(), dq, dk, dv)


def _loss_head(y, target):
    S, Dm = y.shape
    tm = 512

    def body(y_ref, t_ref, l_ref, dy_ref, dyb_ref):
        @pl.when(pl.program_id(0) == 0)
        def _():
            l_ref[...] = jnp.zeros_like(l_ref)

        e = y_ref[...] - t_ref[...]
        dy = e * (1.0 / Dm)
        dy_ref[...] = dy
        dyb_ref[...] = dy.astype(BF16)
        row = jnp.sum(e * e, axis=1, keepdims=True) * (0.5 / Dm)
        l_ref[...] += jnp.sum(row, axis=0, keepdims=True)

    row = pl.BlockSpec((tm, Dm), lambda i: (i, 0))
    return _pc(
        body, name="loss_head", grid=(S // tm,), in_specs=[row, row],
        out_specs=[pl.BlockSpec((1, LANES), lambda i: (0, 0)), row, row],
        out_shape=[jax.ShapeDtypeStruct((1, LANES), F32), jax.ShapeDtypeStruct((S, Dm), F32),
                   jax.ShapeDtypeStruct((S, Dm), BF16)],
        compiler_params=_params(("arbitrary",)),
    )(y, target)


def _relu2_of(u):
    r = jnp.maximum(u.astype(F32), 0.0)
    return r * r


def _drelu2(acc, u):
    return (acc * 2.0 * jnp.maximum(u.astype(F32), 0.0),)


def _add(acc, res):
    return (acc + res,)


def _add_norm_in(res, g):
    def epilogue(acc, r, gv):
        xn = acc + r
        return xn, xn * lax.rsqrt(jnp.mean(xn * xn, axis=-1, keepdims=True) + EPS) * gv

    return dict(outs=[F32, BF16], epilogue=epilogue, extras=[(res, "mn"), (g.reshape(1, D_MODEL), "n")])


_T = dict(tm=1024, tn=1024, tk=1024)


def _rms_bwd_in(x, g, dres):
    def epilogue(dh, xv, gv, dr):
        r = lax.rsqrt(jnp.mean(xv * xv, axis=-1, keepdims=True) + EPS)
        t = dh * gv
        dx = dr + r * t - xv * (r * r * r) * jnp.mean(xv * t, axis=-1, keepdims=True)
        return dx, dx, jnp.sum(dh * xv * r, axis=0, keepdims=True)

    return dict(outs=[F32, BF16, ("colsum",)], epilogue=epilogue,
                extras=[(x, "mn"), (g.reshape(1, D_MODEL), "n"), (dres, "mn")])


def _delta_in(o, col0):
    width = D_MODEL - col0

    def epilogue(do, ov, G):
        parts = [_gmean(do[:, col0 + c * 128:col0 + (c + 1) * 128] * ov[:, c * 128:(c + 1) * 128], G) * float(HEAD)
                 for c in range(width // LANES)]
        return do, jnp.concatenate(parts, axis=1)

    return dict(outs=[F32, (F32, width)], epilogue=epilogue, extras=[(o, width), (_group_matrix(), "full")])


def _loss_in(res, target):
    def epilogue(acc, r, t):
        e = acc + r - t
        dy = e * (1.0 / D_MODEL)
        return dy, dy, jnp.sum(e * e, axis=0, keepdims=True) * (0.5 / D_MODEL)

    return dict(outs=[F32, BF16, ("colsum",)], epilogue=epilogue, extras=[(res, "mn"), (target, "mn")])


def _mlp_fwd(h, wts, layer, tag, tail):
    u = _matmul(h, wts, dims="nn", **_T, outs=[BF16], b_cs=True, b_row0=layer, name=f"mlp_up{tag}")
    out = _matmul(u, wts, dims="nn", **_T, a_pro=_relu2_of, b_rs=1024, b_row0=2 + layer, name=f"mlp_down{tag}", **tail)
    return out, (h, u)


def _mlp_bwd(x, g, wts, layer, saved, dy, dyb, tag):
    h, u = saved
    du = _matmul(dyb, wts, dims="nt", **_T, outs=[BF16], epilogue=_drelu2, extras=[(u, "mn")], b_rs=1024,
                 b_row0=2 + layer, name=f"mlp_du{tag}")
    dw_dn = _matmul(u, dyb, dims="tn", **_T, outs=[F32], a_pro=_relu2_of, name=f"mlp_dwdown{tag}")
    dw_up = _matmul(h, du, dims="tn", **_T, outs=[F32], o_cs=N_CHIPS, name=f"mlp_dwup{tag}")
    dx, dxb, dg = _matmul(du, wts, dims="nt", tm=512, tn=1024, tk=1024, b_cs=True, b_row0=layer, b_rows=1024, name=f"mlp_dh{tag}",
                          **_rms_bwd_in(x, g, dy))
    return dx, dxb, dg, dw_up, dw_dn


def _pattern_view(t, r):
    S, C = t.shape
    return t.reshape(S // r, r * C)


def _local_step(x, pos_col, target, first_of, rest_begin, rest_of, P, red):
    S = x.shape[0]
    tab = _tables(pos_col)
    tile2 = lambda g: jnp.tile(g.reshape(1, HEAD), (1, 2))
    dqg, dkg = tile2(P["dil_q_gain"]), tile2(P["dil_k_gain"])
    sqg, skg = tile2(P["swa_q_gain"]), tile2(P["swa_k_gain"])
    gn = P["ret_gn_gain"].reshape(1, 512)
    sink_b = jnp.repeat(P["swa_sinks"].reshape(16), HEAD).reshape(1, 1024)

    h0 = _rms_fwd(x, P["norm_mix"][0], "rms_mix_fwd0")
    W = first_of(h0)
    proj = _matmul(h0, W["hyb_w_in"], dims="nn", tm=1024, tn=768, tk=1024, outs=[F32], b_cs=True, name="hyb_in")
    rq, rk, rv, dq, dk, dv = _even_pre_fwd(proj, tab, dqg, dkg)
    ro, states = _ret_fwd(rq, rk, rv)
    dil = [(w // r, r) for w, r in DIL_PATTERNS]
    da, dlse = _band_fwd(dq, dk, dv, patterns=dil, nq=1, name="dil_fwd")
    mixed = rest_begin(_even_post_fwd(ro, proj, gn, da))
    x1, h1 = _matmul(mixed, W["hyb_w_out"], dims="nn", **_T, name="hyb_out", **_add_norm_in(x, P["norm_mlp"][0]))
    rest, bias = rest_of(x1)
    W = {**W, **rest}
    (x2, h2), mlp0 = _mlp_fwd(h1, W["packed"], 0, "0", _add_norm_in(x1, P["norm_mix"][1]))

    proj2 = _matmul(h2, W["swa_w_qkv"], dims="nn", tm=1024, tn=384, tk=1024, outs=[F32], b_cs=True,
                    epilogue=_add, extras=[(bias.reshape(1, 1536), "n")], name="swa_qkv")
    sq, sk, sv = _swa_pre_fwd(proj2, tab, sqg, skg)
    swa = [(SWA_DIST, 1)]
    so, slse, so_b = _band_fwd(sq, sk, sv, patterns=swa, nq=2, name="swa_fwd", sinks=sink_b, want_bf16=True)
    x3, h3 = _matmul(so_b, W["swa_w_out"], dims="nn", **_T, name="swa_out", **_add_norm_in(x2, P["norm_mlp"][1]))
    (dy, dyb, loss_cols), mlp1 = _mlp_fwd(h3, W["packed"], 1, "1", _loss_in(x3, target))
    loss = jnp.broadcast_to(jnp.sum(loss_cols), (1, LANES))

    gw, gp = {}, {}
    dx3, dx3b, dg_mlp1, gw["mlp_w_up1"], gw["mlp_w_down1"] = _mlp_bwd(x3, P["norm_mlp"][1], W["packed"], 1, mlp1, dy, dyb, "1")
    dx3b = red.begin("mlp1", {n: (gw[n], 1024) for n in ("mlp_w_up1", "mlp_w_down1")}, dx3b)
    gw["swa_w_out"] = _matmul(so_b, dx3b, dims="tn", **_T, outs=[F32], name="swa_dwout")
    dso, sdelta = _matmul(dx3b, W["swa_w_out"], dims="nt", tm=512, tn=1024, tk=1024, name="swa_do", **_delta_in(so, 0))
    dsq, dsk, dsv, dsink = _band_bwd(sq, sk, sv, slse, sdelta, dso, patterns=swa, nq=2, name="swa_bwd", sinks=sink_b)
    dproj2, gp["swa_b_qkv"], gp["swa_q_gain"], gp["swa_k_gain"] = _swa_pre_bwd(proj2, tab, sqg, skg, dsq, dsk, dsv)
    gp["swa_sinks"] = dsink
    gw["swa_w_qkv"] = _matmul(h2, dproj2, dims="tn", tm=1024, tn=384, tk=1024, outs=[F32], o_cs=N_CHIPS, name="swa_dwqkv")
    dx2, dx2b, dg_mix1 = _matmul(dproj2, W["swa_w_qkv"], dims="nt", tm=512, tn=1024, tk=384, b_cs=True, name="swa_dh",
                                 **_rms_bwd_in(x2, P["norm_mix"][1], dx3))
    dx2b = red.begin("swa", {"swa_w_qkv": (gw["swa_w_qkv"], 1024), "swa_w_out": (gw["swa_w_out"], 256)}, dx2b)
    dx2b = red.advance("mlp1", dx2b, dx2b)

    dx1, dx1b, dg_mlp0, gw["mlp_w_up0"], gw["mlp_w_down0"] = _mlp_bwd(x1, P["norm_mlp"][0], W["packed"], 0, mlp0, dx2, dx2b, "0")
    gw["hyb_w_out"] = _matmul(mixed, dx1b, dims="tn", **_T, outs=[F32], name="hyb_dwout")
    dx1b = red.begin("mlp0", {"mlp_w_up0": (gw["mlp_w_up0"], 1024), "mlp_w_down0": (gw["mlp_w_down0"], 1024),
                              "hyb_w_out": (gw["hyb_w_out"], 256)}, dx1b)
    dx1b = red.advance("swa", dx1b, dx1b)
    red.finish("mlp1", dx1b)
    dmixed, ddelta = _matmul(dx1b, W["hyb_w_out"], dims="nt", tm=512, tn=1024, tk=1024, name="hyb_dmixed", **_delta_in(da, 512))
    dro, drg, gp["ret_gn_gain"] = _even_post_bwd(ro, proj, gn, dmixed)
    drq, drk, drv = _ret_bwd(rq, rk, rv, states, dro)
    ddq, ddk, ddv = _band_bwd(dq, dk, dv, dlse, ddelta, dmixed, patterns=dil, nq=1, name="dil_bwd", do_col0=4)
    ddq = red.advance("mlp0", ddq, ddq)
    red.finish("swa", ddq)
    dproj, gp["dil_q_gain"], gp["dil_k_gain"] = _even_pre_bwd(proj, tab, dqg, dkg, drq, drk, drv, drg, [ddq], [ddk], [ddv])
    gw["hyb_w_in"] = _matmul(h0, dproj, dims="tn", tm=1024, tn=768, tk=1024, outs=[F32], o_cs=N_CHIPS, name="hyb_dwin")
    dproj = red.begin("win", {"hyb_w_in": (gw["hyb_w_in"], 1024)}, dproj)
    grad_x, _, dg_mix0 = _matmul(dproj, W["hyb_w_in"], dims="nt", tm=512, tn=1024, tk=768, b_cs=True, name="hyb_dh",
                                 **_rms_bwd_in(x, P["norm_mix"][0], dx1))
    red.finish("mlp0", grad_x)
    gp["norm_mix"] = jnp.concatenate([dg_mix0, dg_mix1], axis=0)
    gp["norm_mlp"] = jnp.concatenate([dg_mlp0, dg_mlp1], axis=0)
    return loss, grad_x, gp


HBM = pl.BlockSpec(memory_space=pltpu.HBM)


def _place():
    x, y, c = lax.axis_index("x"), lax.axis_index("y"), lax.axis_index("c")
    chips = [(1 - x, y), (x, 1 - y), (1 - x, 1 - y)]
    return x, y, c, chips


def _allgather_shards(buf):
    _, R, Wd = buf.shape
    Rh = R // 2

    def body(b_ref, out_ref, send_sems, recv_sems):
        x, y, c, chips = _place()
        sibling = (x, y, 1 - c)

        def copy(k, chip, core, to):
            block = b_ref.at[2 * chip[0] + chip[1], pl.ds(core * Rh, Rh), :]
            return pltpu.make_async_remote_copy(
                src_ref=block, dst_ref=block, send_sem=send_sems.at[k], recv_sem=recv_sems.at[k],
                device_id=to, device_id_type=MESH)

        first = [copy(k, (x, y), c, (*chip, c)) for k, chip in enumerate(chips)]
        for cp in first:
            cp.start()
        passed = [copy(3 + k, chip, c, sibling) for k, chip in enumerate(chips)]
        for k, chip in enumerate(chips):
            copy(k, chip, c, (x, y, c)).wait_recv()
            passed[k].start()
        for k, chip in enumerate(chips):
            copy(3 + k, chip, 1 - c, (x, y, c)).wait_recv()
        for cp in first + passed:
            cp.wait_send()

    return _pc(
        body, name="allgather_first", in_specs=[HBM], out_specs=HBM,
        out_shape=jax.ShapeDtypeStruct(buf.shape, buf.dtype), input_output_aliases={0: 0},
        scratch_shapes=[pltpu.SemaphoreType.DMA((6,)), pltpu.SemaphoreType.DMA((6,))],
    )(buf)


SEM = pl.BlockSpec(memory_space=pltpu.SEMAPHORE)
EFFECT = pltpu.SideEffectType.DATAFLOW_SIDE_EFFECTING


def _half_block(ref, chip, core):
    rh = ref.shape[1] // 2
    return ref.at[2 * chip[0] + chip[1], pl.ds(core * rh, rh), :]


def _gather_start(buf, ride, name):
    def body(b_ref, ride_ref, s0, s1, s2, r0, r1, r2, b_out, ride_out):
        x, y, c, chips = _place()
        for chip, s, r in zip(chips, (s0, s1, s2), (r0, r1, r2)):
            mine = _half_block(b_ref, (x, y), c)
            pltpu.make_async_remote_copy(src_ref=mine, dst_ref=mine, send_sem=s, recv_sem=r,
                                         device_id=(*chip, c), device_id_type=MESH).start()

    sem = pltpu.SemaphoreType.DMA(())
    return _pc(
        body, name=name,
        out_shape=(sem,) * 6 + (pltpu.HBM(buf.shape, buf.dtype), pltpu.HBM(ride.shape, ride.dtype)),
        in_specs=(HBM, HBM), out_specs=(SEM,) * 6 + (HBM, HBM), input_output_aliases={0: 6, 1: 7},
        compiler_params=pltpu.CompilerParams(has_side_effects=EFFECT),
    )(pltpu.with_memory_space_constraint(buf, pltpu.HBM), pltpu.with_memory_space_constraint(ride, pltpu.HBM))


def _gather_wait(buf, sems, after, name):
    def body(b_ref, s0, s1, s2, r0, r1, r2, after_ref, b_out):
        x, y, c, chips = _place()
        for chip, s, r in zip(chips, (s0, s1, s2), (r0, r1, r2)):
            cp = pltpu.make_async_remote_copy(src_ref=_half_block(b_ref, (x, y), c), dst_ref=_half_block(b_ref, chip, c),
                                              send_sem=s, recv_sem=r, device_id=(*chip, c), device_id_type=MESH)
            cp.wait_send()
            cp.wait_recv()

    return _pc(
        body, name=name, out_shape=pltpu.HBM(buf.shape, buf.dtype),
        in_specs=(HBM,) + (SEM,) * 6 + (pl.BlockSpec(memory_space=pl.ANY),), out_specs=HBM, input_output_aliases={0: 0},
        compiler_params=pltpu.CompilerParams(has_side_effects=EFFECT),
    )(buf, *sems, after)


def _gather_handover(buf, name):
    def body(b_ref, out_ref, send_sems, recv_sems):
        x, y, c, chips = _place()
        cps = []
        for k, chip in enumerate(chips):
            mine = _half_block(b_ref, chip, c)
            cps.append(pltpu.make_async_remote_copy(src_ref=mine, dst_ref=mine, send_sem=send_sems.at[k],
                                                    recv_sem=recv_sems.at[k], device_id=(x, y, 1 - c), device_id_type=MESH))
        for cp in cps:
            cp.start()
        for k, chip in enumerate(chips):
            theirs = _half_block(b_ref, chip, 1 - c)
            pltpu.make_async_remote_copy(src_ref=theirs, dst_ref=theirs, send_sem=send_sems.at[k], recv_sem=recv_sems.at[k],
                                         device_id=(x, y, 1 - c), device_id_type=MESH).wait_recv()
        for cp in cps:
            cp.wait_send()

    return _pc(
        body, name=name, in_specs=[HBM], out_specs=HBM,
        out_shape=jax.ShapeDtypeStruct(buf.shape, buf.dtype), input_output_aliases={0: 0},
        scratch_shapes=[pltpu.SemaphoreType.DMA((3,)), pltpu.SemaphoreType.DMA((3,))],
    )(buf)


def _swap_halves(ts):
    nt = len(ts)

    def body(*refs):
        t_refs, l_refs, send_sems, recv_sems = refs[:nt], refs[nt:2 * nt], refs[-2], refs[-1]
        x, y, c, _ = _place()
        cps = []
        for k in range(nt):
            rh = t_refs[k].shape[1] // 2
            cps.append(pltpu.make_async_remote_copy(
                src_ref=t_refs[k].at[:, pl.ds((1 - c) * rh, rh), :], dst_ref=l_refs[k],
                send_sem=send_sems.at[k], recv_sem=recv_sems.at[k], device_id=(x, y, 1 - c), device_id_type=MESH))
        for cp in cps:
            cp.start()
        for cp in cps:
            cp.wait()

    return _pc(
        body, name="grad_swap_halves", in_specs=[HBM] * nt, out_specs=[HBM] * nt,
        out_shape=[jax.ShapeDtypeStruct((t.shape[0], t.shape[1] // 2, t.shape[2]), F32) for t in ts],
        scratch_shapes=[pltpu.SemaphoreType.DMA((nt,)), pltpu.SemaphoreType.DMA((nt,))],
    )(*ts)


def _pair_sum(t, l, place, name):
    _, r, cols = t.shape
    rh = r // 2
    tr = min(rh, 256)
    nr = rh // tr

    def body(pl_ref, t_ref, l_ref, o_ref):
        o_ref[...] = (t_ref[...] + l_ref[...]).astype(BF16)

    other = lambda s, p: s + jnp.where(s >= p[0], 1, 0)
    return _pc(
        body, name=name,
        grid_spec=pltpu.PrefetchScalarGridSpec(
            num_scalar_prefetch=1, grid=(N_CHIPS - 1, nr),
            in_specs=[pl.BlockSpec((None, tr, cols), lambda s, i, p: (other(s, p), p[1] * nr + i, 0)),
                      pl.BlockSpec((None, tr, cols), lambda s, i, p: (other(s, p), i, 0))],
            out_specs=pl.BlockSpec((None, tr, cols), lambda s, i, p: (other(s, p), i, 0))),
        out_shape=jax.ShapeDtypeStruct((N_CHIPS, rh, cols), BF16),
        compiler_params=_params(("parallel", "parallel")),
    )(place, t, l)


def _exchange_chips(ps):
    nt = len(ps)

    def body(*refs):
        p_refs, r_refs, send_sems, recv_sems = refs[:nt], refs[nt:2 * nt], refs[-2], refs[-1]
        x, y, c, chips = _place()
        cps = []
        for t in range(nt):
            for k, chip in enumerate(chips):
                cps.append(pltpu.make_async_remote_copy(
                    src_ref=p_refs[t].at[2 * chip[0] + chip[1]], dst_ref=r_refs[t].at[k],
                    send_sem=send_sems.at[3 * t + k], recv_sem=recv_sems.at[3 * t + k],
                    device_id=(*chip, c), device_id_type=MESH))
        for cp in cps:
            cp.start()
        for cp in cps:
            cp.wait()

    return _pc(
        body, name="grad_exchange_chips", in_specs=[HBM] * nt, out_specs=[HBM] * nt,
        out_shape=[jax.ShapeDtypeStruct((3,) + p.shape[1:], BF16) for p in ps],
        scratch_shapes=[pltpu.SemaphoreType.DMA((3 * nt,)), pltpu.SemaphoreType.DMA((3 * nt,))],
    )(*ps)


def _final_sum(t, l, rcv, place, name, layer=0, layers=1, into=None):
    _, r, cols = t.shape
    rh = r // 2
    tr = min(rh, 256)
    nr = rh // tr

    def body(pl_ref, t_ref, l_ref, r_ref, *rest):
        acc = t_ref[...] + l_ref[...]
        for k in range(3):
            acc = acc + r_ref[k].astype(F32)
        rest[-1][...] = acc

    in_specs = [pl.BlockSpec((None, tr, cols), lambda i, p: (p[0], p[1] * nr + i, 0)),
                pl.BlockSpec((None, tr, cols), lambda i, p: (p[0], i, 0)),
                pl.BlockSpec((3, tr, cols), lambda i, p: (0, i, 0))]
    args = [place, t, l, rcv]
    aliases = {}
    if into is not None:
        in_specs.append(pl.BlockSpec(memory_space=pl.ANY))
        args.append(into)
        aliases = {4: 0}
    return _pc(
        body, name=name,
        grid_spec=pltpu.PrefetchScalarGridSpec(
            num_scalar_prefetch=1, grid=(nr,), in_specs=in_specs,
            out_specs=pl.BlockSpec((tr, cols), lambda i, p: (2 * nr * layer + p[1] * nr + i, 0))),
        out_shape=jax.ShapeDtypeStruct((layers * r, cols), F32), input_output_aliases=aliases,
        compiler_params=_params(("parallel",)),
    )(*args)


def _share_halves(hs, name):
    nt = len(hs)
    n = sum(layers for _, layers in hs)

    def body(*refs):
        h_refs, send_sems, recv_sems = refs[:nt], refs[-2], refs[-1]
        x, y, c, _ = _place()
        cps = []
        for k, (_, layers) in enumerate(hs):
            rh = h_refs[k].shape[0] // (2 * layers)
            for layer in range(layers):
                half = h_refs[k].at[pl.ds((2 * layer + c) * rh, rh), :]
                cps.append(pltpu.make_async_remote_copy(
                    src_ref=half, dst_ref=half, send_sem=send_sems.at[len(cps)], recv_sem=recv_sems.at[len(cps)],
                    device_id=(x, y, 1 - c), device_id_type=MESH))
        for cp in cps:
            cp.start()
        for cp in cps:
            cp.wait()

    return _pc(
        body, name=name, in_specs=[HBM] * nt, out_specs=[HBM] * nt,
        out_shape=[jax.ShapeDtypeStruct(h.shape, F32) for h, _ in hs],
        input_output_aliases={k: k for k in range(nt)},
        scratch_shapes=[pltpu.SemaphoreType.DMA((n,)), pltpu.SemaphoreType.DMA((n,))],
    )(*[h for h, _ in hs])


def _split_start(name, bufs, ride, n, copies_of):
    nb = len(bufs)

    def body(*refs):
        sems = refs[nb + 1:nb + 1 + 2 * n]
        for cp in copies_of(refs[:nb], sems[:n], sems[n:]):
            (cp[0] if isinstance(cp, tuple) else cp).start()

    outs = _pc(
        body, name=name,
        out_shape=(pltpu.SemaphoreType.DMA(()),) * (2 * n) + tuple(pltpu.HBM(b.shape, b.dtype) for b in bufs)
        + (pltpu.HBM(ride.shape, ride.dtype),),
        in_specs=(HBM,) * (nb + 1), out_specs=(SEM,) * (2 * n) + (HBM,) * (nb + 1),
        input_output_aliases={k: 2 * n + k for k in range(nb + 1)},
        compiler_params=pltpu.CompilerParams(has_side_effects=EFFECT),
    )(*[pltpu.with_memory_space_constraint(b, pltpu.HBM) for b in bufs], pltpu.with_memory_space_constraint(ride, pltpu.HBM))
    return list(outs[:2 * n]), list(outs[2 * n:2 * n + nb]), outs[-1]


def _split_wait(name, bufs, sems, after, n, copies_of):
    nb = len(bufs)

    def body(*refs):
        s = refs[nb:nb + 2 * n]
        for cp in copies_of(refs[:nb], s[:n], s[n:]):
            sent, landed = cp if isinstance(cp, tuple) else (cp, cp)
            sent.wait_send()
            landed.wait_recv()

    outs = _pc(
        body, name=name, out_shape=tuple(pltpu.HBM(b.shape, b.dtype) for b in bufs),
        in_specs=(HBM,) * nb + (SEM,) * (2 * n) + (pl.BlockSpec(memory_space=pl.ANY),), out_specs=(HBM,) * nb,
        input_output_aliases={k: k for k in range(nb)},
        compiler_params=pltpu.CompilerParams(has_side_effects=EFFECT),
    )(*bufs, *sems, after)
    return list(outs)


def _handover_copies(refs, send, recv):
    x, y, c, chips = _place()
    cps = []
    for k, chip in enumerate(chips):
        mine, theirs = _half_block(refs[0], chip, c), _half_block(refs[0], chip, 1 - c)
        desc = lambda blk: pltpu.make_async_remote_copy(src_ref=blk, dst_ref=blk, send_sem=send[k], recv_sem=recv[k],
                                                        device_id=(x, y, 1 - c), device_id_type=MESH)
        cps.append((desc(mine), desc(theirs)))
    return cps


def _swap_copies(nt):
    def copies_of(refs, send, recv):
        x, y, c, _ = _place()
        cps = []
        for k in range(nt):
            rh = refs[k].shape[1] // 2
            cps.append(pltpu.make_async_remote_copy(
                src_ref=refs[k].at[:, pl.ds((1 - c) * rh, rh), :], dst_ref=refs[nt + k],
                send_sem=send[k], recv_sem=recv[k], device_id=(x, y, 1 - c), device_id_type=MESH))
        return cps
    return copies_of


def _exchange_copies(nt):
    def copies_of(refs, send, recv):
        x, y, c, chips = _place()
        cps = []
        for t in range(nt):
            for k, chip in enumerate(chips):
                cps.append(pltpu.make_async_remote_copy(
                    src_ref=refs[t].at[2 * chip[0] + chip[1]], dst_ref=refs[nt + t].at[k],
                    send_sem=send[3 * t + k], recv_sem=recv[3 * t + k], device_id=(*chip, c), device_id_type=MESH))
        return cps
    return copies_of


class _StagedReduce:
    def __init__(self, place):
        self.place = place
        self.groups = {}
        self.halves = {}

    @staticmethod
    def slab(t, r):
        return t.reshape(N_CHIPS, r, t.size // (N_CHIPS * r))

    def begin(self, g, grads, ride):
        names = list(grads)
        ts = [self.slab(t, r) for t, r in grads.values()]
        lands = [lax.empty((N_CHIPS, t.shape[1] // 2, t.shape[2]), F32) for t in ts]
        sems, bufs, ride = _split_start(f"grad_swap_start_{g}", ts + lands, ride, len(ts), _swap_copies(len(ts)))
        self.groups[g] = dict(names=names, bufs=bufs, sems=sems)
        return ride

    def advance(self, g, after, ride):
        st = self.groups[g]
        nt = len(st["names"])
        bufs = _split_wait(f"grad_swap_wait_{g}", st["bufs"], st["sems"], after, nt, _swap_copies(nt))
        st["ts"], st["ls"] = bufs[:nt], bufs[nt:]
        ps = [_pair_sum(t, l, self.place, f"pair_sum_{n}") for t, l, n in zip(st["ts"], st["ls"], st["names"])]
        lands = [lax.empty((3,) + p.shape[1:], BF16) for p in ps]
        st["sems"], st["bufs"], ride = _split_start(f"grad_exchange_start_{g}", ps + lands, ride, 3 * nt, _exchange_copies(nt))
        return ride

    def finish(self, g, after):
        st = self.groups[g]
        nt = len(st["names"])
        bufs = _split_wait(f"grad_exchange_wait_{g}", st["bufs"], st["sems"], after, 3 * nt, _exchange_copies(nt))
        for t, l, r, n in zip(st["ts"], st["ls"], bufs[nt:], st["names"]):
            if n[-1] in "01":
                self.halves[n[:-1]] = _final_sum(t, l, r, self.place, f"final_sum_{n}", layer=int(n[-1]), layers=2,
                                                 into=self.halves.get(n[:-1]))
            else:
                self.halves[n] = _final_sum(t, l, r, self.place, f"final_sum_{n}")


def _allgather_small(v):
    rows = v.shape[0]

    def body(v_ref, out_ref, send_sems, recv_sems):
        x, y, c, _ = _place()
        me = 4 * x + 2 * y + c
        out_ref[me] = v_ref[...]
        cps = []
        for k in range(1, 8):
            fx, fy, fc = (k >> 2) & 1, (k >> 1) & 1, k & 1
            to = (1 - x if fx else x, 1 - y if fy else y, 1 - c if fc else c)
            cps.append(pltpu.make_async_remote_copy(
                src_ref=v_ref, dst_ref=out_ref.at[me], send_sem=send_sems.at[k - 1], recv_sem=recv_sems.at[k - 1],
                device_id=to, device_id_type=MESH))
        for cp in cps:
            cp.start()
        for cp in cps:
            cp.wait()

    return _pc(
        body, name="allgather_small",
        in_specs=[pl.BlockSpec(memory_space=pltpu.VMEM)], out_specs=pl.BlockSpec(memory_space=pltpu.VMEM),
        out_shape=jax.ShapeDtypeStruct((8, rows, LANES), F32),
        scratch_shapes=[pltpu.SemaphoreType.DMA((7,)), pltpu.SemaphoreType.DMA((7,))],
    )(v)


def _adamw_math(w, g, m, v):
    m = ADAM_B1 * m + (1.0 - ADAM_B1) * g
    v = ADAM_B2 * v + (1.0 - ADAM_B2) * (g * g)
    m_hat = m / (1.0 - ADAM_B1 ** ADAM_STEP)
    v_hat = v / (1.0 - ADAM_B2 ** ADAM_STEP)
    return -ADAM_LR * (m_hat / (jnp.sqrt(v_hat) + ADAM_EPS) + ADAM_WD * w), m, v


def _adamw(w, g, m, v, name):
    r, cols = w.shape
    tr = min(r, 256)

    def body(w_ref, g_ref, m_ref, v_ref, go_ref, d_ref, mo_ref, vo_ref):
        gv = g_ref[...]
        d, mn, vn = _adamw_math(w_ref[...], gv, m_ref[...], v_ref[...])
        go_ref[...] = gv
        d_ref[...] = d
        mo_ref[...] = mn
        vo_ref[...] = vn

    row = pl.BlockSpec((tr, cols), lambda i: (i, 0))
    return _pc(
        body, name=name, grid=(r // tr,), in_specs=[row] * 4, out_specs=[row] * 4,
        out_shape=[jax.ShapeDtypeStruct((r, cols), F32)] * 4,
        compiler_params=_params(("parallel",)),
    )(w, g, m, v)


def _adamw_small(w, gathered, m, v):
    rows = w.shape[0]

    def body(w_ref, g_ref, m_ref, v_ref, go_ref, d_ref, mo_ref, vo_ref):
        g = g_ref[0]
        for k in range(1, 8):
            g = g + g_ref[k]
        d, mn, vn = _adamw_math(w_ref[...], g, m_ref[...], v_ref[...])
        go_ref[...] = g
        d_ref[...] = d
        mo_ref[...] = mn
        vo_ref[...] = vn

    return _pc(
        body, name="adamw_small",
        out_shape=[jax.ShapeDtypeStruct((rows, LANES), F32)] * 4,
    )(w, gathered, m, v)


_BIAS_ROWS = 32


def _own_slot(flat, chip):
    return lax.dynamic_update_slice(lax.empty((N_CHIPS,) + flat.shape, flat.dtype), flat[None], (chip, 0, 0))


def _pack_first(hyb_w_in, hyb_w_out):
    return jnp.concatenate([t.astype(BF16).reshape(-1, 1024) for t in (hyb_w_in, hyb_w_out)], axis=0)


def _unpack_first(g):
    return {"hyb_w_in": g[:, 0:768, :].reshape(N_CHIPS, 1024, 768), "hyb_w_out": g[:, 768:1024, :].reshape(1024, 1024)}


def _pack_rest(mlp_w_up, mlp_w_down, swa_w_qkv, swa_w_out, swa_b_qkv):
    parts = [t.astype(BF16).reshape(-1, 1024) for t in (mlp_w_up, mlp_w_down, swa_w_qkv, swa_w_out)]
    bias = lax.bitcast_convert_type(swa_b_qkv.reshape(384), BF16).reshape(1, 768)
    bias = jnp.pad(bias, ((0, _BIAS_ROWS - 1), (0, 256)))
    return jnp.concatenate(parts + [bias], axis=0)


def _unpack_rest(g):
    W = {
        "packed": g,
        "swa_w_qkv": g[:, 4096:4480, :].reshape(N_CHIPS, 1024, 384),
        "swa_w_out": g[:, 4480:4736, :].reshape(1024, 1024),
    }
    bias = lax.bitcast_convert_type(g[:, 4736, :768].reshape(N_CHIPS, 384, 2), F32).reshape(1536)
    return W, bias


_SMALL = (("norm_mix", 16), ("norm_mlp", 16), ("ret_gn_gain", 4), ("dil_q_gain", 1), ("dil_k_gain", 1),
          ("swa_b_qkv", 12), ("swa_q_gain", 1), ("swa_k_gain", 1), ("swa_sinks", 1), ("loss", 1))
_SUBLANES = 8


def _slot(r):
    return -(-r // _SUBLANES) * _SUBLANES


def _pack_small(d):
    return jnp.concatenate([jnp.pad(d[n].reshape(r, LANES), ((0, _slot(r) - r), (0, 0))) for n, r in _SMALL], axis=0)


def _unpack_small(p):
    out, o = {}, 0
    for n, r in _SMALL:
        out[n] = p[o:o + r]
        o += _slot(r)
    return out


def kernel(x, positions, norm_mix, norm_mlp, mlp_w_up, mlp_w_down, hyb_w_in, hyb_w_out, ret_gn_gain, dil_q_gain, dil_k_gain, swa_w_qkv, swa_b_qkv, swa_w_out, swa_q_gain, swa_k_gain, swa_sinks, loss_target, m_norm_mix, m_norm_mlp, m_mlp_w_up, m_mlp_w_down, m_hyb_w_in, m_hyb_w_out, m_ret_gn_gain, m_dil_q_gain, m_dil_k_gain, m_swa_w_qkv, m_swa_b_qkv, m_swa_w_out, m_swa_q_gain, m_swa_k_gain, m_swa_sinks, v_norm_mix, v_norm_mlp, v_mlp_w_up, v_mlp_w_down, v_hyb_w_in, v_hyb_w_out, v_ret_gn_gain, v_dil_q_gain, v_dil_k_gain, v_swa_w_qkv, v_swa_b_qkv, v_swa_w_out, v_swa_q_gain, v_swa_k_gain, v_swa_sinks):
    ax, ay, ac = lax.axis_index("x"), lax.axis_index("y"), lax.axis_index("c")
    chip = 2 * ax + ay
    place = jnp.stack([chip, ac]).astype(jnp.int32)
    S = x.shape[1]

    first = _own_slot(_pack_first(hyb_w_in[0], hyb_w_out[0]), chip)
    rest = _own_slot(_pack_rest(mlp_w_up, mlp_w_down, swa_w_qkv[0], swa_w_out[0], swa_b_qkv[0]), chip)
    *sems, first, pos_col = _gather_start(first, positions.reshape(S, 1), "allgather_first_start")
    flight = {}

    def first_of(after):
        g = _gather_handover(_gather_wait(first, sems, after, "allgather_first_wait"), "allgather_first_handover")
        *flight["sems"], flight["buf"], g = _gather_start(rest, g, "allgather_rest_start")
        return _unpack_first(g)

    def rest_begin(ride):
        buf = _gather_wait(flight["buf"], flight["sems"], ride, "allgather_rest_wait")
        flight["sems"], flight["bufs"], ride = _split_start("allgather_rest_handover_start", [buf], ride, 3, _handover_copies)
        return ride

    def rest_of(after):
        return _unpack_rest(_split_wait("allgather_rest_handover_wait", flight["bufs"], flight["sems"], after, 3,
                                        _handover_copies)[0])

    P = dict(norm_mix=norm_mix, norm_mlp=norm_mlp, ret_gn_gain=ret_gn_gain, dil_q_gain=dil_q_gain, dil_k_gain=dil_k_gain,
             swa_q_gain=swa_q_gain, swa_k_gain=swa_k_gain, swa_sinks=swa_sinks)

    red = _StagedReduce(place)
    loss_l, grad_x, gp = _local_step(x[0], pos_col, loss_target[0], first_of, rest_begin, rest_of, P, red)

    params = dict(mlp_w_up=(mlp_w_up, m_mlp_w_up, v_mlp_w_up), mlp_w_down=(mlp_w_down, m_mlp_w_down, v_mlp_w_down),
                  hyb_w_in=(hyb_w_in, m_hyb_w_in, v_hyb_w_in), hyb_w_out=(hyb_w_out, m_hyb_w_out, v_hyb_w_out),
                  swa_w_qkv=(swa_w_qkv, m_swa_w_qkv, v_swa_w_qkv), swa_w_out=(swa_w_out, m_swa_w_out, v_swa_w_out))
    big = {}

    def update(names, share_name):
        hs = [(red.halves[n], params[n][0].shape[0]) for n in names]
        for n, g in zip(names, _share_halves(hs, share_name)):
            rows = g.shape[0]
            w, m, v = (t.reshape(rows, -1) for t in params[n])
            big[n] = [t.reshape(params[n][0].shape) for t in _adamw(w, g, m, v, f"adamw_{n}")]

    names = ["mlp_w_up", "mlp_w_down", "hyb_w_out", "swa_w_qkv", "swa_w_out"]
    red.halves[names[0]] = red.advance("win", grad_x, red.halves[names[0]])
    update(names, "grad_share_halves")
    red.finish("win", big[names[-1]][1])
    update(["hyb_w_in"], "grad_share_last")

    gsm = dict(gp, loss=loss_l)
    gsm["swa_sinks"] = jnp.pad(gp["swa_sinks"].reshape(16, HEAD)[:, 0], (0, LANES - 16))
    gathered = _allgather_small(_pack_small(gsm))

    def small_pack(norm_mix, norm_mlp, gn, dq, dk, b, sq, sk, sinks):
        dup = lambda t: jnp.tile(t.reshape(1, HEAD), (1, 2))
        bias = lax.dynamic_update_slice(jnp.zeros((12, LANES), F32), b.reshape(3, LANES), (3 * chip, 0))
        return _pack_small(dict(norm_mix=norm_mix, norm_mlp=norm_mlp, ret_gn_gain=gn, dil_q_gain=dup(dq), dil_k_gain=dup(dk),
                                swa_b_qkv=bias, swa_q_gain=dup(sq), swa_k_gain=dup(sk),
                                swa_sinks=jnp.pad(sinks.reshape(16), (0, LANES - 16)), loss=jnp.zeros((1, LANES), F32)))

    pw = small_pack(norm_mix, norm_mlp, ret_gn_gain, dil_q_gain, dil_k_gain, swa_b_qkv, swa_q_gain, swa_k_gain, swa_sinks)
    pm = small_pack(m_norm_mix, m_norm_mlp, m_ret_gn_gain, m_dil_q_gain, m_dil_k_gain, m_swa_b_qkv, m_swa_q_gain, m_swa_k_gain, m_swa_sinks)
    pv = small_pack(v_norm_mix, v_norm_mlp, v_ret_gn_gain, v_dil_q_gain, v_dil_k_gain, v_swa_b_qkv, v_swa_q_gain, v_swa_k_gain, v_swa_sinks)
    small = [_unpack_small(t) for t in _adamw_small(pw, gathered, pm, pv)]

    def small_out(n, k):
        t = small[k][n]
        if n in ("norm_mix", "norm_mlp"):
            return t.reshape(2, D_MODEL)
        if n == "ret_gn_gain":
            return t.reshape(1, RET_HEADS, 128)
        if n == "swa_b_qkv":
            return lax.dynamic_slice(t, (3 * chip, 0), (3, LANES)).reshape(1, 384)
        if n == "swa_sinks":
            return t[0, :16].reshape(1, 16)
        return t[0, :HEAD].reshape(1, HEAD)

    order = ["norm_mix", "norm_mlp", "mlp_w_up", "mlp_w_down", "hyb_w_in", "hyb_w_out", "ret_gn_gain", "dil_q_gain",
             "dil_k_gain", "swa_w_qkv", "swa_b_qkv", "swa_w_out", "swa_q_gain", "swa_k_gain", "swa_sinks"]
    is_big = {"mlp_w_up", "mlp_w_down", "hyb_w_in", "hyb_w_out", "swa_w_qkv", "swa_w_out"}
    outs = [small[0]["loss"][0, 0], grad_x[None]]
    for k in range(4):
        outs += [big[n][k] if n in is_big else small_out(n, k) for n in order]
    return tuple(outs)
```

```python
import functools
import math

import numpy as np
import jax
import jax.numpy as jnp
from jax import lax
from jax.experimental import pallas as pl
from jax.experimental.pallas import tpu as pltpu

F32, BF16 = jnp.float32, jnp.bfloat16
HIGHEST = lax.Precision.HIGHEST
MESH = pl.DeviceIdType.MESH

LANES = 128
VMEM_LIMIT = 48 << 20
D_MODEL = 1024
D_FF = 4096
HEAD = 64
EPS = 1e-6
BLK = 128
RET_HEADS = 4
RET_THETA = 10000.0
ROPE_THETA = 500000.0
ROPE_DIMS = 16
DIL_PATTERNS = ((128, 1), (512, 4), (2048, 16))
SWA_DIST = 127
N_CHIPS = 4
ADAM_LR, ADAM_B1, ADAM_B2, ADAM_EPS, ADAM_WD, ADAM_STEP = 0.001, 0.9, 0.999, 1e-08, 0.01, 10

_LOG_GAMMA = [float(np.log1p(-np.exp2(np.float32(-5.0 - h)))) for h in range(RET_HEADS)]


def _pc(body, **kw):
    return pl.pallas_call(body, **kw)


def _params(sem):
    return pltpu.CompilerParams(dimension_semantics=sem, vmem_limit_bytes=VMEM_LIMIT)


def _matmul(a, b, *, dims, tm, tn, tk, outs, name, epilogue=None, extras=(), b_cs=False, b_rs=0, b_row0=0, b_rows=0,
            o_cs=0, a_pro=None):
    if dims == "nn":
        M, K = a.shape
        N = b.shape[0] * b.shape[2] if b_cs else b.shape[1]
        a_spec = pl.BlockSpec((tm, tk), lambda i, j, k: (i, k))
        if b_cs:
            npt = b.shape[2] // tn
            b_spec = pl.BlockSpec((None, tk, tn), lambda i, j, k: (j // npt, k + b_row0, j % npt))
        elif b_rs:
            K, N, kps = b.shape[0] * b_rs, b.shape[2], b_rs // tk
            b_spec = pl.BlockSpec((None, tk, tn), lambda i, j, k: (k // kps, b_row0 + k % kps, j))
        else:
            b_spec = pl.BlockSpec((tk, tn), lambda i, j, k: (k, j))
        contract = (((1,), (0,)), ((), ()))
    elif dims == "nt":
        M, K = a.shape
        N = (b_rows or b.shape[1]) if b_cs else b.shape[0]
        a_spec = pl.BlockSpec((tm, tk), lambda i, j, k: (i, k))
        if b_cs:
            kpt = b.shape[2] // tk
            b_spec = pl.BlockSpec((None, tn, tk), lambda i, j, k: (k // kpt, j + b_row0, k % kpt))
        elif b_rs:
            N, jps = b.shape[0] * b_rs, b_rs // tn
            b_spec = pl.BlockSpec((None, tn, tk), lambda i, j, k: (j // jps, b_row0 + j % jps, k))
        else:
            b_spec = pl.BlockSpec((tn, tk), lambda i, j, k: (j, k))
        contract = (((1,), (1,)), ((), ()))
    else:
        K, M = a.shape
        N = b.shape[1]
        a_spec = pl.BlockSpec((tk, tm), lambda i, j, k: (k, i))
        b_spec = pl.BlockSpec((tk, tn), lambda i, j, k: (k, j))
        contract = (((0,), (0,)), ((), ()))
    assert M % tm == 0 and N % tn == 0 and K % tk == 0, (name, M, N, K, tm, tn, tk)
    nk = K // tk
    ex_specs = []
    for arr, kind in extras:
        if kind == "mn":
            ex_specs.append(pl.BlockSpec((tm, tn), lambda i, j, k: (i, j)))
        elif kind == "n":
            ex_specs.append(pl.BlockSpec((1, tn), lambda i, j, k: (0, j)))
        elif kind == "full":
            ex_specs.append(pl.BlockSpec(arr.shape, lambda i, j, k, nd=arr.ndim: (0,) * nd))
        else:
            ex_specs.append(pl.BlockSpec((tm, kind), lambda i, j, k: (i, 0)))
    if o_cs:
        n_sh = N // o_cs
        opt = n_sh // tn
        o_shape = (o_cs, M, n_sh)
        o_spec = pl.BlockSpec((None, tm, tn), lambda i, j, k: (j // opt, i, j % opt))
    else:
        o_shape = (M, N)
        o_spec = pl.BlockSpec((tm, tn), lambda i, j, k: (i, j))
    o_specs, o_shapes, summed = [], [], []
    for o in outs:
        if isinstance(o, tuple) and o[0] == "colsum":
            assert N == tn
            o_specs.append(pl.BlockSpec((1, tn), lambda i, j, k: (0, j)))
            o_shapes.append(jax.ShapeDtypeStruct((1, N), F32))
            summed.append(True)
        elif isinstance(o, tuple):
            o_specs.append(pl.BlockSpec((tm, o[1]), lambda i, j, k: (i, 0)))
            o_shapes.append(jax.ShapeDtypeStruct((M, o[1]), o[0]))
            summed.append(False)
        else:
            o_specs.append(o_spec)
            o_shapes.append(jax.ShapeDtypeStruct(o_shape, o))
            summed.append(False)
    n_ex, n_out = len(extras), len(outs)
    if epilogue is None:
        epilogue = lambda acc: (acc,)

    def body(a_ref, b_ref, *rest):
        ex, o_refs, acc = rest[:n_ex], rest[n_ex:n_ex + n_out], rest[-1]
        i, k = pl.program_id(0), pl.program_id(2)

        @pl.when(k == 0)
        def _():
            acc[...] = jnp.zeros_like(acc)

        av = a_ref[...] if a_pro is None else a_pro(a_ref[...])
        acc[...] += lax.dot_general(av.astype(BF16), b_ref[...].astype(BF16), contract, preferred_element_type=F32)

        @pl.when(k == nk - 1)
        def _():
            vals = epilogue(acc[...], *[e[...] for e in ex])
            for r, v, sm in zip(o_refs, vals, summed):
                if sm:
                    @pl.when(i == 0)
                    def _(r=r):
                        r[...] = jnp.zeros_like(r)

                    r[...] += v
                else:
                    r[...] = v.astype(r.dtype)

    res = _pc(
        body, name=name, grid=(M // tm, N // tn, nk),
        in_specs=[a_spec, b_spec] + ex_specs, out_specs=o_specs, out_shape=o_shapes,
        scratch_shapes=[pltpu.VMEM((tm, tn), F32)],
        compiler_params=_params(("arbitrary" if any(summed) else "parallel", "parallel", "arbitrary")),
    )(a, b, *[e for e, _ in extras])
    return res[0] if n_out == 1 else res


def _roll(x, s):
    return pltpu.roll(x, s % LANES, 1)


def _rope(x, A, B, C, half):
    return x * A + _roll(x, LANES - half) * B + _roll(x, half) * C


def _rope_t(g, A, B, C, half):
    return g * A + _roll(g * B, half) + _roll(g * C, LANES - half)


def _gmean(x, G):
    hi = x.astype(BF16)
    lo = (x - hi.astype(F32)).astype(BF16)
    Gb = G.astype(BF16)
    return jnp.dot(hi, Gb, preferred_element_type=F32) + jnp.dot(lo, Gb, preferred_element_type=F32)


def _head_mask(shape, half):
    lane = lax.broadcasted_iota(jnp.int32, shape, len(shape) - 1)
    return (lane >= HEAD) if half else (lane < HEAD)


def _group_matrix():
    i = np.arange(LANES)
    return jnp.asarray((i[:, None] // HEAD == i[None, :] // HEAD).astype(np.float32) / HEAD)


def _rope_inv():
    l = np.arange(LANES) % HEAD
    inv_r = np.power(np.float32(RET_THETA), -(l % 32).astype(np.float32) * np.float32(2.0 / HEAD))
    hp = ROPE_DIMS // 2
    inv_p = np.power(np.float32(ROPE_THETA), -(l % hp).astype(np.float32) * np.float32(2.0 / ROPE_DIMS))
    inv_p = np.where(l < ROPE_DIMS, inv_p, 0.0)
    return jnp.asarray(np.stack([inv_r, inv_p]).astype(np.float32))


def _tables(pos_col):
    S = pos_col.shape[0]
    tm = 512
    hp = ROPE_DIMS // 2

    def body(p_ref, inv_ref, o_ref):
        p = p_ref[...].astype(F32)
        lane = lax.broadcasted_iota(jnp.int32, (tm, LANES), 1) % HEAD
        ang = p * inv_ref[0:1, :]
        c, s = jnp.cos(ang), jnp.sin(ang)
        o_ref[:, 0:128] = c
        o_ref[:, 128:256] = jnp.where(lane < 32, -s, 0.0)
        o_ref[:, 256:384] = jnp.where(lane >= 32, s, 0.0)
        ang = p * inv_ref[1:2, :]
        c, s = jnp.cos(ang), jnp.sin(ang)
        o_ref[:, 384:512] = c
        o_ref[:, 512:640] = jnp.where(lane < hp, -s, 0.0)
        o_ref[:, 640:768] = jnp.where((lane >= hp) & (lane < ROPE_DIMS), s, 0.0)

    return _pc(
        body, name="rope_tables", grid=(S // tm,),
        in_specs=[pl.BlockSpec((tm, 1), lambda i: (i, 0)), pl.BlockSpec((2, LANES), lambda i: (0, 0))],
        out_specs=pl.BlockSpec((tm, 768), lambda i: (i, 0)),
        out_shape=jax.ShapeDtypeStruct((S, 768), F32),
        compiler_params=_params(("parallel",)),
    )(pos_col, _rope_inv())


def _tab(tab_ref, which):
    o = 384 * which
    return tab_ref[:, o:o + 128], tab_ref[:, o + 128:o + 256], tab_ref[:, o + 256:o + 384]


def _rms_fwd(x, g, name):
    S, Dm = x.shape
    tm = 512

    def body(x_ref, g_ref, h_ref):
        xv = x_ref[...]
        r = lax.rsqrt(jnp.mean(xv * xv, axis=-1, keepdims=True) + EPS)
        h_ref[...] = (xv * r * g_ref[...]).astype(BF16)

    return _pc(
        body, name=name, grid=(S // tm,),
        in_specs=[pl.BlockSpec((tm, Dm), lambda i: (i, 0)), pl.BlockSpec((1, Dm), lambda i: (0, 0))],
        out_specs=pl.BlockSpec((tm, Dm), lambda i: (i, 0)),
        out_shape=jax.ShapeDtypeStruct((S, Dm), BF16),
        compiler_params=_params(("parallel",)),
    )(x, g.reshape(1, Dm))


def _rms_bwd(x, g, dh, dres, name):
    S, Dm = x.shape
    tm = 512

    def body(x_ref, g_ref, dh_ref, dres_ref, dx_ref, dxb_ref, dg_ref):
        xv, dhv = x_ref[...], dh_ref[...]
        r = lax.rsqrt(jnp.mean(xv * xv, axis=-1, keepdims=True) + EPS)
        t = dhv * g_ref[...]
        dx = dres_ref[...] + r * t - xv * (r * r * r) * jnp.mean(xv * t, axis=-1, keepdims=True)
        dx_ref[...] = dx
        dxb_ref[...] = dx.astype(BF16)

        @pl.when(pl.program_id(0) == 0)
        def _():
            dg_ref[...] = jnp.zeros_like(dg_ref)

        dg_ref[...] += jnp.sum(dhv * xv * r, axis=0, keepdims=True)

    row = pl.BlockSpec((tm, Dm), lambda i: (i, 0))
    vec = pl.BlockSpec((1, Dm), lambda i: (0, 0))
    return _pc(
        body, name=name, grid=(S // tm,),
        in_specs=[row, vec, row, row], out_specs=[row, row, vec],
        out_shape=[jax.ShapeDtypeStruct((S, Dm), F32), jax.ShapeDtypeStruct((S, Dm), BF16),
                   jax.ShapeDtypeStruct((1, Dm), F32)],
        compiler_params=_params(("arbitrary",)),
    )(x, g.reshape(1, Dm), dh, dres)


def _hn_fwd(x, gain, G):
    r = lax.rsqrt(_gmean(x * x, G) + EPS)
    return x * r * gain


def _hn_bwd(x, gain, dy, G):
    r = lax.rsqrt(_gmean(x * x, G) + EPS)
    t = dy * gain
    dx = r * t - x * (r * r * r) * _gmean(x * t, G)
    return dx, jnp.sum(dy * x * r, axis=0, keepdims=True)


def _fold_halves(v):
    return v + _roll(v, HEAD)


def _even_pre_fwd(proj, tab, qg, kg):
    S = proj.shape[0]
    tm = 256

    def body(p_ref, tab_ref, qg_ref, kg_ref, g_ref, rq_ref, rk_ref, rv_ref, dq_ref, dk_ref, dv_ref):
        Ar, Br, Cr = _tab(tab_ref, 0)
        Ap, Bp, Cp = _tab(tab_ref, 1)
        G = g_ref[...]
        for c in range(2):
            sl = slice(c * 128, (c + 1) * 128)
            rq_ref[:, sl] = _rope(p_ref[:, c * 128:(c + 1) * 128], Ar, Br, Cr, 32).astype(BF16)
            rk_ref[:, sl] = (_rope(p_ref[:, 256 + c * 128:256 + (c + 1) * 128], Ar, Br, Cr, 32) * 0.125).astype(BF16)
        rv_ref[...] = p_ref[:, 512:1024].astype(BF16)
        for c in range(4):
            sl = slice(c * 128, (c + 1) * 128)
            q = _hn_fwd(p_ref[:, 1536 + c * 128:1536 + (c + 1) * 128], qg_ref[...], G)
            dq_ref[:, sl] = _rope(q, Ap, Bp, Cp, 8).astype(BF16)
            k = _hn_fwd(p_ref[:, 2048 + c * 128:2048 + (c + 1) * 128], kg_ref[...], G)
            dk_ref[:, sl] = _rope(k, Ap, Bp, Cp, 8).astype(BF16)
        dv_ref[...] = p_ref[:, 2560:3072].astype(BF16)

    row = lambda w: pl.BlockSpec((tm, w), lambda i: (i, 0))
    vec = pl.BlockSpec((1, LANES), lambda i: (0, 0))
    return _pc(
        body, name="even_pre_fwd", grid=(S // tm,),
        in_specs=[row(3072), row(768), vec, vec, pl.BlockSpec((LANES, LANES), lambda i: (0, 0))],
        out_specs=[row(256), row(256), row(512), row(512), row(512), row(512)],
        out_shape=[jax.ShapeDtypeStruct((S, w), BF16) for w in (256, 256, 512, 512, 512, 512)],
        compiler_params=_params(("parallel",)),
    )(proj, tab, qg, kg, _group_matrix())


def _even_pre_bwd(proj, tab, qg, kg, drq, drk, drv, drg, dqs, dks, dvs):
    S = proj.shape[0]
    tm = 256
    npat = len(dqs)

    def body(p_ref, tab_ref, qg_ref, kg_ref, g_ref, drq_ref, drk_ref, drv_ref, drg_ref, *rest):
        dq_refs, dk_refs, dv_refs = rest[:npat], rest[npat:2 * npat], rest[2 * npat:3 * npat]
        dp_ref, dqg_ref, dkg_ref = rest[3 * npat:]
        Ar, Br, Cr = _tab(tab_ref, 0)
        Ap, Bp, Cp = _tab(tab_ref, 1)
        G = g_ref[...]
        for c in range(2):
            sl = slice(c * 128, (c + 1) * 128)
            dp_ref[:, c * 128:(c + 1) * 128] = _rope_t(drq_ref[:, sl], Ar, Br, Cr, 32).astype(BF16)
            dp_ref[:, 256 + c * 128:256 + (c + 1) * 128] = _rope_t(drk_ref[:, sl] * 0.125, Ar, Br, Cr, 32).astype(BF16)
        dp_ref[:, 512:1024] = drv_ref[...].astype(BF16)
        dp_ref[:, 1024:1536] = drg_ref[...].astype(BF16)
        accq = jnp.zeros((1, LANES), F32)
        acck = jnp.zeros((1, LANES), F32)
        for c in range(4):
            sl = slice(c * 128, (c + 1) * 128)
            g = dq_refs[0][:, sl]
            for r in dq_refs[1:]:
                g = g + r[:, sl]
            dx, dg = _hn_bwd(p_ref[:, 1536 + c * 128:1536 + (c + 1) * 128], qg_ref[...], _rope_t(g, Ap, Bp, Cp, 8), G)
            dp_ref[:, 1536 + c * 128:1536 + (c + 1) * 128] = dx.astype(BF16)
            accq = accq + dg
            g = dk_refs[0][:, sl]
            for r in dk_refs[1:]:
                g = g + r[:, sl]
            dx, dg = _hn_bwd(p_ref[:, 2048 + c * 128:2048 + (c + 1) * 128], kg_ref[...], _rope_t(g, Ap, Bp, Cp, 8), G)
            dp_ref[:, 2048 + c * 128:2048 + (c + 1) * 128] = dx.astype(BF16)
            acck = acck + dg
        g = dv_refs[0][...]
        for r in dv_refs[1:]:
            g = g + r[...]
        dp_ref[:, 2560:3072] = g.astype(BF16)

        @pl.when(pl.program_id(0) == 0)
        def _():
            dqg_ref[...] = jnp.zeros_like(dqg_ref)
            dkg_ref[...] = jnp.zeros_like(dkg_ref)

        dqg_ref[...] += _fold_halves(accq)
        dkg_ref[...] += _fold_halves(acck)

    row = lambda w: pl.BlockSpec((tm, w), lambda i: (i, 0))
    vec = pl.BlockSpec((1, LANES), lambda i: (0, 0))
    return _pc(
        body, name="even_pre_bwd", grid=(S // tm,),
        in_specs=[row(3072), row(768), vec, vec, pl.BlockSpec((LANES, LANES), lambda i: (0, 0)),
                  row(256), row(256), row(512), row(512)] + [row(512)] * (3 * npat),
        out_specs=[row(3072), vec, vec],
        out_shape=[jax.ShapeDtypeStruct((S, 3072), BF16), jax.ShapeDtypeStruct((1, LANES), F32),
                   jax.ShapeDtypeStruct((1, LANES), F32)],
        compiler_params=_params(("arbitrary",)),
    )(proj, tab, qg, kg, _group_matrix(), drq, drk, drv, drg, *dqs, *dks, *dvs)


def _ret_consts(pair, half):
    lg = jnp.where(pair == 0, _LOG_GAMMA[half], _LOG_GAMMA[2 + half]).astype(F32)
    i = lax.broadcasted_iota(jnp.int32, (BLK, BLK), 0)
    j = lax.broadcasted_iota(jnp.int32, (BLK, BLK), 1)
    diff = (i - j).astype(F32)
    decay = jnp.where(diff >= 0, jnp.exp(lg * jnp.maximum(diff, 0.0)), 0.0)
    t = lax.broadcasted_iota(jnp.int32, (BLK, 1), 0).astype(F32)
    xi = jnp.exp(lg * (t + 1.0))
    zeta = jnp.exp(lg * (BLK - 1.0 - t))
    cd = jnp.exp(jnp.full((1, 1), BLK, F32) * lg)
    return decay, xi, zeta, cd


RET_STEP = 8


def _ret_fwd(rq, rk, rv):
    S = rq.shape[0]
    nc = S // BLK
    rows = RET_STEP * BLK

    def body(q_ref, k_ref, v_ref, o_ref, st_ref, R):
        p, n = pl.program_id(0), pl.program_id(1)

        @pl.when(n == 0)
        def _():
            R[...] = jnp.zeros_like(R)

        consts = [_ret_consts(p, half) for half in range(2)]
        masks = [_head_mask((BLK, LANES), half) for half in range(2)]
        for ci in range(RET_STEP):
            rs = slice(ci * BLK, (ci + 1) * BLK)
            q2, k2 = q_ref[rs, :], k_ref[rs, :]
            for half in range(2):
                decay, xi, zeta, cd = consts[half]
                m = masks[half]
                qm = jnp.where(m, q2, jnp.zeros_like(q2))
                km = jnp.where(m, k2, jnp.zeros_like(k2))
                v = v_ref[rs, half * 128:(half + 1) * 128]
                Rb = R[half].astype(BF16)
                st_ref[ci, half] = Rb
                sc = lax.dot_general(qm, k2, (((1,), (1,)), ((), ())), preferred_element_type=F32) * decay
                o = jnp.dot(sc.astype(BF16), v, preferred_element_type=F32)
                o = o + jnp.dot(qm, Rb, preferred_element_type=F32) * xi
                o_ref[rs, half * 128:(half + 1) * 128] = o
                kz = (km.astype(F32) * zeta).astype(BF16)
                R[half] = R[half] * cd + lax.dot_general(kz, v, (((0,), (0,)), ((), ())), preferred_element_type=F32)

    return _pc(
        body, name="ret_fwd", grid=(2, nc // RET_STEP),
        in_specs=[pl.BlockSpec((rows, 128), lambda p, n: (n, p)), pl.BlockSpec((rows, 128), lambda p, n: (n, p)),
                  pl.BlockSpec((rows, 256), lambda p, n: (n, p))],
        out_specs=[pl.BlockSpec((rows, 256), lambda p, n: (n, p)),
                   pl.BlockSpec((None, RET_STEP, 2, 128, 128), lambda p, n: (p, n, 0, 0, 0))],
        out_shape=[jax.ShapeDtypeStruct((S, 512), F32), jax.ShapeDtypeStruct((2, nc, 2, 128, 128), BF16)],
        scratch_shapes=[pltpu.VMEM((2, 128, 128), F32)],
        compiler_params=_params(("parallel", "arbitrary")),
    )(rq, rk, rv)


def _ret_bwd(rq, rk, rv, states, do):
    S = rq.shape[0]
    nc = S // BLK
    ns = nc // RET_STEP
    rows = RET_STEP * BLK
    nt = (((1,), (1,)), ((), ()))
    tn = (((0,), (0,)), ((), ()))

    def body(q_ref, k_ref, v_ref, st_ref, do_ref, dq_ref, dk_ref, dv_ref, U):
        p, n = pl.program_id(0), pl.program_id(1)

        @pl.when(n == 0)
        def _():
            U[...] = jnp.zeros_like(U)

        consts = [_ret_consts(p, half) for half in range(2)]
        masks = [_head_mask((BLK, LANES), half) for half in range(2)]
        for ci in reversed(range(RET_STEP)):
            rs = slice(ci * BLK, (ci + 1) * BLK)
            q2, k2 = q_ref[rs, :], k_ref[rs, :]
            dq_acc = jnp.zeros((BLK, LANES), F32)
            dk_acc = jnp.zeros((BLK, LANES), F32)
            for half in range(2):
                decay, xi, zeta, cd = consts[half]
                m = masks[half]
                qm = jnp.where(m, q2, jnp.zeros_like(q2))
                km = jnp.where(m, k2, jnp.zeros_like(k2))
                v = v_ref[rs, half * 128:(half + 1) * 128]
                dob = do_ref[rs, half * 128:(half + 1) * 128].astype(BF16)
                Rb = st_ref[ci, half]
                Ub = U[half].astype(BF16)
                dsc = (lax.dot_general(dob, v, nt, preferred_element_type=F32) * decay).astype(BF16)
                xdo = (dob.astype(F32) * xi).astype(BF16)
                dq_acc += jnp.dot(dsc, km, preferred_element_type=F32) + lax.dot_general(xdo, Rb, nt, preferred_element_type=F32)
                dk_acc += lax.dot_general(dsc, qm, tn, preferred_element_type=F32) \
                    + lax.dot_general(v, Ub, nt, preferred_element_type=F32) * zeta
                sc = (lax.dot_general(qm, k2, nt, preferred_element_type=F32) * decay).astype(BF16)
                kz = (km.astype(F32) * zeta).astype(BF16)
                dv_ref[rs, half * 128:(half + 1) * 128] = lax.dot_general(sc, dob, tn, preferred_element_type=F32) \
                    + jnp.dot(kz, Ub, preferred_element_type=F32)
                U[half] = U[half] * cd + lax.dot_general(qm, xdo, tn, preferred_element_type=F32)
            dq_ref[rs, :] = dq_acc
            dk_ref[rs, :] = dk_acc

    rev = lambda w: pl.BlockSpec((rows, w), lambda p, n: (ns - 1 - n, p))
    return _pc(
        body, name="ret_bwd", grid=(2, ns),
        in_specs=[rev(128), rev(128), rev(256),
                  pl.BlockSpec((None, RET_STEP, 2, 128, 128), lambda p, n: (p, ns - 1 - n, 0, 0, 0)), rev(256)],
        out_specs=[rev(128), rev(128), rev(256)],
        out_shape=[jax.ShapeDtypeStruct((S, 256), F32), jax.ShapeDtypeStruct((S, 256), F32),
                   jax.ShapeDtypeStruct((S, 512), F32)],
        scratch_shapes=[pltpu.VMEM((2, 128, 128), F32)],
        compiler_params=_params(("parallel", "arbitrary")),
    )(rq, rk, rv, states, do)


def _col_of(b, m):
    return jnp.max(jnp.where(m, b, -jnp.inf), axis=1, keepdims=True)


def _attn_fwd(q, k, v, *, nq, max_dist, name, sinks=None, want_bf16=False):
    L, Ck = k.shape
    nb, ncol = L // BLK, Ck // LANES
    scale = HEAD ** -0.5
    has_sink = sinks is not None

    def body(*refs):
        q_ref, kp_ref, kc_ref, vp_ref, vc_ref = refs[:5]
        sk_ref = refs[5] if has_sink else None
        outs = refs[5 + has_sink:]
        n = pl.program_id(1)
        kcat = jnp.concatenate([kp_ref[...], kc_ref[...]], axis=0)
        vcat = jnp.concatenate([vp_ref[...], vc_ref[...]], axis=0)
        r = lax.broadcasted_iota(jnp.int32, (BLK, 2 * BLK), 0)
        c = lax.broadcasted_iota(jnp.int32, (BLK, 2 * BLK), 1)
        dist = r + BLK - c
        valid = (dist >= 0) & (dist <= max_dist) & ((c >= BLK) | (n > 0))
        for i in range(nq):
            q2 = q_ref[:, i * 128:(i + 1) * 128]
            o2 = jnp.zeros((BLK, LANES), F32)
            l2 = jnp.zeros((BLK, LANES), F32)
            for half in range(2):
                m = _head_mask((BLK, LANES), half)
                qm = jnp.where(m, q2, jnp.zeros_like(q2))
                s = lax.dot_general(qm, kcat, (((1,), (1,)), ((), ())), preferred_element_type=F32) * scale
                s = jnp.where(valid, s, -jnp.inf)
                mx = jnp.max(s, axis=1, keepdims=True)
                if has_sink:
                    snk = _col_of(sk_ref[:, i * 128:(i + 1) * 128], _head_mask((1, LANES), half))
                    mx = jnp.maximum(mx, snk)
                pr = jnp.exp(s - mx)
                den = jnp.sum(pr, axis=1, keepdims=True)
                if has_sink:
                    den = den + jnp.exp(snk - mx)
                pv = jnp.dot(pr.astype(BF16), vcat, preferred_element_type=F32) / den
                o2 = jnp.where(m, pv, o2)
                l2 = jnp.where(m, mx + jnp.log(den), l2)
            outs[0][:, i * 128:(i + 1) * 128] = o2
            outs[1][:, i * 128:(i + 1) * 128] = l2
            if want_bf16:
                outs[2][:, i * 128:(i + 1) * 128] = o2.astype(BF16)

    qspec = pl.BlockSpec((BLK, nq * 128), lambda j, n: (n, j))
    cur = pl.BlockSpec((BLK, 128), lambda j, n: (n, j))
    prev = pl.BlockSpec((BLK, 128), lambda j, n: (jnp.maximum(n - 1, 0), j))
    in_specs = [qspec, prev, cur, prev, cur]
    args = [q, k, k, v, v]
    if has_sink:
        in_specs.append(pl.BlockSpec((1, nq * 128), lambda j, n: (0, j)))
        args.append(sinks)
    out_dts = [F32, F32] + ([BF16] if want_bf16 else [])
    return _pc(
        body, name=name, grid=(ncol, nb), in_specs=in_specs,
        out_specs=[qspec] * len(out_dts),
        out_shape=[jax.ShapeDtypeStruct(q.shape, dt) for dt in out_dts],
        compiler_params=_params(("parallel", "parallel")),
    )(*args)


def _attn_bwd(q, k, v, o, lse, do, *, nq, max_dist, name, sinks=None):
    L, Ck = k.shape
    nb, ncol = L // BLK, Ck // LANES
    scale = HEAD ** -0.5
    has_sink = sinks is not None
    nt = (((1,), (1,)), ((), ()))
    tn = (((0,), (0,)), ((), ()))

    def body(*refs):
        (qc_ref, qn_ref, kp_ref, kc_ref, vp_ref, vc_ref, oc_ref, on_ref, lc_ref, ln_ref, dc_ref, dn_ref) = refs[:12]
        sk_ref = refs[12] if has_sink else None
        outs = refs[12 + has_sink:]
        dq_ref, dk_ref, dv_ref = outs[:3]
        n = pl.program_id(1)
        kc, vc = kc_ref[...], vc_ref[...]
        kcat = jnp.concatenate([kp_ref[...], kc], axis=0)
        vcat = jnp.concatenate([vp_ref[...], vc], axis=0)
        r = lax.broadcasted_iota(jnp.int32, (BLK, 2 * BLK), 0)
        c = lax.broadcasted_iota(jnp.int32, (BLK, 2 * BLK), 1)
        dist = r + BLK - c
        valid_q = (dist >= 0) & (dist <= max_dist) & ((c >= BLK) | (n > 0))
        r2 = lax.broadcasted_iota(jnp.int32, (2 * BLK, BLK), 0)
        c2 = lax.broadcasted_iota(jnp.int32, (2 * BLK, BLK), 1)
        dist2 = r2 - c2
        valid_k = (dist2 >= 0) & (dist2 <= max_dist) & ((r2 < BLK) | (n < nb - 1))
        dk_acc = jnp.zeros((BLK, LANES), F32)
        dv_acc = jnp.zeros((BLK, LANES), F32)
        for i in range(nq):
            sl = slice(i * 128, (i + 1) * 128)
            qcur, docur = qc_ref[:, sl], dc_ref[:, sl]
            qcat = jnp.concatenate([qcur, qn_ref[:, sl]], axis=0)
            docat = jnp.concatenate([docur, dn_ref[:, sl]], axis=0)
            ocat = jnp.concatenate([oc_ref[:, sl], on_ref[:, sl]], axis=0)
            lcat = jnp.concatenate([lc_ref[:, sl], ln_ref[:, sl]], axis=0)
            dq2 = jnp.zeros((BLK, LANES), F32)
            ds2 = jnp.zeros((1, LANES), F32)
            for half in range(2):
                m1 = _head_mask((BLK, LANES), half)
                m2 = _head_mask((2 * BLK, LANES), half)
                dom = jnp.where(m2, docat, 0.0)
                delta = jnp.sum(dom * ocat, axis=1, keepdims=True)
                lcol = _col_of(lcat, m2)
                domb = dom.astype(BF16)
                qmcat = jnp.where(m2, qcat, jnp.zeros_like(qcat))
                qm = qmcat[:BLK]
                s = lax.dot_general(qm, kcat, nt, preferred_element_type=F32) * scale
                pr = jnp.where(valid_q, jnp.exp(s - lcol[:BLK]), 0.0)
                dp = lax.dot_general(domb[:BLK], vcat, nt, preferred_element_type=F32)
                ds = (pr * (dp - delta[:BLK])).astype(BF16)
                dq2 = jnp.where(m1, jnp.dot(ds, kcat, preferred_element_type=F32) * scale, dq2)
                if has_sink:
                    snk = _col_of(sk_ref[:, sl], _head_mask((1, LANES), half))
                    contrib = jnp.sum(-jnp.exp(snk - lcol[:BLK]) * delta[:BLK], axis=0, keepdims=True)
                    ds2 = jnp.where(_head_mask((1, LANES), half), contrib, ds2)
                s = lax.dot_general(qmcat, kc, nt, preferred_element_type=F32) * scale
                pr = jnp.where(valid_k, jnp.exp(s - lcol), 0.0)
                dv_acc += lax.dot_general(pr.astype(BF16), domb, tn, preferred_element_type=F32)
                dp = lax.dot_general(domb, vc, nt, preferred_element_type=F32)
                ds = (pr * (dp - delta)).astype(BF16)
                dk_acc += lax.dot_general(ds, qmcat, tn, preferred_element_type=F32) * scale
            dq_ref[:, sl] = dq2
            if has_sink:
                @pl.when(n == 0)
                def _():
                    outs[3][:, sl] = jnp.zeros((1, LANES), F32)

                outs[3][:, sl] += ds2
        dk_ref[...] = dk_acc
        dv_ref[...] = dv_acc

    qcur = pl.BlockSpec((BLK, nq * 128), lambda j, n: (n, j))
    qnext = pl.BlockSpec((BLK, nq * 128), lambda j, n: (jnp.minimum(n + 1, nb - 1), j))
    cur = pl.BlockSpec((BLK, 128), lambda j, n: (n, j))
    prev = pl.BlockSpec((BLK, 128), lambda j, n: (jnp.maximum(n - 1, 0), j))
    in_specs = [qcur, qnext, prev, cur, prev, cur, qcur, qnext, qcur, qnext, qcur, qnext]
    args = [q, q, k, k, v, v, o, o, lse, lse, do, do]
    out_specs = [qcur, cur, cur]
    out_shape = [jax.ShapeDtypeStruct(q.shape, F32), jax.ShapeDtypeStruct(k.shape, F32), jax.ShapeDtypeStruct(k.shape, F32)]
    if has_sink:
        vec = pl.BlockSpec((1, nq * 128), lambda j, n: (0, j))
        in_specs.append(vec)
        args.append(sinks)
        out_specs.append(vec)
        out_shape.append(jax.ShapeDtypeStruct((1, q.shape[1]), F32))
    return _pc(
        body, name=name, grid=(ncol, nb), in_specs=in_specs, out_specs=out_specs, out_shape=out_shape,
        compiler_params=_params(("parallel", "arbitrary")),
    )(*args)


ATT_TILE = 2048


def _rows(ref, start, n, r):
    if r == 1:
        return ref[pl.ds(start, n), :]
    return ref[pl.ds(start, n, stride=r), :]


def _twice(x):
    return jnp.concatenate([x, x], axis=0)


def _stack_heads(x, masks):
    zero = jnp.zeros_like(x)
    return jnp.concatenate([jnp.where(masks[0], x, zero), jnp.where(masks[1], x, zero)], axis=0)


def _set_rows(ref, start, n, r, val):
    if r == 1:
        ref[pl.ds(start, n), :] = val
    else:
        ref[pl.ds(start, n, stride=r), :] = val


def _band_geometry(S, patterns):
    rmax = max(r for _, r in patterns)
    H = BLK * rmax
    T = min(S, ATT_TILE)
    assert T % H == 0 and S % T == 0
    return H, T, S // T, T // BLK


def _band_fwd(q, k, v, *, patterns, nq, name, sinks=None, want_bf16=False):
    S, Ck = k.shape
    H, T, nt, nbt = _band_geometry(S, patterns)
    ncol = Ck // LANES
    scale = HEAD ** -0.5
    has_sink = sinks is not None
    nt_dims = (((1,), (1,)), ((), ()))

    def body(*refs):
        q_ref, kp_ref, kc_ref, vp_ref, vc_ref = refs[:5]
        sk_ref = refs[5] if has_sink else None
        n_out = 3 if want_bf16 else 2
        outs = refs[5 + has_sink:5 + has_sink + n_out]
        qf, kf, vf, M, L, A = refs[5 + has_sink + n_out:]
        t = pl.program_id(1)
        kf[0:H, :] = kp_ref[...].astype(F32)
        kf[H:H + T, :] = kc_ref[...].astype(F32)
        vf[0:H, :] = vp_ref[...].astype(F32)
        vf[H:H + T, :] = vc_ref[...].astype(F32)
        r_i = lax.broadcasted_iota(jnp.int32, (BLK, 2 * BLK), 0)
        c_i = lax.broadcasted_iota(jnp.int32, (BLK, 2 * BLK), 1)
        dist_i = r_i + BLK - c_i
        masks = [_head_mask((BLK, LANES), h) for h in range(2)]

        for i in range(nq):
            qf[...] = q_ref[:, i * 128:(i + 1) * 128].astype(F32) * scale
            for p, (dist, r) in enumerate(patterns):
                in_band = (dist_i >= 0) & (dist_i <= dist)
                in_band_first = in_band & ((c_i >= BLK) | (t > 0))
                in_band, in_band_first = _twice(in_band), _twice(in_band_first)

                def unit(j, b, p=p, r=r, in_band=in_band, in_band_first=in_band_first):
                    q0 = j + b * (BLK * r)
                    q2 = _rows(qf, q0, BLK, r).astype(BF16)
                    kcat = _rows(kf, H + q0 - BLK * r, 2 * BLK, r).astype(BF16)
                    vcat = _rows(vf, H + q0 - BLK * r, 2 * BLK, r).astype(BF16)
                    valid = in_band if b > 0 else in_band_first
                    s = lax.dot_general(_stack_heads(q2, masks), kcat, nt_dims, preferred_element_type=F32)
                    s = jnp.where(valid, s, -jnp.inf)
                    mx = jnp.max(s, axis=1, keepdims=True)
                    pr = jnp.exp(s - mx)
                    den = jnp.sum(pr, axis=1, keepdims=True)
                    pv = jnp.dot(pr.astype(BF16), vcat, preferred_element_type=F32)
                    m2 = jnp.where(masks[0], mx[:BLK], mx[BLK:])
                    l2 = jnp.where(masks[0], den[:BLK], den[BLK:])
                    a2 = jnp.where(masks[0], pv[:BLK], pv[BLK:])
                    if p > 0:
                        mo = _rows(M, q0, BLK, r)
                        mn = jnp.maximum(mo, m2)
                        wa, wb = jnp.exp(mo - mn), jnp.exp(m2 - mn)
                        l2 = wa * _rows(L, q0, BLK, r) + wb * l2
                        a2 = wa * _rows(A, q0, BLK, r) + wb * a2
                        m2 = mn
                    _set_rows(M, q0, BLK, r, m2)
                    _set_rows(L, q0, BLK, r, l2)
                    _set_rows(A, q0, BLK, r, a2)

                for u in range(nbt):
                    unit(u % r, u // r)
            sl = slice(i * 128, (i + 1) * 128)
            mm, ll, aa = M[...], L[...], A[...]
            if has_sink:
                snk = sk_ref[:, sl]
                mn = jnp.maximum(mm, snk)
                w = jnp.exp(mm - mn)
                ll = ll * w + jnp.exp(snk - mn)
                aa = aa * w
                mm = mn
            o = aa / ll
            outs[0][:, sl] = o
            outs[1][:, sl] = mm + jnp.log(ll)
            if want_bf16:
                outs[2][:, sl] = o.astype(BF16)

    th = T // H
    qspec = pl.BlockSpec((T, nq * 128), lambda j, t: (t, j))
    cur = pl.BlockSpec((T, 128), lambda j, t: (t, j))
    prev = pl.BlockSpec((H, 128), lambda j, t: (jnp.maximum(t * th - 1, 0), j))
    in_specs = [qspec, prev, cur, prev, cur]
    args = [q, k, k, v, v]
    if has_sink:
        in_specs.append(pl.BlockSpec((1, nq * 128), lambda j, t: (0, j)))
        args.append(sinks)
    out_dts = [F32, F32] + ([BF16] if want_bf16 else [])
    return _pc(
        body, name=name, grid=(ncol, nt), in_specs=in_specs,
        out_specs=[qspec] * len(out_dts),
        out_shape=[jax.ShapeDtypeStruct(q.shape, dt) for dt in out_dts],
        scratch_shapes=[pltpu.VMEM((T, LANES), F32), pltpu.VMEM((H + T, LANES), F32), pltpu.VMEM((H + T, LANES), F32),
                        pltpu.VMEM((T, LANES), F32), pltpu.VMEM((T, LANES), F32), pltpu.VMEM((T, LANES), F32)],
        compiler_params=_params(("parallel", "parallel")),
    )(*args)


def _band_bwd(q, k, v, lse, delta, do, *, patterns, nq, name, sinks=None, do_col0=0):
    S, Ck = k.shape
    H, T, nt, nbt = _band_geometry(S, patterns)
    ncol = Ck // LANES
    scale = HEAD ** -0.5
    has_sink = sinks is not None
    nt_dims = (((1,), (1,)), ((), ()))
    tn_dims = (((0,), (0,)), ((), ()))

    def body(*refs):
        (qc_ref, qn_ref, kp_ref, kc_ref, vp_ref, vc_ref, lc_ref, ln_ref, ec_ref, en_ref, dc_ref, dn_ref) = refs[:12]
        sk_ref = refs[12] if has_sink else None
        n_out = 4 if has_sink else 3
        outs = refs[12 + has_sink:12 + has_sink + n_out]
        dq_ref, dk_ref, dv_ref = outs[:3]
        qf, kf, vf, lf, ef, df = refs[12 + has_sink + n_out:]
        t = pl.program_id(1)
        kf[0:H, :] = kp_ref[...].astype(F32)
        kf[H:H + T, :] = kc_ref[...].astype(F32)
        vf[0:H, :] = vp_ref[...].astype(F32)
        vf[H:H + T, :] = vc_ref[...].astype(F32)
        dk_ref[...] = jnp.zeros_like(dk_ref)
        dv_ref[...] = jnp.zeros_like(dv_ref)
        r_i = lax.broadcasted_iota(jnp.int32, (BLK, 2 * BLK), 0)
        c_i = lax.broadcasted_iota(jnp.int32, (BLK, 2 * BLK), 1)
        dist_q = r_i + BLK - c_i
        dist_h = dist_q[:, :BLK]
        m1 = [_head_mask((BLK, LANES), h) for h in range(2)]

        def stacked_inputs(q2, do2, l2, e2):
            spread = lambda v: jnp.concatenate([jnp.where(m1[0], v, _roll(v, HEAD)), jnp.where(m1[1], v, _roll(v, HEAD))], axis=0)
            return _stack_heads(q2, m1), _stack_heads(do2.astype(BF16), m1), spread(l2), spread(e2)

        for i in range(nq):
            sl = slice(i * 128, (i + 1) * 128)
            qf[0:T, :] = qc_ref[:, sl].astype(F32) * scale
            qf[T:T + H, :] = qn_ref[:, sl].astype(F32) * scale
            for buf, c_ref, n_ref in ((lf, lc_ref, ln_ref), (ef, ec_ref, en_ref), (df, dc_ref, dn_ref)):
                buf[0:T, :] = c_ref[:, sl]
                buf[T:T + H, :] = n_ref[:, sl]
            if has_sink:
                @pl.when(t == 0)
                def _():
                    outs[3][:, sl] = jnp.zeros((1, LANES), F32)

                outs[3][:, sl] += jnp.sum(-jnp.exp(sk_ref[:, sl] - lc_ref[:, sl]) * ec_ref[:, sl], axis=0, keepdims=True)
            for p, (dist, r) in enumerate(patterns):
                band_q = (dist_q >= 0) & (dist_q <= dist)
                band_first = band_q & ((c_i >= BLK) | (t > 0))
                band_h = (dist_h >= 0) & (dist_h <= dist)
                band_q, band_first, band_h = _twice(band_q), _twice(band_first), _twice(band_h)

                def add_rows(ref, start, val, r=r):
                    _set_rows(ref, start, BLK, r, _rows(ref, start, BLK, r) + val)

                def unit(j, b, p=p, r=r, band_q=band_q, band_first=band_first):
                    q0 = j + b * (BLK * r)
                    q2 = _rows(qf, q0, BLK, r).astype(BF16)
                    do2, l2, e2 = _rows(df, q0, BLK, r), _rows(lf, q0, BLK, r), _rows(ef, q0, BLK, r)
                    kcat = _rows(kf, H + q0 - BLK * r, 2 * BLK, r).astype(BF16)
                    vcat = _rows(vf, H + q0 - BLK * r, 2 * BLK, r).astype(BF16)
                    valid = band_q if b > 0 else band_first
                    qs, dos, ls, es = stacked_inputs(q2, do2, l2, e2)
                    s = lax.dot_general(qs, kcat, nt_dims, preferred_element_type=F32)
                    pr = jnp.where(valid, jnp.exp(s - jnp.concatenate([ls, ls], axis=1)), 0.0)
                    dp = lax.dot_general(dos, vcat, nt_dims, preferred_element_type=F32)
                    ds = (pr * (dp - jnp.concatenate([es, es], axis=1))).astype(BF16)
                    dqs = jnp.dot(ds, kcat, preferred_element_type=F32) * scale
                    dq2 = jnp.where(m1[0], dqs[:BLK], dqs[BLK:])
                    dvc = lax.dot_general(pr.astype(BF16), dos, tn_dims, preferred_element_type=F32)
                    dkc = lax.dot_general(ds, qs, tn_dims, preferred_element_type=F32)
                    if p > 0:
                        dq2 = dq2 + _rows(dq_ref.at[:, sl], q0, BLK, r)
                    _set_rows(dq_ref.at[:, sl], q0, BLK, r, dq2)
                    add_rows(dk_ref, q0, dkc[BLK:])
                    add_rows(dv_ref, q0, dvc[BLK:])
                    if b > 0:
                        add_rows(dk_ref, q0 - BLK * r, dkc[:BLK])
                        add_rows(dv_ref, q0 - BLK * r, dvc[:BLK])

                def halo_unit(j, r=r, band_h=band_h):
                    k0 = j + (nbt // r - 1) * (BLK * r)
                    q2 = _rows(qf, T + j, BLK, r).astype(BF16)
                    do2, l2, e2 = _rows(df, T + j, BLK, r), _rows(lf, T + j, BLK, r), _rows(ef, T + j, BLK, r)
                    kc = _rows(kf, H + k0, BLK, r).astype(BF16)
                    vc = _rows(vf, H + k0, BLK, r).astype(BF16)
                    qs, dos, ls, es = stacked_inputs(q2, do2, l2, e2)
                    s = lax.dot_general(qs, kc, nt_dims, preferred_element_type=F32)
                    pr = jnp.where(band_h, jnp.exp(s - ls), 0.0)
                    dp = lax.dot_general(dos, vc, nt_dims, preferred_element_type=F32)
                    ds = (pr * (dp - es)).astype(BF16)
                    add_rows(dk_ref, k0, lax.dot_general(ds, qs, tn_dims, preferred_element_type=F32))
                    add_rows(dv_ref, k0, lax.dot_general(pr.astype(BF16), dos, tn_dims, preferred_element_type=F32))

                for u in range(nbt):
                    unit(u % r, u // r)
                if nt > 1:
                    @pl.when(t < nt - 1)
                    def _(r=r, halo_unit=halo_unit):
                        for j in range(r):
                            halo_unit(j)

    th = T // H
    last = S // H - 1
    qcur = pl.BlockSpec((T, nq * 128), lambda j, t: (t, j))
    qnext = pl.BlockSpec((H, nq * 128), lambda j, t: (jnp.minimum((t + 1) * th, last), j))
    cur = pl.BlockSpec((T, 128), lambda j, t: (t, j))
    prev = pl.BlockSpec((H, 128), lambda j, t: (jnp.maximum(t * th - 1, 0), j))
    dcur = pl.BlockSpec((T, nq * 128), lambda j, t: (t, j + do_col0))
    dnext = pl.BlockSpec((H, nq * 128), lambda j, t: (jnp.minimum((t + 1) * th, last), j + do_col0))
    in_specs = [qcur, qnext, prev, cur, prev, cur, qcur, qnext, qcur, qnext, dcur, dnext]
    args = [q, q, k, k, v, v, lse, lse, delta, delta, do, do]
    out_specs = [qcur, cur, cur]
    out_shape = [jax.ShapeDtypeStruct(q.shape, F32), jax.ShapeDtypeStruct(k.shape, F32), jax.ShapeDtypeStruct(k.shape, F32)]
    if has_sink:
        vec = pl.BlockSpec((1, nq * 128), lambda j, t: (0, j))
        in_specs.append(vec)
        args.append(sinks)
        out_specs.append(vec)
        out_shape.append(jax.ShapeDtypeStruct((1, q.shape[1]), F32))
    big = pltpu.VMEM((T + H, LANES), F32)
    return _pc(
        body, name=name, grid=(ncol, nt), in_specs=in_specs, out_specs=out_specs, out_shape=out_shape,
        scratch_shapes=[big] * 6,
        compiler_params=_params(("parallel", "arbitrary")),
    )(*args)


def _delta(do, o, name):
    S, C = do.shape
    tm = 512

    def body(do_ref, o_ref, g_ref, e_ref):
        for c in range(C // LANES):
            sl = slice(c * 128, (c + 1) * 128)
            e_ref[:, sl] = _gmean(do_ref[:, sl] * o_ref[:, sl], g_ref[...]) * float(HEAD)

    row = pl.BlockSpec((tm, C), lambda i: (i, 0))
    return _pc(
        body, name=name, grid=(S // tm,),
        in_specs=[row, row, pl.BlockSpec((LANES, LANES), lambda i: (0, 0))], out_specs=row,
        out_shape=jax.ShapeDtypeStruct((S, C), F32),
        compiler_params=_params(("parallel",)),
    )(do, o, _group_matrix())


def _even_post_fwd(ro, proj, gn, da):
    S = ro.shape[0]
    tm = 256

    def body(ro_ref, rg_ref, gn_ref, da_ref, mix_ref):
        for c in range(4):
            sl = slice(c * 128, (c + 1) * 128)
            x = ro_ref[:, sl]
            mu = jnp.mean(x, axis=1, keepdims=True)
            xc = x - mu
            var = jnp.mean(xc * xc, axis=1, keepdims=True)
            y = xc * lax.rsqrt(var + EPS) * gn_ref[:, sl]
            z = rg_ref[:, sl]
            mix_ref[:, sl] = (z * jax.nn.sigmoid(z) * y).astype(BF16)
        mix_ref[:, 512:1024] = da_ref[...].astype(BF16)

    row = lambda w: pl.BlockSpec((tm, w), lambda i: (i, 0))
    return _pc(
        body, name="even_post_fwd", grid=(S // tm,),
        in_specs=[row(512), pl.BlockSpec((tm, 512), lambda i: (i, 2)), pl.BlockSpec((1, 512), lambda i: (0, 0)), row(512)],
        out_specs=row(1024), out_shape=jax.ShapeDtypeStruct((S, 1024), BF16),
        compiler_params=_params(("parallel",)),
    )(ro, proj, gn, da)


def _even_post_bwd(ro, proj, gn, dmixed):
    S = ro.shape[0]
    tm = 256

    def body(ro_ref, rg_ref, gn_ref, dm_ref, dro_ref, drg_ref, dgn_ref):
        @pl.when(pl.program_id(0) == 0)
        def _():
            dgn_ref[...] = jnp.zeros_like(dgn_ref)

        for c in range(4):
            sl = slice(c * 128, (c + 1) * 128)
            x = ro_ref[:, sl]
            mu = jnp.mean(x, axis=1, keepdims=True)
            xc = x - mu
            rstd = lax.rsqrt(jnp.mean(xc * xc, axis=1, keepdims=True) + EPS)
            xh = xc * rstd
            gain = gn_ref[:, sl]
            y = xh * gain
            z = rg_ref[:, sl]
            sg = jax.nn.sigmoid(z)
            dra = dm_ref[:, sl]
            drg_ref[:, sl] = dra * y * sg * (1.0 + z * (1.0 - sg))
            dy = dra * z * sg
            dgn_ref[:, sl] += jnp.sum(dy * xh, axis=0, keepdims=True)
            dxh = dy * gain
            dro_ref[:, sl] = rstd * (dxh - jnp.mean(dxh, axis=1, keepdims=True)
                                     - xh * jnp.mean(dxh * xh, axis=1, keepdims=True))

    row = lambda w: pl.BlockSpec((tm, w), lambda i: (i, 0))
    vec = pl.BlockSpec((1, 512), lambda i: (0, 0))
    return _pc(
        body, name="even_post_bwd", grid=(S // tm,),
        in_specs=[row(512), pl.BlockSpec((tm, 512), lambda i: (i, 2)), vec, row(512)],
        out_specs=[row(512), row(512), vec],
        out_shape=[jax.ShapeDtypeStruct((S, 512), F32), jax.ShapeDtypeStruct((S, 512), F32),
                   jax.ShapeDtypeStruct((1, 512), F32)],
        compiler_params=_params(("arbitrary",)),
    )(ro, proj, gn, dmixed)


def _swa_pre_fwd(proj, tab, qg, kg):
    S = proj.shape[0]
    tm = 256

    def body(p_ref, tab_ref, qg_ref, kg_ref, g_ref, q_ref, k_ref, v_ref):
        Ap, Bp, Cp = _tab(tab_ref, 1)
        G = g_ref[...]
        lo = _head_mask((tm, LANES), 0)
        for c in range(8):
            sl = slice(c * 128, (c + 1) * 128)
            q_ref[:, sl] = _rope(_hn_fwd(p_ref[:, sl], qg_ref[...], G), Ap, Bp, Cp, 8).astype(BF16)
        for c in range(2):
            kn = _rope(_hn_fwd(p_ref[:, 1024 + c * 128:1024 + (c + 1) * 128], kg_ref[...], G), Ap, Bp, Cp, 8)
            vv = p_ref[:, 1280 + c * 128:1280 + (c + 1) * 128]
            for t, ref in ((kn, k_ref), (vv, v_ref)):
                sw = _roll(t, HEAD)
                ref[:, (2 * c) * 128:(2 * c + 1) * 128] = jnp.where(lo, t, sw).astype(BF16)
                ref[:, (2 * c + 1) * 128:(2 * c + 2) * 128] = jnp.where(lo, sw, t).astype(BF16)

    row = lambda w: pl.BlockSpec((tm, w), lambda i: (i, 0))
    vec = pl.BlockSpec((1, LANES), lambda i: (0, 0))
    return _pc(
        body, name="swa_pre_fwd", grid=(S // tm,),
        in_specs=[row(1536), row(768), vec, vec, pl.BlockSpec((LANES, LANES), lambda i: (0, 0))],
        out_specs=[row(1024), row(512), row(512)],
        out_shape=[jax.ShapeDtypeStruct((S, w), BF16) for w in (1024, 512, 512)],
        compiler_params=_params(("parallel",)),
    )(proj, tab, qg, kg, _group_matrix())


def _swa_pre_bwd(proj, tab, qg, kg, dq, dk, dv):
    S = proj.shape[0]
    tm = 256

    def body(p_ref, tab_ref, qg_ref, kg_ref, g_ref, dq_ref, dk_ref, dv_ref, dp_ref, db_ref, dqg_ref, dkg_ref):
        Ap, Bp, Cp = _tab(tab_ref, 1)
        G = g_ref[...]
        lo = _head_mask((tm, LANES), 0)

        @pl.when(pl.program_id(0) == 0)
        def _():
            db_ref[...] = jnp.zeros_like(db_ref)
            dqg_ref[...] = jnp.zeros_like(dqg_ref)
            dkg_ref[...] = jnp.zeros_like(dkg_ref)

        accq = jnp.zeros((1, LANES), F32)
        acck = jnp.zeros((1, LANES), F32)
        for c in range(8):
            sl = slice(c * 128, (c + 1) * 128)
            dx, dg = _hn_bwd(p_ref[:, sl], qg_ref[...], _rope_t(dq_ref[:, sl], Ap, Bp, Cp, 8), G)
            dp_ref[:, sl] = dx.astype(BF16)
            db_ref[:, sl] += jnp.sum(dx, axis=0, keepdims=True)
            accq = accq + dg
        for c in range(2):
            folded = []
            for ref in (dk_ref, dv_ref):
                a = ref[:, (2 * c) * 128:(2 * c + 1) * 128]
                b = ref[:, (2 * c + 1) * 128:(2 * c + 2) * 128]
                folded.append(jnp.where(lo, a + _roll(a, HEAD), b + _roll(b, HEAD)))
            ks = slice(1024 + c * 128, 1024 + (c + 1) * 128)
            dx, dg = _hn_bwd(p_ref[:, ks], kg_ref[...], _rope_t(folded[0], Ap, Bp, Cp, 8), G)
            dp_ref[:, ks] = dx.astype(BF16)
            db_ref[:, ks] += jnp.sum(dx, axis=0, keepdims=True)
            acck = acck + dg
            vs = slice(1280 + c * 128, 1280 + (c + 1) * 128)
            dp_ref[:, vs] = folded[1].astype(BF16)
            db_ref[:, vs] += jnp.sum(folded[1], axis=0, keepdims=True)
        dqg_ref[...] += _fold_halves(accq)
        dkg_ref[...] += _fold_halves(acck)

    row = lambda w: pl.BlockSpec((tm, w), lambda i: (i, 0))
    vec = pl.BlockSpec((1, LANES), lambda i: (0, 0))
    return _pc(
        body, name="swa_pre_bwd", grid=(S // tm,),
        in_specs=[row(1536), row(768), vec, vec, pl.BlockSpec((LANES, LANES), lambda i: (0, 0)),
                  row(1024), row(512), row(512)],
        out_specs=[row(1536), pl.BlockSpec((1, 1536), lambda i: (0, 0)), vec, vec],
        out_shape=[jax.ShapeDtypeStruct((S, 1536), BF16), jax.ShapeDtypeStruct((1, 1536), F32),
                   jax.ShapeDtypeStruct((1, LANES), F32), jax.ShapeDtypeStruct((1, LANES), F32)],
        compiler_params=_params(("arbitrary",)),
    )(proj, tab, qg, kg, _group_matrix(), dq, dk, dv)


def _loss_head(y, target):
    S, Dm = y.shape
    tm = 512

    def body(y_ref, t_ref, l_ref, dy_ref, dyb_ref):
        @pl.when(pl.program_id(0) == 0)
        def _():
            l_ref[...] = jnp.zeros_like(l_ref)

        e = y_ref[...] - t_ref[...]
        dy = e * (1.0 / Dm)
        dy_ref[...] = dy
        dyb_ref[...] = dy.astype(BF16)
        row = jnp.sum(e * e, axis=1, keepdims=True) * (0.5 / Dm)
        l_ref[...] += jnp.sum(row, axis=0, keepdims=True)

    row = pl.BlockSpec((tm, Dm), lambda i: (i, 0))
    return _pc(
        body, name="loss_head", grid=(S // tm,), in_specs=[row, row],
        out_specs=[pl.BlockSpec((1, LANES), lambda i: (0, 0)), row, row],
        out_shape=[jax.ShapeDtypeStruct((1, LANES), F32), jax.ShapeDtypeStruct((S, Dm), F32),
                   jax.ShapeDtypeStruct((S, Dm), BF16)],
        compiler_params=_params(("arbitrary",)),
    )(y, target)


def _relu2_of(u):
    r = jnp.maximum(u.astype(F32), 0.0)
    return r * r


def _drelu2(acc, u):
    return (acc * 2.0 * jnp.maximum(u.astype(F32), 0.0),)


def _add(acc, res):
    return (acc + res,)


def _add_norm_in(res, g):
    def epilogue(acc, r, gv):
        xn = acc + r
        return xn, xn * lax.rsqrt(jnp.mean(xn * xn, axis=-1, keepdims=True) + EPS) * gv

    return dict(outs=[F32, BF16], epilogue=epilogue, extras=[(res, "mn"), (g.reshape(1, D_MODEL), "n")])


_T = dict(tm=1024, tn=1024, tk=1024)


def _rms_bwd_in(x, g, dres):
    def epilogue(dh, xv, gv, dr):
        r = lax.rsqrt(jnp.mean(xv * xv, axis=-1, keepdims=True) + EPS)
        t = dh * gv
        dx = dr + r * t - xv * (r * r * r) * jnp.mean(xv * t, axis=-1, keepdims=True)
        return dx, dx, jnp.sum(dh * xv * r, axis=0, keepdims=True)

    return dict(outs=[F32, BF16, ("colsum",)], epilogue=epilogue,
                extras=[(x, "mn"), (g.reshape(1, D_MODEL), "n"), (dres, "mn")])


def _delta_in(o, col0):
    width = D_MODEL - col0

    def epilogue(do, ov, G):
        parts = [_gmean(do[:, col0 + c * 128:col0 + (c + 1) * 128] * ov[:, c * 128:(c + 1) * 128], G) * float(HEAD)
                 for c in range(width // LANES)]
        return do, jnp.concatenate(parts, axis=1)

    return dict(outs=[F32, (F32, width)], epilogue=epilogue, extras=[(o, width), (_group_matrix(), "full")])


def _loss_in(res, target):
    def epilogue(acc, r, t):
        e = acc + r - t
        dy = e * (1.0 / D_MODEL)
        return dy, dy, jnp.sum(e * e, axis=0, keepdims=True) * (0.5 / D_MODEL)

    return dict(outs=[F32, BF16, ("colsum",)], epilogue=epilogue, extras=[(res, "mn"), (target, "mn")])


def _mlp_fwd(h, wts, layer, tag, tail):
    u = _matmul(h, wts, dims="nn", **_T, outs=[BF16], b_cs=True, b_row0=layer, name=f"mlp_up{tag}")
    out = _matmul(u, wts, dims="nn", **_T, a_pro=_relu2_of, b_rs=1024, b_row0=2 + layer, name=f"mlp_down{tag}", **tail)
    return out, (h, u)


def _mlp_bwd(x, g, wts, layer, saved, dy, dyb, tag):
    h, u = saved
    du = _matmul(dyb, wts, dims="nt", **_T, outs=[BF16], epilogue=_drelu2, extras=[(u, "mn")], b_rs=1024,
                 b_row0=2 + layer, name=f"mlp_du{tag}")
    dw_dn = _matmul(u, dyb, dims="tn", **_T, outs=[F32], a_pro=_relu2_of, name=f"mlp_dwdown{tag}")
    dw_up = _matmul(h, du, dims="tn", **_T, outs=[F32], o_cs=N_CHIPS, name=f"mlp_dwup{tag}")
    dx, dxb, dg = _matmul(du, wts, dims="nt", tm=512, tn=1024, tk=1024, b_cs=True, b_row0=layer, b_rows=1024, name=f"mlp_dh{tag}",
                          **_rms_bwd_in(x, g, dy))
    return dx, dxb, dg, dw_up, dw_dn


def _pattern_view(t, r):
    S, C = t.shape
    return t.reshape(S // r, r * C)


def _local_step(x, pos_col, target, first_of, rest_begin, rest_of, P, red):
    S = x.shape[0]
    tab = _tables(pos_col)
    tile2 = lambda g: jnp.tile(g.reshape(1, HEAD), (1, 2))
    dqg, dkg = tile2(P["dil_q_gain"]), tile2(P["dil_k_gain"])
    sqg, skg = tile2(P["swa_q_gain"]), tile2(P["swa_k_gain"])
    gn = P["ret_gn_gain"].reshape(1, 512)
    sink_b = jnp.repeat(P["swa_sinks"].reshape(16), HEAD).reshape(1, 1024)

    h0 = _rms_fwd(x, P["norm_mix"][0], "rms_mix_fwd0")
    W = first_of(h0)
    proj = _matmul(h0, W["hyb_w_in"], dims="nn", tm=1024, tn=768, tk=1024, outs=[F32], b_cs=True, name="hyb_in")
    rq, rk, rv, dq, dk, dv = _even_pre_fwd(proj, tab, dqg, dkg)
    ro, states = _ret_fwd(rq, rk, rv)
    dil = [(w // r, r) for w, r in DIL_PATTERNS]
    da, dlse = _band_fwd(dq, dk, dv, patterns=dil, nq=1, name="dil_fwd")
    mixed = rest_begin(_even_post_fwd(ro, proj, gn, da))
    x1, h1 = _matmul(mixed, W["hyb_w_out"], dims="nn", **_T, name="hyb_out", **_add_norm_in(x, P["norm_mlp"][0]))
    rest, bias = rest_of(x1)
    W = {**W, **rest}
    (x2, h2), mlp0 = _mlp_fwd(h1, W["packed"], 0, "0", _add_norm_in(x1, P["norm_mix"][1]))

    proj2 = _matmul(h2, W["swa_w_qkv"], dims="nn", tm=1024, tn=384, tk=1024, outs=[F32], b_cs=True,
                    epilogue=_add, extras=[(bias.reshape(1, 1536), "n")], name="swa_qkv")
    sq, sk, sv = _swa_pre_fwd(proj2, tab, sqg, skg)
    swa = [(SWA_DIST, 1)]
    so, slse, so_b = _band_fwd(sq, sk, sv, patterns=swa, nq=2, name="swa_fwd", sinks=sink_b, want_bf16=True)
    x3, h3 = _matmul(so_b, W["swa_w_out"], dims="nn", **_T, name="swa_out", **_add_norm_in(x2, P["norm_mlp"][1]))
    (dy, dyb, loss_cols), mlp1 = _mlp_fwd(h3, W["packed"], 1, "1", _loss_in(x3, target))
    loss = jnp.broadcast_to(jnp.sum(loss_cols), (1, LANES))

    gw, gp = {}, {}
    dx3, dx3b, dg_mlp1, gw["mlp_w_up1"], gw["mlp_w_down1"] = _mlp_bwd(x3, P["norm_mlp"][1], W["packed"], 1, mlp1, dy, dyb, "1")
    dx3b = red.begin("mlp1", {n: (gw[n], 1024) for n in ("mlp_w_up1", "mlp_w_down1")}, dx3b)
    gw["swa_w_out"] = _matmul(so_b, dx3b, dims="tn", **_T, outs=[F32], name="swa_dwout")
    dso, sdelta = _matmul(dx3b, W["swa_w_out"], dims="nt", tm=512, tn=1024, tk=1024, name="swa_do", **_delta_in(so, 0))
    dsq, dsk, dsv, dsink = _band_bwd(sq, sk, sv, slse, sdelta, dso, patterns=swa, nq=2, name="swa_bwd", sinks=sink_b)
    dproj2, gp["swa_b_qkv"], gp["swa_q_gain"], gp["swa_k_gain"] = _swa_pre_bwd(proj2, tab, sqg, skg, dsq, dsk, dsv)
    gp["swa_sinks"] = dsink
    gw["swa_w_qkv"] = _matmul(h2, dproj2, dims="tn", tm=1024, tn=384, tk=1024, outs=[F32], o_cs=N_CHIPS, name="swa_dwqkv")
    dx2, dx2b, dg_mix1 = _matmul(dproj2, W["swa_w_qkv"], dims="nt", tm=512, tn=1024, tk=384, b_cs=True, name="swa_dh",
                                 **_rms_bwd_in(x2, P["norm_mix"][1], dx3))
    dx2b = red.begin("swa", {"swa_w_qkv": (gw["swa_w_qkv"], 1024), "swa_w_out": (gw["swa_w_out"], 256)}, dx2b)
    dx2b = red.advance("mlp1", dx2b, dx2b)

    dx1, dx1b, dg_mlp0, gw["mlp_w_up0"], gw["mlp_w_down0"] = _mlp_bwd(x1, P["norm_mlp"][0], W["packed"], 0, mlp0, dx2, dx2b, "0")
    gw["hyb_w_out"] = _matmul(mixed, dx1b, dims="tn", **_T, outs=[F32], name="hyb_dwout")
    dx1b = red.begin("mlp0", {"mlp_w_up0": (gw["mlp_w_up0"], 1024), "mlp_w_down0": (gw["mlp_w_down0"], 1024),
                              "hyb_w_out": (gw["hyb_w_out"], 256)}, dx1b)
    dx1b = red.advance("swa", dx1b, dx1b)
    red.finish("mlp1", dx1b)
    dmixed, ddelta = _matmul(dx1b, W["hyb_w_out"], dims="nt", tm=512, tn=1024, tk=1024, name="hyb_dmixed", **_delta_in(da, 512))
    dro, drg, gp["ret_gn_gain"] = _even_post_bwd(ro, proj, gn, dmixed)
    drq, drk, drv = _ret_bwd(rq, rk, rv, states, dro)
    ddq, ddk, ddv = _band_bwd(dq, dk, dv, dlse, ddelta, dmixed, patterns=dil, nq=1, name="dil_bwd", do_col0=4)
    ddq = red.advance("mlp0", ddq, ddq)
    red.finish("swa", ddq)
    dproj, gp["dil_q_gain"], gp["dil_k_gain"] = _even_pre_bwd(proj, tab, dqg, dkg, drq, drk, drv, drg, [ddq], [ddk], [ddv])
    gw["hyb_w_in"] = _matmul(h0, dproj, dims="tn", tm=1024, tn=768, tk=1024, outs=[F32], o_cs=N_CHIPS, name="hyb_dwin")
    dproj = red.begin("win", {"hyb_w_in": (gw["hyb_w_in"], 1024)}, dproj)
    grad_x, _, dg_mix0 = _matmul(dproj, W["hyb_w_in"], dims="nt", tm=512, tn=1024, tk=768, b_cs=True, name="hyb_dh",
                                 **_rms_bwd_in(x, P["norm_mix"][0], dx1))
    red.finish("mlp0", grad_x)
    gp["norm_mix"] = jnp.concatenate([dg_mix0, dg_mix1], axis=0)
    gp["norm_mlp"] = jnp.concatenate([dg_mlp0, dg_mlp1], axis=0)
    return loss, grad_x, gp


HBM = pl.BlockSpec(memory_space=pltpu.HBM)


def _place():
    x, y, c = lax.axis_index("x"), lax.axis_index("y"), lax.axis_index("c")
    chips = [(1 - x, y), (x, 1 - y), (1 - x, 1 - y)]
    return x, y, c, chips


def _allgather_shards(buf):
    _, R, Wd = buf.shape
    Rh = R // 2

    def body(b_ref, out_ref, send_sems, recv_sems):
        x, y, c, chips = _place()
        sibling = (x, y, 1 - c)

        def copy(k, chip, core, to):
            block = b_ref.at[2 * chip[0] + chip[1], pl.ds(core * Rh, Rh), :]
            return pltpu.make_async_remote_copy(
                src_ref=block, dst_ref=block, send_sem=send_sems.at[k], recv_sem=recv_sems.at[k],
                device_id=to, device_id_type=MESH)

        first = [copy(k, (x, y), c, (*chip, c)) for k, chip in enumerate(chips)]
        for cp in first:
            cp.start()
        passed = [copy(3 + k, chip, c, sibling) for k, chip in enumerate(chips)]
        for k, chip in enumerate(chips):
            copy(k, chip, c, (x, y, c)).wait_recv()
            passed[k].start()
        for k, chip in enumerate(chips):
            copy(3 + k, chip, 1 - c, (x, y, c)).wait_recv()
        for cp in first + passed:
            cp.wait_send()

    return _pc(
        body, name="allgather_first", in_specs=[HBM], out_specs=HBM,
        out_shape=jax.ShapeDtypeStruct(buf.shape, buf.dtype), input_output_aliases={0: 0},
        scratch_shapes=[pltpu.SemaphoreType.DMA((6,)), pltpu.SemaphoreType.DMA((6,))],
    )(buf)


SEM = pl.BlockSpec(memory_space=pltpu.SEMAPHORE)
EFFECT = pltpu.SideEffectType.DATAFLOW_SIDE_EFFECTING


def _half_block(ref, chip, core):
    rh = ref.shape[1] // 2
    return ref.at[2 * chip[0] + chip[1], pl.ds(core * rh, rh), :]


def _gather_start(buf, ride, name):
    def body(b_ref, ride_ref, s0, s1, s2, r0, r1, r2, b_out, ride_out):
        x, y, c, chips = _place()
        for chip, s, r in zip(chips, (s0, s1, s2), (r0, r1, r2)):
            mine = _half_block(b_ref, (x, y), c)
            pltpu.make_async_remote_copy(src_ref=mine, dst_ref=mine, send_sem=s, recv_sem=r,
                                         device_id=(*chip, c), device_id_type=MESH).start()

    sem = pltpu.SemaphoreType.DMA(())
    return _pc(
        body, name=name,
        out_shape=(sem,) * 6 + (pltpu.HBM(buf.shape, buf.dtype), pltpu.HBM(ride.shape, ride.dtype)),
        in_specs=(HBM, HBM), out_specs=(SEM,) * 6 + (HBM, HBM), input_output_aliases={0: 6, 1: 7},
        compiler_params=pltpu.CompilerParams(has_side_effects=EFFECT),
    )(pltpu.with_memory_space_constraint(buf, pltpu.HBM), pltpu.with_memory_space_constraint(ride, pltpu.HBM))


def _gather_wait(buf, sems, after, name):
    def body(b_ref, s0, s1, s2, r0, r1, r2, after_ref, b_out):
        x, y, c, chips = _place()
        for chip, s, r in zip(chips, (s0, s1, s2), (r0, r1, r2)):
            cp = pltpu.make_async_remote_copy(src_ref=_half_block(b_ref, (x, y), c), dst_ref=_half_block(b_ref, chip, c),
                                              send_sem=s, recv_sem=r, device_id=(*chip, c), device_id_type=MESH)
            cp.wait_send()
            cp.wait_recv()

    return _pc(
        body, name=name, out_shape=pltpu.HBM(buf.shape, buf.dtype),
        in_specs=(HBM,) + (SEM,) * 6 + (pl.BlockSpec(memory_space=pl.ANY),), out_specs=HBM, input_output_aliases={0: 0},
        compiler_params=pltpu.CompilerParams(has_side_effects=EFFECT),
    )(buf, *sems, after)


def _gather_handover(buf, name):
    def body(b_ref, out_ref, send_sems, recv_sems):
        x, y, c, chips = _place()
        cps = []
        for k, chip in enumerate(chips):
            mine = _half_block(b_ref, chip, c)
            cps.append(pltpu.make_async_remote_copy(src_ref=mine, dst_ref=mine, send_sem=send_sems.at[k],
                                                    recv_sem=recv_sems.at[k], device_id=(x, y, 1 - c), device_id_type=MESH))
        for cp in cps:
            cp.start()
        for k, chip in enumerate(chips):
            theirs = _half_block(b_ref, chip, 1 - c)
            pltpu.make_async_remote_copy(src_ref=theirs, dst_ref=theirs, send_sem=send_sems.at[k], recv_sem=recv_sems.at[k],
                                         device_id=(x, y, 1 - c), device_id_type=MESH).wait_recv()
        for cp in cps:
            cp.wait_send()

    return _pc(
        body, name=name, in_specs=[HBM], out_specs=HBM,
        out_shape=jax.ShapeDtypeStruct(buf.shape, buf.dtype), input_output_aliases={0: 0},
        scratch_shapes=[pltpu.SemaphoreType.DMA((3,)), pltpu.SemaphoreType.DMA((3,))],
    )(buf)


def _swap_halves(ts):
    nt = len(ts)

    def body(*refs):
        t_refs, l_refs, send_sems, recv_sems = refs[:nt], refs[nt:2 * nt], refs[-2], refs[-1]
        x, y, c, _ = _place()
        cps = []
        for k in range(nt):
            rh = t_refs[k].shape[1] // 2
            cps.append(pltpu.make_async_remote_copy(
                src_ref=t_refs[k].at[:, pl.ds((1 - c) * rh, rh), :], dst_ref=l_refs[k],
                send_sem=send_sems.at[k], recv_sem=recv_sems.at[k], device_id=(x, y, 1 - c), device_id_type=MESH))
        for cp in cps:
            cp.start()
        for cp in cps:
            cp.wait()

    return _pc(
        body, name="grad_swap_halves", in_specs=[HBM] * nt, out_specs=[HBM] * nt,
        out_shape=[jax.ShapeDtypeStruct((t.shape[0], t.shape[1] // 2, t.shape[2]), F32) for t in ts],
        scratch_shapes=[pltpu.SemaphoreType.DMA((nt,)), pltpu.SemaphoreType.DMA((nt,))],
    )(*ts)


def _pair_sum(t, l, place, name):
    _, r, cols = t.shape
    rh = r // 2
    tr = min(rh, 256)
    nr = rh // tr

    def body(pl_ref, t_ref, l_ref, o_ref):
        o_ref[...] = (t_ref[...] + l_ref[...]).astype(BF16)

    other = lambda s, p: s + jnp.where(s >= p[0], 1, 0)
    return _pc(
        body, name=name,
        grid_spec=pltpu.PrefetchScalarGridSpec(
            num_scalar_prefetch=1, grid=(N_CHIPS - 1, nr),
            in_specs=[pl.BlockSpec((None, tr, cols), lambda s, i, p: (other(s, p), p[1] * nr + i, 0)),
                      pl.BlockSpec((None, tr, cols), lambda s, i, p: (other(s, p), i, 0))],
            out_specs=pl.BlockSpec((None, tr, cols), lambda s, i, p: (other(s, p), i, 0))),
        out_shape=jax.ShapeDtypeStruct((N_CHIPS, rh, cols), BF16),
        compiler_params=_params(("parallel", "parallel")),
    )(place, t, l)


def _exchange_chips(ps):
    nt = len(ps)

    def body(*refs):
        p_refs, r_refs, send_sems, recv_sems = refs[:nt], refs[nt:2 * nt], refs[-2], refs[-1]
        x, y, c, chips = _place()
        cps = []
        for t in range(nt):
            for k, chip in enumerate(chips):
                cps.append(pltpu.make_async_remote_copy(
                    src_ref=p_refs[t].at[2 * chip[0] + chip[1]], dst_ref=r_refs[t].at[k],
                    send_sem=send_sems.at[3 * t + k], recv_sem=recv_sems.at[3 * t + k],
                    device_id=(*chip, c), device_id_type=MESH))
        for cp in cps:
            cp.start()
        for cp in cps:
            cp.wait()

    return _pc(
        body, name="grad_exchange_chips", in_specs=[HBM] * nt, out_specs=[HBM] * nt,
        out_shape=[jax.ShapeDtypeStruct((3,) + p.shape[1:], BF16) for p in ps],
        scratch_shapes=[pltpu.SemaphoreType.DMA((3 * nt,)), pltpu.SemaphoreType.DMA((3 * nt,))],
    )(*ps)


def _final_sum(t, l, rcv, place, name, layer=0, layers=1, into=None):
    _, r, cols = t.shape
    rh = r // 2
    tr = min(rh, 256)
    nr = rh // tr

    def body(pl_ref, t_ref, l_ref, r_ref, *rest):
        acc = t_ref[...] + l_ref[...]
        for k in range(3):
            acc = acc + r_ref[k].astype(F32)
        rest[-1][...] = acc

    in_specs = [pl.BlockSpec((None, tr, cols), lambda i, p: (p[0], p[1] * nr + i, 0)),
                pl.BlockSpec((None, tr, cols), lambda i, p: (p[0], i, 0)),
                pl.BlockSpec((3, tr, cols), lambda i, p: (0, i, 0))]
    args = [place, t, l, rcv]
    aliases = {}
    if into is not None:
        in_specs.append(pl.BlockSpec(memory_space=pl.ANY))
        args.append(into)
        aliases = {4: 0}
    return _pc(
        body, name=name,
        grid_spec=pltpu.PrefetchScalarGridSpec(
            num_scalar_prefetch=1, grid=(nr,), in_specs=in_specs,
            out_specs=pl.BlockSpec((tr, cols), lambda i, p: (2 * nr * layer + p[1] * nr + i, 0))),
        out_shape=jax.ShapeDtypeStruct((layers * r, cols), F32), input_output_aliases=aliases,
        compiler_params=_params(("parallel",)),
    )(*args)


def _share_halves(hs, name):
    nt = len(hs)
    n = sum(layers for _, layers in hs)

    def body(*refs):
        h_refs, send_sems, recv_sems = refs[:nt], refs[-2], refs[-1]
        x, y, c, _ = _place()
        cps = []
        for k, (_, layers) in enumerate(hs):
            rh = h_refs[k].shape[0] // (2 * layers)
            for layer in range(layers):
                half = h_refs[k].at[pl.ds((2 * layer + c) * rh, rh), :]
                cps.append(pltpu.make_async_remote_copy(
                    src_ref=half, dst_ref=half, send_sem=send_sems.at[len(cps)], recv_sem=recv_sems.at[len(cps)],
                    device_id=(x, y, 1 - c), device_id_type=MESH))
        for cp in cps:
            cp.start()
        for cp in cps:
            cp.wait()

    return _pc(
        body, name=name, in_specs=[HBM] * nt, out_specs=[HBM] * nt,
        out_shape=[jax.ShapeDtypeStruct(h.shape, F32) for h, _ in hs],
        input_output_aliases={k: k for k in range(nt)},
        scratch_shapes=[pltpu.SemaphoreType.DMA((n,)), pltpu.SemaphoreType.DMA((n,))],
    )(*[h for h, _ in hs])


def _split_start(name, bufs, ride, n, copies_of):
    nb = len(bufs)

    def body(*refs):
        sems = refs[nb + 1:nb + 1 + 2 * n]
        for cp in copies_of(refs[:nb], sems[:n], sems[n:]):
            (cp[0] if isinstance(cp, tuple) else cp).start()

    outs = _pc(
        body, name=name,
        out_shape=(pltpu.SemaphoreType.DMA(()),) * (2 * n) + tuple(pltpu.HBM(b.shape, b.dtype) for b in bufs)
        + (pltpu.HBM(ride.shape, ride.dtype),),
        in_specs=(HBM,) * (nb + 1), out_specs=(SEM,) * (2 * n) + (HBM,) * (nb + 1),
        input_output_aliases={k: 2 * n + k for k in range(nb + 1)},
        compiler_params=pltpu.CompilerParams(has_side_effects=EFFECT),
    )(*[pltpu.with_memory_space_constraint(b, pltpu.HBM) for b in bufs], pltpu.with_memory_space_constraint(ride, pltpu.HBM))
    return list(outs[:2 * n]), list(outs[2 * n:2 * n + nb]), outs[-1]


def _split_wait(name, bufs, sems, after, n, copies_of):
    nb = len(bufs)

    def body(*refs):
        s = refs[nb:nb + 2 * n]
        for cp in copies_of(refs[:nb], s[:n], s[n:]):
            sent, landed = cp if isinstance(cp, tuple) else (cp, cp)
            sent.wait_send()
            landed.wait_recv()

    outs = _pc(
        body, name=name, out_shape=tuple(pltpu.HBM(b.shape, b.dtype) for b in bufs),
        in_specs=(HBM,) * nb + (SEM,) * (2 * n) + (pl.BlockSpec(memory_space=pl.ANY),), out_specs=(HBM,) * nb,
        input_output_aliases={k: k for k in range(nb)},
        compiler_params=pltpu.CompilerParams(has_side_effects=EFFECT),
    )(*bufs, *sems, after)
    return list(outs)


def _handover_copies(refs, send, recv):
    x, y, c, chips = _place()
    cps = []
    for k, chip in enumerate(chips):
        mine, theirs = _half_block(refs[0], chip, c), _half_block(refs[0], chip, 1 - c)
        desc = lambda blk: pltpu.make_async_remote_copy(src_ref=blk, dst_ref=blk, send_sem=send[k], recv_sem=recv[k],
                                                        device_id=(x, y, 1 - c), device_id_type=MESH)
        cps.append((desc(mine), desc(theirs)))
    return cps


def _swap_copies(nt):
    def copies_of(refs, send, recv):
        x, y, c, _ = _place()
        cps = []
        for k in range(nt):
            rh = refs[k].shape[1] // 2
            cps.append(pltpu.make_async_remote_copy(
                src_ref=refs[k].at[:, pl.ds((1 - c) * rh, rh), :], dst_ref=refs[nt + k],
                send_sem=send[k], recv_sem=recv[k], device_id=(x, y, 1 - c), device_id_type=MESH))
        return cps
    return copies_of


def _exchange_copies(nt):
    def copies_of(refs, send, recv):
        x, y, c, chips = _place()
        cps = []
        for t in range(nt):
            for k, chip in enumerate(chips):
                cps.append(pltpu.make_async_remote_copy(
                    src_ref=refs[t].at[2 * chip[0] + chip[1]], dst_ref=refs[nt + t].at[k],
                    send_sem=send[3 * t + k], recv_sem=recv[3 * t + k], device_id=(*chip, c), device_id_type=MESH))
        return cps
    return copies_of


class _StagedReduce:
    def __init__(self, place):
        self.place = place
        self.groups = {}
        self.halves = {}

    @staticmethod
    def slab(t, r):
        return t.reshape(N_CHIPS, r, t.size // (N_CHIPS * r))

    def begin(self, g, grads, ride):
        names = list(grads)
        ts = [self.slab(t, r) for t, r in grads.values()]
        lands = [lax.empty((N_CHIPS, t.shape[1] // 2, t.shape[2]), F32) for t in ts]
        sems, bufs, ride = _split_start(f"grad_swap_start_{g}", ts + lands, ride, len(ts), _swap_copies(len(ts)))
        self.groups[g] = dict(names=names, bufs=bufs, sems=sems)
        return ride

    def advance(self, g, after, ride):
        st = self.groups[g]
        nt = len(st["names"])
        bufs = _split_wait(f"grad_swap_wait_{g}", st["bufs"], st["sems"], after, nt, _swap_copies(nt))
        st["ts"], st["ls"] = bufs[:nt], bufs[nt:]
        ps = [_pair_sum(t, l, self.place, f"pair_sum_{n}") for t, l, n in zip(st["ts"], st["ls"], st["names"])]
        lands = [lax.empty((3,) + p.shape[1:], BF16) for p in ps]
        st["sems"], st["bufs"], ride = _split_start(f"grad_exchange_start_{g}", ps + lands, ride, 3 * nt, _exchange_copies(nt))
        return ride

    def finish(self, g, after):
        st = self.groups[g]
        nt = len(st["names"])
        bufs = _split_wait(f"grad_exchange_wait_{g}", st["bufs"], st["sems"], after, 3 * nt, _exchange_copies(nt))
        for t, l, r, n in zip(st["ts"], st["ls"], bufs[nt:], st["names"]):
            if n[-1] in "01":
                self.halves[n[:-1]] = _final_sum(t, l, r, self.place, f"final_sum_{n}", layer=int(n[-1]), layers=2,
                                                 into=self.halves.get(n[:-1]))
            else:
                self.halves[n] = _final_sum(t, l, r, self.place, f"final_sum_{n}")


def _allgather_small(v):
    rows = v.shape[0]

    def body(v_ref, out_ref, send_sems, recv_sems):
        x, y, c, _ = _place()
        me = 4 * x + 2 * y + c
        out_ref[me] = v_ref[...]
        cps = []
        for k in range(1, 8):
            fx, fy, fc = (k >> 2) & 1, (k >> 1) & 1, k & 1
            to = (1 - x if fx else x, 1 - y if fy else y, 1 - c if fc else c)
            cps.append(pltpu.make_async_remote_copy(
                src_ref=v_ref, dst_ref=out_ref.at[me], send_sem=send_sems.at[k - 1], recv_sem=recv_sems.at[k - 1],
                device_id=to, device_id_type=MESH))
        for cp in cps:
            cp.start()
        for cp in cps:
            cp.wait()

    return _pc(
        body, name="allgather_small",
        in_specs=[pl.BlockSpec(memory_space=pltpu.VMEM)], out_specs=pl.BlockSpec(memory_space=pltpu.VMEM),
        out_shape=jax.ShapeDtypeStruct((8, rows, LANES), F32),
        scratch_shapes=[pltpu.SemaphoreType.DMA((7,)), pltpu.SemaphoreType.DMA((7,))],
    )(v)


def _adamw_math(w, g, m, v):
    m = ADAM_B1 * m + (1.0 - ADAM_B1) * g
    v = ADAM_B2 * v + (1.0 - ADAM_B2) * (g * g)
    m_hat = m / (1.0 - ADAM_B1 ** ADAM_STEP)
    v_hat = v / (1.0 - ADAM_B2 ** ADAM_STEP)
    return -ADAM_LR * (m_hat / (jnp.sqrt(v_hat) + ADAM_EPS) + ADAM_WD * w), m, v


def _adamw(w, g, m, v, name):
    r, cols = w.shape
    tr = min(r, 256)

    def body(w_ref, g_ref, m_ref, v_ref, go_ref, d_ref, mo_ref, vo_ref):
        gv = g_ref[...]
        d, mn, vn = _adamw_math(w_ref[...], gv, m_ref[...], v_ref[...])
        go_ref[...] = gv
        d_ref[...] = d
        mo_ref[...] = mn
        vo_ref[...] = vn

    row = pl.BlockSpec((tr, cols), lambda i: (i, 0))
    return _pc(
        body, name=name, grid=(r // tr,), in_specs=[row] * 4, out_specs=[row] * 4,
        out_shape=[jax.ShapeDtypeStruct((r, cols), F32)] * 4,
        compiler_params=_params(("parallel",)),
    )(w, g, m, v)


def _adamw_small(w, gathered, m, v):
    rows = w.shape[0]

    def body(w_ref, g_ref, m_ref, v_ref, go_ref, d_ref, mo_ref, vo_ref):
        g = g_ref[0]
        for k in range(1, 8):
            g = g + g_ref[k]
        d, mn, vn = _adamw_math(w_ref[...], g, m_ref[...], v_ref[...])
        go_ref[...] = g
        d_ref[...] = d
        mo_ref[...] = mn
        vo_ref[...] = vn

    return _pc(
        body, name="adamw_small",
        out_shape=[jax.ShapeDtypeStruct((rows, LANES), F32)] * 4,
    )(w, gathered, m, v)


_BIAS_ROWS = 32


def _own_slot(flat, chip):
    return lax.dynamic_update_slice(lax.empty((N_CHIPS,) + flat.shape, flat.dtype), flat[None], (chip, 0, 0))


def _pack_first(hyb_w_in, hyb_w_out):
    return jnp.concatenate([t.astype(BF16).reshape(-1, 1024) for t in (hyb_w_in, hyb_w_out)], axis=0)


def _unpack_first(g):
    return {"hyb_w_in": g[:, 0:768, :].reshape(N_CHIPS, 1024, 768), "hyb_w_out": g[:, 768:1024, :].reshape(1024, 1024)}


def _pack_rest(mlp_w_up, mlp_w_down, swa_w_qkv, swa_w_out, swa_b_qkv):
    parts = [t.astype(BF16).reshape(-1, 1024) for t in (mlp_w_up, mlp_w_down, swa_w_qkv, swa_w_out)]
    bias = lax.bitcast_convert_type(swa_b_qkv.reshape(384), BF16).reshape(1, 768)
    bias = jnp.pad(bias, ((0, _BIAS_ROWS - 1), (0, 256)))
    return jnp.concatenate(parts + [bias], axis=0)


def _unpack_rest(g):
    W = {
        "packed": g,
        "swa_w_qkv": g[:, 4096:4480, :].reshape(N_CHIPS, 1024, 384),
        "swa_w_out": g[:, 4480:4736, :].reshape(1024, 1024),
    }
    bias = lax.bitcast_convert_type(g[:, 4736, :768].reshape(N_CHIPS, 384, 2), F32).reshape(1536)
    return W, bias


_SMALL = (("norm_mix", 16), ("norm_mlp", 16), ("ret_gn_gain", 4), ("dil_q_gain", 1), ("dil_k_gain", 1),
          ("swa_b_qkv", 12), ("swa_q_gain", 1), ("swa_k_gain", 1), ("swa_sinks", 1), ("loss", 1))
_SUBLANES = 8


def _slot(r):
    return -(-r // _SUBLANES) * _SUBLANES


def _pack_small(d):
    return jnp.concatenate([jnp.pad(d[n].reshape(r, LANES), ((0, _slot(r) - r), (0, 0))) for n, r in _SMALL], axis=0)


def _unpack_small(p):
    out, o = {}, 0
    for n, r in _SMALL:
        out[n] = p[o:o + r]
        o += _slot(r)
    return out


def kernel(x, positions, norm_mix, norm_mlp, mlp_w_up, mlp_w_down, hyb_w_in, hyb_w_out, ret_gn_gain, dil_q_gain, dil_k_gain, swa_w_qkv, swa_b_qkv, swa_w_out, swa_q_gain, swa_k_gain, swa_sinks, loss_target, m_norm_mix, m_norm_mlp, m_mlp_w_up, m_mlp_w_down, m_hyb_w_in, m_hyb_w_out, m_ret_gn_gain, m_dil_q_gain, m_dil_k_gain, m_swa_w_qkv, m_swa_b_qkv, m_swa_w_out, m_swa_q_gain, m_swa_k_gain, m_swa_sinks, v_norm_mix, v_norm_mlp, v_mlp_w_up, v_mlp_w_down, v_hyb_w_in, v_hyb_w_out, v_ret_gn_gain, v_dil_q_gain, v_dil_k_gain, v_swa_w_qkv, v_swa_b_qkv, v_swa_w_out, v_swa_q_gain, v_swa_k_gain, v_swa_sinks):
    ax, ay, ac = lax.axis_index("x"), lax.axis_index("y"), lax.axis_index("c")
    chip = 2 * ax + ay
    place = jnp.stack([chip, ac]).astype(jnp.int32)
    S = x.shape[1]

    first = _own_slot(_pack_first(hyb_w_in[0], hyb_w_out[0]), chip)
    *sems, first, pos_col = _gather_start(first, positions.reshape(S, 1), "allgather_first_start")
    pos_col, late = lax.optimization_barrier((pos_col, (mlp_w_up, mlp_w_down, swa_w_qkv, swa_w_out, swa_b_qkv)))
    rest = _own_slot(_pack_rest(late[0], late[1], late[2][0], late[3][0], late[4][0]), chip)
    flight = {}

    def first_of(after):
        g = _gather_handover(_gather_wait(first, sems, after, "allgather_first_wait"), "allgather_first_handover")
        *flight["sems"], flight["buf"], g = _gather_start(rest, g, "allgather_rest_start")
        return _unpack_first(g)

    def rest_begin(ride):
        buf = _gather_wait(flight["buf"], flight["sems"], ride, "allgather_rest_wait")
        flight["sems"], flight["bufs"], ride = _split_start("allgather_rest_handover_start", [buf], ride, 3, _handover_copies)
        return ride

    def rest_of(after):
        return _unpack_rest(_split_wait("allgather_rest_handover_wait", flight["bufs"], flight["sems"], after, 3,
                                        _handover_copies)[0])

    P = dict(norm_mix=norm_mix, norm_mlp=norm_mlp, ret_gn_gain=ret_gn_gain, dil_q_gain=dil_q_gain, dil_k_gain=dil_k_gain,
             swa_q_gain=swa_q_gain, swa_k_gain=swa_k_gain, swa_sinks=swa_sinks)

    red = _StagedReduce(place)
    loss_l, grad_x, gp = _local_step(x[0], pos_col, loss_target[0], first_of, rest_begin, rest_of, P, red)

    params = dict(mlp_w_up=(mlp_w_up, m_mlp_w_up, v_mlp_w_up), mlp_w_down=(mlp_w_down, m_mlp_w_down, v_mlp_w_down),
                  hyb_w_in=(hyb_w_in, m_hyb_w_in, v_hyb_w_in), hyb_w_out=(hyb_w_out, m_hyb_w_out, v_hyb_w_out),
                  swa_w_qkv=(swa_w_qkv, m_swa_w_qkv, v_swa_w_qkv), swa_w_out=(swa_w_out, m_swa_w_out, v_swa_w_out))
    big = {}

    def update(names, share_name):
        hs = [(red.halves[n], params[n][0].shape[0]) for n in names]
        for n, g in zip(names, _share_halves(hs, share_name)):
            rows = g.shape[0]
            w, m, v = (t.reshape(rows, -1) for t in params[n])
            big[n] = [t.reshape(params[n][0].shape) for t in _adamw(w, g, m, v, f"adamw_{n}")]

    names = ["mlp_w_up", "mlp_w_down", "hyb_w_out", "swa_w_qkv", "swa_w_out"]
    red.halves[names[0]] = red.advance("win", grad_x, red.halves[names[0]])
    update(names, "grad_share_halves")
    red.finish("win", big[names[-1]][1])
    update(["hyb_w_in"], "grad_share_last")

    gsm = dict(gp, loss=loss_l)
    gsm["swa_sinks"] = jnp.pad(gp["swa_sinks"].reshape(16, HEAD)[:, 0], (0, LANES - 16))
    gathered = _allgather_small(_pack_small(gsm))

    def small_pack(norm_mix, norm_mlp, gn, dq, dk, b, sq, sk, sinks):
        dup = lambda t: jnp.tile(t.reshape(1, HEAD), (1, 2))
        bias = lax.dynamic_update_slice(jnp.zeros((12, LANES), F32), b.reshape(3, LANES), (3 * chip, 0))
        return _pack_small(dict(norm_mix=norm_mix, norm_mlp=norm_mlp, ret_gn_gain=gn, dil_q_gain=dup(dq), dil_k_gain=dup(dk),
                                swa_b_qkv=bias, swa_q_gain=dup(sq), swa_k_gain=dup(sk),
                                swa_sinks=jnp.pad(sinks.reshape(16), (0, LANES - 16)), loss=jnp.zeros((1, LANES), F32)))

    pw = small_pack(norm_mix, norm_mlp, ret_gn_gain, dil_q_gain, dil_k_gain, swa_b_qkv, swa_q_gain, swa_k_gain, swa_sinks)
    pm = small_pack(m_norm_mix, m_norm_mlp, m_ret_gn_gain, m_dil_q_gain, m_dil_k_gain, m_swa_b_qkv, m_swa_q_gain, m_swa_k_gain, m_swa_sinks)
    pv = small_pack(v_norm_mix, v_norm_mlp, v_ret_gn_gain, v_dil_q_gain, v_dil_k_gain, v_swa_b_qkv, v_swa_q_gain, v_swa_k_gain, v_swa_sinks)
    small = [_unpack_small(t) for t in _adamw_small(pw, gathered, pm, pv)]

    def small_out(n, k):
        t = small[k][n]
        if n in ("norm_mix", "norm_mlp"):
            return t.reshape(2, D_MODEL)
        if n == "ret_gn_gain":
            return t.reshape(1, RET_HEADS, 128)
        if n == "swa_b_qkv":
            return lax.dynamic_slice(t, (3 * chip, 0), (3, LANES)).reshape(1, 384)
        if n == "swa_sinks":
            return t[0, :16].reshape(1, 16)
        return t[0, :HEAD].reshape(1, HEAD)

    order = ["norm_mix", "norm_mlp", "mlp_w_up", "mlp_w_down", "hyb_w_in", "hyb_w_out", "ret_gn_gain", "dil_q_gain",
             "dil_k_gain", "swa_w_qkv", "swa_b_qkv", "swa_w_out", "swa_q_gain", "swa_k_gain", "swa_sinks"]
    is_big = {"mlp_w_up", "mlp_w_down", "hyb_w_in", "hyb_w_out", "swa_w_qkv", "swa_w_out"}
    outs = [small[0]["loss"][0, 0], grad_x[None]]
    for k in range(4):
        outs += [big[n][k] if n in is_big else small_out(n, k) for n in order]
    return tuple(outs)
```

```python
import numpy as np
import jax
import jax.numpy as jnp
from jax import lax
from jax.experimental import pallas as pl
from jax.experimental.pallas import tpu as pltpu

F32, BF16 = jnp.float32, jnp.bfloat16
MESH = pl.DeviceIdType.MESH

LANES = 128
VMEM_LIMIT = 48 << 20
VMEM_LIMIT_WIDE = 60 << 20
D_MODEL = 1024
HEAD = 64
EPS = 1e-6
BLK = 128
RET_HEADS = 4
RET_THETA = 10000.0
ROPE_THETA = 500000.0
ROPE_DIMS = 16
DIL_PATTERNS = ((128, 1), (512, 4), (2048, 16))
SWA_DIST = 127
N_CHIPS = 4
ADAM_LR, ADAM_B1, ADAM_B2, ADAM_EPS, ADAM_WD, ADAM_STEP = 0.001, 0.9, 0.999, 1e-08, 0.01, 10

_LOG_GAMMA = [float(np.log1p(-np.exp2(np.float32(-5.0 - h)))) for h in range(RET_HEADS)]


def _pc(body, **kw):
    return pl.pallas_call(body, **kw)


def _params(sem, limit=VMEM_LIMIT):
    return pltpu.CompilerParams(dimension_semantics=sem, vmem_limit_bytes=limit)


def _matmul(a, b, *, dims, tm, tn, tk, outs, name, epilogue=None, extras=(), b_cs=False, b_rs=0, b_row0=0, b_rows=0,
            o_cs=0, a_pro=None, vmem=VMEM_LIMIT):
    if dims == "nn":
        M, K = a.shape
        N = b.shape[0] * b.shape[2] if b_cs else b.shape[1]
        a_spec = pl.BlockSpec((tm, tk), lambda i, j, k: (i, k))
        if b_cs:
            npt = b.shape[2] // tn
            b_spec = pl.BlockSpec((None, tk, tn), lambda i, j, k: (j // npt, k + b_row0, j % npt))
        elif b_rs:
            K, N, kps = b.shape[0] * b_rs, b.shape[2], b_rs // tk
            b_spec = pl.BlockSpec((None, tk, tn), lambda i, j, k: (k // kps, b_row0 + k % kps, j))
        else:
            b_spec = pl.BlockSpec((tk, tn), lambda i, j, k: (k, j))
        contract = (((1,), (0,)), ((), ()))
    elif dims == "nt":
        M, K = a.shape
        N = (b_rows or b.shape[1]) if b_cs else b.shape[0]
        a_spec = pl.BlockSpec((tm, tk), lambda i, j, k: (i, k))
        if b_cs:
            kpt = b.shape[2] // tk
            b_spec = pl.BlockSpec((None, tn, tk), lambda i, j, k: (k // kpt, j + b_row0, k % kpt))
        elif b_rs:
            N, jps = b.shape[0] * b_rs, b_rs // tn
            b_spec = pl.BlockSpec((None, tn, tk), lambda i, j, k: (j // jps, b_row0 + j % jps, k))
        else:
            b_spec = pl.BlockSpec((tn, tk), lambda i, j, k: (j, k))
        contract = (((1,), (1,)), ((), ()))
    else:
        K, M = a.shape
        N = b.shape[1]
        a_spec = pl.BlockSpec((tk, tm), lambda i, j, k: (k, i))
        b_spec = pl.BlockSpec((tk, tn), lambda i, j, k: (k, j))
        contract = (((0,), (0,)), ((), ()))
    assert M % tm == 0 and N % tn == 0 and K % tk == 0, (name, M, N, K, tm, tn, tk)
    nk = K // tk
    ex_specs = []
    for arr, kind in extras:
        if kind == "mn":
            ex_specs.append(pl.BlockSpec((tm, tn), lambda i, j, k: (i, j)))
        elif kind == "n":
            ex_specs.append(pl.BlockSpec((1, tn), lambda i, j, k: (0, j)))
        elif kind == "full":
            ex_specs.append(pl.BlockSpec(arr.shape, lambda i, j, k, nd=arr.ndim: (0,) * nd))
        else:
            ex_specs.append(pl.BlockSpec((tm, kind), lambda i, j, k: (i, 0)))
    if o_cs:
        n_sh = N // o_cs
        opt = n_sh // tn
        o_shape = (o_cs, M, n_sh)
        o_spec = pl.BlockSpec((None, tm, tn), lambda i, j, k: (j // opt, i, j % opt))
    else:
        o_shape = (M, N)
        o_spec = pl.BlockSpec((tm, tn), lambda i, j, k: (i, j))
    o_specs, o_shapes, summed = [], [], []
    for o in outs:
        if isinstance(o, tuple) and o[0] == "colsum":
            assert N == tn
            o_specs.append(pl.BlockSpec((1, tn), lambda i, j, k: (0, j)))
            o_shapes.append(jax.ShapeDtypeStruct((1, N), F32))
            summed.append(True)
        elif isinstance(o, tuple):
            o_specs.append(pl.BlockSpec((tm, o[1]), lambda i, j, k: (i, 0)))
            o_shapes.append(jax.ShapeDtypeStruct((M, o[1]), o[0]))
            summed.append(False)
        else:
            o_specs.append(o_spec)
            o_shapes.append(jax.ShapeDtypeStruct(o_shape, o))
            summed.append(False)
    n_ex, n_out = len(extras), len(outs)
    if epilogue is None:
        epilogue = lambda acc: (acc,)

    def body(a_ref, b_ref, *rest):
        ex, o_refs, acc = rest[:n_ex], rest[n_ex:n_ex + n_out], rest[-1]
        i, k = pl.program_id(0), pl.program_id(2)

        @pl.when(k == 0)
        def _():
            acc[...] = jnp.zeros_like(acc)

        av = a_ref[...] if a_pro is None else a_pro(a_ref[...])
        acc[...] += lax.dot_general(av.astype(BF16), b_ref[...].astype(BF16), contract, preferred_element_type=F32)

        @pl.when(k == nk - 1)
        def _():
            vals = epilogue(acc[...], *[e[...] for e in ex])
            for r, v, sm in zip(o_refs, vals, summed):
                if sm:
                    @pl.when(i == 0)
                    def _(r=r):
                        r[...] = jnp.zeros_like(r)

                    r[...] += v
                else:
                    r[...] = v.astype(r.dtype)

    res = _pc(
        body, name=name, grid=(M // tm, N // tn, nk),
        in_specs=[a_spec, b_spec] + ex_specs, out_specs=o_specs, out_shape=o_shapes,
        scratch_shapes=[pltpu.VMEM((tm, tn), F32)],
        compiler_params=_params(("arbitrary" if any(summed) else "parallel", "parallel", "arbitrary"), vmem),
    )(a, b, *[e for e, _ in extras])
    return res[0] if n_out == 1 else res


def _roll(x, s):
    return pltpu.roll(x, s % LANES, 1)


def _rope(x, A, B, C, half):
    return x * A + _roll(x, LANES - half) * B + _roll(x, half) * C


def _rope_t(g, A, B, C, half):
    return g * A + _roll(g * B, half) + _roll(g * C, LANES - half)


def _gmean(x, G):
    hi = x.astype(BF16)
    lo = (x - hi.astype(F32)).astype(BF16)
    Gb = G.astype(BF16)
    return jnp.dot(hi, Gb, preferred_element_type=F32) + jnp.dot(lo, Gb, preferred_element_type=F32)


def _head_mask(shape, half):
    lane = lax.broadcasted_iota(jnp.int32, shape, len(shape) - 1)
    return (lane >= HEAD) if half else (lane < HEAD)


def _group_matrix():
    i = np.arange(LANES)
    return jnp.asarray((i[:, None] // HEAD == i[None, :] // HEAD).astype(np.float32) / HEAD)


def _rope_inv():
    l = np.arange(LANES) % HEAD
    inv_r = np.power(np.float32(RET_THETA), -(l % 32).astype(np.float32) * np.float32(2.0 / HEAD))
    hp = ROPE_DIMS // 2
    inv_p = np.power(np.float32(ROPE_THETA), -(l % hp).astype(np.float32) * np.float32(2.0 / ROPE_DIMS))
    inv_p = np.where(l < ROPE_DIMS, inv_p, 0.0)
    return jnp.asarray(np.stack([inv_r, inv_p]).astype(np.float32))


def _tables(pos_col):
    S = pos_col.shape[0]
    tm = 512
    hp = ROPE_DIMS // 2

    def body(p_ref, inv_ref, o_ref):
        p = p_ref[...].astype(F32)
        lane = lax.broadcasted_iota(jnp.int32, (tm, LANES), 1) % HEAD
        ang = p * inv_ref[0:1, :]
        c, s = jnp.cos(ang), jnp.sin(ang)
        o_ref[:, 0:128] = c
        o_ref[:, 128:256] = jnp.where(lane < 32, -s, 0.0)
        o_ref[:, 256:384] = jnp.where(lane >= 32, s, 0.0)
        ang = p * inv_ref[1:2, :]
        c, s = jnp.cos(ang), jnp.sin(ang)
        o_ref[:, 384:512] = c
        o_ref[:, 512:640] = jnp.where(lane < hp, -s, 0.0)
        o_ref[:, 640:768] = jnp.where((lane >= hp) & (lane < ROPE_DIMS), s, 0.0)

    return _pc(
        body, name="rope_tables", grid=(S // tm,),
        in_specs=[pl.BlockSpec((tm, 1), lambda i: (i, 0)), pl.BlockSpec((2, LANES), lambda i: (0, 0))],
        out_specs=pl.BlockSpec((tm, 768), lambda i: (i, 0)),
        out_shape=jax.ShapeDtypeStruct((S, 768), F32),
        compiler_params=_params(("parallel",)),
    )(pos_col, _rope_inv())


def _tab(tab_ref, which):
    o = 384 * which
    return tab_ref[:, o:o + 128], tab_ref[:, o + 128:o + 256], tab_ref[:, o + 256:o + 384]


def _rms_fwd(x, g, name):
    S, Dm = x.shape
    tm = 512

    def body(x_ref, g_ref, h_ref):
        xv = x_ref[...]
        r = lax.rsqrt(jnp.mean(xv * xv, axis=-1, keepdims=True) + EPS)
        h_ref[...] = (xv * r * g_ref[...]).astype(BF16)

    return _pc(
        body, name=name, grid=(S // tm,),
        in_specs=[pl.BlockSpec((tm, Dm), lambda i: (i, 0)), pl.BlockSpec((1, Dm), lambda i: (0, 0))],
        out_specs=pl.BlockSpec((tm, Dm), lambda i: (i, 0)),
        out_shape=jax.ShapeDtypeStruct((S, Dm), BF16),
        compiler_params=_params(("parallel",)),
    )(x, g.reshape(1, Dm))


def _hn_fwd(x, gain, G):
    r = lax.rsqrt(_gmean(x * x, G) + EPS)
    return x * r * gain


def _hn_bwd(x, gain, dy, G):
    r = lax.rsqrt(_gmean(x * x, G) + EPS)
    t = dy * gain
    dx = r * t - x * (r * r * r) * _gmean(x * t, G)
    return dx, jnp.sum(dy * x * r, axis=0, keepdims=True)


def _fold_halves(v):
    return v + _roll(v, HEAD)


def _even_pre_fwd(proj, tab, qg, kg):
    S = proj.shape[0]
    tm = 256

    def body(p_ref, tab_ref, qg_ref, kg_ref, g_ref, rq_ref, rk_ref, rv_ref, dq_ref, dk_ref, dv_ref):
        Ar, Br, Cr = _tab(tab_ref, 0)
        Ap, Bp, Cp = _tab(tab_ref, 1)
        G = g_ref[...]
        for c in range(2):
            sl = slice(c * 128, (c + 1) * 128)
            rq_ref[:, sl] = _rope(p_ref[:, c * 128:(c + 1) * 128], Ar, Br, Cr, 32).astype(BF16)
            rk_ref[:, sl] = (_rope(p_ref[:, 256 + c * 128:256 + (c + 1) * 128], Ar, Br, Cr, 32) * 0.125).astype(BF16)
        rv_ref[...] = p_ref[:, 512:1024].astype(BF16)
        for c in range(4):
            sl = slice(c * 128, (c + 1) * 128)
            q = _hn_fwd(p_ref[:, 1536 + c * 128:1536 + (c + 1) * 128], qg_ref[...], G)
            dq_ref[:, sl] = _rope(q, Ap, Bp, Cp, 8).astype(BF16)
            k = _hn_fwd(p_ref[:, 2048 + c * 128:2048 + (c + 1) * 128], kg_ref[...], G)
            dk_ref[:, sl] = _rope(k, Ap, Bp, Cp, 8).astype(BF16)
        dv_ref[...] = p_ref[:, 2560:3072].astype(BF16)

    row = lambda w: pl.BlockSpec((tm, w), lambda i: (i, 0))
    vec = pl.BlockSpec((1, LANES), lambda i: (0, 0))
    return _pc(
        body, name="even_pre_fwd", grid=(S // tm,),
        in_specs=[row(3072), row(768), vec, vec, pl.BlockSpec((LANES, LANES), lambda i: (0, 0))],
        out_specs=[row(256), row(256), row(512), row(512), row(512), row(512)],
        out_shape=[jax.ShapeDtypeStruct((S, w), BF16) for w in (256, 256, 512, 512, 512, 512)],
        compiler_params=_params(("parallel",)),
    )(proj, tab, qg, kg, _group_matrix())


def _even_pre_bwd(proj, tab, qg, kg, drq, drk, drv, drg, dqs, dks, dvs):
    S = proj.shape[0]
    tm = 256
    npat = len(dqs)

    def body(p_ref, tab_ref, qg_ref, kg_ref, g_ref, drq_ref, drk_ref, drv_ref, drg_ref, *rest):
        dq_refs, dk_refs, dv_refs = rest[:npat], rest[npat:2 * npat], rest[2 * npat:3 * npat]
        dp_ref, dqg_ref, dkg_ref = rest[3 * npat:]
        Ar, Br, Cr = _tab(tab_ref, 0)
        Ap, Bp, Cp = _tab(tab_ref, 1)
        G = g_ref[...]
        for c in range(2):
            sl = slice(c * 128, (c + 1) * 128)
            dp_ref[:, c * 128:(c + 1) * 128] = _rope_t(drq_ref[:, sl], Ar, Br, Cr, 32).astype(BF16)
            dp_ref[:, 256 + c * 128:256 + (c + 1) * 128] = _rope_t(drk_ref[:, sl] * 0.125, Ar, Br, Cr, 32).astype(BF16)
        dp_ref[:, 512:1024] = drv_ref[...].astype(BF16)
        dp_ref[:, 1024:1536] = drg_ref[...].astype(BF16)
        accq = jnp.zeros((1, LANES), F32)
        acck = jnp.zeros((1, LANES), F32)
        for c in range(4):
            sl = slice(c * 128, (c + 1) * 128)
            g = dq_refs[0][:, sl]
            for r in dq_refs[1:]:
                g = g + r[:, sl]
            dx, dg = _hn_bwd(p_ref[:, 1536 + c * 128:1536 + (c + 1) * 128], qg_ref[...], _rope_t(g, Ap, Bp, Cp, 8), G)
            dp_ref[:, 1536 + c * 128:1536 + (c + 1) * 128] = dx.astype(BF16)
            accq = accq + dg
            g = dk_refs[0][:, sl]
            for r in dk_refs[1:]:
                g = g + r[:, sl]
            dx, dg = _hn_bwd(p_ref[:, 2048 + c * 128:2048 + (c + 1) * 128], kg_ref[...], _rope_t(g, Ap, Bp, Cp, 8), G)
            dp_ref[:, 2048 + c * 128:2048 + (c + 1) * 128] = dx.astype(BF16)
            acck = acck + dg
        g = dv_refs[0][...]
        for r in dv_refs[1:]:
            g = g + r[...]
        dp_ref[:, 2560:3072] = g.astype(BF16)

        @pl.when(pl.program_id(0) == 0)
        def _():
            dqg_ref[...] = jnp.zeros_like(dqg_ref)
            dkg_ref[...] = jnp.zeros_like(dkg_ref)

        dqg_ref[...] += _fold_halves(accq)
        dkg_ref[...] += _fold_halves(acck)

    row = lambda w: pl.BlockSpec((tm, w), lambda i: (i, 0))
    vec = pl.BlockSpec((1, LANES), lambda i: (0, 0))
    return _pc(
        body, name="even_pre_bwd", grid=(S // tm,),
        in_specs=[row(3072), row(768), vec, vec, pl.BlockSpec((LANES, LANES), lambda i: (0, 0)),
                  row(256), row(256), row(512), row(512)] + [row(512)] * (3 * npat),
        out_specs=[row(3072), vec, vec],
        out_shape=[jax.ShapeDtypeStruct((S, 3072), BF16), jax.ShapeDtypeStruct((1, LANES), F32),
                   jax.ShapeDtypeStruct((1, LANES), F32)],
        compiler_params=_params(("arbitrary",)),
    )(proj, tab, qg, kg, _group_matrix(), drq, drk, drv, drg, *dqs, *dks, *dvs)


def _ret_consts(pair, half):
    lg = jnp.where(pair == 0, _LOG_GAMMA[half], _LOG_GAMMA[2 + half]).astype(F32)
    i = lax.broadcasted_iota(jnp.int32, (BLK, BLK), 0)
    j = lax.broadcasted_iota(jnp.int32, (BLK, BLK), 1)
    diff = (i - j).astype(F32)
    decay = jnp.where(diff >= 0, jnp.exp(lg * jnp.maximum(diff, 0.0)), 0.0)
    t = lax.broadcasted_iota(jnp.int32, (BLK, 1), 0).astype(F32)
    xi = jnp.exp(lg * (t + 1.0))
    zeta = jnp.exp(lg * (BLK - 1.0 - t))
    cd = jnp.exp(jnp.full((1, 1), BLK, F32) * lg)
    return decay, xi, zeta, cd


RET_STEP = 8


def _ret_fwd(rq, rk, rv):
    S = rq.shape[0]
    nc = S // BLK
    rows = RET_STEP * BLK

    def body(q_ref, k_ref, v_ref, o_ref, st_ref, R):
        p, n = pl.program_id(0), pl.program_id(1)

        @pl.when(n == 0)
        def _():
            R[...] = jnp.zeros_like(R)

        consts = [_ret_consts(p, half) for half in range(2)]
        masks = [_head_mask((BLK, LANES), half) for half in range(2)]
        for ci in range(RET_STEP):
            rs = slice(ci * BLK, (ci + 1) * BLK)
            q2, k2 = q_ref[rs, :], k_ref[rs, :]
            for half in range(2):
                decay, xi, zeta, cd = consts[half]
                m = masks[half]
                qm = jnp.where(m, q2, jnp.zeros_like(q2))
                km = jnp.where(m, k2, jnp.zeros_like(k2))
                v = v_ref[rs, half * 128:(half + 1) * 128]
                Rb = R[half].astype(BF16)
                st_ref[ci, half] = Rb
                sc = lax.dot_general(qm, k2, (((1,), (1,)), ((), ())), preferred_element_type=F32) * decay
                o = jnp.dot(sc.astype(BF16), v, preferred_element_type=F32)
                o = o + jnp.dot(qm, Rb, preferred_element_type=F32) * xi
                o_ref[rs, half * 128:(half + 1) * 128] = o
                kz = (km.astype(F32) * zeta).astype(BF16)
                R[half] = R[half] * cd + lax.dot_general(kz, v, (((0,), (0,)), ((), ())), preferred_element_type=F32)

    return _pc(
        body, name="ret_fwd", grid=(2, nc // RET_STEP),
        in_specs=[pl.BlockSpec((rows, 128), lambda p, n: (n, p)), pl.BlockSpec((rows, 128), lambda p, n: (n, p)),
                  pl.BlockSpec((rows, 256), lambda p, n: (n, p))],
        out_specs=[pl.BlockSpec((rows, 256), lambda p, n: (n, p)),
                   pl.BlockSpec((None, RET_STEP, 2, 128, 128), lambda p, n: (p, n, 0, 0, 0))],
        out_shape=[jax.ShapeDtypeStruct((S, 512), F32), jax.ShapeDtypeStruct((2, nc, 2, 128, 128), BF16)],
        scratch_shapes=[pltpu.VMEM((2, 128, 128), F32)],
        compiler_params=_params(("parallel", "arbitrary")),
    )(rq, rk, rv)


def _ret_bwd(rq, rk, rv, states, do):
    S = rq.shape[0]
    nc = S // BLK
    ns = nc // RET_STEP
    rows = RET_STEP * BLK
    nt = (((1,), (1,)), ((), ()))
    tn = (((0,), (0,)), ((), ()))

    def body(q_ref, k_ref, v_ref, st_ref, do_ref, dq_ref, dk_ref, dv_ref, U):
        p, n = pl.program_id(0), pl.program_id(1)

        @pl.when(n == 0)
        def _():
            U[...] = jnp.zeros_like(U)

        consts = [_ret_consts(p, half) for half in range(2)]
        masks = [_head_mask((BLK, LANES), half) for half in range(2)]
        for ci in reversed(range(RET_STEP)):
            rs = slice(ci * BLK, (ci + 1) * BLK)
            q2, k2 = q_ref[rs, :], k_ref[rs, :]
            dq_acc = jnp.zeros((BLK, LANES), F32)
            dk_acc = jnp.zeros((BLK, LANES), F32)
            for half in range(2):
                decay, xi, zeta, cd = consts[half]
                m = masks[half]
                qm = jnp.where(m, q2, jnp.zeros_like(q2))
                km = jnp.where(m, k2, jnp.zeros_like(k2))
                v = v_ref[rs, half * 128:(half + 1) * 128]
                dob = do_ref[rs, half * 128:(half + 1) * 128].astype(BF16)
                Rb = st_ref[ci, half]
                Ub = U[half].astype(BF16)
                dsc = (lax.dot_general(dob, v, nt, preferred_element_type=F32) * decay).astype(BF16)
                xdo = (dob.astype(F32) * xi).astype(BF16)
                dq_acc += jnp.dot(dsc, km, preferred_element_type=F32) + lax.dot_general(xdo, Rb, nt, preferred_element_type=F32)
                dk_acc += lax.dot_general(dsc, qm, tn, preferred_element_type=F32) \
                    + lax.dot_general(v, Ub, nt, preferred_element_type=F32) * zeta
                sc = (lax.dot_general(qm, k2, nt, preferred_element_type=F32) * decay).astype(BF16)
                kz = (km.astype(F32) * zeta).astype(BF16)
                dv_ref[rs, half * 128:(half + 1) * 128] = lax.dot_general(sc, dob, tn, preferred_element_type=F32) \
                    + jnp.dot(kz, Ub, preferred_element_type=F32)
                U[half] = U[half] * cd + lax.dot_general(qm, xdo, tn, preferred_element_type=F32)
            dq_ref[rs, :] = dq_acc
            dk_ref[rs, :] = dk_acc

    rev = lambda w: pl.BlockSpec((rows, w), lambda p, n: (ns - 1 - n, p))
    return _pc(
        body, name="ret_bwd", grid=(2, ns),
        in_specs=[rev(128), rev(128), rev(256),
                  pl.BlockSpec((None, RET_STEP, 2, 128, 128), lambda p, n: (p, ns - 1 - n, 0, 0, 0)), rev(256)],
        out_specs=[rev(128), rev(128), rev(256)],
        out_shape=[jax.ShapeDtypeStruct((S, 256), F32), jax.ShapeDtypeStruct((S, 256), F32),
                   jax.ShapeDtypeStruct((S, 512), F32)],
        scratch_shapes=[pltpu.VMEM((2, 128, 128), F32)],
        compiler_params=_params(("parallel", "arbitrary")),
    )(rq, rk, rv, states, do)


ATT_TILE = 2048


def _rows(ref, start, n, r):
    if r == 1:
        return ref[pl.ds(start, n), :]
    return ref[pl.ds(start, n, stride=r), :]


def _twice(x):
    return jnp.concatenate([x, x], axis=0)


def _stack_heads(x, masks):
    zero = jnp.zeros_like(x)
    return jnp.concatenate([jnp.where(masks[0], x, zero), jnp.where(masks[1], x, zero)], axis=0)


def _set_rows(ref, start, n, r, val):
    if r == 1:
        ref[pl.ds(start, n), :] = val
    else:
        ref[pl.ds(start, n, stride=r), :] = val


def _band_geometry(S, patterns):
    rmax = max(r for _, r in patterns)
    H = BLK * rmax
    T = min(S, ATT_TILE)
    assert T % H == 0 and S % T == 0
    return H, T, S // T, T // BLK


def _band_fwd(q, k, v, *, patterns, nq, name, sinks=None, want_bf16=False):
    S, Ck = k.shape
    H, T, nt, nbt = _band_geometry(S, patterns)
    ncol = Ck // LANES
    scale = HEAD ** -0.5
    has_sink = sinks is not None
    nt_dims = (((1,), (1,)), ((), ()))

    def body(*refs):
        q_ref, kp_ref, kc_ref, vp_ref, vc_ref = refs[:5]
        sk_ref = refs[5] if has_sink else None
        n_out = 3 if want_bf16 else 2
        outs = refs[5 + has_sink:5 + has_sink + n_out]
        qf, kf, vf, M, L, A = refs[5 + has_sink + n_out:]
        t = pl.program_id(1)
        kf[0:H, :] = kp_ref[...].astype(F32)
        kf[H:H + T, :] = kc_ref[...].astype(F32)
        vf[0:H, :] = vp_ref[...].astype(F32)
        vf[H:H + T, :] = vc_ref[...].astype(F32)
        r_i = lax.broadcasted_iota(jnp.int32, (BLK, 2 * BLK), 0)
        c_i = lax.broadcasted_iota(jnp.int32, (BLK, 2 * BLK), 1)
        dist_i = r_i + BLK - c_i
        masks = [_head_mask((BLK, LANES), h) for h in range(2)]

        for i in range(nq):
            qf[...] = q_ref[:, i * 128:(i + 1) * 128].astype(F32) * scale
            for p, (dist, r) in enumerate(patterns):
                in_band = (dist_i >= 0) & (dist_i <= dist)
                in_band_first = in_band & ((c_i >= BLK) | (t > 0))
                in_band, in_band_first = _twice(in_band), _twice(in_band_first)

                def unit(j, b, p=p, r=r, in_band=in_band, in_band_first=in_band_first):
                    q0 = j + b * (BLK * r)
                    q2 = _rows(qf, q0, BLK, r).astype(BF16)
                    kcat = _rows(kf, H + q0 - BLK * r, 2 * BLK, r).astype(BF16)
                    vcat = _rows(vf, H + q0 - BLK * r, 2 * BLK, r).astype(BF16)
                    valid = in_band if b > 0 else in_band_first
                    s = lax.dot_general(_stack_heads(q2, masks), kcat, nt_dims, preferred_element_type=F32)
                    s = jnp.where(valid, s, -jnp.inf)
                    mx = jnp.max(s, axis=1, keepdims=True)
                    pr = jnp.exp(s - mx)
                    den = jnp.sum(pr, axis=1, keepdims=True)
                    pv = jnp.dot(pr.astype(BF16), vcat, preferred_element_type=F32)
                    m2 = jnp.where(masks[0], mx[:BLK], mx[BLK:])
                    l2 = jnp.where(masks[0], den[:BLK], den[BLK:])
                    a2 = jnp.where(masks[0], pv[:BLK], pv[BLK:])
                    if p > 0:
                        mo = _rows(M, q0, BLK, r)
                        mn = jnp.maximum(mo, m2)
                        wa, wb = jnp.exp(mo - mn), jnp.exp(m2 - mn)
                        l2 = wa * _rows(L, q0, BLK, r) + wb * l2
                        a2 = wa * _rows(A, q0, BLK, r) + wb * a2
                        m2 = mn
                    _set_rows(M, q0, BLK, r, m2)
                    _set_rows(L, q0, BLK, r, l2)
                    _set_rows(A, q0, BLK, r, a2)

                for u in range(nbt):
                    unit(u % r, u // r)
            sl = slice(i * 128, (i + 1) * 128)
            mm, ll, aa = M[...], L[...], A[...]
            if has_sink:
                snk = sk_ref[:, sl]
                mn = jnp.maximum(mm, snk)
                w = jnp.exp(mm - mn)
                ll = ll * w + jnp.exp(snk - mn)
                aa = aa * w
                mm = mn
            o = aa / ll
            outs[0][:, sl] = o
            outs[1][:, sl] = mm + jnp.log(ll)
            if want_bf16:
                outs[2][:, sl] = o.astype(BF16)

    th = T // H
    qspec = pl.BlockSpec((T, nq * 128), lambda j, t: (t, j))
    cur = pl.BlockSpec((T, 128), lambda j, t: (t, j))
    prev = pl.BlockSpec((H, 128), lambda j, t: (jnp.maximum(t * th - 1, 0), j))
    in_specs = [qspec, prev, cur, prev, cur]
    args = [q, k, k, v, v]
    if has_sink:
        in_specs.append(pl.BlockSpec((1, nq * 128), lambda j, t: (0, j)))
        args.append(sinks)
    out_dts = [F32, F32] + ([BF16] if want_bf16 else [])
    return _pc(
        body, name=name, grid=(ncol, nt), in_specs=in_specs,
        out_specs=[qspec] * len(out_dts),
        out_shape=[jax.ShapeDtypeStruct(q.shape, dt) for dt in out_dts],
        scratch_shapes=[pltpu.VMEM((T, LANES), F32), pltpu.VMEM((H + T, LANES), F32), pltpu.VMEM((H + T, LANES), F32),
                        pltpu.VMEM((T, LANES), F32), pltpu.VMEM((T, LANES), F32), pltpu.VMEM((T, LANES), F32)],
        compiler_params=_params(("parallel", "parallel")),
    )(*args)


def _band_bwd(q, k, v, lse, delta, do, *, patterns, nq, name, sinks=None, do_col0=0):
    S, Ck = k.shape
    H, T, nt, nbt = _band_geometry(S, patterns)
    ncol = Ck // LANES
    scale = HEAD ** -0.5
    has_sink = sinks is not None
    nt_dims = (((1,), (1,)), ((), ()))
    tn_dims = (((0,), (0,)), ((), ()))

    def body(*refs):
        (qc_ref, qn_ref, kp_ref, kc_ref, vp_ref, vc_ref, lc_ref, ln_ref, ec_ref, en_ref, dc_ref, dn_ref) = refs[:12]
        sk_ref = refs[12] if has_sink else None
        n_out = 4 if has_sink else 3
        outs = refs[12 + has_sink:12 + has_sink + n_out]
        dq_ref, dk_ref, dv_ref = outs[:3]
        qf, kf, vf, lf, ef, df = refs[12 + has_sink + n_out:]
        t = pl.program_id(1)
        kf[0:H, :] = kp_ref[...].astype(F32)
        kf[H:H + T, :] = kc_ref[...].astype(F32)
        vf[0:H, :] = vp_ref[...].astype(F32)
        vf[H:H + T, :] = vc_ref[...].astype(F32)
        dk_ref[...] = jnp.zeros_like(dk_ref)
        dv_ref[...] = jnp.zeros_like(dv_ref)
        r_i = lax.broadcasted_iota(jnp.int32, (BLK, 2 * BLK), 0)
        c_i = lax.broadcasted_iota(jnp.int32, (BLK, 2 * BLK), 1)
        dist_q = r_i + BLK - c_i
        dist_h = dist_q[:, :BLK]
        m1 = [_head_mask((BLK, LANES), h) for h in range(2)]

        def stacked_inputs(q2, do2, l2, e2):
            spread = lambda v: jnp.concatenate([jnp.where(m1[0], v, _roll(v, HEAD)), jnp.where(m1[1], v, _roll(v, HEAD))], axis=0)
            return _stack_heads(q2, m1), _stack_heads(do2.astype(BF16), m1), spread(l2), spread(e2)

        for i in range(nq):
            sl = slice(i * 128, (i + 1) * 128)
            qf[0:T, :] = qc_ref[:, sl].astype(F32) * scale
            qf[T:T + H, :] = qn_ref[:, sl].astype(F32) * scale
            for buf, c_ref, n_ref in ((lf, lc_ref, ln_ref), (ef, ec_ref, en_ref), (df, dc_ref, dn_ref)):
                buf[0:T, :] = c_ref[:, sl]
                buf[T:T + H, :] = n_ref[:, sl]
            if has_sink:
                @pl.when(t == 0)
                def _():
                    outs[3][:, sl] = jnp.zeros((1, LANES), F32)

                outs[3][:, sl] += jnp.sum(-jnp.exp(sk_ref[:, sl] - lc_ref[:, sl]) * ec_ref[:, sl], axis=0, keepdims=True)
            for p, (dist, r) in enumerate(patterns):
                band_q = (dist_q >= 0) & (dist_q <= dist)
                band_first = band_q & ((c_i >= BLK) | (t > 0))
                band_h = (dist_h >= 0) & (dist_h <= dist)
                band_q, band_first, band_h = _twice(band_q), _twice(band_first), _twice(band_h)

                def add_rows(ref, start, val, r=r):
                    _set_rows(ref, start, BLK, r, _rows(ref, start, BLK, r) + val)

                def unit(j, b, p=p, r=r, band_q=band_q, band_first=band_first):
                    q0 = j + b * (BLK * r)
                    q2 = _rows(qf, q0, BLK, r).astype(BF16)
                    do2, l2, e2 = _rows(df, q0, BLK, r), _rows(lf, q0, BLK, r), _rows(ef, q0, BLK, r)
                    kcat = _rows(kf, H + q0 - BLK * r, 2 * BLK, r).astype(BF16)
                    vcat = _rows(vf, H + q0 - BLK * r, 2 * BLK, r).astype(BF16)
                    valid = band_q if b > 0 else band_first
                    qs, dos, ls, es = stacked_inputs(q2, do2, l2, e2)
                    s = lax.dot_general(qs, kcat, nt_dims, preferred_element_type=F32)
                    pr = jnp.where(valid, jnp.exp(s - jnp.concatenate([ls, ls], axis=1)), 0.0)
                    dp = lax.dot_general(dos, vcat, nt_dims, preferred_element_type=F32)
                    ds = (pr * (dp - jnp.concatenate([es, es], axis=1))).astype(BF16)
                    dqs = jnp.dot(ds, kcat, preferred_element_type=F32) * scale
                    dq2 = jnp.where(m1[0], dqs[:BLK], dqs[BLK:])
                    dvc = lax.dot_general(pr.astype(BF16), dos, tn_dims, preferred_element_type=F32)
                    dkc = lax.dot_general(ds, qs, tn_dims, preferred_element_type=F32)
                    if p > 0:
                        dq2 = dq2 + _rows(dq_ref.at[:, sl], q0, BLK, r)
                    _set_rows(dq_ref.at[:, sl], q0, BLK, r, dq2)
                    add_rows(dk_ref, q0, dkc[BLK:])
                    add_rows(dv_ref, q0, dvc[BLK:])
                    if b > 0:
                        add_rows(dk_ref, q0 - BLK * r, dkc[:BLK])
                        add_rows(dv_ref, q0 - BLK * r, dvc[:BLK])

                def halo_unit(j, r=r, band_h=band_h):
                    k0 = j + (nbt // r - 1) * (BLK * r)
                    q2 = _rows(qf, T + j, BLK, r).astype(BF16)
                    do2, l2, e2 = _rows(df, T + j, BLK, r), _rows(lf, T + j, BLK, r), _rows(ef, T + j, BLK, r)
                    kc = _rows(kf, H + k0, BLK, r).astype(BF16)
                    vc = _rows(vf, H + k0, BLK, r).astype(BF16)
                    qs, dos, ls, es = stacked_inputs(q2, do2, l2, e2)
                    s = lax.dot_general(qs, kc, nt_dims, preferred_element_type=F32)
                    pr = jnp.where(band_h, jnp.exp(s - ls), 0.0)
                    dp = lax.dot_general(dos, vc, nt_dims, preferred_element_type=F32)
                    ds = (pr * (dp - es)).astype(BF16)
                    add_rows(dk_ref, k0, lax.dot_general(ds, qs, tn_dims, preferred_element_type=F32))
                    add_rows(dv_ref, k0, lax.dot_general(pr.astype(BF16), dos, tn_dims, preferred_element_type=F32))

                for u in range(nbt):
                    unit(u % r, u // r)
                if nt > 1:
                    @pl.when(t < nt - 1)
                    def _(r=r, halo_unit=halo_unit):
                        for j in range(r):
                            halo_unit(j)

    th = T // H
    last = S // H - 1
    qcur = pl.BlockSpec((T, nq * 128), lambda j, t: (t, j))
    qnext = pl.BlockSpec((H, nq * 128), lambda j, t: (jnp.minimum((t + 1) * th, last), j))
    cur = pl.BlockSpec((T, 128), lambda j, t: (t, j))
    prev = pl.BlockSpec((H, 128), lambda j, t: (jnp.maximum(t * th - 1, 0), j))
    dcur = pl.BlockSpec((T, nq * 128), lambda j, t: (t, j + do_col0))
    dnext = pl.BlockSpec((H, nq * 128), lambda j, t: (jnp.minimum((t + 1) * th, last), j + do_col0))
    in_specs = [qcur, qnext, prev, cur, prev, cur, qcur, qnext, qcur, qnext, dcur, dnext]
    args = [q, q, k, k, v, v, lse, lse, delta, delta, do, do]
    out_specs = [qcur, cur, cur]
    out_shape = [jax.ShapeDtypeStruct(q.shape, F32), jax.ShapeDtypeStruct(k.shape, F32), jax.ShapeDtypeStruct(k.shape, F32)]
    if has_sink:
        vec = pl.BlockSpec((1, nq * 128), lambda j, t: (0, j))
        in_specs.append(vec)
        args.append(sinks)
        out_specs.append(vec)
        out_shape.append(jax.ShapeDtypeStruct((1, q.shape[1]), F32))
    big = pltpu.VMEM((T + H, LANES), F32)
    return _pc(
        body, name=name, grid=(ncol, nt), in_specs=in_specs, out_specs=out_specs, out_shape=out_shape,
        scratch_shapes=[big] * 6,
        compiler_params=_params(("parallel", "arbitrary")),
    )(*args)


def _even_post_fwd(ro, proj, gn, da):
    S = ro.shape[0]
    tm = 256

    def body(ro_ref, rg_ref, gn_ref, da_ref, mix_ref):
        for c in range(4):
            sl = slice(c * 128, (c + 1) * 128)
            x = ro_ref[:, sl]
            mu = jnp.mean(x, axis=1, keepdims=True)
            xc = x - mu
            var = jnp.mean(xc * xc, axis=1, keepdims=True)
            y = xc * lax.rsqrt(var + EPS) * gn_ref[:, sl]
            z = rg_ref[:, sl]
            mix_ref[:, sl] = (z * jax.nn.sigmoid(z) * y).astype(BF16)
        mix_ref[:, 512:1024] = da_ref[...].astype(BF16)

    row = lambda w: pl.BlockSpec((tm, w), lambda i: (i, 0))
    return _pc(
        body, name="even_post_fwd", grid=(S // tm,),
        in_specs=[row(512), pl.BlockSpec((tm, 512), lambda i: (i, 2)), pl.BlockSpec((1, 512), lambda i: (0, 0)), row(512)],
        out_specs=row(1024), out_shape=jax.ShapeDtypeStruct((S, 1024), BF16),
        compiler_params=_params(("parallel",)),
    )(ro, proj, gn, da)


def _even_post_bwd(ro, proj, gn, dmixed):
    S = ro.shape[0]
    tm = 256

    def body(ro_ref, rg_ref, gn_ref, dm_ref, dro_ref, drg_ref, dgn_ref):
        @pl.when(pl.program_id(0) == 0)
        def _():
            dgn_ref[...] = jnp.zeros_like(dgn_ref)

        for c in range(4):
            sl = slice(c * 128, (c + 1) * 128)
            x = ro_ref[:, sl]
            mu = jnp.mean(x, axis=1, keepdims=True)
            xc = x - mu
            rstd = lax.rsqrt(jnp.mean(xc * xc, axis=1, keepdims=True) + EPS)
            xh = xc * rstd
            gain = gn_ref[:, sl]
            y = xh * gain
            z = rg_ref[:, sl]
            sg = jax.nn.sigmoid(z)
            dra = dm_ref[:, sl]
            drg_ref[:, sl] = dra * y * sg * (1.0 + z * (1.0 - sg))
            dy = dra * z * sg
            dgn_ref[:, sl] += jnp.sum(dy * xh, axis=0, keepdims=True)
            dxh = dy * gain
            dro_ref[:, sl] = rstd * (dxh - jnp.mean(dxh, axis=1, keepdims=True)
                                     - xh * jnp.mean(dxh * xh, axis=1, keepdims=True))

    row = lambda w: pl.BlockSpec((tm, w), lambda i: (i, 0))
    vec = pl.BlockSpec((1, 512), lambda i: (0, 0))
    return _pc(
        body, name="even_post_bwd", grid=(S // tm,),
        in_specs=[row(512), pl.BlockSpec((tm, 512), lambda i: (i, 2)), vec, row(512)],
        out_specs=[row(512), row(512), vec],
        out_shape=[jax.ShapeDtypeStruct((S, 512), F32), jax.ShapeDtypeStruct((S, 512), F32),
                   jax.ShapeDtypeStruct((1, 512), F32)],
        compiler_params=_params(("arbitrary",)),
    )(ro, proj, gn, dmixed)


def _swa_pre_fwd(proj, tab, qg, kg):
    S = proj.shape[0]
    tm = 256

    def body(p_ref, tab_ref, qg_ref, kg_ref, g_ref, q_ref, k_ref, v_ref):
        Ap, Bp, Cp = _tab(tab_ref, 1)
        G = g_ref[...]
        lo = _head_mask((tm, LANES), 0)
        for c in range(8):
            sl = slice(c * 128, (c + 1) * 128)
            q_ref[:, sl] = _rope(_hn_fwd(p_ref[:, sl], qg_ref[...], G), Ap, Bp, Cp, 8).astype(BF16)
        for c in range(2):
            kn = _rope(_hn_fwd(p_ref[:, 1024 + c * 128:1024 + (c + 1) * 128], kg_ref[...], G), Ap, Bp, Cp, 8)
            vv = p_ref[:, 1280 + c * 128:1280 + (c + 1) * 128]
            for t, ref in ((kn, k_ref), (vv, v_ref)):
                sw = _roll(t, HEAD)
                ref[:, (2 * c) * 128:(2 * c + 1) * 128] = jnp.where(lo, t, sw).astype(BF16)
                ref[:, (2 * c + 1) * 128:(2 * c + 2) * 128] = jnp.where(lo, sw, t).astype(BF16)

    row = lambda w: pl.BlockSpec((tm, w), lambda i: (i, 0))
    vec = pl.BlockSpec((1, LANES), lambda i: (0, 0))
    return _pc(
        body, name="swa_pre_fwd", grid=(S // tm,),
        in_specs=[row(1536), row(768), vec, vec, pl.BlockSpec((LANES, LANES), lambda i: (0, 0))],
        out_specs=[row(1024), row(512), row(512)],
        out_shape=[jax.ShapeDtypeStruct((S, w), BF16) for w in (1024, 512, 512)],
        compiler_params=_params(("parallel",)),
    )(proj, tab, qg, kg, _group_matrix())


def _swa_pre_bwd(proj, tab, qg, kg, dq, dk, dv):
    S = proj.shape[0]
    tm = 256

    def body(p_ref, tab_ref, qg_ref, kg_ref, g_ref, dq_ref, dk_ref, dv_ref, dp_ref, db_ref, dqg_ref, dkg_ref):
        Ap, Bp, Cp = _tab(tab_ref, 1)
        G = g_ref[...]
        lo = _head_mask((tm, LANES), 0)

        @pl.when(pl.program_id(0) == 0)
        def _():
            db_ref[...] = jnp.zeros_like(db_ref)
            dqg_ref[...] = jnp.zeros_like(dqg_ref)
            dkg_ref[...] = jnp.zeros_like(dkg_ref)

        accq = jnp.zeros((1, LANES), F32)
        acck = jnp.zeros((1, LANES), F32)
        for c in range(8):
            sl = slice(c * 128, (c + 1) * 128)
            dx, dg = _hn_bwd(p_ref[:, sl], qg_ref[...], _rope_t(dq_ref[:, sl], Ap, Bp, Cp, 8), G)
            dp_ref[:, sl] = dx.astype(BF16)
            db_ref[:, sl] += jnp.sum(dx, axis=0, keepdims=True)
            accq = accq + dg
        for c in range(2):
            folded = []
            for ref in (dk_ref, dv_ref):
                a = ref[:, (2 * c) * 128:(2 * c + 1) * 128]
                b = ref[:, (2 * c + 1) * 128:(2 * c + 2) * 128]
                folded.append(jnp.where(lo, a + _roll(a, HEAD), b + _roll(b, HEAD)))
            ks = slice(1024 + c * 128, 1024 + (c + 1) * 128)
            dx, dg = _hn_bwd(p_ref[:, ks], kg_ref[...], _rope_t(folded[0], Ap, Bp, Cp, 8), G)
            dp_ref[:, ks] = dx.astype(BF16)
            db_ref[:, ks] += jnp.sum(dx, axis=0, keepdims=True)
            acck = acck + dg
            vs = slice(1280 + c * 128, 1280 + (c + 1) * 128)
            dp_ref[:, vs] = folded[1].astype(BF16)
            db_ref[:, vs] += jnp.sum(folded[1], axis=0, keepdims=True)
        dqg_ref[...] += _fold_halves(accq)
        dkg_ref[...] += _fold_halves(acck)

    row = lambda w: pl.BlockSpec((tm, w), lambda i: (i, 0))
    vec = pl.BlockSpec((1, LANES), lambda i: (0, 0))
    return _pc(
        body, name="swa_pre_bwd", grid=(S // tm,),
        in_specs=[row(1536), row(768), vec, vec, pl.BlockSpec((LANES, LANES), lambda i: (0, 0)),
                  row(1024), row(512), row(512)],
        out_specs=[row(1536), pl.BlockSpec((1, 1536), lambda i: (0, 0)), vec, vec],
        out_shape=[jax.ShapeDtypeStruct((S, 1536), BF16), jax.ShapeDtypeStruct((1, 1536), F32),
                   jax.ShapeDtypeStruct((1, LANES), F32), jax.ShapeDtypeStruct((1, LANES), F32)],
        compiler_params=_params(("arbitrary",)),
    )(proj, tab, qg, kg, _group_matrix(), dq, dk, dv)


def _relu2_of(u):
    r = jnp.maximum(u.astype(F32), 0.0)
    return r * r


def _drelu2(acc, u):
    return (acc * 2.0 * jnp.maximum(u.astype(F32), 0.0),)


def _add(acc, res):
    return (acc + res,)


def _add_norm_in(res, g):
    def epilogue(acc, r, gv):
        xn = acc + r
        return xn, xn * lax.rsqrt(jnp.mean(xn * xn, axis=-1, keepdims=True) + EPS) * gv

    return dict(outs=[F32, BF16], epilogue=epilogue, extras=[(res, "mn"), (g.reshape(1, D_MODEL), "n")])


_T = dict(tm=1024, tn=1024, tk=1024)


def _rms_bwd_in(x, g, dres):
    def epilogue(dh, xv, gv, dr):
        r = lax.rsqrt(jnp.mean(xv * xv, axis=-1, keepdims=True) + EPS)
        t = dh * gv
        dx = dr + r * t - xv * (r * r * r) * jnp.mean(xv * t, axis=-1, keepdims=True)
        return dx, dx, jnp.sum(dh * xv * r, axis=0, keepdims=True)

    return dict(outs=[F32, BF16, ("colsum",)], epilogue=epilogue,
                extras=[(x, "mn"), (g.reshape(1, D_MODEL), "n"), (dres, "mn")])


def _delta_in(o, col0):
    width = D_MODEL - col0

    def epilogue(do, ov, G):
        parts = [_gmean(do[:, col0 + c * 128:col0 + (c + 1) * 128] * ov[:, c * 128:(c + 1) * 128], G) * float(HEAD)
                 for c in range(width // LANES)]
        return do, jnp.concatenate(parts, axis=1)

    return dict(outs=[F32, (F32, width)], epilogue=epilogue, extras=[(o, width), (_group_matrix(), "full")])


def _loss_in(res, target):
    def epilogue(acc, r, t):
        e = acc + r - t
        dy = e * (1.0 / D_MODEL)
        return dy, dy, jnp.sum(e * e, axis=0, keepdims=True) * (0.5 / D_MODEL)

    return dict(outs=[F32, BF16, ("colsum",)], epilogue=epilogue, extras=[(res, "mn"), (target, "mn")])


def _mlp_fwd(h, wts, layer, tag, tail):
    u = _matmul(h, wts, dims="nn", **_T, outs=[BF16], b_cs=True, b_row0=layer, name=f"mlp_up{tag}")
    out = _matmul(u, wts, dims="nn", **_T, a_pro=_relu2_of, b_rs=1024, b_row0=2 + layer, name=f"mlp_down{tag}", **tail)
    return out, (h, u)


def _mlp_bwd(x, g, wts, layer, saved, dy, dyb, tag):
    h, u = saved
    du = _matmul(dyb, wts, dims="nt", **_T, outs=[BF16], epilogue=_drelu2, extras=[(u, "mn")], b_rs=1024,
                 b_row0=2 + layer, name=f"mlp_du{tag}")
    dw_dn = _matmul(u, dyb, dims="tn", **_T, outs=[F32], a_pro=_relu2_of, name=f"mlp_dwdown{tag}")
    dw_up = _matmul(h, du, dims="tn", **_T, outs=[F32], o_cs=N_CHIPS, name=f"mlp_dwup{tag}")
    dx, dxb, dg = _matmul(du, wts, dims="nt", **_T, b_cs=True, b_row0=layer, b_rows=1024, name=f"mlp_dh{tag}",
                          vmem=VMEM_LIMIT_WIDE, **_rms_bwd_in(x, g, dy))
    return dx, dxb, dg, dw_up, dw_dn


def _local_step(x, pos_col, target, first_of, rest_begin, rest_of, P, red):
    S = x.shape[0]
    tab = _tables(pos_col)
    tile2 = lambda g: jnp.tile(g.reshape(1, HEAD), (1, 2))
    dqg, dkg = tile2(P["dil_q_gain"]), tile2(P["dil_k_gain"])
    sqg, skg = tile2(P["swa_q_gain"]), tile2(P["swa_k_gain"])
    gn = P["ret_gn_gain"].reshape(1, 512)
    sink_b = jnp.repeat(P["swa_sinks"].reshape(16), HEAD).reshape(1, 1024)

    h0 = _rms_fwd(x, P["norm_mix"][0], "rms_mix_fwd0")
    W = first_of(h0)
    proj = _matmul(h0, W["hyb_w_in"], dims="nn", tm=1024, tn=768, tk=1024, outs=[F32], b_cs=True, name="hyb_in")
    rq, rk, rv, dq, dk, dv = _even_pre_fwd(proj, tab, dqg, dkg)
    ro, states = _ret_fwd(rq, rk, rv)
    dil = [(w // r, r) for w, r in DIL_PATTERNS]
    da, dlse = _band_fwd(dq, dk, dv, patterns=dil, nq=1, name="dil_fwd")
    mixed = rest_begin(_even_post_fwd(ro, proj, gn, da))
    x1, h1 = _matmul(mixed, W["hyb_w_out"], dims="nn", **_T, name="hyb_out", **_add_norm_in(x, P["norm_mlp"][0]))
    rest, bias = rest_of(x1)
    W = {**W, **rest}
    (x2, h2), mlp0 = _mlp_fwd(h1, W["packed"], 0, "0", _add_norm_in(x1, P["norm_mix"][1]))

    proj2 = _matmul(h2, W["swa_w_qkv"], dims="nn", tm=1024, tn=384, tk=1024, outs=[F32], b_cs=True,
                    epilogue=_add, extras=[(bias.reshape(1, 1536), "n")], name="swa_qkv")
    sq, sk, sv = _swa_pre_fwd(proj2, tab, sqg, skg)
    swa = [(SWA_DIST, 1)]
    so, slse, so_b = _band_fwd(sq, sk, sv, patterns=swa, nq=2, name="swa_fwd", sinks=sink_b, want_bf16=True)
    x3, h3 = _matmul(so_b, W["swa_w_out"], dims="nn", **_T, name="swa_out", **_add_norm_in(x2, P["norm_mlp"][1]))
    (dy, dyb, loss_cols), mlp1 = _mlp_fwd(h3, W["packed"], 1, "1", _loss_in(x3, target))
    loss = jnp.broadcast_to(jnp.sum(loss_cols), (1, LANES))

    gw, gp = {}, {}
    dx3, dx3b, dg_mlp1, gw["mlp_w_up1"], gw["mlp_w_down1"] = _mlp_bwd(x3, P["norm_mlp"][1], W["packed"], 1, mlp1, dy, dyb, "1")
    dx3b = red.begin("mlp1", {n: (gw[n], 1024) for n in ("mlp_w_up1", "mlp_w_down1")}, dx3b)
    gw["swa_w_out"] = _matmul(so_b, dx3b, dims="tn", **_T, outs=[F32], name="swa_dwout")
    dso, sdelta = _matmul(dx3b, W["swa_w_out"], dims="nt", tm=512, tn=1024, tk=1024, name="swa_do", **_delta_in(so, 0))
    dsq, dsk, dsv, dsink = _band_bwd(sq, sk, sv, slse, sdelta, dso, patterns=swa, nq=2, name="swa_bwd", sinks=sink_b)
    dproj2, gp["swa_b_qkv"], gp["swa_q_gain"], gp["swa_k_gain"] = _swa_pre_bwd(proj2, tab, sqg, skg, dsq, dsk, dsv)
    gp["swa_sinks"] = dsink
    gw["swa_w_qkv"] = _matmul(h2, dproj2, dims="tn", tm=1024, tn=384, tk=1024, outs=[F32], o_cs=N_CHIPS, name="swa_dwqkv")
    dx2, dx2b, dg_mix1 = _matmul(dproj2, W["swa_w_qkv"], dims="nt", tm=1024, tn=1024, tk=384, b_cs=True, name="swa_dh",
                                 vmem=VMEM_LIMIT_WIDE, **_rms_bwd_in(x2, P["norm_mix"][1], dx3))
    dx2b = red.begin("swa", {"swa_w_qkv": (gw["swa_w_qkv"], 1024), "swa_w_out": (gw["swa_w_out"], 256)}, dx2b)
    dx2b = red.advance("mlp1", dx2b, dx2b)

    dx1, dx1b, dg_mlp0, gw["mlp_w_up0"], gw["mlp_w_down0"] = _mlp_bwd(x1, P["norm_mlp"][0], W["packed"], 0, mlp0, dx2, dx2b, "0")
    gw["hyb_w_out"] = _matmul(mixed, dx1b, dims="tn", **_T, outs=[F32], name="hyb_dwout")
    dx1b = red.begin("mlp0", {"mlp_w_up0": (gw["mlp_w_up0"], 1024), "mlp_w_down0": (gw["mlp_w_down0"], 1024),
                              "hyb_w_out": (gw["hyb_w_out"], 256)}, dx1b)
    dx1b = red.advance("swa", dx1b, dx1b)
    red.finish("mlp1", dx1b)
    dmixed, ddelta = _matmul(dx1b, W["hyb_w_out"], dims="nt", tm=512, tn=1024, tk=1024, name="hyb_dmixed", **_delta_in(da, 512))
    dro, drg, gp["ret_gn_gain"] = _even_post_bwd(ro, proj, gn, dmixed)
    drq, drk, drv = _ret_bwd(rq, rk, rv, states, dro)
    ddq, ddk, ddv = _band_bwd(dq, dk, dv, dlse, ddelta, dmixed, patterns=dil, nq=1, name="dil_bwd", do_col0=4)
    ddq = red.advance("mlp0", ddq, ddq)
    red.finish("swa", ddq)
    dproj, gp["dil_q_gain"], gp["dil_k_gain"] = _even_pre_bwd(proj, tab, dqg, dkg, drq, drk, drv, drg, [ddq], [ddk], [ddv])
    gw["hyb_w_in"] = _matmul(h0, dproj, dims="tn", tm=1024, tn=768, tk=1024, outs=[F32], o_cs=N_CHIPS, name="hyb_dwin")
    dproj = red.begin("win", {"hyb_w_in": (gw["hyb_w_in"], 1024)}, dproj)
    grad_x, _, dg_mix0 = _matmul(dproj, W["hyb_w_in"], dims="nt", tm=1024, tn=1024, tk=768, b_cs=True, name="hyb_dh",
                                 vmem=VMEM_LIMIT_WIDE, **_rms_bwd_in(x, P["norm_mix"][0], dx1))
    red.finish("mlp0", grad_x)
    gp["norm_mix"] = jnp.concatenate([dg_mix0, dg_mix1], axis=0)
    gp["norm_mlp"] = jnp.concatenate([dg_mlp0, dg_mlp1], axis=0)
    return loss, grad_x, gp


HBM = pl.BlockSpec(memory_space=pltpu.HBM)


def _place():
    x, y, c = lax.axis_index("x"), lax.axis_index("y"), lax.axis_index("c")
    chips = [(1 - x, y), (x, 1 - y), (1 - x, 1 - y)]
    return x, y, c, chips


SEM = pl.BlockSpec(memory_space=pltpu.SEMAPHORE)
EFFECT = pltpu.SideEffectType.DATAFLOW_SIDE_EFFECTING


def _half_block(ref, chip, core):
    rh = ref.shape[1] // 2
    return ref.at[2 * chip[0] + chip[1], pl.ds(core * rh, rh), :]


def _gather_start(buf, ride, name):
    def body(b_ref, ride_ref, s0, s1, s2, r0, r1, r2, b_out, ride_out):
        x, y, c, chips = _place()
        for chip, s, r in zip(chips, (s0, s1, s2), (r0, r1, r2)):
            mine = _half_block(b_ref, (x, y), c)
            pltpu.make_async_remote_copy(src_ref=mine, dst_ref=mine, send_sem=s, recv_sem=r,
                                         device_id=(*chip, c), device_id_type=MESH).start()

    sem = pltpu.SemaphoreType.DMA(())
    return _pc(
        body, name=name,
        out_shape=(sem,) * 6 + (pltpu.HBM(buf.shape, buf.dtype), pltpu.HBM(ride.shape, ride.dtype)),
        in_specs=(HBM, HBM), out_specs=(SEM,) * 6 + (HBM, HBM), input_output_aliases={0: 6, 1: 7},
        compiler_params=pltpu.CompilerParams(has_side_effects=EFFECT),
    )(pltpu.with_memory_space_constraint(buf, pltpu.HBM), pltpu.with_memory_space_constraint(ride, pltpu.HBM))


def _gather_wait(buf, sems, after, name):
    def body(b_ref, s0, s1, s2, r0, r1, r2, after_ref, b_out):
        x, y, c, chips = _place()
        for chip, s, r in zip(chips, (s0, s1, s2), (r0, r1, r2)):
            cp = pltpu.make_async_remote_copy(src_ref=_half_block(b_ref, (x, y), c), dst_ref=_half_block(b_ref, chip, c),
                                              send_sem=s, recv_sem=r, device_id=(*chip, c), device_id_type=MESH)
            cp.wait_send()
            cp.wait_recv()

    return _pc(
        body, name=name, out_shape=pltpu.HBM(buf.shape, buf.dtype),
        in_specs=(HBM,) + (SEM,) * 6 + (pl.BlockSpec(memory_space=pl.ANY),), out_specs=HBM, input_output_aliases={0: 0},
        compiler_params=pltpu.CompilerParams(has_side_effects=EFFECT),
    )(buf, *sems, after)


def _gather_handover(buf, name):
    def body(b_ref, out_ref, send_sems, recv_sems):
        x, y, c, chips = _place()
        cps = []
        for k, chip in enumerate(chips):
            mine = _half_block(b_ref, chip, c)
            cps.append(pltpu.make_async_remote_copy(src_ref=mine, dst_ref=mine, send_sem=send_sems.at[k],
                                                    recv_sem=recv_sems.at[k], device_id=(x, y, 1 - c), device_id_type=MESH))
        for cp in cps:
            cp.start()
        for k, chip in enumerate(chips):
            theirs = _half_block(b_ref, chip, 1 - c)
            pltpu.make_async_remote_copy(src_ref=theirs, dst_ref=theirs, send_sem=send_sems.at[k], recv_sem=recv_sems.at[k],
                                         device_id=(x, y, 1 - c), device_id_type=MESH).wait_recv()
        for cp in cps:
            cp.wait_send()

    return _pc(
        body, name=name, in_specs=[HBM], out_specs=HBM,
        out_shape=jax.ShapeDtypeStruct(buf.shape, buf.dtype), input_output_aliases={0: 0},
        scratch_shapes=[pltpu.SemaphoreType.DMA((3,)), pltpu.SemaphoreType.DMA((3,))],
    )(buf)


def _pair_sum(t, l, place, name):
    _, r, cols = t.shape
    rh = r // 2
    tr = min(rh, 256)
    nr = rh // tr

    def body(pl_ref, t_ref, l_ref, o_ref):
        o_ref[...] = (t_ref[...] + l_ref[...]).astype(BF16)

    other = lambda s, p: s + jnp.where(s >= p[0], 1, 0)
    return _pc(
        body, name=name,
        grid_spec=pltpu.PrefetchScalarGridSpec(
            num_scalar_prefetch=1, grid=(N_CHIPS - 1, nr),
            in_specs=[pl.BlockSpec((None, tr, cols), lambda s, i, p: (other(s, p), p[1] * nr + i, 0)),
                      pl.BlockSpec((None, tr, cols), lambda s, i, p: (other(s, p), i, 0))],
            out_specs=pl.BlockSpec((None, tr, cols), lambda s, i, p: (other(s, p), i, 0))),
        out_shape=jax.ShapeDtypeStruct((N_CHIPS, rh, cols), BF16),
        compiler_params=_params(("parallel", "parallel")),
    )(place, t, l)


def _final_sum(t, l, rcv, place, name, layer=0, layers=1, into=None):
    _, r, cols = t.shape
    rh = r // 2
    tr = min(rh, 256)
    nr = rh // tr

    def body(pl_ref, t_ref, l_ref, r_ref, *rest):
        acc = t_ref[...] + l_ref[...]
        for k in range(3):
            acc = acc + r_ref[k].astype(F32)
        rest[-1][...] = acc

    in_specs = [pl.BlockSpec((None, tr, cols), lambda i, p: (p[0], p[1] * nr + i, 0)),
                pl.BlockSpec((None, tr, cols), lambda i, p: (p[0], i, 0)),
                pl.BlockSpec((3, tr, cols), lambda i, p: (0, i, 0))]
    args = [place, t, l, rcv]
    aliases = {}
    if into is not None:
        in_specs.append(pl.BlockSpec(memory_space=pl.ANY))
        args.append(into)
        aliases = {4: 0}
    return _pc(
        body, name=name,
        grid_spec=pltpu.PrefetchScalarGridSpec(
            num_scalar_prefetch=1, grid=(nr,), in_specs=in_specs,
            out_specs=pl.BlockSpec((tr, cols), lambda i, p: (2 * nr * layer + p[1] * nr + i, 0))),
        out_shape=jax.ShapeDtypeStruct((layers * r, cols), F32), input_output_aliases=aliases,
        compiler_params=_params(("parallel",)),
    )(*args)


def _share_halves(hs, name):
    nt = len(hs)
    n = sum(layers for _, layers in hs)

    def body(*refs):
        h_refs, send_sems, recv_sems = refs[:nt], refs[-2], refs[-1]
        x, y, c, _ = _place()
        cps = []
        for k, (_, layers) in enumerate(hs):
            rh = h_refs[k].shape[0] // (2 * layers)
            for layer in range(layers):
                half = h_refs[k].at[pl.ds((2 * layer + c) * rh, rh), :]
                cps.append(pltpu.make_async_remote_copy(
                    src_ref=half, dst_ref=half, send_sem=send_sems.at[len(cps)], recv_sem=recv_sems.at[len(cps)],
                    device_id=(x, y, 1 - c), device_id_type=MESH))
        for cp in cps:
            cp.start()
        for cp in cps:
            cp.wait()

    return _pc(
        body, name=name, in_specs=[HBM] * nt, out_specs=[HBM] * nt,
        out_shape=[jax.ShapeDtypeStruct(h.shape, F32) for h, _ in hs],
        input_output_aliases={k: k for k in range(nt)},
        scratch_shapes=[pltpu.SemaphoreType.DMA((n,)), pltpu.SemaphoreType.DMA((n,))],
    )(*[h for h, _ in hs])


def _split_start(name, bufs, ride, n, copies_of):
    nb = len(bufs)

    def body(*refs):
        sems = refs[nb + 1:nb + 1 + 2 * n]
        for cp in copies_of(refs[:nb], sems[:n], sems[n:]):
            (cp[0] if isinstance(cp, tuple) else cp).start()

    outs = _pc(
        body, name=name,
        out_shape=(pltpu.SemaphoreType.DMA(()),) * (2 * n) + tuple(pltpu.HBM(b.shape, b.dtype) for b in bufs)
        + (pltpu.HBM(ride.shape, ride.dtype),),
        in_specs=(HBM,) * (nb + 1), out_specs=(SEM,) * (2 * n) + (HBM,) * (nb + 1),
        input_output_aliases={k: 2 * n + k for k in range(nb + 1)},
        compiler_params=pltpu.CompilerParams(has_side_effects=EFFECT),
    )(*[pltpu.with_memory_space_constraint(b, pltpu.HBM) for b in bufs], pltpu.with_memory_space_constraint(ride, pltpu.HBM))
    return list(outs[:2 * n]), list(outs[2 * n:2 * n + nb]), outs[-1]


def _split_wait(name, bufs, sems, after, n, copies_of):
    nb = len(bufs)

    def body(*refs):
        s = refs[nb:nb + 2 * n]
        for cp in copies_of(refs[:nb], s[:n], s[n:]):
            sent, landed = cp if isinstance(cp, tuple) else (cp, cp)
            sent.wait_send()
            landed.wait_recv()

    outs = _pc(
        body, name=name, out_shape=tuple(pltpu.HBM(b.shape, b.dtype) for b in bufs),
        in_specs=(HBM,) * nb + (SEM,) * (2 * n) + (pl.BlockSpec(memory_space=pl.ANY),), out_specs=(HBM,) * nb,
        input_output_aliases={k: k for k in range(nb)},
        compiler_params=pltpu.CompilerParams(has_side_effects=EFFECT),
    )(*bufs, *sems, after)
    return list(outs)


def _handover_copies(refs, send, recv):
    x, y, c, chips = _place()
    cps = []
    for k, chip in enumerate(chips):
        mine, theirs = _half_block(refs[0], chip, c), _half_block(refs[0], chip, 1 - c)
        desc = lambda blk: pltpu.make_async_remote_copy(src_ref=blk, dst_ref=blk, send_sem=send[k], recv_sem=recv[k],
                                                        device_id=(x, y, 1 - c), device_id_type=MESH)
        cps.append((desc(mine), desc(theirs)))
    return cps


def _swap_copies(nt):
    def copies_of(refs, send, recv):
        x, y, c, _ = _place()
        cps = []
        for k in range(nt):
            rh = refs[k].shape[1] // 2
            cps.append(pltpu.make_async_remote_copy(
                src_ref=refs[k].at[:, pl.ds((1 - c) * rh, rh), :], dst_ref=refs[nt + k],
                send_sem=send[k], recv_sem=recv[k], device_id=(x, y, 1 - c), device_id_type=MESH))
        return cps
    return copies_of


def _exchange_copies(nt):
    def copies_of(refs, send, recv):
        x, y, c, chips = _place()
        cps = []
        for t in range(nt):
            for k, chip in enumerate(chips):
                cps.append(pltpu.make_async_remote_copy(
                    src_ref=refs[t].at[2 * chip[0] + chip[1]], dst_ref=refs[nt + t].at[k],
                    send_sem=send[3 * t + k], recv_sem=recv[3 * t + k], device_id=(*chip, c), device_id_type=MESH))
        return cps
    return copies_of


class _StagedReduce:
    def __init__(self, place):
        self.place = place
        self.groups = {}
        self.halves = {}

    @staticmethod
    def slab(t, r):
        return t.reshape(N_CHIPS, r, t.size // (N_CHIPS * r))

    def begin(self, g, grads, ride):
        names = list(grads)
        ts = [self.slab(t, r) for t, r in grads.values()]
        lands = [lax.empty((N_CHIPS, t.shape[1] // 2, t.shape[2]), F32) for t in ts]
        sems, bufs, ride = _split_start(f"grad_swap_start_{g}", ts + lands, ride, len(ts), _swap_copies(len(ts)))
        self.groups[g] = dict(names=names, bufs=bufs, sems=sems)
        return ride

    def advance(self, g, after, ride):
        st = self.groups[g]
        nt = len(st["names"])
        bufs = _split_wait(f"grad_swap_wait_{g}", st["bufs"], st["sems"], after, nt, _swap_copies(nt))
        st["ts"], st["ls"] = bufs[:nt], bufs[nt:]
        ps = [_pair_sum(t, l, self.place, f"pair_sum_{n}") for t, l, n in zip(st["ts"], st["ls"], st["names"])]
        lands = [lax.empty((3,) + p.shape[1:], BF16) for p in ps]
        st["sems"], st["bufs"], ride = _split_start(f"grad_exchange_start_{g}", ps + lands, ride, 3 * nt, _exchange_copies(nt))
        return ride

    def finish(self, g, after):
        st = self.groups[g]
        nt = len(st["names"])
        bufs = _split_wait(f"grad_exchange_wait_{g}", st["bufs"], st["sems"], after, 3 * nt, _exchange_copies(nt))
        for t, l, r, n in zip(st["ts"], st["ls"], bufs[nt:], st["names"]):
            if n[-1] in "01":
                self.halves[n[:-1]] = _final_sum(t, l, r, self.place, f"final_sum_{n}", layer=int(n[-1]), layers=2,
                                                 into=self.halves.get(n[:-1]))
            else:
                self.halves[n] = _final_sum(t, l, r, self.place, f"final_sum_{n}")


def _allgather_small(v):
    rows = v.shape[0]

    def body(v_ref, out_ref, send_sems, recv_sems):
        x, y, c, _ = _place()
        me = 4 * x + 2 * y + c
        out_ref[me] = v_ref[...]
        cps = []
        for k in range(1, 8):
            fx, fy, fc = (k >> 2) & 1, (k >> 1) & 1, k & 1
            to = (1 - x if fx else x, 1 - y if fy else y, 1 - c if fc else c)
            cps.append(pltpu.make_async_remote_copy(
                src_ref=v_ref, dst_ref=out_ref.at[me], send_sem=send_sems.at[k - 1], recv_sem=recv_sems.at[k - 1],
                device_id=to, device_id_type=MESH))
        for cp in cps:
            cp.start()
        for cp in cps:
            cp.wait()

    return _pc(
        body, name="allgather_small",
        in_specs=[pl.BlockSpec(memory_space=pltpu.VMEM)], out_specs=pl.BlockSpec(memory_space=pltpu.VMEM),
        out_shape=jax.ShapeDtypeStruct((8, rows, LANES), F32),
        scratch_shapes=[pltpu.SemaphoreType.DMA((7,)), pltpu.SemaphoreType.DMA((7,))],
    )(v)


def _adamw_math(w, g, m, v):
    m = ADAM_B1 * m + (1.0 - ADAM_B1) * g
    v = ADAM_B2 * v + (1.0 - ADAM_B2) * (g * g)
    m_hat = m / (1.0 - ADAM_B1 ** ADAM_STEP)
    v_hat = v / (1.0 - ADAM_B2 ** ADAM_STEP)
    return -ADAM_LR * (m_hat / (jnp.sqrt(v_hat) + ADAM_EPS) + ADAM_WD * w), m, v


def _adamw(w, g, m, v, name):
    r, cols = w.shape
    tr = min(r, 256)

    def body(w_ref, g_ref, m_ref, v_ref, go_ref, d_ref, mo_ref, vo_ref):
        gv = g_ref[...]
        d, mn, vn = _adamw_math(w_ref[...], gv, m_ref[...], v_ref[...])
        go_ref[...] = gv
        d_ref[...] = d
        mo_ref[...] = mn
        vo_ref[...] = vn

    row = pl.BlockSpec((tr, cols), lambda i: (i, 0))
    return _pc(
        body, name=name, grid=(r // tr,), in_specs=[row] * 4, out_specs=[row] * 4,
        out_shape=[jax.ShapeDtypeStruct((r, cols), F32)] * 4,
        compiler_params=_params(("parallel",)),
    )(w, g, m, v)


def _adamw_small(w, gathered, m, v):
    rows = w.shape[0]

    def body(w_ref, g_ref, m_ref, v_ref, go_ref, d_ref, mo_ref, vo_ref):
        g = g_ref[0]
        for k in range(1, 8):
            g = g + g_ref[k]
        d, mn, vn = _adamw_math(w_ref[...], g, m_ref[...], v_ref[...])
        go_ref[...] = g
        d_ref[...] = d
        mo_ref[...] = mn
        vo_ref[...] = vn

    return _pc(
        body, name="adamw_small",
        out_shape=[jax.ShapeDtypeStruct((rows, LANES), F32)] * 4,
    )(w, gathered, m, v)


_BIAS_ROWS = 32


def _own_slot(flat, chip):
    return lax.dynamic_update_slice(lax.empty((N_CHIPS,) + flat.shape, flat.dtype), flat[None], (chip, 0, 0))


def _pack_first(hyb_w_in, hyb_w_out):
    return jnp.concatenate([t.astype(BF16).reshape(-1, 1024) for t in (hyb_w_in, hyb_w_out)], axis=0)


def _unpack_first(g):
    return {"hyb_w_in": g[:, 0:768, :].reshape(N_CHIPS, 1024, 768), "hyb_w_out": g[:, 768:1024, :].reshape(1024, 1024)}


def _pack_rest(mlp_w_up, mlp_w_down, swa_w_qkv, swa_w_out, swa_b_qkv):
    parts = [t.astype(BF16).reshape(-1, 1024) for t in (mlp_w_up, mlp_w_down, swa_w_qkv, swa_w_out)]
    bias = lax.bitcast_convert_type(swa_b_qkv.reshape(384), BF16).reshape(1, 768)
    bias = jnp.pad(bias, ((0, _BIAS_ROWS - 1), (0, 256)))
    return jnp.concatenate(parts + [bias], axis=0)


def _unpack_rest(g):
    W = {
        "packed": g,
        "swa_w_qkv": g[:, 4096:4480, :].reshape(N_CHIPS, 1024, 384),
        "swa_w_out": g[:, 4480:4736, :].reshape(1024, 1024),
    }
    bias = lax.bitcast_convert_type(g[:, 4736, :768].reshape(N_CHIPS, 384, 2), F32).reshape(1536)
    return W, bias


_SMALL = (("norm_mix", 16), ("norm_mlp", 16), ("ret_gn_gain", 4), ("dil_q_gain", 1), ("dil_k_gain", 1),
          ("swa_b_qkv", 12), ("swa_q_gain", 1), ("swa_k_gain", 1), ("swa_sinks", 1), ("loss", 1))
_SUBLANES = 8


def _slot(r):
    return -(-r // _SUBLANES) * _SUBLANES


def _pack_small(d):
    return jnp.concatenate([jnp.pad(d[n].reshape(r, LANES), ((0, _slot(r) - r), (0, 0))) for n, r in _SMALL], axis=0)


def _unpack_small(p):
    out, o = {}, 0
    for n, r in _SMALL:
        out[n] = p[o:o + r]
        o += _slot(r)
    return out


def kernel(x, positions, norm_mix, norm_mlp, mlp_w_up, mlp_w_down, hyb_w_in, hyb_w_out, ret_gn_gain, dil_q_gain, dil_k_gain, swa_w_qkv, swa_b_qkv, swa_w_out, swa_q_gain, swa_k_gain, swa_sinks, loss_target, m_norm_mix, m_norm_mlp, m_mlp_w_up, m_mlp_w_down, m_hyb_w_in, m_hyb_w_out, m_ret_gn_gain, m_dil_q_gain, m_dil_k_gain, m_swa_w_qkv, m_swa_b_qkv, m_swa_w_out, m_swa_q_gain, m_swa_k_gain, m_swa_sinks, v_norm_mix, v_norm_mlp, v_mlp_w_up, v_mlp_w_down, v_hyb_w_in, v_hyb_w_out, v_ret_gn_gain, v_dil_q_gain, v_dil_k_gain, v_swa_w_qkv, v_swa_b_qkv, v_swa_w_out, v_swa_q_gain, v_swa_k_gain, v_swa_sinks):
    ax, ay, ac = lax.axis_index("x"), lax.axis_index("y"), lax.axis_index("c")
    chip = 2 * ax + ay
    place = jnp.stack([chip, ac]).astype(jnp.int32)
    S = x.shape[1]

    first = _own_slot(_pack_first(hyb_w_in[0], hyb_w_out[0]), chip)
    rest = _own_slot(_pack_rest(mlp_w_up, mlp_w_down, swa_w_qkv[0], swa_w_out[0], swa_b_qkv[0]), chip)
    *sems, first, pos_col = _gather_start(first, positions.reshape(S, 1), "allgather_first_start")
    flight = {}

    def first_of(after):
        g = _gather_handover(_gather_wait(first, sems, after, "allgather_first_wait"), "allgather_first_handover")
        *flight["sems"], flight["buf"], g = _gather_start(rest, g, "allgather_rest_start")
        return _unpack_first(g)

    def rest_begin(ride):
        buf = _gather_wait(flight["buf"], flight["sems"], ride, "allgather_rest_wait")
        flight["sems"], flight["bufs"], ride = _split_start("allgather_rest_handover_start", [buf], ride, 3, _handover_copies)
        return ride

    def rest_of(after):
        return _unpack_rest(_split_wait("allgather_rest_handover_wait", flight["bufs"], flight["sems"], after, 3,
                                        _handover_copies)[0])

    P = dict(norm_mix=norm_mix, norm_mlp=norm_mlp, ret_gn_gain=ret_gn_gain, dil_q_gain=dil_q_gain, dil_k_gain=dil_k_gain,
             swa_q_gain=swa_q_gain, swa_k_gain=swa_k_gain, swa_sinks=swa_sinks)

    red = _StagedReduce(place)
    loss_l, grad_x, gp = _local_step(x[0], pos_col, loss_target[0], first_of, rest_begin, rest_of, P, red)

    params = dict(mlp_w_up=(mlp_w_up, m_mlp_w_up, v_mlp_w_up), mlp_w_down=(mlp_w_down, m_mlp_w_down, v_mlp_w_down),
                  hyb_w_in=(hyb_w_in, m_hyb_w_in, v_hyb_w_in), hyb_w_out=(hyb_w_out, m_hyb_w_out, v_hyb_w_out),
                  swa_w_qkv=(swa_w_qkv, m_swa_w_qkv, v_swa_w_qkv), swa_w_out=(swa_w_out, m_swa_w_out, v_swa_w_out))
    big = {}

    def update(names, share_name):
        hs = [(red.halves[n], params[n][0].shape[0]) for n in names]
        for n, g in zip(names, _share_halves(hs, share_name)):
            rows = g.shape[0]
            w, m, v = (t.reshape(rows, -1) for t in params[n])
            big[n] = [t.reshape(params[n][0].shape) for t in _adamw(w, g, m, v, f"adamw_{n}")]

    names = ["mlp_w_up", "mlp_w_down", "hyb_w_out", "swa_w_qkv", "swa_w_out"]
    red.halves[names[0]] = red.advance("win", grad_x, red.halves[names[0]])
    update(names, "grad_share_halves")
    red.finish("win", big[names[-1]][1])
    update(["hyb_w_in"], "grad_share_last")

    gsm = dict(gp, loss=loss_l)
    gsm["swa_sinks"] = jnp.pad(gp["swa_sinks"].reshape(16, HEAD)[:, 0], (0, LANES - 16))
    gathered = _allgather_small(_pack_small(gsm))

    def small_pack(norm_mix, norm_mlp, gn, dq, dk, b, sq, sk, sinks):
        dup = lambda t: jnp.tile(t.reshape(1, HEAD), (1, 2))
        bias = lax.dynamic_update_slice(jnp.zeros((12, LANES), F32), b.reshape(3, LANES), (3 * chip, 0))
        return _pack_small(dict(norm_mix=norm_mix, norm_mlp=norm_mlp, ret_gn_gain=gn, dil_q_gain=dup(dq), dil_k_gain=dup(dk),
                                swa_b_qkv=bias, swa_q_gain=dup(sq), swa_k_gain=dup(sk),
                                swa_sinks=jnp.pad(sinks.reshape(16), (0, LANES - 16)), loss=jnp.zeros((1, LANES), F32)))

    pw = small_pack(norm_mix, norm_mlp, ret_gn_gain, dil_q_gain, dil_k_gain, swa_b_qkv, swa_q_gain, swa_k_gain, swa_sinks)
    pm = small_pack(m_norm_mix, m_norm_mlp, m_ret_gn_gain, m_dil_q_gain, m_dil_k_gain, m_swa_b_qkv, m_swa_q_gain, m_swa_k_gain, m_swa_sinks)
    pv = small_pack(v_norm_mix, v_norm_mlp, v_ret_gn_gain, v_dil_q_gain, v_dil_k_gain, v_swa_b_qkv, v_swa_q_gain, v_swa_k_gain, v_swa_sinks)
    small = [_unpack_small(t) for t in _adamw_small(pw, gathered, pm, pv)]

    def small_out(n, k):
        t = small[k][n]
        if n in ("norm_mix", "norm_mlp"):
            return t.reshape(2, D_MODEL)
        if n == "ret_gn_gain":
            return t.reshape(1, RET_HEADS, 128)
        if n == "swa_b_qkv":
            return lax.dynamic_slice(t, (3 * chip, 0), (3, LANES)).reshape(1, 384)
        if n == "swa_sinks":
            return t[0, :16].reshape(1, 16)
        return t[0, :HEAD].reshape(1, HEAD)

    order = ["norm_mix", "norm_mlp", "mlp_w_up", "mlp_w_down", "hyb_w_in", "hyb_w_out", "ret_gn_gain", "dil_q_gain",
             "dil_k_gain", "swa_w_qkv", "swa_b_qkv", "swa_w_out", "swa_q_gain", "swa_k_gain", "swa_sinks"]
    is_big = {"mlp_w_up", "mlp_w_down", "hyb_w_in", "hyb_w_out", "swa_w_qkv", "swa_w_out"}
    outs = [small[0]["loss"][0, 0], grad_x[None]]
    for k in range(4):
        outs += [big[n][k] if n in is_big else small_out(n, k) for n in order]
    return tuple(outs)
```

```python
import numpy as np
import jax
import jax.numpy as jnp
from jax import lax
from jax.experimental import pallas as pl
from jax.experimental.pallas import tpu as pltpu

F32, BF16 = jnp.float32, jnp.bfloat16
MESH = pl.DeviceIdType.MESH

LANES = 128
VMEM_LIMIT = 48 << 20
VMEM_LIMIT_WIDE = 60 << 20
D_MODEL = 1024
HEAD = 64
EPS = 1e-6
BLK = 128
RET_HEADS = 4
RET_THETA = 10000.0
ROPE_THETA = 500000.0
ROPE_DIMS = 16
DIL_PATTERNS = ((128, 1), (512, 4), (2048, 16))
SWA_DIST = 127
N_CHIPS = 4
ADAM_LR, ADAM_B1, ADAM_B2, ADAM_EPS, ADAM_WD, ADAM_STEP = 0.001, 0.9, 0.999, 1e-08, 0.01, 10

_LOG_GAMMA = [float(np.log1p(-np.exp2(np.float32(-5.0 - h)))) for h in range(RET_HEADS)]


def _pc(body, **kw):
    return pl.pallas_call(body, **kw)


def _params(sem, limit=VMEM_LIMIT):
    return pltpu.CompilerParams(dimension_semantics=sem, vmem_limit_bytes=limit)


def _matmul(a, b, *, dims, tm, tn, tk, outs, name, epilogue=None, extras=(), b_cs=False, b_rs=0, b_row0=0, b_rows=0,
            o_cs=0, a_pro=None, vmem=VMEM_LIMIT):
    if dims == "nn":
        M, K = a.shape
        N = b.shape[0] * b.shape[2] if b_cs else b.shape[1]
        a_spec = pl.BlockSpec((tm, tk), lambda i, j, k: (i, k))
        if b_cs:
            npt = b.shape[2] // tn
            b_spec = pl.BlockSpec((None, tk, tn), lambda i, j, k: (j // npt, k + b_row0, j % npt))
        elif b_rs:
            K, N, kps = b.shape[0] * b_rs, b.shape[2], b_rs // tk
            b_spec = pl.BlockSpec((None, tk, tn), lambda i, j, k: (k // kps, b_row0 + k % kps, j))
        else:
            b_spec = pl.BlockSpec((tk, tn), lambda i, j, k: (k, j))
        contract = (((1,), (0,)), ((), ()))
    elif dims == "nt":
        M, K = a.shape
        N = (b_rows or b.shape[1]) if b_cs else b.shape[0]
        a_spec = pl.BlockSpec((tm, tk), lambda i, j, k: (i, k))
        if b_cs:
            kpt = b.shape[2] // tk
            b_spec = pl.BlockSpec((None, tn, tk), lambda i, j, k: (k // kpt, j + b_row0, k % kpt))
        elif b_rs:
            N, jps = b.shape[0] * b_rs, b_rs // tn
            b_spec = pl.BlockSpec((None, tn, tk), lambda i, j, k: (j // jps, b_row0 + j % jps, k))
        else:
            b_spec = pl.BlockSpec((tn, tk), lambda i, j, k: (j, k))
        contract = (((1,), (1,)), ((), ()))
    else:
        K, M = a.shape
        N = b.shape[1]
        a_spec = pl.BlockSpec((tk, tm), lambda i, j, k: (k, i))
        b_spec = pl.BlockSpec((tk, tn), lambda i, j, k: (k, j))
        contract = (((0,), (0,)), ((), ()))
    assert M % tm == 0 and N % tn == 0 and K % tk == 0, (name, M, N, K, tm, tn, tk)
    nk = K // tk
    ex_specs = []
    for arr, kind in extras:
        if kind == "mn":
            ex_specs.append(pl.BlockSpec((tm, tn), lambda i, j, k: (i, j)))
        elif kind == "n":
            ex_specs.append(pl.BlockSpec((1, tn), lambda i, j, k: (0, j)))
        elif kind == "full":
            ex_specs.append(pl.BlockSpec(arr.shape, lambda i, j, k, nd=arr.ndim: (0,) * nd))
        else:
            ex_specs.append(pl.BlockSpec((tm, kind), lambda i, j, k: (i, 0)))
    if o_cs:
        n_sh = N // o_cs
        opt = n_sh // tn
        o_shape = (o_cs, M, n_sh)
        o_spec = pl.BlockSpec((None, tm, tn), lambda i, j, k: (j // opt, i, j % opt))
    else:
        o_shape = (M, N)
        o_spec = pl.BlockSpec((tm, tn), lambda i, j, k: (i, j))
    o_specs, o_shapes, summed = [], [], []
    for o in outs:
        if isinstance(o, tuple) and o[0] == "colsum":
            assert N == tn
            o_specs.append(pl.BlockSpec((1, tn), lambda i, j, k: (0, j)))
            o_shapes.append(jax.ShapeDtypeStruct((1, N), F32))
            summed.append(True)
        elif isinstance(o, tuple):
            o_specs.append(pl.BlockSpec((tm, o[1]), lambda i, j, k: (i, 0)))
            o_shapes.append(jax.ShapeDtypeStruct((M, o[1]), o[0]))
            summed.append(False)
        else:
            o_specs.append(o_spec)
            o_shapes.append(jax.ShapeDtypeStruct(o_shape, o))
            summed.append(False)
    n_ex, n_out = len(extras), len(outs)
    if epilogue is None:
        epilogue = lambda acc: (acc,)

    def body(a_ref, b_ref, *rest):
        ex, o_refs, acc = rest[:n_ex], rest[n_ex:n_ex + n_out], rest[-1]
        i, k = pl.program_id(0), pl.program_id(2)

        @pl.when(k == 0)
        def _():
            acc[...] = jnp.zeros_like(acc)

        av = a_ref[...] if a_pro is None else a_pro(a_ref[...])
        acc[...] += lax.dot_general(av.astype(BF16), b_ref[...].astype(BF16), contract, preferred_element_type=F32)

        @pl.when(k == nk - 1)
        def _():
            vals = epilogue(acc[...], *[e[...] for e in ex])
            for r, v, sm in zip(o_refs, vals, summed):
                if sm:
                    @pl.when(i == 0)
                    def _(r=r):
                        r[...] = jnp.zeros_like(r)

                    r[...] += v
                else:
                    r[...] = v.astype(r.dtype)

    res = _pc(
        body, name=name, grid=(M // tm, N // tn, nk),
        in_specs=[a_spec, b_spec] + ex_specs, out_specs=o_specs, out_shape=o_shapes,
        scratch_shapes=[pltpu.VMEM((tm, tn), F32)],
        compiler_params=_params(("arbitrary" if any(summed) else "parallel", "parallel", "arbitrary"), vmem),
    )(a, b, *[e for e, _ in extras])
    return res[0] if n_out == 1 else res


def _roll(x, s):
    return pltpu.roll(x, s % LANES, 1)


def _rope(x, A, B, C, half):
    return x * A + _roll(x, LANES - half) * B + _roll(x, half) * C


def _rope_t(g, A, B, C, half):
    return g * A + _roll(g * B, half) + _roll(g * C, LANES - half)


def _gmean(x, G):
    hi = x.astype(BF16)
    lo = (x - hi.astype(F32)).astype(BF16)
    Gb = G.astype(BF16)
    return jnp.dot(hi, Gb, preferred_element_type=F32) + jnp.dot(lo, Gb, preferred_element_type=F32)


def _head_mask(shape, half):
    lane = lax.broadcasted_iota(jnp.int32, shape, len(shape) - 1)
    return (lane >= HEAD) if half else (lane < HEAD)


def _group_matrix():
    i = np.arange(LANES)
    return jnp.asarray((i[:, None] // HEAD == i[None, :] // HEAD).astype(np.float32) / HEAD)


def _rope_inv():
    l = np.arange(LANES) % HEAD
    inv_r = np.power(np.float32(RET_THETA), -(l % 32).astype(np.float32) * np.float32(2.0 / HEAD))
    hp = ROPE_DIMS // 2
    inv_p = np.power(np.float32(ROPE_THETA), -(l % hp).astype(np.float32) * np.float32(2.0 / ROPE_DIMS))
    inv_p = np.where(l < ROPE_DIMS, inv_p, 0.0)
    return jnp.asarray(np.stack([inv_r, inv_p]).astype(np.float32))


def _tables(pos_col):
    S = pos_col.shape[0]
    tm = 512
    hp = ROPE_DIMS // 2

    def body(p_ref, inv_ref, o_ref):
        p = p_ref[...].astype(F32)
        lane = lax.broadcasted_iota(jnp.int32, (tm, LANES), 1) % HEAD
        ang = p * inv_ref[0:1, :]
        c, s = jnp.cos(ang), jnp.sin(ang)
        o_ref[:, 0:128] = c
        o_ref[:, 128:256] = jnp.where(lane < 32, -s, 0.0)
        o_ref[:, 256:384] = jnp.where(lane >= 32, s, 0.0)
        ang = p * inv_ref[1:2, :]
        c, s = jnp.cos(ang), jnp.sin(ang)
        o_ref[:, 384:512] = c
        o_ref[:, 512:640] = jnp.where(lane < hp, -s, 0.0)
        o_ref[:, 640:768] = jnp.where((lane >= hp) & (lane < ROPE_DIMS), s, 0.0)

    return _pc(
        body, name="rope_tables", grid=(S // tm,),
        in_specs=[pl.BlockSpec((tm, 1), lambda i: (i, 0)), pl.BlockSpec((2, LANES), lambda i: (0, 0))],
        out_specs=pl.BlockSpec((tm, 768), lambda i: (i, 0)),
        out_shape=jax.ShapeDtypeStruct((S, 768), F32),
        compiler_params=_params(("parallel",)),
    )(pos_col, _rope_inv())


def _tab(tab_ref, which):
    o = 384 * which
    return tab_ref[:, o:o + 128], tab_ref[:, o + 128:o + 256], tab_ref[:, o + 256:o + 384]


def _rms_fwd(x, g, name):
    S, Dm = x.shape
    tm = 512

    def body(x_ref, g_ref, h_ref):
        xv = x_ref[...]
        r = lax.rsqrt(jnp.mean(xv * xv, axis=-1, keepdims=True) + EPS)
        h_ref[...] = (xv * r * g_ref[...]).astype(BF16)

    return _pc(
        body, name=name, grid=(S // tm,),
        in_specs=[pl.BlockSpec((tm, Dm), lambda i: (i, 0)), pl.BlockSpec((1, Dm), lambda i: (0, 0))],
        out_specs=pl.BlockSpec((tm, Dm), lambda i: (i, 0)),
        out_shape=jax.ShapeDtypeStruct((S, Dm), BF16),
        compiler_params=_params(("parallel",)),
    )(x, g.reshape(1, Dm))


def _hn_fwd(x, gain, G):
    r = lax.rsqrt(_gmean(x * x, G) + EPS)
    return x * r * gain


def _hn_bwd(x, gain, dy, G):
    r = lax.rsqrt(_gmean(x * x, G) + EPS)
    t = dy * gain
    dx = r * t - x * (r * r * r) * _gmean(x * t, G)
    return dx, jnp.sum(dy * x * r, axis=0, keepdims=True)


def _fold_halves(v):
    return v + _roll(v, HEAD)


def _even_pre_fwd(proj, tab, qg, kg):
    S = proj.shape[0]
    tm = 512

    def body(p_ref, tab_ref, qg_ref, kg_ref, g_ref, rq_ref, rk_ref, rv_ref, dq_ref, dk_ref, dv_ref):
        Ar, Br, Cr = _tab(tab_ref, 0)
        Ap, Bp, Cp = _tab(tab_ref, 1)
        G = g_ref[...]
        for c in range(2):
            sl = slice(c * 128, (c + 1) * 128)
            rq_ref[:, sl] = _rope(p_ref[:, c * 128:(c + 1) * 128], Ar, Br, Cr, 32).astype(BF16)
            rk_ref[:, sl] = (_rope(p_ref[:, 256 + c * 128:256 + (c + 1) * 128], Ar, Br, Cr, 32) * 0.125).astype(BF16)
        rv_ref[...] = p_ref[:, 512:1024].astype(BF16)
        for c in range(4):
            sl = slice(c * 128, (c + 1) * 128)
            q = _hn_fwd(p_ref[:, 1536 + c * 128:1536 + (c + 1) * 128], qg_ref[...], G)
            dq_ref[:, sl] = _rope(q, Ap, Bp, Cp, 8).astype(BF16)
            k = _hn_fwd(p_ref[:, 2048 + c * 128:2048 + (c + 1) * 128], kg_ref[...], G)
            dk_ref[:, sl] = _rope(k, Ap, Bp, Cp, 8).astype(BF16)
        dv_ref[...] = p_ref[:, 2560:3072].astype(BF16)

    row = lambda w: pl.BlockSpec((tm, w), lambda i: (i, 0))
    vec = pl.BlockSpec((1, LANES), lambda i: (0, 0))
    return _pc(
        body, name="even_pre_fwd", grid=(S // tm,),
        in_specs=[row(3072), row(768), vec, vec, pl.BlockSpec((LANES, LANES), lambda i: (0, 0))],
        out_specs=[row(256), row(256), row(512), row(512), row(512), row(512)],
        out_shape=[jax.ShapeDtypeStruct((S, w), BF16) for w in (256, 256, 512, 512, 512, 512)],
        compiler_params=_params(("parallel",), VMEM_LIMIT_WIDE),
    )(proj, tab, qg, kg, _group_matrix())


def _even_pre_bwd(proj, tab, qg, kg, drq, drk, drv, drg, dqs, dks, dvs):
    S = proj.shape[0]
    tm = 512
    npat = len(dqs)

    def body(p_ref, tab_ref, qg_ref, kg_ref, g_ref, drq_ref, drk_ref, drv_ref, drg_ref, *rest):
        dq_refs, dk_refs, dv_refs = rest[:npat], rest[npat:2 * npat], rest[2 * npat:3 * npat]
        dp_ref, dqg_ref, dkg_ref = rest[3 * npat:]
        Ar, Br, Cr = _tab(tab_ref, 0)
        Ap, Bp, Cp = _tab(tab_ref, 1)
        G = g_ref[...]
        for c in range(2):
            sl = slice(c * 128, (c + 1) * 128)
            dp_ref[:, c * 128:(c + 1) * 128] = _rope_t(drq_ref[:, sl], Ar, Br, Cr, 32).astype(BF16)
            dp_ref[:, 256 + c * 128:256 + (c + 1) * 128] = _rope_t(drk_ref[:, sl] * 0.125, Ar, Br, Cr, 32).astype(BF16)
        dp_ref[:, 512:1024] = drv_ref[...].astype(BF16)
        dp_ref[:, 1024:1536] = drg_ref[...].astype(BF16)
        accq = jnp.zeros((1, LANES), F32)
        acck = jnp.zeros((1, LANES), F32)
        for c in range(4):
            sl = slice(c * 128, (c + 1) * 128)
            g = dq_refs[0][:, sl]
            for r in dq_refs[1:]:
                g = g + r[:, sl]
            dx, dg = _hn_bwd(p_ref[:, 1536 + c * 128:1536 + (c + 1) * 128], qg_ref[...], _rope_t(g, Ap, Bp, Cp, 8), G)
            dp_ref[:, 1536 + c * 128:1536 + (c + 1) * 128] = dx.astype(BF16)
            accq = accq + dg
            g = dk_refs[0][:, sl]
            for r in dk_refs[1:]:
                g = g + r[:, sl]
            dx, dg = _hn_bwd(p_ref[:, 2048 + c * 128:2048 + (c + 1) * 128], kg_ref[...], _rope_t(g, Ap, Bp, Cp, 8), G)
            dp_ref[:, 2048 + c * 128:2048 + (c + 1) * 128] = dx.astype(BF16)
            acck = acck + dg
        g = dv_refs[0][...]
        for r in dv_refs[1:]:
            g = g + r[...]
        dp_ref[:, 2560:3072] = g.astype(BF16)

        @pl.when(pl.program_id(0) == 0)
        def _():
            dqg_ref[...] = jnp.zeros_like(dqg_ref)
            dkg_ref[...] = jnp.zeros_like(dkg_ref)

        dqg_ref[...] += _fold_halves(accq)
        dkg_ref[...] += _fold_halves(acck)

    row = lambda w: pl.BlockSpec((tm, w), lambda i: (i, 0))
    vec = pl.BlockSpec((1, LANES), lambda i: (0, 0))
    return _pc(
        body, name="even_pre_bwd", grid=(S // tm,),
        in_specs=[row(3072), row(768), vec, vec, pl.BlockSpec((LANES, LANES), lambda i: (0, 0)),
                  row(256), row(256), row(512), row(512)] + [row(512)] * (3 * npat),
        out_specs=[row(3072), vec, vec],
        out_shape=[jax.ShapeDtypeStruct((S, 3072), BF16), jax.ShapeDtypeStruct((1, LANES), F32),
                   jax.ShapeDtypeStruct((1, LANES), F32)],
        compiler_params=_params(("arbitrary",), VMEM_LIMIT_WIDE),
    )(proj, tab, qg, kg, _group_matrix(), drq, drk, drv, drg, *dqs, *dks, *dvs)


def _ret_consts(pair, half):
    lg = jnp.where(pair == 0, _LOG_GAMMA[half], _LOG_GAMMA[2 + half]).astype(F32)
    i = lax.broadcasted_iota(jnp.int32, (BLK, BLK), 0)
    j = lax.broadcasted_iota(jnp.int32, (BLK, BLK), 1)
    diff = (i - j).astype(F32)
    decay = jnp.where(diff >= 0, jnp.exp(lg * jnp.maximum(diff, 0.0)), 0.0)
    t = lax.broadcasted_iota(jnp.int32, (BLK, 1), 0).astype(F32)
    xi = jnp.exp(lg * (t + 1.0))
    zeta = jnp.exp(lg * (BLK - 1.0 - t))
    cd = jnp.exp(jnp.full((1, 1), BLK, F32) * lg)
    return decay, xi, zeta, cd


RET_STEP = 8


def _ret_fwd(rq, rk, rv):
    S = rq.shape[0]
    nc = S // BLK
    rows = RET_STEP * BLK

    def body(q_ref, k_ref, v_ref, o_ref, st_ref, R):
        p, n = pl.program_id(0), pl.program_id(1)

        @pl.when(n == 0)
        def _():
            R[...] = jnp.zeros_like(R)

        consts = [_ret_consts(p, half) for half in range(2)]
        masks = [_head_mask((BLK, LANES), half) for half in range(2)]
        for ci in range(RET_STEP):
            rs = slice(ci * BLK, (ci + 1) * BLK)
            q2, k2 = q_ref[rs, :], k_ref[rs, :]
            for half in range(2):
                decay, xi, zeta, cd = consts[half]
                m = masks[half]
                qm = jnp.where(m, q2, jnp.zeros_like(q2))
                km = jnp.where(m, k2, jnp.zeros_like(k2))
                v = v_ref[rs, half * 128:(half + 1) * 128]
                Rb = R[half].astype(BF16)
                st_ref[ci, half] = Rb
                sc = lax.dot_general(qm, k2, (((1,), (1,)), ((), ())), preferred_element_type=F32) * decay
                o = jnp.dot(sc.astype(BF16), v, preferred_element_type=F32)
                o = o + jnp.dot(qm, Rb, preferred_element_type=F32) * xi
                o_ref[rs, half * 128:(half + 1) * 128] = o
                kz = (km.astype(F32) * zeta).astype(BF16)
                R[half] = R[half] * cd + lax.dot_general(kz, v, (((0,), (0,)), ((), ())), preferred_element_type=F32)

    return _pc(
        body, name="ret_fwd", grid=(2, nc // RET_STEP),
        in_specs=[pl.BlockSpec((rows, 128), lambda p, n: (n, p)), pl.BlockSpec((rows, 128), lambda p, n: (n, p)),
                  pl.BlockSpec((rows, 256), lambda p, n: (n, p))],
        out_specs=[pl.BlockSpec((rows, 256), lambda p, n: (n, p)),
                   pl.BlockSpec((None, RET_STEP, 2, 128, 128), lambda p, n: (p, n, 0, 0, 0))],
        out_shape=[jax.ShapeDtypeStruct((S, 512), F32), jax.ShapeDtypeStruct((2, nc, 2, 128, 128), BF16)],
        scratch_shapes=[pltpu.VMEM((2, 128, 128), F32)],
        compiler_params=_params(("parallel", "arbitrary")),
    )(rq, rk, rv)


def _ret_bwd(rq, rk, rv, states, do):
    S = rq.shape[0]
    nc = S // BLK
    ns = nc // RET_STEP
    rows = RET_STEP * BLK
    nt = (((1,), (1,)), ((), ()))
    tn = (((0,), (0,)), ((), ()))

    def body(q_ref, k_ref, v_ref, st_ref, do_ref, dq_ref, dk_ref, dv_ref, U):
        p, n = pl.program_id(0), pl.program_id(1)

        @pl.when(n == 0)
        def _():
            U[...] = jnp.zeros_like(U)

        consts = [_ret_consts(p, half) for half in range(2)]
        masks = [_head_mask((BLK, LANES), half) for half in range(2)]
        for ci in reversed(range(RET_STEP)):
            rs = slice(ci * BLK, (ci + 1) * BLK)
            q2, k2 = q_ref[rs, :], k_ref[rs, :]
            dq_acc = jnp.zeros((BLK, LANES), F32)
            dk_acc = jnp.zeros((BLK, LANES), F32)
            for half in range(2):
                decay, xi, zeta, cd = consts[half]
                m = masks[half]
                qm = jnp.where(m, q2, jnp.zeros_like(q2))
                km = jnp.where(m, k2, jnp.zeros_like(k2))
                v = v_ref[rs, half * 128:(half + 1) * 128]
                dob = do_ref[rs, half * 128:(half + 1) * 128].astype(BF16)
                Rb = st_ref[ci, half]
                Ub = U[half].astype(BF16)
                dsc = (lax.dot_general(dob, v, nt, preferred_element_type=F32) * decay).astype(BF16)
                xdo = (dob.astype(F32) * xi).astype(BF16)
                dq_acc += jnp.dot(dsc, km, preferred_element_type=F32) + lax.dot_general(xdo, Rb, nt, preferred_element_type=F32)
                dk_acc += lax.dot_general(dsc, qm, tn, preferred_element_type=F32) \
                    + lax.dot_general(v, Ub, nt, preferred_element_type=F32) * zeta
                sc = (lax.dot_general(qm, k2, nt, preferred_element_type=F32) * decay).astype(BF16)
                kz = (km.astype(F32) * zeta).astype(BF16)
                dv_ref[rs, half * 128:(half + 1) * 128] = lax.dot_general(sc, dob, tn, preferred_element_type=F32) \
                    + jnp.dot(kz, Ub, preferred_element_type=F32)
                U[half] = U[half] * cd + lax.dot_general(qm, xdo, tn, preferred_element_type=F32)
            dq_ref[rs, :] = dq_acc
            dk_ref[rs, :] = dk_acc

    rev = lambda w: pl.BlockSpec((rows, w), lambda p, n: (ns - 1 - n, p))
    return _pc(
        body, name="ret_bwd", grid=(2, ns),
        in_specs=[rev(128), rev(128), rev(256),
                  pl.BlockSpec((None, RET_STEP, 2, 128, 128), lambda p, n: (p, ns - 1 - n, 0, 0, 0)), rev(256)],
        out_specs=[rev(128), rev(128), rev(256)],
        out_shape=[jax.ShapeDtypeStruct((S, 256), F32), jax.ShapeDtypeStruct((S, 256), F32),
                   jax.ShapeDtypeStruct((S, 512), F32)],
        scratch_shapes=[pltpu.VMEM((2, 128, 128), F32)],
        compiler_params=_params(("parallel", "arbitrary")),
    )(rq, rk, rv, states, do)


ATT_TILE = 2048


def _rows(ref, start, n, r):
    if r == 1:
        return ref[pl.ds(start, n), :]
    return ref[pl.ds(start, n, stride=r), :]


def _twice(x):
    return jnp.concatenate([x, x], axis=0)


def _stack_heads(x, masks):
    zero = jnp.zeros_like(x)
    return jnp.concatenate([jnp.where(masks[0], x, zero), jnp.where(masks[1], x, zero)], axis=0)


def _set_rows(ref, start, n, r, val):
    if r == 1:
        ref[pl.ds(start, n), :] = val
    else:
        ref[pl.ds(start, n, stride=r), :] = val


def _band_geometry(S, patterns):
    rmax = max(r for _, r in patterns)
    H = BLK * rmax
    T = min(S, ATT_TILE)
    assert T % H == 0 and S % T == 0
    return H, T, S // T, T // BLK


def _band_fwd(q, k, v, *, patterns, nq, name, sinks=None, want_bf16=False):
    S, Ck = k.shape
    H, T, nt, nbt = _band_geometry(S, patterns)
    ncol = Ck // LANES
    scale = HEAD ** -0.5
    has_sink = sinks is not None
    nt_dims = (((1,), (1,)), ((), ()))

    def body(*refs):
        q_ref, kp_ref, kc_ref, vp_ref, vc_ref = refs[:5]
        sk_ref = refs[5] if has_sink else None
        n_out = 3 if want_bf16 else 2
        outs = refs[5 + has_sink:5 + has_sink + n_out]
        qf, kf, vf, M, L, A = refs[5 + has_sink + n_out:]
        t = pl.program_id(1)
        kf[0:H, :] = kp_ref[...].astype(F32)
        kf[H:H + T, :] = kc_ref[...].astype(F32)
        vf[0:H, :] = vp_ref[...].astype(F32)
        vf[H:H + T, :] = vc_ref[...].astype(F32)
        r_i = lax.broadcasted_iota(jnp.int32, (BLK, 2 * BLK), 0)
        c_i = lax.broadcasted_iota(jnp.int32, (BLK, 2 * BLK), 1)
        dist_i = r_i + BLK - c_i
        masks = [_head_mask((BLK, LANES), h) for h in range(2)]

        for i in range(nq):
            qf[...] = q_ref[:, i * 128:(i + 1) * 128].astype(F32) * scale
            for p, (dist, r) in enumerate(patterns):
                in_band = (dist_i >= 0) & (dist_i <= dist)
                in_band_first = in_band & ((c_i >= BLK) | (t > 0))
                in_band, in_band_first = _twice(in_band), _twice(in_band_first)

                def unit(j, b, p=p, r=r, in_band=in_band, in_band_first=in_band_first):
                    q0 = j + b * (BLK * r)
                    q2 = _rows(qf, q0, BLK, r).astype(BF16)
                    kcat = _rows(kf, H + q0 - BLK * r, 2 * BLK, r).astype(BF16)
                    vcat = _rows(vf, H + q0 - BLK * r, 2 * BLK, r).astype(BF16)
                    valid = in_band if b > 0 else in_band_first
                    s = lax.dot_general(_stack_heads(q2, masks), kcat, nt_dims, preferred_element_type=F32)
                    s = jnp.where(valid, s, -jnp.inf)
                    mx = jnp.max(s, axis=1, keepdims=True)
                    pr = jnp.exp(s - mx)
                    den = jnp.sum(pr, axis=1, keepdims=True)
                    pv = jnp.dot(pr.astype(BF16), vcat, preferred_element_type=F32)
                    m2 = jnp.where(masks[0], mx[:BLK], mx[BLK:])
                    l2 = jnp.where(masks[0], den[:BLK], den[BLK:])
                    a2 = jnp.where(masks[0], pv[:BLK], pv[BLK:])
                    if p > 0:
                        mo = _rows(M, q0, BLK, r)
                        mn = jnp.maximum(mo, m2)
                        wa, wb = jnp.exp(mo - mn), jnp.exp(m2 - mn)
                        l2 = wa * _rows(L, q0, BLK, r) + wb * l2
                        a2 = wa * _rows(A, q0, BLK, r) + wb * a2
                        m2 = mn
                    _set_rows(M, q0, BLK, r, m2)
                    _set_rows(L, q0, BLK, r, l2)
                    _set_rows(A, q0, BLK, r, a2)

                for u in range(nbt):
                    unit(u % r, u // r)
            sl = slice(i * 128, (i + 1) * 128)
            mm, ll, aa = M[...], L[...], A[...]
            if has_sink:
                snk = sk_ref[:, sl]
                mn = jnp.maximum(mm, snk)
                w = jnp.exp(mm - mn)
                ll = ll * w + jnp.exp(snk - mn)
                aa = aa * w
                mm = mn
            o = aa / ll
            outs[0][:, sl] = o
            outs[1][:, sl] = mm + jnp.log(ll)
            if want_bf16:
                outs[2][:, sl] = o.astype(BF16)

    th = T // H
    qspec = pl.BlockSpec((T, nq * 128), lambda j, t: (t, j))
    cur = pl.BlockSpec((T, 128), lambda j, t: (t, j))
    prev = pl.BlockSpec((H, 128), lambda j, t: (jnp.maximum(t * th - 1, 0), j))
    in_specs = [qspec, prev, cur, prev, cur]
    args = [q, k, k, v, v]
    if has_sink:
        in_specs.append(pl.BlockSpec((1, nq * 128), lambda j, t: (0, j)))
        args.append(sinks)
    out_dts = [F32, F32] + ([BF16] if want_bf16 else [])
    return _pc(
        body, name=name, grid=(ncol, nt), in_specs=in_specs,
        out_specs=[qspec] * len(out_dts),
        out_shape=[jax.ShapeDtypeStruct(q.shape, dt) for dt in out_dts],
        scratch_shapes=[pltpu.VMEM((T, LANES), F32), pltpu.VMEM((H + T, LANES), F32), pltpu.VMEM((H + T, LANES), F32),
                        pltpu.VMEM((T, LANES), F32), pltpu.VMEM((T, LANES), F32), pltpu.VMEM((T, LANES), F32)],
        compiler_params=_params(("parallel", "parallel")),
    )(*args)


def _band_bwd(q, k, v, lse, delta, do, *, patterns, nq, name, sinks=None, do_col0=0):
    S, Ck = k.shape
    H, T, nt, nbt = _band_geometry(S, patterns)
    ncol = Ck // LANES
    scale = HEAD ** -0.5
    has_sink = sinks is not None
    nt_dims = (((1,), (1,)), ((), ()))
    tn_dims = (((0,), (0,)), ((), ()))

    def body(*refs):
        (qc_ref, qn_ref, kp_ref, kc_ref, vp_ref, vc_ref, lc_ref, ln_ref, ec_ref, en_ref, dc_ref, dn_ref) = refs[:12]
        sk_ref = refs[12] if has_sink else None
        n_out = 4 if has_sink else 3
        outs = refs[12 + has_sink:12 + has_sink + n_out]
        dq_ref, dk_ref, dv_ref = outs[:3]
        qf, kf, vf, lf, ef, df = refs[12 + has_sink + n_out:]
        t = pl.program_id(1)
        kf[0:H, :] = kp_ref[...].astype(F32)
        kf[H:H + T, :] = kc_ref[...].astype(F32)
        vf[0:H, :] = vp_ref[...].astype(F32)
        vf[H:H + T, :] = vc_ref[...].astype(F32)
        dk_ref[...] = jnp.zeros_like(dk_ref)
        dv_ref[...] = jnp.zeros_like(dv_ref)
        r_i = lax.broadcasted_iota(jnp.int32, (BLK, 2 * BLK), 0)
        c_i = lax.broadcasted_iota(jnp.int32, (BLK, 2 * BLK), 1)
        dist_q = r_i + BLK - c_i
        dist_h = dist_q[:, :BLK]
        m1 = [_head_mask((BLK, LANES), h) for h in range(2)]

        def stacked_inputs(q2, do2, l2, e2):
            spread = lambda v: jnp.concatenate([jnp.where(m1[0], v, _roll(v, HEAD)), jnp.where(m1[1], v, _roll(v, HEAD))], axis=0)
            return _stack_heads(q2, m1), _stack_heads(do2.astype(BF16), m1), spread(l2), spread(e2)

        for i in range(nq):
            sl = slice(i * 128, (i + 1) * 128)
            qf[0:T, :] = qc_ref[:, sl].astype(F32) * scale
            qf[T:T + H, :] = qn_ref[:, sl].astype(F32) * scale
            for buf, c_ref, n_ref in ((lf, lc_ref, ln_ref), (ef, ec_ref, en_ref), (df, dc_ref, dn_ref)):
                buf[0:T, :] = c_ref[:, sl]
                buf[T:T + H, :] = n_ref[:, sl]
            if has_sink:
                @pl.when(t == 0)
                def _():
                    outs[3][:, sl] = jnp.zeros((1, LANES), F32)

                outs[3][:, sl] += jnp.sum(-jnp.exp(sk_ref[:, sl] - lc_ref[:, sl]) * ec_ref[:, sl], axis=0, keepdims=True)
            for p, (dist, r) in enumerate(patterns):
                band_q = (dist_q >= 0) & (dist_q <= dist)
                band_first = band_q & ((c_i >= BLK) | (t > 0))
                band_h = (dist_h >= 0) & (dist_h <= dist)
                band_q, band_first, band_h = _twice(band_q), _twice(band_first), _twice(band_h)

                def add_rows(ref, start, val, r=r):
                    _set_rows(ref, start, BLK, r, _rows(ref, start, BLK, r) + val)

                def unit(j, b, p=p, r=r, band_q=band_q, band_first=band_first):
                    q0 = j + b * (BLK * r)
                    q2 = _rows(qf, q0, BLK, r).astype(BF16)
                    do2, l2, e2 = _rows(df, q0, BLK, r), _rows(lf, q0, BLK, r), _rows(ef, q0, BLK, r)
                    kcat = _rows(kf, H + q0 - BLK * r, 2 * BLK, r).astype(BF16)
                    vcat = _rows(vf, H + q0 - BLK * r, 2 * BLK, r).astype(BF16)
                    valid = band_q if b > 0 else band_first
                    qs, dos, ls, es = stacked_inputs(q2, do2, l2, e2)
                    s = lax.dot_general(qs, kcat, nt_dims, preferred_element_type=F32)
                    pr = jnp.where(valid, jnp.exp(s - jnp.concatenate([ls, ls], axis=1)), 0.0)
                    dp = lax.dot_general(dos, vcat, nt_dims, preferred_element_type=F32)
                    ds = (pr * (dp - jnp.concatenate([es, es], axis=1))).astype(BF16)
                    dqs = jnp.dot(ds, kcat, preferred_element_type=F32) * scale
                    dq2 = jnp.where(m1[0], dqs[:BLK], dqs[BLK:])
                    dvc = lax.dot_general(pr.astype(BF16), dos, tn_dims, preferred_element_type=F32)
                    dkc = lax.dot_general(ds, qs, tn_dims, preferred_element_type=F32)
                    if p > 0:
                        dq2 = dq2 + _rows(dq_ref.at[:, sl], q0, BLK, r)
                    _set_rows(dq_ref.at[:, sl], q0, BLK, r, dq2)
                    add_rows(dk_ref, q0, dkc[BLK:])
                    add_rows(dv_ref, q0, dvc[BLK:])
                    if b > 0:
                        add_rows(dk_ref, q0 - BLK * r, dkc[:BLK])
                        add_rows(dv_ref, q0 - BLK * r, dvc[:BLK])

                def halo_unit(j, r=r, band_h=band_h):
                    k0 = j + (nbt // r - 1) * (BLK * r)
                    q2 = _rows(qf, T + j, BLK, r).astype(BF16)
                    do2, l2, e2 = _rows(df, T + j, BLK, r), _rows(lf, T + j, BLK, r), _rows(ef, T + j, BLK, r)
                    kc = _rows(kf, H + k0, BLK, r).astype(BF16)
                    vc = _rows(vf, H + k0, BLK, r).astype(BF16)
                    qs, dos, ls, es = stacked_inputs(q2, do2, l2, e2)
                    s = lax.dot_general(qs, kc, nt_dims, preferred_element_type=F32)
                    pr = jnp.where(band_h, jnp.exp(s - ls), 0.0)
                    dp = lax.dot_general(dos, vc, nt_dims, preferred_element_type=F32)
                    ds = (pr * (dp - es)).astype(BF16)
                    add_rows(dk_ref, k0, lax.dot_general(ds, qs, tn_dims, preferred_element_type=F32))
                    add_rows(dv_ref, k0, lax.dot_general(pr.astype(BF16), dos, tn_dims, preferred_element_type=F32))

                for u in range(nbt):
                    unit(u % r, u // r)
                if nt > 1:
                    @pl.when(t < nt - 1)
                    def _(r=r, halo_unit=halo_unit):
                        for j in range(r):
                            halo_unit(j)

    th = T // H
    last = S // H - 1
    qcur = pl.BlockSpec((T, nq * 128), lambda j, t: (t, j))
    qnext = pl.BlockSpec((H, nq * 128), lambda j, t: (jnp.minimum((t + 1) * th, last), j))
    cur = pl.BlockSpec((T, 128), lambda j, t: (t, j))
    prev = pl.BlockSpec((H, 128), lambda j, t: (jnp.maximum(t * th - 1, 0), j))
    dcur = pl.BlockSpec((T, nq * 128), lambda j, t: (t, j + do_col0))
    dnext = pl.BlockSpec((H, nq * 128), lambda j, t: (jnp.minimum((t + 1) * th, last), j + do_col0))
    in_specs = [qcur, qnext, prev, cur, prev, cur, qcur, qnext, qcur, qnext, dcur, dnext]
    args = [q, q, k, k, v, v, lse, lse, delta, delta, do, do]
    out_specs = [qcur, cur, cur]
    out_shape = [jax.ShapeDtypeStruct(q.shape, F32), jax.ShapeDtypeStruct(k.shape, F32), jax.ShapeDtypeStruct(k.shape, F32)]
    if has_sink:
        vec = pl.BlockSpec((1, nq * 128), lambda j, t: (0, j))
        in_specs.append(vec)
        args.append(sinks)
        out_specs.append(vec)
        out_shape.append(jax.ShapeDtypeStruct((1, q.shape[1]), F32))
    big = pltpu.VMEM((T + H, LANES), F32)
    return _pc(
        body, name=name, grid=(ncol, nt), in_specs=in_specs, out_specs=out_specs, out_shape=out_shape,
        scratch_shapes=[big] * 6,
        compiler_params=_params(("parallel", "arbitrary")),
    )(*args)


def _even_post_fwd(ro, proj, gn, da):
    S = ro.shape[0]
    tm = 512

    def body(ro_ref, rg_ref, gn_ref, da_ref, mix_ref):
        for c in range(4):
            sl = slice(c * 128, (c + 1) * 128)
            x = ro_ref[:, sl]
            mu = jnp.mean(x, axis=1, keepdims=True)
            xc = x - mu
            var = jnp.mean(xc * xc, axis=1, keepdims=True)
            y = xc * lax.rsqrt(var + EPS) * gn_ref[:, sl]
            z = rg_ref[:, sl]
            mix_ref[:, sl] = (z * jax.nn.sigmoid(z) * y).astype(BF16)
        mix_ref[:, 512:1024] = da_ref[...].astype(BF16)

    row = lambda w: pl.BlockSpec((tm, w), lambda i: (i, 0))
    return _pc(
        body, name="even_post_fwd", grid=(S // tm,),
        in_specs=[row(512), pl.BlockSpec((tm, 512), lambda i: (i, 2)), pl.BlockSpec((1, 512), lambda i: (0, 0)), row(512)],
        out_specs=row(1024), out_shape=jax.ShapeDtypeStruct((S, 1024), BF16),
        compiler_params=_params(("parallel",), VMEM_LIMIT_WIDE),
    )(ro, proj, gn, da)


def _even_post_bwd(ro, proj, gn, dmixed):
    S = ro.shape[0]
    tm = 512

    def body(ro_ref, rg_ref, gn_ref, dm_ref, dro_ref, drg_ref, dgn_ref):
        @pl.when(pl.program_id(0) == 0)
        def _():
            dgn_ref[...] = jnp.zeros_like(dgn_ref)

        for c in range(4):
            sl = slice(c * 128, (c + 1) * 128)
            x = ro_ref[:, sl]
            mu = jnp.mean(x, axis=1, keepdims=True)
            xc = x - mu
            rstd = lax.rsqrt(jnp.mean(xc * xc, axis=1, keepdims=True) + EPS)
            xh = xc * rstd
            gain = gn_ref[:, sl]
            y = xh * gain
            z = rg_ref[:, sl]
            sg = jax.nn.sigmoid(z)
            dra = dm_ref[:, sl]
            drg_ref[:, sl] = dra * y * sg * (1.0 + z * (1.0 - sg))
            dy = dra * z * sg
            dgn_ref[:, sl] += jnp.sum(dy * xh, axis=0, keepdims=True)
            dxh = dy * gain
            dro_ref[:, sl] = rstd * (dxh - jnp.mean(dxh, axis=1, keepdims=True)
                                     - xh * jnp.mean(dxh * xh, axis=1, keepdims=True))

    row = lambda w: pl.BlockSpec((tm, w), lambda i: (i, 0))
    vec = pl.BlockSpec((1, 512), lambda i: (0, 0))
    return _pc(
        body, name="even_post_bwd", grid=(S // tm,),
        in_specs=[row(512), pl.BlockSpec((tm, 512), lambda i: (i, 2)), vec, row(512)],
        out_specs=[row(512), row(512), vec],
        out_shape=[jax.ShapeDtypeStruct((S, 512), F32), jax.ShapeDtypeStruct((S, 512), F32),
                   jax.ShapeDtypeStruct((1, 512), F32)],
        compiler_params=_params(("arbitrary",), VMEM_LIMIT_WIDE),
    )(ro, proj, gn, dmixed)


def _swa_pre_fwd(proj, tab, qg, kg):
    S = proj.shape[0]
    tm = 512

    def body(p_ref, tab_ref, qg_ref, kg_ref, g_ref, q_ref, k_ref, v_ref):
        Ap, Bp, Cp = _tab(tab_ref, 1)
        G = g_ref[...]
        lo = _head_mask((tm, LANES), 0)
        for c in range(8):
            sl = slice(c * 128, (c + 1) * 128)
            q_ref[:, sl] = _rope(_hn_fwd(p_ref[:, sl], qg_ref[...], G), Ap, Bp, Cp, 8).astype(BF16)
        for c in range(2):
            kn = _rope(_hn_fwd(p_ref[:, 1024 + c * 128:1024 + (c + 1) * 128], kg_ref[...], G), Ap, Bp, Cp, 8)
            vv = p_ref[:, 1280 + c * 128:1280 + (c + 1) * 128]
            for t, ref in ((kn, k_ref), (vv, v_ref)):
                sw = _roll(t, HEAD)
                ref[:, (2 * c) * 128:(2 * c + 1) * 128] = jnp.where(lo, t, sw).astype(BF16)
                ref[:, (2 * c + 1) * 128:(2 * c + 2) * 128] = jnp.where(lo, sw, t).astype(BF16)

    row = lambda w: pl.BlockSpec((tm, w), lambda i: (i, 0))
    vec = pl.BlockSpec((1, LANES), lambda i: (0, 0))
    return _pc(
        body, name="swa_pre_fwd", grid=(S // tm,),
        in_specs=[row(1536), row(768), vec, vec, pl.BlockSpec((LANES, LANES), lambda i: (0, 0))],
        out_specs=[row(1024), row(512), row(512)],
        out_shape=[jax.ShapeDtypeStruct((S, w), BF16) for w in (1024, 512, 512)],
        compiler_params=_params(("parallel",), VMEM_LIMIT_WIDE),
    )(proj, tab, qg, kg, _group_matrix())


def _swa_pre_bwd(proj, tab, qg, kg, dq, dk, dv):
    S = proj.shape[0]
    tm = 512

    def body(p_ref, tab_ref, qg_ref, kg_ref, g_ref, dq_ref, dk_ref, dv_ref, dp_ref, db_ref, dqg_ref, dkg_ref):
        Ap, Bp, Cp = _tab(tab_ref, 1)
        G = g_ref[...]
        lo = _head_mask((tm, LANES), 0)

        @pl.when(pl.program_id(0) == 0)
        def _():
            db_ref[...] = jnp.zeros_like(db_ref)
            dqg_ref[...] = jnp.zeros_like(dqg_ref)
            dkg_ref[...] = jnp.zeros_like(dkg_ref)

        accq = jnp.zeros((1, LANES), F32)
        acck = jnp.zeros((1, LANES), F32)
        for c in range(8):
            sl = slice(c * 128, (c + 1) * 128)
            dx, dg = _hn_bwd(p_ref[:, sl], qg_ref[...], _rope_t(dq_ref[:, sl], Ap, Bp, Cp, 8), G)
            dp_ref[:, sl] = dx.astype(BF16)
            db_ref[:, sl] += jnp.sum(dx, axis=0, keepdims=True)
            accq = accq + dg
        for c in range(2):
            folded = []
            for ref in (dk_ref, dv_ref):
                a = ref[:, (2 * c) * 128:(2 * c + 1) * 128]
                b = ref[:, (2 * c + 1) * 128:(2 * c + 2) * 128]
                folded.append(jnp.where(lo, a + _roll(a, HEAD), b + _roll(b, HEAD)))
            ks = slice(1024 + c * 128, 1024 + (c + 1) * 128)
            dx, dg = _hn_bwd(p_ref[:, ks], kg_ref[...], _rope_t(folded[0], Ap, Bp, Cp, 8), G)
            dp_ref[:, ks] = dx.astype(BF16)
            db_ref[:, ks] += jnp.sum(dx, axis=0, keepdims=True)
            acck = acck + dg
            vs = slice(1280 + c * 128, 1280 + (c + 1) * 128)
            dp_ref[:, vs] = folded[1].astype(BF16)
            db_ref[:, vs] += jnp.sum(folded[1], axis=0, keepdims=True)
        dqg_ref[...] += _fold_halves(accq)
        dkg_ref[...] += _fold_halves(acck)

    row = lambda w: pl.BlockSpec((tm, w), lambda i: (i, 0))
    vec = pl.BlockSpec((1, LANES), lambda i: (0, 0))
    return _pc(
        body, name="swa_pre_bwd", grid=(S // tm,),
        in_specs=[row(1536), row(768), vec, vec, pl.BlockSpec((LANES, LANES), lambda i: (0, 0)),
                  row(1024), row(512), row(512)],
        out_specs=[row(1536), pl.BlockSpec((1, 1536), lambda i: (0, 0)), vec, vec],
        out_shape=[jax.ShapeDtypeStruct((S, 1536), BF16), jax.ShapeDtypeStruct((1, 1536), F32),
                   jax.ShapeDtypeStruct((1, LANES), F32), jax.ShapeDtypeStruct((1, LANES), F32)],
        compiler_params=_params(("arbitrary",), VMEM_LIMIT_WIDE),
    )(proj, tab, qg, kg, _group_matrix(), dq, dk, dv)


def _relu2_of(u):
    r = jnp.maximum(u.astype(F32), 0.0)
    return r * r


def _drelu2(acc, u):
    return (acc * 2.0 * jnp.maximum(u.astype(F32), 0.0),)


def _add(acc, res):
    return (acc + res,)


def _add_norm_in(res, g):
    def epilogue(acc, r, gv):
        xn = acc + r
        return xn, xn * lax.rsqrt(jnp.mean(xn * xn, axis=-1, keepdims=True) + EPS) * gv

    return dict(outs=[F32, BF16], epilogue=epilogue, extras=[(res, "mn"), (g.reshape(1, D_MODEL), "n")])


_T = dict(tm=1024, tn=1024, tk=1024)


def _rms_bwd_in(x, g, dres):
    def epilogue(dh, xv, gv, dr):
        r = lax.rsqrt(jnp.mean(xv * xv, axis=-1, keepdims=True) + EPS)
        t = dh * gv
        dx = dr + r * t - xv * (r * r * r) * jnp.mean(xv * t, axis=-1, keepdims=True)
        return dx, dx, jnp.sum(dh * xv * r, axis=0, keepdims=True)

    return dict(outs=[F32, BF16, ("colsum",)], epilogue=epilogue,
                extras=[(x, "mn"), (g.reshape(1, D_MODEL), "n"), (dres, "mn")])


def _delta_in(o, col0):
    width = D_MODEL - col0

    def epilogue(do, ov, G):
        parts = [_gmean(do[:, col0 + c * 128:col0 + (c + 1) * 128] * ov[:, c * 128:(c + 1) * 128], G) * float(HEAD)
                 for c in range(width // LANES)]
        return do, jnp.concatenate(parts, axis=1)

    return dict(outs=[F32, (F32, width)], epilogue=epilogue, extras=[(o, width), (_group_matrix(), "full")])


def _loss_in(res, target):
    def epilogue(acc, r, t):
        e = acc + r - t
        dy = e * (1.0 / D_MODEL)
        return dy, dy, jnp.sum(e * e, axis=0, keepdims=True) * (0.5 / D_MODEL)

    return dict(outs=[F32, BF16, ("colsum",)], epilogue=epilogue, extras=[(res, "mn"), (target, "mn")])


def _mlp_fwd(h, wts, layer, tag, tail):
    u = _matmul(h, wts, dims="nn", **_T, outs=[BF16], b_cs=True, b_row0=layer, name=f"mlp_up{tag}")
    out = _matmul(u, wts, dims="nn", **_T, a_pro=_relu2_of, b_rs=1024, b_row0=2 + layer, name=f"mlp_down{tag}", **tail)
    return out, (h, u)


def _mlp_bwd(x, g, wts, layer, saved, dy, dyb, tag):
    h, u = saved
    du = _matmul(dyb, wts, dims="nt", **_T, outs=[BF16], epilogue=_drelu2, extras=[(u, "mn")], b_rs=1024,
                 b_row0=2 + layer, name=f"mlp_du{tag}")
    dw_dn = _matmul(u, dyb, dims="tn", **_T, outs=[F32], a_pro=_relu2_of, name=f"mlp_dwdown{tag}")
    dw_up = _matmul(h, du, dims="tn", **_T, outs=[F32], o_cs=N_CHIPS, name=f"mlp_dwup{tag}")
    dx, dxb, dg = _matmul(du, wts, dims="nt", **_T, b_cs=True, b_row0=layer, b_rows=1024, name=f"mlp_dh{tag}",
                          vmem=VMEM_LIMIT_WIDE, **_rms_bwd_in(x, g, dy))
    return dx, dxb, dg, dw_up, dw_dn


def _local_step(x, pos_col, target, first_of, rest_begin, rest_of, P, red):
    S = x.shape[0]
    tab = _tables(pos_col)
    tile2 = lambda g: jnp.tile(g.reshape(1, HEAD), (1, 2))
    dqg, dkg = tile2(P["dil_q_gain"]), tile2(P["dil_k_gain"])
    sqg, skg = tile2(P["swa_q_gain"]), tile2(P["swa_k_gain"])
    gn = P["ret_gn_gain"].reshape(1, 512)
    sink_b = jnp.repeat(P["swa_sinks"].reshape(16), HEAD).reshape(1, 1024)

    h0 = _rms_fwd(x, P["norm_mix"][0], "rms_mix_fwd0")
    W = first_of(h0)
    proj = _matmul(h0, W["hyb_w_in"], dims="nn", tm=1024, tn=768, tk=1024, outs=[F32], b_cs=True, name="hyb_in")
    rq, rk, rv, dq, dk, dv = _even_pre_fwd(proj, tab, dqg, dkg)
    ro, states = _ret_fwd(rq, rk, rv)
    dil = [(w // r, r) for w, r in DIL_PATTERNS]
    da, dlse = _band_fwd(dq, dk, dv, patterns=dil, nq=1, name="dil_fwd")
    mixed = rest_begin(_even_post_fwd(ro, proj, gn, da))
    x1, h1 = _matmul(mixed, W["hyb_w_out"], dims="nn", **_T, name="hyb_out", **_add_norm_in(x, P["norm_mlp"][0]))
    rest, bias = rest_of(x1)
    W = {**W, **rest}
    (x2, h2), mlp0 = _mlp_fwd(h1, W["packed"], 0, "0", _add_norm_in(x1, P["norm_mix"][1]))

    proj2 = _matmul(h2, W["swa_w_qkv"], dims="nn", tm=1024, tn=384, tk=1024, outs=[F32], b_cs=True,
                    epilogue=_add, extras=[(bias.reshape(1, 1536), "n")], name="swa_qkv")
    sq, sk, sv = _swa_pre_fwd(proj2, tab, sqg, skg)
    swa = [(SWA_DIST, 1)]
    so, slse, so_b = _band_fwd(sq, sk, sv, patterns=swa, nq=2, name="swa_fwd", sinks=sink_b, want_bf16=True)
    x3, h3 = _matmul(so_b, W["swa_w_out"], dims="nn", **_T, name="swa_out", **_add_norm_in(x2, P["norm_mlp"][1]))
    (dy, dyb, loss_cols), mlp1 = _mlp_fwd(h3, W["packed"], 1, "1", _loss_in(x3, target))
    loss = jnp.broadcast_to(jnp.sum(loss_cols), (1, LANES))

    gw, gp = {}, {}
    dx3, dx3b, dg_mlp1, gw["mlp_w_up1"], gw["mlp_w_down1"] = _mlp_bwd(x3, P["norm_mlp"][1], W["packed"], 1, mlp1, dy, dyb, "1")
    dx3b = red.begin("mlp1", {n: (gw[n], 1024) for n in ("mlp_w_up1", "mlp_w_down1")}, dx3b)
    gw["swa_w_out"] = _matmul(so_b, dx3b, dims="tn", **_T, outs=[F32], name="swa_dwout")
    dso, sdelta = _matmul(dx3b, W["swa_w_out"], dims="nt", **_T, name="swa_do", vmem=VMEM_LIMIT_WIDE, **_delta_in(so, 0))
    dsq, dsk, dsv, dsink = _band_bwd(sq, sk, sv, slse, sdelta, dso, patterns=swa, nq=2, name="swa_bwd", sinks=sink_b)
    dproj2, gp["swa_b_qkv"], gp["swa_q_gain"], gp["swa_k_gain"] = _swa_pre_bwd(proj2, tab, sqg, skg, dsq, dsk, dsv)
    gp["swa_sinks"] = dsink
    gw["swa_w_qkv"] = _matmul(h2, dproj2, dims="tn", tm=1024, tn=384, tk=1024, outs=[F32], o_cs=N_CHIPS, name="swa_dwqkv")
    dx2, dx2b, dg_mix1 = _matmul(dproj2, W["swa_w_qkv"], dims="nt", tm=1024, tn=1024, tk=384, b_cs=True, name="swa_dh",
                                 vmem=VMEM_LIMIT_WIDE, **_rms_bwd_in(x2, P["norm_mix"][1], dx3))
    dx2b = red.begin("swa", {"swa_w_qkv": (gw["swa_w_qkv"], 1024), "swa_w_out": (gw["swa_w_out"], 256)}, dx2b)
    dx2b = red.advance("mlp1", dx2b, dx2b)

    dx1, dx1b, dg_mlp0, gw["mlp_w_up0"], gw["mlp_w_down0"] = _mlp_bwd(x1, P["norm_mlp"][0], W["packed"], 0, mlp0, dx2, dx2b, "0")
    gw["hyb_w_out"] = _matmul(mixed, dx1b, dims="tn", **_T, outs=[F32], name="hyb_dwout")
    dx1b = red.begin("mlp0", {"mlp_w_up0": (gw["mlp_w_up0"], 1024), "mlp_w_down0": (gw["mlp_w_down0"], 1024),
                              "hyb_w_out": (gw["hyb_w_out"], 256)}, dx1b)
    dx1b = red.advance("swa", dx1b, dx1b)
    red.finish("mlp1", dx1b)
    dmixed, ddelta = _matmul(dx1b, W["hyb_w_out"], dims="nt", **_T, name="hyb_dmixed", vmem=VMEM_LIMIT_WIDE,
                             **_delta_in(da, 512))
    dro, drg, gp["ret_gn_gain"] = _even_post_bwd(ro, proj, gn, dmixed)
    drq, drk, drv = _ret_bwd(rq, rk, rv, states, dro)
    ddq, ddk, ddv = _band_bwd(dq, dk, dv, dlse, ddelta, dmixed, patterns=dil, nq=1, name="dil_bwd", do_col0=4)
    ddq = red.advance("mlp0", ddq, ddq)
    red.finish("swa", ddq)
    dproj, gp["dil_q_gain"], gp["dil_k_gain"] = _even_pre_bwd(proj, tab, dqg, dkg, drq, drk, drv, drg, [ddq], [ddk], [ddv])
    gw["hyb_w_in"] = _matmul(h0, dproj, dims="tn", tm=1024, tn=768, tk=1024, outs=[F32], o_cs=N_CHIPS, name="hyb_dwin")
    dproj = red.begin("win", {"hyb_w_in": (gw["hyb_w_in"], 1024)}, dproj)
    grad_x, _, dg_mix0 = _matmul(dproj, W["hyb_w_in"], dims="nt", tm=1024, tn=1024, tk=768, b_cs=True, name="hyb_dh",
                                 vmem=VMEM_LIMIT_WIDE, **_rms_bwd_in(x, P["norm_mix"][0], dx1))
    red.finish("mlp0", grad_x)
    gp["norm_mix"] = jnp.concatenate([dg_mix0, dg_mix1], axis=0)
    gp["norm_mlp"] = jnp.concatenate([dg_mlp0, dg_mlp1], axis=0)
    return loss, grad_x, gp


HBM = pl.BlockSpec(memory_space=pltpu.HBM)


def _place():
    x, y, c = lax.axis_index("x"), lax.axis_index("y"), lax.axis_index("c")
    chips = [(1 - x, y), (x, 1 - y), (1 - x, 1 - y)]
    return x, y, c, chips


SEM = pl.BlockSpec(memory_space=pltpu.SEMAPHORE)
EFFECT = pltpu.SideEffectType.DATAFLOW_SIDE_EFFECTING


def _half_block(ref, chip, core):
    rh = ref.shape[1] // 2
    return ref.at[2 * chip[0] + chip[1], pl.ds(core * rh, rh), :]


def _gather_start(buf, ride, name):
    def body(b_ref, ride_ref, s0, s1, s2, r0, r1, r2, b_out, ride_out):
        x, y, c, chips = _place()
        for chip, s, r in zip(chips, (s0, s1, s2), (r0, r1, r2)):
            mine = _half_block(b_ref, (x, y), c)
            pltpu.make_async_remote_copy(src_ref=mine, dst_ref=mine, send_sem=s, recv_sem=r,
                                         device_id=(*chip, c), device_id_type=MESH).start()

    sem = pltpu.SemaphoreType.DMA(())
    return _pc(
        body, name=name,
        out_shape=(sem,) * 6 + (pltpu.HBM(buf.shape, buf.dtype), pltpu.HBM(ride.shape, ride.dtype)),
        in_specs=(HBM, HBM), out_specs=(SEM,) * 6 + (HBM, HBM), input_output_aliases={0: 6, 1: 7},
        compiler_params=pltpu.CompilerParams(has_side_effects=EFFECT),
    )(pltpu.with_memory_space_constraint(buf, pltpu.HBM), pltpu.with_memory_space_constraint(ride, pltpu.HBM))


def _gather_wait(buf, sems, after, name):
    def body(b_ref, s0, s1, s2, r0, r1, r2, after_ref, b_out):
        x, y, c, chips = _place()
        for chip, s, r in zip(chips, (s0, s1, s2), (r0, r1, r2)):
            cp = pltpu.make_async_remote_copy(src_ref=_half_block(b_ref, (x, y), c), dst_ref=_half_block(b_ref, chip, c),
                                              send_sem=s, recv_sem=r, device_id=(*chip, c), device_id_type=MESH)
            cp.wait_send()
            cp.wait_recv()

    return _pc(
        body, name=name, out_shape=pltpu.HBM(buf.shape, buf.dtype),
        in_specs=(HBM,) + (SEM,) * 6 + (pl.BlockSpec(memory_space=pl.ANY),), out_specs=HBM, input_output_aliases={0: 0},
        compiler_params=pltpu.CompilerParams(has_side_effects=EFFECT),
    )(buf, *sems, after)


def _gather_handover(buf, name):
    def body(b_ref, out_ref, send_sems, recv_sems):
        x, y, c, chips = _place()
        cps = []
        for k, chip in enumerate(chips):
            mine = _half_block(b_ref, chip, c)
            cps.append(pltpu.make_async_remote_copy(src_ref=mine, dst_ref=mine, send_sem=send_sems.at[k],
                                                    recv_sem=recv_sems.at[k], device_id=(x, y, 1 - c), device_id_type=MESH))
        for cp in cps:
            cp.start()
        for k, chip in enumerate(chips):
            theirs = _half_block(b_ref, chip, 1 - c)
            pltpu.make_async_remote_copy(src_ref=theirs, dst_ref=theirs, send_sem=send_sems.at[k], recv_sem=recv_sems.at[k],
                                         device_id=(x, y, 1 - c), device_id_type=MESH).wait_recv()
        for cp in cps:
            cp.wait_send()

    return _pc(
        body, name=name, in_specs=[HBM], out_specs=HBM,
        out_shape=jax.ShapeDtypeStruct(buf.shape, buf.dtype), input_output_aliases={0: 0},
        scratch_shapes=[pltpu.SemaphoreType.DMA((3,)), pltpu.SemaphoreType.DMA((3,))],
    )(buf)


def _pair_sum(t, l, place, name):
    _, r, cols = t.shape
    rh = r // 2
    tr = min(rh, 256)
    nr = rh // tr

    def body(pl_ref, t_ref, l_ref, o_ref):
        o_ref[...] = (t_ref[...] + l_ref[...]).astype(BF16)

    other = lambda s, p: s + jnp.where(s >= p[0], 1, 0)
    return _pc(
        body, name=name,
        grid_spec=pltpu.PrefetchScalarGridSpec(
            num_scalar_prefetch=1, grid=(N_CHIPS - 1, nr),
            in_specs=[pl.BlockSpec((None, tr, cols), lambda s, i, p: (other(s, p), p[1] * nr + i, 0)),
                      pl.BlockSpec((None, tr, cols), lambda s, i, p: (other(s, p), i, 0))],
            out_specs=pl.BlockSpec((None, tr, cols), lambda s, i, p: (other(s, p), i, 0))),
        out_shape=jax.ShapeDtypeStruct((N_CHIPS, rh, cols), BF16),
        compiler_params=_params(("parallel", "parallel")),
    )(place, t, l)


def _final_sum(t, l, rcv, place, name, layer=0, layers=1, into=None):
    _, r, cols = t.shape
    rh = r // 2
    tr = min(rh, 256)
    nr = rh // tr

    def body(pl_ref, t_ref, l_ref, r_ref, *rest):
        acc = t_ref[...] + l_ref[...]
        for k in range(3):
            acc = acc + r_ref[k].astype(F32)
        rest[-1][...] = acc

    in_specs = [pl.BlockSpec((None, tr, cols), lambda i, p: (p[0], p[1] * nr + i, 0)),
                pl.BlockSpec((None, tr, cols), lambda i, p: (p[0], i, 0)),
                pl.BlockSpec((3, tr, cols), lambda i, p: (0, i, 0))]
    args = [place, t, l, rcv]
    aliases = {}
    if into is not None:
        in_specs.append(pl.BlockSpec(memory_space=pl.ANY))
        args.append(into)
        aliases = {4: 0}
    return _pc(
        body, name=name,
        grid_spec=pltpu.PrefetchScalarGridSpec(
            num_scalar_prefetch=1, grid=(nr,), in_specs=in_specs,
            out_specs=pl.BlockSpec((tr, cols), lambda i, p: (2 * nr * layer + p[1] * nr + i, 0))),
        out_shape=jax.ShapeDtypeStruct((layers * r, cols), F32), input_output_aliases=aliases,
        compiler_params=_params(("parallel",)),
    )(*args)


def _share_halves(hs, name):
    nt = len(hs)
    n = sum(layers for _, layers in hs)

    def body(*refs):
        h_refs, send_sems, recv_sems = refs[:nt], refs[-2], refs[-1]
        x, y, c, _ = _place()
        cps = []
        for k, (_, layers) in enumerate(hs):
            rh = h_refs[k].shape[0] // (2 * layers)
            for layer in range(layers):
                half = h_refs[k].at[pl.ds((2 * layer + c) * rh, rh), :]
                cps.append(pltpu.make_async_remote_copy(
                    src_ref=half, dst_ref=half, send_sem=send_sems.at[len(cps)], recv_sem=recv_sems.at[len(cps)],
                    device_id=(x, y, 1 - c), device_id_type=MESH))
        for cp in cps:
            cp.start()
        for cp in cps:
            cp.wait()

    return _pc(
        body, name=name, in_specs=[HBM] * nt, out_specs=[HBM] * nt,
        out_shape=[jax.ShapeDtypeStruct(h.shape, F32) for h, _ in hs],
        input_output_aliases={k: k for k in range(nt)},
        scratch_shapes=[pltpu.SemaphoreType.DMA((n,)), pltpu.SemaphoreType.DMA((n,))],
    )(*[h for h, _ in hs])


def _split_start(name, bufs, ride, n, copies_of):
    nb = len(bufs)

    def body(*refs):
        sems = refs[nb + 1:nb + 1 + 2 * n]
        for cp in copies_of(refs[:nb], sems[:n], sems[n:]):
            (cp[0] if isinstance(cp, tuple) else cp).start()

    outs = _pc(
        body, name=name,
        out_shape=(pltpu.SemaphoreType.DMA(()),) * (2 * n) + tuple(pltpu.HBM(b.shape, b.dtype) for b in bufs)
        + (pltpu.HBM(ride.shape, ride.dtype),),
        in_specs=(HBM,) * (nb + 1), out_specs=(SEM,) * (2 * n) + (HBM,) * (nb + 1),
        input_output_aliases={k: 2 * n + k for k in range(nb + 1)},
        compiler_params=pltpu.CompilerParams(has_side_effects=EFFECT),
    )(*[pltpu.with_memory_space_constraint(b, pltpu.HBM) for b in bufs], pltpu.with_memory_space_constraint(ride, pltpu.HBM))
    return list(outs[:2 * n]), list(outs[2 * n:2 * n + nb]), outs[-1]


def _split_wait(name, bufs, sems, after, n, copies_of):
    nb = len(bufs)

    def body(*refs):
        s = refs[nb:nb + 2 * n]
        for cp in copies_of(refs[:nb], s[:n], s[n:]):
            sent, landed = cp if isinstance(cp, tuple) else (cp, cp)
            sent.wait_send()
            landed.wait_recv()

    outs = _pc(
        body, name=name, out_shape=tuple(pltpu.HBM(b.shape, b.dtype) for b in bufs),
        in_specs=(HBM,) * nb + (SEM,) * (2 * n) + (pl.BlockSpec(memory_space=pl.ANY),), out_specs=(HBM,) * nb,
        input_output_aliases={k: k for k in range(nb)},
        compiler_params=pltpu.CompilerParams(has_side_effects=EFFECT),
    )(*bufs, *sems, after)
    return list(outs)


def _handover_copies(refs, send, recv):
    x, y, c, chips = _place()
    cps = []
    for k, chip in enumerate(chips):
        mine, theirs = _half_block(refs[0], chip, c), _half_block(refs[0], chip, 1 - c)
        desc = lambda blk: pltpu.make_async_remote_copy(src_ref=blk, dst_ref=blk, send_sem=send[k], recv_sem=recv[k],
                                                        device_id=(x, y, 1 - c), device_id_type=MESH)
        cps.append((desc(mine), desc(theirs)))
    return cps


def _swap_copies(nt):
    def copies_of(refs, send, recv):
        x, y, c, _ = _place()
        cps = []
        for k in range(nt):
            rh = refs[k].shape[1] // 2
            cps.append(pltpu.make_async_remote_copy(
                src_ref=refs[k].at[:, pl.ds((1 - c) * rh, rh), :], dst_ref=refs[nt + k],
                send_sem=send[k], recv_sem=recv[k], device_id=(x, y, 1 - c), device_id_type=MESH))
        return cps
    return copies_of


def _exchange_copies(nt):
    def copies_of(refs, send, recv):
        x, y, c, chips = _place()
        cps = []
        for t in range(nt):
            for k, chip in enumerate(chips):
                cps.append(pltpu.make_async_remote_copy(
                    src_ref=refs[t].at[2 * chip[0] + chip[1]], dst_ref=refs[nt + t].at[k],
                    send_sem=send[3 * t + k], recv_sem=recv[3 * t + k], device_id=(*chip, c), device_id_type=MESH))
        return cps
    return copies_of


class _StagedReduce:
    def __init__(self, place):
        self.place = place
        self.groups = {}
        self.halves = {}

    @staticmethod
    def slab(t, r):
        return t.reshape(N_CHIPS, r, t.size // (N_CHIPS * r))

    def begin(self, g, grads, ride):
        names = list(grads)
        ts = [self.slab(t, r) for t, r in grads.values()]
        lands = [lax.empty((N_CHIPS, t.shape[1] // 2, t.shape[2]), F32) for t in ts]
        sems, bufs, ride = _split_start(f"grad_swap_start_{g}", ts + lands, ride, len(ts), _swap_copies(len(ts)))
        self.groups[g] = dict(names=names, bufs=bufs, sems=sems)
        return ride

    def advance(self, g, after, ride):
        st = self.groups[g]
        nt = len(st["names"])
        bufs = _split_wait(f"grad_swap_wait_{g}", st["bufs"], st["sems"], after, nt, _swap_copies(nt))
        st["ts"], st["ls"] = bufs[:nt], bufs[nt:]
        ps = [_pair_sum(t, l, self.place, f"pair_sum_{n}") for t, l, n in zip(st["ts"], st["ls"], st["names"])]
        lands = [lax.empty((3,) + p.shape[1:], BF16) for p in ps]
        st["sems"], st["bufs"], ride = _split_start(f"grad_exchange_start_{g}", ps + lands, ride, 3 * nt, _exchange_copies(nt))
        return ride

    def finish(self, g, after):
        st = self.groups[g]
        nt = len(st["names"])
        bufs = _split_wait(f"grad_exchange_wait_{g}", st["bufs"], st["sems"], after, 3 * nt, _exchange_copies(nt))
        for t, l, r, n in zip(st["ts"], st["ls"], bufs[nt:], st["names"]):
            if n[-1] in "01":
                self.halves[n[:-1]] = _final_sum(t, l, r, self.place, f"final_sum_{n}", layer=int(n[-1]), layers=2,
                                                 into=self.halves.get(n[:-1]))
            else:
                self.halves[n] = _final_sum(t, l, r, self.place, f"final_sum_{n}")


def _allgather_small(v):
    rows = v.shape[0]

    def body(v_ref, out_ref, send_sems, recv_sems):
        x, y, c, _ = _place()
        me = 4 * x + 2 * y + c
        out_ref[me] = v_ref[...]
        cps = []
        for k in range(1, 8):
            fx, fy, fc = (k >> 2) & 1, (k >> 1) & 1, k & 1
            to = (1 - x if fx else x, 1 - y if fy else y, 1 - c if fc else c)
            cps.append(pltpu.make_async_remote_copy(
                src_ref=v_ref, dst_ref=out_ref.at[me], send_sem=send_sems.at[k - 1], recv_sem=recv_sems.at[k - 1],
                device_id=to, device_id_type=MESH))
        for cp in cps:
            cp.start()
        for cp in cps:
            cp.wait()

    return _pc(
        body, name="allgather_small",
        in_specs=[pl.BlockSpec(memory_space=pltpu.VMEM)], out_specs=pl.BlockSpec(memory_space=pltpu.VMEM),
        out_shape=jax.ShapeDtypeStruct((8, rows, LANES), F32),
        scratch_shapes=[pltpu.SemaphoreType.DMA((7,)), pltpu.SemaphoreType.DMA((7,))],
    )(v)


def _adamw_math(w, g, m, v):
    m = ADAM_B1 * m + (1.0 - ADAM_B1) * g
    v = ADAM_B2 * v + (1.0 - ADAM_B2) * (g * g)
    m_hat = m / (1.0 - ADAM_B1 ** ADAM_STEP)
    v_hat = v / (1.0 - ADAM_B2 ** ADAM_STEP)
    return -ADAM_LR * (m_hat / (jnp.sqrt(v_hat) + ADAM_EPS) + ADAM_WD * w), m, v


def _adamw(w, g, m, v, name):
    r, cols = w.shape
    tr = min(r, 256)

    def body(w_ref, g_ref, m_ref, v_ref, go_ref, d_ref, mo_ref, vo_ref):
        gv = g_ref[...]
        d, mn, vn = _adamw_math(w_ref[...], gv, m_ref[...], v_ref[...])
        go_ref[...] = gv
        d_ref[...] = d
        mo_ref[...] = mn
        vo_ref[...] = vn

    row = pl.BlockSpec((tr, cols), lambda i: (i, 0))
    return _pc(
        body, name=name, grid=(r // tr,), in_specs=[row] * 4, out_specs=[row] * 4,
        out_shape=[jax.ShapeDtypeStruct((r, cols), F32)] * 4,
        compiler_params=_params(("parallel",)),
    )(w, g, m, v)


def _adamw_small(w, gathered, m, v):
    rows = w.shape[0]

    def body(w_ref, g_ref, m_ref, v_ref, go_ref, d_ref, mo_ref, vo_ref):
        g = g_ref[0]
        for k in range(1, 8):
            g = g + g_ref[k]
        d, mn, vn = _adamw_math(w_ref[...], g, m_ref[...], v_ref[...])
        go_ref[...] = g
        d_ref[...] = d
        mo_ref[...] = mn
        vo_ref[...] = vn

    return _pc(
        body, name="adamw_small",
        out_shape=[jax.ShapeDtypeStruct((rows, LANES), F32)] * 4,
    )(w, gathered, m, v)


_BIAS_ROWS = 32


def _own_slot(flat, chip):
    return lax.dynamic_update_slice(lax.empty((N_CHIPS,) + flat.shape, flat.dtype), flat[None], (chip, 0, 0))


def _pack_first(hyb_w_in, hyb_w_out):
    return jnp.concatenate([t.astype(BF16).reshape(-1, 1024) for t in (hyb_w_in, hyb_w_out)], axis=0)


def _unpack_first(g):
    return {"hyb_w_in": g[:, 0:768, :].reshape(N_CHIPS, 1024, 768), "hyb_w_out": g[:, 768:1024, :].reshape(1024, 1024)}


def _pack_rest(mlp_w_up, mlp_w_down, swa_w_qkv, swa_w_out, swa_b_qkv):
    parts = [t.astype(BF16).reshape(-1, 1024) for t in (mlp_w_up, mlp_w_down, swa_w_qkv, swa_w_out)]
    bias = lax.bitcast_convert_type(swa_b_qkv.reshape(384), BF16).reshape(1, 768)
    bias = jnp.pad(bias, ((0, _BIAS_ROWS - 1), (0, 256)))
    return jnp.concatenate(parts + [bias], axis=0)


def _unpack_rest(g):
    W = {
        "packed": g,
        "swa_w_qkv": g[:, 4096:4480, :].reshape(N_CHIPS, 1024, 384),
        "swa_w_out": g[:, 4480:4736, :].reshape(1024, 1024),
    }
    bias = lax.bitcast_convert_type(g[:, 4736, :768].reshape(N_CHIPS, 384, 2), F32).reshape(1536)
    return W, bias


_SMALL = (("norm_mix", 16), ("norm_mlp", 16), ("ret_gn_gain", 4), ("dil_q_gain", 1), ("dil_k_gain", 1),
          ("swa_b_qkv", 12), ("swa_q_gain", 1), ("swa_k_gain", 1), ("swa_sinks", 1), ("loss", 1))
_SUBLANES = 8


def _slot(r):
    return -(-r // _SUBLANES) * _SUBLANES


def _pack_small(d):
    return jnp.concatenate([jnp.pad(d[n].reshape(r, LANES), ((0, _slot(r) - r), (0, 0))) for n, r in _SMALL], axis=0)


def _unpack_small(p):
    out, o = {}, 0
    for n, r in _SMALL:
        out[n] = p[o:o + r]
        o += _slot(r)
    return out


def kernel(x, positions, norm_mix, norm_mlp, mlp_w_up, mlp_w_down, hyb_w_in, hyb_w_out, ret_gn_gain, dil_q_gain, dil_k_gain, swa_w_qkv, swa_b_qkv, swa_w_out, swa_q_gain, swa_k_gain, swa_sinks, loss_target, m_norm_mix, m_norm_mlp, m_mlp_w_up, m_mlp_w_down, m_hyb_w_in, m_hyb_w_out, m_ret_gn_gain, m_dil_q_gain, m_dil_k_gain, m_swa_w_qkv, m_swa_b_qkv, m_swa_w_out, m_swa_q_gain, m_swa_k_gain, m_swa_sinks, v_norm_mix, v_norm_mlp, v_mlp_w_up, v_mlp_w_down, v_hyb_w_in, v_hyb_w_out, v_ret_gn_gain, v_dil_q_gain, v_dil_k_gain, v_swa_w_qkv, v_swa_b_qkv, v_swa_w_out, v_swa_q_gain, v_swa_k_gain, v_swa_sinks):
    ax, ay, ac = lax.axis_index("x"), lax.axis_index("y"), lax.axis_index("c")
    chip = 2 * ax + ay
    place = jnp.stack([chip, ac]).astype(jnp.int32)
    S = x.shape[1]

    first = _own_slot(_pack_first(hyb_w_in[0], hyb_w_out[0]), chip)
    rest = _own_slot(_pack_rest(mlp_w_up, mlp_w_down, swa_w_qkv[0], swa_w_out[0], swa_b_qkv[0]), chip)
    *sems, first, pos_col = _gather_start(first, positions.reshape(S, 1), "allgather_first_start")
    flight = {}

    def first_of(after):
        g = _gather_handover(_gather_wait(first, sems, after, "allgather_first_wait"), "allgather_first_handover")
        *flight["sems"], flight["buf"], g = _gather_start(rest, g, "allgather_rest_start")
        return _unpack_first(g)

    def rest_begin(ride):
        buf = _gather_wait(flight["buf"], flight["sems"], ride, "allgather_rest_wait")
        flight["sems"], flight["bufs"], ride = _split_start("allgather_rest_handover_start", [buf], ride, 3, _handover_copies)
        return ride

    def rest_of(after):
        return _unpack_rest(_split_wait("allgather_rest_handover_wait", flight["bufs"], flight["sems"], after, 3,
                                        _handover_copies)[0])

    P = dict(norm_mix=norm_mix, norm_mlp=norm_mlp, ret_gn_gain=ret_gn_gain, dil_q_gain=dil_q_gain, dil_k_gain=dil_k_gain,
             swa_q_gain=swa_q_gain, swa_k_gain=swa_k_gain, swa_sinks=swa_sinks)

    red = _StagedReduce(place)
    loss_l, grad_x, gp = _local_step(x[0], pos_col, loss_target[0], first_of, rest_begin, rest_of, P, red)

    params = dict(mlp_w_up=(mlp_w_up, m_mlp_w_up, v_mlp_w_up), mlp_w_down=(mlp_w_down, m_mlp_w_down, v_mlp_w_down),
                  hyb_w_in=(hyb_w_in, m_hyb_w_in, v_hyb_w_in), hyb_w_out=(hyb_w_out, m_hyb_w_out, v_hyb_w_out),
                  swa_w_qkv=(swa_w_qkv, m_swa_w_qkv, v_swa_w_qkv), swa_w_out=(swa_w_out, m_swa_w_out, v_swa_w_out))
    big = {}

    def update(names, share_name):
        hs = [(red.halves[n], params[n][0].shape[0]) for n in names]
        for n, g in zip(names, _share_halves(hs, share_name)):
            rows = g.shape[0]
            w, m, v = (t.reshape(rows, -1) for t in params[n])
            big[n] = [t.reshape(params[n][0].shape) for t in _adamw(w, g, m, v, f"adamw_{n}")]

    names = ["mlp_w_up", "mlp_w_down", "hyb_w_out", "swa_w_qkv", "swa_w_out"]
    red.halves[names[0]] = red.advance("win", grad_x, red.halves[names[0]])
    update(names, "grad_share_halves")
    red.finish("win", big[names[-1]][1])
    update(["hyb_w_in"], "grad_share_last")

    gsm = dict(gp, loss=loss_l)
    gsm["swa_sinks"] = jnp.pad(gp["swa_sinks"].reshape(16, HEAD)[:, 0], (0, LANES - 16))
    gathered = _allgather_small(_pack_small(gsm))

    def small_pack(norm_mix, norm_mlp, gn, dq, dk, b, sq, sk, sinks):
        dup = lambda t: jnp.tile(t.reshape(1, HEAD), (1, 2))
        bias = lax.dynamic_update_slice(jnp.zeros((12, LANES), F32), b.reshape(3, LANES), (3 * chip, 0))
        return _pack_small(dict(norm_mix=norm_mix, norm_mlp=norm_mlp, ret_gn_gain=gn, dil_q_gain=dup(dq), dil_k_gain=dup(dk),
                                swa_b_qkv=bias, swa_q_gain=dup(sq), swa_k_gain=dup(sk),
                                swa_sinks=jnp.pad(sinks.reshape(16), (0, LANES - 16)), loss=jnp.zeros((1, LANES), F32)))

    pw = small_pack(norm_mix, norm_mlp, ret_gn_gain, dil_q_gain, dil_k_gain, swa_b_qkv, swa_q_gain, swa_k_gain, swa_sinks)
    pm = small_pack(m_norm_mix, m_norm_mlp, m_ret_gn_gain, m_dil_q_gain, m_dil_k_gain, m_swa_b_qkv, m_swa_q_gain, m_swa_k_gain, m_swa_sinks)
    pv = small_pack(v_norm_mix, v_norm_mlp, v_ret_gn_gain, v_dil_q_gain, v_dil_k_gain, v_swa_b_qkv, v_swa_q_gain, v_swa_k_gain, v_swa_sinks)
    small = [_unpack_small(t) for t in _adamw_small(pw, gathered, pm, pv)]

    def small_out(n, k):
        t = small[k][n]
        if n in ("norm_mix", "norm_mlp"):
            return t.reshape(2, D_MODEL)
        if n == "ret_gn_gain":
            return t.reshape(1, RET_HEADS, 128)
        if n == "swa_b_qkv":
            return lax.dynamic_slice(t, (3 * chip, 0), (3, LANES)).reshape(1, 384)
        if n == "swa_sinks":
            return t[0, :16].reshape(1, 16)
        return t[0, :HEAD].reshape(1, HEAD)

    order = ["norm_mix", "norm_mlp", "mlp_w_up", "mlp_w_down", "hyb_w_in", "hyb_w_out", "ret_gn_gain", "dil_q_gain",
             "dil_k_gain", "swa_w_qkv", "swa_b_qkv", "swa_w_out", "swa_q_gain", "swa_k_gain", "swa_sinks"]
    is_big = {"mlp_w_up", "mlp_w_down", "hyb_w_in", "hyb_w_out", "swa_w_qkv", "swa_w_out"}
    outs = [small[0]["loss"][0, 0], grad_x[None]]
    for k in range(4):
        outs += [big[n][k] if n in is_big else small_out(n, k) for n in order]
    return tuple(outs)
```

```python
import numpy as np
import jax
import jax.numpy as jnp
from jax import lax
from jax.experimental import pallas as pl
from jax.experimental.pallas import tpu as pltpu

F32, BF16 = jnp.float32, jnp.bfloat16
MESH = pl.DeviceIdType.MESH

LANES = 128
VMEM_LIMIT = 48 << 20
VMEM_LIMIT_WIDE = 60 << 20
D_MODEL = 1024
HEAD = 64
EPS = 1e-6
BLK = 128
RET_HEADS = 4
RET_THETA = 10000.0
ROPE_THETA = 500000.0
ROPE_DIMS = 16
DIL_PATTERNS = ((128, 1), (512, 4), (2048, 16))
SWA_DIST = 127
N_CHIPS = 4
ADAM_LR, ADAM_B1, ADAM_B2, ADAM_EPS, ADAM_WD, ADAM_STEP = 0.001, 0.9, 0.999, 1e-08, 0.01, 10

_LOG_GAMMA = [float(np.log1p(-np.exp2(np.float32(-5.0 - h)))) for h in range(RET_HEADS)]


def _pc(body, **kw):
    return pl.pallas_call(body, **kw)


def _params(sem, limit=VMEM_LIMIT):
    return pltpu.CompilerParams(dimension_semantics=sem, vmem_limit_bytes=limit)


def _matmul(a, b, *, dims, tm, tn, tk, outs, name, epilogue=None, extras=(), b_cs=False, b_rs=0, b_row0=0, b_rows=0,
            o_cs=0, a_pro=None, vmem=VMEM_LIMIT):
    if dims == "nn":
        M, K = a.shape
        N = b.shape[0] * b.shape[2] if b_cs else b.shape[1]
        a_spec = pl.BlockSpec((tm, tk), lambda i, j, k: (i, k))
        if b_cs:
            npt = b.shape[2] // tn
            b_spec = pl.BlockSpec((None, tk, tn), lambda i, j, k: (j // npt, k + b_row0, j % npt))
        elif b_rs:
            K, N, kps = b.shape[0] * b_rs, b.shape[2], b_rs // tk
            b_spec = pl.BlockSpec((None, tk, tn), lambda i, j, k: (k // kps, b_row0 + k % kps, j))
        else:
            b_spec = pl.BlockSpec((tk, tn), lambda i, j, k: (k, j))
        contract = (((1,), (0,)), ((), ()))
    elif dims == "nt":
        M, K = a.shape
        N = (b_rows or b.shape[1]) if b_cs else b.shape[0]
        a_spec = pl.BlockSpec((tm, tk), lambda i, j, k: (i, k))
        if b_cs:
            kpt = b.shape[2] // tk
            b_spec = pl.BlockSpec((None, tn, tk), lambda i, j, k: (k // kpt, j + b_row0, k % kpt))
        elif b_rs:
            N, jps = b.shape[0] * b_rs, b_rs // tn
            b_spec = pl.BlockSpec((None, tn, tk), lambda i, j, k: (j // jps, b_row0 + j % jps, k))
        else:
            b_spec = pl.BlockSpec((tn, tk), lambda i, j, k: (j, k))
        contract = (((1,), (1,)), ((), ()))
    else:
        K, M = a.shape
        N = b.shape[1]
        a_spec = pl.BlockSpec((tk, tm), lambda i, j, k: (k, i))
        b_spec = pl.BlockSpec((tk, tn), lambda i, j, k: (k, j))
        contract = (((0,), (0,)), ((), ()))
    assert M % tm == 0 and N % tn == 0 and K % tk == 0, (name, M, N, K, tm, tn, tk)
    nk = K // tk
    ex_specs = []
    for arr, kind in extras:
        if kind == "mn":
            ex_specs.append(pl.BlockSpec((tm, tn), lambda i, j, k: (i, j)))
        elif kind == "n":
            ex_specs.append(pl.BlockSpec((1, tn), lambda i, j, k: (0, j)))
        elif kind == "full":
            ex_specs.append(pl.BlockSpec(arr.shape, lambda i, j, k, nd=arr.ndim: (0,) * nd))
        else:
            ex_specs.append(pl.BlockSpec((tm, kind), lambda i, j, k: (i, 0)))
    if o_cs:
        n_sh = N // o_cs
        opt = n_sh // tn
        o_shape = (o_cs, M, n_sh)
        o_spec = pl.BlockSpec((None, tm, tn), lambda i, j, k: (j // opt, i, j % opt))
    else:
        o_shape = (M, N)
        o_spec = pl.BlockSpec((tm, tn), lambda i, j, k: (i, j))
    o_specs, o_shapes, summed = [], [], []
    for o in outs:
        if isinstance(o, tuple) and o[0] == "colsum":
            assert N == tn
            o_specs.append(pl.BlockSpec((1, tn), lambda i, j, k: (0, j)))
            o_shapes.append(jax.ShapeDtypeStruct((1, N), F32))
            summed.append(True)
        elif isinstance(o, tuple):
            o_specs.append(pl.BlockSpec((tm, o[1]), lambda i, j, k: (i, 0)))
            o_shapes.append(jax.ShapeDtypeStruct((M, o[1]), o[0]))
            summed.append(False)
        else:
            o_specs.append(o_spec)
            o_shapes.append(jax.ShapeDtypeStruct(o_shape, o))
            summed.append(False)
    n_ex, n_out = len(extras), len(outs)
    if epilogue is None:
        epilogue = lambda acc: (acc,)

    def body(a_ref, b_ref, *rest):
        ex, o_refs, acc = rest[:n_ex], rest[n_ex:n_ex + n_out], rest[-1]
        i, k = pl.program_id(0), pl.program_id(2)

        @pl.when(k == 0)
        def _():
            acc[...] = jnp.zeros_like(acc)

        av = a_ref[...] if a_pro is None else a_pro(a_ref[...])
        acc[...] += lax.dot_general(av.astype(BF16), b_ref[...].astype(BF16), contract, preferred_element_type=F32)

        @pl.when(k == nk - 1)
        def _():
            vals = epilogue(acc[...], *[e[...] for e in ex])
            for r, v, sm in zip(o_refs, vals, summed):
                if sm:
                    @pl.when(i == 0)
                    def _(r=r):
                        r[...] = jnp.zeros_like(r)

                    r[...] += v
                else:
                    r[...] = v.astype(r.dtype)

    res = _pc(
        body, name=name, grid=(M // tm, N // tn, nk),
        in_specs=[a_spec, b_spec] + ex_specs, out_specs=o_specs, out_shape=o_shapes,
        scratch_shapes=[pltpu.VMEM((tm, tn), F32)],
        compiler_params=_params(("arbitrary" if any(summed) else "parallel", "parallel", "arbitrary"), vmem),
    )(a, b, *[e for e, _ in extras])
    return res[0] if n_out == 1 else res


def _roll(x, s):
    return pltpu.roll(x, s % LANES, 1)


def _rope(x, A, B, C, half):
    return x * A + _roll(x, LANES - half) * B + _roll(x, half) * C


def _rope_t(g, A, B, C, half):
    return g * A + _roll(g * B, half) + _roll(g * C, LANES - half)


def _gmean(x, G):
    hi = x.astype(BF16)
    lo = (x - hi.astype(F32)).astype(BF16)
    Gb = G.astype(BF16)
    return jnp.dot(hi, Gb, preferred_element_type=F32) + jnp.dot(lo, Gb, preferred_element_type=F32)


def _head_mask(shape, half):
    lane = lax.broadcasted_iota(jnp.int32, shape, len(shape) - 1)
    return (lane >= HEAD) if half else (lane < HEAD)


def _group_matrix():
    i = np.arange(LANES)
    return jnp.asarray((i[:, None] // HEAD == i[None, :] // HEAD).astype(np.float32) / HEAD)


def _rope_inv():
    l = np.arange(LANES) % HEAD
    inv_r = np.power(np.float32(RET_THETA), -(l % 32).astype(np.float32) * np.float32(2.0 / HEAD))
    hp = ROPE_DIMS // 2
    inv_p = np.power(np.float32(ROPE_THETA), -(l % hp).astype(np.float32) * np.float32(2.0 / ROPE_DIMS))
    inv_p = np.where(l < ROPE_DIMS, inv_p, 0.0)
    return jnp.asarray(np.stack([inv_r, inv_p]).astype(np.float32))


def _tables(pos_col):
    S = pos_col.shape[0]
    tm = 512
    hp = ROPE_DIMS // 2

    def body(p_ref, inv_ref, o_ref):
        p = p_ref[...].astype(F32)
        lane = lax.broadcasted_iota(jnp.int32, (tm, LANES), 1) % HEAD
        ang = p * inv_ref[0:1, :]
        c, s = jnp.cos(ang), jnp.sin(ang)
        o_ref[:, 0:128] = c
        o_ref[:, 128:256] = jnp.where(lane < 32, -s, 0.0)
        o_ref[:, 256:384] = jnp.where(lane >= 32, s, 0.0)
        ang = p * inv_ref[1:2, :]
        c, s = jnp.cos(ang), jnp.sin(ang)
        o_ref[:, 384:512] = c
        o_ref[:, 512:640] = jnp.where(lane < hp, -s, 0.0)
        o_ref[:, 640:768] = jnp.where((lane >= hp) & (lane < ROPE_DIMS), s, 0.0)

    return _pc(
        body, name="rope_tables", grid=(S // tm,),
        in_specs=[pl.BlockSpec((tm, 1), lambda i: (i, 0)), pl.BlockSpec((2, LANES), lambda i: (0, 0))],
        out_specs=pl.BlockSpec((tm, 768), lambda i: (i, 0)),
        out_shape=jax.ShapeDtypeStruct((S, 768), F32),
        compiler_params=_params(("parallel",)),
    )(pos_col, _rope_inv())


def _tab(tab_ref, which):
    o = 384 * which
    return tab_ref[:, o:o + 128], tab_ref[:, o + 128:o + 256], tab_ref[:, o + 256:o + 384]


def _rms_fwd(x, g, name):
    S, Dm = x.shape
    tm = 512

    def body(x_ref, g_ref, h_ref):
        xv = x_ref[...]
        r = lax.rsqrt(jnp.mean(xv * xv, axis=-1, keepdims=True) + EPS)
        h_ref[...] = (xv * r * g_ref[...]).astype(BF16)

    return _pc(
        body, name=name, grid=(S // tm,),
        in_specs=[pl.BlockSpec((tm, Dm), lambda i: (i, 0)), pl.BlockSpec((1, Dm), lambda i: (0, 0))],
        out_specs=pl.BlockSpec((tm, Dm), lambda i: (i, 0)),
        out_shape=jax.ShapeDtypeStruct((S, Dm), BF16),
        compiler_params=_params(("parallel",)),
    )(x, g.reshape(1, Dm))


def _hn_fwd(x, gain, G):
    r = lax.rsqrt(_gmean(x * x, G) + EPS)
    return x * r * gain


def _hn_bwd(x, gain, dy, G):
    r = lax.rsqrt(_gmean(x * x, G) + EPS)
    t = dy * gain
    dx = r * t - x * (r * r * r) * _gmean(x * t, G)
    return dx, jnp.sum(dy * x * r, axis=0, keepdims=True)


def _fold_halves(v):
    return v + _roll(v, HEAD)


def _even_pre_fwd(proj, tab, qg, kg):
    S = proj.shape[0]
    tm = 512

    def body(p_ref, tab_ref, qg_ref, kg_ref, g_ref, rq_ref, rk_ref, rv_ref, dq_ref, dk_ref, dv_ref):
        Ar, Br, Cr = _tab(tab_ref, 0)
        Ap, Bp, Cp = _tab(tab_ref, 1)
        G = g_ref[...]
        for c in range(2):
            sl = slice(c * 128, (c + 1) * 128)
            rq_ref[:, sl] = _rope(p_ref[:, c * 128:(c + 1) * 128], Ar, Br, Cr, 32).astype(BF16)
            rk_ref[:, sl] = (_rope(p_ref[:, 256 + c * 128:256 + (c + 1) * 128], Ar, Br, Cr, 32) * 0.125).astype(BF16)
        rv_ref[...] = p_ref[:, 512:1024].astype(BF16)
        for c in range(4):
            sl = slice(c * 128, (c + 1) * 128)
            q = _hn_fwd(p_ref[:, 1536 + c * 128:1536 + (c + 1) * 128], qg_ref[...], G)
            dq_ref[:, sl] = _rope(q, Ap, Bp, Cp, 8).astype(BF16)
            k = _hn_fwd(p_ref[:, 2048 + c * 128:2048 + (c + 1) * 128], kg_ref[...], G)
            dk_ref[:, sl] = _rope(k, Ap, Bp, Cp, 8).astype(BF16)
        dv_ref[...] = p_ref[:, 2560:3072].astype(BF16)

    row = lambda w: pl.BlockSpec((tm, w), lambda i: (i, 0))
    vec = pl.BlockSpec((1, LANES), lambda i: (0, 0))
    return _pc(
        body, name="even_pre_fwd", grid=(S // tm,),
        in_specs=[row(3072), row(768), vec, vec, pl.BlockSpec((LANES, LANES), lambda i: (0, 0))],
        out_specs=[row(256), row(256), row(512), row(512), row(512), row(512)],
        out_shape=[jax.ShapeDtypeStruct((S, w), BF16) for w in (256, 256, 512, 512, 512, 512)],
        compiler_params=_params(("parallel",), VMEM_LIMIT_WIDE),
    )(proj, tab, qg, kg, _group_matrix())


def _even_pre_bwd(proj, tab, qg, kg, drq, drk, drv, drg, dqs, dks, dvs):
    S = proj.shape[0]
    tm = 512
    npat = len(dqs)

    def body(p_ref, tab_ref, qg_ref, kg_ref, g_ref, drq_ref, drk_ref, drv_ref, drg_ref, *rest):
        dq_refs, dk_refs, dv_refs = rest[:npat], rest[npat:2 * npat], rest[2 * npat:3 * npat]
        dp_ref, dqg_ref, dkg_ref = rest[3 * npat:]
        Ar, Br, Cr = _tab(tab_ref, 0)
        Ap, Bp, Cp = _tab(tab_ref, 1)
        G = g_ref[...]
        for c in range(2):
            sl = slice(c * 128, (c + 1) * 128)
            dp_ref[:, c * 128:(c + 1) * 128] = _rope_t(drq_ref[:, sl], Ar, Br, Cr, 32).astype(BF16)
            dp_ref[:, 256 + c * 128:256 + (c + 1) * 128] = _rope_t(drk_ref[:, sl] * 0.125, Ar, Br, Cr, 32).astype(BF16)
        dp_ref[:, 512:1024] = drv_ref[...].astype(BF16)
        dp_ref[:, 1024:1536] = drg_ref[...].astype(BF16)
        accq = jnp.zeros((1, LANES), F32)
        acck = jnp.zeros((1, LANES), F32)
        for c in range(4):
            sl = slice(c * 128, (c + 1) * 128)
            g = dq_refs[0][:, sl]
            for r in dq_refs[1:]:
                g = g + r[:, sl]
            dx, dg = _hn_bwd(p_ref[:, 1536 + c * 128:1536 + (c + 1) * 128], qg_ref[...], _rope_t(g, Ap, Bp, Cp, 8), G)
            dp_ref[:, 1536 + c * 128:1536 + (c + 1) * 128] = dx.astype(BF16)
            accq = accq + dg
            g = dk_refs[0][:, sl]
            for r in dk_refs[1:]:
                g = g + r[:, sl]
            dx, dg = _hn_bwd(p_ref[:, 2048 + c * 128:2048 + (c + 1) * 128], kg_ref[...], _rope_t(g, Ap, Bp, Cp, 8), G)
            dp_ref[:, 2048 + c * 128:2048 + (c + 1) * 128] = dx.astype(BF16)
            acck = acck + dg
        g = dv_refs[0][...]
        for r in dv_refs[1:]:
            g = g + r[...]
        dp_ref[:, 2560:3072] = g.astype(BF16)

        @pl.when(pl.program_id(0) == 0)
        def _():
            dqg_ref[...] = jnp.zeros_like(dqg_ref)
            dkg_ref[...] = jnp.zeros_like(dkg_ref)

        dqg_ref[...] += _fold_halves(accq)
        dkg_ref[...] += _fold_halves(acck)

    row = lambda w: pl.BlockSpec((tm, w), lambda i: (i, 0))
    vec = pl.BlockSpec((1, LANES), lambda i: (0, 0))
    return _pc(
        body, name="even_pre_bwd", grid=(S // tm,),
        in_specs=[row(3072), row(768), vec, vec, pl.BlockSpec((LANES, LANES), lambda i: (0, 0)),
                  row(256), row(256), row(512), row(512)] + [row(512)] * (3 * npat),
        out_specs=[row(3072), vec, vec],
        out_shape=[jax.ShapeDtypeStruct((S, 3072), BF16), jax.ShapeDtypeStruct((1, LANES), F32),
                   jax.ShapeDtypeStruct((1, LANES), F32)],
        compiler_params=_params(("arbitrary",), VMEM_LIMIT_WIDE),
    )(proj, tab, qg, kg, _group_matrix(), drq, drk, drv, drg, *dqs, *dks, *dvs)


def _ret_consts(pair, half):
    lg = jnp.where(pair == 0, _LOG_GAMMA[half], _LOG_GAMMA[2 + half]).astype(F32)
    i = lax.broadcasted_iota(jnp.int32, (BLK, BLK), 0)
    j = lax.broadcasted_iota(jnp.int32, (BLK, BLK), 1)
    diff = (i - j).astype(F32)
    decay = jnp.where(diff >= 0, jnp.exp(lg * jnp.maximum(diff, 0.0)), 0.0)
    t = lax.broadcasted_iota(jnp.int32, (BLK, 1), 0).astype(F32)
    xi = jnp.exp(lg * (t + 1.0))
    zeta = jnp.exp(lg * (BLK - 1.0 - t))
    cd = jnp.exp(jnp.full((1, 1), BLK, F32) * lg)
    return decay, xi, zeta, cd


RET_STEP = 8


def _ret_fwd(rq, rk, rv):
    S = rq.shape[0]
    nc = S // BLK
    rows = RET_STEP * BLK

    def body(q_ref, k_ref, v_ref, o_ref, st_ref, R):
        p, n = pl.program_id(0), pl.program_id(1)

        @pl.when(n == 0)
        def _():
            R[...] = jnp.zeros_like(R)

        consts = [_ret_consts(p, half) for half in range(2)]
        masks = [_head_mask((BLK, LANES), half) for half in range(2)]
        for ci in range(RET_STEP):
            rs = slice(ci * BLK, (ci + 1) * BLK)
            q2, k2 = q_ref[rs, :], k_ref[rs, :]
            for half in range(2):
                decay, xi, zeta, cd = consts[half]
                m = masks[half]
                qm = jnp.where(m, q2, jnp.zeros_like(q2))
                km = jnp.where(m, k2, jnp.zeros_like(k2))
                v = v_ref[rs, half * 128:(half + 1) * 128]
                Rb = R[half].astype(BF16)
                st_ref[ci, half] = Rb
                sc = lax.dot_general(qm, k2, (((1,), (1,)), ((), ())), preferred_element_type=F32) * decay
                o = jnp.dot(sc.astype(BF16), v, preferred_element_type=F32)
                o = o + jnp.dot(qm, Rb, preferred_element_type=F32) * xi
                o_ref[rs, half * 128:(half + 1) * 128] = o
                kz = (km.astype(F32) * zeta).astype(BF16)
                R[half] = R[half] * cd + lax.dot_general(kz, v, (((0,), (0,)), ((), ())), preferred_element_type=F32)

    return _pc(
        body, name="ret_fwd", grid=(2, nc // RET_STEP),
        in_specs=[pl.BlockSpec((rows, 128), lambda p, n: (n, p)), pl.BlockSpec((rows, 128), lambda p, n: (n, p)),
                  pl.BlockSpec((rows, 256), lambda p, n: (n, p))],
        out_specs=[pl.BlockSpec((rows, 256), lambda p, n: (n, p)),
                   pl.BlockSpec((None, RET_STEP, 2, 128, 128), lambda p, n: (p, n, 0, 0, 0))],
        out_shape=[jax.ShapeDtypeStruct((S, 512), F32), jax.ShapeDtypeStruct((2, nc, 2, 128, 128), BF16)],
        scratch_shapes=[pltpu.VMEM((2, 128, 128), F32)],
        compiler_params=_params(("parallel", "arbitrary")),
    )(rq, rk, rv)


def _ret_bwd(rq, rk, rv, states, do):
    S = rq.shape[0]
    nc = S // BLK
    ns = nc // RET_STEP
    rows = RET_STEP * BLK
    nt = (((1,), (1,)), ((), ()))
    tn = (((0,), (0,)), ((), ()))

    def body(q_ref, k_ref, v_ref, st_ref, do_ref, dq_ref, dk_ref, dv_ref, U):
        p, n = pl.program_id(0), pl.program_id(1)

        @pl.when(n == 0)
        def _():
            U[...] = jnp.zeros_like(U)

        consts = [_ret_consts(p, half) for half in range(2)]
        masks = [_head_mask((BLK, LANES), half) for half in range(2)]
        for ci in reversed(range(RET_STEP)):
            rs = slice(ci * BLK, (ci + 1) * BLK)
            q2, k2 = q_ref[rs, :], k_ref[rs, :]
            dq_acc = jnp.zeros((BLK, LANES), F32)
            dk_acc = jnp.zeros((BLK, LANES), F32)
            for half in range(2):
                decay, xi, zeta, cd = consts[half]
                m = masks[half]
                qm = jnp.where(m, q2, jnp.zeros_like(q2))
                km = jnp.where(m, k2, jnp.zeros_like(k2))
                v = v_ref[rs, half * 128:(half + 1) * 128]
                dob = do_ref[rs, half * 128:(half + 1) * 128].astype(BF16)
                Rb = st_ref[ci, half]
                Ub = U[half].astype(BF16)
                dsc = (lax.dot_general(dob, v, nt, preferred_element_type=F32) * decay).astype(BF16)
                xdo = (dob.astype(F32) * xi).astype(BF16)
                dq_acc += jnp.dot(dsc, km, preferred_element_type=F32) + lax.dot_general(xdo, Rb, nt, preferred_element_type=F32)
                dk_acc += lax.dot_general(dsc, qm, tn, preferred_element_type=F32) \
                    + lax.dot_general(v, Ub, nt, preferred_element_type=F32) * zeta
                sc = (lax.dot_general(qm, k2, nt, preferred_element_type=F32) * decay).astype(BF16)
                kz = (km.astype(F32) * zeta).astype(BF16)
                dv_ref[rs, half * 128:(half + 1) * 128] = lax.dot_general(sc, dob, tn, preferred_element_type=F32) \
                    + jnp.dot(kz, Ub, preferred_element_type=F32)
                U[half] = U[half] * cd + lax.dot_general(qm, xdo, tn, preferred_element_type=F32)
            dq_ref[rs, :] = dq_acc
            dk_ref[rs, :] = dk_acc

    rev = lambda w: pl.BlockSpec((rows, w), lambda p, n: (ns - 1 - n, p))
    return _pc(
        body, name="ret_bwd", grid=(2, ns),
        in_specs=[rev(128), rev(128), rev(256),
                  pl.BlockSpec((None, RET_STEP, 2, 128, 128), lambda p, n: (p, ns - 1 - n, 0, 0, 0)), rev(256)],
        out_specs=[rev(128), rev(128), rev(256)],
        out_shape=[jax.ShapeDtypeStruct((S, 256), F32), jax.ShapeDtypeStruct((S, 256), F32),
                   jax.ShapeDtypeStruct((S, 512), F32)],
        scratch_shapes=[pltpu.VMEM((2, 128, 128), F32)],
        compiler_params=_params(("parallel", "arbitrary")),
    )(rq, rk, rv, states, do)


ATT_TILE = 2048


def _rows(ref, start, n, r):
    if r == 1:
        return ref[pl.ds(start, n), :]
    return ref[pl.ds(start, n, stride=r), :]


def _twice(x):
    return jnp.concatenate([x, x], axis=0)


def _stack_heads(x, masks):
    zero = jnp.zeros_like(x)
    return jnp.concatenate([jnp.where(masks[0], x, zero), jnp.where(masks[1], x, zero)], axis=0)


def _set_rows(ref, start, n, r, val):
    if r == 1:
        ref[pl.ds(start, n), :] = val
    else:
        ref[pl.ds(start, n, stride=r), :] = val


def _band_geometry(S, patterns):
    rmax = max(r for _, r in patterns)
    H = BLK * rmax
    T = min(S, ATT_TILE)
    assert T % H == 0 and S % T == 0
    return H, T, S // T, T // BLK


def _band_fwd(q, k, v, *, patterns, nq, name, sinks=None, want_bf16=False):
    S, Ck = k.shape
    H, T, nt, nbt = _band_geometry(S, patterns)
    ncol = Ck // LANES
    scale = HEAD ** -0.5
    has_sink = sinks is not None
    nt_dims = (((1,), (1,)), ((), ()))

    def body(*refs):
        q_ref, kp_ref, kc_ref, vp_ref, vc_ref = refs[:5]
        sk_ref = refs[5] if has_sink else None
        n_out = 3 if want_bf16 else 2
        outs = refs[5 + has_sink:5 + has_sink + n_out]
        qf, kf, vf, M, L, A = refs[5 + has_sink + n_out:]
        t = pl.program_id(1)
        kf[0:H, :] = kp_ref[...].astype(F32)
        kf[H:H + T, :] = kc_ref[...].astype(F32)
        vf[0:H, :] = vp_ref[...].astype(F32)
        vf[H:H + T, :] = vc_ref[...].astype(F32)
        r_i = lax.broadcasted_iota(jnp.int32, (BLK, 2 * BLK), 0)
        c_i = lax.broadcasted_iota(jnp.int32, (BLK, 2 * BLK), 1)
        dist_i = r_i + BLK - c_i
        masks = [_head_mask((BLK, LANES), h) for h in range(2)]

        for i in range(nq):
            qf[...] = q_ref[:, i * 128:(i + 1) * 128].astype(F32) * scale
            for p, (dist, r) in enumerate(patterns):
                in_band = (dist_i >= 0) & (dist_i <= dist)
                in_band_first = in_band & ((c_i >= BLK) | (t > 0))
                in_band, in_band_first = _twice(in_band), _twice(in_band_first)

                def unit(j, b, p=p, r=r, in_band=in_band, in_band_first=in_band_first):
                    q0 = j + b * (BLK * r)
                    q2 = _rows(qf, q0, BLK, r).astype(BF16)
                    kcat = _rows(kf, H + q0 - BLK * r, 2 * BLK, r).astype(BF16)
                    vcat = _rows(vf, H + q0 - BLK * r, 2 * BLK, r).astype(BF16)
                    valid = in_band if b > 0 else in_band_first
                    s = lax.dot_general(_stack_heads(q2, masks), kcat, nt_dims, preferred_element_type=F32)
                    s = jnp.where(valid, s, -jnp.inf)
                    mx = jnp.max(s, axis=1, keepdims=True)
                    pr = jnp.exp(s - mx)
                    den = jnp.sum(pr, axis=1, keepdims=True)
                    pv = jnp.dot(pr.astype(BF16), vcat, preferred_element_type=F32)
                    m2 = jnp.where(masks[0], mx[:BLK], mx[BLK:])
                    l2 = jnp.where(masks[0], den[:BLK], den[BLK:])
                    a2 = jnp.where(masks[0], pv[:BLK], pv[BLK:])
                    if p > 0:
                        mo = _rows(M, q0, BLK, r)
                        mn = jnp.maximum(mo, m2)
                        wa, wb = jnp.exp(mo - mn), jnp.exp(m2 - mn)
                        l2 = wa * _rows(L, q0, BLK, r) + wb * l2
                        a2 = wa * _rows(A, q0, BLK, r) + wb * a2
                        m2 = mn
                    _set_rows(M, q0, BLK, r, m2)
                    _set_rows(L, q0, BLK, r, l2)
                    _set_rows(A, q0, BLK, r, a2)

                for u in range(nbt):
                    unit(u % r, u // r)
            sl = slice(i * 128, (i + 1) * 128)
            mm, ll, aa = M[...], L[...], A[...]
            if has_sink:
                snk = sk_ref[:, sl]
                mn = jnp.maximum(mm, snk)
                w = jnp.exp(mm - mn)
                ll = ll * w + jnp.exp(snk - mn)
                aa = aa * w
                mm = mn
            o = aa / ll
            outs[0][:, sl] = o
            outs[1][:, sl] = mm + jnp.log(ll)
            if want_bf16:
                outs[2][:, sl] = o.astype(BF16)

    th = T // H
    qspec = pl.BlockSpec((T, nq * 128), lambda j, t: (t, j))
    cur = pl.BlockSpec((T, 128), lambda j, t: (t, j))
    prev = pl.BlockSpec((H, 128), lambda j, t: (jnp.maximum(t * th - 1, 0), j))
    in_specs = [qspec, prev, cur, prev, cur]
    args = [q, k, k, v, v]
    if has_sink:
        in_specs.append(pl.BlockSpec((1, nq * 128), lambda j, t: (0, j)))
        args.append(sinks)
    out_dts = [F32, F32] + ([BF16] if want_bf16 else [])
    return _pc(
        body, name=name, grid=(ncol, nt), in_specs=in_specs,
        out_specs=[qspec] * len(out_dts),
        out_shape=[jax.ShapeDtypeStruct(q.shape, dt) for dt in out_dts],
        scratch_shapes=[pltpu.VMEM((T, LANES), F32), pltpu.VMEM((H + T, LANES), F32), pltpu.VMEM((H + T, LANES), F32),
                        pltpu.VMEM((T, LANES), F32), pltpu.VMEM((T, LANES), F32), pltpu.VMEM((T, LANES), F32)],
        compiler_params=_params(("parallel", "parallel")),
    )(*args)


def _band_bwd(q, k, v, lse, delta, do, *, patterns, nq, name, sinks=None, do_col0=0):
    S, Ck = k.shape
    H, T, nt, nbt = _band_geometry(S, patterns)
    ncol = Ck // LANES
    scale = HEAD ** -0.5
    has_sink = sinks is not None
    nt_dims = (((1,), (1,)), ((), ()))
    tn_dims = (((0,), (0,)), ((), ()))

    def body(*refs):
        (qc_ref, qn_ref, kp_ref, kc_ref, vp_ref, vc_ref, lc_ref, ln_ref, ec_ref, en_ref, dc_ref, dn_ref) = refs[:12]
        sk_ref = refs[12] if has_sink else None
        n_out = 4 if has_sink else 3
        outs = refs[12 + has_sink:12 + has_sink + n_out]
        dq_ref, dk_ref, dv_ref = outs[:3]
        qf, kf, vf, lf, ef, df = refs[12 + has_sink + n_out:]
        t = pl.program_id(1)
        kf[0:H, :] = kp_ref[...].astype(F32)
        kf[H:H + T, :] = kc_ref[...].astype(F32)
        vf[0:H, :] = vp_ref[...].astype(F32)
        vf[H:H + T, :] = vc_ref[...].astype(F32)
        dk_ref[...] = jnp.zeros_like(dk_ref)
        dv_ref[...] = jnp.zeros_like(dv_ref)
        r_i = lax.broadcasted_iota(jnp.int32, (BLK, 2 * BLK), 0)
        c_i = lax.broadcasted_iota(jnp.int32, (BLK, 2 * BLK), 1)
        dist_q = r_i + BLK - c_i
        dist_h = dist_q[:, :BLK]
        m1 = [_head_mask((BLK, LANES), h) for h in range(2)]

        def stacked_inputs(q2, do2, l2, e2):
            spread = lambda v: jnp.concatenate([jnp.where(m1[0], v, _roll(v, HEAD)), jnp.where(m1[1], v, _roll(v, HEAD))], axis=0)
            return _stack_heads(q2, m1), _stack_heads(do2.astype(BF16), m1), spread(l2), spread(e2)

        for i in range(nq):
            sl = slice(i * 128, (i + 1) * 128)
            qf[0:T, :] = qc_ref[:, sl].astype(F32) * scale
            qf[T:T + H, :] = qn_ref[:, sl].astype(F32) * scale
            for buf, c_ref, n_ref in ((lf, lc_ref, ln_ref), (ef, ec_ref, en_ref), (df, dc_ref, dn_ref)):
                buf[0:T, :] = c_ref[:, sl]
                buf[T:T + H, :] = n_ref[:, sl]
            if has_sink:
                @pl.when(t == 0)
                def _():
                    outs[3][:, sl] = jnp.zeros((1, LANES), F32)

                outs[3][:, sl] += jnp.sum(-jnp.exp(sk_ref[:, sl] - lc_ref[:, sl]) * ec_ref[:, sl], axis=0, keepdims=True)
            for p, (dist, r) in enumerate(patterns):
                band_q = (dist_q >= 0) & (dist_q <= dist)
                band_first = band_q & ((c_i >= BLK) | (t > 0))
                band_h = (dist_h >= 0) & (dist_h <= dist)
                band_q, band_first, band_h = _twice(band_q), _twice(band_first), _twice(band_h)

                def add_rows(ref, start, val, r=r):
                    _set_rows(ref, start, BLK, r, _rows(ref, start, BLK, r) + val)

                def unit(j, b, p=p, r=r, band_q=band_q, band_first=band_first):
                    q0 = j + b * (BLK * r)
                    q2 = _rows(qf, q0, BLK, r).astype(BF16)
                    do2, l2, e2 = _rows(df, q0, BLK, r), _rows(lf, q0, BLK, r), _rows(ef, q0, BLK, r)
                    kcat = _rows(kf, H + q0 - BLK * r, 2 * BLK, r).astype(BF16)
                    vcat = _rows(vf, H + q0 - BLK * r, 2 * BLK, r).astype(BF16)
                    valid = band_q if b > 0 else band_first
                    qs, dos, ls, es = stacked_inputs(q2, do2, l2, e2)
                    s = lax.dot_general(qs, kcat, nt_dims, preferred_element_type=F32)
                    pr = jnp.where(valid, jnp.exp(s - jnp.concatenate([ls, ls], axis=1)), 0.0)
                    dp = lax.dot_general(dos, vcat, nt_dims, preferred_element_type=F32)
                    ds = (pr * (dp - jnp.concatenate([es, es], axis=1))).astype(BF16)
                    dqs = jnp.dot(ds, kcat, preferred_element_type=F32) * scale
                    dq2 = jnp.where(m1[0], dqs[:BLK], dqs[BLK:])
                    dvc = lax.dot_general(pr.astype(BF16), dos, tn_dims, preferred_element_type=F32)
                    dkc = lax.dot_general(ds, qs, tn_dims, preferred_element_type=F32)
                    if p > 0:
                        dq2 = dq2 + _rows(dq_ref.at[:, sl], q0, BLK, r)
                    _set_rows(dq_ref.at[:, sl], q0, BLK, r, dq2)
                    add_rows(dk_ref, q0, dkc[BLK:])
                    add_rows(dv_ref, q0, dvc[BLK:])
                    if b > 0:
                        add_rows(dk_ref, q0 - BLK * r, dkc[:BLK])
                        add_rows(dv_ref, q0 - BLK * r, dvc[:BLK])

                def halo_unit(j, r=r, band_h=band_h):
                    k0 = j + (nbt // r - 1) * (BLK * r)
                    q2 = _rows(qf, T + j, BLK, r).astype(BF16)
                    do2, l2, e2 = _rows(df, T + j, BLK, r), _rows(lf, T + j, BLK, r), _rows(ef, T + j, BLK, r)
                    kc = _rows(kf, H + k0, BLK, r).astype(BF16)
                    vc = _rows(vf, H + k0, BLK, r).astype(BF16)
                    qs, dos, ls, es = stacked_inputs(q2, do2, l2, e2)
                    s = lax.dot_general(qs, kc, nt_dims, preferred_element_type=F32)
                    pr = jnp.where(band_h, jnp.exp(s - ls), 0.0)
                    dp = lax.dot_general(dos, vc, nt_dims, preferred_element_type=F32)
                    ds = (pr * (dp - es)).astype(BF16)
                    add_rows(dk_ref, k0, lax.dot_general(ds, qs, tn_dims, preferred_element_type=F32))
                    add_rows(dv_ref, k0, lax.dot_general(pr.astype(BF16), dos, tn_dims, preferred_element_type=F32))

                for u in range(nbt):
                    unit(u % r, u // r)
                if nt > 1:
                    @pl.when(t < nt - 1)
                    def _(r=r, halo_unit=halo_unit):
                        for j in range(r):
                            halo_unit(j)

    th = T // H
    last = S // H - 1
    qcur = pl.BlockSpec((T, nq * 128), lambda j, t: (t, j))
    qnext = pl.BlockSpec((H, nq * 128), lambda j, t: (jnp.minimum((t + 1) * th, last), j))
    cur = pl.BlockSpec((T, 128), lambda j, t: (t, j))
    prev = pl.BlockSpec((H, 128), lambda j, t: (jnp.maximum(t * th - 1, 0), j))
    dcur = pl.BlockSpec((T, nq * 128), lambda j, t: (t, j + do_col0))
    dnext = pl.BlockSpec((H, nq * 128), lambda j, t: (jnp.minimum((t + 1) * th, last), j + do_col0))
    in_specs = [qcur, qnext, prev, cur, prev, cur, qcur, qnext, qcur, qnext, dcur, dnext]
    args = [q, q, k, k, v, v, lse, lse, delta, delta, do, do]
    out_specs = [qcur, cur, cur]
    out_shape = [jax.ShapeDtypeStruct(q.shape, F32), jax.ShapeDtypeStruct(k.shape, F32), jax.ShapeDtypeStruct(k.shape, F32)]
    if has_sink:
        vec = pl.BlockSpec((1, nq * 128), lambda j, t: (0, j))
        in_specs.append(vec)
        args.append(sinks)
        out_specs.append(vec)
        out_shape.append(jax.ShapeDtypeStruct((1, q.shape[1]), F32))
    big = pltpu.VMEM((T + H, LANES), F32)
    return _pc(
        body, name=name, grid=(ncol, nt), in_specs=in_specs, out_specs=out_specs, out_shape=out_shape,
        scratch_shapes=[big] * 6,
        compiler_params=_params(("parallel", "arbitrary")),
    )(*args)


def _even_post_fwd(ro, proj, gn, da):
    S = ro.shape[0]
    tm = 512

    def body(ro_ref, rg_ref, gn_ref, da_ref, mix_ref):
        for c in range(4):
            sl = slice(c * 128, (c + 1) * 128)
            x = ro_ref[:, sl]
            mu = jnp.mean(x, axis=1, keepdims=True)
            xc = x - mu
            var = jnp.mean(xc * xc, axis=1, keepdims=True)
            y = xc * lax.rsqrt(var + EPS) * gn_ref[:, sl]
            z = rg_ref[:, sl]
            mix_ref[:, sl] = (z * jax.nn.sigmoid(z) * y).astype(BF16)
        mix_ref[:, 512:1024] = da_ref[...].astype(BF16)

    row = lambda w: pl.BlockSpec((tm, w), lambda i: (i, 0))
    return _pc(
        body, name="even_post_fwd", grid=(S // tm,),
        in_specs=[row(512), pl.BlockSpec((tm, 512), lambda i: (i, 2)), pl.BlockSpec((1, 512), lambda i: (0, 0)), row(512)],
        out_specs=row(1024), out_shape=jax.ShapeDtypeStruct((S, 1024), BF16),
        compiler_params=_params(("parallel",), VMEM_LIMIT_WIDE),
    )(ro, proj, gn, da)


def _even_post_bwd(ro, proj, gn, dmixed):
    S = ro.shape[0]
    tm = 512

    def body(ro_ref, rg_ref, gn_ref, dm_ref, dro_ref, drg_ref, dgn_ref):
        @pl.when(pl.program_id(0) == 0)
        def _():
            dgn_ref[...] = jnp.zeros_like(dgn_ref)

        for c in range(4):
            sl = slice(c * 128, (c + 1) * 128)
            x = ro_ref[:, sl]
            mu = jnp.mean(x, axis=1, keepdims=True)
            xc = x - mu
            rstd = lax.rsqrt(jnp.mean(xc * xc, axis=1, keepdims=True) + EPS)
            xh = xc * rstd
            gain = gn_ref[:, sl]
            y = xh * gain
            z = rg_ref[:, sl]
            sg = jax.nn.sigmoid(z)
            dra = dm_ref[:, sl]
            drg_ref[:, sl] = dra * y * sg * (1.0 + z * (1.0 - sg))
            dy = dra * z * sg
            dgn_ref[:, sl] += jnp.sum(dy * xh, axis=0, keepdims=True)
            dxh = dy * gain
            dro_ref[:, sl] = rstd * (dxh - jnp.mean(dxh, axis=1, keepdims=True)
                                     - xh * jnp.mean(dxh * xh, axis=1, keepdims=True))

    row = lambda w: pl.BlockSpec((tm, w), lambda i: (i, 0))
    vec = pl.BlockSpec((1, 512), lambda i: (0, 0))
    return _pc(
        body, name="even_post_bwd", grid=(S // tm,),
        in_specs=[row(512), pl.BlockSpec((tm, 512), lambda i: (i, 2)), vec, row(512)],
        out_specs=[row(512), row(512), vec],
        out_shape=[jax.ShapeDtypeStruct((S, 512), F32), jax.ShapeDtypeStruct((S, 512), F32),
                   jax.ShapeDtypeStruct((1, 512), F32)],
        compiler_params=_params(("arbitrary",), VMEM_LIMIT_WIDE),
    )(ro, proj, gn, dmixed)


def _swa_pre_fwd(proj, tab, qg, kg):
    S = proj.shape[0]
    tm = 512

    def body(p_ref, tab_ref, qg_ref, kg_ref, g_ref, q_ref, k_ref, v_ref):
        Ap, Bp, Cp = _tab(tab_ref, 1)
        G = g_ref[...]
        lo = _head_mask((tm, LANES), 0)
        for c in range(8):
            sl = slice(c * 128, (c + 1) * 128)
            q_ref[:, sl] = _rope(_hn_fwd(p_ref[:, sl], qg_ref[...], G), Ap, Bp, Cp, 8).astype(BF16)
        for c in range(2):
            kn = _rope(_hn_fwd(p_ref[:, 1024 + c * 128:1024 + (c + 1) * 128], kg_ref[...], G), Ap, Bp, Cp, 8)
            vv = p_ref[:, 1280 + c * 128:1280 + (c + 1) * 128]
            for t, ref in ((kn, k_ref), (vv, v_ref)):
                sw = _roll(t, HEAD)
                ref[:, (2 * c) * 128:(2 * c + 1) * 128] = jnp.where(lo, t, sw).astype(BF16)
                ref[:, (2 * c + 1) * 128:(2 * c + 2) * 128] = jnp.where(lo, sw, t).astype(BF16)

    row = lambda w: pl.BlockSpec((tm, w), lambda i: (i, 0))
    vec = pl.BlockSpec((1, LANES), lambda i: (0, 0))
    return _pc(
        body, name="swa_pre_fwd", grid=(S // tm,),
        in_specs=[row(1536), row(768), vec, vec, pl.BlockSpec((LANES, LANES), lambda i: (0, 0))],
        out_specs=[row(1024), row(512), row(512)],
        out_shape=[jax.ShapeDtypeStruct((S, w), BF16) for w in (1024, 512, 512)],
        compiler_params=_params(("parallel",), VMEM_LIMIT_WIDE),
    )(proj, tab, qg, kg, _group_matrix())


def _swa_pre_bwd(proj, tab, qg, kg, dq, dk, dv):
    S = proj.shape[0]
    tm = 512

    def body(p_ref, tab_ref, qg_ref, kg_ref, g_ref, dq_ref, dk_ref, dv_ref, dp_ref, db_ref, dqg_ref, dkg_ref):
        Ap, Bp, Cp = _tab(tab_ref, 1)
        G = g_ref[...]
        lo = _head_mask((tm, LANES), 0)

        @pl.when(pl.program_id(0) == 0)
        def _():
            db_ref[...] = jnp.zeros_like(db_ref)
            dqg_ref[...] = jnp.zeros_like(dqg_ref)
            dkg_ref[...] = jnp.zeros_like(dkg_ref)

        accq = jnp.zeros((1, LANES), F32)
        acck = jnp.zeros((1, LANES), F32)
        for c in range(8):
            sl = slice(c * 128, (c + 1) * 128)
            dx, dg = _hn_bwd(p_ref[:, sl], qg_ref[...], _rope_t(dq_ref[:, sl], Ap, Bp, Cp, 8), G)
            dp_ref[:, sl] = dx.astype(BF16)
            db_ref[:, sl] += jnp.sum(dx, axis=0, keepdims=True)
            accq = accq + dg
        for c in range(2):
            folded = []
            for ref in (dk_ref, dv_ref):
                a = ref[:, (2 * c) * 128:(2 * c + 1) * 128]
                b = ref[:, (2 * c + 1) * 128:(2 * c + 2) * 128]
                folded.append(jnp.where(lo, a + _roll(a, HEAD), b + _roll(b, HEAD)))
            ks = slice(1024 + c * 128, 1024 + (c + 1) * 128)
            dx, dg = _hn_bwd(p_ref[:, ks], kg_ref[...], _rope_t(folded[0], Ap, Bp, Cp, 8), G)
            dp_ref[:, ks] = dx.astype(BF16)
            db_ref[:, ks] += jnp.sum(dx, axis=0, keepdims=True)
            acck = acck + dg
            vs = slice(1280 + c * 128, 1280 + (c + 1) * 128)
            dp_ref[:, vs] = folded[1].astype(BF16)
            db_ref[:, vs] += jnp.sum(folded[1], axis=0, keepdims=True)
        dqg_ref[...] += _fold_halves(accq)
        dkg_ref[...] += _fold_halves(acck)

    row = lambda w: pl.BlockSpec((tm, w), lambda i: (i, 0))
    vec = pl.BlockSpec((1, LANES), lambda i: (0, 0))
    return _pc(
        body, name="swa_pre_bwd", grid=(S // tm,),
        in_specs=[row(1536), row(768), vec, vec, pl.BlockSpec((LANES, LANES), lambda i: (0, 0)),
                  row(1024), row(512), row(512)],
        out_specs=[row(1536), pl.BlockSpec((1, 1536), lambda i: (0, 0)), vec, vec],
        out_shape=[jax.ShapeDtypeStruct((S, 1536), BF16), jax.ShapeDtypeStruct((1, 1536), F32),
                   jax.ShapeDtypeStruct((1, LANES), F32), jax.ShapeDtypeStruct((1, LANES), F32)],
        compiler_params=_params(("arbitrary",), VMEM_LIMIT_WIDE),
    )(proj, tab, qg, kg, _group_matrix(), dq, dk, dv)


def _relu2_of(u):
    r = jnp.maximum(u.astype(F32), 0.0)
    return r * r


def _drelu2(acc, u):
    return (acc * 2.0 * jnp.maximum(u.astype(F32), 0.0),)


def _add(acc, res):
    return (acc + res,)


def _add_norm_in(res, g):
    def epilogue(acc, r, gv):
        xn = acc + r
        return xn, xn * lax.rsqrt(jnp.mean(xn * xn, axis=-1, keepdims=True) + EPS) * gv

    return dict(outs=[F32, BF16], epilogue=epilogue, extras=[(res, "mn"), (g.reshape(1, D_MODEL), "n")])


_T = dict(tm=1024, tn=1024, tk=1024)


def _rms_bwd_in(x, g, dres):
    def epilogue(dh, xv, gv, dr):
        r = lax.rsqrt(jnp.mean(xv * xv, axis=-1, keepdims=True) + EPS)
        t = dh * gv
        dx = dr + r * t - xv * (r * r * r) * jnp.mean(xv * t, axis=-1, keepdims=True)
        return dx, dx, jnp.sum(dh * xv * r, axis=0, keepdims=True)

    return dict(outs=[F32, BF16, ("colsum",)], epilogue=epilogue,
                extras=[(x, "mn"), (g.reshape(1, D_MODEL), "n"), (dres, "mn")])


def _delta_in(o, col0):
    width = D_MODEL - col0

    def epilogue(do, ov, G):
        parts = [_gmean(do[:, col0 + c * 128:col0 + (c + 1) * 128] * ov[:, c * 128:(c + 1) * 128], G) * float(HEAD)
                 for c in range(width // LANES)]
        return do, jnp.concatenate(parts, axis=1)

    return dict(outs=[F32, (F32, width)], epilogue=epilogue, extras=[(o, width), (_group_matrix(), "full")])


def _loss_in(res, target):
    def epilogue(acc, r, t):
        e = acc + r - t
        dy = e * (1.0 / D_MODEL)
        return dy, dy, jnp.sum(e * e, axis=0, keepdims=True) * (0.5 / D_MODEL)

    return dict(outs=[F32, BF16, ("colsum",)], epilogue=epilogue, extras=[(res, "mn"), (target, "mn")])


def _mlp_fwd(h, wts, layer, tag, tail):
    u = _matmul(h, wts, dims="nn", **_T, outs=[BF16], b_cs=True, b_row0=layer, name=f"mlp_up{tag}")
    out = _matmul(u, wts, dims="nn", **_T, a_pro=_relu2_of, b_rs=1024, b_row0=2 + layer, name=f"mlp_down{tag}", **tail)
    return out, (h, u)


def _mlp_bwd(x, g, wts, layer, saved, dy, dyb, tag):
    h, u = saved
    du = _matmul(dyb, wts, dims="nt", **_T, outs=[BF16], epilogue=_drelu2, extras=[(u, "mn")], b_rs=1024,
                 b_row0=2 + layer, name=f"mlp_du{tag}")
    dw_dn = _matmul(u, dyb, dims="tn", **_T, outs=[F32], a_pro=_relu2_of, name=f"mlp_dwdown{tag}")
    dw_up = _matmul(h, du, dims="tn", **_T, outs=[F32], o_cs=N_CHIPS, name=f"mlp_dwup{tag}")
    dx, dxb, dg = _matmul(du, wts, dims="nt", **_T, b_cs=True, b_row0=layer, b_rows=1024, name=f"mlp_dh{tag}",
                          vmem=VMEM_LIMIT_WIDE, **_rms_bwd_in(x, g, dy))
    return dx, dxb, dg, dw_up, dw_dn


def _local_step(x, pos_col, target, first_of, rest_begin, rest_of, P, red):
    S = x.shape[0]
    tab = _tables(pos_col)
    tile2 = lambda g: jnp.tile(g.reshape(1, HEAD), (1, 2))
    dqg, dkg = tile2(P["dil_q_gain"]), tile2(P["dil_k_gain"])
    sqg, skg = tile2(P["swa_q_gain"]), tile2(P["swa_k_gain"])
    gn = P["ret_gn_gain"].reshape(1, 512)
    sink_b = jnp.repeat(P["swa_sinks"].reshape(16), HEAD).reshape(1, 1024)

    h0 = _rms_fwd(x, P["norm_mix"][0], "rms_mix_fwd0")
    W = first_of(h0)
    proj = _matmul(h0, W["hyb_w_in"], dims="nn", tm=1024, tn=768, tk=1024, outs=[F32], b_cs=True, name="hyb_in")
    rq, rk, rv, dq, dk, dv = _even_pre_fwd(proj, tab, dqg, dkg)
    ro, states = _ret_fwd(rq, rk, rv)
    dil = [(w // r, r) for w, r in DIL_PATTERNS]
    da, dlse = _band_fwd(dq, dk, dv, patterns=dil, nq=1, name="dil_fwd")
    mixed = rest_begin(_even_post_fwd(ro, proj, gn, da))
    x1, h1 = _matmul(mixed, W["hyb_w_out"], dims="nn", **_T, name="hyb_out", **_add_norm_in(x, P["norm_mlp"][0]))
    rest, bias = rest_of(x1)
    W = {**W, **rest}
    (x2, h2), mlp0 = _mlp_fwd(h1, W["packed"], 0, "0", _add_norm_in(x1, P["norm_mix"][1]))

    proj2 = _matmul(h2, W["swa_w_qkv"], dims="nn", tm=1024, tn=384, tk=1024, outs=[F32], b_cs=True,
                    epilogue=_add, extras=[(bias.reshape(1, 1536), "n")], name="swa_qkv")
    sq, sk, sv = _swa_pre_fwd(proj2, tab, sqg, skg)
    swa = [(SWA_DIST, 1)]
    so, slse, so_b = _band_fwd(sq, sk, sv, patterns=swa, nq=2, name="swa_fwd", sinks=sink_b, want_bf16=True)
    x3, h3 = _matmul(so_b, W["swa_w_out"], dims="nn", **_T, name="swa_out", **_add_norm_in(x2, P["norm_mlp"][1]))
    (dy, dyb, loss_cols), mlp1 = _mlp_fwd(h3, W["packed"], 1, "1", _loss_in(x3, target))
    loss = jnp.broadcast_to(jnp.sum(loss_cols), (1, LANES))

    gw, gp = {}, {}
    dx3, dx3b, dg_mlp1, gw["mlp_w_up1"], gw["mlp_w_down1"] = _mlp_bwd(x3, P["norm_mlp"][1], W["packed"], 1, mlp1, dy, dyb, "1")
    dx3b = red.begin("mlp1", {n: (gw[n], 1024) for n in ("mlp_w_up1", "mlp_w_down1")}, dx3b)
    gw["swa_w_out"] = _matmul(so_b, dx3b, dims="tn", **_T, outs=[F32], name="swa_dwout")
    dso, sdelta = _matmul(dx3b, W["swa_w_out"], dims="nt", **_T, name="swa_do", vmem=VMEM_LIMIT_WIDE, **_delta_in(so, 0))
    dsq, dsk, dsv, dsink = _band_bwd(sq, sk, sv, slse, sdelta, dso, patterns=swa, nq=2, name="swa_bwd", sinks=sink_b)
    dproj2, gp["swa_b_qkv"], gp["swa_q_gain"], gp["swa_k_gain"] = _swa_pre_bwd(proj2, tab, sqg, skg, dsq, dsk, dsv)
    gp["swa_sinks"] = dsink
    gw["swa_w_qkv"] = _matmul(h2, dproj2, dims="tn", tm=1024, tn=384, tk=1024, outs=[F32], o_cs=N_CHIPS, name="swa_dwqkv")
    dx2, dx2b, dg_mix1 = _matmul(dproj2, W["swa_w_qkv"], dims="nt", tm=1024, tn=1024, tk=384, b_cs=True, name="swa_dh",
                                 vmem=VMEM_LIMIT_WIDE, **_rms_bwd_in(x2, P["norm_mix"][1], dx3))
    dx2b = red.begin("swa", {"swa_w_qkv": (gw["swa_w_qkv"], 1024), "swa_w_out": (gw["swa_w_out"], 256)}, dx2b)
    dx2b = red.advance("mlp1", dx2b, dx2b)

    dx1, dx1b, dg_mlp0, gw["mlp_w_up0"], gw["mlp_w_down0"] = _mlp_bwd(x1, P["norm_mlp"][0], W["packed"], 0, mlp0, dx2, dx2b, "0")
    gw["hyb_w_out"] = _matmul(mixed, dx1b, dims="tn", **_T, outs=[F32], name="hyb_dwout")
    dx1b = red.begin("mlp0", {"mlp_w_up0": (gw["mlp_w_up0"], 1024), "mlp_w_down0": (gw["mlp_w_down0"], 1024),
                              "hyb_w_out": (gw["hyb_w_out"], 256)}, dx1b)
    dx1b = red.advance("swa", dx1b, dx1b)
    red.finish("mlp1", dx1b)
    dmixed, ddelta = _matmul(dx1b, W["hyb_w_out"], dims="nt", **_T, name="hyb_dmixed", vmem=VMEM_LIMIT_WIDE,
                             **_delta_in(da, 512))
    dro, drg, gp["ret_gn_gain"] = _even_post_bwd(ro, proj, gn, dmixed)
    drq, drk, drv = _ret_bwd(rq, rk, rv, states, dro)
    ddq, ddk, ddv = _band_bwd(dq, dk, dv, dlse, ddelta, dmixed, patterns=dil, nq=1, name="dil_bwd", do_col0=4)
    ddq = red.advance("mlp0", ddq, ddq)
    red.finish("swa", ddq)
    dproj, gp["dil_q_gain"], gp["dil_k_gain"] = _even_pre_bwd(proj, tab, dqg, dkg, drq, drk, drv, drg, [ddq], [ddk], [ddv])
    gw["hyb_w_in"] = _matmul(h0, dproj, dims="tn", tm=1024, tn=768, tk=1024, outs=[F32], o_cs=N_CHIPS, name="hyb_dwin")
    dproj = red.begin("win", {"hyb_w_in": (gw["hyb_w_in"], 1024)}, dproj)
    grad_x, _, dg_mix0 = _matmul(dproj, W["hyb_w_in"], dims="nt", tm=1024, tn=1024, tk=768, b_cs=True, name="hyb_dh",
                                 vmem=VMEM_LIMIT_WIDE, **_rms_bwd_in(x, P["norm_mix"][0], dx1))
    red.finish("mlp0", grad_x)
    gp["norm_mix"] = jnp.concatenate([dg_mix0, dg_mix1], axis=0)
    gp["norm_mlp"] = jnp.concatenate([dg_mlp0, dg_mlp1], axis=0)
    return loss, grad_x, gp


HBM = pl.BlockSpec(memory_space=pltpu.HBM)


def _place():
    x, y, c = lax.axis_index("x"), lax.axis_index("y"), lax.axis_index("c")
    chips = [(1 - x, y), (x, 1 - y), (1 - x, 1 - y)]
    return x, y, c, chips


SEM = pl.BlockSpec(memory_space=pltpu.SEMAPHORE)
EFFECT = pltpu.SideEffectType.DATAFLOW_SIDE_EFFECTING


def _half_block(ref, chip, core):
    rh = ref.shape[1] // 2
    return ref.at[2 * chip[0] + chip[1], pl.ds(core * rh, rh), :]


def _gather_start(buf, ride, name):
    def body(b_ref, ride_ref, s0, s1, s2, r0, r1, r2, b_out, ride_out):
        x, y, c, chips = _place()
        for chip, s, r in zip(chips, (s0, s1, s2), (r0, r1, r2)):
            mine = _half_block(b_ref, (x, y), c)
            pltpu.make_async_remote_copy(src_ref=mine, dst_ref=mine, send_sem=s, recv_sem=r,
                                         device_id=(*chip, c), device_id_type=MESH).start()

    sem = pltpu.SemaphoreType.DMA(())
    return _pc(
        body, name=name,
        out_shape=(sem,) * 6 + (pltpu.HBM(buf.shape, buf.dtype), pltpu.HBM(ride.shape, ride.dtype)),
        in_specs=(HBM, HBM), out_specs=(SEM,) * 6 + (HBM, HBM), input_output_aliases={0: 6, 1: 7},
        compiler_params=pltpu.CompilerParams(has_side_effects=EFFECT),
    )(pltpu.with_memory_space_constraint(buf, pltpu.HBM), pltpu.with_memory_space_constraint(ride, pltpu.HBM))


def _gather_wait(buf, sems, after, name):
    def body(b_ref, s0, s1, s2, r0, r1, r2, after_ref, b_out):
        x, y, c, chips = _place()
        for chip, s, r in zip(chips, (s0, s1, s2), (r0, r1, r2)):
            cp = pltpu.make_async_remote_copy(src_ref=_half_block(b_ref, (x, y), c), dst_ref=_half_block(b_ref, chip, c),
                                              send_sem=s, recv_sem=r, device_id=(*chip, c), device_id_type=MESH)
            cp.wait_send()
            cp.wait_recv()

    return _pc(
        body, name=name, out_shape=pltpu.HBM(buf.shape, buf.dtype),
        in_specs=(HBM,) + (SEM,) * 6 + (pl.BlockSpec(memory_space=pl.ANY),), out_specs=HBM, input_output_aliases={0: 0},
        compiler_params=pltpu.CompilerParams(has_side_effects=EFFECT),
    )(buf, *sems, after)


def _gather_handover(buf, name):
    def body(b_ref, out_ref, send_sems, recv_sems):
        x, y, c, chips = _place()
        cps = []
        for k, chip in enumerate(chips):
            mine = _half_block(b_ref, chip, c)
            cps.append(pltpu.make_async_remote_copy(src_ref=mine, dst_ref=mine, send_sem=send_sems.at[k],
                                                    recv_sem=recv_sems.at[k], device_id=(x, y, 1 - c), device_id_type=MESH))
        for cp in cps:
            cp.start()
        for k, chip in enumerate(chips):
            theirs = _half_block(b_ref, chip, 1 - c)
            pltpu.make_async_remote_copy(src_ref=theirs, dst_ref=theirs, send_sem=send_sems.at[k], recv_sem=recv_sems.at[k],
                                         device_id=(x, y, 1 - c), device_id_type=MESH).wait_recv()
        for cp in cps:
            cp.wait_send()

    return _pc(
        body, name=name, in_specs=[HBM], out_specs=HBM,
        out_shape=jax.ShapeDtypeStruct(buf.shape, buf.dtype), input_output_aliases={0: 0},
        scratch_shapes=[pltpu.SemaphoreType.DMA((3,)), pltpu.SemaphoreType.DMA((3,))],
    )(buf)


def _pair_sum(t, l, place, name):
    _, r, cols = t.shape
    rh = r // 2
    tr = min(rh, 256)
    nr = rh // tr

    def body(pl_ref, t_ref, l_ref, o_ref):
        o_ref[...] = (t_ref[...] + l_ref[...]).astype(BF16)

    other = lambda s, p: s + jnp.where(s >= p[0], 1, 0)
    return _pc(
        body, name=name,
        grid_spec=pltpu.PrefetchScalarGridSpec(
            num_scalar_prefetch=1, grid=(N_CHIPS - 1, nr),
            in_specs=[pl.BlockSpec((None, tr, cols), lambda s, i, p: (other(s, p), p[1] * nr + i, 0)),
                      pl.BlockSpec((None, tr, cols), lambda s, i, p: (other(s, p), i, 0))],
            out_specs=pl.BlockSpec((None, tr, cols), lambda s, i, p: (other(s, p), i, 0))),
        out_shape=jax.ShapeDtypeStruct((N_CHIPS, rh, cols), BF16),
        compiler_params=_params(("parallel", "parallel")),
    )(place, t, l)


def _final_sum(t, l, rcv, place, name, layer=0, layers=1, into=None):
    _, r, cols = t.shape
    rh = r // 2
    tr = min(rh, 256)
    nr = rh // tr

    def body(pl_ref, t_ref, l_ref, r_ref, *rest):
        acc = t_ref[...] + l_ref[...]
        for k in range(3):
            acc = acc + r_ref[k].astype(F32)
        rest[-1][...] = acc

    in_specs = [pl.BlockSpec((None, tr, cols), lambda i, p: (p[0], p[1] * nr + i, 0)),
                pl.BlockSpec((None, tr, cols), lambda i, p: (p[0], i, 0)),
                pl.BlockSpec((3, tr, cols), lambda i, p: (0, i, 0))]
    args = [place, t, l, rcv]
    aliases = {}
    if into is not None:
        in_specs.append(pl.BlockSpec(memory_space=pl.ANY))
        args.append(into)
        aliases = {4: 0}
    return _pc(
        body, name=name,
        grid_spec=pltpu.PrefetchScalarGridSpec(
            num_scalar_prefetch=1, grid=(nr,), in_specs=in_specs,
            out_specs=pl.BlockSpec((tr, cols), lambda i, p: (2 * nr * layer + p[1] * nr + i, 0))),
        out_shape=jax.ShapeDtypeStruct((layers * r, cols), F32), input_output_aliases=aliases,
        compiler_params=_params(("parallel",)),
    )(*args)


def _share_halves(hs, name):
    nt = len(hs)
    n = sum(layers for _, layers in hs)

    def body(*refs):
        h_refs, send_sems, recv_sems = refs[:nt], refs[-2], refs[-1]
        x, y, c, _ = _place()
        cps = []
        for k, (_, layers) in enumerate(hs):
            rh = h_refs[k].shape[0] // (2 * layers)
            for layer in range(layers):
                half = h_refs[k].at[pl.ds((2 * layer + c) * rh, rh), :]
                cps.append(pltpu.make_async_remote_copy(
                    src_ref=half, dst_ref=half, send_sem=send_sems.at[len(cps)], recv_sem=recv_sems.at[len(cps)],
                    device_id=(x, y, 1 - c), device_id_type=MESH))
        for cp in cps:
            cp.start()
        for cp in cps:
            cp.wait()

    return _pc(
        body, name=name, in_specs=[HBM] * nt, out_specs=[HBM] * nt,
        out_shape=[jax.ShapeDtypeStruct(h.shape, F32) for h, _ in hs],
        input_output_aliases={k: k for k in range(nt)},
        scratch_shapes=[pltpu.SemaphoreType.DMA((n,)), pltpu.SemaphoreType.DMA((n,))],
    )(*[h for h, _ in hs])


def _split_start(name, bufs, ride, n, copies_of):
    nb = len(bufs)

    def body(*refs):
        sems = refs[nb + 1:nb + 1 + 2 * n]
        for cp in copies_of(refs[:nb], sems[:n], sems[n:]):
            (cp[0] if isinstance(cp, tuple) else cp).start()

    outs = _pc(
        body, name=name,
        out_shape=(pltpu.SemaphoreType.DMA(()),) * (2 * n) + tuple(pltpu.HBM(b.shape, b.dtype) for b in bufs)
        + (pltpu.HBM(ride.shape, ride.dtype),),
        in_specs=(HBM,) * (nb + 1), out_specs=(SEM,) * (2 * n) + (HBM,) * (nb + 1),
        input_output_aliases={k: 2 * n + k for k in range(nb + 1)},
        compiler_params=pltpu.CompilerParams(has_side_effects=EFFECT),
    )(*[pltpu.with_memory_space_constraint(b, pltpu.HBM) for b in bufs], pltpu.with_memory_space_constraint(ride, pltpu.HBM))
    return list(outs[:2 * n]), list(outs[2 * n:2 * n + nb]), outs[-1]


def _split_wait(name, bufs, sems, after, n, copies_of):
    nb = len(bufs)

    def body(*refs):
        s = refs[nb:nb + 2 * n]
        for cp in copies_of(refs[:nb], s[:n], s[n:]):
            sent, landed = cp if isinstance(cp, tuple) else (cp, cp)
            sent.wait_send()
            landed.wait_recv()

    outs = _pc(
        body, name=name, out_shape=tuple(pltpu.HBM(b.shape, b.dtype) for b in bufs),
        in_specs=(HBM,) * nb + (SEM,) * (2 * n) + (pl.BlockSpec(memory_space=pl.ANY),), out_specs=(HBM,) * nb,
        input_output_aliases={k: k for k in range(nb)},
        compiler_params=pltpu.CompilerParams(has_side_effects=EFFECT),
    )(*bufs, *sems, after)
    return list(outs)


def _handover_copies(refs, send, recv):
    x, y, c, chips = _place()
    cps = []
    for k, chip in enumerate(chips):
        mine, theirs = _half_block(refs[0], chip, c), _half_block(refs[0], chip, 1 - c)
        desc = lambda blk: pltpu.make_async_remote_copy(src_ref=blk, dst_ref=blk, send_sem=send[k], recv_sem=recv[k],
                                                        device_id=(x, y, 1 - c), device_id_type=MESH)
        cps.append((desc(mine), desc(theirs)))
    return cps


def _share_copies(layers_of):
    def copies_of(refs, send, recv):
        x, y, c, _ = _place()
        cps = []
        for k, layers in enumerate(layers_of):
            rh = refs[k].shape[0] // (2 * layers)
            for layer in range(layers):
                i = len(cps)
                desc = lambda half: pltpu.make_async_remote_copy(
                    src_ref=refs[k].at[pl.ds((2 * layer + half) * rh, rh), :], dst_ref=refs[k].at[pl.ds((2 * layer + half) * rh, rh), :],
                    send_sem=send[i], recv_sem=recv[i], device_id=(x, y, 1 - c), device_id_type=MESH)
                cps.append((desc(c), desc(1 - c)))
        return cps
    return copies_of


def _swap_copies(nt):
    def copies_of(refs, send, recv):
        x, y, c, _ = _place()
        cps = []
        for k in range(nt):
            rh = refs[k].shape[1] // 2
            cps.append(pltpu.make_async_remote_copy(
                src_ref=refs[k].at[:, pl.ds((1 - c) * rh, rh), :], dst_ref=refs[nt + k],
                send_sem=send[k], recv_sem=recv[k], device_id=(x, y, 1 - c), device_id_type=MESH))
        return cps
    return copies_of


def _exchange_copies(nt):
    def copies_of(refs, send, recv):
        x, y, c, chips = _place()
        cps = []
        for t in range(nt):
            for k, chip in enumerate(chips):
                cps.append(pltpu.make_async_remote_copy(
                    src_ref=refs[t].at[2 * chip[0] + chip[1]], dst_ref=refs[nt + t].at[k],
                    send_sem=send[3 * t + k], recv_sem=recv[3 * t + k], device_id=(*chip, c), device_id_type=MESH))
        return cps
    return copies_of


class _StagedReduce:
    def __init__(self, place):
        self.place = place
        self.groups = {}
        self.halves = {}

    @staticmethod
    def slab(t, r):
        return t.reshape(N_CHIPS, r, t.size // (N_CHIPS * r))

    def begin(self, g, grads, ride):
        names = list(grads)
        ts = [self.slab(t, r) for t, r in grads.values()]
        lands = [lax.empty((N_CHIPS, t.shape[1] // 2, t.shape[2]), F32) for t in ts]
        sems, bufs, ride = _split_start(f"grad_swap_start_{g}", ts + lands, ride, len(ts), _swap_copies(len(ts)))
        self.groups[g] = dict(names=names, bufs=bufs, sems=sems)
        return ride

    def advance(self, g, after, ride):
        st = self.groups[g]
        nt = len(st["names"])
        bufs = _split_wait(f"grad_swap_wait_{g}", st["bufs"], st["sems"], after, nt, _swap_copies(nt))
        st["ts"], st["ls"] = bufs[:nt], bufs[nt:]
        ps = [_pair_sum(t, l, self.place, f"pair_sum_{n}") for t, l, n in zip(st["ts"], st["ls"], st["names"])]
        lands = [lax.empty((3,) + p.shape[1:], BF16) for p in ps]
        st["sems"], st["bufs"], ride = _split_start(f"grad_exchange_start_{g}", ps + lands, ride, 3 * nt, _exchange_copies(nt))
        return ride

    def finish(self, g, after):
        st = self.groups[g]
        nt = len(st["names"])
        bufs = _split_wait(f"grad_exchange_wait_{g}", st["bufs"], st["sems"], after, 3 * nt, _exchange_copies(nt))
        for t, l, r, n in zip(st["ts"], st["ls"], bufs[nt:], st["names"]):
            if n[-1] in "01":
                self.halves[n[:-1]] = _final_sum(t, l, r, self.place, f"final_sum_{n}", layer=int(n[-1]), layers=2,
                                                 into=self.halves.get(n[:-1]))
            else:
                self.halves[n] = _final_sum(t, l, r, self.place, f"final_sum_{n}")


def _allgather_small(v):
    rows = v.shape[0]

    def body(v_ref, out_ref, send_sems, recv_sems):
        x, y, c, _ = _place()
        me = 4 * x + 2 * y + c
        out_ref[me] = v_ref[...]
        cps = []
        for k in range(1, 8):
            fx, fy, fc = (k >> 2) & 1, (k >> 1) & 1, k & 1
            to = (1 - x if fx else x, 1 - y if fy else y, 1 - c if fc else c)
            cps.append(pltpu.make_async_remote_copy(
                src_ref=v_ref, dst_ref=out_ref.at[me], send_sem=send_sems.at[k - 1], recv_sem=recv_sems.at[k - 1],
                device_id=to, device_id_type=MESH))
        for cp in cps:
            cp.start()
        for cp in cps:
            cp.wait()

    return _pc(
        body, name="allgather_small",
        in_specs=[pl.BlockSpec(memory_space=pltpu.VMEM)], out_specs=pl.BlockSpec(memory_space=pltpu.VMEM),
        out_shape=jax.ShapeDtypeStruct((8, rows, LANES), F32),
        scratch_shapes=[pltpu.SemaphoreType.DMA((7,)), pltpu.SemaphoreType.DMA((7,))],
    )(v)


def _adamw_math(w, g, m, v):
    m = ADAM_B1 * m + (1.0 - ADAM_B1) * g
    v = ADAM_B2 * v + (1.0 - ADAM_B2) * (g * g)
    m_hat = m / (1.0 - ADAM_B1 ** ADAM_STEP)
    v_hat = v / (1.0 - ADAM_B2 ** ADAM_STEP)
    return -ADAM_LR * (m_hat / (jnp.sqrt(v_hat) + ADAM_EPS) + ADAM_WD * w), m, v


def _adamw(w, g, m, v, name):
    r, cols = w.shape
    tr = min(r, 256)

    def body(w_ref, g_ref, m_ref, v_ref, go_ref, d_ref, mo_ref, vo_ref):
        gv = g_ref[...]
        d, mn, vn = _adamw_math(w_ref[...], gv, m_ref[...], v_ref[...])
        go_ref[...] = gv
        d_ref[...] = d
        mo_ref[...] = mn
        vo_ref[...] = vn

    row = pl.BlockSpec((tr, cols), lambda i: (i, 0))
    return _pc(
        body, name=name, grid=(r // tr,), in_specs=[row] * 4, out_specs=[row] * 4,
        out_shape=[jax.ShapeDtypeStruct((r, cols), F32)] * 4,
        compiler_params=_params(("parallel",)),
    )(w, g, m, v)


def _adamw_small(w, gathered, m, v):
    rows = w.shape[0]

    def body(w_ref, g_ref, m_ref, v_ref, go_ref, d_ref, mo_ref, vo_ref):
        g = g_ref[0]
        for k in range(1, 8):
            g = g + g_ref[k]
        d, mn, vn = _adamw_math(w_ref[...], g, m_ref[...], v_ref[...])
        go_ref[...] = g
        d_ref[...] = d
        mo_ref[...] = mn
        vo_ref[...] = vn

    return _pc(
        body, name="adamw_small",
        out_shape=[jax.ShapeDtypeStruct((rows, LANES), F32)] * 4,
    )(w, gathered, m, v)


_BIAS_ROWS = 32


def _own_slot(flat, chip):
    return lax.dynamic_update_slice(lax.empty((N_CHIPS,) + flat.shape, flat.dtype), flat[None], (chip, 0, 0))


def _pack_first(hyb_w_in, hyb_w_out):
    return jnp.concatenate([t.astype(BF16).reshape(-1, 1024) for t in (hyb_w_in, hyb_w_out)], axis=0)


def _unpack_first(g):
    return {"hyb_w_in": g[:, 0:768, :].reshape(N_CHIPS, 1024, 768), "hyb_w_out": g[:, 768:1024, :].reshape(1024, 1024)}


def _pack_rest(mlp_w_up, mlp_w_down, swa_w_qkv, swa_w_out, swa_b_qkv):
    parts = [t.astype(BF16).reshape(-1, 1024) for t in (mlp_w_up, mlp_w_down, swa_w_qkv, swa_w_out)]
    bias = lax.bitcast_convert_type(swa_b_qkv.reshape(384), BF16).reshape(1, 768)
    bias = jnp.pad(bias, ((0, _BIAS_ROWS - 1), (0, 256)))
    return jnp.concatenate(parts + [bias], axis=0)


def _unpack_rest(g):
    W = {
        "packed": g,
        "swa_w_qkv": g[:, 4096:4480, :].reshape(N_CHIPS, 1024, 384),
        "swa_w_out": g[:, 4480:4736, :].reshape(1024, 1024),
    }
    bias = lax.bitcast_convert_type(g[:, 4736, :768].reshape(N_CHIPS, 384, 2), F32).reshape(1536)
    return W, bias


_SMALL = (("norm_mix", 16), ("norm_mlp", 16), ("ret_gn_gain", 4), ("dil_q_gain", 1), ("dil_k_gain", 1),
          ("swa_b_qkv", 12), ("swa_q_gain", 1), ("swa_k_gain", 1), ("swa_sinks", 1), ("loss", 1))
_SUBLANES = 8


def _slot(r):
    return -(-r // _SUBLANES) * _SUBLANES


def _pack_small(d):
    return jnp.concatenate([jnp.pad(d[n].reshape(r, LANES), ((0, _slot(r) - r), (0, 0))) for n, r in _SMALL], axis=0)


def _unpack_small(p):
    out, o = {}, 0
    for n, r in _SMALL:
        out[n] = p[o:o + r]
        o += _slot(r)
    return out


def kernel(x, positions, norm_mix, norm_mlp, mlp_w_up, mlp_w_down, hyb_w_in, hyb_w_out, ret_gn_gain, dil_q_gain, dil_k_gain, swa_w_qkv, swa_b_qkv, swa_w_out, swa_q_gain, swa_k_gain, swa_sinks, loss_target, m_norm_mix, m_norm_mlp, m_mlp_w_up, m_mlp_w_down, m_hyb_w_in, m_hyb_w_out, m_ret_gn_gain, m_dil_q_gain, m_dil_k_gain, m_swa_w_qkv, m_swa_b_qkv, m_swa_w_out, m_swa_q_gain, m_swa_k_gain, m_swa_sinks, v_norm_mix, v_norm_mlp, v_mlp_w_up, v_mlp_w_down, v_hyb_w_in, v_hyb_w_out, v_ret_gn_gain, v_dil_q_gain, v_dil_k_gain, v_swa_w_qkv, v_swa_b_qkv, v_swa_w_out, v_swa_q_gain, v_swa_k_gain, v_swa_sinks):
    ax, ay, ac = lax.axis_index("x"), lax.axis_index("y"), lax.axis_index("c")
    chip = 2 * ax + ay
    place = jnp.stack([chip, ac]).astype(jnp.int32)
    S = x.shape[1]

    first = _own_slot(_pack_first(hyb_w_in[0], hyb_w_out[0]), chip)
    rest = _own_slot(_pack_rest(mlp_w_up, mlp_w_down, swa_w_qkv[0], swa_w_out[0], swa_b_qkv[0]), chip)
    *sems, first, pos_col = _gather_start(first, positions.reshape(S, 1), "allgather_first_start")
    flight = {}

    def first_of(after):
        g = _gather_handover(_gather_wait(first, sems, after, "allgather_first_wait"), "allgather_first_handover")
        *flight["sems"], flight["buf"], g = _gather_start(rest, g, "allgather_rest_start")
        return _unpack_first(g)

    def rest_begin(ride):
        buf = _gather_wait(flight["buf"], flight["sems"], ride, "allgather_rest_wait")
        flight["sems"], flight["bufs"], ride = _split_start("allgather_rest_handover_start", [buf], ride, 3, _handover_copies)
        return ride

    def rest_of(after):
        return _unpack_rest(_split_wait("allgather_rest_handover_wait", flight["bufs"], flight["sems"], after, 3,
                                        _handover_copies)[0])

    P = dict(norm_mix=norm_mix, norm_mlp=norm_mlp, ret_gn_gain=ret_gn_gain, dil_q_gain=dil_q_gain, dil_k_gain=dil_k_gain,
             swa_q_gain=swa_q_gain, swa_k_gain=swa_k_gain, swa_sinks=swa_sinks)

    red = _StagedReduce(place)
    loss_l, grad_x, gp = _local_step(x[0], pos_col, loss_target[0], first_of, rest_begin, rest_of, P, red)

    params = dict(mlp_w_up=(mlp_w_up, m_mlp_w_up, v_mlp_w_up), mlp_w_down=(mlp_w_down, m_mlp_w_down, v_mlp_w_down),
                  hyb_w_in=(hyb_w_in, m_hyb_w_in, v_hyb_w_in), hyb_w_out=(hyb_w_out, m_hyb_w_out, v_hyb_w_out),
                  swa_w_qkv=(swa_w_qkv, m_swa_w_qkv, v_swa_w_qkv), swa_w_out=(swa_w_out, m_swa_w_out, v_swa_w_out))
    big = {}

    def adamw_of(n, g):
        rows = g.shape[0]
        w, m, v = (t.reshape(rows, -1) for t in params[n])
        big[n] = [t.reshape(params[n][0].shape) for t in _adamw(w, g, m, v, f"adamw_{n}")]

    names = ["mlp_w_up", "mlp_w_down", "hyb_w_out", "swa_w_qkv", "swa_w_out"]
    red.halves[names[0]] = red.advance("win", grad_x, red.halves[names[0]])

    gsm = dict(gp, loss=loss_l)
    gsm["swa_sinks"] = jnp.pad(gp["swa_sinks"].reshape(16, HEAD)[:, 0], (0, LANES - 16))
    layers = [params[n][0].shape[0] for n in names]
    sems, shared, packed = _split_start("grad_share_start", [red.halves[n] for n in names], _pack_small(gsm), sum(layers),
                                        _share_copies(layers))
    gathered = _allgather_small(packed)

    def small_pack(norm_mix, norm_mlp, gn, dq, dk, b, sq, sk, sinks):
        dup = lambda t: jnp.tile(t.reshape(1, HEAD), (1, 2))
        bias = lax.dynamic_update_slice(jnp.zeros((12, LANES), F32), b.reshape(3, LANES), (3 * chip, 0))
        return _pack_small(dict(norm_mix=norm_mix, norm_mlp=norm_mlp, ret_gn_gain=gn, dil_q_gain=dup(dq), dil_k_gain=dup(dk),
                                swa_b_qkv=bias, swa_q_gain=dup(sq), swa_k_gain=dup(sk),
                                swa_sinks=jnp.pad(sinks.reshape(16), (0, LANES - 16)), loss=jnp.zeros((1, LANES), F32)))

    pw = small_pack(norm_mix, norm_mlp, ret_gn_gain, dil_q_gain, dil_k_gain, swa_b_qkv, swa_q_gain, swa_k_gain, swa_sinks)
    pm = small_pack(m_norm_mix, m_norm_mlp, m_ret_gn_gain, m_dil_q_gain, m_dil_k_gain, m_swa_b_qkv, m_swa_q_gain, m_swa_k_gain, m_swa_sinks)
    pv = small_pack(v_norm_mix, v_norm_mlp, v_ret_gn_gain, v_dil_q_gain, v_dil_k_gain, v_swa_b_qkv, v_swa_q_gain, v_swa_k_gain, v_swa_sinks)
    small_flat = _adamw_small(pw, gathered, pm, pv)
    small = [_unpack_small(t) for t in small_flat]

    for n, g in zip(names, _split_wait("grad_share_wait", shared, sems, small_flat[1], sum(layers), _share_copies(layers))):
        adamw_of(n, g)
    red.finish("win", big[names[-1]][1])
    adamw_of("hyb_w_in", _share_halves([(red.halves["hyb_w_in"], 1)], "grad_share_last")[0])

    def small_out(n, k):
        t = small[k][n]
        if n in ("norm_mix", "norm_mlp"):
            return t.reshape(2, D_MODEL)
        if n == "ret_gn_gain":
            return t.reshape(1, RET_HEADS, 128)
        if n == "swa_b_qkv":
            return lax.dynamic_slice(t, (3 * chip, 0), (3, LANES)).reshape(1, 384)
        if n == "swa_sinks":
            return t[0, :16].reshape(1, 16)
        return t[0, :HEAD].reshape(1, HEAD)

    order = ["norm_mix", "norm_mlp", "mlp_w_up", "mlp_w_down", "hyb_w_in", "hyb_w_out", "ret_gn_gain", "dil_q_gain",
             "dil_k_gain", "swa_w_qkv", "swa_b_qkv", "swa_w_out", "swa_q_gain", "swa_k_gain", "swa_sinks"]
    is_big = {"mlp_w_up", "mlp_w_down", "hyb_w_in", "hyb_w_out", "swa_w_qkv", "swa_w_out"}
    outs = [small[0]["loss"][0, 0], grad_x[None]]
    for k in range(4):
        outs += [big[n][k] if n in is_big else small_out(n, k) for n in order]
    return tuple(outs)
```

```python
import numpy as np
import jax
import jax.numpy as jnp
from jax import lax
from jax.experimental import pallas as pl
from jax.experimental.pallas import tpu as pltpu

F32, BF16 = jnp.float32, jnp.bfloat16
MESH = pl.DeviceIdType.MESH

LANES = 128
VMEM_LIMIT = 48 << 20
VMEM_LIMIT_WIDE = 60 << 20
D_MODEL = 1024
HEAD = 64
EPS = 1e-6
BLK = 128
RET_HEADS = 4
RET_THETA = 10000.0
ROPE_THETA = 500000.0
ROPE_DIMS = 16
DIL_PATTERNS = ((128, 1), (512, 4), (2048, 16))
SWA_DIST = 127
N_CHIPS = 4
ADAM_LR, ADAM_B1, ADAM_B2, ADAM_EPS, ADAM_WD, ADAM_STEP = 0.001, 0.9, 0.999, 1e-08, 0.01, 10

_LOG_GAMMA = [float(np.log1p(-np.exp2(np.float32(-5.0 - h)))) for h in range(RET_HEADS)]


def _pc(body, **kw):
    return pl.pallas_call(body, **kw)


def _params(sem, limit=VMEM_LIMIT):
    return pltpu.CompilerParams(dimension_semantics=sem, vmem_limit_bytes=limit)


def _matmul(a, b, *, dims, tm, tn, tk, outs, name, epilogue=None, extras=(), b_cs=False, b_rs=0, b_row0=0, b_rows=0,
            o_cs=0, a_pro=None, vmem=VMEM_LIMIT):
    if dims == "nn":
        M, K = a.shape
        N = b.shape[0] * b.shape[2] if b_cs else b.shape[1]
        a_spec = pl.BlockSpec((tm, tk), lambda i, j, k: (i, k))
        if b_cs:
            npt = b.shape[2] // tn
            b_spec = pl.BlockSpec((None, tk, tn), lambda i, j, k: (j // npt, k + b_row0, j % npt))
        elif b_rs:
            K, N, kps = b.shape[0] * b_rs, b.shape[2], b_rs // tk
            b_spec = pl.BlockSpec((None, tk, tn), lambda i, j, k: (k // kps, b_row0 + k % kps, j))
        else:
            b_spec = pl.BlockSpec((tk, tn), lambda i, j, k: (k, j))
        contract = (((1,), (0,)), ((), ()))
    elif dims == "nt":
        M, K = a.shape
        N = (b_rows or b.shape[1]) if b_cs else b.shape[0]
        a_spec = pl.BlockSpec((tm, tk), lambda i, j, k: (i, k))
        if b_cs:
            kpt = b.shape[2] // tk
            b_spec = pl.BlockSpec((None, tn, tk), lambda i, j, k: (k // kpt, j + b_row0, k % kpt))
        elif b_rs:
            N, jps = b.shape[0] * b_rs, b_rs // tn
            b_spec = pl.BlockSpec((None, tn, tk), lambda i, j, k: (j // jps, b_row0 + j % jps, k))
        else:
            b_spec = pl.BlockSpec((tn, tk), lambda i, j, k: (j, k))
        contract = (((1,), (1,)), ((), ()))
    else:
        K, M = a.shape
        N = b.shape[1]
        a_spec = pl.BlockSpec((tk, tm), lambda i, j, k: (k, i))
        b_spec = pl.BlockSpec((tk, tn), lambda i, j, k: (k, j))
        contract = (((0,), (0,)), ((), ()))
    assert M % tm == 0 and N % tn == 0 and K % tk == 0, (name, M, N, K, tm, tn, tk)
    nk = K // tk
    ex_specs = []
    for arr, kind in extras:
        if kind == "mn":
            ex_specs.append(pl.BlockSpec((tm, tn), lambda i, j, k: (i, j)))
        elif kind == "n":
            ex_specs.append(pl.BlockSpec((1, tn), lambda i, j, k: (0, j)))
        elif kind == "full":
            ex_specs.append(pl.BlockSpec(arr.shape, lambda i, j, k, nd=arr.ndim: (0,) * nd))
        else:
            ex_specs.append(pl.BlockSpec((tm, kind), lambda i, j, k: (i, 0)))
    if o_cs:
        n_sh = N // o_cs
        opt = n_sh // tn
        o_shape = (o_cs, M, n_sh)
        o_spec = pl.BlockSpec((None, tm, tn), lambda i, j, k: (j // opt, i, j % opt))
    else:
        o_shape = (M, N)
        o_spec = pl.BlockSpec((tm, tn), lambda i, j, k: (i, j))
    o_specs, o_shapes, summed = [], [], []
    for o in outs:
        if isinstance(o, tuple) and o[0] == "colsum":
            assert N == tn
            o_specs.append(pl.BlockSpec((1, tn), lambda i, j, k: (0, j)))
            o_shapes.append(jax.ShapeDtypeStruct((1, N), F32))
            summed.append(True)
        elif isinstance(o, tuple):
            o_specs.append(pl.BlockSpec((tm, o[1]), lambda i, j, k: (i, 0)))
            o_shapes.append(jax.ShapeDtypeStruct((M, o[1]), o[0]))
            summed.append(False)
        else:
            o_specs.append(o_spec)
            o_shapes.append(jax.ShapeDtypeStruct(o_shape, o))
            summed.append(False)
    n_ex, n_out = len(extras), len(outs)
    if epilogue is None:
        epilogue = lambda acc: (acc,)

    def body(a_ref, b_ref, *rest):
        ex, o_refs, acc = rest[:n_ex], rest[n_ex:n_ex + n_out], rest[-1]
        i, k = pl.program_id(0), pl.program_id(2)

        @pl.when(k == 0)
        def _():
            acc[...] = jnp.zeros_like(acc)

        av = a_ref[...] if a_pro is None else a_pro(a_ref[...])
        acc[...] += lax.dot_general(av.astype(BF16), b_ref[...].astype(BF16), contract, preferred_element_type=F32)

        @pl.when(k == nk - 1)
        def _():
            vals = epilogue(acc[...], *[e[...] for e in ex])
            for r, v, sm in zip(o_refs, vals, summed):
                if sm:
                    @pl.when(i == 0)
                    def _(r=r):
                        r[...] = jnp.zeros_like(r)

                    r[...] += v
                else:
                    r[...] = v.astype(r.dtype)

    res = _pc(
        body, name=name, grid=(M // tm, N // tn, nk),
        in_specs=[a_spec, b_spec] + ex_specs, out_specs=o_specs, out_shape=o_shapes,
        scratch_shapes=[pltpu.VMEM((tm, tn), F32)],
        compiler_params=_params(("arbitrary" if any(summed) else "parallel", "parallel", "arbitrary"), vmem),
    )(a, b, *[e for e, _ in extras])
    return res[0] if n_out == 1 else res


def _roll(x, s):
    return pltpu.roll(x, s % LANES, 1)


def _rope(x, A, B, C, half):
    return x * A + _roll(x, LANES - half) * B + _roll(x, half) * C


def _rope_t(g, A, B, C, half):
    return g * A + _roll(g * B, half) + _roll(g * C, LANES - half)


def _gmean(x, G):
    hi = x.astype(BF16)
    lo = (x - hi.astype(F32)).astype(BF16)
    Gb = G.astype(BF16)
    return jnp.dot(hi, Gb, preferred_element_type=F32) + jnp.dot(lo, Gb, preferred_element_type=F32)


def _head_mask(shape, half):
    lane = lax.broadcasted_iota(jnp.int32, shape, len(shape) - 1)
    return (lane >= HEAD) if half else (lane < HEAD)


def _group_matrix():
    i = np.arange(LANES)
    return jnp.asarray((i[:, None] // HEAD == i[None, :] // HEAD).astype(np.float32) / HEAD)


def _rope_inv():
    l = np.arange(LANES) % HEAD
    inv_r = np.power(np.float32(RET_THETA), -(l % 32).astype(np.float32) * np.float32(2.0 / HEAD))
    hp = ROPE_DIMS // 2
    inv_p = np.power(np.float32(ROPE_THETA), -(l % hp).astype(np.float32) * np.float32(2.0 / ROPE_DIMS))
    inv_p = np.where(l < ROPE_DIMS, inv_p, 0.0)
    return jnp.asarray(np.stack([inv_r, inv_p]).astype(np.float32))


def _tables(pos_col):
    S = pos_col.shape[0]
    tm = 512
    hp = ROPE_DIMS // 2

    def body(p_ref, inv_ref, o_ref):
        p = p_ref[...].astype(F32)
        lane = lax.broadcasted_iota(jnp.int32, (tm, LANES), 1) % HEAD
        ang = p * inv_ref[0:1, :]
        c, s = jnp.cos(ang), jnp.sin(ang)
        o_ref[:, 0:128] = c
        o_ref[:, 128:256] = jnp.where(lane < 32, -s, 0.0)
        o_ref[:, 256:384] = jnp.where(lane >= 32, s, 0.0)
        ang = p * inv_ref[1:2, :]
        c, s = jnp.cos(ang), jnp.sin(ang)
        o_ref[:, 384:512] = c
        o_ref[:, 512:640] = jnp.where(lane < hp, -s, 0.0)
        o_ref[:, 640:768] = jnp.where((lane >= hp) & (lane < ROPE_DIMS), s, 0.0)

    return _pc(
        body, name="rope_tables", grid=(S // tm,),
        in_specs=[pl.BlockSpec((tm, 1), lambda i: (i, 0)), pl.BlockSpec((2, LANES), lambda i: (0, 0))],
        out_specs=pl.BlockSpec((tm, 768), lambda i: (i, 0)),
        out_shape=jax.ShapeDtypeStruct((S, 768), F32),
        compiler_params=_params(("parallel",)),
    )(pos_col, _rope_inv())


def _tab(tab_ref, which):
    o = 384 * which
    return tab_ref[:, o:o + 128], tab_ref[:, o + 128:o + 256], tab_ref[:, o + 256:o + 384]


def _rms_fwd(x, g, name):
    S, Dm = x.shape
    tm = 512

    def body(x_ref, g_ref, h_ref):
        xv = x_ref[...]
        r = lax.rsqrt(jnp.mean(xv * xv, axis=-1, keepdims=True) + EPS)
        h_ref[...] = (xv * r * g_ref[...]).astype(BF16)

    return _pc(
        body, name=name, grid=(S // tm,),
        in_specs=[pl.BlockSpec((tm, Dm), lambda i: (i, 0)), pl.BlockSpec((1, Dm), lambda i: (0, 0))],
        out_specs=pl.BlockSpec((tm, Dm), lambda i: (i, 0)),
        out_shape=jax.ShapeDtypeStruct((S, Dm), BF16),
        compiler_params=_params(("parallel",)),
    )(x, g.reshape(1, Dm))


def _hn_fwd(x, gain, G):
    r = lax.rsqrt(_gmean(x * x, G) + EPS)
    return x * r * gain


def _hn_bwd(x, gain, dy, G):
    r = lax.rsqrt(_gmean(x * x, G) + EPS)
    t = dy * gain
    dx = r * t - x * (r * r * r) * _gmean(x * t, G)
    return dx, jnp.sum(dy * x * r, axis=0, keepdims=True)


def _fold_halves(v):
    return v + _roll(v, HEAD)


def _even_pre_fwd(proj, tab, qg, kg):
    S = proj.shape[0]
    tm = 512

    def body(p_ref, tab_ref, qg_ref, kg_ref, g_ref, rq_ref, rk_ref, rv_ref, dq_ref, dk_ref, dv_ref):
        Ar, Br, Cr = _tab(tab_ref, 0)
        Ap, Bp, Cp = _tab(tab_ref, 1)
        G = g_ref[...]
        for c in range(2):
            sl = slice(c * 128, (c + 1) * 128)
            rq_ref[:, sl] = _rope(p_ref[:, c * 128:(c + 1) * 128], Ar, Br, Cr, 32).astype(BF16)
            rk_ref[:, sl] = (_rope(p_ref[:, 256 + c * 128:256 + (c + 1) * 128], Ar, Br, Cr, 32) * 0.125).astype(BF16)
        rv_ref[...] = p_ref[:, 512:1024].astype(BF16)
        for c in range(4):
            sl = slice(c * 128, (c + 1) * 128)
            q = _hn_fwd(p_ref[:, 1536 + c * 128:1536 + (c + 1) * 128], qg_ref[...], G)
            dq_ref[:, sl] = _rope(q, Ap, Bp, Cp, 8).astype(BF16)
            k = _hn_fwd(p_ref[:, 2048 + c * 128:2048 + (c + 1) * 128], kg_ref[...], G)
            dk_ref[:, sl] = _rope(k, Ap, Bp, Cp, 8).astype(BF16)
        dv_ref[...] = p_ref[:, 2560:3072].astype(BF16)

    row = lambda w: pl.BlockSpec((tm, w), lambda i: (i, 0))
    vec = pl.BlockSpec((1, LANES), lambda i: (0, 0))
    return _pc(
        body, name="even_pre_fwd", grid=(S // tm,),
        in_specs=[row(3072), row(768), vec, vec, pl.BlockSpec((LANES, LANES), lambda i: (0, 0))],
        out_specs=[row(256), row(256), row(512), row(512), row(512), row(512)],
        out_shape=[jax.ShapeDtypeStruct((S, w), BF16) for w in (256, 256, 512, 512, 512, 512)],
        compiler_params=_params(("parallel",), VMEM_LIMIT_WIDE),
    )(proj, tab, qg, kg, _group_matrix())


def _even_pre_bwd(proj, tab, qg, kg, drq, drk, drv, drg, dqs, dks, dvs):
    S = proj.shape[0]
    tm = 512
    npat = len(dqs)

    def body(p_ref, tab_ref, qg_ref, kg_ref, g_ref, drq_ref, drk_ref, drv_ref, drg_ref, *rest):
        dq_refs, dk_refs, dv_refs = rest[:npat], rest[npat:2 * npat], rest[2 * npat:3 * npat]
        dp_ref, dqg_ref, dkg_ref = rest[3 * npat:]
        Ar, Br, Cr = _tab(tab_ref, 0)
        Ap, Bp, Cp = _tab(tab_ref, 1)
        G = g_ref[...]
        for c in range(2):
            sl = slice(c * 128, (c + 1) * 128)
            dp_ref[:, c * 128:(c + 1) * 128] = _rope_t(drq_ref[:, sl], Ar, Br, Cr, 32).astype(BF16)
            dp_ref[:, 256 + c * 128:256 + (c + 1) * 128] = _rope_t(drk_ref[:, sl] * 0.125, Ar, Br, Cr, 32).astype(BF16)
        dp_ref[:, 512:1024] = drv_ref[...].astype(BF16)
        dp_ref[:, 1024:1536] = drg_ref[...].astype(BF16)
        accq = jnp.zeros((1, LANES), F32)
        acck = jnp.zeros((1, LANES), F32)
        for c in range(4):
            sl = slice(c * 128, (c + 1) * 128)
            g = dq_refs[0][:, sl]
            for r in dq_refs[1:]:
                g = g + r[:, sl]
            dx, dg = _hn_bwd(p_ref[:, 1536 + c * 128:1536 + (c + 1) * 128], qg_ref[...], _rope_t(g, Ap, Bp, Cp, 8), G)
            dp_ref[:, 1536 + c * 128:1536 + (c + 1) * 128] = dx.astype(BF16)
            accq = accq + dg
            g = dk_refs[0][:, sl]
            for r in dk_refs[1:]:
                g = g + r[:, sl]
            dx, dg = _hn_bwd(p_ref[:, 2048 + c * 128:2048 + (c + 1) * 128], kg_ref[...], _rope_t(g, Ap, Bp, Cp, 8), G)
            dp_ref[:, 2048 + c * 128:2048 + (c + 1) * 128] = dx.astype(BF16)
            acck = acck + dg
        g = dv_refs[0][...]
        for r in dv_refs[1:]:
            g = g + r[...]
        dp_ref[:, 2560:3072] = g.astype(BF16)

        @pl.when(pl.program_id(0) == 0)
        def _():
            dqg_ref[...] = jnp.zeros_like(dqg_ref)
            dkg_ref[...] = jnp.zeros_like(dkg_ref)

        dqg_ref[...] += _fold_halves(accq)
        dkg_ref[...] += _fold_halves(acck)

    row = lambda w: pl.BlockSpec((tm, w), lambda i: (i, 0))
    vec = pl.BlockSpec((1, LANES), lambda i: (0, 0))
    return _pc(
        body, name="even_pre_bwd", grid=(S // tm,),
        in_specs=[row(3072), row(768), vec, vec, pl.BlockSpec((LANES, LANES), lambda i: (0, 0)),
                  row(256), row(256), row(512), row(512)] + [row(512)] * (3 * npat),
        out_specs=[row(3072), vec, vec],
        out_shape=[jax.ShapeDtypeStruct((S, 3072), BF16), jax.ShapeDtypeStruct((1, LANES), F32),
                   jax.ShapeDtypeStruct((1, LANES), F32)],
        compiler_params=_params(("arbitrary",), VMEM_LIMIT_WIDE),
    )(proj, tab, qg, kg, _group_matrix(), drq, drk, drv, drg, *dqs, *dks, *dvs)


def _ret_consts(pair, half):
    lg = jnp.where(pair == 0, _LOG_GAMMA[half], _LOG_GAMMA[2 + half]).astype(F32)
    i = lax.broadcasted_iota(jnp.int32, (BLK, BLK), 0)
    j = lax.broadcasted_iota(jnp.int32, (BLK, BLK), 1)
    diff = (i - j).astype(F32)
    decay = jnp.where(diff >= 0, jnp.exp(lg * jnp.maximum(diff, 0.0)), 0.0)
    t = lax.broadcasted_iota(jnp.int32, (BLK, 1), 0).astype(F32)
    xi = jnp.exp(lg * (t + 1.0))
    zeta = jnp.exp(lg * (BLK - 1.0 - t))
    cd = jnp.exp(jnp.full((1, 1), BLK, F32) * lg)
    return decay, xi, zeta, cd


RET_STEP = 8


def _ret_fwd(rq, rk, rv):
    S = rq.shape[0]
    nc = S // BLK
    rows = RET_STEP * BLK

    def body(q_ref, k_ref, v_ref, o_ref, st_ref, R):
        p, n = pl.program_id(0), pl.program_id(1)

        @pl.when(n == 0)
        def _():
            R[...] = jnp.zeros_like(R)

        consts = [_ret_consts(p, half) for half in range(2)]
        masks = [_head_mask((BLK, LANES), half) for half in range(2)]
        for ci in range(RET_STEP):
            rs = slice(ci * BLK, (ci + 1) * BLK)
            q2, k2 = q_ref[rs, :], k_ref[rs, :]
            for half in range(2):
                decay, xi, zeta, cd = consts[half]
                m = masks[half]
                qm = jnp.where(m, q2, jnp.zeros_like(q2))
                km = jnp.where(m, k2, jnp.zeros_like(k2))
                v = v_ref[rs, half * 128:(half + 1) * 128]
                Rb = R[half].astype(BF16)
                st_ref[ci, half] = Rb
                sc = lax.dot_general(qm, k2, (((1,), (1,)), ((), ())), preferred_element_type=F32) * decay
                o = jnp.dot(sc.astype(BF16), v, preferred_element_type=F32)
                o = o + jnp.dot(qm, Rb, preferred_element_type=F32) * xi
                o_ref[rs, half * 128:(half + 1) * 128] = o
                kz = (km.astype(F32) * zeta).astype(BF16)
                R[half] = R[half] * cd + lax.dot_general(kz, v, (((0,), (0,)), ((), ())), preferred_element_type=F32)

    return _pc(
        body, name="ret_fwd", grid=(2, nc // RET_STEP),
        in_specs=[pl.BlockSpec((rows, 128), lambda p, n: (n, p)), pl.BlockSpec((rows, 128), lambda p, n: (n, p)),
                  pl.BlockSpec((rows, 256), lambda p, n: (n, p))],
        out_specs=[pl.BlockSpec((rows, 256), lambda p, n: (n, p)),
                   pl.BlockSpec((None, RET_STEP, 2, 128, 128), lambda p, n: (p, n, 0, 0, 0))],
        out_shape=[jax.ShapeDtypeStruct((S, 512), F32), jax.ShapeDtypeStruct((2, nc, 2, 128, 128), BF16)],
        scratch_shapes=[pltpu.VMEM((2, 128, 128), F32)],
        compiler_params=_params(("parallel", "arbitrary")),
    )(rq, rk, rv)


def _ret_bwd(rq, rk, rv, states, do):
    S = rq.shape[0]
    nc = S // BLK
    ns = nc // RET_STEP
    rows = RET_STEP * BLK
    nt = (((1,), (1,)), ((), ()))
    tn = (((0,), (0,)), ((), ()))

    def body(q_ref, k_ref, v_ref, st_ref, do_ref, dq_ref, dk_ref, dv_ref, U):
        p, n = pl.program_id(0), pl.program_id(1)

        @pl.when(n == 0)
        def _():
            U[...] = jnp.zeros_like(U)

        consts = [_ret_consts(p, half) for half in range(2)]
        masks = [_head_mask((BLK, LANES), half) for half in range(2)]
        for ci in reversed(range(RET_STEP)):
            rs = slice(ci * BLK, (ci + 1) * BLK)
            q2, k2 = q_ref[rs, :], k_ref[rs, :]
            dq_acc = jnp.zeros((BLK, LANES), F32)
            dk_acc = jnp.zeros((BLK, LANES), F32)
            for half in range(2):
                decay, xi, zeta, cd = consts[half]
                m = masks[half]
                qm = jnp.where(m, q2, jnp.zeros_like(q2))
                km = jnp.where(m, k2, jnp.zeros_like(k2))
                v = v_ref[rs, half * 128:(half + 1) * 128]
                dob = do_ref[rs, half * 128:(half + 1) * 128].astype(BF16)
                Rb = st_ref[ci, half]
                Ub = U[half].astype(BF16)
                dsc = (lax.dot_general(dob, v, nt, preferred_element_type=F32) * decay).astype(BF16)
                xdo = (dob.astype(F32) * xi).astype(BF16)
                dq_acc += jnp.dot(dsc, km, preferred_element_type=F32) + lax.dot_general(xdo, Rb, nt, preferred_element_type=F32)
                dk_acc += lax.dot_general(dsc, qm, tn, preferred_element_type=F32) \
                    + lax.dot_general(v, Ub, nt, preferred_element_type=F32) * zeta
                sc = (lax.dot_general(qm, k2, nt, preferred_element_type=F32) * decay).astype(BF16)
                kz = (km.astype(F32) * zeta).astype(BF16)
                dv_ref[rs, half * 128:(half + 1) * 128] = lax.dot_general(sc, dob, tn, preferred_element_type=F32) \
                    + jnp.dot(kz, Ub, preferred_element_type=F32)
                U[half] = U[half] * cd + lax.dot_general(qm, xdo, tn, preferred_element_type=F32)
            dq_ref[rs, :] = dq_acc
            dk_ref[rs, :] = dk_acc

    rev = lambda w: pl.BlockSpec((rows, w), lambda p, n: (ns - 1 - n, p))
    return _pc(
        body, name="ret_bwd", grid=(2, ns),
        in_specs=[rev(128), rev(128), rev(256),
                  pl.BlockSpec((None, RET_STEP, 2, 128, 128), lambda p, n: (p, ns - 1 - n, 0, 0, 0)), rev(256)],
        out_specs=[rev(128), rev(128), rev(256)],
        out_shape=[jax.ShapeDtypeStruct((S, 256), F32), jax.ShapeDtypeStruct((S, 256), F32),
                   jax.ShapeDtypeStruct((S, 512), F32)],
        scratch_shapes=[pltpu.VMEM((2, 128, 128), F32)],
        compiler_params=_params(("parallel", "arbitrary")),
    )(rq, rk, rv, states, do)


ATT_TILE = 2048


def _rows(ref, start, n, r):
    if r == 1:
        return ref[pl.ds(start, n), :]
    return ref[pl.ds(start, n, stride=r), :]


def _twice(x):
    return jnp.concatenate([x, x], axis=0)


def _stack_heads(x, masks):
    zero = jnp.zeros_like(x)
    return jnp.concatenate([jnp.where(masks[0], x, zero), jnp.where(masks[1], x, zero)], axis=0)


def _set_rows(ref, start, n, r, val):
    if r == 1:
        ref[pl.ds(start, n), :] = val
    else:
        ref[pl.ds(start, n, stride=r), :] = val


def _band_geometry(S, patterns):
    rmax = max(r for _, r in patterns)
    H = BLK * rmax
    T = min(S, ATT_TILE)
    assert T % H == 0 and S % T == 0
    return H, T, S // T, T // BLK


def _band_fwd(q, k, v, *, patterns, nq, name, sinks=None, want_bf16=False):
    S, Ck = k.shape
    H, T, nt, nbt = _band_geometry(S, patterns)
    ncol = Ck // LANES
    scale = HEAD ** -0.5
    has_sink = sinks is not None
    nt_dims = (((1,), (1,)), ((), ()))

    def body(*refs):
        q_ref, kp_ref, kc_ref, vp_ref, vc_ref = refs[:5]
        sk_ref = refs[5] if has_sink else None
        n_out = 3 if want_bf16 else 2
        outs = refs[5 + has_sink:5 + has_sink + n_out]
        qf, kf, vf, M, L, A = refs[5 + has_sink + n_out:]
        t = pl.program_id(1)
        kf[0:H, :] = kp_ref[...].astype(F32)
        kf[H:H + T, :] = kc_ref[...].astype(F32)
        vf[0:H, :] = vp_ref[...].astype(F32)
        vf[H:H + T, :] = vc_ref[...].astype(F32)
        r_i = lax.broadcasted_iota(jnp.int32, (BLK, 2 * BLK), 0)
        c_i = lax.broadcasted_iota(jnp.int32, (BLK, 2 * BLK), 1)
        dist_i = r_i + BLK - c_i
        masks = [_head_mask((BLK, LANES), h) for h in range(2)]

        for i in range(nq):
            qf[...] = q_ref[:, i * 128:(i + 1) * 128].astype(F32) * scale
            for p, (dist, r) in enumerate(patterns):
                in_band = (dist_i >= 0) & (dist_i <= dist)
                in_band_first = in_band & ((c_i >= BLK) | (t > 0))
                in_band, in_band_first = _twice(in_band), _twice(in_band_first)

                def unit(j, b, p=p, r=r, in_band=in_band, in_band_first=in_band_first):
                    q0 = j + b * (BLK * r)
                    q2 = _rows(qf, q0, BLK, r).astype(BF16)
                    kcat = _rows(kf, H + q0 - BLK * r, 2 * BLK, r).astype(BF16)
                    vcat = _rows(vf, H + q0 - BLK * r, 2 * BLK, r).astype(BF16)
                    valid = in_band if b > 0 else in_band_first
                    s = lax.dot_general(_stack_heads(q2, masks), kcat, nt_dims, preferred_element_type=F32)
                    s = jnp.where(valid, s, -jnp.inf)
                    mx = jnp.max(s, axis=1, keepdims=True)
                    pr = jnp.exp(s - mx)
                    den = jnp.sum(pr, axis=1, keepdims=True)
                    pv = jnp.dot(pr.astype(BF16), vcat, preferred_element_type=F32)
                    m2 = jnp.where(masks[0], mx[:BLK], mx[BLK:])
                    l2 = jnp.where(masks[0], den[:BLK], den[BLK:])
                    a2 = jnp.where(masks[0], pv[:BLK], pv[BLK:])
                    if p > 0:
                        mo = _rows(M, q0, BLK, r)
                        mn = jnp.maximum(mo, m2)
                        wa, wb = jnp.exp(mo - mn), jnp.exp(m2 - mn)
                        l2 = wa * _rows(L, q0, BLK, r) + wb * l2
                        a2 = wa * _rows(A, q0, BLK, r) + wb * a2
                        m2 = mn
                    _set_rows(M, q0, BLK, r, m2)
                    _set_rows(L, q0, BLK, r, l2)
                    _set_rows(A, q0, BLK, r, a2)

                for u in range(nbt):
                    unit(u % r, u // r)
            sl = slice(i * 128, (i + 1) * 128)
            mm, ll, aa = M[...], L[...], A[...]
            if has_sink:
                snk = sk_ref[:, sl]
                mn = jnp.maximum(mm, snk)
                w = jnp.exp(mm - mn)
                ll = ll * w + jnp.exp(snk - mn)
                aa = aa * w
                mm = mn
            o = aa / ll
            outs[0][:, sl] = o
            outs[1][:, sl] = mm + jnp.log(ll)
            if want_bf16:
                outs[2][:, sl] = o.astype(BF16)

    th = T // H
    qspec = pl.BlockSpec((T, nq * 128), lambda j, t: (t, j))
    cur = pl.BlockSpec((T, 128), lambda j, t: (t, j))
    prev = pl.BlockSpec((H, 128), lambda j, t: (jnp.maximum(t * th - 1, 0), j))
    in_specs = [qspec, prev, cur, prev, cur]
    args = [q, k, k, v, v]
    if has_sink:
        in_specs.append(pl.BlockSpec((1, nq * 128), lambda j, t: (0, j)))
        args.append(sinks)
    out_dts = [F32, F32] + ([BF16] if want_bf16 else [])
    return _pc(
        body, name=name, grid=(ncol, nt), in_specs=in_specs,
        out_specs=[qspec] * len(out_dts),
        out_shape=[jax.ShapeDtypeStruct(q.shape, dt) for dt in out_dts],
        scratch_shapes=[pltpu.VMEM((T, LANES), F32), pltpu.VMEM((H + T, LANES), F32), pltpu.VMEM((H + T, LANES), F32),
                        pltpu.VMEM((T, LANES), F32), pltpu.VMEM((T, LANES), F32), pltpu.VMEM((T, LANES), F32)],
        compiler_params=_params(("parallel", "parallel")),
    )(*args)


def _band_bwd(q, k, v, lse, delta, do, *, patterns, nq, name, sinks=None, do_col0=0):
    S, Ck = k.shape
    H, T, nt, nbt = _band_geometry(S, patterns)
    ncol = Ck // LANES
    scale = HEAD ** -0.5
    has_sink = sinks is not None
    nt_dims = (((1,), (1,)), ((), ()))
    tn_dims = (((0,), (0,)), ((), ()))

    def body(*refs):
        (qc_ref, qn_ref, kp_ref, kc_ref, vp_ref, vc_ref, lc_ref, ln_ref, ec_ref, en_ref, dc_ref, dn_ref) = refs[:12]
        sk_ref = refs[12] if has_sink else None
        n_out = 4 if has_sink else 3
        outs = refs[12 + has_sink:12 + has_sink + n_out]
        dq_ref, dk_ref, dv_ref = outs[:3]
        qf, kf, vf, lf, ef, df = refs[12 + has_sink + n_out:]
        t = pl.program_id(1)
        kf[0:H, :] = kp_ref[...].astype(F32)
        kf[H:H + T, :] = kc_ref[...].astype(F32)
        vf[0:H, :] = vp_ref[...].astype(F32)
        vf[H:H + T, :] = vc_ref[...].astype(F32)
        dk_ref[...] = jnp.zeros_like(dk_ref)
        dv_ref[...] = jnp.zeros_like(dv_ref)
        r_i = lax.broadcasted_iota(jnp.int32, (BLK, 2 * BLK), 0)
        c_i = lax.broadcasted_iota(jnp.int32, (BLK, 2 * BLK), 1)
        dist_q = r_i + BLK - c_i
        dist_h = dist_q[:, :BLK]
        m1 = [_head_mask((BLK, LANES), h) for h in range(2)]

        def stacked_inputs(q2, do2, l2, e2):
            spread = lambda v: jnp.concatenate([jnp.where(m1[0], v, _roll(v, HEAD)), jnp.where(m1[1], v, _roll(v, HEAD))], axis=0)
            return _stack_heads(q2, m1), _stack_heads(do2.astype(BF16), m1), spread(l2), spread(e2)

        for i in range(nq):
            sl = slice(i * 128, (i + 1) * 128)
            qf[0:T, :] = qc_ref[:, sl].astype(F32) * scale
            qf[T:T + H, :] = qn_ref[:, sl].astype(F32) * scale
            for buf, c_ref, n_ref in ((lf, lc_ref, ln_ref), (ef, ec_ref, en_ref), (df, dc_ref, dn_ref)):
                buf[0:T, :] = c_ref[:, sl]
                buf[T:T + H, :] = n_ref[:, sl]
            if has_sink:
                @pl.when(t == 0)
                def _():
                    outs[3][:, sl] = jnp.zeros((1, LANES), F32)

                outs[3][:, sl] += jnp.sum(-jnp.exp(sk_ref[:, sl] - lc_ref[:, sl]) * ec_ref[:, sl], axis=0, keepdims=True)
            for p, (dist, r) in enumerate(patterns):
                band_q = (dist_q >= 0) & (dist_q <= dist)
                band_first = band_q & ((c_i >= BLK) | (t > 0))
                band_h = (dist_h >= 0) & (dist_h <= dist)
                band_q, band_first, band_h = _twice(band_q), _twice(band_first), _twice(band_h)

                def add_rows(ref, start, val, r=r):
                    _set_rows(ref, start, BLK, r, _rows(ref, start, BLK, r) + val)

                def unit(j, b, p=p, r=r, band_q=band_q, band_first=band_first):
                    q0 = j + b * (BLK * r)
                    q2 = _rows(qf, q0, BLK, r).astype(BF16)
                    do2, l2, e2 = _rows(df, q0, BLK, r), _rows(lf, q0, BLK, r), _rows(ef, q0, BLK, r)
                    kcat = _rows(kf, H + q0 - BLK * r, 2 * BLK, r).astype(BF16)
                    vcat = _rows(vf, H + q0 - BLK * r, 2 * BLK, r).astype(BF16)
                    valid = band_q if b > 0 else band_first
                    qs, dos, ls, es = stacked_inputs(q2, do2, l2, e2)
                    s = lax.dot_general(qs, kcat, nt_dims, preferred_element_type=F32)
                    pr = jnp.where(valid, jnp.exp(s - jnp.concatenate([ls, ls], axis=1)), 0.0)
                    dp = lax.dot_general(dos, vcat, nt_dims, preferred_element_type=F32)
                    ds = (pr * (dp - jnp.concatenate([es, es], axis=1))).astype(BF16)
                    dqs = jnp.dot(ds, kcat, preferred_element_type=F32) * scale
                    dq2 = jnp.where(m1[0], dqs[:BLK], dqs[BLK:])
                    dvc = lax.dot_general(pr.astype(BF16), dos, tn_dims, preferred_element_type=F32)
                    dkc = lax.dot_general(ds, qs, tn_dims, preferred_element_type=F32)
                    if p > 0:
                        dq2 = dq2 + _rows(dq_ref.at[:, sl], q0, BLK, r)
                    _set_rows(dq_ref.at[:, sl], q0, BLK, r, dq2)
                    add_rows(dk_ref, q0, dkc[BLK:])
                    add_rows(dv_ref, q0, dvc[BLK:])
                    if b > 0:
                        add_rows(dk_ref, q0 - BLK * r, dkc[:BLK])
                        add_rows(dv_ref, q0 - BLK * r, dvc[:BLK])

                def halo_unit(j, r=r, band_h=band_h):
                    k0 = j + (nbt // r - 1) * (BLK * r)
                    q2 = _rows(qf, T + j, BLK, r).astype(BF16)
                    do2, l2, e2 = _rows(df, T + j, BLK, r), _rows(lf, T + j, BLK, r), _rows(ef, T + j, BLK, r)
                    kc = _rows(kf, H + k0, BLK, r).astype(BF16)
                    vc = _rows(vf, H + k0, BLK, r).astype(BF16)
                    qs, dos, ls, es = stacked_inputs(q2, do2, l2, e2)
                    s = lax.dot_general(qs, kc, nt_dims, preferred_element_type=F32)
                    pr = jnp.where(band_h, jnp.exp(s - ls), 0.0)
                    dp = lax.dot_general(dos, vc, nt_dims, preferred_element_type=F32)
                    ds = (pr * (dp - es)).astype(BF16)
                    add_rows(dk_ref, k0, lax.dot_general(ds, qs, tn_dims, preferred_element_type=F32))
                    add_rows(dv_ref, k0, lax.dot_general(pr.astype(BF16), dos, tn_dims, preferred_element_type=F32))

                for u in range(nbt):
                    unit(u % r, u // r)
                if nt > 1:
                    @pl.when(t < nt - 1)
                    def _(r=r, halo_unit=halo_unit):
                        for j in range(r):
                            halo_unit(j)

    th = T // H
    last = S // H - 1
    qcur = pl.BlockSpec((T, nq * 128), lambda j, t: (t, j))
    qnext = pl.BlockSpec((H, nq * 128), lambda j, t: (jnp.minimum((t + 1) * th, last), j))
    cur = pl.BlockSpec((T, 128), lambda j, t: (t, j))
    prev = pl.BlockSpec((H, 128), lambda j, t: (jnp.maximum(t * th - 1, 0), j))
    dcur = pl.BlockSpec((T, nq * 128), lambda j, t: (t, j + do_col0))
    dnext = pl.BlockSpec((H, nq * 128), lambda j, t: (jnp.minimum((t + 1) * th, last), j + do_col0))
    in_specs = [qcur, qnext, prev, cur, prev, cur, qcur, qnext, qcur, qnext, dcur, dnext]
    args = [q, q, k, k, v, v, lse, lse, delta, delta, do, do]
    out_specs = [qcur, cur, cur]
    out_shape = [jax.ShapeDtypeStruct(q.shape, F32), jax.ShapeDtypeStruct(k.shape, F32), jax.ShapeDtypeStruct(k.shape, F32)]
    if has_sink:
        vec = pl.BlockSpec((1, nq * 128), lambda j, t: (0, j))
        in_specs.append(vec)
        args.append(sinks)
        out_specs.append(vec)
        out_shape.append(jax.ShapeDtypeStruct((1, q.shape[1]), F32))
    big = pltpu.VMEM((T + H, LANES), F32)
    return _pc(
        body, name=name, grid=(ncol, nt), in_specs=in_specs, out_specs=out_specs, out_shape=out_shape,
        scratch_shapes=[big] * 6,
        compiler_params=_params(("parallel", "arbitrary")),
    )(*args)


def _even_post_fwd(ro, proj, gn, da):
    S = ro.shape[0]
    tm = 512

    def body(ro_ref, rg_ref, gn_ref, da_ref, mix_ref):
        for c in range(4):
            sl = slice(c * 128, (c + 1) * 128)
            x = ro_ref[:, sl]
            mu = jnp.mean(x, axis=1, keepdims=True)
            xc = x - mu
            var = jnp.mean(xc * xc, axis=1, keepdims=True)
            y = xc * lax.rsqrt(var + EPS) * gn_ref[:, sl]
            z = rg_ref[:, sl]
            mix_ref[:, sl] = (z * jax.nn.sigmoid(z) * y).astype(BF16)
        mix_ref[:, 512:1024] = da_ref[...].astype(BF16)

    row = lambda w: pl.BlockSpec((tm, w), lambda i: (i, 0))
    return _pc(
        body, name="even_post_fwd", grid=(S // tm,),
        in_specs=[row(512), pl.BlockSpec((tm, 512), lambda i: (i, 2)), pl.BlockSpec((1, 512), lambda i: (0, 0)), row(512)],
        out_specs=row(1024), out_shape=jax.ShapeDtypeStruct((S, 1024), BF16),
        compiler_params=_params(("parallel",), VMEM_LIMIT_WIDE),
    )(ro, proj, gn, da)


def _even_post_bwd(ro, proj, gn, dmixed):
    S = ro.shape[0]
    tm = 512

    def body(ro_ref, rg_ref, gn_ref, dm_ref, dro_ref, drg_ref, dgn_ref):
        @pl.when(pl.program_id(0) == 0)
        def _():
            dgn_ref[...] = jnp.zeros_like(dgn_ref)

        for c in range(4):
            sl = slice(c * 128, (c + 1) * 128)
            x = ro_ref[:, sl]
            mu = jnp.mean(x, axis=1, keepdims=True)
            xc = x - mu
            rstd = lax.rsqrt(jnp.mean(xc * xc, axis=1, keepdims=True) + EPS)
            xh = xc * rstd
            gain = gn_ref[:, sl]
            y = xh * gain
            z = rg_ref[:, sl]
            sg = jax.nn.sigmoid(z)
            dra = dm_ref[:, sl]
            drg_ref[:, sl] = dra * y * sg * (1.0 + z * (1.0 - sg))
            dy = dra * z * sg
            dgn_ref[:, sl] += jnp.sum(dy * xh, axis=0, keepdims=True)
            dxh = dy * gain
            dro_ref[:, sl] = rstd * (dxh - jnp.mean(dxh, axis=1, keepdims=True)
                                     - xh * jnp.mean(dxh * xh, axis=1, keepdims=True))

    row = lambda w: pl.BlockSpec((tm, w), lambda i: (i, 0))
    vec = pl.BlockSpec((1, 512), lambda i: (0, 0))
    return _pc(
        body, name="even_post_bwd", grid=(S // tm,),
        in_specs=[row(512), pl.BlockSpec((tm, 512), lambda i: (i, 2)), vec, row(512)],
        out_specs=[row(512), row(512), vec],
        out_shape=[jax.ShapeDtypeStruct((S, 512), F32), jax.ShapeDtypeStruct((S, 512), F32),
                   jax.ShapeDtypeStruct((1, 512), F32)],
        compiler_params=_params(("arbitrary",), VMEM_LIMIT_WIDE),
    )(ro, proj, gn, dmixed)


def _swa_pre_fwd(proj, tab, qg, kg):
    S = proj.shape[0]
    tm = 512

    def body(p_ref, tab_ref, qg_ref, kg_ref, g_ref, q_ref, k_ref, v_ref):
        Ap, Bp, Cp = _tab(tab_ref, 1)
        G = g_ref[...]
        lo = _head_mask((tm, LANES), 0)
        for c in range(8):
            sl = slice(c * 128, (c + 1) * 128)
            q_ref[:, sl] = _rope(_hn_fwd(p_ref[:, sl], qg_ref[...], G), Ap, Bp, Cp, 8).astype(BF16)
        for c in range(2):
            kn = _rope(_hn_fwd(p_ref[:, 1024 + c * 128:1024 + (c + 1) * 128], kg_ref[...], G), Ap, Bp, Cp, 8)
            vv = p_ref[:, 1280 + c * 128:1280 + (c + 1) * 128]
            for t, ref in ((kn, k_ref), (vv, v_ref)):
                sw = _roll(t, HEAD)
                ref[:, (2 * c) * 128:(2 * c + 1) * 128] = jnp.where(lo, t, sw).astype(BF16)
                ref[:, (2 * c + 1) * 128:(2 * c + 2) * 128] = jnp.where(lo, sw, t).astype(BF16)

    row = lambda w: pl.BlockSpec((tm, w), lambda i: (i, 0))
    vec = pl.BlockSpec((1, LANES), lambda i: (0, 0))
    return _pc(
        body, name="swa_pre_fwd", grid=(S // tm,),
        in_specs=[row(1536), row(768), vec, vec, pl.BlockSpec((LANES, LANES), lambda i: (0, 0))],
        out_specs=[row(1024), row(512), row(512)],
        out_shape=[jax.ShapeDtypeStruct((S, w), BF16) for w in (1024, 512, 512)],
        compiler_params=_params(("parallel",), VMEM_LIMIT_WIDE),
    )(proj, tab, qg, kg, _group_matrix())


def _swa_pre_bwd(proj, tab, qg, kg, dq, dk, dv):
    S = proj.shape[0]
    tm = 512

    def body(p_ref, tab_ref, qg_ref, kg_ref, g_ref, dq_ref, dk_ref, dv_ref, dp_ref, db_ref, dqg_ref, dkg_ref):
        Ap, Bp, Cp = _tab(tab_ref, 1)
        G = g_ref[...]
        lo = _head_mask((tm, LANES), 0)

        @pl.when(pl.program_id(0) == 0)
        def _():
            db_ref[...] = jnp.zeros_like(db_ref)
            dqg_ref[...] = jnp.zeros_like(dqg_ref)
            dkg_ref[...] = jnp.zeros_like(dkg_ref)

        accq = jnp.zeros((1, LANES), F32)
        acck = jnp.zeros((1, LANES), F32)
        for c in range(8):
            sl = slice(c * 128, (c + 1) * 128)
            dx, dg = _hn_bwd(p_ref[:, sl], qg_ref[...], _rope_t(dq_ref[:, sl], Ap, Bp, Cp, 8), G)
            dp_ref[:, sl] = dx.astype(BF16)
            db_ref[:, sl] += jnp.sum(dx, axis=0, keepdims=True)
            accq = accq + dg
        for c in range(2):
            folded = []
            for ref in (dk_ref, dv_ref):
                a = ref[:, (2 * c) * 128:(2 * c + 1) * 128]
                b = ref[:, (2 * c + 1) * 128:(2 * c + 2) * 128]
                folded.append(jnp.where(lo, a + _roll(a, HEAD), b + _roll(b, HEAD)))
            ks = slice(1024 + c * 128, 1024 + (c + 1) * 128)
            dx, dg = _hn_bwd(p_ref[:, ks], kg_ref[...], _rope_t(folded[0], Ap, Bp, Cp, 8), G)
            dp_ref[:, ks] = dx.astype(BF16)
            db_ref[:, ks] += jnp.sum(dx, axis=0, keepdims=True)
            acck = acck + dg
            vs = slice(1280 + c * 128, 1280 + (c + 1) * 128)
            dp_ref[:, vs] = folded[1].astype(BF16)
            db_ref[:, vs] += jnp.sum(folded[1], axis=0, keepdims=True)
        dqg_ref[...] += _fold_halves(accq)
        dkg_ref[...] += _fold_halves(acck)

    row = lambda w: pl.BlockSpec((tm, w), lambda i: (i, 0))
    vec = pl.BlockSpec((1, LANES), lambda i: (0, 0))
    return _pc(
        body, name="swa_pre_bwd", grid=(S // tm,),
        in_specs=[row(1536), row(768), vec, vec, pl.BlockSpec((LANES, LANES), lambda i: (0, 0)),
                  row(1024), row(512), row(512)],
        out_specs=[row(1536), pl.BlockSpec((1, 1536), lambda i: (0, 0)), vec, vec],
        out_shape=[jax.ShapeDtypeStruct((S, 1536), BF16), jax.ShapeDtypeStruct((1, 1536), F32),
                   jax.ShapeDtypeStruct((1, LANES), F32), jax.ShapeDtypeStruct((1, LANES), F32)],
        compiler_params=_params(("arbitrary",), VMEM_LIMIT_WIDE),
    )(proj, tab, qg, kg, _group_matrix(), dq, dk, dv)


def _relu2_of(u):
    r = jnp.maximum(u.astype(F32), 0.0)
    return r * r


def _drelu2(acc, u):
    return (acc * 2.0 * jnp.maximum(u.astype(F32), 0.0),)


def _add(acc, res):
    return (acc + res,)


def _add_norm_in(res, g):
    def epilogue(acc, r, gv):
        xn = acc + r
        return xn, xn * lax.rsqrt(jnp.mean(xn * xn, axis=-1, keepdims=True) + EPS) * gv

    return dict(outs=[F32, BF16], epilogue=epilogue, extras=[(res, "mn"), (g.reshape(1, D_MODEL), "n")])


_T = dict(tm=1024, tn=1024, tk=1024)


def _rms_bwd_in(x, g, dres):
    def epilogue(dh, xv, gv, dr):
        r = lax.rsqrt(jnp.mean(xv * xv, axis=-1, keepdims=True) + EPS)
        t = dh * gv
        dx = dr + r * t - xv * (r * r * r) * jnp.mean(xv * t, axis=-1, keepdims=True)
        return dx, dx, jnp.sum(dh * xv * r, axis=0, keepdims=True)

    return dict(outs=[F32, BF16, ("colsum",)], epilogue=epilogue,
                extras=[(x, "mn"), (g.reshape(1, D_MODEL), "n"), (dres, "mn")])


def _delta_in(o, col0):
    width = D_MODEL - col0

    def epilogue(do, ov, G):
        parts = [_gmean(do[:, col0 + c * 128:col0 + (c + 1) * 128] * ov[:, c * 128:(c + 1) * 128], G) * float(HEAD)
                 for c in range(width // LANES)]
        return do, jnp.concatenate(parts, axis=1)

    return dict(outs=[F32, (F32, width)], epilogue=epilogue, extras=[(o, width), (_group_matrix(), "full")])


def _loss_in(res, target):
    def epilogue(acc, r, t):
        e = acc + r - t
        dy = e * (1.0 / D_MODEL)
        return dy, dy, jnp.sum(e * e, axis=0, keepdims=True) * (0.5 / D_MODEL)

    return dict(outs=[F32, BF16, ("colsum",)], epilogue=epilogue, extras=[(res, "mn"), (target, "mn")])


def _mlp_fwd(h, wts, layer, tag, tail):
    u = _matmul(h, wts, dims="nn", **_T, outs=[BF16], b_cs=True, b_row0=layer, name=f"mlp_up{tag}")
    out = _matmul(u, wts, dims="nn", **_T, a_pro=_relu2_of, b_rs=1024, b_row0=2 + layer, name=f"mlp_down{tag}", **tail)
    return out, (h, u)


def _mlp_bwd(x, g, wts, layer, saved, dy, dyb, tag):
    h, u = saved
    du = _matmul(dyb, wts, dims="nt", **_T, outs=[BF16], epilogue=_drelu2, extras=[(u, "mn")], b_rs=1024,
                 b_row0=2 + layer, name=f"mlp_du{tag}")
    dw_dn = _matmul(u, dyb, dims="tn", **_T, outs=[F32], a_pro=_relu2_of, name=f"mlp_dwdown{tag}")
    dw_up = _matmul(h, du, dims="tn", **_T, outs=[F32], o_cs=N_CHIPS, name=f"mlp_dwup{tag}")
    dx, dxb, dg = _matmul(du, wts, dims="nt", **_T, b_cs=True, b_row0=layer, b_rows=1024, name=f"mlp_dh{tag}",
                          vmem=VMEM_LIMIT_WIDE, **_rms_bwd_in(x, g, dy))
    return dx, dxb, dg, dw_up, dw_dn


def _local_step(x, pos_col, target, first_of, rest_begin, rest_of, P, red):
    S = x.shape[0]
    tab = _tables(pos_col)
    tile2 = lambda g: jnp.tile(g.reshape(1, HEAD), (1, 2))
    dqg, dkg = tile2(P["dil_q_gain"]), tile2(P["dil_k_gain"])
    sqg, skg = tile2(P["swa_q_gain"]), tile2(P["swa_k_gain"])
    gn = P["ret_gn_gain"].reshape(1, 512)
    sink_b = jnp.repeat(P["swa_sinks"].reshape(16), HEAD).reshape(1, 1024)

    h0 = _rms_fwd(x, P["norm_mix"][0], "rms_mix_fwd0")
    W = first_of((h0, tab))
    proj = _matmul(h0, W["hyb_w_in"], dims="nn", tm=1024, tn=768, tk=1024, outs=[F32], b_cs=True, name="hyb_in")
    rq, rk, rv, dq, dk, dv = _even_pre_fwd(proj, tab, dqg, dkg)
    ro, states = _ret_fwd(rq, rk, rv)
    dil = [(w // r, r) for w, r in DIL_PATTERNS]
    da, dlse = _band_fwd(dq, dk, dv, patterns=dil, nq=1, name="dil_fwd")
    mixed = rest_begin(_even_post_fwd(ro, proj, gn, da))
    x1, h1 = _matmul(mixed, W["hyb_w_out"], dims="nn", **_T, name="hyb_out", **_add_norm_in(x, P["norm_mlp"][0]))
    rest, bias = rest_of(x1)
    W = {**W, **rest}
    (x2, h2), mlp0 = _mlp_fwd(h1, W["packed"], 0, "0", _add_norm_in(x1, P["norm_mix"][1]))

    proj2 = _matmul(h2, W["swa_w_qkv"], dims="nn", tm=1024, tn=384, tk=1024, outs=[F32], b_cs=True,
                    epilogue=_add, extras=[(bias.reshape(1, 1536), "n")], name="swa_qkv")
    sq, sk, sv = _swa_pre_fwd(proj2, tab, sqg, skg)
    swa = [(SWA_DIST, 1)]
    so, slse, so_b = _band_fwd(sq, sk, sv, patterns=swa, nq=2, name="swa_fwd", sinks=sink_b, want_bf16=True)
    x3, h3 = _matmul(so_b, W["swa_w_out"], dims="nn", **_T, name="swa_out", **_add_norm_in(x2, P["norm_mlp"][1]))
    (dy, dyb, loss_cols), mlp1 = _mlp_fwd(h3, W["packed"], 1, "1", _loss_in(x3, target))
    loss = jnp.broadcast_to(jnp.sum(loss_cols), (1, LANES))

    gw, gp = {}, {}
    dx3, dx3b, dg_mlp1, gw["mlp_w_up1"], gw["mlp_w_down1"] = _mlp_bwd(x3, P["norm_mlp"][1], W["packed"], 1, mlp1, dy, dyb, "1")
    dx3b = red.begin("mlp1", {n: (gw[n], 1024) for n in ("mlp_w_up1", "mlp_w_down1")}, dx3b)
    gw["swa_w_out"] = _matmul(so_b, dx3b, dims="tn", **_T, outs=[F32], name="swa_dwout")
    dso, sdelta = _matmul(dx3b, W["swa_w_out"], dims="nt", **_T, name="swa_do", vmem=VMEM_LIMIT_WIDE, **_delta_in(so, 0))
    dsq, dsk, dsv, dsink = _band_bwd(sq, sk, sv, slse, sdelta, dso, patterns=swa, nq=2, name="swa_bwd", sinks=sink_b)
    dproj2, gp["swa_b_qkv"], gp["swa_q_gain"], gp["swa_k_gain"] = _swa_pre_bwd(proj2, tab, sqg, skg, dsq, dsk, dsv)
    gp["swa_sinks"] = dsink
    gw["swa_w_qkv"] = _matmul(h2, dproj2, dims="tn", tm=1024, tn=384, tk=1024, outs=[F32], o_cs=N_CHIPS, name="swa_dwqkv")
    dx2, dx2b, dg_mix1 = _matmul(dproj2, W["swa_w_qkv"], dims="nt", tm=1024, tn=1024, tk=384, b_cs=True, name="swa_dh",
                                 vmem=VMEM_LIMIT_WIDE, **_rms_bwd_in(x2, P["norm_mix"][1], dx3))
    dx2b = red.begin("swa", {"swa_w_qkv": (gw["swa_w_qkv"], 1024), "swa_w_out": (gw["swa_w_out"], 256)}, dx2b)
    dx2b = red.advance("mlp1", dx2b, dx2b)

    dx1, dx1b, dg_mlp0, gw["mlp_w_up0"], gw["mlp_w_down0"] = _mlp_bwd(x1, P["norm_mlp"][0], W["packed"], 0, mlp0, dx2, dx2b, "0")
    gw["hyb_w_out"] = _matmul(mixed, dx1b, dims="tn", **_T, outs=[F32], name="hyb_dwout")
    dx1b = red.begin("mlp0", {"mlp_w_up0": (gw["mlp_w_up0"], 1024), "mlp_w_down0": (gw["mlp_w_down0"], 1024),
                              "hyb_w_out": (gw["hyb_w_out"], 256)}, dx1b)
    dx1b = red.advance("swa", dx1b, dx1b)
    red.finish("mlp1", dx1b)
    dmixed, ddelta = _matmul(dx1b, W["hyb_w_out"], dims="nt", **_T, name="hyb_dmixed", vmem=VMEM_LIMIT_WIDE,
                             **_delta_in(da, 512))
    dro, drg, gp["ret_gn_gain"] = _even_post_bwd(ro, proj, gn, dmixed)
    drq, drk, drv = _ret_bwd(rq, rk, rv, states, dro)
    ddq, ddk, ddv = _band_bwd(dq, dk, dv, dlse, ddelta, dmixed, patterns=dil, nq=1, name="dil_bwd", do_col0=4)
    ddq = red.advance("mlp0", ddq, ddq)
    red.finish("swa", ddq)
    dproj, gp["dil_q_gain"], gp["dil_k_gain"] = _even_pre_bwd(proj, tab, dqg, dkg, drq, drk, drv, drg, [ddq], [ddk], [ddv])
    gw["hyb_w_in"] = _matmul(h0, dproj, dims="tn", tm=1024, tn=768, tk=1024, outs=[F32], o_cs=N_CHIPS, name="hyb_dwin")
    dproj = red.begin("win", {"hyb_w_in": (gw["hyb_w_in"], 1024)}, dproj)
    grad_x, _, dg_mix0 = _matmul(dproj, W["hyb_w_in"], dims="nt", tm=1024, tn=1024, tk=768, b_cs=True, name="hyb_dh",
                                 vmem=VMEM_LIMIT_WIDE, **_rms_bwd_in(x, P["norm_mix"][0], dx1))
    red.finish("mlp0", grad_x)
    gp["norm_mix"] = jnp.concatenate([dg_mix0, dg_mix1], axis=0)
    gp["norm_mlp"] = jnp.concatenate([dg_mlp0, dg_mlp1], axis=0)
    return loss, grad_x, gp


HBM = pl.BlockSpec(memory_space=pltpu.HBM)


def _place():
    x, y, c = lax.axis_index("x"), lax.axis_index("y"), lax.axis_index("c")
    chips = [(1 - x, y), (x, 1 - y), (1 - x, 1 - y)]
    return x, y, c, chips


SEM = pl.BlockSpec(memory_space=pltpu.SEMAPHORE)
EFFECT = pltpu.SideEffectType.DATAFLOW_SIDE_EFFECTING


def _half_block(ref, chip, core):
    rh = ref.shape[1] // 2
    return ref.at[2 * chip[0] + chip[1], pl.ds(core * rh, rh), :]


def _gather_start(buf, ride, name):
    def body(b_ref, ride_ref, s0, s1, s2, r0, r1, r2, b_out, ride_out):
        x, y, c, chips = _place()
        for chip, s, r in zip(chips, (s0, s1, s2), (r0, r1, r2)):
            mine = _half_block(b_ref, (x, y), c)
            pltpu.make_async_remote_copy(src_ref=mine, dst_ref=mine, send_sem=s, recv_sem=r,
                                         device_id=(*chip, c), device_id_type=MESH).start()

    sem = pltpu.SemaphoreType.DMA(())
    return _pc(
        body, name=name,
        out_shape=(sem,) * 6 + (pltpu.HBM(buf.shape, buf.dtype), pltpu.HBM(ride.shape, ride.dtype)),
        in_specs=(HBM, HBM), out_specs=(SEM,) * 6 + (HBM, HBM), input_output_aliases={0: 6, 1: 7},
        compiler_params=pltpu.CompilerParams(has_side_effects=EFFECT),
    )(pltpu.with_memory_space_constraint(buf, pltpu.HBM), pltpu.with_memory_space_constraint(ride, pltpu.HBM))


def _gather_wait(buf, sems, afters, name):
    def body(b_ref, s0, s1, s2, r0, r1, r2, *unread):
        x, y, c, chips = _place()
        for chip, s, r in zip(chips, (s0, s1, s2), (r0, r1, r2)):
            cp = pltpu.make_async_remote_copy(src_ref=_half_block(b_ref, (x, y), c), dst_ref=_half_block(b_ref, chip, c),
                                              send_sem=s, recv_sem=r, device_id=(*chip, c), device_id_type=MESH)
            cp.wait_send()
            cp.wait_recv()

    return _pc(
        body, name=name, out_shape=pltpu.HBM(buf.shape, buf.dtype),
        in_specs=(HBM,) + (SEM,) * 6 + (pl.BlockSpec(memory_space=pl.ANY),) * len(afters), out_specs=HBM,
        input_output_aliases={0: 0}, compiler_params=pltpu.CompilerParams(has_side_effects=EFFECT),
    )(buf, *sems, *afters)


def _gather_handover(buf, name):
    def body(b_ref, out_ref, send_sems, recv_sems):
        x, y, c, chips = _place()
        cps = []
        for k, chip in enumerate(chips):
            mine = _half_block(b_ref, chip, c)
            cps.append(pltpu.make_async_remote_copy(src_ref=mine, dst_ref=mine, send_sem=send_sems.at[k],
                                                    recv_sem=recv_sems.at[k], device_id=(x, y, 1 - c), device_id_type=MESH))
        for cp in cps:
            cp.start()
        for k, chip in enumerate(chips):
            theirs = _half_block(b_ref, chip, 1 - c)
            pltpu.make_async_remote_copy(src_ref=theirs, dst_ref=theirs, send_sem=send_sems.at[k], recv_sem=recv_sems.at[k],
                                         device_id=(x, y, 1 - c), device_id_type=MESH).wait_recv()
        for cp in cps:
            cp.wait_send()

    return _pc(
        body, name=name, in_specs=[HBM], out_specs=HBM,
        out_shape=jax.ShapeDtypeStruct(buf.shape, buf.dtype), input_output_aliases={0: 0},
        scratch_shapes=[pltpu.SemaphoreType.DMA((3,)), pltpu.SemaphoreType.DMA((3,))],
    )(buf)


def _pair_sum(t, l, place, name):
    _, r, cols = t.shape
    rh = r // 2
    tr = min(rh, 256)
    nr = rh // tr

    def body(pl_ref, t_ref, l_ref, o_ref):
        o_ref[...] = (t_ref[...] + l_ref[...]).astype(BF16)

    other = lambda s, p: s + jnp.where(s >= p[0], 1, 0)
    return _pc(
        body, name=name,
        grid_spec=pltpu.PrefetchScalarGridSpec(
            num_scalar_prefetch=1, grid=(N_CHIPS - 1, nr),
            in_specs=[pl.BlockSpec((None, tr, cols), lambda s, i, p: (other(s, p), p[1] * nr + i, 0)),
                      pl.BlockSpec((None, tr, cols), lambda s, i, p: (other(s, p), i, 0))],
            out_specs=pl.BlockSpec((None, tr, cols), lambda s, i, p: (other(s, p), i, 0))),
        out_shape=jax.ShapeDtypeStruct((N_CHIPS, rh, cols), BF16),
        compiler_params=_params(("parallel", "parallel")),
    )(place, t, l)


def _final_sum(t, l, rcv, place, name, layer=0, layers=1, into=None):
    _, r, cols = t.shape
    rh = r // 2
    tr = min(rh, 256)
    nr = rh // tr

    def body(pl_ref, t_ref, l_ref, r_ref, *rest):
        acc = t_ref[...] + l_ref[...]
        for k in range(3):
            acc = acc + r_ref[k].astype(F32)
        rest[-1][...] = acc

    in_specs = [pl.BlockSpec((None, tr, cols), lambda i, p: (p[0], p[1] * nr + i, 0)),
                pl.BlockSpec((None, tr, cols), lambda i, p: (p[0], i, 0)),
                pl.BlockSpec((3, tr, cols), lambda i, p: (0, i, 0))]
    args = [place, t, l, rcv]
    aliases = {}
    if into is not None:
        in_specs.append(pl.BlockSpec(memory_space=pl.ANY))
        args.append(into)
        aliases = {4: 0}
    return _pc(
        body, name=name,
        grid_spec=pltpu.PrefetchScalarGridSpec(
            num_scalar_prefetch=1, grid=(nr,), in_specs=in_specs,
            out_specs=pl.BlockSpec((tr, cols), lambda i, p: (2 * nr * layer + p[1] * nr + i, 0))),
        out_shape=jax.ShapeDtypeStruct((layers * r, cols), F32), input_output_aliases=aliases,
        compiler_params=_params(("parallel",)),
    )(*args)


def _share_halves(hs, name):
    nt = len(hs)
    n = sum(layers for _, layers in hs)

    def body(*refs):
        h_refs, send_sems, recv_sems = refs[:nt], refs[-2], refs[-1]
        x, y, c, _ = _place()
        cps = []
        for k, (_, layers) in enumerate(hs):
            rh = h_refs[k].shape[0] // (2 * layers)
            for layer in range(layers):
                half = h_refs[k].at[pl.ds((2 * layer + c) * rh, rh), :]
                cps.append(pltpu.make_async_remote_copy(
                    src_ref=half, dst_ref=half, send_sem=send_sems.at[len(cps)], recv_sem=recv_sems.at[len(cps)],
                    device_id=(x, y, 1 - c), device_id_type=MESH))
        for cp in cps:
            cp.start()
        for cp in cps:
            cp.wait()

    return _pc(
        body, name=name, in_specs=[HBM] * nt, out_specs=[HBM] * nt,
        out_shape=[jax.ShapeDtypeStruct(h.shape, F32) for h, _ in hs],
        input_output_aliases={k: k for k in range(nt)},
        scratch_shapes=[pltpu.SemaphoreType.DMA((n,)), pltpu.SemaphoreType.DMA((n,))],
    )(*[h for h, _ in hs])


def _split_start(name, bufs, ride, n, copies_of):
    nb = len(bufs)

    def body(*refs):
        sems = refs[nb + 1:nb + 1 + 2 * n]
        for cp in copies_of(refs[:nb], sems[:n], sems[n:]):
            (cp[0] if isinstance(cp, tuple) else cp).start()

    outs = _pc(
        body, name=name,
        out_shape=(pltpu.SemaphoreType.DMA(()),) * (2 * n) + tuple(pltpu.HBM(b.shape, b.dtype) for b in bufs)
        + (pltpu.HBM(ride.shape, ride.dtype),),
        in_specs=(HBM,) * (nb + 1), out_specs=(SEM,) * (2 * n) + (HBM,) * (nb + 1),
        input_output_aliases={k: 2 * n + k for k in range(nb + 1)},
        compiler_params=pltpu.CompilerParams(has_side_effects=EFFECT),
    )(*[pltpu.with_memory_space_constraint(b, pltpu.HBM) for b in bufs], pltpu.with_memory_space_constraint(ride, pltpu.HBM))
    return list(outs[:2 * n]), list(outs[2 * n:2 * n + nb]), outs[-1]


def _split_wait(name, bufs, sems, after, n, copies_of):
    nb = len(bufs)

    def body(*refs):
        s = refs[nb:nb + 2 * n]
        for cp in copies_of(refs[:nb], s[:n], s[n:]):
            sent, landed = cp if isinstance(cp, tuple) else (cp, cp)
            sent.wait_send()
            landed.wait_recv()

    outs = _pc(
        body, name=name, out_shape=tuple(pltpu.HBM(b.shape, b.dtype) for b in bufs),
        in_specs=(HBM,) * nb + (SEM,) * (2 * n) + (pl.BlockSpec(memory_space=pl.ANY),), out_specs=(HBM,) * nb,
        input_output_aliases={k: k for k in range(nb)},
        compiler_params=pltpu.CompilerParams(has_side_effects=EFFECT),
    )(*bufs, *sems, after)
    return list(outs)


def _handover_copies(refs, send, recv):
    x, y, c, chips = _place()
    cps = []
    for k, chip in enumerate(chips):
        mine, theirs = _half_block(refs[0], chip, c), _half_block(refs[0], chip, 1 - c)
        desc = lambda blk: pltpu.make_async_remote_copy(src_ref=blk, dst_ref=blk, send_sem=send[k], recv_sem=recv[k],
                                                        device_id=(x, y, 1 - c), device_id_type=MESH)
        cps.append((desc(mine), desc(theirs)))
    return cps


def _share_copies(layers_of):
    def copies_of(refs, send, recv):
        x, y, c, _ = _place()
        cps = []
        for k, layers in enumerate(layers_of):
            rh = refs[k].shape[0] // (2 * layers)
            for layer in range(layers):
                i = len(cps)
                desc = lambda half: pltpu.make_async_remote_copy(
                    src_ref=refs[k].at[pl.ds((2 * layer + half) * rh, rh), :], dst_ref=refs[k].at[pl.ds((2 * layer + half) * rh, rh), :],
                    send_sem=send[i], recv_sem=recv[i], device_id=(x, y, 1 - c), device_id_type=MESH)
                cps.append((desc(c), desc(1 - c)))
        return cps
    return copies_of


def _swap_copies(nt):
    def copies_of(refs, send, recv):
        x, y, c, _ = _place()
        cps = []
        for k in range(nt):
            rh = refs[k].shape[1] // 2
            cps.append(pltpu.make_async_remote_copy(
                src_ref=refs[k].at[:, pl.ds((1 - c) * rh, rh), :], dst_ref=refs[nt + k],
                send_sem=send[k], recv_sem=recv[k], device_id=(x, y, 1 - c), device_id_type=MESH))
        return cps
    return copies_of


def _exchange_copies(nt):
    def copies_of(refs, send, recv):
        x, y, c, chips = _place()
        cps = []
        for t in range(nt):
            for k, chip in enumerate(chips):
                cps.append(pltpu.make_async_remote_copy(
                    src_ref=refs[t].at[2 * chip[0] + chip[1]], dst_ref=refs[nt + t].at[k],
                    send_sem=send[3 * t + k], recv_sem=recv[3 * t + k], device_id=(*chip, c), device_id_type=MESH))
        return cps
    return copies_of


class _StagedReduce:
    def __init__(self, place):
        self.place = place
        self.groups = {}
        self.halves = {}

    @staticmethod
    def slab(t, r):
        return t.reshape(N_CHIPS, r, t.size // (N_CHIPS * r))

    def begin(self, g, grads, ride):
        names = list(grads)
        ts = [self.slab(t, r) for t, r in grads.values()]
        lands = [lax.empty((N_CHIPS, t.shape[1] // 2, t.shape[2]), F32) for t in ts]
        sems, bufs, ride = _split_start(f"grad_swap_start_{g}", ts + lands, ride, len(ts), _swap_copies(len(ts)))
        self.groups[g] = dict(names=names, bufs=bufs, sems=sems)
        return ride

    def advance(self, g, after, ride):
        st = self.groups[g]
        nt = len(st["names"])
        bufs = _split_wait(f"grad_swap_wait_{g}", st["bufs"], st["sems"], after, nt, _swap_copies(nt))
        st["ts"], st["ls"] = bufs[:nt], bufs[nt:]
        ps = [_pair_sum(t, l, self.place, f"pair_sum_{n}") for t, l, n in zip(st["ts"], st["ls"], st["names"])]
        lands = [lax.empty((3,) + p.shape[1:], BF16) for p in ps]
        st["sems"], st["bufs"], ride = _split_start(f"grad_exchange_start_{g}", ps + lands, ride, 3 * nt, _exchange_copies(nt))
        return ride

    def finish(self, g, after):
        st = self.groups[g]
        nt = len(st["names"])
        bufs = _split_wait(f"grad_exchange_wait_{g}", st["bufs"], st["sems"], after, 3 * nt, _exchange_copies(nt))
        for t, l, r, n in zip(st["ts"], st["ls"], bufs[nt:], st["names"]):
            if n[-1] in "01":
                self.halves[n[:-1]] = _final_sum(t, l, r, self.place, f"final_sum_{n}", layer=int(n[-1]), layers=2,
                                                 into=self.halves.get(n[:-1]))
            else:
                self.halves[n] = _final_sum(t, l, r, self.place, f"final_sum_{n}")


def _allgather_small(v):
    rows = v.shape[0]

    def body(v_ref, out_ref, send_sems, recv_sems):
        x, y, c, _ = _place()
        me = 4 * x + 2 * y + c
        out_ref[me] = v_ref[...]
        cps = []
        for k in range(1, 8):
            fx, fy, fc = (k >> 2) & 1, (k >> 1) & 1, k & 1
            to = (1 - x if fx else x, 1 - y if fy else y, 1 - c if fc else c)
            cps.append(pltpu.make_async_remote_copy(
                src_ref=v_ref, dst_ref=out_ref.at[me], send_sem=send_sems.at[k - 1], recv_sem=recv_sems.at[k - 1],
                device_id=to, device_id_type=MESH))
        for cp in cps:
            cp.start()
        for cp in cps:
            cp.wait()

    return _pc(
        body, name="allgather_small",
        in_specs=[pl.BlockSpec(memory_space=pltpu.VMEM)], out_specs=pl.BlockSpec(memory_space=pltpu.VMEM),
        out_shape=jax.ShapeDtypeStruct((8, rows, LANES), F32),
        scratch_shapes=[pltpu.SemaphoreType.DMA((7,)), pltpu.SemaphoreType.DMA((7,))],
    )(v)


def _adamw_math(w, g, m, v):
    m = ADAM_B1 * m + (1.0 - ADAM_B1) * g
    v = ADAM_B2 * v + (1.0 - ADAM_B2) * (g * g)
    m_hat = m / (1.0 - ADAM_B1 ** ADAM_STEP)
    v_hat = v / (1.0 - ADAM_B2 ** ADAM_STEP)
    return -ADAM_LR * (m_hat / (jnp.sqrt(v_hat) + ADAM_EPS) + ADAM_WD * w), m, v


def _adamw(w, g, m, v, name):
    r, cols = w.shape
    tr = min(r, 256)

    def body(w_ref, g_ref, m_ref, v_ref, go_ref, d_ref, mo_ref, vo_ref):
        gv = g_ref[...]
        d, mn, vn = _adamw_math(w_ref[...], gv, m_ref[...], v_ref[...])
        go_ref[...] = gv
        d_ref[...] = d
        mo_ref[...] = mn
        vo_ref[...] = vn

    row = pl.BlockSpec((tr, cols), lambda i: (i, 0))
    return _pc(
        body, name=name, grid=(r // tr,), in_specs=[row] * 4, out_specs=[row] * 4,
        out_shape=[jax.ShapeDtypeStruct((r, cols), F32)] * 4,
        compiler_params=_params(("parallel",)),
    )(w, g, m, v)


def _adamw_small(w, gathered, m, v):
    rows = w.shape[0]

    def body(w_ref, g_ref, m_ref, v_ref, go_ref, d_ref, mo_ref, vo_ref):
        g = g_ref[0]
        for k in range(1, 8):
            g = g + g_ref[k]
        d, mn, vn = _adamw_math(w_ref[...], g, m_ref[...], v_ref[...])
        go_ref[...] = g
        d_ref[...] = d
        mo_ref[...] = mn
        vo_ref[...] = vn

    return _pc(
        body, name="adamw_small",
        out_shape=[jax.ShapeDtypeStruct((rows, LANES), F32)] * 4,
    )(w, gathered, m, v)


_BIAS_ROWS = 32


def _own_slot(flat, chip):
    return lax.dynamic_update_slice(lax.empty((N_CHIPS,) + flat.shape, flat.dtype), flat[None], (chip, 0, 0))


def _pack_first(hyb_w_in, hyb_w_out):
    return jnp.concatenate([t.astype(BF16).reshape(-1, 1024) for t in (hyb_w_in, hyb_w_out)], axis=0)


def _unpack_first(g):
    return {"hyb_w_in": g[:, 0:768, :].reshape(N_CHIPS, 1024, 768), "hyb_w_out": g[:, 768:1024, :].reshape(1024, 1024)}


def _pack_rest(mlp_w_up, mlp_w_down, swa_w_qkv, swa_w_out, swa_b_qkv):
    parts = [t.astype(BF16).reshape(-1, 1024) for t in (mlp_w_up, mlp_w_down, swa_w_qkv, swa_w_out)]
    bias = lax.bitcast_convert_type(swa_b_qkv.reshape(384), BF16).reshape(1, 768)
    bias = jnp.pad(bias, ((0, _BIAS_ROWS - 1), (0, 256)))
    return jnp.concatenate(parts + [bias], axis=0)


def _unpack_rest(g):
    W = {
        "packed": g,
        "swa_w_qkv": g[:, 4096:4480, :].reshape(N_CHIPS, 1024, 384),
        "swa_w_out": g[:, 4480:4736, :].reshape(1024, 1024),
    }
    bias = lax.bitcast_convert_type(g[:, 4736, :768].reshape(N_CHIPS, 384, 2), F32).reshape(1536)
    return W, bias


_SMALL = (("norm_mix", 16), ("norm_mlp", 16), ("ret_gn_gain", 4), ("dil_q_gain", 1), ("dil_k_gain", 1),
          ("swa_b_qkv", 12), ("swa_q_gain", 1), ("swa_k_gain", 1), ("swa_sinks", 1), ("loss", 1))
_SUBLANES = 8


def _slot(r):
    return -(-r // _SUBLANES) * _SUBLANES


def _pack_small(d):
    return jnp.concatenate([jnp.pad(d[n].reshape(r, LANES), ((0, _slot(r) - r), (0, 0))) for n, r in _SMALL], axis=0)


def _unpack_small(p):
    out, o = {}, 0
    for n, r in _SMALL:
        out[n] = p[o:o + r]
        o += _slot(r)
    return out


def kernel(x, positions, norm_mix, norm_mlp, mlp_w_up, mlp_w_down, hyb_w_in, hyb_w_out, ret_gn_gain, dil_q_gain, dil_k_gain, swa_w_qkv, swa_b_qkv, swa_w_out, swa_q_gain, swa_k_gain, swa_sinks, loss_target, m_norm_mix, m_norm_mlp, m_mlp_w_up, m_mlp_w_down, m_hyb_w_in, m_hyb_w_out, m_ret_gn_gain, m_dil_q_gain, m_dil_k_gain, m_swa_w_qkv, m_swa_b_qkv, m_swa_w_out, m_swa_q_gain, m_swa_k_gain, m_swa_sinks, v_norm_mix, v_norm_mlp, v_mlp_w_up, v_mlp_w_down, v_hyb_w_in, v_hyb_w_out, v_ret_gn_gain, v_dil_q_gain, v_dil_k_gain, v_swa_w_qkv, v_swa_b_qkv, v_swa_w_out, v_swa_q_gain, v_swa_k_gain, v_swa_sinks):
    ax, ay, ac = lax.axis_index("x"), lax.axis_index("y"), lax.axis_index("c")
    chip = 2 * ax + ay
    place = jnp.stack([chip, ac]).astype(jnp.int32)
    S = x.shape[1]

    first = _own_slot(_pack_first(hyb_w_in[0], hyb_w_out[0]), chip)
    rest = _own_slot(_pack_rest(mlp_w_up, mlp_w_down, swa_w_qkv[0], swa_w_out[0], swa_b_qkv[0]), chip)
    *sems, first, pos_col = _gather_start(first, positions.reshape(S, 1), "allgather_first_start")
    flight = {}

    def first_of(afters):
        g = _gather_handover(_gather_wait(first, sems, [*afters, rest], "allgather_first_wait"), "allgather_first_handover")
        *flight["sems"], flight["buf"], g = _gather_start(rest, g, "allgather_rest_start")
        return _unpack_first(g)

    def rest_begin(ride):
        buf = _gather_wait(flight["buf"], flight["sems"], [ride], "allgather_rest_wait")
        flight["sems"], flight["bufs"], ride = _split_start("allgather_rest_handover_start", [buf], ride, 3, _handover_copies)
        return ride

    def rest_of(after):
        return _unpack_rest(_split_wait("allgather_rest_handover_wait", flight["bufs"], flight["sems"], after, 3,
                                        _handover_copies)[0])

    P = dict(norm_mix=norm_mix, norm_mlp=norm_mlp, ret_gn_gain=ret_gn_gain, dil_q_gain=dil_q_gain, dil_k_gain=dil_k_gain,
             swa_q_gain=swa_q_gain, swa_k_gain=swa_k_gain, swa_sinks=swa_sinks)

    red = _StagedReduce(place)
    loss_l, grad_x, gp = _local_step(x[0], pos_col, loss_target[0], first_of, rest_begin, rest_of, P, red)

    params = dict(mlp_w_up=(mlp_w_up, m_mlp_w_up, v_mlp_w_up), mlp_w_down=(mlp_w_down, m_mlp_w_down, v_mlp_w_down),
                  hyb_w_in=(hyb_w_in, m_hyb_w_in, v_hyb_w_in), hyb_w_out=(hyb_w_out, m_hyb_w_out, v_hyb_w_out),
                  swa_w_qkv=(swa_w_qkv, m_swa_w_qkv, v_swa_w_qkv), swa_w_out=(swa_w_out, m_swa_w_out, v_swa_w_out))
    big = {}

    def adamw_of(n, g):
        rows = g.shape[0]
        w, m, v = (t.reshape(rows, -1) for t in params[n])
        big[n] = [t.reshape(params[n][0].shape) for t in _adamw(w, g, m, v, f"adamw_{n}")]

    names = ["mlp_w_up", "mlp_w_down", "hyb_w_out", "swa_w_qkv", "swa_w_out"]
    red.halves[names[0]] = red.advance("win", grad_x, red.halves[names[0]])

    gsm = dict(gp, loss=loss_l)
    gsm["swa_sinks"] = jnp.pad(gp["swa_sinks"].reshape(16, HEAD)[:, 0], (0, LANES - 16))
    layers = [params[n][0].shape[0] for n in names]
    sems, shared, packed = _split_start("grad_share_start", [red.halves[n] for n in names], _pack_small(gsm), sum(layers),
                                        _share_copies(layers))
    gathered = _allgather_small(packed)

    def small_pack(norm_mix, norm_mlp, gn, dq, dk, b, sq, sk, sinks):
        dup = lambda t: jnp.tile(t.reshape(1, HEAD), (1, 2))
        bias = lax.dynamic_update_slice(jnp.zeros((12, LANES), F32), b.reshape(3, LANES), (3 * chip, 0))
        return _pack_small(dict(norm_mix=norm_mix, norm_mlp=norm_mlp, ret_gn_gain=gn, dil_q_gain=dup(dq), dil_k_gain=dup(dk),
                                swa_b_qkv=bias, swa_q_gain=dup(sq), swa_k_gain=dup(sk),
                                swa_sinks=jnp.pad(sinks.reshape(16), (0, LANES - 16)), loss=jnp.zeros((1, LANES), F32)))

    pw = small_pack(norm_mix, norm_mlp, ret_gn_gain, dil_q_gain, dil_k_gain, swa_b_qkv, swa_q_gain, swa_k_gain, swa_sinks)
    pm = small_pack(m_norm_mix, m_norm_mlp, m_ret_gn_gain, m_dil_q_gain, m_dil_k_gain, m_swa_b_qkv, m_swa_q_gain, m_swa_k_gain, m_swa_sinks)
    pv = small_pack(v_norm_mix, v_norm_mlp, v_ret_gn_gain, v_dil_q_gain, v_dil_k_gain, v_swa_b_qkv, v_swa_q_gain, v_swa_k_gain, v_swa_sinks)
    small_flat = _adamw_small(pw, gathered, pm, pv)
    small = [_unpack_small(t) for t in small_flat]

    for n, g in zip(names, _split_wait("grad_share_wait", shared, sems, small_flat[1], sum(layers), _share_copies(layers))):
        adamw_of(n, g)
    red.finish("win", big[names[-1]][1])
    adamw_of("hyb_w_in", _share_halves([(red.halves["hyb_w_in"], 1)], "grad_share_last")[0])

    def small_out(n, k):
        t = small[k][n]
        if n in ("norm_mix", "norm_mlp"):
            return t.reshape(2, D_MODEL)
        if n == "ret_gn_gain":
            return t.reshape(1, RET_HEADS, 128)
        if n == "swa_b_qkv":
            return lax.dynamic_slice(t, (3 * chip, 0), (3, LANES)).reshape(1, 384)
        if n == "swa_sinks":
            return t[0, :16].reshape(1, 16)
        return t[0, :HEAD].reshape(1, HEAD)

    order = ["norm_mix", "norm_mlp", "mlp_w_up", "mlp_w_down", "hyb_w_in", "hyb_w_out", "ret_gn_gain", "dil_q_gain",
             "dil_k_gain", "swa_w_qkv", "swa_b_qkv", "swa_w_out", "swa_q_gain", "swa_k_gain", "swa_sinks"]
    is_big = {"mlp_w_up", "mlp_w_down", "hyb_w_in", "hyb_w_out", "swa_w_qkv", "swa_w_out"}
    outs = [small[0]["loss"][0, 0], grad_x[None]]
    for k in range(4):
        outs += [big[n][k] if n in is_big else small_out(n, k) for n in order]
    return tuple(outs)
```

```python
import numpy as np
import jax
import jax.numpy as jnp
from jax import lax
from jax.experimental import pallas as pl
from jax.experimental.pallas import tpu as pltpu

F32, BF16 = jnp.float32, jnp.bfloat16
MESH = pl.DeviceIdType.MESH

LANES = 128
VMEM_LIMIT = 48 << 20
VMEM_LIMIT_WIDE = 60 << 20
D_MODEL = 1024
HEAD = 64
EPS = 1e-6
BLK = 128
RET_HEADS = 4
RET_THETA = 10000.0
ROPE_THETA = 500000.0
ROPE_DIMS = 16
DIL_PATTERNS = ((128, 1), (512, 4), (2048, 16))
SWA_DIST = 127
N_CHIPS = 4
ADAM_LR, ADAM_B1, ADAM_B2, ADAM_EPS, ADAM_WD, ADAM_STEP = 0.001, 0.9, 0.999, 1e-08, 0.01, 10

_LOG_GAMMA = [float(np.log1p(-np.exp2(np.float32(-5.0 - h)))) for h in range(RET_HEADS)]


def _pc(body, **kw):
    return pl.pallas_call(body, **kw)


def _params(sem, limit=VMEM_LIMIT):
    return pltpu.CompilerParams(dimension_semantics=sem, vmem_limit_bytes=limit)


def _matmul(a, b, *, dims, tm, tn, tk, outs, name, epilogue=None, extras=(), b_cs=False, b_rs=0, b_row0=0, b_rows=0,
            b_cols=0, o_cs=0, a_pro=None, vmem=VMEM_LIMIT):
    if dims == "nn":
        M, K = a.shape
        N = b.shape[0] * (b_cols or b.shape[2]) if b_cs else b.shape[1]
        a_spec = pl.BlockSpec((tm, tk), lambda i, j, k: (i, k))
        if b_cs:
            npt = (b_cols or b.shape[2]) // tn
            b_spec = pl.BlockSpec((None, tk, tn), lambda i, j, k: (j // npt, k + b_row0, j % npt))
        elif b_rs:
            K, N, kps = b.shape[0] * b_rs, b.shape[2], b_rs // tk
            b_spec = pl.BlockSpec((None, tk, tn), lambda i, j, k: (k // kps, b_row0 + k % kps, j))
        else:
            b_spec = pl.BlockSpec((tk, tn), lambda i, j, k: (k, j))
        contract = (((1,), (0,)), ((), ()))
    elif dims == "nt":
        M, K = a.shape
        N = (b_rows or b.shape[1]) if b_cs else b.shape[0]
        a_spec = pl.BlockSpec((tm, tk), lambda i, j, k: (i, k))
        if b_cs:
            kpt = (b_cols or b.shape[2]) // tk
            b_spec = pl.BlockSpec((None, tn, tk), lambda i, j, k: (k // kpt, j + b_row0, k % kpt))
        elif b_rs:
            N, jps = b.shape[0] * b_rs, b_rs // tn
            b_spec = pl.BlockSpec((None, tn, tk), lambda i, j, k: (j // jps, b_row0 + j % jps, k))
        else:
            b_spec = pl.BlockSpec((tn, tk), lambda i, j, k: (j, k))
        contract = (((1,), (1,)), ((), ()))
    else:
        K, M = a.shape
        N = b.shape[1]
        a_spec = pl.BlockSpec((tk, tm), lambda i, j, k: (k, i))
        b_spec = pl.BlockSpec((tk, tn), lambda i, j, k: (k, j))
        contract = (((0,), (0,)), ((), ()))
    assert M % tm == 0 and N % tn == 0 and K % tk == 0, (name, M, N, K, tm, tn, tk)
    nk = K // tk
    ex_specs = []
    for arr, kind in extras:
        if kind == "mn":
            ex_specs.append(pl.BlockSpec((tm, tn), lambda i, j, k: (i, j)))
        elif kind == "n":
            ex_specs.append(pl.BlockSpec((1, tn), lambda i, j, k: (0, j)))
        elif kind == "full":
            ex_specs.append(pl.BlockSpec(arr.shape, lambda i, j, k, nd=arr.ndim: (0,) * nd))
        else:
            ex_specs.append(pl.BlockSpec((tm, kind), lambda i, j, k: (i, 0)))
    if o_cs:
        n_sh = N // o_cs
        opt = n_sh // tn
        o_shape = (o_cs, M, n_sh)
        o_spec = pl.BlockSpec((None, tm, tn), lambda i, j, k: (j // opt, i, j % opt))
    else:
        o_shape = (M, N)
        o_spec = pl.BlockSpec((tm, tn), lambda i, j, k: (i, j))
    o_specs, o_shapes, summed = [], [], []
    for o in outs:
        if isinstance(o, tuple) and o[0] == "colsum":
            assert N == tn
            o_specs.append(pl.BlockSpec((1, tn), lambda i, j, k: (0, j)))
            o_shapes.append(jax.ShapeDtypeStruct((1, N), F32))
            summed.append(True)
        elif isinstance(o, tuple):
            o_specs.append(pl.BlockSpec((tm, o[1]), lambda i, j, k: (i, 0)))
            o_shapes.append(jax.ShapeDtypeStruct((M, o[1]), o[0]))
            summed.append(False)
        else:
            o_specs.append(o_spec)
            o_shapes.append(jax.ShapeDtypeStruct(o_shape, o))
            summed.append(False)
    n_ex, n_out = len(extras), len(outs)
    if epilogue is None:
        epilogue = lambda acc: (acc,)

    def body(a_ref, b_ref, *rest):
        ex, o_refs, acc = rest[:n_ex], rest[n_ex:n_ex + n_out], rest[-1]
        i, k = pl.program_id(0), pl.program_id(2)

        @pl.when(k == 0)
        def _():
            acc[...] = jnp.zeros_like(acc)

        av = a_ref[...] if a_pro is None else a_pro(a_ref[...])
        acc[...] += lax.dot_general(av.astype(BF16), b_ref[...].astype(BF16), contract, preferred_element_type=F32)

        @pl.when(k == nk - 1)
        def _():
            vals = epilogue(acc[...], *[e[...] for e in ex])
            for r, v, sm in zip(o_refs, vals, summed):
                if sm:
                    @pl.when(i == 0)
                    def _(r=r):
                        r[...] = jnp.zeros_like(r)

                    r[...] += v
                else:
                    r[...] = v.astype(r.dtype)

    res = _pc(
        body, name=name, grid=(M // tm, N // tn, nk),
        in_specs=[a_spec, b_spec] + ex_specs, out_specs=o_specs, out_shape=o_shapes,
        scratch_shapes=[pltpu.VMEM((tm, tn), F32)],
        compiler_params=_params(("arbitrary" if any(summed) else "parallel", "parallel", "arbitrary"), vmem),
    )(a, b, *[e for e, _ in extras])
    return res[0] if n_out == 1 else res


def _roll(x, s):
    return pltpu.roll(x, s % LANES, 1)


def _rope(x, A, B, C, half):
    return x * A + _roll(x, LANES - half) * B + _roll(x, half) * C


def _rope_t(g, A, B, C, half):
    return g * A + _roll(g * B, half) + _roll(g * C, LANES - half)


def _gmean(x, G):
    hi = x.astype(BF16)
    lo = (x - hi.astype(F32)).astype(BF16)
    Gb = G.astype(BF16)
    return jnp.dot(hi, Gb, preferred_element_type=F32) + jnp.dot(lo, Gb, preferred_element_type=F32)


def _head_mask(shape, half):
    lane = lax.broadcasted_iota(jnp.int32, shape, len(shape) - 1)
    return (lane >= HEAD) if half else (lane < HEAD)


def _group_matrix():
    i = np.arange(LANES)
    return jnp.asarray((i[:, None] // HEAD == i[None, :] // HEAD).astype(np.float32) / HEAD)


def _rope_inv():
    l = np.arange(LANES) % HEAD
    inv_r = np.power(np.float32(RET_THETA), -(l % 32).astype(np.float32) * np.float32(2.0 / HEAD))
    hp = ROPE_DIMS // 2
    inv_p = np.power(np.float32(ROPE_THETA), -(l % hp).astype(np.float32) * np.float32(2.0 / ROPE_DIMS))
    inv_p = np.where(l < ROPE_DIMS, inv_p, 0.0)
    return jnp.asarray(np.stack([inv_r, inv_p]).astype(np.float32))


def _tables(pos_col):
    S = pos_col.shape[0]
    tm = 512
    hp = ROPE_DIMS // 2

    def body(p_ref, inv_ref, o_ref):
        p = p_ref[...].astype(F32)
        lane = lax.broadcasted_iota(jnp.int32, (tm, LANES), 1) % HEAD
        ang = p * inv_ref[0:1, :]
        c, s = jnp.cos(ang), jnp.sin(ang)
        o_ref[:, 0:128] = c
        o_ref[:, 128:256] = jnp.where(lane < 32, -s, 0.0)
        o_ref[:, 256:384] = jnp.where(lane >= 32, s, 0.0)
        ang = p * inv_ref[1:2, :]
        c, s = jnp.cos(ang), jnp.sin(ang)
        o_ref[:, 384:512] = c
        o_ref[:, 512:640] = jnp.where(lane < hp, -s, 0.0)
        o_ref[:, 640:768] = jnp.where((lane >= hp) & (lane < ROPE_DIMS), s, 0.0)

    return _pc(
        body, name="rope_tables", grid=(S // tm,),
        in_specs=[pl.BlockSpec((tm, 1), lambda i: (i, 0)), pl.BlockSpec((2, LANES), lambda i: (0, 0))],
        out_specs=pl.BlockSpec((tm, 768), lambda i: (i, 0)),
        out_shape=jax.ShapeDtypeStruct((S, 768), F32),
        compiler_params=_params(("parallel",)),
    )(pos_col, _rope_inv())


def _tab(tab_ref, which):
    o = 384 * which
    return tab_ref[:, o:o + 128], tab_ref[:, o + 128:o + 256], tab_ref[:, o + 256:o + 384]


def _rms_fwd(x, g, name):
    S, Dm = x.shape
    tm = 512

    def body(x_ref, g_ref, h_ref):
        xv = x_ref[...]
        r = lax.rsqrt(jnp.mean(xv * xv, axis=-1, keepdims=True) + EPS)
        h_ref[...] = (xv * r * g_ref[...]).astype(BF16)

    return _pc(
        body, name=name, grid=(S // tm,),
        in_specs=[pl.BlockSpec((tm, Dm), lambda i: (i, 0)), pl.BlockSpec((1, Dm), lambda i: (0, 0))],
        out_specs=pl.BlockSpec((tm, Dm), lambda i: (i, 0)),
        out_shape=jax.ShapeDtypeStruct((S, Dm), BF16),
        compiler_params=_params(("parallel",)),
    )(x, g.reshape(1, Dm))


def _hn_fwd(x, gain, G):
    r = lax.rsqrt(_gmean(x * x, G) + EPS)
    return x * r * gain


def _hn_bwd(x, gain, dy, G):
    r = lax.rsqrt(_gmean(x * x, G) + EPS)
    t = dy * gain
    dx = r * t - x * (r * r * r) * _gmean(x * t, G)
    return dx, jnp.sum(dy * x * r, axis=0, keepdims=True)


def _fold_halves(v):
    return v + _roll(v, HEAD)


def _even_pre_fwd(proj, tab, qg, kg):
    S = proj.shape[0]
    tm = 512

    def body(p_ref, tab_ref, qg_ref, kg_ref, g_ref, rq_ref, rk_ref, rv_ref, dq_ref, dk_ref, dv_ref):
        Ar, Br, Cr = _tab(tab_ref, 0)
        Ap, Bp, Cp = _tab(tab_ref, 1)
        G = g_ref[...]
        for c in range(2):
            sl = slice(c * 128, (c + 1) * 128)
            rq_ref[:, sl] = _rope(p_ref[:, c * 128:(c + 1) * 128], Ar, Br, Cr, 32).astype(BF16)
            rk_ref[:, sl] = (_rope(p_ref[:, 256 + c * 128:256 + (c + 1) * 128], Ar, Br, Cr, 32) * 0.125).astype(BF16)
        rv_ref[...] = p_ref[:, 512:1024].astype(BF16)
        for c in range(4):
            sl = slice(c * 128, (c + 1) * 128)
            q = _hn_fwd(p_ref[:, 1536 + c * 128:1536 + (c + 1) * 128], qg_ref[...], G)
            dq_ref[:, sl] = _rope(q, Ap, Bp, Cp, 8).astype(BF16)
            k = _hn_fwd(p_ref[:, 2048 + c * 128:2048 + (c + 1) * 128], kg_ref[...], G)
            dk_ref[:, sl] = _rope(k, Ap, Bp, Cp, 8).astype(BF16)
        dv_ref[...] = p_ref[:, 2560:3072].astype(BF16)

    row = lambda w: pl.BlockSpec((tm, w), lambda i: (i, 0))
    vec = pl.BlockSpec((1, LANES), lambda i: (0, 0))
    return _pc(
        body, name="even_pre_fwd", grid=(S // tm,),
        in_specs=[row(3072), row(768), vec, vec, pl.BlockSpec((LANES, LANES), lambda i: (0, 0))],
        out_specs=[row(256), row(256), row(512), row(512), row(512), row(512)],
        out_shape=[jax.ShapeDtypeStruct((S, w), BF16) for w in (256, 256, 512, 512, 512, 512)],
        compiler_params=_params(("parallel",), VMEM_LIMIT_WIDE),
    )(proj, tab, qg, kg, _group_matrix())


def _even_pre_bwd(proj, tab, qg, kg, drq, drk, drv, drg, dqs, dks, dvs):
    S = proj.shape[0]
    tm = 512
    npat = len(dqs)

    def body(p_ref, tab_ref, qg_ref, kg_ref, g_ref, drq_ref, drk_ref, drv_ref, drg_ref, *rest):
        dq_refs, dk_refs, dv_refs = rest[:npat], rest[npat:2 * npat], rest[2 * npat:3 * npat]
        dp_ref, dqg_ref, dkg_ref = rest[3 * npat:]
        Ar, Br, Cr = _tab(tab_ref, 0)
        Ap, Bp, Cp = _tab(tab_ref, 1)
        G = g_ref[...]
        for c in range(2):
            sl = slice(c * 128, (c + 1) * 128)
            dp_ref[:, c * 128:(c + 1) * 128] = _rope_t(drq_ref[:, sl], Ar, Br, Cr, 32).astype(BF16)
            dp_ref[:, 256 + c * 128:256 + (c + 1) * 128] = _rope_t(drk_ref[:, sl] * 0.125, Ar, Br, Cr, 32).astype(BF16)
        dp_ref[:, 512:1024] = drv_ref[...].astype(BF16)
        dp_ref[:, 1024:1536] = drg_ref[...].astype(BF16)
        accq = jnp.zeros((1, LANES), F32)
        acck = jnp.zeros((1, LANES), F32)
        for c in range(4):
            sl = slice(c * 128, (c + 1) * 128)
            g = dq_refs[0][:, sl]
            for r in dq_refs[1:]:
                g = g + r[:, sl]
            dx, dg = _hn_bwd(p_ref[:, 1536 + c * 128:1536 + (c + 1) * 128], qg_ref[...], _rope_t(g, Ap, Bp, Cp, 8), G)
            dp_ref[:, 1536 + c * 128:1536 + (c + 1) * 128] = dx.astype(BF16)
            accq = accq + dg
            g = dk_refs[0][:, sl]
            for r in dk_refs[1:]:
                g = g + r[:, sl]
            dx, dg = _hn_bwd(p_ref[:, 2048 + c * 128:2048 + (c + 1) * 128], kg_ref[...], _rope_t(g, Ap, Bp, Cp, 8), G)
            dp_ref[:, 2048 + c * 128:2048 + (c + 1) * 128] = dx.astype(BF16)
            acck = acck + dg
        g = dv_refs[0][...]
        for r in dv_refs[1:]:
            g = g + r[...]
        dp_ref[:, 2560:3072] = g.astype(BF16)

        @pl.when(pl.program_id(0) == 0)
        def _():
            dqg_ref[...] = jnp.zeros_like(dqg_ref)
            dkg_ref[...] = jnp.zeros_like(dkg_ref)

        dqg_ref[...] += _fold_halves(accq)
        dkg_ref[...] += _fold_halves(acck)

    row = lambda w: pl.BlockSpec((tm, w), lambda i: (i, 0))
    vec = pl.BlockSpec((1, LANES), lambda i: (0, 0))
    return _pc(
        body, name="even_pre_bwd", grid=(S // tm,),
        in_specs=[row(3072), row(768), vec, vec, pl.BlockSpec((LANES, LANES), lambda i: (0, 0)),
                  row(256), row(256), row(512), row(512)] + [row(512)] * (3 * npat),
        out_specs=[row(3072), vec, vec],
        out_shape=[jax.ShapeDtypeStruct((S, 3072), BF16), jax.ShapeDtypeStruct((1, LANES), F32),
                   jax.ShapeDtypeStruct((1, LANES), F32)],
        compiler_params=_params(("arbitrary",), VMEM_LIMIT_WIDE),
    )(proj, tab, qg, kg, _group_matrix(), drq, drk, drv, drg, *dqs, *dks, *dvs)


def _ret_consts(pair, half):
    lg = jnp.where(pair == 0, _LOG_GAMMA[half], _LOG_GAMMA[2 + half]).astype(F32)
    i = lax.broadcasted_iota(jnp.int32, (BLK, BLK), 0)
    j = lax.broadcasted_iota(jnp.int32, (BLK, BLK), 1)
    diff = (i - j).astype(F32)
    decay = jnp.where(diff >= 0, jnp.exp(lg * jnp.maximum(diff, 0.0)), 0.0)
    t = lax.broadcasted_iota(jnp.int32, (BLK, 1), 0).astype(F32)
    xi = jnp.exp(lg * (t + 1.0))
    zeta = jnp.exp(lg * (BLK - 1.0 - t))
    cd = jnp.exp(jnp.full((1, 1), BLK, F32) * lg)
    return decay, xi, zeta, cd


RET_STEP = 8


def _ret_fwd(rq, rk, rv):
    S = rq.shape[0]
    nc = S // BLK
    rows = RET_STEP * BLK

    def body(q_ref, k_ref, v_ref, o_ref, st_ref, R):
        p, n = pl.program_id(0), pl.program_id(1)

        @pl.when(n == 0)
        def _():
            R[...] = jnp.zeros_like(R)

        consts = [_ret_consts(p, half) for half in range(2)]
        masks = [_head_mask((BLK, LANES), half) for half in range(2)]
        for ci in range(RET_STEP):
            rs = slice(ci * BLK, (ci + 1) * BLK)
            q2, k2 = q_ref[rs, :], k_ref[rs, :]
            for half in range(2):
                decay, xi, zeta, cd = consts[half]
                m = masks[half]
                qm = jnp.where(m, q2, jnp.zeros_like(q2))
                km = jnp.where(m, k2, jnp.zeros_like(k2))
                v = v_ref[rs, half * 128:(half + 1) * 128]
                Rb = R[half].astype(BF16)
                st_ref[ci, half] = Rb
                sc = lax.dot_general(qm, k2, (((1,), (1,)), ((), ())), preferred_element_type=F32) * decay
                o = jnp.dot(sc.astype(BF16), v, preferred_element_type=F32)
                o = o + jnp.dot(qm, Rb, preferred_element_type=F32) * xi
                o_ref[rs, half * 128:(half + 1) * 128] = o
                kz = (km.astype(F32) * zeta).astype(BF16)
                R[half] = R[half] * cd + lax.dot_general(kz, v, (((0,), (0,)), ((), ())), preferred_element_type=F32)

    return _pc(
        body, name="ret_fwd", grid=(2, nc // RET_STEP),
        in_specs=[pl.BlockSpec((rows, 128), lambda p, n: (n, p)), pl.BlockSpec((rows, 128), lambda p, n: (n, p)),
                  pl.BlockSpec((rows, 256), lambda p, n: (n, p))],
        out_specs=[pl.BlockSpec((rows, 256), lambda p, n: (n, p)),
                   pl.BlockSpec((None, RET_STEP, 2, 128, 128), lambda p, n: (p, n, 0, 0, 0))],
        out_shape=[jax.ShapeDtypeStruct((S, 512), F32), jax.ShapeDtypeStruct((2, nc, 2, 128, 128), BF16)],
        scratch_shapes=[pltpu.VMEM((2, 128, 128), F32)],
        compiler_params=_params(("parallel", "arbitrary")),
    )(rq, rk, rv)


def _ret_bwd(rq, rk, rv, states, do):
    S = rq.shape[0]
    nc = S // BLK
    ns = nc // RET_STEP
    rows = RET_STEP * BLK
    nt = (((1,), (1,)), ((), ()))
    tn = (((0,), (0,)), ((), ()))

    def body(q_ref, k_ref, v_ref, st_ref, do_ref, dq_ref, dk_ref, dv_ref, U):
        p, n = pl.program_id(0), pl.program_id(1)

        @pl.when(n == 0)
        def _():
            U[...] = jnp.zeros_like(U)

        consts = [_ret_consts(p, half) for half in range(2)]
        masks = [_head_mask((BLK, LANES), half) for half in range(2)]
        for ci in reversed(range(RET_STEP)):
            rs = slice(ci * BLK, (ci + 1) * BLK)
            q2, k2 = q_ref[rs, :], k_ref[rs, :]
            dq_acc = jnp.zeros((BLK, LANES), F32)
            dk_acc = jnp.zeros((BLK, LANES), F32)
            for half in range(2):
                decay, xi, zeta, cd = consts[half]
                m = masks[half]
                qm = jnp.where(m, q2, jnp.zeros_like(q2))
                km = jnp.where(m, k2, jnp.zeros_like(k2))
                v = v_ref[rs, half * 128:(half + 1) * 128]
                dob = do_ref[rs, half * 128:(half + 1) * 128].astype(BF16)
                Rb = st_ref[ci, half]
                Ub = U[half].astype(BF16)
                dsc = (lax.dot_general(dob, v, nt, preferred_element_type=F32) * decay).astype(BF16)
                xdo = (dob.astype(F32) * xi).astype(BF16)
                dq_acc += jnp.dot(dsc, km, preferred_element_type=F32) + lax.dot_general(xdo, Rb, nt, preferred_element_type=F32)
                dk_acc += lax.dot_general(dsc, qm, tn, preferred_element_type=F32) \
                    + lax.dot_general(v, Ub, nt, preferred_element_type=F32) * zeta
                sc = (lax.dot_general(qm, k2, nt, preferred_element_type=F32) * decay).astype(BF16)
                kz = (km.astype(F32) * zeta).astype(BF16)
                dv_ref[rs, half * 128:(half + 1) * 128] = lax.dot_general(sc, dob, tn, preferred_element_type=F32) \
                    + jnp.dot(kz, Ub, preferred_element_type=F32)
                U[half] = U[half] * cd + lax.dot_general(qm, xdo, tn, preferred_element_type=F32)
            dq_ref[rs, :] = dq_acc
            dk_ref[rs, :] = dk_acc

    rev = lambda w: pl.BlockSpec((rows, w), lambda p, n: (ns - 1 - n, p))
    return _pc(
        body, name="ret_bwd", grid=(2, ns),
        in_specs=[rev(128), rev(128), rev(256),
                  pl.BlockSpec((None, RET_STEP, 2, 128, 128), lambda p, n: (p, ns - 1 - n, 0, 0, 0)), rev(256)],
        out_specs=[rev(128), rev(128), rev(256)],
        out_shape=[jax.ShapeDtypeStruct((S, 256), F32), jax.ShapeDtypeStruct((S, 256), F32),
                   jax.ShapeDtypeStruct((S, 512), F32)],
        scratch_shapes=[pltpu.VMEM((2, 128, 128), F32)],
        compiler_params=_params(("parallel", "arbitrary")),
    )(rq, rk, rv, states, do)


ATT_TILE = 2048


def _rows(ref, start, n, r):
    if r == 1:
        return ref[pl.ds(start, n), :]
    return ref[pl.ds(start, n, stride=r), :]


def _twice(x):
    return jnp.concatenate([x, x], axis=0)


def _stack_heads(x, masks):
    zero = jnp.zeros_like(x)
    return jnp.concatenate([jnp.where(masks[0], x, zero), jnp.where(masks[1], x, zero)], axis=0)


def _set_rows(ref, start, n, r, val):
    if r == 1:
        ref[pl.ds(start, n), :] = val
    else:
        ref[pl.ds(start, n, stride=r), :] = val


def _band_geometry(S, patterns):
    rmax = max(r for _, r in patterns)
    H = BLK * rmax
    T = min(S, ATT_TILE)
    assert T % H == 0 and S % T == 0
    return H, T, S // T, T // BLK


def _band_fwd(q, k, v, *, patterns, nq, name, sinks=None, want_bf16=False):
    S, Ck = k.shape
    H, T, nt, nbt = _band_geometry(S, patterns)
    ncol = Ck // LANES
    scale = HEAD ** -0.5
    has_sink = sinks is not None
    nt_dims = (((1,), (1,)), ((), ()))

    def body(*refs):
        q_ref, kp_ref, kc_ref, vp_ref, vc_ref = refs[:5]
        sk_ref = refs[5] if has_sink else None
        n_out = 3 if want_bf16 else 2
        outs = refs[5 + has_sink:5 + has_sink + n_out]
        qf, kf, vf, M, L, A = refs[5 + has_sink + n_out:]
        t = pl.program_id(1)
        kf[0:H, :] = kp_ref[...].astype(F32)
        kf[H:H + T, :] = kc_ref[...].astype(F32)
        vf[0:H, :] = vp_ref[...].astype(F32)
        vf[H:H + T, :] = vc_ref[...].astype(F32)
        r_i = lax.broadcasted_iota(jnp.int32, (BLK, 2 * BLK), 0)
        c_i = lax.broadcasted_iota(jnp.int32, (BLK, 2 * BLK), 1)
        dist_i = r_i + BLK - c_i
        masks = [_head_mask((BLK, LANES), h) for h in range(2)]

        for i in range(nq):
            qf[...] = q_ref[:, i * 128:(i + 1) * 128].astype(F32) * scale
            for p, (dist, r) in enumerate(patterns):
                in_band = (dist_i >= 0) & (dist_i <= dist)
                in_band_first = in_band & ((c_i >= BLK) | (t > 0))
                in_band, in_band_first = _twice(in_band), _twice(in_band_first)

                def unit(j, b, p=p, r=r, in_band=in_band, in_band_first=in_band_first):
                    q0 = j + b * (BLK * r)
                    q2 = _rows(qf, q0, BLK, r).astype(BF16)
                    kcat = _rows(kf, H + q0 - BLK * r, 2 * BLK, r).astype(BF16)
                    vcat = _rows(vf, H + q0 - BLK * r, 2 * BLK, r).astype(BF16)
                    valid = in_band if b > 0 else in_band_first
                    s = lax.dot_general(_stack_heads(q2, masks), kcat, nt_dims, preferred_element_type=F32)
                    s = jnp.where(valid, s, -jnp.inf)
                    mx = jnp.max(s, axis=1, keepdims=True)
                    pr = jnp.exp(s - mx)
                    den = jnp.sum(pr, axis=1, keepdims=True)
                    pv = jnp.dot(pr.astype(BF16), vcat, preferred_element_type=F32)
                    m2 = jnp.where(masks[0], mx[:BLK], mx[BLK:])
                    l2 = jnp.where(masks[0], den[:BLK], den[BLK:])
                    a2 = jnp.where(masks[0], pv[:BLK], pv[BLK:])
                    if p > 0:
                        mo = _rows(M, q0, BLK, r)
                        mn = jnp.maximum(mo, m2)
                        wa, wb = jnp.exp(mo - mn), jnp.exp(m2 - mn)
                        l2 = wa * _rows(L, q0, BLK, r) + wb * l2
                        a2 = wa * _rows(A, q0, BLK, r) + wb * a2
                        m2 = mn
                    _set_rows(M, q0, BLK, r, m2)
                    _set_rows(L, q0, BLK, r, l2)
                    _set_rows(A, q0, BLK, r, a2)

                for u in range(nbt):
                    unit(u % r, u // r)
            sl = slice(i * 128, (i + 1) * 128)
            mm, ll, aa = M[...], L[...], A[...]
            if has_sink:
                snk = sk_ref[:, sl]
                mn = jnp.maximum(mm, snk)
                w = jnp.exp(mm - mn)
                ll = ll * w + jnp.exp(snk - mn)
                aa = aa * w
                mm = mn
            o = aa / ll
            outs[0][:, sl] = o
            outs[1][:, sl] = mm + jnp.log(ll)
            if want_bf16:
                outs[2][:, sl] = o.astype(BF16)

    th = T // H
    qspec = pl.BlockSpec((T, nq * 128), lambda j, t: (t, j))
    cur = pl.BlockSpec((T, 128), lambda j, t: (t, j))
    prev = pl.BlockSpec((H, 128), lambda j, t: (jnp.maximum(t * th - 1, 0), j))
    in_specs = [qspec, prev, cur, prev, cur]
    args = [q, k, k, v, v]
    if has_sink:
        in_specs.append(pl.BlockSpec((1, nq * 128), lambda j, t: (0, j)))
        args.append(sinks)
    out_dts = [F32, F32] + ([BF16] if want_bf16 else [])
    return _pc(
        body, name=name, grid=(ncol, nt), in_specs=in_specs,
        out_specs=[qspec] * len(out_dts),
        out_shape=[jax.ShapeDtypeStruct(q.shape, dt) for dt in out_dts],
        scratch_shapes=[pltpu.VMEM((T, LANES), F32), pltpu.VMEM((H + T, LANES), F32), pltpu.VMEM((H + T, LANES), F32),
                        pltpu.VMEM((T, LANES), F32), pltpu.VMEM((T, LANES), F32), pltpu.VMEM((T, LANES), F32)],
        compiler_params=_params(("parallel", "parallel")),
    )(*args)


def _band_bwd(q, k, v, lse, delta, do, *, patterns, nq, name, sinks=None, do_col0=0):
    S, Ck = k.shape
    H, T, nt, nbt = _band_geometry(S, patterns)
    ncol = Ck // LANES
    scale = HEAD ** -0.5
    has_sink = sinks is not None
    nt_dims = (((1,), (1,)), ((), ()))
    tn_dims = (((0,), (0,)), ((), ()))

    def body(*refs):
        (qc_ref, qn_ref, kp_ref, kc_ref, vp_ref, vc_ref, lc_ref, ln_ref, ec_ref, en_ref, dc_ref, dn_ref) = refs[:12]
        sk_ref = refs[12] if has_sink else None
        n_out = 4 if has_sink else 3
        outs = refs[12 + has_sink:12 + has_sink + n_out]
        dq_ref, dk_ref, dv_ref = outs[:3]
        qf, kf, vf, lf, ef, df = refs[12 + has_sink + n_out:]
        t = pl.program_id(1)
        kf[0:H, :] = kp_ref[...].astype(F32)
        kf[H:H + T, :] = kc_ref[...].astype(F32)
        vf[0:H, :] = vp_ref[...].astype(F32)
        vf[H:H + T, :] = vc_ref[...].astype(F32)
        dk_ref[...] = jnp.zeros_like(dk_ref)
        dv_ref[...] = jnp.zeros_like(dv_ref)
        r_i = lax.broadcasted_iota(jnp.int32, (BLK, 2 * BLK), 0)
        c_i = lax.broadcasted_iota(jnp.int32, (BLK, 2 * BLK), 1)
        dist_q = r_i + BLK - c_i
        dist_h = dist_q[:, :BLK]
        m1 = [_head_mask((BLK, LANES), h) for h in range(2)]

        def stacked_inputs(q2, do2, l2, e2):
            spread = lambda v: jnp.concatenate([jnp.where(m1[0], v, _roll(v, HEAD)), jnp.where(m1[1], v, _roll(v, HEAD))], axis=0)
            return _stack_heads(q2, m1), _stack_heads(do2.astype(BF16), m1), spread(l2), spread(e2)

        for i in range(nq):
            sl = slice(i * 128, (i + 1) * 128)
            qf[0:T, :] = qc_ref[:, sl].astype(F32) * scale
            qf[T:T + H, :] = qn_ref[:, sl].astype(F32) * scale
            for buf, c_ref, n_ref in ((lf, lc_ref, ln_ref), (ef, ec_ref, en_ref), (df, dc_ref, dn_ref)):
                buf[0:T, :] = c_ref[:, sl]
                buf[T:T + H, :] = n_ref[:, sl]
            if has_sink:
                @pl.when(t == 0)
                def _():
                    outs[3][:, sl] = jnp.zeros((1, LANES), F32)

                outs[3][:, sl] += jnp.sum(-jnp.exp(sk_ref[:, sl] - lc_ref[:, sl]) * ec_ref[:, sl], axis=0, keepdims=True)
            for p, (dist, r) in enumerate(patterns):
                band_q = (dist_q >= 0) & (dist_q <= dist)
                band_first = band_q & ((c_i >= BLK) | (t > 0))
                band_h = (dist_h >= 0) & (dist_h <= dist)
                band_q, band_first, band_h = _twice(band_q), _twice(band_first), _twice(band_h)

                def add_rows(ref, start, val, r=r):
                    _set_rows(ref, start, BLK, r, _rows(ref, start, BLK, r) + val)

                def unit(j, b, p=p, r=r, band_q=band_q, band_first=band_first):
                    q0 = j + b * (BLK * r)
                    q2 = _rows(qf, q0, BLK, r).astype(BF16)
                    do2, l2, e2 = _rows(df, q0, BLK, r), _rows(lf, q0, BLK, r), _rows(ef, q0, BLK, r)
                    kcat = _rows(kf, H + q0 - BLK * r, 2 * BLK, r).astype(BF16)
                    vcat = _rows(vf, H + q0 - BLK * r, 2 * BLK, r).astype(BF16)
                    valid = band_q if b > 0 else band_first
                    qs, dos, ls, es = stacked_inputs(q2, do2, l2, e2)
                    s = lax.dot_general(qs, kcat, nt_dims, preferred_element_type=F32)
                    pr = jnp.where(valid, jnp.exp(s - jnp.concatenate([ls, ls], axis=1)), 0.0)
                    dp = lax.dot_general(dos, vcat, nt_dims, preferred_element_type=F32)
                    ds = (pr * (dp - jnp.concatenate([es, es], axis=1))).astype(BF16)
                    dqs = jnp.dot(ds, kcat, preferred_element_type=F32) * scale
                    dq2 = jnp.where(m1[0], dqs[:BLK], dqs[BLK:])
                    dvc = lax.dot_general(pr.astype(BF16), dos, tn_dims, preferred_element_type=F32)
                    dkc = lax.dot_general(ds, qs, tn_dims, preferred_element_type=F32)
                    if p > 0:
                        dq2 = dq2 + _rows(dq_ref.at[:, sl], q0, BLK, r)
                    _set_rows(dq_ref.at[:, sl], q0, BLK, r, dq2)
                    add_rows(dk_ref, q0, dkc[BLK:])
                    add_rows(dv_ref, q0, dvc[BLK:])
                    if b > 0:
                        add_rows(dk_ref, q0 - BLK * r, dkc[:BLK])
                        add_rows(dv_ref, q0 - BLK * r, dvc[:BLK])

                def halo_unit(j, r=r, band_h=band_h):
                    k0 = j + (nbt // r - 1) * (BLK * r)
                    q2 = _rows(qf, T + j, BLK, r).astype(BF16)
                    do2, l2, e2 = _rows(df, T + j, BLK, r), _rows(lf, T + j, BLK, r), _rows(ef, T + j, BLK, r)
                    kc = _rows(kf, H + k0, BLK, r).astype(BF16)
                    vc = _rows(vf, H + k0, BLK, r).astype(BF16)
                    qs, dos, ls, es = stacked_inputs(q2, do2, l2, e2)
                    s = lax.dot_general(qs, kc, nt_dims, preferred_element_type=F32)
                    pr = jnp.where(band_h, jnp.exp(s - ls), 0.0)
                    dp = lax.dot_general(dos, vc, nt_dims, preferred_element_type=F32)
                    ds = (pr * (dp - es)).astype(BF16)
                    add_rows(dk_ref, k0, lax.dot_general(ds, qs, tn_dims, preferred_element_type=F32))
                    add_rows(dv_ref, k0, lax.dot_general(pr.astype(BF16), dos, tn_dims, preferred_element_type=F32))

                for u in range(nbt):
                    unit(u % r, u // r)
                if nt > 1:
                    @pl.when(t < nt - 1)
                    def _(r=r, halo_unit=halo_unit):
                        for j in range(r):
                            halo_unit(j)

    th = T // H
    last = S // H - 1
    qcur = pl.BlockSpec((T, nq * 128), lambda j, t: (t, j))
    qnext = pl.BlockSpec((H, nq * 128), lambda j, t: (jnp.minimum((t + 1) * th, last), j))
    cur = pl.BlockSpec((T, 128), lambda j, t: (t, j))
    prev = pl.BlockSpec((H, 128), lambda j, t: (jnp.maximum(t * th - 1, 0), j))
    dcur = pl.BlockSpec((T, nq * 128), lambda j, t: (t, j + do_col0))
    dnext = pl.BlockSpec((H, nq * 128), lambda j, t: (jnp.minimum((t + 1) * th, last), j + do_col0))
    in_specs = [qcur, qnext, prev, cur, prev, cur, qcur, qnext, qcur, qnext, dcur, dnext]
    args = [q, q, k, k, v, v, lse, lse, delta, delta, do, do]
    out_specs = [qcur, cur, cur]
    out_shape = [jax.ShapeDtypeStruct(q.shape, F32), jax.ShapeDtypeStruct(k.shape, F32), jax.ShapeDtypeStruct(k.shape, F32)]
    if has_sink:
        vec = pl.BlockSpec((1, nq * 128), lambda j, t: (0, j))
        in_specs.append(vec)
        args.append(sinks)
        out_specs.append(vec)
        out_shape.append(jax.ShapeDtypeStruct((1, q.shape[1]), F32))
    big = pltpu.VMEM((T + H, LANES), F32)
    return _pc(
        body, name=name, grid=(ncol, nt), in_specs=in_specs, out_specs=out_specs, out_shape=out_shape,
        scratch_shapes=[big] * 6,
        compiler_params=_params(("parallel", "arbitrary")),
    )(*args)


def _even_post_fwd(ro, proj, gn, da):
    S = ro.shape[0]
    tm = 512

    def body(ro_ref, rg_ref, gn_ref, da_ref, mix_ref):
        for c in range(4):
            sl = slice(c * 128, (c + 1) * 128)
            x = ro_ref[:, sl]
            mu = jnp.mean(x, axis=1, keepdims=True)
            xc = x - mu
            var = jnp.mean(xc * xc, axis=1, keepdims=True)
            y = xc * lax.rsqrt(var + EPS) * gn_ref[:, sl]
            z = rg_ref[:, sl]
            mix_ref[:, sl] = (z * jax.nn.sigmoid(z) * y).astype(BF16)
        mix_ref[:, 512:1024] = da_ref[...].astype(BF16)

    row = lambda w: pl.BlockSpec((tm, w), lambda i: (i, 0))
    return _pc(
        body, name="even_post_fwd", grid=(S // tm,),
        in_specs=[row(512), pl.BlockSpec((tm, 512), lambda i: (i, 2)), pl.BlockSpec((1, 512), lambda i: (0, 0)), row(512)],
        out_specs=row(1024), out_shape=jax.ShapeDtypeStruct((S, 1024), BF16),
        compiler_params=_params(("parallel",), VMEM_LIMIT_WIDE),
    )(ro, proj, gn, da)


def _even_post_bwd(ro, proj, gn, dmixed):
    S = ro.shape[0]
    tm = 512

    def body(ro_ref, rg_ref, gn_ref, dm_ref, dro_ref, drg_ref, dgn_ref):
        @pl.when(pl.program_id(0) == 0)
        def _():
            dgn_ref[...] = jnp.zeros_like(dgn_ref)

        for c in range(4):
            sl = slice(c * 128, (c + 1) * 128)
            x = ro_ref[:, sl]
            mu = jnp.mean(x, axis=1, keepdims=True)
            xc = x - mu
            rstd = lax.rsqrt(jnp.mean(xc * xc, axis=1, keepdims=True) + EPS)
            xh = xc * rstd
            gain = gn_ref[:, sl]
            y = xh * gain
            z = rg_ref[:, sl]
            sg = jax.nn.sigmoid(z)
            dra = dm_ref[:, sl]
            drg_ref[:, sl] = dra * y * sg * (1.0 + z * (1.0 - sg))
            dy = dra * z * sg
            dgn_ref[:, sl] += jnp.sum(dy * xh, axis=0, keepdims=True)
            dxh = dy * gain
            dro_ref[:, sl] = rstd * (dxh - jnp.mean(dxh, axis=1, keepdims=True)
                                     - xh * jnp.mean(dxh * xh, axis=1, keepdims=True))

    row = lambda w: pl.BlockSpec((tm, w), lambda i: (i, 0))
    vec = pl.BlockSpec((1, 512), lambda i: (0, 0))
    return _pc(
        body, name="even_post_bwd", grid=(S // tm,),
        in_specs=[row(512), pl.BlockSpec((tm, 512), lambda i: (i, 2)), vec, row(512)],
        out_specs=[row(512), row(512), vec],
        out_shape=[jax.ShapeDtypeStruct((S, 512), F32), jax.ShapeDtypeStruct((S, 512), F32),
                   jax.ShapeDtypeStruct((1, 512), F32)],
        compiler_params=_params(("arbitrary",), VMEM_LIMIT_WIDE),
    )(ro, proj, gn, dmixed)


def _swa_pre_fwd(proj, tab, qg, kg):
    S = proj.shape[0]
    tm = 512

    def body(p_ref, tab_ref, qg_ref, kg_ref, g_ref, q_ref, k_ref, v_ref):
        Ap, Bp, Cp = _tab(tab_ref, 1)
        G = g_ref[...]
        lo = _head_mask((tm, LANES), 0)
        for c in range(8):
            sl = slice(c * 128, (c + 1) * 128)
            q_ref[:, sl] = _rope(_hn_fwd(p_ref[:, sl], qg_ref[...], G), Ap, Bp, Cp, 8).astype(BF16)
        for c in range(2):
            kn = _rope(_hn_fwd(p_ref[:, 1024 + c * 128:1024 + (c + 1) * 128], kg_ref[...], G), Ap, Bp, Cp, 8)
            vv = p_ref[:, 1280 + c * 128:1280 + (c + 1) * 128]
            for t, ref in ((kn, k_ref), (vv, v_ref)):
                sw = _roll(t, HEAD)
                ref[:, (2 * c) * 128:(2 * c + 1) * 128] = jnp.where(lo, t, sw).astype(BF16)
                ref[:, (2 * c + 1) * 128:(2 * c + 2) * 128] = jnp.where(lo, sw, t).astype(BF16)

    row = lambda w: pl.BlockSpec((tm, w), lambda i: (i, 0))
    vec = pl.BlockSpec((1, LANES), lambda i: (0, 0))
    return _pc(
        body, name="swa_pre_fwd", grid=(S // tm,),
        in_specs=[row(1536), row(768), vec, vec, pl.BlockSpec((LANES, LANES), lambda i: (0, 0))],
        out_specs=[row(1024), row(512), row(512)],
        out_shape=[jax.ShapeDtypeStruct((S, w), BF16) for w in (1024, 512, 512)],
        compiler_params=_params(("parallel",), VMEM_LIMIT_WIDE),
    )(proj, tab, qg, kg, _group_matrix())


def _swa_pre_bwd(proj, tab, qg, kg, dq, dk, dv):
    S = proj.shape[0]
    tm = 512

    def body(p_ref, tab_ref, qg_ref, kg_ref, g_ref, dq_ref, dk_ref, dv_ref, dp_ref, db_ref, dqg_ref, dkg_ref):
        Ap, Bp, Cp = _tab(tab_ref, 1)
        G = g_ref[...]
        lo = _head_mask((tm, LANES), 0)

        @pl.when(pl.program_id(0) == 0)
        def _():
            db_ref[...] = jnp.zeros_like(db_ref)
            dqg_ref[...] = jnp.zeros_like(dqg_ref)
            dkg_ref[...] = jnp.zeros_like(dkg_ref)

        accq = jnp.zeros((1, LANES), F32)
        acck = jnp.zeros((1, LANES), F32)
        for c in range(8):
            sl = slice(c * 128, (c + 1) * 128)
            dx, dg = _hn_bwd(p_ref[:, sl], qg_ref[...], _rope_t(dq_ref[:, sl], Ap, Bp, Cp, 8), G)
            dp_ref[:, sl] = dx.astype(BF16)
            db_ref[:, sl] += jnp.sum(dx, axis=0, keepdims=True)
            accq = accq + dg
        for c in range(2):
            folded = []
            for ref in (dk_ref, dv_ref):
                a = ref[:, (2 * c) * 128:(2 * c + 1) * 128]
                b = ref[:, (2 * c + 1) * 128:(2 * c + 2) * 128]
                folded.append(jnp.where(lo, a + _roll(a, HEAD), b + _roll(b, HEAD)))
            ks = slice(1024 + c * 128, 1024 + (c + 1) * 128)
            dx, dg = _hn_bwd(p_ref[:, ks], kg_ref[...], _rope_t(folded[0], Ap, Bp, Cp, 8), G)
            dp_ref[:, ks] = dx.astype(BF16)
            db_ref[:, ks] += jnp.sum(dx, axis=0, keepdims=True)
            acck = acck + dg
            vs = slice(1280 + c * 128, 1280 + (c + 1) * 128)
            dp_ref[:, vs] = folded[1].astype(BF16)
            db_ref[:, vs] += jnp.sum(folded[1], axis=0, keepdims=True)
        dqg_ref[...] += _fold_halves(accq)
        dkg_ref[...] += _fold_halves(acck)

    row = lambda w: pl.BlockSpec((tm, w), lambda i: (i, 0))
    vec = pl.BlockSpec((1, LANES), lambda i: (0, 0))
    return _pc(
        body, name="swa_pre_bwd", grid=(S // tm,),
        in_specs=[row(1536), row(768), vec, vec, pl.BlockSpec((LANES, LANES), lambda i: (0, 0)),
                  row(1024), row(512), row(512)],
        out_specs=[row(1536), pl.BlockSpec((1, 1536), lambda i: (0, 0)), vec, vec],
        out_shape=[jax.ShapeDtypeStruct((S, 1536), BF16), jax.ShapeDtypeStruct((1, 1536), F32),
                   jax.ShapeDtypeStruct((1, LANES), F32), jax.ShapeDtypeStruct((1, LANES), F32)],
        compiler_params=_params(("arbitrary",), VMEM_LIMIT_WIDE),
    )(proj, tab, qg, kg, _group_matrix(), dq, dk, dv)


def _relu2_of(u):
    r = jnp.maximum(u.astype(F32), 0.0)
    return r * r


def _drelu2(acc, u):
    return (acc * 2.0 * jnp.maximum(u.astype(F32), 0.0),)


def _add(acc, res):
    return (acc + res,)


def _add_norm_in(res, g):
    def epilogue(acc, r, gv):
        xn = acc + r
        return xn, xn * lax.rsqrt(jnp.mean(xn * xn, axis=-1, keepdims=True) + EPS) * gv

    return dict(outs=[F32, BF16], epilogue=epilogue, extras=[(res, "mn"), (g.reshape(1, D_MODEL), "n")])


_T = dict(tm=1024, tn=1024, tk=1024)


def _rms_bwd_in(x, g, dres):
    def epilogue(dh, xv, gv, dr):
        r = lax.rsqrt(jnp.mean(xv * xv, axis=-1, keepdims=True) + EPS)
        t = dh * gv
        dx = dr + r * t - xv * (r * r * r) * jnp.mean(xv * t, axis=-1, keepdims=True)
        return dx, dx, jnp.sum(dh * xv * r, axis=0, keepdims=True)

    return dict(outs=[F32, BF16, ("colsum",)], epilogue=epilogue,
                extras=[(x, "mn"), (g.reshape(1, D_MODEL), "n"), (dres, "mn")])


def _delta_in(o, col0):
    width = D_MODEL - col0

    def epilogue(do, ov, G):
        parts = [_gmean(do[:, col0 + c * 128:col0 + (c + 1) * 128] * ov[:, c * 128:(c + 1) * 128], G) * float(HEAD)
                 for c in range(width // LANES)]
        return do, jnp.concatenate(parts, axis=1)

    return dict(outs=[F32, (F32, width)], epilogue=epilogue, extras=[(o, width), (_group_matrix(), "full")])


def _loss_in(res, target):
    def epilogue(acc, r, t):
        e = acc + r - t
        dy = e * (1.0 / D_MODEL)
        return dy, dy, jnp.sum(e * e, axis=0, keepdims=True) * (0.5 / D_MODEL)

    return dict(outs=[F32, BF16, ("colsum",)], epilogue=epilogue, extras=[(res, "mn"), (target, "mn")])


def _mlp_fwd(h, wts, layer, tag, tail):
    u = _matmul(h, wts, dims="nn", **_T, outs=[BF16], b_cs=True, b_row0=layer, name=f"mlp_up{tag}")
    out = _matmul(u, wts, dims="nn", **_T, a_pro=_relu2_of, b_rs=1024, b_row0=2 + layer, name=f"mlp_down{tag}", **tail)
    return out, (h, u)


def _mlp_bwd(x, g, wts, layer, saved, dy, dyb, tag):
    h, u = saved
    du = _matmul(dyb, wts, dims="nt", **_T, outs=[BF16], epilogue=_drelu2, extras=[(u, "mn")], b_rs=1024,
                 b_row0=2 + layer, name=f"mlp_du{tag}")
    dw_dn = _matmul(u, dyb, dims="tn", **_T, outs=[F32], a_pro=_relu2_of, name=f"mlp_dwdown{tag}")
    dw_up = _matmul(h, du, dims="tn", **_T, outs=[F32], o_cs=N_CHIPS, name=f"mlp_dwup{tag}")
    dx, dxb, dg = _matmul(du, wts, dims="nt", **_T, b_cs=True, b_row0=layer, b_rows=1024, name=f"mlp_dh{tag}",
                          vmem=VMEM_LIMIT_WIDE, **_rms_bwd_in(x, g, dy))
    return dx, dxb, dg, dw_up, dw_dn


def _local_step(x, pos_col, target, first_of, rest_begin, rest_of, P, red):
    S = x.shape[0]
    tab = _tables(pos_col)
    tile2 = lambda g: jnp.tile(g.reshape(1, HEAD), (1, 2))
    dqg, dkg = tile2(P["dil_q_gain"]), tile2(P["dil_k_gain"])
    sqg, skg = tile2(P["swa_q_gain"]), tile2(P["swa_k_gain"])
    gn = P["ret_gn_gain"].reshape(1, 512)
    sink_b = jnp.repeat(P["swa_sinks"].reshape(16), HEAD).reshape(1, 1024)

    h0 = _rms_fwd(x, P["norm_mix"][0], "rms_mix_fwd0")
    W = first_of((h0, tab))
    proj = _matmul(h0, W["hyb_w_in"], dims="nn", tm=1024, tn=768, tk=1024, outs=[F32], b_cs=True, b_cols=768, name="hyb_in")
    rq, rk, rv, dq, dk, dv = _even_pre_fwd(proj, tab, dqg, dkg)
    ro, states = _ret_fwd(rq, rk, rv)
    dil = [(w // r, r) for w, r in DIL_PATTERNS]
    da, dlse = _band_fwd(dq, dk, dv, patterns=dil, nq=1, name="dil_fwd")
    mixed = rest_begin(_even_post_fwd(ro, proj, gn, da))
    x1, h1 = _matmul(mixed, W["hyb_w_out"], dims="nn", **_T, name="hyb_out", **_add_norm_in(x, P["norm_mlp"][0]))
    rest, bias = rest_of(x1)
    W = {**W, **rest}
    (x2, h2), mlp0 = _mlp_fwd(h1, W["packed"], 0, "0", _add_norm_in(x1, P["norm_mix"][1]))

    proj2 = _matmul(h2, W["swa_w_qkv"], dims="nn", tm=1024, tn=384, tk=1024, outs=[F32], b_cs=True,
                    epilogue=_add, extras=[(bias.reshape(1, 1536), "n")], name="swa_qkv")
    sq, sk, sv = _swa_pre_fwd(proj2, tab, sqg, skg)
    swa = [(SWA_DIST, 1)]
    so, slse, so_b = _band_fwd(sq, sk, sv, patterns=swa, nq=2, name="swa_fwd", sinks=sink_b, want_bf16=True)
    x3, h3 = _matmul(so_b, W["swa_w_out"], dims="nn", **_T, name="swa_out", **_add_norm_in(x2, P["norm_mlp"][1]))
    (dy, dyb, loss_cols), mlp1 = _mlp_fwd(h3, W["packed"], 1, "1", _loss_in(x3, target))
    loss = jnp.broadcast_to(jnp.sum(loss_cols), (1, LANES))

    gw, gp = {}, {}
    dx3, dx3b, dg_mlp1, gw["mlp_w_up1"], gw["mlp_w_down1"] = _mlp_bwd(x3, P["norm_mlp"][1], W["packed"], 1, mlp1, dy, dyb, "1")
    dx3b = red.begin("mlp1", {n: (gw[n], 1024) for n in ("mlp_w_up1", "mlp_w_down1")}, dx3b)
    gw["swa_w_out"] = _matmul(so_b, dx3b, dims="tn", **_T, outs=[F32], name="swa_dwout")
    dso, sdelta = _matmul(dx3b, W["swa_w_out"], dims="nt", **_T, name="swa_do", vmem=VMEM_LIMIT_WIDE, **_delta_in(so, 0))
    dsq, dsk, dsv, dsink = _band_bwd(sq, sk, sv, slse, sdelta, dso, patterns=swa, nq=2, name="swa_bwd", sinks=sink_b)
    dproj2, gp["swa_b_qkv"], gp["swa_q_gain"], gp["swa_k_gain"] = _swa_pre_bwd(proj2, tab, sqg, skg, dsq, dsk, dsv)
    gp["swa_sinks"] = dsink
    gw["swa_w_qkv"] = _matmul(h2, dproj2, dims="tn", tm=1024, tn=384, tk=1024, outs=[F32], o_cs=N_CHIPS, name="swa_dwqkv")
    dx2, dx2b, dg_mix1 = _matmul(dproj2, W["swa_w_qkv"], dims="nt", tm=1024, tn=1024, tk=384, b_cs=True, name="swa_dh",
                                 vmem=VMEM_LIMIT_WIDE, **_rms_bwd_in(x2, P["norm_mix"][1], dx3))
    dx2b = red.begin("swa", {"swa_w_qkv": (gw["swa_w_qkv"], 1024), "swa_w_out": (gw["swa_w_out"], 256)}, dx2b)
    dx2b = red.advance("mlp1", dx2b, dx2b)

    dx1, dx1b, dg_mlp0, gw["mlp_w_up0"], gw["mlp_w_down0"] = _mlp_bwd(x1, P["norm_mlp"][0], W["packed"], 0, mlp0, dx2, dx2b, "0")
    gw["hyb_w_out"] = _matmul(mixed, dx1b, dims="tn", **_T, outs=[F32], name="hyb_dwout")
    dx1b = red.begin("mlp0", {"mlp_w_up0": (gw["mlp_w_up0"], 1024), "mlp_w_down0": (gw["mlp_w_down0"], 1024),
                              "hyb_w_out": (gw["hyb_w_out"], 256)}, dx1b)
    dx1b = red.advance("swa", dx1b, dx1b)
    red.finish("mlp1", dx1b)
    dmixed, ddelta = _matmul(dx1b, W["hyb_w_out"], dims="nt", **_T, name="hyb_dmixed", vmem=VMEM_LIMIT_WIDE,
                             **_delta_in(da, 512))
    dro, drg, gp["ret_gn_gain"] = _even_post_bwd(ro, proj, gn, dmixed)
    drq, drk, drv = _ret_bwd(rq, rk, rv, states, dro)
    ddq, ddk, ddv = _band_bwd(dq, dk, dv, dlse, ddelta, dmixed, patterns=dil, nq=1, name="dil_bwd", do_col0=4)
    ddq = red.advance("mlp0", ddq, ddq)
    red.finish("swa", ddq)
    dproj, gp["dil_q_gain"], gp["dil_k_gain"] = _even_pre_bwd(proj, tab, dqg, dkg, drq, drk, drv, drg, [ddq], [ddk], [ddv])
    gw["hyb_w_in"] = _matmul(h0, dproj, dims="tn", tm=1024, tn=768, tk=1024, outs=[F32], o_cs=N_CHIPS, name="hyb_dwin")
    dproj = red.begin("win", {"hyb_w_in": (gw["hyb_w_in"], 1024)}, dproj)
    grad_x, _, dg_mix0 = _matmul(dproj, W["hyb_w_in"], dims="nt", tm=1024, tn=1024, tk=768, b_cs=True, b_rows=1024, b_cols=768, name="hyb_dh",
                                 vmem=VMEM_LIMIT_WIDE, **_rms_bwd_in(x, P["norm_mix"][0], dx1))
    red.finish("mlp0", grad_x)
    gp["norm_mix"] = jnp.concatenate([dg_mix0, dg_mix1], axis=0)
    gp["norm_mlp"] = jnp.concatenate([dg_mlp0, dg_mlp1], axis=0)
    return loss, grad_x, gp


HBM = pl.BlockSpec(memory_space=pltpu.HBM)


def _place():
    x, y, c = lax.axis_index("x"), lax.axis_index("y"), lax.axis_index("c")
    chips = [(1 - x, y), (x, 1 - y), (1 - x, 1 - y)]
    return x, y, c, chips


SEM = pl.BlockSpec(memory_space=pltpu.SEMAPHORE)
EFFECT = pltpu.SideEffectType.DATAFLOW_SIDE_EFFECTING


def _half_block(ref, chip, core):
    rh = ref.shape[1] // 2
    return ref.at[2 * chip[0] + chip[1], pl.ds(core * rh, rh), :]


def _gather_start(buf, ride, name):
    def body(b_ref, ride_ref, s0, s1, s2, r0, r1, r2, b_out, ride_out):
        x, y, c, chips = _place()
        for chip, s, r in zip(chips, (s0, s1, s2), (r0, r1, r2)):
            mine = _half_block(b_ref, (x, y), c)
            pltpu.make_async_remote_copy(src_ref=mine, dst_ref=mine, send_sem=s, recv_sem=r,
                                         device_id=(*chip, c), device_id_type=MESH).start()

    sem = pltpu.SemaphoreType.DMA(())
    return _pc(
        body, name=name,
        out_shape=(sem,) * 6 + (pltpu.HBM(buf.shape, buf.dtype), pltpu.HBM(ride.shape, ride.dtype)),
        in_specs=(HBM, HBM), out_specs=(SEM,) * 6 + (HBM, HBM), input_output_aliases={0: 6, 1: 7},
        compiler_params=pltpu.CompilerParams(has_side_effects=EFFECT),
    )(pltpu.with_memory_space_constraint(buf, pltpu.HBM), pltpu.with_memory_space_constraint(ride, pltpu.HBM))


def _gather_wait(buf, sems, afters, name):
    def body(b_ref, s0, s1, s2, r0, r1, r2, *unread):
        x, y, c, chips = _place()
        for chip, s, r in zip(chips, (s0, s1, s2), (r0, r1, r2)):
            cp = pltpu.make_async_remote_copy(src_ref=_half_block(b_ref, (x, y), c), dst_ref=_half_block(b_ref, chip, c),
                                              send_sem=s, recv_sem=r, device_id=(*chip, c), device_id_type=MESH)
            cp.wait_send()
            cp.wait_recv()

    return _pc(
        body, name=name, out_shape=pltpu.HBM(buf.shape, buf.dtype),
        in_specs=(HBM,) + (SEM,) * 6 + (pl.BlockSpec(memory_space=pl.ANY),) * len(afters), out_specs=HBM,
        input_output_aliases={0: 0}, compiler_params=pltpu.CompilerParams(has_side_effects=EFFECT),
    )(buf, *sems, *afters)


def _gather_handover(buf, name):
    def body(b_ref, out_ref, send_sems, recv_sems):
        x, y, c, chips = _place()
        cps = []
        for k, chip in enumerate(chips):
            mine = _half_block(b_ref, chip, c)
            cps.append(pltpu.make_async_remote_copy(src_ref=mine, dst_ref=mine, send_sem=send_sems.at[k],
                                                    recv_sem=recv_sems.at[k], device_id=(x, y, 1 - c), device_id_type=MESH))
        for cp in cps:
            cp.start()
        for k, chip in enumerate(chips):
            theirs = _half_block(b_ref, chip, 1 - c)
            pltpu.make_async_remote_copy(src_ref=theirs, dst_ref=theirs, send_sem=send_sems.at[k], recv_sem=recv_sems.at[k],
                                         device_id=(x, y, 1 - c), device_id_type=MESH).wait_recv()
        for cp in cps:
            cp.wait_send()

    return _pc(
        body, name=name, in_specs=[HBM], out_specs=HBM,
        out_shape=jax.ShapeDtypeStruct(buf.shape, buf.dtype), input_output_aliases={0: 0},
        scratch_shapes=[pltpu.SemaphoreType.DMA((3,)), pltpu.SemaphoreType.DMA((3,))],
    )(buf)


def _pair_sum(t, l, place, name):
    _, r, cols = t.shape
    rh = r // 2
    tr = min(rh, 256)
    nr = rh // tr

    def body(pl_ref, t_ref, l_ref, o_ref):
        o_ref[...] = (t_ref[...] + l_ref[...]).astype(BF16)

    other = lambda s, p: s + jnp.where(s >= p[0], 1, 0)
    return _pc(
        body, name=name,
        grid_spec=pltpu.PrefetchScalarGridSpec(
            num_scalar_prefetch=1, grid=(N_CHIPS - 1, nr),
            in_specs=[pl.BlockSpec((None, tr, cols), lambda s, i, p: (other(s, p), p[1] * nr + i, 0)),
                      pl.BlockSpec((None, tr, cols), lambda s, i, p: (other(s, p), i, 0))],
            out_specs=pl.BlockSpec((None, tr, cols), lambda s, i, p: (other(s, p), i, 0))),
        out_shape=jax.ShapeDtypeStruct((N_CHIPS, rh, cols), BF16),
        compiler_params=_params(("parallel", "parallel")),
    )(place, t, l)


def _final_sum(t, l, rcv, place, name, layer=0, layers=1, into=None):
    _, r, cols = t.shape
    rh = r // 2
    tr = min(rh, 256)
    nr = rh // tr

    def body(pl_ref, t_ref, l_ref, r_ref, *rest):
        acc = t_ref[...] + l_ref[...]
        for k in range(3):
            acc = acc + r_ref[k].astype(F32)
        rest[-1][...] = acc

    in_specs = [pl.BlockSpec((None, tr, cols), lambda i, p: (p[0], p[1] * nr + i, 0)),
                pl.BlockSpec((None, tr, cols), lambda i, p: (p[0], i, 0)),
                pl.BlockSpec((3, tr, cols), lambda i, p: (0, i, 0))]
    args = [place, t, l, rcv]
    aliases = {}
    if into is not None:
        in_specs.append(pl.BlockSpec(memory_space=pl.ANY))
        args.append(into)
        aliases = {4: 0}
    return _pc(
        body, name=name,
        grid_spec=pltpu.PrefetchScalarGridSpec(
            num_scalar_prefetch=1, grid=(nr,), in_specs=in_specs,
            out_specs=pl.BlockSpec((tr, cols), lambda i, p: (2 * nr * layer + p[1] * nr + i, 0))),
        out_shape=jax.ShapeDtypeStruct((layers * r, cols), F32), input_output_aliases=aliases,
        compiler_params=_params(("parallel",)),
    )(*args)


def _share_halves(hs, name):
    nt = len(hs)
    n = sum(layers for _, layers in hs)

    def body(*refs):
        h_refs, send_sems, recv_sems = refs[:nt], refs[-2], refs[-1]
        x, y, c, _ = _place()
        cps = []
        for k, (_, layers) in enumerate(hs):
            rh = h_refs[k].shape[0] // (2 * layers)
            for layer in range(layers):
                half = h_refs[k].at[pl.ds((2 * layer + c) * rh, rh), :]
                cps.append(pltpu.make_async_remote_copy(
                    src_ref=half, dst_ref=half, send_sem=send_sems.at[len(cps)], recv_sem=recv_sems.at[len(cps)],
                    device_id=(x, y, 1 - c), device_id_type=MESH))
        for cp in cps:
            cp.start()
        for cp in cps:
            cp.wait()

    return _pc(
        body, name=name, in_specs=[HBM] * nt, out_specs=[HBM] * nt,
        out_shape=[jax.ShapeDtypeStruct(h.shape, F32) for h, _ in hs],
        input_output_aliases={k: k for k in range(nt)},
        scratch_shapes=[pltpu.SemaphoreType.DMA((n,)), pltpu.SemaphoreType.DMA((n,))],
    )(*[h for h, _ in hs])


def _split_start(name, bufs, ride, n, copies_of):
    nb = len(bufs)

    def body(*refs):
        sems = refs[nb + 1:nb + 1 + 2 * n]
        for cp in copies_of(refs[:nb], sems[:n], sems[n:]):
            (cp[0] if isinstance(cp, tuple) else cp).start()

    outs = _pc(
        body, name=name,
        out_shape=(pltpu.SemaphoreType.DMA(()),) * (2 * n) + tuple(pltpu.HBM(b.shape, b.dtype) for b in bufs)
        + (pltpu.HBM(ride.shape, ride.dtype),),
        in_specs=(HBM,) * (nb + 1), out_specs=(SEM,) * (2 * n) + (HBM,) * (nb + 1),
        input_output_aliases={k: 2 * n + k for k in range(nb + 1)},
        compiler_params=pltpu.CompilerParams(has_side_effects=EFFECT),
    )(*[pltpu.with_memory_space_constraint(b, pltpu.HBM) for b in bufs], pltpu.with_memory_space_constraint(ride, pltpu.HBM))
    return list(outs[:2 * n]), list(outs[2 * n:2 * n + nb]), outs[-1]


def _split_wait(name, bufs, sems, after, n, copies_of):
    nb = len(bufs)

    def body(*refs):
        s = refs[nb:nb + 2 * n]
        for cp in copies_of(refs[:nb], s[:n], s[n:]):
            sent, landed = cp if isinstance(cp, tuple) else (cp, cp)
            sent.wait_send()
            landed.wait_recv()

    outs = _pc(
        body, name=name, out_shape=tuple(pltpu.HBM(b.shape, b.dtype) for b in bufs),
        in_specs=(HBM,) * nb + (SEM,) * (2 * n) + (pl.BlockSpec(memory_space=pl.ANY),), out_specs=(HBM,) * nb,
        input_output_aliases={k: k for k in range(nb)},
        compiler_params=pltpu.CompilerParams(has_side_effects=EFFECT),
    )(*bufs, *sems, after)
    return list(outs)


def _handover_copies(refs, send, recv):
    x, y, c, chips = _place()
    cps = []
    for k, chip in enumerate(chips):
        mine, theirs = _half_block(refs[0], chip, c), _half_block(refs[0], chip, 1 - c)
        desc = lambda blk: pltpu.make_async_remote_copy(src_ref=blk, dst_ref=blk, send_sem=send[k], recv_sem=recv[k],
                                                        device_id=(x, y, 1 - c), device_id_type=MESH)
        cps.append((desc(mine), desc(theirs)))
    return cps


def _share_copies(layers_of):
    def copies_of(refs, send, recv):
        x, y, c, _ = _place()
        cps = []
        for k, layers in enumerate(layers_of):
            rh = refs[k].shape[0] // (2 * layers)
            for layer in range(layers):
                i = len(cps)
                desc = lambda half: pltpu.make_async_remote_copy(
                    src_ref=refs[k].at[pl.ds((2 * layer + half) * rh, rh), :], dst_ref=refs[k].at[pl.ds((2 * layer + half) * rh, rh), :],
                    send_sem=send[i], recv_sem=recv[i], device_id=(x, y, 1 - c), device_id_type=MESH)
                cps.append((desc(c), desc(1 - c)))
        return cps
    return copies_of


def _swap_copies(nt):
    def copies_of(refs, send, recv):
        x, y, c, _ = _place()
        cps = []
        for k in range(nt):
            rh = refs[k].shape[1] // 2
            cps.append(pltpu.make_async_remote_copy(
                src_ref=refs[k].at[:, pl.ds((1 - c) * rh, rh), :], dst_ref=refs[nt + k],
                send_sem=send[k], recv_sem=recv[k], device_id=(x, y, 1 - c), device_id_type=MESH))
        return cps
    return copies_of


def _exchange_copies(nt):
    def copies_of(refs, send, recv):
        x, y, c, chips = _place()
        cps = []
        for t in range(nt):
            for k, chip in enumerate(chips):
                cps.append(pltpu.make_async_remote_copy(
                    src_ref=refs[t].at[2 * chip[0] + chip[1]], dst_ref=refs[nt + t].at[k],
                    send_sem=send[3 * t + k], recv_sem=recv[3 * t + k], device_id=(*chip, c), device_id_type=MESH))
        return cps
    return copies_of


class _StagedReduce:
    def __init__(self, place):
        self.place = place
        self.groups = {}
        self.halves = {}

    @staticmethod
    def slab(t, r):
        return t.reshape(N_CHIPS, r, t.size // (N_CHIPS * r))

    def begin(self, g, grads, ride):
        names = list(grads)
        ts = [self.slab(t, r) for t, r in grads.values()]
        lands = [lax.empty((N_CHIPS, t.shape[1] // 2, t.shape[2]), F32) for t in ts]
        sems, bufs, ride = _split_start(f"grad_swap_start_{g}", ts + lands, ride, len(ts), _swap_copies(len(ts)))
        self.groups[g] = dict(names=names, bufs=bufs, sems=sems)
        return ride

    def advance(self, g, after, ride):
        st = self.groups[g]
        nt = len(st["names"])
        bufs = _split_wait(f"grad_swap_wait_{g}", st["bufs"], st["sems"], after, nt, _swap_copies(nt))
        st["ts"], st["ls"] = bufs[:nt], bufs[nt:]
        ps = [_pair_sum(t, l, self.place, f"pair_sum_{n}") for t, l, n in zip(st["ts"], st["ls"], st["names"])]
        lands = [lax.empty((3,) + p.shape[1:], BF16) for p in ps]
        st["sems"], st["bufs"], ride = _split_start(f"grad_exchange_start_{g}", ps + lands, ride, 3 * nt, _exchange_copies(nt))
        return ride

    def finish(self, g, after):
        st = self.groups[g]
        nt = len(st["names"])
        bufs = _split_wait(f"grad_exchange_wait_{g}", st["bufs"], st["sems"], after, 3 * nt, _exchange_copies(nt))
        for t, l, r, n in zip(st["ts"], st["ls"], bufs[nt:], st["names"]):
            if n[-1] in "01":
                self.halves[n[:-1]] = _final_sum(t, l, r, self.place, f"final_sum_{n}", layer=int(n[-1]), layers=2,
                                                 into=self.halves.get(n[:-1]))
            else:
                self.halves[n] = _final_sum(t, l, r, self.place, f"final_sum_{n}")


def _allgather_small(v):
    rows = v.shape[0]

    def body(v_ref, out_ref, send_sems, recv_sems):
        x, y, c, _ = _place()
        me = 4 * x + 2 * y + c
        out_ref[me] = v_ref[...]
        cps = []
        for k in range(1, 8):
            fx, fy, fc = (k >> 2) & 1, (k >> 1) & 1, k & 1
            to = (1 - x if fx else x, 1 - y if fy else y, 1 - c if fc else c)
            cps.append(pltpu.make_async_remote_copy(
                src_ref=v_ref, dst_ref=out_ref.at[me], send_sem=send_sems.at[k - 1], recv_sem=recv_sems.at[k - 1],
                device_id=to, device_id_type=MESH))
        for cp in cps:
            cp.start()
        for cp in cps:
            cp.wait()

    return _pc(
        body, name="allgather_small",
        in_specs=[pl.BlockSpec(memory_space=pltpu.VMEM)], out_specs=pl.BlockSpec(memory_space=pltpu.VMEM),
        out_shape=jax.ShapeDtypeStruct((8, rows, LANES), F32),
        scratch_shapes=[pltpu.SemaphoreType.DMA((7,)), pltpu.SemaphoreType.DMA((7,))],
    )(v)


def _adamw_math(w, g, m, v):
    m = ADAM_B1 * m + (1.0 - ADAM_B1) * g
    v = ADAM_B2 * v + (1.0 - ADAM_B2) * (g * g)
    m_hat = m / (1.0 - ADAM_B1 ** ADAM_STEP)
    v_hat = v / (1.0 - ADAM_B2 ** ADAM_STEP)
    return -ADAM_LR * (m_hat / (jnp.sqrt(v_hat) + ADAM_EPS) + ADAM_WD * w), m, v


def _adamw(w, g, m, v, name):
    r, cols = w.shape
    tr = min(r, 256)

    def body(w_ref, g_ref, m_ref, v_ref, go_ref, d_ref, mo_ref, vo_ref):
        gv = g_ref[...]
        d, mn, vn = _adamw_math(w_ref[...], gv, m_ref[...], v_ref[...])
        go_ref[...] = gv
        d_ref[...] = d
        mo_ref[...] = mn
        vo_ref[...] = vn

    row = pl.BlockSpec((tr, cols), lambda i: (i, 0))
    return _pc(
        body, name=name, grid=(r // tr,), in_specs=[row] * 4, out_specs=[row] * 4,
        out_shape=[jax.ShapeDtypeStruct((r, cols), F32)] * 4,
        compiler_params=_params(("parallel",)),
    )(w, g, m, v)


def _adamw_small(w, gathered, m, v):
    rows = w.shape[0]

    def body(w_ref, g_ref, m_ref, v_ref, go_ref, d_ref, mo_ref, vo_ref):
        g = g_ref[0]
        for k in range(1, 8):
            g = g + g_ref[k]
        d, mn, vn = _adamw_math(w_ref[...], g, m_ref[...], v_ref[...])
        go_ref[...] = g
        d_ref[...] = d
        mo_ref[...] = mn
        vo_ref[...] = vn

    return _pc(
        body, name="adamw_small",
        out_shape=[jax.ShapeDtypeStruct((rows, LANES), F32)] * 4,
    )(w, gathered, m, v)


_BIAS_ROWS = 32


def _own_slot(flat, chip):
    return lax.dynamic_update_slice(lax.empty((N_CHIPS,) + flat.shape, flat.dtype), flat[None], (chip, 0, 0))


def _pack_first(hyb_w_in, hyb_w_out):
    return jnp.concatenate([jnp.pad(hyb_w_in.astype(BF16), ((0, 0), (0, 256))), hyb_w_out.astype(BF16)], axis=0)


def _unpack_first(g):
    return {"hyb_w_in": g, "hyb_w_out": g[:, 1024:1280, :].reshape(1024, 1024)}


def _pack_rest(mlp_w_up, mlp_w_down, swa_w_qkv, swa_w_out, swa_b_qkv):
    parts = [t.astype(BF16).reshape(-1, 1024) for t in (mlp_w_up, mlp_w_down, swa_w_qkv, swa_w_out)]
    bias = lax.bitcast_convert_type(swa_b_qkv.reshape(384), BF16).reshape(1, 768)
    bias = jnp.pad(bias, ((0, _BIAS_ROWS - 1), (0, 256)))
    return jnp.concatenate(parts + [bias], axis=0)


def _unpack_rest(g):
    W = {
        "packed": g,
        "swa_w_qkv": g[:, 4096:4480, :].reshape(N_CHIPS, 1024, 384),
        "swa_w_out": g[:, 4480:4736, :].reshape(1024, 1024),
    }
    bias = lax.bitcast_convert_type(g[:, 4736, :768].reshape(N_CHIPS, 384, 2), F32).reshape(1536)
    return W, bias


_SMALL = (("norm_mix", 16), ("norm_mlp", 16), ("ret_gn_gain", 4), ("dil_q_gain", 1), ("dil_k_gain", 1),
          ("swa_b_qkv", 12), ("swa_q_gain", 1), ("swa_k_gain", 1), ("swa_sinks", 1), ("loss", 1))
_SUBLANES = 8


def _slot(r):
    return -(-r // _SUBLANES) * _SUBLANES


def _pack_small(d):
    return jnp.concatenate([jnp.pad(d[n].reshape(r, LANES), ((0, _slot(r) - r), (0, 0))) for n, r in _SMALL], axis=0)


def _unpack_small(p):
    out, o = {}, 0
    for n, r in _SMALL:
        out[n] = p[o:o + r]
        o += _slot(r)
    return out


def kernel(x, positions, norm_mix, norm_mlp, mlp_w_up, mlp_w_down, hyb_w_in, hyb_w_out, ret_gn_gain, dil_q_gain, dil_k_gain, swa_w_qkv, swa_b_qkv, swa_w_out, swa_q_gain, swa_k_gain, swa_sinks, loss_target, m_norm_mix, m_norm_mlp, m_mlp_w_up, m_mlp_w_down, m_hyb_w_in, m_hyb_w_out, m_ret_gn_gain, m_dil_q_gain, m_dil_k_gain, m_swa_w_qkv, m_swa_b_qkv, m_swa_w_out, m_swa_q_gain, m_swa_k_gain, m_swa_sinks, v_norm_mix, v_norm_mlp, v_mlp_w_up, v_mlp_w_down, v_hyb_w_in, v_hyb_w_out, v_ret_gn_gain, v_dil_q_gain, v_dil_k_gain, v_swa_w_qkv, v_swa_b_qkv, v_swa_w_out, v_swa_q_gain, v_swa_k_gain, v_swa_sinks):
    ax, ay, ac = lax.axis_index("x"), lax.axis_index("y"), lax.axis_index("c")
    chip = 2 * ax + ay
    place = jnp.stack([chip, ac]).astype(jnp.int32)
    S = x.shape[1]

    first = _own_slot(_pack_first(hyb_w_in[0], hyb_w_out[0]), chip)
    rest = _own_slot(_pack_rest(mlp_w_up, mlp_w_down, swa_w_qkv[0], swa_w_out[0], swa_b_qkv[0]), chip)
    *sems, first, pos_col = _gather_start(first, positions.reshape(S, 1), "allgather_first_start")
    flight = {}

    def first_of(afters):
        g = _gather_handover(_gather_wait(first, sems, [*afters, rest], "allgather_first_wait"), "allgather_first_handover")
        *flight["sems"], flight["buf"], g = _gather_start(rest, g, "allgather_rest_start")
        return _unpack_first(g)

    def rest_begin(ride):
        buf = _gather_wait(flight["buf"], flight["sems"], [ride], "allgather_rest_wait")
        flight["sems"], flight["bufs"], ride = _split_start("allgather_rest_handover_start", [buf], ride, 3, _handover_copies)
        return ride

    def rest_of(after):
        return _unpack_rest(_split_wait("allgather_rest_handover_wait", flight["bufs"], flight["sems"], after, 3,
                                        _handover_copies)[0])

    P = dict(norm_mix=norm_mix, norm_mlp=norm_mlp, ret_gn_gain=ret_gn_gain, dil_q_gain=dil_q_gain, dil_k_gain=dil_k_gain,
             swa_q_gain=swa_q_gain, swa_k_gain=swa_k_gain, swa_sinks=swa_sinks)

    red = _StagedReduce(place)
    loss_l, grad_x, gp = _local_step(x[0], pos_col, loss_target[0], first_of, rest_begin, rest_of, P, red)

    params = dict(mlp_w_up=(mlp_w_up, m_mlp_w_up, v_mlp_w_up), mlp_w_down=(mlp_w_down, m_mlp_w_down, v_mlp_w_down),
                  hyb_w_in=(hyb_w_in, m_hyb_w_in, v_hyb_w_in), hyb_w_out=(hyb_w_out, m_hyb_w_out, v_hyb_w_out),
                  swa_w_qkv=(swa_w_qkv, m_swa_w_qkv, v_swa_w_qkv), swa_w_out=(swa_w_out, m_swa_w_out, v_swa_w_out))
    big = {}

    def adamw_of(n, g):
        rows = g.shape[0]
        w, m, v = (t.reshape(rows, -1) for t in params[n])
        big[n] = [t.reshape(params[n][0].shape) for t in _adamw(w, g, m, v, f"adamw_{n}")]

    names = ["mlp_w_up", "mlp_w_down", "hyb_w_out", "swa_w_qkv", "swa_w_out"]
    red.halves[names[0]] = red.advance("win", grad_x, red.halves[names[0]])

    gsm = dict(gp, loss=loss_l)
    gsm["swa_sinks"] = jnp.pad(gp["swa_sinks"].reshape(16, HEAD)[:, 0], (0, LANES - 16))
    layers = [params[n][0].shape[0] for n in names]
    sems, shared, packed = _split_start("grad_share_start", [red.halves[n] for n in names], _pack_small(gsm), sum(layers),
                                        _share_copies(layers))
    gathered = _allgather_small(packed)

    def small_pack(norm_mix, norm_mlp, gn, dq, dk, b, sq, sk, sinks):
        dup = lambda t: jnp.tile(t.reshape(1, HEAD), (1, 2))
        bias = lax.dynamic_update_slice(jnp.zeros((12, LANES), F32), b.reshape(3, LANES), (3 * chip, 0))
        return _pack_small(dict(norm_mix=norm_mix, norm_mlp=norm_mlp, ret_gn_gain=gn, dil_q_gain=dup(dq), dil_k_gain=dup(dk),
                                swa_b_qkv=bias, swa_q_gain=dup(sq), swa_k_gain=dup(sk),
                                swa_sinks=jnp.pad(sinks.reshape(16), (0, LANES - 16)), loss=jnp.zeros((1, LANES), F32)))

    pw = small_pack(norm_mix, norm_mlp, ret_gn_gain, dil_q_gain, dil_k_gain, swa_b_qkv, swa_q_gain, swa_k_gain, swa_sinks)
    pm = small_pack(m_norm_mix, m_norm_mlp, m_ret_gn_gain, m_dil_q_gain, m_dil_k_gain, m_swa_b_qkv, m_swa_q_gain, m_swa_k_gain, m_swa_sinks)
    pv = small_pack(v_norm_mix, v_norm_mlp, v_ret_gn_gain, v_dil_q_gain, v_dil_k_gain, v_swa_b_qkv, v_swa_q_gain, v_swa_k_gain, v_swa_sinks)
    small_flat = _adamw_small(pw, gathered, pm, pv)
    small = [_unpack_small(t) for t in small_flat]

    for n, g in zip(names, _split_wait("grad_share_wait", shared, sems, small_flat[1], sum(layers), _share_copies(layers))):
        adamw_of(n, g)
    red.finish("win", big[names[-1]][1])
    adamw_of("hyb_w_in", _share_halves([(red.halves["hyb_w_in"], 1)], "grad_share_last")[0])

    def small_out(n, k):
        t = small[k][n]
        if n in ("norm_mix", "norm_mlp"):
            return t.reshape(2, D_MODEL)
        if n == "ret_gn_gain":
            return t.reshape(1, RET_HEADS, 128)
        if n == "swa_b_qkv":
            return lax.dynamic_slice(t, (3 * chip, 0), (3, LANES)).reshape(1, 384)
        if n == "swa_sinks":
            return t[0, :16].reshape(1, 16)
        return t[0, :HEAD].reshape(1, HEAD)

    order = ["norm_mix", "norm_mlp", "mlp_w_up", "mlp_w_down", "hyb_w_in", "hyb_w_out", "ret_gn_gain", "dil_q_gain",
             "dil_k_gain", "swa_w_qkv", "swa_b_qkv", "swa_w_out", "swa_q_gain", "swa_k_gain", "swa_sinks"]
    is_big = {"mlp_w_up", "mlp_w_down", "hyb_w_in", "hyb_w_out", "swa_w_qkv", "swa_w_out"}
    outs = [small[0]["loss"][0, 0], grad_x[None]]
    for k in range(4):
        outs += [big[n][k] if n in is_big else small_out(n, k) for n in order]
    return tuple(outs)
```

```python
import numpy as np
import jax
import jax.numpy as jnp
from jax import lax
from jax.experimental import pallas as pl
from jax.experimental.pallas import tpu as pltpu

F32, BF16 = jnp.float32, jnp.bfloat16
MESH = pl.DeviceIdType.MESH

LANES = 128
VMEM_LIMIT = 48 << 20
VMEM_LIMIT_WIDE = 60 << 20
D_MODEL = 1024
HEAD = 64
EPS = 1e-6
BLK = 128
RET_HEADS = 4
RET_THETA = 10000.0
ROPE_THETA = 500000.0
ROPE_DIMS = 16
DIL_PATTERNS = ((128, 1), (512, 4), (2048, 16))
SWA_DIST = 127
N_CHIPS = 4
ADAM_LR, ADAM_B1, ADAM_B2, ADAM_EPS, ADAM_WD, ADAM_STEP = 0.001, 0.9, 0.999, 1e-08, 0.01, 10

_LOG_GAMMA = [float(np.log1p(-np.exp2(np.float32(-5.0 - h)))) for h in range(RET_HEADS)]


def _pc(body, **kw):
    return pl.pallas_call(body, **kw)


def _params(sem, limit=VMEM_LIMIT):
    return pltpu.CompilerParams(dimension_semantics=sem, vmem_limit_bytes=limit)


def _matmul(a, b, *, dims, tm, tn, tk, outs, name, epilogue=None, extras=(), b_cs=False, b_rs=0, b_row0=0, b_rows=0,
            o_cs=0, a_pro=None, vmem=VMEM_LIMIT):
    if dims == "nn":
        M, K = a.shape
        N = b.shape[0] * b.shape[2] if b_cs else b.shape[1]
        a_spec = pl.BlockSpec((tm, tk), lambda i, j, k: (i, k))
        if b_cs:
            npt = b.shape[2] // tn
            b_spec = pl.BlockSpec((None, tk, tn), lambda i, j, k: (j // npt, k + b_row0, j % npt))
        elif b_rs:
            K, N, kps = b.shape[0] * b_rs, b.shape[2], b_rs // tk
            b_spec = pl.BlockSpec((None, tk, tn), lambda i, j, k: (k // kps, b_row0 + k % kps, j))
        else:
            b_spec = pl.BlockSpec((tk, tn), lambda i, j, k: (k, j))
        contract = (((1,), (0,)), ((), ()))
    elif dims == "nt":
        M, K = a.shape
        N = (b_rows or b.shape[1]) if b_cs else b.shape[0]
        a_spec = pl.BlockSpec((tm, tk), lambda i, j, k: (i, k))
        if b_cs:
            kpt = b.shape[2] // tk
            b_spec = pl.BlockSpec((None, tn, tk), lambda i, j, k: (k // kpt, j + b_row0, k % kpt))
        elif b_rs:
            N, jps = b.shape[0] * b_rs, b_rs // tn
            b_spec = pl.BlockSpec((None, tn, tk), lambda i, j, k: (j // jps, b_row0 + j % jps, k))
        else:
            b_spec = pl.BlockSpec((tn, tk), lambda i, j, k: (j, k))
        contract = (((1,), (1,)), ((), ()))
    else:
        K, M = a.shape
        N = b.shape[1]
        a_spec = pl.BlockSpec((tk, tm), lambda i, j, k: (k, i))
        b_spec = pl.BlockSpec((tk, tn), lambda i, j, k: (k, j))
        contract = (((0,), (0,)), ((), ()))
    assert M % tm == 0 and N % tn == 0 and K % tk == 0, (name, M, N, K, tm, tn, tk)
    nk = K // tk
    ex_specs = []
    for arr, kind in extras:
        if kind == "mn":
            ex_specs.append(pl.BlockSpec((tm, tn), lambda i, j, k: (i, j)))
        elif kind == "n":
            ex_specs.append(pl.BlockSpec((1, tn), lambda i, j, k: (0, j)))
        elif kind == "full":
            ex_specs.append(pl.BlockSpec(arr.shape, lambda i, j, k, nd=arr.ndim: (0,) * nd))
        else:
            ex_specs.append(pl.BlockSpec((tm, kind), lambda i, j, k: (i, 0)))
    if o_cs:
        n_sh = N // o_cs
        opt = n_sh // tn
        o_shape = (o_cs, M, n_sh)
        o_spec = pl.BlockSpec((None, tm, tn), lambda i, j, k: (j // opt, i, j % opt))
    else:
        o_shape = (M, N)
        o_spec = pl.BlockSpec((tm, tn), lambda i, j, k: (i, j))
    o_specs, o_shapes, summed = [], [], []
    for o in outs:
        if isinstance(o, tuple) and o[0] == "colsum":
            assert N == tn
            o_specs.append(pl.BlockSpec((1, tn), lambda i, j, k: (0, j)))
            o_shapes.append(jax.ShapeDtypeStruct((1, N), F32))
            summed.append(True)
        elif isinstance(o, tuple):
            o_specs.append(pl.BlockSpec((tm, o[1]), lambda i, j, k: (i, 0)))
            o_shapes.append(jax.ShapeDtypeStruct((M, o[1]), o[0]))
            summed.append(False)
        else:
            o_specs.append(o_spec)
            o_shapes.append(jax.ShapeDtypeStruct(o_shape, o))
            summed.append(False)
    n_ex, n_out = len(extras), len(outs)
    if epilogue is None:
        epilogue = lambda acc: (acc,)

    def body(a_ref, b_ref, *rest):
        ex, o_refs, acc = rest[:n_ex], rest[n_ex:n_ex + n_out], rest[-1]
        i, k = pl.program_id(0), pl.program_id(2)

        @pl.when(k == 0)
        def _():
            acc[...] = jnp.zeros_like(acc)

        av = a_ref[...] if a_pro is None else a_pro(a_ref[...])
        acc[...] += lax.dot_general(av.astype(BF16), b_ref[...].astype(BF16), contract, preferred_element_type=F32)

        @pl.when(k == nk - 1)
        def _():
            vals = epilogue(acc[...], *[e[...] for e in ex])
            for r, v, sm in zip(o_refs, vals, summed):
                if sm:
                    @pl.when(i == 0)
                    def _(r=r):
                        r[...] = jnp.zeros_like(r)

                    r[...] += v
                else:
                    r[...] = v.astype(r.dtype)

    res = _pc(
        body, name=name, grid=(M // tm, N // tn, nk),
        in_specs=[a_spec, b_spec] + ex_specs, out_specs=o_specs, out_shape=o_shapes,
        scratch_shapes=[pltpu.VMEM((tm, tn), F32)],
        compiler_params=_params(("arbitrary" if any(summed) else "parallel", "parallel", "arbitrary"), vmem),
    )(a, b, *[e for e, _ in extras])
    return res[0] if n_out == 1 else res


def _roll(x, s):
    return pltpu.roll(x, s % LANES, 1)


def _rope(x, A, B, C, half):
    return x * A + _roll(x, LANES - half) * B + _roll(x, half) * C


def _rope_t(g, A, B, C, half):
    return g * A + _roll(g * B, half) + _roll(g * C, LANES - half)


def _gmean(x, G):
    hi = x.astype(BF16)
    lo = (x - hi.astype(F32)).astype(BF16)
    Gb = G.astype(BF16)
    return jnp.dot(hi, Gb, preferred_element_type=F32) + jnp.dot(lo, Gb, preferred_element_type=F32)


def _head_mask(shape, half):
    lane = lax.broadcasted_iota(jnp.int32, shape, len(shape) - 1)
    return (lane >= HEAD) if half else (lane < HEAD)


def _group_matrix():
    i = np.arange(LANES)
    return jnp.asarray((i[:, None] // HEAD == i[None, :] // HEAD).astype(np.float32) / HEAD)


def _rope_inv():
    l = np.arange(LANES) % HEAD
    inv_r = np.power(np.float32(RET_THETA), -(l % 32).astype(np.float32) * np.float32(2.0 / HEAD))
    hp = ROPE_DIMS // 2
    inv_p = np.power(np.float32(ROPE_THETA), -(l % hp).astype(np.float32) * np.float32(2.0 / ROPE_DIMS))
    inv_p = np.where(l < ROPE_DIMS, inv_p, 0.0)
    return jnp.asarray(np.stack([inv_r, inv_p]).astype(np.float32))


def _tables(pos_col):
    S = pos_col.shape[0]
    tm = 512
    hp = ROPE_DIMS // 2

    def body(p_ref, inv_ref, o_ref):
        p = p_ref[...].astype(F32)
        lane = lax.broadcasted_iota(jnp.int32, (tm, LANES), 1) % HEAD
        ang = p * inv_ref[0:1, :]
        c, s = jnp.cos(ang), jnp.sin(ang)
        o_ref[:, 0:128] = c
        o_ref[:, 128:256] = jnp.where(lane < 32, -s, 0.0)
        o_ref[:, 256:384] = jnp.where(lane >= 32, s, 0.0)
        ang = p * inv_ref[1:2, :]
        c, s = jnp.cos(ang), jnp.sin(ang)
        o_ref[:, 384:512] = c
        o_ref[:, 512:640] = jnp.where(lane < hp, -s, 0.0)
        o_ref[:, 640:768] = jnp.where((lane >= hp) & (lane < ROPE_DIMS), s, 0.0)

    return _pc(
        body, name="rope_tables", grid=(S // tm,),
        in_specs=[pl.BlockSpec((tm, 1), lambda i: (i, 0)), pl.BlockSpec((2, LANES), lambda i: (0, 0))],
        out_specs=pl.BlockSpec((tm, 768), lambda i: (i, 0)),
        out_shape=jax.ShapeDtypeStruct((S, 768), F32),
        compiler_params=_params(("parallel",)),
    )(pos_col, _rope_inv())


def _tab(tab_ref, which):
    o = 384 * which
    return tab_ref[:, o:o + 128], tab_ref[:, o + 128:o + 256], tab_ref[:, o + 256:o + 384]


def _rms_fwd(x, g, name):
    S, Dm = x.shape
    tm = 512

    def body(x_ref, g_ref, h_ref):
        xv = x_ref[...]
        r = lax.rsqrt(jnp.mean(xv * xv, axis=-1, keepdims=True) + EPS)
        h_ref[...] = (xv * r * g_ref[...]).astype(BF16)

    return _pc(
        body, name=name, grid=(S // tm,),
        in_specs=[pl.BlockSpec((tm, Dm), lambda i: (i, 0)), pl.BlockSpec((1, Dm), lambda i: (0, 0))],
        out_specs=pl.BlockSpec((tm, Dm), lambda i: (i, 0)),
        out_shape=jax.ShapeDtypeStruct((S, Dm), BF16),
        compiler_params=_params(("parallel",)),
    )(x, g.reshape(1, Dm))


def _hn_fwd(x, gain, G):
    r = lax.rsqrt(_gmean(x * x, G) + EPS)
    return x * r * gain


def _hn_bwd(x, gain, dy, G):
    r = lax.rsqrt(_gmean(x * x, G) + EPS)
    t = dy * gain
    dx = r * t - x * (r * r * r) * _gmean(x * t, G)
    return dx, jnp.sum(dy * x * r, axis=0, keepdims=True)


def _fold_halves(v):
    return v + _roll(v, HEAD)


def _even_pre_fwd(proj, tab, qg, kg):
    S = proj.shape[0]
    tm = 512

    def body(p_ref, tab_ref, qg_ref, kg_ref, g_ref, rq_ref, rk_ref, rv_ref, dq_ref, dk_ref, dv_ref):
        Ar, Br, Cr = _tab(tab_ref, 0)
        Ap, Bp, Cp = _tab(tab_ref, 1)
        G = g_ref[...]
        for c in range(2):
            sl = slice(c * 128, (c + 1) * 128)
            rq_ref[:, sl] = _rope(p_ref[:, c * 128:(c + 1) * 128], Ar, Br, Cr, 32).astype(BF16)
            rk_ref[:, sl] = (_rope(p_ref[:, 256 + c * 128:256 + (c + 1) * 128], Ar, Br, Cr, 32) * 0.125).astype(BF16)
        rv_ref[...] = p_ref[:, 512:1024].astype(BF16)
        for c in range(4):
            sl = slice(c * 128, (c + 1) * 128)
            q = _hn_fwd(p_ref[:, 1536 + c * 128:1536 + (c + 1) * 128], qg_ref[...], G)
            dq_ref[:, sl] = _rope(q, Ap, Bp, Cp, 8).astype(BF16)
            k = _hn_fwd(p_ref[:, 2048 + c * 128:2048 + (c + 1) * 128], kg_ref[...], G)
            dk_ref[:, sl] = _rope(k, Ap, Bp, Cp, 8).astype(BF16)
        dv_ref[...] = p_ref[:, 2560:3072].astype(BF16)

    row = lambda w: pl.BlockSpec((tm, w), lambda i: (i, 0))
    vec = pl.BlockSpec((1, LANES), lambda i: (0, 0))
    return _pc(
        body, name="even_pre_fwd", grid=(S // tm,),
        in_specs=[row(3072), row(768), vec, vec, pl.BlockSpec((LANES, LANES), lambda i: (0, 0))],
        out_specs=[row(256), row(256), row(512), row(512), row(512), row(512)],
        out_shape=[jax.ShapeDtypeStruct((S, w), BF16) for w in (256, 256, 512, 512, 512, 512)],
        compiler_params=_params(("parallel",), VMEM_LIMIT_WIDE),
    )(proj, tab, qg, kg, _group_matrix())


def _even_pre_bwd(proj, tab, qg, kg, drq, drk, drv, drg, dqs, dks, dvs):
    S = proj.shape[0]
    tm = 512
    npat = len(dqs)

    def body(p_ref, tab_ref, qg_ref, kg_ref, g_ref, drq_ref, drk_ref, drv_ref, drg_ref, *rest):
        dq_refs, dk_refs, dv_refs = rest[:npat], rest[npat:2 * npat], rest[2 * npat:3 * npat]
        dp_ref, dqg_ref, dkg_ref = rest[3 * npat:]
        Ar, Br, Cr = _tab(tab_ref, 0)
        Ap, Bp, Cp = _tab(tab_ref, 1)
        G = g_ref[...]
        for c in range(2):
            sl = slice(c * 128, (c + 1) * 128)
            dp_ref[:, c * 128:(c + 1) * 128] = _rope_t(drq_ref[:, sl], Ar, Br, Cr, 32).astype(BF16)
            dp_ref[:, 256 + c * 128:256 + (c + 1) * 128] = _rope_t(drk_ref[:, sl] * 0.125, Ar, Br, Cr, 32).astype(BF16)
        dp_ref[:, 512:1024] = drv_ref[...].astype(BF16)
        dp_ref[:, 1024:1536] = drg_ref[...].astype(BF16)
        accq = jnp.zeros((1, LANES), F32)
        acck = jnp.zeros((1, LANES), F32)
        for c in range(4):
            sl = slice(c * 128, (c + 1) * 128)
            g = dq_refs[0][:, sl]
            for r in dq_refs[1:]:
                g = g + r[:, sl]
            dx, dg = _hn_bwd(p_ref[:, 1536 + c * 128:1536 + (c + 1) * 128], qg_ref[...], _rope_t(g, Ap, Bp, Cp, 8), G)
            dp_ref[:, 1536 + c * 128:1536 + (c + 1) * 128] = dx.astype(BF16)
            accq = accq + dg
            g = dk_refs[0][:, sl]
            for r in dk_refs[1:]:
                g = g + r[:, sl]
            dx, dg = _hn_bwd(p_ref[:, 2048 + c * 128:2048 + (c + 1) * 128], kg_ref[...], _rope_t(g, Ap, Bp, Cp, 8), G)
            dp_ref[:, 2048 + c * 128:2048 + (c + 1) * 128] = dx.astype(BF16)
            acck = acck + dg
        g = dv_refs[0][...]
        for r in dv_refs[1:]:
            g = g + r[...]
        dp_ref[:, 2560:3072] = g.astype(BF16)

        @pl.when(pl.program_id(0) == 0)
        def _():
            dqg_ref[...] = jnp.zeros_like(dqg_ref)
            dkg_ref[...] = jnp.zeros_like(dkg_ref)

        dqg_ref[...] += _fold_halves(accq)
        dkg_ref[...] += _fold_halves(acck)

    row = lambda w: pl.BlockSpec((tm, w), lambda i: (i, 0))
    vec = pl.BlockSpec((1, LANES), lambda i: (0, 0))
    return _pc(
        body, name="even_pre_bwd", grid=(S // tm,),
        in_specs=[row(3072), row(768), vec, vec, pl.BlockSpec((LANES, LANES), lambda i: (0, 0)),
                  row(256), row(256), row(512), row(512)] + [row(512)] * (3 * npat),
        out_specs=[row(3072), vec, vec],
        out_shape=[jax.ShapeDtypeStruct((S, 3072), BF16), jax.ShapeDtypeStruct((1, LANES), F32),
                   jax.ShapeDtypeStruct((1, LANES), F32)],
        compiler_params=_params(("arbitrary",), VMEM_LIMIT_WIDE),
    )(proj, tab, qg, kg, _group_matrix(), drq, drk, drv, drg, *dqs, *dks, *dvs)


def _ret_consts(pair, half):
    lg = jnp.where(pair == 0, _LOG_GAMMA[half], _LOG_GAMMA[2 + half]).astype(F32)
    i = lax.broadcasted_iota(jnp.int32, (BLK, BLK), 0)
    j = lax.broadcasted_iota(jnp.int32, (BLK, BLK), 1)
    diff = (i - j).astype(F32)
    decay = jnp.where(diff >= 0, jnp.exp(lg * jnp.maximum(diff, 0.0)), 0.0)
    t = lax.broadcasted_iota(jnp.int32, (BLK, 1), 0).astype(F32)
    xi = jnp.exp(lg * (t + 1.0))
    zeta = jnp.exp(lg * (BLK - 1.0 - t))
    cd = jnp.exp(jnp.full((1, 1), BLK, F32) * lg)
    return decay, xi, zeta, cd


RET_STEP = 16


def _ret_fwd(rq, rk, rv):
    S = rq.shape[0]
    nc = S // BLK
    rows = RET_STEP * BLK

    def body(q_ref, k_ref, v_ref, o_ref, st_ref, R):
        p, n = pl.program_id(0), pl.program_id(1)

        @pl.when(n == 0)
        def _():
            R[...] = jnp.zeros_like(R)

        consts = [_ret_consts(p, half) for half in range(2)]
        masks = [_head_mask((BLK, LANES), half) for half in range(2)]
        for ci in range(RET_STEP):
            rs = slice(ci * BLK, (ci + 1) * BLK)
            q2, k2 = q_ref[rs, :], k_ref[rs, :]
            for half in range(2):
                decay, xi, zeta, cd = consts[half]
                m = masks[half]
                qm = jnp.where(m, q2, jnp.zeros_like(q2))
                km = jnp.where(m, k2, jnp.zeros_like(k2))
                v = v_ref[rs, half * 128:(half + 1) * 128]
                Rb = R[half].astype(BF16)
                st_ref[ci, half] = Rb
                sc = lax.dot_general(qm, k2, (((1,), (1,)), ((), ())), preferred_element_type=F32) * decay
                o = jnp.dot(sc.astype(BF16), v, preferred_element_type=F32)
                o = o + jnp.dot(qm, Rb, preferred_element_type=F32) * xi
                o_ref[rs, half * 128:(half + 1) * 128] = o
                kz = (km.astype(F32) * zeta).astype(BF16)
                R[half] = R[half] * cd + lax.dot_general(kz, v, (((0,), (0,)), ((), ())), preferred_element_type=F32)

    return _pc(
        body, name="ret_fwd", grid=(2, nc // RET_STEP),
        in_specs=[pl.BlockSpec((rows, 128), lambda p, n: (n, p)), pl.BlockSpec((rows, 128), lambda p, n: (n, p)),
                  pl.BlockSpec((rows, 256), lambda p, n: (n, p))],
        out_specs=[pl.BlockSpec((rows, 256), lambda p, n: (n, p)),
                   pl.BlockSpec((None, RET_STEP, 2, 128, 128), lambda p, n: (p, n, 0, 0, 0))],
        out_shape=[jax.ShapeDtypeStruct((S, 512), F32), jax.ShapeDtypeStruct((2, nc, 2, 128, 128), BF16)],
        scratch_shapes=[pltpu.VMEM((2, 128, 128), F32)],
        compiler_params=_params(("parallel", "arbitrary")),
    )(rq, rk, rv)


def _ret_bwd(rq, rk, rv, states, do):
    S = rq.shape[0]
    nc = S // BLK
    ns = nc // RET_STEP
    rows = RET_STEP * BLK
    nt = (((1,), (1,)), ((), ()))
    tn = (((0,), (0,)), ((), ()))

    def body(q_ref, k_ref, v_ref, st_ref, do_ref, dq_ref, dk_ref, dv_ref, U):
        p, n = pl.program_id(0), pl.program_id(1)

        @pl.when(n == 0)
        def _():
            U[...] = jnp.zeros_like(U)

        consts = [_ret_consts(p, half) for half in range(2)]
        masks = [_head_mask((BLK, LANES), half) for half in range(2)]
        for ci in reversed(range(RET_STEP)):
            rs = slice(ci * BLK, (ci + 1) * BLK)
            q2, k2 = q_ref[rs, :], k_ref[rs, :]
            dq_acc = jnp.zeros((BLK, LANES), F32)
            dk_acc = jnp.zeros((BLK, LANES), F32)
            for half in range(2):
                decay, xi, zeta, cd = consts[half]
                m = masks[half]
                qm = jnp.where(m, q2, jnp.zeros_like(q2))
                km = jnp.where(m, k2, jnp.zeros_like(k2))
                v = v_ref[rs, half * 128:(half + 1) * 128]
                dob = do_ref[rs, half * 128:(half + 1) * 128].astype(BF16)
                Rb = st_ref[ci, half]
                Ub = U[half].astype(BF16)
                dsc = (lax.dot_general(dob, v, nt, preferred_element_type=F32) * decay).astype(BF16)
                xdo = (dob.astype(F32) * xi).astype(BF16)
                dq_acc += jnp.dot(dsc, km, preferred_element_type=F32) + lax.dot_general(xdo, Rb, nt, preferred_element_type=F32)
                dk_acc += lax.dot_general(dsc, qm, tn, preferred_element_type=F32) \
                    + lax.dot_general(v, Ub, nt, preferred_element_type=F32) * zeta
                sc = (lax.dot_general(qm, k2, nt, preferred_element_type=F32) * decay).astype(BF16)
                kz = (km.astype(F32) * zeta).astype(BF16)
                dv_ref[rs, half * 128:(half + 1) * 128] = lax.dot_general(sc, dob, tn, preferred_element_type=F32) \
                    + jnp.dot(kz, Ub, preferred_element_type=F32)
                U[half] = U[half] * cd + lax.dot_general(qm, xdo, tn, preferred_element_type=F32)
            dq_ref[rs, :] = dq_acc
            dk_ref[rs, :] = dk_acc

    rev = lambda w: pl.BlockSpec((rows, w), lambda p, n: (ns - 1 - n, p))
    return _pc(
        body, name="ret_bwd", grid=(2, ns),
        in_specs=[rev(128), rev(128), rev(256),
                  pl.BlockSpec((None, RET_STEP, 2, 128, 128), lambda p, n: (p, ns - 1 - n, 0, 0, 0)), rev(256)],
        out_specs=[rev(128), rev(128), rev(256)],
        out_shape=[jax.ShapeDtypeStruct((S, 256), F32), jax.ShapeDtypeStruct((S, 256), F32),
                   jax.ShapeDtypeStruct((S, 512), F32)],
        scratch_shapes=[pltpu.VMEM((2, 128, 128), F32)],
        compiler_params=_params(("parallel", "arbitrary")),
    )(rq, rk, rv, states, do)


ATT_TILE = 2048


def _rows(ref, start, n, r):
    if r == 1:
        return ref[pl.ds(start, n), :]
    return ref[pl.ds(start, n, stride=r), :]


def _twice(x):
    return jnp.concatenate([x, x], axis=0)


def _stack_heads(x, masks):
    zero = jnp.zeros_like(x)
    return jnp.concatenate([jnp.where(masks[0], x, zero), jnp.where(masks[1], x, zero)], axis=0)


def _set_rows(ref, start, n, r, val):
    if r == 1:
        ref[pl.ds(start, n), :] = val
    else:
        ref[pl.ds(start, n, stride=r), :] = val


def _band_geometry(S, patterns):
    rmax = max(r for _, r in patterns)
    H = BLK * rmax
    T = min(S, ATT_TILE)
    assert T % H == 0 and S % T == 0
    return H, T, S // T, T // BLK


def _band_fwd(q, k, v, *, patterns, nq, name, sinks=None, want_bf16=False):
    S, Ck = k.shape
    H, T, nt, nbt = _band_geometry(S, patterns)
    ncol = Ck // LANES
    scale = HEAD ** -0.5
    has_sink = sinks is not None
    nt_dims = (((1,), (1,)), ((), ()))

    def body(*refs):
        q_ref, kp_ref, kc_ref, vp_ref, vc_ref = refs[:5]
        sk_ref = refs[5] if has_sink else None
        n_out = 3 if want_bf16 else 2
        outs = refs[5 + has_sink:5 + has_sink + n_out]
        qf, kf, vf, M, L, A = refs[5 + has_sink + n_out:]
        t = pl.program_id(1)
        kf[0:H, :] = kp_ref[...].astype(F32)
        kf[H:H + T, :] = kc_ref[...].astype(F32)
        vf[0:H, :] = vp_ref[...].astype(F32)
        vf[H:H + T, :] = vc_ref[...].astype(F32)
        r_i = lax.broadcasted_iota(jnp.int32, (BLK, 2 * BLK), 0)
        c_i = lax.broadcasted_iota(jnp.int32, (BLK, 2 * BLK), 1)
        dist_i = r_i + BLK - c_i
        masks = [_head_mask((BLK, LANES), h) for h in range(2)]

        for i in range(nq):
            qf[...] = q_ref[:, i * 128:(i + 1) * 128].astype(F32) * scale
            for p, (dist, r) in enumerate(patterns):
                in_band = (dist_i >= 0) & (dist_i <= dist)
                in_band_first = in_band & ((c_i >= BLK) | (t > 0))
                in_band, in_band_first = _twice(in_band), _twice(in_band_first)

                def unit(j, b, p=p, r=r, in_band=in_band, in_band_first=in_band_first):
                    q0 = j + b * (BLK * r)
                    q2 = _rows(qf, q0, BLK, r).astype(BF16)
                    kcat = _rows(kf, H + q0 - BLK * r, 2 * BLK, r).astype(BF16)
                    vcat = _rows(vf, H + q0 - BLK * r, 2 * BLK, r).astype(BF16)
                    valid = in_band if b > 0 else in_band_first
                    s = lax.dot_general(_stack_heads(q2, masks), kcat, nt_dims, preferred_element_type=F32)
                    s = jnp.where(valid, s, -jnp.inf)
                    mx = jnp.max(s, axis=1, keepdims=True)
                    pr = jnp.exp(s - mx)
                    den = jnp.sum(pr, axis=1, keepdims=True)
                    pv = jnp.dot(pr.astype(BF16), vcat, preferred_element_type=F32)
                    m2 = jnp.where(masks[0], mx[:BLK], mx[BLK:])
                    l2 = jnp.where(masks[0], den[:BLK], den[BLK:])
                    a2 = jnp.where(masks[0], pv[:BLK], pv[BLK:])
                    if p > 0:
                        mo = _rows(M, q0, BLK, r)
                        mn = jnp.maximum(mo, m2)
                        wa, wb = jnp.exp(mo - mn), jnp.exp(m2 - mn)
                        l2 = wa * _rows(L, q0, BLK, r) + wb * l2
                        a2 = wa * _rows(A, q0, BLK, r) + wb * a2
                        m2 = mn
                    _set_rows(M, q0, BLK, r, m2)
                    _set_rows(L, q0, BLK, r, l2)
                    _set_rows(A, q0, BLK, r, a2)

                for u in range(nbt):
                    unit(u % r, u // r)
            sl = slice(i * 128, (i + 1) * 128)
            mm, ll, aa = M[...], L[...], A[...]
            if has_sink:
                snk = sk_ref[:, sl]
                mn = jnp.maximum(mm, snk)
                w = jnp.exp(mm - mn)
                ll = ll * w + jnp.exp(snk - mn)
                aa = aa * w
                mm = mn
            o = aa / ll
            outs[0][:, sl] = o
            outs[1][:, sl] = mm + jnp.log(ll)
            if want_bf16:
                outs[2][:, sl] = o.astype(BF16)

    th = T // H
    qspec = pl.BlockSpec((T, nq * 128), lambda j, t: (t, j))
    cur = pl.BlockSpec((T, 128), lambda j, t: (t, j))
    prev = pl.BlockSpec((H, 128), lambda j, t: (jnp.maximum(t * th - 1, 0), j))
    in_specs = [qspec, prev, cur, prev, cur]
    args = [q, k, k, v, v]
    if has_sink:
        in_specs.append(pl.BlockSpec((1, nq * 128), lambda j, t: (0, j)))
        args.append(sinks)
    out_dts = [F32, F32] + ([BF16] if want_bf16 else [])
    return _pc(
        body, name=name, grid=(ncol, nt), in_specs=in_specs,
        out_specs=[qspec] * len(out_dts),
        out_shape=[jax.ShapeDtypeStruct(q.shape, dt) for dt in out_dts],
        scratch_shapes=[pltpu.VMEM((T, LANES), F32), pltpu.VMEM((H + T, LANES), F32), pltpu.VMEM((H + T, LANES), F32),
                        pltpu.VMEM((T, LANES), F32), pltpu.VMEM((T, LANES), F32), pltpu.VMEM((T, LANES), F32)],
        compiler_params=_params(("parallel", "parallel")),
    )(*args)


def _band_bwd(q, k, v, lse, delta, do, *, patterns, nq, name, sinks=None, do_col0=0):
    S, Ck = k.shape
    H, T, nt, nbt = _band_geometry(S, patterns)
    ncol = Ck // LANES
    scale = HEAD ** -0.5
    has_sink = sinks is not None
    nt_dims = (((1,), (1,)), ((), ()))
    tn_dims = (((0,), (0,)), ((), ()))

    def body(*refs):
        (qc_ref, qn_ref, kp_ref, kc_ref, vp_ref, vc_ref, lc_ref, ln_ref, ec_ref, en_ref, dc_ref, dn_ref) = refs[:12]
        sk_ref = refs[12] if has_sink else None
        n_out = 4 if has_sink else 3
        outs = refs[12 + has_sink:12 + has_sink + n_out]
        dq_ref, dk_ref, dv_ref = outs[:3]
        qf, kf, vf, lf, ef, df = refs[12 + has_sink + n_out:]
        t = pl.program_id(1)
        kf[0:H, :] = kp_ref[...].astype(F32)
        kf[H:H + T, :] = kc_ref[...].astype(F32)
        vf[0:H, :] = vp_ref[...].astype(F32)
        vf[H:H + T, :] = vc_ref[...].astype(F32)
        dk_ref[...] = jnp.zeros_like(dk_ref)
        dv_ref[...] = jnp.zeros_like(dv_ref)
        r_i = lax.broadcasted_iota(jnp.int32, (BLK, 2 * BLK), 0)
        c_i = lax.broadcasted_iota(jnp.int32, (BLK, 2 * BLK), 1)
        dist_q = r_i + BLK - c_i
        dist_h = dist_q[:, :BLK]
        m1 = [_head_mask((BLK, LANES), h) for h in range(2)]

        def stacked_inputs(q2, do2, l2, e2):
            spread = lambda v: jnp.concatenate([jnp.where(m1[0], v, _roll(v, HEAD)), jnp.where(m1[1], v, _roll(v, HEAD))], axis=0)
            return _stack_heads(q2, m1), _stack_heads(do2.astype(BF16), m1), spread(l2), spread(e2)

        for i in range(nq):
            sl = slice(i * 128, (i + 1) * 128)
            qf[0:T, :] = qc_ref[:, sl].astype(F32) * scale
            qf[T:T + H, :] = qn_ref[:, sl].astype(F32) * scale
            for buf, c_ref, n_ref in ((lf, lc_ref, ln_ref), (ef, ec_ref, en_ref), (df, dc_ref, dn_ref)):
                buf[0:T, :] = c_ref[:, sl]
                buf[T:T + H, :] = n_ref[:, sl]
            if has_sink:
                @pl.when(t == 0)
                def _():
                    outs[3][:, sl] = jnp.zeros((1, LANES), F32)

                outs[3][:, sl] += jnp.sum(-jnp.exp(sk_ref[:, sl] - lc_ref[:, sl]) * ec_ref[:, sl], axis=0, keepdims=True)
            for p, (dist, r) in enumerate(patterns):
                band_q = (dist_q >= 0) & (dist_q <= dist)
                band_first = band_q & ((c_i >= BLK) | (t > 0))
                band_h = (dist_h >= 0) & (dist_h <= dist)
                band_q, band_first, band_h = _twice(band_q), _twice(band_first), _twice(band_h)

                def add_rows(ref, start, val, r=r):
                    _set_rows(ref, start, BLK, r, _rows(ref, start, BLK, r) + val)

                def unit(j, b, p=p, r=r, band_q=band_q, band_first=band_first):
                    q0 = j + b * (BLK * r)
                    q2 = _rows(qf, q0, BLK, r).astype(BF16)
                    do2, l2, e2 = _rows(df, q0, BLK, r), _rows(lf, q0, BLK, r), _rows(ef, q0, BLK, r)
                    kcat = _rows(kf, H + q0 - BLK * r, 2 * BLK, r).astype(BF16)
                    vcat = _rows(vf, H + q0 - BLK * r, 2 * BLK, r).astype(BF16)
                    valid = band_q if b > 0 else band_first
                    qs, dos, ls, es = stacked_inputs(q2, do2, l2, e2)
                    s = lax.dot_general(qs, kcat, nt_dims, preferred_element_type=F32)
                    pr = jnp.where(valid, jnp.exp(s - jnp.concatenate([ls, ls], axis=1)), 0.0)
                    dp = lax.dot_general(dos, vcat, nt_dims, preferred_element_type=F32)
                    ds = (pr * (dp - jnp.concatenate([es, es], axis=1))).astype(BF16)
                    dqs = jnp.dot(ds, kcat, preferred_element_type=F32) * scale
                    dq2 = jnp.where(m1[0], dqs[:BLK], dqs[BLK:])
                    dvc = lax.dot_general(pr.astype(BF16), dos, tn_dims, preferred_element_type=F32)
                    dkc = lax.dot_general(ds, qs, tn_dims, preferred_element_type=F32)
                    if p > 0:
                        dq2 = dq2 + _rows(dq_ref.at[:, sl], q0, BLK, r)
                    _set_rows(dq_ref.at[:, sl], q0, BLK, r, dq2)
                    add_rows(dk_ref, q0, dkc[BLK:])
                    add_rows(dv_ref, q0, dvc[BLK:])
                    if b > 0:
                        add_rows(dk_ref, q0 - BLK * r, dkc[:BLK])
                        add_rows(dv_ref, q0 - BLK * r, dvc[:BLK])

                def halo_unit(j, r=r, band_h=band_h):
                    k0 = j + (nbt // r - 1) * (BLK * r)
                    q2 = _rows(qf, T + j, BLK, r).astype(BF16)
                    do2, l2, e2 = _rows(df, T + j, BLK, r), _rows(lf, T + j, BLK, r), _rows(ef, T + j, BLK, r)
                    kc = _rows(kf, H + k0, BLK, r).astype(BF16)
                    vc = _rows(vf, H + k0, BLK, r).astype(BF16)
                    qs, dos, ls, es = stacked_inputs(q2, do2, l2, e2)
                    s = lax.dot_general(qs, kc, nt_dims, preferred_element_type=F32)
                    pr = jnp.where(band_h, jnp.exp(s - ls), 0.0)
                    dp = lax.dot_general(dos, vc, nt_dims, preferred_element_type=F32)
                    ds = (pr * (dp - es)).astype(BF16)
                    add_rows(dk_ref, k0, lax.dot_general(ds, qs, tn_dims, preferred_element_type=F32))
                    add_rows(dv_ref, k0, lax.dot_general(pr.astype(BF16), dos, tn_dims, preferred_element_type=F32))

                for u in range(nbt):
                    unit(u % r, u // r)
                if nt > 1:
                    @pl.when(t < nt - 1)
                    def _(r=r, halo_unit=halo_unit):
                        for j in range(r):
                            halo_unit(j)

    th = T // H
    last = S // H - 1
    qcur = pl.BlockSpec((T, nq * 128), lambda j, t: (t, j))
    qnext = pl.BlockSpec((H, nq * 128), lambda j, t: (jnp.minimum((t + 1) * th, last), j))
    cur = pl.BlockSpec((T, 128), lambda j, t: (t, j))
    prev = pl.BlockSpec((H, 128), lambda j, t: (jnp.maximum(t * th - 1, 0), j))
    dcur = pl.BlockSpec((T, nq * 128), lambda j, t: (t, j + do_col0))
    dnext = pl.BlockSpec((H, nq * 128), lambda j, t: (jnp.minimum((t + 1) * th, last), j + do_col0))
    in_specs = [qcur, qnext, prev, cur, prev, cur, qcur, qnext, qcur, qnext, dcur, dnext]
    args = [q, q, k, k, v, v, lse, lse, delta, delta, do, do]
    out_specs = [qcur, cur, cur]
    out_shape = [jax.ShapeDtypeStruct(q.shape, F32), jax.ShapeDtypeStruct(k.shape, F32), jax.ShapeDtypeStruct(k.shape, F32)]
    if has_sink:
        vec = pl.BlockSpec((1, nq * 128), lambda j, t: (0, j))
        in_specs.append(vec)
        args.append(sinks)
        out_specs.append(vec)
        out_shape.append(jax.ShapeDtypeStruct((1, q.shape[1]), F32))
    big = pltpu.VMEM((T + H, LANES), F32)
    return _pc(
        body, name=name, grid=(ncol, nt), in_specs=in_specs, out_specs=out_specs, out_shape=out_shape,
        scratch_shapes=[big] * 6,
        compiler_params=_params(("parallel", "arbitrary")),
    )(*args)


def _even_post_fwd(ro, proj, gn, da):
    S = ro.shape[0]
    tm = 512

    def body(ro_ref, rg_ref, gn_ref, da_ref, mix_ref):
        for c in range(4):
            sl = slice(c * 128, (c + 1) * 128)
            x = ro_ref[:, sl]
            mu = jnp.mean(x, axis=1, keepdims=True)
            xc = x - mu
            var = jnp.mean(xc * xc, axis=1, keepdims=True)
            y = xc * lax.rsqrt(var + EPS) * gn_ref[:, sl]
            z = rg_ref[:, sl]
            mix_ref[:, sl] = (z * jax.nn.sigmoid(z) * y).astype(BF16)
        mix_ref[:, 512:1024] = da_ref[...].astype(BF16)

    row = lambda w: pl.BlockSpec((tm, w), lambda i: (i, 0))
    return _pc(
        body, name="even_post_fwd", grid=(S // tm,),
        in_specs=[row(512), pl.BlockSpec((tm, 512), lambda i: (i, 2)), pl.BlockSpec((1, 512), lambda i: (0, 0)), row(512)],
        out_specs=row(1024), out_shape=jax.ShapeDtypeStruct((S, 1024), BF16),
        compiler_params=_params(("parallel",), VMEM_LIMIT_WIDE),
    )(ro, proj, gn, da)


def _even_post_bwd(ro, proj, gn, dmixed):
    S = ro.shape[0]
    tm = 512

    def body(ro_ref, rg_ref, gn_ref, dm_ref, dro_ref, drg_ref, dgn_ref):
        @pl.when(pl.program_id(0) == 0)
        def _():
            dgn_ref[...] = jnp.zeros_like(dgn_ref)

        for c in range(4):
            sl = slice(c * 128, (c + 1) * 128)
            x = ro_ref[:, sl]
            mu = jnp.mean(x, axis=1, keepdims=True)
            xc = x - mu
            rstd = lax.rsqrt(jnp.mean(xc * xc, axis=1, keepdims=True) + EPS)
            xh = xc * rstd
            gain = gn_ref[:, sl]
            y = xh * gain
            z = rg_ref[:, sl]
            sg = jax.nn.sigmoid(z)
            dra = dm_ref[:, sl]
            drg_ref[:, sl] = dra * y * sg * (1.0 + z * (1.0 - sg))
            dy = dra * z * sg
            dgn_ref[:, sl] += jnp.sum(dy * xh, axis=0, keepdims=True)
            dxh = dy * gain
            dro_ref[:, sl] = rstd * (dxh - jnp.mean(dxh, axis=1, keepdims=True)
                                     - xh * jnp.mean(dxh * xh, axis=1, keepdims=True))

    row = lambda w: pl.BlockSpec((tm, w), lambda i: (i, 0))
    vec = pl.BlockSpec((1, 512), lambda i: (0, 0))
    return _pc(
        body, name="even_post_bwd", grid=(S // tm,),
        in_specs=[row(512), pl.BlockSpec((tm, 512), lambda i: (i, 2)), vec, row(512)],
        out_specs=[row(512), row(512), vec],
        out_shape=[jax.ShapeDtypeStruct((S, 512), F32), jax.ShapeDtypeStruct((S, 512), F32),
                   jax.ShapeDtypeStruct((1, 512), F32)],
        compiler_params=_params(("arbitrary",), VMEM_LIMIT_WIDE),
    )(ro, proj, gn, dmixed)


def _swa_pre_fwd(proj, tab, qg, kg):
    S = proj.shape[0]
    tm = 512

    def body(p_ref, tab_ref, qg_ref, kg_ref, g_ref, q_ref, k_ref, v_ref):
        Ap, Bp, Cp = _tab(tab_ref, 1)
        G = g_ref[...]
        lo = _head_mask((tm, LANES), 0)
        for c in range(8):
            sl = slice(c * 128, (c + 1) * 128)
            q_ref[:, sl] = _rope(_hn_fwd(p_ref[:, sl], qg_ref[...], G), Ap, Bp, Cp, 8).astype(BF16)
        for c in range(2):
            kn = _rope(_hn_fwd(p_ref[:, 1024 + c * 128:1024 + (c + 1) * 128], kg_ref[...], G), Ap, Bp, Cp, 8)
            vv = p_ref[:, 1280 + c * 128:1280 + (c + 1) * 128]
            for t, ref in ((kn, k_ref), (vv, v_ref)):
                sw = _roll(t, HEAD)
                ref[:, (2 * c) * 128:(2 * c + 1) * 128] = jnp.where(lo, t, sw).astype(BF16)
                ref[:, (2 * c + 1) * 128:(2 * c + 2) * 128] = jnp.where(lo, sw, t).astype(BF16)

    row = lambda w: pl.BlockSpec((tm, w), lambda i: (i, 0))
    vec = pl.BlockSpec((1, LANES), lambda i: (0, 0))
    return _pc(
        body, name="swa_pre_fwd", grid=(S // tm,),
        in_specs=[row(1536), row(768), vec, vec, pl.BlockSpec((LANES, LANES), lambda i: (0, 0))],
        out_specs=[row(1024), row(512), row(512)],
        out_shape=[jax.ShapeDtypeStruct((S, w), BF16) for w in (1024, 512, 512)],
        compiler_params=_params(("parallel",), VMEM_LIMIT_WIDE),
    )(proj, tab, qg, kg, _group_matrix())


def _swa_pre_bwd(proj, tab, qg, kg, dq, dk, dv):
    S = proj.shape[0]
    tm = 512

    def body(p_ref, tab_ref, qg_ref, kg_ref, g_ref, dq_ref, dk_ref, dv_ref, dp_ref, db_ref, dqg_ref, dkg_ref):
        Ap, Bp, Cp = _tab(tab_ref, 1)
        G = g_ref[...]
        lo = _head_mask((tm, LANES), 0)

        @pl.when(pl.program_id(0) == 0)
        def _():
            db_ref[...] = jnp.zeros_like(db_ref)
            dqg_ref[...] = jnp.zeros_like(dqg_ref)
            dkg_ref[...] = jnp.zeros_like(dkg_ref)

        accq = jnp.zeros((1, LANES), F32)
        acck = jnp.zeros((1, LANES), F32)
        for c in range(8):
            sl = slice(c * 128, (c + 1) * 128)
            dx, dg = _hn_bwd(p_ref[:, sl], qg_ref[...], _rope_t(dq_ref[:, sl], Ap, Bp, Cp, 8), G)
            dp_ref[:, sl] = dx.astype(BF16)
            db_ref[:, sl] += jnp.sum(dx, axis=0, keepdims=True)
            accq = accq + dg
        for c in range(2):
            folded = []
            for ref in (dk_ref, dv_ref):
                a = ref[:, (2 * c) * 128:(2 * c + 1) * 128]
                b = ref[:, (2 * c + 1) * 128:(2 * c + 2) * 128]
                folded.append(jnp.where(lo, a + _roll(a, HEAD), b + _roll(b, HEAD)))
            ks = slice(1024 + c * 128, 1024 + (c + 1) * 128)
            dx, dg = _hn_bwd(p_ref[:, ks], kg_ref[...], _rope_t(folded[0], Ap, Bp, Cp, 8), G)
            dp_ref[:, ks] = dx.astype(BF16)
            db_ref[:, ks] += jnp.sum(dx, axis=0, keepdims=True)
            acck = acck + dg
            vs = slice(1280 + c * 128, 1280 + (c + 1) * 128)
            dp_ref[:, vs] = folded[1].astype(BF16)
            db_ref[:, vs] += jnp.sum(folded[1], axis=0, keepdims=True)
        dqg_ref[...] += _fold_halves(accq)
        dkg_ref[...] += _fold_halves(acck)

    row = lambda w: pl.BlockSpec((tm, w), lambda i: (i, 0))
    vec = pl.BlockSpec((1, LANES), lambda i: (0, 0))
    return _pc(
        body, name="swa_pre_bwd", grid=(S // tm,),
        in_specs=[row(1536), row(768), vec, vec, pl.BlockSpec((LANES, LANES), lambda i: (0, 0)),
                  row(1024), row(512), row(512)],
        out_specs=[row(1536), pl.BlockSpec((1, 1536), lambda i: (0, 0)), vec, vec],
        out_shape=[jax.ShapeDtypeStruct((S, 1536), BF16), jax.ShapeDtypeStruct((1, 1536), F32),
                   jax.ShapeDtypeStruct((1, LANES), F32), jax.ShapeDtypeStruct((1, LANES), F32)],
        compiler_params=_params(("arbitrary",), VMEM_LIMIT_WIDE),
    )(proj, tab, qg, kg, _group_matrix(), dq, dk, dv)


def _relu2_of(u):
    r = jnp.maximum(u.astype(F32), 0.0)
    return r * r


def _drelu2(acc, u):
    return (acc * 2.0 * jnp.maximum(u.astype(F32), 0.0),)


def _add(acc, res):
    return (acc + res,)


def _add_norm_in(res, g):
    def epilogue(acc, r, gv):
        xn = acc + r
        return xn, xn * lax.rsqrt(jnp.mean(xn * xn, axis=-1, keepdims=True) + EPS) * gv

    return dict(outs=[F32, BF16], epilogue=epilogue, extras=[(res, "mn"), (g.reshape(1, D_MODEL), "n")])


_T = dict(tm=1024, tn=1024, tk=1024)


def _rms_bwd_in(x, g, dres):
    def epilogue(dh, xv, gv, dr):
        r = lax.rsqrt(jnp.mean(xv * xv, axis=-1, keepdims=True) + EPS)
        t = dh * gv
        dx = dr + r * t - xv * (r * r * r) * jnp.mean(xv * t, axis=-1, keepdims=True)
        return dx, dx, jnp.sum(dh * xv * r, axis=0, keepdims=True)

    return dict(outs=[F32, BF16, ("colsum",)], epilogue=epilogue,
                extras=[(x, "mn"), (g.reshape(1, D_MODEL), "n"), (dres, "mn")])


def _delta_in(o, col0):
    width = D_MODEL - col0

    def epilogue(do, ov, G):
        parts = [_gmean(do[:, col0 + c * 128:col0 + (c + 1) * 128] * ov[:, c * 128:(c + 1) * 128], G) * float(HEAD)
                 for c in range(width // LANES)]
        return do, jnp.concatenate(parts, axis=1)

    return dict(outs=[F32, (F32, width)], epilogue=epilogue, extras=[(o, width), (_group_matrix(), "full")])


def _loss_in(res, target):
    def epilogue(acc, r, t):
        e = acc + r - t
        dy = e * (1.0 / D_MODEL)
        return dy, dy, jnp.sum(e * e, axis=0, keepdims=True) * (0.5 / D_MODEL)

    return dict(outs=[F32, BF16, ("colsum",)], epilogue=epilogue, extras=[(res, "mn"), (target, "mn")])


def _mlp_fwd(h, wts, layer, tag, tail):
    u = _matmul(h, wts, dims="nn", **_T, outs=[BF16], b_cs=True, b_row0=layer, name=f"mlp_up{tag}")
    out = _matmul(u, wts, dims="nn", **_T, a_pro=_relu2_of, b_rs=1024, b_row0=2 + layer, name=f"mlp_down{tag}", **tail)
    return out, (h, u)


def _mlp_bwd(x, g, wts, layer, saved, dy, dyb, tag):
    h, u = saved
    du = _matmul(dyb, wts, dims="nt", **_T, outs=[BF16], epilogue=_drelu2, extras=[(u, "mn")], b_rs=1024,
                 b_row0=2 + layer, name=f"mlp_du{tag}")
    dw_dn = _matmul(u, dyb, dims="tn", **_T, outs=[F32], a_pro=_relu2_of, name=f"mlp_dwdown{tag}")
    dw_up = _matmul(h, du, dims="tn", **_T, outs=[F32], o_cs=N_CHIPS, name=f"mlp_dwup{tag}")
    dx, dxb, dg = _matmul(du, wts, dims="nt", **_T, b_cs=True, b_row0=layer, b_rows=1024, name=f"mlp_dh{tag}",
                          vmem=VMEM_LIMIT_WIDE, **_rms_bwd_in(x, g, dy))
    return dx, dxb, dg, dw_up, dw_dn


def _local_step(x, pos_col, target, first_of, rest_begin, rest_of, P, red):
    S = x.shape[0]
    tab = _tables(pos_col)
    tile2 = lambda g: jnp.tile(g.reshape(1, HEAD), (1, 2))
    dqg, dkg = tile2(P["dil_q_gain"]), tile2(P["dil_k_gain"])
    sqg, skg = tile2(P["swa_q_gain"]), tile2(P["swa_k_gain"])
    gn = P["ret_gn_gain"].reshape(1, 512)
    sink_b = jnp.repeat(P["swa_sinks"].reshape(16), HEAD).reshape(1, 1024)

    h0 = _rms_fwd(x, P["norm_mix"][0], "rms_mix_fwd0")
    W = first_of((h0, tab))
    proj = _matmul(h0, W["hyb_w_in"], dims="nn", tm=1024, tn=768, tk=1024, outs=[F32], b_cs=True, name="hyb_in")
    rq, rk, rv, dq, dk, dv = _even_pre_fwd(proj, tab, dqg, dkg)
    ro, states = _ret_fwd(rq, rk, rv)
    dil = [(w // r, r) for w, r in DIL_PATTERNS]
    da, dlse = _band_fwd(dq, dk, dv, patterns=dil, nq=1, name="dil_fwd")
    mixed = rest_begin(_even_post_fwd(ro, proj, gn, da))
    x1, h1 = _matmul(mixed, W["hyb_w_out"], dims="nn", **_T, name="hyb_out", **_add_norm_in(x, P["norm_mlp"][0]))
    rest, bias = rest_of(x1)
    W = {**W, **rest}
    (x2, h2), mlp0 = _mlp_fwd(h1, W["packed"], 0, "0", _add_norm_in(x1, P["norm_mix"][1]))

    proj2 = _matmul(h2, W["swa_w_qkv"], dims="nn", tm=1024, tn=384, tk=1024, outs=[F32], b_cs=True,
                    epilogue=_add, extras=[(bias.reshape(1, 1536), "n")], name="swa_qkv")
    sq, sk, sv = _swa_pre_fwd(proj2, tab, sqg, skg)
    swa = [(SWA_DIST, 1)]
    so, slse, so_b = _band_fwd(sq, sk, sv, patterns=swa, nq=2, name="swa_fwd", sinks=sink_b, want_bf16=True)
    x3, h3 = _matmul(so_b, W["swa_w_out"], dims="nn", **_T, name="swa_out", **_add_norm_in(x2, P["norm_mlp"][1]))
    (dy, dyb, loss_cols), mlp1 = _mlp_fwd(h3, W["packed"], 1, "1", _loss_in(x3, target))
    loss = jnp.broadcast_to(jnp.sum(loss_cols), (1, LANES))

    gw, gp = {}, {}
    dx3, dx3b, dg_mlp1, gw["mlp_w_up1"], gw["mlp_w_down1"] = _mlp_bwd(x3, P["norm_mlp"][1], W["packed"], 1, mlp1, dy, dyb, "1")
    dx3b = red.begin("mlp1", {n: (gw[n], 1024) for n in ("mlp_w_up1", "mlp_w_down1")}, dx3b)
    gw["swa_w_out"] = _matmul(so_b, dx3b, dims="tn", **_T, outs=[F32], name="swa_dwout")
    dso, sdelta = _matmul(dx3b, W["swa_w_out"], dims="nt", **_T, name="swa_do", vmem=VMEM_LIMIT_WIDE, **_delta_in(so, 0))
    dsq, dsk, dsv, dsink = _band_bwd(sq, sk, sv, slse, sdelta, dso, patterns=swa, nq=2, name="swa_bwd", sinks=sink_b)
    dproj2, gp["swa_b_qkv"], gp["swa_q_gain"], gp["swa_k_gain"] = _swa_pre_bwd(proj2, tab, sqg, skg, dsq, dsk, dsv)
    gp["swa_sinks"] = dsink
    gw["swa_w_qkv"] = _matmul(h2, dproj2, dims="tn", tm=1024, tn=384, tk=1024, outs=[F32], o_cs=N_CHIPS, name="swa_dwqkv")
    dx2, dx2b, dg_mix1 = _matmul(dproj2, W["swa_w_qkv"], dims="nt", tm=1024, tn=1024, tk=384, b_cs=True, name="swa_dh",
                                 vmem=VMEM_LIMIT_WIDE, **_rms_bwd_in(x2, P["norm_mix"][1], dx3))
    dx2b = red.begin("swa", {"swa_w_qkv": (gw["swa_w_qkv"], 1024), "swa_w_out": (gw["swa_w_out"], 256)}, dx2b)
    dx2b = red.advance("mlp1", dx2b, dx2b)

    dx1, dx1b, dg_mlp0, gw["mlp_w_up0"], gw["mlp_w_down0"] = _mlp_bwd(x1, P["norm_mlp"][0], W["packed"], 0, mlp0, dx2, dx2b, "0")
    gw["hyb_w_out"] = _matmul(mixed, dx1b, dims="tn", **_T, outs=[F32], name="hyb_dwout")
    dx1b = red.begin("mlp0", {"mlp_w_up0": (gw["mlp_w_up0"], 1024), "mlp_w_down0": (gw["mlp_w_down0"], 1024),
                              "hyb_w_out": (gw["hyb_w_out"], 256)}, dx1b)
    dx1b = red.advance("swa", dx1b, dx1b)
    red.finish("mlp1", dx1b)
    dmixed, ddelta = _matmul(dx1b, W["hyb_w_out"], dims="nt", **_T, name="hyb_dmixed", vmem=VMEM_LIMIT_WIDE,
                             **_delta_in(da, 512))
    dro, drg, gp["ret_gn_gain"] = _even_post_bwd(ro, proj, gn, dmixed)
    drq, drk, drv = _ret_bwd(rq, rk, rv, states, dro)
    ddq, ddk, ddv = _band_bwd(dq, dk, dv, dlse, ddelta, dmixed, patterns=dil, nq=1, name="dil_bwd", do_col0=4)
    ddq = red.advance("mlp0", ddq, ddq)
    red.finish("swa", ddq)
    dproj, gp["dil_q_gain"], gp["dil_k_gain"] = _even_pre_bwd(proj, tab, dqg, dkg, drq, drk, drv, drg, [ddq], [ddk], [ddv])
    gw["hyb_w_in"] = _matmul(h0, dproj, dims="tn", tm=1024, tn=768, tk=1024, outs=[F32], o_cs=N_CHIPS, name="hyb_dwin")
    dproj = red.begin("win", {"hyb_w_in": (gw["hyb_w_in"], 1024)}, dproj)
    grad_x, _, dg_mix0 = _matmul(dproj, W["hyb_w_in"], dims="nt", tm=1024, tn=1024, tk=768, b_cs=True, name="hyb_dh",
                                 vmem=VMEM_LIMIT_WIDE, **_rms_bwd_in(x, P["norm_mix"][0], dx1))
    red.finish("mlp0", grad_x)
    gp["norm_mix"] = jnp.concatenate([dg_mix0, dg_mix1], axis=0)
    gp["norm_mlp"] = jnp.concatenate([dg_mlp0, dg_mlp1], axis=0)
    return loss, grad_x, gp


HBM = pl.BlockSpec(memory_space=pltpu.HBM)


def _place():
    x, y, c = lax.axis_index("x"), lax.axis_index("y"), lax.axis_index("c")
    chips = [(1 - x, y), (x, 1 - y), (1 - x, 1 - y)]
    return x, y, c, chips


SEM = pl.BlockSpec(memory_space=pltpu.SEMAPHORE)
EFFECT = pltpu.SideEffectType.DATAFLOW_SIDE_EFFECTING


def _half_block(ref, chip, core):
    rh = ref.shape[1] // 2
    return ref.at[2 * chip[0] + chip[1], pl.ds(core * rh, rh), :]


def _gather_start(buf, ride, name):
    def body(b_ref, ride_ref, s0, s1, s2, r0, r1, r2, b_out, ride_out):
        x, y, c, chips = _place()
        for chip, s, r in zip(chips, (s0, s1, s2), (r0, r1, r2)):
            mine = _half_block(b_ref, (x, y), c)
            pltpu.make_async_remote_copy(src_ref=mine, dst_ref=mine, send_sem=s, recv_sem=r,
                                         device_id=(*chip, c), device_id_type=MESH).start()

    sem = pltpu.SemaphoreType.DMA(())
    return _pc(
        body, name=name,
        out_shape=(sem,) * 6 + (pltpu.HBM(buf.shape, buf.dtype), pltpu.HBM(ride.shape, ride.dtype)),
        in_specs=(HBM, HBM), out_specs=(SEM,) * 6 + (HBM, HBM), input_output_aliases={0: 6, 1: 7},
        compiler_params=pltpu.CompilerParams(has_side_effects=EFFECT),
    )(pltpu.with_memory_space_constraint(buf, pltpu.HBM), pltpu.with_memory_space_constraint(ride, pltpu.HBM))


def _gather_wait(buf, sems, afters, name):
    def body(b_ref, s0, s1, s2, r0, r1, r2, *unread):
        x, y, c, chips = _place()
        for chip, s, r in zip(chips, (s0, s1, s2), (r0, r1, r2)):
            cp = pltpu.make_async_remote_copy(src_ref=_half_block(b_ref, (x, y), c), dst_ref=_half_block(b_ref, chip, c),
                                              send_sem=s, recv_sem=r, device_id=(*chip, c), device_id_type=MESH)
            cp.wait_send()
            cp.wait_recv()

    return _pc(
        body, name=name, out_shape=pltpu.HBM(buf.shape, buf.dtype),
        in_specs=(HBM,) + (SEM,) * 6 + (pl.BlockSpec(memory_space=pl.ANY),) * len(afters), out_specs=HBM,
        input_output_aliases={0: 0}, compiler_params=pltpu.CompilerParams(has_side_effects=EFFECT),
    )(buf, *sems, *afters)


def _gather_handover(buf, name):
    def body(b_ref, out_ref, send_sems, recv_sems):
        x, y, c, chips = _place()
        cps = []
        for k, chip in enumerate(chips):
            mine = _half_block(b_ref, chip, c)
            cps.append(pltpu.make_async_remote_copy(src_ref=mine, dst_ref=mine, send_sem=send_sems.at[k],
                                                    recv_sem=recv_sems.at[k], device_id=(x, y, 1 - c), device_id_type=MESH))
        for cp in cps:
            cp.start()
        for k, chip in enumerate(chips):
            theirs = _half_block(b_ref, chip, 1 - c)
            pltpu.make_async_remote_copy(src_ref=theirs, dst_ref=theirs, send_sem=send_sems.at[k], recv_sem=recv_sems.at[k],
                                         device_id=(x, y, 1 - c), device_id_type=MESH).wait_recv()
        for cp in cps:
            cp.wait_send()

    return _pc(
        body, name=name, in_specs=[HBM], out_specs=HBM,
        out_shape=jax.ShapeDtypeStruct(buf.shape, buf.dtype), input_output_aliases={0: 0},
        scratch_shapes=[pltpu.SemaphoreType.DMA((3,)), pltpu.SemaphoreType.DMA((3,))],
    )(buf)


def _pair_sum(t, l, place, name):
    _, r, cols = t.shape
    rh = r // 2
    tr = min(rh, 256)
    nr = rh // tr

    def body(pl_ref, t_ref, l_ref, o_ref):
        o_ref[...] = (t_ref[...] + l_ref[...]).astype(BF16)

    other = lambda s, p: s + jnp.where(s >= p[0], 1, 0)
    return _pc(
        body, name=name,
        grid_spec=pltpu.PrefetchScalarGridSpec(
            num_scalar_prefetch=1, grid=(N_CHIPS - 1, nr),
            in_specs=[pl.BlockSpec((None, tr, cols), lambda s, i, p: (other(s, p), p[1] * nr + i, 0)),
                      pl.BlockSpec((None, tr, cols), lambda s, i, p: (other(s, p), i, 0))],
            out_specs=pl.BlockSpec((None, tr, cols), lambda s, i, p: (other(s, p), i, 0))),
        out_shape=jax.ShapeDtypeStruct((N_CHIPS, rh, cols), BF16),
        compiler_params=_params(("parallel", "parallel")),
    )(place, t, l)


def _final_sum(t, l, rcv, place, name, layer=0, layers=1, into=None):
    _, r, cols = t.shape
    rh = r // 2
    tr = min(rh, 256)
    nr = rh // tr

    def body(pl_ref, t_ref, l_ref, r_ref, *rest):
        acc = t_ref[...] + l_ref[...]
        for k in range(3):
            acc = acc + r_ref[k].astype(F32)
        rest[-1][...] = acc

    in_specs = [pl.BlockSpec((None, tr, cols), lambda i, p: (p[0], p[1] * nr + i, 0)),
                pl.BlockSpec((None, tr, cols), lambda i, p: (p[0], i, 0)),
                pl.BlockSpec((3, tr, cols), lambda i, p: (0, i, 0))]
    args = [place, t, l, rcv]
    aliases = {}
    if into is not None:
        in_specs.append(pl.BlockSpec(memory_space=pl.ANY))
        args.append(into)
        aliases = {4: 0}
    return _pc(
        body, name=name,
        grid_spec=pltpu.PrefetchScalarGridSpec(
            num_scalar_prefetch=1, grid=(nr,), in_specs=in_specs,
            out_specs=pl.BlockSpec((tr, cols), lambda i, p: (2 * nr * layer + p[1] * nr + i, 0))),
        out_shape=jax.ShapeDtypeStruct((layers * r, cols), F32), input_output_aliases=aliases,
        compiler_params=_params(("parallel",)),
    )(*args)


def _share_halves(hs, name):
    nt = len(hs)
    n = sum(layers for _, layers in hs)

    def body(*refs):
        h_refs, send_sems, recv_sems = refs[:nt], refs[-2], refs[-1]
        x, y, c, _ = _place()
        cps = []
        for k, (_, layers) in enumerate(hs):
            rh = h_refs[k].shape[0] // (2 * layers)
            for layer in range(layers):
                half = h_refs[k].at[pl.ds((2 * layer + c) * rh, rh), :]
                cps.append(pltpu.make_async_remote_copy(
                    src_ref=half, dst_ref=half, send_sem=send_sems.at[len(cps)], recv_sem=recv_sems.at[len(cps)],
                    device_id=(x, y, 1 - c), device_id_type=MESH))
        for cp in cps:
            cp.start()
        for cp in cps:
            cp.wait()

    return _pc(
        body, name=name, in_specs=[HBM] * nt, out_specs=[HBM] * nt,
        out_shape=[jax.ShapeDtypeStruct(h.shape, F32) for h, _ in hs],
        input_output_aliases={k: k for k in range(nt)},
        scratch_shapes=[pltpu.SemaphoreType.DMA((n,)), pltpu.SemaphoreType.DMA((n,))],
    )(*[h for h, _ in hs])


def _split_start(name, bufs, ride, n, copies_of):
    nb = len(bufs)

    def body(*refs):
        sems = refs[nb + 1:nb + 1 + 2 * n]
        for cp in copies_of(refs[:nb], sems[:n], sems[n:]):
            (cp[0] if isinstance(cp, tuple) else cp).start()

    outs = _pc(
        body, name=name,
        out_shape=(pltpu.SemaphoreType.DMA(()),) * (2 * n) + tuple(pltpu.HBM(b.shape, b.dtype) for b in bufs)
        + (pltpu.HBM(ride.shape, ride.dtype),),
        in_specs=(HBM,) * (nb + 1), out_specs=(SEM,) * (2 * n) + (HBM,) * (nb + 1),
        input_output_aliases={k: 2 * n + k for k in range(nb + 1)},
        compiler_params=pltpu.CompilerParams(has_side_effects=EFFECT),
    )(*[pltpu.with_memory_space_constraint(b, pltpu.HBM) for b in bufs], pltpu.with_memory_space_constraint(ride, pltpu.HBM))
    return list(outs[:2 * n]), list(outs[2 * n:2 * n + nb]), outs[-1]


def _split_wait(name, bufs, sems, after, n, copies_of):
    nb = len(bufs)

    def body(*refs):
        s = refs[nb:nb + 2 * n]
        for cp in copies_of(refs[:nb], s[:n], s[n:]):
            sent, landed = cp if isinstance(cp, tuple) else (cp, cp)
            sent.wait_send()
            landed.wait_recv()

    outs = _pc(
        body, name=name, out_shape=tuple(pltpu.HBM(b.shape, b.dtype) for b in bufs),
        in_specs=(HBM,) * nb + (SEM,) * (2 * n) + (pl.BlockSpec(memory_space=pl.ANY),), out_specs=(HBM,) * nb,
        input_output_aliases={k: k for k in range(nb)},
        compiler_params=pltpu.CompilerParams(has_side_effects=EFFECT),
    )(*bufs, *sems, after)
    return list(outs)


def _handover_copies(refs, send, recv):
    x, y, c, chips = _place()
    cps = []
    for k, chip in enumerate(chips):
        mine, theirs = _half_block(refs[0], chip, c), _half_block(refs[0], chip, 1 - c)
        desc = lambda blk: pltpu.make_async_remote_copy(src_ref=blk, dst_ref=blk, send_sem=send[k], recv_sem=recv[k],
                                                        device_id=(x, y, 1 - c), device_id_type=MESH)
        cps.append((desc(mine), desc(theirs)))
    return cps


def _share_copies(layers_of):
    def copies_of(refs, send, recv):
        x, y, c, _ = _place()
        cps = []
        for k, layers in enumerate(layers_of):
            rh = refs[k].shape[0] // (2 * layers)
            for layer in range(layers):
                i = len(cps)
                desc = lambda half: pltpu.make_async_remote_copy(
                    src_ref=refs[k].at[pl.ds((2 * layer + half) * rh, rh), :], dst_ref=refs[k].at[pl.ds((2 * layer + half) * rh, rh), :],
                    send_sem=send[i], recv_sem=recv[i], device_id=(x, y, 1 - c), device_id_type=MESH)
                cps.append((desc(c), desc(1 - c)))
        return cps
    return copies_of


def _swap_copies(nt):
    def copies_of(refs, send, recv):
        x, y, c, _ = _place()
        cps = []
        for k in range(nt):
            rh = refs[k].shape[1] // 2
            cps.append(pltpu.make_async_remote_copy(
                src_ref=refs[k].at[:, pl.ds((1 - c) * rh, rh), :], dst_ref=refs[nt + k],
                send_sem=send[k], recv_sem=recv[k], device_id=(x, y, 1 - c), device_id_type=MESH))
        return cps
    return copies_of


def _exchange_copies(nt):
    def copies_of(refs, send, recv):
        x, y, c, chips = _place()
        cps = []
        for t in range(nt):
            for k, chip in enumerate(chips):
                cps.append(pltpu.make_async_remote_copy(
                    src_ref=refs[t].at[2 * chip[0] + chip[1]], dst_ref=refs[nt + t].at[k],
                    send_sem=send[3 * t + k], recv_sem=recv[3 * t + k], device_id=(*chip, c), device_id_type=MESH))
        return cps
    return copies_of


class _StagedReduce:
    def __init__(self, place):
        self.place = place
        self.groups = {}
        self.halves = {}

    @staticmethod
    def slab(t, r):
        return t.reshape(N_CHIPS, r, t.size // (N_CHIPS * r))

    def begin(self, g, grads, ride):
        names = list(grads)
        ts = [self.slab(t, r) for t, r in grads.values()]
        lands = [lax.empty((N_CHIPS, t.shape[1] // 2, t.shape[2]), F32) for t in ts]
        sems, bufs, ride = _split_start(f"grad_swap_start_{g}", ts + lands, ride, len(ts), _swap_copies(len(ts)))
        self.groups[g] = dict(names=names, bufs=bufs, sems=sems)
        return ride

    def advance(self, g, after, ride):
        st = self.groups[g]
        nt = len(st["names"])
        bufs = _split_wait(f"grad_swap_wait_{g}", st["bufs"], st["sems"], after, nt, _swap_copies(nt))
        st["ts"], st["ls"] = bufs[:nt], bufs[nt:]
        ps = [_pair_sum(t, l, self.place, f"pair_sum_{n}") for t, l, n in zip(st["ts"], st["ls"], st["names"])]
        lands = [lax.empty((3,) + p.shape[1:], BF16) for p in ps]
        st["sems"], st["bufs"], ride = _split_start(f"grad_exchange_start_{g}", ps + lands, ride, 3 * nt, _exchange_copies(nt))
        return ride

    def finish(self, g, after):
        st = self.groups[g]
        nt = len(st["names"])
        bufs = _split_wait(f"grad_exchange_wait_{g}", st["bufs"], st["sems"], after, 3 * nt, _exchange_copies(nt))
        for t, l, r, n in zip(st["ts"], st["ls"], bufs[nt:], st["names"]):
            if n[-1] in "01":
                self.halves[n[:-1]] = _final_sum(t, l, r, self.place, f"final_sum_{n}", layer=int(n[-1]), layers=2,
                                                 into=self.halves.get(n[:-1]))
            else:
                self.halves[n] = _final_sum(t, l, r, self.place, f"final_sum_{n}")


def _allgather_small(v):
    rows = v.shape[0]

    def body(v_ref, out_ref, send_sems, recv_sems):
        x, y, c, _ = _place()
        me = 4 * x + 2 * y + c
        out_ref[me] = v_ref[...]
        cps = []
        for k in range(1, 8):
            fx, fy, fc = (k >> 2) & 1, (k >> 1) & 1, k & 1
            to = (1 - x if fx else x, 1 - y if fy else y, 1 - c if fc else c)
            cps.append(pltpu.make_async_remote_copy(
                src_ref=v_ref, dst_ref=out_ref.at[me], send_sem=send_sems.at[k - 1], recv_sem=recv_sems.at[k - 1],
                device_id=to, device_id_type=MESH))
        for cp in cps:
            cp.start()
        for cp in cps:
            cp.wait()

    return _pc(
        body, name="allgather_small",
        in_specs=[pl.BlockSpec(memory_space=pltpu.VMEM)], out_specs=pl.BlockSpec(memory_space=pltpu.VMEM),
        out_shape=jax.ShapeDtypeStruct((8, rows, LANES), F32),
        scratch_shapes=[pltpu.SemaphoreType.DMA((7,)), pltpu.SemaphoreType.DMA((7,))],
    )(v)


def _adamw_math(w, g, m, v):
    m = ADAM_B1 * m + (1.0 - ADAM_B1) * g
    v = ADAM_B2 * v + (1.0 - ADAM_B2) * (g * g)
    m_hat = m / (1.0 - ADAM_B1 ** ADAM_STEP)
    v_hat = v / (1.0 - ADAM_B2 ** ADAM_STEP)
    return -ADAM_LR * (m_hat / (jnp.sqrt(v_hat) + ADAM_EPS) + ADAM_WD * w), m, v


def _adamw(w, g, m, v, name):
    r, cols = w.shape
    tr = min(r, 256)

    def body(w_ref, g_ref, m_ref, v_ref, go_ref, d_ref, mo_ref, vo_ref):
        gv = g_ref[...]
        d, mn, vn = _adamw_math(w_ref[...], gv, m_ref[...], v_ref[...])
        go_ref[...] = gv
        d_ref[...] = d
        mo_ref[...] = mn
        vo_ref[...] = vn

    row = pl.BlockSpec((tr, cols), lambda i: (i, 0))
    return _pc(
        body, name=name, grid=(r // tr,), in_specs=[row] * 4, out_specs=[row] * 4,
        out_shape=[jax.ShapeDtypeStruct((r, cols), F32)] * 4,
        compiler_params=_params(("parallel",)),
    )(w, g, m, v)


def _adamw_small(w, gathered, m, v):
    rows = w.shape[0]

    def body(w_ref, g_ref, m_ref, v_ref, go_ref, d_ref, mo_ref, vo_ref):
        g = g_ref[0]
        for k in range(1, 8):
            g = g + g_ref[k]
        d, mn, vn = _adamw_math(w_ref[...], g, m_ref[...], v_ref[...])
        go_ref[...] = g
        d_ref[...] = d
        mo_ref[...] = mn
        vo_ref[...] = vn

    return _pc(
        body, name="adamw_small",
        out_shape=[jax.ShapeDtypeStruct((rows, LANES), F32)] * 4,
    )(w, gathered, m, v)


_BIAS_ROWS = 32


def _own_slot(flat, chip):
    return lax.dynamic_update_slice(lax.empty((N_CHIPS,) + flat.shape, flat.dtype), flat[None], (chip, 0, 0))


def _pack_first(hyb_w_in, hyb_w_out):
    return jnp.concatenate([t.astype(BF16).reshape(-1, 1024) for t in (hyb_w_in, hyb_w_out)], axis=0)


def _unpack_first(g):
    return {"hyb_w_in": g[:, 0:768, :].reshape(N_CHIPS, 1024, 768), "hyb_w_out": g[:, 768:1024, :].reshape(1024, 1024)}


def _pack_rest(mlp_w_up, mlp_w_down, swa_w_qkv, swa_w_out, swa_b_qkv):
    parts = [t.astype(BF16).reshape(-1, 1024) for t in (mlp_w_up, mlp_w_down, swa_w_qkv, swa_w_out)]
    bias = lax.bitcast_convert_type(swa_b_qkv.reshape(384), BF16).reshape(1, 768)
    bias = jnp.pad(bias, ((0, _BIAS_ROWS - 1), (0, 256)))
    return jnp.concatenate(parts + [bias], axis=0)


def _unpack_rest(g):
    W = {
        "packed": g,
        "swa_w_qkv": g[:, 4096:4480, :].reshape(N_CHIPS, 1024, 384),
        "swa_w_out": g[:, 4480:4736, :].reshape(1024, 1024),
    }
    bias = lax.bitcast_convert_type(g[:, 4736, :768].reshape(N_CHIPS, 384, 2), F32).reshape(1536)
    return W, bias


_SMALL = (("norm_mix", 16), ("norm_mlp", 16), ("ret_gn_gain", 4), ("dil_q_gain", 1), ("dil_k_gain", 1),
          ("swa_b_qkv", 12), ("swa_q_gain", 1), ("swa_k_gain", 1), ("swa_sinks", 1), ("loss", 1))
_SUBLANES = 8


def _slot(r):
    return -(-r // _SUBLANES) * _SUBLANES


def _pack_small(d):
    return jnp.concatenate([jnp.pad(d[n].reshape(r, LANES), ((0, _slot(r) - r), (0, 0))) for n, r in _SMALL], axis=0)


def _unpack_small(p):
    out, o = {}, 0
    for n, r in _SMALL:
        out[n] = p[o:o + r]
        o += _slot(r)
    return out


def kernel(x, positions, norm_mix, norm_mlp, mlp_w_up, mlp_w_down, hyb_w_in, hyb_w_out, ret_gn_gain, dil_q_gain, dil_k_gain, swa_w_qkv, swa_b_qkv, swa_w_out, swa_q_gain, swa_k_gain, swa_sinks, loss_target, m_norm_mix, m_norm_mlp, m_mlp_w_up, m_mlp_w_down, m_hyb_w_in, m_hyb_w_out, m_ret_gn_gain, m_dil_q_gain, m_dil_k_gain, m_swa_w_qkv, m_swa_b_qkv, m_swa_w_out, m_swa_q_gain, m_swa_k_gain, m_swa_sinks, v_norm_mix, v_norm_mlp, v_mlp_w_up, v_mlp_w_down, v_hyb_w_in, v_hyb_w_out, v_ret_gn_gain, v_dil_q_gain, v_dil_k_gain, v_swa_w_qkv, v_swa_b_qkv, v_swa_w_out, v_swa_q_gain, v_swa_k_gain, v_swa_sinks):
    ax, ay, ac = lax.axis_index("x"), lax.axis_index("y"), lax.axis_index("c")
    chip = 2 * ax + ay
    place = jnp.stack([chip, ac]).astype(jnp.int32)
    S = x.shape[1]

    first = _own_slot(_pack_first(hyb_w_in[0], hyb_w_out[0]), chip)
    rest = _own_slot(_pack_rest(mlp_w_up, mlp_w_down, swa_w_qkv[0], swa_w_out[0], swa_b_qkv[0]), chip)
    *sems, first, pos_col = _gather_start(first, positions.reshape(S, 1), "allgather_first_start")
    flight = {}

    def first_of(afters):
        g = _gather_handover(_gather_wait(first, sems, [*afters, rest], "allgather_first_wait"), "allgather_first_handover")
        *flight["sems"], flight["buf"], g = _gather_start(rest, g, "allgather_rest_start")
        return _unpack_first(g)

    def rest_begin(ride):
        buf = _gather_wait(flight["buf"], flight["sems"], [ride], "allgather_rest_wait")
        flight["sems"], flight["bufs"], ride = _split_start("allgather_rest_handover_start", [buf], ride, 3, _handover_copies)
        return ride

    def rest_of(after):
        return _unpack_rest(_split_wait("allgather_rest_handover_wait", flight["bufs"], flight["sems"], after, 3,
                                        _handover_copies)[0])

    P = dict(norm_mix=norm_mix, norm_mlp=norm_mlp, ret_gn_gain=ret_gn_gain, dil_q_gain=dil_q_gain, dil_k_gain=dil_k_gain,
             swa_q_gain=swa_q_gain, swa_k_gain=swa_k_gain, swa_sinks=swa_sinks)

    red = _StagedReduce(place)
    loss_l, grad_x, gp = _local_step(x[0], pos_col, loss_target[0], first_of, rest_begin, rest_of, P, red)

    params = dict(mlp_w_up=(mlp_w_up, m_mlp_w_up, v_mlp_w_up), mlp_w_down=(mlp_w_down, m_mlp_w_down, v_mlp_w_down),
                  hyb_w_in=(hyb_w_in, m_hyb_w_in, v_hyb_w_in), hyb_w_out=(hyb_w_out, m_hyb_w_out, v_hyb_w_out),
                  swa_w_qkv=(swa_w_qkv, m_swa_w_qkv, v_swa_w_qkv), swa_w_out=(swa_w_out, m_swa_w_out, v_swa_w_out))
    big = {}

    def adamw_of(n, g):
        rows = g.shape[0]
        w, m, v = (t.reshape(rows, -1) for t in params[n])
        big[n] = [t.reshape(params[n][0].shape) for t in _adamw(w, g, m, v, f"adamw_{n}")]

    names = ["mlp_w_up", "mlp_w_down", "hyb_w_out", "swa_w_qkv", "swa_w_out"]
    red.halves[names[0]] = red.advance("win", grad_x, red.halves[names[0]])

    gsm = dict(gp, loss=loss_l)
    gsm["swa_sinks"] = jnp.pad(gp["swa_sinks"].reshape(16, HEAD)[:, 0], (0, LANES - 16))
    layers = [params[n][0].shape[0] for n in names]
    sems, shared, packed = _split_start("grad_share_start", [red.halves[n] for n in names], _pack_small(gsm), sum(layers),
                                        _share_copies(layers))
    gathered = _allgather_small(packed)

    def small_pack(norm_mix, norm_mlp, gn, dq, dk, b, sq, sk, sinks):
        dup = lambda t: jnp.tile(t.reshape(1, HEAD), (1, 2))
        bias = lax.dynamic_update_slice(jnp.zeros((12, LANES), F32), b.reshape(3, LANES), (3 * chip, 0))
        return _pack_small(dict(norm_mix=norm_mix, norm_mlp=norm_mlp, ret_gn_gain=gn, dil_q_gain=dup(dq), dil_k_gain=dup(dk),
                                swa_b_qkv=bias, swa_q_gain=dup(sq), swa_k_gain=dup(sk),
                                swa_sinks=jnp.pad(sinks.reshape(16), (0, LANES - 16)), loss=jnp.zeros((1, LANES), F32)))

    pw = small_pack(norm_mix, norm_mlp, ret_gn_gain, dil_q_gain, dil_k_gain, swa_b_qkv, swa_q_gain, swa_k_gain, swa_sinks)
    pm = small_pack(m_norm_mix, m_norm_mlp, m_ret_gn_gain, m_dil_q_gain, m_dil_k_gain, m_swa_b_qkv, m_swa_q_gain, m_swa_k_gain, m_swa_sinks)
    pv = small_pack(v_norm_mix, v_norm_mlp, v_ret_gn_gain, v_dil_q_gain, v_dil_k_gain, v_swa_b_qkv, v_swa_q_gain, v_swa_k_gain, v_swa_sinks)
    small_flat = _adamw_small(pw, gathered, pm, pv)
    small = [_unpack_small(t) for t in small_flat]

    for n, g in zip(names, _split_wait("grad_share_wait", shared, sems, small_flat[1], sum(layers), _share_copies(layers))):
        adamw_of(n, g)
    red.finish("win", big[names[-1]][1])
    adamw_of("hyb_w_in", _share_halves([(red.halves["hyb_w_in"], 1)], "grad_share_last")[0])

    def small_out(n, k):
        t = small[k][n]
        if n in ("norm_mix", "norm_mlp"):
            return t.reshape(2, D_MODEL)
        if n == "ret_gn_gain":
            return t.reshape(1, RET_HEADS, 128)
        if n == "swa_b_qkv":
            return lax.dynamic_slice(t, (3 * chip, 0), (3, LANES)).reshape(1, 384)
        if n == "swa_sinks":
            return t[0, :16].reshape(1, 16)
        return t[0, :HEAD].reshape(1, HEAD)

    order = ["norm_mix", "norm_mlp", "mlp_w_up", "mlp_w_down", "hyb_w_in", "hyb_w_out", "ret_gn_gain", "dil_q_gain",
             "dil_k_gain", "swa_w_qkv", "swa_b_qkv", "swa_w_out", "swa_q_gain", "swa_k_gain", "swa_sinks"]
    is_big = {"mlp_w_up", "mlp_w_down", "hyb_w_in", "hyb_w_out", "swa_w_qkv", "swa_w_out"}
    outs = [small[0]["loss"][0, 0], grad_x[None]]
    for k in range(4):
        outs += [big[n][k] if n in is_big else small_out(n, k) for n in order]
    return tuple(outs)
```

```python
import numpy as np
import jax
import jax.numpy as jnp
from jax import lax
from jax.experimental import pallas as pl
from jax.experimental.pallas import tpu as pltpu

F32, BF16 = jnp.float32, jnp.bfloat16
MESH = pl.DeviceIdType.MESH

LANES = 128
VMEM_LIMIT = 48 << 20
VMEM_LIMIT_WIDE = 60 << 20
D_MODEL = 1024
HEAD = 64
EPS = 1e-6
BLK = 128
RET_HEADS = 4
RET_THETA = 10000.0
ROPE_THETA = 500000.0
ROPE_DIMS = 16
DIL_PATTERNS = ((128, 1), (512, 4), (2048, 16))
SWA_DIST = 127
N_CHIPS = 4
ADAM_LR, ADAM_B1, ADAM_B2, ADAM_EPS, ADAM_WD, ADAM_STEP = 0.001, 0.9, 0.999, 1e-08, 0.01, 10

_LOG_GAMMA = [float(np.log1p(-np.exp2(np.float32(-5.0 - h)))) for h in range(RET_HEADS)]


def _pc(body, **kw):
    return pl.pallas_call(body, **kw)


def _params(sem, limit=VMEM_LIMIT):
    return pltpu.CompilerParams(dimension_semantics=sem, vmem_limit_bytes=limit)


def _matmul(a, b, *, dims, tm, tn, tk, outs, name, epilogue=None, extras=(), b_cs=False, b_rs=0, b_row0=0, b_rows=0,
            o_cs=0, a_pro=None, vmem=VMEM_LIMIT):
    if dims == "nn":
        M, K = a.shape
        N = b.shape[0] * b.shape[2] if b_cs else b.shape[1]
        a_spec = pl.BlockSpec((tm, tk), lambda i, j, k: (i, k))
        if b_cs:
            npt = b.shape[2] // tn
            b_spec = pl.BlockSpec((None, tk, tn), lambda i, j, k: (j // npt, k + b_row0, j % npt))
        elif b_rs:
            K, N, kps = b.shape[0] * b_rs, b.shape[2], b_rs // tk
            b_spec = pl.BlockSpec((None, tk, tn), lambda i, j, k: (k // kps, b_row0 + k % kps, j))
        else:
            b_spec = pl.BlockSpec((tk, tn), lambda i, j, k: (k, j))
        contract = (((1,), (0,)), ((), ()))
    elif dims == "nt":
        M, K = a.shape
        N = (b_rows or b.shape[1]) if b_cs else b.shape[0]
        a_spec = pl.BlockSpec((tm, tk), lambda i, j, k: (i, k))
        if b_cs:
            kpt = b.shape[2] // tk
            b_spec = pl.BlockSpec((None, tn, tk), lambda i, j, k: (k // kpt, j + b_row0, k % kpt))
        elif b_rs:
            N, jps = b.shape[0] * b_rs, b_rs // tn
            b_spec = pl.BlockSpec((None, tn, tk), lambda i, j, k: (j // jps, b_row0 + j % jps, k))
        else:
            b_spec = pl.BlockSpec((tn, tk), lambda i, j, k: (j, k))
        contract = (((1,), (1,)), ((), ()))
    else:
        K, M = a.shape
        N = b.shape[1]
        a_spec = pl.BlockSpec((tk, tm), lambda i, j, k: (k, i))
        b_spec = pl.BlockSpec((tk, tn), lambda i, j, k: (k, j))
        contract = (((0,), (0,)), ((), ()))
    assert M % tm == 0 and N % tn == 0 and K % tk == 0, (name, M, N, K, tm, tn, tk)
    nk = K // tk
    ex_specs = []
    for arr, kind in extras:
        if kind == "mn":
            ex_specs.append(pl.BlockSpec((tm, tn), lambda i, j, k: (i, j)))
        elif kind == "n":
            ex_specs.append(pl.BlockSpec((1, tn), lambda i, j, k: (0, j)))
        elif kind == "full":
            ex_specs.append(pl.BlockSpec(arr.shape, lambda i, j, k, nd=arr.ndim: (0,) * nd))
        else:
            ex_specs.append(pl.BlockSpec((tm, kind), lambda i, j, k: (i, 0)))
    if o_cs:
        n_sh = N // o_cs
        opt = n_sh // tn
        o_shape = (o_cs, M, n_sh)
        o_spec = pl.BlockSpec((None, tm, tn), lambda i, j, k: (j // opt, i, j % opt))
    else:
        o_shape = (M, N)
        o_spec = pl.BlockSpec((tm, tn), lambda i, j, k: (i, j))
    o_specs, o_shapes, summed = [], [], []
    for o in outs:
        if isinstance(o, tuple) and o[0] == "colsum":
            assert N == tn
            o_specs.append(pl.BlockSpec((1, tn), lambda i, j, k: (0, j)))
            o_shapes.append(jax.ShapeDtypeStruct((1, N), F32))
            summed.append(True)
        elif isinstance(o, tuple):
            o_specs.append(pl.BlockSpec((tm, o[1]), lambda i, j, k: (i, 0)))
            o_shapes.append(jax.ShapeDtypeStruct((M, o[1]), o[0]))
            summed.append(False)
        else:
            o_specs.append(o_spec)
            o_shapes.append(jax.ShapeDtypeStruct(o_shape, o))
            summed.append(False)
    n_ex, n_out = len(extras), len(outs)
    if epilogue is None:
        epilogue = lambda acc: (acc,)

    def body(a_ref, b_ref, *rest):
        ex, o_refs, acc = rest[:n_ex], rest[n_ex:n_ex + n_out], rest[-1]
        i, k = pl.program_id(0), pl.program_id(2)

        @pl.when(k == 0)
        def _():
            acc[...] = jnp.zeros_like(acc)

        av = a_ref[...] if a_pro is None else a_pro(a_ref[...])
        acc[...] += lax.dot_general(av.astype(BF16), b_ref[...].astype(BF16), contract, preferred_element_type=F32)

        @pl.when(k == nk - 1)
        def _():
            vals = epilogue(acc[...], *[e[...] for e in ex])
            for r, v, sm in zip(o_refs, vals, summed):
                if sm:
                    @pl.when(i == 0)
                    def _(r=r):
                        r[...] = jnp.zeros_like(r)

                    r[...] += v
                else:
                    r[...] = v.astype(r.dtype)

    res = _pc(
        body, name=name, grid=(M // tm, N // tn, nk),
        in_specs=[a_spec, b_spec] + ex_specs, out_specs=o_specs, out_shape=o_shapes,
        scratch_shapes=[pltpu.VMEM((tm, tn), F32)],
        compiler_params=_params(("arbitrary" if any(summed) else "parallel", "parallel", "arbitrary"), vmem),
    )(a, b, *[e for e, _ in extras])
    return res[0] if n_out == 1 else res


def _roll(x, s):
    return pltpu.roll(x, s % LANES, 1)


def _rope(x, A, B, C, half):
    return x * A + _roll(x, LANES - half) * B + _roll(x, half) * C


def _rope_t(g, A, B, C, half):
    return g * A + _roll(g * B, half) + _roll(g * C, LANES - half)


def _gmean(x, G):
    hi = x.astype(BF16)
    lo = (x - hi.astype(F32)).astype(BF16)
    Gb = G.astype(BF16)
    return jnp.dot(hi, Gb, preferred_element_type=F32) + jnp.dot(lo, Gb, preferred_element_type=F32)


def _head_mask(shape, half):
    lane = lax.broadcasted_iota(jnp.int32, shape, len(shape) - 1)
    return (lane >= HEAD) if half else (lane < HEAD)


def _group_matrix():
    i = np.arange(LANES)
    return jnp.asarray((i[:, None] // HEAD == i[None, :] // HEAD).astype(np.float32) / HEAD)


def _rope_inv():
    l = np.arange(LANES) % HEAD
    inv_r = np.power(np.float32(RET_THETA), -(l % 32).astype(np.float32) * np.float32(2.0 / HEAD))
    hp = ROPE_DIMS // 2
    inv_p = np.power(np.float32(ROPE_THETA), -(l % hp).astype(np.float32) * np.float32(2.0 / ROPE_DIMS))
    inv_p = np.where(l < ROPE_DIMS, inv_p, 0.0)
    return jnp.asarray(np.stack([inv_r, inv_p]).astype(np.float32))


def _tables(pos_col):
    S = pos_col.shape[0]
    tm = 512
    hp = ROPE_DIMS // 2

    def body(p_ref, inv_ref, o_ref):
        p = p_ref[...].astype(F32)
        lane = lax.broadcasted_iota(jnp.int32, (tm, LANES), 1) % HEAD
        ang = p * inv_ref[0:1, :]
        c, s = jnp.cos(ang), jnp.sin(ang)
        o_ref[:, 0:128] = c
        o_ref[:, 128:256] = jnp.where(lane < 32, -s, 0.0)
        o_ref[:, 256:384] = jnp.where(lane >= 32, s, 0.0)
        ang = p * inv_ref[1:2, :]
        c, s = jnp.cos(ang), jnp.sin(ang)
        o_ref[:, 384:512] = c
        o_ref[:, 512:640] = jnp.where(lane < hp, -s, 0.0)
        o_ref[:, 640:768] = jnp.where((lane >= hp) & (lane < ROPE_DIMS), s, 0.0)

    return _pc(
        body, name="rope_tables", grid=(S // tm,),
        in_specs=[pl.BlockSpec((tm, 1), lambda i: (i, 0)), pl.BlockSpec((2, LANES), lambda i: (0, 0))],
        out_specs=pl.BlockSpec((tm, 768), lambda i: (i, 0)),
        out_shape=jax.ShapeDtypeStruct((S, 768), F32),
        compiler_params=_params(("parallel",)),
    )(pos_col, _rope_inv())


def _tab(tab_ref, which):
    o = 384 * which
    return tab_ref[:, o:o + 128], tab_ref[:, o + 128:o + 256], tab_ref[:, o + 256:o + 384]


def _rms_fwd(x, g, name):
    S, Dm = x.shape
    tm = 512

    def body(x_ref, g_ref, h_ref):
        xv = x_ref[...]
        r = lax.rsqrt(jnp.mean(xv * xv, axis=-1, keepdims=True) + EPS)
        h_ref[...] = (xv * r * g_ref[...]).astype(BF16)

    return _pc(
        body, name=name, grid=(S // tm,),
        in_specs=[pl.BlockSpec((tm, Dm), lambda i: (i, 0)), pl.BlockSpec((1, Dm), lambda i: (0, 0))],
        out_specs=pl.BlockSpec((tm, Dm), lambda i: (i, 0)),
        out_shape=jax.ShapeDtypeStruct((S, Dm), BF16),
        compiler_params=_params(("parallel",)),
    )(x, g.reshape(1, Dm))


def _hn_fwd(x, gain, G):
    r = lax.rsqrt(_gmean(x * x, G) + EPS)
    return x * r * gain


def _hn_bwd(x, gain, dy, G):
    r = lax.rsqrt(_gmean(x * x, G) + EPS)
    t = dy * gain
    dx = r * t - x * (r * r * r) * _gmean(x * t, G)
    return dx, jnp.sum(dy * x * r, axis=0, keepdims=True)


def _fold_halves(v):
    return v + _roll(v, HEAD)


def _even_pre_fwd(proj, tab, qg, kg):
    S = proj.shape[0]
    tm = 512

    def body(p_ref, tab_ref, qg_ref, kg_ref, g_ref, rq_ref, rk_ref, rv_ref, dq_ref, dk_ref, dv_ref):
        Ar, Br, Cr = _tab(tab_ref, 0)
        Ap, Bp, Cp = _tab(tab_ref, 1)
        G = g_ref[...]
        for c in range(2):
            sl = slice(c * 128, (c + 1) * 128)
            rq_ref[:, sl] = _rope(p_ref[:, c * 128:(c + 1) * 128], Ar, Br, Cr, 32).astype(BF16)
            rk_ref[:, sl] = (_rope(p_ref[:, 256 + c * 128:256 + (c + 1) * 128], Ar, Br, Cr, 32) * 0.125).astype(BF16)
        rv_ref[...] = p_ref[:, 512:1024].astype(BF16)
        for c in range(4):
            sl = slice(c * 128, (c + 1) * 128)
            q = _hn_fwd(p_ref[:, 1536 + c * 128:1536 + (c + 1) * 128], qg_ref[...], G)
            dq_ref[:, sl] = _rope(q, Ap, Bp, Cp, 8).astype(BF16)
            k = _hn_fwd(p_ref[:, 2048 + c * 128:2048 + (c + 1) * 128], kg_ref[...], G)
            dk_ref[:, sl] = _rope(k, Ap, Bp, Cp, 8).astype(BF16)
        dv_ref[...] = p_ref[:, 2560:3072].astype(BF16)

    row = lambda w: pl.BlockSpec((tm, w), lambda i: (i, 0))
    vec = pl.BlockSpec((1, LANES), lambda i: (0, 0))
    return _pc(
        body, name="even_pre_fwd", grid=(S // tm,),
        in_specs=[row(3072), row(768), vec, vec, pl.BlockSpec((LANES, LANES), lambda i: (0, 0))],
        out_specs=[row(256), row(256), row(512), row(512), row(512), row(512)],
        out_shape=[jax.ShapeDtypeStruct((S, w), BF16) for w in (256, 256, 512, 512, 512, 512)],
        compiler_params=_params(("parallel",), VMEM_LIMIT_WIDE),
    )(proj, tab, qg, kg, _group_matrix())


def _even_pre_bwd(proj, tab, qg, kg, drq, drk, drv, drg, dqs, dks, dvs):
    S = proj.shape[0]
    tm = 512
    npat = len(dqs)

    def body(p_ref, tab_ref, qg_ref, kg_ref, g_ref, drq_ref, drk_ref, drv_ref, drg_ref, *rest):
        dq_refs, dk_refs, dv_refs = rest[:npat], rest[npat:2 * npat], rest[2 * npat:3 * npat]
        dp_ref, dqg_ref, dkg_ref = rest[3 * npat:]
        Ar, Br, Cr = _tab(tab_ref, 0)
        Ap, Bp, Cp = _tab(tab_ref, 1)
        G = g_ref[...]
        for c in range(2):
            sl = slice(c * 128, (c + 1) * 128)
            dp_ref[:, c * 128:(c + 1) * 128] = _rope_t(drq_ref[:, sl], Ar, Br, Cr, 32).astype(BF16)
            dp_ref[:, 256 + c * 128:256 + (c + 1) * 128] = _rope_t(drk_ref[:, sl] * 0.125, Ar, Br, Cr, 32).astype(BF16)
        dp_ref[:, 512:1024] = drv_ref[...].astype(BF16)
        dp_ref[:, 1024:1536] = drg_ref[...].astype(BF16)
        accq = jnp.zeros((1, LANES), F32)
        acck = jnp.zeros((1, LANES), F32)
        for c in range(4):
            sl = slice(c * 128, (c + 1) * 128)
            g = dq_refs[0][:, sl]
            for r in dq_refs[1:]:
                g = g + r[:, sl]
            dx, dg = _hn_bwd(p_ref[:, 1536 + c * 128:1536 + (c + 1) * 128], qg_ref[...], _rope_t(g, Ap, Bp, Cp, 8), G)
            dp_ref[:, 1536 + c * 128:1536 + (c + 1) * 128] = dx.astype(BF16)
            accq = accq + dg
            g = dk_refs[0][:, sl]
            for r in dk_refs[1:]:
                g = g + r[:, sl]
            dx, dg = _hn_bwd(p_ref[:, 2048 + c * 128:2048 + (c + 1) * 128], kg_ref[...], _rope_t(g, Ap, Bp, Cp, 8), G)
            dp_ref[:, 2048 + c * 128:2048 + (c + 1) * 128] = dx.astype(BF16)
            acck = acck + dg
        g = dv_refs[0][...]
        for r in dv_refs[1:]:
            g = g + r[...]
        dp_ref[:, 2560:3072] = g.astype(BF16)

        @pl.when(pl.program_id(0) == 0)
        def _():
            dqg_ref[...] = jnp.zeros_like(dqg_ref)
            dkg_ref[...] = jnp.zeros_like(dkg_ref)

        dqg_ref[...] += _fold_halves(accq)
        dkg_ref[...] += _fold_halves(acck)

    row = lambda w: pl.BlockSpec((tm, w), lambda i: (i, 0))
    vec = pl.BlockSpec((1, LANES), lambda i: (0, 0))
    return _pc(
        body, name="even_pre_bwd", grid=(S // tm,),
        in_specs=[row(3072), row(768), vec, vec, pl.BlockSpec((LANES, LANES), lambda i: (0, 0)),
                  row(256), row(256), row(512), row(512)] + [row(512)] * (3 * npat),
        out_specs=[row(3072), vec, vec],
        out_shape=[jax.ShapeDtypeStruct((S, 3072), BF16), jax.ShapeDtypeStruct((1, LANES), F32),
                   jax.ShapeDtypeStruct((1, LANES), F32)],
        compiler_params=_params(("arbitrary",), VMEM_LIMIT_WIDE),
    )(proj, tab, qg, kg, _group_matrix(), drq, drk, drv, drg, *dqs, *dks, *dvs)


def _ret_consts(pair, half):
    lg = jnp.where(pair == 0, _LOG_GAMMA[half], _LOG_GAMMA[2 + half]).astype(F32)
    i = lax.broadcasted_iota(jnp.int32, (BLK, BLK), 0)
    j = lax.broadcasted_iota(jnp.int32, (BLK, BLK), 1)
    diff = (i - j).astype(F32)
    decay = jnp.where(diff >= 0, jnp.exp(lg * jnp.maximum(diff, 0.0)), 0.0)
    t = lax.broadcasted_iota(jnp.int32, (BLK, 1), 0).astype(F32)
    xi = jnp.exp(lg * (t + 1.0))
    zeta = jnp.exp(lg * (BLK - 1.0 - t))
    cd = jnp.exp(jnp.full((1, 1), BLK, F32) * lg)
    return decay, xi, zeta, cd


RET_STEP = 8


def _ret_fwd(rq, rk, rv):
    S = rq.shape[0]
    nc = S // BLK
    rows = RET_STEP * BLK

    def body(q_ref, k_ref, v_ref, o_ref, st_ref, R):
        p, n = pl.program_id(0), pl.program_id(1)

        @pl.when(n == 0)
        def _():
            R[...] = jnp.zeros_like(R)

        consts = [_ret_consts(p, half) for half in range(2)]
        masks = [_head_mask((BLK, LANES), half) for half in range(2)]
        for ci in range(RET_STEP):
            rs = slice(ci * BLK, (ci + 1) * BLK)
            q2, k2 = q_ref[rs, :], k_ref[rs, :]
            for half in range(2):
                decay, xi, zeta, cd = consts[half]
                m = masks[half]
                qm = jnp.where(m, q2, jnp.zeros_like(q2))
                km = jnp.where(m, k2, jnp.zeros_like(k2))
                v = v_ref[rs, half * 128:(half + 1) * 128]
                Rb = R[half].astype(BF16)
                st_ref[ci, half] = Rb
                sc = lax.dot_general(qm, k2, (((1,), (1,)), ((), ())), preferred_element_type=F32) * decay
                o = jnp.dot(sc.astype(BF16), v, preferred_element_type=F32)
                o = o + jnp.dot(qm, Rb, preferred_element_type=F32) * xi
                o_ref[rs, half * 128:(half + 1) * 128] = o
                kz = (km.astype(F32) * zeta).astype(BF16)
                R[half] = R[half] * cd + lax.dot_general(kz, v, (((0,), (0,)), ((), ())), preferred_element_type=F32)

    return _pc(
        body, name="ret_fwd", grid=(2, nc // RET_STEP),
        in_specs=[pl.BlockSpec((rows, 128), lambda p, n: (n, p)), pl.BlockSpec((rows, 128), lambda p, n: (n, p)),
                  pl.BlockSpec((rows, 256), lambda p, n: (n, p))],
        out_specs=[pl.BlockSpec((rows, 256), lambda p, n: (n, p)),
                   pl.BlockSpec((None, RET_STEP, 2, 128, 128), lambda p, n: (p, n, 0, 0, 0))],
        out_shape=[jax.ShapeDtypeStruct((S, 512), F32), jax.ShapeDtypeStruct((2, nc, 2, 128, 128), BF16)],
        scratch_shapes=[pltpu.VMEM((2, 128, 128), F32)],
        compiler_params=_params(("parallel", "arbitrary")),
    )(rq, rk, rv)


def _ret_bwd(rq, rk, rv, states, do):
    S = rq.shape[0]
    nc = S // BLK
    ns = nc // RET_STEP
    rows = RET_STEP * BLK
    nt = (((1,), (1,)), ((), ()))
    tn = (((0,), (0,)), ((), ()))

    def body(q_ref, k_ref, v_ref, st_ref, do_ref, dq_ref, dk_ref, dv_ref, U):
        p, n = pl.program_id(0), pl.program_id(1)

        @pl.when(n == 0)
        def _():
            U[...] = jnp.zeros_like(U)

        consts = [_ret_consts(p, half) for half in range(2)]
        masks = [_head_mask((BLK, LANES), half) for half in range(2)]
        for ci in reversed(range(RET_STEP)):
            rs = slice(ci * BLK, (ci + 1) * BLK)
            q2, k2 = q_ref[rs, :], k_ref[rs, :]
            dq_acc = jnp.zeros((BLK, LANES), F32)
            dk_acc = jnp.zeros((BLK, LANES), F32)
            for half in range(2):
                decay, xi, zeta, cd = consts[half]
                m = masks[half]
                qm = jnp.where(m, q2, jnp.zeros_like(q2))
                km = jnp.where(m, k2, jnp.zeros_like(k2))
                v = v_ref[rs, half * 128:(half + 1) * 128]
                dob = do_ref[rs, half * 128:(half + 1) * 128].astype(BF16)
                Rb = st_ref[ci, half]
                Ub = U[half].astype(BF16)
                dsc = (lax.dot_general(dob, v, nt, preferred_element_type=F32) * decay).astype(BF16)
                xdo = (dob.astype(F32) * xi).astype(BF16)
                dq_acc += jnp.dot(dsc, km, preferred_element_type=F32) + lax.dot_general(xdo, Rb, nt, preferred_element_type=F32)
                dk_acc += lax.dot_general(dsc, qm, tn, preferred_element_type=F32) \
                    + lax.dot_general(v, Ub, nt, preferred_element_type=F32) * zeta
                sc = (lax.dot_general(qm, k2, nt, preferred_element_type=F32) * decay).astype(BF16)
                kz = (km.astype(F32) * zeta).astype(BF16)
                dv_ref[rs, half * 128:(half + 1) * 128] = lax.dot_general(sc, dob, tn, preferred_element_type=F32) \
                    + jnp.dot(kz, Ub, preferred_element_type=F32)
                U[half] = U[half] * cd + lax.dot_general(qm, xdo, tn, preferred_element_type=F32)
            dq_ref[rs, :] = dq_acc
            dk_ref[rs, :] = dk_acc

    rev = lambda w: pl.BlockSpec((rows, w), lambda p, n: (ns - 1 - n, p))
    return _pc(
        body, name="ret_bwd", grid=(2, ns),
        in_specs=[rev(128), rev(128), rev(256),
                  pl.BlockSpec((None, RET_STEP, 2, 128, 128), lambda p, n: (p, ns - 1 - n, 0, 0, 0)), rev(256)],
        out_specs=[rev(128), rev(128), rev(256)],
        out_shape=[jax.ShapeDtypeStruct((S, 256), F32), jax.ShapeDtypeStruct((S, 256), F32),
                   jax.ShapeDtypeStruct((S, 512), F32)],
        scratch_shapes=[pltpu.VMEM((2, 128, 128), F32)],
        compiler_params=_params(("parallel", "arbitrary")),
    )(rq, rk, rv, states, do)


ATT_TILE = 2048


def _rows(ref, start, n, r):
    if r == 1:
        return ref[pl.ds(start, n), :]
    return ref[pl.ds(start, n, stride=r), :]


def _twice(x):
    return jnp.concatenate([x, x], axis=0)


def _stack_heads(x, masks):
    zero = jnp.zeros_like(x)
    return jnp.concatenate([jnp.where(masks[0], x, zero), jnp.where(masks[1], x, zero)], axis=0)


def _set_rows(ref, start, n, r, val):
    if r == 1:
        ref[pl.ds(start, n), :] = val
    else:
        ref[pl.ds(start, n, stride=r), :] = val


def _band_geometry(S, patterns):
    rmax = max(r for _, r in patterns)
    H = BLK * rmax
    T = min(S, ATT_TILE)
    assert T % H == 0 and S % T == 0
    return H, T, S // T, T // BLK


def _band_fwd(q, k, v, *, patterns, nq, name, sinks=None, want_bf16=False):
    S, Ck = k.shape
    H, T, nt, nbt = _band_geometry(S, patterns)
    ncol = Ck // LANES
    scale = HEAD ** -0.5
    has_sink = sinks is not None
    nt_dims = (((1,), (1,)), ((), ()))

    def body(*refs):
        q_ref, kp_ref, kc_ref, vp_ref, vc_ref = refs[:5]
        sk_ref = refs[5] if has_sink else None
        n_out = 3 if want_bf16 else 2
        outs = refs[5 + has_sink:5 + has_sink + n_out]
        qf, kf, vf, M, L, A = refs[5 + has_sink + n_out:]
        t = pl.program_id(1)
        kf[0:H, :] = kp_ref[...].astype(F32)
        kf[H:H + T, :] = kc_ref[...].astype(F32)
        vf[0:H, :] = vp_ref[...].astype(F32)
        vf[H:H + T, :] = vc_ref[...].astype(F32)
        r_i = lax.broadcasted_iota(jnp.int32, (BLK, 2 * BLK), 0)
        c_i = lax.broadcasted_iota(jnp.int32, (BLK, 2 * BLK), 1)
        dist_i = r_i + BLK - c_i
        masks = [_head_mask((BLK, LANES), h) for h in range(2)]

        for i in range(nq):
            qf[...] = q_ref[:, i * 128:(i + 1) * 128].astype(F32) * scale
            for p, (dist, r) in enumerate(patterns):
                in_band = (dist_i >= 0) & (dist_i <= dist)
                in_band_first = in_band & ((c_i >= BLK) | (t > 0))
                in_band, in_band_first = _twice(in_band), _twice(in_band_first)

                def unit(j, b, p=p, r=r, in_band=in_band, in_band_first=in_band_first):
                    q0 = j + b * (BLK * r)
                    q2 = _rows(qf, q0, BLK, r).astype(BF16)
                    kcat = _rows(kf, H + q0 - BLK * r, 2 * BLK, r).astype(BF16)
                    vcat = _rows(vf, H + q0 - BLK * r, 2 * BLK, r).astype(BF16)
                    valid = in_band if b > 0 else in_band_first
                    s = lax.dot_general(_stack_heads(q2, masks), kcat, nt_dims, preferred_element_type=F32)
                    s = jnp.where(valid, s, -jnp.inf)
                    mx = jnp.max(s, axis=1, keepdims=True)
                    pr = jnp.exp(s - mx)
                    den = jnp.sum(pr, axis=1, keepdims=True)
                    pv = jnp.dot(pr.astype(BF16), vcat, preferred_element_type=F32)
                    m2 = jnp.where(masks[0], mx[:BLK], mx[BLK:])
                    l2 = jnp.where(masks[0], den[:BLK], den[BLK:])
                    a2 = jnp.where(masks[0], pv[:BLK], pv[BLK:])
                    if p > 0:
                        mo = _rows(M, q0, BLK, r)
                        mn = jnp.maximum(mo, m2)
                        wa, wb = jnp.exp(mo - mn), jnp.exp(m2 - mn)
                        l2 = wa * _rows(L, q0, BLK, r) + wb * l2
                        a2 = wa * _rows(A, q0, BLK, r) + wb * a2
                        m2 = mn
                    _set_rows(M, q0, BLK, r, m2)
                    _set_rows(L, q0, BLK, r, l2)
                    _set_rows(A, q0, BLK, r, a2)

                for u in range(nbt):
                    unit(u % r, u // r)
            sl = slice(i * 128, (i + 1) * 128)
            mm, ll, aa = M[...], L[...], A[...]
            if has_sink:
                snk = sk_ref[:, sl]
                mn = jnp.maximum(mm, snk)
                w = jnp.exp(mm - mn)
                ll = ll * w + jnp.exp(snk - mn)
                aa = aa * w
                mm = mn
            o = aa / ll
            outs[0][:, sl] = o
            outs[1][:, sl] = mm + jnp.log(ll)
            if want_bf16:
                outs[2][:, sl] = o.astype(BF16)

    th = T // H
    qspec = pl.BlockSpec((T, nq * 128), lambda j, t: (t, j))
    cur = pl.BlockSpec((T, 128), lambda j, t: (t, j))
    prev = pl.BlockSpec((H, 128), lambda j, t: (jnp.maximum(t * th - 1, 0), j))
    in_specs = [qspec, prev, cur, prev, cur]
    args = [q, k, k, v, v]
    if has_sink:
        in_specs.append(pl.BlockSpec((1, nq * 128), lambda j, t: (0, j)))
        args.append(sinks)
    out_dts = [F32, F32] + ([BF16] if want_bf16 else [])
    return _pc(
        body, name=name, grid=(ncol, nt), in_specs=in_specs,
        out_specs=[qspec] * len(out_dts),
        out_shape=[jax.ShapeDtypeStruct(q.shape, dt) for dt in out_dts],
        scratch_shapes=[pltpu.VMEM((T, LANES), F32), pltpu.VMEM((H + T, LANES), F32), pltpu.VMEM((H + T, LANES), F32),
                        pltpu.VMEM((T, LANES), F32), pltpu.VMEM((T, LANES), F32), pltpu.VMEM((T, LANES), F32)],
        compiler_params=_params(("parallel", "parallel")),
    )(*args)


def _band_bwd(q, k, v, lse, delta, do, *, patterns, nq, name, sinks=None, do_col0=0):
    S, Ck = k.shape
    H, T, nt, nbt = _band_geometry(S, patterns)
    ncol = Ck // LANES
    scale = HEAD ** -0.5
    has_sink = sinks is not None
    nt_dims = (((1,), (1,)), ((), ()))
    tn_dims = (((0,), (0,)), ((), ()))

    def body(*refs):
        (qc_ref, qn_ref, kp_ref, kc_ref, vp_ref, vc_ref, lc_ref, ln_ref, ec_ref, en_ref, dc_ref, dn_ref) = refs[:12]
        sk_ref = refs[12] if has_sink else None
        n_out = 4 if has_sink else 3
        outs = refs[12 + has_sink:12 + has_sink + n_out]
        dq_ref, dk_ref, dv_ref = outs[:3]
        qf, kf, vf, lf, ef, df = refs[12 + has_sink + n_out:]
        t = pl.program_id(1)
        kf[0:H, :] = kp_ref[...].astype(F32)
        kf[H:H + T, :] = kc_ref[...].astype(F32)
        vf[0:H, :] = vp_ref[...].astype(F32)
        vf[H:H + T, :] = vc_ref[...].astype(F32)
        dk_ref[...] = jnp.zeros_like(dk_ref)
        dv_ref[...] = jnp.zeros_like(dv_ref)
        r_i = lax.broadcasted_iota(jnp.int32, (BLK, 2 * BLK), 0)
        c_i = lax.broadcasted_iota(jnp.int32, (BLK, 2 * BLK), 1)
        dist_q = r_i + BLK - c_i
        dist_h = dist_q[:, :BLK]
        m1 = [_head_mask((BLK, LANES), h) for h in range(2)]

        def stacked_inputs(q2, do2, l2, e2):
            spread = lambda v: jnp.concatenate([jnp.where(m1[0], v, _roll(v, HEAD)), jnp.where(m1[1], v, _roll(v, HEAD))], axis=0)
            return _stack_heads(q2, m1), _stack_heads(do2.astype(BF16), m1), spread(l2), spread(e2)

        for i in range(nq):
            sl = slice(i * 128, (i + 1) * 128)
            qf[0:T, :] = qc_ref[:, sl].astype(F32) * scale
            qf[T:T + H, :] = qn_ref[:, sl].astype(F32) * scale
            for buf, c_ref, n_ref in ((lf, lc_ref, ln_ref), (ef, ec_ref, en_ref), (df, dc_ref, dn_ref)):
                buf[0:T, :] = c_ref[:, sl]
                buf[T:T + H, :] = n_ref[:, sl]
            if has_sink:
                @pl.when(t == 0)
                def _():
                    outs[3][:, sl] = jnp.zeros((1, LANES), F32)

                outs[3][:, sl] += jnp.sum(-jnp.exp(sk_ref[:, sl] - lc_ref[:, sl]) * ec_ref[:, sl], axis=0, keepdims=True)
            for p, (dist, r) in enumerate(patterns):
                band_q = (dist_q >= 0) & (dist_q <= dist)
                band_first = band_q & ((c_i >= BLK) | (t > 0))
                band_h = (dist_h >= 0) & (dist_h <= dist)
                band_q, band_first, band_h = _twice(band_q), _twice(band_first), _twice(band_h)

                def add_rows(ref, start, val, r=r):
                    _set_rows(ref, start, BLK, r, _rows(ref, start, BLK, r) + val)

                def unit(j, b, p=p, r=r, band_q=band_q, band_first=band_first):
                    q0 = j + b * (BLK * r)
                    q2 = _rows(qf, q0, BLK, r).astype(BF16)
                    do2, l2, e2 = _rows(df, q0, BLK, r), _rows(lf, q0, BLK, r), _rows(ef, q0, BLK, r)
                    kcat = _rows(kf, H + q0 - BLK * r, 2 * BLK, r).astype(BF16)
                    vcat = _rows(vf, H + q0 - BLK * r, 2 * BLK, r).astype(BF16)
                    valid = band_q if b > 0 else band_first
                    qs, dos, ls, es = stacked_inputs(q2, do2, l2, e2)
                    s = lax.dot_general(qs, kcat, nt_dims, preferred_element_type=F32)
                    pr = jnp.where(valid, jnp.exp(s - jnp.concatenate([ls, ls], axis=1)), 0.0)
                    dp = lax.dot_general(dos, vcat, nt_dims, preferred_element_type=F32)
                    ds = (pr * (dp - jnp.concatenate([es, es], axis=1))).astype(BF16)
                    dqs = jnp.dot(ds, kcat, preferred_element_type=F32) * scale
                    dq2 = jnp.where(m1[0], dqs[:BLK], dqs[BLK:])
                    dvc = lax.dot_general(pr.astype(BF16), dos, tn_dims, preferred_element_type=F32)
                    dkc = lax.dot_general(ds, qs, tn_dims, preferred_element_type=F32)
                    if p > 0:
                        dq2 = dq2 + _rows(dq_ref.at[:, sl], q0, BLK, r)
                    _set_rows(dq_ref.at[:, sl], q0, BLK, r, dq2)
                    add_rows(dk_ref, q0, dkc[BLK:])
                    add_rows(dv_ref, q0, dvc[BLK:])
                    if b > 0:
                        add_rows(dk_ref, q0 - BLK * r, dkc[:BLK])
                        add_rows(dv_ref, q0 - BLK * r, dvc[:BLK])

                def halo_unit(j, r=r, band_h=band_h):
                    k0 = j + (nbt // r - 1) * (BLK * r)
                    q2 = _rows(qf, T + j, BLK, r).astype(BF16)
                    do2, l2, e2 = _rows(df, T + j, BLK, r), _rows(lf, T + j, BLK, r), _rows(ef, T + j, BLK, r)
                    kc = _rows(kf, H + k0, BLK, r).astype(BF16)
                    vc = _rows(vf, H + k0, BLK, r).astype(BF16)
                    qs, dos, ls, es = stacked_inputs(q2, do2, l2, e2)
                    s = lax.dot_general(qs, kc, nt_dims, preferred_element_type=F32)
                    pr = jnp.where(band_h, jnp.exp(s - ls), 0.0)
                    dp = lax.dot_general(dos, vc, nt_dims, preferred_element_type=F32)
                    ds = (pr * (dp - es)).astype(BF16)
                    add_rows(dk_ref, k0, lax.dot_general(ds, qs, tn_dims, preferred_element_type=F32))
                    add_rows(dv_ref, k0, lax.dot_general(pr.astype(BF16), dos, tn_dims, preferred_element_type=F32))

                for u in range(nbt):
                    unit(u % r, u // r)
                if nt > 1:
                    @pl.when(t < nt - 1)
                    def _(r=r, halo_unit=halo_unit):
                        for j in range(r):
                            halo_unit(j)

    th = T // H
    last = S // H - 1
    qcur = pl.BlockSpec((T, nq * 128), lambda j, t: (t, j))
    qnext = pl.BlockSpec((H, nq * 128), lambda j, t: (jnp.minimum((t + 1) * th, last), j))
    cur = pl.BlockSpec((T, 128), lambda j, t: (t, j))
    prev = pl.BlockSpec((H, 128), lambda j, t: (jnp.maximum(t * th - 1, 0), j))
    dcur = pl.BlockSpec((T, nq * 128), lambda j, t: (t, j + do_col0))
    dnext = pl.BlockSpec((H, nq * 128), lambda j, t: (jnp.minimum((t + 1) * th, last), j + do_col0))
    in_specs = [qcur, qnext, prev, cur, prev, cur, qcur, qnext, qcur, qnext, dcur, dnext]
    args = [q, q, k, k, v, v, lse, lse, delta, delta, do, do]
    out_specs = [qcur, cur, cur]
    out_shape = [jax.ShapeDtypeStruct(q.shape, F32), jax.ShapeDtypeStruct(k.shape, F32), jax.ShapeDtypeStruct(k.shape, F32)]
    if has_sink:
        vec = pl.BlockSpec((1, nq * 128), lambda j, t: (0, j))
        in_specs.append(vec)
        args.append(sinks)
        out_specs.append(vec)
        out_shape.append(jax.ShapeDtypeStruct((1, q.shape[1]), F32))
    big = pltpu.VMEM((T + H, LANES), F32)
    return _pc(
        body, name=name, grid=(ncol, nt), in_specs=in_specs, out_specs=out_specs, out_shape=out_shape,
        scratch_shapes=[big] * 6,
        compiler_params=_params(("parallel", "arbitrary")),
    )(*args)


def _even_post_fwd(ro, proj, gn, da):
    S = ro.shape[0]
    tm = 512

    def body(ro_ref, rg_ref, gn_ref, da_ref, mix_ref):
        for c in range(4):
            sl = slice(c * 128, (c + 1) * 128)
            x = ro_ref[:, sl]
            mu = jnp.mean(x, axis=1, keepdims=True)
            xc = x - mu
            var = jnp.mean(xc * xc, axis=1, keepdims=True)
            y = xc * lax.rsqrt(var + EPS) * gn_ref[:, sl]
            z = rg_ref[:, sl]
            mix_ref[:, sl] = (z * jax.nn.sigmoid(z) * y).astype(BF16)
        mix_ref[:, 512:1024] = da_ref[...].astype(BF16)

    row = lambda w: pl.BlockSpec((tm, w), lambda i: (i, 0))
    return _pc(
        body, name="even_post_fwd", grid=(S // tm,),
        in_specs=[row(512), pl.BlockSpec((tm, 512), lambda i: (i, 2)), pl.BlockSpec((1, 512), lambda i: (0, 0)), row(512)],
        out_specs=row(1024), out_shape=jax.ShapeDtypeStruct((S, 1024), BF16),
        compiler_params=_params(("parallel",), VMEM_LIMIT_WIDE),
    )(ro, proj, gn, da)


def _even_post_bwd(ro, proj, gn, dmixed):
    S = ro.shape[0]
    tm = 512

    def body(ro_ref, rg_ref, gn_ref, dm_ref, dro_ref, drg_ref, dgn_ref):
        @pl.when(pl.program_id(0) == 0)
        def _():
            dgn_ref[...] = jnp.zeros_like(dgn_ref)

        for c in range(4):
            sl = slice(c * 128, (c + 1) * 128)
            x = ro_ref[:, sl]
            mu = jnp.mean(x, axis=1, keepdims=True)
            xc = x - mu
            rstd = lax.rsqrt(jnp.mean(xc * xc, axis=1, keepdims=True) + EPS)
            xh = xc * rstd
            gain = gn_ref[:, sl]
            y = xh * gain
            z = rg_ref[:, sl]
            sg = jax.nn.sigmoid(z)
            dra = dm_ref[:, sl]
            drg_ref[:, sl] = dra * y * sg * (1.0 + z * (1.0 - sg))
            dy = dra * z * sg
            dgn_ref[:, sl] += jnp.sum(dy * xh, axis=0, keepdims=True)
            dxh = dy * gain
            dro_ref[:, sl] = rstd * (dxh - jnp.mean(dxh, axis=1, keepdims=True)
                                     - xh * jnp.mean(dxh * xh, axis=1, keepdims=True))

    row = lambda w: pl.BlockSpec((tm, w), lambda i: (i, 0))
    vec = pl.BlockSpec((1, 512), lambda i: (0, 0))
    return _pc(
        body, name="even_post_bwd", grid=(S // tm,),
        in_specs=[row(512), pl.BlockSpec((tm, 512), lambda i: (i, 2)), vec, row(512)],
        out_specs=[row(512), row(512), vec],
        out_shape=[jax.ShapeDtypeStruct((S, 512), F32), jax.ShapeDtypeStruct((S, 512), F32),
                   jax.ShapeDtypeStruct((1, 512), F32)],
        compiler_params=_params(("arbitrary",), VMEM_LIMIT_WIDE),
    )(ro, proj, gn, dmixed)


def _swa_pre_fwd(proj, tab, qg, kg):
    S = proj.shape[0]
    tm = 512

    def body(p_ref, tab_ref, qg_ref, kg_ref, g_ref, q_ref, k_ref, v_ref):
        Ap, Bp, Cp = _tab(tab_ref, 1)
        G = g_ref[...]
        lo = _head_mask((tm, LANES), 0)
        for c in range(8):
            sl = slice(c * 128, (c + 1) * 128)
            q_ref[:, sl] = _rope(_hn_fwd(p_ref[:, sl], qg_ref[...], G), Ap, Bp, Cp, 8).astype(BF16)
        for c in range(2):
            kn = _rope(_hn_fwd(p_ref[:, 1024 + c * 128:1024 + (c + 1) * 128], kg_ref[...], G), Ap, Bp, Cp, 8)
            vv = p_ref[:, 1280 + c * 128:1280 + (c + 1) * 128]
            for t, ref in ((kn, k_ref), (vv, v_ref)):
                sw = _roll(t, HEAD)
                ref[:, (2 * c) * 128:(2 * c + 1) * 128] = jnp.where(lo, t, sw).astype(BF16)
                ref[:, (2 * c + 1) * 128:(2 * c + 2) * 128] = jnp.where(lo, sw, t).astype(BF16)

    row = lambda w: pl.BlockSpec((tm, w), lambda i: (i, 0))
    vec = pl.BlockSpec((1, LANES), lambda i: (0, 0))
    return _pc(
        body, name="swa_pre_fwd", grid=(S // tm,),
        in_specs=[row(1536), row(768), vec, vec, pl.BlockSpec((LANES, LANES), lambda i: (0, 0))],
        out_specs=[row(1024), row(512), row(512)],
        out_shape=[jax.ShapeDtypeStruct((S, w), BF16) for w in (1024, 512, 512)],
        compiler_params=_params(("parallel",), VMEM_LIMIT_WIDE),
    )(proj, tab, qg, kg, _group_matrix())


def _swa_pre_bwd(proj, tab, qg, kg, dq, dk, dv):
    S = proj.shape[0]
    tm = 512

    def body(p_ref, tab_ref, qg_ref, kg_ref, g_ref, dq_ref, dk_ref, dv_ref, dp_ref, db_ref, dqg_ref, dkg_ref):
        Ap, Bp, Cp = _tab(tab_ref, 1)
        G = g_ref[...]
        lo = _head_mask((tm, LANES), 0)

        @pl.when(pl.program_id(0) == 0)
        def _():
            db_ref[...] = jnp.zeros_like(db_ref)
            dqg_ref[...] = jnp.zeros_like(dqg_ref)
            dkg_ref[...] = jnp.zeros_like(dkg_ref)

        accq = jnp.zeros((1, LANES), F32)
        acck = jnp.zeros((1, LANES), F32)
        for c in range(8):
            sl = slice(c * 128, (c + 1) * 128)
            dx, dg = _hn_bwd(p_ref[:, sl], qg_ref[...], _rope_t(dq_ref[:, sl], Ap, Bp, Cp, 8), G)
            dp_ref[:, sl] = dx.astype(BF16)
            db_ref[:, sl] += jnp.sum(dx, axis=0, keepdims=True)
            accq = accq + dg
        for c in range(2):
            folded = []
            for ref in (dk_ref, dv_ref):
                a = ref[:, (2 * c) * 128:(2 * c + 1) * 128]
                b = ref[:, (2 * c + 1) * 128:(2 * c + 2) * 128]
                folded.append(jnp.where(lo, a + _roll(a, HEAD), b + _roll(b, HEAD)))
            ks = slice(1024 + c * 128, 1024 + (c + 1) * 128)
            dx, dg = _hn_bwd(p_ref[:, ks], kg_ref[...], _rope_t(folded[0], Ap, Bp, Cp, 8), G)
            dp_ref[:, ks] = dx.astype(BF16)
            db_ref[:, ks] += jnp.sum(dx, axis=0, keepdims=True)
            acck = acck + dg
            vs = slice(1280 + c * 128, 1280 + (c + 1) * 128)
            dp_ref[:, vs] = folded[1].astype(BF16)
            db_ref[:, vs] += jnp.sum(folded[1], axis=0, keepdims=True)
        dqg_ref[...] += _fold_halves(accq)
        dkg_ref[...] += _fold_halves(acck)

    row = lambda w: pl.BlockSpec((tm, w), lambda i: (i, 0))
    vec = pl.BlockSpec((1, LANES), lambda i: (0, 0))
    return _pc(
        body, name="swa_pre_bwd", grid=(S // tm,),
        in_specs=[row(1536), row(768), vec, vec, pl.BlockSpec((LANES, LANES), lambda i: (0, 0)),
                  row(1024), row(512), row(512)],
        out_specs=[row(1536), pl.BlockSpec((1, 1536), lambda i: (0, 0)), vec, vec],
        out_shape=[jax.ShapeDtypeStruct((S, 1536), BF16), jax.ShapeDtypeStruct((1, 1536), F32),
                   jax.ShapeDtypeStruct((1, LANES), F32), jax.ShapeDtypeStruct((1, LANES), F32)],
        compiler_params=_params(("arbitrary",), VMEM_LIMIT_WIDE),
    )(proj, tab, qg, kg, _group_matrix(), dq, dk, dv)


def _relu2_of(u):
    r = jnp.maximum(u.astype(F32), 0.0)
    return r * r


def _drelu2(acc, u):
    return (acc * 2.0 * jnp.maximum(u.astype(F32), 0.0),)


def _add(acc, res):
    return (acc + res,)


def _add_norm_in(res, g):
    def epilogue(acc, r, gv):
        xn = acc + r
        return xn, xn * lax.rsqrt(jnp.mean(xn * xn, axis=-1, keepdims=True) + EPS) * gv

    return dict(outs=[F32, BF16], epilogue=epilogue, extras=[(res, "mn"), (g.reshape(1, D_MODEL), "n")])


_T = dict(tm=1024, tn=1024, tk=1024)


def _rms_bwd_in(x, g, dres):
    def epilogue(dh, xv, gv, dr):
        r = lax.rsqrt(jnp.mean(xv * xv, axis=-1, keepdims=True) + EPS)
        t = dh * gv
        dx = dr + r * t - xv * (r * r * r) * jnp.mean(xv * t, axis=-1, keepdims=True)
        return dx, dx, jnp.sum(dh * xv * r, axis=0, keepdims=True)

    return dict(outs=[F32, BF16, ("colsum",)], epilogue=epilogue,
                extras=[(x, "mn"), (g.reshape(1, D_MODEL), "n"), (dres, "mn")])


def _delta_in(o, col0):
    width = D_MODEL - col0

    def epilogue(do, ov, G):
        parts = [_gmean(do[:, col0 + c * 128:col0 + (c + 1) * 128] * ov[:, c * 128:(c + 1) * 128], G) * float(HEAD)
                 for c in range(width // LANES)]
        return do, jnp.concatenate(parts, axis=1)

    return dict(outs=[F32, (F32, width)], epilogue=epilogue, extras=[(o, width), (_group_matrix(), "full")])


def _loss_in(res, target):
    def epilogue(acc, r, t):
        e = acc + r - t
        dy = e * (1.0 / D_MODEL)
        return dy, dy, jnp.sum(e * e, axis=0, keepdims=True) * (0.5 / D_MODEL)

    return dict(outs=[F32, BF16, ("colsum",)], epilogue=epilogue, extras=[(res, "mn"), (target, "mn")])


def _mlp_fwd(h, wts, layer, tag, tail):
    u = _matmul(h, wts, dims="nn", **_T, outs=[BF16], b_cs=True, b_row0=layer, name=f"mlp_up{tag}")
    out = _matmul(u, wts, dims="nn", **_T, a_pro=_relu2_of, b_rs=1024, b_row0=2 + layer, name=f"mlp_down{tag}", **tail)
    return out, (h, u)


def _mlp_bwd(x, g, wts, layer, saved, dy, dyb, tag):
    h, u = saved
    du = _matmul(dyb, wts, dims="nt", **_T, outs=[BF16], epilogue=_drelu2, extras=[(u, "mn")], b_rs=1024,
                 b_row0=2 + layer, name=f"mlp_du{tag}")
    dw_dn = _matmul(u, dyb, dims="tn", **_T, outs=[F32], a_pro=_relu2_of, name=f"mlp_dwdown{tag}")
    dw_up = _matmul(h, du, dims="tn", **_T, outs=[F32], o_cs=N_CHIPS, name=f"mlp_dwup{tag}")
    dx, dxb, dg = _matmul(du, wts, dims="nt", **_T, b_cs=True, b_row0=layer, b_rows=1024, name=f"mlp_dh{tag}",
                          vmem=VMEM_LIMIT_WIDE, **_rms_bwd_in(x, g, dy))
    return dx, dxb, dg, dw_up, dw_dn


def _local_step(x, pos_col, target, first_of, rest_begin, rest_of, P, red):
    S = x.shape[0]
    tab = _tables(pos_col)
    tile2 = lambda g: jnp.tile(g.reshape(1, HEAD), (1, 2))
    dqg, dkg = tile2(P["dil_q_gain"]), tile2(P["dil_k_gain"])
    sqg, skg = tile2(P["swa_q_gain"]), tile2(P["swa_k_gain"])
    gn = P["ret_gn_gain"].reshape(1, 512)
    sink_b = jnp.repeat(P["swa_sinks"].reshape(16), HEAD).reshape(1, 1024)

    h0 = _rms_fwd(x, P["norm_mix"][0], "rms_mix_fwd0")
    W = first_of((h0, tab))
    proj = _matmul(h0, W["hyb_w_in"], dims="nn", tm=1024, tn=768, tk=1024, outs=[F32], b_cs=True, name="hyb_in")
    rq, rk, rv, dq, dk, dv = _even_pre_fwd(proj, tab, dqg, dkg)
    ro, states = _ret_fwd(rq, rk, rv)
    dil = [(w // r, r) for w, r in DIL_PATTERNS]
    da, dlse = _band_fwd(dq, dk, dv, patterns=dil, nq=1, name="dil_fwd")
    mixed = rest_begin(_even_post_fwd(ro, proj, gn, da))
    x1, h1 = _matmul(mixed, W["hyb_w_out"], dims="nn", **_T, name="hyb_out", **_add_norm_in(x, P["norm_mlp"][0]))
    rest, bias = rest_of(x1)
    W = {**W, **rest}
    (x2, h2), mlp0 = _mlp_fwd(h1, W["packed"], 0, "0", _add_norm_in(x1, P["norm_mix"][1]))

    proj2 = _matmul(h2, W["swa_w_qkv"], dims="nn", tm=1024, tn=384, tk=1024, outs=[F32], b_cs=True,
                    epilogue=_add, extras=[(bias.reshape(1, 1536), "n")], name="swa_qkv")
    sq, sk, sv = _swa_pre_fwd(proj2, tab, sqg, skg)
    swa = [(SWA_DIST, 1)]
    so, slse, so_b = _band_fwd(sq, sk, sv, patterns=swa, nq=2, name="swa_fwd", sinks=sink_b, want_bf16=True)
    x3, h3 = _matmul(so_b, W["swa_w_out"], dims="nn", **_T, name="swa_out", **_add_norm_in(x2, P["norm_mlp"][1]))
    (dy, dyb, loss_cols), mlp1 = _mlp_fwd(h3, W["packed"], 1, "1", _loss_in(x3, target))
    loss = jnp.broadcast_to(jnp.sum(loss_cols), (1, LANES))

    gw, gp = {}, {}
    dx3, dx3b, dg_mlp1, gw["mlp_w_up1"], gw["mlp_w_down1"] = _mlp_bwd(x3, P["norm_mlp"][1], W["packed"], 1, mlp1, dy, dyb, "1")
    dx3b = red.begin("mlp1", {n: (gw[n], 1024) for n in ("mlp_w_up1", "mlp_w_down1")}, dx3b)
    gw["swa_w_out"] = _matmul(so_b, dx3b, dims="tn", **_T, outs=[F32], name="swa_dwout")
    dso, sdelta = _matmul(dx3b, W["swa_w_out"], dims="nt", **_T, name="swa_do", vmem=VMEM_LIMIT_WIDE, **_delta_in(so, 0))
    dsq, dsk, dsv, dsink = _band_bwd(sq, sk, sv, slse, sdelta, dso, patterns=swa, nq=2, name="swa_bwd", sinks=sink_b)
    dproj2, gp["swa_b_qkv"], gp["swa_q_gain"], gp["swa_k_gain"] = _swa_pre_bwd(proj2, tab, sqg, skg, dsq, dsk, dsv)
    gp["swa_sinks"] = dsink
    gw["swa_w_qkv"] = _matmul(h2, dproj2, dims="tn", tm=1024, tn=384, tk=1024, outs=[F32], o_cs=N_CHIPS, name="swa_dwqkv")
    dx2, dx2b, dg_mix1 = _matmul(dproj2, W["swa_w_qkv"], dims="nt", tm=1024, tn=1024, tk=384, b_cs=True, name="swa_dh",
                                 vmem=VMEM_LIMIT_WIDE, **_rms_bwd_in(x2, P["norm_mix"][1], dx3))
    dx2b = red.advance("mlp1", dx2b, dx2b)

    dx1, dx1b, dg_mlp0, gw["mlp_w_up0"], gw["mlp_w_down0"] = _mlp_bwd(x1, P["norm_mlp"][0], W["packed"], 0, mlp0, dx2, dx2b, "0")
    gw["hyb_w_out"] = _matmul(mixed, dx1b, dims="tn", **_T, outs=[F32], name="hyb_dwout")
    dx1b = red.begin("mlp0", {"mlp_w_up0": (gw["mlp_w_up0"], 1024), "mlp_w_down0": (gw["mlp_w_down0"], 1024),
                              "hyb_w_out": (gw["hyb_w_out"], 256), "swa_w_qkv": (gw["swa_w_qkv"], 1024),
                              "swa_w_out": (gw["swa_w_out"], 256)}, dx1b)
    red.finish("mlp1", dx1b)
    dmixed, ddelta = _matmul(dx1b, W["hyb_w_out"], dims="nt", **_T, name="hyb_dmixed", vmem=VMEM_LIMIT_WIDE,
                             **_delta_in(da, 512))
    dro, drg, gp["ret_gn_gain"] = _even_post_bwd(ro, proj, gn, dmixed)
    drq, drk, drv = _ret_bwd(rq, rk, rv, states, dro)
    ddq, ddk, ddv = _band_bwd(dq, dk, dv, dlse, ddelta, dmixed, patterns=dil, nq=1, name="dil_bwd", do_col0=4)
    ddq = red.advance("mlp0", ddq, ddq)
    dproj, gp["dil_q_gain"], gp["dil_k_gain"] = _even_pre_bwd(proj, tab, dqg, dkg, drq, drk, drv, drg, [ddq], [ddk], [ddv])
    gw["hyb_w_in"] = _matmul(h0, dproj, dims="tn", tm=1024, tn=768, tk=1024, outs=[F32], o_cs=N_CHIPS, name="hyb_dwin")
    dproj = red.begin("win", {"hyb_w_in": (gw["hyb_w_in"], 1024)}, dproj)
    grad_x, _, dg_mix0 = _matmul(dproj, W["hyb_w_in"], dims="nt", tm=1024, tn=1024, tk=768, b_cs=True, name="hyb_dh",
                                 vmem=VMEM_LIMIT_WIDE, **_rms_bwd_in(x, P["norm_mix"][0], dx1))
    red.finish("mlp0", grad_x)
    gp["norm_mix"] = jnp.concatenate([dg_mix0, dg_mix1], axis=0)
    gp["norm_mlp"] = jnp.concatenate([dg_mlp0, dg_mlp1], axis=0)
    return loss, grad_x, gp


HBM = pl.BlockSpec(memory_space=pltpu.HBM)


def _place():
    x, y, c = lax.axis_index("x"), lax.axis_index("y"), lax.axis_index("c")
    chips = [(1 - x, y), (x, 1 - y), (1 - x, 1 - y)]
    return x, y, c, chips


SEM = pl.BlockSpec(memory_space=pltpu.SEMAPHORE)
EFFECT = pltpu.SideEffectType.DATAFLOW_SIDE_EFFECTING


def _half_block(ref, chip, core):
    rh = ref.shape[1] // 2
    return ref.at[2 * chip[0] + chip[1], pl.ds(core * rh, rh), :]


def _gather_start(buf, ride, name):
    def body(b_ref, ride_ref, s0, s1, s2, r0, r1, r2, b_out, ride_out):
        x, y, c, chips = _place()
        for chip, s, r in zip(chips, (s0, s1, s2), (r0, r1, r2)):
            mine = _half_block(b_ref, (x, y), c)
            pltpu.make_async_remote_copy(src_ref=mine, dst_ref=mine, send_sem=s, recv_sem=r,
                                         device_id=(*chip, c), device_id_type=MESH).start()

    sem = pltpu.SemaphoreType.DMA(())
    return _pc(
        body, name=name,
        out_shape=(sem,) * 6 + (pltpu.HBM(buf.shape, buf.dtype), pltpu.HBM(ride.shape, ride.dtype)),
        in_specs=(HBM, HBM), out_specs=(SEM,) * 6 + (HBM, HBM), input_output_aliases={0: 6, 1: 7},
        compiler_params=pltpu.CompilerParams(has_side_effects=EFFECT),
    )(pltpu.with_memory_space_constraint(buf, pltpu.HBM), pltpu.with_memory_space_constraint(ride, pltpu.HBM))


def _gather_wait(buf, sems, afters, name):
    def body(b_ref, s0, s1, s2, r0, r1, r2, *unread):
        x, y, c, chips = _place()
        for chip, s, r in zip(chips, (s0, s1, s2), (r0, r1, r2)):
            cp = pltpu.make_async_remote_copy(src_ref=_half_block(b_ref, (x, y), c), dst_ref=_half_block(b_ref, chip, c),
                                              send_sem=s, recv_sem=r, device_id=(*chip, c), device_id_type=MESH)
            cp.wait_send()
            cp.wait_recv()

    return _pc(
        body, name=name, out_shape=pltpu.HBM(buf.shape, buf.dtype),
        in_specs=(HBM,) + (SEM,) * 6 + (pl.BlockSpec(memory_space=pl.ANY),) * len(afters), out_specs=HBM,
        input_output_aliases={0: 0}, compiler_params=pltpu.CompilerParams(has_side_effects=EFFECT),
    )(buf, *sems, *afters)


def _gather_handover(buf, name):
    def body(b_ref, out_ref, send_sems, recv_sems):
        x, y, c, chips = _place()
        cps = []
        for k, chip in enumerate(chips):
            mine = _half_block(b_ref, chip, c)
            cps.append(pltpu.make_async_remote_copy(src_ref=mine, dst_ref=mine, send_sem=send_sems.at[k],
                                                    recv_sem=recv_sems.at[k], device_id=(x, y, 1 - c), device_id_type=MESH))
        for cp in cps:
            cp.start()
        for k, chip in enumerate(chips):
            theirs = _half_block(b_ref, chip, 1 - c)
            pltpu.make_async_remote_copy(src_ref=theirs, dst_ref=theirs, send_sem=send_sems.at[k], recv_sem=recv_sems.at[k],
                                         device_id=(x, y, 1 - c), device_id_type=MESH).wait_recv()
        for cp in cps:
            cp.wait_send()

    return _pc(
        body, name=name, in_specs=[HBM], out_specs=HBM,
        out_shape=jax.ShapeDtypeStruct(buf.shape, buf.dtype), input_output_aliases={0: 0},
        scratch_shapes=[pltpu.SemaphoreType.DMA((3,)), pltpu.SemaphoreType.DMA((3,))],
    )(buf)


def _pair_sum(t, l, place, name):
    _, r, cols = t.shape
    rh = r // 2
    tr = min(rh, 256)
    nr = rh // tr

    def body(pl_ref, t_ref, l_ref, o_ref):
        o_ref[...] = (t_ref[...] + l_ref[...]).astype(BF16)

    other = lambda s, p: s + jnp.where(s >= p[0], 1, 0)
    return _pc(
        body, name=name,
        grid_spec=pltpu.PrefetchScalarGridSpec(
            num_scalar_prefetch=1, grid=(N_CHIPS - 1, nr),
            in_specs=[pl.BlockSpec((None, tr, cols), lambda s, i, p: (other(s, p), p[1] * nr + i, 0)),
                      pl.BlockSpec((None, tr, cols), lambda s, i, p: (other(s, p), i, 0))],
            out_specs=pl.BlockSpec((None, tr, cols), lambda s, i, p: (other(s, p), i, 0))),
        out_shape=jax.ShapeDtypeStruct((N_CHIPS, rh, cols), BF16),
        compiler_params=_params(("parallel", "parallel")),
    )(place, t, l)


def _final_sum(t, l, rcv, place, name, layer=0, layers=1, into=None):
    _, r, cols = t.shape
    rh = r // 2
    tr = min(rh, 256)
    nr = rh // tr

    def body(pl_ref, t_ref, l_ref, r_ref, *rest):
        acc = t_ref[...] + l_ref[...]
        for k in range(3):
            acc = acc + r_ref[k].astype(F32)
        rest[-1][...] = acc

    in_specs = [pl.BlockSpec((None, tr, cols), lambda i, p: (p[0], p[1] * nr + i, 0)),
                pl.BlockSpec((None, tr, cols), lambda i, p: (p[0], i, 0)),
                pl.BlockSpec((3, tr, cols), lambda i, p: (0, i, 0))]
    args = [place, t, l, rcv]
    aliases = {}
    if into is not None:
        in_specs.append(pl.BlockSpec(memory_space=pl.ANY))
        args.append(into)
        aliases = {4: 0}
    return _pc(
        body, name=name,
        grid_spec=pltpu.PrefetchScalarGridSpec(
            num_scalar_prefetch=1, grid=(nr,), in_specs=in_specs,
            out_specs=pl.BlockSpec((tr, cols), lambda i, p: (2 * nr * layer + p[1] * nr + i, 0))),
        out_shape=jax.ShapeDtypeStruct((layers * r, cols), F32), input_output_aliases=aliases,
        compiler_params=_params(("parallel",)),
    )(*args)


def _share_halves(hs, name):
    nt = len(hs)
    n = sum(layers for _, layers in hs)

    def body(*refs):
        h_refs, send_sems, recv_sems = refs[:nt], refs[-2], refs[-1]
        x, y, c, _ = _place()
        cps = []
        for k, (_, layers) in enumerate(hs):
            rh = h_refs[k].shape[0] // (2 * layers)
            for layer in range(layers):
                half = h_refs[k].at[pl.ds((2 * layer + c) * rh, rh), :]
                cps.append(pltpu.make_async_remote_copy(
                    src_ref=half, dst_ref=half, send_sem=send_sems.at[len(cps)], recv_sem=recv_sems.at[len(cps)],
                    device_id=(x, y, 1 - c), device_id_type=MESH))
        for cp in cps:
            cp.start()
        for cp in cps:
            cp.wait()

    return _pc(
        body, name=name, in_specs=[HBM] * nt, out_specs=[HBM] * nt,
        out_shape=[jax.ShapeDtypeStruct(h.shape, F32) for h, _ in hs],
        input_output_aliases={k: k for k in range(nt)},
        scratch_shapes=[pltpu.SemaphoreType.DMA((n,)), pltpu.SemaphoreType.DMA((n,))],
    )(*[h for h, _ in hs])


def _split_start(name, bufs, ride, n, copies_of):
    nb = len(bufs)

    def body(*refs):
        sems = refs[nb + 1:nb + 1 + 2 * n]
        for cp in copies_of(refs[:nb], sems[:n], sems[n:]):
            (cp[0] if isinstance(cp, tuple) else cp).start()

    outs = _pc(
        body, name=name,
        out_shape=(pltpu.SemaphoreType.DMA(()),) * (2 * n) + tuple(pltpu.HBM(b.shape, b.dtype) for b in bufs)
        + (pltpu.HBM(ride.shape, ride.dtype),),
        in_specs=(HBM,) * (nb + 1), out_specs=(SEM,) * (2 * n) + (HBM,) * (nb + 1),
        input_output_aliases={k: 2 * n + k for k in range(nb + 1)},
        compiler_params=pltpu.CompilerParams(has_side_effects=EFFECT),
    )(*[pltpu.with_memory_space_constraint(b, pltpu.HBM) for b in bufs], pltpu.with_memory_space_constraint(ride, pltpu.HBM))
    return list(outs[:2 * n]), list(outs[2 * n:2 * n + nb]), outs[-1]


def _split_wait(name, bufs, sems, after, n, copies_of):
    nb = len(bufs)

    def body(*refs):
        s = refs[nb:nb + 2 * n]
        for cp in copies_of(refs[:nb], s[:n], s[n:]):
            sent, landed = cp if isinstance(cp, tuple) else (cp, cp)
            sent.wait_send()
            landed.wait_recv()

    outs = _pc(
        body, name=name, out_shape=tuple(pltpu.HBM(b.shape, b.dtype) for b in bufs),
        in_specs=(HBM,) * nb + (SEM,) * (2 * n) + (pl.BlockSpec(memory_space=pl.ANY),), out_specs=(HBM,) * nb,
        input_output_aliases={k: k for k in range(nb)},
        compiler_params=pltpu.CompilerParams(has_side_effects=EFFECT),
    )(*bufs, *sems, after)
    return list(outs)


def _handover_copies(refs, send, recv):
    x, y, c, chips = _place()
    cps = []
    for k, chip in enumerate(chips):
        mine, theirs = _half_block(refs[0], chip, c), _half_block(refs[0], chip, 1 - c)
        desc = lambda blk: pltpu.make_async_remote_copy(src_ref=blk, dst_ref=blk, send_sem=send[k], recv_sem=recv[k],
                                                        device_id=(x, y, 1 - c), device_id_type=MESH)
        cps.append((desc(mine), desc(theirs)))
    return cps


def _share_copies(layers_of):
    def copies_of(refs, send, recv):
        x, y, c, _ = _place()
        cps = []
        for k, layers in enumerate(layers_of):
            rh = refs[k].shape[0] // (2 * layers)
            for layer in range(layers):
                i = len(cps)
                desc = lambda half: pltpu.make_async_remote_copy(
                    src_ref=refs[k].at[pl.ds((2 * layer + half) * rh, rh), :], dst_ref=refs[k].at[pl.ds((2 * layer + half) * rh, rh), :],
                    send_sem=send[i], recv_sem=recv[i], device_id=(x, y, 1 - c), device_id_type=MESH)
                cps.append((desc(c), desc(1 - c)))
        return cps
    return copies_of


def _swap_copies(nt):
    def copies_of(refs, send, recv):
        x, y, c, _ = _place()
        cps = []
        for k in range(nt):
            rh = refs[k].shape[1] // 2
            cps.append(pltpu.make_async_remote_copy(
                src_ref=refs[k].at[:, pl.ds((1 - c) * rh, rh), :], dst_ref=refs[nt + k],
                send_sem=send[k], recv_sem=recv[k], device_id=(x, y, 1 - c), device_id_type=MESH))
        return cps
    return copies_of


def _exchange_copies(nt):
    def copies_of(refs, send, recv):
        x, y, c, chips = _place()
        cps = []
        for t in range(nt):
            for k, chip in enumerate(chips):
                cps.append(pltpu.make_async_remote_copy(
                    src_ref=refs[t].at[2 * chip[0] + chip[1]], dst_ref=refs[nt + t].at[k],
                    send_sem=send[3 * t + k], recv_sem=recv[3 * t + k], device_id=(*chip, c), device_id_type=MESH))
        return cps
    return copies_of


class _StagedReduce:
    def __init__(self, place):
        self.place = place
        self.groups = {}
        self.halves = {}

    @staticmethod
    def slab(t, r):
        return t.reshape(N_CHIPS, r, t.size // (N_CHIPS * r))

    def begin(self, g, grads, ride):
        names = list(grads)
        ts = [self.slab(t, r) for t, r in grads.values()]
        lands = [lax.empty((N_CHIPS, t.shape[1] // 2, t.shape[2]), F32) for t in ts]
        sems, bufs, ride = _split_start(f"grad_swap_start_{g}", ts + lands, ride, len(ts), _swap_copies(len(ts)))
        self.groups[g] = dict(names=names, bufs=bufs, sems=sems)
        return ride

    def advance(self, g, after, ride):
        st = self.groups[g]
        nt = len(st["names"])
        bufs = _split_wait(f"grad_swap_wait_{g}", st["bufs"], st["sems"], after, nt, _swap_copies(nt))
        st["ts"], st["ls"] = bufs[:nt], bufs[nt:]
        ps = [_pair_sum(t, l, self.place, f"pair_sum_{n}") for t, l, n in zip(st["ts"], st["ls"], st["names"])]
        lands = [lax.empty((3,) + p.shape[1:], BF16) for p in ps]
        st["sems"], st["bufs"], ride = _split_start(f"grad_exchange_start_{g}", ps + lands, ride, 3 * nt, _exchange_copies(nt))
        return ride

    def finish(self, g, after):
        st = self.groups[g]
        nt = len(st["names"])
        bufs = _split_wait(f"grad_exchange_wait_{g}", st["bufs"], st["sems"], after, 3 * nt, _exchange_copies(nt))
        for t, l, r, n in zip(st["ts"], st["ls"], bufs[nt:], st["names"]):
            if n[-1] in "01":
                self.halves[n[:-1]] = _final_sum(t, l, r, self.place, f"final_sum_{n}", layer=int(n[-1]), layers=2,
                                                 into=self.halves.get(n[:-1]))
            else:
                self.halves[n] = _final_sum(t, l, r, self.place, f"final_sum_{n}")


def _allgather_small(v):
    rows = v.shape[0]

    def body(v_ref, out_ref, send_sems, recv_sems):
        x, y, c, _ = _place()
        me = 4 * x + 2 * y + c
        out_ref[me] = v_ref[...]
        cps = []
        for k in range(1, 8):
            fx, fy, fc = (k >> 2) & 1, (k >> 1) & 1, k & 1
            to = (1 - x if fx else x, 1 - y if fy else y, 1 - c if fc else c)
            cps.append(pltpu.make_async_remote_copy(
                src_ref=v_ref, dst_ref=out_ref.at[me], send_sem=send_sems.at[k - 1], recv_sem=recv_sems.at[k - 1],
                device_id=to, device_id_type=MESH))
        for cp in cps:
            cp.start()
        for cp in cps:
            cp.wait()

    return _pc(
        body, name="allgather_small",
        in_specs=[pl.BlockSpec(memory_space=pltpu.VMEM)], out_specs=pl.BlockSpec(memory_space=pltpu.VMEM),
        out_shape=jax.ShapeDtypeStruct((8, rows, LANES), F32),
        scratch_shapes=[pltpu.SemaphoreType.DMA((7,)), pltpu.SemaphoreType.DMA((7,))],
    )(v)


def _adamw_math(w, g, m, v):
    m = ADAM_B1 * m + (1.0 - ADAM_B1) * g
    v = ADAM_B2 * v + (1.0 - ADAM_B2) * (g * g)
    m_hat = m / (1.0 - ADAM_B1 ** ADAM_STEP)
    v_hat = v / (1.0 - ADAM_B2 ** ADAM_STEP)
    return -ADAM_LR * (m_hat / (jnp.sqrt(v_hat) + ADAM_EPS) + ADAM_WD * w), m, v


def _adamw(w, g, m, v, name):
    r, cols = w.shape
    tr = min(r, 256)

    def body(w_ref, g_ref, m_ref, v_ref, go_ref, d_ref, mo_ref, vo_ref):
        gv = g_ref[...]
        d, mn, vn = _adamw_math(w_ref[...], gv, m_ref[...], v_ref[...])
        go_ref[...] = gv
        d_ref[...] = d
        mo_ref[...] = mn
        vo_ref[...] = vn

    row = pl.BlockSpec((tr, cols), lambda i: (i, 0))
    return _pc(
        body, name=name, grid=(r // tr,), in_specs=[row] * 4, out_specs=[row] * 4,
        out_shape=[jax.ShapeDtypeStruct((r, cols), F32)] * 4,
        compiler_params=_params(("parallel",)),
    )(w, g, m, v)


def _adamw_small(w, gathered, m, v):
    rows = w.shape[0]

    def body(w_ref, g_ref, m_ref, v_ref, go_ref, d_ref, mo_ref, vo_ref):
        g = g_ref[0]
        for k in range(1, 8):
            g = g + g_ref[k]
        d, mn, vn = _adamw_math(w_ref[...], g, m_ref[...], v_ref[...])
        go_ref[...] = g
        d_ref[...] = d
        mo_ref[...] = mn
        vo_ref[...] = vn

    return _pc(
        body, name="adamw_small",
        out_shape=[jax.ShapeDtypeStruct((rows, LANES), F32)] * 4,
    )(w, gathered, m, v)


_BIAS_ROWS = 32


def _own_slot(flat, chip):
    return lax.dynamic_update_slice(lax.empty((N_CHIPS,) + flat.shape, flat.dtype), flat[None], (chip, 0, 0))


def _pack_first(hyb_w_in, hyb_w_out):
    return jnp.concatenate([t.astype(BF16).reshape(-1, 1024) for t in (hyb_w_in, hyb_w_out)], axis=0)


def _unpack_first(g):
    return {"hyb_w_in": g[:, 0:768, :].reshape(N_CHIPS, 1024, 768), "hyb_w_out": g[:, 768:1024, :].reshape(1024, 1024)}


def _pack_rest(mlp_w_up, mlp_w_down, swa_w_qkv, swa_w_out, swa_b_qkv):
    parts = [t.astype(BF16).reshape(-1, 1024) for t in (mlp_w_up, mlp_w_down, swa_w_qkv, swa_w_out)]
    bias = lax.bitcast_convert_type(swa_b_qkv.reshape(384), BF16).reshape(1, 768)
    bias = jnp.pad(bias, ((0, _BIAS_ROWS - 1), (0, 256)))
    return jnp.concatenate(parts + [bias], axis=0)


def _unpack_rest(g):
    W = {
        "packed": g,
        "swa_w_qkv": g[:, 4096:4480, :].reshape(N_CHIPS, 1024, 384),
        "swa_w_out": g[:, 4480:4736, :].reshape(1024, 1024),
    }
    bias = lax.bitcast_convert_type(g[:, 4736, :768].reshape(N_CHIPS, 384, 2), F32).reshape(1536)
    return W, bias


_SMALL = (("norm_mix", 16), ("norm_mlp", 16), ("ret_gn_gain", 4), ("dil_q_gain", 1), ("dil_k_gain", 1),
          ("swa_b_qkv", 12), ("swa_q_gain", 1), ("swa_k_gain", 1), ("swa_sinks", 1), ("loss", 1))
_SUBLANES = 8


def _slot(r):
    return -(-r // _SUBLANES) * _SUBLANES


def _pack_small(d):
    return jnp.concatenate([jnp.pad(d[n].reshape(r, LANES), ((0, _slot(r) - r), (0, 0))) for n, r in _SMALL], axis=0)


def _unpack_small(p):
    out, o = {}, 0
    for n, r in _SMALL:
        out[n] = p[o:o + r]
        o += _slot(r)
    return out


def kernel(x, positions, norm_mix, norm_mlp, mlp_w_up, mlp_w_down, hyb_w_in, hyb_w_out, ret_gn_gain, dil_q_gain, dil_k_gain, swa_w_qkv, swa_b_qkv, swa_w_out, swa_q_gain, swa_k_gain, swa_sinks, loss_target, m_norm_mix, m_norm_mlp, m_mlp_w_up, m_mlp_w_down, m_hyb_w_in, m_hyb_w_out, m_ret_gn_gain, m_dil_q_gain, m_dil_k_gain, m_swa_w_qkv, m_swa_b_qkv, m_swa_w_out, m_swa_q_gain, m_swa_k_gain, m_swa_sinks, v_norm_mix, v_norm_mlp, v_mlp_w_up, v_mlp_w_down, v_hyb_w_in, v_hyb_w_out, v_ret_gn_gain, v_dil_q_gain, v_dil_k_gain, v_swa_w_qkv, v_swa_b_qkv, v_swa_w_out, v_swa_q_gain, v_swa_k_gain, v_swa_sinks):
    ax, ay, ac = lax.axis_index("x"), lax.axis_index("y"), lax.axis_index("c")
    chip = 2 * ax + ay
    place = jnp.stack([chip, ac]).astype(jnp.int32)
    S = x.shape[1]

    first = _own_slot(_pack_first(hyb_w_in[0], hyb_w_out[0]), chip)
    rest = _own_slot(_pack_rest(mlp_w_up, mlp_w_down, swa_w_qkv[0], swa_w_out[0], swa_b_qkv[0]), chip)
    *sems, first, pos_col = _gather_start(first, positions.reshape(S, 1), "allgather_first_start")
    flight = {}

    def first_of(afters):
        g = _gather_handover(_gather_wait(first, sems, [*afters, rest], "allgather_first_wait"), "allgather_first_handover")
        *flight["sems"], flight["buf"], g = _gather_start(rest, g, "allgather_rest_start")
        return _unpack_first(g)

    def rest_begin(ride):
        buf = _gather_wait(flight["buf"], flight["sems"], [ride], "allgather_rest_wait")
        flight["sems"], flight["bufs"], ride = _split_start("allgather_rest_handover_start", [buf], ride, 3, _handover_copies)
        return ride

    def rest_of(after):
        return _unpack_rest(_split_wait("allgather_rest_handover_wait", flight["bufs"], flight["sems"], after, 3,
                                        _handover_copies)[0])

    P = dict(norm_mix=norm_mix, norm_mlp=norm_mlp, ret_gn_gain=ret_gn_gain, dil_q_gain=dil_q_gain, dil_k_gain=dil_k_gain,
             swa_q_gain=swa_q_gain, swa_k_gain=swa_k_gain, swa_sinks=swa_sinks)

    red = _StagedReduce(place)
    loss_l, grad_x, gp = _local_step(x[0], pos_col, loss_target[0], first_of, rest_begin, rest_of, P, red)

    params = dict(mlp_w_up=(mlp_w_up, m_mlp_w_up, v_mlp_w_up), mlp_w_down=(mlp_w_down, m_mlp_w_down, v_mlp_w_down),
                  hyb_w_in=(hyb_w_in, m_hyb_w_in, v_hyb_w_in), hyb_w_out=(hyb_w_out, m_hyb_w_out, v_hyb_w_out),
                  swa_w_qkv=(swa_w_qkv, m_swa_w_qkv, v_swa_w_qkv), swa_w_out=(swa_w_out, m_swa_w_out, v_swa_w_out))
    big = {}

    def adamw_of(n, g):
        rows = g.shape[0]
        w, m, v = (t.reshape(rows, -1) for t in params[n])
        big[n] = [t.reshape(params[n][0].shape) for t in _adamw(w, g, m, v, f"adamw_{n}")]

    names = ["mlp_w_up", "mlp_w_down", "hyb_w_out", "swa_w_qkv", "swa_w_out"]
    red.halves[names[0]] = red.advance("win", grad_x, red.halves[names[0]])

    gsm = dict(gp, loss=loss_l)
    gsm["swa_sinks"] = jnp.pad(gp["swa_sinks"].reshape(16, HEAD)[:, 0], (0, LANES - 16))
    layers = [params[n][0].shape[0] for n in names]
    sems, shared, packed = _split_start("grad_share_start", [red.halves[n] for n in names], _pack_small(gsm), sum(layers),
                                        _share_copies(layers))
    gathered = _allgather_small(packed)

    def small_pack(norm_mix, norm_mlp, gn, dq, dk, b, sq, sk, sinks):
        dup = lambda t: jnp.tile(t.reshape(1, HEAD), (1, 2))
        bias = lax.dynamic_update_slice(jnp.zeros((12, LANES), F32), b.reshape(3, LANES), (3 * chip, 0))
        return _pack_small(dict(norm_mix=norm_mix, norm_mlp=norm_mlp, ret_gn_gain=gn, dil_q_gain=dup(dq), dil_k_gain=dup(dk),
                                swa_b_qkv=bias, swa_q_gain=dup(sq), swa_k_gain=dup(sk),
                                swa_sinks=jnp.pad(sinks.reshape(16), (0, LANES - 16)), loss=jnp.zeros((1, LANES), F32)))

    pw = small_pack(norm_mix, norm_mlp, ret_gn_gain, dil_q_gain, dil_k_gain, swa_b_qkv, swa_q_gain, swa_k_gain, swa_sinks)
    pm = small_pack(m_norm_mix, m_norm_mlp, m_ret_gn_gain, m_dil_q_gain, m_dil_k_gain, m_swa_b_qkv, m_swa_q_gain, m_swa_k_gain, m_swa_sinks)
    pv = small_pack(v_norm_mix, v_norm_mlp, v_ret_gn_gain, v_dil_q_gain, v_dil_k_gain, v_swa_b_qkv, v_swa_q_gain, v_swa_k_gain, v_swa_sinks)
    small_flat = _adamw_small(pw, gathered, pm, pv)
    small = [_unpack_small(t) for t in small_flat]

    for n, g in zip(names, _split_wait("grad_share_wait", shared, sems, small_flat[1], sum(layers), _share_copies(layers))):
        adamw_of(n, g)
    red.finish("win", big[names[-1]][1])
    adamw_of("hyb_w_in", _share_halves([(red.halves["hyb_w_in"], 1)], "grad_share_last")[0])

    def small_out(n, k):
        t = small[k][n]
        if n in ("norm_mix", "norm_mlp"):
            return t.reshape(2, D_MODEL)
        if n == "ret_gn_gain":
            return t.reshape(1, RET_HEADS, 128)
        if n == "swa_b_qkv":
            return lax.dynamic_slice(t, (3 * chip, 0), (3, LANES)).reshape(1, 384)
        if n == "swa_sinks":
            return t[0, :16].reshape(1, 16)
        return t[0, :HEAD].reshape(1, HEAD)

    order = ["norm_mix", "norm_mlp", "mlp_w_up", "mlp_w_down", "hyb_w_in", "hyb_w_out", "ret_gn_gain", "dil_q_gain",
             "dil_k_gain", "swa_w_qkv", "swa_b_qkv", "swa_w_out", "swa_q_gain", "swa_k_gain", "swa_sinks"]
    is_big = {"mlp_w_up", "mlp_w_down", "hyb_w_in", "hyb_w_out", "swa_w_qkv", "swa_w_out"}
    outs = [small[0]["loss"][0, 0], grad_x[None]]
    for k in range(4):
        outs += [big[n][k] if n in is_big else small_out(n, k) for n in order]
    return tuple(outs)
```

```python
import numpy as np
import jax
import jax.numpy as jnp
from jax import lax
from jax.experimental import pallas as pl
from jax.experimental.pallas import tpu as pltpu

F32, BF16 = jnp.float32, jnp.bfloat16
MESH = pl.DeviceIdType.MESH

LANES = 128
VMEM_LIMIT = 48 << 20
VMEM_LIMIT_WIDE = 60 << 20
D_MODEL = 1024
HEAD = 64
EPS = 1e-6
BLK = 128
RET_HEADS = 4
RET_THETA = 10000.0
ROPE_THETA = 500000.0
ROPE_DIMS = 16
DIL_PATTERNS = ((128, 1), (512, 4), (2048, 16))
SWA_DIST = 127
N_CHIPS = 4
ADAM_LR, ADAM_B1, ADAM_B2, ADAM_EPS, ADAM_WD, ADAM_STEP = 0.001, 0.9, 0.999, 1e-08, 0.01, 10

_LOG_GAMMA = [float(np.log1p(-np.exp2(np.float32(-5.0 - h)))) for h in range(RET_HEADS)]


def _pc(body, **kw):
    return pl.pallas_call(body, **kw)


def _params(sem, limit=VMEM_LIMIT):
    return pltpu.CompilerParams(dimension_semantics=sem, vmem_limit_bytes=limit)


def _matmul(a, b, *, dims, tm, tn, tk, outs, name, epilogue=None, extras=(), b_cs=False, b_rs=0, b_row0=0, b_rows=0,
            o_cs=0, a_pro=None, vmem=VMEM_LIMIT):
    if dims == "nn":
        M, K = a.shape
        N = b.shape[0] * b.shape[2] if b_cs else b.shape[1]
        a_spec = pl.BlockSpec((tm, tk), lambda i, j, k: (i, k))
        if b_cs:
            npt = b.shape[2] // tn
            b_spec = pl.BlockSpec((None, tk, tn), lambda i, j, k: (j // npt, k + b_row0, j % npt))
        elif b_rs:
            K, N, kps = b.shape[0] * b_rs, b.shape[2], b_rs // tk
            b_spec = pl.BlockSpec((None, tk, tn), lambda i, j, k: (k // kps, b_row0 + k % kps, j))
        else:
            b_spec = pl.BlockSpec((tk, tn), lambda i, j, k: (k, j))
        contract = (((1,), (0,)), ((), ()))
    elif dims == "nt":
        M, K = a.shape
        N = (b_rows or b.shape[1]) if b_cs else b.shape[0]
        a_spec = pl.BlockSpec((tm, tk), lambda i, j, k: (i, k))
        if b_cs:
            kpt = b.shape[2] // tk
            b_spec = pl.BlockSpec((None, tn, tk), lambda i, j, k: (k // kpt, j + b_row0, k % kpt))
        elif b_rs:
            N, jps = b.shape[0] * b_rs, b_rs // tn
            b_spec = pl.BlockSpec((None, tn, tk), lambda i, j, k: (j // jps, b_row0 + j % jps, k))
        else:
            b_spec = pl.BlockSpec((tn, tk), lambda i, j, k: (j, k))
        contract = (((1,), (1,)), ((), ()))
    else:
        K, M = a.shape
        N = b.shape[1]
        a_spec = pl.BlockSpec((tk, tm), lambda i, j, k: (k, i))
        b_spec = pl.BlockSpec((tk, tn), lambda i, j, k: (k, j))
        contract = (((0,), (0,)), ((), ()))
    assert M % tm == 0 and N % tn == 0 and K % tk == 0, (name, M, N, K, tm, tn, tk)
    nk = K // tk
    ex_specs = []
    for arr, kind in extras:
        if kind == "mn":
            ex_specs.append(pl.BlockSpec((tm, tn), lambda i, j, k: (i, j)))
        elif kind == "n":
            ex_specs.append(pl.BlockSpec((1, tn), lambda i, j, k: (0, j)))
        elif kind == "full":
            ex_specs.append(pl.BlockSpec(arr.shape, lambda i, j, k, nd=arr.ndim: (0,) * nd))
        else:
            ex_specs.append(pl.BlockSpec((tm, kind), lambda i, j, k: (i, 0)))
    if o_cs:
        n_sh = N // o_cs
        opt = n_sh // tn
        o_shape = (o_cs, M, n_sh)
        o_spec = pl.BlockSpec((None, tm, tn), lambda i, j, k: (j // opt, i, j % opt))
    else:
        o_shape = (M, N)
        o_spec = pl.BlockSpec((tm, tn), lambda i, j, k: (i, j))
    o_specs, o_shapes, summed = [], [], []
    for o in outs:
        if isinstance(o, tuple) and o[0] == "colsum":
            assert N == tn
            o_specs.append(pl.BlockSpec((1, tn), lambda i, j, k: (0, j)))
            o_shapes.append(jax.ShapeDtypeStruct((1, N), F32))
            summed.append(True)
        elif isinstance(o, tuple):
            o_specs.append(pl.BlockSpec((tm, o[1]), lambda i, j, k: (i, 0)))
            o_shapes.append(jax.ShapeDtypeStruct((M, o[1]), o[0]))
            summed.append(False)
        else:
            o_specs.append(o_spec)
            o_shapes.append(jax.ShapeDtypeStruct(o_shape, o))
            summed.append(False)
    n_ex, n_out = len(extras), len(outs)
    if epilogue is None:
        epilogue = lambda acc: (acc,)

    def body(a_ref, b_ref, *rest):
        ex, o_refs, acc = rest[:n_ex], rest[n_ex:n_ex + n_out], rest[-1]
        i, k = pl.program_id(0), pl.program_id(2)

        @pl.when(k == 0)
        def _():
            acc[...] = jnp.zeros_like(acc)

        av = a_ref[...] if a_pro is None else a_pro(a_ref[...])
        acc[...] += lax.dot_general(av.astype(BF16), b_ref[...].astype(BF16), contract, preferred_element_type=F32)

        @pl.when(k == nk - 1)
        def _():
            vals = epilogue(acc[...], *[e[...] for e in ex])
            for r, v, sm in zip(o_refs, vals, summed):
                if sm:
                    @pl.when(i == 0)
                    def _(r=r):
                        r[...] = jnp.zeros_like(r)

                    r[...] += v
                else:
                    r[...] = v.astype(r.dtype)

    res = _pc(
        body, name=name, grid=(M // tm, N // tn, nk),
        in_specs=[a_spec, b_spec] + ex_specs, out_specs=o_specs, out_shape=o_shapes,
        scratch_shapes=[pltpu.VMEM((tm, tn), F32)],
        compiler_params=_params(("arbitrary" if any(summed) else "parallel", "parallel", "arbitrary"), vmem),
    )(a, b, *[e for e, _ in extras])
    return res[0] if n_out == 1 else res


def _roll(x, s):
    return pltpu.roll(x, s % LANES, 1)


def _rope(x, A, B, C, half):
    return x * A + _roll(x, LANES - half) * B + _roll(x, half) * C


def _rope_t(g, A, B, C, half):
    return g * A + _roll(g * B, half) + _roll(g * C, LANES - half)


def _gmean(x, G):
    hi = x.astype(BF16)
    lo = (x - hi.astype(F32)).astype(BF16)
    Gb = G.astype(BF16)
    return jnp.dot(hi, Gb, preferred_element_type=F32) + jnp.dot(lo, Gb, preferred_element_type=F32)


def _head_mask(shape, half):
    lane = lax.broadcasted_iota(jnp.int32, shape, len(shape) - 1)
    return (lane >= HEAD) if half else (lane < HEAD)


def _group_matrix():
    i = np.arange(LANES)
    return jnp.asarray((i[:, None] // HEAD == i[None, :] // HEAD).astype(np.float32) / HEAD)


def _rope_inv():
    l = np.arange(LANES) % HEAD
    inv_r = np.power(np.float32(RET_THETA), -(l % 32).astype(np.float32) * np.float32(2.0 / HEAD))
    hp = ROPE_DIMS // 2
    inv_p = np.power(np.float32(ROPE_THETA), -(l % hp).astype(np.float32) * np.float32(2.0 / ROPE_DIMS))
    inv_p = np.where(l < ROPE_DIMS, inv_p, 0.0)
    return jnp.asarray(np.stack([inv_r, inv_p]).astype(np.float32))


def _tables(pos_col):
    S = pos_col.shape[0]
    tm = 512
    hp = ROPE_DIMS // 2

    def body(p_ref, inv_ref, o_ref):
        p = p_ref[...].astype(F32)
        lane = lax.broadcasted_iota(jnp.int32, (tm, LANES), 1) % HEAD
        ang = p * inv_ref[0:1, :]
        c, s = jnp.cos(ang), jnp.sin(ang)
        o_ref[:, 0:128] = c
        o_ref[:, 128:256] = jnp.where(lane < 32, -s, 0.0)
        o_ref[:, 256:384] = jnp.where(lane >= 32, s, 0.0)
        ang = p * inv_ref[1:2, :]
        c, s = jnp.cos(ang), jnp.sin(ang)
        o_ref[:, 384:512] = c
        o_ref[:, 512:640] = jnp.where(lane < hp, -s, 0.0)
        o_ref[:, 640:768] = jnp.where((lane >= hp) & (lane < ROPE_DIMS), s, 0.0)

    return _pc(
        body, name="rope_tables", grid=(S // tm,),
        in_specs=[pl.BlockSpec((tm, 1), lambda i: (i, 0)), pl.BlockSpec((2, LANES), lambda i: (0, 0))],
        out_specs=pl.BlockSpec((tm, 768), lambda i: (i, 0)),
        out_shape=jax.ShapeDtypeStruct((S, 768), F32),
        compiler_params=_params(("parallel",)),
    )(pos_col, _rope_inv())


def _tab(tab_ref, which):
    o = 384 * which
    return tab_ref[:, o:o + 128], tab_ref[:, o + 128:o + 256], tab_ref[:, o + 256:o + 384]


def _rms_fwd(x, g, name):
    S, Dm = x.shape
    tm = 512

    def body(x_ref, g_ref, h_ref):
        xv = x_ref[...]
        r = lax.rsqrt(jnp.mean(xv * xv, axis=-1, keepdims=True) + EPS)
        h_ref[...] = (xv * r * g_ref[...]).astype(BF16)

    return _pc(
        body, name=name, grid=(S // tm,),
        in_specs=[pl.BlockSpec((tm, Dm), lambda i: (i, 0)), pl.BlockSpec((1, Dm), lambda i: (0, 0))],
        out_specs=pl.BlockSpec((tm, Dm), lambda i: (i, 0)),
        out_shape=jax.ShapeDtypeStruct((S, Dm), BF16),
        compiler_params=_params(("parallel",)),
    )(x, g.reshape(1, Dm))


def _hn_fwd(x, gain, G):
    r = lax.rsqrt(_gmean(x * x, G) + EPS)
    return x * r * gain


def _hn_bwd(x, gain, dy, G):
    r = lax.rsqrt(_gmean(x * x, G) + EPS)
    t = dy * gain
    dx = r * t - x * (r * r * r) * _gmean(x * t, G)
    return dx, jnp.sum(dy * x * r, axis=0, keepdims=True)


def _fold_halves(v):
    return v + _roll(v, HEAD)


def _even_pre_fwd(proj, tab, qg, kg):
    S = proj.shape[0]
    tm = 512

    def body(p_ref, tab_ref, qg_ref, kg_ref, g_ref, rq_ref, rk_ref, rv_ref, dq_ref, dk_ref, dv_ref):
        Ar, Br, Cr = _tab(tab_ref, 0)
        Ap, Bp, Cp = _tab(tab_ref, 1)
        G = g_ref[...]
        for c in range(2):
            sl = slice(c * 128, (c + 1) * 128)
            rq_ref[:, sl] = _rope(p_ref[:, c * 128:(c + 1) * 128], Ar, Br, Cr, 32).astype(BF16)
            rk_ref[:, sl] = (_rope(p_ref[:, 256 + c * 128:256 + (c + 1) * 128], Ar, Br, Cr, 32) * 0.125).astype(BF16)
        rv_ref[...] = p_ref[:, 512:1024].astype(BF16)
        for c in range(4):
            sl = slice(c * 128, (c + 1) * 128)
            q = _hn_fwd(p_ref[:, 1536 + c * 128:1536 + (c + 1) * 128], qg_ref[...], G)
            dq_ref[:, sl] = _rope(q, Ap, Bp, Cp, 8).astype(BF16)
            k = _hn_fwd(p_ref[:, 2048 + c * 128:2048 + (c + 1) * 128], kg_ref[...], G)
            dk_ref[:, sl] = _rope(k, Ap, Bp, Cp, 8).astype(BF16)
        dv_ref[...] = p_ref[:, 2560:3072].astype(BF16)

    row = lambda w: pl.BlockSpec((tm, w), lambda i: (i, 0))
    vec = pl.BlockSpec((1, LANES), lambda i: (0, 0))
    return _pc(
        body, name="even_pre_fwd", grid=(S // tm,),
        in_specs=[row(3072), row(768), vec, vec, pl.BlockSpec((LANES, LANES), lambda i: (0, 0))],
        out_specs=[row(256), row(256), row(512), row(512), row(512), row(512)],
        out_shape=[jax.ShapeDtypeStruct((S, w), BF16) for w in (256, 256, 512, 512, 512, 512)],
        compiler_params=_params(("parallel",), VMEM_LIMIT_WIDE),
    )(proj, tab, qg, kg, _group_matrix())


def _even_pre_bwd(proj, tab, qg, kg, drq, drk, drv, drg, dqs, dks, dvs):
    S = proj.shape[0]
    tm = 512
    npat = len(dqs)

    def body(p_ref, tab_ref, qg_ref, kg_ref, g_ref, drq_ref, drk_ref, drv_ref, drg_ref, *rest):
        dq_refs, dk_refs, dv_refs = rest[:npat], rest[npat:2 * npat], rest[2 * npat:3 * npat]
        dp_ref, dqg_ref, dkg_ref = rest[3 * npat:]
        Ar, Br, Cr = _tab(tab_ref, 0)
        Ap, Bp, Cp = _tab(tab_ref, 1)
        G = g_ref[...]
        for c in range(2):
            sl = slice(c * 128, (c + 1) * 128)
            dp_ref[:, c * 128:(c + 1) * 128] = _rope_t(drq_ref[:, sl], Ar, Br, Cr, 32).astype(BF16)
            dp_ref[:, 256 + c * 128:256 + (c + 1) * 128] = _rope_t(drk_ref[:, sl] * 0.125, Ar, Br, Cr, 32).astype(BF16)
        dp_ref[:, 512:1024] = drv_ref[...].astype(BF16)
        dp_ref[:, 1024:1536] = drg_ref[...].astype(BF16)
        accq = jnp.zeros((1, LANES), F32)
        acck = jnp.zeros((1, LANES), F32)
        for c in range(4):
            sl = slice(c * 128, (c + 1) * 128)
            g = dq_refs[0][:, sl]
            for r in dq_refs[1:]:
                g = g + r[:, sl]
            dx, dg = _hn_bwd(p_ref[:, 1536 + c * 128:1536 + (c + 1) * 128], qg_ref[...], _rope_t(g, Ap, Bp, Cp, 8), G)
            dp_ref[:, 1536 + c * 128:1536 + (c + 1) * 128] = dx.astype(BF16)
            accq = accq + dg
            g = dk_refs[0][:, sl]
            for r in dk_refs[1:]:
                g = g + r[:, sl]
            dx, dg = _hn_bwd(p_ref[:, 2048 + c * 128:2048 + (c + 1) * 128], kg_ref[...], _rope_t(g, Ap, Bp, Cp, 8), G)
            dp_ref[:, 2048 + c * 128:2048 + (c + 1) * 128] = dx.astype(BF16)
            acck = acck + dg
        g = dv_refs[0][...]
        for r in dv_refs[1:]:
            g = g + r[...]
        dp_ref[:, 2560:3072] = g.astype(BF16)

        @pl.when(pl.program_id(0) == 0)
        def _():
            dqg_ref[...] = jnp.zeros_like(dqg_ref)
            dkg_ref[...] = jnp.zeros_like(dkg_ref)

        dqg_ref[...] += _fold_halves(accq)
        dkg_ref[...] += _fold_halves(acck)

    row = lambda w: pl.BlockSpec((tm, w), lambda i: (i, 0))
    vec = pl.BlockSpec((1, LANES), lambda i: (0, 0))
    return _pc(
        body, name="even_pre_bwd", grid=(S // tm,),
        in_specs=[row(3072), row(768), vec, vec, pl.BlockSpec((LANES, LANES), lambda i: (0, 0)),
                  row(256), row(256), row(512), row(512)] + [row(512)] * (3 * npat),
        out_specs=[row(3072), vec, vec],
        out_shape=[jax.ShapeDtypeStruct((S, 3072), BF16), jax.ShapeDtypeStruct((1, LANES), F32),
                   jax.ShapeDtypeStruct((1, LANES), F32)],
        compiler_params=_params(("arbitrary",), VMEM_LIMIT_WIDE),
    )(proj, tab, qg, kg, _group_matrix(), drq, drk, drv, drg, *dqs, *dks, *dvs)


def _ret_consts(pair, half):
    lg = jnp.where(pair == 0, _LOG_GAMMA[half], _LOG_GAMMA[2 + half]).astype(F32)
    i = lax.broadcasted_iota(jnp.int32, (BLK, BLK), 0)
    j = lax.broadcasted_iota(jnp.int32, (BLK, BLK), 1)
    diff = (i - j).astype(F32)
    decay = jnp.where(diff >= 0, jnp.exp(lg * jnp.maximum(diff, 0.0)), 0.0)
    t = lax.broadcasted_iota(jnp.int32, (BLK, 1), 0).astype(F32)
    xi = jnp.exp(lg * (t + 1.0))
    zeta = jnp.exp(lg * (BLK - 1.0 - t))
    cd = jnp.exp(jnp.full((1, 1), BLK, F32) * lg)
    return decay, xi, zeta, cd


RET_STEP = 8


def _ret_fwd(rq, rk, rv):
    S = rq.shape[0]
    nc = S // BLK
    rows = RET_STEP * BLK

    def body(q_ref, k_ref, v_ref, o_ref, st_ref, R):
        p, n = pl.program_id(0), pl.program_id(1)

        @pl.when(n == 0)
        def _():
            R[...] = jnp.zeros_like(R)

        consts = [_ret_consts(p, half) for half in range(2)]
        masks = [_head_mask((BLK, LANES), half) for half in range(2)]
        for ci in range(RET_STEP):
            rs = slice(ci * BLK, (ci + 1) * BLK)
            q2, k2 = q_ref[rs, :], k_ref[rs, :]
            for half in range(2):
                decay, xi, zeta, cd = consts[half]
                m = masks[half]
                qm = jnp.where(m, q2, jnp.zeros_like(q2))
                km = jnp.where(m, k2, jnp.zeros_like(k2))
                v = v_ref[rs, half * 128:(half + 1) * 128]
                Rb = R[half].astype(BF16)
                st_ref[ci, half] = Rb
                sc = lax.dot_general(qm, k2, (((1,), (1,)), ((), ())), preferred_element_type=F32) * decay
                o = jnp.dot(sc.astype(BF16), v, preferred_element_type=F32)
                o = o + jnp.dot(qm, Rb, preferred_element_type=F32) * xi
                o_ref[rs, half * 128:(half + 1) * 128] = o
                kz = (km.astype(F32) * zeta).astype(BF16)
                R[half] = R[half] * cd + lax.dot_general(kz, v, (((0,), (0,)), ((), ())), preferred_element_type=F32)

    return _pc(
        body, name="ret_fwd", grid=(2, nc // RET_STEP),
        in_specs=[pl.BlockSpec((rows, 128), lambda p, n: (n, p)), pl.BlockSpec((rows, 128), lambda p, n: (n, p)),
                  pl.BlockSpec((rows, 256), lambda p, n: (n, p))],
        out_specs=[pl.BlockSpec((rows, 256), lambda p, n: (n, p)),
                   pl.BlockSpec((None, RET_STEP, 2, 128, 128), lambda p, n: (p, n, 0, 0, 0))],
        out_shape=[jax.ShapeDtypeStruct((S, 512), F32), jax.ShapeDtypeStruct((2, nc, 2, 128, 128), BF16)],
        scratch_shapes=[pltpu.VMEM((2, 128, 128), F32)],
        compiler_params=_params(("parallel", "arbitrary")),
    )(rq, rk, rv)


def _ret_bwd(rq, rk, rv, states, do):
    S = rq.shape[0]
    nc = S // BLK
    ns = nc // RET_STEP
    rows = RET_STEP * BLK
    nt = (((1,), (1,)), ((), ()))
    tn = (((0,), (0,)), ((), ()))

    def body(q_ref, k_ref, v_ref, st_ref, do_ref, dq_ref, dk_ref, dv_ref, U):
        p, n = pl.program_id(0), pl.program_id(1)

        @pl.when(n == 0)
        def _():
            U[...] = jnp.zeros_like(U)

        consts = [_ret_consts(p, half) for half in range(2)]
        masks = [_head_mask((BLK, LANES), half) for half in range(2)]
        for ci in reversed(range(RET_STEP)):
            rs = slice(ci * BLK, (ci + 1) * BLK)
            q2, k2 = q_ref[rs, :], k_ref[rs, :]
            dq_acc = jnp.zeros((BLK, LANES), F32)
            dk_acc = jnp.zeros((BLK, LANES), F32)
            for half in range(2):
                decay, xi, zeta, cd = consts[half]
                m = masks[half]
                qm = jnp.where(m, q2, jnp.zeros_like(q2))
                km = jnp.where(m, k2, jnp.zeros_like(k2))
                v = v_ref[rs, half * 128:(half + 1) * 128]
                dob = do_ref[rs, half * 128:(half + 1) * 128].astype(BF16)
                Rb = st_ref[ci, half]
                Ub = U[half].astype(BF16)
                dsc = (lax.dot_general(dob, v, nt, preferred_element_type=F32) * decay).astype(BF16)
                xdo = (dob.astype(F32) * xi).astype(BF16)
                dq_acc += jnp.dot(dsc, km, preferred_element_type=F32) + lax.dot_general(xdo, Rb, nt, preferred_element_type=F32)
                dk_acc += lax.dot_general(dsc, qm, tn, preferred_element_type=F32) \
                    + lax.dot_general(v, Ub, nt, preferred_element_type=F32) * zeta
                sc = (lax.dot_general(qm, k2, nt, preferred_element_type=F32) * decay).astype(BF16)
                kz = (km.astype(F32) * zeta).astype(BF16)
                dv_ref[rs, half * 128:(half + 1) * 128] = lax.dot_general(sc, dob, tn, preferred_element_type=F32) \
                    + jnp.dot(kz, Ub, preferred_element_type=F32)
                U[half] = U[half] * cd + lax.dot_general(qm, xdo, tn, preferred_element_type=F32)
            dq_ref[rs, :] = dq_acc
            dk_ref[rs, :] = dk_acc

    rev = lambda w: pl.BlockSpec((rows, w), lambda p, n: (ns - 1 - n, p))
    return _pc(
        body, name="ret_bwd", grid=(2, ns),
        in_specs=[rev(128), rev(128), rev(256),
                  pl.BlockSpec((None, RET_STEP, 2, 128, 128), lambda p, n: (p, ns - 1 - n, 0, 0, 0)), rev(256)],
        out_specs=[rev(128), rev(128), rev(256)],
        out_shape=[jax.ShapeDtypeStruct((S, 256), F32), jax.ShapeDtypeStruct((S, 256), F32),
                   jax.ShapeDtypeStruct((S, 512), F32)],
        scratch_shapes=[pltpu.VMEM((2, 128, 128), F32)],
        compiler_params=_params(("parallel", "arbitrary")),
    )(rq, rk, rv, states, do)


ATT_TILE = 2048


def _rows(ref, start, n, r):
    if r == 1:
        return ref[pl.ds(start, n), :]
    return ref[pl.ds(start, n, stride=r), :]


def _twice(x):
    return jnp.concatenate([x, x], axis=0)


def _stack_heads(x, masks):
    zero = jnp.zeros_like(x)
    return jnp.concatenate([jnp.where(masks[0], x, zero), jnp.where(masks[1], x, zero)], axis=0)


def _set_rows(ref, start, n, r, val):
    if r == 1:
        ref[pl.ds(start, n), :] = val
    else:
        ref[pl.ds(start, n, stride=r), :] = val


def _band_geometry(S, patterns):
    rmax = max(r for _, r in patterns)
    H = BLK * rmax
    T = min(S, ATT_TILE)
    assert T % H == 0 and S % T == 0
    return H, T, S // T, T // BLK


def _band_fwd(q, k, v, *, patterns, nq, name, sinks=None, want_bf16=False):
    S, Ck = k.shape
    H, T, nt, nbt = _band_geometry(S, patterns)
    ncol = Ck // LANES
    scale = HEAD ** -0.5
    has_sink = sinks is not None
    nt_dims = (((1,), (1,)), ((), ()))

    def body(*refs):
        q_ref, kp_ref, kc_ref, vp_ref, vc_ref = refs[:5]
        sk_ref = refs[5] if has_sink else None
        n_out = 3 if want_bf16 else 2
        outs = refs[5 + has_sink:5 + has_sink + n_out]
        qf, kf, vf, M, L, A = refs[5 + has_sink + n_out:]
        t = pl.program_id(1)
        kf[0:H, :] = kp_ref[...].astype(F32)
        kf[H:H + T, :] = kc_ref[...].astype(F32)
        vf[0:H, :] = vp_ref[...].astype(F32)
        vf[H:H + T, :] = vc_ref[...].astype(F32)
        r_i = lax.broadcasted_iota(jnp.int32, (BLK, 2 * BLK), 0)
        c_i = lax.broadcasted_iota(jnp.int32, (BLK, 2 * BLK), 1)
        dist_i = r_i + BLK - c_i
        masks = [_head_mask((BLK, LANES), h) for h in range(2)]

        for i in range(nq):
            qf[...] = q_ref[:, i * 128:(i + 1) * 128].astype(F32) * scale
            for p, (dist, r) in enumerate(patterns):
                in_band = (dist_i >= 0) & (dist_i <= dist)
                in_band_first = in_band & ((c_i >= BLK) | (t > 0))
                in_band, in_band_first = _twice(in_band), _twice(in_band_first)

                def unit(j, b, p=p, r=r, in_band=in_band, in_band_first=in_band_first):
                    q0 = j + b * (BLK * r)
                    q2 = _rows(qf, q0, BLK, r).astype(BF16)
                    kcat = _rows(kf, H + q0 - BLK * r, 2 * BLK, r).astype(BF16)
                    vcat = _rows(vf, H + q0 - BLK * r, 2 * BLK, r).astype(BF16)
                    valid = in_band if b > 0 else in_band_first
                    s = lax.dot_general(_stack_heads(q2, masks), kcat, nt_dims, preferred_element_type=F32)
                    s = jnp.where(valid, s, -jnp.inf)
                    mx = jnp.max(s, axis=1, keepdims=True)
                    pr = jnp.exp(s - mx)
                    den = jnp.sum(pr, axis=1, keepdims=True)
                    pv = jnp.dot(pr.astype(BF16), vcat, preferred_element_type=F32)
                    m2 = jnp.where(masks[0], mx[:BLK], mx[BLK:])
                    l2 = jnp.where(masks[0], den[:BLK], den[BLK:])
                    a2 = jnp.where(masks[0], pv[:BLK], pv[BLK:])
                    if p > 0:
                        mo = _rows(M, q0, BLK, r)
                        mn = jnp.maximum(mo, m2)
                        wa, wb = jnp.exp(mo - mn), jnp.exp(m2 - mn)
                        l2 = wa * _rows(L, q0, BLK, r) + wb * l2
                        a2 = wa * _rows(A, q0, BLK, r) + wb * a2
                        m2 = mn
                    _set_rows(M, q0, BLK, r, m2)
                    _set_rows(L, q0, BLK, r, l2)
                    _set_rows(A, q0, BLK, r, a2)

                for u in range(nbt):
                    unit(u % r, u // r)
            sl = slice(i * 128, (i + 1) * 128)
            mm, ll, aa = M[...], L[...], A[...]
            if has_sink:
                snk = sk_ref[:, sl]
                mn = jnp.maximum(mm, snk)
                w = jnp.exp(mm - mn)
                ll = ll * w + jnp.exp(snk - mn)
                aa = aa * w
                mm = mn
            o = aa / ll
            outs[0][:, sl] = o
            outs[1][:, sl] = mm + jnp.log(ll)
            if want_bf16:
                outs[2][:, sl] = o.astype(BF16)

    th = T // H
    qspec = pl.BlockSpec((T, nq * 128), lambda j, t: (t, j))
    cur = pl.BlockSpec((T, 128), lambda j, t: (t, j))
    prev = pl.BlockSpec((H, 128), lambda j, t: (jnp.maximum(t * th - 1, 0), j))
    in_specs = [qspec, prev, cur, prev, cur]
    args = [q, k, k, v, v]
    if has_sink:
        in_specs.append(pl.BlockSpec((1, nq * 128), lambda j, t: (0, j)))
        args.append(sinks)
    out_dts = [F32, F32] + ([BF16] if want_bf16 else [])
    return _pc(
        body, name=name, grid=(ncol, nt), in_specs=in_specs,
        out_specs=[qspec] * len(out_dts),
        out_shape=[jax.ShapeDtypeStruct(q.shape, dt) for dt in out_dts],
        scratch_shapes=[pltpu.VMEM((T, LANES), F32), pltpu.VMEM((H + T, LANES), F32), pltpu.VMEM((H + T, LANES), F32),
                        pltpu.VMEM((T, LANES), F32), pltpu.VMEM((T, LANES), F32), pltpu.VMEM((T, LANES), F32)],
        compiler_params=_params(("parallel", "parallel")),
    )(*args)


def _band_bwd(q, k, v, lse, delta, do, *, patterns, nq, name, sinks=None, do_col0=0):
    S, Ck = k.shape
    H, T, nt, nbt = _band_geometry(S, patterns)
    ncol = Ck // LANES
    scale = HEAD ** -0.5
    has_sink = sinks is not None
    nt_dims = (((1,), (1,)), ((), ()))
    tn_dims = (((0,), (0,)), ((), ()))

    def body(*refs):
        (qc_ref, qn_ref, kp_ref, kc_ref, vp_ref, vc_ref, lc_ref, ln_ref, ec_ref, en_ref, dc_ref, dn_ref) = refs[:12]
        sk_ref = refs[12] if has_sink else None
        n_out = 4 if has_sink else 3
        outs = refs[12 + has_sink:12 + has_sink + n_out]
        dq_ref, dk_ref, dv_ref = outs[:3]
        qf, kf, vf, lf, ef, df = refs[12 + has_sink + n_out:]
        t = pl.program_id(1)
        kf[0:H, :] = kp_ref[...].astype(F32)
        kf[H:H + T, :] = kc_ref[...].astype(F32)
        vf[0:H, :] = vp_ref[...].astype(F32)
        vf[H:H + T, :] = vc_ref[...].astype(F32)
        dk_ref[...] = jnp.zeros_like(dk_ref)
        dv_ref[...] = jnp.zeros_like(dv_ref)
        r_i = lax.broadcasted_iota(jnp.int32, (BLK, 2 * BLK), 0)
        c_i = lax.broadcasted_iota(jnp.int32, (BLK, 2 * BLK), 1)
        dist_q = r_i + BLK - c_i
        dist_h = dist_q[:, :BLK]
        m1 = [_head_mask((BLK, LANES), h) for h in range(2)]

        def stacked_inputs(q2, do2, l2, e2):
            spread = lambda v: jnp.concatenate([jnp.where(m1[0], v, _roll(v, HEAD)), jnp.where(m1[1], v, _roll(v, HEAD))], axis=0)
            return _stack_heads(q2, m1), _stack_heads(do2.astype(BF16), m1), spread(l2), spread(e2)

        for i in range(nq):
            sl = slice(i * 128, (i + 1) * 128)
            qf[0:T, :] = qc_ref[:, sl].astype(F32) * scale
            qf[T:T + H, :] = qn_ref[:, sl].astype(F32) * scale
            for buf, c_ref, n_ref in ((lf, lc_ref, ln_ref), (ef, ec_ref, en_ref), (df, dc_ref, dn_ref)):
                buf[0:T, :] = c_ref[:, sl]
                buf[T:T + H, :] = n_ref[:, sl]
            if has_sink:
                @pl.when(t == 0)
                def _():
                    outs[3][:, sl] = jnp.zeros((1, LANES), F32)

                outs[3][:, sl] += jnp.sum(-jnp.exp(sk_ref[:, sl] - lc_ref[:, sl]) * ec_ref[:, sl], axis=0, keepdims=True)
            for p, (dist, r) in enumerate(patterns):
                band_q = (dist_q >= 0) & (dist_q <= dist)
                band_first = band_q & ((c_i >= BLK) | (t > 0))
                band_h = (dist_h >= 0) & (dist_h <= dist)
                band_q, band_first, band_h = _twice(band_q), _twice(band_first), _twice(band_h)

                def add_rows(ref, start, val, r=r):
                    _set_rows(ref, start, BLK, r, _rows(ref, start, BLK, r) + val)

                def unit(j, b, p=p, r=r, band_q=band_q, band_first=band_first):
                    q0 = j + b * (BLK * r)
                    q2 = _rows(qf, q0, BLK, r).astype(BF16)
                    do2, l2, e2 = _rows(df, q0, BLK, r), _rows(lf, q0, BLK, r), _rows(ef, q0, BLK, r)
                    kcat = _rows(kf, H + q0 - BLK * r, 2 * BLK, r).astype(BF16)
                    vcat = _rows(vf, H + q0 - BLK * r, 2 * BLK, r).astype(BF16)
                    valid = band_q if b > 0 else band_first
                    qs, dos, ls, es = stacked_inputs(q2, do2, l2, e2)
                    s = lax.dot_general(qs, kcat, nt_dims, preferred_element_type=F32)
                    pr = jnp.where(valid, jnp.exp(s - jnp.concatenate([ls, ls], axis=1)), 0.0)
                    dp = lax.dot_general(dos, vcat, nt_dims, preferred_element_type=F32)
                    ds = (pr * (dp - jnp.concatenate([es, es], axis=1))).astype(BF16)
                    dqs = jnp.dot(ds, kcat, preferred_element_type=F32) * scale
                    dq2 = jnp.where(m1[0], dqs[:BLK], dqs[BLK:])
                    dvc = lax.dot_general(pr.astype(BF16), dos, tn_dims, preferred_element_type=F32)
                    dkc = lax.dot_general(ds, qs, tn_dims, preferred_element_type=F32)
                    if p > 0:
                        dq2 = dq2 + _rows(dq_ref.at[:, sl], q0, BLK, r)
                    _set_rows(dq_ref.at[:, sl], q0, BLK, r, dq2)
                    add_rows(dk_ref, q0, dkc[BLK:])
                    add_rows(dv_ref, q0, dvc[BLK:])
                    if b > 0:
                        add_rows(dk_ref, q0 - BLK * r, dkc[:BLK])
                        add_rows(dv_ref, q0 - BLK * r, dvc[:BLK])

                def halo_unit(j, r=r, band_h=band_h):
                    k0 = j + (nbt // r - 1) * (BLK * r)
                    q2 = _rows(qf, T + j, BLK, r).astype(BF16)
                    do2, l2, e2 = _rows(df, T + j, BLK, r), _rows(lf, T + j, BLK, r), _rows(ef, T + j, BLK, r)
                    kc = _rows(kf, H + k0, BLK, r).astype(BF16)
                    vc = _rows(vf, H + k0, BLK, r).astype(BF16)
                    qs, dos, ls, es = stacked_inputs(q2, do2, l2, e2)
                    s = lax.dot_general(qs, kc, nt_dims, preferred_element_type=F32)
                    pr = jnp.where(band_h, jnp.exp(s - ls), 0.0)
                    dp = lax.dot_general(dos, vc, nt_dims, preferred_element_type=F32)
                    ds = (pr * (dp - es)).astype(BF16)
                    add_rows(dk_ref, k0, lax.dot_general(ds, qs, tn_dims, preferred_element_type=F32))
                    add_rows(dv_ref, k0, lax.dot_general(pr.astype(BF16), dos, tn_dims, preferred_element_type=F32))

                for u in range(nbt):
                    unit(u % r, u // r)
                if nt > 1:
                    @pl.when(t < nt - 1)
                    def _(r=r, halo_unit=halo_unit):
                        for j in range(r):
                            halo_unit(j)

    th = T // H
    last = S // H - 1
    qcur = pl.BlockSpec((T, nq * 128), lambda j, t: (t, j))
    qnext = pl.BlockSpec((H, nq * 128), lambda j, t: (jnp.minimum((t + 1) * th, last), j))
    cur = pl.BlockSpec((T, 128), lambda j, t: (t, j))
    prev = pl.BlockSpec((H, 128), lambda j, t: (jnp.maximum(t * th - 1, 0), j))
    dcur = pl.BlockSpec((T, nq * 128), lambda j, t: (t, j + do_col0))
    dnext = pl.BlockSpec((H, nq * 128), lambda j, t: (jnp.minimum((t + 1) * th, last), j + do_col0))
    in_specs = [qcur, qnext, prev, cur, prev, cur, qcur, qnext, qcur, qnext, dcur, dnext]
    args = [q, q, k, k, v, v, lse, lse, delta, delta, do, do]
    out_specs = [qcur, cur, cur]
    out_shape = [jax.ShapeDtypeStruct(q.shape, F32), jax.ShapeDtypeStruct(k.shape, F32), jax.ShapeDtypeStruct(k.shape, F32)]
    if has_sink:
        vec = pl.BlockSpec((1, nq * 128), lambda j, t: (0, j))
        in_specs.append(vec)
        args.append(sinks)
        out_specs.append(vec)
        out_shape.append(jax.ShapeDtypeStruct((1, q.shape[1]), F32))
    big = pltpu.VMEM((T + H, LANES), F32)
    return _pc(
        body, name=name, grid=(ncol, nt), in_specs=in_specs, out_specs=out_specs, out_shape=out_shape,
        scratch_shapes=[big] * 6,
        compiler_params=_params(("parallel", "arbitrary")),
    )(*args)


def _even_post_fwd(ro, proj, gn, da):
    S = ro.shape[0]
    tm = 512

    def body(ro_ref, rg_ref, gn_ref, da_ref, mix_ref):
        for c in range(4):
            sl = slice(c * 128, (c + 1) * 128)
            x = ro_ref[:, sl]
            mu = jnp.mean(x, axis=1, keepdims=True)
            xc = x - mu
            var = jnp.mean(xc * xc, axis=1, keepdims=True)
            y = xc * lax.rsqrt(var + EPS) * gn_ref[:, sl]
            z = rg_ref[:, sl]
            mix_ref[:, sl] = (z * jax.nn.sigmoid(z) * y).astype(BF16)
        mix_ref[:, 512:1024] = da_ref[...].astype(BF16)

    row = lambda w: pl.BlockSpec((tm, w), lambda i: (i, 0))
    return _pc(
        body, name="even_post_fwd", grid=(S // tm,),
        in_specs=[row(512), pl.BlockSpec((tm, 512), lambda i: (i, 2)), pl.BlockSpec((1, 512), lambda i: (0, 0)), row(512)],
        out_specs=row(1024), out_shape=jax.ShapeDtypeStruct((S, 1024), BF16),
        compiler_params=_params(("parallel",), VMEM_LIMIT_WIDE),
    )(ro, proj, gn, da)


def _even_post_bwd(ro, proj, gn, dmixed):
    S = ro.shape[0]
    tm = 512

    def body(ro_ref, rg_ref, gn_ref, dm_ref, dro_ref, drg_ref, dgn_ref):
        @pl.when(pl.program_id(0) == 0)
        def _():
            dgn_ref[...] = jnp.zeros_like(dgn_ref)

        for c in range(4):
            sl = slice(c * 128, (c + 1) * 128)
            x = ro_ref[:, sl]
            mu = jnp.mean(x, axis=1, keepdims=True)
            xc = x - mu
            rstd = lax.rsqrt(jnp.mean(xc * xc, axis=1, keepdims=True) + EPS)
            xh = xc * rstd
            gain = gn_ref[:, sl]
            y = xh * gain
            z = rg_ref[:, sl]
            sg = jax.nn.sigmoid(z)
            dra = dm_ref[:, sl]
            drg_ref[:, sl] = dra * y * sg * (1.0 + z * (1.0 - sg))
            dy = dra * z * sg
            dgn_ref[:, sl] += jnp.sum(dy * xh, axis=0, keepdims=True)
            dxh = dy * gain
            dro_ref[:, sl] = rstd * (dxh - jnp.mean(dxh, axis=1, keepdims=True)
                                     - xh * jnp.mean(dxh * xh, axis=1, keepdims=True))

    row = lambda w: pl.BlockSpec((tm, w), lambda i: (i, 0))
    vec = pl.BlockSpec((1, 512), lambda i: (0, 0))
    return _pc(
        body, name="even_post_bwd", grid=(S // tm,),
        in_specs=[row(512), pl.BlockSpec((tm, 512), lambda i: (i, 2)), vec, row(512)],
        out_specs=[row(512), row(512), vec],
        out_shape=[jax.ShapeDtypeStruct((S, 512), F32), jax.ShapeDtypeStruct((S, 512), F32),
                   jax.ShapeDtypeStruct((1, 512), F32)],
        compiler_params=_params(("arbitrary",), VMEM_LIMIT_WIDE),
    )(ro, proj, gn, dmixed)


def _swa_pre_fwd(proj, tab, qg, kg):
    S = proj.shape[0]
    tm = 512

    def body(p_ref, tab_ref, qg_ref, kg_ref, g_ref, q_ref, k_ref, v_ref):
        Ap, Bp, Cp = _tab(tab_ref, 1)
        G = g_ref[...]
        lo = _head_mask((tm, LANES), 0)
        for c in range(8):
            sl = slice(c * 128, (c + 1) * 128)
            q_ref[:, sl] = _rope(_hn_fwd(p_ref[:, sl], qg_ref[...], G), Ap, Bp, Cp, 8).astype(BF16)
        for c in range(2):
            kn = _rope(_hn_fwd(p_ref[:, 1024 + c * 128:1024 + (c + 1) * 128], kg_ref[...], G), Ap, Bp, Cp, 8)
            vv = p_ref[:, 1280 + c * 128:1280 + (c + 1) * 128]
            for t, ref in ((kn, k_ref), (vv, v_ref)):
                sw = _roll(t, HEAD)
                ref[:, (2 * c) * 128:(2 * c + 1) * 128] = jnp.where(lo, t, sw).astype(BF16)
                ref[:, (2 * c + 1) * 128:(2 * c + 2) * 128] = jnp.where(lo, sw, t).astype(BF16)

    row = lambda w: pl.BlockSpec((tm, w), lambda i: (i, 0))
    vec = pl.BlockSpec((1, LANES), lambda i: (0, 0))
    return _pc(
        body, name="swa_pre_fwd", grid=(S // tm,),
        in_specs=[row(1536), row(768), vec, vec, pl.BlockSpec((LANES, LANES), lambda i: (0, 0))],
        out_specs=[row(1024), row(512), row(512)],
        out_shape=[jax.ShapeDtypeStruct((S, w), BF16) for w in (1024, 512, 512)],
        compiler_params=_params(("parallel",), VMEM_LIMIT_WIDE),
    )(proj, tab, qg, kg, _group_matrix())


def _swa_pre_bwd(proj, tab, qg, kg, dq, dk, dv):
    S = proj.shape[0]
    tm = 512

    def body(p_ref, tab_ref, qg_ref, kg_ref, g_ref, dq_ref, dk_ref, dv_ref, dp_ref, db_ref, dqg_ref, dkg_ref):
        Ap, Bp, Cp = _tab(tab_ref, 1)
        G = g_ref[...]
        lo = _head_mask((tm, LANES), 0)

        @pl.when(pl.program_id(0) == 0)
        def _():
            db_ref[...] = jnp.zeros_like(db_ref)
            dqg_ref[...] = jnp.zeros_like(dqg_ref)
            dkg_ref[...] = jnp.zeros_like(dkg_ref)

        accq = jnp.zeros((1, LANES), F32)
        acck = jnp.zeros((1, LANES), F32)
        for c in range(8):
            sl = slice(c * 128, (c + 1) * 128)
            dx, dg = _hn_bwd(p_ref[:, sl], qg_ref[...], _rope_t(dq_ref[:, sl], Ap, Bp, Cp, 8), G)
            dp_ref[:, sl] = dx.astype(BF16)
            db_ref[:, sl] += jnp.sum(dx, axis=0, keepdims=True)
            accq = accq + dg
        for c in range(2):
            folded = []
            for ref in (dk_ref, dv_ref):
                a = ref[:, (2 * c) * 128:(2 * c + 1) * 128]
                b = ref[:, (2 * c + 1) * 128:(2 * c + 2) * 128]
                folded.append(jnp.where(lo, a + _roll(a, HEAD), b + _roll(b, HEAD)))
            ks = slice(1024 + c * 128, 1024 + (c + 1) * 128)
            dx, dg = _hn_bwd(p_ref[:, ks], kg_ref[...], _rope_t(folded[0], Ap, Bp, Cp, 8), G)
            dp_ref[:, ks] = dx.astype(BF16)
            db_ref[:, ks] += jnp.sum(dx, axis=0, keepdims=True)
            acck = acck + dg
            vs = slice(1280 + c * 128, 1280 + (c + 1) * 128)
            dp_ref[:, vs] = folded[1].astype(BF16)
            db_ref[:, vs] += jnp.sum(folded[1], axis=0, keepdims=True)
        dqg_ref[...] += _fold_halves(accq)
        dkg_ref[...] += _fold_halves(acck)

    row = lambda w: pl.BlockSpec((tm, w), lambda i: (i, 0))
    vec = pl.BlockSpec((1, LANES), lambda i: (0, 0))
    return _pc(
        body, name="swa_pre_bwd", grid=(S // tm,),
        in_specs=[row(1536), row(768), vec, vec, pl.BlockSpec((LANES, LANES), lambda i: (0, 0)),
                  row(1024), row(512), row(512)],
        out_specs=[row(1536), pl.BlockSpec((1, 1536), lambda i: (0, 0)), vec, vec],
        out_shape=[jax.ShapeDtypeStruct((S, 1536), BF16), jax.ShapeDtypeStruct((1, 1536), F32),
                   jax.ShapeDtypeStruct((1, LANES), F32), jax.ShapeDtypeStruct((1, LANES), F32)],
        compiler_params=_params(("arbitrary",), VMEM_LIMIT_WIDE),
    )(proj, tab, qg, kg, _group_matrix(), dq, dk, dv)


def _relu2_of(u):
    r = jnp.maximum(u.astype(F32), 0.0)
    return r * r


def _drelu2(acc, u):
    return (acc * 2.0 * jnp.maximum(u.astype(F32), 0.0),)


def _add(acc, res):
    return (acc + res,)


def _add_norm_in(res, g):
    def epilogue(acc, r, gv):
        xn = acc + r
        return xn, xn * lax.rsqrt(jnp.mean(xn * xn, axis=-1, keepdims=True) + EPS) * gv

    return dict(outs=[F32, BF16], epilogue=epilogue, extras=[(res, "mn"), (g.reshape(1, D_MODEL), "n")])


_T = dict(tm=1024, tn=1024, tk=1024)


def _rms_bwd_in(x, g, dres):
    def epilogue(dh, xv, gv, dr):
        r = lax.rsqrt(jnp.mean(xv * xv, axis=-1, keepdims=True) + EPS)
        t = dh * gv
        dx = dr + r * t - xv * (r * r * r) * jnp.mean(xv * t, axis=-1, keepdims=True)
        return dx, dx, jnp.sum(dh * xv * r, axis=0, keepdims=True)

    return dict(outs=[F32, BF16, ("colsum",)], epilogue=epilogue,
                extras=[(x, "mn"), (g.reshape(1, D_MODEL), "n"), (dres, "mn")])


def _delta_in(o, col0):
    width = D_MODEL - col0

    def epilogue(do, ov, G):
        parts = [_gmean(do[:, col0 + c * 128:col0 + (c + 1) * 128] * ov[:, c * 128:(c + 1) * 128], G) * float(HEAD)
                 for c in range(width // LANES)]
        return do, jnp.concatenate(parts, axis=1)

    return dict(outs=[F32, (F32, width)], epilogue=epilogue, extras=[(o, width), (_group_matrix(), "full")])


def _loss_in(res, target):
    def epilogue(acc, r, t):
        e = acc + r - t
        dy = e * (1.0 / D_MODEL)
        return dy, dy, jnp.sum(e * e, axis=0, keepdims=True) * (0.5 / D_MODEL)

    return dict(outs=[F32, BF16, ("colsum",)], epilogue=epilogue, extras=[(res, "mn"), (target, "mn")])


def _mlp_fwd(h, wts, layer, tag, tail):
    u = _matmul(h, wts, dims="nn", **_T, outs=[BF16], b_cs=True, b_row0=layer, name=f"mlp_up{tag}")
    out = _matmul(u, wts, dims="nn", **_T, a_pro=_relu2_of, b_rs=1024, b_row0=2 + layer, name=f"mlp_down{tag}", **tail)
    return out, (h, u)


def _mlp_bwd(x, g, wts, layer, saved, dy, dyb, tag):
    h, u = saved
    du = _matmul(dyb, wts, dims="nt", **_T, outs=[BF16], epilogue=_drelu2, extras=[(u, "mn")], b_rs=1024,
                 b_row0=2 + layer, name=f"mlp_du{tag}")
    dw_dn = _matmul(u, dyb, dims="tn", **_T, outs=[F32], a_pro=_relu2_of, name=f"mlp_dwdown{tag}")
    dw_up = _matmul(h, du, dims="tn", **_T, outs=[F32], o_cs=N_CHIPS, name=f"mlp_dwup{tag}")
    dx, dxb, dg = _matmul(du, wts, dims="nt", **_T, b_cs=True, b_row0=layer, b_rows=1024, name=f"mlp_dh{tag}",
                          vmem=VMEM_LIMIT_WIDE, **_rms_bwd_in(x, g, dy))
    return dx, dxb, dg, dw_up, dw_dn


def _local_step(x, pos_col, target, first_of, rest_begin, rest_of, P, red):
    S = x.shape[0]
    tab = _tables(pos_col)
    tile2 = lambda g: jnp.tile(g.reshape(1, HEAD), (1, 2))
    dqg, dkg = tile2(P["dil_q_gain"]), tile2(P["dil_k_gain"])
    sqg, skg = tile2(P["swa_q_gain"]), tile2(P["swa_k_gain"])
    gn = P["ret_gn_gain"].reshape(1, 512)
    sink_b = jnp.repeat(P["swa_sinks"].reshape(16), HEAD).reshape(1, 1024)

    h0 = _rms_fwd(x, P["norm_mix"][0], "rms_mix_fwd0")
    W = first_of((h0, tab))
    proj = _matmul(h0, W["hyb_w_in"], dims="nn", tm=1024, tn=768, tk=1024, outs=[F32], b_cs=True, name="hyb_in")
    rq, rk, rv, dq, dk, dv = _even_pre_fwd(proj, tab, dqg, dkg)
    ro, states = _ret_fwd(rq, rk, rv)
    dil = [(w // r, r) for w, r in DIL_PATTERNS]
    da, dlse = _band_fwd(dq, dk, dv, patterns=dil, nq=1, name="dil_fwd")
    mixed = rest_begin(_even_post_fwd(ro, proj, gn, da))
    x1, h1 = _matmul(mixed, W["hyb_w_out"], dims="nn", **_T, name="hyb_out", **_add_norm_in(x, P["norm_mlp"][0]))
    rest, bias = rest_of(x1)
    W = {**W, **rest}
    (x2, h2), mlp0 = _mlp_fwd(h1, W["packed"], 0, "0", _add_norm_in(x1, P["norm_mix"][1]))

    proj2 = _matmul(h2, W["swa_w_qkv"], dims="nn", tm=1024, tn=384, tk=1024, outs=[F32], b_cs=True,
                    epilogue=_add, extras=[(bias.reshape(1, 1536), "n")], name="swa_qkv")
    sq, sk, sv = _swa_pre_fwd(proj2, tab, sqg, skg)
    swa = [(SWA_DIST, 1)]
    so, slse, so_b = _band_fwd(sq, sk, sv, patterns=swa, nq=2, name="swa_fwd", sinks=sink_b, want_bf16=True)
    x3, h3 = _matmul(so_b, W["swa_w_out"], dims="nn", **_T, name="swa_out", **_add_norm_in(x2, P["norm_mlp"][1]))
    (dy, dyb, loss_cols), mlp1 = _mlp_fwd(h3, W["packed"], 1, "1", _loss_in(x3, target))
    loss = jnp.broadcast_to(jnp.sum(loss_cols), (1, LANES))

    gw, gp = {}, {}
    dx3, dx3b, dg_mlp1, gw["mlp_w_up1"], gw["mlp_w_down1"] = _mlp_bwd(x3, P["norm_mlp"][1], W["packed"], 1, mlp1, dy, dyb, "1")
    dx3b = red.begin("mlp1", {n: (gw[n], 1024) for n in ("mlp_w_up1", "mlp_w_down1")}, dx3b)
    gw["swa_w_out"] = _matmul(so_b, dx3b, dims="tn", **_T, outs=[F32], name="swa_dwout")
    dso, sdelta = _matmul(dx3b, W["swa_w_out"], dims="nt", **_T, name="swa_do", vmem=VMEM_LIMIT_WIDE, **_delta_in(so, 0))
    dsq, dsk, dsv, dsink = _band_bwd(sq, sk, sv, slse, sdelta, dso, patterns=swa, nq=2, name="swa_bwd", sinks=sink_b)
    dproj2, gp["swa_b_qkv"], gp["swa_q_gain"], gp["swa_k_gain"] = _swa_pre_bwd(proj2, tab, sqg, skg, dsq, dsk, dsv)
    gp["swa_sinks"] = dsink
    gw["swa_w_qkv"] = _matmul(h2, dproj2, dims="tn", tm=1024, tn=384, tk=1024, outs=[F32], o_cs=N_CHIPS, name="swa_dwqkv")
    dx2, dx2b, dg_mix1 = _matmul(dproj2, W["swa_w_qkv"], dims="nt", tm=1024, tn=1024, tk=384, b_cs=True, name="swa_dh",
                                 vmem=VMEM_LIMIT_WIDE, **_rms_bwd_in(x2, P["norm_mix"][1], dx3))
    dx2b = red.advance("mlp1", dx2b, dx2b)

    dx1, dx1b, dg_mlp0, gw["mlp_w_up0"], gw["mlp_w_down0"] = _mlp_bwd(x1, P["norm_mlp"][0], W["packed"], 0, mlp0, dx2, dx2b, "0")
    gw["hyb_w_out"] = _matmul(mixed, dx1b, dims="tn", **_T, outs=[F32], name="hyb_dwout")
    dx1b = red.begin("mlp0", {"mlp_w_up0": (gw["mlp_w_up0"], 1024), "mlp_w_down0": (gw["mlp_w_down0"], 1024),
                              "hyb_w_out": (gw["hyb_w_out"], 256), "swa_w_qkv": (gw["swa_w_qkv"], 1024),
                              "swa_w_out": (gw["swa_w_out"], 256)}, dx1b)
    red.finish("mlp1", dx1b)
    dmixed, ddelta = _matmul(dx1b, W["hyb_w_out"], dims="nt", **_T, name="hyb_dmixed", vmem=VMEM_LIMIT_WIDE,
                             **_delta_in(da, 512))
    dro, drg, gp["ret_gn_gain"] = _even_post_bwd(ro, proj, gn, dmixed)
    drq, drk, drv = _ret_bwd(rq, rk, rv, states, dro)
    ddq, ddk, ddv = _band_bwd(dq, dk, dv, dlse, ddelta, dmixed, patterns=dil, nq=1, name="dil_bwd", do_col0=4)
    ddq = red.advance("mlp0", ddq, ddq)
    dproj, gp["dil_q_gain"], gp["dil_k_gain"] = _even_pre_bwd(proj, tab, dqg, dkg, drq, drk, drv, drg, [ddq], [ddk], [ddv])
    gw["hyb_w_in"] = _matmul(h0, dproj, dims="tn", tm=1024, tn=768, tk=1024, outs=[F32], o_cs=N_CHIPS, name="hyb_dwin")
    dproj = red.begin("win", {"hyb_w_in": (gw["hyb_w_in"], 1024)}, dproj)
    grad_x, _, dg_mix0 = _matmul(dproj, W["hyb_w_in"], dims="nt", tm=1024, tn=1024, tk=768, b_cs=True, name="hyb_dh",
                                 vmem=VMEM_LIMIT_WIDE, **_rms_bwd_in(x, P["norm_mix"][0], dx1))
    red.finish("mlp0", grad_x)
    gp["norm_mix"] = jnp.concatenate([dg_mix0, dg_mix1], axis=0)
    gp["norm_mlp"] = jnp.concatenate([dg_mlp0, dg_mlp1], axis=0)
    return loss, grad_x, gp


HBM = pl.BlockSpec(memory_space=pltpu.HBM)


def _place():
    x, y, c = lax.axis_index("x"), lax.axis_index("y"), lax.axis_index("c")
    chips = [(1 - x, y), (x, 1 - y), (1 - x, 1 - y)]
    return x, y, c, chips


SEM = pl.BlockSpec(memory_space=pltpu.SEMAPHORE)
EFFECT = pltpu.SideEffectType.DATAFLOW_SIDE_EFFECTING


def _half_block(ref, chip, core):
    rh = ref.shape[1] // 2
    return ref.at[2 * chip[0] + chip[1], pl.ds(core * rh, rh), :]


def _gather_start(buf, ride, name):
    def body(b_ref, ride_ref, s0, s1, s2, r0, r1, r2, b_out, ride_out):
        x, y, c, chips = _place()
        for chip, s, r in zip(chips, (s0, s1, s2), (r0, r1, r2)):
            mine = _half_block(b_ref, (x, y), c)
            pltpu.make_async_remote_copy(src_ref=mine, dst_ref=mine, send_sem=s, recv_sem=r,
                                         device_id=(*chip, c), device_id_type=MESH).start()

    sem = pltpu.SemaphoreType.DMA(())
    return _pc(
        body, name=name,
        out_shape=(sem,) * 6 + (pltpu.HBM(buf.shape, buf.dtype), pltpu.HBM(ride.shape, ride.dtype)),
        in_specs=(HBM, HBM), out_specs=(SEM,) * 6 + (HBM, HBM), input_output_aliases={0: 6, 1: 7},
        compiler_params=pltpu.CompilerParams(has_side_effects=EFFECT),
    )(pltpu.with_memory_space_constraint(buf, pltpu.HBM), pltpu.with_memory_space_constraint(ride, pltpu.HBM))


def _gather_wait(buf, sems, afters, name):
    def body(b_ref, s0, s1, s2, r0, r1, r2, *unread):
        x, y, c, chips = _place()
        for chip, s, r in zip(chips, (s0, s1, s2), (r0, r1, r2)):
            cp = pltpu.make_async_remote_copy(src_ref=_half_block(b_ref, (x, y), c), dst_ref=_half_block(b_ref, chip, c),
                                              send_sem=s, recv_sem=r, device_id=(*chip, c), device_id_type=MESH)
            cp.wait_send()
            cp.wait_recv()

    return _pc(
        body, name=name, out_shape=pltpu.HBM(buf.shape, buf.dtype),
        in_specs=(HBM,) + (SEM,) * 6 + (pl.BlockSpec(memory_space=pl.ANY),) * len(afters), out_specs=HBM,
        input_output_aliases={0: 0}, compiler_params=pltpu.CompilerParams(has_side_effects=EFFECT),
    )(buf, *sems, *afters)


def _gather_handover(buf, name):
    def body(b_ref, out_ref, send_sems, recv_sems):
        x, y, c, chips = _place()
        cps = []
        for k, chip in enumerate(chips):
            mine = _half_block(b_ref, chip, c)
            cps.append(pltpu.make_async_remote_copy(src_ref=mine, dst_ref=mine, send_sem=send_sems.at[k],
                                                    recv_sem=recv_sems.at[k], device_id=(x, y, 1 - c), device_id_type=MESH))
        for cp in cps:
            cp.start()
        for k, chip in enumerate(chips):
            theirs = _half_block(b_ref, chip, 1 - c)
            pltpu.make_async_remote_copy(src_ref=theirs, dst_ref=theirs, send_sem=send_sems.at[k], recv_sem=recv_sems.at[k],
                                         device_id=(x, y, 1 - c), device_id_type=MESH).wait_recv()
        for cp in cps:
            cp.wait_send()

    return _pc(
        body, name=name, in_specs=[HBM], out_specs=HBM,
        out_shape=jax.ShapeDtypeStruct(buf.shape, buf.dtype), input_output_aliases={0: 0},
        scratch_shapes=[pltpu.SemaphoreType.DMA((3,)), pltpu.SemaphoreType.DMA((3,))],
    )(buf)


def _pair_sum(t, l, place, name):
    _, r, cols = t.shape
    rh = r // 2
    tr = min(rh, 256)
    nr = rh // tr

    def body(pl_ref, t_ref, l_ref, o_ref):
        o_ref[...] = (t_ref[...] + l_ref[...]).astype(BF16)

    other = lambda s, p: s + jnp.where(s >= p[0], 1, 0)
    return _pc(
        body, name=name,
        grid_spec=pltpu.PrefetchScalarGridSpec(
            num_scalar_prefetch=1, grid=(N_CHIPS - 1, nr),
            in_specs=[pl.BlockSpec((None, tr, cols), lambda s, i, p: (other(s, p), p[1] * nr + i, 0)),
                      pl.BlockSpec((None, tr, cols), lambda s, i, p: (other(s, p), i, 0))],
            out_specs=pl.BlockSpec((None, tr, cols), lambda s, i, p: (other(s, p), i, 0))),
        out_shape=jax.ShapeDtypeStruct((N_CHIPS, rh, cols), BF16),
        compiler_params=_params(("parallel", "parallel")),
    )(place, t, l)


def _final_sum(t, l, rcv, place, name, layer=0, layers=1, into=None):
    _, r, cols = t.shape
    rh = r // 2
    tr = min(rh, 256)
    nr = rh // tr

    def body(pl_ref, t_ref, l_ref, r_ref, *rest):
        acc = t_ref[...] + l_ref[...]
        for k in range(3):
            acc = acc + r_ref[k].astype(F32)
        rest[-1][...] = acc

    in_specs = [pl.BlockSpec((None, tr, cols), lambda i, p: (p[0], p[1] * nr + i, 0)),
                pl.BlockSpec((None, tr, cols), lambda i, p: (p[0], i, 0)),
                pl.BlockSpec((3, tr, cols), lambda i, p: (0, i, 0))]
    args = [place, t, l, rcv]
    aliases = {}
    if into is not None:
        in_specs.append(pl.BlockSpec(memory_space=pl.ANY))
        args.append(into)
        aliases = {4: 0}
    return _pc(
        body, name=name,
        grid_spec=pltpu.PrefetchScalarGridSpec(
            num_scalar_prefetch=1, grid=(nr,), in_specs=in_specs,
            out_specs=pl.BlockSpec((tr, cols), lambda i, p: (2 * nr * layer + p[1] * nr + i, 0))),
        out_shape=jax.ShapeDtypeStruct((layers * r, cols), F32), input_output_aliases=aliases,
        compiler_params=_params(("parallel",)),
    )(*args)


def _share_halves(hs, name):
    nt = len(hs)
    n = sum(layers for _, layers in hs)

    def body(*refs):
        h_refs, send_sems, recv_sems = refs[:nt], refs[-2], refs[-1]
        x, y, c, _ = _place()
        cps = []
        for k, (_, layers) in enumerate(hs):
            rh = h_refs[k].shape[0] // (2 * layers)
            for layer in range(layers):
                half = h_refs[k].at[pl.ds((2 * layer + c) * rh, rh), :]
                cps.append(pltpu.make_async_remote_copy(
                    src_ref=half, dst_ref=half, send_sem=send_sems.at[len(cps)], recv_sem=recv_sems.at[len(cps)],
                    device_id=(x, y, 1 - c), device_id_type=MESH))
        for cp in cps:
            cp.start()
        for cp in cps:
            cp.wait()

    return _pc(
        body, name=name, in_specs=[HBM] * nt, out_specs=[HBM] * nt,
        out_shape=[jax.ShapeDtypeStruct(h.shape, F32) for h, _ in hs],
        input_output_aliases={k: k for k in range(nt)},
        scratch_shapes=[pltpu.SemaphoreType.DMA((n,)), pltpu.SemaphoreType.DMA((n,))],
    )(*[h for h, _ in hs])


def _split_start(name, bufs, ride, n, copies_of):
    nb = len(bufs)

    def body(*refs):
        sems = refs[nb + 1:nb + 1 + 2 * n]
        for cp in copies_of(refs[:nb], sems[:n], sems[n:]):
            (cp[0] if isinstance(cp, tuple) else cp).start()

    outs = _pc(
        body, name=name,
        out_shape=(pltpu.SemaphoreType.DMA(()),) * (2 * n) + tuple(pltpu.HBM(b.shape, b.dtype) for b in bufs)
        + (pltpu.HBM(ride.shape, ride.dtype),),
        in_specs=(HBM,) * (nb + 1), out_specs=(SEM,) * (2 * n) + (HBM,) * (nb + 1),
        input_output_aliases={k: 2 * n + k for k in range(nb + 1)},
        compiler_params=pltpu.CompilerParams(has_side_effects=EFFECT),
    )(*[pltpu.with_memory_space_constraint(b, pltpu.HBM) for b in bufs], pltpu.with_memory_space_constraint(ride, pltpu.HBM))
    return list(outs[:2 * n]), list(outs[2 * n:2 * n + nb]), outs[-1]


def _split_wait(name, bufs, sems, after, n, copies_of):
    nb = len(bufs)

    def body(*refs):
        s = refs[nb:nb + 2 * n]
        for cp in copies_of(refs[:nb], s[:n], s[n:]):
            sent, landed = cp if isinstance(cp, tuple) else (cp, cp)
            sent.wait_send()
            landed.wait_recv()

    outs = _pc(
        body, name=name, out_shape=tuple(pltpu.HBM(b.shape, b.dtype) for b in bufs),
        in_specs=(HBM,) * nb + (SEM,) * (2 * n) + (pl.BlockSpec(memory_space=pl.ANY),), out_specs=(HBM,) * nb,
        input_output_aliases={k: k for k in range(nb)},
        compiler_params=pltpu.CompilerParams(has_side_effects=EFFECT),
    )(*bufs, *sems, after)
    return list(outs)


def _handover_copies(refs, send, recv):
    x, y, c, chips = _place()
    cps = []
    for k, chip in enumerate(chips):
        mine, theirs = _half_block(refs[0], chip, c), _half_block(refs[0], chip, 1 - c)
        desc = lambda blk: pltpu.make_async_remote_copy(src_ref=blk, dst_ref=blk, send_sem=send[k], recv_sem=recv[k],
                                                        device_id=(x, y, 1 - c), device_id_type=MESH)
        cps.append((desc(mine), desc(theirs)))
    return cps


def _share_copies(layers_of):
    def copies_of(refs, send, recv):
        x, y, c, _ = _place()
        cps = []
        for k, layers in enumerate(layers_of):
            rh = refs[k].shape[0] // (2 * layers)
            for layer in range(layers):
                i = len(cps)
                desc = lambda half: pltpu.make_async_remote_copy(
                    src_ref=refs[k].at[pl.ds((2 * layer + half) * rh, rh), :], dst_ref=refs[k].at[pl.ds((2 * layer + half) * rh, rh), :],
                    send_sem=send[i], recv_sem=recv[i], device_id=(x, y, 1 - c), device_id_type=MESH)
                cps.append((desc(c), desc(1 - c)))
        return cps
    return copies_of


def _swap_copies(nt):
    def copies_of(refs, send, recv):
        x, y, c, _ = _place()
        cps = []
        for k in range(nt):
            rh = refs[k].shape[1] // 2
            cps.append(pltpu.make_async_remote_copy(
                src_ref=refs[k].at[:, pl.ds((1 - c) * rh, rh), :], dst_ref=refs[nt + k],
                send_sem=send[k], recv_sem=recv[k], device_id=(x, y, 1 - c), device_id_type=MESH))
        return cps
    return copies_of


def _exchange_copies(nt):
    def copies_of(refs, send, recv):
        x, y, c, chips = _place()
        cps = []
        for t in range(nt):
            for k, chip in enumerate(chips):
                cps.append(pltpu.make_async_remote_copy(
                    src_ref=refs[t].at[2 * chip[0] + chip[1]], dst_ref=refs[nt + t].at[k],
                    send_sem=send[3 * t + k], recv_sem=recv[3 * t + k], device_id=(*chip, c), device_id_type=MESH))
        return cps
    return copies_of


class _StagedReduce:
    def __init__(self, place):
        self.place = place
        self.groups = {}
        self.halves = {}

    @staticmethod
    def slab(t, r):
        return t.reshape(N_CHIPS, r, t.size // (N_CHIPS * r))

    def begin(self, g, grads, ride):
        names = list(grads)
        ts = [self.slab(t, r) for t, r in grads.values()]
        lands = [lax.empty((N_CHIPS, t.shape[1] // 2, t.shape[2]), F32) for t in ts]
        sems, bufs, ride = _split_start(f"grad_swap_start_{g}", ts + lands, ride, len(ts), _swap_copies(len(ts)))
        self.groups[g] = dict(names=names, bufs=bufs, sems=sems)
        return ride

    def advance(self, g, after, ride):
        st = self.groups[g]
        nt = len(st["names"])
        bufs = _split_wait(f"grad_swap_wait_{g}", st["bufs"], st["sems"], after, nt, _swap_copies(nt))
        st["ts"], st["ls"] = bufs[:nt], bufs[nt:]
        ps = [_pair_sum(t, l, self.place, f"pair_sum_{n}") for t, l, n in zip(st["ts"], st["ls"], st["names"])]
        lands = [lax.empty((3,) + p.shape[1:], BF16) for p in ps]
        st["sems"], st["bufs"], ride = _split_start(f"grad_exchange_start_{g}", ps + lands, ride, 3 * nt, _exchange_copies(nt))
        return ride

    def finish(self, g, after):
        st = self.groups[g]
        nt = len(st["names"])
        bufs = _split_wait(f"grad_exchange_wait_{g}", st["bufs"], st["sems"], after, 3 * nt, _exchange_copies(nt))
        for t, l, r, n in zip(st["ts"], st["ls"], bufs[nt:], st["names"]):
            if n[-1] in "01":
                self.halves[n[:-1]] = _final_sum(t, l, r, self.place, f"final_sum_{n}", layer=int(n[-1]), layers=2,
                                                 into=self.halves.get(n[:-1]))
            else:
                self.halves[n] = _final_sum(t, l, r, self.place, f"final_sum_{n}")


def _allgather_small(v):
    rows = v.shape[0]

    def body(v_ref, out_ref, send_sems, recv_sems):
        x, y, c, _ = _place()
        me = 4 * x + 2 * y + c
        out_ref[me] = v_ref[...]
        cps = []
        for k in range(1, 8):
            fx, fy, fc = (k >> 2) & 1, (k >> 1) & 1, k & 1
            to = (1 - x if fx else x, 1 - y if fy else y, 1 - c if fc else c)
            cps.append(pltpu.make_async_remote_copy(
                src_ref=v_ref, dst_ref=out_ref.at[me], send_sem=send_sems.at[k - 1], recv_sem=recv_sems.at[k - 1],
                device_id=to, device_id_type=MESH))
        for cp in cps:
            cp.start()
        for cp in cps:
            cp.wait()

    return _pc(
        body, name="allgather_small",
        in_specs=[pl.BlockSpec(memory_space=pltpu.VMEM)], out_specs=pl.BlockSpec(memory_space=pltpu.VMEM),
        out_shape=jax.ShapeDtypeStruct((8, rows, LANES), F32),
        scratch_shapes=[pltpu.SemaphoreType.DMA((7,)), pltpu.SemaphoreType.DMA((7,))],
    )(v)


def _adamw_math(w, g, m, v):
    m = ADAM_B1 * m + (1.0 - ADAM_B1) * g
    v = ADAM_B2 * v + (1.0 - ADAM_B2) * (g * g)
    m_hat = m / (1.0 - ADAM_B1 ** ADAM_STEP)
    v_hat = v / (1.0 - ADAM_B2 ** ADAM_STEP)
    return -ADAM_LR * (m_hat / (jnp.sqrt(v_hat) + ADAM_EPS) + ADAM_WD * w), m, v


def _adamw(w, g, m, v, name):
    r, cols = w.shape
    tr = min(r, 256)

    def body(w_ref, g_ref, m_ref, v_ref, go_ref, d_ref, mo_ref, vo_ref):
        gv = g_ref[...]
        d, mn, vn = _adamw_math(w_ref[...], gv, m_ref[...], v_ref[...])
        go_ref[...] = gv
        d_ref[...] = d
        mo_ref[...] = mn
        vo_ref[...] = vn

    row = pl.BlockSpec((tr, cols), lambda i: (i, 0))
    return _pc(
        body, name=name, grid=(r // tr,), in_specs=[row] * 4, out_specs=[row] * 4,
        out_shape=[jax.ShapeDtypeStruct((r, cols), F32)] * 4,
        compiler_params=_params(("parallel",)),
    )(w, g, m, v)


def _adamw_small(w, gathered, m, v):
    rows = w.shape[0]

    def body(w_ref, g_ref, m_ref, v_ref, go_ref, d_ref, mo_ref, vo_ref):
        g = g_ref[0]
        for k in range(1, 8):
            g = g + g_ref[k]
        d, mn, vn = _adamw_math(w_ref[...], g, m_ref[...], v_ref[...])
        go_ref[...] = g
        d_ref[...] = d
        mo_ref[...] = mn
        vo_ref[...] = vn

    return _pc(
        body, name="adamw_small",
        out_shape=[jax.ShapeDtypeStruct((rows, LANES), F32)] * 4,
    )(w, gathered, m, v)


_BIAS_ROWS = 32


def _own_slot(flat, chip):
    return lax.dynamic_update_slice(lax.empty((N_CHIPS,) + flat.shape, flat.dtype), flat[None], (chip, 0, 0))


def _pack_first(hyb_w_in, hyb_w_out):
    return jnp.concatenate([t.astype(BF16).reshape(-1, 1024) for t in (hyb_w_in, hyb_w_out)], axis=0)


def _unpack_first(g):
    return {"hyb_w_in": g[:, 0:768, :].reshape(N_CHIPS, 1024, 768), "hyb_w_out": g[:, 768:1024, :].reshape(1024, 1024)}


def _pack_rest(mlp_w_up, mlp_w_down, swa_w_qkv, swa_w_out, swa_b_qkv):
    parts = [t.astype(BF16).reshape(-1, 1024) for t in (mlp_w_up, mlp_w_down, swa_w_qkv, swa_w_out)]
    bias = lax.bitcast_convert_type(swa_b_qkv.reshape(384), BF16).reshape(1, 768)
    bias = jnp.pad(bias, ((0, _BIAS_ROWS - 1), (0, 256)))
    return jnp.concatenate(parts + [bias], axis=0)


def _unpack_rest(g):
    W = {
        "packed": g,
        "swa_w_qkv": g[:, 4096:4480, :].reshape(N_CHIPS, 1024, 384),
        "swa_w_out": g[:, 4480:4736, :].reshape(1024, 1024),
    }
    bias = lax.bitcast_convert_type(g[:, 4736, :768].reshape(N_CHIPS, 384, 2), F32).reshape(1536)
    return W, bias


_SMALL = (("norm_mix", 16), ("norm_mlp", 16), ("ret_gn_gain", 4), ("dil_q_gain", 1), ("dil_k_gain", 1),
          ("swa_b_qkv", 12), ("swa_q_gain", 1), ("swa_k_gain", 1), ("swa_sinks", 1), ("loss", 1))
_SUBLANES = 8


def _slot(r):
    return -(-r // _SUBLANES) * _SUBLANES


def _pack_small(d):
    return jnp.concatenate([jnp.pad(d[n].reshape(r, LANES), ((0, _slot(r) - r), (0, 0))) for n, r in _SMALL], axis=0)


def _unpack_small(p):
    out, o = {}, 0
    for n, r in _SMALL:
        out[n] = p[o:o + r]
        o += _slot(r)
    return out


def kernel(x, positions, norm_mix, norm_mlp, mlp_w_up, mlp_w_down, hyb_w_in, hyb_w_out, ret_gn_gain, dil_q_gain, dil_k_gain, swa_w_qkv, swa_b_qkv, swa_w_out, swa_q_gain, swa_k_gain, swa_sinks, loss_target, m_norm_mix, m_norm_mlp, m_mlp_w_up, m_mlp_w_down, m_hyb_w_in, m_hyb_w_out, m_ret_gn_gain, m_dil_q_gain, m_dil_k_gain, m_swa_w_qkv, m_swa_b_qkv, m_swa_w_out, m_swa_q_gain, m_swa_k_gain, m_swa_sinks, v_norm_mix, v_norm_mlp, v_mlp_w_up, v_mlp_w_down, v_hyb_w_in, v_hyb_w_out, v_ret_gn_gain, v_dil_q_gain, v_dil_k_gain, v_swa_w_qkv, v_swa_b_qkv, v_swa_w_out, v_swa_q_gain, v_swa_k_gain, v_swa_sinks):
    ax, ay, ac = lax.axis_index("x"), lax.axis_index("y"), lax.axis_index("c")
    chip = 2 * ax + ay
    place = jnp.stack([chip, ac]).astype(jnp.int32)
    S = x.shape[1]

    first = _own_slot(_pack_first(hyb_w_in[0], hyb_w_out[0]), chip)
    rest = _own_slot(_pack_rest(mlp_w_up, mlp_w_down, swa_w_qkv[0], swa_w_out[0], swa_b_qkv[0]), chip)
    *sems, first, pos_col = _gather_start(first, positions.reshape(S, 1), "allgather_first_start")
    flight = {}

    def first_of(afters):
        g = _gather_handover(_gather_wait(first, sems, [*afters, rest], "allgather_first_wait"), "allgather_first_handover")
        *flight["sems"], flight["buf"], g = _gather_start(rest, g, "allgather_rest_start")
        return _unpack_first(g)

    def rest_begin(ride):
        buf = _gather_wait(flight["buf"], flight["sems"], [ride], "allgather_rest_wait")
        flight["sems"], flight["bufs"], ride = _split_start("allgather_rest_handover_start", [buf], ride, 3, _handover_copies)
        return ride

    def rest_of(after):
        return _unpack_rest(_split_wait("allgather_rest_handover_wait", flight["bufs"], flight["sems"], after, 3,
                                        _handover_copies)[0])

    P = dict(norm_mix=norm_mix, norm_mlp=norm_mlp, ret_gn_gain=ret_gn_gain, dil_q_gain=dil_q_gain, dil_k_gain=dil_k_gain,
             swa_q_gain=swa_q_gain, swa_k_gain=swa_k_gain, swa_sinks=swa_sinks)

    red = _StagedReduce(place)
    loss_l, grad_x, gp = _local_step(x[0], pos_col, loss_target[0], first_of, rest_begin, rest_of, P, red)

    params = dict(mlp_w_up=(mlp_w_up, m_mlp_w_up, v_mlp_w_up), mlp_w_down=(mlp_w_down, m_mlp_w_down, v_mlp_w_down),
                  hyb_w_in=(hyb_w_in, m_hyb_w_in, v_hyb_w_in), hyb_w_out=(hyb_w_out, m_hyb_w_out, v_hyb_w_out),
                  swa_w_qkv=(swa_w_qkv, m_swa_w_qkv, v_swa_w_qkv), swa_w_out=(swa_w_out, m_swa_w_out, v_swa_w_out))
    big = {}

    def adamw_of(n, g):
        rows = g.shape[0]
        w, m, v = (t.reshape(rows, -1) for t in params[n])
        big[n] = [t.reshape(params[n][0].shape) for t in _adamw(w, g, m, v, f"adamw_{n}")]

    names = ["mlp_w_up", "mlp_w_down", "hyb_w_out", "swa_w_qkv", "swa_w_out"]
    red.halves[names[0]] = red.advance("win", grad_x, red.halves[names[0]])

    gsm = dict(gp, loss=loss_l)
    gsm["swa_sinks"] = jnp.pad(gp["swa_sinks"].reshape(16, HEAD)[:, 0], (0, LANES - 16))
    layers = [params[n][0].shape[0] for n in names]
    gathered = _allgather_small(_pack_small(gsm))
    sems, shared, gathered = _split_start("grad_share_start", [red.halves[n] for n in names], gathered, sum(layers),
                                          _share_copies(layers))

    def small_pack(norm_mix, norm_mlp, gn, dq, dk, b, sq, sk, sinks):
        dup = lambda t: jnp.tile(t.reshape(1, HEAD), (1, 2))
        bias = lax.dynamic_update_slice(jnp.zeros((12, LANES), F32), b.reshape(3, LANES), (3 * chip, 0))
        return _pack_small(dict(norm_mix=norm_mix, norm_mlp=norm_mlp, ret_gn_gain=gn, dil_q_gain=dup(dq), dil_k_gain=dup(dk),
                                swa_b_qkv=bias, swa_q_gain=dup(sq), swa_k_gain=dup(sk),
                                swa_sinks=jnp.pad(sinks.reshape(16), (0, LANES - 16)), loss=jnp.zeros((1, LANES), F32)))

    pw = small_pack(norm_mix, norm_mlp, ret_gn_gain, dil_q_gain, dil_k_gain, swa_b_qkv, swa_q_gain, swa_k_gain, swa_sinks)
    pm = small_pack(m_norm_mix, m_norm_mlp, m_ret_gn_gain, m_dil_q_gain, m_dil_k_gain, m_swa_b_qkv, m_swa_q_gain, m_swa_k_gain, m_swa_sinks)
    pv = small_pack(v_norm_mix, v_norm_mlp, v_ret_gn_gain, v_dil_q_gain, v_dil_k_gain, v_swa_b_qkv, v_swa_q_gain, v_swa_k_gain, v_swa_sinks)
    small_flat = _adamw_small(pw, gathered, pm, pv)
    small = [_unpack_small(t) for t in small_flat]

    for n, g in zip(names, _split_wait("grad_share_wait", shared, sems, small_flat[1], sum(layers), _share_copies(layers))):
        adamw_of(n, g)
    red.finish("win", big[names[-1]][1])
    adamw_of("hyb_w_in", _share_halves([(red.halves["hyb_w_in"], 1)], "grad_share_last")[0])

    def small_out(n, k):
        t = small[k][n]
        if n in ("norm_mix", "norm_mlp"):
            return t.reshape(2, D_MODEL)
        if n == "ret_gn_gain":
            return t.reshape(1, RET_HEADS, 128)
        if n == "swa_b_qkv":
            return lax.dynamic_slice(t, (3 * chip, 0), (3, LANES)).reshape(1, 384)
        if n == "swa_sinks":
            return t[0, :16].reshape(1, 16)
        return t[0, :HEAD].reshape(1, HEAD)

    order = ["norm_mix", "norm_mlp", "mlp_w_up", "mlp_w_down", "hyb_w_in", "hyb_w_out", "ret_gn_gain", "dil_q_gain",
             "dil_k_gain", "swa_w_qkv", "swa_b_qkv", "swa_w_out", "swa_q_gain", "swa_k_gain", "swa_sinks"]
    is_big = {"mlp_w_up", "mlp_w_down", "hyb_w_in", "hyb_w_out", "swa_w_qkv", "swa_w_out"}
    outs = [small[0]["loss"][0, 0], grad_x[None]]
    for k in range(4):
        outs += [big[n][k] if n in is_big else small_out(n, k) for n in order]
    return tuple(outs)
```
